```python
import functools
import jax, jax.numpy as jnp
from jax import lax
import numpy as np


D_MODEL = 1024
BATCH = 16
SEQ = 2048
DEPTH = 2

GRID_W = 64
CTX_LEN = 256
N_MIXERS = 2
N_MOD = 6
EPS = 1e-6
GLA_HEADS = 4
GLA_KEY_DIM = D_MODEL // 2
GLA_VAL_DIM = D_MODEL
GLA_HEAD_K = GLA_KEY_DIM // GLA_HEADS
GLA_HEAD_V = GLA_VAL_DIM // GLA_HEADS
GLA_GATE_RANK = 16
GLA_GATE_TAU = 16.0
GLA_CHUNK = 64
GLA_IN_DIM = 2 * GLA_KEY_DIM + 2 * GLA_VAL_DIM + 2 * GLA_GATE_RANK
SC_DIM = D_MODEL
CONV_WIDTH = 3
FFN_HIDDEN = 5 * D_MODEL // 2

kernel_name = 'hybrid_gla_shortconv_convffn_dit'


def rmsnorm(x, gain):
    x32 = x.astype(jnp.float32)
    y = x32 * lax.rsqrt(jnp.mean(x32 * x32, axis=-1, keepdims=True) + EPS)
    return y.astype(x.dtype) * gain


def modulate(x, gain, shift, scale):
    return rmsnorm(x, gain) * (1 + scale) + shift


def dwconv3(u, w, axis):
    n = u.shape[axis]
    pad = [(0, 0)] * u.ndim
    pad[axis] = (1, 1)
    up = jnp.pad(u, pad)
    out = lax.slice_in_dim(up, 0, n, axis=axis) * w[0]
    for tap in range(1, CONV_WIDTH):
        out = out + lax.slice_in_dim(up, tap, tap + n, axis=axis) * w[tap]
    return out


def conv_grid(u, w, rows, axis):
    b, t, ch = u.shape
    return dwconv3(u.reshape(b, rows, GRID_W, ch), w, axis).reshape(b, t, ch)


def conv_seq(u, w):
    return dwconv3(u, w, 1)


def heads(t, dh):
    return t.reshape(t.shape[0], t.shape[1], -1, dh)


def gla_log_decay(a_low, w_a2, b_a):
    z = (a_low @ w_a2 + b_a).astype(jnp.float32)
    return heads(jax.nn.log_sigmoid(z) / GLA_GATE_TAU, GLA_HEAD_K)


def gla_scan(q, k, v, log_a, s0):
    bsz, t, nh, _ = q.shape
    dv = v.shape[-1]
    n = t // GLA_CHUNK

    def to_chunks(a):
        return a.astype(jnp.float32).reshape(bsz, n, GLA_CHUNK, nh, a.shape[-1]).transpose(1, 0, 3, 2, 4)

    xs = tuple(to_chunks(a) for a in (q, k, v, log_a))
    mask = jnp.tril(jnp.ones((GLA_CHUNK, GLA_CHUNK), dtype=bool))

    def step(s, inp):
        qi, ki, vi, gi = inp
        bcum = jnp.cumsum(gi, axis=-2)
        b_last = bcum[..., -1:, :]
        q_s = qi * jnp.exp(bcum)
        k_s = ki * jnp.exp(-bcum)
        k_d = ki * jnp.exp(b_last - bcum)
        att = jnp.where(mask, jnp.einsum('bhik,bhjk->bhij', q_s, k_s), 0.0)
        o = jnp.einsum('bhik,bhkv->bhiv', q_s, s) + jnp.einsum('bhij,bhjv->bhiv', att, vi)
        s_new = jnp.exp(b_last[..., 0, :])[..., None] * s + jnp.einsum('bhjk,bhjv->bhkv', k_d, vi)
        return s_new, o

    s_fin, oc = lax.scan(step, s0.astype(jnp.float32), xs)
    o = oc.transpose(1, 0, 3, 2, 4).reshape(bsz, t, nh, dv)
    return o, s_fin


def gla_state(k, v, log_a):
    bcum = jnp.cumsum(log_a, axis=1)
    k_d = k.astype(jnp.float32) * jnp.exp(bcum[:, -1:] - bcum)
    return jnp.einsum('bthk,bthv->bhkv', k_d, v.astype(jnp.float32))


def gla_split_cols(p):
    kt, vt, r = GLA_KEY_DIM, GLA_VAL_DIM, GLA_GATE_RANK
    return jnp.split(p, [kt, 2 * kt, 2 * kt + vt, 2 * kt + 2 * vt, 2 * kt + 2 * vt + r], axis=-1)


def gla_mixer(h, w_in, w_a2, b_a, head_gain, w_out, s0_f, s0_b):
    q, k, v, g, a_f, a_b = gla_split_cols(h @ w_in)
    q = heads(q, GLA_HEAD_K) * (GLA_HEAD_K ** -0.5)
    k = heads(k, GLA_HEAD_K)
    v = heads(v, GLA_HEAD_V)
    la_f = gla_log_decay(a_f, w_a2[0], b_a[0])
    la_b = gla_log_decay(a_b, w_a2[1], b_a[1])
    o_f, s_f = gla_scan(q, k, v, la_f, s0_f)
    flip = functools.partial(jnp.flip, axis=1)
    o_b, s_b = gla_scan(flip(q), flip(k), flip(v), flip(la_b), s0_b)
    o = o_f + flip(o_b)
    o = o * lax.rsqrt(jnp.mean(o * o, axis=-1, keepdims=True) + EPS)
    o = (o.astype(h.dtype) * head_gain).reshape(h.shape[0], h.shape[1], GLA_VAL_DIM)
    return (o * jax.nn.silu(g)) @ w_out, s_f, s_b


def gla_context_states(h, w_in, w_a2, b_a):
    kt, vt = GLA_KEY_DIM, GLA_VAL_DIM
    k, v = jnp.split(h @ w_in[:, kt:2 * kt + vt], [kt], axis=-1)
    a_f, a_b = jnp.split(h @ w_in[:, 2 * kt + 2 * vt:], 2, axis=-1)
    k = heads(k, GLA_HEAD_K)
    v = heads(v, GLA_HEAD_V)
    s_f = gla_state(k, v, gla_log_decay(a_f, w_a2[0], b_a[0]))
    s_b = gla_state(jnp.flip(k, 1), jnp.flip(v, 1), jnp.flip(gla_log_decay(a_b, w_a2[1], b_a[1]), 1))
    return s_f, s_b


def short_conv_mixer(h, w_in, conv_w, w_out, conv_fn):
    bg, cg, v = jnp.split(h @ w_in, 3, axis=-1)
    return (bg * conv_fn(cg * v, conv_w)) @ w_out


def conv_ffn(h, w_up, conv_w, conv_b, w_down, conv_fn):
    u = conv_fn(h @ w_up, conv_w) + conv_b
    a, gt = jnp.split(u, 2, axis=-1)
    return (a * jax.nn.silu(gt)) @ w_down


def _fwd_setup_inputs(seed: int = 0) -> dict:
    key = jax.random.key(seed)
    ks = jax.random.split(key, 24)
    n_a = (DEPTH + N_MIXERS - 1) // N_MIXERS
    n_b = DEPTH // N_MIXERS
    f32 = jnp.float32

    def nrm(k, shape, scale):
        return jax.random.normal(k, shape, f32) * scale

    return {
        'x': nrm(ks[0], (BATCH, SEQ, D_MODEL), 1.0),
        'c': nrm(ks[1], (BATCH, D_MODEL), 1.0),
        'ctx': nrm(ks[2], (BATCH, CTX_LEN, D_MODEL), 1.0),
        'c_ctx': nrm(ks[3], (D_MODEL,), 1.0),
        'ada_w': nrm(ks[4], (DEPTH, D_MODEL, N_MOD * D_MODEL), 0.5 * D_MODEL ** -0.5),
        'ada_b': nrm(ks[5], (DEPTH, N_MOD * D_MODEL), 0.02),
        'norm_mix': 1.0 + nrm(ks[6], (DEPTH, D_MODEL), 0.02),
        'norm_ffn': 1.0 + nrm(ks[7], (DEPTH, D_MODEL), 0.02),
        'gla_w_in': nrm(ks[8], (n_a, D_MODEL, GLA_IN_DIM), D_MODEL ** -0.5),
        'gla_w_a2': nrm(ks[9], (n_a, 2, GLA_GATE_RANK, GLA_KEY_DIM), GLA_GATE_RANK ** -0.5),
        'gla_b_a': nrm(ks[10], (n_a, 2, GLA_KEY_DIM), 0.1),
        'gla_head_norm': 1.0 + nrm(ks[11], (n_a, GLA_HEAD_V), 0.02),
        'gla_w_out': nrm(ks[12], (n_a, GLA_VAL_DIM, D_MODEL), GLA_VAL_DIM ** -0.5),
        'sc_w_in': nrm(ks[13], (n_b, D_MODEL, 3 * SC_DIM), D_MODEL ** -0.5),
        'sc_conv_w': nrm(ks[14], (n_b, CONV_WIDTH, SC_DIM), CONV_WIDTH ** -0.5),
        'sc_w_out': nrm(ks[15], (n_b, SC_DIM, D_MODEL), SC_DIM ** -0.5),
        'ffn_w_up': nrm(ks[16], (DEPTH, D_MODEL, 2 * FFN_HIDDEN), D_MODEL ** -0.5),
        'ffn_conv_w': nrm(ks[17], (DEPTH, CONV_WIDTH, 2 * FFN_HIDDEN), CONV_WIDTH ** -0.5),
        'ffn_conv_b': nrm(ks[18], (DEPTH, 2 * FFN_HIDDEN), 0.02),
        'ffn_w_down': nrm(ks[19], (DEPTH, FFN_HIDDEN, D_MODEL), FFN_HIDDEN ** -0.5),
        'final_norm': 1.0 + nrm(ks[20], (D_MODEL,), 0.02),
    }


def _fwd_reference(x, c, ctx, c_ctx, ada_w, ada_b, norm_mix, norm_ffn, gla_w_in, gla_w_a2, gla_b_a,
              gla_head_norm, gla_w_out, sc_w_in, sc_conv_w, sc_w_out, ffn_w_up, ffn_conv_w,
              ffn_conv_b, ffn_w_down, final_norm):
    rows = x.shape[1] // GRID_W
    conv_lat_rows = functools.partial(conv_grid, rows=rows, axis=2)
    conv_lat_cols = functools.partial(conv_grid, rows=rows, axis=1)
    h, hc = x, ctx
    sc, scc = jax.nn.silu(c), jax.nn.silu(c_ctx)
    for i in range(DEPTH):
        mixer, j = i % N_MIXERS, i // N_MIXERS
        ctx_later = any(l % N_MIXERS == 0 for l in range(i + 1, DEPTH))
        m = [t[:, None, :] for t in jnp.split(sc @ ada_w[i] + ada_b[i], N_MOD, axis=-1)]
        need_ctx = (mixer == 0) or ctx_later
        if need_ctx:
            mc = jnp.split(scc @ ada_w[i] + ada_b[i], N_MOD, axis=-1)
            hnc = modulate(hc, norm_mix[i], mc[0], mc[1])
        hn = modulate(h, norm_mix[i], m[0], m[1])
        if mixer == 0:
            if ctx_later:
                zero = jnp.zeros((hc.shape[0], GLA_HEADS, GLA_HEAD_K, GLA_HEAD_V), jnp.float32)
                yc, s_f, s_b = gla_mixer(hnc, gla_w_in[j], gla_w_a2[j], gla_b_a[j], gla_head_norm[j],
                                         gla_w_out[j], zero, zero)
            else:
                s_f, s_b = gla_context_states(hnc, gla_w_in[j], gla_w_a2[j], gla_b_a[j])
            y, _, _ = gla_mixer(hn, gla_w_in[j], gla_w_a2[j], gla_b_a[j], gla_head_norm[j],
                                gla_w_out[j], s_f, s_b)
        else:
            y = short_conv_mixer(hn, sc_w_in[j], sc_conv_w[j], sc_w_out[j], conv_lat_rows)
            if ctx_later:
                yc = short_conv_mixer(hnc, sc_w_in[j], sc_conv_w[j], sc_w_out[j], conv_seq)
        h = h + m[2] * y
        h = h + m[5] * conv_ffn(modulate(h, norm_ffn[i], m[3], m[4]), ffn_w_up[i], ffn_conv_w[i],
                                ffn_conv_b[i], ffn_w_down[i], conv_lat_cols)
        if ctx_later:
            hc = hc + mc[2] * yc
            hc = hc + mc[5] * conv_ffn(modulate(hc, norm_ffn[i], mc[3], mc[4]), ffn_w_up[i],
                                       ffn_conv_w[i], ffn_conv_b[i], ffn_w_down[i], conv_seq)
    return rmsnorm(h, final_norm)


import jax as _jax
import jax.numpy as _jnp

TWIN_FORMAT = 'train_step'
FWD_PARAMS = ['x', 'c', 'ctx', 'c_ctx', 'ada_w', 'ada_b', 'norm_mix', 'norm_ffn', 'gla_w_in', 'gla_w_a2', 'gla_b_a', 'gla_head_norm', 'gla_w_out', 'sc_w_in', 'sc_conv_w', 'sc_w_out', 'ffn_w_up', 'ffn_conv_w', 'ffn_conv_b', 'ffn_w_down', 'final_norm']
TWIN_WEIGHTS = ['c_ctx', 'ada_w', 'ada_b', 'norm_mix', 'norm_ffn', 'gla_w_in', 'gla_w_a2', 'gla_b_a', 'gla_head_norm', 'gla_w_out', 'sc_w_in', 'sc_conv_w', 'sc_w_out', 'ffn_w_up', 'ffn_conv_w', 'ffn_conv_b', 'ffn_w_down', 'final_norm']
TWIN_DIFF_INPUT = 'x'
TWIN_INPUTS = ['x', 'c', 'ctx', 'c_ctx', 'ada_w', 'ada_b', 'norm_mix', 'norm_ffn', 'gla_w_in', 'gla_w_a2', 'gla_b_a', 'gla_head_norm', 'gla_w_out', 'sc_w_in', 'sc_conv_w', 'sc_w_out', 'ffn_w_up', 'ffn_conv_w', 'ffn_conv_b', 'ffn_w_down', 'final_norm', 'loss_target', 'm_c_ctx', 'm_ada_w', 'm_ada_b', 'm_norm_mix', 'm_norm_ffn', 'm_gla_w_in', 'm_gla_w_a2', 'm_gla_b_a', 'm_gla_head_norm', 'm_gla_w_out', 'm_sc_w_in', 'm_sc_conv_w', 'm_sc_w_out', 'm_ffn_w_up', 'm_ffn_conv_w', 'm_ffn_conv_b', 'm_ffn_w_down', 'm_final_norm', 'v_c_ctx', 'v_ada_w', 'v_ada_b', 'v_norm_mix', 'v_norm_ffn', 'v_gla_w_in', 'v_gla_w_a2', 'v_gla_b_a', 'v_gla_head_norm', 'v_gla_w_out', 'v_sc_w_in', 'v_sc_conv_w', 'v_sc_w_out', 'v_ffn_w_up', 'v_ffn_conv_w', 'v_ffn_conv_b', 'v_ffn_w_down', 'v_final_norm']
TWIN_OUTPUTS = ['loss', 'grad_x', 'grad_c_ctx', 'grad_ada_w', 'grad_ada_b', 'grad_norm_mix', 'grad_norm_ffn', 'grad_gla_w_in', 'grad_gla_w_a2', 'grad_gla_b_a', 'grad_gla_head_norm', 'grad_gla_w_out', 'grad_sc_w_in', 'grad_sc_conv_w', 'grad_sc_w_out', 'grad_ffn_w_up', 'grad_ffn_conv_w', 'grad_ffn_conv_b', 'grad_ffn_w_down', 'grad_final_norm', 'delta_c_ctx', 'delta_ada_w', 'delta_ada_b', 'delta_norm_mix', 'delta_norm_ffn', 'delta_gla_w_in', 'delta_gla_w_a2', 'delta_gla_b_a', 'delta_gla_head_norm', 'delta_gla_w_out', 'delta_sc_w_in', 'delta_sc_conv_w', 'delta_sc_w_out', 'delta_ffn_w_up', 'delta_ffn_conv_w', 'delta_ffn_conv_b', 'delta_ffn_w_down', 'delta_final_norm', 'new_m_c_ctx', 'new_m_ada_w', 'new_m_ada_b', 'new_m_norm_mix', 'new_m_norm_ffn', 'new_m_gla_w_in', 'new_m_gla_w_a2', 'new_m_gla_b_a', 'new_m_gla_head_norm', 'new_m_gla_w_out', 'new_m_sc_w_in', 'new_m_sc_conv_w', 'new_m_sc_w_out', 'new_m_ffn_w_up', 'new_m_ffn_conv_w', 'new_m_ffn_conv_b', 'new_m_ffn_w_down', 'new_m_final_norm', 'new_v_c_ctx', 'new_v_ada_w', 'new_v_ada_b', 'new_v_norm_mix', 'new_v_norm_ffn', 'new_v_gla_w_in', 'new_v_gla_w_a2', 'new_v_gla_b_a', 'new_v_gla_head_norm', 'new_v_gla_w_out', 'new_v_sc_w_in', 'new_v_sc_conv_w', 'new_v_sc_w_out', 'new_v_ffn_w_up', 'new_v_ffn_conv_w', 'new_v_ffn_conv_b', 'new_v_ffn_w_down', 'new_v_final_norm']
TWIN_LEAF_KINDS = {'loss': 'loss', 'grad_x': 'grad_x', 'grad_c_ctx': 'grad_w', 'grad_ada_w': 'grad_w', 'grad_ada_b': 'grad_w', 'grad_norm_mix': 'grad_w', 'grad_norm_ffn': 'grad_w', 'grad_gla_w_in': 'grad_w', 'grad_gla_w_a2': 'grad_w', 'grad_gla_b_a': 'grad_w', 'grad_gla_head_norm': 'grad_w', 'grad_gla_w_out': 'grad_w', 'grad_sc_w_in': 'grad_w', 'grad_sc_conv_w': 'grad_w', 'grad_sc_w_out': 'grad_w', 'grad_ffn_w_up': 'grad_w', 'grad_ffn_conv_w': 'grad_w', 'grad_ffn_conv_b': 'grad_w', 'grad_ffn_w_down': 'grad_w', 'grad_final_norm': 'grad_w', 'delta_c_ctx': 'delta_w', 'delta_ada_w': 'delta_w', 'delta_ada_b': 'delta_w', 'delta_norm_mix': 'delta_w', 'delta_norm_ffn': 'delta_w', 'delta_gla_w_in': 'delta_w', 'delta_gla_w_a2': 'delta_w', 'delta_gla_b_a': 'delta_w', 'delta_gla_head_norm': 'delta_w', 'delta_gla_w_out': 'delta_w', 'delta_sc_w_in': 'delta_w', 'delta_sc_conv_w': 'delta_w', 'delta_sc_w_out': 'delta_w', 'delta_ffn_w_up': 'delta_w', 'delta_ffn_conv_w': 'delta_w', 'delta_ffn_conv_b': 'delta_w', 'delta_ffn_w_down': 'delta_w', 'delta_final_norm': 'delta_w', 'new_m_c_ctx': 'new_m', 'new_m_ada_w': 'new_m', 'new_m_ada_b': 'new_m', 'new_m_norm_mix': 'new_m', 'new_m_norm_ffn': 'new_m', 'new_m_gla_w_in': 'new_m', 'new_m_gla_w_a2': 'new_m', 'new_m_gla_b_a': 'new_m', 'new_m_gla_head_norm': 'new_m', 'new_m_gla_w_out': 'new_m', 'new_m_sc_w_in': 'new_m', 'new_m_sc_conv_w': 'new_m', 'new_m_sc_w_out': 'new_m', 'new_m_ffn_w_up': 'new_m', 'new_m_ffn_conv_w': 'new_m', 'new_m_ffn_conv_b': 'new_m', 'new_m_ffn_w_down': 'new_m', 'new_m_final_norm': 'new_m', 'new_v_c_ctx': 'new_v', 'new_v_ada_w': 'new_v', 'new_v_ada_b': 'new_v', 'new_v_norm_mix': 'new_v', 'new_v_norm_ffn': 'new_v', 'new_v_gla_w_in': 'new_v', 'new_v_gla_w_a2': 'new_v', 'new_v_gla_b_a': 'new_v', 'new_v_gla_head_norm': 'new_v', 'new_v_gla_w_out': 'new_v', 'new_v_sc_w_in': 'new_v', 'new_v_sc_conv_w': 'new_v', 'new_v_sc_w_out': 'new_v', 'new_v_ffn_w_up': 'new_v', 'new_v_ffn_conv_w': 'new_v', 'new_v_ffn_conv_b': 'new_v', 'new_v_ffn_w_down': 'new_v', 'new_v_final_norm': 'new_v'}


def _forward(args):
    return _fwd_reference(*[args[k] for k in FWD_PARAMS])


def _output_shape():
    out = _jax.eval_shape(lambda: _forward(_fwd_setup_inputs(0)))
    return out.shape, out.dtype

N_MICROBATCH = 1
ADAM_LR = 0.001
ADAM_B1 = 0.9
ADAM_B2 = 0.999
ADAM_EPS = 1e-08
ADAM_WD = 0.01
ADAM_STEP = 10
PER_EXAMPLE_BATCH_AXIS = {'x': 0, 'c': 0, 'ctx': 0, 'loss_target': 0}
SHARED_INPUTS = []
_WEIGHT_DTYPES = {'c_ctx': _jnp.float32, 'ada_w': _jnp.float32, 'ada_b': _jnp.float32, 'norm_mix': _jnp.float32, 'norm_ffn': _jnp.float32, 'gla_w_in': _jnp.float32, 'gla_w_a2': _jnp.float32, 'gla_b_a': _jnp.float32, 'gla_head_norm': _jnp.float32, 'gla_w_out': _jnp.float32, 'sc_w_in': _jnp.float32, 'sc_conv_w': _jnp.float32, 'sc_w_out': _jnp.float32, 'ffn_w_up': _jnp.float32, 'ffn_conv_w': _jnp.float32, 'ffn_conv_b': _jnp.float32, 'ffn_w_down': _jnp.float32, 'final_norm': _jnp.float32}
MOMENT_SCALE = {'c_ctx': 6.100946e-03, 'ada_w': 7.600182e-02, 'ada_b': 1.251452e-01, 'norm_mix': 9.785245e-02, 'norm_ffn': 6.452856e-02, 'gla_w_in': 4.955129e-02, 'gla_w_a2': 7.263015e-03, 'gla_b_a': 1.838260e-02, 'gla_head_norm': 1.008690e-01, 'gla_w_out': 4.108159e-02, 'sc_w_in': 6.451342e-02, 'sc_conv_w': 6.522281e-02, 'sc_w_out': 6.411145e-02, 'ffn_w_up': 2.996889e-02, 'ffn_conv_w': 2.978145e-02, 'ffn_conv_b': 2.492300e-02, 'ffn_w_down': 4.686560e-02, 'final_norm': 3.219384e+01}


def _to_microbatches(a, axis):
    t = _jnp.moveaxis(a, axis, 0)
    t = t.reshape((N_MICROBATCH, t.shape[0] // N_MICROBATCH) + t.shape[1:])
    return _jnp.moveaxis(t, 1, axis + 1)


def setup_inputs(seed: int = 0) -> dict:
    inp = _fwd_setup_inputs(seed)
    key = _jax.random.fold_in(_jax.random.key(seed), 7919)
    shape, _ = _output_shape()
    out = dict(inp)
    out["loss_target"] = _jax.random.normal(_jax.random.fold_in(key, 0), shape, _jnp.float32)
    for i, name in enumerate(TWIN_WEIGHTS):
        w = inp[name].astype(_jnp.float32)
        if MOMENT_SCALE is None:
            s = _jnp.sqrt(_jnp.mean(_jnp.square(w)) + 1e-30)
        else:
            s = MOMENT_SCALE[name]
        km, kv = _jax.random.split(_jax.random.fold_in(key, i + 1))
        out[name] = w
        out["m_" + name] = s * _jax.random.normal(km, w.shape, _jnp.float32)
        out["v_" + name] = (s * s) * _jax.random.uniform(kv, w.shape, _jnp.float32, 0.5, 1.5)
    if N_MICROBATCH > 1:
        for name, axis in PER_EXAMPLE_BATCH_AXIS.items():
            out[name] = _to_microbatches(out[name], axis)
    return {'x': out['x'], 'c': out['c'], 'ctx': out['ctx'], 'c_ctx': out['c_ctx'], 'ada_w': out['ada_w'], 'ada_b': out['ada_b'], 'norm_mix': out['norm_mix'], 'norm_ffn': out['norm_ffn'], 'gla_w_in': out['gla_w_in'], 'gla_w_a2': out['gla_w_a2'], 'gla_b_a': out['gla_b_a'], 'gla_head_norm': out['gla_head_norm'], 'gla_w_out': out['gla_w_out'], 'sc_w_in': out['sc_w_in'], 'sc_conv_w': out['sc_conv_w'], 'sc_w_out': out['sc_w_out'], 'ffn_w_up': out['ffn_w_up'], 'ffn_conv_w': out['ffn_conv_w'], 'ffn_conv_b': out['ffn_conv_b'], 'ffn_w_down': out['ffn_w_down'], 'final_norm': out['final_norm'], 'loss_target': out['loss_target'], 'm_c_ctx': out['m_c_ctx'], 'm_ada_w': out['m_ada_w'], 'm_ada_b': out['m_ada_b'], 'm_norm_mix': out['m_norm_mix'], 'm_norm_ffn': out['m_norm_ffn'], 'm_gla_w_in': out['m_gla_w_in'], 'm_gla_w_a2': out['m_gla_w_a2'], 'm_gla_b_a': out['m_gla_b_a'], 'm_gla_head_norm': out['m_gla_head_norm'], 'm_gla_w_out': out['m_gla_w_out'], 'm_sc_w_in': out['m_sc_w_in'], 'm_sc_conv_w': out['m_sc_conv_w'], 'm_sc_w_out': out['m_sc_w_out'], 'm_ffn_w_up': out['m_ffn_w_up'], 'm_ffn_conv_w': out['m_ffn_conv_w'], 'm_ffn_conv_b': out['m_ffn_conv_b'], 'm_ffn_w_down': out['m_ffn_w_down'], 'm_final_norm': out['m_final_norm'], 'v_c_ctx': out['v_c_ctx'], 'v_ada_w': out['v_ada_w'], 'v_ada_b': out['v_ada_b'], 'v_norm_mix': out['v_norm_mix'], 'v_norm_ffn': out['v_norm_ffn'], 'v_gla_w_in': out['v_gla_w_in'], 'v_gla_w_a2': out['v_gla_w_a2'], 'v_gla_b_a': out['v_gla_b_a'], 'v_gla_head_norm': out['v_gla_head_norm'], 'v_gla_w_out': out['v_gla_w_out'], 'v_sc_w_in': out['v_sc_w_in'], 'v_sc_conv_w': out['v_sc_conv_w'], 'v_sc_w_out': out['v_sc_w_out'], 'v_ffn_w_up': out['v_ffn_w_up'], 'v_ffn_conv_w': out['v_ffn_conv_w'], 'v_ffn_conv_b': out['v_ffn_conv_b'], 'v_ffn_w_down': out['v_ffn_w_down'], 'v_final_norm': out['v_final_norm']}


def _loss(weights, diff, rest, loss_target):
    with _jax.named_scope("forward"):
        args = {**rest, TWIN_DIFF_INPUT: diff, **{k: w.astype(_WEIGHT_DTYPES[k]) for k, w in weights.items()}}
        y = _forward(args)
    with _jax.named_scope("loss_head"):
        err = _jnp.square(y.astype(_jnp.float32) - loss_target)
        return 0.5 * _jnp.sum(_jnp.mean(err, axis=-1)) if err.ndim else 0.5 * err


def _adamw(w, g, m, v):
    m = ADAM_B1 * m + (1.0 - ADAM_B1) * g
    v = ADAM_B2 * v + (1.0 - ADAM_B2) * _jnp.square(g)
    m_hat = m / (1.0 - ADAM_B1 ** ADAM_STEP)
    v_hat = v / (1.0 - ADAM_B2 ** ADAM_STEP)
    delta = -ADAM_LR * (m_hat / (_jnp.sqrt(v_hat) + ADAM_EPS) + ADAM_WD * w)
    return delta, m, v


def reference(x, c, ctx, c_ctx, ada_w, ada_b, norm_mix, norm_ffn, gla_w_in, gla_w_a2, gla_b_a, gla_head_norm, gla_w_out, sc_w_in, sc_conv_w, sc_w_out, ffn_w_up, ffn_conv_w, ffn_conv_b, ffn_w_down, final_norm, loss_target, m_c_ctx, m_ada_w, m_ada_b, m_norm_mix, m_norm_ffn, m_gla_w_in, m_gla_w_a2, m_gla_b_a, m_gla_head_norm, m_gla_w_out, m_sc_w_in, m_sc_conv_w, m_sc_w_out, m_ffn_w_up, m_ffn_conv_w, m_ffn_conv_b, m_ffn_w_down, m_final_norm, v_c_ctx, v_ada_w, v_ada_b, v_norm_mix, v_norm_ffn, v_gla_w_in, v_gla_w_a2, v_gla_b_a, v_gla_head_norm, v_gla_w_out, v_sc_w_in, v_sc_conv_w, v_sc_w_out, v_ffn_w_up, v_ffn_conv_w, v_ffn_conv_b, v_ffn_w_down, v_final_norm):
    given = dict(x=x, c=c, ctx=ctx, c_ctx=c_ctx, ada_w=ada_w, ada_b=ada_b, norm_mix=norm_mix, norm_ffn=norm_ffn, gla_w_in=gla_w_in, gla_w_a2=gla_w_a2, gla_b_a=gla_b_a, gla_head_norm=gla_head_norm, gla_w_out=gla_w_out, sc_w_in=sc_w_in, sc_conv_w=sc_conv_w, sc_w_out=sc_w_out, ffn_w_up=ffn_w_up, ffn_conv_w=ffn_conv_w, ffn_conv_b=ffn_conv_b, ffn_w_down=ffn_w_down, final_norm=final_norm, loss_target=loss_target, m_c_ctx=m_c_ctx, m_ada_w=m_ada_w, m_ada_b=m_ada_b, m_norm_mix=m_norm_mix, m_norm_ffn=m_norm_ffn, m_gla_w_in=m_gla_w_in, m_gla_w_a2=m_gla_w_a2, m_gla_b_a=m_gla_b_a, m_gla_head_norm=m_gla_head_norm, m_gla_w_out=m_gla_w_out, m_sc_w_in=m_sc_w_in, m_sc_conv_w=m_sc_conv_w, m_sc_w_out=m_sc_w_out, m_ffn_w_up=m_ffn_w_up, m_ffn_conv_w=m_ffn_conv_w, m_ffn_conv_b=m_ffn_conv_b, m_ffn_w_down=m_ffn_w_down, m_final_norm=m_final_norm, v_c_ctx=v_c_ctx, v_ada_w=v_ada_w, v_ada_b=v_ada_b, v_norm_mix=v_norm_mix, v_norm_ffn=v_norm_ffn, v_gla_w_in=v_gla_w_in, v_gla_w_a2=v_gla_w_a2, v_gla_b_a=v_gla_b_a, v_gla_head_norm=v_gla_head_norm, v_gla_w_out=v_gla_w_out, v_sc_w_in=v_sc_w_in, v_sc_conv_w=v_sc_conv_w, v_sc_w_out=v_sc_w_out, v_ffn_w_up=v_ffn_w_up, v_ffn_conv_w=v_ffn_conv_w, v_ffn_conv_b=v_ffn_conv_b, v_ffn_w_down=v_ffn_w_down, v_final_norm=v_final_norm)
    weights = {n: given[n] for n in TWIN_WEIGHTS}
    shared = {n: given[n] for n in SHARED_INPUTS}
    per_example = {n: given[n] for n in ['x', 'c', 'ctx']}
    grad_fn = _jax.value_and_grad(_loss, argnums=(0, 1))

    def one_microbatch(ex, loss_target):
        ex = dict(ex)
        diff = ex.pop(TWIN_DIFF_INPUT)
        return grad_fn(weights, diff, {**shared, **ex}, loss_target)

    if N_MICROBATCH == 1:
        loss, (grad_w, grad_x) = one_microbatch(per_example, given["loss_target"])
    else:
        def body(carry, xs):
            loss_sum, grad_sum = carry
            l_k, (gw_k, gx_k) = one_microbatch(xs[0], xs[1])
            with _jax.named_scope("update"):
                return (loss_sum + l_k, _jax.tree.map(_jnp.add, grad_sum, gw_k)), gx_k

        init = (_jnp.zeros((), _jnp.float32), _jax.tree.map(_jnp.zeros_like, weights))
        (loss, grad_w), grad_x = _jax.lax.scan(body, init, (per_example, given["loss_target"]))
    with _jax.named_scope("update"):
        delta_w, new_m, new_v = {}, {}, {}
        for n in TWIN_WEIGHTS:
            delta_w[n], new_m[n], new_v[n] = _adamw(weights[n], grad_w[n], given["m_" + n], given["v_" + n])
    return (loss, grad_x, *[grad_w[n] for n in TWIN_WEIGHTS], *[delta_w[n] for n in TWIN_WEIGHTS],
            *[new_m[n] for n in TWIN_WEIGHTS], *[new_v[n] for n in TWIN_WEIGHTS])
```

```python
import functools

import jax
import jax.numpy as jnp
from jax import lax
from jax.experimental import pallas as pl
from jax.experimental.pallas import tpu as pltpu

F32 = jnp.float32
BF16 = jnp.bfloat16
MESH = pl.DeviceIdType.MESH

EPS = 1e-6
D = 1024
N_MOD = 6
HEADS = 4
DK = 128
DV = 256
KEY = HEADS * DK
RANK = 16
TAU = 16.0
CH = 64
GRID_W = 64
HID = 2560
GLA_IN = 2 * KEY + 2 * D + 2 * RANK
GLA_IN_PAD = 3200
Q_SCALE = DK ** -0.5
N_CHIPS = 4
N_DEV = 8

ADAM_LR = 0.001
ADAM_B1 = 0.9
ADAM_B2 = 0.999
ADAM_EPS = 1e-08
ADAM_WD = 0.01
ADAM_STEP = 10

VMEM_LIMIT = 56 * 1024 * 1024


def _params(sem):
    return pltpu.CompilerParams(dimension_semantics=sem, vmem_limit_bytes=VMEM_LIMIT)


def _tile(n, pref, mult=8):
    if n <= pref:
        return n
    for t in range(pref - pref % mult, 0, -mult):
        if n % t == 0:
            return t
    raise ValueError((n, pref, mult))


_NN = (((1,), (0,)), ((), ()))
_NT = (((1,), (1,)), ((), ()))
_TN = (((0,), (0,)), ((), ()))


def _dot(a, b, dims=_NN):
    return lax.dot_general(a.astype(BF16), b.astype(BF16), dims, preferred_element_type=F32)


def _dot_hi(a, b, dims=_NN):
    return lax.dot_general(a, b, dims, precision=lax.Precision.HIGHEST, preferred_element_type=F32)


def _sigmoid(x):
    return 1.0 / (1.0 + jnp.exp(-x))


def _rowsum(x):
    return jnp.sum(x, axis=0, keepdims=True)


def _mm(a, b, form, out_dtype, name, tm=512, tn=512, tk=512):
    if form == "tn":
        K, M = a.shape
    else:
        M, K = a.shape
    N = b.shape[0] if form == "nt" else b.shape[1]
    tm = _tile(M, tm, 128)
    tn = _tile(N, tn, 128)
    tk = _tile(K, tk, 128)
    nk = K // tk
    dims = {"nn": _NN, "nt": _NT, "tn": _TN}[form]

    def body(a_ref, b_ref, o_ref, acc_ref):
        k = pl.program_id(2)

        @pl.when(k == 0)
        def _():
            acc_ref[...] = jnp.zeros_like(acc_ref)

        acc_ref[...] += _dot(a_ref[...], b_ref[...], dims)

        @pl.when(k == nk - 1)
        def _():
            o_ref[...] = acc_ref[...].astype(o_ref.dtype)

    if form == "tn":
        a_spec = pl.BlockSpec((tk, tm), lambda i, j, k: (k, i))
    else:
        a_spec = pl.BlockSpec((tm, tk), lambda i, j, k: (i, k))
    if form == "nt":
        b_spec = pl.BlockSpec((tn, tk), lambda i, j, k: (j, k))
    else:
        b_spec = pl.BlockSpec((tk, tn), lambda i, j, k: (k, j))
    return pl.pallas_call(
        body,
        name=name,
        grid=(M // tm, N // tn, nk),
        in_specs=[a_spec, b_spec],
        out_specs=pl.BlockSpec((tm, tn), lambda i, j, k: (i, j)),
        out_shape=jax.ShapeDtypeStruct((M, N), out_dtype),
        scratch_shapes=[pltpu.VMEM((tm, tn), F32)],
        compiler_params=_params(("parallel", "parallel", "arbitrary")),
    )(a, b)


def _mod_fwd(h, gain, shift, scale, tpb_rows, name, y=None, gate=None):
    n = h.shape[0]
    tt = _tile(tpb_rows, 256)
    tpb = tpb_rows // tt
    has_res = y is not None

    def body(*refs):
        if has_res:
            h_ref, y_ref, gate_ref, gain_ref, sh_ref, sc_ref, hout_ref, hn_ref = refs
            hv = h_ref[...] + gate_ref[0] * y_ref[...]
            hout_ref[...] = hv
        else:
            h_ref, gain_ref, sh_ref, sc_ref, hn_ref = refs
            hv = h_ref[...]
        r = lax.rsqrt(jnp.mean(hv * hv, axis=-1, keepdims=True) + EPS)
        hn = (hv * r) * gain_ref[...] * (1.0 + sc_ref[0]) + sh_ref[0]
        hn_ref[...] = hn.astype(BF16)

    row = pl.BlockSpec((tt, D), lambda i: (i, 0))
    per_b = pl.BlockSpec((1, 1, D), lambda i: (i // tpb, 0, 0))
    vec = pl.BlockSpec((1, D), lambda i: (0, 0))
    if has_res:
        in_specs = [row, row, per_b, vec, per_b, per_b]
        args = (h, y, gate, gain, shift, scale)
        out_specs = [row, row]
        out_shape = [jax.ShapeDtypeStruct((n, D), F32), jax.ShapeDtypeStruct((n, D), BF16)]
    else:
        in_specs = [row, vec, per_b, per_b]
        args = (h, gain, shift, scale)
        out_specs = row
        out_shape = jax.ShapeDtypeStruct((n, D), BF16)
    return pl.pallas_call(
        body, name=name, grid=(n // tt,), in_specs=in_specs, out_specs=out_specs, out_shape=out_shape,
        compiler_params=_params(("parallel",)),
    )(*args)


def _mod_bwd(h_in, dhn, gain, scale, tpb_rows, name, dhn_row0=0, dh_out=None, y_prev=None, gate_prev=None,
             need_dh=True):
    n = h_in.shape[0]
    nb = n // tpb_rows
    tt = _tile(tpb_rows, 256)
    tpb = tpb_rows // tt
    off = dhn_row0 // tt
    assert dhn_row0 % tt == 0
    has_out = dh_out is not None
    has_prev = y_prev is not None

    def body(*refs):
        it = iter(refs)
        h_ref, dhn_ref, gain_ref, sc_ref = next(it), next(it), next(it), next(it)
        dho_ref = next(it) if has_out else None
        yp_ref, gp_ref = (next(it), next(it)) if has_prev else (None, None)
        dh_ref = next(it) if need_dh else None
        dsc_ref, dsh_ref, dgain_ref = next(it), next(it), next(it)
        dyp_ref, dgp_ref = (next(it), next(it)) if has_prev else (None, None)
        i = pl.program_id(0)

        @pl.when(i == 0)
        def _():
            dgain_ref[...] = jnp.zeros_like(dgain_ref)

        @pl.when(i % tpb == 0)
        def _():
            dsc_ref[...] = jnp.zeros_like(dsc_ref)
            dsh_ref[...] = jnp.zeros_like(dsh_ref)
            if has_prev:
                dgp_ref[...] = jnp.zeros_like(dgp_ref)

        hv = h_ref[...]
        r = lax.rsqrt(jnp.mean(hv * hv, axis=-1, keepdims=True) + EPS)
        y = hv * r
        gain_v = gain_ref[...]
        g = dhn_ref[...].astype(F32)
        dsh_ref[0] += _rowsum(g)
        dsc_ref[0] += _rowsum(g * (y * gain_v))
        drn = g * (1.0 + sc_ref[0])
        dgain_ref[...] += _rowsum(drn * y)
        if need_dh:
            dy = drn * gain_v
            dh = r * (dy - y * jnp.mean(dy * y, axis=-1, keepdims=True))
            if has_out:
                dh = dh + dho_ref[...]
            dh_ref[...] = dh
            if has_prev:
                dyp_ref[...] = (dh * gp_ref[0]).astype(BF16)
                dgp_ref[0] += _rowsum(dh * yp_ref[...])

    row = pl.BlockSpec((tt, D), lambda i: (i, 0))
    row_off = pl.BlockSpec((tt, D), lambda i: (i + off, 0))
    per_b = pl.BlockSpec((1, 1, D), lambda i: (i // tpb, 0, 0))
    vec = pl.BlockSpec((1, D), lambda i: (0, 0))
    in_specs = [row, row_off, vec, per_b]
    args = [h_in, dhn, gain, scale]
    if has_out:
        in_specs.append(row)
        args.append(dh_out)
    if has_prev:
        in_specs += [row, per_b]
        args += [y_prev, gate_prev]
    out_specs, out_shape, names = [], [], []
    if need_dh:
        out_specs.append(row)
        out_shape.append(jax.ShapeDtypeStruct((n, D), F32))
        names.append("dh")
    for nm in ("dscale", "dshift"):
        out_specs.append(per_b)
        out_shape.append(jax.ShapeDtypeStruct((nb, 1, D), F32))
        names.append(nm)
    out_specs.append(vec)
    out_shape.append(jax.ShapeDtypeStruct((1, D), F32))
    names.append("dgain")
    if has_prev:
        out_specs += [row, per_b]
        out_shape += [jax.ShapeDtypeStruct((n, D), BF16), jax.ShapeDtypeStruct((nb, 1, D), F32)]
        names += ["dy_prev", "dgate_prev"]
    outs = pl.pallas_call(
        body, name=name, grid=(n // tt,), in_specs=in_specs, out_specs=out_specs, out_shape=out_shape,
        compiler_params=_params(("arbitrary",)),
    )(*args)
    return dict(zip(names, outs))


def _final(h, f, gate, gain, tgt, tpb_rows):
    n = h.shape[0]
    nb = n // tpb_rows
    tt = _tile(tpb_rows, 256)
    tpb = tpb_rows // tt

    def body(h_ref, f_ref, gate_ref, gain_ref, tgt_ref, loss_ref, dh_ref, df_ref, dgate_ref, dgain_ref):
        i = pl.program_id(0)

        @pl.when(i == 0)
        def _():
            loss_ref[...] = jnp.zeros_like(loss_ref)
            dgain_ref[...] = jnp.zeros_like(dgain_ref)

        @pl.when(i % tpb == 0)
        def _():
            dgate_ref[...] = jnp.zeros_like(dgate_ref)

        fv = f_ref[...]
        gate_v = gate_ref[0]
        hv = h_ref[...] + gate_v * fv
        r = lax.rsqrt(jnp.mean(hv * hv, axis=-1, keepdims=True) + EPS)
        y = hv * r
        gain_v = gain_ref[...]
        e = y * gain_v - tgt_ref[...]
        s = jnp.sum(_rowsum(e * e), axis=1, keepdims=True) * (0.5 / D)
        loss_ref[...] += jnp.broadcast_to(s, loss_ref.shape)
        dout = e * (1.0 / D)
        dgain_ref[...] += _rowsum(dout * y)
        dy = dout * gain_v
        dh = r * (dy - y * jnp.mean(dy * y, axis=-1, keepdims=True))
        dh_ref[...] = dh
        df_ref[...] = (dh * gate_v).astype(BF16)
        dgate_ref[0] += _rowsum(dh * fv)

    row = pl.BlockSpec((tt, D), lambda i: (i, 0))
    per_b = pl.BlockSpec((1, 1, D), lambda i: (i // tpb, 0, 0))
    vec = pl.BlockSpec((1, D), lambda i: (0, 0))
    return pl.pallas_call(
        body, name="final_loss", grid=(n // tt,),
        in_specs=[row, row, per_b, vec, row],
        out_specs=[pl.BlockSpec((1, 128), lambda i: (0, 0)), row, row, per_b, vec],
        out_shape=[jax.ShapeDtypeStruct((1, 128), F32), jax.ShapeDtypeStruct((n, D), F32),
                   jax.ShapeDtypeStruct((n, D), BF16), jax.ShapeDtypeStruct((nb, 1, D), F32),
                   jax.ShapeDtypeStruct((1, D), F32)],
        compiler_params=_params(("arbitrary",)),
    )(h, f, gate, gain, tgt)


def _shift_dn(x, s):
    return jnp.concatenate([jnp.zeros((s, x.shape[1]), x.dtype), x[: x.shape[0] - s]], axis=0)


def _shift_up(x, s):
    return jnp.concatenate([x[s:], jnp.zeros((s, x.shape[1]), x.dtype)], axis=0)


def _row_dn1(x):
    t = lax.broadcasted_iota(jnp.int32, x.shape, 0)
    return jnp.where(t % GRID_W == 0, 0.0, pltpu.roll(x, 1, 0))


def _row_up1(x):
    t = lax.broadcasted_iota(jnp.int32, x.shape, 0)
    return jnp.where(t % GRID_W == GRID_W - 1, 0.0, pltpu.roll(x, x.shape[0] - 1, 0))


def _silu(x):
    return x * _sigmoid(x)


def _dsilu(x):
    s = _sigmoid(x)
    return s * (1.0 + x * (1.0 - s))


def _ffn_mid_fwd(u0, cw, cb, nb, t, name):
    nc = HID // 128

    def body(u_ref, w_ref, b_ref, z_ref):
        x = u_ref[...]
        u = (_shift_dn(x, GRID_W) * w_ref[0:1, :] + x * w_ref[1:2, :] + _shift_up(x, GRID_W) * w_ref[2:3, :]
             + b_ref[...])
        z_ref[...] = (u[:, :128] * _silu(u[:, 128:])).astype(BF16)

    return pl.pallas_call(
        body, name=name, grid=(nc, nb),
        in_specs=[pl.BlockSpec((t, 256), lambda j, b: (b, j)), pl.BlockSpec((3, 256), lambda j, b: (0, j)),
                  pl.BlockSpec((1, 256), lambda j, b: (0, j))],
        out_specs=pl.BlockSpec((t, 128), lambda j, b: (b, j)),
        out_shape=jax.ShapeDtypeStruct((nb * t, HID), BF16),
        compiler_params=_params(("parallel", "parallel")),
    )(u0, cw, cb)


def _ffn_mid_bwd(u0, cw, cb, dz, nb, t, name):
    nc = HID // 128

    def body(u_ref, w_ref, b_ref, dz_ref, du_ref, dw_ref, db_ref):
        b = pl.program_id(1)

        @pl.when(b == 0)
        def _():
            dw_ref[...] = jnp.zeros_like(dw_ref)
            db_ref[...] = jnp.zeros_like(db_ref)

        x = u_ref[...]
        w0, w1, w2 = w_ref[0:1, :], w_ref[1:2, :], w_ref[2:3, :]
        xd = _shift_dn(x, GRID_W)
        xu = _shift_up(x, GRID_W)
        u = xd * w0 + x * w1 + xu * w2 + b_ref[...]
        a = u[:, :128]
        gt = u[:, 128:]
        dzv = dz_ref[...]
        du = jnp.concatenate([dzv * _silu(gt), dzv * a * _dsilu(gt)], axis=1)
        db_ref[...] += _rowsum(du)
        dw_ref[0:1, :] += _rowsum(du * xd)
        dw_ref[1:2, :] += _rowsum(du * x)
        dw_ref[2:3, :] += _rowsum(du * xu)
        dx = _shift_up(du, GRID_W) * w0 + du * w1 + _shift_dn(du, GRID_W) * w2
        du_ref[...] = dx.astype(BF16)

    return pl.pallas_call(
        body, name=name, grid=(nc, nb),
        in_specs=[pl.BlockSpec((t, 256), lambda j, b: (b, j)), pl.BlockSpec((3, 256), lambda j, b: (0, j)),
                  pl.BlockSpec((1, 256), lambda j, b: (0, j)), pl.BlockSpec((t, 128), lambda j, b: (b, j))],
        out_specs=[pl.BlockSpec((t, 256), lambda j, b: (b, j)), pl.BlockSpec((3, 256), lambda j, b: (0, j)),
                   pl.BlockSpec((1, 256), lambda j, b: (0, j))],
        out_shape=[jax.ShapeDtypeStruct((nb * t, 2 * HID), BF16), jax.ShapeDtypeStruct((3, 2 * HID), F32),
                   jax.ShapeDtypeStruct((1, 2 * HID), F32)],
        compiler_params=_params(("parallel", "arbitrary")),
    )(u0, cw, cb, dz)


def _sc_mid_fwd(p, cw, nb, t):
    nc = D // 128

    def body(p_ref, w_ref, y_ref):
        x = p_ref[...]
        cv = x[:, 128:256] * x[:, 256:]
        cc = _row_dn1(cv) * w_ref[0:1, :] + cv * w_ref[1:2, :] + _row_up1(cv) * w_ref[2:3, :]
        y_ref[...] = (x[:, :128] * cc).astype(BF16)

    return pl.pallas_call(
        body, name="sc_mid_fwd", grid=(nc, nb),
        in_specs=[pl.BlockSpec((t, 384), lambda j, b: (b, j)), pl.BlockSpec((3, 128), lambda j, b: (0, j))],
        out_specs=pl.BlockSpec((t, 128), lambda j, b: (b, j)),
        out_shape=jax.ShapeDtypeStruct((nb * t, D), BF16),
        compiler_params=_params(("parallel", "parallel")),
    )(p, cw)


def _sc_mid_bwd(p, cw, dyb, nb, t):
    nc = D // 128

    def body(p_ref, w_ref, dy_ref, dp_ref, dw_ref):
        b = pl.program_id(1)

        @pl.when(b == 0)
        def _():
            dw_ref[...] = jnp.zeros_like(dw_ref)

        x = p_ref[...]
        w0, w1, w2 = w_ref[0:1, :], w_ref[1:2, :], w_ref[2:3, :]
        bg, cg, v = x[:, :128], x[:, 128:256], x[:, 256:]
        cv = cg * v
        cvd = _row_dn1(cv)
        cvu = _row_up1(cv)
        cc = cvd * w0 + cv * w1 + cvu * w2
        dy = dy_ref[...]
        dcc = dy * bg
        dw_ref[0:1, :] += _rowsum(dcc * cvd)
        dw_ref[1:2, :] += _rowsum(dcc * cv)
        dw_ref[2:3, :] += _rowsum(dcc * cvu)
        dcv = _row_up1(dcc) * w0 + dcc * w1 + _row_dn1(dcc) * w2
        dp_ref[...] = jnp.concatenate([dy * cc, dcv * v, dcv * cg], axis=1).astype(BF16)

    return pl.pallas_call(
        body, name="sc_mid_bwd", grid=(nc, nb),
        in_specs=[pl.BlockSpec((t, 384), lambda j, b: (b, j)), pl.BlockSpec((3, 128), lambda j, b: (0, j)),
                  pl.BlockSpec((t, 128), lambda j, b: (b, j))],
        out_specs=[pl.BlockSpec((t, 384), lambda j, b: (b, j)), pl.BlockSpec((3, 128), lambda j, b: (0, j))],
        out_shape=[jax.ShapeDtypeStruct((nb * t, 3 * D), BF16), jax.ShapeDtypeStruct((3, D), F32)],
        compiler_params=_params(("parallel", "arbitrary")),
    )(p, cw, dyb)


def _gla_decay_fwd(p_all, w2, b2):
    n = p_all.shape[0]
    tt = _tile(n, 512)

    def body(a_ref, w_ref, b_ref, la_ref):
        z = _dot(a_ref[...], w_ref[...]) + b_ref[...]
        la_ref[...] = (jnp.minimum(z, 0.0) - jnp.log(1.0 + jnp.exp(-jnp.abs(z)))) * (1.0 / TAU)

    return pl.pallas_call(
        body, name="gla_decay_fwd", grid=(n // tt,),
        in_specs=[pl.BlockSpec((tt, 128), lambda i: (i, (2 * KEY + 2 * D) // 128)),
                  pl.BlockSpec((128, 2 * KEY), lambda i: (0, 0)), pl.BlockSpec((1, 2 * KEY), lambda i: (0, 0))],
        out_specs=pl.BlockSpec((tt, 2 * KEY), lambda i: (i, 0)),
        out_shape=jax.ShapeDtypeStruct((n, 2 * KEY), F32),
        compiler_params=_params(("parallel",)),
    )(p_all, w2, b2)


def _gla_blocks(nb, nm, ncx):
    def main_idx(d, i):
        return jnp.clip(jnp.where(d == 0, i - ncx, nm - 1 - (i - ncx)), 0, nm - 1)

    def rowblk(d, b, i):
        cidx = jnp.where(d == 0, i, ncx - 1 - i)
        return jnp.where(i < ncx, nb * nm + b * ncx + cidx, b * nm + main_idx(d, i))

    def mainblk(d, b, i):
        return b * nm + main_idx(d, i)

    return rowblk, mainblk


def _gla_chunk(d, q_ref, k_ref, la_ref):
    g = la_ref[...]
    row = lax.broadcasted_iota(jnp.int32, (CH, CH), 0)
    col = lax.broadcasted_iota(jnp.int32, (CH, CH), 1)
    diff = row - col
    mask = jnp.where(d == 0, diff, -diff) >= 0
    mf = jnp.where(mask, 1.0, 0.0).astype(F32)
    bc = _dot_hi(mf, g)
    bl = _rowsum(g)
    eq = jnp.exp(bc)
    ek = jnp.exp(-bc)
    ed = jnp.exp(bl - bc)
    kv = k_ref[...]
    qs = q_ref[...] * Q_SCALE * eq
    ks = kv * ek
    kd = kv * ed
    return mask, mf, bl, eq, ek, ed, qs, ks, kd


def _gla_scan_fwd(p_all, la_all, nb, t, tc):
    nm, ncx = t // CH, tc // CH
    nst = nm + ncx
    rowblk, mainblk = _gla_blocks(nb, nm, ncx)

    def body(q_ref, k_ref, v_ref, la_ref, o_ref, ss_ref, st_ref):
        d = pl.program_id(0)
        i = pl.program_id(3)

        @pl.when(i == 0)
        def _():
            st_ref[...] = jnp.zeros_like(st_ref)

        mask, _, bl, _, _, _, qs, ks, kd = _gla_chunk(d, q_ref, k_ref, la_ref)
        st = st_ref[...]
        ss_ref[0, 0, 0, 0] = st
        v = v_ref[...]
        att = jnp.where(mask, _dot(qs, ks, _NT), 0.0)
        o_ref[0] = _dot(qs, st, _NT) + _dot(att, v)
        st_ref[...] = st * jnp.exp(bl) + _dot(v, kd, _TN)

    return pl.pallas_call(
        body, name="gla_scan_fwd", grid=(2, nb, HEADS, nst),
        in_specs=[
            pl.BlockSpec((CH, DK), lambda d, b, h, i: (rowblk(d, b, i), h)),
            pl.BlockSpec((CH, DK), lambda d, b, h, i: (rowblk(d, b, i), HEADS + h)),
            pl.BlockSpec((CH, DV), lambda d, b, h, i: (rowblk(d, b, i), HEADS + h)),
            pl.BlockSpec((CH, DK), lambda d, b, h, i: (rowblk(d, b, i), d * HEADS + h)),
        ],
        out_specs=[
            pl.BlockSpec((1, CH, DV), lambda d, b, h, i: (d, mainblk(d, b, i), h)),
            pl.BlockSpec((1, 1, 1, 1, DV, DK), lambda d, b, h, i: (d, b, h, i, 0, 0)),
        ],
        out_shape=[jax.ShapeDtypeStruct((2, nb * t, D), F32),
                   jax.ShapeDtypeStruct((2, nb, HEADS, nst, DV, DK), F32)],
        scratch_shapes=[pltpu.VMEM((DV, DK), F32)],
        compiler_params=_params(("parallel", "parallel", "parallel", "arbitrary")),
    )(p_all, p_all, p_all, la_all)


def _gla_scan_bwd(p_all, la_all, do, ss, nb, t, tc):
    nm, ncx = t // CH, tc // CH
    nst = nm + ncx
    ntot = nb * (t + tc)
    rowblk, mainblk = _gla_blocks(nb, nm, ncx)

    def body(q_ref, k_ref, v_ref, la_ref, do_ref, ss_ref, dq_ref, dk_ref, dv_ref, dla_ref, dst_ref):
        d = pl.program_id(0)
        ip = pl.program_id(3)
        i = nst - 1 - ip

        @pl.when(ip == 0)
        def _():
            dst_ref[...] = jnp.zeros_like(dst_ref)

        mask, mf, bl, eq, ek, ed, qs, ks, kd = _gla_chunk(d, q_ref, k_ref, la_ref)
        st = ss_ref[0, 0, 0, 0]
        dst = dst_ref[...]
        v = v_ref[...]
        dov = do_ref[...] * jnp.where(i >= ncx, 1.0, 0.0)
        att = jnp.where(mask, _dot(qs, ks, _NT), 0.0)
        datt = jnp.where(mask, _dot(dov, v, _NT), 0.0)
        dqs = _dot(dov, st) + _dot(datt, ks)
        dks = _dot(datt, qs, _TN)
        dv_ref[0] = _dot(att, dov, _TN) + _dot(kd, dst, _NT)
        dkd = _dot(v, dst)
        e = jnp.exp(bl)
        dbl = e * _rowsum(st * dst) + _rowsum(dkd * kd)
        dst_ref[...] = _dot(dov, qs, _TN) + dst * e
        dq_ref[0] = dqs * eq * Q_SCALE
        dk_ref[0] = dks * ek + dkd * ed
        db = dqs * qs - dks * ks - dkd * kd
        dla_ref[...] = _dot_hi(mf, db, _TN) + dbl

    rev = lambda f: (lambda d, b, h, ip: f(d, b, h, nst - 1 - ip))
    return pl.pallas_call(
        body, name="gla_scan_bwd", grid=(2, nb, HEADS, nst),
        in_specs=[
            pl.BlockSpec((CH, DK), rev(lambda d, b, h, i: (rowblk(d, b, i), h))),
            pl.BlockSpec((CH, DK), rev(lambda d, b, h, i: (rowblk(d, b, i), HEADS + h))),
            pl.BlockSpec((CH, DV), rev(lambda d, b, h, i: (rowblk(d, b, i), HEADS + h))),
            pl.BlockSpec((CH, DK), rev(lambda d, b, h, i: (rowblk(d, b, i), d * HEADS + h))),
            pl.BlockSpec((CH, DV), rev(lambda d, b, h, i: (mainblk(d, b, i), h))),
            pl.BlockSpec((1, 1, 1, 1, DV, DK), rev(lambda d, b, h, i: (d, b, h, i, 0, 0))),
        ],
        out_specs=[
            pl.BlockSpec((1, CH, DK), rev(lambda d, b, h, i: (d, rowblk(d, b, i), h))),
            pl.BlockSpec((1, CH, DK), rev(lambda d, b, h, i: (d, rowblk(d, b, i), h))),
            pl.BlockSpec((1, CH, DV), rev(lambda d, b, h, i: (d, rowblk(d, b, i), h))),
            pl.BlockSpec((CH, DK), rev(lambda d, b, h, i: (rowblk(d, b, i), d * HEADS + h))),
        ],
        out_shape=[jax.ShapeDtypeStruct((2, ntot, KEY), F32), jax.ShapeDtypeStruct((2, ntot, KEY), F32),
                   jax.ShapeDtypeStruct((2, ntot, D), F32), jax.ShapeDtypeStruct((ntot, 2 * KEY), F32)],
        scratch_shapes=[pltpu.VMEM((DV, DK), F32)],
        compiler_params=_params(("parallel", "parallel", "parallel", "arbitrary")),
    )(p_all, p_all, p_all, la_all, do, ss)


def _gla_post_fwd(o2, p_all, head_gain, n):
    tt = _tile(n, 256)

    def body(o_ref, g_ref, hg_ref, y_ref):
        o = o_ref[0] + o_ref[1]
        gv = g_ref[...]
        hg = hg_ref[...]
        for h in range(HEADS):
            oh = o[:, h * DV:(h + 1) * DV]
            r = lax.rsqrt(jnp.mean(oh * oh, axis=-1, keepdims=True) + EPS)
            y_ref[:, h * DV:(h + 1) * DV] = ((oh * r) * hg * _silu(gv[:, h * DV:(h + 1) * DV])).astype(BF16)

    return pl.pallas_call(
        body, name="gla_post_fwd", grid=(n // tt,),
        in_specs=[pl.BlockSpec((2, tt, D), lambda i: (0, i, 0)), pl.BlockSpec((tt, D), lambda i: (i, 2)),
                  pl.BlockSpec((1, DV), lambda i: (0, 0))],
        out_specs=pl.BlockSpec((tt, D), lambda i: (i, 0)),
        out_shape=jax.ShapeDtypeStruct((n, D), BF16),
        compiler_params=_params(("parallel",)),
    )(o2, p_all, head_gain)


def _gla_post_bwd(o2, p_all, head_gain, dyb, n):
    tt = _tile(n, 256)

    def body(o_ref, g_ref, hg_ref, dy_ref, do_ref, dg_ref, dhg_ref):
        i = pl.program_id(0)

        @pl.when(i == 0)
        def _():
            dhg_ref[...] = jnp.zeros_like(dhg_ref)

        o = o_ref[0] + o_ref[1]
        gv = g_ref[...]
        hg = hg_ref[...]
        dy = dy_ref[...]
        acc = jnp.zeros((1, DV), F32)
        for h in range(HEADS):
            sl = slice(h * DV, (h + 1) * DV)
            oh = o[:, sl]
            r = lax.rsqrt(jnp.mean(oh * oh, axis=-1, keepdims=True) + EPS)
            on = oh * r
            gh = gv[:, sl]
            dyh = dy[:, sl]
            dg_ref[:, sl] = dyh * (on * hg) * _dsilu(gh)
            dog = dyh * _silu(gh)
            acc = acc + _rowsum(dog * on)
            don = dog * hg
            do_ref[:, sl] = r * (don - on * jnp.mean(don * on, axis=-1, keepdims=True))
        dhg_ref[...] += acc

    return pl.pallas_call(
        body, name="gla_post_bwd", grid=(n // tt,),
        in_specs=[pl.BlockSpec((2, tt, D), lambda i: (0, i, 0)), pl.BlockSpec((tt, D), lambda i: (i, 2)),
                  pl.BlockSpec((1, DV), lambda i: (0, 0)), pl.BlockSpec((tt, D), lambda i: (i, 0))],
        out_specs=[pl.BlockSpec((tt, D), lambda i: (i, 0)), pl.BlockSpec((tt, D), lambda i: (i, 0)),
                   pl.BlockSpec((1, DV), lambda i: (0, 0))],
        out_shape=[jax.ShapeDtypeStruct((n, D), F32), jax.ShapeDtypeStruct((n, D), F32),
                   jax.ShapeDtypeStruct((1, DV), F32)],
        compiler_params=_params(("arbitrary",)),
    )(o2, p_all, head_gain, dyb)


def _gla_assemble(p_all, w2, b2, dq, dk, dv, dla, dgate, n):
    ntot = p_all.shape[0]
    tt = _tile(n, 128)
    nmain = n // tt
    assert ntot % tt == 0

    def body(a_ref, w_ref, b_ref, dq_ref, dk_ref, dv_ref, dla_ref, dg_ref, dp_ref, dw_ref, db_ref):
        i = pl.program_id(0)

        @pl.when(i == 0)
        def _():
            dw_ref[...] = jnp.zeros_like(dw_ref)
            db_ref[...] = jnp.zeros_like(db_ref)

        a = a_ref[...]
        w = w_ref[...]
        z = _dot(a, w) + b_ref[...]
        dz = dla_ref[...] * (1.0 / (1.0 + jnp.exp(z))) * (1.0 / TAU)
        dw_ref[...] += _dot(a, dz, _TN)
        db_ref[...] += _rowsum(dz)
        dp_ref[:, 0:KEY] = ((dq_ref[0] + dq_ref[1]) * 1.0).astype(BF16)
        dp_ref[:, KEY:2 * KEY] = (dk_ref[0] + dk_ref[1]).astype(BF16)
        dp_ref[:, 2 * KEY:2 * KEY + D] = (dv_ref[0] + dv_ref[1]).astype(BF16)
        dp_ref[:, 2 * KEY + D:2 * KEY + 2 * D] = (dg_ref[...] * jnp.where(i < nmain, 1.0, 0.0)).astype(BF16)
        dp_ref[:, 2 * KEY + 2 * D:GLA_IN_PAD] = _dot(dz, w, _NT).astype(BF16)

    return pl.pallas_call(
        body, name="gla_assemble", grid=(ntot // tt,),
        in_specs=[pl.BlockSpec((tt, 128), lambda i: (i, (2 * KEY + 2 * D) // 128)),
                  pl.BlockSpec((128, 2 * KEY), lambda i: (0, 0)), pl.BlockSpec((1, 2 * KEY), lambda i: (0, 0)),
                  pl.BlockSpec((2, tt, KEY), lambda i: (0, i, 0)), pl.BlockSpec((2, tt, KEY), lambda i: (0, i, 0)),
                  pl.BlockSpec((2, tt, D), lambda i: (0, i, 0)), pl.BlockSpec((tt, 2 * KEY), lambda i: (i, 0)),
                  pl.BlockSpec((tt, D), lambda i: (jnp.minimum(i, nmain - 1), 0))],
        out_specs=[pl.BlockSpec((tt, GLA_IN_PAD), lambda i: (i, 0)), pl.BlockSpec((128, 2 * KEY), lambda i: (0, 0)),
                   pl.BlockSpec((1, 2 * KEY), lambda i: (0, 0))],
        out_shape=[jax.ShapeDtypeStruct((ntot, GLA_IN_PAD), BF16), jax.ShapeDtypeStruct((128, 2 * KEY), F32),
                   jax.ShapeDtypeStruct((1, 2 * KEY), F32)],
        compiler_params=_params(("arbitrary",)),
    )(p_all, w2, b2, dq, dk, dv, dla, dgate)


ADA_ROWS = 24
ADA_SH = N_MOD * D // N_CHIPS


def _ada_fwd(cvec, ada_w, ada_b_sh):
    def body(c_ref, w_ref, b_ref, o_ref):
        o_ref[0] = _dot(_silu(c_ref[...]), w_ref[0]) + b_ref[0]

    return pl.pallas_call(
        body, name="ada_fwd", grid=(2,),
        in_specs=[pl.BlockSpec((ADA_ROWS, D), lambda l: (0, 0)), pl.BlockSpec((1, D, ADA_SH), lambda l: (l, 0, 0)),
                  pl.BlockSpec((1, 1, ADA_SH), lambda l: (l, 0, 0))],
        out_specs=pl.BlockSpec((1, ADA_ROWS, ADA_SH), lambda l: (l, 0, 0)),
        out_shape=jax.ShapeDtypeStruct((2, ADA_ROWS, ADA_SH), F32),
        compiler_params=_params(("parallel",)),
    )(cvec, ada_w, ada_b_sh)


def _ada_bwd(cvec, ada_w, dmod_sh):
    def body(c_ref, w_ref, dm_ref, gw_ref, dc_ref):
        dm = dm_ref[0]
        gw_ref[0] = _dot(_silu(c_ref[...]), dm, _TN)
        dc_ref[0] = _dot(dm, w_ref[0], _NT)

    return pl.pallas_call(
        body, name="ada_bwd", grid=(2,),
        in_specs=[pl.BlockSpec((ADA_ROWS, D), lambda l: (0, 0)), pl.BlockSpec((1, D, ADA_SH), lambda l: (l, 0, 0)),
                  pl.BlockSpec((1, ADA_ROWS, ADA_SH), lambda l: (l, 0, 0))],
        out_specs=[pl.BlockSpec((1, D, ADA_SH), lambda l: (l, 0, 0)), pl.BlockSpec((1, ADA_ROWS, D), lambda l: (l, 0, 0))],
        out_shape=[jax.ShapeDtypeStruct((2, D, ADA_SH), F32), jax.ShapeDtypeStruct((2, ADA_ROWS, D), F32)],
        compiler_params=_params(("parallel",)),
    )(cvec, ada_w, dmod_sh)


def _sum_slots(x, name):
    s, r, _ = x.shape

    def body(x_ref, o_ref):
        acc = x_ref[0]
        for k in range(1, s):
            acc = acc + x_ref[k]
        o_ref[...] = acc

    return pl.pallas_call(
        body, name=name, out_shape=jax.ShapeDtypeStruct((r, 128), F32),
        in_specs=[pl.BlockSpec(memory_space=pltpu.VMEM)], out_specs=pl.BlockSpec(memory_space=pltpu.VMEM),
    )(x)


def _cctx_grad(dscc_parts, c_ctx):
    def body(p_ref, c_ref, o_ref):
        acc = p_ref[0]
        for k in range(1, N_CHIPS):
            acc = acc + p_ref[k]
        o_ref[...] = acc * _dsilu(c_ref[...])

    return pl.pallas_call(
        body, name="cctx_grad", out_shape=jax.ShapeDtypeStruct((8, 128), F32),
        in_specs=[pl.BlockSpec(memory_space=pltpu.VMEM)] * 2, out_specs=pl.BlockSpec(memory_space=pltpu.VMEM),
    )(dscc_parts, c_ctx)


def _adamw(w, g, m, v, name):
    r, cdim = w.shape
    tr = _tile(r, 256)
    c1 = 1.0 - ADAM_B1 ** ADAM_STEP
    c2 = 1.0 - ADAM_B2 ** ADAM_STEP

    def body(w_ref, g_ref, m_ref, v_ref, d_ref, mo_ref, vo_ref):
        gv = g_ref[...]
        mn = ADAM_B1 * m_ref[...] + (1.0 - ADAM_B1) * gv
        vn = ADAM_B2 * v_ref[...] + (1.0 - ADAM_B2) * (gv * gv)
        mo_ref[...] = mn
        vo_ref[...] = vn
        d_ref[...] = -ADAM_LR * ((mn / c1) / (jnp.sqrt(vn / c2) + ADAM_EPS) + ADAM_WD * w_ref[...])

    spec = pl.BlockSpec((tr, cdim), lambda i: (i, 0))
    sds = jax.ShapeDtypeStruct((r, cdim), F32)
    return pl.pallas_call(
        body, name=name, grid=(r // tr,), in_specs=[spec] * 4, out_specs=[spec] * 3, out_shape=[sds] * 3,
        compiler_params=_params(("parallel",)),
    )(w, g, m, v)


def _place():
    x, y, c = lax.axis_index("x"), lax.axis_index("y"), lax.axis_index("c")
    return x, y, c


def _allgather_small(blk, name):
    m_per, n = blk.shape

    def body(x_ref, out_ref, send_sems, recv_sems, local_sem):
        x, y, c = _place()
        me, sibling = (x, y, c), (x, y, 1 - c)
        chips = [(1 - x, y), (x, 1 - y), (1 - x, 1 - y)]

        def rows(px, py, pc):
            return out_ref.at[pl.ds((4 * px + 2 * py + pc) * m_per, m_per), :]

        def copy(k, block, to, src=None):
            return pltpu.make_async_remote_copy(
                src_ref=rows(*block) if src is None else src, dst_ref=rows(*block),
                send_sem=send_sems.at[k], recv_sem=recv_sems.at[k], device_id=to, device_id_type=MESH)

        mine = pltpu.make_async_copy(x_ref, rows(*me), local_sem)
        mine.start()
        first = [copy(0, me, sibling, src=x_ref)]
        first += [copy(1 + j, me, (*chip, c), src=x_ref) for j, chip in enumerate(chips)]
        for cp in first:
            cp.start()
        passed = [copy(4 + j, (*chip, c), sibling) for j, chip in enumerate(chips)]
        for j, chip in enumerate(chips):
            copy(1 + j, (*chip, c), me).wait_recv()
            passed[j].start()
        copy(0, sibling, me).wait_recv()
        for j, chip in enumerate(chips):
            copy(4 + j, (*chip, 1 - c), me).wait_recv()
        for cp in first + passed:
            cp.wait_send()
        mine.wait()

    return pl.pallas_call(
        body, name=name,
        out_shape=jax.ShapeDtypeStruct((N_DEV * m_per, n), blk.dtype),
        in_specs=[pl.BlockSpec(memory_space=pltpu.VMEM)],
        out_specs=pl.BlockSpec(memory_space=pltpu.VMEM),
        scratch_shapes=[pltpu.SemaphoreType.DMA((7,)), pltpu.SemaphoreType.DMA((7,)), pltpu.SemaphoreType.DMA],
    )(blk)


def _other_chips(x, y):
    return [(1 - x, y), (x, 1 - y), (1 - x, 1 - y)]


def _weights_allgather(own):
    r = own.shape[0]
    hr = r // 2

    def body(own_ref, out_ref, send_sems, recv_sems, fsend_sems, frecv_sems, local_sem):
        x, y, c = _place()
        chip = 2 * x + y
        others = _other_chips(x, y)

        def half(ch, cc):
            return out_ref.at[ch, pl.ds(cc * hr, hr), :]

        mine = pltpu.make_async_copy(own_ref, out_ref.at[chip], local_sem)
        mine.start()
        sends = []
        for j, (ox, oy) in enumerate(others):
            cp = pltpu.make_async_remote_copy(
                src_ref=own_ref.at[pl.ds(c * hr, hr), :], dst_ref=half(chip, c),
                send_sem=send_sems.at[j], recv_sem=recv_sems.at[j], device_id=(ox, oy, c), device_id_type=MESH)
            cp.start()
            sends.append(cp)
        fwd = []
        for j, (ox, oy) in enumerate(others):
            och = 2 * ox + oy
            pltpu.make_async_remote_copy(
                src_ref=half(och, c), dst_ref=half(och, c), send_sem=send_sems.at[j], recv_sem=recv_sems.at[j],
                device_id=(ox, oy, c), device_id_type=MESH).wait_recv()
            cp = pltpu.make_async_remote_copy(
                src_ref=half(och, c), dst_ref=half(och, c), send_sem=fsend_sems.at[j], recv_sem=frecv_sems.at[j],
                device_id=(x, y, 1 - c), device_id_type=MESH)
            cp.start()
            fwd.append(cp)
        for j, (ox, oy) in enumerate(others):
            och = 2 * ox + oy
            pltpu.make_async_remote_copy(
                src_ref=half(och, 1 - c), dst_ref=half(och, 1 - c), send_sem=fsend_sems.at[j],
                recv_sem=frecv_sems.at[j], device_id=(x, y, 1 - c), device_id_type=MESH).wait_recv()
        for cp in sends + fwd:
            cp.wait_send()
        mine.wait()

    any_spec = pl.BlockSpec(memory_space=pl.ANY)
    return pl.pallas_call(
        body, name="weights_allgather",
        out_shape=jax.ShapeDtypeStruct((N_CHIPS, r, D), own.dtype),
        in_specs=[any_spec], out_specs=any_spec,
        scratch_shapes=[pltpu.SemaphoreType.DMA((3,)), pltpu.SemaphoreType.DMA((3,)),
                        pltpu.SemaphoreType.DMA((3,)), pltpu.SemaphoreType.DMA((3,)), pltpu.SemaphoreType.DMA],
    )(own)


def _rs_pair_exchange(g):
    r = g.shape[1]
    hr = r // 2

    def body(g_ref, got_ref, send_sem, recv_sem):
        x, y, c = _place()
        cp = pltpu.make_async_remote_copy(
            src_ref=g_ref.at[:, pl.ds((1 - c) * hr, hr), :], dst_ref=got_ref, send_sem=send_sem, recv_sem=recv_sem,
            device_id=(x, y, 1 - c), device_id_type=MESH)
        cp.start()
        cp.wait()

    any_spec = pl.BlockSpec(memory_space=pl.ANY)
    return pl.pallas_call(
        body, name="rs_pair_exchange",
        out_shape=jax.ShapeDtypeStruct((N_CHIPS, hr, D), F32),
        in_specs=[any_spec], out_specs=any_spec,
        scratch_shapes=[pltpu.SemaphoreType.DMA, pltpu.SemaphoreType.DMA],
    )(g)


def _rs_chip_sum(place, g, got):
    r = g.shape[1]
    hr = r // 2
    tr = _tile(hr, 640, 16)
    nt = hr // tr

    def body(pl_ref, g_ref, got_ref, p16_ref, p32_ref):
        s = pl.program_id(1)
        p = g_ref[0] + got_ref[0]
        p16_ref[0] = p.astype(BF16)

        @pl.when(s == pl_ref[1])
        def _():
            p32_ref[...] = p

    return pl.pallas_call(
        body, name="rs_chip_sum",
        grid_spec=pltpu.PrefetchScalarGridSpec(
            num_scalar_prefetch=1, grid=(nt, N_CHIPS),
            in_specs=[pl.BlockSpec((1, tr, D), lambda i, s, pr: (s, pr[0] * nt + i, 0)),
                      pl.BlockSpec((1, tr, D), lambda i, s, pr: (s, i, 0))],
            out_specs=[pl.BlockSpec((1, tr, D), lambda i, s, pr: (s, i, 0)),
                       pl.BlockSpec((tr, D), lambda i, s, pr: (i, 0))]),
        out_shape=[jax.ShapeDtypeStruct((N_CHIPS, hr, D), BF16), jax.ShapeDtypeStruct((hr, D), F32)],
        compiler_params=_params(("parallel", "arbitrary")),
    )(place, g, got)


def _rs_scatter(p16):
    def body(p_ref, out_ref, send_sems, recv_sems, local_sem):
        x, y, c = _place()
        chip = 2 * x + y
        others = _other_chips(x, y)
        mine = pltpu.make_async_copy(p_ref.at[chip], out_ref.at[chip], local_sem)
        mine.start()
        sends = []
        for j, (ox, oy) in enumerate(others):
            cp = pltpu.make_async_remote_copy(
                src_ref=p_ref.at[2 * ox + oy], dst_ref=out_ref.at[chip], send_sem=send_sems.at[j],
                recv_sem=recv_sems.at[j], device_id=(ox, oy, c), device_id_type=MESH)
            cp.start()
            sends.append(cp)
        for j, (ox, oy) in enumerate(others):
            och = 2 * ox + oy
            pltpu.make_async_remote_copy(
                src_ref=p_ref.at[och], dst_ref=out_ref.at[och], send_sem=send_sems.at[j], recv_sem=recv_sems.at[j],
                device_id=(ox, oy, c), device_id_type=MESH).wait_recv()
        for cp in sends:
            cp.wait_send()
        mine.wait()

    any_spec = pl.BlockSpec(memory_space=pl.ANY)
    return pl.pallas_call(
        body, name="rs_scatter",
        out_shape=jax.ShapeDtypeStruct(p16.shape, p16.dtype),
        in_specs=[any_spec], out_specs=any_spec,
        scratch_shapes=[pltpu.SemaphoreType.DMA((3,)), pltpu.SemaphoreType.DMA((3,)), pltpu.SemaphoreType.DMA],
    )(p16)


def _rs_final_sum(place, parts, p32):
    hr = parts.shape[1]
    tr = _tile(hr, 640, 16)

    def body(pl_ref, parts_ref, p32_ref, o_ref):
        chip = pl_ref[1]
        own = p32_ref[...]
        acc = jnp.where(chip == 0, own, parts_ref[0].astype(F32))
        for s in range(1, N_CHIPS):
            acc = acc + jnp.where(chip == s, own, parts_ref[s].astype(F32))
        o_ref[...] = acc

    return pl.pallas_call(
        body, name="rs_final_sum",
        grid_spec=pltpu.PrefetchScalarGridSpec(
            num_scalar_prefetch=1, grid=(hr // tr,),
            in_specs=[pl.BlockSpec((N_CHIPS, tr, D), lambda i, pr: (0, i, 0)),
                      pl.BlockSpec((tr, D), lambda i, pr: (i, 0))],
            out_specs=pl.BlockSpec((tr, D), lambda i, pr: (i, 0))),
        out_shape=jax.ShapeDtypeStruct((hr, D), F32),
        compiler_params=_params(("parallel",)),
    )(place, parts, p32)


def _rs_pair_gather(halfsum):
    hr = halfsum.shape[0]

    def body(h_ref, out_ref, send_sem, recv_sem, local_sem):
        x, y, c = _place()
        mine = pltpu.make_async_copy(h_ref, out_ref.at[pl.ds(c * hr, hr), :], local_sem)
        mine.start()
        cp = pltpu.make_async_remote_copy(
            src_ref=h_ref, dst_ref=out_ref.at[pl.ds(c * hr, hr), :], send_sem=send_sem, recv_sem=recv_sem,
            device_id=(x, y, 1 - c), device_id_type=MESH)
        cp.start()
        pltpu.make_async_remote_copy(
            src_ref=h_ref, dst_ref=out_ref.at[pl.ds((1 - c) * hr, hr), :], send_sem=send_sem, recv_sem=recv_sem,
            device_id=(x, y, 1 - c), device_id_type=MESH).wait_recv()
        cp.wait_send()
        mine.wait()

    any_spec = pl.BlockSpec(memory_space=pl.ANY)
    return pl.pallas_call(
        body, name="rs_pair_gather",
        out_shape=jax.ShapeDtypeStruct((2 * hr, D), F32),
        in_specs=[any_spec], out_specs=any_spec,
        scratch_shapes=[pltpu.SemaphoreType.DMA, pltpu.SemaphoreType.DMA, pltpu.SemaphoreType.DMA],
    )(halfsum)


def _interleave(w, parts):
    n = w.shape[0] // parts
    return w.reshape(parts, n // 128, 128, *w.shape[1:]).swapaxes(0, 1).reshape(w.shape)


def _deinterleave(w, parts):
    n = w.shape[0] // parts
    return w.reshape(n // 128, parts, 128, *w.shape[1:]).swapaxes(0, 1).reshape(w.shape)


def _local_step(x, ctx, tgt, mods, mc, wts, small):
    nb, t, _ = x.shape
    tc = ctx.shape[1]
    n = nb * t
    nc = nb * tc
    xf = x.reshape(n, D)
    cf = ctx.reshape(nc, D)
    tf = tgt.reshape(n, D)
    vec = lambda a: a.reshape(1, -1)
    m = [[mods[l, :, k, :].reshape(nb, 1, D) for k in range(N_MOD)] for l in range(2)]
    mc_b = [jnp.broadcast_to(mc[k].reshape(1, 1, D), (nb, 1, D)) for k in range(2)]

    w_gin = jnp.pad(wts["gla_in_t"], ((0, GLA_IN_PAD - GLA_IN), (0, 0)))
    w_sin = _interleave(wts["sc_in_t"], 3)
    w_up = [_interleave(wts["up_t"][l], 2) for l in range(2)]
    cw = [_interleave(small["ffn_conv_w"][l].T, 2).T for l in range(2)]
    cb = [_interleave(small["ffn_conv_b"][l], 2).reshape(1, -1) for l in range(2)]
    w2 = jnp.zeros((128, 2 * KEY), F32)
    w2 = w2.at[0:RANK, 0:KEY].set(small["gla_w_a2"][0]).at[RANK:2 * RANK, KEY:].set(small["gla_w_a2"][1])
    b2 = small["gla_b_a"].reshape(1, 2 * KEY)
    hg = small["gla_head_norm"].reshape(1, DV)

    hn0 = _mod_fwd(xf, vec(small["norm_mix"][0]), m[0][0], m[0][1], t, "mod0_main")
    hnc = _mod_fwd(cf, vec(small["norm_mix"][0]), mc_b[0], mc_b[1], tc, "mod0_ctx")
    hn_all = jnp.concatenate([hn0, hnc], axis=0)
    p_all = _mm(hn_all, w_gin, "nt", F32, "gla_in_proj", tn=640)
    la_all = _gla_decay_fwd(p_all, w2, b2)
    o2, ss = _gla_scan_fwd(p_all, la_all, nb, t, tc)
    yb0 = _gla_post_fwd(o2, p_all, hg, n)
    y0 = _mm(yb0, wts["gla_out"], "nn", F32, "gla_out_proj")
    h1, hn1 = _mod_fwd(xf, vec(small["norm_ffn"][0]), m[0][3], m[0][4], t, "mod0_ffn", y=y0, gate=m[0][2])
    u0 = _mm(hn1, w_up[0], "nt", F32, "ffn0_up")
    z0 = _ffn_mid_fwd(u0, cw[0], cb[0], nb, t, "ffn0_mid_fwd")
    f0 = _mm(z0, wts["down"][0], "nn", F32, "ffn0_down", tk=640)
    h2, hn2 = _mod_fwd(h1, vec(small["norm_mix"][1]), m[1][0], m[1][1], t, "mod1_mix", y=f0, gate=m[0][5])
    p1 = _mm(hn2, w_sin, "nt", F32, "sc_in_proj", tn=384)
    yb1 = _sc_mid_fwd(p1, small["sc_conv_w"], nb, t)
    y1 = _mm(yb1, wts["sc_out"], "nn", F32, "sc_out_proj")
    h3, hn3 = _mod_fwd(h2, vec(small["norm_ffn"][1]), m[1][3], m[1][4], t, "mod1_ffn", y=y1, gate=m[1][2])
    u1 = _mm(hn3, w_up[1], "nt", F32, "ffn1_up")
    z1 = _ffn_mid_fwd(u1, cw[1], cb[1], nb, t, "ffn1_mid_fwd")
    f1 = _mm(z1, wts["down"][1], "nn", F32, "ffn1_down", tk=640)
    loss, dh4, df1, dm15, dfinal = _final(h3, f1, m[1][5], vec(small["final_norm"]), tf, t)

    gb, gs = {}, {}
    dmods = [[None] * N_MOD for _ in range(2)]
    dmods[1][5] = dm15

    def ffn_bwd(l, df, u, z, hn, tag):
        dz = _mm(df, wts["down"][l], "nt", F32, f"ffn{l}_down_dx", tn=640)
        gdown = _mm(z, df, "tn", F32, f"ffn{l}_down_dw", tm=640)
        du, dcw, dcb = _ffn_mid_bwd(u, cw[l], cb[l], dz, nb, t, f"ffn{l}_mid_bwd")
        dhn = _mm(du, w_up[l], "nn", F32, f"ffn{l}_up_dx")
        gup = _deinterleave(_mm(du, hn, "tn", F32, f"ffn{l}_up_dw"), 2)
        return dhn, gdown, gup, _deinterleave(dcw.T, 2).T, _deinterleave(dcb.reshape(-1), 2)

    dhn3, gdown1, gup1, dcw1, dcb1 = ffn_bwd(1, df1, u1, z1, hn3, "ffn1")
    r = _mod_bwd(h3, dhn3, vec(small["norm_ffn"][1]), m[1][4], t, "mod1_ffn_bwd", dh_out=dh4, y_prev=y1,
                 gate_prev=m[1][2])
    dh3, dmods[1][4], dmods[1][3], dnf1, dy1, dmods[1][2] = (r["dh"], r["dscale"], r["dshift"], r["dgain"],
                                                             r["dy_prev"], r["dgate_prev"])
    dyb1 = _mm(dy1, wts["sc_out"], "nt", F32, "sc_out_dx")
    gb["sc_out"] = _mm(yb1, dy1, "tn", F32, "sc_out_dw")
    dp1, dscw = _sc_mid_bwd(p1, small["sc_conv_w"], dyb1, nb, t)
    dhn2 = _mm(dp1, w_sin, "nn", F32, "sc_in_dx")
    gb["sc_in_t"] = _deinterleave(_mm(dp1, hn2, "tn", F32, "sc_in_dw", tm=384), 3)
    r = _mod_bwd(h2, dhn2, vec(small["norm_mix"][1]), m[1][1], t, "mod1_mix_bwd", dh_out=dh3, y_prev=f0,
                 gate_prev=m[0][5])
    dh2, dmods[1][1], dmods[1][0], dnm1, df0, dmods[0][5] = (r["dh"], r["dscale"], r["dshift"], r["dgain"],
                                                             r["dy_prev"], r["dgate_prev"])
    dhn1, gdown0, gup0, dcw0, dcb0 = ffn_bwd(0, df0, u0, z0, hn1, "ffn0")
    r = _mod_bwd(h1, dhn1, vec(small["norm_ffn"][0]), m[0][4], t, "mod0_ffn_bwd", dh_out=dh2, y_prev=y0,
                 gate_prev=m[0][2])
    dh1, dmods[0][4], dmods[0][3], dnf0, dy0, dmods[0][2] = (r["dh"], r["dscale"], r["dshift"], r["dgain"],
                                                             r["dy_prev"], r["dgate_prev"])
    dyb0 = _mm(dy0, wts["gla_out"], "nt", F32, "gla_out_dx")
    gb["gla_out"] = _mm(yb0, dy0, "tn", F32, "gla_out_dw")
    do, dgate, dhg = _gla_post_bwd(o2, p_all, hg, dyb0, n)
    dq, dk, dv, dla = _gla_scan_bwd(p_all, la_all, do, ss, nb, t, tc)
    dp, dw2, db2 = _gla_assemble(p_all, w2, b2, dq, dk, dv, dla, dgate, n)
    dhn_all = _mm(dp, w_gin, "nn", F32, "gla_in_dx", tk=640)
    gb["gla_in_t"] = _mm(dp, hn_all, "tn", F32, "gla_in_dw", tm=640)[:GLA_IN]
    r = _mod_bwd(xf, dhn_all, vec(small["norm_mix"][0]), m[0][1], t, "mod0_main_bwd", dh_out=dh1)
    grad_x, dmods[0][1], dmods[0][0], dnm0 = r["dh"], r["dscale"], r["dshift"], r["dgain"]
    rc = _mod_bwd(cf, dhn_all, vec(small["norm_mix"][0]), mc_b[1], tc, "mod0_ctx_bwd", dhn_row0=n, need_dh=False)
    dmc = jnp.stack([jnp.sum(rc["dshift"], axis=0).reshape(D), jnp.sum(rc["dscale"], axis=0).reshape(D)])
    dnm0 = dnm0 + rc["dgain"]

    gb["up_t"] = [gup0, gup1]
    gb["down"] = [gdown0, gdown1]
    gs["norm_mix"] = jnp.concatenate([dnm0, dnm1], axis=0)
    gs["norm_ffn"] = jnp.concatenate([dnf0, dnf1], axis=0)
    gs["final_norm"] = dfinal.reshape(D)
    gs["gla_w_a2"] = jnp.stack([dw2[0:RANK, 0:KEY], dw2[RANK:2 * RANK, KEY:]])
    gs["gla_b_a"] = db2.reshape(2, KEY)
    gs["gla_head_norm"] = dhg.reshape(DV)
    gs["sc_conv_w"] = dscw
    gs["ffn_conv_w"] = jnp.stack([dcw0, dcw1])
    gs["ffn_conv_b"] = jnp.stack([dcb0, dcb1])
    dmods_arr = jnp.stack([jnp.stack([dmods[l][k].reshape(nb, D) for k in range(N_MOD)], axis=1) for l in range(2)])
    return loss, grad_x.reshape(nb, t, D), gb, gs, dmods_arr, dmc


def _pack(arrs):
    parts, meta, off = [], [], 0
    for a in arrs:
        r = a.size // 128
        rp = -(-r // 8) * 8
        a2 = a.reshape(r, 128).astype(F32)
        if rp != r:
            a2 = jnp.pad(a2, ((0, rp - r), (0, 0)))
        parts.append(a2)
        meta.append((off, r, a.shape))
        off += rp
    return jnp.concatenate(parts, axis=0), meta


def _unpack(buf, meta, lead=()):
    return [buf[..., off:off + r, :].reshape(*lead, *shape) for off, r, shape in meta]


_BIG = ("gla_in_t", "sc_in_t", "up_t0", "up_t1", "gla_out", "sc_out", "down0", "down1")
_BIG_ROWS = {"gla_in_t": GLA_IN // N_CHIPS, "sc_in_t": 3 * D // N_CHIPS, "up_t0": 2 * HID // N_CHIPS,
             "up_t1": 2 * HID // N_CHIPS, "gla_out": D // N_CHIPS, "sc_out": D // N_CHIPS,
             "down0": HID // N_CHIPS, "down1": HID // N_CHIPS}
_BIG_TOTAL = sum(_BIG_ROWS.values())
_BIG_PAD = -(-_BIG_TOTAL // 32) * 32


def _big_offsets():
    off, out = 0, {}
    for k in _BIG:
        out[k] = off
        off += _BIG_ROWS[k]
    return out


def kernel(x, c, ctx, c_ctx, ada_w, ada_b, norm_mix, norm_ffn, gla_w_in, gla_w_a2, gla_b_a, gla_head_norm, gla_w_out, sc_w_in, sc_conv_w, sc_w_out, ffn_w_up, ffn_conv_w, ffn_conv_b, ffn_w_down, final_norm, loss_target, m_c_ctx, m_ada_w, m_ada_b, m_norm_mix, m_norm_ffn, m_gla_w_in, m_gla_w_a2, m_gla_b_a, m_gla_head_norm, m_gla_w_out, m_sc_w_in, m_sc_conv_w, m_sc_w_out, m_ffn_w_up, m_ffn_conv_w, m_ffn_conv_b, m_ffn_w_down, m_final_norm, v_c_ctx, v_ada_w, v_ada_b, v_norm_mix, v_norm_ffn, v_gla_w_in, v_gla_w_a2, v_gla_b_a, v_gla_head_norm, v_gla_w_out, v_sc_w_in, v_sc_conv_w, v_sc_w_out, v_ffn_w_up, v_ffn_conv_w, v_ffn_conv_b, v_ffn_w_down, v_final_norm):
    ix, iy, ic = _place()
    chip = 2 * ix + iy
    dev = 2 * chip + ic
    place = jnp.stack([ic, chip]).astype(jnp.int32)
    nb = x.shape[0]
    offs = _big_offsets()

    buf, meta = _pack([c, ffn_conv_w, sc_conv_w, gla_w_a2, gla_b_a])
    got = _allgather_small(buf, "gather_small_in").reshape(N_DEV, buf.shape[0], 128)
    c_all, fcw, scw, wa2, ba = _unpack(got, meta, (N_DEV,))
    c_all = c_all.reshape(N_DEV * nb, D)
    per_chip = lambda a: a[0::2]
    ffn_conv_w_full = jnp.moveaxis(per_chip(fcw), 0, 2).reshape(2, 3, 2 * HID)
    sc_conv_w_full = jnp.moveaxis(per_chip(scw)[:, 0], 0, 1).reshape(3, D)
    gla_w_a2_full = jnp.moveaxis(per_chip(wa2)[:, 0], 0, 2).reshape(2, RANK, KEY)
    gla_b_a_full = jnp.moveaxis(per_chip(ba)[:, 0], 0, 1).reshape(2, KEY)

    own = {"gla_in_t": gla_w_in[0].T, "sc_in_t": sc_w_in[0].T, "up_t0": ffn_w_up[0].T, "up_t1": ffn_w_up[1].T,
           "gla_out": gla_w_out[0], "sc_out": sc_w_out[0], "down0": ffn_w_down[0], "down1": ffn_w_down[1]}
    own_packed = jnp.concatenate([own[k].astype(BF16) for k in _BIG]
                                 + [jnp.zeros((_BIG_PAD - _BIG_TOTAL, D), BF16)], axis=0)
    wg = _weights_allgather(own_packed)
    full = {k: wg[:, offs[k]:offs[k] + _BIG_ROWS[k], :].reshape(N_CHIPS * _BIG_ROWS[k], D) for k in _BIG}
    wts = {"gla_in_t": full["gla_in_t"], "sc_in_t": full["sc_in_t"], "up_t": [full["up_t0"], full["up_t1"]],
           "gla_out": full["gla_out"], "sc_out": full["sc_out"], "down": [full["down0"], full["down1"]]}

    cvec = jnp.concatenate([c_all, c_ctx.reshape(1, D), jnp.zeros((ADA_ROWS - N_DEV * nb - 1, D), F32)], axis=0)
    ada_b_sh = lax.dynamic_slice_in_dim(ada_b, chip * ADA_SH, ADA_SH, axis=1).reshape(2, 1, ADA_SH)
    mod_sh = _ada_fwd(cvec, ada_w, ada_b_sh)
    got = _allgather_small(mod_sh.reshape(2 * ADA_ROWS, ADA_SH), "gather_mod")
    mod_full = jnp.moveaxis(per_chip(got.reshape(N_DEV, 2, ADA_ROWS, ADA_SH)), 0, 2).reshape(2, ADA_ROWS, N_MOD * D)
    mods = lax.dynamic_slice_in_dim(mod_full, dev * nb, nb, axis=1).reshape(2, nb, N_MOD, D)
    mc = mod_full[0, N_DEV * nb, :2 * D].reshape(2, D)

    small = {"norm_mix": norm_mix, "norm_ffn": norm_ffn, "final_norm": final_norm, "gla_w_a2": gla_w_a2_full,
             "gla_b_a": gla_b_a_full, "gla_head_norm": gla_head_norm[0], "sc_conv_w": sc_conv_w_full,
             "ffn_conv_w": ffn_conv_w_full, "ffn_conv_b": ffn_conv_b}
    loss_p, grad_x, gb, gs, dmods, dmc = _local_step(x, ctx, loss_target, mods, mc, wts, small)

    sum_names = ["norm_mix", "norm_ffn", "final_norm", "gla_w_a2", "gla_b_a", "gla_head_norm", "sc_conv_w",
                 "ffn_conv_w", "ffn_conv_b"]
    buf, meta = _pack([jnp.broadcast_to(loss_p, (8, 128))] + [gs[k] for k in sum_names] + [dmc, dmods])
    n_sum = meta[-1][0]
    got = _allgather_small(buf, "gather_small_grads").reshape(N_DEV, buf.shape[0], 128)
    summed = _sum_slots(got[:, :n_sum], "sum_small_grads")
    parts = _unpack(summed, meta[:-1])
    loss = parts[0][0, 0]
    g_small = dict(zip(sum_names, parts[1:-1]))
    dmc_tot = parts[-1]
    dmods_all = jnp.moveaxis(_unpack(got, meta[-1:], (N_DEV,))[0], 0, 1).reshape(2, N_DEV * nb, N_MOD * D)

    ctx_row = jnp.stack([jnp.concatenate([dmc_tot.reshape(2 * D), jnp.zeros(((N_MOD - 2) * D,), F32)]),
                         jnp.zeros((N_MOD * D,), F32)]).reshape(2, 1, N_MOD * D)
    dmod_ext = jnp.concatenate([dmods_all, ctx_row, jnp.zeros((2, ADA_ROWS - N_DEV * nb - 1, N_MOD * D), F32)], axis=1)
    g_ada_b = _sum_slots(jnp.moveaxis(dmod_ext, 1, 0).reshape(ADA_ROWS, 2 * N_MOD * D // 128, 128),
                         "sum_ada_b").reshape(2, N_MOD * D)
    dmod_sh = lax.dynamic_slice_in_dim(dmod_ext, chip * ADA_SH, ADA_SH, axis=2)
    g_ada_w, dcv = _ada_bwd(cvec, ada_w, dmod_sh)
    dscc_part = (dcv[0, N_DEV * nb] + dcv[1, N_DEV * nb]).reshape(8, 128)
    got = _allgather_small(dscc_part, "gather_dscc").reshape(N_DEV, 8, 128)
    g_c_ctx = _cctx_grad(per_chip(got), c_ctx.reshape(8, 128)).reshape(D)

    gbig = {"gla_in_t": gb["gla_in_t"], "sc_in_t": gb["sc_in_t"], "up_t0": gb["up_t"][0], "up_t1": gb["up_t"][1],
            "gla_out": gb["gla_out"], "sc_out": gb["sc_out"], "down0": gb["down"][0], "down1": gb["down"][1]}
    g_packed = jnp.concatenate([gbig[k].reshape(N_CHIPS, _BIG_ROWS[k], D) for k in _BIG]
                               + [jnp.zeros((N_CHIPS, _BIG_PAD - _BIG_TOTAL, D), F32)], axis=1)
    from_sibling = _rs_pair_exchange(g_packed)
    p16, p32 = _rs_chip_sum(place, g_packed, from_sibling)
    landed = _rs_scatter(p16)
    half = _rs_final_sum(place, landed, p32)
    g_shard = _rs_pair_gather(half)
    seg = {k: g_shard[offs[k]:offs[k] + _BIG_ROWS[k]] for k in _BIG}

    sl_chip = lambda a, axis, width: lax.dynamic_slice_in_dim(a, chip * width, width, axis=axis)
    grads = {
        "c_ctx": g_c_ctx, "ada_w": g_ada_w, "ada_b": g_ada_b, "norm_mix": g_small["norm_mix"],
        "norm_ffn": g_small["norm_ffn"],
        "gla_w_in": seg["gla_in_t"].T[None], "gla_w_a2": sl_chip(g_small["gla_w_a2"], 2, KEY // N_CHIPS)[None],
        "gla_b_a": sl_chip(g_small["gla_b_a"], 1, KEY // N_CHIPS)[None],
        "gla_head_norm": g_small["gla_head_norm"][None], "gla_w_out": seg["gla_out"][None],
        "sc_w_in": seg["sc_in_t"].T[None], "sc_conv_w": sl_chip(g_small["sc_conv_w"], 1, D // N_CHIPS)[None],
        "sc_w_out": seg["sc_out"][None], "ffn_w_up": jnp.stack([seg["up_t0"].T, seg["up_t1"].T]),
        "ffn_conv_w": sl_chip(g_small["ffn_conv_w"], 2, 2 * HID // N_CHIPS), "ffn_conv_b": g_small["ffn_conv_b"],
        "ffn_w_down": jnp.stack([seg["down0"], seg["down1"]]), "final_norm": g_small["final_norm"],
    }
    weights = {"c_ctx": c_ctx, "ada_w": ada_w, "ada_b": ada_b, "norm_mix": norm_mix, "norm_ffn": norm_ffn,
               "gla_w_in": gla_w_in, "gla_w_a2": gla_w_a2, "gla_b_a": gla_b_a, "gla_head_norm": gla_head_norm,
               "gla_w_out": gla_w_out, "sc_w_in": sc_w_in, "sc_conv_w": sc_conv_w, "sc_w_out": sc_w_out,
               "ffn_w_up": ffn_w_up, "ffn_conv_w": ffn_conv_w, "ffn_conv_b": ffn_conv_b, "ffn_w_down": ffn_w_down,
               "final_norm": final_norm}
    mom1 = {"c_ctx": m_c_ctx, "ada_w": m_ada_w, "ada_b": m_ada_b, "norm_mix": m_norm_mix, "norm_ffn": m_norm_ffn,
            "gla_w_in": m_gla_w_in, "gla_w_a2": m_gla_w_a2, "gla_b_a": m_gla_b_a, "gla_head_norm": m_gla_head_norm,
            "gla_w_out": m_gla_w_out, "sc_w_in": m_sc_w_in, "sc_conv_w": m_sc_conv_w, "sc_w_out": m_sc_w_out,
            "ffn_w_up": m_ffn_w_up, "ffn_conv_w": m_ffn_conv_w, "ffn_conv_b": m_ffn_conv_b,
            "ffn_w_down": m_ffn_w_down, "final_norm": m_final_norm}
    mom2 = {"c_ctx": v_c_ctx, "ada_w": v_ada_w, "ada_b": v_ada_b, "norm_mix": v_norm_mix, "norm_ffn": v_norm_ffn,
            "gla_w_in": v_gla_w_in, "gla_w_a2": v_gla_w_a2, "gla_b_a": v_gla_b_a, "gla_head_norm": v_gla_head_norm,
            "gla_w_out": v_gla_w_out, "sc_w_in": v_sc_w_in, "sc_conv_w": v_sc_conv_w, "sc_w_out": v_sc_w_out,
            "ffn_w_up": v_ffn_w_up, "ffn_conv_w": v_ffn_conv_w, "ffn_conv_b": v_ffn_conv_b,
            "ffn_w_down": v_ffn_w_down, "final_norm": v_final_norm}
    names = list(weights)
    grads = {k: grads[k].reshape(weights[k].shape) for k in names}

    big_names = ["ada_w", "gla_w_in", "gla_w_out", "sc_w_in", "sc_w_out", "ffn_w_up", "ffn_w_down"]
    small_names = [k for k in names if k not in big_names]
    delta, new_m, new_v = {}, {}, {}
    for k in big_names:
        shp = weights[k].shape
        as2d = lambda a: a.reshape(-1, shp[-1])
        d_, m_, v_ = _adamw(as2d(weights[k]), as2d(grads[k]), as2d(mom1[k]), as2d(mom2[k]), "adamw_" + k)
        delta[k], new_m[k], new_v[k] = d_.reshape(shp), m_.reshape(shp), v_.reshape(shp)
    packed = [_pack([src[k] for k in small_names]) for src in (weights, grads, mom1, mom2)]
    meta = packed[0][1]
    outs = _adamw(packed[0][0], packed[1][0], packed[2][0], packed[3][0], "adamw_small")
    for dst, o in zip((delta, new_m, new_v), outs):
        for k, a in zip(small_names, _unpack(o, meta)):
            dst[k] = a

    return (loss, grad_x, *[grads[k] for k in names], *[delta[k] for k in names], *[new_m[k] for k in names],
            *[new_v[k] for k in names])
```

```python
import functools

import jax
import jax.numpy as jnp
from jax import lax
from jax.experimental import pallas as pl
from jax.experimental.pallas import tpu as pltpu

F32 = jnp.float32
BF16 = jnp.bfloat16
MESH = pl.DeviceIdType.MESH

EPS = 1e-6
D = 1024
N_MOD = 6
HEADS = 4
DK = 128
DV = 256
KEY = HEADS * DK
RANK = 16
TAU = 16.0
CH = 64
GRID_W = 64
HID = 2560
GLA_IN = 2 * KEY + 2 * D + 2 * RANK
GLA_IN_PAD = 3200
Q_SCALE = DK ** -0.5
N_CHIPS = 4
N_DEV = 8

ADAM_LR = 0.001
ADAM_B1 = 0.9
ADAM_B2 = 0.999
ADAM_EPS = 1e-08
ADAM_WD = 0.01
ADAM_STEP = 10

VMEM_LIMIT = 56 * 1024 * 1024


def _params(sem):
    return pltpu.CompilerParams(dimension_semantics=sem, vmem_limit_bytes=VMEM_LIMIT)


def _tile(n, pref, mult=8):
    if n <= pref:
        return n
    for t in range(pref - pref % mult, 0, -mult):
        if n % t == 0:
            return t
    raise ValueError((n, pref, mult))


_NN = (((1,), (0,)), ((), ()))
_NT = (((1,), (1,)), ((), ()))
_TN = (((0,), (0,)), ((), ()))


def _dot(a, b, dims=_NN):
    return lax.dot_general(a.astype(BF16), b.astype(BF16), dims, preferred_element_type=F32)


def _dot_hi(a, b, dims=_NN):
    return lax.dot_general(a, b, dims, precision=lax.Precision.HIGHEST, preferred_element_type=F32)


def _sigmoid(x):
    return 1.0 / (1.0 + jnp.exp(-x))


def _rowsum(x):
    return jnp.sum(x, axis=0, keepdims=True)


def _mm(a, b, form, out_dtype, name, tm, tn):
    if form == "tn":
        K, M = a.shape
    else:
        M, K = a.shape
    N = b.shape[0] if form == "nt" else b.shape[1]
    tm = _tile(M, tm, 128)
    tn = _tile(N, tn, 128)
    dims = {"nn": _NN, "nt": _NT, "tn": _TN}[form]

    def body(a_ref, b_ref, o_ref):
        o_ref[...] = _dot(a_ref[...], b_ref[...], dims).astype(o_ref.dtype)

    if form == "tn":
        a_spec = pl.BlockSpec((K, tm), lambda i, j: (0, i))
    else:
        a_spec = pl.BlockSpec((tm, K), lambda i, j: (i, 0))
    if form == "nt":
        b_spec = pl.BlockSpec((tn, K), lambda i, j: (j, 0))
    else:
        b_spec = pl.BlockSpec((K, tn), lambda i, j: (0, j))
    return pl.pallas_call(
        body,
        name=name,
        grid=(M // tm, N // tn),
        in_specs=[a_spec, b_spec],
        out_specs=pl.BlockSpec((tm, tn), lambda i, j: (i, j)),
        out_shape=jax.ShapeDtypeStruct((M, N), out_dtype),
        compiler_params=_params(("parallel", "parallel")),
    )(a, b)


def _mod_fwd(h, gain, shift, scale, tpb_rows, name, y=None, gate=None):
    n = h.shape[0]
    tt = _tile(tpb_rows, 256)
    tpb = tpb_rows // tt
    has_res = y is not None

    def body(*refs):
        if has_res:
            h_ref, y_ref, gate_ref, gain_ref, sh_ref, sc_ref, hout_ref, hn_ref = refs
            hv = h_ref[...] + gate_ref[0] * y_ref[...]
            hout_ref[...] = hv
        else:
            h_ref, gain_ref, sh_ref, sc_ref, hn_ref = refs
            hv = h_ref[...]
        r = lax.rsqrt(jnp.mean(hv * hv, axis=-1, keepdims=True) + EPS)
        hn = (hv * r) * gain_ref[...] * (1.0 + sc_ref[0]) + sh_ref[0]
        hn_ref[...] = hn.astype(BF16)

    row = pl.BlockSpec((tt, D), lambda i: (i, 0))
    per_b = pl.BlockSpec((1, 1, D), lambda i: (i // tpb, 0, 0))
    vec = pl.BlockSpec((1, D), lambda i: (0, 0))
    if has_res:
        in_specs = [row, row, per_b, vec, per_b, per_b]
        args = (h, y, gate, gain, shift, scale)
        out_specs = [row, row]
        out_shape = [jax.ShapeDtypeStruct((n, D), F32), jax.ShapeDtypeStruct((n, D), BF16)]
    else:
        in_specs = [row, vec, per_b, per_b]
        args = (h, gain, shift, scale)
        out_specs = row
        out_shape = jax.ShapeDtypeStruct((n, D), BF16)
    return pl.pallas_call(
        body, name=name, grid=(n // tt,), in_specs=in_specs, out_specs=out_specs, out_shape=out_shape,
        compiler_params=_params(("parallel",)),
    )(*args)


def _mod_bwd(h_in, dhn, gain, scale, tpb_rows, name, dhn_row0=0, dh_out=None, y_prev=None, gate_prev=None,
             need_dh=True):
    n = h_in.shape[0]
    nb = n // tpb_rows
    tt = _tile(tpb_rows, 256)
    tpb = tpb_rows // tt
    off = dhn_row0 // tt
    assert dhn_row0 % tt == 0
    has_out = dh_out is not None
    has_prev = y_prev is not None

    def body(*refs):
        it = iter(refs)
        h_ref, dhn_ref, gain_ref, sc_ref = next(it), next(it), next(it), next(it)
        dho_ref = next(it) if has_out else None
        yp_ref, gp_ref = (next(it), next(it)) if has_prev else (None, None)
        dh_ref = next(it) if need_dh else None
        dsc_ref, dsh_ref, dgain_ref = next(it), next(it), next(it)
        dyp_ref, dgp_ref = (next(it), next(it)) if has_prev else (None, None)
        i = pl.program_id(0)

        @pl.when(i == 0)
        def _():
            dgain_ref[...] = jnp.zeros_like(dgain_ref)

        @pl.when(i % tpb == 0)
        def _():
            dsc_ref[...] = jnp.zeros_like(dsc_ref)
            dsh_ref[...] = jnp.zeros_like(dsh_ref)
            if has_prev:
                dgp_ref[...] = jnp.zeros_like(dgp_ref)

        hv = h_ref[...]
        r = lax.rsqrt(jnp.mean(hv * hv, axis=-1, keepdims=True) + EPS)
        y = hv * r
        gain_v = gain_ref[...]
        g = dhn_ref[...].astype(F32)
        dsh_ref[0] += _rowsum(g)
        dsc_ref[0] += _rowsum(g * (y * gain_v))
        drn = g * (1.0 + sc_ref[0])
        dgain_ref[...] += _rowsum(drn * y)
        if need_dh:
            dy = drn * gain_v
            dh = r * (dy - y * jnp.mean(dy * y, axis=-1, keepdims=True))
            if has_out:
                dh = dh + dho_ref[...]
            dh_ref[...] = dh
            if has_prev:
                dyp_ref[...] = (dh * gp_ref[0]).astype(BF16)
                dgp_ref[0] += _rowsum(dh * yp_ref[...])

    row = pl.BlockSpec((tt, D), lambda i: (i, 0))
    row_off = pl.BlockSpec((tt, D), lambda i: (i + off, 0))
    per_b = pl.BlockSpec((1, 1, D), lambda i: (i // tpb, 0, 0))
    vec = pl.BlockSpec((1, D), lambda i: (0, 0))
    in_specs = [row, row_off, vec, per_b]
    args = [h_in, dhn, gain, scale]
    if has_out:
        in_specs.append(row)
        args.append(dh_out)
    if has_prev:
        in_specs += [row, per_b]
        args += [y_prev, gate_prev]
    out_specs, out_shape, names = [], [], []
    if need_dh:
        out_specs.append(row)
        out_shape.append(jax.ShapeDtypeStruct((n, D), F32))
        names.append("dh")
    for nm in ("dscale", "dshift"):
        out_specs.append(per_b)
        out_shape.append(jax.ShapeDtypeStruct((nb, 1, D), F32))
        names.append(nm)
    out_specs.append(vec)
    out_shape.append(jax.ShapeDtypeStruct((1, D), F32))
    names.append("dgain")
    if has_prev:
        out_specs += [row, per_b]
        out_shape += [jax.ShapeDtypeStruct((n, D), BF16), jax.ShapeDtypeStruct((nb, 1, D), F32)]
        names += ["dy_prev", "dgate_prev"]
    outs = pl.pallas_call(
        body, name=name, grid=(n // tt,), in_specs=in_specs, out_specs=out_specs, out_shape=out_shape,
        compiler_params=_params(("arbitrary",)),
    )(*args)
    return dict(zip(names, outs))


def _final(h, f, gate, gain, tgt, tpb_rows):
    n = h.shape[0]
    nb = n // tpb_rows
    tt = _tile(tpb_rows, 256)
    tpb = tpb_rows // tt

    def body(h_ref, f_ref, gate_ref, gain_ref, tgt_ref, loss_ref, dh_ref, df_ref, dgate_ref, dgain_ref):
        i = pl.program_id(0)

        @pl.when(i == 0)
        def _():
            loss_ref[...] = jnp.zeros_like(loss_ref)
            dgain_ref[...] = jnp.zeros_like(dgain_ref)

        @pl.when(i % tpb == 0)
        def _():
            dgate_ref[...] = jnp.zeros_like(dgate_ref)

        fv = f_ref[...]
        gate_v = gate_ref[0]
        hv = h_ref[...] + gate_v * fv
        r = lax.rsqrt(jnp.mean(hv * hv, axis=-1, keepdims=True) + EPS)
        y = hv * r
        gain_v = gain_ref[...]
        e = y * gain_v - tgt_ref[...]
        s = jnp.sum(_rowsum(e * e), axis=1, keepdims=True) * (0.5 / D)
        loss_ref[...] += jnp.broadcast_to(s, loss_ref.shape)
        dout = e * (1.0 / D)
        dgain_ref[...] += _rowsum(dout * y)
        dy = dout * gain_v
        dh = r * (dy - y * jnp.mean(dy * y, axis=-1, keepdims=True))
        dh_ref[...] = dh
        df_ref[...] = (dh * gate_v).astype(BF16)
        dgate_ref[0] += _rowsum(dh * fv)

    row = pl.BlockSpec((tt, D), lambda i: (i, 0))
    per_b = pl.BlockSpec((1, 1, D), lambda i: (i // tpb, 0, 0))
    vec = pl.BlockSpec((1, D), lambda i: (0, 0))
    return pl.pallas_call(
        body, name="final_loss", grid=(n // tt,),
        in_specs=[row, row, per_b, vec, row],
        out_specs=[pl.BlockSpec((1, 128), lambda i: (0, 0)), row, row, per_b, vec],
        out_shape=[jax.ShapeDtypeStruct((1, 128), F32), jax.ShapeDtypeStruct((n, D), F32),
                   jax.ShapeDtypeStruct((n, D), BF16), jax.ShapeDtypeStruct((nb, 1, D), F32),
                   jax.ShapeDtypeStruct((1, D), F32)],
        compiler_params=_params(("arbitrary",)),
    )(h, f, gate, gain, tgt)


def _shift_dn(x, s):
    return jnp.concatenate([jnp.zeros((s, x.shape[1]), x.dtype), x[: x.shape[0] - s]], axis=0)


def _shift_up(x, s):
    return jnp.concatenate([x[s:], jnp.zeros((s, x.shape[1]), x.dtype)], axis=0)


def _row_dn1(x):
    t = lax.broadcasted_iota(jnp.int32, x.shape, 0)
    return jnp.where(t % GRID_W == 0, 0.0, pltpu.roll(x, 1, 0))


def _row_up1(x):
    t = lax.broadcasted_iota(jnp.int32, x.shape, 0)
    return jnp.where(t % GRID_W == GRID_W - 1, 0.0, pltpu.roll(x, x.shape[0] - 1, 0))


def _silu(x):
    return x * _sigmoid(x)


def _dsilu(x):
    s = _sigmoid(x)
    return s * (1.0 + x * (1.0 - s))


def _ffn_mid_fwd(u0, cw, cb, nb, t, name):
    nc = HID // 128

    def body(u_ref, w_ref, b_ref, z_ref):
        x = u_ref[...]
        u = (_shift_dn(x, GRID_W) * w_ref[0:1, :] + x * w_ref[1:2, :] + _shift_up(x, GRID_W) * w_ref[2:3, :]
             + b_ref[...])
        z_ref[...] = (u[:, :128] * _silu(u[:, 128:])).astype(BF16)

    return pl.pallas_call(
        body, name=name, grid=(nc, nb),
        in_specs=[pl.BlockSpec((t, 256), lambda j, b: (b, j)), pl.BlockSpec((3, 256), lambda j, b: (0, j)),
                  pl.BlockSpec((1, 256), lambda j, b: (0, j))],
        out_specs=pl.BlockSpec((t, 128), lambda j, b: (b, j)),
        out_shape=jax.ShapeDtypeStruct((nb * t, HID), BF16),
        compiler_params=_params(("parallel", "parallel")),
    )(u0, cw, cb)


def _ffn_mid_bwd(u0, cw, cb, dz, nb, t, name):
    nc = HID // 128

    def body(u_ref, w_ref, b_ref, dz_ref, du_ref, dw_ref, db_ref):
        b = pl.program_id(1)

        @pl.when(b == 0)
        def _():
            dw_ref[...] = jnp.zeros_like(dw_ref)
            db_ref[...] = jnp.zeros_like(db_ref)

        x = u_ref[...]
        w0, w1, w2 = w_ref[0:1, :], w_ref[1:2, :], w_ref[2:3, :]
        xd = _shift_dn(x, GRID_W)
        xu = _shift_up(x, GRID_W)
        u = xd * w0 + x * w1 + xu * w2 + b_ref[...]
        a = u[:, :128]
        gt = u[:, 128:]
        dzv = dz_ref[...]
        du = jnp.concatenate([dzv * _silu(gt), dzv * a * _dsilu(gt)], axis=1)
        db_ref[...] += _rowsum(du)
        dw_ref[0:1, :] += _rowsum(du * xd)
        dw_ref[1:2, :] += _rowsum(du * x)
        dw_ref[2:3, :] += _rowsum(du * xu)
        dx = _shift_up(du, GRID_W) * w0 + du * w1 + _shift_dn(du, GRID_W) * w2
        du_ref[...] = dx.astype(BF16)

    return pl.pallas_call(
        body, name=name, grid=(nc, nb),
        in_specs=[pl.BlockSpec((t, 256), lambda j, b: (b, j)), pl.BlockSpec((3, 256), lambda j, b: (0, j)),
                  pl.BlockSpec((1, 256), lambda j, b: (0, j)), pl.BlockSpec((t, 128), lambda j, b: (b, j))],
        out_specs=[pl.BlockSpec((t, 256), lambda j, b: (b, j)), pl.BlockSpec((3, 256), lambda j, b: (0, j)),
                   pl.BlockSpec((1, 256), lambda j, b: (0, j))],
        out_shape=[jax.ShapeDtypeStruct((nb * t, 2 * HID), BF16), jax.ShapeDtypeStruct((3, 2 * HID), F32),
                   jax.ShapeDtypeStruct((1, 2 * HID), F32)],
        compiler_params=_params(("parallel", "arbitrary")),
    )(u0, cw, cb, dz)


def _sc_mid_fwd(p, cw, nb, t):
    nc = D // 128

    def body(p_ref, w_ref, y_ref):
        x = p_ref[...]
        cv = x[:, 128:256] * x[:, 256:]
        cc = _row_dn1(cv) * w_ref[0:1, :] + cv * w_ref[1:2, :] + _row_up1(cv) * w_ref[2:3, :]
        y_ref[...] = (x[:, :128] * cc).astype(BF16)

    return pl.pallas_call(
        body, name="sc_mid_fwd", grid=(nc, nb),
        in_specs=[pl.BlockSpec((t, 384), lambda j, b: (b, j)), pl.BlockSpec((3, 128), lambda j, b: (0, j))],
        out_specs=pl.BlockSpec((t, 128), lambda j, b: (b, j)),
        out_shape=jax.ShapeDtypeStruct((nb * t, D), BF16),
        compiler_params=_params(("parallel", "parallel")),
    )(p, cw)


def _sc_mid_bwd(p, cw, dyb, nb, t):
    nc = D // 128

    def body(p_ref, w_ref, dy_ref, dp_ref, dw_ref):
        b = pl.program_id(1)

        @pl.when(b == 0)
        def _():
            dw_ref[...] = jnp.zeros_like(dw_ref)

        x = p_ref[...]
        w0, w1, w2 = w_ref[0:1, :], w_ref[1:2, :], w_ref[2:3, :]
        bg, cg, v = x[:, :128], x[:, 128:256], x[:, 256:]
        cv = cg * v
        cvd = _row_dn1(cv)
        cvu = _row_up1(cv)
        cc = cvd * w0 + cv * w1 + cvu * w2
        dy = dy_ref[...]
        dcc = dy * bg
        dw_ref[0:1, :] += _rowsum(dcc * cvd)
        dw_ref[1:2, :] += _rowsum(dcc * cv)
        dw_ref[2:3, :] += _rowsum(dcc * cvu)
        dcv = _row_up1(dcc) * w0 + dcc * w1 + _row_dn1(dcc) * w2
        dp_ref[...] = jnp.concatenate([dy * cc, dcv * v, dcv * cg], axis=1).astype(BF16)

    return pl.pallas_call(
        body, name="sc_mid_bwd", grid=(nc, nb),
        in_specs=[pl.BlockSpec((t, 384), lambda j, b: (b, j)), pl.BlockSpec((3, 128), lambda j, b: (0, j)),
                  pl.BlockSpec((t, 128), lambda j, b: (b, j))],
        out_specs=[pl.BlockSpec((t, 384), lambda j, b: (b, j)), pl.BlockSpec((3, 128), lambda j, b: (0, j))],
        out_shape=[jax.ShapeDtypeStruct((nb * t, 3 * D), BF16), jax.ShapeDtypeStruct((3, D), F32)],
        compiler_params=_params(("parallel", "arbitrary")),
    )(p, cw, dyb)


def _gla_decay_fwd(p_all, w2, b2):
    n = p_all.shape[0]
    tt = _tile(n, 512)

    def body(a_ref, w_ref, b_ref, la_ref):
        z = _dot(a_ref[...], w_ref[...]) + b_ref[...]
        la_ref[...] = (jnp.minimum(z, 0.0) - jnp.log(1.0 + jnp.exp(-jnp.abs(z)))) * (1.0 / TAU)

    return pl.pallas_call(
        body, name="gla_decay_fwd", grid=(n // tt,),
        in_specs=[pl.BlockSpec((tt, 128), lambda i: (i, (2 * KEY + 2 * D) // 128)),
                  pl.BlockSpec((128, 2 * KEY), lambda i: (0, 0)), pl.BlockSpec((1, 2 * KEY), lambda i: (0, 0))],
        out_specs=pl.BlockSpec((tt, 2 * KEY), lambda i: (i, 0)),
        out_shape=jax.ShapeDtypeStruct((n, 2 * KEY), F32),
        compiler_params=_params(("parallel",)),
    )(p_all, w2, b2)


def _gla_blocks(nb, nm, ncx):
    def main_idx(d, i):
        return jnp.clip(jnp.where(d == 0, i - ncx, nm - 1 - (i - ncx)), 0, nm - 1)

    def rowblk(d, b, i):
        cidx = jnp.where(d == 0, i, ncx - 1 - i)
        return jnp.where(i < ncx, nb * nm + b * ncx + cidx, b * nm + main_idx(d, i))

    def mainblk(d, b, i):
        return b * nm + main_idx(d, i)

    return rowblk, mainblk


def _gla_mask(d):
    row = lax.broadcasted_iota(jnp.int32, (CH, CH), 0)
    col = lax.broadcasted_iota(jnp.int32, (CH, CH), 1)
    diff = row - col
    mask = jnp.where(d == 0, diff, -diff) >= 0
    return mask, jnp.where(mask, 1.0, 0.0).astype(F32)


def _gla_chunk(mf, q, k, g):
    bc = _dot_hi(mf, g)
    bl = _rowsum(g)
    eq = jnp.exp(bc)
    ek = jnp.exp(-bc)
    ed = jnp.exp(bl - bc)
    return bl, eq, ek, ed, q * Q_SCALE * eq, k * ek, k * ed


def _gla_scan_fwd(p_all, la_all, nb, t, tc):
    nm, ncx = t // CH, tc // CH
    nst = nm + ncx
    rowblk, mainblk = _gla_blocks(nb, nm, ncx)

    def body(q_ref, k_ref, v_ref, la_ref, o_ref, ss_ref, st_ref):
        d = pl.program_id(0)
        i = pl.program_id(2)

        @pl.when(i == 0)
        def _():
            st_ref[...] = jnp.zeros_like(st_ref)

        mask, mf = _gla_mask(d)
        for h in range(HEADS):
            ksl = slice(h * DK, (h + 1) * DK)
            vsl = slice(h * DV, (h + 1) * DV)
            bl, _, _, _, qs, ks, kd = _gla_chunk(mf, q_ref[:, ksl], k_ref[:, ksl], la_ref[:, ksl])
            st = st_ref[h]
            ss_ref[0, 0, 0, h] = st
            v = v_ref[:, vsl]
            att = jnp.where(mask, _dot(qs, ks, _NT), 0.0)
            o_ref[0, :, vsl] = _dot(qs, st, _NT) + _dot(att, v)
            st_ref[h] = st * jnp.exp(bl) + _dot(v, kd, _TN)

    return pl.pallas_call(
        body, name="gla_scan_fwd", grid=(2, nb, nst),
        in_specs=[
            pl.BlockSpec((CH, KEY), lambda d, b, i: (rowblk(d, b, i), 0)),
            pl.BlockSpec((CH, KEY), lambda d, b, i: (rowblk(d, b, i), 1)),
            pl.BlockSpec((CH, D), lambda d, b, i: (rowblk(d, b, i), 1)),
            pl.BlockSpec((CH, KEY), lambda d, b, i: (rowblk(d, b, i), d)),
        ],
        out_specs=[
            pl.BlockSpec((1, CH, D), lambda d, b, i: (d, mainblk(d, b, i), 0)),
            pl.BlockSpec((1, 1, 1, HEADS, DV, DK), lambda d, b, i: (d, b, i, 0, 0, 0)),
        ],
        out_shape=[jax.ShapeDtypeStruct((2, nb * t, D), F32),
                   jax.ShapeDtypeStruct((2, nb, nst, HEADS, DV, DK), F32)],
        scratch_shapes=[pltpu.VMEM((HEADS, DV, DK), F32)],
        compiler_params=_params(("parallel", "parallel", "arbitrary")),
    )(p_all, p_all, p_all, la_all)


def _gla_scan_bwd(p_all, la_all, do, ss, nb, t, tc):
    nm, ncx = t // CH, tc // CH
    nst = nm + ncx
    ntot = nb * (t + tc)
    rowblk, mainblk = _gla_blocks(nb, nm, ncx)

    def body(q_ref, k_ref, v_ref, la_ref, do_ref, ss_ref, dq_ref, dk_ref, dv_ref, dla_ref, dst_ref):
        d = pl.program_id(0)
        ip = pl.program_id(2)
        i = nst - 1 - ip

        @pl.when(ip == 0)
        def _():
            dst_ref[...] = jnp.zeros_like(dst_ref)

        mask, mf = _gla_mask(d)
        live = jnp.where(i >= ncx, 1.0, 0.0)
        for h in range(HEADS):
            ksl = slice(h * DK, (h + 1) * DK)
            vsl = slice(h * DV, (h + 1) * DV)
            bl, eq, ek, ed, qs, ks, kd = _gla_chunk(mf, q_ref[:, ksl], k_ref[:, ksl], la_ref[:, ksl])
            st = ss_ref[0, 0, 0, h]
            dst = dst_ref[h]
            v = v_ref[:, vsl]
            dov = do_ref[:, vsl] * live
            att = jnp.where(mask, _dot(qs, ks, _NT), 0.0)
            datt = jnp.where(mask, _dot(dov, v, _NT), 0.0)
            dqs = _dot(dov, st) + _dot(datt, ks)
            dks = _dot(datt, qs, _TN)
            dv_ref[0, :, vsl] = _dot(att, dov, _TN) + _dot(kd, dst, _NT)
            dkd = _dot(v, dst)
            e = jnp.exp(bl)
            dbl = e * _rowsum(st * dst) + _rowsum(dkd * kd)
            dst_ref[h] = _dot(dov, qs, _TN) + dst * e
            dq_ref[0, :, ksl] = dqs * eq * Q_SCALE
            dk_ref[0, :, ksl] = dks * ek + dkd * ed
            db = dqs * qs - dks * ks - dkd * kd
            dla_ref[:, ksl] = _dot_hi(mf, db, _TN) + dbl

    rev = lambda f: (lambda d, b, ip: f(d, b, nst - 1 - ip))
    return pl.pallas_call(
        body, name="gla_scan_bwd", grid=(2, nb, nst),
        in_specs=[
            pl.BlockSpec((CH, KEY), rev(lambda d, b, i: (rowblk(d, b, i), 0))),
            pl.BlockSpec((CH, KEY), rev(lambda d, b, i: (rowblk(d, b, i), 1))),
            pl.BlockSpec((CH, D), rev(lambda d, b, i: (rowblk(d, b, i), 1))),
            pl.BlockSpec((CH, KEY), rev(lambda d, b, i: (rowblk(d, b, i), d))),
            pl.BlockSpec((CH, D), rev(lambda d, b, i: (mainblk(d, b, i), 0))),
            pl.BlockSpec((1, 1, 1, HEADS, DV, DK), rev(lambda d, b, i: (d, b, i, 0, 0, 0))),
        ],
        out_specs=[
            pl.BlockSpec((1, CH, KEY), rev(lambda d, b, i: (d, rowblk(d, b, i), 0))),
            pl.BlockSpec((1, CH, KEY), rev(lambda d, b, i: (d, rowblk(d, b, i), 0))),
            pl.BlockSpec((1, CH, D), rev(lambda d, b, i: (d, rowblk(d, b, i), 0))),
            pl.BlockSpec((CH, KEY), rev(lambda d, b, i: (rowblk(d, b, i), d))),
        ],
        out_shape=[jax.ShapeDtypeStruct((2, ntot, KEY), F32), jax.ShapeDtypeStruct((2, ntot, KEY), F32),
                   jax.ShapeDtypeStruct((2, ntot, D), F32), jax.ShapeDtypeStruct((ntot, 2 * KEY), F32)],
        scratch_shapes=[pltpu.VMEM((HEADS, DV, DK), F32)],
        compiler_params=_params(("parallel", "parallel", "arbitrary")),
    )(p_all, p_all, p_all, la_all, do, ss)


def _gla_post_fwd(o2, p_all, head_gain, n):
    tt = _tile(n, 256)

    def body(o_ref, g_ref, hg_ref, y_ref):
        o = o_ref[0] + o_ref[1]
        gv = g_ref[...]
        hg = hg_ref[...]
        for h in range(HEADS):
            oh = o[:, h * DV:(h + 1) * DV]
            r = lax.rsqrt(jnp.mean(oh * oh, axis=-1, keepdims=True) + EPS)
            y_ref[:, h * DV:(h + 1) * DV] = ((oh * r) * hg * _silu(gv[:, h * DV:(h + 1) * DV])).astype(BF16)

    return pl.pallas_call(
        body, name="gla_post_fwd", grid=(n // tt,),
        in_specs=[pl.BlockSpec((2, tt, D), lambda i: (0, i, 0)), pl.BlockSpec((tt, D), lambda i: (i, 2)),
                  pl.BlockSpec((1, DV), lambda i: (0, 0))],
        out_specs=pl.BlockSpec((tt, D), lambda i: (i, 0)),
        out_shape=jax.ShapeDtypeStruct((n, D), BF16),
        compiler_params=_params(("parallel",)),
    )(o2, p_all, head_gain)


def _gla_post_bwd(o2, p_all, head_gain, dyb, n):
    tt = _tile(n, 256)

    def body(o_ref, g_ref, hg_ref, dy_ref, do_ref, dg_ref, dhg_ref):
        i = pl.program_id(0)

        @pl.when(i == 0)
        def _():
            dhg_ref[...] = jnp.zeros_like(dhg_ref)

        o = o_ref[0] + o_ref[1]
        gv = g_ref[...]
        hg = hg_ref[...]
        dy = dy_ref[...]
        acc = jnp.zeros((1, DV), F32)
        for h in range(HEADS):
            sl = slice(h * DV, (h + 1) * DV)
            oh = o[:, sl]
            r = lax.rsqrt(jnp.mean(oh * oh, axis=-1, keepdims=True) + EPS)
            on = oh * r
            gh = gv[:, sl]
            dyh = dy[:, sl]
            dg_ref[:, sl] = dyh * (on * hg) * _dsilu(gh)
            dog = dyh * _silu(gh)
            acc = acc + _rowsum(dog * on)
            don = dog * hg
            do_ref[:, sl] = r * (don - on * jnp.mean(don * on, axis=-1, keepdims=True))
        dhg_ref[...] += acc

    return pl.pallas_call(
        body, name="gla_post_bwd", grid=(n // tt,),
        in_specs=[pl.BlockSpec((2, tt, D), lambda i: (0, i, 0)), pl.BlockSpec((tt, D), lambda i: (i, 2)),
                  pl.BlockSpec((1, DV), lambda i: (0, 0)), pl.BlockSpec((tt, D), lambda i: (i, 0))],
        out_specs=[pl.BlockSpec((tt, D), lambda i: (i, 0)), pl.BlockSpec((tt, D), lambda i: (i, 0)),
                   pl.BlockSpec((1, DV), lambda i: (0, 0))],
        out_shape=[jax.ShapeDtypeStruct((n, D), F32), jax.ShapeDtypeStruct((n, D), F32),
                   jax.ShapeDtypeStruct((1, DV), F32)],
        compiler_params=_params(("arbitrary",)),
    )(o2, p_all, head_gain, dyb)


def _gla_assemble(p_all, w2, b2, dq, dk, dv, dla, dgate, n):
    ntot = p_all.shape[0]
    tt = _tile(n, 128)
    nmain = n // tt
    assert ntot % tt == 0

    def body(a_ref, w_ref, b_ref, dq_ref, dk_ref, dv_ref, dla_ref, dg_ref, dp_ref, dw_ref, db_ref):
        i = pl.program_id(0)

        @pl.when(i == 0)
        def _():
            dw_ref[...] = jnp.zeros_like(dw_ref)
            db_ref[...] = jnp.zeros_like(db_ref)

        a = a_ref[...]
        w = w_ref[...]
        z = _dot(a, w) + b_ref[...]
        dz = dla_ref[...] * (1.0 / (1.0 + jnp.exp(z))) * (1.0 / TAU)
        dw_ref[...] += _dot(a, dz, _TN)
        db_ref[...] += _rowsum(dz)
        dp_ref[:, 0:KEY] = ((dq_ref[0] + dq_ref[1]) * 1.0).astype(BF16)
        dp_ref[:, KEY:2 * KEY] = (dk_ref[0] + dk_ref[1]).astype(BF16)
        dp_ref[:, 2 * KEY:2 * KEY + D] = (dv_ref[0] + dv_ref[1]).astype(BF16)
        dp_ref[:, 2 * KEY + D:2 * KEY + 2 * D] = (dg_ref[...] * jnp.where(i < nmain, 1.0, 0.0)).astype(BF16)
        dp_ref[:, 2 * KEY + 2 * D:GLA_IN_PAD] = _dot(dz, w, _NT).astype(BF16)

    return pl.pallas_call(
        body, name="gla_assemble", grid=(ntot // tt,),
        in_specs=[pl.BlockSpec((tt, 128), lambda i: (i, (2 * KEY + 2 * D) // 128)),
                  pl.BlockSpec((128, 2 * KEY), lambda i: (0, 0)), pl.BlockSpec((1, 2 * KEY), lambda i: (0, 0)),
                  pl.BlockSpec((2, tt, KEY), lambda i: (0, i, 0)), pl.BlockSpec((2, tt, KEY), lambda i: (0, i, 0)),
                  pl.BlockSpec((2, tt, D), lambda i: (0, i, 0)), pl.BlockSpec((tt, 2 * KEY), lambda i: (i, 0)),
                  pl.BlockSpec((tt, D), lambda i: (jnp.minimum(i, nmain - 1), 0))],
        out_specs=[pl.BlockSpec((tt, GLA_IN_PAD), lambda i: (i, 0)), pl.BlockSpec((128, 2 * KEY), lambda i: (0, 0)),
                   pl.BlockSpec((1, 2 * KEY), lambda i: (0, 0))],
        out_shape=[jax.ShapeDtypeStruct((ntot, GLA_IN_PAD), BF16), jax.ShapeDtypeStruct((128, 2 * KEY), F32),
                   jax.ShapeDtypeStruct((1, 2 * KEY), F32)],
        compiler_params=_params(("arbitrary",)),
    )(p_all, w2, b2, dq, dk, dv, dla, dgate)


ADA_ROWS = 24
ADA_SH = N_MOD * D // N_CHIPS


def _ada_fwd(cvec, ada_w, ada_b_sh):
    def body(c_ref, w_ref, b_ref, o_ref):
        o_ref[0] = _dot(_silu(c_ref[...]), w_ref[0]) + b_ref[0]

    return pl.pallas_call(
        body, name="ada_fwd", grid=(2,),
        in_specs=[pl.BlockSpec((ADA_ROWS, D), lambda l: (0, 0)), pl.BlockSpec((1, D, ADA_SH), lambda l: (l, 0, 0)),
                  pl.BlockSpec((1, 1, ADA_SH), lambda l: (l, 0, 0))],
        out_specs=pl.BlockSpec((1, ADA_ROWS, ADA_SH), lambda l: (l, 0, 0)),
        out_shape=jax.ShapeDtypeStruct((2, ADA_ROWS, ADA_SH), F32),
        compiler_params=_params(("parallel",)),
    )(cvec, ada_w, ada_b_sh)


def _ada_bwd(cvec, ada_w, dmod_sh):
    def body(c_ref, w_ref, dm_ref, gw_ref, dc_ref):
        dm = dm_ref[0]
        gw_ref[0] = _dot(_silu(c_ref[...]), dm, _TN)
        dc_ref[0] = _dot(dm, w_ref[0], _NT)

    return pl.pallas_call(
        body, name="ada_bwd", grid=(2,),
        in_specs=[pl.BlockSpec((ADA_ROWS, D), lambda l: (0, 0)), pl.BlockSpec((1, D, ADA_SH), lambda l: (l, 0, 0)),
                  pl.BlockSpec((1, ADA_ROWS, ADA_SH), lambda l: (l, 0, 0))],
        out_specs=[pl.BlockSpec((1, D, ADA_SH), lambda l: (l, 0, 0)), pl.BlockSpec((1, ADA_ROWS, D), lambda l: (l, 0, 0))],
        out_shape=[jax.ShapeDtypeStruct((2, D, ADA_SH), F32), jax.ShapeDtypeStruct((2, ADA_ROWS, D), F32)],
        compiler_params=_params(("parallel",)),
    )(cvec, ada_w, dmod_sh)


def _sum_slots(x, name):
    s, r, _ = x.shape

    def body(x_ref, o_ref):
        acc = x_ref[0]
        for k in range(1, s):
            acc = acc + x_ref[k]
        o_ref[...] = acc

    return pl.pallas_call(
        body, name=name, out_shape=jax.ShapeDtypeStruct((r, 128), F32),
        in_specs=[pl.BlockSpec(memory_space=pltpu.VMEM)], out_specs=pl.BlockSpec(memory_space=pltpu.VMEM),
    )(x)


def _cctx_grad(dscc_parts, c_ctx):
    def body(p_ref, c_ref, o_ref):
        acc = p_ref[0]
        for k in range(1, N_CHIPS):
            acc = acc + p_ref[k]
        o_ref[...] = acc * _dsilu(c_ref[...])

    return pl.pallas_call(
        body, name="cctx_grad", out_shape=jax.ShapeDtypeStruct((8, 128), F32),
        in_specs=[pl.BlockSpec(memory_space=pltpu.VMEM)] * 2, out_specs=pl.BlockSpec(memory_space=pltpu.VMEM),
    )(dscc_parts, c_ctx)


def _adamw(w, g, m, v, name):
    r, cdim = w.shape
    tr = _tile(r, 256)
    c1 = 1.0 - ADAM_B1 ** ADAM_STEP
    c2 = 1.0 - ADAM_B2 ** ADAM_STEP

    def body(w_ref, g_ref, m_ref, v_ref, d_ref, mo_ref, vo_ref):
        gv = g_ref[...]
        mn = ADAM_B1 * m_ref[...] + (1.0 - ADAM_B1) * gv
        vn = ADAM_B2 * v_ref[...] + (1.0 - ADAM_B2) * (gv * gv)
        mo_ref[...] = mn
        vo_ref[...] = vn
        d_ref[...] = -ADAM_LR * ((mn / c1) / (jnp.sqrt(vn / c2) + ADAM_EPS) + ADAM_WD * w_ref[...])

    spec = pl.BlockSpec((tr, cdim), lambda i: (i, 0))
    sds = jax.ShapeDtypeStruct((r, cdim), F32)
    return pl.pallas_call(
        body, name=name, grid=(r // tr,), in_specs=[spec] * 4, out_specs=[spec] * 3, out_shape=[sds] * 3,
        compiler_params=_params(("parallel",)),
    )(w, g, m, v)


def _place():
    x, y, c = lax.axis_index("x"), lax.axis_index("y"), lax.axis_index("c")
    return x, y, c


def _allgather_small(blk, name):
    m_per, n = blk.shape

    def body(x_ref, out_ref, send_sems, recv_sems, local_sem):
        x, y, c = _place()
        me, sibling = (x, y, c), (x, y, 1 - c)
        chips = [(1 - x, y), (x, 1 - y), (1 - x, 1 - y)]

        def rows(px, py, pc):
            return out_ref.at[pl.ds((4 * px + 2 * py + pc) * m_per, m_per), :]

        def copy(k, block, to, src=None):
            return pltpu.make_async_remote_copy(
                src_ref=rows(*block) if src is None else src, dst_ref=rows(*block),
                send_sem=send_sems.at[k], recv_sem=recv_sems.at[k], device_id=to, device_id_type=MESH)

        mine = pltpu.make_async_copy(x_ref, rows(*me), local_sem)
        mine.start()
        first = [copy(0, me, sibling, src=x_ref)]
        first += [copy(1 + j, me, (*chip, c), src=x_ref) for j, chip in enumerate(chips)]
        for cp in first:
            cp.start()
        passed = [copy(4 + j, (*chip, c), sibling) for j, chip in enumerate(chips)]
        for j, chip in enumerate(chips):
            copy(1 + j, (*chip, c), me).wait_recv()
            passed[j].start()
        copy(0, sibling, me).wait_recv()
        for j, chip in enumerate(chips):
            copy(4 + j, (*chip, 1 - c), me).wait_recv()
        for cp in first + passed:
            cp.wait_send()
        mine.wait()

    return pl.pallas_call(
        body, name=name,
        out_shape=jax.ShapeDtypeStruct((N_DEV * m_per, n), blk.dtype),
        in_specs=[pl.BlockSpec(memory_space=pltpu.VMEM)],
        out_specs=pl.BlockSpec(memory_space=pltpu.VMEM),
        scratch_shapes=[pltpu.SemaphoreType.DMA((7,)), pltpu.SemaphoreType.DMA((7,)), pltpu.SemaphoreType.DMA],
    )(blk)


def _other_chips(x, y):
    return [(1 - x, y), (x, 1 - y), (1 - x, 1 - y)]


def _weights_allgather(own):
    r = own.shape[0]
    hr = r // 2

    def body(own_ref, out_ref, send_sems, recv_sems, fsend_sems, frecv_sems, local_sem):
        x, y, c = _place()
        chip = 2 * x + y
        others = _other_chips(x, y)

        def half(ch, cc):
            return out_ref.at[ch, pl.ds(cc * hr, hr), :]

        mine = pltpu.make_async_copy(own_ref, out_ref.at[chip], local_sem)
        mine.start()
        sends = []
        for j, (ox, oy) in enumerate(others):
            cp = pltpu.make_async_remote_copy(
                src_ref=own_ref.at[pl.ds(c * hr, hr), :], dst_ref=half(chip, c),
                send_sem=send_sems.at[j], recv_sem=recv_sems.at[j], device_id=(ox, oy, c), device_id_type=MESH)
            cp.start()
            sends.append(cp)
        fwd = []
        for j, (ox, oy) in enumerate(others):
            och = 2 * ox + oy
            pltpu.make_async_remote_copy(
                src_ref=half(och, c), dst_ref=half(och, c), send_sem=send_sems.at[j], recv_sem=recv_sems.at[j],
                device_id=(ox, oy, c), device_id_type=MESH).wait_recv()
            cp = pltpu.make_async_remote_copy(
                src_ref=half(och, c), dst_ref=half(och, c), send_sem=fsend_sems.at[j], recv_sem=frecv_sems.at[j],
                device_id=(x, y, 1 - c), device_id_type=MESH)
            cp.start()
            fwd.append(cp)
        for j, (ox, oy) in enumerate(others):
            och = 2 * ox + oy
            pltpu.make_async_remote_copy(
                src_ref=half(och, 1 - c), dst_ref=half(och, 1 - c), send_sem=fsend_sems.at[j],
                recv_sem=frecv_sems.at[j], device_id=(x, y, 1 - c), device_id_type=MESH).wait_recv()
        for cp in sends + fwd:
            cp.wait_send()
        mine.wait()

    any_spec = pl.BlockSpec(memory_space=pl.ANY)
    return pl.pallas_call(
        body, name="weights_allgather",
        out_shape=jax.ShapeDtypeStruct((N_CHIPS, r, D), own.dtype),
        in_specs=[any_spec], out_specs=any_spec,
        scratch_shapes=[pltpu.SemaphoreType.DMA((3,)), pltpu.SemaphoreType.DMA((3,)),
                        pltpu.SemaphoreType.DMA((3,)), pltpu.SemaphoreType.DMA((3,)), pltpu.SemaphoreType.DMA],
    )(own)


def _rs_pair_exchange(g):
    r = g.shape[1]
    hr = r // 2

    def body(g_ref, got_ref, send_sem, recv_sem):
        x, y, c = _place()
        cp = pltpu.make_async_remote_copy(
            src_ref=g_ref.at[:, pl.ds((1 - c) * hr, hr), :], dst_ref=got_ref, send_sem=send_sem, recv_sem=recv_sem,
            device_id=(x, y, 1 - c), device_id_type=MESH)
        cp.start()
        cp.wait()

    any_spec = pl.BlockSpec(memory_space=pl.ANY)
    return pl.pallas_call(
        body, name="rs_pair_exchange",
        out_shape=jax.ShapeDtypeStruct((N_CHIPS, hr, D), F32),
        in_specs=[any_spec], out_specs=any_spec,
        scratch_shapes=[pltpu.SemaphoreType.DMA, pltpu.SemaphoreType.DMA],
    )(g)


def _rs_chip_sum(place, g, got):
    r = g.shape[1]
    hr = r // 2
    tr = _tile(hr, 640, 16)
    nt = hr // tr

    def body(pl_ref, g_ref, got_ref, p16_ref, p32_ref):
        s = pl.program_id(1)
        p = g_ref[0] + got_ref[0]
        p16_ref[0] = p.astype(BF16)

        @pl.when(s == pl_ref[1])
        def _():
            p32_ref[...] = p

    return pl.pallas_call(
        body, name="rs_chip_sum",
        grid_spec=pltpu.PrefetchScalarGridSpec(
            num_scalar_prefetch=1, grid=(nt, N_CHIPS),
            in_specs=[pl.BlockSpec((1, tr, D), lambda i, s, pr: (s, pr[0] * nt + i, 0)),
                      pl.BlockSpec((1, tr, D), lambda i, s, pr: (s, i, 0))],
            out_specs=[pl.BlockSpec((1, tr, D), lambda i, s, pr: (s, i, 0)),
                       pl.BlockSpec((tr, D), lambda i, s, pr: (i, 0))]),
        out_shape=[jax.ShapeDtypeStruct((N_CHIPS, hr, D), BF16), jax.ShapeDtypeStruct((hr, D), F32)],
        compiler_params=_params(("parallel", "arbitrary")),
    )(place, g, got)


def _rs_scatter(p16):
    def body(p_ref, out_ref, send_sems, recv_sems, local_sem):
        x, y, c = _place()
        chip = 2 * x + y
        others = _other_chips(x, y)
        mine = pltpu.make_async_copy(p_ref.at[chip], out_ref.at[chip], local_sem)
        mine.start()
        sends = []
        for j, (ox, oy) in enumerate(others):
            cp = pltpu.make_async_remote_copy(
                src_ref=p_ref.at[2 * ox + oy], dst_ref=out_ref.at[chip], send_sem=send_sems.at[j],
                recv_sem=recv_sems.at[j], device_id=(ox, oy, c), device_id_type=MESH)
            cp.start()
            sends.append(cp)
        for j, (ox, oy) in enumerate(others):
            och = 2 * ox + oy
            pltpu.make_async_remote_copy(
                src_ref=p_ref.at[och], dst_ref=out_ref.at[och], send_sem=send_sems.at[j], recv_sem=recv_sems.at[j],
                device_id=(ox, oy, c), device_id_type=MESH).wait_recv()
        for cp in sends:
            cp.wait_send()
        mine.wait()

    any_spec = pl.BlockSpec(memory_space=pl.ANY)
    return pl.pallas_call(
        body, name="rs_scatter",
        out_shape=jax.ShapeDtypeStruct(p16.shape, p16.dtype),
        in_specs=[any_spec], out_specs=any_spec,
        scratch_shapes=[pltpu.SemaphoreType.DMA((3,)), pltpu.SemaphoreType.DMA((3,)), pltpu.SemaphoreType.DMA],
    )(p16)


def _rs_final_sum(place, parts, p32):
    hr = parts.shape[1]
    tr = _tile(hr, 640, 16)

    def body(pl_ref, parts_ref, p32_ref, o_ref):
        chip = pl_ref[1]
        own = p32_ref[...]
        acc = jnp.where(chip == 0, own, parts_ref[0].astype(F32))
        for s in range(1, N_CHIPS):
            acc = acc + jnp.where(chip == s, own, parts_ref[s].astype(F32))
        o_ref[...] = acc

    return pl.pallas_call(
        body, name="rs_final_sum",
        grid_spec=pltpu.PrefetchScalarGridSpec(
            num_scalar_prefetch=1, grid=(hr // tr,),
            in_specs=[pl.BlockSpec((N_CHIPS, tr, D), lambda i, pr: (0, i, 0)),
                      pl.BlockSpec((tr, D), lambda i, pr: (i, 0))],
            out_specs=pl.BlockSpec((tr, D), lambda i, pr: (i, 0))),
        out_shape=jax.ShapeDtypeStruct((hr, D), F32),
        compiler_params=_params(("parallel",)),
    )(place, parts, p32)


def _rs_pair_gather(halfsum):
    hr = halfsum.shape[0]

    def body(h_ref, out_ref, send_sem, recv_sem, local_sem):
        x, y, c = _place()
        mine = pltpu.make_async_copy(h_ref, out_ref.at[pl.ds(c * hr, hr), :], local_sem)
        mine.start()
        cp = pltpu.make_async_remote_copy(
            src_ref=h_ref, dst_ref=out_ref.at[pl.ds(c * hr, hr), :], send_sem=send_sem, recv_sem=recv_sem,
            device_id=(x, y, 1 - c), device_id_type=MESH)
        cp.start()
        pltpu.make_async_remote_copy(
            src_ref=h_ref, dst_ref=out_ref.at[pl.ds((1 - c) * hr, hr), :], send_sem=send_sem, recv_sem=recv_sem,
            device_id=(x, y, 1 - c), device_id_type=MESH).wait_recv()
        cp.wait_send()
        mine.wait()

    any_spec = pl.BlockSpec(memory_space=pl.ANY)
    return pl.pallas_call(
        body, name="rs_pair_gather",
        out_shape=jax.ShapeDtypeStruct((2 * hr, D), F32),
        in_specs=[any_spec], out_specs=any_spec,
        scratch_shapes=[pltpu.SemaphoreType.DMA, pltpu.SemaphoreType.DMA, pltpu.SemaphoreType.DMA],
    )(halfsum)


def _interleave(w, parts):
    n = w.shape[0] // parts
    return w.reshape(parts, n // 128, 128, *w.shape[1:]).swapaxes(0, 1).reshape(w.shape)


def _deinterleave(w, parts):
    n = w.shape[0] // parts
    return w.reshape(n // 128, parts, 128, *w.shape[1:]).swapaxes(0, 1).reshape(w.shape)


def _local_step(x, ctx, tgt, mods, mc, wts, small):
    nb, t, _ = x.shape
    tc = ctx.shape[1]
    n = nb * t
    nc = nb * tc
    xf = x.reshape(n, D)
    cf = ctx.reshape(nc, D)
    tf = tgt.reshape(n, D)
    vec = lambda a: a.reshape(1, -1)
    m = [[mods[l, :, k, :].reshape(nb, 1, D) for k in range(N_MOD)] for l in range(2)]
    mc_b = [jnp.broadcast_to(mc[k].reshape(1, 1, D), (nb, 1, D)) for k in range(2)]

    w_gin = jnp.pad(wts["gla_in_t"], ((0, GLA_IN_PAD - GLA_IN), (0, 0)))
    w_sin = _interleave(wts["sc_in_t"], 3)
    w_up = [_interleave(wts["up_t"][l], 2) for l in range(2)]
    cw = [_interleave(small["ffn_conv_w"][l].T, 2).T for l in range(2)]
    cb = [_interleave(small["ffn_conv_b"][l], 2).reshape(1, -1) for l in range(2)]
    w2 = jnp.zeros((128, 2 * KEY), F32)
    w2 = w2.at[0:RANK, 0:KEY].set(small["gla_w_a2"][0]).at[RANK:2 * RANK, KEY:].set(small["gla_w_a2"][1])
    b2 = small["gla_b_a"].reshape(1, 2 * KEY)
    hg = small["gla_head_norm"].reshape(1, DV)

    hn0 = _mod_fwd(xf, vec(small["norm_mix"][0]), m[0][0], m[0][1], t, "mod0_main")
    hnc = _mod_fwd(cf, vec(small["norm_mix"][0]), mc_b[0], mc_b[1], tc, "mod0_ctx")
    hn_all = jnp.concatenate([hn0, hnc], axis=0)
    p_all = _mm(hn_all, w_gin, "nt", F32, "gla_in_proj", 768, 3200)
    la_all = _gla_decay_fwd(p_all, w2, b2)
    o2, ss = _gla_scan_fwd(p_all, la_all, nb, t, tc)
    yb0 = _gla_post_fwd(o2, p_all, hg, n)
    y0 = _mm(yb0, wts["gla_out"], "nn", F32, "gla_out_proj", 1024, 1024)
    h1, hn1 = _mod_fwd(xf, vec(small["norm_ffn"][0]), m[0][3], m[0][4], t, "mod0_ffn", y=y0, gate=m[0][2])
    u0 = _mm(hn1, w_up[0], "nt", F32, "ffn0_up", 1024, 1280)
    z0 = _ffn_mid_fwd(u0, cw[0], cb[0], nb, t, "ffn0_mid_fwd")
    f0 = _mm(z0, wts["down"][0], "nn", F32, "ffn0_down", 1024, 1024)
    h2, hn2 = _mod_fwd(h1, vec(small["norm_mix"][1]), m[1][0], m[1][1], t, "mod1_mix", y=f0, gate=m[0][5])
    p1 = _mm(hn2, w_sin, "nt", F32, "sc_in_proj", 1024, 1536)
    yb1 = _sc_mid_fwd(p1, small["sc_conv_w"], nb, t)
    y1 = _mm(yb1, wts["sc_out"], "nn", F32, "sc_out_proj", 1024, 1024)
    h3, hn3 = _mod_fwd(h2, vec(small["norm_ffn"][1]), m[1][3], m[1][4], t, "mod1_ffn", y=y1, gate=m[1][2])
    u1 = _mm(hn3, w_up[1], "nt", F32, "ffn1_up", 1024, 1280)
    z1 = _ffn_mid_fwd(u1, cw[1], cb[1], nb, t, "ffn1_mid_fwd")
    f1 = _mm(z1, wts["down"][1], "nn", F32, "ffn1_down", 1024, 1024)
    loss, dh4, df1, dm15, dfinal = _final(h3, f1, m[1][5], vec(small["final_norm"]), tf, t)

    gb, gs = {}, {}
    dmods = [[None] * N_MOD for _ in range(2)]
    dmods[1][5] = dm15

    def ffn_bwd(l, df, u, z, hn, tag):
        dz = _mm(df, wts["down"][l], "nt", F32, f"ffn{l}_down_dx", 1024, 1280)
        gdown = _mm(z, df, "tn", F32, f"ffn{l}_down_dw", 640, 1024)
        du, dcw, dcb = _ffn_mid_bwd(u, cw[l], cb[l], dz, nb, t, f"ffn{l}_mid_bwd")
        dhn = _mm(du, w_up[l], "nn", F32, f"ffn{l}_up_dx", 512, 512)
        gup = _deinterleave(_mm(du, hn, "tn", F32, f"ffn{l}_up_dw", 640, 1024), 2)
        return dhn, gdown, gup, _deinterleave(dcw.T, 2).T, _deinterleave(dcb.reshape(-1), 2)

    dhn3, gdown1, gup1, dcw1, dcb1 = ffn_bwd(1, df1, u1, z1, hn3, "ffn1")
    r = _mod_bwd(h3, dhn3, vec(small["norm_ffn"][1]), m[1][4], t, "mod1_ffn_bwd", dh_out=dh4, y_prev=y1,
                 gate_prev=m[1][2])
    dh3, dmods[1][4], dmods[1][3], dnf1, dy1, dmods[1][2] = (r["dh"], r["dscale"], r["dshift"], r["dgain"],
                                                             r["dy_prev"], r["dgate_prev"])
    dyb1 = _mm(dy1, wts["sc_out"], "nt", F32, "sc_out_dx", 1024, 1024)
    gb["sc_out"] = _mm(yb1, dy1, "tn", F32, "sc_out_dw", 512, 1024)
    dp1, dscw = _sc_mid_bwd(p1, small["sc_conv_w"], dyb1, nb, t)
    dhn2 = _mm(dp1, w_sin, "nn", F32, "sc_in_dx", 1024, 512)
    gb["sc_in_t"] = _deinterleave(_mm(dp1, hn2, "tn", F32, "sc_in_dw", 768, 1024), 3)
    r = _mod_bwd(h2, dhn2, vec(small["norm_mix"][1]), m[1][1], t, "mod1_mix_bwd", dh_out=dh3, y_prev=f0,
                 gate_prev=m[0][5])
    dh2, dmods[1][1], dmods[1][0], dnm1, df0, dmods[0][5] = (r["dh"], r["dscale"], r["dshift"], r["dgain"],
                                                             r["dy_prev"], r["dgate_prev"])
    dhn1, gdown0, gup0, dcw0, dcb0 = ffn_bwd(0, df0, u0, z0, hn1, "ffn0")
    r = _mod_bwd(h1, dhn1, vec(small["norm_ffn"][0]), m[0][4], t, "mod0_ffn_bwd", dh_out=dh2, y_prev=y0,
                 gate_prev=m[0][2])
    dh1, dmods[0][4], dmods[0][3], dnf0, dy0, dmods[0][2] = (r["dh"], r["dscale"], r["dshift"], r["dgain"],
                                                             r["dy_prev"], r["dgate_prev"])
    dyb0 = _mm(dy0, wts["gla_out"], "nt", F32, "gla_out_dx", 1024, 1024)
    gb["gla_out"] = _mm(yb0, dy0, "tn", F32, "gla_out_dw", 512, 1024)
    do, dgate, dhg = _gla_post_bwd(o2, p_all, hg, dyb0, n)
    dq, dk, dv, dla = _gla_scan_bwd(p_all, la_all, do, ss, nb, t, tc)
    dp, dw2, db2 = _gla_assemble(p_all, w2, b2, dq, dk, dv, dla, dgate, n)
    dhn_all = _mm(dp, w_gin, "nn", F32, "gla_in_dx", 768, 512)
    gb["gla_in_t"] = _mm(dp, hn_all, "tn", F32, "gla_in_dw", 640, 1024)[:GLA_IN]
    r = _mod_bwd(xf, dhn_all, vec(small["norm_mix"][0]), m[0][1], t, "mod0_main_bwd", dh_out=dh1)
    grad_x, dmods[0][1], dmods[0][0], dnm0 = r["dh"], r["dscale"], r["dshift"], r["dgain"]
    rc = _mod_bwd(cf, dhn_all, vec(small["norm_mix"][0]), mc_b[1], tc, "mod0_ctx_bwd", dhn_row0=n, need_dh=False)
    dmc = jnp.stack([jnp.sum(rc["dshift"], axis=0).reshape(D), jnp.sum(rc["dscale"], axis=0).reshape(D)])
    dnm0 = dnm0 + rc["dgain"]

    gb["up_t"] = [gup0, gup1]
    gb["down"] = [gdown0, gdown1]
    gs["norm_mix"] = jnp.concatenate([dnm0, dnm1], axis=0)
    gs["norm_ffn"] = jnp.concatenate([dnf0, dnf1], axis=0)
    gs["final_norm"] = dfinal.reshape(D)
    gs["gla_w_a2"] = jnp.stack([dw2[0:RANK, 0:KEY], dw2[RANK:2 * RANK, KEY:]])
    gs["gla_b_a"] = db2.reshape(2, KEY)
    gs["gla_head_norm"] = dhg.reshape(DV)
    gs["sc_conv_w"] = dscw
    gs["ffn_conv_w"] = jnp.stack([dcw0, dcw1])
    gs["ffn_conv_b"] = jnp.stack([dcb0, dcb1])
    dmods_arr = jnp.stack([jnp.stack([dmods[l][k].reshape(nb, D) for k in range(N_MOD)], axis=1) for l in range(2)])
    return loss, grad_x.reshape(nb, t, D), gb, gs, dmods_arr, dmc


def _pack(arrs):
    parts, meta, off = [], [], 0
    for a in arrs:
        r = a.size // 128
        rp = -(-r // 8) * 8
        a2 = a.reshape(r, 128).astype(F32)
        if rp != r:
            a2 = jnp.pad(a2, ((0, rp - r), (0, 0)))
        parts.append(a2)
        meta.append((off, r, a.shape))
        off += rp
    return jnp.concatenate(parts, axis=0), meta


def _unpack(buf, meta, lead=()):
    return [buf[..., off:off + r, :].reshape(*lead, *shape) for off, r, shape in meta]


_BIG = ("gla_in_t", "sc_in_t", "up_t0", "up_t1", "gla_out", "sc_out", "down0", "down1")
_BIG_ROWS = {"gla_in_t": GLA_IN // N_CHIPS, "sc_in_t": 3 * D // N_CHIPS, "up_t0": 2 * HID // N_CHIPS,
             "up_t1": 2 * HID // N_CHIPS, "gla_out": D // N_CHIPS, "sc_out": D // N_CHIPS,
             "down0": HID // N_CHIPS, "down1": HID // N_CHIPS}
_BIG_TOTAL = sum(_BIG_ROWS.values())
_BIG_PAD = -(-_BIG_TOTAL // 32) * 32


def _big_offsets():
    off, out = 0, {}
    for k in _BIG:
        out[k] = off
        off += _BIG_ROWS[k]
    return out


def kernel(x, c, ctx, c_ctx, ada_w, ada_b, norm_mix, norm_ffn, gla_w_in, gla_w_a2, gla_b_a, gla_head_norm, gla_w_out, sc_w_in, sc_conv_w, sc_w_out, ffn_w_up, ffn_conv_w, ffn_conv_b, ffn_w_down, final_norm, loss_target, m_c_ctx, m_ada_w, m_ada_b, m_norm_mix, m_norm_ffn, m_gla_w_in, m_gla_w_a2, m_gla_b_a, m_gla_head_norm, m_gla_w_out, m_sc_w_in, m_sc_conv_w, m_sc_w_out, m_ffn_w_up, m_ffn_conv_w, m_ffn_conv_b, m_ffn_w_down, m_final_norm, v_c_ctx, v_ada_w, v_ada_b, v_norm_mix, v_norm_ffn, v_gla_w_in, v_gla_w_a2, v_gla_b_a, v_gla_head_norm, v_gla_w_out, v_sc_w_in, v_sc_conv_w, v_sc_w_out, v_ffn_w_up, v_ffn_conv_w, v_ffn_conv_b, v_ffn_w_down, v_final_norm):
    ix, iy, ic = _place()
    chip = 2 * ix + iy
    dev = 2 * chip + ic
    place = jnp.stack([ic, chip]).astype(jnp.int32)
    nb = x.shape[0]
    offs = _big_offsets()

    buf, meta = _pack([c, ffn_conv_w, sc_conv_w, gla_w_a2, gla_b_a])
    got = _allgather_small(buf, "gather_small_in").reshape(N_DEV, buf.shape[0], 128)
    c_all, fcw, scw, wa2, ba = _unpack(got, meta, (N_DEV,))
    c_all = c_all.reshape(N_DEV * nb, D)
    per_chip = lambda a: a[0::2]
    ffn_conv_w_full = jnp.moveaxis(per_chip(fcw), 0, 2).reshape(2, 3, 2 * HID)
    sc_conv_w_full = jnp.moveaxis(per_chip(scw)[:, 0], 0, 1).reshape(3, D)
    gla_w_a2_full = jnp.moveaxis(per_chip(wa2)[:, 0], 0, 2).reshape(2, RANK, KEY)
    gla_b_a_full = jnp.moveaxis(per_chip(ba)[:, 0], 0, 1).reshape(2, KEY)

    own = {"gla_in_t": gla_w_in[0].T, "sc_in_t": sc_w_in[0].T, "up_t0": ffn_w_up[0].T, "up_t1": ffn_w_up[1].T,
           "gla_out": gla_w_out[0], "sc_out": sc_w_out[0], "down0": ffn_w_down[0], "down1": ffn_w_down[1]}
    own_packed = jnp.concatenate([own[k].astype(BF16) for k in _BIG]
                                 + [jnp.zeros((_BIG_PAD - _BIG_TOTAL, D), BF16)], axis=0)
    wg = _weights_allgather(own_packed)
    full = {k: wg[:, offs[k]:offs[k] + _BIG_ROWS[k], :].reshape(N_CHIPS * _BIG_ROWS[k], D) for k in _BIG}
    wts = {"gla_in_t": full["gla_in_t"], "sc_in_t": full["sc_in_t"], "up_t": [full["up_t0"], full["up_t1"]],
           "gla_out": full["gla_out"], "sc_out": full["sc_out"], "down": [full["down0"], full["down1"]]}

    cvec = jnp.concatenate([c_all, c_ctx.reshape(1, D), jnp.zeros((ADA_ROWS - N_DEV * nb - 1, D), F32)], axis=0)
    ada_b_sh = lax.dynamic_slice_in_dim(ada_b, chip * ADA_SH, ADA_SH, axis=1).reshape(2, 1, ADA_SH)
    mod_sh = _ada_fwd(cvec, ada_w, ada_b_sh)
    got = _allgather_small(mod_sh.reshape(2 * ADA_ROWS, ADA_SH), "gather_mod")
    mod_full = jnp.moveaxis(per_chip(got.reshape(N_DEV, 2, ADA_ROWS, ADA_SH)), 0, 2).reshape(2, ADA_ROWS, N_MOD * D)
    mods = lax.dynamic_slice_in_dim(mod_full, dev * nb, nb, axis=1).reshape(2, nb, N_MOD, D)
    mc = mod_full[0, N_DEV * nb, :2 * D].reshape(2, D)

    small = {"norm_mix": norm_mix, "norm_ffn": norm_ffn, "final_norm": final_norm, "gla_w_a2": gla_w_a2_full,
             "gla_b_a": gla_b_a_full, "gla_head_norm": gla_head_norm[0], "sc_conv_w": sc_conv_w_full,
             "ffn_conv_w": ffn_conv_w_full, "ffn_conv_b": ffn_conv_b}
    loss_p, grad_x, gb, gs, dmods, dmc = _local_step(x, ctx, loss_target, mods, mc, wts, small)

    sum_names = ["norm_mix", "norm_ffn", "final_norm", "gla_w_a2", "gla_b_a", "gla_head_norm", "sc_conv_w",
                 "ffn_conv_w", "ffn_conv_b"]
    buf, meta = _pack([jnp.broadcast_to(loss_p, (8, 128))] + [gs[k] for k in sum_names] + [dmc, dmods])
    n_sum = meta[-1][0]
    got = _allgather_small(buf, "gather_small_grads").reshape(N_DEV, buf.shape[0], 128)
    summed = _sum_slots(got[:, :n_sum], "sum_small_grads")
    parts = _unpack(summed, meta[:-1])
    loss = parts[0][0, 0]
    g_small = dict(zip(sum_names, parts[1:-1]))
    dmc_tot = parts[-1]
    dmods_all = jnp.moveaxis(_unpack(got, meta[-1:], (N_DEV,))[0], 0, 1).reshape(2, N_DEV * nb, N_MOD * D)

    ctx_row = jnp.stack([jnp.concatenate([dmc_tot.reshape(2 * D), jnp.zeros(((N_MOD - 2) * D,), F32)]),
                         jnp.zeros((N_MOD * D,), F32)]).reshape(2, 1, N_MOD * D)
    dmod_ext = jnp.concatenate([dmods_all, ctx_row, jnp.zeros((2, ADA_ROWS - N_DEV * nb - 1, N_MOD * D), F32)], axis=1)
    g_ada_b = _sum_slots(jnp.moveaxis(dmod_ext, 1, 0).reshape(ADA_ROWS, 2 * N_MOD * D // 128, 128),
                         "sum_ada_b").reshape(2, N_MOD * D)
    dmod_sh = lax.dynamic_slice_in_dim(dmod_ext, chip * ADA_SH, ADA_SH, axis=2)
    g_ada_w, dcv = _ada_bwd(cvec, ada_w, dmod_sh)
    dscc_part = (dcv[0, N_DEV * nb] + dcv[1, N_DEV * nb]).reshape(8, 128)
    got = _allgather_small(dscc_part, "gather_dscc").reshape(N_DEV, 8, 128)
    g_c_ctx = _cctx_grad(per_chip(got), c_ctx.reshape(8, 128)).reshape(D)

    gbig = {"gla_in_t": gb["gla_in_t"], "sc_in_t": gb["sc_in_t"], "up_t0": gb["up_t"][0], "up_t1": gb["up_t"][1],
            "gla_out": gb["gla_out"], "sc_out": gb["sc_out"], "down0": gb["down"][0], "down1": gb["down"][1]}
    g_packed = jnp.concatenate([gbig[k].reshape(N_CHIPS, _BIG_ROWS[k], D) for k in _BIG]
                               + [jnp.zeros((N_CHIPS, _BIG_PAD - _BIG_TOTAL, D), F32)], axis=1)
    from_sibling = _rs_pair_exchange(g_packed)
    p16, p32 = _rs_chip_sum(place, g_packed, from_sibling)
    landed = _rs_scatter(p16)
    half = _rs_final_sum(place, landed, p32)
    g_shard = _rs_pair_gather(half)
    seg = {k: g_shard[offs[k]:offs[k] + _BIG_ROWS[k]] for k in _BIG}

    sl_chip = lambda a, axis, width: lax.dynamic_slice_in_dim(a, chip * width, width, axis=axis)
    grads = {
        "c_ctx": g_c_ctx, "ada_w": g_ada_w, "ada_b": g_ada_b, "norm_mix": g_small["norm_mix"],
        "norm_ffn": g_small["norm_ffn"],
        "gla_w_in": seg["gla_in_t"].T[None], "gla_w_a2": sl_chip(g_small["gla_w_a2"], 2, KEY // N_CHIPS)[None],
        "gla_b_a": sl_chip(g_small["gla_b_a"], 1, KEY // N_CHIPS)[None],
        "gla_head_norm": g_small["gla_head_norm"][None], "gla_w_out": seg["gla_out"][None],
        "sc_w_in": seg["sc_in_t"].T[None], "sc_conv_w": sl_chip(g_small["sc_conv_w"], 1, D // N_CHIPS)[None],
        "sc_w_out": seg["sc_out"][None], "ffn_w_up": jnp.stack([seg["up_t0"].T, seg["up_t1"].T]),
        "ffn_conv_w": sl_chip(g_small["ffn_conv_w"], 2, 2 * HID // N_CHIPS), "ffn_conv_b": g_small["ffn_conv_b"],
        "ffn_w_down": jnp.stack([seg["down0"], seg["down1"]]), "final_norm": g_small["final_norm"],
    }
    weights = {"c_ctx": c_ctx, "ada_w": ada_w, "ada_b": ada_b, "norm_mix": norm_mix, "norm_ffn": norm_ffn,
               "gla_w_in": gla_w_in, "gla_w_a2": gla_w_a2, "gla_b_a": gla_b_a, "gla_head_norm": gla_head_norm,
               "gla_w_out": gla_w_out, "sc_w_in": sc_w_in, "sc_conv_w": sc_conv_w, "sc_w_out": sc_w_out,
               "ffn_w_up": ffn_w_up, "ffn_conv_w": ffn_conv_w, "ffn_conv_b": ffn_conv_b, "ffn_w_down": ffn_w_down,
               "final_norm": final_norm}
    mom1 = {"c_ctx": m_c_ctx, "ada_w": m_ada_w, "ada_b": m_ada_b, "norm_mix": m_norm_mix, "norm_ffn": m_norm_ffn,
            "gla_w_in": m_gla_w_in, "gla_w_a2": m_gla_w_a2, "gla_b_a": m_gla_b_a, "gla_head_norm": m_gla_head_norm,
            "gla_w_out": m_gla_w_out, "sc_w_in": m_sc_w_in, "sc_conv_w": m_sc_conv_w, "sc_w_out": m_sc_w_out,
            "ffn_w_up": m_ffn_w_up, "ffn_conv_w": m_ffn_conv_w, "ffn_conv_b": m_ffn_conv_b,
            "ffn_w_down": m_ffn_w_down, "final_norm": m_final_norm}
    mom2 = {"c_ctx": v_c_ctx, "ada_w": v_ada_w, "ada_b": v_ada_b, "norm_mix": v_norm_mix, "norm_ffn": v_norm_ffn,
            "gla_w_in": v_gla_w_in, "gla_w_a2": v_gla_w_a2, "gla_b_a": v_gla_b_a, "gla_head_norm": v_gla_head_norm,
            "gla_w_out": v_gla_w_out, "sc_w_in": v_sc_w_in, "sc_conv_w": v_sc_conv_w, "sc_w_out": v_sc_w_out,
            "ffn_w_up": v_ffn_w_up, "ffn_conv_w": v_ffn_conv_w, "ffn_conv_b": v_ffn_conv_b,
            "ffn_w_down": v_ffn_w_down, "final_norm": v_final_norm}
    names = list(weights)
    grads = {k: grads[k].reshape(weights[k].shape) for k in names}

    big_names = ["ada_w", "gla_w_in", "gla_w_out", "sc_w_in", "sc_w_out", "ffn_w_up", "ffn_w_down"]
    small_names = [k for k in names if k not in big_names]
    delta, new_m, new_v = {}, {}, {}
    for k in big_names:
        shp = weights[k].shape
        as2d = lambda a: a.reshape(-1, shp[-1])
        d_, m_, v_ = _adamw(as2d(weights[k]), as2d(grads[k]), as2d(mom1[k]), as2d(mom2[k]), "adamw_" + k)
        delta[k], new_m[k], new_v[k] = d_.reshape(shp), m_.reshape(shp), v_.reshape(shp)
    packed = [_pack([src[k] for k in small_names]) for src in (weights, grads, mom1, mom2)]
    meta = packed[0][1]
    outs = _adamw(packed[0][0], packed[1][0], packed[2][0], packed[3][0], "adamw_small")
    for dst, o in zip((delta, new_m, new_v), outs):
        for k, a in zip(small_names, _unpack(o, meta)):
            dst[k] = a

    return (loss, grad_x, *[grads[k] for k in names], *[delta[k] for k in names], *[new_m[k] for k in names],
            *[new_v[k] for k in names])
```

```python
import functools

import jax
import jax.numpy as jnp
from jax import lax
from jax.experimental import pallas as pl
from jax.experimental.pallas import tpu as pltpu

F32 = jnp.float32
BF16 = jnp.bfloat16
MESH = pl.DeviceIdType.MESH

EPS = 1e-6
D = 1024
N_MOD = 6
HEADS = 4
DK = 128
DV = 256
KEY = HEADS * DK
RANK = 16
TAU = 16.0
CH = 64
GRID_W = 64
HID = 2560
GLA_IN = 2 * KEY + 2 * D + 2 * RANK
GLA_IN_PAD = 3200
Q_SCALE = DK ** -0.5
N_CHIPS = 4
N_DEV = 8

ADAM_LR = 0.001
ADAM_B1 = 0.9
ADAM_B2 = 0.999
ADAM_EPS = 1e-08
ADAM_WD = 0.01
ADAM_STEP = 10

VMEM_LIMIT = 56 * 1024 * 1024


def _params(sem):
    return pltpu.CompilerParams(dimension_semantics=sem, vmem_limit_bytes=VMEM_LIMIT)


def _tile(n, pref, mult=8):
    if n <= pref:
        return n
    for t in range(pref - pref % mult, 0, -mult):
        if n % t == 0:
            return t
    raise ValueError((n, pref, mult))


_NN = (((1,), (0,)), ((), ()))
_NT = (((1,), (1,)), ((), ()))
_TN = (((0,), (0,)), ((), ()))


def _dot(a, b, dims=_NN):
    return lax.dot_general(a.astype(BF16), b.astype(BF16), dims, preferred_element_type=F32)


def _dot_hi(a, b, dims=_NN):
    return lax.dot_general(a, b, dims, precision=lax.Precision.HIGHEST, preferred_element_type=F32)


def _sigmoid(x):
    return 1.0 / (1.0 + jnp.exp(-x))


def _rowsum(x):
    return jnp.sum(x, axis=0, keepdims=True)


def _mm(a, b, form, out_dtype, name, tm, tn):
    if form == "tn":
        K, M = a.shape
    else:
        M, K = a.shape
    N = b.shape[0] if form == "nt" else b.shape[1]
    tm = _tile(M, tm, 128)
    tn = _tile(N, tn, 128)
    dims = {"nn": _NN, "nt": _NT, "tn": _TN}[form]

    def body(a_ref, b_ref, o_ref):
        o_ref[...] = _dot(a_ref[...], b_ref[...], dims).astype(o_ref.dtype)

    if form == "tn":
        a_spec = pl.BlockSpec((K, tm), lambda i, j: (0, i))
    else:
        a_spec = pl.BlockSpec((tm, K), lambda i, j: (i, 0))
    if form == "nt":
        b_spec = pl.BlockSpec((tn, K), lambda i, j: (j, 0))
    else:
        b_spec = pl.BlockSpec((K, tn), lambda i, j: (0, j))
    return pl.pallas_call(
        body,
        name=name,
        grid=(M // tm, N // tn),
        in_specs=[a_spec, b_spec],
        out_specs=pl.BlockSpec((tm, tn), lambda i, j: (i, j)),
        out_shape=jax.ShapeDtypeStruct((M, N), out_dtype),
        compiler_params=_params(("parallel", "parallel")),
    )(a, b)


def _mod_fwd(h, gain, shift, scale, tpb_rows, name, y=None, gate=None):
    n = h.shape[0]
    tt = _tile(tpb_rows, 256)
    tpb = tpb_rows // tt
    has_res = y is not None

    def body(*refs):
        if has_res:
            h_ref, y_ref, gate_ref, gain_ref, sh_ref, sc_ref, hout_ref, hn_ref = refs
            hv = h_ref[...] + gate_ref[0] * y_ref[...]
            hout_ref[...] = hv
        else:
            h_ref, gain_ref, sh_ref, sc_ref, hn_ref = refs
            hv = h_ref[...]
        r = lax.rsqrt(jnp.mean(hv * hv, axis=-1, keepdims=True) + EPS)
        hn = (hv * r) * gain_ref[...] * (1.0 + sc_ref[0]) + sh_ref[0]
        hn_ref[...] = hn.astype(BF16)

    row = pl.BlockSpec((tt, D), lambda i: (i, 0))
    per_b = pl.BlockSpec((1, 1, D), lambda i: (i // tpb, 0, 0))
    vec = pl.BlockSpec((1, D), lambda i: (0, 0))
    if has_res:
        in_specs = [row, row, per_b, vec, per_b, per_b]
        args = (h, y, gate, gain, shift, scale)
        out_specs = [row, row]
        out_shape = [jax.ShapeDtypeStruct((n, D), F32), jax.ShapeDtypeStruct((n, D), BF16)]
    else:
        in_specs = [row, vec, per_b, per_b]
        args = (h, gain, shift, scale)
        out_specs = row
        out_shape = jax.ShapeDtypeStruct((n, D), BF16)
    return pl.pallas_call(
        body, name=name, grid=(n // tt,), in_specs=in_specs, out_specs=out_specs, out_shape=out_shape,
        compiler_params=_params(("parallel",)),
    )(*args)


def _mod_bwd(h_in, dhn, gain, scale, tpb_rows, name, dhn_row0=0, dh_out=None, y_prev=None, gate_prev=None,
             need_dh=True):
    n = h_in.shape[0]
    nb = n // tpb_rows
    tt = _tile(tpb_rows, 256)
    tpb = tpb_rows // tt
    off = dhn_row0 // tt
    assert dhn_row0 % tt == 0
    has_out = dh_out is not None
    has_prev = y_prev is not None

    def body(*refs):
        it = iter(refs)
        h_ref, dhn_ref, gain_ref, sc_ref = next(it), next(it), next(it), next(it)
        dho_ref = next(it) if has_out else None
        yp_ref, gp_ref = (next(it), next(it)) if has_prev else (None, None)
        dh_ref = next(it) if need_dh else None
        dsc_ref, dsh_ref, dgain_ref = next(it), next(it), next(it)
        dyp_ref, dgp_ref = (next(it), next(it)) if has_prev else (None, None)
        i = pl.program_id(0)

        @pl.when(i == 0)
        def _():
            dgain_ref[...] = jnp.zeros_like(dgain_ref)

        @pl.when(i % tpb == 0)
        def _():
            dsc_ref[...] = jnp.zeros_like(dsc_ref)
            dsh_ref[...] = jnp.zeros_like(dsh_ref)
            if has_prev:
                dgp_ref[...] = jnp.zeros_like(dgp_ref)

        hv = h_ref[...]
        r = lax.rsqrt(jnp.mean(hv * hv, axis=-1, keepdims=True) + EPS)
        y = hv * r
        gain_v = gain_ref[...]
        g = dhn_ref[...].astype(F32)
        dsh_ref[0] += _rowsum(g)
        dsc_ref[0] += _rowsum(g * (y * gain_v))
        drn = g * (1.0 + sc_ref[0])
        dgain_ref[...] += _rowsum(drn * y)
        if need_dh:
            dy = drn * gain_v
            dh = r * (dy - y * jnp.mean(dy * y, axis=-1, keepdims=True))
            if has_out:
                dh = dh + dho_ref[...]
            dh_ref[...] = dh
            if has_prev:
                dyp_ref[...] = (dh * gp_ref[0]).astype(BF16)
                dgp_ref[0] += _rowsum(dh * yp_ref[...])

    row = pl.BlockSpec((tt, D), lambda i: (i, 0))
    row_off = pl.BlockSpec((tt, D), lambda i: (i + off, 0))
    per_b = pl.BlockSpec((1, 1, D), lambda i: (i // tpb, 0, 0))
    vec = pl.BlockSpec((1, D), lambda i: (0, 0))
    in_specs = [row, row_off, vec, per_b]
    args = [h_in, dhn, gain, scale]
    if has_out:
        in_specs.append(row)
        args.append(dh_out)
    if has_prev:
        in_specs += [row, per_b]
        args += [y_prev, gate_prev]
    out_specs, out_shape, names = [], [], []
    if need_dh:
        out_specs.append(row)
        out_shape.append(jax.ShapeDtypeStruct((n, D), F32))
        names.append("dh")
    for nm in ("dscale", "dshift"):
        out_specs.append(per_b)
        out_shape.append(jax.ShapeDtypeStruct((nb, 1, D), F32))
        names.append(nm)
    out_specs.append(vec)
    out_shape.append(jax.ShapeDtypeStruct((1, D), F32))
    names.append("dgain")
    if has_prev:
        out_specs += [row, per_b]
        out_shape += [jax.ShapeDtypeStruct((n, D), BF16), jax.ShapeDtypeStruct((nb, 1, D), F32)]
        names += ["dy_prev", "dgate_prev"]
    outs = pl.pallas_call(
        body, name=name, grid=(n // tt,), in_specs=in_specs, out_specs=out_specs, out_shape=out_shape,
        compiler_params=_params(("arbitrary",)),
    )(*args)
    return dict(zip(names, outs))


def _final(h, f, gate, gain, tgt, tpb_rows):
    n = h.shape[0]
    nb = n // tpb_rows
    tt = _tile(tpb_rows, 256)
    tpb = tpb_rows // tt

    def body(h_ref, f_ref, gate_ref, gain_ref, tgt_ref, loss_ref, dh_ref, df_ref, dgate_ref, dgain_ref):
        i = pl.program_id(0)

        @pl.when(i == 0)
        def _():
            loss_ref[...] = jnp.zeros_like(loss_ref)
            dgain_ref[...] = jnp.zeros_like(dgain_ref)

        @pl.when(i % tpb == 0)
        def _():
            dgate_ref[...] = jnp.zeros_like(dgate_ref)

        fv = f_ref[...]
        gate_v = gate_ref[0]
        hv = h_ref[...] + gate_v * fv
        r = lax.rsqrt(jnp.mean(hv * hv, axis=-1, keepdims=True) + EPS)
        y = hv * r
        gain_v = gain_ref[...]
        e = y * gain_v - tgt_ref[...]
        s = jnp.sum(_rowsum(e * e), axis=1, keepdims=True) * (0.5 / D)
        loss_ref[...] += jnp.broadcast_to(s, loss_ref.shape)
        dout = e * (1.0 / D)
        dgain_ref[...] += _rowsum(dout * y)
        dy = dout * gain_v
        dh = r * (dy - y * jnp.mean(dy * y, axis=-1, keepdims=True))
        dh_ref[...] = dh
        df_ref[...] = (dh * gate_v).astype(BF16)
        dgate_ref[0] += _rowsum(dh * fv)

    row = pl.BlockSpec((tt, D), lambda i: (i, 0))
    per_b = pl.BlockSpec((1, 1, D), lambda i: (i // tpb, 0, 0))
    vec = pl.BlockSpec((1, D), lambda i: (0, 0))
    return pl.pallas_call(
        body, name="final_loss", grid=(n // tt,),
        in_specs=[row, row, per_b, vec, row],
        out_specs=[pl.BlockSpec((1, 128), lambda i: (0, 0)), row, row, per_b, vec],
        out_shape=[jax.ShapeDtypeStruct((1, 128), F32), jax.ShapeDtypeStruct((n, D), F32),
                   jax.ShapeDtypeStruct((n, D), BF16), jax.ShapeDtypeStruct((nb, 1, D), F32),
                   jax.ShapeDtypeStruct((1, D), F32)],
        compiler_params=_params(("arbitrary",)),
    )(h, f, gate, gain, tgt)


def _shift_dn(x, s):
    return jnp.concatenate([jnp.zeros((s, x.shape[1]), x.dtype), x[: x.shape[0] - s]], axis=0)


def _shift_up(x, s):
    return jnp.concatenate([x[s:], jnp.zeros((s, x.shape[1]), x.dtype)], axis=0)


def _row_dn1(x):
    t = lax.broadcasted_iota(jnp.int32, x.shape, 0)
    return jnp.where(t % GRID_W == 0, 0.0, pltpu.roll(x, 1, 0))


def _row_up1(x):
    t = lax.broadcasted_iota(jnp.int32, x.shape, 0)
    return jnp.where(t % GRID_W == GRID_W - 1, 0.0, pltpu.roll(x, x.shape[0] - 1, 0))


def _silu(x):
    return x * _sigmoid(x)


def _dsilu(x):
    s = _sigmoid(x)
    return s * (1.0 + x * (1.0 - s))


def _ffn_mid_fwd(u0, cw, cb, nb, t, name):
    nc = HID // 128

    def body(u_ref, w_ref, b_ref, z_ref):
        x = u_ref[...]
        u = (_shift_dn(x, GRID_W) * w_ref[0:1, :] + x * w_ref[1:2, :] + _shift_up(x, GRID_W) * w_ref[2:3, :]
             + b_ref[...])
        z_ref[...] = (u[:, :128] * _silu(u[:, 128:])).astype(BF16)

    return pl.pallas_call(
        body, name=name, grid=(nc, nb),
        in_specs=[pl.BlockSpec((t, 256), lambda j, b: (b, j)), pl.BlockSpec((3, 256), lambda j, b: (0, j)),
                  pl.BlockSpec((1, 256), lambda j, b: (0, j))],
        out_specs=pl.BlockSpec((t, 128), lambda j, b: (b, j)),
        out_shape=jax.ShapeDtypeStruct((nb * t, HID), BF16),
        compiler_params=_params(("parallel", "parallel")),
    )(u0, cw, cb)


def _ffn_mid_bwd(u0, cw, cb, dz, nb, t, name):
    nc = HID // 128

    def body(u_ref, w_ref, b_ref, dz_ref, du_ref, dw_ref, db_ref):
        b = pl.program_id(1)

        @pl.when(b == 0)
        def _():
            dw_ref[...] = jnp.zeros_like(dw_ref)
            db_ref[...] = jnp.zeros_like(db_ref)

        x = u_ref[...]
        w0, w1, w2 = w_ref[0:1, :], w_ref[1:2, :], w_ref[2:3, :]
        xd = _shift_dn(x, GRID_W)
        xu = _shift_up(x, GRID_W)
        u = xd * w0 + x * w1 + xu * w2 + b_ref[...]
        a = u[:, :128]
        gt = u[:, 128:]
        dzv = dz_ref[...]
        du = jnp.concatenate([dzv * _silu(gt), dzv * a * _dsilu(gt)], axis=1)
        db_ref[...] += _rowsum(du)
        dw_ref[0:1, :] += _rowsum(du * xd)
        dw_ref[1:2, :] += _rowsum(du * x)
        dw_ref[2:3, :] += _rowsum(du * xu)
        dx = _shift_up(du, GRID_W) * w0 + du * w1 + _shift_dn(du, GRID_W) * w2
        du_ref[...] = dx.astype(BF16)

    return pl.pallas_call(
        body, name=name, grid=(nc, nb),
        in_specs=[pl.BlockSpec((t, 256), lambda j, b: (b, j)), pl.BlockSpec((3, 256), lambda j, b: (0, j)),
                  pl.BlockSpec((1, 256), lambda j, b: (0, j)), pl.BlockSpec((t, 128), lambda j, b: (b, j))],
        out_specs=[pl.BlockSpec((t, 256), lambda j, b: (b, j)), pl.BlockSpec((3, 256), lambda j, b: (0, j)),
                   pl.BlockSpec((1, 256), lambda j, b: (0, j))],
        out_shape=[jax.ShapeDtypeStruct((nb * t, 2 * HID), BF16), jax.ShapeDtypeStruct((3, 2 * HID), F32),
                   jax.ShapeDtypeStruct((1, 2 * HID), F32)],
        compiler_params=_params(("parallel", "arbitrary")),
    )(u0, cw, cb, dz)


def _sc_mid_fwd(p, cw, nb, t):
    nc = D // 128

    def body(p_ref, w_ref, y_ref):
        x = p_ref[...]
        cv = x[:, 128:256] * x[:, 256:]
        cc = _row_dn1(cv) * w_ref[0:1, :] + cv * w_ref[1:2, :] + _row_up1(cv) * w_ref[2:3, :]
        y_ref[...] = (x[:, :128] * cc).astype(BF16)

    return pl.pallas_call(
        body, name="sc_mid_fwd", grid=(nc, nb),
        in_specs=[pl.BlockSpec((t, 384), lambda j, b: (b, j)), pl.BlockSpec((3, 128), lambda j, b: (0, j))],
        out_specs=pl.BlockSpec((t, 128), lambda j, b: (b, j)),
        out_shape=jax.ShapeDtypeStruct((nb * t, D), BF16),
        compiler_params=_params(("parallel", "parallel")),
    )(p, cw)


def _sc_mid_bwd(p, cw, dyb, nb, t):
    nc = D // 128

    def body(p_ref, w_ref, dy_ref, dp_ref, dw_ref):
        b = pl.program_id(1)

        @pl.when(b == 0)
        def _():
            dw_ref[...] = jnp.zeros_like(dw_ref)

        x = p_ref[...]
        w0, w1, w2 = w_ref[0:1, :], w_ref[1:2, :], w_ref[2:3, :]
        bg, cg, v = x[:, :128], x[:, 128:256], x[:, 256:]
        cv = cg * v
        cvd = _row_dn1(cv)
        cvu = _row_up1(cv)
        cc = cvd * w0 + cv * w1 + cvu * w2
        dy = dy_ref[...]
        dcc = dy * bg
        dw_ref[0:1, :] += _rowsum(dcc * cvd)
        dw_ref[1:2, :] += _rowsum(dcc * cv)
        dw_ref[2:3, :] += _rowsum(dcc * cvu)
        dcv = _row_up1(dcc) * w0 + dcc * w1 + _row_dn1(dcc) * w2
        dp_ref[...] = jnp.concatenate([dy * cc, dcv * v, dcv * cg], axis=1).astype(BF16)

    return pl.pallas_call(
        body, name="sc_mid_bwd", grid=(nc, nb),
        in_specs=[pl.BlockSpec((t, 384), lambda j, b: (b, j)), pl.BlockSpec((3, 128), lambda j, b: (0, j)),
                  pl.BlockSpec((t, 128), lambda j, b: (b, j))],
        out_specs=[pl.BlockSpec((t, 384), lambda j, b: (b, j)), pl.BlockSpec((3, 128), lambda j, b: (0, j))],
        out_shape=[jax.ShapeDtypeStruct((nb * t, 3 * D), BF16), jax.ShapeDtypeStruct((3, D), F32)],
        compiler_params=_params(("parallel", "arbitrary")),
    )(p, cw, dyb)


def _gla_decay_fwd(p_all, w2, b2):
    n = p_all.shape[0]
    tt = _tile(n, 512)

    def body(a_ref, w_ref, b_ref, la_ref):
        z = _dot(a_ref[...], w_ref[...]) + b_ref[...]
        la_ref[...] = (jnp.minimum(z, 0.0) - jnp.log(1.0 + jnp.exp(-jnp.abs(z)))) * (1.0 / TAU)

    return pl.pallas_call(
        body, name="gla_decay_fwd", grid=(n // tt,),
        in_specs=[pl.BlockSpec((tt, 128), lambda i: (i, (2 * KEY + 2 * D) // 128)),
                  pl.BlockSpec((128, 2 * KEY), lambda i: (0, 0)), pl.BlockSpec((1, 2 * KEY), lambda i: (0, 0))],
        out_specs=pl.BlockSpec((tt, 2 * KEY), lambda i: (i, 0)),
        out_shape=jax.ShapeDtypeStruct((n, 2 * KEY), F32),
        compiler_params=_params(("parallel",)),
    )(p_all, w2, b2)


def _gla_blocks(nb, nm, ncx):
    def main_idx(d, i):
        return jnp.clip(jnp.where(d == 0, i - ncx, nm - 1 - (i - ncx)), 0, nm - 1)

    def rowblk(d, b, i):
        cidx = jnp.where(d == 0, i, ncx - 1 - i)
        return jnp.where(i < ncx, nb * nm + b * ncx + cidx, b * nm + main_idx(d, i))

    def mainblk(d, b, i):
        return b * nm + main_idx(d, i)

    return rowblk, mainblk


def _gla_mask(d):
    row = lax.broadcasted_iota(jnp.int32, (CH, CH), 0)
    col = lax.broadcasted_iota(jnp.int32, (CH, CH), 1)
    diff = row - col
    mask = jnp.where(d == 0, diff, -diff) >= 0
    return mask, jnp.where(mask, 1.0, 0.0).astype(F32)


def _gla_chunk(mf, q, k, g):
    bc = _dot_hi(mf, g)
    bl = _rowsum(g)
    eq = jnp.exp(bc)
    ek = jnp.exp(-bc)
    ed = jnp.exp(bl - bc)
    return bl, eq, ek, ed, q * Q_SCALE * eq, k * ek, k * ed


def _gla_scan_fwd(p_all, la_all, nb, t, tc):
    nm, ncx = t // CH, tc // CH
    nst = nm + ncx
    rowblk, mainblk = _gla_blocks(nb, nm, ncx)

    def body(q_ref, k_ref, v_ref, la_ref, o_ref, ss_ref, st_ref):
        d = pl.program_id(0)
        i = pl.program_id(2)

        @pl.when(i == 0)
        def _():
            st_ref[...] = jnp.zeros_like(st_ref)

        mask, mf = _gla_mask(d)
        for h in range(HEADS):
            ksl = slice(h * DK, (h + 1) * DK)
            vsl = slice(h * DV, (h + 1) * DV)
            bl, _, _, _, qs, ks, kd = _gla_chunk(mf, q_ref[:, ksl], k_ref[:, ksl], la_ref[:, ksl])
            st = st_ref[h]
            ss_ref[0, 0, 0, h] = st
            v = v_ref[:, vsl]
            att = jnp.where(mask, _dot(qs, ks, _NT), 0.0)
            o_ref[0, :, vsl] = _dot(qs, st, _NT) + _dot(att, v)
            st_ref[h] = st * jnp.exp(bl) + _dot(v, kd, _TN)

    return pl.pallas_call(
        body, name="gla_scan_fwd", grid=(2, nb, nst),
        in_specs=[
            pl.BlockSpec((CH, KEY), lambda d, b, i: (rowblk(d, b, i), 0)),
            pl.BlockSpec((CH, KEY), lambda d, b, i: (rowblk(d, b, i), 1)),
            pl.BlockSpec((CH, D), lambda d, b, i: (rowblk(d, b, i), 1)),
            pl.BlockSpec((CH, KEY), lambda d, b, i: (rowblk(d, b, i), d)),
        ],
        out_specs=[
            pl.BlockSpec((1, CH, D), lambda d, b, i: (d, mainblk(d, b, i), 0)),
            pl.BlockSpec((1, 1, 1, HEADS, DV, DK), lambda d, b, i: (d, b, i, 0, 0, 0)),
        ],
        out_shape=[jax.ShapeDtypeStruct((2, nb * t, D), F32),
                   jax.ShapeDtypeStruct((2, nb, nst, HEADS, DV, DK), F32)],
        scratch_shapes=[pltpu.VMEM((HEADS, DV, DK), F32)],
        compiler_params=_params(("parallel", "parallel", "arbitrary")),
    )(p_all, p_all, p_all, la_all)


def _gla_scan_bwd(p_all, la_all, do, ss, nb, t, tc):
    nm, ncx = t // CH, tc // CH
    nst = nm + ncx
    ntot = nb * (t + tc)
    rowblk, mainblk = _gla_blocks(nb, nm, ncx)

    def body(q_ref, k_ref, v_ref, la_ref, do_ref, ss_ref, dq_ref, dk_ref, dv_ref, dla_ref, dst_ref):
        d = pl.program_id(0)
        ip = pl.program_id(2)
        i = nst - 1 - ip

        @pl.when(ip == 0)
        def _():
            dst_ref[...] = jnp.zeros_like(dst_ref)

        mask, mf = _gla_mask(d)
        live = jnp.where(i >= ncx, 1.0, 0.0)
        for h in range(HEADS):
            ksl = slice(h * DK, (h + 1) * DK)
            vsl = slice(h * DV, (h + 1) * DV)
            bl, eq, ek, ed, qs, ks, kd = _gla_chunk(mf, q_ref[:, ksl], k_ref[:, ksl], la_ref[:, ksl])
            st = ss_ref[0, 0, 0, h]
            dst = dst_ref[h]
            v = v_ref[:, vsl]
            dov = do_ref[:, vsl] * live
            att = jnp.where(mask, _dot(qs, ks, _NT), 0.0)
            datt = jnp.where(mask, _dot(dov, v, _NT), 0.0)
            dqs = _dot(dov, st) + _dot(datt, ks)
            dks = _dot(datt, qs, _TN)
            dv_ref[0, :, vsl] = _dot(att, dov, _TN) + _dot(kd, dst, _NT)
            dkd = _dot(v, dst)
            e = jnp.exp(bl)
            dbl = e * _rowsum(st * dst) + _rowsum(dkd * kd)
            dst_ref[h] = _dot(dov, qs, _TN) + dst * e
            dq_ref[0, :, ksl] = dqs * eq * Q_SCALE
            dk_ref[0, :, ksl] = dks * ek + dkd * ed
            db = dqs * qs - dks * ks - dkd * kd
            dla_ref[:, ksl] = _dot_hi(mf, db, _TN) + dbl

    rev = lambda f: (lambda d, b, ip: f(d, b, nst - 1 - ip))
    return pl.pallas_call(
        body, name="gla_scan_bwd", grid=(2, nb, nst),
        in_specs=[
            pl.BlockSpec((CH, KEY), rev(lambda d, b, i: (rowblk(d, b, i), 0))),
            pl.BlockSpec((CH, KEY), rev(lambda d, b, i: (rowblk(d, b, i), 1))),
            pl.BlockSpec((CH, D), rev(lambda d, b, i: (rowblk(d, b, i), 1))),
            pl.BlockSpec((CH, KEY), rev(lambda d, b, i: (rowblk(d, b, i), d))),
            pl.BlockSpec((CH, D), rev(lambda d, b, i: (mainblk(d, b, i), 0))),
            pl.BlockSpec((1, 1, 1, HEADS, DV, DK), rev(lambda d, b, i: (d, b, i, 0, 0, 0))),
        ],
        out_specs=[
            pl.BlockSpec((1, CH, KEY), rev(lambda d, b, i: (d, rowblk(d, b, i), 0))),
            pl.BlockSpec((1, CH, KEY), rev(lambda d, b, i: (d, rowblk(d, b, i), 0))),
            pl.BlockSpec((1, CH, D), rev(lambda d, b, i: (d, rowblk(d, b, i), 0))),
            pl.BlockSpec((CH, KEY), rev(lambda d, b, i: (rowblk(d, b, i), d))),
        ],
        out_shape=[jax.ShapeDtypeStruct((2, ntot, KEY), F32), jax.ShapeDtypeStruct((2, ntot, KEY), F32),
                   jax.ShapeDtypeStruct((2, ntot, D), F32), jax.ShapeDtypeStruct((ntot, 2 * KEY), F32)],
        scratch_shapes=[pltpu.VMEM((HEADS, DV, DK), F32)],
        compiler_params=_params(("parallel", "parallel", "arbitrary")),
    )(p_all, p_all, p_all, la_all, do, ss)


def _gla_post_fwd(o2, p_all, head_gain, n):
    tt = _tile(n, 256)

    def body(o_ref, g_ref, hg_ref, y_ref):
        o = o_ref[0] + o_ref[1]
        gv = g_ref[...]
        hg = hg_ref[...]
        for h in range(HEADS):
            oh = o[:, h * DV:(h + 1) * DV]
            r = lax.rsqrt(jnp.mean(oh * oh, axis=-1, keepdims=True) + EPS)
            y_ref[:, h * DV:(h + 1) * DV] = ((oh * r) * hg * _silu(gv[:, h * DV:(h + 1) * DV])).astype(BF16)

    return pl.pallas_call(
        body, name="gla_post_fwd", grid=(n // tt,),
        in_specs=[pl.BlockSpec((2, tt, D), lambda i: (0, i, 0)), pl.BlockSpec((tt, D), lambda i: (i, 2)),
                  pl.BlockSpec((1, DV), lambda i: (0, 0))],
        out_specs=pl.BlockSpec((tt, D), lambda i: (i, 0)),
        out_shape=jax.ShapeDtypeStruct((n, D), BF16),
        compiler_params=_params(("parallel",)),
    )(o2, p_all, head_gain)


def _gla_post_bwd(o2, p_all, head_gain, dyb, n):
    tt = _tile(n, 256)

    def body(o_ref, g_ref, hg_ref, dy_ref, do_ref, dg_ref, dhg_ref):
        i = pl.program_id(0)

        @pl.when(i == 0)
        def _():
            dhg_ref[...] = jnp.zeros_like(dhg_ref)

        o = o_ref[0] + o_ref[1]
        gv = g_ref[...]
        hg = hg_ref[...]
        dy = dy_ref[...]
        acc = jnp.zeros((1, DV), F32)
        for h in range(HEADS):
            sl = slice(h * DV, (h + 1) * DV)
            oh = o[:, sl]
            r = lax.rsqrt(jnp.mean(oh * oh, axis=-1, keepdims=True) + EPS)
            on = oh * r
            gh = gv[:, sl]
            dyh = dy[:, sl]
            dg_ref[:, sl] = dyh * (on * hg) * _dsilu(gh)
            dog = dyh * _silu(gh)
            acc = acc + _rowsum(dog * on)
            don = dog * hg
            do_ref[:, sl] = r * (don - on * jnp.mean(don * on, axis=-1, keepdims=True))
        dhg_ref[...] += acc

    return pl.pallas_call(
        body, name="gla_post_bwd", grid=(n // tt,),
        in_specs=[pl.BlockSpec((2, tt, D), lambda i: (0, i, 0)), pl.BlockSpec((tt, D), lambda i: (i, 2)),
                  pl.BlockSpec((1, DV), lambda i: (0, 0)), pl.BlockSpec((tt, D), lambda i: (i, 0))],
        out_specs=[pl.BlockSpec((tt, D), lambda i: (i, 0)), pl.BlockSpec((tt, D), lambda i: (i, 0)),
                   pl.BlockSpec((1, DV), lambda i: (0, 0))],
        out_shape=[jax.ShapeDtypeStruct((n, D), F32), jax.ShapeDtypeStruct((n, D), F32),
                   jax.ShapeDtypeStruct((1, DV), F32)],
        compiler_params=_params(("arbitrary",)),
    )(o2, p_all, head_gain, dyb)


def _gla_assemble(p_all, w2, b2, dq, dk, dv, dla, dgate, n):
    ntot = p_all.shape[0]
    tt = _tile(n, 128)
    nmain = n // tt
    assert ntot % tt == 0

    def body(a_ref, w_ref, b_ref, dq_ref, dk_ref, dv_ref, dla_ref, dg_ref, dp_ref, dw_ref, db_ref):
        i = pl.program_id(0)

        @pl.when(i == 0)
        def _():
            dw_ref[...] = jnp.zeros_like(dw_ref)
            db_ref[...] = jnp.zeros_like(db_ref)

        a = a_ref[...]
        w = w_ref[...]
        z = _dot(a, w) + b_ref[...]
        dz = dla_ref[...] * (1.0 / (1.0 + jnp.exp(z))) * (1.0 / TAU)
        dw_ref[...] += _dot(a, dz, _TN)
        db_ref[...] += _rowsum(dz)
        dp_ref[:, 0:KEY] = ((dq_ref[0] + dq_ref[1]) * 1.0).astype(BF16)
        dp_ref[:, KEY:2 * KEY] = (dk_ref[0] + dk_ref[1]).astype(BF16)
        dp_ref[:, 2 * KEY:2 * KEY + D] = (dv_ref[0] + dv_ref[1]).astype(BF16)
        dp_ref[:, 2 * KEY + D:2 * KEY + 2 * D] = (dg_ref[...] * jnp.where(i < nmain, 1.0, 0.0)).astype(BF16)
        dp_ref[:, 2 * KEY + 2 * D:GLA_IN_PAD] = _dot(dz, w, _NT).astype(BF16)

    return pl.pallas_call(
        body, name="gla_assemble", grid=(ntot // tt,),
        in_specs=[pl.BlockSpec((tt, 128), lambda i: (i, (2 * KEY + 2 * D) // 128)),
                  pl.BlockSpec((128, 2 * KEY), lambda i: (0, 0)), pl.BlockSpec((1, 2 * KEY), lambda i: (0, 0)),
                  pl.BlockSpec((2, tt, KEY), lambda i: (0, i, 0)), pl.BlockSpec((2, tt, KEY), lambda i: (0, i, 0)),
                  pl.BlockSpec((2, tt, D), lambda i: (0, i, 0)), pl.BlockSpec((tt, 2 * KEY), lambda i: (i, 0)),
                  pl.BlockSpec((tt, D), lambda i: (jnp.minimum(i, nmain - 1), 0))],
        out_specs=[pl.BlockSpec((tt, GLA_IN_PAD), lambda i: (i, 0)), pl.BlockSpec((128, 2 * KEY), lambda i: (0, 0)),
                   pl.BlockSpec((1, 2 * KEY), lambda i: (0, 0))],
        out_shape=[jax.ShapeDtypeStruct((ntot, GLA_IN_PAD), BF16), jax.ShapeDtypeStruct((128, 2 * KEY), F32),
                   jax.ShapeDtypeStruct((1, 2 * KEY), F32)],
        compiler_params=_params(("arbitrary",)),
    )(p_all, w2, b2, dq, dk, dv, dla, dgate)


ADA_ROWS = 24
ADA_SH = N_MOD * D // N_CHIPS


def _ada_fwd(cvec, ada_w, ada_b_sh):
    def body(c_ref, w_ref, b_ref, o_ref):
        o_ref[0] = _dot(_silu(c_ref[...]), w_ref[0]) + b_ref[0]

    return pl.pallas_call(
        body, name="ada_fwd", grid=(2,),
        in_specs=[pl.BlockSpec((ADA_ROWS, D), lambda l: (0, 0)), pl.BlockSpec((1, D, ADA_SH), lambda l: (l, 0, 0)),
                  pl.BlockSpec((1, 1, ADA_SH), lambda l: (l, 0, 0))],
        out_specs=pl.BlockSpec((1, ADA_ROWS, ADA_SH), lambda l: (l, 0, 0)),
        out_shape=jax.ShapeDtypeStruct((2, ADA_ROWS, ADA_SH), F32),
        compiler_params=_params(("parallel",)),
    )(cvec, ada_w, ada_b_sh)


def _ada_bwd(cvec, ada_w, dmod_sh):
    def body(c_ref, w_ref, dm_ref, gw_ref, dc_ref):
        dm = dm_ref[0]
        gw_ref[0] = _dot(_silu(c_ref[...]), dm, _TN)
        dc_ref[0] = _dot(dm, w_ref[0], _NT)

    return pl.pallas_call(
        body, name="ada_bwd", grid=(2,),
        in_specs=[pl.BlockSpec((ADA_ROWS, D), lambda l: (0, 0)), pl.BlockSpec((1, D, ADA_SH), lambda l: (l, 0, 0)),
                  pl.BlockSpec((1, ADA_ROWS, ADA_SH), lambda l: (l, 0, 0))],
        out_specs=[pl.BlockSpec((1, D, ADA_SH), lambda l: (l, 0, 0)), pl.BlockSpec((1, ADA_ROWS, D), lambda l: (l, 0, 0))],
        out_shape=[jax.ShapeDtypeStruct((2, D, ADA_SH), F32), jax.ShapeDtypeStruct((2, ADA_ROWS, D), F32)],
        compiler_params=_params(("parallel",)),
    )(cvec, ada_w, dmod_sh)


def _sum_slots(x, name):
    s, r, _ = x.shape

    def body(x_ref, o_ref):
        acc = x_ref[0]
        for k in range(1, s):
            acc = acc + x_ref[k]
        o_ref[...] = acc

    return pl.pallas_call(
        body, name=name, out_shape=jax.ShapeDtypeStruct((r, 128), F32),
        in_specs=[pl.BlockSpec(memory_space=pltpu.VMEM)], out_specs=pl.BlockSpec(memory_space=pltpu.VMEM),
    )(x)


def _cctx_grad(dscc_parts, c_ctx):
    def body(p_ref, c_ref, o_ref):
        acc = p_ref[0]
        for k in range(1, N_CHIPS):
            acc = acc + p_ref[k]
        o_ref[...] = acc * _dsilu(c_ref[...])

    return pl.pallas_call(
        body, name="cctx_grad", out_shape=jax.ShapeDtypeStruct((8, 128), F32),
        in_specs=[pl.BlockSpec(memory_space=pltpu.VMEM)] * 2, out_specs=pl.BlockSpec(memory_space=pltpu.VMEM),
    )(dscc_parts, c_ctx)


def _adamw(w, g, m, v, name):
    r, cdim = w.shape
    tr = _tile(r, 256)
    c1 = 1.0 - ADAM_B1 ** ADAM_STEP
    c2 = 1.0 - ADAM_B2 ** ADAM_STEP

    def body(w_ref, g_ref, m_ref, v_ref, d_ref, mo_ref, vo_ref):
        gv = g_ref[...]
        mn = ADAM_B1 * m_ref[...] + (1.0 - ADAM_B1) * gv
        vn = ADAM_B2 * v_ref[...] + (1.0 - ADAM_B2) * (gv * gv)
        mo_ref[...] = mn
        vo_ref[...] = vn
        d_ref[...] = -ADAM_LR * ((mn / c1) / (jnp.sqrt(vn / c2) + ADAM_EPS) + ADAM_WD * w_ref[...])

    spec = pl.BlockSpec((tr, cdim), lambda i: (i, 0))
    sds = jax.ShapeDtypeStruct((r, cdim), F32)
    return pl.pallas_call(
        body, name=name, grid=(r // tr,), in_specs=[spec] * 4, out_specs=[spec] * 3, out_shape=[sds] * 3,
        compiler_params=_params(("parallel",)),
    )(w, g, m, v)


def _place():
    x, y, c = lax.axis_index("x"), lax.axis_index("y"), lax.axis_index("c")
    return x, y, c


def _allgather_small(blk, name):
    m_per, n = blk.shape

    def body(x_ref, out_ref, send_sems, recv_sems, local_sem):
        x, y, c = _place()
        me, sibling = (x, y, c), (x, y, 1 - c)
        chips = [(1 - x, y), (x, 1 - y), (1 - x, 1 - y)]

        def rows(px, py, pc):
            return out_ref.at[pl.ds((4 * px + 2 * py + pc) * m_per, m_per), :]

        def copy(k, block, to, src=None):
            return pltpu.make_async_remote_copy(
                src_ref=rows(*block) if src is None else src, dst_ref=rows(*block),
                send_sem=send_sems.at[k], recv_sem=recv_sems.at[k], device_id=to, device_id_type=MESH)

        mine = pltpu.make_async_copy(x_ref, rows(*me), local_sem)
        mine.start()
        first = [copy(0, me, sibling, src=x_ref)]
        first += [copy(1 + j, me, (*chip, c), src=x_ref) for j, chip in enumerate(chips)]
        for cp in first:
            cp.start()
        passed = [copy(4 + j, (*chip, c), sibling) for j, chip in enumerate(chips)]
        for j, chip in enumerate(chips):
            copy(1 + j, (*chip, c), me).wait_recv()
            passed[j].start()
        copy(0, sibling, me).wait_recv()
        for j, chip in enumerate(chips):
            copy(4 + j, (*chip, 1 - c), me).wait_recv()
        for cp in first + passed:
            cp.wait_send()
        mine.wait()

    return pl.pallas_call(
        body, name=name,
        out_shape=jax.ShapeDtypeStruct((N_DEV * m_per, n), blk.dtype),
        in_specs=[pl.BlockSpec(memory_space=pltpu.VMEM)],
        out_specs=pl.BlockSpec(memory_space=pltpu.VMEM),
        scratch_shapes=[pltpu.SemaphoreType.DMA((7,)), pltpu.SemaphoreType.DMA((7,)), pltpu.SemaphoreType.DMA],
    )(blk)


def _other_chips(x, y):
    return [(1 - x, y), (x, 1 - y), (1 - x, 1 - y)]


def _weights_allgather(own):
    r = own.shape[0]
    hr = r // 2

    def body(own_ref, out_ref, send_sems, recv_sems, fsend_sems, frecv_sems, own_sems):
        x, y, c = _place()
        chip = 2 * x + y
        others = _other_chips(x, y)

        def half(ch, cc):
            return out_ref.at[ch, pl.ds(cc * hr, hr), :]

        mine = pltpu.make_async_remote_copy(
            src_ref=own_ref, dst_ref=out_ref.at[chip], send_sem=own_sems.at[0], recv_sem=own_sems.at[1],
            device_id=(x, y, 1 - c), device_id_type=MESH)
        mine.start()
        sends = []
        for j, (ox, oy) in enumerate(others):
            cp = pltpu.make_async_remote_copy(
                src_ref=own_ref.at[pl.ds(c * hr, hr), :], dst_ref=half(chip, c),
                send_sem=send_sems.at[j], recv_sem=recv_sems.at[j], device_id=(ox, oy, c), device_id_type=MESH)
            cp.start()
            sends.append(cp)
        fwd = []
        for j, (ox, oy) in enumerate(others):
            och = 2 * ox + oy
            pltpu.make_async_remote_copy(
                src_ref=half(och, c), dst_ref=half(och, c), send_sem=send_sems.at[j], recv_sem=recv_sems.at[j],
                device_id=(ox, oy, c), device_id_type=MESH).wait_recv()
            cp = pltpu.make_async_remote_copy(
                src_ref=half(och, c), dst_ref=half(och, c), send_sem=fsend_sems.at[j], recv_sem=frecv_sems.at[j],
                device_id=(x, y, 1 - c), device_id_type=MESH)
            cp.start()
            fwd.append(cp)
        for j, (ox, oy) in enumerate(others):
            och = 2 * ox + oy
            pltpu.make_async_remote_copy(
                src_ref=half(och, 1 - c), dst_ref=half(och, 1 - c), send_sem=fsend_sems.at[j],
                recv_sem=frecv_sems.at[j], device_id=(x, y, 1 - c), device_id_type=MESH).wait_recv()
        for cp in sends + fwd:
            cp.wait_send()
        mine.wait()

    any_spec = pl.BlockSpec(memory_space=pl.ANY)
    return pl.pallas_call(
        body, name="weights_allgather",
        out_shape=jax.ShapeDtypeStruct((N_CHIPS, r, D), own.dtype),
        in_specs=[any_spec], out_specs=any_spec,
        scratch_shapes=[pltpu.SemaphoreType.DMA((3,)), pltpu.SemaphoreType.DMA((3,)),
                        pltpu.SemaphoreType.DMA((3,)), pltpu.SemaphoreType.DMA((3,)), pltpu.SemaphoreType.DMA((2,))],
    )(own)


def _rs_pair_exchange(g):
    r = g.shape[1]
    hr = r // 2

    def body(g_ref, got_ref, send_sem, recv_sem):
        x, y, c = _place()
        cp = pltpu.make_async_remote_copy(
            src_ref=g_ref.at[:, pl.ds((1 - c) * hr, hr), :], dst_ref=got_ref, send_sem=send_sem, recv_sem=recv_sem,
            device_id=(x, y, 1 - c), device_id_type=MESH)
        cp.start()
        cp.wait()

    any_spec = pl.BlockSpec(memory_space=pl.ANY)
    return pl.pallas_call(
        body, name="rs_pair_exchange",
        out_shape=jax.ShapeDtypeStruct((N_CHIPS, hr, D), F32),
        in_specs=[any_spec], out_specs=any_spec,
        scratch_shapes=[pltpu.SemaphoreType.DMA, pltpu.SemaphoreType.DMA],
    )(g)


def _rs_chip_sum(place, g, got):
    r = g.shape[1]
    hr = r // 2
    tr = _tile(hr, 640, 16)
    nt = hr // tr

    def body(pl_ref, g_ref, got_ref, p16_ref, p32_ref):
        s = pl.program_id(1)
        p = g_ref[0] + got_ref[0]
        p16_ref[0] = p.astype(BF16)

        @pl.when(s == pl_ref[1])
        def _():
            p32_ref[...] = p

    return pl.pallas_call(
        body, name="rs_chip_sum",
        grid_spec=pltpu.PrefetchScalarGridSpec(
            num_scalar_prefetch=1, grid=(nt, N_CHIPS),
            in_specs=[pl.BlockSpec((1, tr, D), lambda i, s, pr: (s, pr[0] * nt + i, 0)),
                      pl.BlockSpec((1, tr, D), lambda i, s, pr: (s, i, 0))],
            out_specs=[pl.BlockSpec((1, tr, D), lambda i, s, pr: (s, i, 0)),
                       pl.BlockSpec((tr, D), lambda i, s, pr: (i, 0))]),
        out_shape=[jax.ShapeDtypeStruct((N_CHIPS, hr, D), BF16), jax.ShapeDtypeStruct((hr, D), F32)],
        compiler_params=_params(("parallel", "arbitrary")),
    )(place, g, got)


def _rs_scatter(p16):
    def body(p_ref, out_ref, send_sems, recv_sems):
        x, y, c = _place()
        chip = 2 * x + y
        others = _other_chips(x, y)
        sends = []
        for j, (ox, oy) in enumerate(others):
            cp = pltpu.make_async_remote_copy(
                src_ref=p_ref.at[2 * ox + oy], dst_ref=out_ref.at[chip], send_sem=send_sems.at[j],
                recv_sem=recv_sems.at[j], device_id=(ox, oy, c), device_id_type=MESH)
            cp.start()
            sends.append(cp)
        for j, (ox, oy) in enumerate(others):
            och = 2 * ox + oy
            pltpu.make_async_remote_copy(
                src_ref=p_ref.at[och], dst_ref=out_ref.at[och], send_sem=send_sems.at[j], recv_sem=recv_sems.at[j],
                device_id=(ox, oy, c), device_id_type=MESH).wait_recv()
        for cp in sends:
            cp.wait_send()

    any_spec = pl.BlockSpec(memory_space=pl.ANY)
    return pl.pallas_call(
        body, name="rs_scatter",
        out_shape=jax.ShapeDtypeStruct(p16.shape, p16.dtype),
        in_specs=[any_spec], out_specs=any_spec,
        scratch_shapes=[pltpu.SemaphoreType.DMA((3,)), pltpu.SemaphoreType.DMA((3,))],
    )(p16)


def _rs_final_sum(place, parts, p32):
    hr = parts.shape[1]
    tr = _tile(hr, 640, 16)
    nt = hr // tr

    def body(pl_ref, a_ref, b_ref, c_ref, p32_ref, o_ref):
        o_ref[...] = ((p32_ref[...] + a_ref[0].astype(F32)) + b_ref[0].astype(F32)) + c_ref[0].astype(F32)

    def other(j):
        return pl.BlockSpec((1, tr, D), lambda i, pr: (j + jnp.where(pr[1] <= j, 1, 0), i, 0))

    return pl.pallas_call(
        body, name="rs_final_sum",
        grid_spec=pltpu.PrefetchScalarGridSpec(
            num_scalar_prefetch=1, grid=(nt,),
            in_specs=[other(0), other(1), other(2), pl.BlockSpec((tr, D), lambda i, pr: (i, 0))],
            out_specs=pl.BlockSpec((tr, D), lambda i, pr: (pr[0] * nt + i, 0))),
        out_shape=jax.ShapeDtypeStruct((2 * hr, D), F32),
        compiler_params=_params(("parallel",)),
    )(place, parts, parts, parts, p32)


def _rs_pair_gather(both):
    hr = both.shape[0] // 2

    def body(in_ref, out_ref, send_sem, recv_sem):
        x, y, c = _place()
        mine = out_ref.at[pl.ds(c * hr, hr), :]
        cp = pltpu.make_async_remote_copy(
            src_ref=mine, dst_ref=mine, send_sem=send_sem, recv_sem=recv_sem,
            device_id=(x, y, 1 - c), device_id_type=MESH)
        cp.start()
        theirs = out_ref.at[pl.ds((1 - c) * hr, hr), :]
        pltpu.make_async_remote_copy(
            src_ref=theirs, dst_ref=theirs, send_sem=send_sem, recv_sem=recv_sem,
            device_id=(x, y, 1 - c), device_id_type=MESH).wait_recv()
        cp.wait_send()

    any_spec = pl.BlockSpec(memory_space=pl.ANY)
    return pl.pallas_call(
        body, name="rs_pair_gather",
        out_shape=jax.ShapeDtypeStruct(both.shape, F32),
        in_specs=[any_spec], out_specs=any_spec, input_output_aliases={0: 0},
        scratch_shapes=[pltpu.SemaphoreType.DMA, pltpu.SemaphoreType.DMA],
    )(both)


def _interleave(w, parts):
    n = w.shape[0] // parts
    return w.reshape(parts, n // 128, 128, *w.shape[1:]).swapaxes(0, 1).reshape(w.shape)


def _deinterleave(w, parts):
    n = w.shape[0] // parts
    return w.reshape(n // 128, parts, 128, *w.shape[1:]).swapaxes(0, 1).reshape(w.shape)


def _local_step(x, ctx, tgt, mods, mc, wts, small):
    nb, t, _ = x.shape
    tc = ctx.shape[1]
    n = nb * t
    nc = nb * tc
    xf = x.reshape(n, D)
    cf = ctx.reshape(nc, D)
    tf = tgt.reshape(n, D)
    vec = lambda a: a.reshape(1, -1)
    m = [[mods[l, :, k, :].reshape(nb, 1, D) for k in range(N_MOD)] for l in range(2)]
    mc_b = [jnp.broadcast_to(mc[k].reshape(1, 1, D), (nb, 1, D)) for k in range(2)]

    w_gin = jnp.pad(wts["gla_in_t"], ((0, GLA_IN_PAD - GLA_IN), (0, 0)))
    w_sin = _interleave(wts["sc_in_t"], 3)
    w_up = [_interleave(wts["up_t"][l], 2) for l in range(2)]
    cw = [_interleave(small["ffn_conv_w"][l].T, 2).T for l in range(2)]
    cb = [_interleave(small["ffn_conv_b"][l], 2).reshape(1, -1) for l in range(2)]
    w2 = jnp.zeros((128, 2 * KEY), F32)
    w2 = w2.at[0:RANK, 0:KEY].set(small["gla_w_a2"][0]).at[RANK:2 * RANK, KEY:].set(small["gla_w_a2"][1])
    b2 = small["gla_b_a"].reshape(1, 2 * KEY)
    hg = small["gla_head_norm"].reshape(1, DV)

    hn0 = _mod_fwd(xf, vec(small["norm_mix"][0]), m[0][0], m[0][1], t, "mod0_main")
    hnc = _mod_fwd(cf, vec(small["norm_mix"][0]), mc_b[0], mc_b[1], tc, "mod0_ctx")
    hn_all = jnp.concatenate([hn0, hnc], axis=0)
    p_all = _mm(hn_all, w_gin, "nt", F32, "gla_in_proj", 768, 3200)
    la_all = _gla_decay_fwd(p_all, w2, b2)
    o2, ss = _gla_scan_fwd(p_all, la_all, nb, t, tc)
    yb0 = _gla_post_fwd(o2, p_all, hg, n)
    y0 = _mm(yb0, wts["gla_out"], "nn", F32, "gla_out_proj", 1024, 1024)
    h1, hn1 = _mod_fwd(xf, vec(small["norm_ffn"][0]), m[0][3], m[0][4], t, "mod0_ffn", y=y0, gate=m[0][2])
    u0 = _mm(hn1, w_up[0], "nt", F32, "ffn0_up", 1024, 1280)
    z0 = _ffn_mid_fwd(u0, cw[0], cb[0], nb, t, "ffn0_mid_fwd")
    f0 = _mm(z0, wts["down"][0], "nn", F32, "ffn0_down", 1024, 1024)
    h2, hn2 = _mod_fwd(h1, vec(small["norm_mix"][1]), m[1][0], m[1][1], t, "mod1_mix", y=f0, gate=m[0][5])
    p1 = _mm(hn2, w_sin, "nt", F32, "sc_in_proj", 1024, 1536)
    yb1 = _sc_mid_fwd(p1, small["sc_conv_w"], nb, t)
    y1 = _mm(yb1, wts["sc_out"], "nn", F32, "sc_out_proj", 1024, 1024)
    h3, hn3 = _mod_fwd(h2, vec(small["norm_ffn"][1]), m[1][3], m[1][4], t, "mod1_ffn", y=y1, gate=m[1][2])
    u1 = _mm(hn3, w_up[1], "nt", F32, "ffn1_up", 1024, 1280)
    z1 = _ffn_mid_fwd(u1, cw[1], cb[1], nb, t, "ffn1_mid_fwd")
    f1 = _mm(z1, wts["down"][1], "nn", F32, "ffn1_down", 1024, 1024)
    loss, dh4, df1, dm15, dfinal = _final(h3, f1, m[1][5], vec(small["final_norm"]), tf, t)

    gb, gs = {}, {}
    dmods = [[None] * N_MOD for _ in range(2)]
    dmods[1][5] = dm15

    def ffn_bwd(l, df, u, z, hn, tag):
        dz = _mm(df, wts["down"][l], "nt", F32, f"ffn{l}_down_dx", 1024, 1280)
        gdown = _mm(z, df, "tn", F32, f"ffn{l}_down_dw", 640, 1024)
        du, dcw, dcb = _ffn_mid_bwd(u, cw[l], cb[l], dz, nb, t, f"ffn{l}_mid_bwd")
        dhn = _mm(du, w_up[l], "nn", F32, f"ffn{l}_up_dx", 512, 512)
        gup = _deinterleave(_mm(du, hn, "tn", F32, f"ffn{l}_up_dw", 640, 1024), 2)
        return dhn, gdown, gup, _deinterleave(dcw.T, 2).T, _deinterleave(dcb.reshape(-1), 2)

    dhn3, gdown1, gup1, dcw1, dcb1 = ffn_bwd(1, df1, u1, z1, hn3, "ffn1")
    r = _mod_bwd(h3, dhn3, vec(small["norm_ffn"][1]), m[1][4], t, "mod1_ffn_bwd", dh_out=dh4, y_prev=y1,
                 gate_prev=m[1][2])
    dh3, dmods[1][4], dmods[1][3], dnf1, dy1, dmods[1][2] = (r["dh"], r["dscale"], r["dshift"], r["dgain"],
                                                             r["dy_prev"], r["dgate_prev"])
    dyb1 = _mm(dy1, wts["sc_out"], "nt", F32, "sc_out_dx", 1024, 1024)
    gb["sc_out"] = _mm(yb1, dy1, "tn", F32, "sc_out_dw", 512, 1024)
    dp1, dscw = _sc_mid_bwd(p1, small["sc_conv_w"], dyb1, nb, t)
    dhn2 = _mm(dp1, w_sin, "nn", F32, "sc_in_dx", 1024, 512)
    gb["sc_in_t"] = _deinterleave(_mm(dp1, hn2, "tn", F32, "sc_in_dw", 768, 1024), 3)
    r = _mod_bwd(h2, dhn2, vec(small["norm_mix"][1]), m[1][1], t, "mod1_mix_bwd", dh_out=dh3, y_prev=f0,
                 gate_prev=m[0][5])
    dh2, dmods[1][1], dmods[1][0], dnm1, df0, dmods[0][5] = (r["dh"], r["dscale"], r["dshift"], r["dgain"],
                                                             r["dy_prev"], r["dgate_prev"])
    dhn1, gdown0, gup0, dcw0, dcb0 = ffn_bwd(0, df0, u0, z0, hn1, "ffn0")
    r = _mod_bwd(h1, dhn1, vec(small["norm_ffn"][0]), m[0][4], t, "mod0_ffn_bwd", dh_out=dh2, y_prev=y0,
                 gate_prev=m[0][2])
    dh1, dmods[0][4], dmods[0][3], dnf0, dy0, dmods[0][2] = (r["dh"], r["dscale"], r["dshift"], r["dgain"],
                                                             r["dy_prev"], r["dgate_prev"])
    dyb0 = _mm(dy0, wts["gla_out"], "nt", F32, "gla_out_dx", 1024, 1024)
    gb["gla_out"] = _mm(yb0, dy0, "tn", F32, "gla_out_dw", 512, 1024)
    do, dgate, dhg = _gla_post_bwd(o2, p_all, hg, dyb0, n)
    dq, dk, dv, dla = _gla_scan_bwd(p_all, la_all, do, ss, nb, t, tc)
    dp, dw2, db2 = _gla_assemble(p_all, w2, b2, dq, dk, dv, dla, dgate, n)
    dhn_all = _mm(dp, w_gin, "nn", F32, "gla_in_dx", 768, 512)
    gb["gla_in_t"] = _mm(dp, hn_all, "tn", F32, "gla_in_dw", 640, 1024)[:GLA_IN]
    r = _mod_bwd(xf, dhn_all, vec(small["norm_mix"][0]), m[0][1], t, "mod0_main_bwd", dh_out=dh1)
    grad_x, dmods[0][1], dmods[0][0], dnm0 = r["dh"], r["dscale"], r["dshift"], r["dgain"]
    rc = _mod_bwd(cf, dhn_all, vec(small["norm_mix"][0]), mc_b[1], tc, "mod0_ctx_bwd", dhn_row0=n, need_dh=False)
    dmc = jnp.stack([jnp.sum(rc["dshift"], axis=0).reshape(D), jnp.sum(rc["dscale"], axis=0).reshape(D)])
    dnm0 = dnm0 + rc["dgain"]

    gb["up_t"] = [gup0, gup1]
    gb["down"] = [gdown0, gdown1]
    gs["norm_mix"] = jnp.concatenate([dnm0, dnm1], axis=0)
    gs["norm_ffn"] = jnp.concatenate([dnf0, dnf1], axis=0)
    gs["final_norm"] = dfinal.reshape(D)
    gs["gla_w_a2"] = jnp.stack([dw2[0:RANK, 0:KEY], dw2[RANK:2 * RANK, KEY:]])
    gs["gla_b_a"] = db2.reshape(2, KEY)
    gs["gla_head_norm"] = dhg.reshape(DV)
    gs["sc_conv_w"] = dscw
    gs["ffn_conv_w"] = jnp.stack([dcw0, dcw1])
    gs["ffn_conv_b"] = jnp.stack([dcb0, dcb1])
    dmods_arr = jnp.stack([jnp.stack([dmods[l][k].reshape(nb, D) for k in range(N_MOD)], axis=1) for l in range(2)])
    return loss, grad_x.reshape(nb, t, D), gb, gs, dmods_arr, dmc


def _pack(arrs):
    parts, meta, off = [], [], 0
    for a in arrs:
        r = a.size // 128
        rp = -(-r // 8) * 8
        a2 = a.reshape(r, 128).astype(F32)
        if rp != r:
            a2 = jnp.pad(a2, ((0, rp - r), (0, 0)))
        parts.append(a2)
        meta.append((off, r, a.shape))
        off += rp
    return jnp.concatenate(parts, axis=0), meta


def _unpack(buf, meta, lead=()):
    return [buf[..., off:off + r, :].reshape(*lead, *shape) for off, r, shape in meta]


_BIG = ("gla_in_t", "sc_in_t", "up_t0", "up_t1", "gla_out", "sc_out", "down0", "down1")
_BIG_ROWS = {"gla_in_t": GLA_IN // N_CHIPS, "sc_in_t": 3 * D // N_CHIPS, "up_t0": 2 * HID // N_CHIPS,
             "up_t1": 2 * HID // N_CHIPS, "gla_out": D // N_CHIPS, "sc_out": D // N_CHIPS,
             "down0": HID // N_CHIPS, "down1": HID // N_CHIPS}
_BIG_TOTAL = sum(_BIG_ROWS.values())
_BIG_PAD = -(-_BIG_TOTAL // 32) * 32


def _big_offsets():
    off, out = 0, {}
    for k in _BIG:
        out[k] = off
        off += _BIG_ROWS[k]
    return out


def kernel(x, c, ctx, c_ctx, ada_w, ada_b, norm_mix, norm_ffn, gla_w_in, gla_w_a2, gla_b_a, gla_head_norm, gla_w_out, sc_w_in, sc_conv_w, sc_w_out, ffn_w_up, ffn_conv_w, ffn_conv_b, ffn_w_down, final_norm, loss_target, m_c_ctx, m_ada_w, m_ada_b, m_norm_mix, m_norm_ffn, m_gla_w_in, m_gla_w_a2, m_gla_b_a, m_gla_head_norm, m_gla_w_out, m_sc_w_in, m_sc_conv_w, m_sc_w_out, m_ffn_w_up, m_ffn_conv_w, m_ffn_conv_b, m_ffn_w_down, m_final_norm, v_c_ctx, v_ada_w, v_ada_b, v_norm_mix, v_norm_ffn, v_gla_w_in, v_gla_w_a2, v_gla_b_a, v_gla_head_norm, v_gla_w_out, v_sc_w_in, v_sc_conv_w, v_sc_w_out, v_ffn_w_up, v_ffn_conv_w, v_ffn_conv_b, v_ffn_w_down, v_final_norm):
    ix, iy, ic = _place()
    chip = 2 * ix + iy
    dev = 2 * chip + ic
    place = jnp.stack([ic, chip]).astype(jnp.int32)
    nb = x.shape[0]
    offs = _big_offsets()

    buf, meta = _pack([c, ffn_conv_w, sc_conv_w, gla_w_a2, gla_b_a])
    got = _allgather_small(buf, "gather_small_in").reshape(N_DEV, buf.shape[0], 128)
    c_all, fcw, scw, wa2, ba = _unpack(got, meta, (N_DEV,))
    c_all = c_all.reshape(N_DEV * nb, D)
    per_chip = lambda a: a[0::2]
    ffn_conv_w_full = jnp.moveaxis(per_chip(fcw), 0, 2).reshape(2, 3, 2 * HID)
    sc_conv_w_full = jnp.moveaxis(per_chip(scw)[:, 0], 0, 1).reshape(3, D)
    gla_w_a2_full = jnp.moveaxis(per_chip(wa2)[:, 0], 0, 2).reshape(2, RANK, KEY)
    gla_b_a_full = jnp.moveaxis(per_chip(ba)[:, 0], 0, 1).reshape(2, KEY)

    own = {"gla_in_t": gla_w_in[0].T, "sc_in_t": sc_w_in[0].T, "up_t0": ffn_w_up[0].T, "up_t1": ffn_w_up[1].T,
           "gla_out": gla_w_out[0], "sc_out": sc_w_out[0], "down0": ffn_w_down[0], "down1": ffn_w_down[1]}
    own_packed = jnp.concatenate([own[k].astype(BF16) for k in _BIG]
                                 + [jnp.zeros((_BIG_PAD - _BIG_TOTAL, D), BF16)], axis=0)
    wg = _weights_allgather(own_packed)
    full = {k: wg[:, offs[k]:offs[k] + _BIG_ROWS[k], :].reshape(N_CHIPS * _BIG_ROWS[k], D) for k in _BIG}
    wts = {"gla_in_t": full["gla_in_t"], "sc_in_t": full["sc_in_t"], "up_t": [full["up_t0"], full["up_t1"]],
           "gla_out": full["gla_out"], "sc_out": full["sc_out"], "down": [full["down0"], full["down1"]]}

    cvec = jnp.concatenate([c_all, c_ctx.reshape(1, D), jnp.zeros((ADA_ROWS - N_DEV * nb - 1, D), F32)], axis=0)
    ada_b_sh = lax.dynamic_slice_in_dim(ada_b, chip * ADA_SH, ADA_SH, axis=1).reshape(2, 1, ADA_SH)
    mod_sh = _ada_fwd(cvec, ada_w, ada_b_sh)
    got = _allgather_small(mod_sh.reshape(2 * ADA_ROWS, ADA_SH), "gather_mod")
    mod_full = jnp.moveaxis(per_chip(got.reshape(N_DEV, 2, ADA_ROWS, ADA_SH)), 0, 2).reshape(2, ADA_ROWS, N_MOD * D)
    mods = lax.dynamic_slice_in_dim(mod_full, dev * nb, nb, axis=1).reshape(2, nb, N_MOD, D)
    mc = mod_full[0, N_DEV * nb, :2 * D].reshape(2, D)

    small = {"norm_mix": norm_mix, "norm_ffn": norm_ffn, "final_norm": final_norm, "gla_w_a2": gla_w_a2_full,
             "gla_b_a": gla_b_a_full, "gla_head_norm": gla_head_norm[0], "sc_conv_w": sc_conv_w_full,
             "ffn_conv_w": ffn_conv_w_full, "ffn_conv_b": ffn_conv_b}
    loss_p, grad_x, gb, gs, dmods, dmc = _local_step(x, ctx, loss_target, mods, mc, wts, small)

    sum_names = ["norm_mix", "norm_ffn", "final_norm", "gla_w_a2", "gla_b_a", "gla_head_norm", "sc_conv_w",
                 "ffn_conv_w", "ffn_conv_b"]
    buf, meta = _pack([jnp.broadcast_to(loss_p, (8, 128))] + [gs[k] for k in sum_names] + [dmc, dmods])
    n_sum = meta[-1][0]
    got = _allgather_small(buf, "gather_small_grads").reshape(N_DEV, buf.shape[0], 128)
    summed = _sum_slots(got[:, :n_sum], "sum_small_grads")
    parts = _unpack(summed, meta[:-1])
    loss = parts[0][0, 0]
    g_small = dict(zip(sum_names, parts[1:-1]))
    dmc_tot = parts[-1]
    dmods_all = jnp.moveaxis(_unpack(got, meta[-1:], (N_DEV,))[0], 0, 1).reshape(2, N_DEV * nb, N_MOD * D)

    ctx_row = jnp.stack([jnp.concatenate([dmc_tot.reshape(2 * D), jnp.zeros(((N_MOD - 2) * D,), F32)]),
                         jnp.zeros((N_MOD * D,), F32)]).reshape(2, 1, N_MOD * D)
    dmod_ext = jnp.concatenate([dmods_all, ctx_row, jnp.zeros((2, ADA_ROWS - N_DEV * nb - 1, N_MOD * D), F32)], axis=1)
    g_ada_b = _sum_slots(jnp.moveaxis(dmod_ext, 1, 0).reshape(ADA_ROWS, 2 * N_MOD * D // 128, 128),
                         "sum_ada_b").reshape(2, N_MOD * D)
    dmod_sh = lax.dynamic_slice_in_dim(dmod_ext, chip * ADA_SH, ADA_SH, axis=2)
    g_ada_w, dcv = _ada_bwd(cvec, ada_w, dmod_sh)
    dscc_part = (dcv[0, N_DEV * nb] + dcv[1, N_DEV * nb]).reshape(8, 128)
    got = _allgather_small(dscc_part, "gather_dscc").reshape(N_DEV, 8, 128)
    g_c_ctx = _cctx_grad(per_chip(got), c_ctx.reshape(8, 128)).reshape(D)

    gbig = {"gla_in_t": gb["gla_in_t"], "sc_in_t": gb["sc_in_t"], "up_t0": gb["up_t"][0], "up_t1": gb["up_t"][1],
            "gla_out": gb["gla_out"], "sc_out": gb["sc_out"], "down0": gb["down"][0], "down1": gb["down"][1]}
    g_packed = jnp.concatenate([gbig[k].reshape(N_CHIPS, _BIG_ROWS[k], D) for k in _BIG]
                               + [jnp.zeros((N_CHIPS, _BIG_PAD - _BIG_TOTAL, D), F32)], axis=1)
    from_sibling = _rs_pair_exchange(g_packed)
    p16, p32 = _rs_chip_sum(place, g_packed, from_sibling)
    landed = _rs_scatter(p16)
    half = _rs_final_sum(place, landed, p32)
    g_shard = _rs_pair_gather(half)
    seg = {k: g_shard[offs[k]:offs[k] + _BIG_ROWS[k]] for k in _BIG}

    sl_chip = lambda a, axis, width: lax.dynamic_slice_in_dim(a, chip * width, width, axis=axis)
    grads = {
        "c_ctx": g_c_ctx, "ada_w": g_ada_w, "ada_b": g_ada_b, "norm_mix": g_small["norm_mix"],
        "norm_ffn": g_small["norm_ffn"],
        "gla_w_in": seg["gla_in_t"].T[None], "gla_w_a2": sl_chip(g_small["gla_w_a2"], 2, KEY // N_CHIPS)[None],
        "gla_b_a": sl_chip(g_small["gla_b_a"], 1, KEY // N_CHIPS)[None],
        "gla_head_norm": g_small["gla_head_norm"][None], "gla_w_out": seg["gla_out"][None],
        "sc_w_in": seg["sc_in_t"].T[None], "sc_conv_w": sl_chip(g_small["sc_conv_w"], 1, D // N_CHIPS)[None],
        "sc_w_out": seg["sc_out"][None], "ffn_w_up": jnp.stack([seg["up_t0"].T, seg["up_t1"].T]),
        "ffn_conv_w": sl_chip(g_small["ffn_conv_w"], 2, 2 * HID // N_CHIPS), "ffn_conv_b": g_small["ffn_conv_b"],
        "ffn_w_down": jnp.stack([seg["down0"], seg["down1"]]), "final_norm": g_small["final_norm"],
    }
    weights = {"c_ctx": c_ctx, "ada_w": ada_w, "ada_b": ada_b, "norm_mix": norm_mix, "norm_ffn": norm_ffn,
               "gla_w_in": gla_w_in, "gla_w_a2": gla_w_a2, "gla_b_a": gla_b_a, "gla_head_norm": gla_head_norm,
               "gla_w_out": gla_w_out, "sc_w_in": sc_w_in, "sc_conv_w": sc_conv_w, "sc_w_out": sc_w_out,
               "ffn_w_up": ffn_w_up, "ffn_conv_w": ffn_conv_w, "ffn_conv_b": ffn_conv_b, "ffn_w_down": ffn_w_down,
               "final_norm": final_norm}
    mom1 = {"c_ctx": m_c_ctx, "ada_w": m_ada_w, "ada_b": m_ada_b, "norm_mix": m_norm_mix, "norm_ffn": m_norm_ffn,
            "gla_w_in": m_gla_w_in, "gla_w_a2": m_gla_w_a2, "gla_b_a": m_gla_b_a, "gla_head_norm": m_gla_head_norm,
            "gla_w_out": m_gla_w_out, "sc_w_in": m_sc_w_in, "sc_conv_w": m_sc_conv_w, "sc_w_out": m_sc_w_out,
            "ffn_w_up": m_ffn_w_up, "ffn_conv_w": m_ffn_conv_w, "ffn_conv_b": m_ffn_conv_b,
            "ffn_w_down": m_ffn_w_down, "final_norm": m_final_norm}
    mom2 = {"c_ctx": v_c_ctx, "ada_w": v_ada_w, "ada_b": v_ada_b, "norm_mix": v_norm_mix, "norm_ffn": v_norm_ffn,
            "gla_w_in": v_gla_w_in, "gla_w_a2": v_gla_w_a2, "gla_b_a": v_gla_b_a, "gla_head_norm": v_gla_head_norm,
            "gla_w_out": v_gla_w_out, "sc_w_in": v_sc_w_in, "sc_conv_w": v_sc_conv_w, "sc_w_out": v_sc_w_out,
            "ffn_w_up": v_ffn_w_up, "ffn_conv_w": v_ffn_conv_w, "ffn_conv_b": v_ffn_conv_b,
            "ffn_w_down": v_ffn_w_down, "final_norm": v_final_norm}
    names = list(weights)
    grads = {k: grads[k].reshape(weights[k].shape) for k in names}

    big_names = ["ada_w", "gla_w_in", "gla_w_out", "sc_w_in", "sc_w_out", "ffn_w_up", "ffn_w_down"]
    small_names = [k for k in names if k not in big_names]
    delta, new_m, new_v = {}, {}, {}
    for k in big_names:
        shp = weights[k].shape
        as2d = lambda a: a.reshape(-1, shp[-1])
        d_, m_, v_ = _adamw(as2d(weights[k]), as2d(grads[k]), as2d(mom1[k]), as2d(mom2[k]), "adamw_" + k)
        delta[k], new_m[k], new_v[k] = d_.reshape(shp), m_.reshape(shp), v_.reshape(shp)
    packed = [_pack([src[k] for k in small_names]) for src in (weights, grads, mom1, mom2)]
    meta = packed[0][1]
    outs = _adamw(packed[0][0], packed[1][0], packed[2][0], packed[3][0], "adamw_small")
    for dst, o in zip((delta, new_m, new_v), outs):
        for k, a in zip(small_names, _unpack(o, meta)):
            dst[k] = a

    return (loss, grad_x, *[grads[k] for k in names], *[delta[k] for k in names], *[new_m[k] for k in names],
            *[new_v[k] for k in names])
```

```python
import functools

import jax
import jax.numpy as jnp
from jax import lax
from jax.experimental import pallas as pl
from jax.experimental.pallas import tpu as pltpu

F32 = jnp.float32
BF16 = jnp.bfloat16
MESH = pl.DeviceIdType.MESH

EPS = 1e-6
D = 1024
N_MOD = 6
HEADS = 4
DK = 128
DV = 256
KEY = HEADS * DK
RANK = 16
TAU = 16.0
CH = 64
GRID_W = 64
HID = 2560
GLA_IN = 2 * KEY + 2 * D + 2 * RANK
GLA_IN_PAD = 3200
Q_SCALE = DK ** -0.5
N_CHIPS = 4
N_DEV = 8

ADAM_LR = 0.001
ADAM_B1 = 0.9
ADAM_B2 = 0.999
ADAM_EPS = 1e-08
ADAM_WD = 0.01
ADAM_STEP = 10

VMEM_LIMIT = 56 * 1024 * 1024


def _params(sem):
    return pltpu.CompilerParams(dimension_semantics=sem, vmem_limit_bytes=VMEM_LIMIT)


def _tile(n, pref, mult=8):
    if n <= pref:
        return n
    for t in range(pref - pref % mult, 0, -mult):
        if n % t == 0:
            return t
    raise ValueError((n, pref, mult))


_NN = (((1,), (0,)), ((), ()))
_NT = (((1,), (1,)), ((), ()))
_TN = (((0,), (0,)), ((), ()))


def _dot(a, b, dims=_NN):
    return lax.dot_general(a.astype(BF16), b.astype(BF16), dims, preferred_element_type=F32)


def _dot_hi(a, b, dims=_NN):
    return lax.dot_general(a, b, dims, precision=lax.Precision.HIGHEST, preferred_element_type=F32)


def _sigmoid(x):
    return 1.0 / (1.0 + jnp.exp(-x))


def _rowsum(x):
    return jnp.sum(x, axis=0, keepdims=True)


def _mm(a, b, form, out_dtype, name, tm, tn):
    if form == "tn":
        K, M = a.shape
    else:
        M, K = a.shape
    N = b.shape[0] if form == "nt" else b.shape[1]
    tm = _tile(M, tm, 128)
    tn = _tile(N, tn, 128)
    dims = {"nn": _NN, "nt": _NT, "tn": _TN}[form]

    def body(a_ref, b_ref, o_ref):
        o_ref[...] = _dot(a_ref[...], b_ref[...], dims).astype(o_ref.dtype)

    if form == "tn":
        a_spec = pl.BlockSpec((K, tm), lambda i, j: (0, i))
    else:
        a_spec = pl.BlockSpec((tm, K), lambda i, j: (i, 0))
    if form == "nt":
        b_spec = pl.BlockSpec((tn, K), lambda i, j: (j, 0))
    else:
        b_spec = pl.BlockSpec((K, tn), lambda i, j: (0, j))
    return pl.pallas_call(
        body,
        name=name,
        grid=(M // tm, N // tn),
        in_specs=[a_spec, b_spec],
        out_specs=pl.BlockSpec((tm, tn), lambda i, j: (i, j)),
        out_shape=jax.ShapeDtypeStruct((M, N), out_dtype),
        compiler_params=_params(("parallel", "parallel")),
    )(a, b)


def _mod_fwd(h, gain, shift, scale, tpb_rows, name, y=None, gate=None):
    n = h.shape[0]
    tt = _tile(tpb_rows, 256)
    tpb = tpb_rows // tt
    has_res = y is not None

    def body(*refs):
        if has_res:
            h_ref, y_ref, gate_ref, gain_ref, sh_ref, sc_ref, hout_ref, hn_ref = refs
            hv = h_ref[...] + gate_ref[0] * y_ref[...]
            hout_ref[...] = hv
        else:
            h_ref, gain_ref, sh_ref, sc_ref, hn_ref = refs
            hv = h_ref[...]
        r = lax.rsqrt(jnp.mean(hv * hv, axis=-1, keepdims=True) + EPS)
        hn = (hv * r) * gain_ref[...] * (1.0 + sc_ref[0]) + sh_ref[0]
        hn_ref[...] = hn.astype(BF16)

    row = pl.BlockSpec((tt, D), lambda i: (i, 0))
    per_b = pl.BlockSpec((1, 1, D), lambda i: (i // tpb, 0, 0))
    vec = pl.BlockSpec((1, D), lambda i: (0, 0))
    if has_res:
        in_specs = [row, row, per_b, vec, per_b, per_b]
        args = (h, y, gate, gain, shift, scale)
        out_specs = [row, row]
        out_shape = [jax.ShapeDtypeStruct((n, D), F32), jax.ShapeDtypeStruct((n, D), BF16)]
    else:
        in_specs = [row, vec, per_b, per_b]
        args = (h, gain, shift, scale)
        out_specs = row
        out_shape = jax.ShapeDtypeStruct((n, D), BF16)
    return pl.pallas_call(
        body, name=name, grid=(n // tt,), in_specs=in_specs, out_specs=out_specs, out_shape=out_shape,
        compiler_params=_params(("parallel",)),
    )(*args)


def _mod_bwd(h_in, dhn, gain, scale, tpb_rows, name, dhn_row0=0, dh_out=None, y_prev=None, gate_prev=None,
             need_dh=True):
    n = h_in.shape[0]
    nb = n // tpb_rows
    tt = _tile(tpb_rows, 256)
    tpb = tpb_rows // tt
    off = dhn_row0 // tt
    assert dhn_row0 % tt == 0
    has_out = dh_out is not None
    has_prev = y_prev is not None

    def body(*refs):
        it = iter(refs)
        h_ref, dhn_ref, gain_ref, sc_ref = next(it), next(it), next(it), next(it)
        dho_ref = next(it) if has_out else None
        yp_ref, gp_ref = (next(it), next(it)) if has_prev else (None, None)
        dh_ref = next(it) if need_dh else None
        dsc_ref, dsh_ref, dgain_ref = next(it), next(it), next(it)
        dyp_ref, dgp_ref = (next(it), next(it)) if has_prev else (None, None)
        i = pl.program_id(0)

        @pl.when(i == 0)
        def _():
            dgain_ref[...] = jnp.zeros_like(dgain_ref)

        @pl.when(i % tpb == 0)
        def _():
            dsc_ref[...] = jnp.zeros_like(dsc_ref)
            dsh_ref[...] = jnp.zeros_like(dsh_ref)
            if has_prev:
                dgp_ref[...] = jnp.zeros_like(dgp_ref)

        hv = h_ref[...]
        r = lax.rsqrt(jnp.mean(hv * hv, axis=-1, keepdims=True) + EPS)
        y = hv * r
        gain_v = gain_ref[...]
        g = dhn_ref[...].astype(F32)
        dsh_ref[0] += _rowsum(g)
        dsc_ref[0] += _rowsum(g * (y * gain_v))
        drn = g * (1.0 + sc_ref[0])
        dgain_ref[...] += _rowsum(drn * y)
        if need_dh:
            dy = drn * gain_v
            dh = r * (dy - y * jnp.mean(dy * y, axis=-1, keepdims=True))
            if has_out:
                dh = dh + dho_ref[...]
            dh_ref[...] = dh
            if has_prev:
                dyp_ref[...] = (dh * gp_ref[0]).astype(BF16)
                dgp_ref[0] += _rowsum(dh * yp_ref[...])

    row = pl.BlockSpec((tt, D), lambda i: (i, 0))
    row_off = pl.BlockSpec((tt, D), lambda i: (i + off, 0))
    per_b = pl.BlockSpec((1, 1, D), lambda i: (i // tpb, 0, 0))
    vec = pl.BlockSpec((1, D), lambda i: (0, 0))
    in_specs = [row, row_off, vec, per_b]
    args = [h_in, dhn, gain, scale]
    if has_out:
        in_specs.append(row)
        args.append(dh_out)
    if has_prev:
        in_specs += [row, per_b]
        args += [y_prev, gate_prev]
    out_specs, out_shape, names = [], [], []
    if need_dh:
        out_specs.append(row)
        out_shape.append(jax.ShapeDtypeStruct((n, D), F32))
        names.append("dh")
    for nm in ("dscale", "dshift"):
        out_specs.append(per_b)
        out_shape.append(jax.ShapeDtypeStruct((nb, 1, D), F32))
        names.append(nm)
    out_specs.append(vec)
    out_shape.append(jax.ShapeDtypeStruct((1, D), F32))
    names.append("dgain")
    if has_prev:
        out_specs += [row, per_b]
        out_shape += [jax.ShapeDtypeStruct((n, D), BF16), jax.ShapeDtypeStruct((nb, 1, D), F32)]
        names += ["dy_prev", "dgate_prev"]
    outs = pl.pallas_call(
        body, name=name, grid=(n // tt,), in_specs=in_specs, out_specs=out_specs, out_shape=out_shape,
        compiler_params=_params(("arbitrary",)),
    )(*args)
    return dict(zip(names, outs))


def _final(h, f, gate, gain, tgt, tpb_rows):
    n = h.shape[0]
    nb = n // tpb_rows
    tt = _tile(tpb_rows, 256)
    tpb = tpb_rows // tt

    def body(h_ref, f_ref, gate_ref, gain_ref, tgt_ref, loss_ref, dh_ref, df_ref, dgate_ref, dgain_ref):
        i = pl.program_id(0)

        @pl.when(i == 0)
        def _():
            loss_ref[...] = jnp.zeros_like(loss_ref)
            dgain_ref[...] = jnp.zeros_like(dgain_ref)

        @pl.when(i % tpb == 0)
        def _():
            dgate_ref[...] = jnp.zeros_like(dgate_ref)

        fv = f_ref[...]
        gate_v = gate_ref[0]
        hv = h_ref[...] + gate_v * fv
        r = lax.rsqrt(jnp.mean(hv * hv, axis=-1, keepdims=True) + EPS)
        y = hv * r
        gain_v = gain_ref[...]
        e = y * gain_v - tgt_ref[...]
        s = jnp.sum(_rowsum(e * e), axis=1, keepdims=True) * (0.5 / D)
        loss_ref[...] += jnp.broadcast_to(s, loss_ref.shape)
        dout = e * (1.0 / D)
        dgain_ref[...] += _rowsum(dout * y)
        dy = dout * gain_v
        dh = r * (dy - y * jnp.mean(dy * y, axis=-1, keepdims=True))
        dh_ref[...] = dh
        df_ref[...] = (dh * gate_v).astype(BF16)
        dgate_ref[0] += _rowsum(dh * fv)

    row = pl.BlockSpec((tt, D), lambda i: (i, 0))
    per_b = pl.BlockSpec((1, 1, D), lambda i: (i // tpb, 0, 0))
    vec = pl.BlockSpec((1, D), lambda i: (0, 0))
    return pl.pallas_call(
        body, name="final_loss", grid=(n // tt,),
        in_specs=[row, row, per_b, vec, row],
        out_specs=[pl.BlockSpec((1, 128), lambda i: (0, 0)), row, row, per_b, vec],
        out_shape=[jax.ShapeDtypeStruct((1, 128), F32), jax.ShapeDtypeStruct((n, D), F32),
                   jax.ShapeDtypeStruct((n, D), BF16), jax.ShapeDtypeStruct((nb, 1, D), F32),
                   jax.ShapeDtypeStruct((1, D), F32)],
        compiler_params=_params(("arbitrary",)),
    )(h, f, gate, gain, tgt)


def _shift_dn(x, s):
    return jnp.concatenate([jnp.zeros((s, x.shape[1]), x.dtype), x[: x.shape[0] - s]], axis=0)


def _shift_up(x, s):
    return jnp.concatenate([x[s:], jnp.zeros((s, x.shape[1]), x.dtype)], axis=0)


def _row_dn1(x):
    t = lax.broadcasted_iota(jnp.int32, x.shape, 0)
    return jnp.where(t % GRID_W == 0, 0.0, pltpu.roll(x, 1, 0))


def _row_up1(x):
    t = lax.broadcasted_iota(jnp.int32, x.shape, 0)
    return jnp.where(t % GRID_W == GRID_W - 1, 0.0, pltpu.roll(x, x.shape[0] - 1, 0))


def _silu(x):
    return x * _sigmoid(x)


def _dsilu(x):
    s = _sigmoid(x)
    return s * (1.0 + x * (1.0 - s))


def _ffn_mid_fwd(u0, cw, cb, nb, t, name):
    nc = HID // 128

    def body(u_ref, w_ref, b_ref, z_ref):
        x = u_ref[...]
        u = (_shift_dn(x, GRID_W) * w_ref[0:1, :] + x * w_ref[1:2, :] + _shift_up(x, GRID_W) * w_ref[2:3, :]
             + b_ref[...])
        z_ref[...] = (u[:, :128] * _silu(u[:, 128:])).astype(BF16)

    return pl.pallas_call(
        body, name=name, grid=(nc, nb),
        in_specs=[pl.BlockSpec((t, 256), lambda j, b: (b, j)), pl.BlockSpec((3, 256), lambda j, b: (0, j)),
                  pl.BlockSpec((1, 256), lambda j, b: (0, j))],
        out_specs=pl.BlockSpec((t, 128), lambda j, b: (b, j)),
        out_shape=jax.ShapeDtypeStruct((nb * t, HID), BF16),
        compiler_params=_params(("parallel", "parallel")),
    )(u0, cw, cb)


def _ffn_mid_bwd(u0, cw, cb, dz, nb, t, name):
    nc = HID // 128

    def body(u_ref, w_ref, b_ref, dz_ref, du_ref, dw_ref, db_ref):
        b = pl.program_id(1)

        @pl.when(b == 0)
        def _():
            dw_ref[...] = jnp.zeros_like(dw_ref)
            db_ref[...] = jnp.zeros_like(db_ref)

        x = u_ref[...]
        w0, w1, w2 = w_ref[0:1, :], w_ref[1:2, :], w_ref[2:3, :]
        xd = _shift_dn(x, GRID_W)
        xu = _shift_up(x, GRID_W)
        u = xd * w0 + x * w1 + xu * w2 + b_ref[...]
        a = u[:, :128]
        gt = u[:, 128:]
        dzv = dz_ref[...]
        du = jnp.concatenate([dzv * _silu(gt), dzv * a * _dsilu(gt)], axis=1)
        db_ref[...] += _rowsum(du)
        dw_ref[0:1, :] += _rowsum(du * xd)
        dw_ref[1:2, :] += _rowsum(du * x)
        dw_ref[2:3, :] += _rowsum(du * xu)
        dx = _shift_up(du, GRID_W) * w0 + du * w1 + _shift_dn(du, GRID_W) * w2
        du_ref[...] = dx.astype(BF16)

    return pl.pallas_call(
        body, name=name, grid=(nc, nb),
        in_specs=[pl.BlockSpec((t, 256), lambda j, b: (b, j)), pl.BlockSpec((3, 256), lambda j, b: (0, j)),
                  pl.BlockSpec((1, 256), lambda j, b: (0, j)), pl.BlockSpec((t, 128), lambda j, b: (b, j))],
        out_specs=[pl.BlockSpec((t, 256), lambda j, b: (b, j)), pl.BlockSpec((3, 256), lambda j, b: (0, j)),
                   pl.BlockSpec((1, 256), lambda j, b: (0, j))],
        out_shape=[jax.ShapeDtypeStruct((nb * t, 2 * HID), BF16), jax.ShapeDtypeStruct((3, 2 * HID), F32),
                   jax.ShapeDtypeStruct((1, 2 * HID), F32)],
        compiler_params=_params(("parallel", "arbitrary")),
    )(u0, cw, cb, dz)


def _sc_mid_fwd(p, cw, nb, t):
    nc = D // 128

    def body(p_ref, w_ref, y_ref):
        x = p_ref[...]
        cv = x[:, 128:256] * x[:, 256:]
        cc = _row_dn1(cv) * w_ref[0:1, :] + cv * w_ref[1:2, :] + _row_up1(cv) * w_ref[2:3, :]
        y_ref[...] = (x[:, :128] * cc).astype(BF16)

    return pl.pallas_call(
        body, name="sc_mid_fwd", grid=(nc, nb),
        in_specs=[pl.BlockSpec((t, 384), lambda j, b: (b, j)), pl.BlockSpec((3, 128), lambda j, b: (0, j))],
        out_specs=pl.BlockSpec((t, 128), lambda j, b: (b, j)),
        out_shape=jax.ShapeDtypeStruct((nb * t, D), BF16),
        compiler_params=_params(("parallel", "parallel")),
    )(p, cw)


def _sc_mid_bwd(p, cw, dyb, nb, t):
    nc = D // 128

    def body(p_ref, w_ref, dy_ref, dp_ref, dw_ref):
        b = pl.program_id(1)

        @pl.when(b == 0)
        def _():
            dw_ref[...] = jnp.zeros_like(dw_ref)

        x = p_ref[...]
        w0, w1, w2 = w_ref[0:1, :], w_ref[1:2, :], w_ref[2:3, :]
        bg, cg, v = x[:, :128], x[:, 128:256], x[:, 256:]
        cv = cg * v
        cvd = _row_dn1(cv)
        cvu = _row_up1(cv)
        cc = cvd * w0 + cv * w1 + cvu * w2
        dy = dy_ref[...]
        dcc = dy * bg
        dw_ref[0:1, :] += _rowsum(dcc * cvd)
        dw_ref[1:2, :] += _rowsum(dcc * cv)
        dw_ref[2:3, :] += _rowsum(dcc * cvu)
        dcv = _row_up1(dcc) * w0 + dcc * w1 + _row_dn1(dcc) * w2
        dp_ref[...] = jnp.concatenate([dy * cc, dcv * v, dcv * cg], axis=1).astype(BF16)

    return pl.pallas_call(
        body, name="sc_mid_bwd", grid=(nc, nb),
        in_specs=[pl.BlockSpec((t, 384), lambda j, b: (b, j)), pl.BlockSpec((3, 128), lambda j, b: (0, j)),
                  pl.BlockSpec((t, 128), lambda j, b: (b, j))],
        out_specs=[pl.BlockSpec((t, 384), lambda j, b: (b, j)), pl.BlockSpec((3, 128), lambda j, b: (0, j))],
        out_shape=[jax.ShapeDtypeStruct((nb * t, 3 * D), BF16), jax.ShapeDtypeStruct((3, D), F32)],
        compiler_params=_params(("parallel", "arbitrary")),
    )(p, cw, dyb)


def _gla_decay_fwd(p_all, w2, b2):
    n = p_all.shape[0]
    tt = _tile(n, 512)

    def body(a_ref, w_ref, b_ref, la_ref):
        z = _dot(a_ref[...], w_ref[...]) + b_ref[...]
        la_ref[...] = (jnp.minimum(z, 0.0) - jnp.log(1.0 + jnp.exp(-jnp.abs(z)))) * (1.0 / TAU)

    return pl.pallas_call(
        body, name="gla_decay_fwd", grid=(n // tt,),
        in_specs=[pl.BlockSpec((tt, 128), lambda i: (i, (2 * KEY + 2 * D) // 128)),
                  pl.BlockSpec((128, 2 * KEY), lambda i: (0, 0)), pl.BlockSpec((1, 2 * KEY), lambda i: (0, 0))],
        out_specs=pl.BlockSpec((tt, 2 * KEY), lambda i: (i, 0)),
        out_shape=jax.ShapeDtypeStruct((n, 2 * KEY), F32),
        compiler_params=_params(("parallel",)),
    )(p_all, w2, b2)


def _gla_blocks(nb, nm, ncx):
    def main_idx(d, i):
        return jnp.clip(jnp.where(d == 0, i - ncx, nm - 1 - (i - ncx)), 0, nm - 1)

    def rowblk(d, b, i):
        cidx = jnp.where(d == 0, i, ncx - 1 - i)
        return jnp.where(i < ncx, nb * nm + b * ncx + cidx, b * nm + main_idx(d, i))

    def mainblk(d, b, i):
        return b * nm + main_idx(d, i)

    return rowblk, mainblk


def _gla_mask(d):
    row = lax.broadcasted_iota(jnp.int32, (CH, CH), 0)
    col = lax.broadcasted_iota(jnp.int32, (CH, CH), 1)
    diff = row - col
    mask = jnp.where(d == 0, diff, -diff) >= 0
    return mask, jnp.where(mask, 1.0, 0.0).astype(F32)


def _gla_chunk(mf, q, k, g):
    bc = _dot_hi(mf, g)
    bl = _rowsum(g)
    eq = jnp.exp(bc)
    ek = jnp.exp(-bc)
    ed = jnp.exp(bl - bc)
    return bl, eq, ek, ed, q * Q_SCALE * eq, k * ek, k * ed


def _gla_scan_fwd(p_all, la_all, nb, t, tc, own):
    nm, ncx = t // CH, tc // CH
    nst = nm + ncx
    rowblk, mainblk = _gla_blocks(nb, nm, ncx)

    def body(q_ref, k_ref, v_ref, la_ref, own_ref, o_ref, ss_ref, wg_ref, st_ref, *sems):
        d = pl.program_id(0)
        b = pl.program_id(1)
        i = pl.program_id(2)
        ag = _AllGather(own_ref, wg_ref, sems)

        @pl.when((d == 0) & (b == 0) & (i == 0))
        def _():
            ag.start()

        @pl.when(i == 0)
        def _():
            st_ref[...] = jnp.zeros_like(st_ref)

        mask, mf = _gla_mask(d)
        for h in range(HEADS):
            ksl = slice(h * DK, (h + 1) * DK)
            vsl = slice(h * DV, (h + 1) * DV)
            bl, _, _, _, qs, ks, kd = _gla_chunk(mf, q_ref[:, ksl], k_ref[:, ksl], la_ref[:, ksl])
            st = st_ref[h]
            ss_ref[0, 0, 0, h] = st
            v = v_ref[:, vsl]
            att = jnp.where(mask, _dot(qs, ks, _NT), 0.0)
            o_ref[0, :, vsl] = _dot(qs, st, _NT) + _dot(att, v)
            st_ref[h] = st * jnp.exp(bl) + _dot(v, kd, _TN)

        @pl.when((d == 1) & (b == nb - 1) & (i == nst - 1))
        def _():
            ag.finish()

    any_spec = pl.BlockSpec(memory_space=pl.ANY)
    return pl.pallas_call(
        body, name="gla_scan_fwd", grid=(2, nb, nst),
        in_specs=[
            pl.BlockSpec((CH, KEY), lambda d, b, i: (rowblk(d, b, i), 0)),
            pl.BlockSpec((CH, KEY), lambda d, b, i: (rowblk(d, b, i), 1)),
            pl.BlockSpec((CH, D), lambda d, b, i: (rowblk(d, b, i), 1)),
            pl.BlockSpec((CH, KEY), lambda d, b, i: (rowblk(d, b, i), d)),
            any_spec,
        ],
        out_specs=[
            pl.BlockSpec((1, CH, D), lambda d, b, i: (d, mainblk(d, b, i), 0)),
            pl.BlockSpec((1, 1, 1, HEADS, DV, DK), lambda d, b, i: (d, b, i, 0, 0, 0)),
            any_spec,
        ],
        out_shape=[jax.ShapeDtypeStruct((2, nb * t, D), F32),
                   jax.ShapeDtypeStruct((2, nb, nst, HEADS, DV, DK), F32),
                   jax.ShapeDtypeStruct((N_CHIPS, own.shape[0], D), own.dtype)],
        scratch_shapes=[pltpu.VMEM((HEADS, DV, DK), F32)] + list(_AllGather.SEMS),
        compiler_params=_params(("arbitrary", "arbitrary", "arbitrary")),
    )(p_all, p_all, p_all, la_all, own)


def _gla_scan_bwd(p_all, la_all, do, ss, nb, t, tc, p16):
    nm, ncx = t // CH, tc // CH
    nst = nm + ncx
    ntot = nb * (t + tc)
    rowblk, mainblk = _gla_blocks(nb, nm, ncx)

    def body(q_ref, k_ref, v_ref, la_ref, do_ref, ss_ref, p16_ref, dq_ref, dk_ref, dv_ref, dla_ref, landed_ref,
             dst_ref, *sems):
        d = pl.program_id(0)
        b = pl.program_id(1)
        ip = pl.program_id(2)
        i = nst - 1 - ip
        sc = _Scatter(p16_ref, landed_ref, sems)

        @pl.when((d == 0) & (b == 0) & (ip == 0))
        def _():
            sc.start()

        @pl.when(ip == 0)
        def _():
            dst_ref[...] = jnp.zeros_like(dst_ref)

        mask, mf = _gla_mask(d)
        live = jnp.where(i >= ncx, 1.0, 0.0)
        for h in range(HEADS):
            ksl = slice(h * DK, (h + 1) * DK)
            vsl = slice(h * DV, (h + 1) * DV)
            bl, eq, ek, ed, qs, ks, kd = _gla_chunk(mf, q_ref[:, ksl], k_ref[:, ksl], la_ref[:, ksl])
            st = ss_ref[0, 0, 0, h]
            dst = dst_ref[h]
            v = v_ref[:, vsl]
            dov = do_ref[:, vsl] * live
            att = jnp.where(mask, _dot(qs, ks, _NT), 0.0)
            datt = jnp.where(mask, _dot(dov, v, _NT), 0.0)
            dqs = _dot(dov, st) + _dot(datt, ks)
            dks = _dot(datt, qs, _TN)
            dv_ref[0, :, vsl] = _dot(att, dov, _TN) + _dot(kd, dst, _NT)
            dkd = _dot(v, dst)
            e = jnp.exp(bl)
            dbl = e * _rowsum(st * dst) + _rowsum(dkd * kd)
            dst_ref[h] = _dot(dov, qs, _TN) + dst * e
            dq_ref[0, :, ksl] = dqs * eq * Q_SCALE
            dk_ref[0, :, ksl] = dks * ek + dkd * ed
            db = dqs * qs - dks * ks - dkd * kd
            dla_ref[:, ksl] = _dot_hi(mf, db, _TN) + dbl

        @pl.when((d == 1) & (b == nb - 1) & (ip == nst - 1))
        def _():
            sc.finish()

    rev = lambda f: (lambda d, b, ip: f(d, b, nst - 1 - ip))
    any_spec = pl.BlockSpec(memory_space=pl.ANY)
    return pl.pallas_call(
        body, name="gla_scan_bwd", grid=(2, nb, nst),
        in_specs=[
            pl.BlockSpec((CH, KEY), rev(lambda d, b, i: (rowblk(d, b, i), 0))),
            pl.BlockSpec((CH, KEY), rev(lambda d, b, i: (rowblk(d, b, i), 1))),
            pl.BlockSpec((CH, D), rev(lambda d, b, i: (rowblk(d, b, i), 1))),
            pl.BlockSpec((CH, KEY), rev(lambda d, b, i: (rowblk(d, b, i), d))),
            pl.BlockSpec((CH, D), rev(lambda d, b, i: (mainblk(d, b, i), 0))),
            pl.BlockSpec((1, 1, 1, HEADS, DV, DK), rev(lambda d, b, i: (d, b, i, 0, 0, 0))),
            any_spec,
        ],
        out_specs=[
            pl.BlockSpec((1, CH, KEY), rev(lambda d, b, i: (d, rowblk(d, b, i), 0))),
            pl.BlockSpec((1, CH, KEY), rev(lambda d, b, i: (d, rowblk(d, b, i), 0))),
            pl.BlockSpec((1, CH, D), rev(lambda d, b, i: (d, rowblk(d, b, i), 0))),
            pl.BlockSpec((CH, KEY), rev(lambda d, b, i: (rowblk(d, b, i), d))),
            any_spec,
        ],
        out_shape=[jax.ShapeDtypeStruct((2, ntot, KEY), F32), jax.ShapeDtypeStruct((2, ntot, KEY), F32),
                   jax.ShapeDtypeStruct((2, ntot, D), F32), jax.ShapeDtypeStruct((ntot, 2 * KEY), F32),
                   jax.ShapeDtypeStruct(p16.shape, p16.dtype)],
        scratch_shapes=[pltpu.VMEM((HEADS, DV, DK), F32)] + list(_Scatter.SEMS),
        compiler_params=_params(("arbitrary", "arbitrary", "arbitrary")),
    )(p_all, p_all, p_all, la_all, do, ss, p16)


def _gla_post_fwd(o2, p_all, head_gain, n):
    tt = _tile(n, 256)

    def body(o_ref, g_ref, hg_ref, y_ref):
        o = o_ref[0] + o_ref[1]
        gv = g_ref[...]
        hg = hg_ref[...]
        for h in range(HEADS):
            oh = o[:, h * DV:(h + 1) * DV]
            r = lax.rsqrt(jnp.mean(oh * oh, axis=-1, keepdims=True) + EPS)
            y_ref[:, h * DV:(h + 1) * DV] = ((oh * r) * hg * _silu(gv[:, h * DV:(h + 1) * DV])).astype(BF16)

    return pl.pallas_call(
        body, name="gla_post_fwd", grid=(n // tt,),
        in_specs=[pl.BlockSpec((2, tt, D), lambda i: (0, i, 0)), pl.BlockSpec((tt, D), lambda i: (i, 2)),
                  pl.BlockSpec((1, DV), lambda i: (0, 0))],
        out_specs=pl.BlockSpec((tt, D), lambda i: (i, 0)),
        out_shape=jax.ShapeDtypeStruct((n, D), BF16),
        compiler_params=_params(("parallel",)),
    )(o2, p_all, head_gain)


def _gla_post_bwd(o2, p_all, head_gain, dyb, n):
    tt = _tile(n, 256)

    def body(o_ref, g_ref, hg_ref, dy_ref, do_ref, dg_ref, dhg_ref):
        i = pl.program_id(0)

        @pl.when(i == 0)
        def _():
            dhg_ref[...] = jnp.zeros_like(dhg_ref)

        o = o_ref[0] + o_ref[1]
        gv = g_ref[...]
        hg = hg_ref[...]
        dy = dy_ref[...]
        acc = jnp.zeros((1, DV), F32)
        for h in range(HEADS):
            sl = slice(h * DV, (h + 1) * DV)
            oh = o[:, sl]
            r = lax.rsqrt(jnp.mean(oh * oh, axis=-1, keepdims=True) + EPS)
            on = oh * r
            gh = gv[:, sl]
            dyh = dy[:, sl]
            dg_ref[:, sl] = dyh * (on * hg) * _dsilu(gh)
            dog = dyh * _silu(gh)
            acc = acc + _rowsum(dog * on)
            don = dog * hg
            do_ref[:, sl] = r * (don - on * jnp.mean(don * on, axis=-1, keepdims=True))
        dhg_ref[...] += acc

    return pl.pallas_call(
        body, name="gla_post_bwd", grid=(n // tt,),
        in_specs=[pl.BlockSpec((2, tt, D), lambda i: (0, i, 0)), pl.BlockSpec((tt, D), lambda i: (i, 2)),
                  pl.BlockSpec((1, DV), lambda i: (0, 0)), pl.BlockSpec((tt, D), lambda i: (i, 0))],
        out_specs=[pl.BlockSpec((tt, D), lambda i: (i, 0)), pl.BlockSpec((tt, D), lambda i: (i, 0)),
                   pl.BlockSpec((1, DV), lambda i: (0, 0))],
        out_shape=[jax.ShapeDtypeStruct((n, D), F32), jax.ShapeDtypeStruct((n, D), F32),
                   jax.ShapeDtypeStruct((1, DV), F32)],
        compiler_params=_params(("arbitrary",)),
    )(o2, p_all, head_gain, dyb)


def _gla_assemble(p_all, w2, b2, dq, dk, dv, dla, dgate, n):
    ntot = p_all.shape[0]
    tt = _tile(n, 128)
    nmain = n // tt
    assert ntot % tt == 0

    def body(a_ref, w_ref, b_ref, dq_ref, dk_ref, dv_ref, dla_ref, dg_ref, dp_ref, dw_ref, db_ref):
        i = pl.program_id(0)

        @pl.when(i == 0)
        def _():
            dw_ref[...] = jnp.zeros_like(dw_ref)
            db_ref[...] = jnp.zeros_like(db_ref)

        a = a_ref[...]
        w = w_ref[...]
        z = _dot(a, w) + b_ref[...]
        dz = dla_ref[...] * (1.0 / (1.0 + jnp.exp(z))) * (1.0 / TAU)
        dw_ref[...] += _dot(a, dz, _TN)
        db_ref[...] += _rowsum(dz)
        dp_ref[:, 0:KEY] = ((dq_ref[0] + dq_ref[1]) * 1.0).astype(BF16)
        dp_ref[:, KEY:2 * KEY] = (dk_ref[0] + dk_ref[1]).astype(BF16)
        dp_ref[:, 2 * KEY:2 * KEY + D] = (dv_ref[0] + dv_ref[1]).astype(BF16)
        dp_ref[:, 2 * KEY + D:2 * KEY + 2 * D] = (dg_ref[...] * jnp.where(i < nmain, 1.0, 0.0)).astype(BF16)
        dp_ref[:, 2 * KEY + 2 * D:GLA_IN_PAD] = _dot(dz, w, _NT).astype(BF16)

    return pl.pallas_call(
        body, name="gla_assemble", grid=(ntot // tt,),
        in_specs=[pl.BlockSpec((tt, 128), lambda i: (i, (2 * KEY + 2 * D) // 128)),
                  pl.BlockSpec((128, 2 * KEY), lambda i: (0, 0)), pl.BlockSpec((1, 2 * KEY), lambda i: (0, 0)),
                  pl.BlockSpec((2, tt, KEY), lambda i: (0, i, 0)), pl.BlockSpec((2, tt, KEY), lambda i: (0, i, 0)),
                  pl.BlockSpec((2, tt, D), lambda i: (0, i, 0)), pl.BlockSpec((tt, 2 * KEY), lambda i: (i, 0)),
                  pl.BlockSpec((tt, D), lambda i: (jnp.minimum(i, nmain - 1), 0))],
        out_specs=[pl.BlockSpec((tt, GLA_IN_PAD), lambda i: (i, 0)), pl.BlockSpec((128, 2 * KEY), lambda i: (0, 0)),
                   pl.BlockSpec((1, 2 * KEY), lambda i: (0, 0))],
        out_shape=[jax.ShapeDtypeStruct((ntot, GLA_IN_PAD), BF16), jax.ShapeDtypeStruct((128, 2 * KEY), F32),
                   jax.ShapeDtypeStruct((1, 2 * KEY), F32)],
        compiler_params=_params(("arbitrary",)),
    )(p_all, w2, b2, dq, dk, dv, dla, dgate)


ADA_ROWS = 24
ADA_SH = N_MOD * D // N_CHIPS


def _ada_fwd(cvec, ada_w, ada_b_sh):
    def body(c_ref, w_ref, b_ref, o_ref):
        o_ref[0] = _dot(_silu(c_ref[...]), w_ref[0]) + b_ref[0]

    return pl.pallas_call(
        body, name="ada_fwd", grid=(2,),
        in_specs=[pl.BlockSpec((ADA_ROWS, D), lambda l: (0, 0)), pl.BlockSpec((1, D, ADA_SH), lambda l: (l, 0, 0)),
                  pl.BlockSpec((1, 1, ADA_SH), lambda l: (l, 0, 0))],
        out_specs=pl.BlockSpec((1, ADA_ROWS, ADA_SH), lambda l: (l, 0, 0)),
        out_shape=jax.ShapeDtypeStruct((2, ADA_ROWS, ADA_SH), F32),
        compiler_params=_params(("parallel",)),
    )(cvec, ada_w, ada_b_sh)


def _ada_bwd(cvec, ada_w, dmod_sh):
    def body(c_ref, w_ref, dm_ref, gw_ref, dc_ref):
        dm = dm_ref[0]
        gw_ref[0] = _dot(_silu(c_ref[...]), dm, _TN)
        dc_ref[0] = _dot(dm, w_ref[0], _NT)

    return pl.pallas_call(
        body, name="ada_bwd", grid=(2,),
        in_specs=[pl.BlockSpec((ADA_ROWS, D), lambda l: (0, 0)), pl.BlockSpec((1, D, ADA_SH), lambda l: (l, 0, 0)),
                  pl.BlockSpec((1, ADA_ROWS, ADA_SH), lambda l: (l, 0, 0))],
        out_specs=[pl.BlockSpec((1, D, ADA_SH), lambda l: (l, 0, 0)), pl.BlockSpec((1, ADA_ROWS, D), lambda l: (l, 0, 0))],
        out_shape=[jax.ShapeDtypeStruct((2, D, ADA_SH), F32), jax.ShapeDtypeStruct((2, ADA_ROWS, D), F32)],
        compiler_params=_params(("parallel",)),
    )(cvec, ada_w, dmod_sh)


def _sum_slots(x, name):
    s, r, _ = x.shape

    def body(x_ref, o_ref):
        acc = x_ref[0]
        for k in range(1, s):
            acc = acc + x_ref[k]
        o_ref[...] = acc

    return pl.pallas_call(
        body, name=name, out_shape=jax.ShapeDtypeStruct((r, 128), F32),
        in_specs=[pl.BlockSpec(memory_space=pltpu.VMEM)], out_specs=pl.BlockSpec(memory_space=pltpu.VMEM),
    )(x)


def _cctx_grad(dscc_parts, c_ctx):
    def body(p_ref, c_ref, o_ref):
        acc = p_ref[0]
        for k in range(1, N_CHIPS):
            acc = acc + p_ref[k]
        o_ref[...] = acc * _dsilu(c_ref[...])

    return pl.pallas_call(
        body, name="cctx_grad", out_shape=jax.ShapeDtypeStruct((8, 128), F32),
        in_specs=[pl.BlockSpec(memory_space=pltpu.VMEM)] * 2, out_specs=pl.BlockSpec(memory_space=pltpu.VMEM),
    )(dscc_parts, c_ctx)


def _adamw(w, g, m, v, name):
    r, cdim = w.shape
    tr = _tile(r, 256)
    c1 = 1.0 - ADAM_B1 ** ADAM_STEP
    c2 = 1.0 - ADAM_B2 ** ADAM_STEP

    def body(w_ref, g_ref, m_ref, v_ref, d_ref, mo_ref, vo_ref):
        gv = g_ref[...]
        mn = ADAM_B1 * m_ref[...] + (1.0 - ADAM_B1) * gv
        vn = ADAM_B2 * v_ref[...] + (1.0 - ADAM_B2) * (gv * gv)
        mo_ref[...] = mn
        vo_ref[...] = vn
        d_ref[...] = -ADAM_LR * ((mn / c1) / (jnp.sqrt(vn / c2) + ADAM_EPS) + ADAM_WD * w_ref[...])

    spec = pl.BlockSpec((tr, cdim), lambda i: (i, 0))
    sds = jax.ShapeDtypeStruct((r, cdim), F32)
    return pl.pallas_call(
        body, name=name, grid=(r // tr,), in_specs=[spec] * 4, out_specs=[spec] * 3, out_shape=[sds] * 3,
        compiler_params=_params(("parallel",)),
    )(w, g, m, v)


def _place():
    x, y, c = lax.axis_index("x"), lax.axis_index("y"), lax.axis_index("c")
    return x, y, c


def _allgather_small(blk, name):
    m_per, n = blk.shape

    def body(x_ref, out_ref, send_sems, recv_sems, local_sem):
        x, y, c = _place()
        me, sibling = (x, y, c), (x, y, 1 - c)
        chips = [(1 - x, y), (x, 1 - y), (1 - x, 1 - y)]

        def rows(px, py, pc):
            return out_ref.at[pl.ds((4 * px + 2 * py + pc) * m_per, m_per), :]

        def copy(k, block, to, src=None):
            return pltpu.make_async_remote_copy(
                src_ref=rows(*block) if src is None else src, dst_ref=rows(*block),
                send_sem=send_sems.at[k], recv_sem=recv_sems.at[k], device_id=to, device_id_type=MESH)

        mine = pltpu.make_async_copy(x_ref, rows(*me), local_sem)
        mine.start()
        first = [copy(0, me, sibling, src=x_ref)]
        first += [copy(1 + j, me, (*chip, c), src=x_ref) for j, chip in enumerate(chips)]
        for cp in first:
            cp.start()
        passed = [copy(4 + j, (*chip, c), sibling) for j, chip in enumerate(chips)]
        for j, chip in enumerate(chips):
            copy(1 + j, (*chip, c), me).wait_recv()
            passed[j].start()
        copy(0, sibling, me).wait_recv()
        for j, chip in enumerate(chips):
            copy(4 + j, (*chip, 1 - c), me).wait_recv()
        for cp in first + passed:
            cp.wait_send()
        mine.wait()

    return pl.pallas_call(
        body, name=name,
        out_shape=jax.ShapeDtypeStruct((N_DEV * m_per, n), blk.dtype),
        in_specs=[pl.BlockSpec(memory_space=pltpu.VMEM)],
        out_specs=pl.BlockSpec(memory_space=pltpu.VMEM),
        scratch_shapes=[pltpu.SemaphoreType.DMA((7,)), pltpu.SemaphoreType.DMA((7,)), pltpu.SemaphoreType.DMA],
    )(blk)


def _other_chips(x, y):
    return [(1 - x, y), (x, 1 - y), (1 - x, 1 - y)]


class _AllGather:
    SEMS = [pltpu.SemaphoreType.DMA((3,)), pltpu.SemaphoreType.DMA((3,)), pltpu.SemaphoreType.DMA((3,)),
            pltpu.SemaphoreType.DMA((3,)), pltpu.SemaphoreType.DMA((2,))]

    def __init__(self, own_ref, out_ref, sems):
        self.own_ref, self.out_ref = own_ref, out_ref
        self.send_sems, self.recv_sems, self.fsend_sems, self.frecv_sems, self.own_sems = sems
        self.hr = own_ref.shape[0] // 2

    def _half(self, ch, cc):
        return self.out_ref.at[ch, pl.ds(cc * self.hr, self.hr), :]

    def _own_slot(self):
        x, y, c = _place()
        return pltpu.make_async_remote_copy(
            src_ref=self.own_ref, dst_ref=self.out_ref.at[2 * x + y], send_sem=self.own_sems.at[0],
            recv_sem=self.own_sems.at[1], device_id=(x, y, 1 - c), device_id_type=MESH)

    def _send(self, j, ox, oy):
        x, y, c = _place()
        return pltpu.make_async_remote_copy(
            src_ref=self.own_ref.at[pl.ds(c * self.hr, self.hr), :], dst_ref=self._half(2 * x + y, c),
            send_sem=self.send_sems.at[j], recv_sem=self.recv_sems.at[j], device_id=(ox, oy, c),
            device_id_type=MESH)

    def _landed(self, j, ox, oy):
        x, y, c = _place()
        ref = self._half(2 * ox + oy, c)
        return pltpu.make_async_remote_copy(
            src_ref=ref, dst_ref=ref, send_sem=self.send_sems.at[j], recv_sem=self.recv_sems.at[j],
            device_id=(ox, oy, c), device_id_type=MESH)

    def _pass_on(self, j, ox, oy, cc):
        x, y, c = _place()
        ref = self._half(2 * ox + oy, cc)
        return pltpu.make_async_remote_copy(
            src_ref=ref, dst_ref=ref, send_sem=self.fsend_sems.at[j], recv_sem=self.frecv_sems.at[j],
            device_id=(x, y, 1 - c), device_id_type=MESH)

    def start(self):
        x, y, c = _place()
        self._own_slot().start()
        for j, (ox, oy) in enumerate(_other_chips(x, y)):
            self._send(j, ox, oy).start()

    def finish(self):
        x, y, c = _place()
        others = _other_chips(x, y)
        for j, (ox, oy) in enumerate(others):
            self._landed(j, ox, oy).wait_recv()
            self._pass_on(j, ox, oy, c).start()
        for j, (ox, oy) in enumerate(others):
            self._pass_on(j, ox, oy, 1 - c).wait_recv()
        for j, (ox, oy) in enumerate(others):
            self._send(j, ox, oy).wait_send()
            self._pass_on(j, ox, oy, c).wait_send()
        self._own_slot().wait()


def _weights_allgather(own, name):
    def body(own_ref, out_ref, *sems):
        ag = _AllGather(own_ref, out_ref, sems)
        ag.start()
        ag.finish()

    any_spec = pl.BlockSpec(memory_space=pl.ANY)
    return pl.pallas_call(
        body, name=name,
        out_shape=jax.ShapeDtypeStruct((N_CHIPS, own.shape[0], D), own.dtype),
        in_specs=[any_spec], out_specs=any_spec, scratch_shapes=list(_AllGather.SEMS),
    )(own)


def _rs_pair_exchange(g, name):
    r = g.shape[1]
    hr = r // 2

    def body(g_ref, got_ref, send_sem, recv_sem):
        x, y, c = _place()
        cp = pltpu.make_async_remote_copy(
            src_ref=g_ref.at[:, pl.ds((1 - c) * hr, hr), :], dst_ref=got_ref, send_sem=send_sem, recv_sem=recv_sem,
            device_id=(x, y, 1 - c), device_id_type=MESH)
        cp.start()
        cp.wait()

    any_spec = pl.BlockSpec(memory_space=pl.ANY)
    return pl.pallas_call(
        body, name=name,
        out_shape=jax.ShapeDtypeStruct((N_CHIPS, hr, D), F32),
        in_specs=[any_spec], out_specs=any_spec,
        scratch_shapes=[pltpu.SemaphoreType.DMA, pltpu.SemaphoreType.DMA],
    )(g)


def _rs_chip_sum(place, g, got, name):
    r = g.shape[1]
    hr = r // 2
    tr = _tile(hr, 640, 16)
    nt = hr // tr

    def body(pl_ref, g_ref, got_ref, p16_ref, p32_ref):
        s = pl.program_id(1)
        p = g_ref[0] + got_ref[0]
        p16_ref[0] = p.astype(BF16)

        @pl.when(s == pl_ref[1])
        def _():
            p32_ref[...] = p

    return pl.pallas_call(
        body, name=name,
        grid_spec=pltpu.PrefetchScalarGridSpec(
            num_scalar_prefetch=1, grid=(nt, N_CHIPS),
            in_specs=[pl.BlockSpec((1, tr, D), lambda i, s, pr: (s, pr[0] * nt + i, 0)),
                      pl.BlockSpec((1, tr, D), lambda i, s, pr: (s, i, 0))],
            out_specs=[pl.BlockSpec((1, tr, D), lambda i, s, pr: (s, i, 0)),
                       pl.BlockSpec((tr, D), lambda i, s, pr: (i, 0))]),
        out_shape=[jax.ShapeDtypeStruct((N_CHIPS, hr, D), BF16), jax.ShapeDtypeStruct((hr, D), F32)],
        compiler_params=_params(("parallel", "arbitrary")),
    )(place, g, got)


class _Scatter:
    SEMS = [pltpu.SemaphoreType.DMA((3,)), pltpu.SemaphoreType.DMA((3,))]

    def __init__(self, p_ref, out_ref, sems):
        self.p_ref, self.out_ref = p_ref, out_ref
        self.send_sems, self.recv_sems = sems

    def _copy(self, j, ox, oy, src_slot, dst_slot):
        x, y, c = _place()
        return pltpu.make_async_remote_copy(
            src_ref=self.p_ref.at[src_slot], dst_ref=self.out_ref.at[dst_slot], send_sem=self.send_sems.at[j],
            recv_sem=self.recv_sems.at[j], device_id=(ox, oy, c), device_id_type=MESH)

    def start(self):
        x, y, c = _place()
        for j, (ox, oy) in enumerate(_other_chips(x, y)):
            self._copy(j, ox, oy, 2 * ox + oy, 2 * x + y).start()

    def finish(self):
        x, y, c = _place()
        others = _other_chips(x, y)
        for j, (ox, oy) in enumerate(others):
            self._copy(j, ox, oy, 2 * ox + oy, 2 * ox + oy).wait_recv()
        for j, (ox, oy) in enumerate(others):
            self._copy(j, ox, oy, 2 * ox + oy, 2 * x + y).wait_send()


def _rs_scatter(p16, name):
    def body(p_ref, out_ref, *sems):
        sc = _Scatter(p_ref, out_ref, sems)
        sc.start()
        sc.finish()

    any_spec = pl.BlockSpec(memory_space=pl.ANY)
    return pl.pallas_call(
        body, name=name,
        out_shape=jax.ShapeDtypeStruct(p16.shape, p16.dtype),
        in_specs=[any_spec], out_specs=any_spec, scratch_shapes=list(_Scatter.SEMS),
    )(p16)


def _rs_final_sum(place, parts, p32, name):
    hr = parts.shape[1]
    tr = _tile(hr, 640, 16)
    nt = hr // tr

    def body(pl_ref, a_ref, b_ref, c_ref, p32_ref, o_ref):
        o_ref[...] = ((p32_ref[...] + a_ref[0].astype(F32)) + b_ref[0].astype(F32)) + c_ref[0].astype(F32)

    def other(j):
        return pl.BlockSpec((1, tr, D), lambda i, pr: (j + jnp.where(pr[1] <= j, 1, 0), i, 0))

    return pl.pallas_call(
        body, name=name,
        grid_spec=pltpu.PrefetchScalarGridSpec(
            num_scalar_prefetch=1, grid=(nt,),
            in_specs=[other(0), other(1), other(2), pl.BlockSpec((tr, D), lambda i, pr: (i, 0))],
            out_specs=pl.BlockSpec((tr, D), lambda i, pr: (pr[0] * nt + i, 0))),
        out_shape=jax.ShapeDtypeStruct((2 * hr, D), F32),
        compiler_params=_params(("parallel",)),
    )(place, parts, parts, parts, p32)


def _rs_pair_gather(both, name):
    hr = both.shape[0] // 2

    def body(in_ref, out_ref, send_sem, recv_sem):
        x, y, c = _place()
        mine = out_ref.at[pl.ds(c * hr, hr), :]
        cp = pltpu.make_async_remote_copy(
            src_ref=mine, dst_ref=mine, send_sem=send_sem, recv_sem=recv_sem,
            device_id=(x, y, 1 - c), device_id_type=MESH)
        cp.start()
        theirs = out_ref.at[pl.ds((1 - c) * hr, hr), :]
        pltpu.make_async_remote_copy(
            src_ref=theirs, dst_ref=theirs, send_sem=send_sem, recv_sem=recv_sem,
            device_id=(x, y, 1 - c), device_id_type=MESH).wait_recv()
        cp.wait_send()

    any_spec = pl.BlockSpec(memory_space=pl.ANY)
    return pl.pallas_call(
        body, name=name,
        out_shape=jax.ShapeDtypeStruct(both.shape, F32),
        in_specs=[any_spec], out_specs=any_spec, input_output_aliases={0: 0},
        scratch_shapes=[pltpu.SemaphoreType.DMA, pltpu.SemaphoreType.DMA],
    )(both)


def _interleave(w, parts):
    n = w.shape[0] // parts
    return w.reshape(parts, n // 128, 128, *w.shape[1:]).swapaxes(0, 1).reshape(w.shape)


def _deinterleave(w, parts):
    n = w.shape[0] // parts
    return w.reshape(n // 128, parts, 128, *w.shape[1:]).swapaxes(0, 1).reshape(w.shape)


def _local_step(x, ctx, tgt, mods, mc, gla_in_t, own_main, place, small):
    nb, t, _ = x.shape
    tc = ctx.shape[1]
    n = nb * t
    nc = nb * tc
    xf = x.reshape(n, D)
    cf = ctx.reshape(nc, D)
    tf = tgt.reshape(n, D)
    vec = lambda a: a.reshape(1, -1)
    m = [[mods[l, :, k, :].reshape(nb, 1, D) for k in range(N_MOD)] for l in range(2)]
    mc_b = [jnp.broadcast_to(mc[k].reshape(1, 1, D), (nb, 1, D)) for k in range(2)]

    w_gin = jnp.pad(gla_in_t, ((0, GLA_IN_PAD - GLA_IN), (0, 0)))
    cw = [_interleave(small["ffn_conv_w"][l].T, 2).T for l in range(2)]
    cb = [_interleave(small["ffn_conv_b"][l], 2).reshape(1, -1) for l in range(2)]
    w2 = jnp.zeros((128, 2 * KEY), F32)
    w2 = w2.at[0:RANK, 0:KEY].set(small["gla_w_a2"][0]).at[RANK:2 * RANK, KEY:].set(small["gla_w_a2"][1])
    b2 = small["gla_b_a"].reshape(1, 2 * KEY)
    hg = small["gla_head_norm"].reshape(1, DV)

    hn0 = _mod_fwd(xf, vec(small["norm_mix"][0]), m[0][0], m[0][1], t, "mod0_main")
    hnc = _mod_fwd(cf, vec(small["norm_mix"][0]), mc_b[0], mc_b[1], tc, "mod0_ctx")
    hn_all = jnp.concatenate([hn0, hnc], axis=0)
    p_all = _mm(hn_all, w_gin, "nt", F32, "gla_in_proj", 768, 3200)
    la_all = _gla_decay_fwd(p_all, w2, b2)
    o2, ss, wg = _gla_scan_fwd(p_all, la_all, nb, t, tc, own_main)
    offs = _offsets(_MAIN, _MAIN_ROWS)
    full = {k: wg[:, offs[k]:offs[k] + _MAIN_ROWS[k], :].reshape(N_CHIPS * _MAIN_ROWS[k], D) for k in _MAIN}
    wts = {"gla_out": full["gla_out"], "sc_out": full["sc_out"], "down": [full["down0"], full["down1"]]}
    w_sin = _interleave(full["sc_in_t"], 3)
    w_up = [_interleave(full["up_t0"], 2), _interleave(full["up_t1"], 2)]
    yb0 = _gla_post_fwd(o2, p_all, hg, n)
    y0 = _mm(yb0, wts["gla_out"], "nn", F32, "gla_out_proj", 1024, 1024)
    h1, hn1 = _mod_fwd(xf, vec(small["norm_ffn"][0]), m[0][3], m[0][4], t, "mod0_ffn", y=y0, gate=m[0][2])
    u0 = _mm(hn1, w_up[0], "nt", F32, "ffn0_up", 1024, 1280)
    z0 = _ffn_mid_fwd(u0, cw[0], cb[0], nb, t, "ffn0_mid_fwd")
    f0 = _mm(z0, wts["down"][0], "nn", F32, "ffn0_down", 1024, 1024)
    h2, hn2 = _mod_fwd(h1, vec(small["norm_mix"][1]), m[1][0], m[1][1], t, "mod1_mix", y=f0, gate=m[0][5])
    p1 = _mm(hn2, w_sin, "nt", F32, "sc_in_proj", 1024, 1536)
    yb1 = _sc_mid_fwd(p1, small["sc_conv_w"], nb, t)
    y1 = _mm(yb1, wts["sc_out"], "nn", F32, "sc_out_proj", 1024, 1024)
    h3, hn3 = _mod_fwd(h2, vec(small["norm_ffn"][1]), m[1][3], m[1][4], t, "mod1_ffn", y=y1, gate=m[1][2])
    u1 = _mm(hn3, w_up[1], "nt", F32, "ffn1_up", 1024, 1280)
    z1 = _ffn_mid_fwd(u1, cw[1], cb[1], nb, t, "ffn1_mid_fwd")
    f1 = _mm(z1, wts["down"][1], "nn", F32, "ffn1_down", 1024, 1024)
    loss, dh4, df1, dm15, dfinal = _final(h3, f1, m[1][5], vec(small["final_norm"]), tf, t)

    gb, gs = {}, {}
    dmods = [[None] * N_MOD for _ in range(2)]
    dmods[1][5] = dm15

    def ffn_bwd(l, df, u, z, hn, tag):
        dz = _mm(df, wts["down"][l], "nt", F32, f"ffn{l}_down_dx", 1024, 1280)
        gdown = _mm(z, df, "tn", F32, f"ffn{l}_down_dw", 640, 1024)
        du, dcw, dcb = _ffn_mid_bwd(u, cw[l], cb[l], dz, nb, t, f"ffn{l}_mid_bwd")
        dhn = _mm(du, w_up[l], "nn", F32, f"ffn{l}_up_dx", 512, 512)
        gup = _deinterleave(_mm(du, hn, "tn", F32, f"ffn{l}_up_dw", 640, 1024), 2)
        return dhn, gdown, gup, _deinterleave(dcw.T, 2).T, _deinterleave(dcb.reshape(-1), 2)

    dhn3, gdown1, gup1, dcw1, dcb1 = ffn_bwd(1, df1, u1, z1, hn3, "ffn1")
    r = _mod_bwd(h3, dhn3, vec(small["norm_ffn"][1]), m[1][4], t, "mod1_ffn_bwd", dh_out=dh4, y_prev=y1,
                 gate_prev=m[1][2])
    dh3, dmods[1][4], dmods[1][3], dnf1, dy1, dmods[1][2] = (r["dh"], r["dscale"], r["dshift"], r["dgain"],
                                                             r["dy_prev"], r["dgate_prev"])
    dyb1 = _mm(dy1, wts["sc_out"], "nt", F32, "sc_out_dx", 1024, 1024)
    gb["sc_out"] = _mm(yb1, dy1, "tn", F32, "sc_out_dw", 512, 1024)
    dp1, dscw = _sc_mid_bwd(p1, small["sc_conv_w"], dyb1, nb, t)
    dhn2 = _mm(dp1, w_sin, "nn", F32, "sc_in_dx", 1024, 512)
    gb["sc_in_t"] = _deinterleave(_mm(dp1, hn2, "tn", F32, "sc_in_dw", 768, 1024), 3)
    r = _mod_bwd(h2, dhn2, vec(small["norm_mix"][1]), m[1][1], t, "mod1_mix_bwd", dh_out=dh3, y_prev=f0,
                 gate_prev=m[0][5])
    dh2, dmods[1][1], dmods[1][0], dnm1, df0, dmods[0][5] = (r["dh"], r["dscale"], r["dshift"], r["dgain"],
                                                             r["dy_prev"], r["dgate_prev"])
    dhn1, gdown0, gup0, dcw0, dcb0 = ffn_bwd(0, df0, u0, z0, hn1, "ffn0")
    r = _mod_bwd(h1, dhn1, vec(small["norm_ffn"][0]), m[0][4], t, "mod0_ffn_bwd", dh_out=dh2, y_prev=y0,
                 gate_prev=m[0][2])
    dh1, dmods[0][4], dmods[0][3], dnf0, dy0, dmods[0][2] = (r["dh"], r["dscale"], r["dshift"], r["dgain"],
                                                             r["dy_prev"], r["dgate_prev"])
    dyb0 = _mm(dy0, wts["gla_out"], "nt", F32, "gla_out_dx", 1024, 1024)
    gb["gla_out"] = _mm(yb0, dy0, "tn", F32, "gla_out_dw", 512, 1024)
    gmain = {"sc_in_t": gb["sc_in_t"], "up_t0": gup0, "up_t1": gup1, "gla_out": gb["gla_out"],
             "sc_out": gb["sc_out"], "down0": gdown0, "down1": gdown1}
    g_packed = jnp.concatenate([gmain[k].reshape(N_CHIPS, _MAIN_ROWS[k], D) for k in _MAIN], axis=1)
    from_sibling = _rs_pair_exchange(g_packed, "rs_main_pair_exchange")
    p16, p32 = _rs_chip_sum(place, g_packed, from_sibling, "rs_main_chip_sum")
    do, dgate, dhg = _gla_post_bwd(o2, p_all, hg, dyb0, n)
    dq, dk, dv, dla, landed = _gla_scan_bwd(p_all, la_all, do, ss, nb, t, tc, p16)
    g_main = _rs_pair_gather(_rs_final_sum(place, landed, p32, "rs_main_final_sum"), "rs_main_pair_gather")
    dp, dw2, db2 = _gla_assemble(p_all, w2, b2, dq, dk, dv, dla, dgate, n)
    dhn_all = _mm(dp, w_gin, "nn", F32, "gla_in_dx", 768, 512)
    g_gin = _mm(dp, hn_all, "tn", F32, "gla_in_dw", 640, 1024)[:GLA_IN]
    r = _mod_bwd(xf, dhn_all, vec(small["norm_mix"][0]), m[0][1], t, "mod0_main_bwd", dh_out=dh1)
    grad_x, dmods[0][1], dmods[0][0], dnm0 = r["dh"], r["dscale"], r["dshift"], r["dgain"]
    rc = _mod_bwd(cf, dhn_all, vec(small["norm_mix"][0]), mc_b[1], tc, "mod0_ctx_bwd", dhn_row0=n, need_dh=False)
    dmc = jnp.stack([jnp.sum(rc["dshift"], axis=0).reshape(D), jnp.sum(rc["dscale"], axis=0).reshape(D)])
    dnm0 = dnm0 + rc["dgain"]

    gs["norm_mix"] = jnp.concatenate([dnm0, dnm1], axis=0)
    gs["norm_ffn"] = jnp.concatenate([dnf0, dnf1], axis=0)
    gs["final_norm"] = dfinal.reshape(D)
    gs["gla_w_a2"] = jnp.stack([dw2[0:RANK, 0:KEY], dw2[RANK:2 * RANK, KEY:]])
    gs["gla_b_a"] = db2.reshape(2, KEY)
    gs["gla_head_norm"] = dhg.reshape(DV)
    gs["sc_conv_w"] = dscw
    gs["ffn_conv_w"] = jnp.stack([dcw0, dcw1])
    gs["ffn_conv_b"] = jnp.stack([dcb0, dcb1])
    dmods_arr = jnp.stack([jnp.stack([dmods[l][k].reshape(nb, D) for k in range(N_MOD)], axis=1) for l in range(2)])
    return loss, grad_x.reshape(nb, t, D), g_main, g_gin, gs, dmods_arr, dmc


def _pack(arrs):
    parts, meta, off = [], [], 0
    for a in arrs:
        r = a.size // 128
        rp = -(-r // 8) * 8
        a2 = a.reshape(r, 128).astype(F32)
        if rp != r:
            a2 = jnp.pad(a2, ((0, rp - r), (0, 0)))
        parts.append(a2)
        meta.append((off, r, a.shape))
        off += rp
    return jnp.concatenate(parts, axis=0), meta


def _unpack(buf, meta, lead=()):
    return [buf[..., off:off + r, :].reshape(*lead, *shape) for off, r, shape in meta]


_MAIN = ("up_t0", "up_t1", "down0", "down1", "sc_in_t", "gla_out", "sc_out")
_MAIN_ROWS = {"sc_in_t": 3 * D // N_CHIPS, "up_t0": 2 * HID // N_CHIPS, "up_t1": 2 * HID // N_CHIPS,
              "gla_out": D // N_CHIPS, "sc_out": D // N_CHIPS, "down0": HID // N_CHIPS, "down1": HID // N_CHIPS}
_MAIN_TOTAL = sum(_MAIN_ROWS.values())
_GIN_ROWS = GLA_IN // N_CHIPS
_GIN_PAD = -(-_GIN_ROWS // 32) * 32


def _offsets(names, rows):
    off, out = 0, {}
    for k in names:
        out[k] = off
        off += rows[k]
    return out


def kernel(x, c, ctx, c_ctx, ada_w, ada_b, norm_mix, norm_ffn, gla_w_in, gla_w_a2, gla_b_a, gla_head_norm, gla_w_out, sc_w_in, sc_conv_w, sc_w_out, ffn_w_up, ffn_conv_w, ffn_conv_b, ffn_w_down, final_norm, loss_target, m_c_ctx, m_ada_w, m_ada_b, m_norm_mix, m_norm_ffn, m_gla_w_in, m_gla_w_a2, m_gla_b_a, m_gla_head_norm, m_gla_w_out, m_sc_w_in, m_sc_conv_w, m_sc_w_out, m_ffn_w_up, m_ffn_conv_w, m_ffn_conv_b, m_ffn_w_down, m_final_norm, v_c_ctx, v_ada_w, v_ada_b, v_norm_mix, v_norm_ffn, v_gla_w_in, v_gla_w_a2, v_gla_b_a, v_gla_head_norm, v_gla_w_out, v_sc_w_in, v_sc_conv_w, v_sc_w_out, v_ffn_w_up, v_ffn_conv_w, v_ffn_conv_b, v_ffn_w_down, v_final_norm):
    ix, iy, ic = _place()
    chip = 2 * ix + iy
    dev = 2 * chip + ic
    place = jnp.stack([ic, chip]).astype(jnp.int32)
    nb = x.shape[0]
    offs = _offsets(_MAIN, _MAIN_ROWS)

    buf, meta = _pack([c, ffn_conv_w, sc_conv_w, gla_w_a2, gla_b_a])
    got = _allgather_small(buf, "gather_small_in").reshape(N_DEV, buf.shape[0], 128)
    c_all, fcw, scw, wa2, ba = _unpack(got, meta, (N_DEV,))
    c_all = c_all.reshape(N_DEV * nb, D)
    per_chip = lambda a: a[0::2]
    ffn_conv_w_full = jnp.moveaxis(per_chip(fcw), 0, 2).reshape(2, 3, 2 * HID)
    sc_conv_w_full = jnp.moveaxis(per_chip(scw)[:, 0], 0, 1).reshape(3, D)
    gla_w_a2_full = jnp.moveaxis(per_chip(wa2)[:, 0], 0, 2).reshape(2, RANK, KEY)
    gla_b_a_full = jnp.moveaxis(per_chip(ba)[:, 0], 0, 1).reshape(2, KEY)

    own = {"sc_in_t": sc_w_in[0].T, "up_t0": ffn_w_up[0].T, "up_t1": ffn_w_up[1].T,
           "gla_out": gla_w_out[0], "sc_out": sc_w_out[0], "down0": ffn_w_down[0], "down1": ffn_w_down[1]}
    own_main = jnp.concatenate([own[k].astype(BF16) for k in _MAIN], axis=0)
    own_gin = jnp.pad(gla_w_in[0].T.astype(BF16), ((0, _GIN_PAD - _GIN_ROWS), (0, 0)))
    gla_in_t = _weights_allgather(own_gin, "allgather_gla_in")[:, :_GIN_ROWS, :].reshape(GLA_IN, D)

    cvec = jnp.concatenate([c_all, c_ctx.reshape(1, D), jnp.zeros((ADA_ROWS - N_DEV * nb - 1, D), F32)], axis=0)
    ada_b_sh = lax.dynamic_slice_in_dim(ada_b, chip * ADA_SH, ADA_SH, axis=1).reshape(2, 1, ADA_SH)
    mod_sh = _ada_fwd(cvec, ada_w, ada_b_sh)
    got = _allgather_small(mod_sh.reshape(2 * ADA_ROWS, ADA_SH), "gather_mod")
    mod_full = jnp.moveaxis(per_chip(got.reshape(N_DEV, 2, ADA_ROWS, ADA_SH)), 0, 2).reshape(2, ADA_ROWS, N_MOD * D)
    mods = lax.dynamic_slice_in_dim(mod_full, dev * nb, nb, axis=1).reshape(2, nb, N_MOD, D)
    mc = mod_full[0, N_DEV * nb, :2 * D].reshape(2, D)

    small = {"norm_mix": norm_mix, "norm_ffn": norm_ffn, "final_norm": final_norm, "gla_w_a2": gla_w_a2_full,
             "gla_b_a": gla_b_a_full, "gla_head_norm": gla_head_norm[0], "sc_conv_w": sc_conv_w_full,
             "ffn_conv_w": ffn_conv_w_full, "ffn_conv_b": ffn_conv_b}
    loss_p, grad_x, g_main, g_gin, gs, dmods, dmc = _local_step(x, ctx, loss_target, mods, mc, gla_in_t, own_main,
                                                                place, small)

    sum_names = ["norm_mix", "norm_ffn", "final_norm", "gla_w_a2", "gla_b_a", "gla_head_norm", "sc_conv_w",
                 "ffn_conv_w", "ffn_conv_b"]
    buf, meta = _pack([jnp.broadcast_to(loss_p, (8, 128))] + [gs[k] for k in sum_names] + [dmc, dmods])
    n_sum = meta[-1][0]
    got = _allgather_small(buf, "gather_small_grads").reshape(N_DEV, buf.shape[0], 128)
    summed = _sum_slots(got[:, :n_sum], "sum_small_grads")
    parts = _unpack(summed, meta[:-1])
    loss = parts[0][0, 0]
    g_small = dict(zip(sum_names, parts[1:-1]))
    dmc_tot = parts[-1]
    dmods_all = jnp.moveaxis(_unpack(got, meta[-1:], (N_DEV,))[0], 0, 1).reshape(2, N_DEV * nb, N_MOD * D)

    ctx_row = jnp.stack([jnp.concatenate([dmc_tot.reshape(2 * D), jnp.zeros(((N_MOD - 2) * D,), F32)]),
                         jnp.zeros((N_MOD * D,), F32)]).reshape(2, 1, N_MOD * D)
    dmod_ext = jnp.concatenate([dmods_all, ctx_row, jnp.zeros((2, ADA_ROWS - N_DEV * nb - 1, N_MOD * D), F32)], axis=1)
    g_ada_b = _sum_slots(jnp.moveaxis(dmod_ext, 1, 0).reshape(ADA_ROWS, 2 * N_MOD * D // 128, 128),
                         "sum_ada_b").reshape(2, N_MOD * D)
    dmod_sh = lax.dynamic_slice_in_dim(dmod_ext, chip * ADA_SH, ADA_SH, axis=2)
    g_ada_w, dcv = _ada_bwd(cvec, ada_w, dmod_sh)
    dscc_part = (dcv[0, N_DEV * nb] + dcv[1, N_DEV * nb]).reshape(8, 128)
    got = _allgather_small(dscc_part, "gather_dscc").reshape(N_DEV, 8, 128)
    g_c_ctx = _cctx_grad(per_chip(got), c_ctx.reshape(8, 128)).reshape(D)

    g_packed = jnp.pad(g_gin.reshape(N_CHIPS, _GIN_ROWS, D), ((0, 0), (0, _GIN_PAD - _GIN_ROWS), (0, 0)))
    from_sibling = _rs_pair_exchange(g_packed, "rs_gin_pair_exchange")
    p16, p32 = _rs_chip_sum(place, g_packed, from_sibling, "rs_gin_chip_sum")
    landed = _rs_scatter(p16, "rs_gin_scatter")
    g_gin_shard = _rs_pair_gather(_rs_final_sum(place, landed, p32, "rs_gin_final_sum"), "rs_gin_pair_gather")
    seg = {k: g_main[offs[k]:offs[k] + _MAIN_ROWS[k]] for k in _MAIN}
    seg["gla_in_t"] = g_gin_shard[:_GIN_ROWS]

    sl_chip = lambda a, axis, width: lax.dynamic_slice_in_dim(a, chip * width, width, axis=axis)
    grads = {
        "c_ctx": g_c_ctx, "ada_w": g_ada_w, "ada_b": g_ada_b, "norm_mix": g_small["norm_mix"],
        "norm_ffn": g_small["norm_ffn"],
        "gla_w_in": seg["gla_in_t"].T[None], "gla_w_a2": sl_chip(g_small["gla_w_a2"], 2, KEY // N_CHIPS)[None],
        "gla_b_a": sl_chip(g_small["gla_b_a"], 1, KEY // N_CHIPS)[None],
        "gla_head_norm": g_small["gla_head_norm"][None], "gla_w_out": seg["gla_out"][None],
        "sc_w_in": seg["sc_in_t"].T[None], "sc_conv_w": sl_chip(g_small["sc_conv_w"], 1, D // N_CHIPS)[None],
        "sc_w_out": seg["sc_out"][None], "ffn_w_up": jnp.stack([seg["up_t0"].T, seg["up_t1"].T]),
        "ffn_conv_w": sl_chip(g_small["ffn_conv_w"], 2, 2 * HID // N_CHIPS), "ffn_conv_b": g_small["ffn_conv_b"],
        "ffn_w_down": jnp.stack([seg["down0"], seg["down1"]]), "final_norm": g_small["final_norm"],
    }
    weights = {"c_ctx": c_ctx, "ada_w": ada_w, "ada_b": ada_b, "norm_mix": norm_mix, "norm_ffn": norm_ffn,
               "gla_w_in": gla_w_in, "gla_w_a2": gla_w_a2, "gla_b_a": gla_b_a, "gla_head_norm": gla_head_norm,
               "gla_w_out": gla_w_out, "sc_w_in": sc_w_in, "sc_conv_w": sc_conv_w, "sc_w_out": sc_w_out,
               "ffn_w_up": ffn_w_up, "ffn_conv_w": ffn_conv_w, "ffn_conv_b": ffn_conv_b, "ffn_w_down": ffn_w_down,
               "final_norm": final_norm}
    mom1 = {"c_ctx": m_c_ctx, "ada_w": m_ada_w, "ada_b": m_ada_b, "norm_mix": m_norm_mix, "norm_ffn": m_norm_ffn,
            "gla_w_in": m_gla_w_in, "gla_w_a2": m_gla_w_a2, "gla_b_a": m_gla_b_a, "gla_head_norm": m_gla_head_norm,
            "gla_w_out": m_gla_w_out, "sc_w_in": m_sc_w_in, "sc_conv_w": m_sc_conv_w, "sc_w_out": m_sc_w_out,
            "ffn_w_up": m_ffn_w_up, "ffn_conv_w": m_ffn_conv_w, "ffn_conv_b": m_ffn_conv_b,
            "ffn_w_down": m_ffn_w_down, "final_norm": m_final_norm}
    mom2 = {"c_ctx": v_c_ctx, "ada_w": v_ada_w, "ada_b": v_ada_b, "norm_mix": v_norm_mix, "norm_ffn": v_norm_ffn,
            "gla_w_in": v_gla_w_in, "gla_w_a2": v_gla_w_a2, "gla_b_a": v_gla_b_a, "gla_head_norm": v_gla_head_norm,
            "gla_w_out": v_gla_w_out, "sc_w_in": v_sc_w_in, "sc_conv_w": v_sc_conv_w, "sc_w_out": v_sc_w_out,
            "ffn_w_up": v_ffn_w_up, "ffn_conv_w": v_ffn_conv_w, "ffn_conv_b": v_ffn_conv_b,
            "ffn_w_down": v_ffn_w_down, "final_norm": v_final_norm}
    names = list(weights)
    grads = {k: grads[k].reshape(weights[k].shape) for k in names}

    big_names = ["ada_w", "gla_w_in", "gla_w_out", "sc_w_in", "sc_w_out", "ffn_w_up", "ffn_w_down"]
    small_names = [k for k in names if k not in big_names]
    delta, new_m, new_v = {}, {}, {}
    for k in big_names:
        shp = weights[k].shape
        as2d = lambda a: a.reshape(-1, shp[-1])
        d_, m_, v_ = _adamw(as2d(weights[k]), as2d(grads[k]), as2d(mom1[k]), as2d(mom2[k]), "adamw_" + k)
        delta[k], new_m[k], new_v[k] = d_.reshape(shp), m_.reshape(shp), v_.reshape(shp)
    packed = [_pack([src[k] for k in small_names]) for src in (weights, grads, mom1, mom2)]
    meta = packed[0][1]
    outs = _adamw(packed[0][0], packed[1][0], packed[2][0], packed[3][0], "adamw_small")
    for dst, o in zip((delta, new_m, new_v), outs):
        for k, a in zip(small_names, _unpack(o, meta)):
            dst[k] = a

    return (loss, grad_x, *[grads[k] for k in names], *[delta[k] for k in names], *[new_m[k] for k in names],
            *[new_v[k] for k in names])
```

```python
import functools

import jax
import jax.numpy as jnp
from jax import lax
from jax.experimental import pallas as pl
from jax.experimental.pallas import tpu as pltpu

F32 = jnp.float32
BF16 = jnp.bfloat16
MESH = pl.DeviceIdType.MESH

EPS = 1e-6
D = 1024
N_MOD = 6
HEADS = 4
DK = 128
DV = 256
KEY = HEADS * DK
RANK = 16
TAU = 16.0
CH = 64
GRID_W = 64
HID = 2560
GLA_IN = 2 * KEY + 2 * D + 2 * RANK
GLA_IN_PAD = 3200
Q_SCALE = DK ** -0.5
N_CHIPS = 4
N_DEV = 8

ADAM_LR = 0.001
ADAM_B1 = 0.9
ADAM_B2 = 0.999
ADAM_EPS = 1e-08
ADAM_WD = 0.01
ADAM_STEP = 10

VMEM_LIMIT = 56 * 1024 * 1024


def _params(sem):
    return pltpu.CompilerParams(dimension_semantics=sem, vmem_limit_bytes=VMEM_LIMIT)


def _tile(n, pref, mult=8):
    if n <= pref:
        return n
    for t in range(pref - pref % mult, 0, -mult):
        if n % t == 0:
            return t
    raise ValueError((n, pref, mult))


_NN = (((1,), (0,)), ((), ()))
_NT = (((1,), (1,)), ((), ()))
_TN = (((0,), (0,)), ((), ()))


def _dot(a, b, dims=_NN):
    return lax.dot_general(a.astype(BF16), b.astype(BF16), dims, preferred_element_type=F32)


def _dot_hi(a, b, dims=_NN):
    return lax.dot_general(a, b, dims, precision=lax.Precision.HIGHEST, preferred_element_type=F32)


def _sigmoid(x):
    return 1.0 / (1.0 + jnp.exp(-x))


def _rowsum(x):
    return jnp.sum(x, axis=0, keepdims=True)


def _mm(a, b, form, out_dtype, name, tm, tn):
    if form == "tn":
        K, M = a.shape
    else:
        M, K = a.shape
    N = b.shape[0] if form == "nt" else b.shape[1]
    tm = _tile(M, tm, 128)
    tn = _tile(N, tn, 128)
    dims = {"nn": _NN, "nt": _NT, "tn": _TN}[form]

    def body(a_ref, b_ref, o_ref):
        o_ref[...] = _dot(a_ref[...], b_ref[...], dims).astype(o_ref.dtype)

    if form == "tn":
        a_spec = pl.BlockSpec((K, tm), lambda i, j: (0, i))
    else:
        a_spec = pl.BlockSpec((tm, K), lambda i, j: (i, 0))
    if form == "nt":
        b_spec = pl.BlockSpec((tn, K), lambda i, j: (j, 0))
    else:
        b_spec = pl.BlockSpec((K, tn), lambda i, j: (0, j))
    return pl.pallas_call(
        body,
        name=name,
        grid=(M // tm, N // tn),
        in_specs=[a_spec, b_spec],
        out_specs=pl.BlockSpec((tm, tn), lambda i, j: (i, j)),
        out_shape=jax.ShapeDtypeStruct((M, N), out_dtype),
        compiler_params=_params(("parallel", "parallel")),
    )(a, b)


def _mm_nt_w(a, wg, off, rows, name, tm):
    m = a.shape[0]
    tm = _tile(m, tm, 128)

    def body(a_ref, w_ref, o_ref):
        o_ref[...] = _dot(a_ref[...], w_ref[0], _NT)

    return pl.pallas_call(
        body, name=name, grid=(m // tm, N_CHIPS),
        in_specs=[pl.BlockSpec((tm, D), lambda i, s: (i, 0)),
                  pl.BlockSpec((1, rows, D), lambda i, s: (s, off // rows, 0))],
        out_specs=pl.BlockSpec((tm, rows), lambda i, s: (i, s)),
        out_shape=jax.ShapeDtypeStruct((m, N_CHIPS * rows), F32),
        compiler_params=_params(("parallel", "parallel")),
    )(a, wg)


def _mm_nn_w(a3, wg, off, rows, name, tm, tn):
    parts, m, kp = a3.shape
    assert parts * kp == N_CHIPS * rows
    tm = _tile(m, tm, 128)
    cuts = sorted({s * rows for s in range(N_CHIPS + 1)} | {p * kp for p in range(parts + 1)})
    pieces = [(k0 // kp, k0 % kp, k0 // rows, k0 % rows, k1 - k0) for k0, k1 in zip(cuts[:-1], cuts[1:])]

    def body(a_ref, w_ref, o_ref):
        acc = None
        for p, a0, s, r0, width in pieces:
            term = _dot(a_ref[p, :, a0:a0 + width], w_ref[s, r0:r0 + width, :])
            acc = term if acc is None else acc + term
        o_ref[...] = acc

    return pl.pallas_call(
        body, name=name, grid=(m // tm, D // tn),
        in_specs=[pl.BlockSpec((parts, tm, kp), lambda i, j: (0, i, 0)),
                  pl.BlockSpec((N_CHIPS, rows, tn), lambda i, j: (0, off // rows, j))],
        out_specs=pl.BlockSpec((tm, tn), lambda i, j: (i, j)),
        out_shape=jax.ShapeDtypeStruct((m, D), F32),
        compiler_params=_params(("parallel", "parallel")),
    )(a3, wg)


def _mm_dw(a3, b, g_prev, off, rows, name, tm):
    parts, ntok, cdim = a3.shape
    assert parts * cdim == N_CHIPS * rows and cdim % tm == 0 and rows % tm == 0 and off % tm == 0

    def body(a_ref, b_ref, *rest):
        rest[-1][0] = _dot(a_ref[0], b_ref[...], _TN)

    in_specs = [pl.BlockSpec((1, ntok, tm), lambda i: ((i * tm) // cdim, 0, ((i * tm) % cdim) // tm)),
                pl.BlockSpec((ntok, D), lambda i: (0, 0))]
    args = [a3, b]
    aliases = {}
    if g_prev is not None:
        in_specs.append(pl.BlockSpec(memory_space=pl.ANY))
        args.append(g_prev)
        aliases = {2: 0}
    return pl.pallas_call(
        body, name=name, grid=(N_CHIPS * rows // tm,),
        in_specs=in_specs,
        out_specs=pl.BlockSpec((1, tm, D), lambda i: ((i * tm) // rows, (off + (i * tm) % rows) // tm, 0)),
        out_shape=jax.ShapeDtypeStruct((N_CHIPS, _MAIN_TOTAL, D), F32),
        input_output_aliases=aliases,
        compiler_params=_params(("parallel",)),
    )(*args)


def _mod_fwd(h, gain, shift, scale, tpb_rows, name, y=None, gate=None):
    n = h.shape[0]
    tt = _tile(tpb_rows, 256)
    tpb = tpb_rows // tt
    has_res = y is not None

    def body(*refs):
        if has_res:
            h_ref, y_ref, gate_ref, gain_ref, sh_ref, sc_ref, hout_ref, hn_ref = refs
            hv = h_ref[...] + gate_ref[0] * y_ref[...]
            hout_ref[...] = hv
        else:
            h_ref, gain_ref, sh_ref, sc_ref, hn_ref = refs
            hv = h_ref[...]
        r = lax.rsqrt(jnp.mean(hv * hv, axis=-1, keepdims=True) + EPS)
        hn = (hv * r) * gain_ref[...] * (1.0 + sc_ref[0]) + sh_ref[0]
        hn_ref[...] = hn.astype(BF16)

    row = pl.BlockSpec((tt, D), lambda i: (i, 0))
    per_b = pl.BlockSpec((1, 1, D), lambda i: (i // tpb, 0, 0))
    vec = pl.BlockSpec((1, D), lambda i: (0, 0))
    if has_res:
        in_specs = [row, row, per_b, vec, per_b, per_b]
        args = (h, y, gate, gain, shift, scale)
        out_specs = [row, row]
        out_shape = [jax.ShapeDtypeStruct((n, D), F32), jax.ShapeDtypeStruct((n, D), BF16)]
    else:
        in_specs = [row, vec, per_b, per_b]
        args = (h, gain, shift, scale)
        out_specs = row
        out_shape = jax.ShapeDtypeStruct((n, D), BF16)
    return pl.pallas_call(
        body, name=name, grid=(n // tt,), in_specs=in_specs, out_specs=out_specs, out_shape=out_shape,
        compiler_params=_params(("parallel",)),
    )(*args)


def _mod_bwd(h_in, dhn, gain, scale, tpb_rows, name, dhn_row0=0, dh_out=None, y_prev=None, gate_prev=None,
             need_dh=True):
    n = h_in.shape[0]
    nb = n // tpb_rows
    tt = _tile(tpb_rows, 256)
    tpb = tpb_rows // tt
    off = dhn_row0 // tt
    assert dhn_row0 % tt == 0
    has_out = dh_out is not None
    has_prev = y_prev is not None

    def body(*refs):
        it = iter(refs)
        h_ref, dhn_ref, gain_ref, sc_ref = next(it), next(it), next(it), next(it)
        dho_ref = next(it) if has_out else None
        yp_ref, gp_ref = (next(it), next(it)) if has_prev else (None, None)
        dh_ref = next(it) if need_dh else None
        dsc_ref, dsh_ref, dgain_ref = next(it), next(it), next(it)
        dyp_ref, dgp_ref = (next(it), next(it)) if has_prev else (None, None)
        i = pl.program_id(0)

        @pl.when(i == 0)
        def _():
            dgain_ref[...] = jnp.zeros_like(dgain_ref)

        @pl.when(i % tpb == 0)
        def _():
            dsc_ref[...] = jnp.zeros_like(dsc_ref)
            dsh_ref[...] = jnp.zeros_like(dsh_ref)
            if has_prev:
                dgp_ref[...] = jnp.zeros_like(dgp_ref)

        hv = h_ref[...]
        r = lax.rsqrt(jnp.mean(hv * hv, axis=-1, keepdims=True) + EPS)
        y = hv * r
        gain_v = gain_ref[...]
        g = dhn_ref[...].astype(F32)
        dsh_ref[0] += _rowsum(g)
        dsc_ref[0] += _rowsum(g * (y * gain_v))
        drn = g * (1.0 + sc_ref[0])
        dgain_ref[...] += _rowsum(drn * y)
        if need_dh:
            dy = drn * gain_v
            dh = r * (dy - y * jnp.mean(dy * y, axis=-1, keepdims=True))
            if has_out:
                dh = dh + dho_ref[...]
            dh_ref[...] = dh
            if has_prev:
                dyp_ref[...] = (dh * gp_ref[0]).astype(BF16)
                dgp_ref[0] += _rowsum(dh * yp_ref[...])

    row = pl.BlockSpec((tt, D), lambda i: (i, 0))
    row_off = pl.BlockSpec((tt, D), lambda i: (i + off, 0))
    per_b = pl.BlockSpec((1, 1, D), lambda i: (i // tpb, 0, 0))
    vec = pl.BlockSpec((1, D), lambda i: (0, 0))
    in_specs = [row, row_off, vec, per_b]
    args = [h_in, dhn, gain, scale]
    if has_out:
        in_specs.append(row)
        args.append(dh_out)
    if has_prev:
        in_specs += [row, per_b]
        args += [y_prev, gate_prev]
    out_specs, out_shape, names = [], [], []
    if need_dh:
        out_specs.append(row)
        out_shape.append(jax.ShapeDtypeStruct((n, D), F32))
        names.append("dh")
    for nm in ("dscale", "dshift"):
        out_specs.append(per_b)
        out_shape.append(jax.ShapeDtypeStruct((nb, 1, D), F32))
        names.append(nm)
    out_specs.append(vec)
    out_shape.append(jax.ShapeDtypeStruct((1, D), F32))
    names.append("dgain")
    if has_prev:
        out_specs += [row, per_b]
        out_shape += [jax.ShapeDtypeStruct((n, D), BF16), jax.ShapeDtypeStruct((nb, 1, D), F32)]
        names += ["dy_prev", "dgate_prev"]
    outs = pl.pallas_call(
        body, name=name, grid=(n // tt,), in_specs=in_specs, out_specs=out_specs, out_shape=out_shape,
        compiler_params=_params(("arbitrary",)),
    )(*args)
    return dict(zip(names, outs))


def _final(h, f, gate, gain, tgt, tpb_rows):
    n = h.shape[0]
    nb = n // tpb_rows
    tt = _tile(tpb_rows, 256)
    tpb = tpb_rows // tt

    def body(h_ref, f_ref, gate_ref, gain_ref, tgt_ref, loss_ref, dh_ref, df_ref, dgate_ref, dgain_ref):
        i = pl.program_id(0)

        @pl.when(i == 0)
        def _():
            loss_ref[...] = jnp.zeros_like(loss_ref)
            dgain_ref[...] = jnp.zeros_like(dgain_ref)

        @pl.when(i % tpb == 0)
        def _():
            dgate_ref[...] = jnp.zeros_like(dgate_ref)

        fv = f_ref[...]
        gate_v = gate_ref[0]
        hv = h_ref[...] + gate_v * fv
        r = lax.rsqrt(jnp.mean(hv * hv, axis=-1, keepdims=True) + EPS)
        y = hv * r
        gain_v = gain_ref[...]
        e = y * gain_v - tgt_ref[...]
        s = jnp.sum(_rowsum(e * e), axis=1, keepdims=True) * (0.5 / D)
        loss_ref[...] += jnp.broadcast_to(s, loss_ref.shape)
        dout = e * (1.0 / D)
        dgain_ref[...] += _rowsum(dout * y)
        dy = dout * gain_v
        dh = r * (dy - y * jnp.mean(dy * y, axis=-1, keepdims=True))
        dh_ref[...] = dh
        df_ref[...] = (dh * gate_v).astype(BF16)
        dgate_ref[0] += _rowsum(dh * fv)

    row = pl.BlockSpec((tt, D), lambda i: (i, 0))
    per_b = pl.BlockSpec((1, 1, D), lambda i: (i // tpb, 0, 0))
    vec = pl.BlockSpec((1, D), lambda i: (0, 0))
    return pl.pallas_call(
        body, name="final_loss", grid=(n // tt,),
        in_specs=[row, row, per_b, vec, row],
        out_specs=[pl.BlockSpec((1, 128), lambda i: (0, 0)), row, row, per_b, vec],
        out_shape=[jax.ShapeDtypeStruct((1, 128), F32), jax.ShapeDtypeStruct((n, D), F32),
                   jax.ShapeDtypeStruct((n, D), BF16), jax.ShapeDtypeStruct((nb, 1, D), F32),
                   jax.ShapeDtypeStruct((1, D), F32)],
        compiler_params=_params(("arbitrary",)),
    )(h, f, gate, gain, tgt)


def _shift_dn(x, s):
    return jnp.concatenate([jnp.zeros((s, x.shape[1]), x.dtype), x[: x.shape[0] - s]], axis=0)


def _shift_up(x, s):
    return jnp.concatenate([x[s:], jnp.zeros((s, x.shape[1]), x.dtype)], axis=0)


def _row_dn1(x):
    t = lax.broadcasted_iota(jnp.int32, x.shape, 0)
    return jnp.where(t % GRID_W == 0, 0.0, pltpu.roll(x, 1, 0))


def _row_up1(x):
    t = lax.broadcasted_iota(jnp.int32, x.shape, 0)
    return jnp.where(t % GRID_W == GRID_W - 1, 0.0, pltpu.roll(x, x.shape[0] - 1, 0))


def _silu(x):
    return x * _sigmoid(x)


def _dsilu(x):
    s = _sigmoid(x)
    return s * (1.0 + x * (1.0 - s))


def _conv_cols(x, w_ref):
    return _shift_dn(x, GRID_W) * w_ref[0:1, :] + x * w_ref[1:2, :] + _shift_up(x, GRID_W) * w_ref[2:3, :]


def _conv_cols_bwd(x, du, w_ref, dw_ref, db_ref):
    db_ref[...] += _rowsum(du)
    dw_ref[0:1, :] += _rowsum(du * _shift_dn(x, GRID_W))
    dw_ref[1:2, :] += _rowsum(du * x)
    dw_ref[2:3, :] += _rowsum(du * _shift_up(x, GRID_W))
    return _shift_up(du, GRID_W) * w_ref[0:1, :] + du * w_ref[1:2, :] + _shift_dn(du, GRID_W) * w_ref[2:3, :]


def _ffn_mid_fwd(u0, cw, cb, nb, t, name):
    nc = HID // 128

    def body(ua_ref, ug_ref, wa_ref, wg_ref, ba_ref, bg_ref, z_ref):
        a = _conv_cols(ua_ref[...], wa_ref) + ba_ref[...]
        gt = _conv_cols(ug_ref[...], wg_ref) + bg_ref[...]
        z_ref[...] = (a * _silu(gt)).astype(BF16)

    col = lambda rows, part: pl.BlockSpec((rows, 128), lambda j, b: (b if rows == t else 0, part * nc + j))
    return pl.pallas_call(
        body, name=name, grid=(nc, nb),
        in_specs=[col(t, 0), col(t, 1), col(3, 0), col(3, 1), col(1, 0), col(1, 1)],
        out_specs=pl.BlockSpec((t, 128), lambda j, b: (b, j)),
        out_shape=jax.ShapeDtypeStruct((nb * t, HID), BF16),
        compiler_params=_params(("parallel", "parallel")),
    )(u0, u0, cw, cw, cb, cb)


def _ffn_mid_bwd(u0, cw, cb, dz, nb, t, name):
    nc = HID // 128

    def body(ua_ref, ug_ref, wa_ref, wg_ref, ba_ref, bg_ref, dz_ref, du_ref, dw_ref, db_ref):
        b = pl.program_id(1)

        @pl.when(b == 0)
        def _():
            dw_ref[...] = jnp.zeros_like(dw_ref)
            db_ref[...] = jnp.zeros_like(db_ref)

        xa = ua_ref[...]
        xg = ug_ref[...]
        a = _conv_cols(xa, wa_ref) + ba_ref[...]
        gt = _conv_cols(xg, wg_ref) + bg_ref[...]
        dzv = dz_ref[...]
        du_ref[0] = _conv_cols_bwd(xa, dzv * _silu(gt), wa_ref, dw_ref.at[0], db_ref.at[0]).astype(BF16)
        du_ref[1] = _conv_cols_bwd(xg, dzv * a * _dsilu(gt), wg_ref, dw_ref.at[1], db_ref.at[1]).astype(BF16)

    col = lambda rows, part: pl.BlockSpec((rows, 128), lambda j, b: (b if rows == t else 0, part * nc + j))
    return pl.pallas_call(
        body, name=name, grid=(nc, nb),
        in_specs=[col(t, 0), col(t, 1), col(3, 0), col(3, 1), col(1, 0), col(1, 1),
                  pl.BlockSpec((t, 128), lambda j, b: (b, j))],
        out_specs=[pl.BlockSpec((2, t, 128), lambda j, b: (0, b, j)), pl.BlockSpec((2, 3, 128), lambda j, b: (0, 0, j)),
                   pl.BlockSpec((2, 1, 128), lambda j, b: (0, 0, j))],
        out_shape=[jax.ShapeDtypeStruct((2, nb * t, HID), BF16), jax.ShapeDtypeStruct((2, 3, HID), F32),
                   jax.ShapeDtypeStruct((2, 1, HID), F32)],
        compiler_params=_params(("parallel", "arbitrary")),
    )(u0, u0, cw, cw, cb, cb, dz)


def _sc_mid_fwd(p, cw, nb, t):
    nc = D // 128

    def body(bg_ref, cg_ref, v_ref, w_ref, y_ref):
        cv = cg_ref[...] * v_ref[...]
        cc = _row_dn1(cv) * w_ref[0:1, :] + cv * w_ref[1:2, :] + _row_up1(cv) * w_ref[2:3, :]
        y_ref[...] = (bg_ref[...] * cc).astype(BF16)

    part = lambda k: pl.BlockSpec((t, 128), lambda j, b: (b, k * nc + j))
    return pl.pallas_call(
        body, name="sc_mid_fwd", grid=(nc, nb),
        in_specs=[part(0), part(1), part(2), pl.BlockSpec((3, 128), lambda j, b: (0, j))],
        out_specs=pl.BlockSpec((t, 128), lambda j, b: (b, j)),
        out_shape=jax.ShapeDtypeStruct((nb * t, D), BF16),
        compiler_params=_params(("parallel", "parallel")),
    )(p, p, p, cw)


def _sc_mid_bwd(p, cw, dyb, nb, t):
    nc = D // 128

    def body(bg_ref, cg_ref, v_ref, w_ref, dy_ref, dp_ref, dw_ref):
        b = pl.program_id(1)

        @pl.when(b == 0)
        def _():
            dw_ref[...] = jnp.zeros_like(dw_ref)

        w0, w1, w2 = w_ref[0:1, :], w_ref[1:2, :], w_ref[2:3, :]
        cg, v = cg_ref[...], v_ref[...]
        cv = cg * v
        cvd = _row_dn1(cv)
        cvu = _row_up1(cv)
        cc = cvd * w0 + cv * w1 + cvu * w2
        dy = dy_ref[...]
        dcc = dy * bg_ref[...]
        dw_ref[0:1, :] += _rowsum(dcc * cvd)
        dw_ref[1:2, :] += _rowsum(dcc * cv)
        dw_ref[2:3, :] += _rowsum(dcc * cvu)
        dcv = _row_up1(dcc) * w0 + dcc * w1 + _row_dn1(dcc) * w2
        dp_ref[0] = (dy * cc).astype(BF16)
        dp_ref[1] = (dcv * v).astype(BF16)
        dp_ref[2] = (dcv * cg).astype(BF16)

    part = lambda k: pl.BlockSpec((t, 128), lambda j, b: (b, k * nc + j))
    return pl.pallas_call(
        body, name="sc_mid_bwd", grid=(nc, nb),
        in_specs=[part(0), part(1), part(2), pl.BlockSpec((3, 128), lambda j, b: (0, j)),
                  pl.BlockSpec((t, 128), lambda j, b: (b, j))],
        out_specs=[pl.BlockSpec((3, t, 128), lambda j, b: (0, b, j)), pl.BlockSpec((3, 128), lambda j, b: (0, j))],
        out_shape=[jax.ShapeDtypeStruct((3, nb * t, D), BF16), jax.ShapeDtypeStruct((3, D), F32)],
        compiler_params=_params(("parallel", "arbitrary")),
    )(p, p, p, cw, dyb)


def _gla_decay_fwd(p_all, w2, b2):
    n = p_all.shape[0]
    tt = _tile(n, 512)

    def body(a_ref, w_ref, b_ref, la_ref):
        z = _dot(a_ref[...], w_ref[...]) + b_ref[...]
        la_ref[...] = (jnp.minimum(z, 0.0) - jnp.log(1.0 + jnp.exp(-jnp.abs(z)))) * (1.0 / TAU)

    return pl.pallas_call(
        body, name="gla_decay_fwd", grid=(n // tt,),
        in_specs=[pl.BlockSpec((tt, 128), lambda i: (i, (2 * KEY + 2 * D) // 128)),
                  pl.BlockSpec((128, 2 * KEY), lambda i: (0, 0)), pl.BlockSpec((1, 2 * KEY), lambda i: (0, 0))],
        out_specs=pl.BlockSpec((tt, 2 * KEY), lambda i: (i, 0)),
        out_shape=jax.ShapeDtypeStruct((n, 2 * KEY), F32),
        compiler_params=_params(("parallel",)),
    )(p_all, w2, b2)


def _gla_blocks(nb, nm, ncx):
    def main_idx(d, i):
        return jnp.clip(jnp.where(d == 0, i - ncx, nm - 1 - (i - ncx)), 0, nm - 1)

    def rowblk(d, b, i):
        cidx = jnp.where(d == 0, i, ncx - 1 - i)
        return jnp.where(i < ncx, nb * nm + b * ncx + cidx, b * nm + main_idx(d, i))

    def mainblk(d, b, i):
        return b * nm + main_idx(d, i)

    return rowblk, mainblk


def _gla_mask(d):
    row = lax.broadcasted_iota(jnp.int32, (CH, CH), 0)
    col = lax.broadcasted_iota(jnp.int32, (CH, CH), 1)
    diff = row - col
    mask = jnp.where(d == 0, diff, -diff) >= 0
    return mask, jnp.where(mask, 1.0, 0.0).astype(F32)


def _gla_chunk(mf, q, k, g):
    bc = _dot_hi(mf, g)
    bl = _rowsum(g)
    eq = jnp.exp(bc)
    ek = jnp.exp(-bc)
    ed = jnp.exp(bl - bc)
    return bl, eq, ek, ed, q * Q_SCALE * eq, k * ek, k * ed


def _gla_scan_fwd(p_all, la_all, nb, t, tc, own):
    nm, ncx = t // CH, tc // CH
    nst = nm + ncx
    rowblk, mainblk = _gla_blocks(nb, nm, ncx)

    def body(q_ref, k_ref, v_ref, la_ref, own_ref, o_ref, ss_ref, wg_ref, st_ref, *sems):
        d = pl.program_id(0)
        b = pl.program_id(1)
        i = pl.program_id(2)
        ag = _AllGather(own_ref, wg_ref, sems)

        @pl.when((d == 0) & (b == 0) & (i == 0))
        def _():
            ag.start()

        @pl.when(i == 0)
        def _():
            st_ref[...] = jnp.zeros_like(st_ref)

        mask, mf = _gla_mask(d)
        for h in range(HEADS):
            ksl = slice(h * DK, (h + 1) * DK)
            vsl = slice(h * DV, (h + 1) * DV)
            bl, _, _, _, qs, ks, kd = _gla_chunk(mf, q_ref[:, ksl], k_ref[:, ksl], la_ref[:, ksl])
            st = st_ref[h]
            ss_ref[0, 0, 0, h] = st
            v = v_ref[:, vsl]
            att = jnp.where(mask, _dot(qs, ks, _NT), 0.0)
            o_ref[0, :, vsl] = _dot(qs, st, _NT) + _dot(att, v)
            st_ref[h] = st * jnp.exp(bl) + _dot(v, kd, _TN)

        @pl.when((d == 1) & (b == nb - 1) & (i == nst - 1))
        def _():
            ag.finish()

    any_spec = pl.BlockSpec(memory_space=pl.ANY)
    return pl.pallas_call(
        body, name="gla_scan_fwd", grid=(2, nb, nst),
        in_specs=[
            pl.BlockSpec((CH, KEY), lambda d, b, i: (rowblk(d, b, i), 0)),
            pl.BlockSpec((CH, KEY), lambda d, b, i: (rowblk(d, b, i), 1)),
            pl.BlockSpec((CH, D), lambda d, b, i: (rowblk(d, b, i), 1)),
            pl.BlockSpec((CH, KEY), lambda d, b, i: (rowblk(d, b, i), d)),
            any_spec,
        ],
        out_specs=[
            pl.BlockSpec((1, CH, D), lambda d, b, i: (d, mainblk(d, b, i), 0)),
            pl.BlockSpec((1, 1, 1, HEADS, DV, DK), lambda d, b, i: (d, b, i, 0, 0, 0)),
            any_spec,
        ],
        out_shape=[jax.ShapeDtypeStruct((2, nb * t, D), F32),
                   jax.ShapeDtypeStruct((2, nb, nst, HEADS, DV, DK), F32),
                   jax.ShapeDtypeStruct((N_CHIPS, own.shape[0], D), own.dtype)],
        scratch_shapes=[pltpu.VMEM((HEADS, DV, DK), F32)] + list(_AllGather.SEMS),
        compiler_params=_params(("arbitrary", "arbitrary", "arbitrary")),
    )(p_all, p_all, p_all, la_all, own)


def _gla_scan_bwd(p_all, la_all, do, ss, nb, t, tc, p16):
    nm, ncx = t // CH, tc // CH
    nst = nm + ncx
    ntot = nb * (t + tc)
    rowblk, mainblk = _gla_blocks(nb, nm, ncx)

    def body(q_ref, k_ref, v_ref, la_ref, do_ref, ss_ref, p16_ref, dq_ref, dk_ref, dv_ref, dla_ref, landed_ref,
             dst_ref, *sems):
        d = pl.program_id(0)
        b = pl.program_id(1)
        ip = pl.program_id(2)
        i = nst - 1 - ip
        sc = _Scatter(p16_ref, landed_ref, sems)

        @pl.when((d == 0) & (b == 0) & (ip == 0))
        def _():
            sc.start()

        @pl.when(ip == 0)
        def _():
            dst_ref[...] = jnp.zeros_like(dst_ref)

        mask, mf = _gla_mask(d)
        live = jnp.where(i >= ncx, 1.0, 0.0)
        for h in range(HEADS):
            ksl = slice(h * DK, (h + 1) * DK)
            vsl = slice(h * DV, (h + 1) * DV)
            bl, eq, ek, ed, qs, ks, kd = _gla_chunk(mf, q_ref[:, ksl], k_ref[:, ksl], la_ref[:, ksl])
            st = ss_ref[0, 0, 0, h]
            dst = dst_ref[h]
            v = v_ref[:, vsl]
            dov = do_ref[:, vsl] * live
            att = jnp.where(mask, _dot(qs, ks, _NT), 0.0)
            datt = jnp.where(mask, _dot(dov, v, _NT), 0.0)
            dqs = _dot(dov, st) + _dot(datt, ks)
            dks = _dot(datt, qs, _TN)
            dv_ref[0, :, vsl] = _dot(att, dov, _TN) + _dot(kd, dst, _NT)
            dkd = _dot(v, dst)
            e = jnp.exp(bl)
            dbl = e * _rowsum(st * dst) + _rowsum(dkd * kd)
            dst_ref[h] = _dot(dov, qs, _TN) + dst * e
            dq_ref[0, :, ksl] = dqs * eq * Q_SCALE
            dk_ref[0, :, ksl] = dks * ek + dkd * ed
            db = dqs * qs - dks * ks - dkd * kd
            dla_ref[:, ksl] = _dot_hi(mf, db, _TN) + dbl

        @pl.when((d == 1) & (b == nb - 1) & (ip == nst - 1))
        def _():
            sc.finish()

    rev = lambda f: (lambda d, b, ip: f(d, b, nst - 1 - ip))
    any_spec = pl.BlockSpec(memory_space=pl.ANY)
    return pl.pallas_call(
        body, name="gla_scan_bwd", grid=(2, nb, nst),
        in_specs=[
            pl.BlockSpec((CH, KEY), rev(lambda d, b, i: (rowblk(d, b, i), 0))),
            pl.BlockSpec((CH, KEY), rev(lambda d, b, i: (rowblk(d, b, i), 1))),
            pl.BlockSpec((CH, D), rev(lambda d, b, i: (rowblk(d, b, i), 1))),
            pl.BlockSpec((CH, KEY), rev(lambda d, b, i: (rowblk(d, b, i), d))),
            pl.BlockSpec((CH, D), rev(lambda d, b, i: (mainblk(d, b, i), 0))),
            pl.BlockSpec((1, 1, 1, HEADS, DV, DK), rev(lambda d, b, i: (d, b, i, 0, 0, 0))),
            any_spec,
        ],
        out_specs=[
            pl.BlockSpec((1, CH, KEY), rev(lambda d, b, i: (d, rowblk(d, b, i), 0))),
            pl.BlockSpec((1, CH, KEY), rev(lambda d, b, i: (d, rowblk(d, b, i), 0))),
            pl.BlockSpec((1, CH, D), rev(lambda d, b, i: (d, rowblk(d, b, i), 0))),
            pl.BlockSpec((CH, KEY), rev(lambda d, b, i: (rowblk(d, b, i), d))),
            any_spec,
        ],
        out_shape=[jax.ShapeDtypeStruct((2, ntot, KEY), F32), jax.ShapeDtypeStruct((2, ntot, KEY), F32),
                   jax.ShapeDtypeStruct((2, ntot, D), F32), jax.ShapeDtypeStruct((ntot, 2 * KEY), F32),
                   jax.ShapeDtypeStruct(p16.shape, p16.dtype)],
        scratch_shapes=[pltpu.VMEM((HEADS, DV, DK), F32)] + list(_Scatter.SEMS),
        compiler_params=_params(("arbitrary", "arbitrary", "arbitrary")),
    )(p_all, p_all, p_all, la_all, do, ss, p16)


def _gla_post_fwd(o2, p_all, head_gain, n):
    tt = _tile(n, 256)

    def body(o_ref, g_ref, hg_ref, y_ref):
        o = o_ref[0] + o_ref[1]
        gv = g_ref[...]
        hg = hg_ref[...]
        for h in range(HEADS):
            oh = o[:, h * DV:(h + 1) * DV]
            r = lax.rsqrt(jnp.mean(oh * oh, axis=-1, keepdims=True) + EPS)
            y_ref[:, h * DV:(h + 1) * DV] = ((oh * r) * hg * _silu(gv[:, h * DV:(h + 1) * DV])).astype(BF16)

    return pl.pallas_call(
        body, name="gla_post_fwd", grid=(n // tt,),
        in_specs=[pl.BlockSpec((2, tt, D), lambda i: (0, i, 0)), pl.BlockSpec((tt, D), lambda i: (i, 2)),
                  pl.BlockSpec((1, DV), lambda i: (0, 0))],
        out_specs=pl.BlockSpec((tt, D), lambda i: (i, 0)),
        out_shape=jax.ShapeDtypeStruct((n, D), BF16),
        compiler_params=_params(("parallel",)),
    )(o2, p_all, head_gain)


def _gla_post_bwd(o2, p_all, head_gain, dyb, n):
    tt = _tile(n, 256)

    def body(o_ref, g_ref, hg_ref, dy_ref, do_ref, dg_ref, dhg_ref):
        i = pl.program_id(0)

        @pl.when(i == 0)
        def _():
            dhg_ref[...] = jnp.zeros_like(dhg_ref)

        o = o_ref[0] + o_ref[1]
        gv = g_ref[...]
        hg = hg_ref[...]
        dy = dy_ref[...]
        acc = jnp.zeros((1, DV), F32)
        for h in range(HEADS):
            sl = slice(h * DV, (h + 1) * DV)
            oh = o[:, sl]
            r = lax.rsqrt(jnp.mean(oh * oh, axis=-1, keepdims=True) + EPS)
            on = oh * r
            gh = gv[:, sl]
            dyh = dy[:, sl]
            dg_ref[:, sl] = dyh * (on * hg) * _dsilu(gh)
            dog = dyh * _silu(gh)
            acc = acc + _rowsum(dog * on)
            don = dog * hg
            do_ref[:, sl] = r * (don - on * jnp.mean(don * on, axis=-1, keepdims=True))
        dhg_ref[...] += acc

    return pl.pallas_call(
        body, name="gla_post_bwd", grid=(n // tt,),
        in_specs=[pl.BlockSpec((2, tt, D), lambda i: (0, i, 0)), pl.BlockSpec((tt, D), lambda i: (i, 2)),
                  pl.BlockSpec((1, DV), lambda i: (0, 0)), pl.BlockSpec((tt, D), lambda i: (i, 0))],
        out_specs=[pl.BlockSpec((tt, D), lambda i: (i, 0)), pl.BlockSpec((tt, D), lambda i: (i, 0)),
                   pl.BlockSpec((1, DV), lambda i: (0, 0))],
        out_shape=[jax.ShapeDtypeStruct((n, D), F32), jax.ShapeDtypeStruct((n, D), F32),
                   jax.ShapeDtypeStruct((1, DV), F32)],
        compiler_params=_params(("arbitrary",)),
    )(o2, p_all, head_gain, dyb)


def _gla_assemble(p_all, w2, b2, dq, dk, dv, dla, dgate, n):
    ntot = p_all.shape[0]
    tt = _tile(n, 128)
    nmain = n // tt
    assert ntot % tt == 0

    def body(a_ref, w_ref, b_ref, dq_ref, dk_ref, dv_ref, dla_ref, dg_ref, dp_ref, dw_ref, db_ref):
        i = pl.program_id(0)

        @pl.when(i == 0)
        def _():
            dw_ref[...] = jnp.zeros_like(dw_ref)
            db_ref[...] = jnp.zeros_like(db_ref)

        a = a_ref[...]
        w = w_ref[...]
        z = _dot(a, w) + b_ref[...]
        dz = dla_ref[...] * (1.0 / (1.0 + jnp.exp(z))) * (1.0 / TAU)
        dw_ref[...] += _dot(a, dz, _TN)
        db_ref[...] += _rowsum(dz)
        dp_ref[:, 0:KEY] = ((dq_ref[0] + dq_ref[1]) * 1.0).astype(BF16)
        dp_ref[:, KEY:2 * KEY] = (dk_ref[0] + dk_ref[1]).astype(BF16)
        dp_ref[:, 2 * KEY:2 * KEY + D] = (dv_ref[0] + dv_ref[1]).astype(BF16)
        dp_ref[:, 2 * KEY + D:2 * KEY + 2 * D] = (dg_ref[...] * jnp.where(i < nmain, 1.0, 0.0)).astype(BF16)
        dp_ref[:, 2 * KEY + 2 * D:GLA_IN_PAD] = _dot(dz, w, _NT).astype(BF16)

    return pl.pallas_call(
        body, name="gla_assemble", grid=(ntot // tt,),
        in_specs=[pl.BlockSpec((tt, 128), lambda i: (i, (2 * KEY + 2 * D) // 128)),
                  pl.BlockSpec((128, 2 * KEY), lambda i: (0, 0)), pl.BlockSpec((1, 2 * KEY), lambda i: (0, 0)),
                  pl.BlockSpec((2, tt, KEY), lambda i: (0, i, 0)), pl.BlockSpec((2, tt, KEY), lambda i: (0, i, 0)),
                  pl.BlockSpec((2, tt, D), lambda i: (0, i, 0)), pl.BlockSpec((tt, 2 * KEY), lambda i: (i, 0)),
                  pl.BlockSpec((tt, D), lambda i: (jnp.minimum(i, nmain - 1), 0))],
        out_specs=[pl.BlockSpec((tt, GLA_IN_PAD), lambda i: (i, 0)), pl.BlockSpec((128, 2 * KEY), lambda i: (0, 0)),
                   pl.BlockSpec((1, 2 * KEY), lambda i: (0, 0))],
        out_shape=[jax.ShapeDtypeStruct((ntot, GLA_IN_PAD), BF16), jax.ShapeDtypeStruct((128, 2 * KEY), F32),
                   jax.ShapeDtypeStruct((1, 2 * KEY), F32)],
        compiler_params=_params(("arbitrary",)),
    )(p_all, w2, b2, dq, dk, dv, dla, dgate)


ADA_ROWS = 24
ADA_SH = N_MOD * D // N_CHIPS


def _ada_fwd(cvec, ada_w, ada_b_sh):
    def body(c_ref, w_ref, b_ref, o_ref):
        o_ref[0] = _dot(_silu(c_ref[...]), w_ref[0]) + b_ref[0]

    return pl.pallas_call(
        body, name="ada_fwd", grid=(2,),
        in_specs=[pl.BlockSpec((ADA_ROWS, D), lambda l: (0, 0)), pl.BlockSpec((1, D, ADA_SH), lambda l: (l, 0, 0)),
                  pl.BlockSpec((1, 1, ADA_SH), lambda l: (l, 0, 0))],
        out_specs=pl.BlockSpec((1, ADA_ROWS, ADA_SH), lambda l: (l, 0, 0)),
        out_shape=jax.ShapeDtypeStruct((2, ADA_ROWS, ADA_SH), F32),
        compiler_params=_params(("parallel",)),
    )(cvec, ada_w, ada_b_sh)


def _ada_bwd(cvec, ada_w, dmod_sh):
    def body(c_ref, w_ref, dm_ref, gw_ref, dc_ref):
        dm = dm_ref[0]
        gw_ref[0] = _dot(_silu(c_ref[...]), dm, _TN)
        dc_ref[0] = _dot(dm, w_ref[0], _NT)

    return pl.pallas_call(
        body, name="ada_bwd", grid=(2,),
        in_specs=[pl.BlockSpec((ADA_ROWS, D), lambda l: (0, 0)), pl.BlockSpec((1, D, ADA_SH), lambda l: (l, 0, 0)),
                  pl.BlockSpec((1, ADA_ROWS, ADA_SH), lambda l: (l, 0, 0))],
        out_specs=[pl.BlockSpec((1, D, ADA_SH), lambda l: (l, 0, 0)), pl.BlockSpec((1, ADA_ROWS, D), lambda l: (l, 0, 0))],
        out_shape=[jax.ShapeDtypeStruct((2, D, ADA_SH), F32), jax.ShapeDtypeStruct((2, ADA_ROWS, D), F32)],
        compiler_params=_params(("parallel",)),
    )(cvec, ada_w, dmod_sh)


def _sum_slots(x, name):
    s, r, _ = x.shape

    def body(x_ref, o_ref):
        acc = x_ref[0]
        for k in range(1, s):
            acc = acc + x_ref[k]
        o_ref[...] = acc

    return pl.pallas_call(
        body, name=name, out_shape=jax.ShapeDtypeStruct((r, 128), F32),
        in_specs=[pl.BlockSpec(memory_space=pltpu.VMEM)], out_specs=pl.BlockSpec(memory_space=pltpu.VMEM),
    )(x)


def _cctx_grad(dscc_parts, c_ctx):
    def body(p_ref, c_ref, o_ref):
        acc = p_ref[0]
        for k in range(1, N_CHIPS):
            acc = acc + p_ref[k]
        o_ref[...] = acc * _dsilu(c_ref[...])

    return pl.pallas_call(
        body, name="cctx_grad", out_shape=jax.ShapeDtypeStruct((8, 128), F32),
        in_specs=[pl.BlockSpec(memory_space=pltpu.VMEM)] * 2, out_specs=pl.BlockSpec(memory_space=pltpu.VMEM),
    )(dscc_parts, c_ctx)


def _adamw(w, g, m, v, name):
    r, cdim = w.shape
    tr = _tile(r, 256)
    c1 = 1.0 - ADAM_B1 ** ADAM_STEP
    c2 = 1.0 - ADAM_B2 ** ADAM_STEP

    def body(w_ref, g_ref, m_ref, v_ref, d_ref, mo_ref, vo_ref):
        gv = g_ref[...]
        mn = ADAM_B1 * m_ref[...] + (1.0 - ADAM_B1) * gv
        vn = ADAM_B2 * v_ref[...] + (1.0 - ADAM_B2) * (gv * gv)
        mo_ref[...] = mn
        vo_ref[...] = vn
        d_ref[...] = -ADAM_LR * ((mn / c1) / (jnp.sqrt(vn / c2) + ADAM_EPS) + ADAM_WD * w_ref[...])

    spec = pl.BlockSpec((tr, cdim), lambda i: (i, 0))
    sds = jax.ShapeDtypeStruct((r, cdim), F32)
    return pl.pallas_call(
        body, name=name, grid=(r // tr,), in_specs=[spec] * 4, out_specs=[spec] * 3, out_shape=[sds] * 3,
        compiler_params=_params(("parallel",)),
    )(w, g, m, v)


def _place():
    x, y, c = lax.axis_index("x"), lax.axis_index("y"), lax.axis_index("c")
    return x, y, c


def _allgather_small(blk, name):
    m_per, n = blk.shape

    def body(x_ref, out_ref, send_sems, recv_sems, local_sem):
        x, y, c = _place()
        me, sibling = (x, y, c), (x, y, 1 - c)
        chips = [(1 - x, y), (x, 1 - y), (1 - x, 1 - y)]

        def rows(px, py, pc):
            return out_ref.at[pl.ds((4 * px + 2 * py + pc) * m_per, m_per), :]

        def copy(k, block, to, src=None):
            return pltpu.make_async_remote_copy(
                src_ref=rows(*block) if src is None else src, dst_ref=rows(*block),
                send_sem=send_sems.at[k], recv_sem=recv_sems.at[k], device_id=to, device_id_type=MESH)

        mine = pltpu.make_async_copy(x_ref, rows(*me), local_sem)
        mine.start()
        first = [copy(0, me, sibling, src=x_ref)]
        first += [copy(1 + j, me, (*chip, c), src=x_ref) for j, chip in enumerate(chips)]
        for cp in first:
            cp.start()
        passed = [copy(4 + j, (*chip, c), sibling) for j, chip in enumerate(chips)]
        for j, chip in enumerate(chips):
            copy(1 + j, (*chip, c), me).wait_recv()
            passed[j].start()
        copy(0, sibling, me).wait_recv()
        for j, chip in enumerate(chips):
            copy(4 + j, (*chip, 1 - c), me).wait_recv()
        for cp in first + passed:
            cp.wait_send()
        mine.wait()

    return pl.pallas_call(
        body, name=name,
        out_shape=jax.ShapeDtypeStruct((N_DEV * m_per, n), blk.dtype),
        in_specs=[pl.BlockSpec(memory_space=pltpu.VMEM)],
        out_specs=pl.BlockSpec(memory_space=pltpu.VMEM),
        scratch_shapes=[pltpu.SemaphoreType.DMA((7,)), pltpu.SemaphoreType.DMA((7,)), pltpu.SemaphoreType.DMA],
    )(blk)


def _other_chips(x, y):
    return [(1 - x, y), (x, 1 - y), (1 - x, 1 - y)]


class _AllGather:
    SEMS = [pltpu.SemaphoreType.DMA((3,)), pltpu.SemaphoreType.DMA((3,)), pltpu.SemaphoreType.DMA((3,)),
            pltpu.SemaphoreType.DMA((3,)), pltpu.SemaphoreType.DMA((2,))]

    def __init__(self, own_ref, out_ref, sems):
        self.own_ref, self.out_ref = own_ref, out_ref
        self.send_sems, self.recv_sems, self.fsend_sems, self.frecv_sems, self.own_sems = sems
        self.hr = own_ref.shape[0] // 2

    def _half(self, ch, cc):
        return self.out_ref.at[ch, pl.ds(cc * self.hr, self.hr), :]

    def _own_slot(self):
        x, y, c = _place()
        return pltpu.make_async_remote_copy(
            src_ref=self.own_ref, dst_ref=self.out_ref.at[2 * x + y], send_sem=self.own_sems.at[0],
            recv_sem=self.own_sems.at[1], device_id=(x, y, 1 - c), device_id_type=MESH)

    def _send(self, j, ox, oy):
        x, y, c = _place()
        return pltpu.make_async_remote_copy(
            src_ref=self.own_ref.at[pl.ds(c * self.hr, self.hr), :], dst_ref=self._half(2 * x + y, c),
            send_sem=self.send_sems.at[j], recv_sem=self.recv_sems.at[j], device_id=(ox, oy, c),
            device_id_type=MESH)

    def _landed(self, j, ox, oy):
        x, y, c = _place()
        ref = self._half(2 * ox + oy, c)
        return pltpu.make_async_remote_copy(
            src_ref=ref, dst_ref=ref, send_sem=self.send_sems.at[j], recv_sem=self.recv_sems.at[j],
            device_id=(ox, oy, c), device_id_type=MESH)

    def _pass_on(self, j, ox, oy, cc):
        x, y, c = _place()
        ref = self._half(2 * ox + oy, cc)
        return pltpu.make_async_remote_copy(
            src_ref=ref, dst_ref=ref, send_sem=self.fsend_sems.at[j], recv_sem=self.frecv_sems.at[j],
            device_id=(x, y, 1 - c), device_id_type=MESH)

    def start(self):
        x, y, c = _place()
        self._own_slot().start()
        for j, (ox, oy) in enumerate(_other_chips(x, y)):
            self._send(j, ox, oy).start()

    def finish(self):
        x, y, c = _place()
        others = _other_chips(x, y)
        for j, (ox, oy) in enumerate(others):
            self._landed(j, ox, oy).wait_recv()
            self._pass_on(j, ox, oy, c).start()
        for j, (ox, oy) in enumerate(others):
            self._pass_on(j, ox, oy, 1 - c).wait_recv()
        for j, (ox, oy) in enumerate(others):
            self._send(j, ox, oy).wait_send()
            self._pass_on(j, ox, oy, c).wait_send()
        self._own_slot().wait()


def _weights_allgather(own, name):
    def body(own_ref, out_ref, *sems):
        ag = _AllGather(own_ref, out_ref, sems)
        ag.start()
        ag.finish()

    any_spec = pl.BlockSpec(memory_space=pl.ANY)
    return pl.pallas_call(
        body, name=name,
        out_shape=jax.ShapeDtypeStruct((N_CHIPS, own.shape[0], D), own.dtype),
        in_specs=[any_spec], out_specs=any_spec, scratch_shapes=list(_AllGather.SEMS),
    )(own)


def _rs_pair_exchange(g, name):
    r = g.shape[1]
    hr = r // 2

    def body(g_ref, got_ref, send_sem, recv_sem):
        x, y, c = _place()
        cp = pltpu.make_async_remote_copy(
            src_ref=g_ref.at[:, pl.ds((1 - c) * hr, hr), :], dst_ref=got_ref, send_sem=send_sem, recv_sem=recv_sem,
            device_id=(x, y, 1 - c), device_id_type=MESH)
        cp.start()
        cp.wait()

    any_spec = pl.BlockSpec(memory_space=pl.ANY)
    return pl.pallas_call(
        body, name=name,
        out_shape=jax.ShapeDtypeStruct((N_CHIPS, hr, D), F32),
        in_specs=[any_spec], out_specs=any_spec,
        scratch_shapes=[pltpu.SemaphoreType.DMA, pltpu.SemaphoreType.DMA],
    )(g)


def _rs_chip_sum(place, g, got, name):
    r = g.shape[1]
    hr = r // 2
    tr = _tile(hr, 640, 16)
    nt = hr // tr

    def body(pl_ref, g_ref, got_ref, p16_ref, p32_ref):
        s = pl.program_id(1)
        p = g_ref[0] + got_ref[0]
        p16_ref[0] = p.astype(BF16)

        @pl.when(s == pl_ref[1])
        def _():
            p32_ref[...] = p

    return pl.pallas_call(
        body, name=name,
        grid_spec=pltpu.PrefetchScalarGridSpec(
            num_scalar_prefetch=1, grid=(nt, N_CHIPS),
            in_specs=[pl.BlockSpec((1, tr, D), lambda i, s, pr: (s, pr[0] * nt + i, 0)),
                      pl.BlockSpec((1, tr, D), lambda i, s, pr: (s, i, 0))],
            out_specs=[pl.BlockSpec((1, tr, D), lambda i, s, pr: (s, i, 0)),
                       pl.BlockSpec((tr, D), lambda i, s, pr: (i, 0))]),
        out_shape=[jax.ShapeDtypeStruct((N_CHIPS, hr, D), BF16), jax.ShapeDtypeStruct((hr, D), F32)],
        compiler_params=_params(("parallel", "arbitrary")),
    )(place, g, got)


class _Scatter:
    SEMS = [pltpu.SemaphoreType.DMA((3,)), pltpu.SemaphoreType.DMA((3,))]

    def __init__(self, p_ref, out_ref, sems):
        self.p_ref, self.out_ref = p_ref, out_ref
        self.send_sems, self.recv_sems = sems

    def _copy(self, j, ox, oy, src_slot, dst_slot):
        x, y, c = _place()
        return pltpu.make_async_remote_copy(
            src_ref=self.p_ref.at[src_slot], dst_ref=self.out_ref.at[dst_slot], send_sem=self.send_sems.at[j],
            recv_sem=self.recv_sems.at[j], device_id=(ox, oy, c), device_id_type=MESH)

    def start(self):
        x, y, c = _place()
        for j, (ox, oy) in enumerate(_other_chips(x, y)):
            self._copy(j, ox, oy, 2 * ox + oy, 2 * x + y).start()

    def finish(self):
        x, y, c = _place()
        others = _other_chips(x, y)
        for j, (ox, oy) in enumerate(others):
            self._copy(j, ox, oy, 2 * ox + oy, 2 * ox + oy).wait_recv()
        for j, (ox, oy) in enumerate(others):
            self._copy(j, ox, oy, 2 * ox + oy, 2 * x + y).wait_send()


def _rs_scatter(p16, name):
    def body(p_ref, out_ref, *sems):
        sc = _Scatter(p_ref, out_ref, sems)
        sc.start()
        sc.finish()

    any_spec = pl.BlockSpec(memory_space=pl.ANY)
    return pl.pallas_call(
        body, name=name,
        out_shape=jax.ShapeDtypeStruct(p16.shape, p16.dtype),
        in_specs=[any_spec], out_specs=any_spec, scratch_shapes=list(_Scatter.SEMS),
    )(p16)


def _rs_final_sum(place, parts, p32, name):
    hr = parts.shape[1]
    tr = _tile(hr, 640, 16)
    nt = hr // tr

    def body(pl_ref, a_ref, b_ref, c_ref, p32_ref, o_ref):
        o_ref[...] = ((p32_ref[...] + a_ref[0].astype(F32)) + b_ref[0].astype(F32)) + c_ref[0].astype(F32)

    def other(j):
        return pl.BlockSpec((1, tr, D), lambda i, pr: (j + jnp.where(pr[1] <= j, 1, 0), i, 0))

    return pl.pallas_call(
        body, name=name,
        grid_spec=pltpu.PrefetchScalarGridSpec(
            num_scalar_prefetch=1, grid=(nt,),
            in_specs=[other(0), other(1), other(2), pl.BlockSpec((tr, D), lambda i, pr: (i, 0))],
            out_specs=pl.BlockSpec((tr, D), lambda i, pr: (pr[0] * nt + i, 0))),
        out_shape=jax.ShapeDtypeStruct((2 * hr, D), F32),
        compiler_params=_params(("parallel",)),
    )(place, parts, parts, parts, p32)


def _rs_pair_gather(both, name):
    hr = both.shape[0] // 2

    def body(in_ref, out_ref, send_sem, recv_sem):
        x, y, c = _place()
        mine = out_ref.at[pl.ds(c * hr, hr), :]
        cp = pltpu.make_async_remote_copy(
            src_ref=mine, dst_ref=mine, send_sem=send_sem, recv_sem=recv_sem,
            device_id=(x, y, 1 - c), device_id_type=MESH)
        cp.start()
        theirs = out_ref.at[pl.ds((1 - c) * hr, hr), :]
        pltpu.make_async_remote_copy(
            src_ref=theirs, dst_ref=theirs, send_sem=send_sem, recv_sem=recv_sem,
            device_id=(x, y, 1 - c), device_id_type=MESH).wait_recv()
        cp.wait_send()

    any_spec = pl.BlockSpec(memory_space=pl.ANY)
    return pl.pallas_call(
        body, name=name,
        out_shape=jax.ShapeDtypeStruct(both.shape, F32),
        in_specs=[any_spec], out_specs=any_spec, input_output_aliases={0: 0},
        scratch_shapes=[pltpu.SemaphoreType.DMA, pltpu.SemaphoreType.DMA],
    )(both)


def _local_step(x, ctx, tgt, mods, mc, gla_in_t, own_main, place, small):
    nb, t, _ = x.shape
    tc = ctx.shape[1]
    n = nb * t
    nc = nb * tc
    xf = x.reshape(n, D)
    cf = ctx.reshape(nc, D)
    tf = tgt.reshape(n, D)
    vec = lambda a: a.reshape(1, -1)
    m = [[mods[l, :, k, :].reshape(nb, 1, D) for k in range(N_MOD)] for l in range(2)]
    mc_b = [jnp.broadcast_to(mc[k].reshape(1, 1, D), (nb, 1, D)) for k in range(2)]

    w_gin = jnp.pad(gla_in_t, ((0, GLA_IN_PAD - GLA_IN), (0, 0)))
    cw = [small["ffn_conv_w"][l] for l in range(2)]
    cb = [small["ffn_conv_b"][l].reshape(1, -1) for l in range(2)]
    w2 = jnp.zeros((128, 2 * KEY), F32)
    w2 = w2.at[0:RANK, 0:KEY].set(small["gla_w_a2"][0]).at[RANK:2 * RANK, KEY:].set(small["gla_w_a2"][1])
    b2 = small["gla_b_a"].reshape(1, 2 * KEY)
    hg = small["gla_head_norm"].reshape(1, DV)

    hn0 = _mod_fwd(xf, vec(small["norm_mix"][0]), m[0][0], m[0][1], t, "mod0_main")
    hnc = _mod_fwd(cf, vec(small["norm_mix"][0]), mc_b[0], mc_b[1], tc, "mod0_ctx")
    hn_all = jnp.concatenate([hn0, hnc], axis=0)
    p_all = _mm(hn_all, w_gin, "nt", F32, "gla_in_proj", 768, 3200)
    la_all = _gla_decay_fwd(p_all, w2, b2)
    o2, ss, wg = _gla_scan_fwd(p_all, la_all, nb, t, tc, own_main)
    offs = _offsets(_MAIN, _MAIN_ROWS)
    rows = _MAIN_ROWS

    def w_nt(a, k, name, tm=1024):
        return _mm_nt_w(a, wg, offs[k], rows[k], name, tm)

    def w_nn(a3, k, name, tm, tn):
        return _mm_nn_w(a3, wg, offs[k], rows[k], name, tm, tn)

    yb0 = _gla_post_fwd(o2, p_all, hg, n)
    y0 = w_nn(yb0[None], "gla_out", "gla_out_proj", 1024, 1024)
    h1, hn1 = _mod_fwd(xf, vec(small["norm_ffn"][0]), m[0][3], m[0][4], t, "mod0_ffn", y=y0, gate=m[0][2])
    u0 = w_nt(hn1, "up_t0", "ffn0_up")
    z0 = _ffn_mid_fwd(u0, cw[0], cb[0], nb, t, "ffn0_mid_fwd")
    f0 = w_nn(z0[None], "down0", "ffn0_down", 1024, 1024)
    h2, hn2 = _mod_fwd(h1, vec(small["norm_mix"][1]), m[1][0], m[1][1], t, "mod1_mix", y=f0, gate=m[0][5])
    p1 = w_nt(hn2, "sc_in_t", "sc_in_proj")
    yb1 = _sc_mid_fwd(p1, small["sc_conv_w"], nb, t)
    y1 = w_nn(yb1[None], "sc_out", "sc_out_proj", 1024, 1024)
    h3, hn3 = _mod_fwd(h2, vec(small["norm_ffn"][1]), m[1][3], m[1][4], t, "mod1_ffn", y=y1, gate=m[1][2])
    u1 = w_nt(hn3, "up_t1", "ffn1_up")
    z1 = _ffn_mid_fwd(u1, cw[1], cb[1], nb, t, "ffn1_mid_fwd")
    f1 = w_nn(z1[None], "down1", "ffn1_down", 1024, 1024)
    loss, dh4, df1, dm15, dfinal = _final(h3, f1, m[1][5], vec(small["final_norm"]), tf, t)

    gs = {}
    dmods = [[None] * N_MOD for _ in range(2)]
    dmods[1][5] = dm15

    def w_dw(a3, b, g_prev, k, name, tm):
        return _mm_dw(a3, b, g_prev, offs[k], rows[k], name, tm)

    def ffn_bwd(l, df, u, z, hn, g_prev):
        dz = w_nt(df, f"down{l}", f"ffn{l}_down_dx")
        g_acc = w_dw(z[None], df, g_prev, f"down{l}", f"ffn{l}_down_dw", 640)
        du, dcw, dcb = _ffn_mid_bwd(u, cw[l], cb[l], dz, nb, t, f"ffn{l}_mid_bwd")
        dhn = w_nn(du, f"up_t{l}", f"ffn{l}_up_dx", 512, 512)
        g_acc = w_dw(du, hn, g_acc, f"up_t{l}", f"ffn{l}_up_dw", 640)
        return dhn, g_acc, jnp.moveaxis(dcw, 0, 1).reshape(3, 2 * HID), dcb.reshape(2 * HID)

    dhn3, g_acc, dcw1, dcb1 = ffn_bwd(1, df1, u1, z1, hn3, None)
    r = _mod_bwd(h3, dhn3, vec(small["norm_ffn"][1]), m[1][4], t, "mod1_ffn_bwd", dh_out=dh4, y_prev=y1,
                 gate_prev=m[1][2])
    dh3, dmods[1][4], dmods[1][3], dnf1, dy1, dmods[1][2] = (r["dh"], r["dscale"], r["dshift"], r["dgain"],
                                                             r["dy_prev"], r["dgate_prev"])
    dyb1 = w_nt(dy1, "sc_out", "sc_out_dx")
    g_acc = w_dw(yb1[None], dy1, g_acc, "sc_out", "sc_out_dw", 256)
    dp1, dscw = _sc_mid_bwd(p1, small["sc_conv_w"], dyb1, nb, t)
    dhn2 = w_nn(dp1, "sc_in_t", "sc_in_dx", 1024, 512)
    g_acc = w_dw(dp1, hn2, g_acc, "sc_in_t", "sc_in_dw", 256)
    r = _mod_bwd(h2, dhn2, vec(small["norm_mix"][1]), m[1][1], t, "mod1_mix_bwd", dh_out=dh3, y_prev=f0,
                 gate_prev=m[0][5])
    dh2, dmods[1][1], dmods[1][0], dnm1, df0, dmods[0][5] = (r["dh"], r["dscale"], r["dshift"], r["dgain"],
                                                             r["dy_prev"], r["dgate_prev"])
    dhn1, g_acc, dcw0, dcb0 = ffn_bwd(0, df0, u0, z0, hn1, g_acc)
    r = _mod_bwd(h1, dhn1, vec(small["norm_ffn"][0]), m[0][4], t, "mod0_ffn_bwd", dh_out=dh2, y_prev=y0,
                 gate_prev=m[0][2])
    dh1, dmods[0][4], dmods[0][3], dnf0, dy0, dmods[0][2] = (r["dh"], r["dscale"], r["dshift"], r["dgain"],
                                                             r["dy_prev"], r["dgate_prev"])
    dyb0 = w_nt(dy0, "gla_out", "gla_out_dx")
    g_packed = w_dw(yb0[None], dy0, g_acc, "gla_out", "gla_out_dw", 256)
    from_sibling = _rs_pair_exchange(g_packed, "rs_main_pair_exchange")
    p16, p32 = _rs_chip_sum(place, g_packed, from_sibling, "rs_main_chip_sum")
    do, dgate, dhg = _gla_post_bwd(o2, p_all, hg, dyb0, n)
    dq, dk, dv, dla, landed = _gla_scan_bwd(p_all, la_all, do, ss, nb, t, tc, p16)
    g_main = _rs_pair_gather(_rs_final_sum(place, landed, p32, "rs_main_final_sum"), "rs_main_pair_gather")
    dp, dw2, db2 = _gla_assemble(p_all, w2, b2, dq, dk, dv, dla, dgate, n)
    dhn_all = _mm(dp, w_gin, "nn", F32, "gla_in_dx", 768, 512)
    g_gin = _mm(dp, hn_all, "tn", F32, "gla_in_dw", 640, 1024)[:GLA_IN]
    r = _mod_bwd(xf, dhn_all, vec(small["norm_mix"][0]), m[0][1], t, "mod0_main_bwd", dh_out=dh1)
    grad_x, dmods[0][1], dmods[0][0], dnm0 = r["dh"], r["dscale"], r["dshift"], r["dgain"]
    rc = _mod_bwd(cf, dhn_all, vec(small["norm_mix"][0]), mc_b[1], tc, "mod0_ctx_bwd", dhn_row0=n, need_dh=False)
    dmc = jnp.stack([jnp.sum(rc["dshift"], axis=0).reshape(D), jnp.sum(rc["dscale"], axis=0).reshape(D)])
    dnm0 = dnm0 + rc["dgain"]

    gs["norm_mix"] = jnp.concatenate([dnm0, dnm1], axis=0)
    gs["norm_ffn"] = jnp.concatenate([dnf0, dnf1], axis=0)
    gs["final_norm"] = dfinal.reshape(D)
    gs["gla_w_a2"] = jnp.stack([dw2[0:RANK, 0:KEY], dw2[RANK:2 * RANK, KEY:]])
    gs["gla_b_a"] = db2.reshape(2, KEY)
    gs["gla_head_norm"] = dhg.reshape(DV)
    gs["sc_conv_w"] = dscw
    gs["ffn_conv_w"] = jnp.stack([dcw0, dcw1])
    gs["ffn_conv_b"] = jnp.stack([dcb0, dcb1])
    dmods_arr = jnp.stack([jnp.stack([dmods[l][k].reshape(nb, D) for k in range(N_MOD)], axis=1) for l in range(2)])
    return loss, grad_x.reshape(nb, t, D), g_main, g_gin, gs, dmods_arr, dmc


def _pack(arrs):
    parts, meta, off = [], [], 0
    for a in arrs:
        r = a.size // 128
        rp = -(-r // 8) * 8
        a2 = a.reshape(r, 128).astype(F32)
        if rp != r:
            a2 = jnp.pad(a2, ((0, rp - r), (0, 0)))
        parts.append(a2)
        meta.append((off, r, a.shape))
        off += rp
    return jnp.concatenate(parts, axis=0), meta


def _unpack(buf, meta, lead=()):
    return [buf[..., off:off + r, :].reshape(*lead, *shape) for off, r, shape in meta]


_MAIN = ("up_t0", "up_t1", "down0", "down1", "sc_in_t", "gla_out", "sc_out")
_MAIN_ROWS = {"sc_in_t": 3 * D // N_CHIPS, "up_t0": 2 * HID // N_CHIPS, "up_t1": 2 * HID // N_CHIPS,
              "gla_out": D // N_CHIPS, "sc_out": D // N_CHIPS, "down0": HID // N_CHIPS, "down1": HID // N_CHIPS}
_MAIN_TOTAL = sum(_MAIN_ROWS.values())
_GIN_ROWS = GLA_IN // N_CHIPS
_GIN_PAD = -(-_GIN_ROWS // 32) * 32


def _offsets(names, rows):
    off, out = 0, {}
    for k in names:
        out[k] = off
        off += rows[k]
    return out


def kernel(x, c, ctx, c_ctx, ada_w, ada_b, norm_mix, norm_ffn, gla_w_in, gla_w_a2, gla_b_a, gla_head_norm, gla_w_out, sc_w_in, sc_conv_w, sc_w_out, ffn_w_up, ffn_conv_w, ffn_conv_b, ffn_w_down, final_norm, loss_target, m_c_ctx, m_ada_w, m_ada_b, m_norm_mix, m_norm_ffn, m_gla_w_in, m_gla_w_a2, m_gla_b_a, m_gla_head_norm, m_gla_w_out, m_sc_w_in, m_sc_conv_w, m_sc_w_out, m_ffn_w_up, m_ffn_conv_w, m_ffn_conv_b, m_ffn_w_down, m_final_norm, v_c_ctx, v_ada_w, v_ada_b, v_norm_mix, v_norm_ffn, v_gla_w_in, v_gla_w_a2, v_gla_b_a, v_gla_head_norm, v_gla_w_out, v_sc_w_in, v_sc_conv_w, v_sc_w_out, v_ffn_w_up, v_ffn_conv_w, v_ffn_conv_b, v_ffn_w_down, v_final_norm):
    ix, iy, ic = _place()
    chip = 2 * ix + iy
    dev = 2 * chip + ic
    place = jnp.stack([ic, chip]).astype(jnp.int32)
    nb = x.shape[0]
    offs = _offsets(_MAIN, _MAIN_ROWS)

    buf, meta = _pack([c, ffn_conv_w, sc_conv_w, gla_w_a2, gla_b_a])
    got = _allgather_small(buf, "gather_small_in").reshape(N_DEV, buf.shape[0], 128)
    c_all, fcw, scw, wa2, ba = _unpack(got, meta, (N_DEV,))
    c_all = c_all.reshape(N_DEV * nb, D)
    per_chip = lambda a: a[0::2]
    ffn_conv_w_full = jnp.moveaxis(per_chip(fcw), 0, 2).reshape(2, 3, 2 * HID)
    sc_conv_w_full = jnp.moveaxis(per_chip(scw)[:, 0], 0, 1).reshape(3, D)
    gla_w_a2_full = jnp.moveaxis(per_chip(wa2)[:, 0], 0, 2).reshape(2, RANK, KEY)
    gla_b_a_full = jnp.moveaxis(per_chip(ba)[:, 0], 0, 1).reshape(2, KEY)

    own = {"sc_in_t": sc_w_in[0].T, "up_t0": ffn_w_up[0].T, "up_t1": ffn_w_up[1].T,
           "gla_out": gla_w_out[0], "sc_out": sc_w_out[0], "down0": ffn_w_down[0], "down1": ffn_w_down[1]}
    own_main = jnp.concatenate([own[k].astype(BF16) for k in _MAIN], axis=0)
    own_gin = jnp.pad(gla_w_in[0].T.astype(BF16), ((0, _GIN_PAD - _GIN_ROWS), (0, 0)))
    gla_in_t = _weights_allgather(own_gin, "allgather_gla_in")[:, :_GIN_ROWS, :].reshape(GLA_IN, D)

    cvec = jnp.concatenate([c_all, c_ctx.reshape(1, D), jnp.zeros((ADA_ROWS - N_DEV * nb - 1, D), F32)], axis=0)
    ada_b_sh = lax.dynamic_slice_in_dim(ada_b, chip * ADA_SH, ADA_SH, axis=1).reshape(2, 1, ADA_SH)
    mod_sh = _ada_fwd(cvec, ada_w, ada_b_sh)
    got = _allgather_small(mod_sh.reshape(2 * ADA_ROWS, ADA_SH), "gather_mod")
    mod_full = jnp.moveaxis(per_chip(got.reshape(N_DEV, 2, ADA_ROWS, ADA_SH)), 0, 2).reshape(2, ADA_ROWS, N_MOD * D)
    mods = lax.dynamic_slice_in_dim(mod_full, dev * nb, nb, axis=1).reshape(2, nb, N_MOD, D)
    mc = mod_full[0, N_DEV * nb, :2 * D].reshape(2, D)

    small = {"norm_mix": norm_mix, "norm_ffn": norm_ffn, "final_norm": final_norm, "gla_w_a2": gla_w_a2_full,
             "gla_b_a": gla_b_a_full, "gla_head_norm": gla_head_norm[0], "sc_conv_w": sc_conv_w_full,
             "ffn_conv_w": ffn_conv_w_full, "ffn_conv_b": ffn_conv_b}
    loss_p, grad_x, g_main, g_gin, gs, dmods, dmc = _local_step(x, ctx, loss_target, mods, mc, gla_in_t, own_main,
                                                                place, small)

    sum_names = ["norm_mix", "norm_ffn", "final_norm", "gla_w_a2", "gla_b_a", "gla_head_norm", "sc_conv_w",
                 "ffn_conv_w", "ffn_conv_b"]
    buf, meta = _pack([jnp.broadcast_to(loss_p, (8, 128))] + [gs[k] for k in sum_names] + [dmc, dmods])
    n_sum = meta[-1][0]
    got = _allgather_small(buf, "gather_small_grads").reshape(N_DEV, buf.shape[0], 128)
    summed = _sum_slots(got[:, :n_sum], "sum_small_grads")
    parts = _unpack(summed, meta[:-1])
    loss = parts[0][0, 0]
    g_small = dict(zip(sum_names, parts[1:-1]))
    dmc_tot = parts[-1]
    dmods_all = jnp.moveaxis(_unpack(got, meta[-1:], (N_DEV,))[0], 0, 1).reshape(2, N_DEV * nb, N_MOD * D)

    ctx_row = jnp.stack([jnp.concatenate([dmc_tot.reshape(2 * D), jnp.zeros(((N_MOD - 2) * D,), F32)]),
                         jnp.zeros((N_MOD * D,), F32)]).reshape(2, 1, N_MOD * D)
    dmod_ext = jnp.concatenate([dmods_all, ctx_row, jnp.zeros((2, ADA_ROWS - N_DEV * nb - 1, N_MOD * D), F32)], axis=1)
    g_ada_b = _sum_slots(jnp.moveaxis(dmod_ext, 1, 0).reshape(ADA_ROWS, 2 * N_MOD * D // 128, 128),
                         "sum_ada_b").reshape(2, N_MOD * D)
    dmod_sh = lax.dynamic_slice_in_dim(dmod_ext, chip * ADA_SH, ADA_SH, axis=2)
    g_ada_w, dcv = _ada_bwd(cvec, ada_w, dmod_sh)
    dscc_part = (dcv[0, N_DEV * nb] + dcv[1, N_DEV * nb]).reshape(8, 128)
    got = _allgather_small(dscc_part, "gather_dscc").reshape(N_DEV, 8, 128)
    g_c_ctx = _cctx_grad(per_chip(got), c_ctx.reshape(8, 128)).reshape(D)

    g_packed = jnp.pad(g_gin.reshape(N_CHIPS, _GIN_ROWS, D), ((0, 0), (0, _GIN_PAD - _GIN_ROWS), (0, 0)))
    from_sibling = _rs_pair_exchange(g_packed, "rs_gin_pair_exchange")
    p16, p32 = _rs_chip_sum(place, g_packed, from_sibling, "rs_gin_chip_sum")
    landed = _rs_scatter(p16, "rs_gin_scatter")
    g_gin_shard = _rs_pair_gather(_rs_final_sum(place, landed, p32, "rs_gin_final_sum"), "rs_gin_pair_gather")
    seg = {k: g_main[offs[k]:offs[k] + _MAIN_ROWS[k]] for k in _MAIN}
    seg["gla_in_t"] = g_gin_shard[:_GIN_ROWS]

    sl_chip = lambda a, axis, width: lax.dynamic_slice_in_dim(a, chip * width, width, axis=axis)
    grads = {
        "c_ctx": g_c_ctx, "ada_w": g_ada_w, "ada_b": g_ada_b, "norm_mix": g_small["norm_mix"],
        "norm_ffn": g_small["norm_ffn"],
        "gla_w_in": seg["gla_in_t"].T[None], "gla_w_a2": sl_chip(g_small["gla_w_a2"], 2, KEY // N_CHIPS)[None],
        "gla_b_a": sl_chip(g_small["gla_b_a"], 1, KEY // N_CHIPS)[None],
        "gla_head_norm": g_small["gla_head_norm"][None], "gla_w_out": seg["gla_out"][None],
        "sc_w_in": seg["sc_in_t"].T[None], "sc_conv_w": sl_chip(g_small["sc_conv_w"], 1, D // N_CHIPS)[None],
        "sc_w_out": seg["sc_out"][None], "ffn_w_up": jnp.stack([seg["up_t0"].T, seg["up_t1"].T]),
        "ffn_conv_w": sl_chip(g_small["ffn_conv_w"], 2, 2 * HID // N_CHIPS), "ffn_conv_b": g_small["ffn_conv_b"],
        "ffn_w_down": jnp.stack([seg["down0"], seg["down1"]]), "final_norm": g_small["final_norm"],
    }
    weights = {"c_ctx": c_ctx, "ada_w": ada_w, "ada_b": ada_b, "norm_mix": norm_mix, "norm_ffn": norm_ffn,
               "gla_w_in": gla_w_in, "gla_w_a2": gla_w_a2, "gla_b_a": gla_b_a, "gla_head_norm": gla_head_norm,
               "gla_w_out": gla_w_out, "sc_w_in": sc_w_in, "sc_conv_w": sc_conv_w, "sc_w_out": sc_w_out,
               "ffn_w_up": ffn_w_up, "ffn_conv_w": ffn_conv_w, "ffn_conv_b": ffn_conv_b, "ffn_w_down": ffn_w_down,
               "final_norm": final_norm}
    mom1 = {"c_ctx": m_c_ctx, "ada_w": m_ada_w, "ada_b": m_ada_b, "norm_mix": m_norm_mix, "norm_ffn": m_norm_ffn,
            "gla_w_in": m_gla_w_in, "gla_w_a2": m_gla_w_a2, "gla_b_a": m_gla_b_a, "gla_head_norm": m_gla_head_norm,
            "gla_w_out": m_gla_w_out, "sc_w_in": m_sc_w_in, "sc_conv_w": m_sc_conv_w, "sc_w_out": m_sc_w_out,
            "ffn_w_up": m_ffn_w_up, "ffn_conv_w": m_ffn_conv_w, "ffn_conv_b": m_ffn_conv_b,
            "ffn_w_down": m_ffn_w_down, "final_norm": m_final_norm}
    mom2 = {"c_ctx": v_c_ctx, "ada_w": v_ada_w, "ada_b": v_ada_b, "norm_mix": v_norm_mix, "norm_ffn": v_norm_ffn,
            "gla_w_in": v_gla_w_in, "gla_w_a2": v_gla_w_a2, "gla_b_a": v_gla_b_a, "gla_head_norm": v_gla_head_norm,
            "gla_w_out": v_gla_w_out, "sc_w_in": v_sc_w_in, "sc_conv_w": v_sc_conv_w, "sc_w_out": v_sc_w_out,
            "ffn_w_up": v_ffn_w_up, "ffn_conv_w": v_ffn_conv_w, "ffn_conv_b": v_ffn_conv_b,
            "ffn_w_down": v_ffn_w_down, "final_norm": v_final_norm}
    names = list(weights)
    grads = {k: grads[k].reshape(weights[k].shape) for k in names}

    big_names = ["ada_w", "gla_w_in", "gla_w_out", "sc_w_in", "sc_w_out", "ffn_w_up", "ffn_w_down"]
    small_names = [k for k in names if k not in big_names]
    delta, new_m, new_v = {}, {}, {}
    for k in big_names:
        shp = weights[k].shape
        as2d = lambda a: a.reshape(-1, shp[-1])
        d_, m_, v_ = _adamw(as2d(weights[k]), as2d(grads[k]), as2d(mom1[k]), as2d(mom2[k]), "adamw_" + k)
        delta[k], new_m[k], new_v[k] = d_.reshape(shp), m_.reshape(shp), v_.reshape(shp)
    packed = [_pack([src[k] for k in small_names]) for src in (weights, grads, mom1, mom2)]
    meta = packed[0][1]
    outs = _adamw(packed[0][0], packed[1][0], packed[2][0], packed[3][0], "adamw_small")
    for dst, o in zip((delta, new_m, new_v), outs):
        for k, a in zip(small_names, _unpack(o, meta)):
            dst[k] = a

    return (loss, grad_x, *[grads[k] for k in names], *[delta[k] for k in names], *[new_m[k] for k in names],
            *[new_v[k] for k in names])
```

```python
import functools

import jax
import jax.numpy as jnp
from jax import lax
from jax.experimental import pallas as pl
from jax.experimental.pallas import tpu as pltpu

F32 = jnp.float32
BF16 = jnp.bfloat16
MESH = pl.DeviceIdType.MESH

EPS = 1e-6
D = 1024
N_MOD = 6
HEADS = 4
DK = 128
DV = 256
KEY = HEADS * DK
RANK = 16
TAU = 16.0
CH = 64
GRID_W = 64
HID = 2560
GLA_IN = 2 * KEY + 2 * D + 2 * RANK
GLA_IN_PAD = 3200
Q_SCALE = DK ** -0.5
N_CHIPS = 4
N_DEV = 8

ADAM_LR = 0.001
ADAM_B1 = 0.9
ADAM_B2 = 0.999
ADAM_EPS = 1e-08
ADAM_WD = 0.01
ADAM_STEP = 10

VMEM_LIMIT = 56 * 1024 * 1024


def _params(sem):
    return pltpu.CompilerParams(dimension_semantics=sem, vmem_limit_bytes=VMEM_LIMIT)


def _tile(n, pref, mult=8):
    if n <= pref:
        return n
    for t in range(pref - pref % mult, 0, -mult):
        if n % t == 0:
            return t
    raise ValueError((n, pref, mult))


_NN = (((1,), (0,)), ((), ()))
_NT = (((1,), (1,)), ((), ()))
_TN = (((0,), (0,)), ((), ()))


def _dot(a, b, dims=_NN):
    return lax.dot_general(a.astype(BF16), b.astype(BF16), dims, preferred_element_type=F32)


def _sigmoid(x):
    return 1.0 / (1.0 + jnp.exp(-x))


def _rowsum(x):
    return jnp.sum(x, axis=0, keepdims=True)


def _mm(a, b, form, out_dtype, name, tm, tn):
    if form == "tn":
        K, M = a.shape
    else:
        M, K = a.shape
    N = b.shape[0] if form == "nt" else b.shape[1]
    tm = _tile(M, tm, 128)
    tn = _tile(N, tn, 128)
    dims = {"nn": _NN, "nt": _NT, "tn": _TN}[form]

    def body(a_ref, b_ref, o_ref):
        o_ref[...] = _dot(a_ref[...], b_ref[...], dims).astype(o_ref.dtype)

    if form == "tn":
        a_spec = pl.BlockSpec((K, tm), lambda i, j: (0, i))
    else:
        a_spec = pl.BlockSpec((tm, K), lambda i, j: (i, 0))
    if form == "nt":
        b_spec = pl.BlockSpec((tn, K), lambda i, j: (j, 0))
    else:
        b_spec = pl.BlockSpec((K, tn), lambda i, j: (0, j))
    return pl.pallas_call(
        body,
        name=name,
        grid=(M // tm, N // tn),
        in_specs=[a_spec, b_spec],
        out_specs=pl.BlockSpec((tm, tn), lambda i, j: (i, j)),
        out_shape=jax.ShapeDtypeStruct((M, N), out_dtype),
        compiler_params=_params(("parallel", "parallel")),
    )(a, b)


def _mm_nt_w(a, wg, off, rows, name, tm):
    m = a.shape[0]
    tm = _tile(m, tm, 128)

    def body(a_ref, w_ref, o_ref):
        o_ref[...] = _dot(a_ref[...], w_ref[0], _NT)

    return pl.pallas_call(
        body, name=name, grid=(m // tm, N_CHIPS),
        in_specs=[pl.BlockSpec((tm, D), lambda i, s: (i, 0)),
                  pl.BlockSpec((1, rows, D), lambda i, s: (s, off // rows, 0))],
        out_specs=pl.BlockSpec((tm, rows), lambda i, s: (i, s)),
        out_shape=jax.ShapeDtypeStruct((m, N_CHIPS * rows), F32),
        compiler_params=_params(("parallel", "parallel")),
    )(a, wg)


def _mm_nn_w(a3, wg, off, rows, name, tm, tn):
    parts, m, kp = a3.shape
    assert parts * kp == N_CHIPS * rows
    tm = _tile(m, tm, 128)
    cuts = sorted({s * rows for s in range(N_CHIPS + 1)} | {p * kp for p in range(parts + 1)})
    pieces = [(k0 // kp, k0 % kp, k0 // rows, k0 % rows, k1 - k0) for k0, k1 in zip(cuts[:-1], cuts[1:])]

    def body(a_ref, w_ref, o_ref):
        acc = None
        for p, a0, s, r0, width in pieces:
            term = _dot(a_ref[p, :, a0:a0 + width], w_ref[s, r0:r0 + width, :])
            acc = term if acc is None else acc + term
        o_ref[...] = acc

    return pl.pallas_call(
        body, name=name, grid=(m // tm, D // tn),
        in_specs=[pl.BlockSpec((parts, tm, kp), lambda i, j: (0, i, 0)),
                  pl.BlockSpec((N_CHIPS, rows, tn), lambda i, j: (0, off // rows, j))],
        out_specs=pl.BlockSpec((tm, tn), lambda i, j: (i, j)),
        out_shape=jax.ShapeDtypeStruct((m, D), F32),
        compiler_params=_params(("parallel", "parallel")),
    )(a3, wg)


def _mm_dw(a3, b, g_prev, off, rows, name, tm):
    parts, ntok, cdim = a3.shape
    assert parts * cdim == N_CHIPS * rows and cdim % tm == 0 and rows % tm == 0 and off % tm == 0

    def body(a_ref, b_ref, *rest):
        rest[-1][0] = _dot(a_ref[0], b_ref[...], _TN)

    in_specs = [pl.BlockSpec((1, ntok, tm), lambda i: ((i * tm) // cdim, 0, ((i * tm) % cdim) // tm)),
                pl.BlockSpec((ntok, D), lambda i: (0, 0))]
    args = [a3, b]
    aliases = {}
    if g_prev is not None:
        in_specs.append(pl.BlockSpec(memory_space=pl.ANY))
        args.append(g_prev)
        aliases = {2: 0}
    return pl.pallas_call(
        body, name=name, grid=(N_CHIPS * rows // tm,),
        in_specs=in_specs,
        out_specs=pl.BlockSpec((1, tm, D), lambda i: ((i * tm) // rows, (off + (i * tm) % rows) // tm, 0)),
        out_shape=jax.ShapeDtypeStruct((N_CHIPS, _MAIN_TOTAL, D), F32),
        input_output_aliases=aliases,
        compiler_params=_params(("parallel",)),
    )(*args)


def _mod_fwd(h, gain, shift, scale, tpb_rows, name, y=None, gate=None):
    n = h.shape[0]
    tt = _tile(tpb_rows, 256)
    tpb = tpb_rows // tt
    has_res = y is not None

    def body(*refs):
        if has_res:
            h_ref, y_ref, gate_ref, gain_ref, sh_ref, sc_ref, hout_ref, hn_ref = refs
            hv = h_ref[...] + gate_ref[0] * y_ref[...]
            hout_ref[...] = hv
        else:
            h_ref, gain_ref, sh_ref, sc_ref, hn_ref = refs
            hv = h_ref[...]
        r = lax.rsqrt(jnp.mean(hv * hv, axis=-1, keepdims=True) + EPS)
        hn = (hv * r) * gain_ref[...] * (1.0 + sc_ref[0]) + sh_ref[0]
        hn_ref[...] = hn.astype(BF16)

    row = pl.BlockSpec((tt, D), lambda i: (i, 0))
    per_b = pl.BlockSpec((1, 1, D), lambda i: (i // tpb, 0, 0))
    vec = pl.BlockSpec((1, D), lambda i: (0, 0))
    if has_res:
        in_specs = [row, row, per_b, vec, per_b, per_b]
        args = (h, y, gate, gain, shift, scale)
        out_specs = [row, row]
        out_shape = [jax.ShapeDtypeStruct((n, D), F32), jax.ShapeDtypeStruct((n, D), BF16)]
    else:
        in_specs = [row, vec, per_b, per_b]
        args = (h, gain, shift, scale)
        out_specs = row
        out_shape = jax.ShapeDtypeStruct((n, D), BF16)
    return pl.pallas_call(
        body, name=name, grid=(n // tt,), in_specs=in_specs, out_specs=out_specs, out_shape=out_shape,
        compiler_params=_params(("parallel",)),
    )(*args)


def _mod_bwd(h_in, dhn, gain, scale, tpb_rows, name, dhn_row0=0, dh_out=None, y_prev=None, gate_prev=None,
             need_dh=True):
    n = h_in.shape[0]
    nb = n // tpb_rows
    tt = _tile(tpb_rows, 256)
    tpb = tpb_rows // tt
    off = dhn_row0 // tt
    assert dhn_row0 % tt == 0
    has_out = dh_out is not None
    has_prev = y_prev is not None

    def body(*refs):
        it = iter(refs)
        h_ref, dhn_ref, gain_ref, sc_ref = next(it), next(it), next(it), next(it)
        dho_ref = next(it) if has_out else None
        yp_ref, gp_ref = (next(it), next(it)) if has_prev else (None, None)
        dh_ref = next(it) if need_dh else None
        dsc_ref, dsh_ref, dgain_ref = next(it), next(it), next(it)
        dyp_ref, dgp_ref = (next(it), next(it)) if has_prev else (None, None)
        i = pl.program_id(0)

        @pl.when(i == 0)
        def _():
            dgain_ref[...] = jnp.zeros_like(dgain_ref)

        @pl.when(i % tpb == 0)
        def _():
            dsc_ref[...] = jnp.zeros_like(dsc_ref)
            dsh_ref[...] = jnp.zeros_like(dsh_ref)
            if has_prev:
                dgp_ref[...] = jnp.zeros_like(dgp_ref)

        hv = h_ref[...]
        r = lax.rsqrt(jnp.mean(hv * hv, axis=-1, keepdims=True) + EPS)
        y = hv * r
        gain_v = gain_ref[...]
        g = dhn_ref[...].astype(F32)
        dsh_ref[0] += _rowsum(g)
        dsc_ref[0] += _rowsum(g * (y * gain_v))
        drn = g * (1.0 + sc_ref[0])
        dgain_ref[...] += _rowsum(drn * y)
        if need_dh:
            dy = drn * gain_v
            dh = r * (dy - y * jnp.mean(dy * y, axis=-1, keepdims=True))
            if has_out:
                dh = dh + dho_ref[...]
            dh_ref[...] = dh
            if has_prev:
                dyp_ref[...] = (dh * gp_ref[0]).astype(BF16)
                dgp_ref[0] += _rowsum(dh * yp_ref[...])

    row = pl.BlockSpec((tt, D), lambda i: (i, 0))
    row_off = pl.BlockSpec((tt, D), lambda i: (i + off, 0))
    per_b = pl.BlockSpec((1, 1, D), lambda i: (i // tpb, 0, 0))
    vec = pl.BlockSpec((1, D), lambda i: (0, 0))
    in_specs = [row, row_off, vec, per_b]
    args = [h_in, dhn, gain, scale]
    if has_out:
        in_specs.append(row)
        args.append(dh_out)
    if has_prev:
        in_specs += [row, per_b]
        args += [y_prev, gate_prev]
    out_specs, out_shape, names = [], [], []
    if need_dh:
        out_specs.append(row)
        out_shape.append(jax.ShapeDtypeStruct((n, D), F32))
        names.append("dh")
    for nm in ("dscale", "dshift"):
        out_specs.append(per_b)
        out_shape.append(jax.ShapeDtypeStruct((nb, 1, D), F32))
        names.append(nm)
    out_specs.append(vec)
    out_shape.append(jax.ShapeDtypeStruct((1, D), F32))
    names.append("dgain")
    if has_prev:
        out_specs += [row, per_b]
        out_shape += [jax.ShapeDtypeStruct((n, D), BF16), jax.ShapeDtypeStruct((nb, 1, D), F32)]
        names += ["dy_prev", "dgate_prev"]
    outs = pl.pallas_call(
        body, name=name, grid=(n // tt,), in_specs=in_specs, out_specs=out_specs, out_shape=out_shape,
        compiler_params=_params(("arbitrary",)),
    )(*args)
    return dict(zip(names, outs))


def _final(h, f, gate, gain, tgt, tpb_rows):
    n = h.shape[0]
    nb = n // tpb_rows
    tt = _tile(tpb_rows, 256)
    tpb = tpb_rows // tt

    def body(h_ref, f_ref, gate_ref, gain_ref, tgt_ref, loss_ref, dh_ref, df_ref, dgate_ref, dgain_ref):
        i = pl.program_id(0)

        @pl.when(i == 0)
        def _():
            loss_ref[...] = jnp.zeros_like(loss_ref)
            dgain_ref[...] = jnp.zeros_like(dgain_ref)

        @pl.when(i % tpb == 0)
        def _():
            dgate_ref[...] = jnp.zeros_like(dgate_ref)

        fv = f_ref[...]
        gate_v = gate_ref[0]
        hv = h_ref[...] + gate_v * fv
        r = lax.rsqrt(jnp.mean(hv * hv, axis=-1, keepdims=True) + EPS)
        y = hv * r
        gain_v = gain_ref[...]
        e = y * gain_v - tgt_ref[...]
        s = jnp.sum(_rowsum(e * e), axis=1, keepdims=True) * (0.5 / D)
        loss_ref[...] += jnp.broadcast_to(s, loss_ref.shape)
        dout = e * (1.0 / D)
        dgain_ref[...] += _rowsum(dout * y)
        dy = dout * gain_v
        dh = r * (dy - y * jnp.mean(dy * y, axis=-1, keepdims=True))
        dh_ref[...] = dh
        df_ref[...] = (dh * gate_v).astype(BF16)
        dgate_ref[0] += _rowsum(dh * fv)

    row = pl.BlockSpec((tt, D), lambda i: (i, 0))
    per_b = pl.BlockSpec((1, 1, D), lambda i: (i // tpb, 0, 0))
    vec = pl.BlockSpec((1, D), lambda i: (0, 0))
    return pl.pallas_call(
        body, name="final_loss", grid=(n // tt,),
        in_specs=[row, row, per_b, vec, row],
        out_specs=[pl.BlockSpec((1, 128), lambda i: (0, 0)), row, row, per_b, vec],
        out_shape=[jax.ShapeDtypeStruct((1, 128), F32), jax.ShapeDtypeStruct((n, D), F32),
                   jax.ShapeDtypeStruct((n, D), BF16), jax.ShapeDtypeStruct((nb, 1, D), F32),
                   jax.ShapeDtypeStruct((1, D), F32)],
        compiler_params=_params(("arbitrary",)),
    )(h, f, gate, gain, tgt)


def _shift_dn(x, s):
    return jnp.concatenate([jnp.zeros((s, x.shape[1]), x.dtype), x[: x.shape[0] - s]], axis=0)


def _shift_up(x, s):
    return jnp.concatenate([x[s:], jnp.zeros((s, x.shape[1]), x.dtype)], axis=0)


def _row_dn1(x):
    t = lax.broadcasted_iota(jnp.int32, x.shape, 0)
    return jnp.where(t % GRID_W == 0, 0.0, pltpu.roll(x, 1, 0))


def _row_up1(x):
    t = lax.broadcasted_iota(jnp.int32, x.shape, 0)
    return jnp.where(t % GRID_W == GRID_W - 1, 0.0, pltpu.roll(x, x.shape[0] - 1, 0))


def _silu(x):
    return x * _sigmoid(x)


def _dsilu(x):
    s = _sigmoid(x)
    return s * (1.0 + x * (1.0 - s))


def _conv_cols(x, w_ref):
    return _shift_dn(x, GRID_W) * w_ref[0:1, :] + x * w_ref[1:2, :] + _shift_up(x, GRID_W) * w_ref[2:3, :]


def _conv_cols_bwd(x, du, w_ref, dw_ref, db_ref):
    db_ref[...] += _rowsum(du)
    dw_ref[0:1, :] += _rowsum(du * _shift_dn(x, GRID_W))
    dw_ref[1:2, :] += _rowsum(du * x)
    dw_ref[2:3, :] += _rowsum(du * _shift_up(x, GRID_W))
    return _shift_up(du, GRID_W) * w_ref[0:1, :] + du * w_ref[1:2, :] + _shift_dn(du, GRID_W) * w_ref[2:3, :]


def _ffn_mid_fwd(u0, cw, cb, nb, t, name):
    nc = HID // 128

    def body(ua_ref, ug_ref, wa_ref, wg_ref, ba_ref, bg_ref, z_ref):
        a = _conv_cols(ua_ref[...], wa_ref) + ba_ref[...]
        gt = _conv_cols(ug_ref[...], wg_ref) + bg_ref[...]
        z_ref[...] = (a * _silu(gt)).astype(BF16)

    col = lambda rows, part: pl.BlockSpec((rows, 128), lambda j, b: (b if rows == t else 0, part * nc + j))
    return pl.pallas_call(
        body, name=name, grid=(nc, nb),
        in_specs=[col(t, 0), col(t, 1), col(3, 0), col(3, 1), col(1, 0), col(1, 1)],
        out_specs=pl.BlockSpec((t, 128), lambda j, b: (b, j)),
        out_shape=jax.ShapeDtypeStruct((nb * t, HID), BF16),
        compiler_params=_params(("parallel", "parallel")),
    )(u0, u0, cw, cw, cb, cb)


def _ffn_mid_bwd(u0, cw, cb, dz, nb, t, name):
    nc = HID // 128

    def body(ua_ref, ug_ref, wa_ref, wg_ref, ba_ref, bg_ref, dz_ref, du_ref, dw_ref, db_ref):
        b = pl.program_id(1)

        @pl.when(b == 0)
        def _():
            dw_ref[...] = jnp.zeros_like(dw_ref)
            db_ref[...] = jnp.zeros_like(db_ref)

        xa = ua_ref[...]
        xg = ug_ref[...]
        a = _conv_cols(xa, wa_ref) + ba_ref[...]
        gt = _conv_cols(xg, wg_ref) + bg_ref[...]
        dzv = dz_ref[...]
        du_ref[0] = _conv_cols_bwd(xa, dzv * _silu(gt), wa_ref, dw_ref.at[0], db_ref.at[0]).astype(BF16)
        du_ref[1] = _conv_cols_bwd(xg, dzv * a * _dsilu(gt), wg_ref, dw_ref.at[1], db_ref.at[1]).astype(BF16)

    col = lambda rows, part: pl.BlockSpec((rows, 128), lambda j, b: (b if rows == t else 0, part * nc + j))
    return pl.pallas_call(
        body, name=name, grid=(nc, nb),
        in_specs=[col(t, 0), col(t, 1), col(3, 0), col(3, 1), col(1, 0), col(1, 1),
                  pl.BlockSpec((t, 128), lambda j, b: (b, j))],
        out_specs=[pl.BlockSpec((2, t, 128), lambda j, b: (0, b, j)), pl.BlockSpec((2, 3, 128), lambda j, b: (0, 0, j)),
                   pl.BlockSpec((2, 1, 128), lambda j, b: (0, 0, j))],
        out_shape=[jax.ShapeDtypeStruct((2, nb * t, HID), BF16), jax.ShapeDtypeStruct((2, 3, HID), F32),
                   jax.ShapeDtypeStruct((2, 1, HID), F32)],
        compiler_params=_params(("parallel", "arbitrary")),
    )(u0, u0, cw, cw, cb, cb, dz)


def _sc_mid_fwd(p, cw, nb, t):
    nc = D // 128

    def body(bg_ref, cg_ref, v_ref, w_ref, y_ref):
        cv = cg_ref[...] * v_ref[...]
        cc = _row_dn1(cv) * w_ref[0:1, :] + cv * w_ref[1:2, :] + _row_up1(cv) * w_ref[2:3, :]
        y_ref[...] = (bg_ref[...] * cc).astype(BF16)

    part = lambda k: pl.BlockSpec((t, 128), lambda j, b: (b, k * nc + j))
    return pl.pallas_call(
        body, name="sc_mid_fwd", grid=(nc, nb),
        in_specs=[part(0), part(1), part(2), pl.BlockSpec((3, 128), lambda j, b: (0, j))],
        out_specs=pl.BlockSpec((t, 128), lambda j, b: (b, j)),
        out_shape=jax.ShapeDtypeStruct((nb * t, D), BF16),
        compiler_params=_params(("parallel", "parallel")),
    )(p, p, p, cw)


def _sc_mid_bwd(p, cw, dyb, nb, t):
    nc = D // 128

    def body(bg_ref, cg_ref, v_ref, w_ref, dy_ref, dp_ref, dw_ref):
        b = pl.program_id(1)

        @pl.when(b == 0)
        def _():
            dw_ref[...] = jnp.zeros_like(dw_ref)

        w0, w1, w2 = w_ref[0:1, :], w_ref[1:2, :], w_ref[2:3, :]
        cg, v = cg_ref[...], v_ref[...]
        cv = cg * v
        cvd = _row_dn1(cv)
        cvu = _row_up1(cv)
        cc = cvd * w0 + cv * w1 + cvu * w2
        dy = dy_ref[...]
        dcc = dy * bg_ref[...]
        dw_ref[0:1, :] += _rowsum(dcc * cvd)
        dw_ref[1:2, :] += _rowsum(dcc * cv)
        dw_ref[2:3, :] += _rowsum(dcc * cvu)
        dcv = _row_up1(dcc) * w0 + dcc * w1 + _row_dn1(dcc) * w2
        dp_ref[0] = (dy * cc).astype(BF16)
        dp_ref[1] = (dcv * v).astype(BF16)
        dp_ref[2] = (dcv * cg).astype(BF16)

    part = lambda k: pl.BlockSpec((t, 128), lambda j, b: (b, k * nc + j))
    return pl.pallas_call(
        body, name="sc_mid_bwd", grid=(nc, nb),
        in_specs=[part(0), part(1), part(2), pl.BlockSpec((3, 128), lambda j, b: (0, j)),
                  pl.BlockSpec((t, 128), lambda j, b: (b, j))],
        out_specs=[pl.BlockSpec((3, t, 128), lambda j, b: (0, b, j)), pl.BlockSpec((3, 128), lambda j, b: (0, j))],
        out_shape=[jax.ShapeDtypeStruct((3, nb * t, D), BF16), jax.ShapeDtypeStruct((3, D), F32)],
        compiler_params=_params(("parallel", "arbitrary")),
    )(p, p, p, cw, dyb)


def _gla_decay_fwd(p_all, w2, b2):
    n = p_all.shape[0]
    tt = _tile(n, 512)

    def body(a_ref, w_ref, b_ref, la_ref):
        z = _dot(a_ref[...], w_ref[...]) + b_ref[...]
        la_ref[...] = (jnp.minimum(z, 0.0) - jnp.log(1.0 + jnp.exp(-jnp.abs(z)))) * (1.0 / TAU)

    return pl.pallas_call(
        body, name="gla_decay_fwd", grid=(n // tt,),
        in_specs=[pl.BlockSpec((tt, 128), lambda i: (i, (2 * KEY + 2 * D) // 128)),
                  pl.BlockSpec((128, 2 * KEY), lambda i: (0, 0)), pl.BlockSpec((1, 2 * KEY), lambda i: (0, 0))],
        out_specs=pl.BlockSpec((tt, 2 * KEY), lambda i: (i, 0)),
        out_shape=jax.ShapeDtypeStruct((n, 2 * KEY), F32),
        compiler_params=_params(("parallel",)),
    )(p_all, w2, b2)


def _gla_blocks(nb, nm, ncx):
    def main_idx(d, i):
        return jnp.clip(jnp.where(d == 0, i - ncx, nm - 1 - (i - ncx)), 0, nm - 1)

    def rowblk(d, b, i):
        cidx = jnp.where(d == 0, i, ncx - 1 - i)
        return jnp.where(i < ncx, nb * nm + b * ncx + cidx, b * nm + main_idx(d, i))

    def mainblk(d, b, i):
        return b * nm + main_idx(d, i)

    return rowblk, mainblk


def _gla_mask(d):
    row = lax.broadcasted_iota(jnp.int32, (CH, CH), 0)
    col = lax.broadcasted_iota(jnp.int32, (CH, CH), 1)
    diff = jnp.where(d == 0, row - col, col - row)
    mask = diff >= 0
    return mask, jnp.where(mask, 1.0, 0.0).astype(BF16), jnp.where(diff <= 0, 1.0, 0.0).astype(BF16)


def _tri_sum(m01, x):
    w = x.shape[1]
    hi = x.astype(BF16)
    r1 = x - hi.astype(F32)
    mid = r1.astype(BF16)
    lo = (r1 - mid.astype(F32)).astype(BF16)
    s = lax.dot_general(m01, jnp.concatenate([hi, mid, lo], axis=1), _NN, preferred_element_type=F32)
    return s[:, :w] + s[:, w:2 * w] + s[:, 2 * w:]


def _gla_chunk(q, k, g, bc):
    bl = _rowsum(g)
    eq = jnp.exp(bc)
    ek = jnp.exp(-bc)
    ed = jnp.exp(bl - bc)
    return bl, eq, ek, ed, q * Q_SCALE * eq, k * ek, k * ed


def _gla_scan_fwd(p_all, la_all, nb, t, tc, own):
    nm, ncx = t // CH, tc // CH
    nst = nm + ncx
    rowblk, mainblk = _gla_blocks(nb, nm, ncx)

    def body(q_ref, k_ref, v_ref, la_ref, own_ref, o_ref, ss_ref, wg_ref, st_ref, *sems):
        d = pl.program_id(0)
        b = pl.program_id(1)
        i = pl.program_id(2)
        ag = _AllGather(own_ref, wg_ref, sems)

        @pl.when((d == 0) & (b == 0) & (i == 0))
        def _():
            ag.start()

        @pl.when(i == 0)
        def _():
            st_ref[...] = jnp.zeros_like(st_ref)

        mask, m01, _ = _gla_mask(d)
        q_all, k_all, v_all, g_all = q_ref[...], k_ref[...], v_ref[...], la_ref[...]
        states = [st_ref[h] for h in range(HEADS)]
        bc_all = _tri_sum(m01, g_all)
        outs, new_states = [], []
        for h in range(HEADS):
            ksl = slice(h * DK, (h + 1) * DK)
            v = v_all[:, h * DV:(h + 1) * DV]
            bl, _, _, _, qs, ks, kd = _gla_chunk(q_all[:, ksl], k_all[:, ksl], g_all[:, ksl], bc_all[:, ksl])
            att = jnp.where(mask, _dot(qs, ks, _NT), 0.0)
            outs.append(_dot(qs, states[h], _NT) + _dot(att, v))
            new_states.append(states[h] * jnp.exp(bl) + _dot(v, kd, _TN))
        o_ref[0] = jnp.concatenate(outs, axis=1)
        for h in range(HEADS):
            ss_ref[0, 0, 0, h] = states[h]
            st_ref[h] = new_states[h]

        @pl.when((d == 1) & (b == nb - 1) & (i == nst - 1))
        def _():
            ag.finish()

    any_spec = pl.BlockSpec(memory_space=pl.ANY)
    return pl.pallas_call(
        body, name="gla_scan_fwd", grid=(2, nb, nst),
        in_specs=[
            pl.BlockSpec((CH, KEY), lambda d, b, i: (rowblk(d, b, i), 0)),
            pl.BlockSpec((CH, KEY), lambda d, b, i: (rowblk(d, b, i), 1)),
            pl.BlockSpec((CH, D), lambda d, b, i: (rowblk(d, b, i), 1)),
            pl.BlockSpec((CH, KEY), lambda d, b, i: (rowblk(d, b, i), d)),
            any_spec,
        ],
        out_specs=[
            pl.BlockSpec((1, CH, D), lambda d, b, i: (d, mainblk(d, b, i), 0)),
            pl.BlockSpec((1, 1, 1, HEADS, DV, DK), lambda d, b, i: (d, b, i, 0, 0, 0)),
            any_spec,
        ],
        out_shape=[jax.ShapeDtypeStruct((2, nb * t, D), F32),
                   jax.ShapeDtypeStruct((2, nb, nst, HEADS, DV, DK), F32),
                   jax.ShapeDtypeStruct((N_CHIPS, own.shape[0], D), own.dtype)],
        scratch_shapes=[pltpu.VMEM((HEADS, DV, DK), F32)] + list(_AllGather.SEMS),
        compiler_params=_params(("arbitrary", "arbitrary", "arbitrary")),
    )(p_all, p_all, p_all, la_all, own)


def _gla_scan_bwd(p_all, la_all, do, ss, nb, t, tc, p16):
    nm, ncx = t // CH, tc // CH
    nst = nm + ncx
    ntot = nb * (t + tc)
    rowblk, mainblk = _gla_blocks(nb, nm, ncx)

    def body(q_ref, k_ref, v_ref, la_ref, do_ref, ss_ref, p16_ref, dq_ref, dk_ref, dv_ref, dla_ref, landed_ref,
             dst_ref, *sems):
        d = pl.program_id(0)
        b = pl.program_id(1)
        ip = pl.program_id(2)
        i = nst - 1 - ip
        sc = _Scatter(p16_ref, landed_ref, sems)

        @pl.when((d == 0) & (b == 0) & (ip == 0))
        def _():
            sc.start()

        @pl.when(ip == 0)
        def _():
            dst_ref[...] = jnp.zeros_like(dst_ref)

        mask, m01, m01_t = _gla_mask(d)
        live = jnp.where(i >= ncx, 1.0, 0.0)
        q_all, k_all, v_all, g_all = q_ref[...], k_ref[...], v_ref[...], la_ref[...]
        do_all = do_ref[...] * live
        states = [ss_ref[0, 0, 0, h] for h in range(HEADS)]
        dstates = [dst_ref[h] for h in range(HEADS)]
        bc_all = _tri_sum(m01, g_all)
        dqs_l, dks_l, dvs_l, dbs_l, dbls_l, new_dstates = [], [], [], [], [], []
        for h in range(HEADS):
            ksl = slice(h * DK, (h + 1) * DK)
            vsl = slice(h * DV, (h + 1) * DV)
            bl, eq, ek, ed, qs, ks, kd = _gla_chunk(q_all[:, ksl], k_all[:, ksl], g_all[:, ksl], bc_all[:, ksl])
            st, dst, v, dov = states[h], dstates[h], v_all[:, vsl], do_all[:, vsl]
            att = jnp.where(mask, _dot(qs, ks, _NT), 0.0)
            datt = jnp.where(mask, _dot(dov, v, _NT), 0.0)
            dqs = _dot(dov, st) + _dot(datt, ks)
            dks = _dot(datt, qs, _TN)
            dvs_l.append(_dot(att, dov, _TN) + _dot(kd, dst, _NT))
            dkd = _dot(v, dst)
            e = jnp.exp(bl)
            dbls_l.append(e * _rowsum(st * dst) + _rowsum(dkd * kd))
            new_dstates.append(_dot(dov, qs, _TN) + dst * e)
            dqs_l.append(dqs * eq * Q_SCALE)
            dks_l.append(dks * ek + dkd * ed)
            dbs_l.append(dqs * qs - dks * ks - dkd * kd)
        dq_ref[0] = jnp.concatenate(dqs_l, axis=1)
        dk_ref[0] = jnp.concatenate(dks_l, axis=1)
        dv_ref[0] = jnp.concatenate(dvs_l, axis=1)
        dla_ref[...] = _tri_sum(m01_t, jnp.concatenate(dbs_l, axis=1)) + jnp.concatenate(dbls_l, axis=1)
        for h in range(HEADS):
            dst_ref[h] = new_dstates[h]

        @pl.when((d == 1) & (b == nb - 1) & (ip == nst - 1))
        def _():
            sc.finish()

    rev = lambda f: (lambda d, b, ip: f(d, b, nst - 1 - ip))
    any_spec = pl.BlockSpec(memory_space=pl.ANY)
    return pl.pallas_call(
        body, name="gla_scan_bwd", grid=(2, nb, nst),
        in_specs=[
            pl.BlockSpec((CH, KEY), rev(lambda d, b, i: (rowblk(d, b, i), 0))),
            pl.BlockSpec((CH, KEY), rev(lambda d, b, i: (rowblk(d, b, i), 1))),
            pl.BlockSpec((CH, D), rev(lambda d, b, i: (rowblk(d, b, i), 1))),
            pl.BlockSpec((CH, KEY), rev(lambda d, b, i: (rowblk(d, b, i), d))),
            pl.BlockSpec((CH, D), rev(lambda d, b, i: (mainblk(d, b, i), 0))),
            pl.BlockSpec((1, 1, 1, HEADS, DV, DK), rev(lambda d, b, i: (d, b, i, 0, 0, 0))),
            any_spec,
        ],
        out_specs=[
            pl.BlockSpec((1, CH, KEY), rev(lambda d, b, i: (d, rowblk(d, b, i), 0))),
            pl.BlockSpec((1, CH, KEY), rev(lambda d, b, i: (d, rowblk(d, b, i), 0))),
            pl.BlockSpec((1, CH, D), rev(lambda d, b, i: (d, rowblk(d, b, i), 0))),
            pl.BlockSpec((CH, KEY), rev(lambda d, b, i: (rowblk(d, b, i), d))),
            any_spec,
        ],
        out_shape=[jax.ShapeDtypeStruct((2, ntot, KEY), F32), jax.ShapeDtypeStruct((2, ntot, KEY), F32),
                   jax.ShapeDtypeStruct((2, ntot, D), F32), jax.ShapeDtypeStruct((ntot, 2 * KEY), F32),
                   jax.ShapeDtypeStruct(p16.shape, p16.dtype)],
        scratch_shapes=[pltpu.VMEM((HEADS, DV, DK), F32)] + list(_Scatter.SEMS),
        compiler_params=_params(("arbitrary", "arbitrary", "arbitrary")),
    )(p_all, p_all, p_all, la_all, do, ss, p16)


def _gla_post_fwd(o2, p_all, head_gain, n):
    tt = _tile(n, 256)

    def body(o_ref, g_ref, hg_ref, y_ref):
        o = o_ref[0] + o_ref[1]
        gv = g_ref[...]
        hg = hg_ref[...]
        for h in range(HEADS):
            oh = o[:, h * DV:(h + 1) * DV]
            r = lax.rsqrt(jnp.mean(oh * oh, axis=-1, keepdims=True) + EPS)
            y_ref[:, h * DV:(h + 1) * DV] = ((oh * r) * hg * _silu(gv[:, h * DV:(h + 1) * DV])).astype(BF16)

    return pl.pallas_call(
        body, name="gla_post_fwd", grid=(n // tt,),
        in_specs=[pl.BlockSpec((2, tt, D), lambda i: (0, i, 0)), pl.BlockSpec((tt, D), lambda i: (i, 2)),
                  pl.BlockSpec((1, DV), lambda i: (0, 0))],
        out_specs=pl.BlockSpec((tt, D), lambda i: (i, 0)),
        out_shape=jax.ShapeDtypeStruct((n, D), BF16),
        compiler_params=_params(("parallel",)),
    )(o2, p_all, head_gain)


def _gla_post_bwd(o2, p_all, head_gain, dyb, n):
    tt = _tile(n, 256)

    def body(o_ref, g_ref, hg_ref, dy_ref, do_ref, dg_ref, dhg_ref):
        i = pl.program_id(0)

        @pl.when(i == 0)
        def _():
            dhg_ref[...] = jnp.zeros_like(dhg_ref)

        o = o_ref[0] + o_ref[1]
        gv = g_ref[...]
        hg = hg_ref[...]
        dy = dy_ref[...]
        acc = jnp.zeros((1, DV), F32)
        for h in range(HEADS):
            sl = slice(h * DV, (h + 1) * DV)
            oh = o[:, sl]
            r = lax.rsqrt(jnp.mean(oh * oh, axis=-1, keepdims=True) + EPS)
            on = oh * r
            gh = gv[:, sl]
            dyh = dy[:, sl]
            dg_ref[:, sl] = dyh * (on * hg) * _dsilu(gh)
            dog = dyh * _silu(gh)
            acc = acc + _rowsum(dog * on)
            don = dog * hg
            do_ref[:, sl] = r * (don - on * jnp.mean(don * on, axis=-1, keepdims=True))
        dhg_ref[...] += acc

    return pl.pallas_call(
        body, name="gla_post_bwd", grid=(n // tt,),
        in_specs=[pl.BlockSpec((2, tt, D), lambda i: (0, i, 0)), pl.BlockSpec((tt, D), lambda i: (i, 2)),
                  pl.BlockSpec((1, DV), lambda i: (0, 0)), pl.BlockSpec((tt, D), lambda i: (i, 0))],
        out_specs=[pl.BlockSpec((tt, D), lambda i: (i, 0)), pl.BlockSpec((tt, D), lambda i: (i, 0)),
                   pl.BlockSpec((1, DV), lambda i: (0, 0))],
        out_shape=[jax.ShapeDtypeStruct((n, D), F32), jax.ShapeDtypeStruct((n, D), F32),
                   jax.ShapeDtypeStruct((1, DV), F32)],
        compiler_params=_params(("arbitrary",)),
    )(o2, p_all, head_gain, dyb)


def _gla_assemble(p_all, w2, b2, dq, dk, dv, dla, dgate, n):
    ntot = p_all.shape[0]
    tt = _tile(n, 128)
    nmain = n // tt
    assert ntot % tt == 0

    def body(a_ref, w_ref, b_ref, dq_ref, dk_ref, dv_ref, dla_ref, dg_ref, dp_ref, dw_ref, db_ref):
        i = pl.program_id(0)

        @pl.when(i == 0)
        def _():
            dw_ref[...] = jnp.zeros_like(dw_ref)
            db_ref[...] = jnp.zeros_like(db_ref)

        a = a_ref[...]
        w = w_ref[...]
        z = _dot(a, w) + b_ref[...]
        dz = dla_ref[...] * (1.0 / (1.0 + jnp.exp(z))) * (1.0 / TAU)
        dw_ref[...] += _dot(a, dz, _TN)
        db_ref[...] += _rowsum(dz)
        dp_ref[:, 0:KEY] = ((dq_ref[0] + dq_ref[1]) * 1.0).astype(BF16)
        dp_ref[:, KEY:2 * KEY] = (dk_ref[0] + dk_ref[1]).astype(BF16)
        dp_ref[:, 2 * KEY:2 * KEY + D] = (dv_ref[0] + dv_ref[1]).astype(BF16)
        dp_ref[:, 2 * KEY + D:2 * KEY + 2 * D] = (dg_ref[...] * jnp.where(i < nmain, 1.0, 0.0)).astype(BF16)
        dp_ref[:, 2 * KEY + 2 * D:GLA_IN_PAD] = _dot(dz, w, _NT).astype(BF16)

    return pl.pallas_call(
        body, name="gla_assemble", grid=(ntot // tt,),
        in_specs=[pl.BlockSpec((tt, 128), lambda i: (i, (2 * KEY + 2 * D) // 128)),
                  pl.BlockSpec((128, 2 * KEY), lambda i: (0, 0)), pl.BlockSpec((1, 2 * KEY), lambda i: (0, 0)),
                  pl.BlockSpec((2, tt, KEY), lambda i: (0, i, 0)), pl.BlockSpec((2, tt, KEY), lambda i: (0, i, 0)),
                  pl.BlockSpec((2, tt, D), lambda i: (0, i, 0)), pl.BlockSpec((tt, 2 * KEY), lambda i: (i, 0)),
                  pl.BlockSpec((tt, D), lambda i: (jnp.minimum(i, nmain - 1), 0))],
        out_specs=[pl.BlockSpec((tt, GLA_IN_PAD), lambda i: (i, 0)), pl.BlockSpec((128, 2 * KEY), lambda i: (0, 0)),
                   pl.BlockSpec((1, 2 * KEY), lambda i: (0, 0))],
        out_shape=[jax.ShapeDtypeStruct((ntot, GLA_IN_PAD), BF16), jax.ShapeDtypeStruct((128, 2 * KEY), F32),
                   jax.ShapeDtypeStruct((1, 2 * KEY), F32)],
        compiler_params=_params(("arbitrary",)),
    )(p_all, w2, b2, dq, dk, dv, dla, dgate)


ADA_ROWS = 24
ADA_SH = N_MOD * D // N_CHIPS


def _ada_fwd(cvec, ada_w, ada_b_sh):
    def body(c_ref, w_ref, b_ref, o_ref):
        o_ref[0] = _dot(_silu(c_ref[...]), w_ref[0]) + b_ref[0]

    return pl.pallas_call(
        body, name="ada_fwd", grid=(2,),
        in_specs=[pl.BlockSpec((ADA_ROWS, D), lambda l: (0, 0)), pl.BlockSpec((1, D, ADA_SH), lambda l: (l, 0, 0)),
                  pl.BlockSpec((1, 1, ADA_SH), lambda l: (l, 0, 0))],
        out_specs=pl.BlockSpec((1, ADA_ROWS, ADA_SH), lambda l: (l, 0, 0)),
        out_shape=jax.ShapeDtypeStruct((2, ADA_ROWS, ADA_SH), F32),
        compiler_params=_params(("parallel",)),
    )(cvec, ada_w, ada_b_sh)


def _ada_bwd(cvec, ada_w, dmod_sh):
    def body(c_ref, w_ref, dm_ref, gw_ref, dc_ref):
        dm = dm_ref[0]
        gw_ref[0] = _dot(_silu(c_ref[...]), dm, _TN)
        dc_ref[0] = _dot(dm, w_ref[0], _NT)

    return pl.pallas_call(
        body, name="ada_bwd", grid=(2,),
        in_specs=[pl.BlockSpec((ADA_ROWS, D), lambda l: (0, 0)), pl.BlockSpec((1, D, ADA_SH), lambda l: (l, 0, 0)),
                  pl.BlockSpec((1, ADA_ROWS, ADA_SH), lambda l: (l, 0, 0))],
        out_specs=[pl.BlockSpec((1, D, ADA_SH), lambda l: (l, 0, 0)), pl.BlockSpec((1, ADA_ROWS, D), lambda l: (l, 0, 0))],
        out_shape=[jax.ShapeDtypeStruct((2, D, ADA_SH), F32), jax.ShapeDtypeStruct((2, ADA_ROWS, D), F32)],
        compiler_params=_params(("parallel",)),
    )(cvec, ada_w, dmod_sh)


def _sum_slots(x, name):
    s, r, _ = x.shape

    def body(x_ref, o_ref):
        acc = x_ref[0]
        for k in range(1, s):
            acc = acc + x_ref[k]
        o_ref[...] = acc

    return pl.pallas_call(
        body, name=name, out_shape=jax.ShapeDtypeStruct((r, 128), F32),
        in_specs=[pl.BlockSpec(memory_space=pltpu.VMEM)], out_specs=pl.BlockSpec(memory_space=pltpu.VMEM),
    )(x)


def _cctx_grad(dscc_parts, c_ctx):
    def body(p_ref, c_ref, o_ref):
        acc = p_ref[0]
        for k in range(1, N_CHIPS):
            acc = acc + p_ref[k]
        o_ref[...] = acc * _dsilu(c_ref[...])

    return pl.pallas_call(
        body, name="cctx_grad", out_shape=jax.ShapeDtypeStruct((8, 128), F32),
        in_specs=[pl.BlockSpec(memory_space=pltpu.VMEM)] * 2, out_specs=pl.BlockSpec(memory_space=pltpu.VMEM),
    )(dscc_parts, c_ctx)


def _adamw(w, g, m, v, name):
    r, cdim = w.shape
    tr = _tile(r, 256)
    c1 = 1.0 - ADAM_B1 ** ADAM_STEP
    c2 = 1.0 - ADAM_B2 ** ADAM_STEP

    def body(w_ref, g_ref, m_ref, v_ref, d_ref, mo_ref, vo_ref):
        gv = g_ref[...]
        mn = ADAM_B1 * m_ref[...] + (1.0 - ADAM_B1) * gv
        vn = ADAM_B2 * v_ref[...] + (1.0 - ADAM_B2) * (gv * gv)
        mo_ref[...] = mn
        vo_ref[...] = vn
        d_ref[...] = -ADAM_LR * ((mn / c1) / (jnp.sqrt(vn / c2) + ADAM_EPS) + ADAM_WD * w_ref[...])

    spec = pl.BlockSpec((tr, cdim), lambda i: (i, 0))
    sds = jax.ShapeDtypeStruct((r, cdim), F32)
    return pl.pallas_call(
        body, name=name, grid=(r // tr,), in_specs=[spec] * 4, out_specs=[spec] * 3, out_shape=[sds] * 3,
        compiler_params=_params(("parallel",)),
    )(w, g, m, v)


def _place():
    x, y, c = lax.axis_index("x"), lax.axis_index("y"), lax.axis_index("c")
    return x, y, c


def _allgather_small(blk, name):
    m_per, n = blk.shape

    def body(x_ref, out_ref, send_sems, recv_sems, local_sem):
        x, y, c = _place()
        me, sibling = (x, y, c), (x, y, 1 - c)
        chips = [(1 - x, y), (x, 1 - y), (1 - x, 1 - y)]

        def rows(px, py, pc):
            return out_ref.at[pl.ds((4 * px + 2 * py + pc) * m_per, m_per), :]

        def copy(k, block, to, src=None):
            return pltpu.make_async_remote_copy(
                src_ref=rows(*block) if src is None else src, dst_ref=rows(*block),
                send_sem=send_sems.at[k], recv_sem=recv_sems.at[k], device_id=to, device_id_type=MESH)

        mine = pltpu.make_async_copy(x_ref, rows(*me), local_sem)
        mine.start()
        first = [copy(0, me, sibling, src=x_ref)]
        first += [copy(1 + j, me, (*chip, c), src=x_ref) for j, chip in enumerate(chips)]
        for cp in first:
            cp.start()
        passed = [copy(4 + j, (*chip, c), sibling) for j, chip in enumerate(chips)]
        for j, chip in enumerate(chips):
            copy(1 + j, (*chip, c), me).wait_recv()
            passed[j].start()
        copy(0, sibling, me).wait_recv()
        for j, chip in enumerate(chips):
            copy(4 + j, (*chip, 1 - c), me).wait_recv()
        for cp in first + passed:
            cp.wait_send()
        mine.wait()

    return pl.pallas_call(
        body, name=name,
        out_shape=jax.ShapeDtypeStruct((N_DEV * m_per, n), blk.dtype),
        in_specs=[pl.BlockSpec(memory_space=pltpu.VMEM)],
        out_specs=pl.BlockSpec(memory_space=pltpu.VMEM),
        scratch_shapes=[pltpu.SemaphoreType.DMA((7,)), pltpu.SemaphoreType.DMA((7,)), pltpu.SemaphoreType.DMA],
    )(blk)


def _other_chips(x, y):
    return [(1 - x, y), (x, 1 - y), (1 - x, 1 - y)]


class _AllGather:
    SEMS = [pltpu.SemaphoreType.DMA((3,)), pltpu.SemaphoreType.DMA((3,)), pltpu.SemaphoreType.DMA((3,)),
            pltpu.SemaphoreType.DMA((3,)), pltpu.SemaphoreType.DMA((2,))]

    def __init__(self, own_ref, out_ref, sems):
        self.own_ref, self.out_ref = own_ref, out_ref
        self.send_sems, self.recv_sems, self.fsend_sems, self.frecv_sems, self.own_sems = sems
        self.hr = own_ref.shape[0] // 2

    def _half(self, ch, cc):
        return self.out_ref.at[ch, pl.ds(cc * self.hr, self.hr), :]

    def _own_slot(self):
        x, y, c = _place()
        return pltpu.make_async_remote_copy(
            src_ref=self.own_ref, dst_ref=self.out_ref.at[2 * x + y], send_sem=self.own_sems.at[0],
            recv_sem=self.own_sems.at[1], device_id=(x, y, 1 - c), device_id_type=MESH)

    def _send(self, j, ox, oy):
        x, y, c = _place()
        return pltpu.make_async_remote_copy(
            src_ref=self.own_ref.at[pl.ds(c * self.hr, self.hr), :], dst_ref=self._half(2 * x + y, c),
            send_sem=self.send_sems.at[j], recv_sem=self.recv_sems.at[j], device_id=(ox, oy, c),
            device_id_type=MESH)

    def _landed(self, j, ox, oy):
        x, y, c = _place()
        ref = self._half(2 * ox + oy, c)
        return pltpu.make_async_remote_copy(
            src_ref=ref, dst_ref=ref, send_sem=self.send_sems.at[j], recv_sem=self.recv_sems.at[j],
            device_id=(ox, oy, c), device_id_type=MESH)

    def _pass_on(self, j, ox, oy, cc):
        x, y, c = _place()
        ref = self._half(2 * ox + oy, cc)
        return pltpu.make_async_remote_copy(
            src_ref=ref, dst_ref=ref, send_sem=self.fsend_sems.at[j], recv_sem=self.frecv_sems.at[j],
            device_id=(x, y, 1 - c), device_id_type=MESH)

    def start(self):
        x, y, c = _place()
        self._own_slot().start()
        for j, (ox, oy) in enumerate(_other_chips(x, y)):
            self._send(j, ox, oy).start()

    def finish(self):
        x, y, c = _place()
        others = _other_chips(x, y)
        for j, (ox, oy) in enumerate(others):
            self._landed(j, ox, oy).wait_recv()
            self._pass_on(j, ox, oy, c).start()
        for j, (ox, oy) in enumerate(others):
            self._pass_on(j, ox, oy, 1 - c).wait_recv()
        for j, (ox, oy) in enumerate(others):
            self._send(j, ox, oy).wait_send()
            self._pass_on(j, ox, oy, c).wait_send()
        self._own_slot().wait()


def _weights_allgather(own, name):
    def body(own_ref, out_ref, *sems):
        ag = _AllGather(own_ref, out_ref, sems)
        ag.start()
        ag.finish()

    any_spec = pl.BlockSpec(memory_space=pl.ANY)
    return pl.pallas_call(
        body, name=name,
        out_shape=jax.ShapeDtypeStruct((N_CHIPS, own.shape[0], D), own.dtype),
        in_specs=[any_spec], out_specs=any_spec, scratch_shapes=list(_AllGather.SEMS),
    )(own)


def _rs_pair_exchange(g, name):
    r = g.shape[1]
    hr = r // 2

    def body(g_ref, got_ref, send_sem, recv_sem):
        x, y, c = _place()
        cp = pltpu.make_async_remote_copy(
            src_ref=g_ref.at[:, pl.ds((1 - c) * hr, hr), :], dst_ref=got_ref, send_sem=send_sem, recv_sem=recv_sem,
            device_id=(x, y, 1 - c), device_id_type=MESH)
        cp.start()
        cp.wait()

    any_spec = pl.BlockSpec(memory_space=pl.ANY)
    return pl.pallas_call(
        body, name=name,
        out_shape=jax.ShapeDtypeStruct((N_CHIPS, hr, D), F32),
        in_specs=[any_spec], out_specs=any_spec,
        scratch_shapes=[pltpu.SemaphoreType.DMA, pltpu.SemaphoreType.DMA],
    )(g)


def _rs_chip_sum(place, g, got, name):
    r = g.shape[1]
    hr = r // 2
    tr = _tile(hr, 640, 16)
    nt = hr // tr

    def body(pl_ref, g_ref, got_ref, p16_ref, p32_ref):
        s = pl.program_id(1)
        p = g_ref[0] + got_ref[0]
        p16_ref[0] = p.astype(BF16)

        @pl.when(s == pl_ref[1])
        def _():
            p32_ref[...] = p

    return pl.pallas_call(
        body, name=name,
        grid_spec=pltpu.PrefetchScalarGridSpec(
            num_scalar_prefetch=1, grid=(nt, N_CHIPS),
            in_specs=[pl.BlockSpec((1, tr, D), lambda i, s, pr: (s, pr[0] * nt + i, 0)),
                      pl.BlockSpec((1, tr, D), lambda i, s, pr: (s, i, 0))],
            out_specs=[pl.BlockSpec((1, tr, D), lambda i, s, pr: (s, i, 0)),
                       pl.BlockSpec((tr, D), lambda i, s, pr: (i, 0))]),
        out_shape=[jax.ShapeDtypeStruct((N_CHIPS, hr, D), BF16), jax.ShapeDtypeStruct((hr, D), F32)],
        compiler_params=_params(("parallel", "arbitrary")),
    )(place, g, got)


class _Scatter:
    SEMS = [pltpu.SemaphoreType.DMA((3,)), pltpu.SemaphoreType.DMA((3,))]

    def __init__(self, p_ref, out_ref, sems):
        self.p_ref, self.out_ref = p_ref, out_ref
        self.send_sems, self.recv_sems = sems

    def _copy(self, j, ox, oy, src_slot, dst_slot):
        x, y, c = _place()
        return pltpu.make_async_remote_copy(
            src_ref=self.p_ref.at[src_slot], dst_ref=self.out_ref.at[dst_slot], send_sem=self.send_sems.at[j],
            recv_sem=self.recv_sems.at[j], device_id=(ox, oy, c), device_id_type=MESH)

    def start(self):
        x, y, c = _place()
        for j, (ox, oy) in enumerate(_other_chips(x, y)):
            self._copy(j, ox, oy, 2 * ox + oy, 2 * x + y).start()

    def finish(self):
        x, y, c = _place()
        others = _other_chips(x, y)
        for j, (ox, oy) in enumerate(others):
            self._copy(j, ox, oy, 2 * ox + oy, 2 * ox + oy).wait_recv()
        for j, (ox, oy) in enumerate(others):
            self._copy(j, ox, oy, 2 * ox + oy, 2 * x + y).wait_send()


def _rs_scatter(p16, name):
    def body(p_ref, out_ref, *sems):
        sc = _Scatter(p_ref, out_ref, sems)
        sc.start()
        sc.finish()

    any_spec = pl.BlockSpec(memory_space=pl.ANY)
    return pl.pallas_call(
        body, name=name,
        out_shape=jax.ShapeDtypeStruct(p16.shape, p16.dtype),
        in_specs=[any_spec], out_specs=any_spec, scratch_shapes=list(_Scatter.SEMS),
    )(p16)


def _rs_final_sum(place, parts, p32, name):
    hr = parts.shape[1]
    tr = _tile(hr, 640, 16)
    nt = hr // tr

    def body(pl_ref, a_ref, b_ref, c_ref, p32_ref, o_ref):
        o_ref[...] = ((p32_ref[...] + a_ref[0].astype(F32)) + b_ref[0].astype(F32)) + c_ref[0].astype(F32)

    def other(j):
        return pl.BlockSpec((1, tr, D), lambda i, pr: (j + jnp.where(pr[1] <= j, 1, 0), i, 0))

    return pl.pallas_call(
        body, name=name,
        grid_spec=pltpu.PrefetchScalarGridSpec(
            num_scalar_prefetch=1, grid=(nt,),
            in_specs=[other(0), other(1), other(2), pl.BlockSpec((tr, D), lambda i, pr: (i, 0))],
            out_specs=pl.BlockSpec((tr, D), lambda i, pr: (pr[0] * nt + i, 0))),
        out_shape=jax.ShapeDtypeStruct((2 * hr, D), F32),
        compiler_params=_params(("parallel",)),
    )(place, parts, parts, parts, p32)


def _rs_pair_gather(both, name):
    hr = both.shape[0] // 2

    def body(in_ref, out_ref, send_sem, recv_sem):
        x, y, c = _place()
        mine = out_ref.at[pl.ds(c * hr, hr), :]
        cp = pltpu.make_async_remote_copy(
            src_ref=mine, dst_ref=mine, send_sem=send_sem, recv_sem=recv_sem,
            device_id=(x, y, 1 - c), device_id_type=MESH)
        cp.start()
        theirs = out_ref.at[pl.ds((1 - c) * hr, hr), :]
        pltpu.make_async_remote_copy(
            src_ref=theirs, dst_ref=theirs, send_sem=send_sem, recv_sem=recv_sem,
            device_id=(x, y, 1 - c), device_id_type=MESH).wait_recv()
        cp.wait_send()

    any_spec = pl.BlockSpec(memory_space=pl.ANY)
    return pl.pallas_call(
        body, name=name,
        out_shape=jax.ShapeDtypeStruct(both.shape, F32),
        in_specs=[any_spec], out_specs=any_spec, input_output_aliases={0: 0},
        scratch_shapes=[pltpu.SemaphoreType.DMA, pltpu.SemaphoreType.DMA],
    )(both)


def _local_step(x, ctx, tgt, mods, mc, gla_in_t, own_main, place, small):
    nb, t, _ = x.shape
    tc = ctx.shape[1]
    n = nb * t
    nc = nb * tc
    xf = x.reshape(n, D)
    cf = ctx.reshape(nc, D)
    tf = tgt.reshape(n, D)
    vec = lambda a: a.reshape(1, -1)
    m = [[mods[l, :, k, :].reshape(nb, 1, D) for k in range(N_MOD)] for l in range(2)]
    mc_b = [jnp.broadcast_to(mc[k].reshape(1, 1, D), (nb, 1, D)) for k in range(2)]

    w_gin = jnp.pad(gla_in_t, ((0, GLA_IN_PAD - GLA_IN), (0, 0)))
    cw = [small["ffn_conv_w"][l] for l in range(2)]
    cb = [small["ffn_conv_b"][l].reshape(1, -1) for l in range(2)]
    w2 = jnp.zeros((128, 2 * KEY), F32)
    w2 = w2.at[0:RANK, 0:KEY].set(small["gla_w_a2"][0]).at[RANK:2 * RANK, KEY:].set(small["gla_w_a2"][1])
    b2 = small["gla_b_a"].reshape(1, 2 * KEY)
    hg = small["gla_head_norm"].reshape(1, DV)

    hn0 = _mod_fwd(xf, vec(small["norm_mix"][0]), m[0][0], m[0][1], t, "mod0_main")
    hnc = _mod_fwd(cf, vec(small["norm_mix"][0]), mc_b[0], mc_b[1], tc, "mod0_ctx")
    hn_all = jnp.concatenate([hn0, hnc], axis=0)
    p_all = _mm(hn_all, w_gin, "nt", F32, "gla_in_proj", 768, 3200)
    la_all = _gla_decay_fwd(p_all, w2, b2)
    o2, ss, wg = _gla_scan_fwd(p_all, la_all, nb, t, tc, own_main)
    offs = _offsets(_MAIN, _MAIN_ROWS)
    rows = _MAIN_ROWS

    def w_nt(a, k, name, tm=1024):
        return _mm_nt_w(a, wg, offs[k], rows[k], name, tm)

    def w_nn(a3, k, name, tm, tn):
        return _mm_nn_w(a3, wg, offs[k], rows[k], name, tm, tn)

    yb0 = _gla_post_fwd(o2, p_all, hg, n)
    y0 = w_nn(yb0[None], "gla_out", "gla_out_proj", 1024, 1024)
    h1, hn1 = _mod_fwd(xf, vec(small["norm_ffn"][0]), m[0][3], m[0][4], t, "mod0_ffn", y=y0, gate=m[0][2])
    u0 = w_nt(hn1, "up_t0", "ffn0_up")
    z0 = _ffn_mid_fwd(u0, cw[0], cb[0], nb, t, "ffn0_mid_fwd")
    f0 = w_nn(z0[None], "down0", "ffn0_down", 1024, 1024)
    h2, hn2 = _mod_fwd(h1, vec(small["norm_mix"][1]), m[1][0], m[1][1], t, "mod1_mix", y=f0, gate=m[0][5])
    p1 = w_nt(hn2, "sc_in_t", "sc_in_proj")
    yb1 = _sc_mid_fwd(p1, small["sc_conv_w"], nb, t)
    y1 = w_nn(yb1[None], "sc_out", "sc_out_proj", 1024, 1024)
    h3, hn3 = _mod_fwd(h2, vec(small["norm_ffn"][1]), m[1][3], m[1][4], t, "mod1_ffn", y=y1, gate=m[1][2])
    u1 = w_nt(hn3, "up_t1", "ffn1_up")
    z1 = _ffn_mid_fwd(u1, cw[1], cb[1], nb, t, "ffn1_mid_fwd")
    f1 = w_nn(z1[None], "down1", "ffn1_down", 1024, 1024)
    loss, dh4, df1, dm15, dfinal = _final(h3, f1, m[1][5], vec(small["final_norm"]), tf, t)

    gs = {}
    dmods = [[None] * N_MOD for _ in range(2)]
    dmods[1][5] = dm15

    def w_dw(a3, b, g_prev, k, name, tm):
        return _mm_dw(a3, b, g_prev, offs[k], rows[k], name, tm)

    def ffn_bwd(l, df, u, z, hn, g_prev):
        dz = w_nt(df, f"down{l}", f"ffn{l}_down_dx")
        g_acc = w_dw(z[None], df, g_prev, f"down{l}", f"ffn{l}_down_dw", 640)
        du, dcw, dcb = _ffn_mid_bwd(u, cw[l], cb[l], dz, nb, t, f"ffn{l}_mid_bwd")
        dhn = w_nn(du, f"up_t{l}", f"ffn{l}_up_dx", 512, 512)
        g_acc = w_dw(du, hn, g_acc, f"up_t{l}", f"ffn{l}_up_dw", 640)
        return dhn, g_acc, jnp.moveaxis(dcw, 0, 1).reshape(3, 2 * HID), dcb.reshape(2 * HID)

    dhn3, g_acc, dcw1, dcb1 = ffn_bwd(1, df1, u1, z1, hn3, None)
    r = _mod_bwd(h3, dhn3, vec(small["norm_ffn"][1]), m[1][4], t, "mod1_ffn_bwd", dh_out=dh4, y_prev=y1,
                 gate_prev=m[1][2])
    dh3, dmods[1][4], dmods[1][3], dnf1, dy1, dmods[1][2] = (r["dh"], r["dscale"], r["dshift"], r["dgain"],
                                                             r["dy_prev"], r["dgate_prev"])
    dyb1 = w_nt(dy1, "sc_out", "sc_out_dx")
    g_acc = w_dw(yb1[None], dy1, g_acc, "sc_out", "sc_out_dw", 256)
    dp1, dscw = _sc_mid_bwd(p1, small["sc_conv_w"], dyb1, nb, t)
    dhn2 = w_nn(dp1, "sc_in_t", "sc_in_dx", 1024, 512)
    g_acc = w_dw(dp1, hn2, g_acc, "sc_in_t", "sc_in_dw", 256)
    r = _mod_bwd(h2, dhn2, vec(small["norm_mix"][1]), m[1][1], t, "mod1_mix_bwd", dh_out=dh3, y_prev=f0,
                 gate_prev=m[0][5])
    dh2, dmods[1][1], dmods[1][0], dnm1, df0, dmods[0][5] = (r["dh"], r["dscale"], r["dshift"], r["dgain"],
                                                             r["dy_prev"], r["dgate_prev"])
    dhn1, g_acc, dcw0, dcb0 = ffn_bwd(0, df0, u0, z0, hn1, g_acc)
    r = _mod_bwd(h1, dhn1, vec(small["norm_ffn"][0]), m[0][4], t, "mod0_ffn_bwd", dh_out=dh2, y_prev=y0,
                 gate_prev=m[0][2])
    dh1, dmods[0][4], dmods[0][3], dnf0, dy0, dmods[0][2] = (r["dh"], r["dscale"], r["dshift"], r["dgain"],
                                                             r["dy_prev"], r["dgate_prev"])
    dyb0 = w_nt(dy0, "gla_out", "gla_out_dx")
    g_packed = w_dw(yb0[None], dy0, g_acc, "gla_out", "gla_out_dw", 256)
    from_sibling = _rs_pair_exchange(g_packed, "rs_main_pair_exchange")
    p16, p32 = _rs_chip_sum(place, g_packed, from_sibling, "rs_main_chip_sum")
    do, dgate, dhg = _gla_post_bwd(o2, p_all, hg, dyb0, n)
    dq, dk, dv, dla, landed = _gla_scan_bwd(p_all, la_all, do, ss, nb, t, tc, p16)
    g_main = _rs_pair_gather(_rs_final_sum(place, landed, p32, "rs_main_final_sum"), "rs_main_pair_gather")
    dp, dw2, db2 = _gla_assemble(p_all, w2, b2, dq, dk, dv, dla, dgate, n)
    dhn_all = _mm(dp, w_gin, "nn", F32, "gla_in_dx", 768, 512)
    g_gin = _mm(dp, hn_all, "tn", F32, "gla_in_dw", 640, 1024)[:GLA_IN]
    r = _mod_bwd(xf, dhn_all, vec(small["norm_mix"][0]), m[0][1], t, "mod0_main_bwd", dh_out=dh1)
    grad_x, dmods[0][1], dmods[0][0], dnm0 = r["dh"], r["dscale"], r["dshift"], r["dgain"]
    rc = _mod_bwd(cf, dhn_all, vec(small["norm_mix"][0]), mc_b[1], tc, "mod0_ctx_bwd", dhn_row0=n, need_dh=False)
    dmc = jnp.stack([jnp.sum(rc["dshift"], axis=0).reshape(D), jnp.sum(rc["dscale"], axis=0).reshape(D)])
    dnm0 = dnm0 + rc["dgain"]

    gs["norm_mix"] = jnp.concatenate([dnm0, dnm1], axis=0)
    gs["norm_ffn"] = jnp.concatenate([dnf0, dnf1], axis=0)
    gs["final_norm"] = dfinal.reshape(D)
    gs["gla_w_a2"] = jnp.stack([dw2[0:RANK, 0:KEY], dw2[RANK:2 * RANK, KEY:]])
    gs["gla_b_a"] = db2.reshape(2, KEY)
    gs["gla_head_norm"] = dhg.reshape(DV)
    gs["sc_conv_w"] = dscw
    gs["ffn_conv_w"] = jnp.stack([dcw0, dcw1])
    gs["ffn_conv_b"] = jnp.stack([dcb0, dcb1])
    dmods_arr = jnp.stack([jnp.stack([dmods[l][k].reshape(nb, D) for k in range(N_MOD)], axis=1) for l in range(2)])
    return loss, grad_x.reshape(nb, t, D), g_main, g_gin, gs, dmods_arr, dmc


def _pack(arrs):
    parts, meta, off = [], [], 0
    for a in arrs:
        r = a.size // 128
        rp = -(-r // 8) * 8
        a2 = a.reshape(r, 128).astype(F32)
        if rp != r:
            a2 = jnp.pad(a2, ((0, rp - r), (0, 0)))
        parts.append(a2)
        meta.append((off, r, a.shape))
        off += rp
    return jnp.concatenate(parts, axis=0), meta


def _unpack(buf, meta, lead=()):
    return [buf[..., off:off + r, :].reshape(*lead, *shape) for off, r, shape in meta]


_MAIN = ("up_t0", "up_t1", "down0", "down1", "sc_in_t", "gla_out", "sc_out")
_MAIN_ROWS = {"sc_in_t": 3 * D // N_CHIPS, "up_t0": 2 * HID // N_CHIPS, "up_t1": 2 * HID // N_CHIPS,
              "gla_out": D // N_CHIPS, "sc_out": D // N_CHIPS, "down0": HID // N_CHIPS, "down1": HID // N_CHIPS}
_MAIN_TOTAL = sum(_MAIN_ROWS.values())
_GIN_ROWS = GLA_IN // N_CHIPS
_GIN_PAD = -(-_GIN_ROWS // 32) * 32


def _offsets(names, rows):
    off, out = 0, {}
    for k in names:
        out[k] = off
        off += rows[k]
    return out


def kernel(x, c, ctx, c_ctx, ada_w, ada_b, norm_mix, norm_ffn, gla_w_in, gla_w_a2, gla_b_a, gla_head_norm, gla_w_out, sc_w_in, sc_conv_w, sc_w_out, ffn_w_up, ffn_conv_w, ffn_conv_b, ffn_w_down, final_norm, loss_target, m_c_ctx, m_ada_w, m_ada_b, m_norm_mix, m_norm_ffn, m_gla_w_in, m_gla_w_a2, m_gla_b_a, m_gla_head_norm, m_gla_w_out, m_sc_w_in, m_sc_conv_w, m_sc_w_out, m_ffn_w_up, m_ffn_conv_w, m_ffn_conv_b, m_ffn_w_down, m_final_norm, v_c_ctx, v_ada_w, v_ada_b, v_norm_mix, v_norm_ffn, v_gla_w_in, v_gla_w_a2, v_gla_b_a, v_gla_head_norm, v_gla_w_out, v_sc_w_in, v_sc_conv_w, v_sc_w_out, v_ffn_w_up, v_ffn_conv_w, v_ffn_conv_b, v_ffn_w_down, v_final_norm):
    ix, iy, ic = _place()
    chip = 2 * ix + iy
    dev = 2 * chip + ic
    place = jnp.stack([ic, chip]).astype(jnp.int32)
    nb = x.shape[0]
    offs = _offsets(_MAIN, _MAIN_ROWS)

    buf, meta = _pack([c, ffn_conv_w, sc_conv_w, gla_w_a2, gla_b_a])
    got = _allgather_small(buf, "gather_small_in").reshape(N_DEV, buf.shape[0], 128)
    c_all, fcw, scw, wa2, ba = _unpack(got, meta, (N_DEV,))
    c_all = c_all.reshape(N_DEV * nb, D)
    per_chip = lambda a: a[0::2]
    ffn_conv_w_full = jnp.moveaxis(per_chip(fcw), 0, 2).reshape(2, 3, 2 * HID)
    sc_conv_w_full = jnp.moveaxis(per_chip(scw)[:, 0], 0, 1).reshape(3, D)
    gla_w_a2_full = jnp.moveaxis(per_chip(wa2)[:, 0], 0, 2).reshape(2, RANK, KEY)
    gla_b_a_full = jnp.moveaxis(per_chip(ba)[:, 0], 0, 1).reshape(2, KEY)

    own = {"sc_in_t": sc_w_in[0].T, "up_t0": ffn_w_up[0].T, "up_t1": ffn_w_up[1].T,
           "gla_out": gla_w_out[0], "sc_out": sc_w_out[0], "down0": ffn_w_down[0], "down1": ffn_w_down[1]}
    own_main = jnp.concatenate([own[k].astype(BF16) for k in _MAIN], axis=0)
    own_gin = jnp.pad(gla_w_in[0].T.astype(BF16), ((0, _GIN_PAD - _GIN_ROWS), (0, 0)))
    gla_in_t = _weights_allgather(own_gin, "allgather_gla_in")[:, :_GIN_ROWS, :].reshape(GLA_IN, D)

    cvec = jnp.concatenate([c_all, c_ctx.reshape(1, D), jnp.zeros((ADA_ROWS - N_DEV * nb - 1, D), F32)], axis=0)
    ada_b_sh = lax.dynamic_slice_in_dim(ada_b, chip * ADA_SH, ADA_SH, axis=1).reshape(2, 1, ADA_SH)
    mod_sh = _ada_fwd(cvec, ada_w, ada_b_sh)
    got = _allgather_small(mod_sh.reshape(2 * ADA_ROWS, ADA_SH), "gather_mod")
    mod_full = jnp.moveaxis(per_chip(got.reshape(N_DEV, 2, ADA_ROWS, ADA_SH)), 0, 2).reshape(2, ADA_ROWS, N_MOD * D)
    mods = lax.dynamic_slice_in_dim(mod_full, dev * nb, nb, axis=1).reshape(2, nb, N_MOD, D)
    mc = mod_full[0, N_DEV * nb, :2 * D].reshape(2, D)

    small = {"norm_mix": norm_mix, "norm_ffn": norm_ffn, "final_norm": final_norm, "gla_w_a2": gla_w_a2_full,
             "gla_b_a": gla_b_a_full, "gla_head_norm": gla_head_norm[0], "sc_conv_w": sc_conv_w_full,
             "ffn_conv_w": ffn_conv_w_full, "ffn_conv_b": ffn_conv_b}
    loss_p, grad_x, g_main, g_gin, gs, dmods, dmc = _local_step(x, ctx, loss_target, mods, mc, gla_in_t, own_main,
                                                                place, small)

    sum_names = ["norm_mix", "norm_ffn", "final_norm", "gla_w_a2", "gla_b_a", "gla_head_norm", "sc_conv_w",
                 "ffn_conv_w", "ffn_conv_b"]
    buf, meta = _pack([jnp.broadcast_to(loss_p, (8, 128))] + [gs[k] for k in sum_names] + [dmc, dmods])
    n_sum = meta[-1][0]
    got = _allgather_small(buf, "gather_small_grads").reshape(N_DEV, buf.shape[0], 128)
    summed = _sum_slots(got[:, :n_sum], "sum_small_grads")
    parts = _unpack(summed, meta[:-1])
    loss = parts[0][0, 0]
    g_small = dict(zip(sum_names, parts[1:-1]))
    dmc_tot = parts[-1]
    dmods_all = jnp.moveaxis(_unpack(got, meta[-1:], (N_DEV,))[0], 0, 1).reshape(2, N_DEV * nb, N_MOD * D)

    ctx_row = jnp.stack([jnp.concatenate([dmc_tot.reshape(2 * D), jnp.zeros(((N_MOD - 2) * D,), F32)]),
                         jnp.zeros((N_MOD * D,), F32)]).reshape(2, 1, N_MOD * D)
    dmod_ext = jnp.concatenate([dmods_all, ctx_row, jnp.zeros((2, ADA_ROWS - N_DEV * nb - 1, N_MOD * D), F32)], axis=1)
    g_ada_b = _sum_slots(jnp.moveaxis(dmod_ext, 1, 0).reshape(ADA_ROWS, 2 * N_MOD * D // 128, 128),
                         "sum_ada_b").reshape(2, N_MOD * D)
    dmod_sh = lax.dynamic_slice_in_dim(dmod_ext, chip * ADA_SH, ADA_SH, axis=2)
    g_ada_w, dcv = _ada_bwd(cvec, ada_w, dmod_sh)
    dscc_part = (dcv[0, N_DEV * nb] + dcv[1, N_DEV * nb]).reshape(8, 128)
    got = _allgather_small(dscc_part, "gather_dscc").reshape(N_DEV, 8, 128)
    g_c_ctx = _cctx_grad(per_chip(got), c_ctx.reshape(8, 128)).reshape(D)

    g_packed = jnp.pad(g_gin.reshape(N_CHIPS, _GIN_ROWS, D), ((0, 0), (0, _GIN_PAD - _GIN_ROWS), (0, 0)))
    from_sibling = _rs_pair_exchange(g_packed, "rs_gin_pair_exchange")
    p16, p32 = _rs_chip_sum(place, g_packed, from_sibling, "rs_gin_chip_sum")
    landed = _rs_scatter(p16, "rs_gin_scatter")
    g_gin_shard = _rs_pair_gather(_rs_final_sum(place, landed, p32, "rs_gin_final_sum"), "rs_gin_pair_gather")
    seg = {k: g_main[offs[k]:offs[k] + _MAIN_ROWS[k]] for k in _MAIN}
    seg["gla_in_t"] = g_gin_shard[:_GIN_ROWS]

    sl_chip = lambda a, axis, width: lax.dynamic_slice_in_dim(a, chip * width, width, axis=axis)
    grads = {
        "c_ctx": g_c_ctx, "ada_w": g_ada_w, "ada_b": g_ada_b, "norm_mix": g_small["norm_mix"],
        "norm_ffn": g_small["norm_ffn"],
        "gla_w_in": seg["gla_in_t"].T[None], "gla_w_a2": sl_chip(g_small["gla_w_a2"], 2, KEY // N_CHIPS)[None],
        "gla_b_a": sl_chip(g_small["gla_b_a"], 1, KEY // N_CHIPS)[None],
        "gla_head_norm": g_small["gla_head_norm"][None], "gla_w_out": seg["gla_out"][None],
        "sc_w_in": seg["sc_in_t"].T[None], "sc_conv_w": sl_chip(g_small["sc_conv_w"], 1, D // N_CHIPS)[None],
        "sc_w_out": seg["sc_out"][None], "ffn_w_up": jnp.stack([seg["up_t0"].T, seg["up_t1"].T]),
        "ffn_conv_w": sl_chip(g_small["ffn_conv_w"], 2, 2 * HID // N_CHIPS), "ffn_conv_b": g_small["ffn_conv_b"],
        "ffn_w_down": jnp.stack([seg["down0"], seg["down1"]]), "final_norm": g_small["final_norm"],
    }
    weights = {"c_ctx": c_ctx, "ada_w": ada_w, "ada_b": ada_b, "norm_mix": norm_mix, "norm_ffn": norm_ffn,
               "gla_w_in": gla_w_in, "gla_w_a2": gla_w_a2, "gla_b_a": gla_b_a, "gla_head_norm": gla_head_norm,
               "gla_w_out": gla_w_out, "sc_w_in": sc_w_in, "sc_conv_w": sc_conv_w, "sc_w_out": sc_w_out,
               "ffn_w_up": ffn_w_up, "ffn_conv_w": ffn_conv_w, "ffn_conv_b": ffn_conv_b, "ffn_w_down": ffn_w_down,
               "final_norm": final_norm}
    mom1 = {"c_ctx": m_c_ctx, "ada_w": m_ada_w, "ada_b": m_ada_b, "norm_mix": m_norm_mix, "norm_ffn": m_norm_ffn,
            "gla_w_in": m_gla_w_in, "gla_w_a2": m_gla_w_a2, "gla_b_a": m_gla_b_a, "gla_head_norm": m_gla_head_norm,
            "gla_w_out": m_gla_w_out, "sc_w_in": m_sc_w_in, "sc_conv_w": m_sc_conv_w, "sc_w_out": m_sc_w_out,
            "ffn_w_up": m_ffn_w_up, "ffn_conv_w": m_ffn_conv_w, "ffn_conv_b": m_ffn_conv_b,
            "ffn_w_down": m_ffn_w_down, "final_norm": m_final_norm}
    mom2 = {"c_ctx": v_c_ctx, "ada_w": v_ada_w, "ada_b": v_ada_b, "norm_mix": v_norm_mix, "norm_ffn": v_norm_ffn,
            "gla_w_in": v_gla_w_in, "gla_w_a2": v_gla_w_a2, "gla_b_a": v_gla_b_a, "gla_head_norm": v_gla_head_norm,
            "gla_w_out": v_gla_w_out, "sc_w_in": v_sc_w_in, "sc_conv_w": v_sc_conv_w, "sc_w_out": v_sc_w_out,
            "ffn_w_up": v_ffn_w_up, "ffn_conv_w": v_ffn_conv_w, "ffn_conv_b": v_ffn_conv_b,
            "ffn_w_down": v_ffn_w_down, "final_norm": v_final_norm}
    names = list(weights)
    grads = {k: grads[k].reshape(weights[k].shape) for k in names}

    big_names = ["ada_w", "gla_w_in", "gla_w_out", "sc_w_in", "sc_w_out", "ffn_w_up", "ffn_w_down"]
    small_names = [k for k in names if k not in big_names]
    delta, new_m, new_v = {}, {}, {}
    for k in big_names:
        shp = weights[k].shape
        as2d = lambda a: a.reshape(-1, shp[-1])
        d_, m_, v_ = _adamw(as2d(weights[k]), as2d(grads[k]), as2d(mom1[k]), as2d(mom2[k]), "adamw_" + k)
        delta[k], new_m[k], new_v[k] = d_.reshape(shp), m_.reshape(shp), v_.reshape(shp)
    packed = [_pack([src[k] for k in small_names]) for src in (weights, grads, mom1, mom2)]
    meta = packed[0][1]
    outs = _adamw(packed[0][0], packed[1][0], packed[2][0], packed[3][0], "adamw_small")
    for dst, o in zip((delta, new_m, new_v), outs):
        for k, a in zip(small_names, _unpack(o, meta)):
            dst[k] = a

    return (loss, grad_x, *[grads[k] for k in names], *[delta[k] for k in names], *[new_m[k] for k in names],
            *[new_v[k] for k in names])
```

```python
import functools

import jax
import jax.numpy as jnp
from jax import lax
from jax.experimental import pallas as pl
from jax.experimental.pallas import tpu as pltpu

F32 = jnp.float32
BF16 = jnp.bfloat16
MESH = pl.DeviceIdType.MESH

EPS = 1e-6
D = 1024
N_MOD = 6
HEADS = 4
DK = 128
DV = 256
KEY = HEADS * DK
RANK = 16
TAU = 16.0
CH = 64
GRID_W = 64
HID = 2560
GLA_IN = 2 * KEY + 2 * D + 2 * RANK
GLA_IN_PAD = 3200
Q_SCALE = DK ** -0.5
N_CHIPS = 4
N_DEV = 8

ADAM_LR = 0.001
ADAM_B1 = 0.9
ADAM_B2 = 0.999
ADAM_EPS = 1e-08
ADAM_WD = 0.01
ADAM_STEP = 10

VMEM_LIMIT = 56 * 1024 * 1024


def _params(sem):
    return pltpu.CompilerParams(dimension_semantics=sem, vmem_limit_bytes=VMEM_LIMIT)


def _tile(n, pref, mult=8):
    if n <= pref:
        return n
    for t in range(pref - pref % mult, 0, -mult):
        if n % t == 0:
            return t
    raise ValueError((n, pref, mult))


_NN = (((1,), (0,)), ((), ()))
_NT = (((1,), (1,)), ((), ()))
_TN = (((0,), (0,)), ((), ()))


def _dot(a, b, dims=_NN):
    return lax.dot_general(a.astype(BF16), b.astype(BF16), dims, preferred_element_type=F32)


def _sigmoid(x):
    return 1.0 / (1.0 + jnp.exp(-x))


def _rowsum(x):
    return jnp.sum(x, axis=0, keepdims=True)


def _mm(a, b, form, out_dtype, name, tm, tn):
    if form == "tn":
        K, M = a.shape
    else:
        M, K = a.shape
    N = b.shape[0] if form == "nt" else b.shape[1]
    tm = _tile(M, tm, 128)
    tn = _tile(N, tn, 128)
    dims = {"nn": _NN, "nt": _NT, "tn": _TN}[form]

    def body(a_ref, b_ref, o_ref):
        o_ref[...] = _dot(a_ref[...], b_ref[...], dims).astype(o_ref.dtype)

    if form == "tn":
        a_spec = pl.BlockSpec((K, tm), lambda i, j: (0, i))
    else:
        a_spec = pl.BlockSpec((tm, K), lambda i, j: (i, 0))
    if form == "nt":
        b_spec = pl.BlockSpec((tn, K), lambda i, j: (j, 0))
    else:
        b_spec = pl.BlockSpec((K, tn), lambda i, j: (0, j))
    return pl.pallas_call(
        body,
        name=name,
        grid=(M // tm, N // tn),
        in_specs=[a_spec, b_spec],
        out_specs=pl.BlockSpec((tm, tn), lambda i, j: (i, j)),
        out_shape=jax.ShapeDtypeStruct((M, N), out_dtype),
        compiler_params=_params(("parallel", "parallel")),
    )(a, b)


def _mm_nt_w(a, wg, off, rows, name, tm):
    m = a.shape[0]
    tm = _tile(m, tm, 128)

    def body(a_ref, w_ref, o_ref):
        o_ref[...] = _dot(a_ref[...], w_ref[0], _NT)

    return pl.pallas_call(
        body, name=name, grid=(m // tm, N_CHIPS),
        in_specs=[pl.BlockSpec((tm, D), lambda i, s: (i, 0)),
                  pl.BlockSpec((1, rows, D), lambda i, s: (s, off // rows, 0))],
        out_specs=pl.BlockSpec((tm, rows), lambda i, s: (i, s)),
        out_shape=jax.ShapeDtypeStruct((m, N_CHIPS * rows), F32),
        compiler_params=_params(("parallel", "parallel")),
    )(a, wg)


def _mm_nn_w(a3, wg, off, rows, name, tm, tn):
    parts, m, kp = a3.shape
    assert parts * kp == N_CHIPS * rows
    tm = _tile(m, tm, 128)
    cuts = sorted({s * rows for s in range(N_CHIPS + 1)} | {p * kp for p in range(parts + 1)})
    pieces = [(k0 // kp, k0 % kp, k0 // rows, k0 % rows, k1 - k0) for k0, k1 in zip(cuts[:-1], cuts[1:])]

    def body(a_ref, w_ref, o_ref):
        acc = None
        for p, a0, s, r0, width in pieces:
            term = _dot(a_ref[p, :, a0:a0 + width], w_ref[s, r0:r0 + width, :])
            acc = term if acc is None else acc + term
        o_ref[...] = acc

    return pl.pallas_call(
        body, name=name, grid=(m // tm, D // tn),
        in_specs=[pl.BlockSpec((parts, tm, kp), lambda i, j: (0, i, 0)),
                  pl.BlockSpec((N_CHIPS, rows, tn), lambda i, j: (0, off // rows, j))],
        out_specs=pl.BlockSpec((tm, tn), lambda i, j: (i, j)),
        out_shape=jax.ShapeDtypeStruct((m, D), F32),
        compiler_params=_params(("parallel", "parallel")),
    )(a3, wg)


def _mm_dw(a3, b, g_prev, off, rows, name, tm):
    parts, ntok, cdim = a3.shape
    assert parts * cdim == N_CHIPS * rows and cdim % tm == 0 and rows % tm == 0 and off % tm == 0

    def body(a_ref, b_ref, *rest):
        rest[-1][0] = _dot(a_ref[0], b_ref[...], _TN)

    in_specs = [pl.BlockSpec((1, ntok, tm), lambda i: ((i * tm) // cdim, 0, ((i * tm) % cdim) // tm)),
                pl.BlockSpec((ntok, D), lambda i: (0, 0))]
    args = [a3, b]
    aliases = {}
    if g_prev is not None:
        in_specs.append(pl.BlockSpec(memory_space=pl.ANY))
        args.append(g_prev)
        aliases = {2: 0}
    return pl.pallas_call(
        body, name=name, grid=(N_CHIPS * rows // tm,),
        in_specs=in_specs,
        out_specs=pl.BlockSpec((1, tm, D), lambda i: ((i * tm) // rows, (off + (i * tm) % rows) // tm, 0)),
        out_shape=jax.ShapeDtypeStruct((N_CHIPS, _MAIN_TOTAL, D), F32),
        input_output_aliases=aliases,
        compiler_params=_params(("parallel",)),
    )(*args)


def _mod_fwd(h, gain, shift, scale, tpb_rows, name, y=None, gate=None):
    n = h.shape[0]
    tt = _tile(tpb_rows, 256)
    tpb = tpb_rows // tt
    has_res = y is not None

    def body(*refs):
        if has_res:
            h_ref, y_ref, gate_ref, gain_ref, sh_ref, sc_ref, hout_ref, hn_ref = refs
            hv = h_ref[...] + gate_ref[0] * y_ref[...]
            hout_ref[...] = hv
        else:
            h_ref, gain_ref, sh_ref, sc_ref, hn_ref = refs
            hv = h_ref[...]
        r = lax.rsqrt(jnp.mean(hv * hv, axis=-1, keepdims=True) + EPS)
        hn = (hv * r) * gain_ref[...] * (1.0 + sc_ref[0]) + sh_ref[0]
        hn_ref[...] = hn.astype(BF16)

    row = pl.BlockSpec((tt, D), lambda i: (i, 0))
    per_b = pl.BlockSpec((1, 1, D), lambda i: (i // tpb, 0, 0))
    vec = pl.BlockSpec((1, D), lambda i: (0, 0))
    if has_res:
        in_specs = [row, row, per_b, vec, per_b, per_b]
        args = (h, y, gate, gain, shift, scale)
        out_specs = [row, row]
        out_shape = [jax.ShapeDtypeStruct((n, D), F32), jax.ShapeDtypeStruct((n, D), BF16)]
    else:
        in_specs = [row, vec, per_b, per_b]
        args = (h, gain, shift, scale)
        out_specs = row
        out_shape = jax.ShapeDtypeStruct((n, D), BF16)
    return pl.pallas_call(
        body, name=name, grid=(n // tt,), in_specs=in_specs, out_specs=out_specs, out_shape=out_shape,
        compiler_params=_params(("parallel",)),
    )(*args)


def _mod_bwd(h_in, dhn, gain, scale, tpb_rows, name, dhn_row0=0, dh_out=None, y_prev=None, gate_prev=None,
             need_dh=True):
    n = h_in.shape[0]
    nb = n // tpb_rows
    tt = _tile(tpb_rows, 256)
    tpb = tpb_rows // tt
    off = dhn_row0 // tt
    assert dhn_row0 % tt == 0
    has_out = dh_out is not None
    has_prev = y_prev is not None

    def body(*refs):
        it = iter(refs)
        h_ref, dhn_ref, gain_ref, sc_ref = next(it), next(it), next(it), next(it)
        dho_ref = next(it) if has_out else None
        yp_ref, gp_ref = (next(it), next(it)) if has_prev else (None, None)
        dh_ref = next(it) if need_dh else None
        dsc_ref, dsh_ref, dgain_ref = next(it), next(it), next(it)
        dyp_ref, dgp_ref = (next(it), next(it)) if has_prev else (None, None)
        i = pl.program_id(0)

        @pl.when(i == 0)
        def _():
            dgain_ref[...] = jnp.zeros_like(dgain_ref)

        @pl.when(i % tpb == 0)
        def _():
            dsc_ref[...] = jnp.zeros_like(dsc_ref)
            dsh_ref[...] = jnp.zeros_like(dsh_ref)
            if has_prev:
                dgp_ref[...] = jnp.zeros_like(dgp_ref)

        hv = h_ref[...]
        r = lax.rsqrt(jnp.mean(hv * hv, axis=-1, keepdims=True) + EPS)
        y = hv * r
        gain_v = gain_ref[...]
        g = dhn_ref[...].astype(F32)
        dsh_ref[0] += _rowsum(g)
        dsc_ref[0] += _rowsum(g * (y * gain_v))
        drn = g * (1.0 + sc_ref[0])
        dgain_ref[...] += _rowsum(drn * y)
        if need_dh:
            dy = drn * gain_v
            dh = r * (dy - y * jnp.mean(dy * y, axis=-1, keepdims=True))
            if has_out:
                dh = dh + dho_ref[...]
            dh_ref[...] = dh
            if has_prev:
                dyp_ref[...] = (dh * gp_ref[0]).astype(BF16)
                dgp_ref[0] += _rowsum(dh * yp_ref[...])

    row = pl.BlockSpec((tt, D), lambda i: (i, 0))
    row_off = pl.BlockSpec((tt, D), lambda i: (i + off, 0))
    per_b = pl.BlockSpec((1, 1, D), lambda i: (i // tpb, 0, 0))
    vec = pl.BlockSpec((1, D), lambda i: (0, 0))
    in_specs = [row, row_off, vec, per_b]
    args = [h_in, dhn, gain, scale]
    if has_out:
        in_specs.append(row)
        args.append(dh_out)
    if has_prev:
        in_specs += [row, per_b]
        args += [y_prev, gate_prev]
    out_specs, out_shape, names = [], [], []
    if need_dh:
        out_specs.append(row)
        out_shape.append(jax.ShapeDtypeStruct((n, D), F32))
        names.append("dh")
    for nm in ("dscale", "dshift"):
        out_specs.append(per_b)
        out_shape.append(jax.ShapeDtypeStruct((nb, 1, D), F32))
        names.append(nm)
    out_specs.append(vec)
    out_shape.append(jax.ShapeDtypeStruct((1, D), F32))
    names.append("dgain")
    if has_prev:
        out_specs += [row, per_b]
        out_shape += [jax.ShapeDtypeStruct((n, D), BF16), jax.ShapeDtypeStruct((nb, 1, D), F32)]
        names += ["dy_prev", "dgate_prev"]
    outs = pl.pallas_call(
        body, name=name, grid=(n // tt,), in_specs=in_specs, out_specs=out_specs, out_shape=out_shape,
        compiler_params=_params(("arbitrary",)),
    )(*args)
    return dict(zip(names, outs))


def _final(h, f, gate, gain, tgt, tpb_rows):
    n = h.shape[0]
    nb = n // tpb_rows
    tt = _tile(tpb_rows, 256)
    tpb = tpb_rows // tt

    def body(h_ref, f_ref, gate_ref, gain_ref, tgt_ref, loss_ref, dh_ref, df_ref, dgate_ref, dgain_ref):
        i = pl.program_id(0)

        @pl.when(i == 0)
        def _():
            loss_ref[...] = jnp.zeros_like(loss_ref)
            dgain_ref[...] = jnp.zeros_like(dgain_ref)

        @pl.when(i % tpb == 0)
        def _():
            dgate_ref[...] = jnp.zeros_like(dgate_ref)

        fv = f_ref[...]
        gate_v = gate_ref[0]
        hv = h_ref[...] + gate_v * fv
        r = lax.rsqrt(jnp.mean(hv * hv, axis=-1, keepdims=True) + EPS)
        y = hv * r
        gain_v = gain_ref[...]
        e = y * gain_v - tgt_ref[...]
        s = jnp.sum(_rowsum(e * e), axis=1, keepdims=True) * (0.5 / D)
        loss_ref[...] += jnp.broadcast_to(s, loss_ref.shape)
        dout = e * (1.0 / D)
        dgain_ref[...] += _rowsum(dout * y)
        dy = dout * gain_v
        dh = r * (dy - y * jnp.mean(dy * y, axis=-1, keepdims=True))
        dh_ref[...] = dh
        df_ref[...] = (dh * gate_v).astype(BF16)
        dgate_ref[0] += _rowsum(dh * fv)

    row = pl.BlockSpec((tt, D), lambda i: (i, 0))
    per_b = pl.BlockSpec((1, 1, D), lambda i: (i // tpb, 0, 0))
    vec = pl.BlockSpec((1, D), lambda i: (0, 0))
    return pl.pallas_call(
        body, name="final_loss", grid=(n // tt,),
        in_specs=[row, row, per_b, vec, row],
        out_specs=[pl.BlockSpec((1, 128), lambda i: (0, 0)), row, row, per_b, vec],
        out_shape=[jax.ShapeDtypeStruct((1, 128), F32), jax.ShapeDtypeStruct((n, D), F32),
                   jax.ShapeDtypeStruct((n, D), BF16), jax.ShapeDtypeStruct((nb, 1, D), F32),
                   jax.ShapeDtypeStruct((1, D), F32)],
        compiler_params=_params(("arbitrary",)),
    )(h, f, gate, gain, tgt)


def _shift_dn(x, s):
    return jnp.concatenate([jnp.zeros((s, x.shape[1]), x.dtype), x[: x.shape[0] - s]], axis=0)


def _shift_up(x, s):
    return jnp.concatenate([x[s:], jnp.zeros((s, x.shape[1]), x.dtype)], axis=0)


def _row_dn1(x):
    t = lax.broadcasted_iota(jnp.int32, x.shape, 0)
    return jnp.where(t % GRID_W == 0, 0.0, pltpu.roll(x, 1, 0))


def _row_up1(x):
    t = lax.broadcasted_iota(jnp.int32, x.shape, 0)
    return jnp.where(t % GRID_W == GRID_W - 1, 0.0, pltpu.roll(x, x.shape[0] - 1, 0))


def _silu(x):
    return x * _sigmoid(x)


def _dsilu(x):
    s = _sigmoid(x)
    return s * (1.0 + x * (1.0 - s))


def _conv_cols(x, w_ref):
    return _shift_dn(x, GRID_W) * w_ref[0:1, :] + x * w_ref[1:2, :] + _shift_up(x, GRID_W) * w_ref[2:3, :]


def _conv_cols_bwd(x, du, w_ref, dw_ref, db_ref):
    db_ref[...] += _rowsum(du)
    dw_ref[0:1, :] += _rowsum(du * _shift_dn(x, GRID_W))
    dw_ref[1:2, :] += _rowsum(du * x)
    dw_ref[2:3, :] += _rowsum(du * _shift_up(x, GRID_W))
    return _shift_up(du, GRID_W) * w_ref[0:1, :] + du * w_ref[1:2, :] + _shift_dn(du, GRID_W) * w_ref[2:3, :]


def _ffn_mid_fwd(u0, cw, cb, nb, t, name):
    nc = HID // 128

    def body(ua_ref, ug_ref, wa_ref, wg_ref, ba_ref, bg_ref, z_ref):
        a = _conv_cols(ua_ref[...], wa_ref) + ba_ref[...]
        gt = _conv_cols(ug_ref[...], wg_ref) + bg_ref[...]
        z_ref[...] = (a * _silu(gt)).astype(BF16)

    col = lambda rows, part: pl.BlockSpec((rows, 128), lambda j, b: (b if rows == t else 0, part * nc + j))
    return pl.pallas_call(
        body, name=name, grid=(nc, nb),
        in_specs=[col(t, 0), col(t, 1), col(3, 0), col(3, 1), col(1, 0), col(1, 1)],
        out_specs=pl.BlockSpec((t, 128), lambda j, b: (b, j)),
        out_shape=jax.ShapeDtypeStruct((nb * t, HID), BF16),
        compiler_params=_params(("parallel", "parallel")),
    )(u0, u0, cw, cw, cb, cb)


def _ffn_mid_bwd(u0, cw, cb, dz, nb, t, name):
    nc = HID // 128

    def body(ua_ref, ug_ref, wa_ref, wg_ref, ba_ref, bg_ref, dz_ref, du_ref, dw_ref, db_ref):
        b = pl.program_id(1)

        @pl.when(b == 0)
        def _():
            dw_ref[...] = jnp.zeros_like(dw_ref)
            db_ref[...] = jnp.zeros_like(db_ref)

        xa = ua_ref[...]
        xg = ug_ref[...]
        a = _conv_cols(xa, wa_ref) + ba_ref[...]
        gt = _conv_cols(xg, wg_ref) + bg_ref[...]
        dzv = dz_ref[...]
        du_ref[0] = _conv_cols_bwd(xa, dzv * _silu(gt), wa_ref, dw_ref.at[0], db_ref.at[0]).astype(BF16)
        du_ref[1] = _conv_cols_bwd(xg, dzv * a * _dsilu(gt), wg_ref, dw_ref.at[1], db_ref.at[1]).astype(BF16)

    col = lambda rows, part: pl.BlockSpec((rows, 128), lambda j, b: (b if rows == t else 0, part * nc + j))
    return pl.pallas_call(
        body, name=name, grid=(nc, nb),
        in_specs=[col(t, 0), col(t, 1), col(3, 0), col(3, 1), col(1, 0), col(1, 1),
                  pl.BlockSpec((t, 128), lambda j, b: (b, j))],
        out_specs=[pl.BlockSpec((2, t, 128), lambda j, b: (0, b, j)), pl.BlockSpec((2, 3, 128), lambda j, b: (0, 0, j)),
                   pl.BlockSpec((2, 1, 128), lambda j, b: (0, 0, j))],
        out_shape=[jax.ShapeDtypeStruct((2, nb * t, HID), BF16), jax.ShapeDtypeStruct((2, 3, HID), F32),
                   jax.ShapeDtypeStruct((2, 1, HID), F32)],
        compiler_params=_params(("parallel", "arbitrary")),
    )(u0, u0, cw, cw, cb, cb, dz)


def _sc_mid_fwd(p, cw, nb, t):
    nc = D // 128

    def body(bg_ref, cg_ref, v_ref, w_ref, y_ref):
        cv = cg_ref[...] * v_ref[...]
        cc = _row_dn1(cv) * w_ref[0:1, :] + cv * w_ref[1:2, :] + _row_up1(cv) * w_ref[2:3, :]
        y_ref[...] = (bg_ref[...] * cc).astype(BF16)

    part = lambda k: pl.BlockSpec((t, 128), lambda j, b: (b, k * nc + j))
    return pl.pallas_call(
        body, name="sc_mid_fwd", grid=(nc, nb),
        in_specs=[part(0), part(1), part(2), pl.BlockSpec((3, 128), lambda j, b: (0, j))],
        out_specs=pl.BlockSpec((t, 128), lambda j, b: (b, j)),
        out_shape=jax.ShapeDtypeStruct((nb * t, D), BF16),
        compiler_params=_params(("parallel", "parallel")),
    )(p, p, p, cw)


def _sc_mid_bwd(p, cw, dyb, nb, t):
    nc = D // 128

    def body(bg_ref, cg_ref, v_ref, w_ref, dy_ref, dp_ref, dw_ref):
        b = pl.program_id(1)

        @pl.when(b == 0)
        def _():
            dw_ref[...] = jnp.zeros_like(dw_ref)

        w0, w1, w2 = w_ref[0:1, :], w_ref[1:2, :], w_ref[2:3, :]
        cg, v = cg_ref[...], v_ref[...]
        cv = cg * v
        cvd = _row_dn1(cv)
        cvu = _row_up1(cv)
        cc = cvd * w0 + cv * w1 + cvu * w2
        dy = dy_ref[...]
        dcc = dy * bg_ref[...]
        dw_ref[0:1, :] += _rowsum(dcc * cvd)
        dw_ref[1:2, :] += _rowsum(dcc * cv)
        dw_ref[2:3, :] += _rowsum(dcc * cvu)
        dcv = _row_up1(dcc) * w0 + dcc * w1 + _row_dn1(dcc) * w2
        dp_ref[0] = (dy * cc).astype(BF16)
        dp_ref[1] = (dcv * v).astype(BF16)
        dp_ref[2] = (dcv * cg).astype(BF16)

    part = lambda k: pl.BlockSpec((t, 128), lambda j, b: (b, k * nc + j))
    return pl.pallas_call(
        body, name="sc_mid_bwd", grid=(nc, nb),
        in_specs=[part(0), part(1), part(2), pl.BlockSpec((3, 128), lambda j, b: (0, j)),
                  pl.BlockSpec((t, 128), lambda j, b: (b, j))],
        out_specs=[pl.BlockSpec((3, t, 128), lambda j, b: (0, b, j)), pl.BlockSpec((3, 128), lambda j, b: (0, j))],
        out_shape=[jax.ShapeDtypeStruct((3, nb * t, D), BF16), jax.ShapeDtypeStruct((3, D), F32)],
        compiler_params=_params(("parallel", "arbitrary")),
    )(p, p, p, cw, dyb)


def _gla_decay_fwd(p_all, w2, b2):
    n = p_all.shape[0]
    tt = _tile(n, 512)

    def body(a_ref, w_ref, b_ref, la_ref):
        z = _dot(a_ref[...], w_ref[...]) + b_ref[...]
        la_ref[...] = (jnp.minimum(z, 0.0) - jnp.log(1.0 + jnp.exp(-jnp.abs(z)))) * (1.0 / TAU)

    return pl.pallas_call(
        body, name="gla_decay_fwd", grid=(n // tt,),
        in_specs=[pl.BlockSpec((tt, 128), lambda i: (i, (2 * KEY + 2 * D) // 128)),
                  pl.BlockSpec((128, 2 * KEY), lambda i: (0, 0)), pl.BlockSpec((1, 2 * KEY), lambda i: (0, 0))],
        out_specs=pl.BlockSpec((tt, 2 * KEY), lambda i: (i, 0)),
        out_shape=jax.ShapeDtypeStruct((n, 2 * KEY), F32),
        compiler_params=_params(("parallel",)),
    )(p_all, w2, b2)


def _gla_blocks(nb, nm, ncx):
    def main_idx(d, i):
        return jnp.clip(jnp.where(d == 0, i - ncx, nm - 1 - (i - ncx)), 0, nm - 1)

    def rowblk(d, b, i):
        cidx = jnp.where(d == 0, i, ncx - 1 - i)
        return jnp.where(i < ncx, nb * nm + b * ncx + cidx, b * nm + main_idx(d, i))

    def mainblk(d, b, i):
        return b * nm + main_idx(d, i)

    return rowblk, mainblk


def _gla_mask(d):
    row = lax.broadcasted_iota(jnp.int32, (CH, CH), 0)
    col = lax.broadcasted_iota(jnp.int32, (CH, CH), 1)
    diff = jnp.where(d == 0, row - col, col - row)
    mask = diff >= 0
    return mask, jnp.where(mask, 1.0, 0.0).astype(BF16), jnp.where(diff <= 0, 1.0, 0.0).astype(BF16)


def _tri_sum(m01, x):
    w = x.shape[1]
    hi = x.astype(BF16)
    r1 = x - hi.astype(F32)
    mid = r1.astype(BF16)
    lo = (r1 - mid.astype(F32)).astype(BF16)
    s = lax.dot_general(m01, jnp.concatenate([hi, mid, lo], axis=1), _NN, preferred_element_type=F32)
    return s[:, :w] + s[:, w:2 * w] + s[:, 2 * w:]


def _gla_chunk(q, k, g, bc):
    bl = _rowsum(g)
    eq = jnp.exp(bc)
    ek = jnp.exp(-bc)
    ed = jnp.exp(bl - bc)
    return bl, eq, ek, ed, q * Q_SCALE * eq, k * ek, k * ed


def _gla_scan_fwd(p_all, la_all, nb, t, tc):
    nm, ncx = t // CH, tc // CH
    nst = nm + ncx
    rowblk, mainblk = _gla_blocks(nb, nm, ncx)

    def body(q_ref, k_ref, v_ref, la_ref, o_ref, ss_ref, st_ref):
        d = pl.program_id(0)
        i = pl.program_id(2)

        @pl.when(i == 0)
        def _():
            st_ref[...] = jnp.zeros_like(st_ref)

        mask, m01, _ = _gla_mask(d)
        q_all, k_all, v_all, g_all = q_ref[...], k_ref[...], v_ref[...], la_ref[...]
        states = [st_ref[h] for h in range(HEADS)]
        bc_all = _tri_sum(m01, g_all)
        outs, new_states = [], []
        for h in range(HEADS):
            ksl = slice(h * DK, (h + 1) * DK)
            v = v_all[:, h * DV:(h + 1) * DV]
            bl, _, _, _, qs, ks, kd = _gla_chunk(q_all[:, ksl], k_all[:, ksl], g_all[:, ksl], bc_all[:, ksl])
            att = jnp.where(mask, _dot(qs, ks, _NT), 0.0)
            outs.append(_dot(qs, states[h], _NT) + _dot(att, v))
            new_states.append(states[h] * jnp.exp(bl) + _dot(v, kd, _TN))
        o_ref[0] = jnp.concatenate(outs, axis=1)
        for h in range(HEADS):
            ss_ref[0, 0, 0, h] = states[h]
            st_ref[h] = new_states[h]

    return pl.pallas_call(
        body, name="gla_scan_fwd", grid=(2, nb, nst),
        in_specs=[
            pl.BlockSpec((CH, KEY), lambda d, b, i: (rowblk(d, b, i), 0)),
            pl.BlockSpec((CH, KEY), lambda d, b, i: (rowblk(d, b, i), 1)),
            pl.BlockSpec((CH, D), lambda d, b, i: (rowblk(d, b, i), 1)),
            pl.BlockSpec((CH, KEY), lambda d, b, i: (rowblk(d, b, i), d)),
        ],
        out_specs=[
            pl.BlockSpec((1, CH, D), lambda d, b, i: (d, mainblk(d, b, i), 0)),
            pl.BlockSpec((1, 1, 1, HEADS, DV, DK), lambda d, b, i: (d, b, i, 0, 0, 0)),
        ],
        out_shape=[jax.ShapeDtypeStruct((2, nb * t, D), F32),
                   jax.ShapeDtypeStruct((2, nb, nst, HEADS, DV, DK), F32)],
        scratch_shapes=[pltpu.VMEM((HEADS, DV, DK), F32)],
        compiler_params=_params(("parallel", "parallel", "arbitrary")),
    )(p_all, p_all, p_all, la_all)


def _gla_scan_bwd(p_all, la_all, do, ss, nb, t, tc):
    nm, ncx = t // CH, tc // CH
    nst = nm + ncx
    ntot = nb * (t + tc)
    rowblk, mainblk = _gla_blocks(nb, nm, ncx)

    def body(q_ref, k_ref, v_ref, la_ref, do_ref, ss_ref, dq_ref, dk_ref, dv_ref, dla_ref, dst_ref):
        d = pl.program_id(0)
        ip = pl.program_id(2)
        i = nst - 1 - ip

        @pl.when(ip == 0)
        def _():
            dst_ref[...] = jnp.zeros_like(dst_ref)

        mask, m01, m01_t = _gla_mask(d)
        live = jnp.where(i >= ncx, 1.0, 0.0)
        q_all, k_all, v_all, g_all = q_ref[...], k_ref[...], v_ref[...], la_ref[...]
        do_all = do_ref[...] * live
        states = [ss_ref[0, 0, 0, h] for h in range(HEADS)]
        dstates = [dst_ref[h] for h in range(HEADS)]
        bc_all = _tri_sum(m01, g_all)
        dqs_l, dks_l, dvs_l, dbs_l, dbls_l, new_dstates = [], [], [], [], [], []
        for h in range(HEADS):
            ksl = slice(h * DK, (h + 1) * DK)
            vsl = slice(h * DV, (h + 1) * DV)
            bl, eq, ek, ed, qs, ks, kd = _gla_chunk(q_all[:, ksl], k_all[:, ksl], g_all[:, ksl], bc_all[:, ksl])
            st, dst, v, dov = states[h], dstates[h], v_all[:, vsl], do_all[:, vsl]
            att = jnp.where(mask, _dot(qs, ks, _NT), 0.0)
            datt = jnp.where(mask, _dot(dov, v, _NT), 0.0)
            dqs = _dot(dov, st) + _dot(datt, ks)
            dks = _dot(datt, qs, _TN)
            dvs_l.append(_dot(att, dov, _TN) + _dot(kd, dst, _NT))
            dkd = _dot(v, dst)
            e = jnp.exp(bl)
            dbls_l.append(e * _rowsum(st * dst) + _rowsum(dkd * kd))
            new_dstates.append(_dot(dov, qs, _TN) + dst * e)
            dqs_l.append(dqs * eq * Q_SCALE)
            dks_l.append(dks * ek + dkd * ed)
            dbs_l.append(dqs * qs - dks * ks - dkd * kd)
        dq_ref[0] = jnp.concatenate(dqs_l, axis=1)
        dk_ref[0] = jnp.concatenate(dks_l, axis=1)
        dv_ref[0] = jnp.concatenate(dvs_l, axis=1)
        dla_ref[...] = _tri_sum(m01_t, jnp.concatenate(dbs_l, axis=1)) + jnp.concatenate(dbls_l, axis=1)
        for h in range(HEADS):
            dst_ref[h] = new_dstates[h]

    rev = lambda f: (lambda d, b, ip: f(d, b, nst - 1 - ip))
    return pl.pallas_call(
        body, name="gla_scan_bwd", grid=(2, nb, nst),
        in_specs=[
            pl.BlockSpec((CH, KEY), rev(lambda d, b, i: (rowblk(d, b, i), 0))),
            pl.BlockSpec((CH, KEY), rev(lambda d, b, i: (rowblk(d, b, i), 1))),
            pl.BlockSpec((CH, D), rev(lambda d, b, i: (rowblk(d, b, i), 1))),
            pl.BlockSpec((CH, KEY), rev(lambda d, b, i: (rowblk(d, b, i), d))),
            pl.BlockSpec((CH, D), rev(lambda d, b, i: (mainblk(d, b, i), 0))),
            pl.BlockSpec((1, 1, 1, HEADS, DV, DK), rev(lambda d, b, i: (d, b, i, 0, 0, 0))),
        ],
        out_specs=[
            pl.BlockSpec((1, CH, KEY), rev(lambda d, b, i: (d, rowblk(d, b, i), 0))),
            pl.BlockSpec((1, CH, KEY), rev(lambda d, b, i: (d, rowblk(d, b, i), 0))),
            pl.BlockSpec((1, CH, D), rev(lambda d, b, i: (d, rowblk(d, b, i), 0))),
            pl.BlockSpec((CH, KEY), rev(lambda d, b, i: (rowblk(d, b, i), d))),
        ],
        out_shape=[jax.ShapeDtypeStruct((2, ntot, KEY), F32), jax.ShapeDtypeStruct((2, ntot, KEY), F32),
                   jax.ShapeDtypeStruct((2, ntot, D), F32), jax.ShapeDtypeStruct((ntot, 2 * KEY), F32)],
        scratch_shapes=[pltpu.VMEM((HEADS, DV, DK), F32)],
        compiler_params=_params(("parallel", "parallel", "arbitrary")),
    )(p_all, p_all, p_all, la_all, do, ss)


def _gla_post_fwd(o2, p_all, head_gain, n):
    tt = _tile(n, 256)

    def body(o_ref, g_ref, hg_ref, y_ref):
        o = o_ref[0] + o_ref[1]
        gv = g_ref[...]
        hg = hg_ref[...]
        for h in range(HEADS):
            oh = o[:, h * DV:(h + 1) * DV]
            r = lax.rsqrt(jnp.mean(oh * oh, axis=-1, keepdims=True) + EPS)
            y_ref[:, h * DV:(h + 1) * DV] = ((oh * r) * hg * _silu(gv[:, h * DV:(h + 1) * DV])).astype(BF16)

    return pl.pallas_call(
        body, name="gla_post_fwd", grid=(n // tt,),
        in_specs=[pl.BlockSpec((2, tt, D), lambda i: (0, i, 0)), pl.BlockSpec((tt, D), lambda i: (i, 2)),
                  pl.BlockSpec((1, DV), lambda i: (0, 0))],
        out_specs=pl.BlockSpec((tt, D), lambda i: (i, 0)),
        out_shape=jax.ShapeDtypeStruct((n, D), BF16),
        compiler_params=_params(("parallel",)),
    )(o2, p_all, head_gain)


def _gla_post_bwd(o2, p_all, head_gain, dyb, n):
    tt = _tile(n, 256)

    def body(o_ref, g_ref, hg_ref, dy_ref, do_ref, dg_ref, dhg_ref):
        i = pl.program_id(0)

        @pl.when(i == 0)
        def _():
            dhg_ref[...] = jnp.zeros_like(dhg_ref)

        o = o_ref[0] + o_ref[1]
        gv = g_ref[...]
        hg = hg_ref[...]
        dy = dy_ref[...]
        acc = jnp.zeros((1, DV), F32)
        for h in range(HEADS):
            sl = slice(h * DV, (h + 1) * DV)
            oh = o[:, sl]
            r = lax.rsqrt(jnp.mean(oh * oh, axis=-1, keepdims=True) + EPS)
            on = oh * r
            gh = gv[:, sl]
            dyh = dy[:, sl]
            dg_ref[:, sl] = dyh * (on * hg) * _dsilu(gh)
            dog = dyh * _silu(gh)
            acc = acc + _rowsum(dog * on)
            don = dog * hg
            do_ref[:, sl] = r * (don - on * jnp.mean(don * on, axis=-1, keepdims=True))
        dhg_ref[...] += acc

    return pl.pallas_call(
        body, name="gla_post_bwd", grid=(n // tt,),
        in_specs=[pl.BlockSpec((2, tt, D), lambda i: (0, i, 0)), pl.BlockSpec((tt, D), lambda i: (i, 2)),
                  pl.BlockSpec((1, DV), lambda i: (0, 0)), pl.BlockSpec((tt, D), lambda i: (i, 0))],
        out_specs=[pl.BlockSpec((tt, D), lambda i: (i, 0)), pl.BlockSpec((tt, D), lambda i: (i, 0)),
                   pl.BlockSpec((1, DV), lambda i: (0, 0))],
        out_shape=[jax.ShapeDtypeStruct((n, D), F32), jax.ShapeDtypeStruct((n, D), F32),
                   jax.ShapeDtypeStruct((1, DV), F32)],
        compiler_params=_params(("arbitrary",)),
    )(o2, p_all, head_gain, dyb)


def _gla_assemble(p_all, w2, b2, dq, dk, dv, dla, dgate, n):
    ntot = p_all.shape[0]
    tt = _tile(n, 128)
    nmain = n // tt
    assert ntot % tt == 0

    def body(a_ref, w_ref, b_ref, dq_ref, dk_ref, dv_ref, dla_ref, dg_ref, dp_ref, dw_ref, db_ref):
        i = pl.program_id(0)

        @pl.when(i == 0)
        def _():
            dw_ref[...] = jnp.zeros_like(dw_ref)
            db_ref[...] = jnp.zeros_like(db_ref)

        a = a_ref[...]
        w = w_ref[...]
        z = _dot(a, w) + b_ref[...]
        dz = dla_ref[...] * (1.0 / (1.0 + jnp.exp(z))) * (1.0 / TAU)
        dw_ref[...] += _dot(a, dz, _TN)
        db_ref[...] += _rowsum(dz)
        dp_ref[:, 0:KEY] = ((dq_ref[0] + dq_ref[1]) * 1.0).astype(BF16)
        dp_ref[:, KEY:2 * KEY] = (dk_ref[0] + dk_ref[1]).astype(BF16)
        dp_ref[:, 2 * KEY:2 * KEY + D] = (dv_ref[0] + dv_ref[1]).astype(BF16)
        dp_ref[:, 2 * KEY + D:2 * KEY + 2 * D] = (dg_ref[...] * jnp.where(i < nmain, 1.0, 0.0)).astype(BF16)
        dp_ref[:, 2 * KEY + 2 * D:GLA_IN_PAD] = _dot(dz, w, _NT).astype(BF16)

    return pl.pallas_call(
        body, name="gla_assemble", grid=(ntot // tt,),
        in_specs=[pl.BlockSpec((tt, 128), lambda i: (i, (2 * KEY + 2 * D) // 128)),
                  pl.BlockSpec((128, 2 * KEY), lambda i: (0, 0)), pl.BlockSpec((1, 2 * KEY), lambda i: (0, 0)),
                  pl.BlockSpec((2, tt, KEY), lambda i: (0, i, 0)), pl.BlockSpec((2, tt, KEY), lambda i: (0, i, 0)),
                  pl.BlockSpec((2, tt, D), lambda i: (0, i, 0)), pl.BlockSpec((tt, 2 * KEY), lambda i: (i, 0)),
                  pl.BlockSpec((tt, D), lambda i: (jnp.minimum(i, nmain - 1), 0))],
        out_specs=[pl.BlockSpec((tt, GLA_IN_PAD), lambda i: (i, 0)), pl.BlockSpec((128, 2 * KEY), lambda i: (0, 0)),
                   pl.BlockSpec((1, 2 * KEY), lambda i: (0, 0))],
        out_shape=[jax.ShapeDtypeStruct((ntot, GLA_IN_PAD), BF16), jax.ShapeDtypeStruct((128, 2 * KEY), F32),
                   jax.ShapeDtypeStruct((1, 2 * KEY), F32)],
        compiler_params=_params(("arbitrary",)),
    )(p_all, w2, b2, dq, dk, dv, dla, dgate)


ADA_ROWS = 24
ADA_SH = N_MOD * D // N_CHIPS


def _ada_fwd(cvec, ada_w, ada_b_sh):
    def body(c_ref, w_ref, b_ref, o_ref):
        o_ref[0] = _dot(_silu(c_ref[...]), w_ref[0]) + b_ref[0]

    return pl.pallas_call(
        body, name="ada_fwd", grid=(2,),
        in_specs=[pl.BlockSpec((ADA_ROWS, D), lambda l: (0, 0)), pl.BlockSpec((1, D, ADA_SH), lambda l: (l, 0, 0)),
                  pl.BlockSpec((1, 1, ADA_SH), lambda l: (l, 0, 0))],
        out_specs=pl.BlockSpec((1, ADA_ROWS, ADA_SH), lambda l: (l, 0, 0)),
        out_shape=jax.ShapeDtypeStruct((2, ADA_ROWS, ADA_SH), F32),
        compiler_params=_params(("parallel",)),
    )(cvec, ada_w, ada_b_sh)


def _ada_bwd(cvec, ada_w, dmod_sh):
    def body(c_ref, w_ref, dm_ref, gw_ref, dc_ref):
        dm = dm_ref[0]
        gw_ref[0] = _dot(_silu(c_ref[...]), dm, _TN)
        dc_ref[0] = _dot(dm, w_ref[0], _NT)

    return pl.pallas_call(
        body, name="ada_bwd", grid=(2,),
        in_specs=[pl.BlockSpec((ADA_ROWS, D), lambda l: (0, 0)), pl.BlockSpec((1, D, ADA_SH), lambda l: (l, 0, 0)),
                  pl.BlockSpec((1, ADA_ROWS, ADA_SH), lambda l: (l, 0, 0))],
        out_specs=[pl.BlockSpec((1, D, ADA_SH), lambda l: (l, 0, 0)), pl.BlockSpec((1, ADA_ROWS, D), lambda l: (l, 0, 0))],
        out_shape=[jax.ShapeDtypeStruct((2, D, ADA_SH), F32), jax.ShapeDtypeStruct((2, ADA_ROWS, D), F32)],
        compiler_params=_params(("parallel",)),
    )(cvec, ada_w, dmod_sh)


def _sum_slots(x, name):
    s, r, _ = x.shape

    def body(x_ref, o_ref):
        acc = x_ref[0]
        for k in range(1, s):
            acc = acc + x_ref[k]
        o_ref[...] = acc

    return pl.pallas_call(
        body, name=name, out_shape=jax.ShapeDtypeStruct((r, 128), F32),
        in_specs=[pl.BlockSpec(memory_space=pltpu.VMEM)], out_specs=pl.BlockSpec(memory_space=pltpu.VMEM),
    )(x)


def _cctx_grad(dscc_parts, c_ctx):
    def body(p_ref, c_ref, o_ref):
        acc = p_ref[0]
        for k in range(1, N_CHIPS):
            acc = acc + p_ref[k]
        o_ref[...] = acc * _dsilu(c_ref[...])

    return pl.pallas_call(
        body, name="cctx_grad", out_shape=jax.ShapeDtypeStruct((8, 128), F32),
        in_specs=[pl.BlockSpec(memory_space=pltpu.VMEM)] * 2, out_specs=pl.BlockSpec(memory_space=pltpu.VMEM),
    )(dscc_parts, c_ctx)


def _adamw(w, g, m, v, name):
    r, cdim = w.shape
    tr = _tile(r, 256)
    c1 = 1.0 - ADAM_B1 ** ADAM_STEP
    c2 = 1.0 - ADAM_B2 ** ADAM_STEP

    def body(w_ref, g_ref, m_ref, v_ref, d_ref, mo_ref, vo_ref):
        gv = g_ref[...]
        mn = ADAM_B1 * m_ref[...] + (1.0 - ADAM_B1) * gv
        vn = ADAM_B2 * v_ref[...] + (1.0 - ADAM_B2) * (gv * gv)
        mo_ref[...] = mn
        vo_ref[...] = vn
        d_ref[...] = -ADAM_LR * ((mn / c1) / (jnp.sqrt(vn / c2) + ADAM_EPS) + ADAM_WD * w_ref[...])

    spec = pl.BlockSpec((tr, cdim), lambda i: (i, 0))
    sds = jax.ShapeDtypeStruct((r, cdim), F32)
    return pl.pallas_call(
        body, name=name, grid=(r // tr,), in_specs=[spec] * 4, out_specs=[spec] * 3, out_shape=[sds] * 3,
        compiler_params=_params(("parallel",)),
    )(w, g, m, v)


def _place():
    x, y, c = lax.axis_index("x"), lax.axis_index("y"), lax.axis_index("c")
    return x, y, c


def _allgather_small(blk, name):
    m_per, n = blk.shape

    def body(x_ref, out_ref, send_sems, recv_sems, local_sem):
        x, y, c = _place()
        me, sibling = (x, y, c), (x, y, 1 - c)
        chips = [(1 - x, y), (x, 1 - y), (1 - x, 1 - y)]

        def rows(px, py, pc):
            return out_ref.at[pl.ds((4 * px + 2 * py + pc) * m_per, m_per), :]

        def copy(k, block, to, src=None):
            return pltpu.make_async_remote_copy(
                src_ref=rows(*block) if src is None else src, dst_ref=rows(*block),
                send_sem=send_sems.at[k], recv_sem=recv_sems.at[k], device_id=to, device_id_type=MESH)

        mine = pltpu.make_async_copy(x_ref, rows(*me), local_sem)
        mine.start()
        first = [copy(0, me, sibling, src=x_ref)]
        first += [copy(1 + j, me, (*chip, c), src=x_ref) for j, chip in enumerate(chips)]
        for cp in first:
            cp.start()
        passed = [copy(4 + j, (*chip, c), sibling) for j, chip in enumerate(chips)]
        for j, chip in enumerate(chips):
            copy(1 + j, (*chip, c), me).wait_recv()
            passed[j].start()
        copy(0, sibling, me).wait_recv()
        for j, chip in enumerate(chips):
            copy(4 + j, (*chip, 1 - c), me).wait_recv()
        for cp in first + passed:
            cp.wait_send()
        mine.wait()

    return pl.pallas_call(
        body, name=name,
        out_shape=jax.ShapeDtypeStruct((N_DEV * m_per, n), blk.dtype),
        in_specs=[pl.BlockSpec(memory_space=pltpu.VMEM)],
        out_specs=pl.BlockSpec(memory_space=pltpu.VMEM),
        scratch_shapes=[pltpu.SemaphoreType.DMA((7,)), pltpu.SemaphoreType.DMA((7,)), pltpu.SemaphoreType.DMA],
    )(blk)


def _other_chips(x, y):
    return [(1 - x, y), (x, 1 - y), (1 - x, 1 - y)]


_HBM_SPEC = pl.BlockSpec(memory_space=pltpu.HBM)
_SEM_SPEC = pl.BlockSpec(memory_space=pltpu.SEMAPHORE)
_SPLIT_PARAMS = pltpu.CompilerParams(has_side_effects=pltpu.SideEffectType.DATAFLOW_SIDE_EFFECTING)


def _in_hbm(a):
    return pltpu.with_memory_space_constraint(a, pltpu.HBM)


def _ag_copies(own_ref, land_ref, send_sems, recv_sems):
    x, y, c = _place()
    chip = 2 * x + y
    hr = own_ref.shape[0] // 2

    def half(ch):
        return land_ref.at[ch, pl.ds(c * hr, hr), :]

    def copy(k, src, dst, to):
        return pltpu.make_async_remote_copy(src_ref=src, dst_ref=dst, send_sem=send_sems.at[k],
                                            recv_sem=recv_sems.at[k], device_id=to, device_id_type=MESH)

    sends, expects = [], []
    for j, (ox, oy) in enumerate(_other_chips(x, y)):
        sends.append(copy(j, own_ref.at[pl.ds(c * hr, hr), :], half(chip), (ox, oy, c)))
        expects.append(copy(j, half(2 * ox + oy), half(2 * ox + oy), (ox, oy, c)))
    own_slot = copy(3, own_ref, land_ref.at[chip], (x, y, 1 - c))
    return sends + [own_slot], expects + [own_slot]


def _sc_copies(p_ref, land_ref, send_sems, recv_sems):
    x, y, c = _place()
    chip = 2 * x + y
    sends, expects = [], []
    for j, (ox, oy) in enumerate(_other_chips(x, y)):
        och = 2 * ox + oy
        mk = lambda dst_slot: pltpu.make_async_remote_copy(
            src_ref=p_ref.at[och], dst_ref=land_ref.at[dst_slot], send_sem=send_sems.at[j],
            recv_sem=recv_sems.at[j], device_id=(ox, oy, c), device_id_type=MESH)
        sends.append(mk(chip))
        expects.append(mk(och))
    return sends, expects


def _split_start(src, land_shape, copies, n_copies, name):
    def body(src_ref, land_ref, send_sems, recv_sems, src_thru, land_thru, token):
        for cp in copies(src_ref, land_ref, send_sems, recv_sems)[0]:
            cp.start()
        token[...] = jnp.zeros_like(token)

    land = lax.empty(land_shape, src.dtype)
    return pl.pallas_call(
        body, name=name,
        out_shape=(pltpu.SemaphoreType.DMA((n_copies,)), pltpu.SemaphoreType.DMA((n_copies,)),
                   pltpu.HBM(src.shape, src.dtype), pltpu.HBM(land_shape, src.dtype),
                   jax.ShapeDtypeStruct((8, 128), F32)),
        in_specs=(_HBM_SPEC, _HBM_SPEC),
        out_specs=(_SEM_SPEC, _SEM_SPEC, _HBM_SPEC, _HBM_SPEC, pl.BlockSpec(memory_space=pltpu.VMEM)),
        input_output_aliases={0: 2, 1: 3}, compiler_params=_SPLIT_PARAMS,
    )(_in_hbm(src), _in_hbm(land))


def _split_wait(started, after, copies, name):
    send_sems, recv_sems, src_thru, land_thru, _ = started

    def body(src_ref, land_ref, send_sems, recv_sems, after_ref, src_dead, got_ref):
        sends, expects = copies(src_ref, land_ref, send_sems, recv_sems)
        for cp in sends:
            cp.wait_send()
        for cp in expects:
            cp.wait_recv()

    return pl.pallas_call(
        body, name=name,
        out_shape=(pltpu.HBM(src_thru.shape, src_thru.dtype), pltpu.HBM(land_thru.shape, land_thru.dtype)),
        in_specs=(_HBM_SPEC, _HBM_SPEC, _SEM_SPEC, _SEM_SPEC, pl.BlockSpec(memory_space=pl.ANY)),
        out_specs=(_HBM_SPEC, _HBM_SPEC), input_output_aliases={0: 0, 1: 1}, compiler_params=_SPLIT_PARAMS,
    )(src_thru, land_thru, send_sems, recv_sems, after)[1]


def _ag_pass_on(land, name):
    hr = land.shape[1] // 2

    def body(in_ref, out_ref, send_sems, recv_sems):
        x, y, c = _place()

        def copy(j, ox, oy, cc):
            ref = out_ref.at[2 * ox + oy, pl.ds(cc * hr, hr), :]
            return pltpu.make_async_remote_copy(src_ref=ref, dst_ref=ref, send_sem=send_sems.at[j],
                                                recv_sem=recv_sems.at[j], device_id=(x, y, 1 - c),
                                                device_id_type=MESH)

        others = _other_chips(x, y)
        for j, (ox, oy) in enumerate(others):
            copy(j, ox, oy, c).start()
        for j, (ox, oy) in enumerate(others):
            copy(j, ox, oy, 1 - c).wait_recv()
        for j, (ox, oy) in enumerate(others):
            copy(j, ox, oy, c).wait_send()

    any_spec = pl.BlockSpec(memory_space=pl.ANY)
    return pl.pallas_call(
        body, name=name, out_shape=jax.ShapeDtypeStruct(land.shape, land.dtype),
        in_specs=[any_spec], out_specs=any_spec, input_output_aliases={0: 0},
        scratch_shapes=[pltpu.SemaphoreType.DMA((3,)), pltpu.SemaphoreType.DMA((3,))],
    )(land)


def _rs_pair_exchange(g, name):
    r = g.shape[1]
    hr = r // 2

    def body(g_ref, got_ref, send_sem, recv_sem):
        x, y, c = _place()
        cp = pltpu.make_async_remote_copy(
            src_ref=g_ref.at[:, pl.ds((1 - c) * hr, hr), :], dst_ref=got_ref, send_sem=send_sem, recv_sem=recv_sem,
            device_id=(x, y, 1 - c), device_id_type=MESH)
        cp.start()
        cp.wait()

    any_spec = pl.BlockSpec(memory_space=pl.ANY)
    return pl.pallas_call(
        body, name=name,
        out_shape=jax.ShapeDtypeStruct((N_CHIPS, hr, D), F32),
        in_specs=[any_spec], out_specs=any_spec,
        scratch_shapes=[pltpu.SemaphoreType.DMA, pltpu.SemaphoreType.DMA],
    )(g)


def _rs_chip_sum(place, g, got, name):
    r = g.shape[1]
    hr = r // 2
    tr = _tile(hr, 640, 16)
    nt = hr // tr

    def body(pl_ref, g_ref, got_ref, p16_ref, p32_ref):
        s = pl.program_id(1)
        p = g_ref[0] + got_ref[0]
        p16_ref[0] = p.astype(BF16)

        @pl.when(s == pl_ref[1])
        def _():
            p32_ref[...] = p

    return pl.pallas_call(
        body, name=name,
        grid_spec=pltpu.PrefetchScalarGridSpec(
            num_scalar_prefetch=1, grid=(nt, N_CHIPS),
            in_specs=[pl.BlockSpec((1, tr, D), lambda i, s, pr: (s, pr[0] * nt + i, 0)),
                      pl.BlockSpec((1, tr, D), lambda i, s, pr: (s, i, 0))],
            out_specs=[pl.BlockSpec((1, tr, D), lambda i, s, pr: (s, i, 0)),
                       pl.BlockSpec((tr, D), lambda i, s, pr: (i, 0))]),
        out_shape=[jax.ShapeDtypeStruct((N_CHIPS, hr, D), BF16), jax.ShapeDtypeStruct((hr, D), F32)],
        compiler_params=_params(("parallel", "arbitrary")),
    )(place, g, got)


def _rs_final_sum(place, parts, p32, name):
    hr = parts.shape[1]
    tr = _tile(hr, 640, 16)
    nt = hr // tr

    def body(pl_ref, a_ref, b_ref, c_ref, p32_ref, o_ref):
        o_ref[...] = ((p32_ref[...] + a_ref[0].astype(F32)) + b_ref[0].astype(F32)) + c_ref[0].astype(F32)

    def other(j):
        return pl.BlockSpec((1, tr, D), lambda i, pr: (j + jnp.where(pr[1] <= j, 1, 0), i, 0))

    return pl.pallas_call(
        body, name=name,
        grid_spec=pltpu.PrefetchScalarGridSpec(
            num_scalar_prefetch=1, grid=(nt,),
            in_specs=[other(0), other(1), other(2), pl.BlockSpec((tr, D), lambda i, pr: (i, 0))],
            out_specs=pl.BlockSpec((tr, D), lambda i, pr: (pr[0] * nt + i, 0))),
        out_shape=jax.ShapeDtypeStruct((2 * hr, D), F32),
        compiler_params=_params(("parallel",)),
    )(place, parts, parts, parts, p32)


def _rs_pair_gather(both, name):
    hr = both.shape[0] // 2

    def body(in_ref, out_ref, send_sem, recv_sem):
        x, y, c = _place()
        mine = out_ref.at[pl.ds(c * hr, hr), :]
        cp = pltpu.make_async_remote_copy(
            src_ref=mine, dst_ref=mine, send_sem=send_sem, recv_sem=recv_sem,
            device_id=(x, y, 1 - c), device_id_type=MESH)
        cp.start()
        theirs = out_ref.at[pl.ds((1 - c) * hr, hr), :]
        pltpu.make_async_remote_copy(
            src_ref=theirs, dst_ref=theirs, send_sem=send_sem, recv_sem=recv_sem,
            device_id=(x, y, 1 - c), device_id_type=MESH).wait_recv()
        cp.wait_send()

    any_spec = pl.BlockSpec(memory_space=pl.ANY)
    return pl.pallas_call(
        body, name=name,
        out_shape=jax.ShapeDtypeStruct(both.shape, F32),
        in_specs=[any_spec], out_specs=any_spec, input_output_aliases={0: 0},
        scratch_shapes=[pltpu.SemaphoreType.DMA, pltpu.SemaphoreType.DMA],
    )(both)


def _local_step(x, ctx, tgt, mods, mc, ag_gin, ag_main, place, small):
    nb, t, _ = x.shape
    tc = ctx.shape[1]
    n = nb * t
    nc = nb * tc
    xf = x.reshape(n, D)
    cf = ctx.reshape(nc, D)
    tf = tgt.reshape(n, D)
    vec = lambda a: a.reshape(1, -1)
    m = [[mods[l, :, k, :].reshape(nb, 1, D) for k in range(N_MOD)] for l in range(2)]
    mc_b = [jnp.broadcast_to(mc[k].reshape(1, 1, D), (nb, 1, D)) for k in range(2)]

    cw = [small["ffn_conv_w"][l] for l in range(2)]
    cb = [small["ffn_conv_b"][l].reshape(1, -1) for l in range(2)]
    w2 = jnp.zeros((128, 2 * KEY), F32)
    w2 = w2.at[0:RANK, 0:KEY].set(small["gla_w_a2"][0]).at[RANK:2 * RANK, KEY:].set(small["gla_w_a2"][1])
    b2 = small["gla_b_a"].reshape(1, 2 * KEY)
    hg = small["gla_head_norm"].reshape(1, DV)

    hn0 = _mod_fwd(xf, vec(small["norm_mix"][0]), m[0][0], m[0][1], t, "mod0_main")
    hnc = _mod_fwd(cf, vec(small["norm_mix"][0]), mc_b[0], mc_b[1], tc, "mod0_ctx")
    hn_all = jnp.concatenate([hn0, hnc], axis=0)
    gin = _ag_pass_on(_split_wait(ag_gin, hn_all, _ag_copies, "ag_gin_wait"), "ag_gin_pass_on")
    w_gin = jnp.pad(gin[:, :_GIN_ROWS, :].reshape(GLA_IN, D), ((0, GLA_IN_PAD - GLA_IN), (0, 0)))
    p_all = _mm(hn_all, w_gin, "nt", F32, "gla_in_proj", 768, 3200)
    la_all = _gla_decay_fwd(p_all, w2, b2)
    o2, ss = _gla_scan_fwd(p_all, la_all, nb, t, tc)
    wg = _ag_pass_on(_split_wait(ag_main, o2, _ag_copies, "ag_main_wait"), "ag_main_pass_on")
    offs = _offsets(_MAIN, _MAIN_ROWS)
    rows = _MAIN_ROWS

    def w_nt(a, k, name, tm=1024):
        return _mm_nt_w(a, wg, offs[k], rows[k], name, tm)

    def w_nn(a3, k, name, tm, tn):
        return _mm_nn_w(a3, wg, offs[k], rows[k], name, tm, tn)

    yb0 = _gla_post_fwd(o2, p_all, hg, n)
    y0 = w_nn(yb0[None], "gla_out", "gla_out_proj", 1024, 1024)
    h1, hn1 = _mod_fwd(xf, vec(small["norm_ffn"][0]), m[0][3], m[0][4], t, "mod0_ffn", y=y0, gate=m[0][2])
    u0 = w_nt(hn1, "up_t0", "ffn0_up")
    z0 = _ffn_mid_fwd(u0, cw[0], cb[0], nb, t, "ffn0_mid_fwd")
    f0 = w_nn(z0[None], "down0", "ffn0_down", 1024, 1024)
    h2, hn2 = _mod_fwd(h1, vec(small["norm_mix"][1]), m[1][0], m[1][1], t, "mod1_mix", y=f0, gate=m[0][5])
    p1 = w_nt(hn2, "sc_in_t", "sc_in_proj")
    yb1 = _sc_mid_fwd(p1, small["sc_conv_w"], nb, t)
    y1 = w_nn(yb1[None], "sc_out", "sc_out_proj", 1024, 1024)
    h3, hn3 = _mod_fwd(h2, vec(small["norm_ffn"][1]), m[1][3], m[1][4], t, "mod1_ffn", y=y1, gate=m[1][2])
    u1 = w_nt(hn3, "up_t1", "ffn1_up")
    z1 = _ffn_mid_fwd(u1, cw[1], cb[1], nb, t, "ffn1_mid_fwd")
    f1 = w_nn(z1[None], "down1", "ffn1_down", 1024, 1024)
    loss, dh4, df1, dm15, dfinal = _final(h3, f1, m[1][5], vec(small["final_norm"]), tf, t)

    gs = {}
    dmods = [[None] * N_MOD for _ in range(2)]
    dmods[1][5] = dm15

    def w_dw(a3, b, g_prev, k, name, tm):
        return _mm_dw(a3, b, g_prev, offs[k], rows[k], name, tm)

    def ffn_bwd(l, df, u, z, hn, g_prev):
        dz = w_nt(df, f"down{l}", f"ffn{l}_down_dx")
        g_acc = w_dw(z[None], df, g_prev, f"down{l}", f"ffn{l}_down_dw", 640)
        du, dcw, dcb = _ffn_mid_bwd(u, cw[l], cb[l], dz, nb, t, f"ffn{l}_mid_bwd")
        dhn = w_nn(du, f"up_t{l}", f"ffn{l}_up_dx", 512, 512)
        g_acc = w_dw(du, hn, g_acc, f"up_t{l}", f"ffn{l}_up_dw", 640)
        return dhn, g_acc, jnp.moveaxis(dcw, 0, 1).reshape(3, 2 * HID), dcb.reshape(2 * HID)

    dhn3, g_acc, dcw1, dcb1 = ffn_bwd(1, df1, u1, z1, hn3, None)
    r = _mod_bwd(h3, dhn3, vec(small["norm_ffn"][1]), m[1][4], t, "mod1_ffn_bwd", dh_out=dh4, y_prev=y1,
                 gate_prev=m[1][2])
    dh3, dmods[1][4], dmods[1][3], dnf1, dy1, dmods[1][2] = (r["dh"], r["dscale"], r["dshift"], r["dgain"],
                                                             r["dy_prev"], r["dgate_prev"])
    dyb1 = w_nt(dy1, "sc_out", "sc_out_dx")
    g_acc = w_dw(yb1[None], dy1, g_acc, "sc_out", "sc_out_dw", 256)
    dp1, dscw = _sc_mid_bwd(p1, small["sc_conv_w"], dyb1, nb, t)
    dhn2 = w_nn(dp1, "sc_in_t", "sc_in_dx", 1024, 512)
    g_acc = w_dw(dp1, hn2, g_acc, "sc_in_t", "sc_in_dw", 256)
    r = _mod_bwd(h2, dhn2, vec(small["norm_mix"][1]), m[1][1], t, "mod1_mix_bwd", dh_out=dh3, y_prev=f0,
                 gate_prev=m[0][5])
    dh2, dmods[1][1], dmods[1][0], dnm1, df0, dmods[0][5] = (r["dh"], r["dscale"], r["dshift"], r["dgain"],
                                                             r["dy_prev"], r["dgate_prev"])
    dhn1, g_acc, dcw0, dcb0 = ffn_bwd(0, df0, u0, z0, hn1, g_acc)
    r = _mod_bwd(h1, dhn1, vec(small["norm_ffn"][0]), m[0][4], t, "mod0_ffn_bwd", dh_out=dh2, y_prev=y0,
                 gate_prev=m[0][2])
    dh1, dmods[0][4], dmods[0][3], dnf0, dy0, dmods[0][2] = (r["dh"], r["dscale"], r["dshift"], r["dgain"],
                                                             r["dy_prev"], r["dgate_prev"])
    dyb0 = w_nt(dy0, "gla_out", "gla_out_dx")
    g_packed = w_dw(yb0[None], dy0, g_acc, "gla_out", "gla_out_dw", 256)
    from_sibling = _rs_pair_exchange(g_packed, "rs_main_pair_exchange")
    p16, p32 = _rs_chip_sum(place, g_packed, from_sibling, "rs_main_chip_sum")
    sc_main = _split_start(p16, p16.shape, _sc_copies, 3, "rs_main_scatter_start")
    do, dgate, dhg = _gla_post_bwd(o2, p_all, hg + sc_main[4][0:1, 0:1], dyb0, n)
    dq, dk, dv, dla = _gla_scan_bwd(p_all, la_all, do, ss, nb, t, tc)
    dp, dw2, db2 = _gla_assemble(p_all, w2, b2, dq, dk, dv, dla, dgate, n)
    dhn_all = _mm(dp, w_gin, "nn", F32, "gla_in_dx", 768, 512)
    landed = _split_wait(sc_main, dhn_all, _sc_copies, "rs_main_scatter_wait")
    g_main = _rs_pair_gather(_rs_final_sum(place, landed, p32, "rs_main_final_sum"), "rs_main_pair_gather")
    g_gin = _mm(dp, hn_all, "tn", F32, "gla_in_dw", 640, 1024)[:GLA_IN]
    g_gin = jnp.pad(g_gin.reshape(N_CHIPS, _GIN_ROWS, D), ((0, 0), (0, _GIN_PAD - _GIN_ROWS), (0, 0)))
    from_sibling = _rs_pair_exchange(g_gin, "rs_gin_pair_exchange")
    p16, p32_gin = _rs_chip_sum(place, g_gin, from_sibling, "rs_gin_chip_sum")
    sc_gin = _split_start(p16, p16.shape, _sc_copies, 3, "rs_gin_scatter_start")
    r = _mod_bwd(xf, dhn_all, vec(small["norm_mix"][0]) + sc_gin[4][0:1, 0:1], m[0][1], t, "mod0_main_bwd",
                 dh_out=dh1)
    grad_x, dmods[0][1], dmods[0][0], dnm0 = r["dh"], r["dscale"], r["dshift"], r["dgain"]
    rc = _mod_bwd(cf, dhn_all, vec(small["norm_mix"][0]), mc_b[1], tc, "mod0_ctx_bwd", dhn_row0=n, need_dh=False)
    dmc = jnp.stack([jnp.sum(rc["dshift"], axis=0).reshape(D), jnp.sum(rc["dscale"], axis=0).reshape(D)])
    dnm0 = dnm0 + rc["dgain"]

    gs["norm_mix"] = jnp.concatenate([dnm0, dnm1], axis=0)
    gs["norm_ffn"] = jnp.concatenate([dnf0, dnf1], axis=0)
    gs["final_norm"] = dfinal.reshape(D)
    gs["gla_w_a2"] = jnp.stack([dw2[0:RANK, 0:KEY], dw2[RANK:2 * RANK, KEY:]])
    gs["gla_b_a"] = db2.reshape(2, KEY)
    gs["gla_head_norm"] = dhg.reshape(DV)
    gs["sc_conv_w"] = dscw
    gs["ffn_conv_w"] = jnp.stack([dcw0, dcw1])
    gs["ffn_conv_b"] = jnp.stack([dcb0, dcb1])
    dmods_arr = jnp.stack([jnp.stack([dmods[l][k].reshape(nb, D) for k in range(N_MOD)], axis=1) for l in range(2)])
    return loss, grad_x.reshape(nb, t, D), g_main, sc_gin, p32_gin, gs, dmods_arr, dmc


def _pack(arrs):
    parts, meta, off = [], [], 0
    for a in arrs:
        r = a.size // 128
        rp = -(-r // 8) * 8
        a2 = a.reshape(r, 128).astype(F32)
        if rp != r:
            a2 = jnp.pad(a2, ((0, rp - r), (0, 0)))
        parts.append(a2)
        meta.append((off, r, a.shape))
        off += rp
    return jnp.concatenate(parts, axis=0), meta


def _unpack(buf, meta, lead=()):
    return [buf[..., off:off + r, :].reshape(*lead, *shape) for off, r, shape in meta]


_MAIN = ("up_t0", "up_t1", "down0", "down1", "sc_in_t", "gla_out", "sc_out")
_MAIN_ROWS = {"sc_in_t": 3 * D // N_CHIPS, "up_t0": 2 * HID // N_CHIPS, "up_t1": 2 * HID // N_CHIPS,
              "gla_out": D // N_CHIPS, "sc_out": D // N_CHIPS, "down0": HID // N_CHIPS, "down1": HID // N_CHIPS}
_MAIN_TOTAL = sum(_MAIN_ROWS.values())
_GIN_ROWS = GLA_IN // N_CHIPS
_GIN_PAD = -(-_GIN_ROWS // 32) * 32


def _offsets(names, rows):
    off, out = 0, {}
    for k in names:
        out[k] = off
        off += rows[k]
    return out


def kernel(x, c, ctx, c_ctx, ada_w, ada_b, norm_mix, norm_ffn, gla_w_in, gla_w_a2, gla_b_a, gla_head_norm, gla_w_out, sc_w_in, sc_conv_w, sc_w_out, ffn_w_up, ffn_conv_w, ffn_conv_b, ffn_w_down, final_norm, loss_target, m_c_ctx, m_ada_w, m_ada_b, m_norm_mix, m_norm_ffn, m_gla_w_in, m_gla_w_a2, m_gla_b_a, m_gla_head_norm, m_gla_w_out, m_sc_w_in, m_sc_conv_w, m_sc_w_out, m_ffn_w_up, m_ffn_conv_w, m_ffn_conv_b, m_ffn_w_down, m_final_norm, v_c_ctx, v_ada_w, v_ada_b, v_norm_mix, v_norm_ffn, v_gla_w_in, v_gla_w_a2, v_gla_b_a, v_gla_head_norm, v_gla_w_out, v_sc_w_in, v_sc_conv_w, v_sc_w_out, v_ffn_w_up, v_ffn_conv_w, v_ffn_conv_b, v_ffn_w_down, v_final_norm):
    ix, iy, ic = _place()
    chip = 2 * ix + iy
    dev = 2 * chip + ic
    place = jnp.stack([ic, chip]).astype(jnp.int32)
    nb = x.shape[0]
    offs = _offsets(_MAIN, _MAIN_ROWS)

    own = {"sc_in_t": sc_w_in[0].T, "up_t0": ffn_w_up[0].T, "up_t1": ffn_w_up[1].T,
           "gla_out": gla_w_out[0], "sc_out": sc_w_out[0], "down0": ffn_w_down[0], "down1": ffn_w_down[1]}
    own_main = jnp.concatenate([own[k].astype(BF16) for k in _MAIN], axis=0)
    own_gin = jnp.pad(gla_w_in[0].T.astype(BF16), ((0, _GIN_PAD - _GIN_ROWS), (0, 0)))
    ag_gin = _split_start(own_gin, (N_CHIPS, _GIN_PAD, D), _ag_copies, 4, "ag_gin_start")
    ag_main = _split_start(own_main, (N_CHIPS, _MAIN_TOTAL, D), _ag_copies, 4, "ag_main_start")
    started = ag_gin[4][0:1, :] + ag_main[4][0:1, :]

    buf, meta = _pack([c, ffn_conv_w, sc_conv_w, gla_w_a2, gla_b_a])
    got = _allgather_small(buf + started, "gather_small_in").reshape(N_DEV, buf.shape[0], 128)
    c_all, fcw, scw, wa2, ba = _unpack(got, meta, (N_DEV,))
    c_all = c_all.reshape(N_DEV * nb, D)
    per_chip = lambda a: a[0::2]
    ffn_conv_w_full = jnp.moveaxis(per_chip(fcw), 0, 2).reshape(2, 3, 2 * HID)
    sc_conv_w_full = jnp.moveaxis(per_chip(scw)[:, 0], 0, 1).reshape(3, D)
    gla_w_a2_full = jnp.moveaxis(per_chip(wa2)[:, 0], 0, 2).reshape(2, RANK, KEY)
    gla_b_a_full = jnp.moveaxis(per_chip(ba)[:, 0], 0, 1).reshape(2, KEY)

    cvec = jnp.concatenate([c_all, c_ctx.reshape(1, D), jnp.zeros((ADA_ROWS - N_DEV * nb - 1, D), F32)], axis=0)
    ada_b_sh = lax.dynamic_slice_in_dim(ada_b, chip * ADA_SH, ADA_SH, axis=1).reshape(2, 1, ADA_SH)
    mod_sh = _ada_fwd(cvec, ada_w, ada_b_sh)
    got = _allgather_small(mod_sh.reshape(2 * ADA_ROWS, ADA_SH), "gather_mod")
    mod_full = jnp.moveaxis(per_chip(got.reshape(N_DEV, 2, ADA_ROWS, ADA_SH)), 0, 2).reshape(2, ADA_ROWS, N_MOD * D)
    mods = lax.dynamic_slice_in_dim(mod_full, dev * nb, nb, axis=1).reshape(2, nb, N_MOD, D)
    mc = mod_full[0, N_DEV * nb, :2 * D].reshape(2, D)

    small = {"norm_mix": norm_mix, "norm_ffn": norm_ffn, "final_norm": final_norm, "gla_w_a2": gla_w_a2_full,
             "gla_b_a": gla_b_a_full, "gla_head_norm": gla_head_norm[0], "sc_conv_w": sc_conv_w_full,
             "ffn_conv_w": ffn_conv_w_full, "ffn_conv_b": ffn_conv_b}
    loss_p, grad_x, g_main, sc_gin, p32_gin, gs, dmods, dmc = _local_step(x, ctx, loss_target, mods, mc, ag_gin,
                                                                          ag_main, place, small)

    sum_names = ["norm_mix", "norm_ffn", "final_norm", "gla_w_a2", "gla_b_a", "gla_head_norm", "sc_conv_w",
                 "ffn_conv_w", "ffn_conv_b"]
    buf, meta = _pack([jnp.broadcast_to(loss_p, (8, 128))] + [gs[k] for k in sum_names] + [dmc, dmods])
    n_sum = meta[-1][0]
    got = _allgather_small(buf, "gather_small_grads").reshape(N_DEV, buf.shape[0], 128)
    summed = _sum_slots(got[:, :n_sum], "sum_small_grads")
    parts = _unpack(summed, meta[:-1])
    loss = parts[0][0, 0]
    g_small = dict(zip(sum_names, parts[1:-1]))
    dmc_tot = parts[-1]
    dmods_all = jnp.moveaxis(_unpack(got, meta[-1:], (N_DEV,))[0], 0, 1).reshape(2, N_DEV * nb, N_MOD * D)

    ctx_row = jnp.stack([jnp.concatenate([dmc_tot.reshape(2 * D), jnp.zeros(((N_MOD - 2) * D,), F32)]),
                         jnp.zeros((N_MOD * D,), F32)]).reshape(2, 1, N_MOD * D)
    dmod_ext = jnp.concatenate([dmods_all, ctx_row, jnp.zeros((2, ADA_ROWS - N_DEV * nb - 1, N_MOD * D), F32)], axis=1)
    g_ada_b = _sum_slots(jnp.moveaxis(dmod_ext, 1, 0).reshape(ADA_ROWS, 2 * N_MOD * D // 128, 128),
                         "sum_ada_b").reshape(2, N_MOD * D)
    dmod_sh = lax.dynamic_slice_in_dim(dmod_ext, chip * ADA_SH, ADA_SH, axis=2)
    g_ada_w, dcv = _ada_bwd(cvec, ada_w, dmod_sh)
    dscc_part = (dcv[0, N_DEV * nb] + dcv[1, N_DEV * nb]).reshape(8, 128)
    got = _allgather_small(dscc_part, "gather_dscc").reshape(N_DEV, 8, 128)
    g_c_ctx = _cctx_grad(per_chip(got), c_ctx.reshape(8, 128)).reshape(D)

    landed = _split_wait(sc_gin, g_c_ctx, _sc_copies, "rs_gin_scatter_wait")
    g_gin_shard = _rs_pair_gather(_rs_final_sum(place, landed, p32_gin, "rs_gin_final_sum"), "rs_gin_pair_gather")
    seg = {k: g_main[offs[k]:offs[k] + _MAIN_ROWS[k]] for k in _MAIN}
    seg["gla_in_t"] = g_gin_shard[:_GIN_ROWS]

    sl_chip = lambda a, axis, width: lax.dynamic_slice_in_dim(a, chip * width, width, axis=axis)
    grads = {
        "c_ctx": g_c_ctx, "ada_w": g_ada_w, "ada_b": g_ada_b, "norm_mix": g_small["norm_mix"],
        "norm_ffn": g_small["norm_ffn"],
        "gla_w_in": seg["gla_in_t"].T[None], "gla_w_a2": sl_chip(g_small["gla_w_a2"], 2, KEY // N_CHIPS)[None],
        "gla_b_a": sl_chip(g_small["gla_b_a"], 1, KEY // N_CHIPS)[None],
        "gla_head_norm": g_small["gla_head_norm"][None], "gla_w_out": seg["gla_out"][None],
        "sc_w_in": seg["sc_in_t"].T[None], "sc_conv_w": sl_chip(g_small["sc_conv_w"], 1, D // N_CHIPS)[None],
        "sc_w_out": seg["sc_out"][None], "ffn_w_up": jnp.stack([seg["up_t0"].T, seg["up_t1"].T]),
        "ffn_conv_w": sl_chip(g_small["ffn_conv_w"], 2, 2 * HID // N_CHIPS), "ffn_conv_b": g_small["ffn_conv_b"],
        "ffn_w_down": jnp.stack([seg["down0"], seg["down1"]]), "final_norm": g_small["final_norm"],
    }
    weights = {"c_ctx": c_ctx, "ada_w": ada_w, "ada_b": ada_b, "norm_mix": norm_mix, "norm_ffn": norm_ffn,
               "gla_w_in": gla_w_in, "gla_w_a2": gla_w_a2, "gla_b_a": gla_b_a, "gla_head_norm": gla_head_norm,
               "gla_w_out": gla_w_out, "sc_w_in": sc_w_in, "sc_conv_w": sc_conv_w, "sc_w_out": sc_w_out,
               "ffn_w_up": ffn_w_up, "ffn_conv_w": ffn_conv_w, "ffn_conv_b": ffn_conv_b, "ffn_w_down": ffn_w_down,
               "final_norm": final_norm}
    mom1 = {"c_ctx": m_c_ctx, "ada_w": m_ada_w, "ada_b": m_ada_b, "norm_mix": m_norm_mix, "norm_ffn": m_norm_ffn,
            "gla_w_in": m_gla_w_in, "gla_w_a2": m_gla_w_a2, "gla_b_a": m_gla_b_a, "gla_head_norm": m_gla_head_norm,
            "gla_w_out": m_gla_w_out, "sc_w_in": m_sc_w_in, "sc_conv_w": m_sc_conv_w, "sc_w_out": m_sc_w_out,
            "ffn_w_up": m_ffn_w_up, "ffn_conv_w": m_ffn_conv_w, "ffn_conv_b": m_ffn_conv_b,
            "ffn_w_down": m_ffn_w_down, "final_norm": m_final_norm}
    mom2 = {"c_ctx": v_c_ctx, "ada_w": v_ada_w, "ada_b": v_ada_b, "norm_mix": v_norm_mix, "norm_ffn": v_norm_ffn,
            "gla_w_in": v_gla_w_in, "gla_w_a2": v_gla_w_a2, "gla_b_a": v_gla_b_a, "gla_head_norm": v_gla_head_norm,
            "gla_w_out": v_gla_w_out, "sc_w_in": v_sc_w_in, "sc_conv_w": v_sc_conv_w, "sc_w_out": v_sc_w_out,
            "ffn_w_up": v_ffn_w_up, "ffn_conv_w": v_ffn_conv_w, "ffn_conv_b": v_ffn_conv_b,
            "ffn_w_down": v_ffn_w_down, "final_norm": v_final_norm}
    names = list(weights)
    grads = {k: grads[k].reshape(weights[k].shape) for k in names}

    big_names = ["ada_w", "gla_w_in", "gla_w_out", "sc_w_in", "sc_w_out", "ffn_w_up", "ffn_w_down"]
    small_names = [k for k in names if k not in big_names]
    delta, new_m, new_v = {}, {}, {}
    for k in big_names:
        shp = weights[k].shape
        as2d = lambda a: a.reshape(-1, shp[-1])
        d_, m_, v_ = _adamw(as2d(weights[k]), as2d(grads[k]), as2d(mom1[k]), as2d(mom2[k]), "adamw_" + k)
        delta[k], new_m[k], new_v[k] = d_.reshape(shp), m_.reshape(shp), v_.reshape(shp)
    packed = [_pack([src[k] for k in small_names]) for src in (weights, grads, mom1, mom2)]
    meta = packed[0][1]
    outs = _adamw(packed[0][0], packed[1][0], packed[2][0], packed[3][0], "adamw_small")
    for dst, o in zip((delta, new_m, new_v), outs):
        for k, a in zip(small_names, _unpack(o, meta)):
            dst[k] = a

    return (loss, grad_x, *[grads[k] for k in names], *[delta[k] for k in names], *[new_m[k] for k in names],
            *[new_v[k] for k in names])
```

```python
import functools

import jax
import jax.numpy as jnp
from jax import lax
from jax.experimental import pallas as pl
from jax.experimental.pallas import tpu as pltpu

F32 = jnp.float32
BF16 = jnp.bfloat16
MESH = pl.DeviceIdType.MESH

EPS = 1e-6
D = 1024
N_MOD = 6
HEADS = 4
DK = 128
DV = 256
KEY = HEADS * DK
RANK = 16
TAU = 16.0
CH = 64
GRID_W = 64
HID = 2560
GLA_IN = 2 * KEY + 2 * D + 2 * RANK
GLA_IN_PAD = 3200
Q_SCALE = DK ** -0.5
N_CHIPS = 4
N_DEV = 8

ADAM_LR = 0.001
ADAM_B1 = 0.9
ADAM_B2 = 0.999
ADAM_EPS = 1e-08
ADAM_WD = 0.01
ADAM_STEP = 10

VMEM_LIMIT = 56 * 1024 * 1024


def _params(sem):
    return pltpu.CompilerParams(dimension_semantics=sem, vmem_limit_bytes=VMEM_LIMIT)


def _tile(n, pref, mult=8):
    if n <= pref:
        return n
    for t in range(pref - pref % mult, 0, -mult):
        if n % t == 0:
            return t
    raise ValueError((n, pref, mult))


_NN = (((1,), (0,)), ((), ()))
_NT = (((1,), (1,)), ((), ()))
_TN = (((0,), (0,)), ((), ()))


def _dot(a, b, dims=_NN):
    return lax.dot_general(a.astype(BF16), b.astype(BF16), dims, preferred_element_type=F32)


def _sigmoid(x):
    return 1.0 / (1.0 + jnp.exp(-x))


def _rowsum(x):
    return jnp.sum(x, axis=0, keepdims=True)


def _mm(a, b, form, out_dtype, name, tm, tn):
    if form == "tn":
        K, M = a.shape
    else:
        M, K = a.shape
    N = b.shape[0] if form == "nt" else b.shape[1]
    tm = _tile(M, tm, 128)
    tn = _tile(N, tn, 128)
    dims = {"nn": _NN, "nt": _NT, "tn": _TN}[form]

    def body(a_ref, b_ref, o_ref):
        o_ref[...] = _dot(a_ref[...], b_ref[...], dims).astype(o_ref.dtype)

    if form == "tn":
        a_spec = pl.BlockSpec((K, tm), lambda i, j: (0, i))
    else:
        a_spec = pl.BlockSpec((tm, K), lambda i, j: (i, 0))
    if form == "nt":
        b_spec = pl.BlockSpec((tn, K), lambda i, j: (j, 0))
    else:
        b_spec = pl.BlockSpec((K, tn), lambda i, j: (0, j))
    return pl.pallas_call(
        body,
        name=name,
        grid=(M // tm, N // tn),
        in_specs=[a_spec, b_spec],
        out_specs=pl.BlockSpec((tm, tn), lambda i, j: (i, j)),
        out_shape=jax.ShapeDtypeStruct((M, N), out_dtype),
        compiler_params=_params(("parallel", "parallel")),
    )(a, b)


def _mm_nt_w(a, wg, off, rows, name, tm):
    m = a.shape[0]
    tm = _tile(m, tm, 128)

    def body(a_ref, w_ref, o_ref):
        o_ref[...] = _dot(a_ref[...], w_ref[0], _NT)

    return pl.pallas_call(
        body, name=name, grid=(m // tm, N_CHIPS),
        in_specs=[pl.BlockSpec((tm, D), lambda i, s: (i, 0)),
                  pl.BlockSpec((1, rows, D), lambda i, s: (s, off // rows, 0))],
        out_specs=pl.BlockSpec((tm, rows), lambda i, s: (i, s)),
        out_shape=jax.ShapeDtypeStruct((m, N_CHIPS * rows), F32),
        compiler_params=_params(("parallel", "parallel")),
    )(a, wg)


def _mm_nn_w(a3, wg, off, rows, name, tm, tn):
    parts, m, kp = a3.shape
    assert parts * kp == N_CHIPS * rows
    tm = _tile(m, tm, 128)
    cuts = sorted({s * rows for s in range(N_CHIPS + 1)} | {p * kp for p in range(parts + 1)})
    pieces = [(k0 // kp, k0 % kp, k0 // rows, k0 % rows, k1 - k0) for k0, k1 in zip(cuts[:-1], cuts[1:])]

    def body(a_ref, w_ref, o_ref):
        acc = None
        for p, a0, s, r0, width in pieces:
            term = _dot(a_ref[p, :, a0:a0 + width], w_ref[s, r0:r0 + width, :])
            acc = term if acc is None else acc + term
        o_ref[...] = acc

    return pl.pallas_call(
        body, name=name, grid=(m // tm, D // tn),
        in_specs=[pl.BlockSpec((parts, tm, kp), lambda i, j: (0, i, 0)),
                  pl.BlockSpec((N_CHIPS, rows, tn), lambda i, j: (0, off // rows, j))],
        out_specs=pl.BlockSpec((tm, tn), lambda i, j: (i, j)),
        out_shape=jax.ShapeDtypeStruct((m, D), F32),
        compiler_params=_params(("parallel", "parallel")),
    )(a3, wg)


def _mm_dw(a3, b, g_prev, off, rows, name, tm):
    parts, ntok, cdim = a3.shape
    assert parts * cdim == N_CHIPS * rows and cdim % tm == 0 and rows % tm == 0 and off % tm == 0

    def body(a_ref, b_ref, *rest):
        rest[-1][0] = _dot(a_ref[0], b_ref[...], _TN)

    in_specs = [pl.BlockSpec((1, ntok, tm), lambda i: ((i * tm) // cdim, 0, ((i * tm) % cdim) // tm)),
                pl.BlockSpec((ntok, D), lambda i: (0, 0))]
    args = [a3, b]
    aliases = {}
    if g_prev is not None:
        in_specs.append(pl.BlockSpec(memory_space=pl.ANY))
        args.append(g_prev)
        aliases = {2: 0}
    return pl.pallas_call(
        body, name=name, grid=(N_CHIPS * rows // tm,),
        in_specs=in_specs,
        out_specs=pl.BlockSpec((1, tm, D), lambda i: ((i * tm) // rows, (off + (i * tm) % rows) // tm, 0)),
        out_shape=jax.ShapeDtypeStruct((N_CHIPS, _MAIN_TOTAL, D), F32),
        input_output_aliases=aliases,
        compiler_params=_params(("parallel",)),
    )(*args)


def _mod_fwd(h, gain, shift, scale, tpb_rows, name, y=None, gate=None):
    n = h.shape[0]
    tt = _tile(tpb_rows, 256)
    tpb = tpb_rows // tt
    has_res = y is not None

    def body(*refs):
        if has_res:
            h_ref, y_ref, gate_ref, gain_ref, sh_ref, sc_ref, hout_ref, hn_ref = refs
            hv = h_ref[...] + gate_ref[0] * y_ref[...]
            hout_ref[...] = hv
        else:
            h_ref, gain_ref, sh_ref, sc_ref, hn_ref = refs
            hv = h_ref[...]
        r = lax.rsqrt(jnp.mean(hv * hv, axis=-1, keepdims=True) + EPS)
        hn = (hv * r) * gain_ref[...] * (1.0 + sc_ref[0]) + sh_ref[0]
        hn_ref[...] = hn.astype(BF16)

    row = pl.BlockSpec((tt, D), lambda i: (i, 0))
    per_b = pl.BlockSpec((1, 1, D), lambda i: (i // tpb, 0, 0))
    vec = pl.BlockSpec((1, D), lambda i: (0, 0))
    if has_res:
        in_specs = [row, row, per_b, vec, per_b, per_b]
        args = (h, y, gate, gain, shift, scale)
        out_specs = [row, row]
        out_shape = [jax.ShapeDtypeStruct((n, D), F32), jax.ShapeDtypeStruct((n, D), BF16)]
    else:
        in_specs = [row, vec, per_b, per_b]
        args = (h, gain, shift, scale)
        out_specs = row
        out_shape = jax.ShapeDtypeStruct((n, D), BF16)
    return pl.pallas_call(
        body, name=name, grid=(n // tt,), in_specs=in_specs, out_specs=out_specs, out_shape=out_shape,
        compiler_params=_params(("parallel",)),
    )(*args)


def _mod_bwd(h_in, dhn, gain, scale, tpb_rows, name, dhn_row0=0, dh_out=None, y_prev=None, gate_prev=None,
             need_dh=True):
    n = h_in.shape[0]
    nb = n // tpb_rows
    tt = _tile(tpb_rows, 256)
    tpb = tpb_rows // tt
    off = dhn_row0 // tt
    assert dhn_row0 % tt == 0
    has_out = dh_out is not None
    has_prev = y_prev is not None

    def body(*refs):
        it = iter(refs)
        h_ref, dhn_ref, gain_ref, sc_ref = next(it), next(it), next(it), next(it)
        dho_ref = next(it) if has_out else None
        yp_ref, gp_ref = (next(it), next(it)) if has_prev else (None, None)
        dh_ref = next(it) if need_dh else None
        dsc_ref, dsh_ref, dgain_ref = next(it), next(it), next(it)
        dyp_ref, dgp_ref = (next(it), next(it)) if has_prev else (None, None)
        i = pl.program_id(0)

        @pl.when(i == 0)
        def _():
            dgain_ref[...] = jnp.zeros_like(dgain_ref)

        @pl.when(i % tpb == 0)
        def _():
            dsc_ref[...] = jnp.zeros_like(dsc_ref)
            dsh_ref[...] = jnp.zeros_like(dsh_ref)
            if has_prev:
                dgp_ref[...] = jnp.zeros_like(dgp_ref)

        hv = h_ref[...]
        r = lax.rsqrt(jnp.mean(hv * hv, axis=-1, keepdims=True) + EPS)
        y = hv * r
        gain_v = gain_ref[...]
        g = dhn_ref[...].astype(F32)
        dsh_ref[0] += _rowsum(g)
        dsc_ref[0] += _rowsum(g * (y * gain_v))
        drn = g * (1.0 + sc_ref[0])
        dgain_ref[...] += _rowsum(drn * y)
        if need_dh:
            dy = drn * gain_v
            dh = r * (dy - y * jnp.mean(dy * y, axis=-1, keepdims=True))
            if has_out:
                dh = dh + dho_ref[...]
            dh_ref[...] = dh
            if has_prev:
                dyp_ref[...] = (dh * gp_ref[0]).astype(BF16)
                dgp_ref[0] += _rowsum(dh * yp_ref[...])

    row = pl.BlockSpec((tt, D), lambda i: (i, 0))
    row_off = pl.BlockSpec((tt, D), lambda i: (i + off, 0))
    per_b = pl.BlockSpec((1, 1, D), lambda i: (i // tpb, 0, 0))
    vec = pl.BlockSpec((1, D), lambda i: (0, 0))
    in_specs = [row, row_off, vec, per_b]
    args = [h_in, dhn, gain, scale]
    if has_out:
        in_specs.append(row)
        args.append(dh_out)
    if has_prev:
        in_specs += [row, per_b]
        args += [y_prev, gate_prev]
    out_specs, out_shape, names = [], [], []
    if need_dh:
        out_specs.append(row)
        out_shape.append(jax.ShapeDtypeStruct((n, D), F32))
        names.append("dh")
    for nm in ("dscale", "dshift"):
        out_specs.append(per_b)
        out_shape.append(jax.ShapeDtypeStruct((nb, 1, D), F32))
        names.append(nm)
    out_specs.append(vec)
    out_shape.append(jax.ShapeDtypeStruct((1, D), F32))
    names.append("dgain")
    if has_prev:
        out_specs += [row, per_b]
        out_shape += [jax.ShapeDtypeStruct((n, D), BF16), jax.ShapeDtypeStruct((nb, 1, D), F32)]
        names += ["dy_prev", "dgate_prev"]
    outs = pl.pallas_call(
        body, name=name, grid=(n // tt,), in_specs=in_specs, out_specs=out_specs, out_shape=out_shape,
        compiler_params=_params(("arbitrary",)),
    )(*args)
    return dict(zip(names, outs))


def _final(h, f, gate, gain, tgt, tpb_rows):
    n = h.shape[0]
    nb = n // tpb_rows
    tt = _tile(tpb_rows, 256)
    tpb = tpb_rows // tt

    def body(h_ref, f_ref, gate_ref, gain_ref, tgt_ref, loss_ref, dh_ref, df_ref, dgate_ref, dgain_ref):
        i = pl.program_id(0)

        @pl.when(i == 0)
        def _():
            loss_ref[...] = jnp.zeros_like(loss_ref)
            dgain_ref[...] = jnp.zeros_like(dgain_ref)

        @pl.when(i % tpb == 0)
        def _():
            dgate_ref[...] = jnp.zeros_like(dgate_ref)

        fv = f_ref[...]
        gate_v = gate_ref[0]
        hv = h_ref[...] + gate_v * fv
        r = lax.rsqrt(jnp.mean(hv * hv, axis=-1, keepdims=True) + EPS)
        y = hv * r
        gain_v = gain_ref[...]
        e = y * gain_v - tgt_ref[...]
        s = jnp.sum(_rowsum(e * e), axis=1, keepdims=True) * (0.5 / D)
        loss_ref[...] += jnp.broadcast_to(s, loss_ref.shape)
        dout = e * (1.0 / D)
        dgain_ref[...] += _rowsum(dout * y)
        dy = dout * gain_v
        dh = r * (dy - y * jnp.mean(dy * y, axis=-1, keepdims=True))
        dh_ref[...] = dh
        df_ref[...] = (dh * gate_v).astype(BF16)
        dgate_ref[0] += _rowsum(dh * fv)

    row = pl.BlockSpec((tt, D), lambda i: (i, 0))
    per_b = pl.BlockSpec((1, 1, D), lambda i: (i // tpb, 0, 0))
    vec = pl.BlockSpec((1, D), lambda i: (0, 0))
    return pl.pallas_call(
        body, name="final_loss", grid=(n // tt,),
        in_specs=[row, row, per_b, vec, row],
        out_specs=[pl.BlockSpec((1, 128), lambda i: (0, 0)), row, row, per_b, vec],
        out_shape=[jax.ShapeDtypeStruct((1, 128), F32), jax.ShapeDtypeStruct((n, D), F32),
                   jax.ShapeDtypeStruct((n, D), BF16), jax.ShapeDtypeStruct((nb, 1, D), F32),
                   jax.ShapeDtypeStruct((1, D), F32)],
        compiler_params=_params(("arbitrary",)),
    )(h, f, gate, gain, tgt)


def _shift_dn(x, s):
    return jnp.concatenate([jnp.zeros((s, x.shape[1]), x.dtype), x[: x.shape[0] - s]], axis=0)


def _shift_up(x, s):
    return jnp.concatenate([x[s:], jnp.zeros((s, x.shape[1]), x.dtype)], axis=0)


def _row_dn1(x):
    t = lax.broadcasted_iota(jnp.int32, x.shape, 0)
    return jnp.where(t % GRID_W == 0, 0.0, pltpu.roll(x, 1, 0))


def _row_up1(x):
    t = lax.broadcasted_iota(jnp.int32, x.shape, 0)
    return jnp.where(t % GRID_W == GRID_W - 1, 0.0, pltpu.roll(x, x.shape[0] - 1, 0))


def _silu(x):
    return x * _sigmoid(x)


def _dsilu(x):
    s = _sigmoid(x)
    return s * (1.0 + x * (1.0 - s))


def _conv_cols(x, w_ref):
    return _shift_dn(x, GRID_W) * w_ref[0:1, :] + x * w_ref[1:2, :] + _shift_up(x, GRID_W) * w_ref[2:3, :]


def _conv_cols_bwd(x, du, w_ref, dw_ref, db_ref):
    db_ref[...] += _rowsum(du)
    dw_ref[0:1, :] += _rowsum(du * _shift_dn(x, GRID_W))
    dw_ref[1:2, :] += _rowsum(du * x)
    dw_ref[2:3, :] += _rowsum(du * _shift_up(x, GRID_W))
    return _shift_up(du, GRID_W) * w_ref[0:1, :] + du * w_ref[1:2, :] + _shift_dn(du, GRID_W) * w_ref[2:3, :]


def _ffn_mid_fwd(u0, cw, cb, nb, t, name):
    nc = HID // 128

    def body(ua_ref, ug_ref, wa_ref, wg_ref, ba_ref, bg_ref, z_ref):
        a = _conv_cols(ua_ref[...], wa_ref) + ba_ref[...]
        gt = _conv_cols(ug_ref[...], wg_ref) + bg_ref[...]
        z_ref[...] = (a * _silu(gt)).astype(BF16)

    col = lambda rows, part: pl.BlockSpec((rows, 128), lambda j, b: (b if rows == t else 0, part * nc + j))
    return pl.pallas_call(
        body, name=name, grid=(nc, nb),
        in_specs=[col(t, 0), col(t, 1), col(3, 0), col(3, 1), col(1, 0), col(1, 1)],
        out_specs=pl.BlockSpec((t, 128), lambda j, b: (b, j)),
        out_shape=jax.ShapeDtypeStruct((nb * t, HID), BF16),
        compiler_params=_params(("parallel", "parallel")),
    )(u0, u0, cw, cw, cb, cb)


def _ffn_mid_bwd(u0, cw, cb, dz, nb, t, name):
    nc = HID // 128

    def body(ua_ref, ug_ref, wa_ref, wg_ref, ba_ref, bg_ref, dz_ref, du_ref, dw_ref, db_ref):
        b = pl.program_id(1)

        @pl.when(b == 0)
        def _():
            dw_ref[...] = jnp.zeros_like(dw_ref)
            db_ref[...] = jnp.zeros_like(db_ref)

        xa = ua_ref[...]
        xg = ug_ref[...]
        a = _conv_cols(xa, wa_ref) + ba_ref[...]
        gt = _conv_cols(xg, wg_ref) + bg_ref[...]
        dzv = dz_ref[...]
        du_ref[0] = _conv_cols_bwd(xa, dzv * _silu(gt), wa_ref, dw_ref.at[0], db_ref.at[0]).astype(BF16)
        du_ref[1] = _conv_cols_bwd(xg, dzv * a * _dsilu(gt), wg_ref, dw_ref.at[1], db_ref.at[1]).astype(BF16)

    col = lambda rows, part: pl.BlockSpec((rows, 128), lambda j, b: (b if rows == t else 0, part * nc + j))
    return pl.pallas_call(
        body, name=name, grid=(nc, nb),
        in_specs=[col(t, 0), col(t, 1), col(3, 0), col(3, 1), col(1, 0), col(1, 1),
                  pl.BlockSpec((t, 128), lambda j, b: (b, j))],
        out_specs=[pl.BlockSpec((2, t, 128), lambda j, b: (0, b, j)), pl.BlockSpec((2, 3, 128), lambda j, b: (0, 0, j)),
                   pl.BlockSpec((2, 1, 128), lambda j, b: (0, 0, j))],
        out_shape=[jax.ShapeDtypeStruct((2, nb * t, HID), BF16), jax.ShapeDtypeStruct((2, 3, HID), F32),
                   jax.ShapeDtypeStruct((2, 1, HID), F32)],
        compiler_params=_params(("parallel", "arbitrary")),
    )(u0, u0, cw, cw, cb, cb, dz)


def _sc_mid_fwd(p, cw, nb, t):
    nc = D // 128

    def body(bg_ref, cg_ref, v_ref, w_ref, y_ref):
        cv = cg_ref[...] * v_ref[...]
        cc = _row_dn1(cv) * w_ref[0:1, :] + cv * w_ref[1:2, :] + _row_up1(cv) * w_ref[2:3, :]
        y_ref[...] = (bg_ref[...] * cc).astype(BF16)

    part = lambda k: pl.BlockSpec((t, 128), lambda j, b: (b, k * nc + j))
    return pl.pallas_call(
        body, name="sc_mid_fwd", grid=(nc, nb),
        in_specs=[part(0), part(1), part(2), pl.BlockSpec((3, 128), lambda j, b: (0, j))],
        out_specs=pl.BlockSpec((t, 128), lambda j, b: (b, j)),
        out_shape=jax.ShapeDtypeStruct((nb * t, D), BF16),
        compiler_params=_params(("parallel", "parallel")),
    )(p, p, p, cw)


def _sc_mid_bwd(p, cw, dyb, nb, t):
    nc = D // 128

    def body(bg_ref, cg_ref, v_ref, w_ref, dy_ref, dp_ref, dw_ref):
        b = pl.program_id(1)

        @pl.when(b == 0)
        def _():
            dw_ref[...] = jnp.zeros_like(dw_ref)

        w0, w1, w2 = w_ref[0:1, :], w_ref[1:2, :], w_ref[2:3, :]
        cg, v = cg_ref[...], v_ref[...]
        cv = cg * v
        cvd = _row_dn1(cv)
        cvu = _row_up1(cv)
        cc = cvd * w0 + cv * w1 + cvu * w2
        dy = dy_ref[...]
        dcc = dy * bg_ref[...]
        dw_ref[0:1, :] += _rowsum(dcc * cvd)
        dw_ref[1:2, :] += _rowsum(dcc * cv)
        dw_ref[2:3, :] += _rowsum(dcc * cvu)
        dcv = _row_up1(dcc) * w0 + dcc * w1 + _row_dn1(dcc) * w2
        dp_ref[0] = (dy * cc).astype(BF16)
        dp_ref[1] = (dcv * v).astype(BF16)
        dp_ref[2] = (dcv * cg).astype(BF16)

    part = lambda k: pl.BlockSpec((t, 128), lambda j, b: (b, k * nc + j))
    return pl.pallas_call(
        body, name="sc_mid_bwd", grid=(nc, nb),
        in_specs=[part(0), part(1), part(2), pl.BlockSpec((3, 128), lambda j, b: (0, j)),
                  pl.BlockSpec((t, 128), lambda j, b: (b, j))],
        out_specs=[pl.BlockSpec((3, t, 128), lambda j, b: (0, b, j)), pl.BlockSpec((3, 128), lambda j, b: (0, j))],
        out_shape=[jax.ShapeDtypeStruct((3, nb * t, D), BF16), jax.ShapeDtypeStruct((3, D), F32)],
        compiler_params=_params(("parallel", "arbitrary")),
    )(p, p, p, cw, dyb)


def _gla_decay_fwd(p_all, w2, b2):
    n = p_all.shape[0]
    tt = _tile(n, 512)

    def body(a_ref, w_ref, b_ref, la_ref):
        z = _dot(a_ref[...], w_ref[...]) + b_ref[...]
        la_ref[...] = (jnp.minimum(z, 0.0) - jnp.log(1.0 + jnp.exp(-jnp.abs(z)))) * (1.0 / TAU)

    return pl.pallas_call(
        body, name="gla_decay_fwd", grid=(n // tt,),
        in_specs=[pl.BlockSpec((tt, 128), lambda i: (i, (2 * KEY + 2 * D) // 128)),
                  pl.BlockSpec((128, 2 * KEY), lambda i: (0, 0)), pl.BlockSpec((1, 2 * KEY), lambda i: (0, 0))],
        out_specs=pl.BlockSpec((tt, 2 * KEY), lambda i: (i, 0)),
        out_shape=jax.ShapeDtypeStruct((n, 2 * KEY), F32),
        compiler_params=_params(("parallel",)),
    )(p_all, w2, b2)


def _gla_blocks(nb, nm, ncx):
    def main_idx(d, i):
        return jnp.clip(jnp.where(d == 0, i - ncx, nm - 1 - (i - ncx)), 0, nm - 1)

    def rowblk(d, b, i):
        cidx = jnp.where(d == 0, i, ncx - 1 - i)
        return jnp.where(i < ncx, nb * nm + b * ncx + cidx, b * nm + main_idx(d, i))

    def mainblk(d, b, i):
        return b * nm + main_idx(d, i)

    return rowblk, mainblk


def _gla_mask(d):
    row = lax.broadcasted_iota(jnp.int32, (CH, CH), 0)
    col = lax.broadcasted_iota(jnp.int32, (CH, CH), 1)
    diff = jnp.where(d == 0, row - col, col - row)
    mask = diff >= 0
    return mask, jnp.where(mask, 1.0, 0.0).astype(BF16), jnp.where(diff <= 0, 1.0, 0.0).astype(BF16)


def _tri_sum(m01, x):
    w = x.shape[1]
    hi = x.astype(BF16)
    r1 = x - hi.astype(F32)
    mid = r1.astype(BF16)
    lo = (r1 - mid.astype(F32)).astype(BF16)
    s = lax.dot_general(m01, jnp.concatenate([hi, mid, lo], axis=1), _NN, preferred_element_type=F32)
    return s[:, :w] + s[:, w:2 * w] + s[:, 2 * w:]


def _gla_chunk(q, k, g, bc):
    bl = _rowsum(g)
    eq = jnp.exp(bc)
    ek = jnp.exp(-bc)
    ed = jnp.exp(bl - bc)
    return bl, eq, ek, ed, q * Q_SCALE * eq, k * ek, k * ed


def _gla_scan_fwd(p_all, la_all, nb, t, tc):
    nm, ncx = t // CH, tc // CH
    nst = nm + ncx
    rowblk, mainblk = _gla_blocks(nb, nm, ncx)

    def body(q_ref, k_ref, v_ref, la_ref, o_ref, ss_ref, st_ref):
        d = pl.program_id(0)
        i = pl.program_id(2)

        @pl.when(i == 0)
        def _():
            st_ref[...] = jnp.zeros_like(st_ref)

        mask, m01, _ = _gla_mask(d)
        q_all, k_all, v_all, g_all = q_ref[...], k_ref[...], v_ref[...], la_ref[...]
        states = [st_ref[h] for h in range(HEADS)]
        bc_all = _tri_sum(m01, g_all)
        outs, new_states = [], []
        for h in range(HEADS):
            ksl = slice(h * DK, (h + 1) * DK)
            v = v_all[:, h * DV:(h + 1) * DV]
            bl, _, _, _, qs, ks, kd = _gla_chunk(q_all[:, ksl], k_all[:, ksl], g_all[:, ksl], bc_all[:, ksl])
            att = jnp.where(mask, _dot(qs, ks, _NT), 0.0)
            outs.append(_dot(qs, states[h], _NT) + _dot(att, v))
            new_states.append(states[h] * jnp.exp(bl) + _dot(v, kd, _TN))
        o_ref[0] = jnp.concatenate(outs, axis=1)
        for h in range(HEADS):
            ss_ref[0, 0, 0, h] = states[h]
            st_ref[h] = new_states[h]

    return pl.pallas_call(
        body, name="gla_scan_fwd", grid=(2, nb, nst),
        in_specs=[
            pl.BlockSpec((CH, KEY), lambda d, b, i: (rowblk(d, b, i), 0)),
            pl.BlockSpec((CH, KEY), lambda d, b, i: (rowblk(d, b, i), 1)),
            pl.BlockSpec((CH, D), lambda d, b, i: (rowblk(d, b, i), 1)),
            pl.BlockSpec((CH, KEY), lambda d, b, i: (rowblk(d, b, i), d)),
        ],
        out_specs=[
            pl.BlockSpec((1, CH, D), lambda d, b, i: (d, mainblk(d, b, i), 0)),
            pl.BlockSpec((1, 1, 1, HEADS, DV, DK), lambda d, b, i: (d, b, i, 0, 0, 0)),
        ],
        out_shape=[jax.ShapeDtypeStruct((2, nb * t, D), F32),
                   jax.ShapeDtypeStruct((2, nb, nst, HEADS, DV, DK), F32)],
        scratch_shapes=[pltpu.VMEM((HEADS, DV, DK), F32)],
        compiler_params=_params(("parallel", "parallel", "arbitrary")),
    )(p_all, p_all, p_all, la_all)


def _gla_scan_bwd(p_all, la_all, do, ss, nb, t, tc):
    nm, ncx = t // CH, tc // CH
    nst = nm + ncx
    ntot = nb * (t + tc)
    rowblk, mainblk = _gla_blocks(nb, nm, ncx)

    def body(q_ref, k_ref, v_ref, la_ref, do_ref, ss_ref, dq_ref, dk_ref, dv_ref, dla_ref, dst_ref):
        d = pl.program_id(0)
        ip = pl.program_id(2)
        i = nst - 1 - ip

        @pl.when(ip == 0)
        def _():
            dst_ref[...] = jnp.zeros_like(dst_ref)

        mask, m01, m01_t = _gla_mask(d)
        live = jnp.where(i >= ncx, 1.0, 0.0)
        q_all, k_all, v_all, g_all = q_ref[...], k_ref[...], v_ref[...], la_ref[...]
        do_all = do_ref[...] * live
        states = [ss_ref[0, 0, 0, h] for h in range(HEADS)]
        dstates = [dst_ref[h] for h in range(HEADS)]
        bc_all = _tri_sum(m01, g_all)
        dqs_l, dks_l, dvs_l, dbs_l, dbls_l, new_dstates = [], [], [], [], [], []
        for h in range(HEADS):
            ksl = slice(h * DK, (h + 1) * DK)
            vsl = slice(h * DV, (h + 1) * DV)
            bl, eq, ek, ed, qs, ks, kd = _gla_chunk(q_all[:, ksl], k_all[:, ksl], g_all[:, ksl], bc_all[:, ksl])
            st, dst, v, dov = states[h], dstates[h], v_all[:, vsl], do_all[:, vsl]
            att = jnp.where(mask, _dot(qs, ks, _NT), 0.0)
            datt = jnp.where(mask, _dot(dov, v, _NT), 0.0)
            dqs = _dot(dov, st) + _dot(datt, ks)
            dks = _dot(datt, qs, _TN)
            dvs_l.append(_dot(att, dov, _TN) + _dot(kd, dst, _NT))
            dkd = _dot(v, dst)
            e = jnp.exp(bl)
            dbls_l.append(e * _rowsum(st * dst) + _rowsum(dkd * kd))
            new_dstates.append(_dot(dov, qs, _TN) + dst * e)
            dqs_l.append(dqs * eq * Q_SCALE)
            dks_l.append(dks * ek + dkd * ed)
            dbs_l.append(dqs * qs - dks * ks - dkd * kd)
        dq_ref[0] = jnp.concatenate(dqs_l, axis=1)
        dk_ref[0] = jnp.concatenate(dks_l, axis=1)
        dv_ref[0] = jnp.concatenate(dvs_l, axis=1)
        dla_ref[...] = _tri_sum(m01_t, jnp.concatenate(dbs_l, axis=1)) + jnp.concatenate(dbls_l, axis=1)
        for h in range(HEADS):
            dst_ref[h] = new_dstates[h]

    rev = lambda f: (lambda d, b, ip: f(d, b, nst - 1 - ip))
    return pl.pallas_call(
        body, name="gla_scan_bwd", grid=(2, nb, nst),
        in_specs=[
            pl.BlockSpec((CH, KEY), rev(lambda d, b, i: (rowblk(d, b, i), 0))),
            pl.BlockSpec((CH, KEY), rev(lambda d, b, i: (rowblk(d, b, i), 1))),
            pl.BlockSpec((CH, D), rev(lambda d, b, i: (rowblk(d, b, i), 1))),
            pl.BlockSpec((CH, KEY), rev(lambda d, b, i: (rowblk(d, b, i), d))),
            pl.BlockSpec((CH, D), rev(lambda d, b, i: (mainblk(d, b, i), 0))),
            pl.BlockSpec((1, 1, 1, HEADS, DV, DK), rev(lambda d, b, i: (d, b, i, 0, 0, 0))),
        ],
        out_specs=[
            pl.BlockSpec((1, CH, KEY), rev(lambda d, b, i: (d, rowblk(d, b, i), 0))),
            pl.BlockSpec((1, CH, KEY), rev(lambda d, b, i: (d, rowblk(d, b, i), 0))),
            pl.BlockSpec((1, CH, D), rev(lambda d, b, i: (d, rowblk(d, b, i), 0))),
            pl.BlockSpec((CH, KEY), rev(lambda d, b, i: (rowblk(d, b, i), d))),
        ],
        out_shape=[jax.ShapeDtypeStruct((2, ntot, KEY), F32), jax.ShapeDtypeStruct((2, ntot, KEY), F32),
                   jax.ShapeDtypeStruct((2, ntot, D), F32), jax.ShapeDtypeStruct((ntot, 2 * KEY), F32)],
        scratch_shapes=[pltpu.VMEM((HEADS, DV, DK), F32)],
        compiler_params=_params(("parallel", "parallel", "arbitrary")),
    )(p_all, p_all, p_all, la_all, do, ss)


def _gla_post_fwd(o2, p_all, head_gain, n):
    tt = _tile(n, 256)

    def body(o_ref, g_ref, hg_ref, y_ref):
        o = o_ref[0] + o_ref[1]
        gv = g_ref[...]
        hg = hg_ref[...]
        for h in range(HEADS):
            oh = o[:, h * DV:(h + 1) * DV]
            r = lax.rsqrt(jnp.mean(oh * oh, axis=-1, keepdims=True) + EPS)
            y_ref[:, h * DV:(h + 1) * DV] = ((oh * r) * hg * _silu(gv[:, h * DV:(h + 1) * DV])).astype(BF16)

    return pl.pallas_call(
        body, name="gla_post_fwd", grid=(n // tt,),
        in_specs=[pl.BlockSpec((2, tt, D), lambda i: (0, i, 0)), pl.BlockSpec((tt, D), lambda i: (i, 2)),
                  pl.BlockSpec((1, DV), lambda i: (0, 0))],
        out_specs=pl.BlockSpec((tt, D), lambda i: (i, 0)),
        out_shape=jax.ShapeDtypeStruct((n, D), BF16),
        compiler_params=_params(("parallel",)),
    )(o2, p_all, head_gain)


def _gla_post_bwd(o2, p_all, head_gain, dyb, n):
    tt = _tile(n, 256)

    def body(o_ref, g_ref, hg_ref, dy_ref, do_ref, dg_ref, dhg_ref):
        i = pl.program_id(0)

        @pl.when(i == 0)
        def _():
            dhg_ref[...] = jnp.zeros_like(dhg_ref)

        o = o_ref[0] + o_ref[1]
        gv = g_ref[...]
        hg = hg_ref[...]
        dy = dy_ref[...]
        acc = jnp.zeros((1, DV), F32)
        for h in range(HEADS):
            sl = slice(h * DV, (h + 1) * DV)
            oh = o[:, sl]
            r = lax.rsqrt(jnp.mean(oh * oh, axis=-1, keepdims=True) + EPS)
            on = oh * r
            gh = gv[:, sl]
            dyh = dy[:, sl]
            dg_ref[:, sl] = dyh * (on * hg) * _dsilu(gh)
            dog = dyh * _silu(gh)
            acc = acc + _rowsum(dog * on)
            don = dog * hg
            do_ref[:, sl] = r * (don - on * jnp.mean(don * on, axis=-1, keepdims=True))
        dhg_ref[...] += acc

    return pl.pallas_call(
        body, name="gla_post_bwd", grid=(n // tt,),
        in_specs=[pl.BlockSpec((2, tt, D), lambda i: (0, i, 0)), pl.BlockSpec((tt, D), lambda i: (i, 2)),
                  pl.BlockSpec((1, DV), lambda i: (0, 0)), pl.BlockSpec((tt, D), lambda i: (i, 0))],
        out_specs=[pl.BlockSpec((tt, D), lambda i: (i, 0)), pl.BlockSpec((tt, D), lambda i: (i, 0)),
                   pl.BlockSpec((1, DV), lambda i: (0, 0))],
        out_shape=[jax.ShapeDtypeStruct((n, D), F32), jax.ShapeDtypeStruct((n, D), F32),
                   jax.ShapeDtypeStruct((1, DV), F32)],
        compiler_params=_params(("arbitrary",)),
    )(o2, p_all, head_gain, dyb)


def _gla_assemble(p_all, w2, b2, dq, dk, dv, dla, dgate, n):
    ntot = p_all.shape[0]
    tt = _tile(n, 128)
    nmain = n // tt
    assert ntot % tt == 0

    def body(a_ref, w_ref, b_ref, dq_ref, dk_ref, dv_ref, dla_ref, dg_ref, dp_ref, dw_ref, db_ref):
        i = pl.program_id(0)

        @pl.when(i == 0)
        def _():
            dw_ref[...] = jnp.zeros_like(dw_ref)
            db_ref[...] = jnp.zeros_like(db_ref)

        a = a_ref[...]
        w = w_ref[...]
        z = _dot(a, w) + b_ref[...]
        dz = dla_ref[...] * (1.0 / (1.0 + jnp.exp(z))) * (1.0 / TAU)
        dw_ref[...] += _dot(a, dz, _TN)
        db_ref[...] += _rowsum(dz)
        dp_ref[:, 0:KEY] = ((dq_ref[0] + dq_ref[1]) * 1.0).astype(BF16)
        dp_ref[:, KEY:2 * KEY] = (dk_ref[0] + dk_ref[1]).astype(BF16)
        dp_ref[:, 2 * KEY:2 * KEY + D] = (dv_ref[0] + dv_ref[1]).astype(BF16)
        dp_ref[:, 2 * KEY + D:2 * KEY + 2 * D] = (dg_ref[...] * jnp.where(i < nmain, 1.0, 0.0)).astype(BF16)
        dp_ref[:, 2 * KEY + 2 * D:GLA_IN_PAD] = _dot(dz, w, _NT).astype(BF16)

    return pl.pallas_call(
        body, name="gla_assemble", grid=(ntot // tt,),
        in_specs=[pl.BlockSpec((tt, 128), lambda i: (i, (2 * KEY + 2 * D) // 128)),
                  pl.BlockSpec((128, 2 * KEY), lambda i: (0, 0)), pl.BlockSpec((1, 2 * KEY), lambda i: (0, 0)),
                  pl.BlockSpec((2, tt, KEY), lambda i: (0, i, 0)), pl.BlockSpec((2, tt, KEY), lambda i: (0, i, 0)),
                  pl.BlockSpec((2, tt, D), lambda i: (0, i, 0)), pl.BlockSpec((tt, 2 * KEY), lambda i: (i, 0)),
                  pl.BlockSpec((tt, D), lambda i: (jnp.minimum(i, nmain - 1), 0))],
        out_specs=[pl.BlockSpec((tt, GLA_IN_PAD), lambda i: (i, 0)), pl.BlockSpec((128, 2 * KEY), lambda i: (0, 0)),
                   pl.BlockSpec((1, 2 * KEY), lambda i: (0, 0))],
        out_shape=[jax.ShapeDtypeStruct((ntot, GLA_IN_PAD), BF16), jax.ShapeDtypeStruct((128, 2 * KEY), F32),
                   jax.ShapeDtypeStruct((1, 2 * KEY), F32)],
        compiler_params=_params(("arbitrary",)),
    )(p_all, w2, b2, dq, dk, dv, dla, dgate)


ADA_ROWS = 24
ADA_SH = N_MOD * D // N_CHIPS


def _ada_fwd(cvec, ada_w, ada_b_sh):
    def body(c_ref, w_ref, b_ref, o_ref):
        o_ref[0] = _dot(_silu(c_ref[...]), w_ref[0]) + b_ref[0]

    return pl.pallas_call(
        body, name="ada_fwd", grid=(2,),
        in_specs=[pl.BlockSpec((ADA_ROWS, D), lambda l: (0, 0)), pl.BlockSpec((1, D, ADA_SH), lambda l: (l, 0, 0)),
                  pl.BlockSpec((1, 1, ADA_SH), lambda l: (l, 0, 0))],
        out_specs=pl.BlockSpec((1, ADA_ROWS, ADA_SH), lambda l: (l, 0, 0)),
        out_shape=jax.ShapeDtypeStruct((2, ADA_ROWS, ADA_SH), F32),
        compiler_params=_params(("parallel",)),
    )(cvec, ada_w, ada_b_sh)


def _ada_bwd(cvec, ada_w, dmod_sh):
    def body(c_ref, w_ref, dm_ref, gw_ref, dc_ref):
        dm = dm_ref[0]
        gw_ref[0] = _dot(_silu(c_ref[...]), dm, _TN)
        dc_ref[0] = _dot(dm, w_ref[0], _NT)

    return pl.pallas_call(
        body, name="ada_bwd", grid=(2,),
        in_specs=[pl.BlockSpec((ADA_ROWS, D), lambda l: (0, 0)), pl.BlockSpec((1, D, ADA_SH), lambda l: (l, 0, 0)),
                  pl.BlockSpec((1, ADA_ROWS, ADA_SH), lambda l: (l, 0, 0))],
        out_specs=[pl.BlockSpec((1, D, ADA_SH), lambda l: (l, 0, 0)), pl.BlockSpec((1, ADA_ROWS, D), lambda l: (l, 0, 0))],
        out_shape=[jax.ShapeDtypeStruct((2, D, ADA_SH), F32), jax.ShapeDtypeStruct((2, ADA_ROWS, D), F32)],
        compiler_params=_params(("parallel",)),
    )(cvec, ada_w, dmod_sh)


def _sum_slots(x, name):
    s, r, _ = x.shape

    def body(x_ref, o_ref):
        acc = x_ref[0]
        for k in range(1, s):
            acc = acc + x_ref[k]
        o_ref[...] = acc

    return pl.pallas_call(
        body, name=name, out_shape=jax.ShapeDtypeStruct((r, 128), F32),
        in_specs=[pl.BlockSpec(memory_space=pltpu.VMEM)], out_specs=pl.BlockSpec(memory_space=pltpu.VMEM),
    )(x)


def _cctx_grad(dscc_parts, c_ctx):
    def body(p_ref, c_ref, o_ref):
        acc = p_ref[0]
        for k in range(1, N_CHIPS):
            acc = acc + p_ref[k]
        o_ref[...] = acc * _dsilu(c_ref[...])

    return pl.pallas_call(
        body, name="cctx_grad", out_shape=jax.ShapeDtypeStruct((8, 128), F32),
        in_specs=[pl.BlockSpec(memory_space=pltpu.VMEM)] * 2, out_specs=pl.BlockSpec(memory_space=pltpu.VMEM),
    )(dscc_parts, c_ctx)


def _adamw(w, g, m, v, name, after):
    r, cdim = w.shape
    tr = _tile(r, 256)
    c1 = 1.0 - ADAM_B1 ** ADAM_STEP
    c2 = 1.0 - ADAM_B2 ** ADAM_STEP

    def body(w_ref, g_ref, m_ref, v_ref, after_ref, d_ref, mo_ref, vo_ref):
        gv = g_ref[...]
        mn = ADAM_B1 * m_ref[...] + (1.0 - ADAM_B1) * gv
        vn = ADAM_B2 * v_ref[...] + (1.0 - ADAM_B2) * (gv * gv)
        mo_ref[...] = mn
        vo_ref[...] = vn
        d_ref[...] = -ADAM_LR * ((mn / c1) / (jnp.sqrt(vn / c2) + ADAM_EPS) + ADAM_WD * w_ref[...])

    spec = pl.BlockSpec((tr, cdim), lambda i: (i, 0))
    sds = jax.ShapeDtypeStruct((r, cdim), F32)
    return pl.pallas_call(
        body, name=name, grid=(r // tr,), in_specs=[spec] * 4 + [pl.BlockSpec(memory_space=pl.ANY)],
        out_specs=[spec] * 3, out_shape=[sds] * 3, compiler_params=_params(("parallel",)),
    )(w, g, m, v, after)


def _place():
    x, y, c = lax.axis_index("x"), lax.axis_index("y"), lax.axis_index("c")
    return x, y, c


def _allgather_small(blk, name):
    m_per, n = blk.shape

    def body(x_ref, out_ref, send_sems, recv_sems, local_sem):
        x, y, c = _place()
        me, sibling = (x, y, c), (x, y, 1 - c)
        chips = [(1 - x, y), (x, 1 - y), (1 - x, 1 - y)]

        def rows(px, py, pc):
            return out_ref.at[pl.ds((4 * px + 2 * py + pc) * m_per, m_per), :]

        def copy(k, block, to, src=None):
            return pltpu.make_async_remote_copy(
                src_ref=rows(*block) if src is None else src, dst_ref=rows(*block),
                send_sem=send_sems.at[k], recv_sem=recv_sems.at[k], device_id=to, device_id_type=MESH)

        mine = pltpu.make_async_copy(x_ref, rows(*me), local_sem)
        mine.start()
        first = [copy(0, me, sibling, src=x_ref)]
        first += [copy(1 + j, me, (*chip, c), src=x_ref) for j, chip in enumerate(chips)]
        for cp in first:
            cp.start()
        passed = [copy(4 + j, (*chip, c), sibling) for j, chip in enumerate(chips)]
        for j, chip in enumerate(chips):
            copy(1 + j, (*chip, c), me).wait_recv()
            passed[j].start()
        copy(0, sibling, me).wait_recv()
        for j, chip in enumerate(chips):
            copy(4 + j, (*chip, 1 - c), me).wait_recv()
        for cp in first + passed:
            cp.wait_send()
        mine.wait()

    return pl.pallas_call(
        body, name=name,
        out_shape=jax.ShapeDtypeStruct((N_DEV * m_per, n), blk.dtype),
        in_specs=[pl.BlockSpec(memory_space=pltpu.VMEM)],
        out_specs=pl.BlockSpec(memory_space=pltpu.VMEM),
        scratch_shapes=[pltpu.SemaphoreType.DMA((7,)), pltpu.SemaphoreType.DMA((7,)), pltpu.SemaphoreType.DMA],
    )(blk)


def _other_chips(x, y):
    return [(1 - x, y), (x, 1 - y), (1 - x, 1 - y)]


_HBM_SPEC = pl.BlockSpec(memory_space=pltpu.HBM)
_SEM_SPEC = pl.BlockSpec(memory_space=pltpu.SEMAPHORE)
_SPLIT_PARAMS = pltpu.CompilerParams(has_side_effects=pltpu.SideEffectType.DATAFLOW_SIDE_EFFECTING)


def _in_hbm(a):
    return pltpu.with_memory_space_constraint(a, pltpu.HBM)


def _ag_copies(own_ref, land_ref, send_sems, recv_sems):
    x, y, c = _place()
    chip = 2 * x + y
    hr = own_ref.shape[0] // 2

    def half(ch):
        return land_ref.at[ch, pl.ds(c * hr, hr), :]

    def copy(k, src, dst, to):
        return pltpu.make_async_remote_copy(src_ref=src, dst_ref=dst, send_sem=send_sems.at[k],
                                            recv_sem=recv_sems.at[k], device_id=to, device_id_type=MESH)

    sends, expects = [], []
    for j, (ox, oy) in enumerate(_other_chips(x, y)):
        sends.append(copy(j, own_ref.at[pl.ds(c * hr, hr), :], half(chip), (ox, oy, c)))
        expects.append(copy(j, half(2 * ox + oy), half(2 * ox + oy), (ox, oy, c)))
    own_slot = copy(3, own_ref, land_ref.at[chip], (x, y, 1 - c))
    return sends + [own_slot], expects + [own_slot]


def _sc_copies(p_ref, land_ref, send_sems, recv_sems):
    x, y, c = _place()
    chip = 2 * x + y
    sends, expects = [], []
    for j, (ox, oy) in enumerate(_other_chips(x, y)):
        och = 2 * ox + oy
        mk = lambda dst_slot: pltpu.make_async_remote_copy(
            src_ref=p_ref.at[och], dst_ref=land_ref.at[dst_slot], send_sem=send_sems.at[j],
            recv_sem=recv_sems.at[j], device_id=(ox, oy, c), device_id_type=MESH)
        sends.append(mk(chip))
        expects.append(mk(och))
    return sends, expects


def _split_start(src, land_shape, copies, n_copies, after, name):
    def body(src_ref, land_ref, after_ref, send_sems, recv_sems, src_thru, land_thru, token):
        for cp in copies(src_ref, land_ref, send_sems, recv_sems)[0]:
            cp.start()
        token[...] = jnp.zeros_like(token)

    land = lax.empty(land_shape, src.dtype)
    return pl.pallas_call(
        body, name=name,
        out_shape=(pltpu.SemaphoreType.DMA((n_copies,)), pltpu.SemaphoreType.DMA((n_copies,)),
                   pltpu.HBM(src.shape, src.dtype), pltpu.HBM(land_shape, src.dtype),
                   jax.ShapeDtypeStruct((8, 128), F32)),
        in_specs=(_HBM_SPEC, _HBM_SPEC, pl.BlockSpec(memory_space=pl.ANY)),
        out_specs=(_SEM_SPEC, _SEM_SPEC, _HBM_SPEC, _HBM_SPEC, pl.BlockSpec(memory_space=pltpu.VMEM)),
        input_output_aliases={0: 2, 1: 3}, compiler_params=_SPLIT_PARAMS,
    )(_in_hbm(src), _in_hbm(land), after)


def _split_wait(started, after, copies, name):
    send_sems, recv_sems, src_thru, land_thru, _ = started

    def body(src_ref, land_ref, send_sems, recv_sems, after_ref, src_dead, got_ref):
        sends, expects = copies(src_ref, land_ref, send_sems, recv_sems)
        for cp in sends:
            cp.wait_send()
        for cp in expects:
            cp.wait_recv()

    return pl.pallas_call(
        body, name=name,
        out_shape=(pltpu.HBM(src_thru.shape, src_thru.dtype), pltpu.HBM(land_thru.shape, land_thru.dtype)),
        in_specs=(_HBM_SPEC, _HBM_SPEC, _SEM_SPEC, _SEM_SPEC, pl.BlockSpec(memory_space=pl.ANY)),
        out_specs=(_HBM_SPEC, _HBM_SPEC), input_output_aliases={0: 0, 1: 1}, compiler_params=_SPLIT_PARAMS,
    )(src_thru, land_thru, send_sems, recv_sems, after)[1]


def _ag_pass_on(land, name):
    hr = land.shape[1] // 2

    def body(in_ref, out_ref, send_sems, recv_sems):
        x, y, c = _place()

        def copy(j, ox, oy, cc):
            ref = out_ref.at[2 * ox + oy, pl.ds(cc * hr, hr), :]
            return pltpu.make_async_remote_copy(src_ref=ref, dst_ref=ref, send_sem=send_sems.at[j],
                                                recv_sem=recv_sems.at[j], device_id=(x, y, 1 - c),
                                                device_id_type=MESH)

        others = _other_chips(x, y)
        for j, (ox, oy) in enumerate(others):
            copy(j, ox, oy, c).start()
        for j, (ox, oy) in enumerate(others):
            copy(j, ox, oy, 1 - c).wait_recv()
        for j, (ox, oy) in enumerate(others):
            copy(j, ox, oy, c).wait_send()

    any_spec = pl.BlockSpec(memory_space=pl.ANY)
    return pl.pallas_call(
        body, name=name, out_shape=jax.ShapeDtypeStruct(land.shape, land.dtype),
        in_specs=[any_spec], out_specs=any_spec, input_output_aliases={0: 0},
        scratch_shapes=[pltpu.SemaphoreType.DMA((3,)), pltpu.SemaphoreType.DMA((3,))],
    )(land)


def _rs_pair_exchange(g, name):
    r = g.shape[1]
    hr = r // 2

    def body(g_ref, got_ref, send_sem, recv_sem):
        x, y, c = _place()
        cp = pltpu.make_async_remote_copy(
            src_ref=g_ref.at[:, pl.ds((1 - c) * hr, hr), :], dst_ref=got_ref, send_sem=send_sem, recv_sem=recv_sem,
            device_id=(x, y, 1 - c), device_id_type=MESH)
        cp.start()
        cp.wait()

    any_spec = pl.BlockSpec(memory_space=pl.ANY)
    return pl.pallas_call(
        body, name=name,
        out_shape=jax.ShapeDtypeStruct((N_CHIPS, hr, D), F32),
        in_specs=[any_spec], out_specs=any_spec,
        scratch_shapes=[pltpu.SemaphoreType.DMA, pltpu.SemaphoreType.DMA],
    )(g)


def _rs_chip_sum(place, g, got, name):
    r = g.shape[1]
    hr = r // 2
    tr = _tile(hr, 640, 16)
    nt = hr // tr

    def body(pl_ref, g_ref, got_ref, p16_ref, p32_ref):
        s = pl.program_id(1)
        p = g_ref[0] + got_ref[0]
        p16_ref[0] = p.astype(BF16)

        @pl.when(s == pl_ref[1])
        def _():
            p32_ref[...] = p

    return pl.pallas_call(
        body, name=name,
        grid_spec=pltpu.PrefetchScalarGridSpec(
            num_scalar_prefetch=1, grid=(nt, N_CHIPS),
            in_specs=[pl.BlockSpec((1, tr, D), lambda i, s, pr: (s, pr[0] * nt + i, 0)),
                      pl.BlockSpec((1, tr, D), lambda i, s, pr: (s, i, 0))],
            out_specs=[pl.BlockSpec((1, tr, D), lambda i, s, pr: (s, i, 0)),
                       pl.BlockSpec((tr, D), lambda i, s, pr: (i, 0))]),
        out_shape=[jax.ShapeDtypeStruct((N_CHIPS, hr, D), BF16), jax.ShapeDtypeStruct((hr, D), F32)],
        compiler_params=_params(("parallel", "arbitrary")),
    )(place, g, got)


def _rs_final_sum(place, parts, p32, name):
    hr = parts.shape[1]
    tr = _tile(hr, 640, 16)
    nt = hr // tr

    def body(pl_ref, a_ref, b_ref, c_ref, p32_ref, o_ref):
        o_ref[...] = ((p32_ref[...] + a_ref[0].astype(F32)) + b_ref[0].astype(F32)) + c_ref[0].astype(F32)

    def other(j):
        return pl.BlockSpec((1, tr, D), lambda i, pr: (j + jnp.where(pr[1] <= j, 1, 0), i, 0))

    return pl.pallas_call(
        body, name=name,
        grid_spec=pltpu.PrefetchScalarGridSpec(
            num_scalar_prefetch=1, grid=(nt,),
            in_specs=[other(0), other(1), other(2), pl.BlockSpec((tr, D), lambda i, pr: (i, 0))],
            out_specs=pl.BlockSpec((tr, D), lambda i, pr: (pr[0] * nt + i, 0))),
        out_shape=jax.ShapeDtypeStruct((2 * hr, D), F32),
        compiler_params=_params(("parallel",)),
    )(place, parts, parts, parts, p32)


def _rs_pair_gather(both, name):
    hr = both.shape[0] // 2

    def body(in_ref, out_ref, send_sem, recv_sem):
        x, y, c = _place()
        mine = out_ref.at[pl.ds(c * hr, hr), :]
        cp = pltpu.make_async_remote_copy(
            src_ref=mine, dst_ref=mine, send_sem=send_sem, recv_sem=recv_sem,
            device_id=(x, y, 1 - c), device_id_type=MESH)
        cp.start()
        theirs = out_ref.at[pl.ds((1 - c) * hr, hr), :]
        pltpu.make_async_remote_copy(
            src_ref=theirs, dst_ref=theirs, send_sem=send_sem, recv_sem=recv_sem,
            device_id=(x, y, 1 - c), device_id_type=MESH).wait_recv()
        cp.wait_send()

    any_spec = pl.BlockSpec(memory_space=pl.ANY)
    return pl.pallas_call(
        body, name=name,
        out_shape=jax.ShapeDtypeStruct(both.shape, F32),
        in_specs=[any_spec], out_specs=any_spec, input_output_aliases={0: 0},
        scratch_shapes=[pltpu.SemaphoreType.DMA, pltpu.SemaphoreType.DMA],
    )(both)


def _local_step(x, ctx, tgt, mods, mc, ag_gin, ag_main, place, small):
    nb, t, _ = x.shape
    tc = ctx.shape[1]
    n = nb * t
    nc = nb * tc
    xf = x.reshape(n, D)
    cf = ctx.reshape(nc, D)
    tf = tgt.reshape(n, D)
    vec = lambda a: a.reshape(1, -1)
    m = [[mods[l, :, k, :].reshape(nb, 1, D) for k in range(N_MOD)] for l in range(2)]
    mc_b = [jnp.broadcast_to(mc[k].reshape(1, 1, D), (nb, 1, D)) for k in range(2)]

    cw = [small["ffn_conv_w"][l] for l in range(2)]
    cb = [small["ffn_conv_b"][l].reshape(1, -1) for l in range(2)]
    w2 = jnp.zeros((128, 2 * KEY), F32)
    w2 = w2.at[0:RANK, 0:KEY].set(small["gla_w_a2"][0]).at[RANK:2 * RANK, KEY:].set(small["gla_w_a2"][1])
    b2 = small["gla_b_a"].reshape(1, 2 * KEY)
    hg = small["gla_head_norm"].reshape(1, DV)

    hn0 = _mod_fwd(xf, vec(small["norm_mix"][0]), m[0][0], m[0][1], t, "mod0_main")
    hnc = _mod_fwd(cf, vec(small["norm_mix"][0]), mc_b[0], mc_b[1], tc, "mod0_ctx")
    hn_all = jnp.concatenate([hn0, hnc], axis=0)
    gin = _ag_pass_on(_split_wait(ag_gin, hn_all, _ag_copies, "ag_gin_wait"), "ag_gin_pass_on")
    w_gin = jnp.pad(gin[:, :_GIN_ROWS, :].reshape(GLA_IN, D), ((0, GLA_IN_PAD - GLA_IN), (0, 0)))
    p_all = _mm(hn_all, w_gin, "nt", F32, "gla_in_proj", 768, 3200)
    la_all = _gla_decay_fwd(p_all, w2, b2)
    o2, ss = _gla_scan_fwd(p_all, la_all, nb, t, tc)
    wg = _ag_pass_on(_split_wait(ag_main, o2, _ag_copies, "ag_main_wait"), "ag_main_pass_on")
    offs = _offsets(_MAIN, _MAIN_ROWS)
    rows = _MAIN_ROWS

    def w_nt(a, k, name, tm=1024):
        return _mm_nt_w(a, wg, offs[k], rows[k], name, tm)

    def w_nn(a3, k, name, tm, tn):
        return _mm_nn_w(a3, wg, offs[k], rows[k], name, tm, tn)

    yb0 = _gla_post_fwd(o2, p_all, hg, n)
    y0 = w_nn(yb0[None], "gla_out", "gla_out_proj", 1024, 1024)
    h1, hn1 = _mod_fwd(xf, vec(small["norm_ffn"][0]), m[0][3], m[0][4], t, "mod0_ffn", y=y0, gate=m[0][2])
    u0 = w_nt(hn1, "up_t0", "ffn0_up")
    z0 = _ffn_mid_fwd(u0, cw[0], cb[0], nb, t, "ffn0_mid_fwd")
    f0 = w_nn(z0[None], "down0", "ffn0_down", 1024, 1024)
    h2, hn2 = _mod_fwd(h1, vec(small["norm_mix"][1]), m[1][0], m[1][1], t, "mod1_mix", y=f0, gate=m[0][5])
    p1 = w_nt(hn2, "sc_in_t", "sc_in_proj")
    yb1 = _sc_mid_fwd(p1, small["sc_conv_w"], nb, t)
    y1 = w_nn(yb1[None], "sc_out", "sc_out_proj", 1024, 1024)
    h3, hn3 = _mod_fwd(h2, vec(small["norm_ffn"][1]), m[1][3], m[1][4], t, "mod1_ffn", y=y1, gate=m[1][2])
    u1 = w_nt(hn3, "up_t1", "ffn1_up")
    z1 = _ffn_mid_fwd(u1, cw[1], cb[1], nb, t, "ffn1_mid_fwd")
    f1 = w_nn(z1[None], "down1", "ffn1_down", 1024, 1024)
    loss, dh4, df1, dm15, dfinal = _final(h3, f1, m[1][5], vec(small["final_norm"]), tf, t)

    gs = {}
    dmods = [[None] * N_MOD for _ in range(2)]
    dmods[1][5] = dm15

    def w_dw(a3, b, g_prev, k, name, tm):
        return _mm_dw(a3, b, g_prev, offs[k], rows[k], name, tm)

    def ffn_bwd(l, df, u, z, hn, g_prev):
        dz = w_nt(df, f"down{l}", f"ffn{l}_down_dx")
        g_acc = w_dw(z[None], df, g_prev, f"down{l}", f"ffn{l}_down_dw", 640)
        du, dcw, dcb = _ffn_mid_bwd(u, cw[l], cb[l], dz, nb, t, f"ffn{l}_mid_bwd")
        dhn = w_nn(du, f"up_t{l}", f"ffn{l}_up_dx", 512, 512)
        g_acc = w_dw(du, hn, g_acc, f"up_t{l}", f"ffn{l}_up_dw", 640)
        return dhn, g_acc, jnp.moveaxis(dcw, 0, 1).reshape(3, 2 * HID), dcb.reshape(2 * HID)

    dhn3, g_acc, dcw1, dcb1 = ffn_bwd(1, df1, u1, z1, hn3, None)
    r = _mod_bwd(h3, dhn3, vec(small["norm_ffn"][1]), m[1][4], t, "mod1_ffn_bwd", dh_out=dh4, y_prev=y1,
                 gate_prev=m[1][2])
    dh3, dmods[1][4], dmods[1][3], dnf1, dy1, dmods[1][2] = (r["dh"], r["dscale"], r["dshift"], r["dgain"],
                                                             r["dy_prev"], r["dgate_prev"])
    dyb1 = w_nt(dy1, "sc_out", "sc_out_dx")
    g_acc = w_dw(yb1[None], dy1, g_acc, "sc_out", "sc_out_dw", 256)
    dp1, dscw = _sc_mid_bwd(p1, small["sc_conv_w"], dyb1, nb, t)
    dhn2 = w_nn(dp1, "sc_in_t", "sc_in_dx", 1024, 512)
    g_acc = w_dw(dp1, hn2, g_acc, "sc_in_t", "sc_in_dw", 256)
    r = _mod_bwd(h2, dhn2, vec(small["norm_mix"][1]), m[1][1], t, "mod1_mix_bwd", dh_out=dh3, y_prev=f0,
                 gate_prev=m[0][5])
    dh2, dmods[1][1], dmods[1][0], dnm1, df0, dmods[0][5] = (r["dh"], r["dscale"], r["dshift"], r["dgain"],
                                                             r["dy_prev"], r["dgate_prev"])
    dhn1, g_acc, dcw0, dcb0 = ffn_bwd(0, df0, u0, z0, hn1, g_acc)
    r = _mod_bwd(h1, dhn1, vec(small["norm_ffn"][0]), m[0][4], t, "mod0_ffn_bwd", dh_out=dh2, y_prev=y0,
                 gate_prev=m[0][2])
    dh1, dmods[0][4], dmods[0][3], dnf0, dy0, dmods[0][2] = (r["dh"], r["dscale"], r["dshift"], r["dgain"],
                                                             r["dy_prev"], r["dgate_prev"])
    dyb0 = w_nt(dy0, "gla_out", "gla_out_dx")
    g_packed = w_dw(yb0[None], dy0, g_acc, "gla_out", "gla_out_dw", 256)
    from_sibling = _rs_pair_exchange(g_packed, "rs_main_pair_exchange")
    p16, p32 = _rs_chip_sum(place, g_packed, from_sibling, "rs_main_chip_sum")
    sc_main = _split_start(p16, p16.shape, _sc_copies, 3, p32, "rs_main_scatter_start")
    do, dgate, dhg = _gla_post_bwd(o2, p_all, hg + sc_main[4][0:1, 0:1], dyb0, n)
    dq, dk, dv, dla = _gla_scan_bwd(p_all, la_all, do, ss, nb, t, tc)
    dp, dw2, db2 = _gla_assemble(p_all, w2, b2, dq, dk, dv, dla, dgate, n)
    dhn_all = _mm(dp, w_gin, "nn", F32, "gla_in_dx", 768, 512)
    landed = _split_wait(sc_main, dhn_all, _sc_copies, "rs_main_scatter_wait")
    g_main = _rs_pair_gather(_rs_final_sum(place, landed, p32, "rs_main_final_sum"), "rs_main_pair_gather")
    g_gin = _mm(dp, hn_all, "tn", F32, "gla_in_dw", 640, 1024)[:GLA_IN]
    g_gin = jnp.pad(g_gin.reshape(N_CHIPS, _GIN_ROWS, D), ((0, 0), (0, _GIN_PAD - _GIN_ROWS), (0, 0)))
    from_sibling = _rs_pair_exchange(g_gin, "rs_gin_pair_exchange")
    p16_gin, p32_gin = _rs_chip_sum(place, g_gin, from_sibling, "rs_gin_chip_sum")
    r = _mod_bwd(xf, dhn_all, vec(small["norm_mix"][0]), m[0][1], t, "mod0_main_bwd", dh_out=dh1)
    grad_x, dmods[0][1], dmods[0][0], dnm0 = r["dh"], r["dscale"], r["dshift"], r["dgain"]
    rc = _mod_bwd(cf, dhn_all, vec(small["norm_mix"][0]), mc_b[1], tc, "mod0_ctx_bwd", dhn_row0=n, need_dh=False)
    dmc = jnp.stack([jnp.sum(rc["dshift"], axis=0).reshape(D), jnp.sum(rc["dscale"], axis=0).reshape(D)])
    dnm0 = dnm0 + rc["dgain"]

    gs["norm_mix"] = jnp.concatenate([dnm0, dnm1], axis=0)
    gs["norm_ffn"] = jnp.concatenate([dnf0, dnf1], axis=0)
    gs["final_norm"] = dfinal.reshape(D)
    gs["gla_w_a2"] = jnp.stack([dw2[0:RANK, 0:KEY], dw2[RANK:2 * RANK, KEY:]])
    gs["gla_b_a"] = db2.reshape(2, KEY)
    gs["gla_head_norm"] = dhg.reshape(DV)
    gs["sc_conv_w"] = dscw
    gs["ffn_conv_w"] = jnp.stack([dcw0, dcw1])
    gs["ffn_conv_b"] = jnp.stack([dcb0, dcb1])
    dmods_arr = jnp.stack([jnp.stack([dmods[l][k].reshape(nb, D) for k in range(N_MOD)], axis=1) for l in range(2)])
    return loss, grad_x.reshape(nb, t, D), g_main, p16_gin, p32_gin, gs, dmods_arr, dmc


def _pack(arrs):
    parts, meta, off = [], [], 0
    for a in arrs:
        r = a.size // 128
        rp = -(-r // 8) * 8
        a2 = a.reshape(r, 128).astype(F32)
        if rp != r:
            a2 = jnp.pad(a2, ((0, rp - r), (0, 0)))
        parts.append(a2)
        meta.append((off, r, a.shape))
        off += rp
    return jnp.concatenate(parts, axis=0), meta


def _unpack(buf, meta, lead=()):
    return [buf[..., off:off + r, :].reshape(*lead, *shape) for off, r, shape in meta]


_MAIN = ("up_t0", "up_t1", "down0", "down1", "sc_in_t", "gla_out", "sc_out")
_MAIN_ROWS = {"sc_in_t": 3 * D // N_CHIPS, "up_t0": 2 * HID // N_CHIPS, "up_t1": 2 * HID // N_CHIPS,
              "gla_out": D // N_CHIPS, "sc_out": D // N_CHIPS, "down0": HID // N_CHIPS, "down1": HID // N_CHIPS}
_MAIN_TOTAL = sum(_MAIN_ROWS.values())
_GIN_ROWS = GLA_IN // N_CHIPS
_GIN_PAD = -(-_GIN_ROWS // 32) * 32


def _offsets(names, rows):
    off, out = 0, {}
    for k in names:
        out[k] = off
        off += rows[k]
    return out


def kernel(x, c, ctx, c_ctx, ada_w, ada_b, norm_mix, norm_ffn, gla_w_in, gla_w_a2, gla_b_a, gla_head_norm, gla_w_out, sc_w_in, sc_conv_w, sc_w_out, ffn_w_up, ffn_conv_w, ffn_conv_b, ffn_w_down, final_norm, loss_target, m_c_ctx, m_ada_w, m_ada_b, m_norm_mix, m_norm_ffn, m_gla_w_in, m_gla_w_a2, m_gla_b_a, m_gla_head_norm, m_gla_w_out, m_sc_w_in, m_sc_conv_w, m_sc_w_out, m_ffn_w_up, m_ffn_conv_w, m_ffn_conv_b, m_ffn_w_down, m_final_norm, v_c_ctx, v_ada_w, v_ada_b, v_norm_mix, v_norm_ffn, v_gla_w_in, v_gla_w_a2, v_gla_b_a, v_gla_head_norm, v_gla_w_out, v_sc_w_in, v_sc_conv_w, v_sc_w_out, v_ffn_w_up, v_ffn_conv_w, v_ffn_conv_b, v_ffn_w_down, v_final_norm):
    ix, iy, ic = _place()
    chip = 2 * ix + iy
    dev = 2 * chip + ic
    place = jnp.stack([ic, chip]).astype(jnp.int32)
    nb = x.shape[0]
    offs = _offsets(_MAIN, _MAIN_ROWS)

    buf, meta = _pack([c, ffn_conv_w, sc_conv_w, gla_w_a2, gla_b_a])
    got = _allgather_small(buf, "gather_small_in").reshape(N_DEV, buf.shape[0], 128)
    c_all, fcw, scw, wa2, ba = _unpack(got, meta, (N_DEV,))
    c_all = c_all.reshape(N_DEV * nb, D)
    per_chip = lambda a: a[0::2]
    ffn_conv_w_full = jnp.moveaxis(per_chip(fcw), 0, 2).reshape(2, 3, 2 * HID)
    sc_conv_w_full = jnp.moveaxis(per_chip(scw)[:, 0], 0, 1).reshape(3, D)
    gla_w_a2_full = jnp.moveaxis(per_chip(wa2)[:, 0], 0, 2).reshape(2, RANK, KEY)
    gla_b_a_full = jnp.moveaxis(per_chip(ba)[:, 0], 0, 1).reshape(2, KEY)

    cvec = jnp.concatenate([c_all, c_ctx.reshape(1, D), jnp.zeros((ADA_ROWS - N_DEV * nb - 1, D), F32)], axis=0)
    ada_b_sh = lax.dynamic_slice_in_dim(ada_b, chip * ADA_SH, ADA_SH, axis=1).reshape(2, 1, ADA_SH)
    mod_sh = _ada_fwd(cvec, ada_w, ada_b_sh)
    got = _allgather_small(mod_sh.reshape(2 * ADA_ROWS, ADA_SH), "gather_mod")
    mod_full = jnp.moveaxis(per_chip(got.reshape(N_DEV, 2, ADA_ROWS, ADA_SH)), 0, 2).reshape(2, ADA_ROWS, N_MOD * D)
    mc = mod_full[0, N_DEV * nb, :2 * D].reshape(2, D)

    own = {"sc_in_t": sc_w_in[0].T, "up_t0": ffn_w_up[0].T, "up_t1": ffn_w_up[1].T,
           "gla_out": gla_w_out[0], "sc_out": sc_w_out[0], "down0": ffn_w_down[0], "down1": ffn_w_down[1]}
    own_main = jnp.concatenate([own[k].astype(BF16) for k in _MAIN], axis=0)
    own_gin = jnp.pad(gla_w_in[0].T.astype(BF16), ((0, _GIN_PAD - _GIN_ROWS), (0, 0)))
    ag_gin = _split_start(own_gin, (N_CHIPS, _GIN_PAD, D), _ag_copies, 4, mc, "ag_gin_start")
    ag_main = _split_start(own_main, (N_CHIPS, _MAIN_TOTAL, D), _ag_copies, 4, ag_gin[4], "ag_main_start")
    mods = lax.dynamic_slice_in_dim(mod_full, dev * nb, nb, axis=1).reshape(2, nb, N_MOD, D) + ag_main[4][0, 0]

    small = {"norm_mix": norm_mix, "norm_ffn": norm_ffn, "final_norm": final_norm, "gla_w_a2": gla_w_a2_full,
             "gla_b_a": gla_b_a_full, "gla_head_norm": gla_head_norm[0], "sc_conv_w": sc_conv_w_full,
             "ffn_conv_w": ffn_conv_w_full, "ffn_conv_b": ffn_conv_b}
    loss_p, grad_x, g_main, p16_gin, p32_gin, gs, dmods, dmc = _local_step(x, ctx, loss_target, mods, mc, ag_gin,
                                                                           ag_main, place, small)

    sum_names = ["norm_mix", "norm_ffn", "final_norm", "gla_w_a2", "gla_b_a", "gla_head_norm", "sc_conv_w",
                 "ffn_conv_w", "ffn_conv_b"]
    buf, meta = _pack([jnp.broadcast_to(loss_p, (8, 128))] + [gs[k] for k in sum_names] + [dmc, dmods])
    n_sum = meta[-1][0]
    got = _allgather_small(buf, "gather_small_grads").reshape(N_DEV, buf.shape[0], 128)
    summed = _sum_slots(got[:, :n_sum], "sum_small_grads")
    parts = _unpack(summed, meta[:-1])
    loss = parts[0][0, 0]
    g_small = dict(zip(sum_names, parts[1:-1]))
    dmc_tot = parts[-1]
    dmods_all = jnp.moveaxis(_unpack(got, meta[-1:], (N_DEV,))[0], 0, 1).reshape(2, N_DEV * nb, N_MOD * D)

    ctx_row = jnp.stack([jnp.concatenate([dmc_tot.reshape(2 * D), jnp.zeros(((N_MOD - 2) * D,), F32)]),
                         jnp.zeros((N_MOD * D,), F32)]).reshape(2, 1, N_MOD * D)
    dmod_ext = jnp.concatenate([dmods_all, ctx_row, jnp.zeros((2, ADA_ROWS - N_DEV * nb - 1, N_MOD * D), F32)], axis=1)
    g_ada_b = _sum_slots(jnp.moveaxis(dmod_ext, 1, 0).reshape(ADA_ROWS, 2 * N_MOD * D // 128, 128),
                         "sum_ada_b").reshape(2, N_MOD * D)
    dmod_sh = lax.dynamic_slice_in_dim(dmod_ext, chip * ADA_SH, ADA_SH, axis=2)
    g_ada_w, dcv = _ada_bwd(cvec, ada_w, dmod_sh)
    dscc_part = (dcv[0, N_DEV * nb] + dcv[1, N_DEV * nb]).reshape(8, 128)
    got = _allgather_small(dscc_part, "gather_dscc").reshape(N_DEV, 8, 128)
    g_c_ctx = _cctx_grad(per_chip(got), c_ctx.reshape(8, 128)).reshape(D)

    sc_gin = _split_start(p16_gin, p16_gin.shape, _sc_copies, 3, g_c_ctx, "rs_gin_scatter_start")
    seg = {k: g_main[offs[k]:offs[k] + _MAIN_ROWS[k]] for k in _MAIN}

    sl_chip = lambda a, axis, width: lax.dynamic_slice_in_dim(a, chip * width, width, axis=axis)
    grads = {
        "c_ctx": g_c_ctx, "ada_w": g_ada_w, "ada_b": g_ada_b, "norm_mix": g_small["norm_mix"],
        "norm_ffn": g_small["norm_ffn"],
        "gla_w_a2": sl_chip(g_small["gla_w_a2"], 2, KEY // N_CHIPS)[None],
        "gla_b_a": sl_chip(g_small["gla_b_a"], 1, KEY // N_CHIPS)[None],
        "gla_head_norm": g_small["gla_head_norm"][None], "gla_w_out": seg["gla_out"][None],
        "sc_w_in": seg["sc_in_t"].T[None], "sc_conv_w": sl_chip(g_small["sc_conv_w"], 1, D // N_CHIPS)[None],
        "sc_w_out": seg["sc_out"][None], "ffn_w_up": jnp.stack([seg["up_t0"].T, seg["up_t1"].T]),
        "ffn_conv_w": sl_chip(g_small["ffn_conv_w"], 2, 2 * HID // N_CHIPS), "ffn_conv_b": g_small["ffn_conv_b"],
        "ffn_w_down": jnp.stack([seg["down0"], seg["down1"]]), "final_norm": g_small["final_norm"],
    }
    weights = {"c_ctx": c_ctx, "ada_w": ada_w, "ada_b": ada_b, "norm_mix": norm_mix, "norm_ffn": norm_ffn,
               "gla_w_in": gla_w_in, "gla_w_a2": gla_w_a2, "gla_b_a": gla_b_a, "gla_head_norm": gla_head_norm,
               "gla_w_out": gla_w_out, "sc_w_in": sc_w_in, "sc_conv_w": sc_conv_w, "sc_w_out": sc_w_out,
               "ffn_w_up": ffn_w_up, "ffn_conv_w": ffn_conv_w, "ffn_conv_b": ffn_conv_b, "ffn_w_down": ffn_w_down,
               "final_norm": final_norm}
    mom1 = {"c_ctx": m_c_ctx, "ada_w": m_ada_w, "ada_b": m_ada_b, "norm_mix": m_norm_mix, "norm_ffn": m_norm_ffn,
            "gla_w_in": m_gla_w_in, "gla_w_a2": m_gla_w_a2, "gla_b_a": m_gla_b_a, "gla_head_norm": m_gla_head_norm,
            "gla_w_out": m_gla_w_out, "sc_w_in": m_sc_w_in, "sc_conv_w": m_sc_conv_w, "sc_w_out": m_sc_w_out,
            "ffn_w_up": m_ffn_w_up, "ffn_conv_w": m_ffn_conv_w, "ffn_conv_b": m_ffn_conv_b,
            "ffn_w_down": m_ffn_w_down, "final_norm": m_final_norm}
    mom2 = {"c_ctx": v_c_ctx, "ada_w": v_ada_w, "ada_b": v_ada_b, "norm_mix": v_norm_mix, "norm_ffn": v_norm_ffn,
            "gla_w_in": v_gla_w_in, "gla_w_a2": v_gla_w_a2, "gla_b_a": v_gla_b_a, "gla_head_norm": v_gla_head_norm,
            "gla_w_out": v_gla_w_out, "sc_w_in": v_sc_w_in, "sc_conv_w": v_sc_conv_w, "sc_w_out": v_sc_w_out,
            "ffn_w_up": v_ffn_w_up, "ffn_conv_w": v_ffn_conv_w, "ffn_conv_b": v_ffn_conv_b,
            "ffn_w_down": v_ffn_w_down, "final_norm": v_final_norm}
    names = list(weights)

    big_names = ["ada_w", "gla_w_out", "sc_w_in", "sc_w_out", "ffn_w_up", "ffn_w_down", "gla_w_in"]
    small_names = [k for k in names if k not in big_names]
    delta, new_m, new_v = {}, {}, {}
    done = []

    def big_adamw(k, token):
        shp = weights[k].shape
        as2d = lambda a: a.reshape(-1, shp[-1])
        d_, m_, v_ = _adamw(as2d(weights[k]), as2d(grads[k]), as2d(mom1[k]), as2d(mom2[k]), "adamw_" + k, token)
        done.append(v_[0:1, 0:128])
        delta[k], new_m[k], new_v[k] = d_.reshape(shp), m_.reshape(shp), v_.reshape(shp)

    for k in big_names[:-1]:
        grads[k] = grads[k].reshape(weights[k].shape)
        big_adamw(k, sc_gin[4])
    for k in small_names:
        grads[k] = grads[k].reshape(weights[k].shape)
    packed = [_pack([src[k] for k in small_names]) for src in (weights, grads, mom1, mom2)]
    meta = packed[0][1]
    rows_pad = -packed[0][0].shape[0] % 128
    bufs = [jnp.pad(p[0], ((0, rows_pad), (0, 0))) for p in packed]
    outs = _adamw(bufs[0], bufs[1], bufs[2], bufs[3], "adamw_small", sc_gin[4])
    done.append(outs[2][0:1, :])
    for dst, o in zip((delta, new_m, new_v), outs):
        for k, a in zip(small_names, _unpack(o, meta)):
            dst[k] = a
    landed = _split_wait(sc_gin, jnp.concatenate(done, axis=0), _sc_copies, "rs_gin_scatter_wait")
    g_gin_shard = _rs_pair_gather(_rs_final_sum(place, landed, p32_gin, "rs_gin_final_sum"), "rs_gin_pair_gather")
    grads["gla_w_in"] = g_gin_shard[:_GIN_ROWS].T[None]
    big_adamw("gla_w_in", sc_gin[4])

    return (loss, grad_x, *[grads[k] for k in names], *[delta[k] for k in names], *[new_m[k] for k in names],
            *[new_v[k] for k in names])
```

```python
import functools

import jax
import jax.numpy as jnp
from jax import lax
from jax.experimental import pallas as pl
from jax.experimental.pallas import tpu as pltpu

F32 = jnp.float32
BF16 = jnp.bfloat16
MESH = pl.DeviceIdType.MESH

EPS = 1e-6
D = 1024
N_MOD = 6
HEADS = 4
DK = 128
DV = 256
KEY = HEADS * DK
RANK = 16
TAU = 16.0
CH = 64
GRID_W = 64
HID = 2560
GLA_IN = 2 * KEY + 2 * D + 2 * RANK
GLA_IN_PAD = 3200
Q_SCALE = DK ** -0.5
N_CHIPS = 4
N_DEV = 8

ADAM_LR = 0.001
ADAM_B1 = 0.9
ADAM_B2 = 0.999
ADAM_EPS = 1e-08
ADAM_WD = 0.01
ADAM_STEP = 10

VMEM_LIMIT = 56 * 1024 * 1024


def _params(sem):
    return pltpu.CompilerParams(dimension_semantics=sem, vmem_limit_bytes=VMEM_LIMIT)


def _tile(n, pref, mult=8):
    if n <= pref:
        return n
    for t in range(pref - pref % mult, 0, -mult):
        if n % t == 0:
            return t
    raise ValueError((n, pref, mult))


_NN = (((1,), (0,)), ((), ()))
_NT = (((1,), (1,)), ((), ()))
_TN = (((0,), (0,)), ((), ()))


def _dot(a, b, dims=_NN):
    return lax.dot_general(a.astype(BF16), b.astype(BF16), dims, preferred_element_type=F32)


def _sigmoid(x):
    return 1.0 / (1.0 + jnp.exp(-x))


def _rowsum(x):
    return jnp.sum(x, axis=0, keepdims=True)


def _mm(a, b, form, out_dtype, name, tm, tn):
    if form == "tn":
        K, M = a.shape
    else:
        M, K = a.shape
    N = b.shape[0] if form == "nt" else b.shape[1]
    tm = _tile(M, tm, 128)
    tn = _tile(N, tn, 128)
    dims = {"nn": _NN, "nt": _NT, "tn": _TN}[form]

    def body(a_ref, b_ref, o_ref):
        o_ref[...] = _dot(a_ref[...], b_ref[...], dims).astype(o_ref.dtype)

    if form == "tn":
        a_spec = pl.BlockSpec((K, tm), lambda i, j: (0, i))
    else:
        a_spec = pl.BlockSpec((tm, K), lambda i, j: (i, 0))
    if form == "nt":
        b_spec = pl.BlockSpec((tn, K), lambda i, j: (j, 0))
    else:
        b_spec = pl.BlockSpec((K, tn), lambda i, j: (0, j))
    return pl.pallas_call(
        body,
        name=name,
        grid=(M // tm, N // tn),
        in_specs=[a_spec, b_spec],
        out_specs=pl.BlockSpec((tm, tn), lambda i, j: (i, j)),
        out_shape=jax.ShapeDtypeStruct((M, N), out_dtype),
        compiler_params=_params(("parallel", "parallel")),
    )(a, b)


def _mm_nt_w(a, wg, off, rows, name, tm):
    m = a.shape[0]
    tm = _tile(m, tm, 128)
    if N_CHIPS * rows <= D:

        def body_small(a_ref, w_ref, o_ref):
            av = a_ref[...]
            for s in range(N_CHIPS):
                o_ref[:, s * rows:(s + 1) * rows] = _dot(av, w_ref[s], _NT)

        return pl.pallas_call(
            body_small, name=name, grid=(m // tm,),
            in_specs=[pl.BlockSpec((tm, D), lambda i: (i, 0)),
                      pl.BlockSpec((N_CHIPS, rows, D), lambda i: (0, off // rows, 0))],
            out_specs=pl.BlockSpec((tm, N_CHIPS * rows), lambda i: (i, 0)),
            out_shape=jax.ShapeDtypeStruct((m, N_CHIPS * rows), F32),
            compiler_params=_params(("parallel",)),
        )(a, wg)

    def body(a_ref, w_ref, o_ref):
        o_ref[...] = _dot(a_ref[...], w_ref[0], _NT)

    return pl.pallas_call(
        body, name=name, grid=(m // tm, N_CHIPS),
        in_specs=[pl.BlockSpec((tm, D), lambda i, s: (i, 0)),
                  pl.BlockSpec((1, rows, D), lambda i, s: (s, off // rows, 0))],
        out_specs=pl.BlockSpec((tm, rows), lambda i, s: (i, s)),
        out_shape=jax.ShapeDtypeStruct((m, N_CHIPS * rows), F32),
        compiler_params=_params(("parallel", "parallel")),
    )(a, wg)


def _mm_nn_w(a3, wg, off, rows, name, tm, tn):
    parts, m, kp = a3.shape
    assert parts * kp == N_CHIPS * rows
    tm = _tile(m, tm, 128)
    cuts = sorted({s * rows for s in range(N_CHIPS + 1)} | {p * kp for p in range(parts + 1)})
    pieces = [(k0 // kp, k0 % kp, k0 // rows, k0 % rows, k1 - k0) for k0, k1 in zip(cuts[:-1], cuts[1:])]

    def body(a_ref, w_ref, o_ref):
        acc = None
        for p, a0, s, r0, width in pieces:
            term = _dot(a_ref[p, :, a0:a0 + width], w_ref[s, r0:r0 + width, :])
            acc = term if acc is None else acc + term
        o_ref[...] = acc

    return pl.pallas_call(
        body, name=name, grid=(m // tm, D // tn),
        in_specs=[pl.BlockSpec((parts, tm, kp), lambda i, j: (0, i, 0)),
                  pl.BlockSpec((N_CHIPS, rows, tn), lambda i, j: (0, off // rows, j))],
        out_specs=pl.BlockSpec((tm, tn), lambda i, j: (i, j)),
        out_shape=jax.ShapeDtypeStruct((m, D), F32),
        compiler_params=_params(("parallel", "parallel")),
    )(a3, wg)


def _mm_dw(a3, b, g_prev, off, rows, name, tm):
    parts, ntok, cdim = a3.shape
    assert parts * cdim == N_CHIPS * rows and cdim % tm == 0 and rows % tm == 0 and off % tm == 0

    def body(a_ref, b_ref, *rest):
        rest[-1][0] = _dot(a_ref[0], b_ref[...], _TN)

    in_specs = [pl.BlockSpec((1, ntok, tm), lambda i: ((i * tm) // cdim, 0, ((i * tm) % cdim) // tm)),
                pl.BlockSpec((ntok, D), lambda i: (0, 0))]
    args = [a3, b]
    aliases = {}
    if g_prev is not None:
        in_specs.append(pl.BlockSpec(memory_space=pl.ANY))
        args.append(g_prev)
        aliases = {2: 0}
    return pl.pallas_call(
        body, name=name, grid=(N_CHIPS * rows // tm,),
        in_specs=in_specs,
        out_specs=pl.BlockSpec((1, tm, D), lambda i: ((i * tm) // rows, (off + (i * tm) % rows) // tm, 0)),
        out_shape=jax.ShapeDtypeStruct((N_CHIPS, _MAIN_TOTAL, D), F32),
        input_output_aliases=aliases,
        compiler_params=_params(("parallel",)),
    )(*args)


def _mod_fwd(h, gain, shift, scale, tpb_rows, name, y=None, gate=None):
    n = h.shape[0]
    tt = _tile(tpb_rows, 256)
    tpb = tpb_rows // tt
    has_res = y is not None

    def body(*refs):
        if has_res:
            h_ref, y_ref, gate_ref, gain_ref, sh_ref, sc_ref, hout_ref, hn_ref = refs
            hv = h_ref[...] + gate_ref[0] * y_ref[...]
            hout_ref[...] = hv
        else:
            h_ref, gain_ref, sh_ref, sc_ref, hn_ref = refs
            hv = h_ref[...]
        r = lax.rsqrt(jnp.mean(hv * hv, axis=-1, keepdims=True) + EPS)
        hn = (hv * r) * gain_ref[...] * (1.0 + sc_ref[0]) + sh_ref[0]
        hn_ref[...] = hn.astype(BF16)

    row = pl.BlockSpec((tt, D), lambda i: (i, 0))
    per_b = pl.BlockSpec((1, 1, D), lambda i: (i // tpb, 0, 0))
    vec = pl.BlockSpec((1, D), lambda i: (0, 0))
    if has_res:
        in_specs = [row, row, per_b, vec, per_b, per_b]
        args = (h, y, gate, gain, shift, scale)
        out_specs = [row, row]
        out_shape = [jax.ShapeDtypeStruct((n, D), F32), jax.ShapeDtypeStruct((n, D), BF16)]
    else:
        in_specs = [row, vec, per_b, per_b]
        args = (h, gain, shift, scale)
        out_specs = row
        out_shape = jax.ShapeDtypeStruct((n, D), BF16)
    return pl.pallas_call(
        body, name=name, grid=(n // tt,), in_specs=in_specs, out_specs=out_specs, out_shape=out_shape,
        compiler_params=_params(("parallel",)),
    )(*args)


def _mod_bwd(h_in, dhn, gain, scale, tpb_rows, name, dhn_row0=0, dh_out=None, y_prev=None, gate_prev=None,
             need_dh=True):
    n = h_in.shape[0]
    nb = n // tpb_rows
    tt = _tile(tpb_rows, 256)
    tpb = tpb_rows // tt
    off = dhn_row0 // tt
    assert dhn_row0 % tt == 0
    has_out = dh_out is not None
    has_prev = y_prev is not None

    def body(*refs):
        it = iter(refs)
        h_ref, dhn_ref, gain_ref, sc_ref = next(it), next(it), next(it), next(it)
        dho_ref = next(it) if has_out else None
        yp_ref, gp_ref = (next(it), next(it)) if has_prev else (None, None)
        dh_ref = next(it) if need_dh else None
        dsc_ref, dsh_ref, dgain_ref = next(it), next(it), next(it)
        dyp_ref, dgp_ref = (next(it), next(it)) if has_prev else (None, None)
        i = pl.program_id(0)

        @pl.when(i == 0)
        def _():
            dgain_ref[...] = jnp.zeros_like(dgain_ref)

        @pl.when(i % tpb == 0)
        def _():
            dsc_ref[...] = jnp.zeros_like(dsc_ref)
            dsh_ref[...] = jnp.zeros_like(dsh_ref)
            if has_prev:
                dgp_ref[...] = jnp.zeros_like(dgp_ref)

        hv = h_ref[...]
        r = lax.rsqrt(jnp.mean(hv * hv, axis=-1, keepdims=True) + EPS)
        y = hv * r
        gain_v = gain_ref[...]
        g = dhn_ref[...].astype(F32)
        dsh_ref[0] += _rowsum(g)
        dsc_ref[0] += _rowsum(g * (y * gain_v))
        drn = g * (1.0 + sc_ref[0])
        dgain_ref[...] += _rowsum(drn * y)
        if need_dh:
            dy = drn * gain_v
            dh = r * (dy - y * jnp.mean(dy * y, axis=-1, keepdims=True))
            if has_out:
                dh = dh + dho_ref[...]
            dh_ref[...] = dh
            if has_prev:
                dyp_ref[...] = (dh * gp_ref[0]).astype(BF16)
                dgp_ref[0] += _rowsum(dh * yp_ref[...])

    row = pl.BlockSpec((tt, D), lambda i: (i, 0))
    row_off = pl.BlockSpec((tt, D), lambda i: (i + off, 0))
    per_b = pl.BlockSpec((1, 1, D), lambda i: (i // tpb, 0, 0))
    vec = pl.BlockSpec((1, D), lambda i: (0, 0))
    in_specs = [row, row_off, vec, per_b]
    args = [h_in, dhn, gain, scale]
    if has_out:
        in_specs.append(row)
        args.append(dh_out)
    if has_prev:
        in_specs += [row, per_b]
        args += [y_prev, gate_prev]
    out_specs, out_shape, names = [], [], []
    if need_dh:
        out_specs.append(row)
        out_shape.append(jax.ShapeDtypeStruct((n, D), F32))
        names.append("dh")
    for nm in ("dscale", "dshift"):
        out_specs.append(per_b)
        out_shape.append(jax.ShapeDtypeStruct((nb, 1, D), F32))
        names.append(nm)
    out_specs.append(vec)
    out_shape.append(jax.ShapeDtypeStruct((1, D), F32))
    names.append("dgain")
    if has_prev:
        out_specs += [row, per_b]
        out_shape += [jax.ShapeDtypeStruct((n, D), BF16), jax.ShapeDtypeStruct((nb, 1, D), F32)]
        names += ["dy_prev", "dgate_prev"]
    outs = pl.pallas_call(
        body, name=name, grid=(n // tt,), in_specs=in_specs, out_specs=out_specs, out_shape=out_shape,
        compiler_params=_params(("arbitrary",)),
    )(*args)
    return dict(zip(names, outs))


def _final(h, f, gate, gain, tgt, tpb_rows):
    n = h.shape[0]
    nb = n // tpb_rows
    tt = _tile(tpb_rows, 256)
    tpb = tpb_rows // tt

    def body(h_ref, f_ref, gate_ref, gain_ref, tgt_ref, loss_ref, dh_ref, df_ref, dgate_ref, dgain_ref):
        i = pl.program_id(0)

        @pl.when(i == 0)
        def _():
            loss_ref[...] = jnp.zeros_like(loss_ref)
            dgain_ref[...] = jnp.zeros_like(dgain_ref)

        @pl.when(i % tpb == 0)
        def _():
            dgate_ref[...] = jnp.zeros_like(dgate_ref)

        fv = f_ref[...]
        gate_v = gate_ref[0]
        hv = h_ref[...] + gate_v * fv
        r = lax.rsqrt(jnp.mean(hv * hv, axis=-1, keepdims=True) + EPS)
        y = hv * r
        gain_v = gain_ref[...]
        e = y * gain_v - tgt_ref[...]
        s = jnp.sum(_rowsum(e * e), axis=1, keepdims=True) * (0.5 / D)
        loss_ref[...] += jnp.broadcast_to(s, loss_ref.shape)
        dout = e * (1.0 / D)
        dgain_ref[...] += _rowsum(dout * y)
        dy = dout * gain_v
        dh = r * (dy - y * jnp.mean(dy * y, axis=-1, keepdims=True))
        dh_ref[...] = dh
        df_ref[...] = (dh * gate_v).astype(BF16)
        dgate_ref[0] += _rowsum(dh * fv)

    row = pl.BlockSpec((tt, D), lambda i: (i, 0))
    per_b = pl.BlockSpec((1, 1, D), lambda i: (i // tpb, 0, 0))
    vec = pl.BlockSpec((1, D), lambda i: (0, 0))
    return pl.pallas_call(
        body, name="final_loss", grid=(n // tt,),
        in_specs=[row, row, per_b, vec, row],
        out_specs=[pl.BlockSpec((1, 128), lambda i: (0, 0)), row, row, per_b, vec],
        out_shape=[jax.ShapeDtypeStruct((1, 128), F32), jax.ShapeDtypeStruct((n, D), F32),
                   jax.ShapeDtypeStruct((n, D), BF16), jax.ShapeDtypeStruct((nb, 1, D), F32),
                   jax.ShapeDtypeStruct((1, D), F32)],
        compiler_params=_params(("arbitrary",)),
    )(h, f, gate, gain, tgt)


def _shift_dn(x, s):
    return jnp.concatenate([jnp.zeros((s, x.shape[1]), x.dtype), x[: x.shape[0] - s]], axis=0)


def _shift_up(x, s):
    return jnp.concatenate([x[s:], jnp.zeros((s, x.shape[1]), x.dtype)], axis=0)


def _row_dn1(x):
    t = lax.broadcasted_iota(jnp.int32, x.shape, 0)
    return jnp.where(t % GRID_W == 0, 0.0, pltpu.roll(x, 1, 0))


def _row_up1(x):
    t = lax.broadcasted_iota(jnp.int32, x.shape, 0)
    return jnp.where(t % GRID_W == GRID_W - 1, 0.0, pltpu.roll(x, x.shape[0] - 1, 0))


def _silu(x):
    return x * _sigmoid(x)


def _dsilu(x):
    s = _sigmoid(x)
    return s * (1.0 + x * (1.0 - s))


def _conv_cols(x, w_ref):
    return _shift_dn(x, GRID_W) * w_ref[0:1, :] + x * w_ref[1:2, :] + _shift_up(x, GRID_W) * w_ref[2:3, :]


def _conv_cols_bwd(x, du, w_ref, dw_ref, db_ref):
    db_ref[...] += _rowsum(du)
    dw_ref[0:1, :] += _rowsum(du * _shift_dn(x, GRID_W))
    dw_ref[1:2, :] += _rowsum(du * x)
    dw_ref[2:3, :] += _rowsum(du * _shift_up(x, GRID_W))
    return _shift_up(du, GRID_W) * w_ref[0:1, :] + du * w_ref[1:2, :] + _shift_dn(du, GRID_W) * w_ref[2:3, :]


def _ffn_mid_fwd(u0, cw, cb, nb, t, name):
    nc = HID // 128

    def body(ua_ref, ug_ref, wa_ref, wg_ref, ba_ref, bg_ref, z_ref):
        a = _conv_cols(ua_ref[...], wa_ref) + ba_ref[...]
        gt = _conv_cols(ug_ref[...], wg_ref) + bg_ref[...]
        z_ref[...] = (a * _silu(gt)).astype(BF16)

    col = lambda rows, part: pl.BlockSpec((rows, 128), lambda j, b: (b if rows == t else 0, part * nc + j))
    return pl.pallas_call(
        body, name=name, grid=(nc, nb),
        in_specs=[col(t, 0), col(t, 1), col(3, 0), col(3, 1), col(1, 0), col(1, 1)],
        out_specs=pl.BlockSpec((t, 128), lambda j, b: (b, j)),
        out_shape=jax.ShapeDtypeStruct((nb * t, HID), BF16),
        compiler_params=_params(("parallel", "parallel")),
    )(u0, u0, cw, cw, cb, cb)


def _ffn_mid_bwd(u0, cw, cb, dz, nb, t, name):
    nc = HID // 128

    def body(ua_ref, ug_ref, wa_ref, wg_ref, ba_ref, bg_ref, dz_ref, du_ref, dw_ref, db_ref):
        b = pl.program_id(1)

        @pl.when(b == 0)
        def _():
            dw_ref[...] = jnp.zeros_like(dw_ref)
            db_ref[...] = jnp.zeros_like(db_ref)

        xa = ua_ref[...]
        xg = ug_ref[...]
        a = _conv_cols(xa, wa_ref) + ba_ref[...]
        gt = _conv_cols(xg, wg_ref) + bg_ref[...]
        dzv = dz_ref[...]
        du_ref[0] = _conv_cols_bwd(xa, dzv * _silu(gt), wa_ref, dw_ref.at[0], db_ref.at[0]).astype(BF16)
        du_ref[1] = _conv_cols_bwd(xg, dzv * a * _dsilu(gt), wg_ref, dw_ref.at[1], db_ref.at[1]).astype(BF16)

    col = lambda rows, part: pl.BlockSpec((rows, 128), lambda j, b: (b if rows == t else 0, part * nc + j))
    return pl.pallas_call(
        body, name=name, grid=(nc, nb),
        in_specs=[col(t, 0), col(t, 1), col(3, 0), col(3, 1), col(1, 0), col(1, 1),
                  pl.BlockSpec((t, 128), lambda j, b: (b, j))],
        out_specs=[pl.BlockSpec((2, t, 128), lambda j, b: (0, b, j)), pl.BlockSpec((2, 3, 128), lambda j, b: (0, 0, j)),
                   pl.BlockSpec((2, 1, 128), lambda j, b: (0, 0, j))],
        out_shape=[jax.ShapeDtypeStruct((2, nb * t, HID), BF16), jax.ShapeDtypeStruct((2, 3, HID), F32),
                   jax.ShapeDtypeStruct((2, 1, HID), F32)],
        compiler_params=_params(("parallel", "arbitrary")),
    )(u0, u0, cw, cw, cb, cb, dz)


def _sc_mid_fwd(p, cw, nb, t):
    nc = D // 128

    def body(bg_ref, cg_ref, v_ref, w_ref, y_ref):
        cv = cg_ref[...] * v_ref[...]
        cc = _row_dn1(cv) * w_ref[0:1, :] + cv * w_ref[1:2, :] + _row_up1(cv) * w_ref[2:3, :]
        y_ref[...] = (bg_ref[...] * cc).astype(BF16)

    part = lambda k: pl.BlockSpec((t, 128), lambda j, b: (b, k * nc + j))
    return pl.pallas_call(
        body, name="sc_mid_fwd", grid=(nc, nb),
        in_specs=[part(0), part(1), part(2), pl.BlockSpec((3, 128), lambda j, b: (0, j))],
        out_specs=pl.BlockSpec((t, 128), lambda j, b: (b, j)),
        out_shape=jax.ShapeDtypeStruct((nb * t, D), BF16),
        compiler_params=_params(("parallel", "parallel")),
    )(p, p, p, cw)


def _sc_mid_bwd(p, cw, dyb, nb, t):
    nc = D // 128

    def body(bg_ref, cg_ref, v_ref, w_ref, dy_ref, dp_ref, dw_ref):
        b = pl.program_id(1)

        @pl.when(b == 0)
        def _():
            dw_ref[...] = jnp.zeros_like(dw_ref)

        w0, w1, w2 = w_ref[0:1, :], w_ref[1:2, :], w_ref[2:3, :]
        cg, v = cg_ref[...], v_ref[...]
        cv = cg * v
        cvd = _row_dn1(cv)
        cvu = _row_up1(cv)
        cc = cvd * w0 + cv * w1 + cvu * w2
        dy = dy_ref[...]
        dcc = dy * bg_ref[...]
        dw_ref[0:1, :] += _rowsum(dcc * cvd)
        dw_ref[1:2, :] += _rowsum(dcc * cv)
        dw_ref[2:3, :] += _rowsum(dcc * cvu)
        dcv = _row_up1(dcc) * w0 + dcc * w1 + _row_dn1(dcc) * w2
        dp_ref[0] = (dy * cc).astype(BF16)
        dp_ref[1] = (dcv * v).astype(BF16)
        dp_ref[2] = (dcv * cg).astype(BF16)

    part = lambda k: pl.BlockSpec((t, 128), lambda j, b: (b, k * nc + j))
    return pl.pallas_call(
        body, name="sc_mid_bwd", grid=(nc, nb),
        in_specs=[part(0), part(1), part(2), pl.BlockSpec((3, 128), lambda j, b: (0, j)),
                  pl.BlockSpec((t, 128), lambda j, b: (b, j))],
        out_specs=[pl.BlockSpec((3, t, 128), lambda j, b: (0, b, j)), pl.BlockSpec((3, 128), lambda j, b: (0, j))],
        out_shape=[jax.ShapeDtypeStruct((3, nb * t, D), BF16), jax.ShapeDtypeStruct((3, D), F32)],
        compiler_params=_params(("parallel", "arbitrary")),
    )(p, p, p, cw, dyb)


def _gla_decay_fwd(p_all, w2, b2):
    n = p_all.shape[0]
    tt = _tile(n, 512)

    def body(a_ref, w_ref, b_ref, la_ref):
        z = _dot(a_ref[...], w_ref[...]) + b_ref[...]
        la_ref[...] = (jnp.minimum(z, 0.0) - jnp.log(1.0 + jnp.exp(-jnp.abs(z)))) * (1.0 / TAU)

    return pl.pallas_call(
        body, name="gla_decay_fwd", grid=(n // tt,),
        in_specs=[pl.BlockSpec((tt, 128), lambda i: (i, (2 * KEY + 2 * D) // 128)),
                  pl.BlockSpec((128, 2 * KEY), lambda i: (0, 0)), pl.BlockSpec((1, 2 * KEY), lambda i: (0, 0))],
        out_specs=pl.BlockSpec((tt, 2 * KEY), lambda i: (i, 0)),
        out_shape=jax.ShapeDtypeStruct((n, 2 * KEY), F32),
        compiler_params=_params(("parallel",)),
    )(p_all, w2, b2)


def _gla_blocks(nb, nm, ncx):
    def main_idx(d, i):
        return jnp.clip(jnp.where(d == 0, i - ncx, nm - 1 - (i - ncx)), 0, nm - 1)

    def rowblk(d, b, i):
        cidx = jnp.where(d == 0, i, ncx - 1 - i)
        return jnp.where(i < ncx, nb * nm + b * ncx + cidx, b * nm + main_idx(d, i))

    def mainblk(d, b, i):
        return b * nm + main_idx(d, i)

    return rowblk, mainblk


def _gla_mask(d):
    row = lax.broadcasted_iota(jnp.int32, (CH, CH), 0)
    col = lax.broadcasted_iota(jnp.int32, (CH, CH), 1)
    diff = jnp.where(d == 0, row - col, col - row)
    mask = diff >= 0
    return mask, jnp.where(mask, 1.0, 0.0).astype(BF16), jnp.where(diff <= 0, 1.0, 0.0).astype(BF16)


def _tri_sum(m01, x):
    w = x.shape[1]
    hi = x.astype(BF16)
    r1 = x - hi.astype(F32)
    mid = r1.astype(BF16)
    lo = (r1 - mid.astype(F32)).astype(BF16)
    s = lax.dot_general(m01, jnp.concatenate([hi, mid, lo], axis=1), _NN, preferred_element_type=F32)
    return s[:, :w] + s[:, w:2 * w] + s[:, 2 * w:]


def _gla_chunk(q, k, g, bc):
    bl = _rowsum(g)
    eq = jnp.exp(bc)
    ek = jnp.exp(-bc)
    ed = jnp.exp(bl - bc)
    return bl, eq, ek, ed, q * Q_SCALE * eq, k * ek, k * ed


def _gla_scan_fwd(p_all, la_all, nb, t, tc):
    nm, ncx = t // CH, tc // CH
    nst = nm + ncx
    rowblk, mainblk = _gla_blocks(nb, nm, ncx)

    def body(q_ref, k_ref, v_ref, la_ref, o_ref, ss_ref, st_ref):
        d = pl.program_id(0)
        i = pl.program_id(2)

        @pl.when(i == 0)
        def _():
            st_ref[...] = jnp.zeros_like(st_ref)

        mask, m01, _ = _gla_mask(d)
        q_all, k_all, v_all, g_all = q_ref[...], k_ref[...], v_ref[...], la_ref[...]
        states = [st_ref[h] for h in range(HEADS)]
        bc_all = _tri_sum(m01, g_all)
        outs, new_states = [], []
        for h in range(HEADS):
            ksl = slice(h * DK, (h + 1) * DK)
            v = v_all[:, h * DV:(h + 1) * DV]
            bl, _, _, _, qs, ks, kd = _gla_chunk(q_all[:, ksl], k_all[:, ksl], g_all[:, ksl], bc_all[:, ksl])
            att = jnp.where(mask, _dot(qs, ks, _NT), 0.0)
            outs.append(_dot(qs, states[h], _NT) + _dot(att, v))
            new_states.append(states[h] * jnp.exp(bl) + _dot(v, kd, _TN))
        o_ref[0] = jnp.concatenate(outs, axis=1)
        for h in range(HEADS):
            ss_ref[0, 0, 0, h] = states[h]
            st_ref[h] = new_states[h]

    return pl.pallas_call(
        body, name="gla_scan_fwd", grid=(2, nb, nst),
        in_specs=[
            pl.BlockSpec((CH, KEY), lambda d, b, i: (rowblk(d, b, i), 0)),
            pl.BlockSpec((CH, KEY), lambda d, b, i: (rowblk(d, b, i), 1)),
            pl.BlockSpec((CH, D), lambda d, b, i: (rowblk(d, b, i), 1)),
            pl.BlockSpec((CH, KEY), lambda d, b, i: (rowblk(d, b, i), d)),
        ],
        out_specs=[
            pl.BlockSpec((1, CH, D), lambda d, b, i: (d, mainblk(d, b, i), 0)),
            pl.BlockSpec((1, 1, 1, HEADS, DV, DK), lambda d, b, i: (d, b, i, 0, 0, 0)),
        ],
        out_shape=[jax.ShapeDtypeStruct((2, nb * t, D), F32),
                   jax.ShapeDtypeStruct((2, nb, nst, HEADS, DV, DK), F32)],
        scratch_shapes=[pltpu.VMEM((HEADS, DV, DK), F32)],
        compiler_params=_params(("parallel", "parallel", "arbitrary")),
    )(p_all, p_all, p_all, la_all)


def _gla_scan_bwd(p_all, la_all, do, ss, nb, t, tc, after):
    nm, ncx = t // CH, tc // CH
    nst = nm + ncx
    ntot = nb * (t + tc)
    rowblk, mainblk = _gla_blocks(nb, nm, ncx)

    def body(q_ref, k_ref, v_ref, la_ref, do_ref, ss_ref, after_ref, dq_ref, dk_ref, dv_ref, dla_ref, dst_ref):
        d = pl.program_id(0)
        ip = pl.program_id(2)
        i = nst - 1 - ip

        @pl.when(ip == 0)
        def _():
            dst_ref[...] = jnp.zeros_like(dst_ref)

        mask, m01, m01_t = _gla_mask(d)
        live = jnp.where(i >= ncx, 1.0, 0.0)
        q_all, k_all, v_all, g_all = q_ref[...], k_ref[...], v_ref[...], la_ref[...]
        do_all = do_ref[...] * live
        states = [ss_ref[0, 0, 0, h] for h in range(HEADS)]
        dstates = [dst_ref[h] for h in range(HEADS)]
        bc_all = _tri_sum(m01, g_all)
        dqs_l, dks_l, dvs_l, dbs_l, dbls_l, new_dstates = [], [], [], [], [], []
        for h in range(HEADS):
            ksl = slice(h * DK, (h + 1) * DK)
            vsl = slice(h * DV, (h + 1) * DV)
            bl, eq, ek, ed, qs, ks, kd = _gla_chunk(q_all[:, ksl], k_all[:, ksl], g_all[:, ksl], bc_all[:, ksl])
            st, dst, v, dov = states[h], dstates[h], v_all[:, vsl], do_all[:, vsl]
            att = jnp.where(mask, _dot(qs, ks, _NT), 0.0)
            datt = jnp.where(mask, _dot(dov, v, _NT), 0.0)
            dqs = _dot(dov, st) + _dot(datt, ks)
            dks = _dot(datt, qs, _TN)
            dvs_l.append(_dot(att, dov, _TN) + _dot(kd, dst, _NT))
            dkd = _dot(v, dst)
            e = jnp.exp(bl)
            dbls_l.append(e * _rowsum(st * dst) + _rowsum(dkd * kd))
            new_dstates.append(_dot(dov, qs, _TN) + dst * e)
            dqs_l.append(dqs * eq * Q_SCALE)
            dks_l.append(dks * ek + dkd * ed)
            dbs_l.append(dqs * qs - dks * ks - dkd * kd)
        dq_ref[0] = jnp.concatenate(dqs_l, axis=1)
        dk_ref[0] = jnp.concatenate(dks_l, axis=1)
        dv_ref[0] = jnp.concatenate(dvs_l, axis=1)
        dla_ref[...] = _tri_sum(m01_t, jnp.concatenate(dbs_l, axis=1)) + jnp.concatenate(dbls_l, axis=1)
        for h in range(HEADS):
            dst_ref[h] = new_dstates[h]

    rev = lambda f: (lambda d, b, ip: f(d, b, nst - 1 - ip))
    return pl.pallas_call(
        body, name="gla_scan_bwd", grid=(2, nb, nst),
        in_specs=[
            pl.BlockSpec((CH, KEY), rev(lambda d, b, i: (rowblk(d, b, i), 0))),
            pl.BlockSpec((CH, KEY), rev(lambda d, b, i: (rowblk(d, b, i), 1))),
            pl.BlockSpec((CH, D), rev(lambda d, b, i: (rowblk(d, b, i), 1))),
            pl.BlockSpec((CH, KEY), rev(lambda d, b, i: (rowblk(d, b, i), d))),
            pl.BlockSpec((CH, D), rev(lambda d, b, i: (mainblk(d, b, i), 0))),
            pl.BlockSpec((1, 1, 1, HEADS, DV, DK), rev(lambda d, b, i: (d, b, i, 0, 0, 0))),
            pl.BlockSpec(memory_space=pl.ANY),
        ],
        out_specs=[
            pl.BlockSpec((1, CH, KEY), rev(lambda d, b, i: (d, rowblk(d, b, i), 0))),
            pl.BlockSpec((1, CH, KEY), rev(lambda d, b, i: (d, rowblk(d, b, i), 0))),
            pl.BlockSpec((1, CH, D), rev(lambda d, b, i: (d, rowblk(d, b, i), 0))),
            pl.BlockSpec((CH, KEY), rev(lambda d, b, i: (rowblk(d, b, i), d))),
        ],
        out_shape=[jax.ShapeDtypeStruct((2, ntot, KEY), F32), jax.ShapeDtypeStruct((2, ntot, KEY), F32),
                   jax.ShapeDtypeStruct((2, ntot, D), F32), jax.ShapeDtypeStruct((ntot, 2 * KEY), F32)],
        scratch_shapes=[pltpu.VMEM((HEADS, DV, DK), F32)],
        compiler_params=_params(("parallel", "parallel", "arbitrary")),
    )(p_all, p_all, p_all, la_all, do, ss, after)


def _gla_post_fwd(o2, p_all, head_gain, n):
    tt = _tile(n, 256)

    def body(o_ref, g_ref, hg_ref, y_ref):
        o = o_ref[0] + o_ref[1]
        gv = g_ref[...]
        hg = hg_ref[...]
        for h in range(HEADS):
            oh = o[:, h * DV:(h + 1) * DV]
            r = lax.rsqrt(jnp.mean(oh * oh, axis=-1, keepdims=True) + EPS)
            y_ref[:, h * DV:(h + 1) * DV] = ((oh * r) * hg * _silu(gv[:, h * DV:(h + 1) * DV])).astype(BF16)

    return pl.pallas_call(
        body, name="gla_post_fwd", grid=(n // tt,),
        in_specs=[pl.BlockSpec((2, tt, D), lambda i: (0, i, 0)), pl.BlockSpec((tt, D), lambda i: (i, 2)),
                  pl.BlockSpec((1, DV), lambda i: (0, 0))],
        out_specs=pl.BlockSpec((tt, D), lambda i: (i, 0)),
        out_shape=jax.ShapeDtypeStruct((n, D), BF16),
        compiler_params=_params(("parallel",)),
    )(o2, p_all, head_gain)


def _gla_post_bwd(o2, p_all, head_gain, dyb, n):
    tt = _tile(n, 256)

    def body(o_ref, g_ref, hg_ref, dy_ref, do_ref, dg_ref, dhg_ref):
        i = pl.program_id(0)

        @pl.when(i == 0)
        def _():
            dhg_ref[...] = jnp.zeros_like(dhg_ref)

        o = o_ref[0] + o_ref[1]
        gv = g_ref[...]
        hg = hg_ref[...]
        dy = dy_ref[...]
        acc = jnp.zeros((1, DV), F32)
        for h in range(HEADS):
            sl = slice(h * DV, (h + 1) * DV)
            oh = o[:, sl]
            r = lax.rsqrt(jnp.mean(oh * oh, axis=-1, keepdims=True) + EPS)
            on = oh * r
            gh = gv[:, sl]
            dyh = dy[:, sl]
            dg_ref[:, sl] = dyh * (on * hg) * _dsilu(gh)
            dog = dyh * _silu(gh)
            acc = acc + _rowsum(dog * on)
            don = dog * hg
            do_ref[:, sl] = r * (don - on * jnp.mean(don * on, axis=-1, keepdims=True))
        dhg_ref[...] += acc

    return pl.pallas_call(
        body, name="gla_post_bwd", grid=(n // tt,),
        in_specs=[pl.BlockSpec((2, tt, D), lambda i: (0, i, 0)), pl.BlockSpec((tt, D), lambda i: (i, 2)),
                  pl.BlockSpec((1, DV), lambda i: (0, 0)), pl.BlockSpec((tt, D), lambda i: (i, 0))],
        out_specs=[pl.BlockSpec((tt, D), lambda i: (i, 0)), pl.BlockSpec((tt, D), lambda i: (i, 0)),
                   pl.BlockSpec((1, DV), lambda i: (0, 0))],
        out_shape=[jax.ShapeDtypeStruct((n, D), F32), jax.ShapeDtypeStruct((n, D), F32),
                   jax.ShapeDtypeStruct((1, DV), F32)],
        compiler_params=_params(("arbitrary",)),
    )(o2, p_all, head_gain, dyb)


def _gla_assemble(p_all, w2, b2, dq, dk, dv, dla, dgate, n):
    ntot = p_all.shape[0]
    tt = _tile(n, 128)
    nmain = n // tt
    assert ntot % tt == 0

    def body(a_ref, w_ref, b_ref, dq_ref, dk_ref, dv_ref, dla_ref, dg_ref, dp_ref, dw_ref, db_ref):
        i = pl.program_id(0)

        @pl.when(i == 0)
        def _():
            dw_ref[...] = jnp.zeros_like(dw_ref)
            db_ref[...] = jnp.zeros_like(db_ref)

        a = a_ref[...]
        w = w_ref[...]
        z = _dot(a, w) + b_ref[...]
        dz = dla_ref[...] * (1.0 / (1.0 + jnp.exp(z))) * (1.0 / TAU)
        dw_ref[...] += _dot(a, dz, _TN)
        db_ref[...] += _rowsum(dz)
        dp_ref[:, 0:KEY] = ((dq_ref[0] + dq_ref[1]) * 1.0).astype(BF16)
        dp_ref[:, KEY:2 * KEY] = (dk_ref[0] + dk_ref[1]).astype(BF16)
        dp_ref[:, 2 * KEY:2 * KEY + D] = (dv_ref[0] + dv_ref[1]).astype(BF16)
        dp_ref[:, 2 * KEY + D:2 * KEY + 2 * D] = (dg_ref[...] * jnp.where(i < nmain, 1.0, 0.0)).astype(BF16)
        dp_ref[:, 2 * KEY + 2 * D:GLA_IN_PAD] = _dot(dz, w, _NT).astype(BF16)

    return pl.pallas_call(
        body, name="gla_assemble", grid=(ntot // tt,),
        in_specs=[pl.BlockSpec((tt, 128), lambda i: (i, (2 * KEY + 2 * D) // 128)),
                  pl.BlockSpec((128, 2 * KEY), lambda i: (0, 0)), pl.BlockSpec((1, 2 * KEY), lambda i: (0, 0)),
                  pl.BlockSpec((2, tt, KEY), lambda i: (0, i, 0)), pl.BlockSpec((2, tt, KEY), lambda i: (0, i, 0)),
                  pl.BlockSpec((2, tt, D), lambda i: (0, i, 0)), pl.BlockSpec((tt, 2 * KEY), lambda i: (i, 0)),
                  pl.BlockSpec((tt, D), lambda i: (jnp.minimum(i, nmain - 1), 0))],
        out_specs=[pl.BlockSpec((tt, GLA_IN_PAD), lambda i: (i, 0)), pl.BlockSpec((128, 2 * KEY), lambda i: (0, 0)),
                   pl.BlockSpec((1, 2 * KEY), lambda i: (0, 0))],
        out_shape=[jax.ShapeDtypeStruct((ntot, GLA_IN_PAD), BF16), jax.ShapeDtypeStruct((128, 2 * KEY), F32),
                   jax.ShapeDtypeStruct((1, 2 * KEY), F32)],
        compiler_params=_params(("arbitrary",)),
    )(p_all, w2, b2, dq, dk, dv, dla, dgate)


ADA_ROWS = 24
ADA_SH = N_MOD * D // N_CHIPS


def _ada_fwd(cvec, ada_w, ada_b_sh):
    def body(c_ref, w_ref, b_ref, o_ref):
        o_ref[0] = _dot(_silu(c_ref[...]), w_ref[0]) + b_ref[0]

    return pl.pallas_call(
        body, name="ada_fwd", grid=(2,),
        in_specs=[pl.BlockSpec((ADA_ROWS, D), lambda l: (0, 0)), pl.BlockSpec((1, D, ADA_SH), lambda l: (l, 0, 0)),
                  pl.BlockSpec((1, 1, ADA_SH), lambda l: (l, 0, 0))],
        out_specs=pl.BlockSpec((1, ADA_ROWS, ADA_SH), lambda l: (l, 0, 0)),
        out_shape=jax.ShapeDtypeStruct((2, ADA_ROWS, ADA_SH), F32),
        compiler_params=_params(("parallel",)),
    )(cvec, ada_w, ada_b_sh)


def _ada_bwd(cvec, ada_w, dmod_sh):
    def body(c_ref, w_ref, dm_ref, gw_ref, dc_ref):
        dm = dm_ref[0]
        gw_ref[0] = _dot(_silu(c_ref[...]), dm, _TN)
        dc_ref[0] = _dot(dm, w_ref[0], _NT)

    return pl.pallas_call(
        body, name="ada_bwd", grid=(2,),
        in_specs=[pl.BlockSpec((ADA_ROWS, D), lambda l: (0, 0)), pl.BlockSpec((1, D, ADA_SH), lambda l: (l, 0, 0)),
                  pl.BlockSpec((1, ADA_ROWS, ADA_SH), lambda l: (l, 0, 0))],
        out_specs=[pl.BlockSpec((1, D, ADA_SH), lambda l: (l, 0, 0)), pl.BlockSpec((1, ADA_ROWS, D), lambda l: (l, 0, 0))],
        out_shape=[jax.ShapeDtypeStruct((2, D, ADA_SH), F32), jax.ShapeDtypeStruct((2, ADA_ROWS, D), F32)],
        compiler_params=_params(("parallel",)),
    )(cvec, ada_w, dmod_sh)


def _sum_slots(x, name):
    s, r, _ = x.shape

    def body(x_ref, o_ref):
        acc = x_ref[0]
        for k in range(1, s):
            acc = acc + x_ref[k]
        o_ref[...] = acc

    return pl.pallas_call(
        body, name=name, out_shape=jax.ShapeDtypeStruct((r, 128), F32),
        in_specs=[pl.BlockSpec(memory_space=pltpu.VMEM)], out_specs=pl.BlockSpec(memory_space=pltpu.VMEM),
    )(x)


def _cctx_grad(dscc_parts, c_ctx):
    def body(p_ref, c_ref, o_ref):
        acc = p_ref[0]
        for k in range(1, N_CHIPS):
            acc = acc + p_ref[k]
        o_ref[...] = acc * _dsilu(c_ref[...])

    return pl.pallas_call(
        body, name="cctx_grad", out_shape=jax.ShapeDtypeStruct((8, 128), F32),
        in_specs=[pl.BlockSpec(memory_space=pltpu.VMEM)] * 2, out_specs=pl.BlockSpec(memory_space=pltpu.VMEM),
    )(dscc_parts, c_ctx)


def _adamw(w, g, m, v, name, after):
    r, cdim = w.shape
    tr = _tile(r, 256)
    c1 = 1.0 - ADAM_B1 ** ADAM_STEP
    c2 = 1.0 - ADAM_B2 ** ADAM_STEP

    def body(w_ref, g_ref, m_ref, v_ref, after_ref, d_ref, mo_ref, vo_ref):
        gv = g_ref[...]
        mn = ADAM_B1 * m_ref[...] + (1.0 - ADAM_B1) * gv
        vn = ADAM_B2 * v_ref[...] + (1.0 - ADAM_B2) * (gv * gv)
        mo_ref[...] = mn
        vo_ref[...] = vn
        d_ref[...] = -ADAM_LR * ((mn / c1) / (jnp.sqrt(vn / c2) + ADAM_EPS) + ADAM_WD * w_ref[...])

    spec = pl.BlockSpec((tr, cdim), lambda i: (i, 0))
    sds = jax.ShapeDtypeStruct((r, cdim), F32)
    return pl.pallas_call(
        body, name=name, grid=(r // tr,), in_specs=[spec] * 4 + [pl.BlockSpec(memory_space=pl.ANY)],
        out_specs=[spec] * 3, out_shape=[sds] * 3, compiler_params=_params(("parallel",)),
    )(w, g, m, v, after)


def _place():
    x, y, c = lax.axis_index("x"), lax.axis_index("y"), lax.axis_index("c")
    return x, y, c


def _allgather_small(blk, name):
    m_per, n = blk.shape

    def body(x_ref, out_ref, send_sems, recv_sems, local_sem):
        x, y, c = _place()
        me, sibling = (x, y, c), (x, y, 1 - c)
        chips = [(1 - x, y), (x, 1 - y), (1 - x, 1 - y)]

        def rows(px, py, pc):
            return out_ref.at[pl.ds((4 * px + 2 * py + pc) * m_per, m_per), :]

        def copy(k, block, to, src=None):
            return pltpu.make_async_remote_copy(
                src_ref=rows(*block) if src is None else src, dst_ref=rows(*block),
                send_sem=send_sems.at[k], recv_sem=recv_sems.at[k], device_id=to, device_id_type=MESH)

        mine = pltpu.make_async_copy(x_ref, rows(*me), local_sem)
        mine.start()
        first = [copy(0, me, sibling, src=x_ref)]
        first += [copy(1 + j, me, (*chip, c), src=x_ref) for j, chip in enumerate(chips)]
        for cp in first:
            cp.start()
        passed = [copy(4 + j, (*chip, c), sibling) for j, chip in enumerate(chips)]
        for j, chip in enumerate(chips):
            copy(1 + j, (*chip, c), me).wait_recv()
            passed[j].start()
        copy(0, sibling, me).wait_recv()
        for j, chip in enumerate(chips):
            copy(4 + j, (*chip, 1 - c), me).wait_recv()
        for cp in first + passed:
            cp.wait_send()
        mine.wait()

    return pl.pallas_call(
        body, name=name,
        out_shape=jax.ShapeDtypeStruct((N_DEV * m_per, n), blk.dtype),
        in_specs=[pl.BlockSpec(memory_space=pltpu.VMEM)],
        out_specs=pl.BlockSpec(memory_space=pltpu.VMEM),
        scratch_shapes=[pltpu.SemaphoreType.DMA((7,)), pltpu.SemaphoreType.DMA((7,)), pltpu.SemaphoreType.DMA],
    )(blk)


def _other_chips(x, y):
    return [(1 - x, y), (x, 1 - y), (1 - x, 1 - y)]


_HBM_SPEC = pl.BlockSpec(memory_space=pltpu.HBM)
_SEM_SPEC = pl.BlockSpec(memory_space=pltpu.SEMAPHORE)
_SPLIT_PARAMS = pltpu.CompilerParams(has_side_effects=pltpu.SideEffectType.DATAFLOW_SIDE_EFFECTING)


def _in_hbm(a):
    return pltpu.with_memory_space_constraint(a, pltpu.HBM)


def _ag_copies(own_ref, land_ref, send_sems, recv_sems):
    x, y, c = _place()
    chip = 2 * x + y
    hr = own_ref.shape[0] // 2

    def half(ch):
        return land_ref.at[ch, pl.ds(c * hr, hr), :]

    def copy(k, src, dst, to):
        return pltpu.make_async_remote_copy(src_ref=src, dst_ref=dst, send_sem=send_sems.at[k],
                                            recv_sem=recv_sems.at[k], device_id=to, device_id_type=MESH)

    sends, expects = [], []
    for j, (ox, oy) in enumerate(_other_chips(x, y)):
        sends.append(copy(j, own_ref.at[pl.ds(c * hr, hr), :], half(chip), (ox, oy, c)))
        expects.append(copy(j, half(2 * ox + oy), half(2 * ox + oy), (ox, oy, c)))
    own_slot = copy(3, own_ref, land_ref.at[chip], (x, y, 1 - c))
    return sends + [own_slot], expects + [own_slot]


def _sc_copies(p_ref, land_ref, send_sems, recv_sems):
    x, y, c = _place()
    chip = 2 * x + y
    sends, expects = [], []
    for j, (ox, oy) in enumerate(_other_chips(x, y)):
        och = 2 * ox + oy
        mk = lambda dst_slot: pltpu.make_async_remote_copy(
            src_ref=p_ref.at[och], dst_ref=land_ref.at[dst_slot], send_sem=send_sems.at[j],
            recv_sem=recv_sems.at[j], device_id=(ox, oy, c), device_id_type=MESH)
        sends.append(mk(chip))
        expects.append(mk(och))
    return sends, expects


def _pe_copies(g_ref, land_ref, send_sems, recv_sems):
    x, y, c = _place()
    hr = g_ref.shape[1] // 2
    cp = pltpu.make_async_remote_copy(
        src_ref=g_ref.at[:, pl.ds((1 - c) * hr, hr), :], dst_ref=land_ref, send_sem=send_sems.at[0],
        recv_sem=recv_sems.at[0], device_id=(x, y, 1 - c), device_id_type=MESH)
    return [cp], [cp]


def _split_start(src, land_shape, copies, n_copies, after, name):
    def body(src_ref, land_ref, after_ref, send_sems, recv_sems, src_thru, land_thru, token):
        for cp in copies(src_ref, land_ref, send_sems, recv_sems)[0]:
            cp.start()
        token[...] = jnp.zeros_like(token)

    land = lax.empty(land_shape, src.dtype)
    return pl.pallas_call(
        body, name=name,
        out_shape=(pltpu.SemaphoreType.DMA((n_copies,)), pltpu.SemaphoreType.DMA((n_copies,)),
                   pltpu.HBM(src.shape, src.dtype), pltpu.HBM(land_shape, src.dtype),
                   jax.ShapeDtypeStruct((8, 128), F32)),
        in_specs=(_HBM_SPEC, _HBM_SPEC, pl.BlockSpec(memory_space=pl.ANY)),
        out_specs=(_SEM_SPEC, _SEM_SPEC, _HBM_SPEC, _HBM_SPEC, pl.BlockSpec(memory_space=pltpu.VMEM)),
        input_output_aliases={0: 2, 1: 3}, compiler_params=_SPLIT_PARAMS,
    )(_in_hbm(src), _in_hbm(land), after)


def _split_wait(started, after, copies, name):
    send_sems, recv_sems, src_thru, land_thru, _ = started

    def body(src_ref, land_ref, send_sems, recv_sems, after_ref, src_dead, got_ref):
        sends, expects = copies(src_ref, land_ref, send_sems, recv_sems)
        for cp in sends:
            cp.wait_send()
        for cp in expects:
            cp.wait_recv()

    return pl.pallas_call(
        body, name=name,
        out_shape=(pltpu.HBM(src_thru.shape, src_thru.dtype), pltpu.HBM(land_thru.shape, land_thru.dtype)),
        in_specs=(_HBM_SPEC, _HBM_SPEC, _SEM_SPEC, _SEM_SPEC, pl.BlockSpec(memory_space=pl.ANY)),
        out_specs=(_HBM_SPEC, _HBM_SPEC), input_output_aliases={0: 0, 1: 1}, compiler_params=_SPLIT_PARAMS,
    )(src_thru, land_thru, send_sems, recv_sems, after)


def _ag_pass_on(land, name):
    hr = land.shape[1] // 2

    def body(in_ref, out_ref, send_sems, recv_sems):
        x, y, c = _place()

        def copy(j, ox, oy, cc):
            ref = out_ref.at[2 * ox + oy, pl.ds(cc * hr, hr), :]
            return pltpu.make_async_remote_copy(src_ref=ref, dst_ref=ref, send_sem=send_sems.at[j],
                                                recv_sem=recv_sems.at[j], device_id=(x, y, 1 - c),
                                                device_id_type=MESH)

        others = _other_chips(x, y)
        for j, (ox, oy) in enumerate(others):
            copy(j, ox, oy, c).start()
        for j, (ox, oy) in enumerate(others):
            copy(j, ox, oy, 1 - c).wait_recv()
        for j, (ox, oy) in enumerate(others):
            copy(j, ox, oy, c).wait_send()

    any_spec = pl.BlockSpec(memory_space=pl.ANY)
    return pl.pallas_call(
        body, name=name, out_shape=jax.ShapeDtypeStruct(land.shape, land.dtype),
        in_specs=[any_spec], out_specs=any_spec, input_output_aliases={0: 0},
        scratch_shapes=[pltpu.SemaphoreType.DMA((3,)), pltpu.SemaphoreType.DMA((3,))],
    )(land)


def _rs_pair_exchange(g, name):
    r = g.shape[1]
    hr = r // 2

    def body(g_ref, got_ref, send_sem, recv_sem):
        x, y, c = _place()
        cp = pltpu.make_async_remote_copy(
            src_ref=g_ref.at[:, pl.ds((1 - c) * hr, hr), :], dst_ref=got_ref, send_sem=send_sem, recv_sem=recv_sem,
            device_id=(x, y, 1 - c), device_id_type=MESH)
        cp.start()
        cp.wait()

    any_spec = pl.BlockSpec(memory_space=pl.ANY)
    return pl.pallas_call(
        body, name=name,
        out_shape=jax.ShapeDtypeStruct((N_CHIPS, hr, D), F32),
        in_specs=[any_spec], out_specs=any_spec,
        scratch_shapes=[pltpu.SemaphoreType.DMA, pltpu.SemaphoreType.DMA],
    )(g)


def _rs_chip_sum(place, g, got, name):
    r = g.shape[1]
    hr = r // 2
    tr = _tile(hr, 640, 16)
    nt = hr // tr

    def body(pl_ref, g_ref, got_ref, p16_ref, p32_ref):
        s = pl.program_id(1)
        p = g_ref[0] + got_ref[0]
        p16_ref[0] = p.astype(BF16)

        @pl.when(s == pl_ref[1])
        def _():
            p32_ref[...] = p

    return pl.pallas_call(
        body, name=name,
        grid_spec=pltpu.PrefetchScalarGridSpec(
            num_scalar_prefetch=1, grid=(nt, N_CHIPS),
            in_specs=[pl.BlockSpec((1, tr, D), lambda i, s, pr: (s, pr[0] * nt + i, 0)),
                      pl.BlockSpec((1, tr, D), lambda i, s, pr: (s, i, 0))],
            out_specs=[pl.BlockSpec((1, tr, D), lambda i, s, pr: (s, i, 0)),
                       pl.BlockSpec((tr, D), lambda i, s, pr: (i, 0))]),
        out_shape=[jax.ShapeDtypeStruct((N_CHIPS, hr, D), BF16), jax.ShapeDtypeStruct((hr, D), F32)],
        compiler_params=_params(("parallel", "arbitrary")),
    )(place, g, got)


def _rs_final_sum(place, parts, p32, name):
    hr = parts.shape[1]
    tr = _tile(hr, 640, 16)
    nt = hr // tr

    def body(pl_ref, a_ref, b_ref, c_ref, p32_ref, o_ref):
        o_ref[...] = ((p32_ref[...] + a_ref[0].astype(F32)) + b_ref[0].astype(F32)) + c_ref[0].astype(F32)

    def other(j):
        return pl.BlockSpec((1, tr, D), lambda i, pr: (j + jnp.where(pr[1] <= j, 1, 0), i, 0))

    return pl.pallas_call(
        body, name=name,
        grid_spec=pltpu.PrefetchScalarGridSpec(
            num_scalar_prefetch=1, grid=(nt,),
            in_specs=[other(0), other(1), other(2), pl.BlockSpec((tr, D), lambda i, pr: (i, 0))],
            out_specs=pl.BlockSpec((tr, D), lambda i, pr: (pr[0] * nt + i, 0))),
        out_shape=jax.ShapeDtypeStruct((2 * hr, D), F32),
        compiler_params=_params(("parallel",)),
    )(place, parts, parts, parts, p32)


def _rs_pair_gather(both, name):
    hr = both.shape[0] // 2

    def body(in_ref, out_ref, send_sem, recv_sem):
        x, y, c = _place()
        mine = out_ref.at[pl.ds(c * hr, hr), :]
        cp = pltpu.make_async_remote_copy(
            src_ref=mine, dst_ref=mine, send_sem=send_sem, recv_sem=recv_sem,
            device_id=(x, y, 1 - c), device_id_type=MESH)
        cp.start()
        theirs = out_ref.at[pl.ds((1 - c) * hr, hr), :]
        pltpu.make_async_remote_copy(
            src_ref=theirs, dst_ref=theirs, send_sem=send_sem, recv_sem=recv_sem,
            device_id=(x, y, 1 - c), device_id_type=MESH).wait_recv()
        cp.wait_send()

    any_spec = pl.BlockSpec(memory_space=pl.ANY)
    return pl.pallas_call(
        body, name=name,
        out_shape=jax.ShapeDtypeStruct(both.shape, F32),
        in_specs=[any_spec], out_specs=any_spec, input_output_aliases={0: 0},
        scratch_shapes=[pltpu.SemaphoreType.DMA, pltpu.SemaphoreType.DMA],
    )(both)


def _local_step(x, ctx, tgt, mods, mc, ag_gin, ag_main, place, small):
    nb, t, _ = x.shape
    tc = ctx.shape[1]
    n = nb * t
    nc = nb * tc
    xf = x.reshape(n, D)
    cf = ctx.reshape(nc, D)
    tf = tgt.reshape(n, D)
    vec = lambda a: a.reshape(1, -1)
    m = [[mods[l, :, k, :].reshape(nb, 1, D) for k in range(N_MOD)] for l in range(2)]
    mc_b = [jnp.broadcast_to(mc[k].reshape(1, 1, D), (nb, 1, D)) for k in range(2)]

    cw = [small["ffn_conv_w"][l] for l in range(2)]
    cb = [small["ffn_conv_b"][l].reshape(1, -1) for l in range(2)]
    w2 = jnp.zeros((128, 2 * KEY), F32)
    w2 = w2.at[0:RANK, 0:KEY].set(small["gla_w_a2"][0]).at[RANK:2 * RANK, KEY:].set(small["gla_w_a2"][1])
    b2 = small["gla_b_a"].reshape(1, 2 * KEY)
    hg = small["gla_head_norm"].reshape(1, DV)

    hn0 = _mod_fwd(xf, vec(small["norm_mix"][0]), m[0][0], m[0][1], t, "mod0_main")
    hnc = _mod_fwd(cf, vec(small["norm_mix"][0]), mc_b[0], mc_b[1], tc, "mod0_ctx")
    hn_all = jnp.concatenate([hn0, hnc], axis=0)
    gin = _ag_pass_on(_split_wait(ag_gin, hn_all, _ag_copies, "ag_gin_wait")[1], "ag_gin_pass_on")
    w_gin = jnp.pad(gin[:, :_GIN_ROWS, :].reshape(GLA_IN, D), ((0, GLA_IN_PAD - GLA_IN), (0, 0)))
    p_all = _mm(hn_all, w_gin, "nt", F32, "gla_in_proj", 768, 3200)
    la_all = _gla_decay_fwd(p_all, w2, b2)
    o2, ss = _gla_scan_fwd(p_all, la_all, nb, t, tc)
    wg = _ag_pass_on(_split_wait(ag_main, o2, _ag_copies, "ag_main_wait")[1], "ag_main_pass_on")
    offs = _offsets(_MAIN, _MAIN_ROWS)
    rows = _MAIN_ROWS

    def w_nt(a, k, name, tm=1024):
        return _mm_nt_w(a, wg, offs[k], rows[k], name, tm)

    def w_nn(a3, k, name, tm, tn):
        return _mm_nn_w(a3, wg, offs[k], rows[k], name, tm, tn)

    yb0 = _gla_post_fwd(o2, p_all, hg, n)
    y0 = w_nn(yb0[None], "gla_out", "gla_out_proj", 1024, 1024)
    h1, hn1 = _mod_fwd(xf, vec(small["norm_ffn"][0]), m[0][3], m[0][4], t, "mod0_ffn", y=y0, gate=m[0][2])
    u0 = w_nt(hn1, "up_t0", "ffn0_up")
    z0 = _ffn_mid_fwd(u0, cw[0], cb[0], nb, t, "ffn0_mid_fwd")
    f0 = w_nn(z0[None], "down0", "ffn0_down", 1024, 1024)
    h2, hn2 = _mod_fwd(h1, vec(small["norm_mix"][1]), m[1][0], m[1][1], t, "mod1_mix", y=f0, gate=m[0][5])
    p1 = w_nt(hn2, "sc_in_t", "sc_in_proj")
    yb1 = _sc_mid_fwd(p1, small["sc_conv_w"], nb, t)
    y1 = w_nn(yb1[None], "sc_out", "sc_out_proj", 1024, 1024)
    h3, hn3 = _mod_fwd(h2, vec(small["norm_ffn"][1]), m[1][3], m[1][4], t, "mod1_ffn", y=y1, gate=m[1][2])
    u1 = w_nt(hn3, "up_t1", "ffn1_up")
    z1 = _ffn_mid_fwd(u1, cw[1], cb[1], nb, t, "ffn1_mid_fwd")
    f1 = w_nn(z1[None], "down1", "ffn1_down", 1024, 1024)
    loss, dh4, df1, dm15, dfinal = _final(h3, f1, m[1][5], vec(small["final_norm"]), tf, t)

    gs = {}
    dmods = [[None] * N_MOD for _ in range(2)]
    dmods[1][5] = dm15

    def w_dw(a3, b, g_prev, k, name, tm):
        return _mm_dw(a3, b, g_prev, offs[k], rows[k], name, tm)

    def ffn_bwd(l, df, u, z, hn, g_prev):
        dz = w_nt(df, f"down{l}", f"ffn{l}_down_dx")
        g_acc = w_dw(z[None], df, g_prev, f"down{l}", f"ffn{l}_down_dw", 640)
        du, dcw, dcb = _ffn_mid_bwd(u, cw[l], cb[l], dz, nb, t, f"ffn{l}_mid_bwd")
        dhn = w_nn(du, f"up_t{l}", f"ffn{l}_up_dx", 512, 512)
        g_acc = w_dw(du, hn, g_acc, f"up_t{l}", f"ffn{l}_up_dw", 640)
        return dhn, g_acc, jnp.moveaxis(dcw, 0, 1).reshape(3, 2 * HID), dcb.reshape(2 * HID)

    dhn3, g_acc, dcw1, dcb1 = ffn_bwd(1, df1, u1, z1, hn3, None)
    r = _mod_bwd(h3, dhn3, vec(small["norm_ffn"][1]), m[1][4], t, "mod1_ffn_bwd", dh_out=dh4, y_prev=y1,
                 gate_prev=m[1][2])
    dh3, dmods[1][4], dmods[1][3], dnf1, dy1, dmods[1][2] = (r["dh"], r["dscale"], r["dshift"], r["dgain"],
                                                             r["dy_prev"], r["dgate_prev"])
    dyb1 = w_nt(dy1, "sc_out", "sc_out_dx")
    g_acc = w_dw(yb1[None], dy1, g_acc, "sc_out", "sc_out_dw", 256)
    dp1, dscw = _sc_mid_bwd(p1, small["sc_conv_w"], dyb1, nb, t)
    dhn2 = w_nn(dp1, "sc_in_t", "sc_in_dx", 1024, 512)
    g_acc = w_dw(dp1, hn2, g_acc, "sc_in_t", "sc_in_dw", 256)
    r = _mod_bwd(h2, dhn2, vec(small["norm_mix"][1]), m[1][1], t, "mod1_mix_bwd", dh_out=dh3, y_prev=f0,
                 gate_prev=m[0][5])
    dh2, dmods[1][1], dmods[1][0], dnm1, df0, dmods[0][5] = (r["dh"], r["dscale"], r["dshift"], r["dgain"],
                                                             r["dy_prev"], r["dgate_prev"])
    dhn1, g_acc, dcw0, dcb0 = ffn_bwd(0, df0, u0, z0, hn1, g_acc)
    r = _mod_bwd(h1, dhn1, vec(small["norm_ffn"][0]), m[0][4], t, "mod0_ffn_bwd", dh_out=dh2, y_prev=y0,
                 gate_prev=m[0][2])
    dh1, dmods[0][4], dmods[0][3], dnf0, dy0, dmods[0][2] = (r["dh"], r["dscale"], r["dshift"], r["dgain"],
                                                             r["dy_prev"], r["dgate_prev"])
    g_packed = w_dw(yb0[None], dy0, g_acc, "gla_out", "gla_out_dw", 256)
    pair = _split_start(g_packed, (N_CHIPS, _MAIN_TOTAL // 2, D), _pe_copies, 1, dy0, "rs_main_pair_start")
    dyb0 = w_nt(dy0, "gla_out", "gla_out_dx")
    do, dgate, dhg = _gla_post_bwd(o2, p_all, hg + pair[4][0:1, 0:1], dyb0, n)
    g_packed, from_sibling = _split_wait(pair, do, _pe_copies, "rs_main_pair_wait")
    p16, p32 = _rs_chip_sum(place, g_packed, from_sibling, "rs_main_chip_sum")
    sc_main = _split_start(p16, p16.shape, _sc_copies, 3, p32, "rs_main_scatter_start")
    dq, dk, dv, dla = _gla_scan_bwd(p_all, la_all, do, ss, nb, t, tc, sc_main[4])
    dp, dw2, db2 = _gla_assemble(p_all, w2, b2, dq, dk, dv, dla, dgate, n)
    dhn_all = _mm(dp, w_gin, "nn", F32, "gla_in_dx", 768, 512)
    landed = _split_wait(sc_main, dhn_all, _sc_copies, "rs_main_scatter_wait")[1]
    g_main = _rs_pair_gather(_rs_final_sum(place, landed, p32, "rs_main_final_sum"), "rs_main_pair_gather")
    g_gin = _mm(dp, hn_all, "tn", F32, "gla_in_dw", 640, 1024)[:GLA_IN]
    g_gin = jnp.pad(g_gin.reshape(N_CHIPS, _GIN_ROWS, D), ((0, 0), (0, _GIN_PAD - _GIN_ROWS), (0, 0)))
    from_sibling = _rs_pair_exchange(g_gin, "rs_gin_pair_exchange")
    p16_gin, p32_gin = _rs_chip_sum(place, g_gin, from_sibling, "rs_gin_chip_sum")
    r = _mod_bwd(xf, dhn_all, vec(small["norm_mix"][0]), m[0][1], t, "mod0_main_bwd", dh_out=dh1)
    grad_x, dmods[0][1], dmods[0][0], dnm0 = r["dh"], r["dscale"], r["dshift"], r["dgain"]
    rc = _mod_bwd(cf, dhn_all, vec(small["norm_mix"][0]), mc_b[1], tc, "mod0_ctx_bwd", dhn_row0=n, need_dh=False)
    dmc = jnp.stack([jnp.sum(rc["dshift"], axis=0).reshape(D), jnp.sum(rc["dscale"], axis=0).reshape(D)])
    dnm0 = dnm0 + rc["dgain"]

    gs["norm_mix"] = jnp.concatenate([dnm0, dnm1], axis=0)
    gs["norm_ffn"] = jnp.concatenate([dnf0, dnf1], axis=0)
    gs["final_norm"] = dfinal.reshape(D)
    gs["gla_w_a2"] = jnp.stack([dw2[0:RANK, 0:KEY], dw2[RANK:2 * RANK, KEY:]])
    gs["gla_b_a"] = db2.reshape(2, KEY)
    gs["gla_head_norm"] = dhg.reshape(DV)
    gs["sc_conv_w"] = dscw
    gs["ffn_conv_w"] = jnp.stack([dcw0, dcw1])
    gs["ffn_conv_b"] = jnp.stack([dcb0, dcb1])
    dmods_arr = jnp.stack([jnp.stack([dmods[l][k].reshape(nb, D) for k in range(N_MOD)], axis=1) for l in range(2)])
    return loss, grad_x.reshape(nb, t, D), g_main, p16_gin, p32_gin, gs, dmods_arr, dmc


def _pack(arrs):
    parts, meta, off = [], [], 0
    for a in arrs:
        r = a.size // 128
        rp = -(-r // 8) * 8
        a2 = a.reshape(r, 128).astype(F32)
        if rp != r:
            a2 = jnp.pad(a2, ((0, rp - r), (0, 0)))
        parts.append(a2)
        meta.append((off, r, a.shape))
        off += rp
    return jnp.concatenate(parts, axis=0), meta


def _unpack(buf, meta, lead=()):
    return [buf[..., off:off + r, :].reshape(*lead, *shape) for off, r, shape in meta]


_MAIN = ("up_t0", "up_t1", "down0", "down1", "sc_in_t", "gla_out", "sc_out")
_MAIN_ROWS = {"sc_in_t": 3 * D // N_CHIPS, "up_t0": 2 * HID // N_CHIPS, "up_t1": 2 * HID // N_CHIPS,
              "gla_out": D // N_CHIPS, "sc_out": D // N_CHIPS, "down0": HID // N_CHIPS, "down1": HID // N_CHIPS}
_MAIN_TOTAL = sum(_MAIN_ROWS.values())
_GIN_ROWS = GLA_IN // N_CHIPS
_GIN_PAD = -(-_GIN_ROWS // 32) * 32


def _offsets(names, rows):
    off, out = 0, {}
    for k in names:
        out[k] = off
        off += rows[k]
    return out


def kernel(x, c, ctx, c_ctx, ada_w, ada_b, norm_mix, norm_ffn, gla_w_in, gla_w_a2, gla_b_a, gla_head_norm, gla_w_out, sc_w_in, sc_conv_w, sc_w_out, ffn_w_up, ffn_conv_w, ffn_conv_b, ffn_w_down, final_norm, loss_target, m_c_ctx, m_ada_w, m_ada_b, m_norm_mix, m_norm_ffn, m_gla_w_in, m_gla_w_a2, m_gla_b_a, m_gla_head_norm, m_gla_w_out, m_sc_w_in, m_sc_conv_w, m_sc_w_out, m_ffn_w_up, m_ffn_conv_w, m_ffn_conv_b, m_ffn_w_down, m_final_norm, v_c_ctx, v_ada_w, v_ada_b, v_norm_mix, v_norm_ffn, v_gla_w_in, v_gla_w_a2, v_gla_b_a, v_gla_head_norm, v_gla_w_out, v_sc_w_in, v_sc_conv_w, v_sc_w_out, v_ffn_w_up, v_ffn_conv_w, v_ffn_conv_b, v_ffn_w_down, v_final_norm):
    ix, iy, ic = _place()
    chip = 2 * ix + iy
    dev = 2 * chip + ic
    place = jnp.stack([ic, chip]).astype(jnp.int32)
    nb = x.shape[0]
    offs = _offsets(_MAIN, _MAIN_ROWS)

    buf, meta = _pack([c, ffn_conv_w, sc_conv_w, gla_w_a2, gla_b_a])
    got = _allgather_small(buf, "gather_small_in").reshape(N_DEV, buf.shape[0], 128)
    c_all, fcw, scw, wa2, ba = _unpack(got, meta, (N_DEV,))
    c_all = c_all.reshape(N_DEV * nb, D)
    per_chip = lambda a: a[0::2]
    ffn_conv_w_full = jnp.moveaxis(per_chip(fcw), 0, 2).reshape(2, 3, 2 * HID)
    sc_conv_w_full = jnp.moveaxis(per_chip(scw)[:, 0], 0, 1).reshape(3, D)
    gla_w_a2_full = jnp.moveaxis(per_chip(wa2)[:, 0], 0, 2).reshape(2, RANK, KEY)
    gla_b_a_full = jnp.moveaxis(per_chip(ba)[:, 0], 0, 1).reshape(2, KEY)

    cvec = jnp.concatenate([c_all, c_ctx.reshape(1, D), jnp.zeros((ADA_ROWS - N_DEV * nb - 1, D), F32)], axis=0)
    ada_b_sh = lax.dynamic_slice_in_dim(ada_b, chip * ADA_SH, ADA_SH, axis=1).reshape(2, 1, ADA_SH)
    mod_sh = _ada_fwd(cvec, ada_w, ada_b_sh)
    got = _allgather_small(mod_sh.reshape(2 * ADA_ROWS, ADA_SH), "gather_mod")
    mod_full = jnp.moveaxis(per_chip(got.reshape(N_DEV, 2, ADA_ROWS, ADA_SH)), 0, 2).reshape(2, ADA_ROWS, N_MOD * D)
    mc = mod_full[0, N_DEV * nb, :2 * D].reshape(2, D)

    own = {"sc_in_t": sc_w_in[0].T, "up_t0": ffn_w_up[0].T, "up_t1": ffn_w_up[1].T,
           "gla_out": gla_w_out[0], "sc_out": sc_w_out[0], "down0": ffn_w_down[0], "down1": ffn_w_down[1]}
    own_main = jnp.concatenate([own[k].astype(BF16) for k in _MAIN], axis=0)
    own_gin = jnp.pad(gla_w_in[0].T.astype(BF16), ((0, _GIN_PAD - _GIN_ROWS), (0, 0)))
    ag_gin = _split_start(own_gin, (N_CHIPS, _GIN_PAD, D), _ag_copies, 4, mc, "ag_gin_start")
    ag_main = _split_start(own_main, (N_CHIPS, _MAIN_TOTAL, D), _ag_copies, 4, ag_gin[4], "ag_main_start")
    mods = lax.dynamic_slice_in_dim(mod_full, dev * nb, nb, axis=1).reshape(2, nb, N_MOD, D) + ag_main[4][0, 0]

    small = {"norm_mix": norm_mix, "norm_ffn": norm_ffn, "final_norm": final_norm, "gla_w_a2": gla_w_a2_full,
             "gla_b_a": gla_b_a_full, "gla_head_norm": gla_head_norm[0], "sc_conv_w": sc_conv_w_full,
             "ffn_conv_w": ffn_conv_w_full, "ffn_conv_b": ffn_conv_b}
    loss_p, grad_x, g_main, p16_gin, p32_gin, gs, dmods, dmc = _local_step(x, ctx, loss_target, mods, mc, ag_gin,
                                                                           ag_main, place, small)

    sum_names = ["norm_mix", "norm_ffn", "final_norm", "gla_w_a2", "gla_b_a", "gla_head_norm", "sc_conv_w",
                 "ffn_conv_w", "ffn_conv_b"]
    buf, meta = _pack([jnp.broadcast_to(loss_p, (8, 128))] + [gs[k] for k in sum_names] + [dmc, dmods])
    n_sum = meta[-1][0]
    got = _allgather_small(buf, "gather_small_grads").reshape(N_DEV, buf.shape[0], 128)
    summed = _sum_slots(got[:, :n_sum], "sum_small_grads")
    parts = _unpack(summed, meta[:-1])
    loss = parts[0][0, 0]
    g_small = dict(zip(sum_names, parts[1:-1]))
    dmc_tot = parts[-1]
    dmods_all = jnp.moveaxis(_unpack(got, meta[-1:], (N_DEV,))[0], 0, 1).reshape(2, N_DEV * nb, N_MOD * D)

    ctx_row = jnp.stack([jnp.concatenate([dmc_tot.reshape(2 * D), jnp.zeros(((N_MOD - 2) * D,), F32)]),
                         jnp.zeros((N_MOD * D,), F32)]).reshape(2, 1, N_MOD * D)
    dmod_ext = jnp.concatenate([dmods_all, ctx_row, jnp.zeros((2, ADA_ROWS - N_DEV * nb - 1, N_MOD * D), F32)], axis=1)
    g_ada_b = _sum_slots(jnp.moveaxis(dmod_ext, 1, 0).reshape(ADA_ROWS, 2 * N_MOD * D // 128, 128),
                         "sum_ada_b").reshape(2, N_MOD * D)
    dmod_sh = lax.dynamic_slice_in_dim(dmod_ext, chip * ADA_SH, ADA_SH, axis=2)
    g_ada_w, dcv = _ada_bwd(cvec, ada_w, dmod_sh)
    dscc_part = (dcv[0, N_DEV * nb] + dcv[1, N_DEV * nb]).reshape(8, 128)
    got = _allgather_small(dscc_part, "gather_dscc").reshape(N_DEV, 8, 128)
    g_c_ctx = _cctx_grad(per_chip(got), c_ctx.reshape(8, 128)).reshape(D)

    sc_gin = _split_start(p16_gin, p16_gin.shape, _sc_copies, 3, g_c_ctx, "rs_gin_scatter_start")
    seg = {k: g_main[offs[k]:offs[k] + _MAIN_ROWS[k]] for k in _MAIN}

    sl_chip = lambda a, axis, width: lax.dynamic_slice_in_dim(a, chip * width, width, axis=axis)
    grads = {
        "c_ctx": g_c_ctx, "ada_w": g_ada_w, "ada_b": g_ada_b, "norm_mix": g_small["norm_mix"],
        "norm_ffn": g_small["norm_ffn"],
        "gla_w_a2": sl_chip(g_small["gla_w_a2"], 2, KEY // N_CHIPS)[None],
        "gla_b_a": sl_chip(g_small["gla_b_a"], 1, KEY // N_CHIPS)[None],
        "gla_head_norm": g_small["gla_head_norm"][None], "gla_w_out": seg["gla_out"][None],
        "sc_w_in": seg["sc_in_t"].T[None], "sc_conv_w": sl_chip(g_small["sc_conv_w"], 1, D // N_CHIPS)[None],
        "sc_w_out": seg["sc_out"][None], "ffn_w_up": jnp.stack([seg["up_t0"].T, seg["up_t1"].T]),
        "ffn_conv_w": sl_chip(g_small["ffn_conv_w"], 2, 2 * HID // N_CHIPS), "ffn_conv_b": g_small["ffn_conv_b"],
        "ffn_w_down": jnp.stack([seg["down0"], seg["down1"]]), "final_norm": g_small["final_norm"],
    }
    weights = {"c_ctx": c_ctx, "ada_w": ada_w, "ada_b": ada_b, "norm_mix": norm_mix, "norm_ffn": norm_ffn,
               "gla_w_in": gla_w_in, "gla_w_a2": gla_w_a2, "gla_b_a": gla_b_a, "gla_head_norm": gla_head_norm,
               "gla_w_out": gla_w_out, "sc_w_in": sc_w_in, "sc_conv_w": sc_conv_w, "sc_w_out": sc_w_out,
               "ffn_w_up": ffn_w_up, "ffn_conv_w": ffn_conv_w, "ffn_conv_b": ffn_conv_b, "ffn_w_down": ffn_w_down,
               "final_norm": final_norm}
    mom1 = {"c_ctx": m_c_ctx, "ada_w": m_ada_w, "ada_b": m_ada_b, "norm_mix": m_norm_mix, "norm_ffn": m_norm_ffn,
            "gla_w_in": m_gla_w_in, "gla_w_a2": m_gla_w_a2, "gla_b_a": m_gla_b_a, "gla_head_norm": m_gla_head_norm,
            "gla_w_out": m_gla_w_out, "sc_w_in": m_sc_w_in, "sc_conv_w": m_sc_conv_w, "sc_w_out": m_sc_w_out,
            "ffn_w_up": m_ffn_w_up, "ffn_conv_w": m_ffn_conv_w, "ffn_conv_b": m_ffn_conv_b,
            "ffn_w_down": m_ffn_w_down, "final_norm": m_final_norm}
    mom2 = {"c_ctx": v_c_ctx, "ada_w": v_ada_w, "ada_b": v_ada_b, "norm_mix": v_norm_mix, "norm_ffn": v_norm_ffn,
            "gla_w_in": v_gla_w_in, "gla_w_a2": v_gla_w_a2, "gla_b_a": v_gla_b_a, "gla_head_norm": v_gla_head_norm,
            "gla_w_out": v_gla_w_out, "sc_w_in": v_sc_w_in, "sc_conv_w": v_sc_conv_w, "sc_w_out": v_sc_w_out,
            "ffn_w_up": v_ffn_w_up, "ffn_conv_w": v_ffn_conv_w, "ffn_conv_b": v_ffn_conv_b,
            "ffn_w_down": v_ffn_w_down, "final_norm": v_final_norm}
    names = list(weights)

    big_names = ["ada_w", "gla_w_out", "sc_w_in", "sc_w_out", "ffn_w_up", "ffn_w_down", "gla_w_in"]
    small_names = [k for k in names if k not in big_names]
    delta, new_m, new_v = {}, {}, {}
    done = []

    def big_adamw(k, token):
        shp = weights[k].shape
        as2d = lambda a: a.reshape(-1, shp[-1])
        d_, m_, v_ = _adamw(as2d(weights[k]), as2d(grads[k]), as2d(mom1[k]), as2d(mom2[k]), "adamw_" + k, token)
        done.append(v_[0:1, 0:128])
        delta[k], new_m[k], new_v[k] = d_.reshape(shp), m_.reshape(shp), v_.reshape(shp)

    for k in big_names[:-1]:
        grads[k] = grads[k].reshape(weights[k].shape)
        big_adamw(k, sc_gin[4])
    for k in small_names:
        grads[k] = grads[k].reshape(weights[k].shape)
    packed = [_pack([src[k] for k in small_names]) for src in (weights, grads, mom1, mom2)]
    meta = packed[0][1]
    rows_pad = -packed[0][0].shape[0] % 128
    bufs = [jnp.pad(p[0], ((0, rows_pad), (0, 0))) for p in packed]
    outs = _adamw(bufs[0], bufs[1], bufs[2], bufs[3], "adamw_small", sc_gin[4])
    done.append(outs[2][0:1, :])
    for dst, o in zip((delta, new_m, new_v), outs):
        for k, a in zip(small_names, _unpack(o, meta)):
            dst[k] = a
    landed = _split_wait(sc_gin, jnp.concatenate(done, axis=0), _sc_copies, "rs_gin_scatter_wait")[1]
    g_gin_shard = _rs_pair_gather(_rs_final_sum(place, landed, p32_gin, "rs_gin_final_sum"), "rs_gin_pair_gather")
    grads["gla_w_in"] = g_gin_shard[:_GIN_ROWS].T[None]
    big_adamw("gla_w_in", sc_gin[4])

    return (loss, grad_x, *[grads[k] for k in names], *[delta[k] for k in names], *[new_m[k] for k in names],
            *[new_v[k] for k in names])
```

```python
import functools

import jax
import jax.numpy as jnp
from jax import lax
from jax.experimental import pallas as pl
from jax.experimental.pallas import tpu as pltpu

F32 = jnp.float32
BF16 = jnp.bfloat16
MESH = pl.DeviceIdType.MESH

EPS = 1e-6
D = 1024
N_MOD = 6
HEADS = 4
DK = 128
DV = 256
KEY = HEADS * DK
RANK = 16
TAU = 16.0
CH = 64
GRID_W = 64
HID = 2560
GLA_IN = 2 * KEY + 2 * D + 2 * RANK
GLA_IN_PAD = 3200
Q_SCALE = DK ** -0.5
N_CHIPS = 4
N_DEV = 8

ADAM_LR = 0.001
ADAM_B1 = 0.9
ADAM_B2 = 0.999
ADAM_EPS = 1e-08
ADAM_WD = 0.01
ADAM_STEP = 10

VMEM_LIMIT = 56 * 1024 * 1024


def _params(sem):
    return pltpu.CompilerParams(dimension_semantics=sem, vmem_limit_bytes=VMEM_LIMIT)


def _tile(n, pref, mult=8):
    if n <= pref:
        return n
    for t in range(pref - pref % mult, 0, -mult):
        if n % t == 0:
            return t
    raise ValueError((n, pref, mult))


_NN = (((1,), (0,)), ((), ()))
_NT = (((1,), (1,)), ((), ()))
_TN = (((0,), (0,)), ((), ()))


def _dot(a, b, dims=_NN):
    return lax.dot_general(a.astype(BF16), b.astype(BF16), dims, preferred_element_type=F32)


def _sigmoid(x):
    return 1.0 / (1.0 + jnp.exp(-x))


def _rowsum(x):
    return jnp.sum(x, axis=0, keepdims=True)


def _mm(a, b, form, out_dtype, name, tm, tn):
    if form == "tn":
        K, M = a.shape
    else:
        M, K = a.shape
    N = b.shape[0] if form == "nt" else b.shape[1]
    tm = _tile(M, tm, 128)
    tn = _tile(N, tn, 128)
    dims = {"nn": _NN, "nt": _NT, "tn": _TN}[form]

    def body(a_ref, b_ref, o_ref):
        o_ref[...] = _dot(a_ref[...], b_ref[...], dims).astype(o_ref.dtype)

    if form == "tn":
        a_spec = pl.BlockSpec((K, tm), lambda i, j: (0, i))
    else:
        a_spec = pl.BlockSpec((tm, K), lambda i, j: (i, 0))
    if form == "nt":
        b_spec = pl.BlockSpec((tn, K), lambda i, j: (j, 0))
    else:
        b_spec = pl.BlockSpec((K, tn), lambda i, j: (0, j))
    return pl.pallas_call(
        body,
        name=name,
        grid=(M // tm, N // tn),
        in_specs=[a_spec, b_spec],
        out_specs=pl.BlockSpec((tm, tn), lambda i, j: (i, j)),
        out_shape=jax.ShapeDtypeStruct((M, N), out_dtype),
        compiler_params=_params(("parallel", "parallel")),
    )(a, b)


def _mm_nt_w(a, wg, off, rows, name, tm):
    m = a.shape[0]
    tm = _tile(m, tm, 128)
    if N_CHIPS * rows <= D:

        def body_small(a_ref, w_ref, o_ref):
            av = a_ref[...]
            for s in range(N_CHIPS):
                o_ref[:, s * rows:(s + 1) * rows] = _dot(av, w_ref[s], _NT)

        return pl.pallas_call(
            body_small, name=name, grid=(m // tm,),
            in_specs=[pl.BlockSpec((tm, D), lambda i: (i, 0)),
                      pl.BlockSpec((N_CHIPS, rows, D), lambda i: (0, off // rows, 0))],
            out_specs=pl.BlockSpec((tm, N_CHIPS * rows), lambda i: (i, 0)),
            out_shape=jax.ShapeDtypeStruct((m, N_CHIPS * rows), F32),
            compiler_params=_params(("parallel",)),
        )(a, wg)

    def body(a_ref, w_ref, o_ref):
        o_ref[...] = _dot(a_ref[...], w_ref[0], _NT)

    return pl.pallas_call(
        body, name=name, grid=(m // tm, N_CHIPS),
        in_specs=[pl.BlockSpec((tm, D), lambda i, s: (i, 0)),
                  pl.BlockSpec((1, rows, D), lambda i, s: (s, off // rows, 0))],
        out_specs=pl.BlockSpec((tm, rows), lambda i, s: (i, s)),
        out_shape=jax.ShapeDtypeStruct((m, N_CHIPS * rows), F32),
        compiler_params=_params(("parallel", "parallel")),
    )(a, wg)


def _mm_nn_w(a3, wg, off, rows, name, tm, tn):
    parts, m, kp = a3.shape
    assert parts * kp == N_CHIPS * rows
    tm = _tile(m, tm, 128)
    cuts = sorted({s * rows for s in range(N_CHIPS + 1)} | {p * kp for p in range(parts + 1)})
    pieces = [(k0 // kp, k0 % kp, k0 // rows, k0 % rows, k1 - k0) for k0, k1 in zip(cuts[:-1], cuts[1:])]

    def body(a_ref, w_ref, o_ref):
        acc = None
        for p, a0, s, r0, width in pieces:
            term = _dot(a_ref[p, :, a0:a0 + width], w_ref[s, r0:r0 + width, :])
            acc = term if acc is None else acc + term
        o_ref[...] = acc

    return pl.pallas_call(
        body, name=name, grid=(m // tm, D // tn),
        in_specs=[pl.BlockSpec((parts, tm, kp), lambda i, j: (0, i, 0)),
                  pl.BlockSpec((N_CHIPS, rows, tn), lambda i, j: (0, off // rows, j))],
        out_specs=pl.BlockSpec((tm, tn), lambda i, j: (i, j)),
        out_shape=jax.ShapeDtypeStruct((m, D), F32),
        compiler_params=_params(("parallel", "parallel")),
    )(a3, wg)


def _mm_dw(a3, b, g_prev, off, rows, name, tm):
    parts, ntok, cdim = a3.shape
    assert parts * cdim == N_CHIPS * rows and cdim % tm == 0 and rows % tm == 0 and off % tm == 0

    def body(a_ref, b_ref, *rest):
        rest[-1][0] = _dot(a_ref[0], b_ref[...], _TN)

    in_specs = [pl.BlockSpec((1, ntok, tm), lambda i: ((i * tm) // cdim, 0, ((i * tm) % cdim) // tm)),
                pl.BlockSpec((ntok, D), lambda i: (0, 0))]
    args = [a3, b]
    aliases = {}
    if g_prev is not None:
        in_specs.append(pl.BlockSpec(memory_space=pl.ANY))
        args.append(g_prev)
        aliases = {2: 0}
    return pl.pallas_call(
        body, name=name, grid=(N_CHIPS * rows // tm,),
        in_specs=in_specs,
        out_specs=pl.BlockSpec((1, tm, D), lambda i: ((i * tm) // rows, (off + (i * tm) % rows) // tm, 0)),
        out_shape=jax.ShapeDtypeStruct((N_CHIPS, _MAIN_TOTAL, D), F32),
        input_output_aliases=aliases,
        compiler_params=_params(("parallel",)),
    )(*args)


def _mod_fwd(h, gain, shift, scale, tpb_rows, name, y=None, gate=None):
    n = h.shape[0]
    tt = _tile(tpb_rows, 256)
    tpb = tpb_rows // tt
    has_res = y is not None

    def body(*refs):
        if has_res:
            h_ref, y_ref, gate_ref, gain_ref, sh_ref, sc_ref, hout_ref, hn_ref = refs
            hv = h_ref[...] + gate_ref[0] * y_ref[...]
            hout_ref[...] = hv
        else:
            h_ref, gain_ref, sh_ref, sc_ref, hn_ref = refs
            hv = h_ref[...]
        r = lax.rsqrt(jnp.mean(hv * hv, axis=-1, keepdims=True) + EPS)
        hn = (hv * r) * gain_ref[...] * (1.0 + sc_ref[0]) + sh_ref[0]
        hn_ref[...] = hn.astype(BF16)

    row = pl.BlockSpec((tt, D), lambda i: (i, 0))
    per_b = pl.BlockSpec((1, 1, D), lambda i: (i // tpb, 0, 0))
    vec = pl.BlockSpec((1, D), lambda i: (0, 0))
    if has_res:
        in_specs = [row, row, per_b, vec, per_b, per_b]
        args = (h, y, gate, gain, shift, scale)
        out_specs = [row, row]
        out_shape = [jax.ShapeDtypeStruct((n, D), F32), jax.ShapeDtypeStruct((n, D), BF16)]
    else:
        in_specs = [row, vec, per_b, per_b]
        args = (h, gain, shift, scale)
        out_specs = row
        out_shape = jax.ShapeDtypeStruct((n, D), BF16)
    return pl.pallas_call(
        body, name=name, grid=(n // tt,), in_specs=in_specs, out_specs=out_specs, out_shape=out_shape,
        compiler_params=_params(("parallel",)),
    )(*args)


def _mod_bwd(h_in, dhn, gain, scale, tpb_rows, name, dhn_row0=0, dh_out=None, y_prev=None, gate_prev=None,
             need_dh=True):
    n = h_in.shape[0]
    nb = n // tpb_rows
    tt = _tile(tpb_rows, 256)
    tpb = tpb_rows // tt
    off = dhn_row0 // tt
    assert dhn_row0 % tt == 0
    has_out = dh_out is not None
    has_prev = y_prev is not None

    def body(*refs):
        it = iter(refs)
        h_ref, dhn_ref, gain_ref, sc_ref = next(it), next(it), next(it), next(it)
        dho_ref = next(it) if has_out else None
        yp_ref, gp_ref = (next(it), next(it)) if has_prev else (None, None)
        dh_ref = next(it) if need_dh else None
        dsc_ref, dsh_ref, dgain_ref = next(it), next(it), next(it)
        dyp_ref, dgp_ref = (next(it), next(it)) if has_prev else (None, None)
        i = pl.program_id(0)

        @pl.when(i == 0)
        def _():
            dgain_ref[...] = jnp.zeros_like(dgain_ref)

        @pl.when(i % tpb == 0)
        def _():
            dsc_ref[...] = jnp.zeros_like(dsc_ref)
            dsh_ref[...] = jnp.zeros_like(dsh_ref)
            if has_prev:
                dgp_ref[...] = jnp.zeros_like(dgp_ref)

        hv = h_ref[...]
        r = lax.rsqrt(jnp.mean(hv * hv, axis=-1, keepdims=True) + EPS)
        y = hv * r
        gain_v = gain_ref[...]
        g = dhn_ref[...].astype(F32)
        dsh_ref[0] += _rowsum(g)
        dsc_ref[0] += _rowsum(g * (y * gain_v))
        drn = g * (1.0 + sc_ref[0])
        dgain_ref[...] += _rowsum(drn * y)
        if need_dh:
            dy = drn * gain_v
            dh = r * (dy - y * jnp.mean(dy * y, axis=-1, keepdims=True))
            if has_out:
                dh = dh + dho_ref[...]
            dh_ref[...] = dh
            if has_prev:
                dyp_ref[...] = (dh * gp_ref[0]).astype(BF16)
                dgp_ref[0] += _rowsum(dh * yp_ref[...])

    row = pl.BlockSpec((tt, D), lambda i: (i, 0))
    row_off = pl.BlockSpec((tt, D), lambda i: (i + off, 0))
    per_b = pl.BlockSpec((1, 1, D), lambda i: (i // tpb, 0, 0))
    vec = pl.BlockSpec((1, D), lambda i: (0, 0))
    in_specs = [row, row_off, vec, per_b]
    args = [h_in, dhn, gain, scale]
    if has_out:
        in_specs.append(row)
        args.append(dh_out)
    if has_prev:
        in_specs += [row, per_b]
        args += [y_prev, gate_prev]
    out_specs, out_shape, names = [], [], []
    if need_dh:
        out_specs.append(row)
        out_shape.append(jax.ShapeDtypeStruct((n, D), F32))
        names.append("dh")
    for nm in ("dscale", "dshift"):
        out_specs.append(per_b)
        out_shape.append(jax.ShapeDtypeStruct((nb, 1, D), F32))
        names.append(nm)
    out_specs.append(vec)
    out_shape.append(jax.ShapeDtypeStruct((1, D), F32))
    names.append("dgain")
    if has_prev:
        out_specs += [row, per_b]
        out_shape += [jax.ShapeDtypeStruct((n, D), BF16), jax.ShapeDtypeStruct((nb, 1, D), F32)]
        names += ["dy_prev", "dgate_prev"]
    outs = pl.pallas_call(
        body, name=name, grid=(n // tt,), in_specs=in_specs, out_specs=out_specs, out_shape=out_shape,
        compiler_params=_params(("arbitrary",)),
    )(*args)
    return dict(zip(names, outs))


def _final(h, f, gate, gain, tgt, tpb_rows):
    n = h.shape[0]
    nb = n // tpb_rows
    tt = _tile(tpb_rows, 256)
    tpb = tpb_rows // tt

    def body(h_ref, f_ref, gate_ref, gain_ref, tgt_ref, loss_ref, dh_ref, df_ref, dgate_ref, dgain_ref):
        i = pl.program_id(0)

        @pl.when(i == 0)
        def _():
            loss_ref[...] = jnp.zeros_like(loss_ref)
            dgain_ref[...] = jnp.zeros_like(dgain_ref)

        @pl.when(i % tpb == 0)
        def _():
            dgate_ref[...] = jnp.zeros_like(dgate_ref)

        fv = f_ref[...]
        gate_v = gate_ref[0]
        hv = h_ref[...] + gate_v * fv
        r = lax.rsqrt(jnp.mean(hv * hv, axis=-1, keepdims=True) + EPS)
        y = hv * r
        gain_v = gain_ref[...]
        e = y * gain_v - tgt_ref[...]
        s = jnp.sum(_rowsum(e * e), axis=1, keepdims=True) * (0.5 / D)
        loss_ref[...] += jnp.broadcast_to(s, loss_ref.shape)
        dout = e * (1.0 / D)
        dgain_ref[...] += _rowsum(dout * y)
        dy = dout * gain_v
        dh = r * (dy - y * jnp.mean(dy * y, axis=-1, keepdims=True))
        dh_ref[...] = dh
        df_ref[...] = (dh * gate_v).astype(BF16)
        dgate_ref[0] += _rowsum(dh * fv)

    row = pl.BlockSpec((tt, D), lambda i: (i, 0))
    per_b = pl.BlockSpec((1, 1, D), lambda i: (i // tpb, 0, 0))
    vec = pl.BlockSpec((1, D), lambda i: (0, 0))
    return pl.pallas_call(
        body, name="final_loss", grid=(n // tt,),
        in_specs=[row, row, per_b, vec, row],
        out_specs=[pl.BlockSpec((1, 128), lambda i: (0, 0)), row, row, per_b, vec],
        out_shape=[jax.ShapeDtypeStruct((1, 128), F32), jax.ShapeDtypeStruct((n, D), F32),
                   jax.ShapeDtypeStruct((n, D), BF16), jax.ShapeDtypeStruct((nb, 1, D), F32),
                   jax.ShapeDtypeStruct((1, D), F32)],
        compiler_params=_params(("arbitrary",)),
    )(h, f, gate, gain, tgt)


def _shift_dn(x, s):
    return jnp.concatenate([jnp.zeros((s, x.shape[1]), x.dtype), x[: x.shape[0] - s]], axis=0)


def _shift_up(x, s):
    return jnp.concatenate([x[s:], jnp.zeros((s, x.shape[1]), x.dtype)], axis=0)


def _row_dn1(x):
    t = lax.broadcasted_iota(jnp.int32, x.shape, 0)
    return jnp.where(t % GRID_W == 0, 0.0, pltpu.roll(x, 1, 0))


def _row_up1(x):
    t = lax.broadcasted_iota(jnp.int32, x.shape, 0)
    return jnp.where(t % GRID_W == GRID_W - 1, 0.0, pltpu.roll(x, x.shape[0] - 1, 0))


def _silu(x):
    return x * _sigmoid(x)


def _dsilu(x):
    s = _sigmoid(x)
    return s * (1.0 + x * (1.0 - s))


def _conv_cols(x, w_ref):
    return _shift_dn(x, GRID_W) * w_ref[0:1, :] + x * w_ref[1:2, :] + _shift_up(x, GRID_W) * w_ref[2:3, :]


def _conv_cols_bwd(x, du, w_ref, dw_ref, db_ref):
    db_ref[...] += _rowsum(du)
    dw_ref[0:1, :] += _rowsum(du * _shift_dn(x, GRID_W))
    dw_ref[1:2, :] += _rowsum(du * x)
    dw_ref[2:3, :] += _rowsum(du * _shift_up(x, GRID_W))
    return _shift_up(du, GRID_W) * w_ref[0:1, :] + du * w_ref[1:2, :] + _shift_dn(du, GRID_W) * w_ref[2:3, :]


def _ffn_mid_fwd(u0, cw, cb, nb, t, name):
    nc = HID // 128

    def body(ua_ref, ug_ref, wa_ref, wg_ref, ba_ref, bg_ref, z_ref):
        a = _conv_cols(ua_ref[...], wa_ref) + ba_ref[...]
        gt = _conv_cols(ug_ref[...], wg_ref) + bg_ref[...]
        z_ref[...] = (a * _silu(gt)).astype(BF16)

    col = lambda rows, part: pl.BlockSpec((rows, 128), lambda j, b: (b if rows == t else 0, part * nc + j))
    return pl.pallas_call(
        body, name=name, grid=(nc, nb),
        in_specs=[col(t, 0), col(t, 1), col(3, 0), col(3, 1), col(1, 0), col(1, 1)],
        out_specs=pl.BlockSpec((t, 128), lambda j, b: (b, j)),
        out_shape=jax.ShapeDtypeStruct((nb * t, HID), BF16),
        compiler_params=_params(("parallel", "parallel")),
    )(u0, u0, cw, cw, cb, cb)


def _ffn_mid_bwd(u0, cw, cb, dz, nb, t, name):
    nc = HID // 128

    def body(ua_ref, ug_ref, wa_ref, wg_ref, ba_ref, bg_ref, dz_ref, du_ref, dw_ref, db_ref):
        b = pl.program_id(1)

        @pl.when(b == 0)
        def _():
            dw_ref[...] = jnp.zeros_like(dw_ref)
            db_ref[...] = jnp.zeros_like(db_ref)

        xa = ua_ref[...]
        xg = ug_ref[...]
        a = _conv_cols(xa, wa_ref) + ba_ref[...]
        gt = _conv_cols(xg, wg_ref) + bg_ref[...]
        dzv = dz_ref[...]
        du_ref[0] = _conv_cols_bwd(xa, dzv * _silu(gt), wa_ref, dw_ref.at[0], db_ref.at[0]).astype(BF16)
        du_ref[1] = _conv_cols_bwd(xg, dzv * a * _dsilu(gt), wg_ref, dw_ref.at[1], db_ref.at[1]).astype(BF16)

    col = lambda rows, part: pl.BlockSpec((rows, 128), lambda j, b: (b if rows == t else 0, part * nc + j))
    return pl.pallas_call(
        body, name=name, grid=(nc, nb),
        in_specs=[col(t, 0), col(t, 1), col(3, 0), col(3, 1), col(1, 0), col(1, 1),
                  pl.BlockSpec((t, 128), lambda j, b: (b, j))],
        out_specs=[pl.BlockSpec((2, t, 128), lambda j, b: (0, b, j)), pl.BlockSpec((2, 3, 128), lambda j, b: (0, 0, j)),
                   pl.BlockSpec((2, 1, 128), lambda j, b: (0, 0, j))],
        out_shape=[jax.ShapeDtypeStruct((2, nb * t, HID), BF16), jax.ShapeDtypeStruct((2, 3, HID), F32),
                   jax.ShapeDtypeStruct((2, 1, HID), F32)],
        compiler_params=_params(("parallel", "arbitrary")),
    )(u0, u0, cw, cw, cb, cb, dz)


def _sc_mid_fwd(p, cw, nb, t):
    nc = D // 128

    def body(bg_ref, cg_ref, v_ref, w_ref, y_ref):
        cv = cg_ref[...] * v_ref[...]
        cc = _row_dn1(cv) * w_ref[0:1, :] + cv * w_ref[1:2, :] + _row_up1(cv) * w_ref[2:3, :]
        y_ref[...] = (bg_ref[...] * cc).astype(BF16)

    part = lambda k: pl.BlockSpec((t, 128), lambda j, b: (b, k * nc + j))
    return pl.pallas_call(
        body, name="sc_mid_fwd", grid=(nc, nb),
        in_specs=[part(0), part(1), part(2), pl.BlockSpec((3, 128), lambda j, b: (0, j))],
        out_specs=pl.BlockSpec((t, 128), lambda j, b: (b, j)),
        out_shape=jax.ShapeDtypeStruct((nb * t, D), BF16),
        compiler_params=_params(("parallel", "parallel")),
    )(p, p, p, cw)


def _sc_mid_bwd(p, cw, dyb, nb, t):
    nc = D // 128

    def body(bg_ref, cg_ref, v_ref, w_ref, dy_ref, dp_ref, dw_ref):
        b = pl.program_id(1)

        @pl.when(b == 0)
        def _():
            dw_ref[...] = jnp.zeros_like(dw_ref)

        w0, w1, w2 = w_ref[0:1, :], w_ref[1:2, :], w_ref[2:3, :]
        cg, v = cg_ref[...], v_ref[...]
        cv = cg * v
        cvd = _row_dn1(cv)
        cvu = _row_up1(cv)
        cc = cvd * w0 + cv * w1 + cvu * w2
        dy = dy_ref[...]
        dcc = dy * bg_ref[...]
        dw_ref[0:1, :] += _rowsum(dcc * cvd)
        dw_ref[1:2, :] += _rowsum(dcc * cv)
        dw_ref[2:3, :] += _rowsum(dcc * cvu)
        dcv = _row_up1(dcc) * w0 + dcc * w1 + _row_dn1(dcc) * w2
        dp_ref[0] = (dy * cc).astype(BF16)
        dp_ref[1] = (dcv * v).astype(BF16)
        dp_ref[2] = (dcv * cg).astype(BF16)

    part = lambda k: pl.BlockSpec((t, 128), lambda j, b: (b, k * nc + j))
    return pl.pallas_call(
        body, name="sc_mid_bwd", grid=(nc, nb),
        in_specs=[part(0), part(1), part(2), pl.BlockSpec((3, 128), lambda j, b: (0, j)),
                  pl.BlockSpec((t, 128), lambda j, b: (b, j))],
        out_specs=[pl.BlockSpec((3, t, 128), lambda j, b: (0, b, j)), pl.BlockSpec((3, 128), lambda j, b: (0, j))],
        out_shape=[jax.ShapeDtypeStruct((3, nb * t, D), BF16), jax.ShapeDtypeStruct((3, D), F32)],
        compiler_params=_params(("parallel", "arbitrary")),
    )(p, p, p, cw, dyb)


def _gla_decay_fwd(p_all, w2, b2):
    n = p_all.shape[0]
    tt = _tile(n, 512)

    def body(a_ref, w_ref, b_ref, la_ref):
        z = _dot(a_ref[...], w_ref[...]) + b_ref[...]
        la_ref[...] = (jnp.minimum(z, 0.0) - jnp.log(1.0 + jnp.exp(-jnp.abs(z)))) * (1.0 / TAU)

    return pl.pallas_call(
        body, name="gla_decay_fwd", grid=(n // tt,),
        in_specs=[pl.BlockSpec((tt, 128), lambda i: (i, (2 * KEY + 2 * D) // 128)),
                  pl.BlockSpec((128, 2 * KEY), lambda i: (0, 0)), pl.BlockSpec((1, 2 * KEY), lambda i: (0, 0))],
        out_specs=pl.BlockSpec((tt, 2 * KEY), lambda i: (i, 0)),
        out_shape=jax.ShapeDtypeStruct((n, 2 * KEY), F32),
        compiler_params=_params(("parallel",)),
    )(p_all, w2, b2)


def _gla_blocks(nb, nm, ncx):
    def main_idx(d, i):
        return jnp.clip(jnp.where(d == 0, i - ncx, nm - 1 - (i - ncx)), 0, nm - 1)

    def rowblk(d, b, i):
        cidx = jnp.where(d == 0, i, ncx - 1 - i)
        return jnp.where(i < ncx, nb * nm + b * ncx + cidx, b * nm + main_idx(d, i))

    def mainblk(d, b, i):
        return b * nm + main_idx(d, i)

    return rowblk, mainblk


def _gla_mask(d):
    row = lax.broadcasted_iota(jnp.int32, (CH, CH), 0)
    col = lax.broadcasted_iota(jnp.int32, (CH, CH), 1)
    diff = jnp.where(d == 0, row - col, col - row)
    mask = diff >= 0
    return mask, jnp.where(mask, 1.0, 0.0).astype(BF16), jnp.where(diff <= 0, 1.0, 0.0).astype(BF16)


def _tri_sum(m01, x):
    w = x.shape[1]
    hi = x.astype(BF16)
    r1 = x - hi.astype(F32)
    mid = r1.astype(BF16)
    lo = (r1 - mid.astype(F32)).astype(BF16)
    s = lax.dot_general(m01, jnp.concatenate([hi, mid, lo], axis=1), _NN, preferred_element_type=F32)
    return s[:, :w] + s[:, w:2 * w] + s[:, 2 * w:]


def _gla_chunk(q, k, g, bc):
    bl = _rowsum(g)
    eq = jnp.exp(bc)
    ek = jnp.exp(-bc)
    ed = jnp.exp(bl - bc)
    return bl, eq, ek, ed, q * Q_SCALE * eq, k * ek, k * ed


def _gla_scan_fwd(p_all, la_all, nb, t, tc):
    nm, ncx = t // CH, tc // CH
    nst = nm + ncx
    rowblk, mainblk = _gla_blocks(nb, nm, ncx)

    def body(*refs):
        ins, (o_refs, ss_refs, st_ref) = refs[:8], (refs[8:10], refs[10:12], refs[12])
        i = pl.program_id(1)

        @pl.when(i == 0)
        def _():
            st_ref[...] = jnp.zeros_like(st_ref)

        loaded = [r[...] for r in ins]
        states = [st_ref[j] for j in range(2 * HEADS)]
        outs, new_states = [[], []], []
        for d in range(2):
            q_all, k_all, v_all, g_all = loaded[4 * d:4 * d + 4]
            mask, m01, _ = _gla_mask(d)
            bc_all = _tri_sum(m01, g_all)
            for h in range(HEADS):
                ksl = slice(h * DK, (h + 1) * DK)
                v = v_all[:, h * DV:(h + 1) * DV]
                st = states[d * HEADS + h]
                bl, _, _, _, qs, ks, kd = _gla_chunk(q_all[:, ksl], k_all[:, ksl], g_all[:, ksl], bc_all[:, ksl])
                att = jnp.where(mask, _dot(qs, ks, _NT), 0.0)
                outs[d].append(_dot(qs, st, _NT) + _dot(att, v))
                new_states.append(st * jnp.exp(bl) + _dot(v, kd, _TN))
        for d in range(2):
            o_refs[d][...] = jnp.concatenate(outs[d], axis=1)
            for h in range(HEADS):
                ss_refs[d][0, 0, h] = states[d * HEADS + h]
                st_ref[d * HEADS + h] = new_states[d * HEADS + h]

    def in_specs(d):
        return [pl.BlockSpec((CH, KEY), lambda b, i: (rowblk(d, b, i), 0)),
                pl.BlockSpec((CH, KEY), lambda b, i: (rowblk(d, b, i), 1)),
                pl.BlockSpec((CH, D), lambda b, i: (rowblk(d, b, i), 1)),
                pl.BlockSpec((CH, KEY), lambda b, i: (rowblk(d, b, i), d))]

    outs = pl.pallas_call(
        body, name="gla_scan_fwd", grid=(nb, nst),
        in_specs=in_specs(0) + in_specs(1),
        out_specs=[pl.BlockSpec((CH, D), lambda b, i: (mainblk(0, b, i), 0)),
                   pl.BlockSpec((CH, D), lambda b, i: (mainblk(1, b, i), 0)),
                   pl.BlockSpec((1, 1, HEADS, DV, DK), lambda b, i: (b, i, 0, 0, 0)),
                   pl.BlockSpec((1, 1, HEADS, DV, DK), lambda b, i: (b, i, 0, 0, 0))],
        out_shape=[jax.ShapeDtypeStruct((nb * t, D), F32)] * 2
        + [jax.ShapeDtypeStruct((nb, nst, HEADS, DV, DK), F32)] * 2,
        scratch_shapes=[pltpu.VMEM((2 * HEADS, DV, DK), F32)],
        compiler_params=_params(("parallel", "arbitrary")),
    )(*([p_all, p_all, p_all, la_all] * 2))
    return outs[:2], outs[2:]


def _gla_scan_bwd(p_all, la_all, do, ss, nb, t, tc, after):
    nm, ncx = t // CH, tc // CH
    nst = nm + ncx
    ntot = nb * (t + tc)
    rowblk, mainblk = _gla_blocks(nb, nm, ncx)

    def body(*refs):
        ins, outs, dst_ref = refs[:12], refs[13:21], refs[21]
        ip = pl.program_id(1)
        i = nst - 1 - ip

        @pl.when(ip == 0)
        def _():
            dst_ref[...] = jnp.zeros_like(dst_ref)

        live = jnp.where(i >= ncx, 1.0, 0.0)
        loaded = [[r[...] for r in ins[6 * d:6 * d + 5]] for d in range(2)]
        states = [ins[6 * d + 5][0, 0, h] for d in range(2) for h in range(HEADS)]
        dstates = [dst_ref[j] for j in range(2 * HEADS)]
        results, new_dstates = [], []
        for d in range(2):
            q_all, k_all, v_all, g_all, do_all = loaded[d]
            do_all = do_all * live
            mask, m01, m01_t = _gla_mask(d)
            bc_all = _tri_sum(m01, g_all)
            dqs_l, dks_l, dvs_l, dbs_l, dbls_l = [], [], [], [], []
            for h in range(HEADS):
                ksl = slice(h * DK, (h + 1) * DK)
                vsl = slice(h * DV, (h + 1) * DV)
                bl, eq, ek, ed, qs, ks, kd = _gla_chunk(q_all[:, ksl], k_all[:, ksl], g_all[:, ksl], bc_all[:, ksl])
                st, dst, v, dov = states[d * HEADS + h], dstates[d * HEADS + h], v_all[:, vsl], do_all[:, vsl]
                att = jnp.where(mask, _dot(qs, ks, _NT), 0.0)
                datt = jnp.where(mask, _dot(dov, v, _NT), 0.0)
                dqs = _dot(dov, st) + _dot(datt, ks)
                dks = _dot(datt, qs, _TN)
                dvs_l.append(_dot(att, dov, _TN) + _dot(kd, dst, _NT))
                dkd = _dot(v, dst)
                e = jnp.exp(bl)
                dbls_l.append(e * _rowsum(st * dst) + _rowsum(dkd * kd))
                new_dstates.append(_dot(dov, qs, _TN) + dst * e)
                dqs_l.append(dqs * eq * Q_SCALE)
                dks_l.append(dks * ek + dkd * ed)
                dbs_l.append(dqs * qs - dks * ks - dkd * kd)
            results.append((jnp.concatenate(dqs_l, axis=1), jnp.concatenate(dks_l, axis=1),
                            jnp.concatenate(dvs_l, axis=1),
                            _tri_sum(m01_t, jnp.concatenate(dbs_l, axis=1)) + jnp.concatenate(dbls_l, axis=1)))
        for d in range(2):
            for k in range(4):
                outs[4 * d + k][...] = results[d][k]
        for j in range(2 * HEADS):
            dst_ref[j] = new_dstates[j]

    def in_specs(d):
        return [pl.BlockSpec((CH, KEY), lambda b, ip: (rowblk(d, b, nst - 1 - ip), 0)),
                pl.BlockSpec((CH, KEY), lambda b, ip: (rowblk(d, b, nst - 1 - ip), 1)),
                pl.BlockSpec((CH, D), lambda b, ip: (rowblk(d, b, nst - 1 - ip), 1)),
                pl.BlockSpec((CH, KEY), lambda b, ip: (rowblk(d, b, nst - 1 - ip), d)),
                pl.BlockSpec((CH, D), lambda b, ip: (mainblk(d, b, nst - 1 - ip), 0)),
                pl.BlockSpec((1, 1, HEADS, DV, DK), lambda b, ip: (b, nst - 1 - ip, 0, 0, 0))]

    def out_specs(d):
        row = lambda width: pl.BlockSpec((CH, width), lambda b, ip: (rowblk(d, b, nst - 1 - ip), 0))
        return [row(KEY), row(KEY), row(D), row(KEY)]

    shapes = [jax.ShapeDtypeStruct((ntot, KEY), F32), jax.ShapeDtypeStruct((ntot, KEY), F32),
              jax.ShapeDtypeStruct((ntot, D), F32), jax.ShapeDtypeStruct((ntot, KEY), F32)]
    outs = pl.pallas_call(
        body, name="gla_scan_bwd", grid=(nb, nst),
        in_specs=in_specs(0) + in_specs(1) + [pl.BlockSpec(memory_space=pl.ANY)],
        out_specs=out_specs(0) + out_specs(1),
        out_shape=shapes * 2,
        scratch_shapes=[pltpu.VMEM((2 * HEADS, DV, DK), F32)],
        compiler_params=_params(("parallel", "arbitrary")),
    )(p_all, p_all, p_all, la_all, do, ss[0], p_all, p_all, p_all, la_all, do, ss[1], after)
    return [[outs[k], outs[4 + k]] for k in range(4)]


def _gla_post_fwd(o2, p_all, head_gain, n):
    tt = _tile(n, 256)

    def body(of_ref, ob_ref, g_ref, hg_ref, y_ref):
        o = of_ref[...] + ob_ref[...]
        gv = g_ref[...]
        hg = hg_ref[...]
        for h in range(HEADS):
            oh = o[:, h * DV:(h + 1) * DV]
            r = lax.rsqrt(jnp.mean(oh * oh, axis=-1, keepdims=True) + EPS)
            y_ref[:, h * DV:(h + 1) * DV] = ((oh * r) * hg * _silu(gv[:, h * DV:(h + 1) * DV])).astype(BF16)

    row = pl.BlockSpec((tt, D), lambda i: (i, 0))
    return pl.pallas_call(
        body, name="gla_post_fwd", grid=(n // tt,),
        in_specs=[row, row, pl.BlockSpec((tt, D), lambda i: (i, 2)), pl.BlockSpec((1, DV), lambda i: (0, 0))],
        out_specs=row,
        out_shape=jax.ShapeDtypeStruct((n, D), BF16),
        compiler_params=_params(("parallel",)),
    )(o2[0], o2[1], p_all, head_gain)


def _gla_post_bwd(o2, p_all, head_gain, dyb, n):
    tt = _tile(n, 256)

    def body(of_ref, ob_ref, g_ref, hg_ref, dy_ref, do_ref, dg_ref, dhg_ref):
        i = pl.program_id(0)

        @pl.when(i == 0)
        def _():
            dhg_ref[...] = jnp.zeros_like(dhg_ref)

        o = of_ref[...] + ob_ref[...]
        gv = g_ref[...]
        hg = hg_ref[...]
        dy = dy_ref[...]
        acc = jnp.zeros((1, DV), F32)
        for h in range(HEADS):
            sl = slice(h * DV, (h + 1) * DV)
            oh = o[:, sl]
            r = lax.rsqrt(jnp.mean(oh * oh, axis=-1, keepdims=True) + EPS)
            on = oh * r
            gh = gv[:, sl]
            dyh = dy[:, sl]
            dg_ref[:, sl] = dyh * (on * hg) * _dsilu(gh)
            dog = dyh * _silu(gh)
            acc = acc + _rowsum(dog * on)
            don = dog * hg
            do_ref[:, sl] = r * (don - on * jnp.mean(don * on, axis=-1, keepdims=True))
        dhg_ref[...] += acc

    return pl.pallas_call(
        body, name="gla_post_bwd", grid=(n // tt,),
        in_specs=[pl.BlockSpec((tt, D), lambda i: (i, 0)), pl.BlockSpec((tt, D), lambda i: (i, 0)),
                  pl.BlockSpec((tt, D), lambda i: (i, 2)),
                  pl.BlockSpec((1, DV), lambda i: (0, 0)), pl.BlockSpec((tt, D), lambda i: (i, 0))],
        out_specs=[pl.BlockSpec((tt, D), lambda i: (i, 0)), pl.BlockSpec((tt, D), lambda i: (i, 0)),
                   pl.BlockSpec((1, DV), lambda i: (0, 0))],
        out_shape=[jax.ShapeDtypeStruct((n, D), F32), jax.ShapeDtypeStruct((n, D), F32),
                   jax.ShapeDtypeStruct((1, DV), F32)],
        compiler_params=_params(("arbitrary",)),
    )(o2[0], o2[1], p_all, head_gain, dyb)


def _gla_assemble(p_all, w2, b2, dq, dk, dv, dla, dgate, n):
    ntot = p_all.shape[0]
    tt = _tile(n, 128)
    nmain = n // tt
    assert ntot % tt == 0

    def body(a_ref, w_ref, b_ref, dqf_ref, dqb_ref, dkf_ref, dkb_ref, dvf_ref, dvb_ref, dlf_ref, dlb_ref, dg_ref,
             dp_ref, dw_ref, db_ref):
        i = pl.program_id(0)

        @pl.when(i == 0)
        def _():
            dw_ref[...] = jnp.zeros_like(dw_ref)
            db_ref[...] = jnp.zeros_like(db_ref)

        a = a_ref[...]
        w = w_ref[...]
        z = _dot(a, w) + b_ref[...]
        dla = jnp.concatenate([dlf_ref[...], dlb_ref[...]], axis=1)
        dz = dla * (1.0 / (1.0 + jnp.exp(z))) * (1.0 / TAU)
        dw_ref[...] += _dot(a, dz, _TN)
        db_ref[...] += _rowsum(dz)
        dp_ref[:, 0:KEY] = (dqf_ref[...] + dqb_ref[...]).astype(BF16)
        dp_ref[:, KEY:2 * KEY] = (dkf_ref[...] + dkb_ref[...]).astype(BF16)
        dp_ref[:, 2 * KEY:2 * KEY + D] = (dvf_ref[...] + dvb_ref[...]).astype(BF16)
        dp_ref[:, 2 * KEY + D:2 * KEY + 2 * D] = (dg_ref[...] * jnp.where(i < nmain, 1.0, 0.0)).astype(BF16)
        dp_ref[:, 2 * KEY + 2 * D:GLA_IN_PAD] = _dot(dz, w, _NT).astype(BF16)

    row = lambda width: pl.BlockSpec((tt, width), lambda i: (i, 0))
    return pl.pallas_call(
        body, name="gla_assemble", grid=(ntot // tt,),
        in_specs=[pl.BlockSpec((tt, 128), lambda i: (i, (2 * KEY + 2 * D) // 128)),
                  pl.BlockSpec((128, 2 * KEY), lambda i: (0, 0)), pl.BlockSpec((1, 2 * KEY), lambda i: (0, 0)),
                  row(KEY), row(KEY), row(KEY), row(KEY), row(D), row(D), row(KEY), row(KEY),
                  pl.BlockSpec((tt, D), lambda i: (jnp.minimum(i, nmain - 1), 0))],
        out_specs=[pl.BlockSpec((tt, GLA_IN_PAD), lambda i: (i, 0)), pl.BlockSpec((128, 2 * KEY), lambda i: (0, 0)),
                   pl.BlockSpec((1, 2 * KEY), lambda i: (0, 0))],
        out_shape=[jax.ShapeDtypeStruct((ntot, GLA_IN_PAD), BF16), jax.ShapeDtypeStruct((128, 2 * KEY), F32),
                   jax.ShapeDtypeStruct((1, 2 * KEY), F32)],
        compiler_params=_params(("arbitrary",)),
    )(p_all, w2, b2, dq[0], dq[1], dk[0], dk[1], dv[0], dv[1], dla[0], dla[1], dgate)


ADA_ROWS = 24
ADA_SH = N_MOD * D // N_CHIPS


def _ada_fwd(cvec, ada_w, ada_b_sh):
    def body(c_ref, w_ref, b_ref, o_ref):
        o_ref[0] = _dot(_silu(c_ref[...]), w_ref[0]) + b_ref[0]

    return pl.pallas_call(
        body, name="ada_fwd", grid=(2,),
        in_specs=[pl.BlockSpec((ADA_ROWS, D), lambda l: (0, 0)), pl.BlockSpec((1, D, ADA_SH), lambda l: (l, 0, 0)),
                  pl.BlockSpec((1, 1, ADA_SH), lambda l: (l, 0, 0))],
        out_specs=pl.BlockSpec((1, ADA_ROWS, ADA_SH), lambda l: (l, 0, 0)),
        out_shape=jax.ShapeDtypeStruct((2, ADA_ROWS, ADA_SH), F32),
        compiler_params=_params(("parallel",)),
    )(cvec, ada_w, ada_b_sh)


def _ada_bwd(cvec, ada_w, dmod_sh):
    def body(c_ref, w_ref, dm_ref, gw_ref, dc_ref):
        dm = dm_ref[0]
        gw_ref[0] = _dot(_silu(c_ref[...]), dm, _TN)
        dc_ref[0] = _dot(dm, w_ref[0], _NT)

    return pl.pallas_call(
        body, name="ada_bwd", grid=(2,),
        in_specs=[pl.BlockSpec((ADA_ROWS, D), lambda l: (0, 0)), pl.BlockSpec((1, D, ADA_SH), lambda l: (l, 0, 0)),
                  pl.BlockSpec((1, ADA_ROWS, ADA_SH), lambda l: (l, 0, 0))],
        out_specs=[pl.BlockSpec((1, D, ADA_SH), lambda l: (l, 0, 0)), pl.BlockSpec((1, ADA_ROWS, D), lambda l: (l, 0, 0))],
        out_shape=[jax.ShapeDtypeStruct((2, D, ADA_SH), F32), jax.ShapeDtypeStruct((2, ADA_ROWS, D), F32)],
        compiler_params=_params(("parallel",)),
    )(cvec, ada_w, dmod_sh)


def _sum_slots(x, name):
    s, r, _ = x.shape

    def body(x_ref, o_ref):
        acc = x_ref[0]
        for k in range(1, s):
            acc = acc + x_ref[k]
        o_ref[...] = acc

    return pl.pallas_call(
        body, name=name, out_shape=jax.ShapeDtypeStruct((r, 128), F32),
        in_specs=[pl.BlockSpec(memory_space=pltpu.VMEM)], out_specs=pl.BlockSpec(memory_space=pltpu.VMEM),
    )(x)


def _cctx_grad(dscc_parts, c_ctx):
    def body(p_ref, c_ref, o_ref):
        acc = p_ref[0]
        for k in range(1, N_CHIPS):
            acc = acc + p_ref[k]
        o_ref[...] = acc * _dsilu(c_ref[...])

    return pl.pallas_call(
        body, name="cctx_grad", out_shape=jax.ShapeDtypeStruct((8, 128), F32),
        in_specs=[pl.BlockSpec(memory_space=pltpu.VMEM)] * 2, out_specs=pl.BlockSpec(memory_space=pltpu.VMEM),
    )(dscc_parts, c_ctx)


def _adamw(w, g, m, v, name, after):
    r, cdim = w.shape
    tr = _tile(r, 256)
    c1 = 1.0 - ADAM_B1 ** ADAM_STEP
    c2 = 1.0 - ADAM_B2 ** ADAM_STEP

    def body(w_ref, g_ref, m_ref, v_ref, after_ref, d_ref, mo_ref, vo_ref):
        gv = g_ref[...]
        mn = ADAM_B1 * m_ref[...] + (1.0 - ADAM_B1) * gv
        vn = ADAM_B2 * v_ref[...] + (1.0 - ADAM_B2) * (gv * gv)
        mo_ref[...] = mn
        vo_ref[...] = vn
        d_ref[...] = -ADAM_LR * ((mn / c1) / (jnp.sqrt(vn / c2) + ADAM_EPS) + ADAM_WD * w_ref[...])

    spec = pl.BlockSpec((tr, cdim), lambda i: (i, 0))
    sds = jax.ShapeDtypeStruct((r, cdim), F32)
    return pl.pallas_call(
        body, name=name, grid=(r // tr,), in_specs=[spec] * 4 + [pl.BlockSpec(memory_space=pl.ANY)],
        out_specs=[spec] * 3, out_shape=[sds] * 3, compiler_params=_params(("parallel",)),
    )(w, g, m, v, after)


def _place():
    x, y, c = lax.axis_index("x"), lax.axis_index("y"), lax.axis_index("c")
    return x, y, c


def _allgather_small(blk, name):
    m_per, n = blk.shape

    def body(x_ref, out_ref, send_sems, recv_sems, local_sem):
        x, y, c = _place()
        me, sibling = (x, y, c), (x, y, 1 - c)
        chips = [(1 - x, y), (x, 1 - y), (1 - x, 1 - y)]

        def rows(px, py, pc):
            return out_ref.at[pl.ds((4 * px + 2 * py + pc) * m_per, m_per), :]

        def copy(k, block, to, src=None):
            return pltpu.make_async_remote_copy(
                src_ref=rows(*block) if src is None else src, dst_ref=rows(*block),
                send_sem=send_sems.at[k], recv_sem=recv_sems.at[k], device_id=to, device_id_type=MESH)

        mine = pltpu.make_async_copy(x_ref, rows(*me), local_sem)
        mine.start()
        first = [copy(0, me, sibling, src=x_ref)]
        first += [copy(1 + j, me, (*chip, c), src=x_ref) for j, chip in enumerate(chips)]
        for cp in first:
            cp.start()
        passed = [copy(4 + j, (*chip, c), sibling) for j, chip in enumerate(chips)]
        for j, chip in enumerate(chips):
            copy(1 + j, (*chip, c), me).wait_recv()
            passed[j].start()
        copy(0, sibling, me).wait_recv()
        for j, chip in enumerate(chips):
            copy(4 + j, (*chip, 1 - c), me).wait_recv()
        for cp in first + passed:
            cp.wait_send()
        mine.wait()

    return pl.pallas_call(
        body, name=name,
        out_shape=jax.ShapeDtypeStruct((N_DEV * m_per, n), blk.dtype),
        in_specs=[pl.BlockSpec(memory_space=pltpu.VMEM)],
        out_specs=pl.BlockSpec(memory_space=pltpu.VMEM),
        scratch_shapes=[pltpu.SemaphoreType.DMA((7,)), pltpu.SemaphoreType.DMA((7,)), pltpu.SemaphoreType.DMA],
    )(blk)


def _other_chips(x, y):
    return [(1 - x, y), (x, 1 - y), (1 - x, 1 - y)]


_HBM_SPEC = pl.BlockSpec(memory_space=pltpu.HBM)
_SEM_SPEC = pl.BlockSpec(memory_space=pltpu.SEMAPHORE)
_SPLIT_PARAMS = pltpu.CompilerParams(has_side_effects=pltpu.SideEffectType.DATAFLOW_SIDE_EFFECTING)


def _in_hbm(a):
    return pltpu.with_memory_space_constraint(a, pltpu.HBM)


def _ag_copies(own_ref, land_ref, send_sems, recv_sems):
    x, y, c = _place()
    chip = 2 * x + y
    hr = own_ref.shape[0] // 2

    def half(ch):
        return land_ref.at[ch, pl.ds(c * hr, hr), :]

    def copy(k, src, dst, to):
        return pltpu.make_async_remote_copy(src_ref=src, dst_ref=dst, send_sem=send_sems.at[k],
                                            recv_sem=recv_sems.at[k], device_id=to, device_id_type=MESH)

    sends, expects = [], []
    for j, (ox, oy) in enumerate(_other_chips(x, y)):
        sends.append(copy(j, own_ref.at[pl.ds(c * hr, hr), :], half(chip), (ox, oy, c)))
        expects.append(copy(j, half(2 * ox + oy), half(2 * ox + oy), (ox, oy, c)))
    own_slot = copy(3, own_ref, land_ref.at[chip], (x, y, 1 - c))
    return sends + [own_slot], expects + [own_slot]


def _sc_copies(p_ref, land_ref, send_sems, recv_sems):
    x, y, c = _place()
    chip = 2 * x + y
    sends, expects = [], []
    for j, (ox, oy) in enumerate(_other_chips(x, y)):
        och = 2 * ox + oy
        mk = lambda dst_slot: pltpu.make_async_remote_copy(
            src_ref=p_ref.at[och], dst_ref=land_ref.at[dst_slot], send_sem=send_sems.at[j],
            recv_sem=recv_sems.at[j], device_id=(ox, oy, c), device_id_type=MESH)
        sends.append(mk(chip))
        expects.append(mk(och))
    return sends, expects


def _pe_copies(g_ref, land_ref, send_sems, recv_sems):
    x, y, c = _place()
    hr = g_ref.shape[1] // 2
    cp = pltpu.make_async_remote_copy(
        src_ref=g_ref.at[:, pl.ds((1 - c) * hr, hr), :], dst_ref=land_ref, send_sem=send_sems.at[0],
        recv_sem=recv_sems.at[0], device_id=(x, y, 1 - c), device_id_type=MESH)
    return [cp], [cp]


def _split_start(src, land_shape, copies, n_copies, after, name):
    def body(src_ref, land_ref, after_ref, send_sems, recv_sems, src_thru, land_thru, token):
        for cp in copies(src_ref, land_ref, send_sems, recv_sems)[0]:
            cp.start()
        token[...] = jnp.zeros_like(token)

    land = lax.empty(land_shape, src.dtype)
    return pl.pallas_call(
        body, name=name,
        out_shape=(pltpu.SemaphoreType.DMA((n_copies,)), pltpu.SemaphoreType.DMA((n_copies,)),
                   pltpu.HBM(src.shape, src.dtype), pltpu.HBM(land_shape, src.dtype),
                   jax.ShapeDtypeStruct((8, 128), F32)),
        in_specs=(_HBM_SPEC, _HBM_SPEC, pl.BlockSpec(memory_space=pl.ANY)),
        out_specs=(_SEM_SPEC, _SEM_SPEC, _HBM_SPEC, _HBM_SPEC, pl.BlockSpec(memory_space=pltpu.VMEM)),
        input_output_aliases={0: 2, 1: 3}, compiler_params=_SPLIT_PARAMS,
    )(_in_hbm(src), _in_hbm(land), after)


def _split_wait(started, after, copies, name):
    send_sems, recv_sems, src_thru, land_thru, _ = started

    def body(src_ref, land_ref, send_sems, recv_sems, after_ref, src_dead, got_ref):
        sends, expects = copies(src_ref, land_ref, send_sems, recv_sems)
        for cp in sends:
            cp.wait_send()
        for cp in expects:
            cp.wait_recv()

    return pl.pallas_call(
        body, name=name,
        out_shape=(pltpu.HBM(src_thru.shape, src_thru.dtype), pltpu.HBM(land_thru.shape, land_thru.dtype)),
        in_specs=(_HBM_SPEC, _HBM_SPEC, _SEM_SPEC, _SEM_SPEC, pl.BlockSpec(memory_space=pl.ANY)),
        out_specs=(_HBM_SPEC, _HBM_SPEC), input_output_aliases={0: 0, 1: 1}, compiler_params=_SPLIT_PARAMS,
    )(src_thru, land_thru, send_sems, recv_sems, after)


def _ag_pass_on(land, name):
    hr = land.shape[1] // 2

    def body(in_ref, out_ref, send_sems, recv_sems):
        x, y, c = _place()

        def copy(j, ox, oy, cc):
            ref = out_ref.at[2 * ox + oy, pl.ds(cc * hr, hr), :]
            return pltpu.make_async_remote_copy(src_ref=ref, dst_ref=ref, send_sem=send_sems.at[j],
                                                recv_sem=recv_sems.at[j], device_id=(x, y, 1 - c),
                                                device_id_type=MESH)

        others = _other_chips(x, y)
        for j, (ox, oy) in enumerate(others):
            copy(j, ox, oy, c).start()
        for j, (ox, oy) in enumerate(others):
            copy(j, ox, oy, 1 - c).wait_recv()
        for j, (ox, oy) in enumerate(others):
            copy(j, ox, oy, c).wait_send()

    any_spec = pl.BlockSpec(memory_space=pl.ANY)
    return pl.pallas_call(
        body, name=name, out_shape=jax.ShapeDtypeStruct(land.shape, land.dtype),
        in_specs=[any_spec], out_specs=any_spec, input_output_aliases={0: 0},
        scratch_shapes=[pltpu.SemaphoreType.DMA((3,)), pltpu.SemaphoreType.DMA((3,))],
    )(land)


def _rs_pair_exchange(g, name):
    r = g.shape[1]
    hr = r // 2

    def body(g_ref, got_ref, send_sem, recv_sem):
        x, y, c = _place()
        cp = pltpu.make_async_remote_copy(
            src_ref=g_ref.at[:, pl.ds((1 - c) * hr, hr), :], dst_ref=got_ref, send_sem=send_sem, recv_sem=recv_sem,
            device_id=(x, y, 1 - c), device_id_type=MESH)
        cp.start()
        cp.wait()

    any_spec = pl.BlockSpec(memory_space=pl.ANY)
    return pl.pallas_call(
        body, name=name,
        out_shape=jax.ShapeDtypeStruct((N_CHIPS, hr, D), F32),
        in_specs=[any_spec], out_specs=any_spec,
        scratch_shapes=[pltpu.SemaphoreType.DMA, pltpu.SemaphoreType.DMA],
    )(g)


def _rs_chip_sum(place, g, got, name):
    r = g.shape[1]
    hr = r // 2
    tr = _tile(hr, 640, 16)
    nt = hr // tr

    def body(pl_ref, g_ref, got_ref, p16_ref, p32_ref):
        s = pl.program_id(1)
        p = g_ref[0] + got_ref[0]
        p16_ref[0] = p.astype(BF16)

        @pl.when(s == pl_ref[1])
        def _():
            p32_ref[...] = p

    return pl.pallas_call(
        body, name=name,
        grid_spec=pltpu.PrefetchScalarGridSpec(
            num_scalar_prefetch=1, grid=(nt, N_CHIPS),
            in_specs=[pl.BlockSpec((1, tr, D), lambda i, s, pr: (s, pr[0] * nt + i, 0)),
                      pl.BlockSpec((1, tr, D), lambda i, s, pr: (s, i, 0))],
            out_specs=[pl.BlockSpec((1, tr, D), lambda i, s, pr: (s, i, 0)),
                       pl.BlockSpec((tr, D), lambda i, s, pr: (i, 0))]),
        out_shape=[jax.ShapeDtypeStruct((N_CHIPS, hr, D), BF16), jax.ShapeDtypeStruct((hr, D), F32)],
        compiler_params=_params(("parallel", "arbitrary")),
    )(place, g, got)


def _rs_final_sum(place, parts, p32, name):
    hr = parts.shape[1]
    tr = _tile(hr, 640, 16)
    nt = hr // tr

    def body(pl_ref, a_ref, b_ref, c_ref, p32_ref, o_ref):
        o_ref[...] = ((p32_ref[...] + a_ref[0].astype(F32)) + b_ref[0].astype(F32)) + c_ref[0].astype(F32)

    def other(j):
        return pl.BlockSpec((1, tr, D), lambda i, pr: (j + jnp.where(pr[1] <= j, 1, 0), i, 0))

    return pl.pallas_call(
        body, name=name,
        grid_spec=pltpu.PrefetchScalarGridSpec(
            num_scalar_prefetch=1, grid=(nt,),
            in_specs=[other(0), other(1), other(2), pl.BlockSpec((tr, D), lambda i, pr: (i, 0))],
            out_specs=pl.BlockSpec((tr, D), lambda i, pr: (pr[0] * nt + i, 0))),
        out_shape=jax.ShapeDtypeStruct((2 * hr, D), F32),
        compiler_params=_params(("parallel",)),
    )(place, parts, parts, parts, p32)


def _rs_pair_gather(both, name):
    hr = both.shape[0] // 2

    def body(in_ref, out_ref, send_sem, recv_sem):
        x, y, c = _place()
        mine = out_ref.at[pl.ds(c * hr, hr), :]
        cp = pltpu.make_async_remote_copy(
            src_ref=mine, dst_ref=mine, send_sem=send_sem, recv_sem=recv_sem,
            device_id=(x, y, 1 - c), device_id_type=MESH)
        cp.start()
        theirs = out_ref.at[pl.ds((1 - c) * hr, hr), :]
        pltpu.make_async_remote_copy(
            src_ref=theirs, dst_ref=theirs, send_sem=send_sem, recv_sem=recv_sem,
            device_id=(x, y, 1 - c), device_id_type=MESH).wait_recv()
        cp.wait_send()

    any_spec = pl.BlockSpec(memory_space=pl.ANY)
    return pl.pallas_call(
        body, name=name,
        out_shape=jax.ShapeDtypeStruct(both.shape, F32),
        in_specs=[any_spec], out_specs=any_spec, input_output_aliases={0: 0},
        scratch_shapes=[pltpu.SemaphoreType.DMA, pltpu.SemaphoreType.DMA],
    )(both)


def _local_step(x, ctx, tgt, mods, mc, ag_gin, ag_main, place, small):
    nb, t, _ = x.shape
    tc = ctx.shape[1]
    n = nb * t
    nc = nb * tc
    xf = x.reshape(n, D)
    cf = ctx.reshape(nc, D)
    tf = tgt.reshape(n, D)
    vec = lambda a: a.reshape(1, -1)
    m = [[mods[l, :, k, :].reshape(nb, 1, D) for k in range(N_MOD)] for l in range(2)]
    mc_b = [jnp.broadcast_to(mc[k].reshape(1, 1, D), (nb, 1, D)) for k in range(2)]

    cw = [small["ffn_conv_w"][l] for l in range(2)]
    cb = [small["ffn_conv_b"][l].reshape(1, -1) for l in range(2)]
    w2 = jnp.zeros((128, 2 * KEY), F32)
    w2 = w2.at[0:RANK, 0:KEY].set(small["gla_w_a2"][0]).at[RANK:2 * RANK, KEY:].set(small["gla_w_a2"][1])
    b2 = small["gla_b_a"].reshape(1, 2 * KEY)
    hg = small["gla_head_norm"].reshape(1, DV)

    hn0 = _mod_fwd(xf, vec(small["norm_mix"][0]), m[0][0], m[0][1], t, "mod0_main")
    hnc = _mod_fwd(cf, vec(small["norm_mix"][0]), mc_b[0], mc_b[1], tc, "mod0_ctx")
    hn_all = jnp.concatenate([hn0, hnc], axis=0)
    gin = _ag_pass_on(_split_wait(ag_gin, hn_all, _ag_copies, "ag_gin_wait")[1], "ag_gin_pass_on")
    w_gin = jnp.pad(gin[:, :_GIN_ROWS, :].reshape(GLA_IN, D), ((0, GLA_IN_PAD - GLA_IN), (0, 0)))
    p_all = _mm(hn_all, w_gin, "nt", F32, "gla_in_proj", 768, 3200)
    la_all = _gla_decay_fwd(p_all, w2, b2)
    o2, ss = _gla_scan_fwd(p_all, la_all, nb, t, tc)
    wg = _ag_pass_on(_split_wait(ag_main, o2[0], _ag_copies, "ag_main_wait")[1], "ag_main_pass_on")
    offs = _offsets(_MAIN, _MAIN_ROWS)
    rows = _MAIN_ROWS

    def w_nt(a, k, name, tm=1024):
        return _mm_nt_w(a, wg, offs[k], rows[k], name, tm)

    def w_nn(a3, k, name, tm, tn):
        return _mm_nn_w(a3, wg, offs[k], rows[k], name, tm, tn)

    yb0 = _gla_post_fwd(o2, p_all, hg, n)
    y0 = w_nn(yb0[None], "gla_out", "gla_out_proj", 1024, 1024)
    h1, hn1 = _mod_fwd(xf, vec(small["norm_ffn"][0]), m[0][3], m[0][4], t, "mod0_ffn", y=y0, gate=m[0][2])
    u0 = w_nt(hn1, "up_t0", "ffn0_up")
    z0 = _ffn_mid_fwd(u0, cw[0], cb[0], nb, t, "ffn0_mid_fwd")
    f0 = w_nn(z0[None], "down0", "ffn0_down", 1024, 1024)
    h2, hn2 = _mod_fwd(h1, vec(small["norm_mix"][1]), m[1][0], m[1][1], t, "mod1_mix", y=f0, gate=m[0][5])
    p1 = w_nt(hn2, "sc_in_t", "sc_in_proj")
    yb1 = _sc_mid_fwd(p1, small["sc_conv_w"], nb, t)
    y1 = w_nn(yb1[None], "sc_out", "sc_out_proj", 1024, 1024)
    h3, hn3 = _mod_fwd(h2, vec(small["norm_ffn"][1]), m[1][3], m[1][4], t, "mod1_ffn", y=y1, gate=m[1][2])
    u1 = w_nt(hn3, "up_t1", "ffn1_up")
    z1 = _ffn_mid_fwd(u1, cw[1], cb[1], nb, t, "ffn1_mid_fwd")
    f1 = w_nn(z1[None], "down1", "ffn1_down", 1024, 1024)
    loss, dh4, df1, dm15, dfinal = _final(h3, f1, m[1][5], vec(small["final_norm"]), tf, t)

    gs = {}
    dmods = [[None] * N_MOD for _ in range(2)]
    dmods[1][5] = dm15

    def w_dw(a3, b, g_prev, k, name, tm):
        return _mm_dw(a3, b, g_prev, offs[k], rows[k], name, tm)

    def ffn_bwd(l, df, u, z, hn, g_prev):
        dz = w_nt(df, f"down{l}", f"ffn{l}_down_dx")
        g_acc = w_dw(z[None], df, g_prev, f"down{l}", f"ffn{l}_down_dw", 640)
        du, dcw, dcb = _ffn_mid_bwd(u, cw[l], cb[l], dz, nb, t, f"ffn{l}_mid_bwd")
        dhn = w_nn(du, f"up_t{l}", f"ffn{l}_up_dx", 512, 512)
        g_acc = w_dw(du, hn, g_acc, f"up_t{l}", f"ffn{l}_up_dw", 640)
        return dhn, g_acc, jnp.moveaxis(dcw, 0, 1).reshape(3, 2 * HID), dcb.reshape(2 * HID)

    dhn3, g_acc, dcw1, dcb1 = ffn_bwd(1, df1, u1, z1, hn3, None)
    r = _mod_bwd(h3, dhn3, vec(small["norm_ffn"][1]), m[1][4], t, "mod1_ffn_bwd", dh_out=dh4, y_prev=y1,
                 gate_prev=m[1][2])
    dh3, dmods[1][4], dmods[1][3], dnf1, dy1, dmods[1][2] = (r["dh"], r["dscale"], r["dshift"], r["dgain"],
                                                             r["dy_prev"], r["dgate_prev"])
    dyb1 = w_nt(dy1, "sc_out", "sc_out_dx")
    g_acc = w_dw(yb1[None], dy1, g_acc, "sc_out", "sc_out_dw", 256)
    dp1, dscw = _sc_mid_bwd(p1, small["sc_conv_w"], dyb1, nb, t)
    dhn2 = w_nn(dp1, "sc_in_t", "sc_in_dx", 1024, 512)
    g_acc = w_dw(dp1, hn2, g_acc, "sc_in_t", "sc_in_dw", 256)
    r = _mod_bwd(h2, dhn2, vec(small["norm_mix"][1]), m[1][1], t, "mod1_mix_bwd", dh_out=dh3, y_prev=f0,
                 gate_prev=m[0][5])
    dh2, dmods[1][1], dmods[1][0], dnm1, df0, dmods[0][5] = (r["dh"], r["dscale"], r["dshift"], r["dgain"],
                                                             r["dy_prev"], r["dgate_prev"])
    dhn1, g_acc, dcw0, dcb0 = ffn_bwd(0, df0, u0, z0, hn1, g_acc)
    r = _mod_bwd(h1, dhn1, vec(small["norm_ffn"][0]), m[0][4], t, "mod0_ffn_bwd", dh_out=dh2, y_prev=y0,
                 gate_prev=m[0][2])
    dh1, dmods[0][4], dmods[0][3], dnf0, dy0, dmods[0][2] = (r["dh"], r["dscale"], r["dshift"], r["dgain"],
                                                             r["dy_prev"], r["dgate_prev"])
    g_packed = w_dw(yb0[None], dy0, g_acc, "gla_out", "gla_out_dw", 256)
    pair = _split_start(g_packed, (N_CHIPS, _MAIN_TOTAL // 2, D), _pe_copies, 1, dy0, "rs_main_pair_start")
    dyb0 = w_nt(dy0, "gla_out", "gla_out_dx")
    do, dgate, dhg = _gla_post_bwd(o2, p_all, hg + pair[4][0:1, 0:1], dyb0, n)
    g_packed, from_sibling = _split_wait(pair, do, _pe_copies, "rs_main_pair_wait")
    p16, p32 = _rs_chip_sum(place, g_packed, from_sibling, "rs_main_chip_sum")
    sc_main = _split_start(p16, p16.shape, _sc_copies, 3, p32, "rs_main_scatter_start")
    dq, dk, dv, dla = _gla_scan_bwd(p_all, la_all, do, ss, nb, t, tc, sc_main[4])
    dp, dw2, db2 = _gla_assemble(p_all, w2, b2, dq, dk, dv, dla, dgate, n)
    dhn_all = _mm(dp, w_gin, "nn", F32, "gla_in_dx", 768, 512)
    landed = _split_wait(sc_main, dhn_all, _sc_copies, "rs_main_scatter_wait")[1]
    g_main = _rs_pair_gather(_rs_final_sum(place, landed, p32, "rs_main_final_sum"), "rs_main_pair_gather")
    g_gin = _mm(dp, hn_all, "tn", F32, "gla_in_dw", 640, 1024)[:GLA_IN]
    g_gin = jnp.pad(g_gin.reshape(N_CHIPS, _GIN_ROWS, D), ((0, 0), (0, _GIN_PAD - _GIN_ROWS), (0, 0)))
    from_sibling = _rs_pair_exchange(g_gin, "rs_gin_pair_exchange")
    p16_gin, p32_gin = _rs_chip_sum(place, g_gin, from_sibling, "rs_gin_chip_sum")
    r = _mod_bwd(xf, dhn_all, vec(small["norm_mix"][0]), m[0][1], t, "mod0_main_bwd", dh_out=dh1)
    grad_x, dmods[0][1], dmods[0][0], dnm0 = r["dh"], r["dscale"], r["dshift"], r["dgain"]
    rc = _mod_bwd(cf, dhn_all, vec(small["norm_mix"][0]), mc_b[1], tc, "mod0_ctx_bwd", dhn_row0=n, need_dh=False)
    dmc = jnp.stack([jnp.sum(rc["dshift"], axis=0).reshape(D), jnp.sum(rc["dscale"], axis=0).reshape(D)])
    dnm0 = dnm0 + rc["dgain"]

    gs["norm_mix"] = jnp.concatenate([dnm0, dnm1], axis=0)
    gs["norm_ffn"] = jnp.concatenate([dnf0, dnf1], axis=0)
    gs["final_norm"] = dfinal.reshape(D)
    gs["gla_w_a2"] = jnp.stack([dw2[0:RANK, 0:KEY], dw2[RANK:2 * RANK, KEY:]])
    gs["gla_b_a"] = db2.reshape(2, KEY)
    gs["gla_head_norm"] = dhg.reshape(DV)
    gs["sc_conv_w"] = dscw
    gs["ffn_conv_w"] = jnp.stack([dcw0, dcw1])
    gs["ffn_conv_b"] = jnp.stack([dcb0, dcb1])
    dmods_arr = jnp.stack([jnp.stack([dmods[l][k].reshape(nb, D) for k in range(N_MOD)], axis=1) for l in range(2)])
    return loss, grad_x.reshape(nb, t, D), g_main, p16_gin, p32_gin, gs, dmods_arr, dmc


def _pack(arrs):
    parts, meta, off = [], [], 0
    for a in arrs:
        r = a.size // 128
        rp = -(-r // 8) * 8
        a2 = a.reshape(r, 128).astype(F32)
        if rp != r:
            a2 = jnp.pad(a2, ((0, rp - r), (0, 0)))
        parts.append(a2)
        meta.append((off, r, a.shape))
        off += rp
    return jnp.concatenate(parts, axis=0), meta


def _unpack(buf, meta, lead=()):
    return [buf[..., off:off + r, :].reshape(*lead, *shape) for off, r, shape in meta]


_MAIN = ("up_t0", "up_t1", "down0", "down1", "sc_in_t", "gla_out", "sc_out")
_MAIN_ROWS = {"sc_in_t": 3 * D // N_CHIPS, "up_t0": 2 * HID // N_CHIPS, "up_t1": 2 * HID // N_CHIPS,
              "gla_out": D // N_CHIPS, "sc_out": D // N_CHIPS, "down0": HID // N_CHIPS, "down1": HID // N_CHIPS}
_MAIN_TOTAL = sum(_MAIN_ROWS.values())
_GIN_ROWS = GLA_IN // N_CHIPS
_GIN_PAD = -(-_GIN_ROWS // 32) * 32


def _offsets(names, rows):
    off, out = 0, {}
    for k in names:
        out[k] = off
        off += rows[k]
    return out


def kernel(x, c, ctx, c_ctx, ada_w, ada_b, norm_mix, norm_ffn, gla_w_in, gla_w_a2, gla_b_a, gla_head_norm, gla_w_out, sc_w_in, sc_conv_w, sc_w_out, ffn_w_up, ffn_conv_w, ffn_conv_b, ffn_w_down, final_norm, loss_target, m_c_ctx, m_ada_w, m_ada_b, m_norm_mix, m_norm_ffn, m_gla_w_in, m_gla_w_a2, m_gla_b_a, m_gla_head_norm, m_gla_w_out, m_sc_w_in, m_sc_conv_w, m_sc_w_out, m_ffn_w_up, m_ffn_conv_w, m_ffn_conv_b, m_ffn_w_down, m_final_norm, v_c_ctx, v_ada_w, v_ada_b, v_norm_mix, v_norm_ffn, v_gla_w_in, v_gla_w_a2, v_gla_b_a, v_gla_head_norm, v_gla_w_out, v_sc_w_in, v_sc_conv_w, v_sc_w_out, v_ffn_w_up, v_ffn_conv_w, v_ffn_conv_b, v_ffn_w_down, v_final_norm):
    ix, iy, ic = _place()
    chip = 2 * ix + iy
    dev = 2 * chip + ic
    place = jnp.stack([ic, chip]).astype(jnp.int32)
    nb = x.shape[0]
    offs = _offsets(_MAIN, _MAIN_ROWS)

    buf, meta = _pack([c, ffn_conv_w, sc_conv_w, gla_w_a2, gla_b_a])
    got = _allgather_small(buf, "gather_small_in").reshape(N_DEV, buf.shape[0], 128)
    c_all, fcw, scw, wa2, ba = _unpack(got, meta, (N_DEV,))
    c_all = c_all.reshape(N_DEV * nb, D)
    per_chip = lambda a: a[0::2]
    ffn_conv_w_full = jnp.moveaxis(per_chip(fcw), 0, 2).reshape(2, 3, 2 * HID)
    sc_conv_w_full = jnp.moveaxis(per_chip(scw)[:, 0], 0, 1).reshape(3, D)
    gla_w_a2_full = jnp.moveaxis(per_chip(wa2)[:, 0], 0, 2).reshape(2, RANK, KEY)
    gla_b_a_full = jnp.moveaxis(per_chip(ba)[:, 0], 0, 1).reshape(2, KEY)

    cvec = jnp.concatenate([c_all, c_ctx.reshape(1, D), jnp.zeros((ADA_ROWS - N_DEV * nb - 1, D), F32)], axis=0)
    ada_b_sh = lax.dynamic_slice_in_dim(ada_b, chip * ADA_SH, ADA_SH, axis=1).reshape(2, 1, ADA_SH)
    mod_sh = _ada_fwd(cvec, ada_w, ada_b_sh)
    got = _allgather_small(mod_sh.reshape(2 * ADA_ROWS, ADA_SH), "gather_mod")
    mod_full = jnp.moveaxis(per_chip(got.reshape(N_DEV, 2, ADA_ROWS, ADA_SH)), 0, 2).reshape(2, ADA_ROWS, N_MOD * D)
    mc = mod_full[0, N_DEV * nb, :2 * D].reshape(2, D)

    own = {"sc_in_t": sc_w_in[0].T, "up_t0": ffn_w_up[0].T, "up_t1": ffn_w_up[1].T,
           "gla_out": gla_w_out[0], "sc_out": sc_w_out[0], "down0": ffn_w_down[0], "down1": ffn_w_down[1]}
    own_main = jnp.concatenate([own[k].astype(BF16) for k in _MAIN], axis=0)
    own_gin = jnp.pad(gla_w_in[0].T.astype(BF16), ((0, _GIN_PAD - _GIN_ROWS), (0, 0)))
    ag_gin = _split_start(own_gin, (N_CHIPS, _GIN_PAD, D), _ag_copies, 4, mc, "ag_gin_start")
    ag_main = _split_start(own_main, (N_CHIPS, _MAIN_TOTAL, D), _ag_copies, 4, ag_gin[4], "ag_main_start")
    mods = lax.dynamic_slice_in_dim(mod_full, dev * nb, nb, axis=1).reshape(2, nb, N_MOD, D) + ag_main[4][0, 0]

    small = {"norm_mix": norm_mix, "norm_ffn": norm_ffn, "final_norm": final_norm, "gla_w_a2": gla_w_a2_full,
             "gla_b_a": gla_b_a_full, "gla_head_norm": gla_head_norm[0], "sc_conv_w": sc_conv_w_full,
             "ffn_conv_w": ffn_conv_w_full, "ffn_conv_b": ffn_conv_b}
    loss_p, grad_x, g_main, p16_gin, p32_gin, gs, dmods, dmc = _local_step(x, ctx, loss_target, mods, mc, ag_gin,
                                                                           ag_main, place, small)

    sum_names = ["norm_mix", "norm_ffn", "final_norm", "gla_w_a2", "gla_b_a", "gla_head_norm", "sc_conv_w",
                 "ffn_conv_w", "ffn_conv_b"]
    buf, meta = _pack([jnp.broadcast_to(loss_p, (8, 128))] + [gs[k] for k in sum_names] + [dmc, dmods])
    n_sum = meta[-1][0]
    got = _allgather_small(buf, "gather_small_grads").reshape(N_DEV, buf.shape[0], 128)
    summed = _sum_slots(got[:, :n_sum], "sum_small_grads")
    parts = _unpack(summed, meta[:-1])
    loss = parts[0][0, 0]
    g_small = dict(zip(sum_names, parts[1:-1]))
    dmc_tot = parts[-1]
    dmods_all = jnp.moveaxis(_unpack(got, meta[-1:], (N_DEV,))[0], 0, 1).reshape(2, N_DEV * nb, N_MOD * D)

    ctx_row = jnp.stack([jnp.concatenate([dmc_tot.reshape(2 * D), jnp.zeros(((N_MOD - 2) * D,), F32)]),
                         jnp.zeros((N_MOD * D,), F32)]).reshape(2, 1, N_MOD * D)
    dmod_ext = jnp.concatenate([dmods_all, ctx_row, jnp.zeros((2, ADA_ROWS - N_DEV * nb - 1, N_MOD * D), F32)], axis=1)
    g_ada_b = _sum_slots(jnp.moveaxis(dmod_ext, 1, 0).reshape(ADA_ROWS, 2 * N_MOD * D // 128, 128),
                         "sum_ada_b").reshape(2, N_MOD * D)
    dmod_sh = lax.dynamic_slice_in_dim(dmod_ext, chip * ADA_SH, ADA_SH, axis=2)
    g_ada_w, dcv = _ada_bwd(cvec, ada_w, dmod_sh)
    dscc_part = (dcv[0, N_DEV * nb] + dcv[1, N_DEV * nb]).reshape(8, 128)
    got = _allgather_small(dscc_part, "gather_dscc").reshape(N_DEV, 8, 128)
    g_c_ctx = _cctx_grad(per_chip(got), c_ctx.reshape(8, 128)).reshape(D)

    sc_gin = _split_start(p16_gin, p16_gin.shape, _sc_copies, 3, g_c_ctx, "rs_gin_scatter_start")
    seg = {k: g_main[offs[k]:offs[k] + _MAIN_ROWS[k]] for k in _MAIN}

    sl_chip = lambda a, axis, width: lax.dynamic_slice_in_dim(a, chip * width, width, axis=axis)
    grads = {
        "c_ctx": g_c_ctx, "ada_w": g_ada_w, "ada_b": g_ada_b, "norm_mix": g_small["norm_mix"],
        "norm_ffn": g_small["norm_ffn"],
        "gla_w_a2": sl_chip(g_small["gla_w_a2"], 2, KEY // N_CHIPS)[None],
        "gla_b_a": sl_chip(g_small["gla_b_a"], 1, KEY // N_CHIPS)[None],
        "gla_head_norm": g_small["gla_head_norm"][None], "gla_w_out": seg["gla_out"][None],
        "sc_w_in": seg["sc_in_t"].T[None], "sc_conv_w": sl_chip(g_small["sc_conv_w"], 1, D // N_CHIPS)[None],
        "sc_w_out": seg["sc_out"][None], "ffn_w_up": jnp.stack([seg["up_t0"].T, seg["up_t1"].T]),
        "ffn_conv_w": sl_chip(g_small["ffn_conv_w"], 2, 2 * HID // N_CHIPS), "ffn_conv_b": g_small["ffn_conv_b"],
        "ffn_w_down": jnp.stack([seg["down0"], seg["down1"]]), "final_norm": g_small["final_norm"],
    }
    weights = {"c_ctx": c_ctx, "ada_w": ada_w, "ada_b": ada_b, "norm_mix": norm_mix, "norm_ffn": norm_ffn,
               "gla_w_in": gla_w_in, "gla_w_a2": gla_w_a2, "gla_b_a": gla_b_a, "gla_head_norm": gla_head_norm,
               "gla_w_out": gla_w_out, "sc_w_in": sc_w_in, "sc_conv_w": sc_conv_w, "sc_w_out": sc_w_out,
               "ffn_w_up": ffn_w_up, "ffn_conv_w": ffn_conv_w, "ffn_conv_b": ffn_conv_b, "ffn_w_down": ffn_w_down,
               "final_norm": final_norm}
    mom1 = {"c_ctx": m_c_ctx, "ada_w": m_ada_w, "ada_b": m_ada_b, "norm_mix": m_norm_mix, "norm_ffn": m_norm_ffn,
            "gla_w_in": m_gla_w_in, "gla_w_a2": m_gla_w_a2, "gla_b_a": m_gla_b_a, "gla_head_norm": m_gla_head_norm,
            "gla_w_out": m_gla_w_out, "sc_w_in": m_sc_w_in, "sc_conv_w": m_sc_conv_w, "sc_w_out": m_sc_w_out,
            "ffn_w_up": m_ffn_w_up, "ffn_conv_w": m_ffn_conv_w, "ffn_conv_b": m_ffn_conv_b,
            "ffn_w_down": m_ffn_w_down, "final_norm": m_final_norm}
    mom2 = {"c_ctx": v_c_ctx, "ada_w": v_ada_w, "ada_b": v_ada_b, "norm_mix": v_norm_mix, "norm_ffn": v_norm_ffn,
            "gla_w_in": v_gla_w_in, "gla_w_a2": v_gla_w_a2, "gla_b_a": v_gla_b_a, "gla_head_norm": v_gla_head_norm,
            "gla_w_out": v_gla_w_out, "sc_w_in": v_sc_w_in, "sc_conv_w": v_sc_conv_w, "sc_w_out": v_sc_w_out,
            "ffn_w_up": v_ffn_w_up, "ffn_conv_w": v_ffn_conv_w, "ffn_conv_b": v_ffn_conv_b,
            "ffn_w_down": v_ffn_w_down, "final_norm": v_final_norm}
    names = list(weights)

    big_names = ["ada_w", "gla_w_out", "sc_w_in", "sc_w_out", "ffn_w_up", "ffn_w_down", "gla_w_in"]
    small_names = [k for k in names if k not in big_names]
    delta, new_m, new_v = {}, {}, {}
    done = []

    def big_adamw(k, token):
        shp = weights[k].shape
        as2d = lambda a: a.reshape(-1, shp[-1])
        d_, m_, v_ = _adamw(as2d(weights[k]), as2d(grads[k]), as2d(mom1[k]), as2d(mom2[k]), "adamw_" + k, token)
        done.append(v_[0:1, 0:128])
        delta[k], new_m[k], new_v[k] = d_.reshape(shp), m_.reshape(shp), v_.reshape(shp)

    for k in big_names[:-1]:
        grads[k] = grads[k].reshape(weights[k].shape)
        big_adamw(k, sc_gin[4])
    for k in small_names:
        grads[k] = grads[k].reshape(weights[k].shape)
    packed = [_pack([src[k] for k in small_names]) for src in (weights, grads, mom1, mom2)]
    meta = packed[0][1]
    rows_pad = -packed[0][0].shape[0] % 128
    bufs = [jnp.pad(p[0], ((0, rows_pad), (0, 0))) for p in packed]
    outs = _adamw(bufs[0], bufs[1], bufs[2], bufs[3], "adamw_small", sc_gin[4])
    done.append(outs[2][0:1, :])
    for dst, o in zip((delta, new_m, new_v), outs):
        for k, a in zip(small_names, _unpack(o, meta)):
            dst[k] = a
    landed = _split_wait(sc_gin, jnp.concatenate(done, axis=0), _sc_copies, "rs_gin_scatter_wait")[1]
    g_gin_shard = _rs_pair_gather(_rs_final_sum(place, landed, p32_gin, "rs_gin_final_sum"), "rs_gin_pair_gather")
    grads["gla_w_in"] = g_gin_shard[:_GIN_ROWS].T[None]
    big_adamw("gla_w_in", sc_gin[4])

    return (loss, grad_x, *[grads[k] for k in names], *[delta[k] for k in names], *[new_m[k] for k in names],
            *[new_v[k] for k in names])
```

```python
import functools

import jax
import jax.numpy as jnp
from jax import lax
from jax.experimental import pallas as pl
from jax.experimental.pallas import tpu as pltpu

F32 = jnp.float32
BF16 = jnp.bfloat16
MESH = pl.DeviceIdType.MESH

EPS = 1e-6
D = 1024
N_MOD = 6
HEADS = 4
DK = 128
DV = 256
KEY = HEADS * DK
RANK = 16
TAU = 16.0
CH = 64
GRID_W = 64
HID = 2560
GLA_IN = 2 * KEY + 2 * D + 2 * RANK
GLA_IN_PAD = 3200
Q_SCALE = DK ** -0.5
N_CHIPS = 4
N_DEV = 8

ADAM_LR = 0.001
ADAM_B1 = 0.9
ADAM_B2 = 0.999
ADAM_EPS = 1e-08
ADAM_WD = 0.01
ADAM_STEP = 10

VMEM_LIMIT = 56 * 1024 * 1024


def _params(sem):
    return pltpu.CompilerParams(dimension_semantics=sem, vmem_limit_bytes=VMEM_LIMIT)


def _tile(n, pref, mult=8):
    if n <= pref:
        return n
    for t in range(pref - pref % mult, 0, -mult):
        if n % t == 0:
            return t
    raise ValueError((n, pref, mult))


_NN = (((1,), (0,)), ((), ()))
_NT = (((1,), (1,)), ((), ()))
_TN = (((0,), (0,)), ((), ()))


def _dot(a, b, dims=_NN):
    return lax.dot_general(a.astype(BF16), b.astype(BF16), dims, preferred_element_type=F32)


def _sigmoid(x):
    return 1.0 / (1.0 + jnp.exp(-x))


def _rowsum(x):
    return jnp.sum(x, axis=0, keepdims=True)


def _mm(a, b, form, out_dtype, name, tm, tn):
    if form == "tn":
        K, M = a.shape
    else:
        M, K = a.shape
    N = b.shape[0] if form == "nt" else b.shape[1]
    tm = _tile(M, tm, 128)
    tn = _tile(N, tn, 128)
    dims = {"nn": _NN, "nt": _NT, "tn": _TN}[form]

    def body(a_ref, b_ref, o_ref):
        o_ref[...] = _dot(a_ref[...], b_ref[...], dims).astype(o_ref.dtype)

    if form == "tn":
        a_spec = pl.BlockSpec((K, tm), lambda i, j: (0, i))
    else:
        a_spec = pl.BlockSpec((tm, K), lambda i, j: (i, 0))
    if form == "nt":
        b_spec = pl.BlockSpec((tn, K), lambda i, j: (j, 0))
    else:
        b_spec = pl.BlockSpec((K, tn), lambda i, j: (0, j))
    return pl.pallas_call(
        body,
        name=name,
        grid=(M // tm, N // tn),
        in_specs=[a_spec, b_spec],
        out_specs=pl.BlockSpec((tm, tn), lambda i, j: (i, j)),
        out_shape=jax.ShapeDtypeStruct((M, N), out_dtype),
        compiler_params=_params(("parallel", "parallel")),
    )(a, b)


def _mm_nt_w(a, wg, off, rows, name, tm, out_dtype):
    m = a.shape[0]
    tm = _tile(m, tm, 128)
    if N_CHIPS * rows <= D:

        def body_small(a_ref, w_ref, o_ref):
            av = a_ref[...]
            for s in range(N_CHIPS):
                o_ref[:, s * rows:(s + 1) * rows] = _dot(av, w_ref[s], _NT).astype(o_ref.dtype)

        return pl.pallas_call(
            body_small, name=name, grid=(m // tm,),
            in_specs=[pl.BlockSpec((tm, D), lambda i: (i, 0)),
                      pl.BlockSpec((N_CHIPS, rows, D), lambda i: (0, off // rows, 0))],
            out_specs=pl.BlockSpec((tm, N_CHIPS * rows), lambda i: (i, 0)),
            out_shape=jax.ShapeDtypeStruct((m, N_CHIPS * rows), out_dtype),
            compiler_params=_params(("parallel",)),
        )(a, wg)

    def body(a_ref, w_ref, o_ref):
        o_ref[...] = _dot(a_ref[...], w_ref[0], _NT).astype(o_ref.dtype)

    return pl.pallas_call(
        body, name=name, grid=(m // tm, N_CHIPS),
        in_specs=[pl.BlockSpec((tm, D), lambda i, s: (i, 0)),
                  pl.BlockSpec((1, rows, D), lambda i, s: (s, off // rows, 0))],
        out_specs=pl.BlockSpec((tm, rows), lambda i, s: (i, s)),
        out_shape=jax.ShapeDtypeStruct((m, N_CHIPS * rows), out_dtype),
        compiler_params=_params(("parallel", "parallel")),
    )(a, wg)


def _mm_nn_w(a3, wg, off, rows, name, tm, tn):
    parts, m, kp = a3.shape
    assert parts * kp == N_CHIPS * rows
    tm = _tile(m, tm, 128)
    cuts = sorted({s * rows for s in range(N_CHIPS + 1)} | {p * kp for p in range(parts + 1)})
    pieces = [(k0 // kp, k0 % kp, k0 // rows, k0 % rows, k1 - k0) for k0, k1 in zip(cuts[:-1], cuts[1:])]

    def body(a_ref, w_ref, o_ref):
        acc = None
        for p, a0, s, r0, width in pieces:
            term = _dot(a_ref[p, :, a0:a0 + width], w_ref[s, r0:r0 + width, :])
            acc = term if acc is None else acc + term
        o_ref[...] = acc

    return pl.pallas_call(
        body, name=name, grid=(m // tm, D // tn),
        in_specs=[pl.BlockSpec((parts, tm, kp), lambda i, j: (0, i, 0)),
                  pl.BlockSpec((N_CHIPS, rows, tn), lambda i, j: (0, off // rows, j))],
        out_specs=pl.BlockSpec((tm, tn), lambda i, j: (i, j)),
        out_shape=jax.ShapeDtypeStruct((m, D), F32),
        compiler_params=_params(("parallel", "parallel")),
    )(a3, wg)


def _mm_dw(a3, b, g_prev, off, rows, name, tm):
    parts, ntok, cdim = a3.shape
    assert parts * cdim == N_CHIPS * rows and cdim % tm == 0 and rows % tm == 0 and off % tm == 0

    def body(a_ref, b_ref, *rest):
        rest[-1][0] = _dot(a_ref[0], b_ref[...], _TN)

    in_specs = [pl.BlockSpec((1, ntok, tm), lambda i: ((i * tm) // cdim, 0, ((i * tm) % cdim) // tm)),
                pl.BlockSpec((ntok, D), lambda i: (0, 0))]
    args = [a3, b]
    aliases = {}
    if g_prev is not None:
        in_specs.append(pl.BlockSpec(memory_space=pl.ANY))
        args.append(g_prev)
        aliases = {2: 0}
    return pl.pallas_call(
        body, name=name, grid=(N_CHIPS * rows // tm,),
        in_specs=in_specs,
        out_specs=pl.BlockSpec((1, tm, D), lambda i: ((i * tm) // rows, (off + (i * tm) % rows) // tm, 0)),
        out_shape=jax.ShapeDtypeStruct((N_CHIPS, _MAIN_TOTAL, D), F32),
        input_output_aliases=aliases,
        compiler_params=_params(("parallel",)),
    )(*args)


def _mod_fwd(h, gain, shift, scale, tpb_rows, name, y=None, gate=None):
    n = h.shape[0]
    tt = _tile(tpb_rows, 256)
    tpb = tpb_rows // tt
    has_res = y is not None

    def body(*refs):
        if has_res:
            h_ref, y_ref, gate_ref, gain_ref, sh_ref, sc_ref, hout_ref, hn_ref = refs
            hv = h_ref[...] + gate_ref[0] * y_ref[...]
            hout_ref[...] = hv
        else:
            h_ref, gain_ref, sh_ref, sc_ref, hn_ref = refs
            hv = h_ref[...]
        r = lax.rsqrt(jnp.mean(hv * hv, axis=-1, keepdims=True) + EPS)
        hn = (hv * r) * gain_ref[...] * (1.0 + sc_ref[0]) + sh_ref[0]
        hn_ref[...] = hn.astype(BF16)

    row = pl.BlockSpec((tt, D), lambda i: (i, 0))
    per_b = pl.BlockSpec((1, 1, D), lambda i: (i // tpb, 0, 0))
    vec = pl.BlockSpec((1, D), lambda i: (0, 0))
    if has_res:
        in_specs = [row, row, per_b, vec, per_b, per_b]
        args = (h, y, gate, gain, shift, scale)
        out_specs = [row, row]
        out_shape = [jax.ShapeDtypeStruct((n, D), F32), jax.ShapeDtypeStruct((n, D), BF16)]
    else:
        in_specs = [row, vec, per_b, per_b]
        args = (h, gain, shift, scale)
        out_specs = row
        out_shape = jax.ShapeDtypeStruct((n, D), BF16)
    return pl.pallas_call(
        body, name=name, grid=(n // tt,), in_specs=in_specs, out_specs=out_specs, out_shape=out_shape,
        compiler_params=_params(("parallel",)),
    )(*args)


def _mod_bwd(h_in, dhn, gain, scale, tpb_rows, name, dhn_row0=0, dh_out=None, y_prev=None, gate_prev=None,
             need_dh=True):
    n = h_in.shape[0]
    nb = n // tpb_rows
    tt = _tile(tpb_rows, 256)
    tpb = tpb_rows // tt
    off = dhn_row0 // tt
    assert dhn_row0 % tt == 0
    has_out = dh_out is not None
    has_prev = y_prev is not None

    def body(*refs):
        it = iter(refs)
        h_ref, dhn_ref, gain_ref, sc_ref = next(it), next(it), next(it), next(it)
        dho_ref = next(it) if has_out else None
        yp_ref, gp_ref = (next(it), next(it)) if has_prev else (None, None)
        dh_ref = next(it) if need_dh else None
        dsc_ref, dsh_ref, dgain_ref = next(it), next(it), next(it)
        dyp_ref, dgp_ref = (next(it), next(it)) if has_prev else (None, None)
        i = pl.program_id(0)

        @pl.when(i == 0)
        def _():
            dgain_ref[...] = jnp.zeros_like(dgain_ref)

        @pl.when(i % tpb == 0)
        def _():
            dsc_ref[...] = jnp.zeros_like(dsc_ref)
            dsh_ref[...] = jnp.zeros_like(dsh_ref)
            if has_prev:
                dgp_ref[...] = jnp.zeros_like(dgp_ref)

        hv = h_ref[...]
        r = lax.rsqrt(jnp.mean(hv * hv, axis=-1, keepdims=True) + EPS)
        y = hv * r
        gain_v = gain_ref[...]
        g = dhn_ref[...].astype(F32)
        dsh_ref[0] += _rowsum(g)
        dsc_ref[0] += _rowsum(g * (y * gain_v))
        drn = g * (1.0 + sc_ref[0])
        dgain_ref[...] += _rowsum(drn * y)
        if need_dh:
            dy = drn * gain_v
            dh = r * (dy - y * jnp.mean(dy * y, axis=-1, keepdims=True))
            if has_out:
                dh = dh + dho_ref[...]
            dh_ref[...] = dh
            if has_prev:
                dyp_ref[...] = (dh * gp_ref[0]).astype(BF16)
                dgp_ref[0] += _rowsum(dh * yp_ref[...])

    row = pl.BlockSpec((tt, D), lambda i: (i, 0))
    row_off = pl.BlockSpec((tt, D), lambda i: (i + off, 0))
    per_b = pl.BlockSpec((1, 1, D), lambda i: (i // tpb, 0, 0))
    vec = pl.BlockSpec((1, D), lambda i: (0, 0))
    in_specs = [row, row_off, vec, per_b]
    args = [h_in, dhn, gain, scale]
    if has_out:
        in_specs.append(row)
        args.append(dh_out)
    if has_prev:
        in_specs += [row, per_b]
        args += [y_prev, gate_prev]
    out_specs, out_shape, names = [], [], []
    if need_dh:
        out_specs.append(row)
        out_shape.append(jax.ShapeDtypeStruct((n, D), F32))
        names.append("dh")
    for nm in ("dscale", "dshift"):
        out_specs.append(per_b)
        out_shape.append(jax.ShapeDtypeStruct((nb, 1, D), F32))
        names.append(nm)
    out_specs.append(vec)
    out_shape.append(jax.ShapeDtypeStruct((1, D), F32))
    names.append("dgain")
    if has_prev:
        out_specs += [row, per_b]
        out_shape += [jax.ShapeDtypeStruct((n, D), BF16), jax.ShapeDtypeStruct((nb, 1, D), F32)]
        names += ["dy_prev", "dgate_prev"]
    outs = pl.pallas_call(
        body, name=name, grid=(n // tt,), in_specs=in_specs, out_specs=out_specs, out_shape=out_shape,
        compiler_params=_params(("arbitrary",)),
    )(*args)
    return dict(zip(names, outs))


def _final(h, f, gate, gain, tgt, tpb_rows):
    n = h.shape[0]
    nb = n // tpb_rows
    tt = _tile(tpb_rows, 256)
    tpb = tpb_rows // tt

    def body(h_ref, f_ref, gate_ref, gain_ref, tgt_ref, loss_ref, dh_ref, df_ref, dgate_ref, dgain_ref):
        i = pl.program_id(0)

        @pl.when(i == 0)
        def _():
            loss_ref[...] = jnp.zeros_like(loss_ref)
            dgain_ref[...] = jnp.zeros_like(dgain_ref)

        @pl.when(i % tpb == 0)
        def _():
            dgate_ref[...] = jnp.zeros_like(dgate_ref)

        fv = f_ref[...]
        gate_v = gate_ref[0]
        hv = h_ref[...] + gate_v * fv
        r = lax.rsqrt(jnp.mean(hv * hv, axis=-1, keepdims=True) + EPS)
        y = hv * r
        gain_v = gain_ref[...]
        e = y * gain_v - tgt_ref[...]
        s = jnp.sum(_rowsum(e * e), axis=1, keepdims=True) * (0.5 / D)
        loss_ref[...] += jnp.broadcast_to(s, loss_ref.shape)
        dout = e * (1.0 / D)
        dgain_ref[...] += _rowsum(dout * y)
        dy = dout * gain_v
        dh = r * (dy - y * jnp.mean(dy * y, axis=-1, keepdims=True))
        dh_ref[...] = dh
        df_ref[...] = (dh * gate_v).astype(BF16)
        dgate_ref[0] += _rowsum(dh * fv)

    row = pl.BlockSpec((tt, D), lambda i: (i, 0))
    per_b = pl.BlockSpec((1, 1, D), lambda i: (i // tpb, 0, 0))
    vec = pl.BlockSpec((1, D), lambda i: (0, 0))
    return pl.pallas_call(
        body, name="final_loss", grid=(n // tt,),
        in_specs=[row, row, per_b, vec, row],
        out_specs=[pl.BlockSpec((1, 128), lambda i: (0, 0)), row, row, per_b, vec],
        out_shape=[jax.ShapeDtypeStruct((1, 128), F32), jax.ShapeDtypeStruct((n, D), F32),
                   jax.ShapeDtypeStruct((n, D), BF16), jax.ShapeDtypeStruct((nb, 1, D), F32),
                   jax.ShapeDtypeStruct((1, D), F32)],
        compiler_params=_params(("arbitrary",)),
    )(h, f, gate, gain, tgt)


def _shift_dn(x, s):
    return jnp.concatenate([jnp.zeros((s, x.shape[1]), x.dtype), x[: x.shape[0] - s]], axis=0)


def _shift_up(x, s):
    return jnp.concatenate([x[s:], jnp.zeros((s, x.shape[1]), x.dtype)], axis=0)


def _row_dn1(x):
    t = lax.broadcasted_iota(jnp.int32, x.shape, 0)
    return jnp.where(t % GRID_W == 0, 0.0, pltpu.roll(x, 1, 0))


def _row_up1(x):
    t = lax.broadcasted_iota(jnp.int32, x.shape, 0)
    return jnp.where(t % GRID_W == GRID_W - 1, 0.0, pltpu.roll(x, x.shape[0] - 1, 0))


def _silu(x):
    return x * _sigmoid(x)


def _dsilu(x):
    s = _sigmoid(x)
    return s * (1.0 + x * (1.0 - s))


def _conv_cols(x, w_ref):
    return _shift_dn(x, GRID_W) * w_ref[0:1, :] + x * w_ref[1:2, :] + _shift_up(x, GRID_W) * w_ref[2:3, :]


def _conv_cols_bwd(x, du, w_ref, dw_ref, db_ref):
    db_ref[...] += _rowsum(du)
    dw_ref[0:1, :] += _rowsum(du * _shift_dn(x, GRID_W))
    dw_ref[1:2, :] += _rowsum(du * x)
    dw_ref[2:3, :] += _rowsum(du * _shift_up(x, GRID_W))
    return _shift_up(du, GRID_W) * w_ref[0:1, :] + du * w_ref[1:2, :] + _shift_dn(du, GRID_W) * w_ref[2:3, :]


def _ffn_mid_fwd(u0, cw, cb, nb, t, name):
    nc = HID // 128

    def body(ua_ref, ug_ref, wa_ref, wg_ref, ba_ref, bg_ref, z_ref):
        a = _conv_cols(ua_ref[...].astype(F32), wa_ref) + ba_ref[...]
        gt = _conv_cols(ug_ref[...].astype(F32), wg_ref) + bg_ref[...]
        z_ref[...] = (a * _silu(gt)).astype(BF16)

    col = lambda rows, part: pl.BlockSpec((rows, 128), lambda j, b: (b if rows == t else 0, part * nc + j))
    return pl.pallas_call(
        body, name=name, grid=(nc, nb),
        in_specs=[col(t, 0), col(t, 1), col(3, 0), col(3, 1), col(1, 0), col(1, 1)],
        out_specs=pl.BlockSpec((t, 128), lambda j, b: (b, j)),
        out_shape=jax.ShapeDtypeStruct((nb * t, HID), BF16),
        compiler_params=_params(("parallel", "parallel")),
    )(u0, u0, cw, cw, cb, cb)


def _ffn_mid_bwd(u0, cw, cb, dz, nb, t, name):
    nc = HID // 128

    def body(ua_ref, ug_ref, wa_ref, wg_ref, ba_ref, bg_ref, dz_ref, du_ref, dw_ref, db_ref):
        b = pl.program_id(1)

        @pl.when(b == 0)
        def _():
            dw_ref[...] = jnp.zeros_like(dw_ref)
            db_ref[...] = jnp.zeros_like(db_ref)

        xa = ua_ref[...].astype(F32)
        xg = ug_ref[...].astype(F32)
        a = _conv_cols(xa, wa_ref) + ba_ref[...]
        gt = _conv_cols(xg, wg_ref) + bg_ref[...]
        dzv = dz_ref[...].astype(F32)
        du_ref[0] = _conv_cols_bwd(xa, dzv * _silu(gt), wa_ref, dw_ref.at[0], db_ref.at[0]).astype(BF16)
        du_ref[1] = _conv_cols_bwd(xg, dzv * a * _dsilu(gt), wg_ref, dw_ref.at[1], db_ref.at[1]).astype(BF16)

    col = lambda rows, part: pl.BlockSpec((rows, 128), lambda j, b: (b if rows == t else 0, part * nc + j))
    return pl.pallas_call(
        body, name=name, grid=(nc, nb),
        in_specs=[col(t, 0), col(t, 1), col(3, 0), col(3, 1), col(1, 0), col(1, 1),
                  pl.BlockSpec((t, 128), lambda j, b: (b, j))],
        out_specs=[pl.BlockSpec((2, t, 128), lambda j, b: (0, b, j)), pl.BlockSpec((2, 3, 128), lambda j, b: (0, 0, j)),
                   pl.BlockSpec((2, 1, 128), lambda j, b: (0, 0, j))],
        out_shape=[jax.ShapeDtypeStruct((2, nb * t, HID), BF16), jax.ShapeDtypeStruct((2, 3, HID), F32),
                   jax.ShapeDtypeStruct((2, 1, HID), F32)],
        compiler_params=_params(("parallel", "arbitrary")),
    )(u0, u0, cw, cw, cb, cb, dz)


def _sc_mid_fwd(p, cw, nb, t):
    nc = D // 128

    def body(bg_ref, cg_ref, v_ref, w_ref, y_ref):
        cv = cg_ref[...].astype(F32) * v_ref[...].astype(F32)
        cc = _row_dn1(cv) * w_ref[0:1, :] + cv * w_ref[1:2, :] + _row_up1(cv) * w_ref[2:3, :]
        y_ref[...] = (bg_ref[...].astype(F32) * cc).astype(BF16)

    part = lambda k: pl.BlockSpec((t, 128), lambda j, b: (b, k * nc + j))
    return pl.pallas_call(
        body, name="sc_mid_fwd", grid=(nc, nb),
        in_specs=[part(0), part(1), part(2), pl.BlockSpec((3, 128), lambda j, b: (0, j))],
        out_specs=pl.BlockSpec((t, 128), lambda j, b: (b, j)),
        out_shape=jax.ShapeDtypeStruct((nb * t, D), BF16),
        compiler_params=_params(("parallel", "parallel")),
    )(p, p, p, cw)


def _sc_mid_bwd(p, cw, dyb, nb, t):
    nc = D // 128

    def body(bg_ref, cg_ref, v_ref, w_ref, dy_ref, dp_ref, dw_ref):
        b = pl.program_id(1)

        @pl.when(b == 0)
        def _():
            dw_ref[...] = jnp.zeros_like(dw_ref)

        w0, w1, w2 = w_ref[0:1, :], w_ref[1:2, :], w_ref[2:3, :]
        cg, v = cg_ref[...].astype(F32), v_ref[...].astype(F32)
        cv = cg * v
        cvd = _row_dn1(cv)
        cvu = _row_up1(cv)
        cc = cvd * w0 + cv * w1 + cvu * w2
        dy = dy_ref[...].astype(F32)
        dcc = dy * bg_ref[...].astype(F32)
        dw_ref[0:1, :] += _rowsum(dcc * cvd)
        dw_ref[1:2, :] += _rowsum(dcc * cv)
        dw_ref[2:3, :] += _rowsum(dcc * cvu)
        dcv = _row_up1(dcc) * w0 + dcc * w1 + _row_dn1(dcc) * w2
        dp_ref[0] = (dy * cc).astype(BF16)
        dp_ref[1] = (dcv * v).astype(BF16)
        dp_ref[2] = (dcv * cg).astype(BF16)

    part = lambda k: pl.BlockSpec((t, 128), lambda j, b: (b, k * nc + j))
    return pl.pallas_call(
        body, name="sc_mid_bwd", grid=(nc, nb),
        in_specs=[part(0), part(1), part(2), pl.BlockSpec((3, 128), lambda j, b: (0, j)),
                  pl.BlockSpec((t, 128), lambda j, b: (b, j))],
        out_specs=[pl.BlockSpec((3, t, 128), lambda j, b: (0, b, j)), pl.BlockSpec((3, 128), lambda j, b: (0, j))],
        out_shape=[jax.ShapeDtypeStruct((3, nb * t, D), BF16), jax.ShapeDtypeStruct((3, D), F32)],
        compiler_params=_params(("parallel", "arbitrary")),
    )(p, p, p, cw, dyb)


def _gla_decay_fwd(p_all, w2, b2):
    n = p_all.shape[0]
    tt = _tile(n, 512)

    def body(a_ref, w_ref, b_ref, la_ref):
        z = _dot(a_ref[...], w_ref[...]) + b_ref[...]
        la_ref[...] = (jnp.minimum(z, 0.0) - jnp.log(1.0 + jnp.exp(-jnp.abs(z)))) * (1.0 / TAU)

    return pl.pallas_call(
        body, name="gla_decay_fwd", grid=(n // tt,),
        in_specs=[pl.BlockSpec((tt, 128), lambda i: (i, (2 * KEY + 2 * D) // 128)),
                  pl.BlockSpec((128, 2 * KEY), lambda i: (0, 0)), pl.BlockSpec((1, 2 * KEY), lambda i: (0, 0))],
        out_specs=pl.BlockSpec((tt, 2 * KEY), lambda i: (i, 0)),
        out_shape=jax.ShapeDtypeStruct((n, 2 * KEY), F32),
        compiler_params=_params(("parallel",)),
    )(p_all, w2, b2)


def _gla_blocks(nb, nm, ncx):
    def main_idx(d, i):
        return jnp.clip(jnp.where(d == 0, i - ncx, nm - 1 - (i - ncx)), 0, nm - 1)

    def rowblk(d, b, i):
        cidx = jnp.where(d == 0, i, ncx - 1 - i)
        return jnp.where(i < ncx, nb * nm + b * ncx + cidx, b * nm + main_idx(d, i))

    def mainblk(d, b, i):
        return b * nm + main_idx(d, i)

    return rowblk, mainblk


def _gla_mask(d):
    row = lax.broadcasted_iota(jnp.int32, (CH, CH), 0)
    col = lax.broadcasted_iota(jnp.int32, (CH, CH), 1)
    diff = jnp.where(d == 0, row - col, col - row)
    mask = diff >= 0
    return mask, jnp.where(mask, 1.0, 0.0).astype(BF16), jnp.where(diff <= 0, 1.0, 0.0).astype(BF16)


def _tri_sum(m01, x):
    w = x.shape[1]
    hi = x.astype(BF16)
    r1 = x - hi.astype(F32)
    mid = r1.astype(BF16)
    lo = (r1 - mid.astype(F32)).astype(BF16)
    s = lax.dot_general(m01, jnp.concatenate([hi, mid, lo], axis=1), _NN, preferred_element_type=F32)
    return s[:, :w] + s[:, w:2 * w] + s[:, 2 * w:]


def _gla_chunk(q, k, g, bc):
    bl = _rowsum(g)
    eq = jnp.exp(bc)
    ek = jnp.exp(-bc)
    ed = jnp.exp(bl - bc)
    return bl, eq, ek, ed, q * Q_SCALE * eq, k * ek, k * ed


def _gla_scan_fwd(p_all, la_all, nb, t, tc):
    nm, ncx = t // CH, tc // CH
    nst = nm + ncx
    rowblk, mainblk = _gla_blocks(nb, nm, ncx)

    def body(*refs):
        ins, (o_refs, ss_refs, st_ref) = refs[:8], (refs[8:10], refs[10:12], refs[12])
        i = pl.program_id(1)

        @pl.when(i == 0)
        def _():
            st_ref[...] = jnp.zeros_like(st_ref)

        loaded = [r[...] for r in ins]
        states = [st_ref[j] for j in range(2 * HEADS)]
        outs, new_states = [[], []], []
        for d in range(2):
            q_all, k_all, v_all, g_all = loaded[4 * d:4 * d + 4]
            mask, m01, _ = _gla_mask(d)
            bc_all = _tri_sum(m01, g_all)
            for h in range(HEADS):
                ksl = slice(h * DK, (h + 1) * DK)
                v = v_all[:, h * DV:(h + 1) * DV]
                st = states[d * HEADS + h]
                bl, _, _, _, qs, ks, kd = _gla_chunk(q_all[:, ksl], k_all[:, ksl], g_all[:, ksl], bc_all[:, ksl])
                att = jnp.where(mask, _dot(qs, ks, _NT), 0.0)
                outs[d].append(_dot(qs, st, _NT) + _dot(att, v))
                new_states.append(st * jnp.exp(bl) + _dot(v, kd, _TN))
        for d in range(2):
            o_refs[d][...] = jnp.concatenate(outs[d], axis=1)
            for h in range(HEADS):
                ss_refs[d][0, 0, h] = states[d * HEADS + h]
                st_ref[d * HEADS + h] = new_states[d * HEADS + h]

    def in_specs(d):
        return [pl.BlockSpec((CH, KEY), lambda b, i: (rowblk(d, b, i), 0)),
                pl.BlockSpec((CH, KEY), lambda b, i: (rowblk(d, b, i), 1)),
                pl.BlockSpec((CH, D), lambda b, i: (rowblk(d, b, i), 1)),
                pl.BlockSpec((CH, KEY), lambda b, i: (rowblk(d, b, i), d))]

    outs = pl.pallas_call(
        body, name="gla_scan_fwd", grid=(nb, nst),
        in_specs=in_specs(0) + in_specs(1),
        out_specs=[pl.BlockSpec((CH, D), lambda b, i: (mainblk(0, b, i), 0)),
                   pl.BlockSpec((CH, D), lambda b, i: (mainblk(1, b, i), 0)),
                   pl.BlockSpec((1, 1, HEADS, DV, DK), lambda b, i: (b, i, 0, 0, 0)),
                   pl.BlockSpec((1, 1, HEADS, DV, DK), lambda b, i: (b, i, 0, 0, 0))],
        out_shape=[jax.ShapeDtypeStruct((nb * t, D), F32)] * 2
        + [jax.ShapeDtypeStruct((nb, nst, HEADS, DV, DK), F32)] * 2,
        scratch_shapes=[pltpu.VMEM((2 * HEADS, DV, DK), F32)],
        compiler_params=_params(("parallel", "arbitrary")),
    )(*([p_all, p_all, p_all, la_all] * 2))
    return outs[:2], outs[2:]


def _gla_scan_bwd(p_all, la_all, do, ss, nb, t, tc, after):
    nm, ncx = t // CH, tc // CH
    nst = nm + ncx
    ntot = nb * (t + tc)
    rowblk, mainblk = _gla_blocks(nb, nm, ncx)

    def body(*refs):
        ins, outs, dst_ref = refs[:12], refs[13:21], refs[21]
        ip = pl.program_id(1)
        i = nst - 1 - ip

        @pl.when(ip == 0)
        def _():
            dst_ref[...] = jnp.zeros_like(dst_ref)

        live = jnp.where(i >= ncx, 1.0, 0.0)
        loaded = [[r[...] for r in ins[6 * d:6 * d + 5]] for d in range(2)]
        states = [ins[6 * d + 5][0, 0, h] for d in range(2) for h in range(HEADS)]
        dstates = [dst_ref[j] for j in range(2 * HEADS)]
        results, new_dstates = [], []
        for d in range(2):
            q_all, k_all, v_all, g_all, do_all = loaded[d]
            do_all = do_all * live
            mask, m01, m01_t = _gla_mask(d)
            bc_all = _tri_sum(m01, g_all)
            dqs_l, dks_l, dvs_l, dbs_l, dbls_l = [], [], [], [], []
            for h in range(HEADS):
                ksl = slice(h * DK, (h + 1) * DK)
                vsl = slice(h * DV, (h + 1) * DV)
                bl, eq, ek, ed, qs, ks, kd = _gla_chunk(q_all[:, ksl], k_all[:, ksl], g_all[:, ksl], bc_all[:, ksl])
                st, dst, v, dov = states[d * HEADS + h], dstates[d * HEADS + h], v_all[:, vsl], do_all[:, vsl]
                att = jnp.where(mask, _dot(qs, ks, _NT), 0.0)
                datt = jnp.where(mask, _dot(dov, v, _NT), 0.0)
                dqs = _dot(dov, st) + _dot(datt, ks)
                dks = _dot(datt, qs, _TN)
                dvs_l.append(_dot(att, dov, _TN) + _dot(kd, dst, _NT))
                dkd = _dot(v, dst)
                e = jnp.exp(bl)
                dbls_l.append(e * _rowsum(st * dst) + _rowsum(dkd * kd))
                new_dstates.append(_dot(dov, qs, _TN) + dst * e)
                dqs_l.append(dqs * eq * Q_SCALE)
                dks_l.append(dks * ek + dkd * ed)
                dbs_l.append(dqs * qs - dks * ks - dkd * kd)
            results.append((jnp.concatenate(dqs_l, axis=1), jnp.concatenate(dks_l, axis=1),
                            jnp.concatenate(dvs_l, axis=1),
                            _tri_sum(m01_t, jnp.concatenate(dbs_l, axis=1)) + jnp.concatenate(dbls_l, axis=1)))
        for d in range(2):
            for k in range(4):
                outs[4 * d + k][...] = results[d][k]
        for j in range(2 * HEADS):
            dst_ref[j] = new_dstates[j]

    def in_specs(d):
        return [pl.BlockSpec((CH, KEY), lambda b, ip: (rowblk(d, b, nst - 1 - ip), 0)),
                pl.BlockSpec((CH, KEY), lambda b, ip: (rowblk(d, b, nst - 1 - ip), 1)),
                pl.BlockSpec((CH, D), lambda b, ip: (rowblk(d, b, nst - 1 - ip), 1)),
                pl.BlockSpec((CH, KEY), lambda b, ip: (rowblk(d, b, nst - 1 - ip), d)),
                pl.BlockSpec((CH, D), lambda b, ip: (mainblk(d, b, nst - 1 - ip), 0)),
                pl.BlockSpec((1, 1, HEADS, DV, DK), lambda b, ip: (b, nst - 1 - ip, 0, 0, 0))]

    def out_specs(d):
        row = lambda width: pl.BlockSpec((CH, width), lambda b, ip: (rowblk(d, b, nst - 1 - ip), 0))
        return [row(KEY), row(KEY), row(D), row(KEY)]

    shapes = [jax.ShapeDtypeStruct((ntot, KEY), F32), jax.ShapeDtypeStruct((ntot, KEY), F32),
              jax.ShapeDtypeStruct((ntot, D), F32), jax.ShapeDtypeStruct((ntot, KEY), F32)]
    outs = pl.pallas_call(
        body, name="gla_scan_bwd", grid=(nb, nst),
        in_specs=in_specs(0) + in_specs(1) + [pl.BlockSpec(memory_space=pl.ANY)],
        out_specs=out_specs(0) + out_specs(1),
        out_shape=shapes * 2,
        scratch_shapes=[pltpu.VMEM((2 * HEADS, DV, DK), F32)],
        compiler_params=_params(("parallel", "arbitrary")),
    )(p_all, p_all, p_all, la_all, do, ss[0], p_all, p_all, p_all, la_all, do, ss[1], after)
    return [[outs[k], outs[4 + k]] for k in range(4)]


def _gla_post_fwd(o2, p_all, head_gain, n):
    tt = _tile(n, 256)

    def body(of_ref, ob_ref, g_ref, hg_ref, y_ref):
        o = of_ref[...] + ob_ref[...]
        gv = g_ref[...]
        hg = hg_ref[...]
        for h in range(HEADS):
            oh = o[:, h * DV:(h + 1) * DV]
            r = lax.rsqrt(jnp.mean(oh * oh, axis=-1, keepdims=True) + EPS)
            y_ref[:, h * DV:(h + 1) * DV] = ((oh * r) * hg * _silu(gv[:, h * DV:(h + 1) * DV])).astype(BF16)

    row = pl.BlockSpec((tt, D), lambda i: (i, 0))
    return pl.pallas_call(
        body, name="gla_post_fwd", grid=(n // tt,),
        in_specs=[row, row, pl.BlockSpec((tt, D), lambda i: (i, 2)), pl.BlockSpec((1, DV), lambda i: (0, 0))],
        out_specs=row,
        out_shape=jax.ShapeDtypeStruct((n, D), BF16),
        compiler_params=_params(("parallel",)),
    )(o2[0], o2[1], p_all, head_gain)


def _gla_post_bwd(o2, p_all, head_gain, dyb, n):
    tt = _tile(n, 256)

    def body(of_ref, ob_ref, g_ref, hg_ref, dy_ref, do_ref, dg_ref, dhg_ref):
        i = pl.program_id(0)

        @pl.when(i == 0)
        def _():
            dhg_ref[...] = jnp.zeros_like(dhg_ref)

        o = of_ref[...] + ob_ref[...]
        gv = g_ref[...]
        hg = hg_ref[...]
        dy = dy_ref[...]
        acc = jnp.zeros((1, DV), F32)
        for h in range(HEADS):
            sl = slice(h * DV, (h + 1) * DV)
            oh = o[:, sl]
            r = lax.rsqrt(jnp.mean(oh * oh, axis=-1, keepdims=True) + EPS)
            on = oh * r
            gh = gv[:, sl]
            dyh = dy[:, sl]
            dg_ref[:, sl] = dyh * (on * hg) * _dsilu(gh)
            dog = dyh * _silu(gh)
            acc = acc + _rowsum(dog * on)
            don = dog * hg
            do_ref[:, sl] = r * (don - on * jnp.mean(don * on, axis=-1, keepdims=True))
        dhg_ref[...] += acc

    return pl.pallas_call(
        body, name="gla_post_bwd", grid=(n // tt,),
        in_specs=[pl.BlockSpec((tt, D), lambda i: (i, 0)), pl.BlockSpec((tt, D), lambda i: (i, 0)),
                  pl.BlockSpec((tt, D), lambda i: (i, 2)),
                  pl.BlockSpec((1, DV), lambda i: (0, 0)), pl.BlockSpec((tt, D), lambda i: (i, 0))],
        out_specs=[pl.BlockSpec((tt, D), lambda i: (i, 0)), pl.BlockSpec((tt, D), lambda i: (i, 0)),
                   pl.BlockSpec((1, DV), lambda i: (0, 0))],
        out_shape=[jax.ShapeDtypeStruct((n, D), F32), jax.ShapeDtypeStruct((n, D), F32),
                   jax.ShapeDtypeStruct((1, DV), F32)],
        compiler_params=_params(("arbitrary",)),
    )(o2[0], o2[1], p_all, head_gain, dyb)


def _gla_assemble(p_all, w2, b2, dq, dk, dv, dla, dgate, n):
    ntot = p_all.shape[0]
    tt = _tile(n, 128)
    nmain = n // tt
    assert ntot % tt == 0

    def body(a_ref, w_ref, b_ref, dqf_ref, dqb_ref, dkf_ref, dkb_ref, dvf_ref, dvb_ref, dlf_ref, dlb_ref, dg_ref,
             dp_ref, dw_ref, db_ref):
        i = pl.program_id(0)

        @pl.when(i == 0)
        def _():
            dw_ref[...] = jnp.zeros_like(dw_ref)
            db_ref[...] = jnp.zeros_like(db_ref)

        a = a_ref[...]
        w = w_ref[...]
        z = _dot(a, w) + b_ref[...]
        dla = jnp.concatenate([dlf_ref[...], dlb_ref[...]], axis=1)
        dz = dla * (1.0 / (1.0 + jnp.exp(z))) * (1.0 / TAU)
        dw_ref[...] += _dot(a, dz, _TN)
        db_ref[...] += _rowsum(dz)
        dp_ref[:, 0:KEY] = (dqf_ref[...] + dqb_ref[...]).astype(BF16)
        dp_ref[:, KEY:2 * KEY] = (dkf_ref[...] + dkb_ref[...]).astype(BF16)
        dp_ref[:, 2 * KEY:2 * KEY + D] = (dvf_ref[...] + dvb_ref[...]).astype(BF16)
        dp_ref[:, 2 * KEY + D:2 * KEY + 2 * D] = (dg_ref[...] * jnp.where(i < nmain, 1.0, 0.0)).astype(BF16)
        dp_ref[:, 2 * KEY + 2 * D:GLA_IN_PAD] = _dot(dz, w, _NT).astype(BF16)

    row = lambda width: pl.BlockSpec((tt, width), lambda i: (i, 0))
    return pl.pallas_call(
        body, name="gla_assemble", grid=(ntot // tt,),
        in_specs=[pl.BlockSpec((tt, 128), lambda i: (i, (2 * KEY + 2 * D) // 128)),
                  pl.BlockSpec((128, 2 * KEY), lambda i: (0, 0)), pl.BlockSpec((1, 2 * KEY), lambda i: (0, 0)),
                  row(KEY), row(KEY), row(KEY), row(KEY), row(D), row(D), row(KEY), row(KEY),
                  pl.BlockSpec((tt, D), lambda i: (jnp.minimum(i, nmain - 1), 0))],
        out_specs=[pl.BlockSpec((tt, GLA_IN_PAD), lambda i: (i, 0)), pl.BlockSpec((128, 2 * KEY), lambda i: (0, 0)),
                   pl.BlockSpec((1, 2 * KEY), lambda i: (0, 0))],
        out_shape=[jax.ShapeDtypeStruct((ntot, GLA_IN_PAD), BF16), jax.ShapeDtypeStruct((128, 2 * KEY), F32),
                   jax.ShapeDtypeStruct((1, 2 * KEY), F32)],
        compiler_params=_params(("arbitrary",)),
    )(p_all, w2, b2, dq[0], dq[1], dk[0], dk[1], dv[0], dv[1], dla[0], dla[1], dgate)


ADA_ROWS = 24
ADA_SH = N_MOD * D // N_CHIPS


def _ada_fwd(cvec, ada_w, ada_b_sh):
    def body(c_ref, w_ref, b_ref, o_ref):
        o_ref[0] = _dot(_silu(c_ref[...]), w_ref[0]) + b_ref[0]

    return pl.pallas_call(
        body, name="ada_fwd", grid=(2,),
        in_specs=[pl.BlockSpec((ADA_ROWS, D), lambda l: (0, 0)), pl.BlockSpec((1, D, ADA_SH), lambda l: (l, 0, 0)),
                  pl.BlockSpec((1, 1, ADA_SH), lambda l: (l, 0, 0))],
        out_specs=pl.BlockSpec((1, ADA_ROWS, ADA_SH), lambda l: (l, 0, 0)),
        out_shape=jax.ShapeDtypeStruct((2, ADA_ROWS, ADA_SH), F32),
        compiler_params=_params(("parallel",)),
    )(cvec, ada_w, ada_b_sh)


def _ada_bwd(cvec, ada_w, dmod_sh):
    def body(c_ref, w_ref, dm_ref, gw_ref, dc_ref):
        dm = dm_ref[0]
        gw_ref[0] = _dot(_silu(c_ref[...]), dm, _TN)
        dc_ref[0] = _dot(dm, w_ref[0], _NT)

    return pl.pallas_call(
        body, name="ada_bwd", grid=(2,),
        in_specs=[pl.BlockSpec((ADA_ROWS, D), lambda l: (0, 0)), pl.BlockSpec((1, D, ADA_SH), lambda l: (l, 0, 0)),
                  pl.BlockSpec((1, ADA_ROWS, ADA_SH), lambda l: (l, 0, 0))],
        out_specs=[pl.BlockSpec((1, D, ADA_SH), lambda l: (l, 0, 0)), pl.BlockSpec((1, ADA_ROWS, D), lambda l: (l, 0, 0))],
        out_shape=[jax.ShapeDtypeStruct((2, D, ADA_SH), F32), jax.ShapeDtypeStruct((2, ADA_ROWS, D), F32)],
        compiler_params=_params(("parallel",)),
    )(cvec, ada_w, dmod_sh)


def _sum_slots(x, name):
    s, r, _ = x.shape

    def body(x_ref, o_ref):
        acc = x_ref[0]
        for k in range(1, s):
            acc = acc + x_ref[k]
        o_ref[...] = acc

    return pl.pallas_call(
        body, name=name, out_shape=jax.ShapeDtypeStruct((r, 128), F32),
        in_specs=[pl.BlockSpec(memory_space=pltpu.VMEM)], out_specs=pl.BlockSpec(memory_space=pltpu.VMEM),
    )(x)


def _cctx_grad(dscc_parts, c_ctx):
    def body(p_ref, c_ref, o_ref):
        acc = p_ref[0]
        for k in range(1, N_CHIPS):
            acc = acc + p_ref[k]
        o_ref[...] = acc * _dsilu(c_ref[...])

    return pl.pallas_call(
        body, name="cctx_grad", out_shape=jax.ShapeDtypeStruct((8, 128), F32),
        in_specs=[pl.BlockSpec(memory_space=pltpu.VMEM)] * 2, out_specs=pl.BlockSpec(memory_space=pltpu.VMEM),
    )(dscc_parts, c_ctx)


def _adamw(w, g, m, v, name, after):
    r, cdim = w.shape
    tr = _tile(r, 256)
    c1 = 1.0 - ADAM_B1 ** ADAM_STEP
    c2 = 1.0 - ADAM_B2 ** ADAM_STEP

    def body(w_ref, g_ref, m_ref, v_ref, after_ref, d_ref, mo_ref, vo_ref):
        gv = g_ref[...]
        mn = ADAM_B1 * m_ref[...] + (1.0 - ADAM_B1) * gv
        vn = ADAM_B2 * v_ref[...] + (1.0 - ADAM_B2) * (gv * gv)
        mo_ref[...] = mn
        vo_ref[...] = vn
        d_ref[...] = -ADAM_LR * ((mn / c1) / (jnp.sqrt(vn / c2) + ADAM_EPS) + ADAM_WD * w_ref[...])

    spec = pl.BlockSpec((tr, cdim), lambda i: (i, 0))
    sds = jax.ShapeDtypeStruct((r, cdim), F32)
    return pl.pallas_call(
        body, name=name, grid=(r // tr,), in_specs=[spec] * 4 + [pl.BlockSpec(memory_space=pl.ANY)],
        out_specs=[spec] * 3, out_shape=[sds] * 3, compiler_params=_params(("parallel",)),
    )(w, g, m, v, after)


def _place():
    x, y, c = lax.axis_index("x"), lax.axis_index("y"), lax.axis_index("c")
    return x, y, c


def _allgather_small(blk, name):
    m_per, n = blk.shape

    def body(x_ref, out_ref, send_sems, recv_sems, local_sem):
        x, y, c = _place()
        me, sibling = (x, y, c), (x, y, 1 - c)
        chips = [(1 - x, y), (x, 1 - y), (1 - x, 1 - y)]

        def rows(px, py, pc):
            return out_ref.at[pl.ds((4 * px + 2 * py + pc) * m_per, m_per), :]

        def copy(k, block, to, src=None):
            return pltpu.make_async_remote_copy(
                src_ref=rows(*block) if src is None else src, dst_ref=rows(*block),
                send_sem=send_sems.at[k], recv_sem=recv_sems.at[k], device_id=to, device_id_type=MESH)

        mine = pltpu.make_async_copy(x_ref, rows(*me), local_sem)
        mine.start()
        first = [copy(0, me, sibling, src=x_ref)]
        first += [copy(1 + j, me, (*chip, c), src=x_ref) for j, chip in enumerate(chips)]
        for cp in first:
            cp.start()
        passed = [copy(4 + j, (*chip, c), sibling) for j, chip in enumerate(chips)]
        for j, chip in enumerate(chips):
            copy(1 + j, (*chip, c), me).wait_recv()
            passed[j].start()
        copy(0, sibling, me).wait_recv()
        for j, chip in enumerate(chips):
            copy(4 + j, (*chip, 1 - c), me).wait_recv()
        for cp in first + passed:
            cp.wait_send()
        mine.wait()

    return pl.pallas_call(
        body, name=name,
        out_shape=jax.ShapeDtypeStruct((N_DEV * m_per, n), blk.dtype),
        in_specs=[pl.BlockSpec(memory_space=pltpu.VMEM)],
        out_specs=pl.BlockSpec(memory_space=pltpu.VMEM),
        scratch_shapes=[pltpu.SemaphoreType.DMA((7,)), pltpu.SemaphoreType.DMA((7,)), pltpu.SemaphoreType.DMA],
    )(blk)


def _other_chips(x, y):
    return [(1 - x, y), (x, 1 - y), (1 - x, 1 - y)]


_HBM_SPEC = pl.BlockSpec(memory_space=pltpu.HBM)
_SEM_SPEC = pl.BlockSpec(memory_space=pltpu.SEMAPHORE)
_SPLIT_PARAMS = pltpu.CompilerParams(has_side_effects=pltpu.SideEffectType.DATAFLOW_SIDE_EFFECTING)


def _in_hbm(a):
    return pltpu.with_memory_space_constraint(a, pltpu.HBM)


def _ag_copies(own_ref, land_ref, send_sems, recv_sems):
    x, y, c = _place()
    chip = 2 * x + y
    hr = own_ref.shape[0] // 2

    def half(ch):
        return land_ref.at[ch, pl.ds(c * hr, hr), :]

    def copy(k, src, dst, to):
        return pltpu.make_async_remote_copy(src_ref=src, dst_ref=dst, send_sem=send_sems.at[k],
                                            recv_sem=recv_sems.at[k], device_id=to, device_id_type=MESH)

    sends, expects = [], []
    for j, (ox, oy) in enumerate(_other_chips(x, y)):
        sends.append(copy(j, own_ref.at[pl.ds(c * hr, hr), :], half(chip), (ox, oy, c)))
        expects.append(copy(j, half(2 * ox + oy), half(2 * ox + oy), (ox, oy, c)))
    own_slot = copy(3, own_ref, land_ref.at[chip], (x, y, 1 - c))
    return sends + [own_slot], expects + [own_slot]


def _sc_copies(p_ref, land_ref, send_sems, recv_sems):
    x, y, c = _place()
    chip = 2 * x + y
    sends, expects = [], []
    for j, (ox, oy) in enumerate(_other_chips(x, y)):
        och = 2 * ox + oy
        mk = lambda dst_slot: pltpu.make_async_remote_copy(
            src_ref=p_ref.at[och], dst_ref=land_ref.at[dst_slot], send_sem=send_sems.at[j],
            recv_sem=recv_sems.at[j], device_id=(ox, oy, c), device_id_type=MESH)
        sends.append(mk(chip))
        expects.append(mk(och))
    return sends, expects


def _pe_copies(g_ref, land_ref, send_sems, recv_sems):
    x, y, c = _place()
    hr = g_ref.shape[1] // 2
    cp = pltpu.make_async_remote_copy(
        src_ref=g_ref.at[:, pl.ds((1 - c) * hr, hr), :], dst_ref=land_ref, send_sem=send_sems.at[0],
        recv_sem=recv_sems.at[0], device_id=(x, y, 1 - c), device_id_type=MESH)
    return [cp], [cp]


def _split_start(src, land_shape, copies, n_copies, after, name):
    def body(src_ref, land_ref, after_ref, send_sems, recv_sems, src_thru, land_thru, token):
        for cp in copies(src_ref, land_ref, send_sems, recv_sems)[0]:
            cp.start()
        token[...] = jnp.zeros_like(token)

    land = lax.empty(land_shape, src.dtype)
    return pl.pallas_call(
        body, name=name,
        out_shape=(pltpu.SemaphoreType.DMA((n_copies,)), pltpu.SemaphoreType.DMA((n_copies,)),
                   pltpu.HBM(src.shape, src.dtype), pltpu.HBM(land_shape, src.dtype),
                   jax.ShapeDtypeStruct((8, 128), F32)),
        in_specs=(_HBM_SPEC, _HBM_SPEC, pl.BlockSpec(memory_space=pl.ANY)),
        out_specs=(_SEM_SPEC, _SEM_SPEC, _HBM_SPEC, _HBM_SPEC, pl.BlockSpec(memory_space=pltpu.VMEM)),
        input_output_aliases={0: 2, 1: 3}, compiler_params=_SPLIT_PARAMS,
    )(_in_hbm(src), _in_hbm(land), after)


def _split_wait(started, after, copies, name):
    send_sems, recv_sems, src_thru, land_thru, _ = started

    def body(src_ref, land_ref, send_sems, recv_sems, after_ref, src_dead, got_ref):
        sends, expects = copies(src_ref, land_ref, send_sems, recv_sems)
        for cp in sends:
            cp.wait_send()
        for cp in expects:
            cp.wait_recv()

    return pl.pallas_call(
        body, name=name,
        out_shape=(pltpu.HBM(src_thru.shape, src_thru.dtype), pltpu.HBM(land_thru.shape, land_thru.dtype)),
        in_specs=(_HBM_SPEC, _HBM_SPEC, _SEM_SPEC, _SEM_SPEC, pl.BlockSpec(memory_space=pl.ANY)),
        out_specs=(_HBM_SPEC, _HBM_SPEC), input_output_aliases={0: 0, 1: 1}, compiler_params=_SPLIT_PARAMS,
    )(src_thru, land_thru, send_sems, recv_sems, after)


def _ag_pass_on(land, name):
    hr = land.shape[1] // 2

    def body(in_ref, out_ref, send_sems, recv_sems):
        x, y, c = _place()

        def copy(j, ox, oy, cc):
            ref = out_ref.at[2 * ox + oy, pl.ds(cc * hr, hr), :]
            return pltpu.make_async_remote_copy(src_ref=ref, dst_ref=ref, send_sem=send_sems.at[j],
                                                recv_sem=recv_sems.at[j], device_id=(x, y, 1 - c),
                                                device_id_type=MESH)

        others = _other_chips(x, y)
        for j, (ox, oy) in enumerate(others):
            copy(j, ox, oy, c).start()
        for j, (ox, oy) in enumerate(others):
            copy(j, ox, oy, 1 - c).wait_recv()
        for j, (ox, oy) in enumerate(others):
            copy(j, ox, oy, c).wait_send()

    any_spec = pl.BlockSpec(memory_space=pl.ANY)
    return pl.pallas_call(
        body, name=name, out_shape=jax.ShapeDtypeStruct(land.shape, land.dtype),
        in_specs=[any_spec], out_specs=any_spec, input_output_aliases={0: 0},
        scratch_shapes=[pltpu.SemaphoreType.DMA((3,)), pltpu.SemaphoreType.DMA((3,))],
    )(land)


def _rs_pair_exchange(g, name):
    r = g.shape[1]
    hr = r // 2

    def body(g_ref, got_ref, send_sem, recv_sem):
        x, y, c = _place()
        cp = pltpu.make_async_remote_copy(
            src_ref=g_ref.at[:, pl.ds((1 - c) * hr, hr), :], dst_ref=got_ref, send_sem=send_sem, recv_sem=recv_sem,
            device_id=(x, y, 1 - c), device_id_type=MESH)
        cp.start()
        cp.wait()

    any_spec = pl.BlockSpec(memory_space=pl.ANY)
    return pl.pallas_call(
        body, name=name,
        out_shape=jax.ShapeDtypeStruct((N_CHIPS, hr, D), F32),
        in_specs=[any_spec], out_specs=any_spec,
        scratch_shapes=[pltpu.SemaphoreType.DMA, pltpu.SemaphoreType.DMA],
    )(g)


def _rs_chip_sum(place, g, got, name):
    r = g.shape[1]
    hr = r // 2
    tr = _tile(hr, 640, 16)
    nt = hr // tr

    def body(pl_ref, g_ref, got_ref, p16_ref, p32_ref):
        s = pl.program_id(1)
        p = g_ref[0] + got_ref[0]
        p16_ref[0] = p.astype(BF16)

        @pl.when(s == pl_ref[1])
        def _():
            p32_ref[...] = p

    return pl.pallas_call(
        body, name=name,
        grid_spec=pltpu.PrefetchScalarGridSpec(
            num_scalar_prefetch=1, grid=(nt, N_CHIPS),
            in_specs=[pl.BlockSpec((1, tr, D), lambda i, s, pr: (s, pr[0] * nt + i, 0)),
                      pl.BlockSpec((1, tr, D), lambda i, s, pr: (s, i, 0))],
            out_specs=[pl.BlockSpec((1, tr, D), lambda i, s, pr: (s, i, 0)),
                       pl.BlockSpec((tr, D), lambda i, s, pr: (i, 0))]),
        out_shape=[jax.ShapeDtypeStruct((N_CHIPS, hr, D), BF16), jax.ShapeDtypeStruct((hr, D), F32)],
        compiler_params=_params(("parallel", "arbitrary")),
    )(place, g, got)


def _rs_final_sum(place, parts, p32, name):
    hr = parts.shape[1]
    tr = _tile(hr, 640, 16)
    nt = hr // tr

    def body(pl_ref, a_ref, b_ref, c_ref, p32_ref, o_ref):
        o_ref[...] = ((p32_ref[...] + a_ref[0].astype(F32)) + b_ref[0].astype(F32)) + c_ref[0].astype(F32)

    def other(j):
        return pl.BlockSpec((1, tr, D), lambda i, pr: (j + jnp.where(pr[1] <= j, 1, 0), i, 0))

    return pl.pallas_call(
        body, name=name,
        grid_spec=pltpu.PrefetchScalarGridSpec(
            num_scalar_prefetch=1, grid=(nt,),
            in_specs=[other(0), other(1), other(2), pl.BlockSpec((tr, D), lambda i, pr: (i, 0))],
            out_specs=pl.BlockSpec((tr, D), lambda i, pr: (pr[0] * nt + i, 0))),
        out_shape=jax.ShapeDtypeStruct((2 * hr, D), F32),
        compiler_params=_params(("parallel",)),
    )(place, parts, parts, parts, p32)


def _rs_pair_gather(both, name):
    hr = both.shape[0] // 2

    def body(in_ref, out_ref, send_sem, recv_sem):
        x, y, c = _place()
        mine = out_ref.at[pl.ds(c * hr, hr), :]
        cp = pltpu.make_async_remote_copy(
            src_ref=mine, dst_ref=mine, send_sem=send_sem, recv_sem=recv_sem,
            device_id=(x, y, 1 - c), device_id_type=MESH)
        cp.start()
        theirs = out_ref.at[pl.ds((1 - c) * hr, hr), :]
        pltpu.make_async_remote_copy(
            src_ref=theirs, dst_ref=theirs, send_sem=send_sem, recv_sem=recv_sem,
            device_id=(x, y, 1 - c), device_id_type=MESH).wait_recv()
        cp.wait_send()

    any_spec = pl.BlockSpec(memory_space=pl.ANY)
    return pl.pallas_call(
        body, name=name,
        out_shape=jax.ShapeDtypeStruct(both.shape, F32),
        in_specs=[any_spec], out_specs=any_spec, input_output_aliases={0: 0},
        scratch_shapes=[pltpu.SemaphoreType.DMA, pltpu.SemaphoreType.DMA],
    )(both)


def _local_step(x, ctx, tgt, mods, mc, ag_gin, ag_main, place, small):
    nb, t, _ = x.shape
    tc = ctx.shape[1]
    n = nb * t
    nc = nb * tc
    xf = x.reshape(n, D)
    cf = ctx.reshape(nc, D)
    tf = tgt.reshape(n, D)
    vec = lambda a: a.reshape(1, -1)
    m = [[mods[l, :, k, :].reshape(nb, 1, D) for k in range(N_MOD)] for l in range(2)]
    mc_b = [jnp.broadcast_to(mc[k].reshape(1, 1, D), (nb, 1, D)) for k in range(2)]

    cw = [small["ffn_conv_w"][l] for l in range(2)]
    cb = [small["ffn_conv_b"][l].reshape(1, -1) for l in range(2)]
    w2 = jnp.zeros((128, 2 * KEY), F32)
    w2 = w2.at[0:RANK, 0:KEY].set(small["gla_w_a2"][0]).at[RANK:2 * RANK, KEY:].set(small["gla_w_a2"][1])
    b2 = small["gla_b_a"].reshape(1, 2 * KEY)
    hg = small["gla_head_norm"].reshape(1, DV)

    hn0 = _mod_fwd(xf, vec(small["norm_mix"][0]), m[0][0], m[0][1], t, "mod0_main")
    hnc = _mod_fwd(cf, vec(small["norm_mix"][0]), mc_b[0], mc_b[1], tc, "mod0_ctx")
    hn_all = jnp.concatenate([hn0, hnc], axis=0)
    gin = _ag_pass_on(_split_wait(ag_gin, hn_all, _ag_copies, "ag_gin_wait")[1], "ag_gin_pass_on")
    w_gin = jnp.pad(gin[:, :_GIN_ROWS, :].reshape(GLA_IN, D), ((0, GLA_IN_PAD - GLA_IN), (0, 0)))
    p_all = _mm(hn_all, w_gin, "nt", F32, "gla_in_proj", 768, 3200)
    la_all = _gla_decay_fwd(p_all, w2, b2)
    o2, ss = _gla_scan_fwd(p_all, la_all, nb, t, tc)
    wg = _ag_pass_on(_split_wait(ag_main, o2[0], _ag_copies, "ag_main_wait")[1], "ag_main_pass_on")
    offs = _offsets(_MAIN, _MAIN_ROWS)
    rows = _MAIN_ROWS

    def w_nt(a, k, name, out_dtype=BF16, tm=1024):
        return _mm_nt_w(a, wg, offs[k], rows[k], name, tm, out_dtype)

    def w_nn(a3, k, name, tm, tn):
        return _mm_nn_w(a3, wg, offs[k], rows[k], name, tm, tn)

    yb0 = _gla_post_fwd(o2, p_all, hg, n)
    y0 = w_nn(yb0[None], "gla_out", "gla_out_proj", 1024, 1024)
    h1, hn1 = _mod_fwd(xf, vec(small["norm_ffn"][0]), m[0][3], m[0][4], t, "mod0_ffn", y=y0, gate=m[0][2])
    u0 = w_nt(hn1, "up_t0", "ffn0_up")
    z0 = _ffn_mid_fwd(u0, cw[0], cb[0], nb, t, "ffn0_mid_fwd")
    f0 = w_nn(z0[None], "down0", "ffn0_down", 1024, 1024)
    h2, hn2 = _mod_fwd(h1, vec(small["norm_mix"][1]), m[1][0], m[1][1], t, "mod1_mix", y=f0, gate=m[0][5])
    p1 = w_nt(hn2, "sc_in_t", "sc_in_proj")
    yb1 = _sc_mid_fwd(p1, small["sc_conv_w"], nb, t)
    y1 = w_nn(yb1[None], "sc_out", "sc_out_proj", 1024, 1024)
    h3, hn3 = _mod_fwd(h2, vec(small["norm_ffn"][1]), m[1][3], m[1][4], t, "mod1_ffn", y=y1, gate=m[1][2])
    u1 = w_nt(hn3, "up_t1", "ffn1_up")
    z1 = _ffn_mid_fwd(u1, cw[1], cb[1], nb, t, "ffn1_mid_fwd")
    f1 = w_nn(z1[None], "down1", "ffn1_down", 1024, 1024)
    loss, dh4, df1, dm15, dfinal = _final(h3, f1, m[1][5], vec(small["final_norm"]), tf, t)

    gs = {}
    dmods = [[None] * N_MOD for _ in range(2)]
    dmods[1][5] = dm15

    def w_dw(a3, b, g_prev, k, name, tm):
        return _mm_dw(a3, b, g_prev, offs[k], rows[k], name, tm)

    def ffn_bwd(l, df, u, z, hn, g_prev):
        dz = w_nt(df, f"down{l}", f"ffn{l}_down_dx")
        g_acc = w_dw(z[None], df, g_prev, f"down{l}", f"ffn{l}_down_dw", 640)
        du, dcw, dcb = _ffn_mid_bwd(u, cw[l], cb[l], dz, nb, t, f"ffn{l}_mid_bwd")
        dhn = w_nn(du, f"up_t{l}", f"ffn{l}_up_dx", 512, 512)
        g_acc = w_dw(du, hn, g_acc, f"up_t{l}", f"ffn{l}_up_dw", 640)
        return dhn, g_acc, jnp.moveaxis(dcw, 0, 1).reshape(3, 2 * HID), dcb.reshape(2 * HID)

    dhn3, g_acc, dcw1, dcb1 = ffn_bwd(1, df1, u1, z1, hn3, None)
    r = _mod_bwd(h3, dhn3, vec(small["norm_ffn"][1]), m[1][4], t, "mod1_ffn_bwd", dh_out=dh4, y_prev=y1,
                 gate_prev=m[1][2])
    dh3, dmods[1][4], dmods[1][3], dnf1, dy1, dmods[1][2] = (r["dh"], r["dscale"], r["dshift"], r["dgain"],
                                                             r["dy_prev"], r["dgate_prev"])
    dyb1 = w_nt(dy1, "sc_out", "sc_out_dx")
    g_acc = w_dw(yb1[None], dy1, g_acc, "sc_out", "sc_out_dw", 256)
    dp1, dscw = _sc_mid_bwd(p1, small["sc_conv_w"], dyb1, nb, t)
    dhn2 = w_nn(dp1, "sc_in_t", "sc_in_dx", 1024, 512)
    g_acc = w_dw(dp1, hn2, g_acc, "sc_in_t", "sc_in_dw", 256)
    r = _mod_bwd(h2, dhn2, vec(small["norm_mix"][1]), m[1][1], t, "mod1_mix_bwd", dh_out=dh3, y_prev=f0,
                 gate_prev=m[0][5])
    dh2, dmods[1][1], dmods[1][0], dnm1, df0, dmods[0][5] = (r["dh"], r["dscale"], r["dshift"], r["dgain"],
                                                             r["dy_prev"], r["dgate_prev"])
    dhn1, g_acc, dcw0, dcb0 = ffn_bwd(0, df0, u0, z0, hn1, g_acc)
    r = _mod_bwd(h1, dhn1, vec(small["norm_ffn"][0]), m[0][4], t, "mod0_ffn_bwd", dh_out=dh2, y_prev=y0,
                 gate_prev=m[0][2])
    dh1, dmods[0][4], dmods[0][3], dnf0, dy0, dmods[0][2] = (r["dh"], r["dscale"], r["dshift"], r["dgain"],
                                                             r["dy_prev"], r["dgate_prev"])
    g_packed = w_dw(yb0[None], dy0, g_acc, "gla_out", "gla_out_dw", 256)
    pair = _split_start(g_packed, (N_CHIPS, _MAIN_TOTAL // 2, D), _pe_copies, 1, dy0, "rs_main_pair_start")
    dyb0 = w_nt(dy0, "gla_out", "gla_out_dx", F32)
    do, dgate, dhg = _gla_post_bwd(o2, p_all, hg + pair[4][0:1, 0:1], dyb0, n)
    g_packed, from_sibling = _split_wait(pair, do, _pe_copies, "rs_main_pair_wait")
    p16, p32 = _rs_chip_sum(place, g_packed, from_sibling, "rs_main_chip_sum")
    sc_main = _split_start(p16, p16.shape, _sc_copies, 3, p32, "rs_main_scatter_start")
    dq, dk, dv, dla = _gla_scan_bwd(p_all, la_all, do, ss, nb, t, tc, sc_main[4])
    dp, dw2, db2 = _gla_assemble(p_all, w2, b2, dq, dk, dv, dla, dgate, n)
    dhn_all = _mm(dp, w_gin, "nn", F32, "gla_in_dx", 768, 512)
    landed = _split_wait(sc_main, dhn_all, _sc_copies, "rs_main_scatter_wait")[1]
    g_main = _rs_pair_gather(_rs_final_sum(place, landed, p32, "rs_main_final_sum"), "rs_main_pair_gather")
    g_gin = _mm(dp, hn_all, "tn", F32, "gla_in_dw", 640, 1024)[:GLA_IN]
    g_gin = jnp.pad(g_gin.reshape(N_CHIPS, _GIN_ROWS, D), ((0, 0), (0, _GIN_PAD - _GIN_ROWS), (0, 0)))
    from_sibling = _rs_pair_exchange(g_gin, "rs_gin_pair_exchange")
    p16_gin, p32_gin = _rs_chip_sum(place, g_gin, from_sibling, "rs_gin_chip_sum")
    r = _mod_bwd(xf, dhn_all, vec(small["norm_mix"][0]), m[0][1], t, "mod0_main_bwd", dh_out=dh1)
    grad_x, dmods[0][1], dmods[0][0], dnm0 = r["dh"], r["dscale"], r["dshift"], r["dgain"]
    rc = _mod_bwd(cf, dhn_all, vec(small["norm_mix"][0]), mc_b[1], tc, "mod0_ctx_bwd", dhn_row0=n, need_dh=False)
    dmc = jnp.stack([jnp.sum(rc["dshift"], axis=0).reshape(D), jnp.sum(rc["dscale"], axis=0).reshape(D)])
    dnm0 = dnm0 + rc["dgain"]

    gs["norm_mix"] = jnp.concatenate([dnm0, dnm1], axis=0)
    gs["norm_ffn"] = jnp.concatenate([dnf0, dnf1], axis=0)
    gs["final_norm"] = dfinal.reshape(D)
    gs["gla_w_a2"] = jnp.stack([dw2[0:RANK, 0:KEY], dw2[RANK:2 * RANK, KEY:]])
    gs["gla_b_a"] = db2.reshape(2, KEY)
    gs["gla_head_norm"] = dhg.reshape(DV)
    gs["sc_conv_w"] = dscw
    gs["ffn_conv_w"] = jnp.stack([dcw0, dcw1])
    gs["ffn_conv_b"] = jnp.stack([dcb0, dcb1])
    dmods_arr = jnp.stack([jnp.stack([dmods[l][k].reshape(nb, D) for k in range(N_MOD)], axis=1) for l in range(2)])
    return loss, grad_x.reshape(nb, t, D), g_main, p16_gin, p32_gin, gs, dmods_arr, dmc


def _pack(arrs):
    parts, meta, off = [], [], 0
    for a in arrs:
        r = a.size // 128
        rp = -(-r // 8) * 8
        a2 = a.reshape(r, 128).astype(F32)
        if rp != r:
            a2 = jnp.pad(a2, ((0, rp - r), (0, 0)))
        parts.append(a2)
        meta.append((off, r, a.shape))
        off += rp
    return jnp.concatenate(parts, axis=0), meta


def _unpack(buf, meta, lead=()):
    return [buf[..., off:off + r, :].reshape(*lead, *shape) for off, r, shape in meta]


_MAIN = ("up_t0", "up_t1", "down0", "down1", "sc_in_t", "gla_out", "sc_out")
_MAIN_ROWS = {"sc_in_t": 3 * D // N_CHIPS, "up_t0": 2 * HID // N_CHIPS, "up_t1": 2 * HID // N_CHIPS,
              "gla_out": D // N_CHIPS, "sc_out": D // N_CHIPS, "down0": HID // N_CHIPS, "down1": HID // N_CHIPS}
_MAIN_TOTAL = sum(_MAIN_ROWS.values())
_GIN_ROWS = GLA_IN // N_CHIPS
_GIN_PAD = -(-_GIN_ROWS // 32) * 32


def _offsets(names, rows):
    off, out = 0, {}
    for k in names:
        out[k] = off
        off += rows[k]
    return out


def kernel(x, c, ctx, c_ctx, ada_w, ada_b, norm_mix, norm_ffn, gla_w_in, gla_w_a2, gla_b_a, gla_head_norm, gla_w_out, sc_w_in, sc_conv_w, sc_w_out, ffn_w_up, ffn_conv_w, ffn_conv_b, ffn_w_down, final_norm, loss_target, m_c_ctx, m_ada_w, m_ada_b, m_norm_mix, m_norm_ffn, m_gla_w_in, m_gla_w_a2, m_gla_b_a, m_gla_head_norm, m_gla_w_out, m_sc_w_in, m_sc_conv_w, m_sc_w_out, m_ffn_w_up, m_ffn_conv_w, m_ffn_conv_b, m_ffn_w_down, m_final_norm, v_c_ctx, v_ada_w, v_ada_b, v_norm_mix, v_norm_ffn, v_gla_w_in, v_gla_w_a2, v_gla_b_a, v_gla_head_norm, v_gla_w_out, v_sc_w_in, v_sc_conv_w, v_sc_w_out, v_ffn_w_up, v_ffn_conv_w, v_ffn_conv_b, v_ffn_w_down, v_final_norm):
    ix, iy, ic = _place()
    chip = 2 * ix + iy
    dev = 2 * chip + ic
    place = jnp.stack([ic, chip]).astype(jnp.int32)
    nb = x.shape[0]
    offs = _offsets(_MAIN, _MAIN_ROWS)

    buf, meta = _pack([c, ffn_conv_w, sc_conv_w, gla_w_a2, gla_b_a])
    got = _allgather_small(buf, "gather_small_in").reshape(N_DEV, buf.shape[0], 128)
    c_all, fcw, scw, wa2, ba = _unpack(got, meta, (N_DEV,))
    c_all = c_all.reshape(N_DEV * nb, D)
    per_chip = lambda a: a[0::2]
    ffn_conv_w_full = jnp.moveaxis(per_chip(fcw), 0, 2).reshape(2, 3, 2 * HID)
    sc_conv_w_full = jnp.moveaxis(per_chip(scw)[:, 0], 0, 1).reshape(3, D)
    gla_w_a2_full = jnp.moveaxis(per_chip(wa2)[:, 0], 0, 2).reshape(2, RANK, KEY)
    gla_b_a_full = jnp.moveaxis(per_chip(ba)[:, 0], 0, 1).reshape(2, KEY)

    cvec = jnp.concatenate([c_all, c_ctx.reshape(1, D), jnp.zeros((ADA_ROWS - N_DEV * nb - 1, D), F32)], axis=0)
    ada_b_sh = lax.dynamic_slice_in_dim(ada_b, chip * ADA_SH, ADA_SH, axis=1).reshape(2, 1, ADA_SH)
    mod_sh = _ada_fwd(cvec, ada_w, ada_b_sh)
    got = _allgather_small(mod_sh.reshape(2 * ADA_ROWS, ADA_SH), "gather_mod")
    mod_full = jnp.moveaxis(per_chip(got.reshape(N_DEV, 2, ADA_ROWS, ADA_SH)), 0, 2).reshape(2, ADA_ROWS, N_MOD * D)
    mc = mod_full[0, N_DEV * nb, :2 * D].reshape(2, D)

    own = {"sc_in_t": sc_w_in[0].T, "up_t0": ffn_w_up[0].T, "up_t1": ffn_w_up[1].T,
           "gla_out": gla_w_out[0], "sc_out": sc_w_out[0], "down0": ffn_w_down[0], "down1": ffn_w_down[1]}
    own_main = jnp.concatenate([own[k].astype(BF16) for k in _MAIN], axis=0)
    own_gin = jnp.pad(gla_w_in[0].T.astype(BF16), ((0, _GIN_PAD - _GIN_ROWS), (0, 0)))
    ag_gin = _split_start(own_gin, (N_CHIPS, _GIN_PAD, D), _ag_copies, 4, mc, "ag_gin_start")
    ag_main = _split_start(own_main, (N_CHIPS, _MAIN_TOTAL, D), _ag_copies, 4, ag_gin[4], "ag_main_start")
    mods = lax.dynamic_slice_in_dim(mod_full, dev * nb, nb, axis=1).reshape(2, nb, N_MOD, D) + ag_main[4][0, 0]

    small = {"norm_mix": norm_mix, "norm_ffn": norm_ffn, "final_norm": final_norm, "gla_w_a2": gla_w_a2_full,
             "gla_b_a": gla_b_a_full, "gla_head_norm": gla_head_norm[0], "sc_conv_w": sc_conv_w_full,
             "ffn_conv_w": ffn_conv_w_full, "ffn_conv_b": ffn_conv_b}
    loss_p, grad_x, g_main, p16_gin, p32_gin, gs, dmods, dmc = _local_step(x, ctx, loss_target, mods, mc, ag_gin,
                                                                           ag_main, place, small)

    sum_names = ["norm_mix", "norm_ffn", "final_norm", "gla_w_a2", "gla_b_a", "gla_head_norm", "sc_conv_w",
                 "ffn_conv_w", "ffn_conv_b"]
    buf, meta = _pack([jnp.broadcast_to(loss_p, (8, 128))] + [gs[k] for k in sum_names] + [dmc, dmods])
    n_sum = meta[-1][0]
    got = _allgather_small(buf, "gather_small_grads").reshape(N_DEV, buf.shape[0], 128)
    summed = _sum_slots(got[:, :n_sum], "sum_small_grads")
    parts = _unpack(summed, meta[:-1])
    loss = parts[0][0, 0]
    g_small = dict(zip(sum_names, parts[1:-1]))
    dmc_tot = parts[-1]
    dmods_all = jnp.moveaxis(_unpack(got, meta[-1:], (N_DEV,))[0], 0, 1).reshape(2, N_DEV * nb, N_MOD * D)

    ctx_row = jnp.stack([jnp.concatenate([dmc_tot.reshape(2 * D), jnp.zeros(((N_MOD - 2) * D,), F32)]),
                         jnp.zeros((N_MOD * D,), F32)]).reshape(2, 1, N_MOD * D)
    dmod_ext = jnp.concatenate([dmods_all, ctx_row, jnp.zeros((2, ADA_ROWS - N_DEV * nb - 1, N_MOD * D), F32)], axis=1)
    g_ada_b = _sum_slots(jnp.moveaxis(dmod_ext, 1, 0).reshape(ADA_ROWS, 2 * N_MOD * D // 128, 128),
                         "sum_ada_b").reshape(2, N_MOD * D)
    dmod_sh = lax.dynamic_slice_in_dim(dmod_ext, chip * ADA_SH, ADA_SH, axis=2)
    g_ada_w, dcv = _ada_bwd(cvec, ada_w, dmod_sh)
    dscc_part = (dcv[0, N_DEV * nb] + dcv[1, N_DEV * nb]).reshape(8, 128)
    got = _allgather_small(dscc_part, "gather_dscc").reshape(N_DEV, 8, 128)
    g_c_ctx = _cctx_grad(per_chip(got), c_ctx.reshape(8, 128)).reshape(D)

    sc_gin = _split_start(p16_gin, p16_gin.shape, _sc_copies, 3, g_c_ctx, "rs_gin_scatter_start")
    seg = {k: g_main[offs[k]:offs[k] + _MAIN_ROWS[k]] for k in _MAIN}

    sl_chip = lambda a, axis, width: lax.dynamic_slice_in_dim(a, chip * width, width, axis=axis)
    grads = {
        "c_ctx": g_c_ctx, "ada_w": g_ada_w, "ada_b": g_ada_b, "norm_mix": g_small["norm_mix"],
        "norm_ffn": g_small["norm_ffn"],
        "gla_w_a2": sl_chip(g_small["gla_w_a2"], 2, KEY // N_CHIPS)[None],
        "gla_b_a": sl_chip(g_small["gla_b_a"], 1, KEY // N_CHIPS)[None],
        "gla_head_norm": g_small["gla_head_norm"][None], "gla_w_out": seg["gla_out"][None],
        "sc_w_in": seg["sc_in_t"].T[None], "sc_conv_w": sl_chip(g_small["sc_conv_w"], 1, D // N_CHIPS)[None],
        "sc_w_out": seg["sc_out"][None], "ffn_w_up": jnp.stack([seg["up_t0"].T, seg["up_t1"].T]),
        "ffn_conv_w": sl_chip(g_small["ffn_conv_w"], 2, 2 * HID // N_CHIPS), "ffn_conv_b": g_small["ffn_conv_b"],
        "ffn_w_down": jnp.stack([seg["down0"], seg["down1"]]), "final_norm": g_small["final_norm"],
    }
    weights = {"c_ctx": c_ctx, "ada_w": ada_w, "ada_b": ada_b, "norm_mix": norm_mix, "norm_ffn": norm_ffn,
               "gla_w_in": gla_w_in, "gla_w_a2": gla_w_a2, "gla_b_a": gla_b_a, "gla_head_norm": gla_head_norm,
               "gla_w_out": gla_w_out, "sc_w_in": sc_w_in, "sc_conv_w": sc_conv_w, "sc_w_out": sc_w_out,
               "ffn_w_up": ffn_w_up, "ffn_conv_w": ffn_conv_w, "ffn_conv_b": ffn_conv_b, "ffn_w_down": ffn_w_down,
               "final_norm": final_norm}
    mom1 = {"c_ctx": m_c_ctx, "ada_w": m_ada_w, "ada_b": m_ada_b, "norm_mix": m_norm_mix, "norm_ffn": m_norm_ffn,
            "gla_w_in": m_gla_w_in, "gla_w_a2": m_gla_w_a2, "gla_b_a": m_gla_b_a, "gla_head_norm": m_gla_head_norm,
            "gla_w_out": m_gla_w_out, "sc_w_in": m_sc_w_in, "sc_conv_w": m_sc_conv_w, "sc_w_out": m_sc_w_out,
            "ffn_w_up": m_ffn_w_up, "ffn_conv_w": m_ffn_conv_w, "ffn_conv_b": m_ffn_conv_b,
            "ffn_w_down": m_ffn_w_down, "final_norm": m_final_norm}
    mom2 = {"c_ctx": v_c_ctx, "ada_w": v_ada_w, "ada_b": v_ada_b, "norm_mix": v_norm_mix, "norm_ffn": v_norm_ffn,
            "gla_w_in": v_gla_w_in, "gla_w_a2": v_gla_w_a2, "gla_b_a": v_gla_b_a, "gla_head_norm": v_gla_head_norm,
            "gla_w_out": v_gla_w_out, "sc_w_in": v_sc_w_in, "sc_conv_w": v_sc_conv_w, "sc_w_out": v_sc_w_out,
            "ffn_w_up": v_ffn_w_up, "ffn_conv_w": v_ffn_conv_w, "ffn_conv_b": v_ffn_conv_b,
            "ffn_w_down": v_ffn_w_down, "final_norm": v_final_norm}
    names = list(weights)

    big_names = ["ada_w", "gla_w_out", "sc_w_in", "sc_w_out", "ffn_w_up", "ffn_w_down", "gla_w_in"]
    small_names = [k for k in names if k not in big_names]
    delta, new_m, new_v = {}, {}, {}
    done = []

    def big_adamw(k, token):
        shp = weights[k].shape
        as2d = lambda a: a.reshape(-1, shp[-1])
        d_, m_, v_ = _adamw(as2d(weights[k]), as2d(grads[k]), as2d(mom1[k]), as2d(mom2[k]), "adamw_" + k, token)
        done.append(v_[0:1, 0:128])
        delta[k], new_m[k], new_v[k] = d_.reshape(shp), m_.reshape(shp), v_.reshape(shp)

    for k in big_names[:-1]:
        grads[k] = grads[k].reshape(weights[k].shape)
        big_adamw(k, sc_gin[4])
    for k in small_names:
        grads[k] = grads[k].reshape(weights[k].shape)
    packed = [_pack([src[k] for k in small_names]) for src in (weights, grads, mom1, mom2)]
    meta = packed[0][1]
    rows_pad = -packed[0][0].shape[0] % 128
    bufs = [jnp.pad(p[0], ((0, rows_pad), (0, 0))) for p in packed]
    outs = _adamw(bufs[0], bufs[1], bufs[2], bufs[3], "adamw_small", sc_gin[4])
    done.append(outs[2][0:1, :])
    for dst, o in zip((delta, new_m, new_v), outs):
        for k, a in zip(small_names, _unpack(o, meta)):
            dst[k] = a
    landed = _split_wait(sc_gin, jnp.concatenate(done, axis=0), _sc_copies, "rs_gin_scatter_wait")[1]
    g_gin_shard = _rs_pair_gather(_rs_final_sum(place, landed, p32_gin, "rs_gin_final_sum"), "rs_gin_pair_gather")
    grads["gla_w_in"] = g_gin_shard[:_GIN_ROWS].T[None]
    big_adamw("gla_w_in", sc_gin[4])

    return (loss, grad_x, *[grads[k] for k in names], *[delta[k] for k in names], *[new_m[k] for k in names],
            *[new_v[k] for k in names])
```

```python
import functools

import jax
import jax.numpy as jnp
from jax import lax
from jax.experimental import pallas as pl
from jax.experimental.pallas import tpu as pltpu

F32 = jnp.float32
BF16 = jnp.bfloat16
MESH = pl.DeviceIdType.MESH

EPS = 1e-6
D = 1024
N_MOD = 6
HEADS = 4
DK = 128
DV = 256
KEY = HEADS * DK
RANK = 16
TAU = 16.0
CH = 64
GRID_W = 64
HID = 2560
GLA_IN = 2 * KEY + 2 * D + 2 * RANK
GLA_IN_PAD = 3200
Q_SCALE = DK ** -0.5
N_CHIPS = 4
N_DEV = 8

ADAM_LR = 0.001
ADAM_B1 = 0.9
ADAM_B2 = 0.999
ADAM_EPS = 1e-08
ADAM_WD = 0.01
ADAM_STEP = 10

VMEM_LIMIT = 56 * 1024 * 1024


def _params(sem):
    return pltpu.CompilerParams(dimension_semantics=sem, vmem_limit_bytes=VMEM_LIMIT)


def _tile(n, pref, mult=8):
    if n <= pref:
        return n
    for t in range(pref - pref % mult, 0, -mult):
        if n % t == 0:
            return t
    raise ValueError((n, pref, mult))


_NN = (((1,), (0,)), ((), ()))
_NT = (((1,), (1,)), ((), ()))
_TN = (((0,), (0,)), ((), ()))


def _dot(a, b, dims=_NN):
    return lax.dot_general(a.astype(BF16), b.astype(BF16), dims, preferred_element_type=F32)


def _sigmoid(x):
    return 1.0 / (1.0 + jnp.exp(-x))


def _rowsum(x):
    return jnp.sum(x, axis=0, keepdims=True)


def _mm(a, b, form, out_dtype, name, tm, tn):
    if form == "tn":
        K, M = a.shape
    else:
        M, K = a.shape
    N = b.shape[0] if form == "nt" else b.shape[1]
    tm = _tile(M, tm, 128)
    tn = _tile(N, tn, 128)
    dims = {"nn": _NN, "nt": _NT, "tn": _TN}[form]

    def body(a_ref, b_ref, o_ref):
        o_ref[...] = _dot(a_ref[...], b_ref[...], dims).astype(o_ref.dtype)

    if form == "tn":
        a_spec = pl.BlockSpec((K, tm), lambda i, j: (0, i))
    else:
        a_spec = pl.BlockSpec((tm, K), lambda i, j: (i, 0))
    if form == "nt":
        b_spec = pl.BlockSpec((tn, K), lambda i, j: (j, 0))
    else:
        b_spec = pl.BlockSpec((K, tn), lambda i, j: (0, j))
    return pl.pallas_call(
        body,
        name=name,
        grid=(M // tm, N // tn),
        in_specs=[a_spec, b_spec],
        out_specs=pl.BlockSpec((tm, tn), lambda i, j: (i, j)),
        out_shape=jax.ShapeDtypeStruct((M, N), out_dtype),
        compiler_params=_params(("parallel", "parallel")),
    )(a, b)


def _mm_nt_w(a, wg, off, rows, name, tm, out_dtype):
    m = a.shape[0]
    tm = _tile(m, tm, 128)
    if N_CHIPS * rows <= D:

        def body_small(a_ref, w_ref, o_ref):
            av = a_ref[...]
            for s in range(N_CHIPS):
                o_ref[:, s * rows:(s + 1) * rows] = _dot(av, w_ref[s], _NT).astype(o_ref.dtype)

        return pl.pallas_call(
            body_small, name=name, grid=(m // tm,),
            in_specs=[pl.BlockSpec((tm, D), lambda i: (i, 0)),
                      pl.BlockSpec((N_CHIPS, rows, D), lambda i: (0, off // rows, 0))],
            out_specs=pl.BlockSpec((tm, N_CHIPS * rows), lambda i: (i, 0)),
            out_shape=jax.ShapeDtypeStruct((m, N_CHIPS * rows), out_dtype),
            compiler_params=_params(("parallel",)),
        )(a, wg)

    def body(a_ref, w_ref, o_ref):
        o_ref[...] = _dot(a_ref[...], w_ref[0], _NT).astype(o_ref.dtype)

    return pl.pallas_call(
        body, name=name, grid=(m // tm, N_CHIPS),
        in_specs=[pl.BlockSpec((tm, D), lambda i, s: (i, 0)),
                  pl.BlockSpec((1, rows, D), lambda i, s: (s, off // rows, 0))],
        out_specs=pl.BlockSpec((tm, rows), lambda i, s: (i, s)),
        out_shape=jax.ShapeDtypeStruct((m, N_CHIPS * rows), out_dtype),
        compiler_params=_params(("parallel", "parallel")),
    )(a, wg)


def _mm_nn_w(a3, wg, off, rows, name, tm, tn):
    parts, m, kp = a3.shape
    assert parts * kp == N_CHIPS * rows
    tm = _tile(m, tm, 128)
    cuts = sorted({s * rows for s in range(N_CHIPS + 1)} | {p * kp for p in range(parts + 1)})
    pieces = [(k0 // kp, k0 % kp, k0 // rows, k0 % rows, k1 - k0) for k0, k1 in zip(cuts[:-1], cuts[1:])]

    def body(a_ref, w_ref, o_ref):
        acc = None
        for p, a0, s, r0, width in pieces:
            term = _dot(a_ref[p, :, a0:a0 + width], w_ref[s, r0:r0 + width, :])
            acc = term if acc is None else acc + term
        o_ref[...] = acc

    return pl.pallas_call(
        body, name=name, grid=(m // tm, D // tn),
        in_specs=[pl.BlockSpec((parts, tm, kp), lambda i, j: (0, i, 0)),
                  pl.BlockSpec((N_CHIPS, rows, tn), lambda i, j: (0, off // rows, j))],
        out_specs=pl.BlockSpec((tm, tn), lambda i, j: (i, j)),
        out_shape=jax.ShapeDtypeStruct((m, D), F32),
        compiler_params=_params(("parallel", "parallel")),
    )(a3, wg)


def _mm_dw(a3, b, g_prev, off, rows, name, tm):
    parts, ntok, cdim = a3.shape
    assert parts * cdim == N_CHIPS * rows and cdim % tm == 0 and rows % tm == 0 and off % tm == 0

    def body(a_ref, b_ref, *rest):
        rest[-1][0] = _dot(a_ref[0], b_ref[...], _TN)

    in_specs = [pl.BlockSpec((1, ntok, tm), lambda i: ((i * tm) // cdim, 0, ((i * tm) % cdim) // tm)),
                pl.BlockSpec((ntok, D), lambda i: (0, 0))]
    args = [a3, b]
    aliases = {}
    if g_prev is not None:
        in_specs.append(pl.BlockSpec(memory_space=pl.ANY))
        args.append(g_prev)
        aliases = {2: 0}
    return pl.pallas_call(
        body, name=name, grid=(N_CHIPS * rows // tm,),
        in_specs=in_specs,
        out_specs=pl.BlockSpec((1, tm, D), lambda i: ((i * tm) // rows, (off + (i * tm) % rows) // tm, 0)),
        out_shape=jax.ShapeDtypeStruct((N_CHIPS, _MAIN_TOTAL, D), F32),
        input_output_aliases=aliases,
        compiler_params=_params(("parallel",)),
    )(*args)


def _mod_fwd(h, gain, shift, scale, tpb_rows, name, y=None, gate=None, out_rows=None, into=None, row0=0):
    n = h.shape[0]
    tt = _tile(tpb_rows, 256)
    tpb = tpb_rows // tt
    has_res = y is not None
    assert row0 % tt == 0 and not (has_res and out_rows)

    def body(*refs):
        if has_res:
            h_ref, y_ref, gate_ref, gain_ref, sh_ref, sc_ref, hout_ref, hn_ref = refs
            hv = h_ref[...] + gate_ref[0] * y_ref[...]
            hout_ref[...] = hv
        else:
            h_ref, gain_ref, sh_ref, sc_ref, hn_ref = refs[0], refs[1], refs[2], refs[3], refs[-1]
            hv = h_ref[...]
        r = lax.rsqrt(jnp.mean(hv * hv, axis=-1, keepdims=True) + EPS)
        hn = (hv * r) * gain_ref[...] * (1.0 + sc_ref[0]) + sh_ref[0]
        hn_ref[...] = hn.astype(BF16)

    row = pl.BlockSpec((tt, D), lambda i: (i, 0))
    per_b = pl.BlockSpec((1, 1, D), lambda i: (i // tpb, 0, 0))
    vec = pl.BlockSpec((1, D), lambda i: (0, 0))
    if has_res:
        in_specs = [row, row, per_b, vec, per_b, per_b]
        args = (h, y, gate, gain, shift, scale)
        out_specs = [row, row]
        out_shape = [jax.ShapeDtypeStruct((n, D), F32), jax.ShapeDtypeStruct((n, D), BF16)]
    else:
        in_specs = [row, vec, per_b, per_b]
        args = (h, gain, shift, scale)
        out_specs = pl.BlockSpec((tt, D), lambda i: (i + row0 // tt, 0))
        out_shape = jax.ShapeDtypeStruct((out_rows or n, D), BF16)
    aliases = {}
    if into is not None:
        in_specs = in_specs + [pl.BlockSpec(memory_space=pl.ANY)]
        args = args + (into,)
        aliases = {4: 0}
    return pl.pallas_call(
        body, name=name, grid=(n // tt,), in_specs=in_specs, out_specs=out_specs, out_shape=out_shape,
        input_output_aliases=aliases, compiler_params=_params(("parallel",)),
    )(*args)


def _mod_bwd(h_in, dhn, gain, scale, tpb_rows, name, dhn_row0=0, dh_out=None, y_prev=None, gate_prev=None,
             need_dh=True):
    n = h_in.shape[0]
    nb = n // tpb_rows
    tt = _tile(tpb_rows, 256)
    tpb = tpb_rows // tt
    off = dhn_row0 // tt
    assert dhn_row0 % tt == 0
    has_out = dh_out is not None
    has_prev = y_prev is not None

    def body(*refs):
        it = iter(refs)
        h_ref, dhn_ref, gain_ref, sc_ref = next(it), next(it), next(it), next(it)
        dho_ref = next(it) if has_out else None
        yp_ref, gp_ref = (next(it), next(it)) if has_prev else (None, None)
        dh_ref = next(it) if need_dh else None
        dsc_ref, dsh_ref, dgain_ref = next(it), next(it), next(it)
        dyp_ref, dgp_ref = (next(it), next(it)) if has_prev else (None, None)
        i = pl.program_id(0)

        @pl.when(i == 0)
        def _():
            dgain_ref[...] = jnp.zeros_like(dgain_ref)

        @pl.when(i % tpb == 0)
        def _():
            dsc_ref[...] = jnp.zeros_like(dsc_ref)
            dsh_ref[...] = jnp.zeros_like(dsh_ref)
            if has_prev:
                dgp_ref[...] = jnp.zeros_like(dgp_ref)

        hv = h_ref[...]
        r = lax.rsqrt(jnp.mean(hv * hv, axis=-1, keepdims=True) + EPS)
        y = hv * r
        gain_v = gain_ref[...]
        g = dhn_ref[...].astype(F32)
        dsh_ref[0] += _rowsum(g)
        dsc_ref[0] += _rowsum(g * (y * gain_v))
        drn = g * (1.0 + sc_ref[0])
        dgain_ref[...] += _rowsum(drn * y)
        if need_dh:
            dy = drn * gain_v
            dh = r * (dy - y * jnp.mean(dy * y, axis=-1, keepdims=True))
            if has_out:
                dh = dh + dho_ref[...]
            dh_ref[...] = dh
            if has_prev:
                dyp_ref[...] = (dh * gp_ref[0]).astype(BF16)
                dgp_ref[0] += _rowsum(dh * yp_ref[...])

    row = pl.BlockSpec((tt, D), lambda i: (i, 0))
    row_off = pl.BlockSpec((tt, D), lambda i: (i + off, 0))
    per_b = pl.BlockSpec((1, 1, D), lambda i: (i // tpb, 0, 0))
    vec = pl.BlockSpec((1, D), lambda i: (0, 0))
    in_specs = [row, row_off, vec, per_b]
    args = [h_in, dhn, gain, scale]
    if has_out:
        in_specs.append(row)
        args.append(dh_out)
    if has_prev:
        in_specs += [row, per_b]
        args += [y_prev, gate_prev]
    out_specs, out_shape, names = [], [], []
    if need_dh:
        out_specs.append(row)
        out_shape.append(jax.ShapeDtypeStruct((n, D), F32))
        names.append("dh")
    for nm in ("dscale", "dshift"):
        out_specs.append(per_b)
        out_shape.append(jax.ShapeDtypeStruct((nb, 1, D), F32))
        names.append(nm)
    out_specs.append(vec)
    out_shape.append(jax.ShapeDtypeStruct((1, D), F32))
    names.append("dgain")
    if has_prev:
        out_specs += [row, per_b]
        out_shape += [jax.ShapeDtypeStruct((n, D), BF16), jax.ShapeDtypeStruct((nb, 1, D), F32)]
        names += ["dy_prev", "dgate_prev"]
    outs = pl.pallas_call(
        body, name=name, grid=(n // tt,), in_specs=in_specs, out_specs=out_specs, out_shape=out_shape,
        compiler_params=_params(("arbitrary",)),
    )(*args)
    return dict(zip(names, outs))


def _final(h, f, gate, gain, tgt, tpb_rows):
    n = h.shape[0]
    nb = n // tpb_rows
    tt = _tile(tpb_rows, 256)
    tpb = tpb_rows // tt

    def body(h_ref, f_ref, gate_ref, gain_ref, tgt_ref, loss_ref, dh_ref, df_ref, dgate_ref, dgain_ref):
        i = pl.program_id(0)

        @pl.when(i == 0)
        def _():
            loss_ref[...] = jnp.zeros_like(loss_ref)
            dgain_ref[...] = jnp.zeros_like(dgain_ref)

        @pl.when(i % tpb == 0)
        def _():
            dgate_ref[...] = jnp.zeros_like(dgate_ref)

        fv = f_ref[...]
        gate_v = gate_ref[0]
        hv = h_ref[...] + gate_v * fv
        r = lax.rsqrt(jnp.mean(hv * hv, axis=-1, keepdims=True) + EPS)
        y = hv * r
        gain_v = gain_ref[...]
        e = y * gain_v - tgt_ref[...]
        s = jnp.sum(_rowsum(e * e), axis=1, keepdims=True) * (0.5 / D)
        loss_ref[...] += jnp.broadcast_to(s, loss_ref.shape)
        dout = e * (1.0 / D)
        dgain_ref[...] += _rowsum(dout * y)
        dy = dout * gain_v
        dh = r * (dy - y * jnp.mean(dy * y, axis=-1, keepdims=True))
        dh_ref[...] = dh
        df_ref[...] = (dh * gate_v).astype(BF16)
        dgate_ref[0] += _rowsum(dh * fv)

    row = pl.BlockSpec((tt, D), lambda i: (i, 0))
    per_b = pl.BlockSpec((1, 1, D), lambda i: (i // tpb, 0, 0))
    vec = pl.BlockSpec((1, D), lambda i: (0, 0))
    return pl.pallas_call(
        body, name="final_loss", grid=(n // tt,),
        in_specs=[row, row, per_b, vec, row],
        out_specs=[pl.BlockSpec((1, 128), lambda i: (0, 0)), row, row, per_b, vec],
        out_shape=[jax.ShapeDtypeStruct((1, 128), F32), jax.ShapeDtypeStruct((n, D), F32),
                   jax.ShapeDtypeStruct((n, D), BF16), jax.ShapeDtypeStruct((nb, 1, D), F32),
                   jax.ShapeDtypeStruct((1, D), F32)],
        compiler_params=_params(("arbitrary",)),
    )(h, f, gate, gain, tgt)


def _row_dn1(x):
    t = lax.broadcasted_iota(jnp.int32, x.shape, 0)
    return jnp.where(t % GRID_W == 0, 0.0, pltpu.roll(x, 1, 0))


def _row_up1(x):
    t = lax.broadcasted_iota(jnp.int32, x.shape, 0)
    return jnp.where(t % GRID_W == GRID_W - 1, 0.0, pltpu.roll(x, x.shape[0] - 1, 0))


def _silu(x):
    return x * _sigmoid(x)


def _dsilu(x):
    s = _sigmoid(x)
    return s * (1.0 + x * (1.0 - s))


def _row_ds(i):
    start = i * GRID_W
    return pl.ds(start if isinstance(start, int) else pl.multiple_of(start, GRID_W), GRID_W)


def _grid_row(ref, i, first, last):
    def rows(k):
        return ref[_row_ds(k), :].astype(F32)

    cur = rows(i)
    return (jnp.zeros_like(cur) if first else rows(i - 1)), cur, (jnp.zeros_like(cur) if last else rows(i + 1))


def _over_grid_rows(n_rows, step, carry):
    carry = step(0, carry, True, n_rows == 1)
    if n_rows > 2:
        carry = lax.fori_loop(1, n_rows - 1, lambda i, c: step(i, c, False, False), carry)
    if n_rows > 1:
        carry = step(n_rows - 1, carry, False, True)
    return carry


def _fold8(p):
    return p.reshape(GRID_W // 8, 8, p.shape[1]).sum(axis=0)


def _ffn_mid_fwd(u0, cw, cb, nb, t, name):
    nc = HID // 128

    def conv(x, w_ref):
        zeros = jnp.zeros((GRID_W, x.shape[1]), x.dtype)
        down = jnp.concatenate([zeros, x[: x.shape[0] - GRID_W]], axis=0)
        up = jnp.concatenate([x[GRID_W:], zeros], axis=0)
        return down * w_ref[0:1, :] + x * w_ref[1:2, :] + up * w_ref[2:3, :]

    def body(ua_ref, ug_ref, wa_ref, wg_ref, ba_ref, bg_ref, z_ref):
        a = conv(ua_ref[...].astype(F32), wa_ref) + ba_ref[...]
        gt = conv(ug_ref[...].astype(F32), wg_ref) + bg_ref[...]
        z_ref[...] = (a * _silu(gt)).astype(BF16)

    col = lambda rows, part: pl.BlockSpec((rows, 128), lambda j, b: (b if rows == t else 0, part * nc + j))
    return pl.pallas_call(
        body, name=name, grid=(nc, nb),
        in_specs=[col(t, 0), col(t, 1), col(3, 0), col(3, 1), col(1, 0), col(1, 1)],
        out_specs=pl.BlockSpec((t, 128), lambda j, b: (b, j)),
        out_shape=jax.ShapeDtypeStruct((nb * t, HID), BF16),
        compiler_params=_params(("parallel", "parallel")),
    )(u0, u0, cw, cw, cb, cb)


def _ffn_mid_bwd(u0, cw, cb, dz, nb, t, name):
    nc = HID // 128
    n_rows = t // GRID_W

    def body(ua_ref, ug_ref, wa_ref, wg_ref, ba_ref, bg_ref, dz_ref, du_ref, dw_ref, db_ref, dua_ref, dug_ref):
        b = pl.program_id(1)

        @pl.when(b == 0)
        def _():
            dw_ref[...] = jnp.zeros_like(dw_ref)
            db_ref[...] = jnp.zeros_like(db_ref)

        wa = [wa_ref[k:k + 1, :] for k in range(3)]
        wg = [wg_ref[k:k + 1, :] for k in range(3)]
        ba, bg = ba_ref[...], bg_ref[...]

        def pass1(i, acc, first, last):
            here = _row_ds(i)
            ap, ac, an = _grid_row(ua_ref, i, first, last)
            gp, gc, gn = _grid_row(ug_ref, i, first, last)
            a = ap * wa[0] + ac * wa[1] + an * wa[2] + ba
            gt = gp * wg[0] + gc * wg[1] + gn * wg[2] + bg
            dzv = dz_ref[here, :].astype(F32)
            s = _sigmoid(gt)
            silu = gt * s
            da = dzv * silu
            dg = (dzv * a) * (s + silu * (1.0 - s))
            dua_ref[here, :] = da
            dug_ref[here, :] = dg
            terms = (da, da * ap, da * ac, da * an, dg, dg * gp, dg * gc, dg * gn)
            return tuple(r + _fold8(p) for r, p in zip(acc, terms))

        zero = jnp.zeros((8, 128), F32)
        acc = _over_grid_rows(n_rows, pass1, (zero,) * 8)
        for part in range(2):
            db_ref[part] += _rowsum(acc[4 * part])
            for k in range(3):
                dw_ref[part, k:k + 1, :] += _rowsum(acc[4 * part + 1 + k])

        def pass2(i, carry, first, last):
            for part, (ref, w) in enumerate(((dua_ref, wa), (dug_ref, wg))):
                dp_, dc_, dn_ = _grid_row(ref, i, first, last)
                du_ref[part, _row_ds(i), :] = (dn_ * w[0] + dc_ * w[1] + dp_ * w[2]).astype(BF16)
            return carry

        _over_grid_rows(n_rows, pass2, 0)

    col = lambda rows, part: pl.BlockSpec((rows, 128), lambda j, b: (b if rows == t else 0, part * nc + j))
    return pl.pallas_call(
        body, name=name, grid=(nc, nb),
        in_specs=[col(t, 0), col(t, 1), col(3, 0), col(3, 1), col(1, 0), col(1, 1),
                  pl.BlockSpec((t, 128), lambda j, b: (b, j))],
        out_specs=[pl.BlockSpec((2, t, 128), lambda j, b: (0, b, j)), pl.BlockSpec((2, 3, 128), lambda j, b: (0, 0, j)),
                   pl.BlockSpec((2, 1, 128), lambda j, b: (0, 0, j))],
        out_shape=[jax.ShapeDtypeStruct((2, nb * t, HID), BF16), jax.ShapeDtypeStruct((2, 3, HID), F32),
                   jax.ShapeDtypeStruct((2, 1, HID), F32)],
        scratch_shapes=[pltpu.VMEM((t, 128), F32), pltpu.VMEM((t, 128), F32)],
        compiler_params=_params(("parallel", "arbitrary")),
    )(u0, u0, cw, cw, cb, cb, dz)


def _sc_mid_fwd(p, cw, nb, t):
    nc = D // 128

    def body(bg_ref, cg_ref, v_ref, w_ref, y_ref):
        cv = cg_ref[...].astype(F32) * v_ref[...].astype(F32)
        cc = _row_dn1(cv) * w_ref[0:1, :] + cv * w_ref[1:2, :] + _row_up1(cv) * w_ref[2:3, :]
        y_ref[...] = (bg_ref[...].astype(F32) * cc).astype(BF16)

    part = lambda k: pl.BlockSpec((t, 128), lambda j, b: (b, k * nc + j))
    return pl.pallas_call(
        body, name="sc_mid_fwd", grid=(nc, nb),
        in_specs=[part(0), part(1), part(2), pl.BlockSpec((3, 128), lambda j, b: (0, j))],
        out_specs=pl.BlockSpec((t, 128), lambda j, b: (b, j)),
        out_shape=jax.ShapeDtypeStruct((nb * t, D), BF16),
        compiler_params=_params(("parallel", "parallel")),
    )(p, p, p, cw)


def _sc_mid_bwd(p, cw, dyb, nb, t):
    nc = D // 128

    def body(bg_ref, cg_ref, v_ref, w_ref, dy_ref, dp_ref, dw_ref):
        b = pl.program_id(1)

        @pl.when(b == 0)
        def _():
            dw_ref[...] = jnp.zeros_like(dw_ref)

        w0, w1, w2 = w_ref[0:1, :], w_ref[1:2, :], w_ref[2:3, :]
        cg, v = cg_ref[...].astype(F32), v_ref[...].astype(F32)
        cv = cg * v
        cvd = _row_dn1(cv)
        cvu = _row_up1(cv)
        cc = cvd * w0 + cv * w1 + cvu * w2
        dy = dy_ref[...].astype(F32)
        dcc = dy * bg_ref[...].astype(F32)
        dw_ref[0:1, :] += _rowsum(dcc * cvd)
        dw_ref[1:2, :] += _rowsum(dcc * cv)
        dw_ref[2:3, :] += _rowsum(dcc * cvu)
        dcv = _row_up1(dcc) * w0 + dcc * w1 + _row_dn1(dcc) * w2
        dp_ref[0] = (dy * cc).astype(BF16)
        dp_ref[1] = (dcv * v).astype(BF16)
        dp_ref[2] = (dcv * cg).astype(BF16)

    part = lambda k: pl.BlockSpec((t, 128), lambda j, b: (b, k * nc + j))
    return pl.pallas_call(
        body, name="sc_mid_bwd", grid=(nc, nb),
        in_specs=[part(0), part(1), part(2), pl.BlockSpec((3, 128), lambda j, b: (0, j)),
                  pl.BlockSpec((t, 128), lambda j, b: (b, j))],
        out_specs=[pl.BlockSpec((3, t, 128), lambda j, b: (0, b, j)), pl.BlockSpec((3, 128), lambda j, b: (0, j))],
        out_shape=[jax.ShapeDtypeStruct((3, nb * t, D), BF16), jax.ShapeDtypeStruct((3, D), F32)],
        compiler_params=_params(("parallel", "arbitrary")),
    )(p, p, p, cw, dyb)


def _gla_decay_fwd(p_all, w2, b2):
    n = p_all.shape[0]
    tt = _tile(n, 512)

    def body(a_ref, w_ref, b_ref, la_ref):
        z = _dot(a_ref[...], w_ref[...]) + b_ref[...]
        la_ref[...] = (jnp.minimum(z, 0.0) - jnp.log(1.0 + jnp.exp(-jnp.abs(z)))) * (1.0 / TAU)

    return pl.pallas_call(
        body, name="gla_decay_fwd", grid=(n // tt,),
        in_specs=[pl.BlockSpec((tt, 128), lambda i: (i, (2 * KEY + 2 * D) // 128)),
                  pl.BlockSpec((128, 2 * KEY), lambda i: (0, 0)), pl.BlockSpec((1, 2 * KEY), lambda i: (0, 0))],
        out_specs=pl.BlockSpec((tt, 2 * KEY), lambda i: (i, 0)),
        out_shape=jax.ShapeDtypeStruct((n, 2 * KEY), F32),
        compiler_params=_params(("parallel",)),
    )(p_all, w2, b2)


def _gla_blocks(nb, nm, ncx):
    def main_idx(d, i):
        return jnp.clip(jnp.where(d == 0, i - ncx, nm - 1 - (i - ncx)), 0, nm - 1)

    def rowblk(d, b, i):
        cidx = jnp.where(d == 0, i, ncx - 1 - i)
        return jnp.where(i < ncx, nb * nm + b * ncx + cidx, b * nm + main_idx(d, i))

    def mainblk(d, b, i):
        return b * nm + main_idx(d, i)

    return rowblk, mainblk


def _gla_mask(d):
    row = lax.broadcasted_iota(jnp.int32, (CH, CH), 0)
    col = lax.broadcasted_iota(jnp.int32, (CH, CH), 1)
    diff = jnp.where(d == 0, row - col, col - row)
    mask = diff >= 0
    return mask, jnp.where(mask, 1.0, 0.0).astype(BF16), jnp.where(diff <= 0, 1.0, 0.0).astype(BF16)


def _tri_sum(m01, x):
    w = x.shape[1]
    hi = x.astype(BF16)
    r1 = x - hi.astype(F32)
    mid = r1.astype(BF16)
    lo = (r1 - mid.astype(F32)).astype(BF16)
    s = lax.dot_general(m01, jnp.concatenate([hi, mid, lo], axis=1), _NN, preferred_element_type=F32)
    return s[:, :w] + s[:, w:2 * w] + s[:, 2 * w:]


def _gla_chunk(q, k, g, bc):
    bl = _rowsum(g)
    eq = jnp.exp(bc)
    ek = jnp.exp(-bc)
    ed = jnp.exp(bl - bc)
    return bl, eq, ek, ed, q * Q_SCALE * eq, k * ek, k * ed


def _gla_scan_fwd(p_all, la_all, nb, t, tc):
    nm, ncx = t // CH, tc // CH
    nst = nm + ncx
    rowblk, mainblk = _gla_blocks(nb, nm, ncx)

    def body(*refs):
        ins, (o_refs, ss_refs, st_ref) = refs[:8], (refs[8:10], refs[10:12], refs[12])
        i = pl.program_id(1)

        @pl.when(i == 0)
        def _():
            st_ref[...] = jnp.zeros_like(st_ref)

        loaded = [r[...] for r in ins]
        states = [st_ref[j] for j in range(2 * HEADS)]
        outs, new_states = [[], []], []
        for d in range(2):
            q_all, k_all, v_all, g_all = loaded[4 * d:4 * d + 4]
            mask, m01, _ = _gla_mask(d)
            bc_all = _tri_sum(m01, g_all)
            for h in range(HEADS):
                ksl = slice(h * DK, (h + 1) * DK)
                v = v_all[:, h * DV:(h + 1) * DV]
                st = states[d * HEADS + h]
                bl, _, _, _, qs, ks, kd = _gla_chunk(q_all[:, ksl], k_all[:, ksl], g_all[:, ksl], bc_all[:, ksl])
                att = jnp.where(mask, _dot(qs, ks, _NT), 0.0)
                outs[d].append(_dot(qs, st, _NT) + _dot(att, v))
                new_states.append(st * jnp.exp(bl) + _dot(v, kd, _TN))
        for d in range(2):
            o_refs[d][...] = jnp.concatenate(outs[d], axis=1)
            for h in range(HEADS):
                ss_refs[d][0, 0, h] = states[d * HEADS + h]
                st_ref[d * HEADS + h] = new_states[d * HEADS + h]

    def in_specs(d):
        return [pl.BlockSpec((CH, KEY), lambda b, i: (rowblk(d, b, i), 0)),
                pl.BlockSpec((CH, KEY), lambda b, i: (rowblk(d, b, i), 1)),
                pl.BlockSpec((CH, D), lambda b, i: (rowblk(d, b, i), 1)),
                pl.BlockSpec((CH, KEY), lambda b, i: (rowblk(d, b, i), d))]

    outs = pl.pallas_call(
        body, name="gla_scan_fwd", grid=(nb, nst),
        in_specs=in_specs(0) + in_specs(1),
        out_specs=[pl.BlockSpec((CH, D), lambda b, i: (mainblk(0, b, i), 0)),
                   pl.BlockSpec((CH, D), lambda b, i: (mainblk(1, b, i), 0)),
                   pl.BlockSpec((1, 1, HEADS, DV, DK), lambda b, i: (b, i, 0, 0, 0)),
                   pl.BlockSpec((1, 1, HEADS, DV, DK), lambda b, i: (b, i, 0, 0, 0))],
        out_shape=[jax.ShapeDtypeStruct((nb * t, D), F32)] * 2
        + [jax.ShapeDtypeStruct((nb, nst, HEADS, DV, DK), F32)] * 2,
        scratch_shapes=[pltpu.VMEM((2 * HEADS, DV, DK), F32)],
        compiler_params=_params(("parallel", "arbitrary")),
    )(*([p_all, p_all, p_all, la_all] * 2))
    return outs[:2], outs[2:]


def _gla_scan_bwd(p_all, la_all, do, ss, nb, t, tc, after):
    nm, ncx = t // CH, tc // CH
    nst = nm + ncx
    ntot = nb * (t + tc)
    rowblk, mainblk = _gla_blocks(nb, nm, ncx)

    def body(*refs):
        ins, outs, dst_ref = refs[:12], refs[13:21], refs[21]
        ip = pl.program_id(1)
        i = nst - 1 - ip

        @pl.when(ip == 0)
        def _():
            dst_ref[...] = jnp.zeros_like(dst_ref)

        live = jnp.where(i >= ncx, 1.0, 0.0)
        loaded = [[r[...] for r in ins[6 * d:6 * d + 5]] for d in range(2)]
        states = [ins[6 * d + 5][0, 0, h] for d in range(2) for h in range(HEADS)]
        dstates = [dst_ref[j] for j in range(2 * HEADS)]
        results, new_dstates = [], []
        for d in range(2):
            q_all, k_all, v_all, g_all, do_all = loaded[d]
            do_all = do_all * live
            mask, m01, m01_t = _gla_mask(d)
            bc_all = _tri_sum(m01, g_all)
            dqs_l, dks_l, dvs_l, dbs_l, dbls_l = [], [], [], [], []
            for h in range(HEADS):
                ksl = slice(h * DK, (h + 1) * DK)
                vsl = slice(h * DV, (h + 1) * DV)
                bl, eq, ek, ed, qs, ks, kd = _gla_chunk(q_all[:, ksl], k_all[:, ksl], g_all[:, ksl], bc_all[:, ksl])
                st, dst, v, dov = states[d * HEADS + h], dstates[d * HEADS + h], v_all[:, vsl], do_all[:, vsl]
                att = jnp.where(mask, _dot(qs, ks, _NT), 0.0)
                datt = jnp.where(mask, _dot(dov, v, _NT), 0.0)
                dqs = _dot(dov, st) + _dot(datt, ks)
                dks = _dot(datt, qs, _TN)
                dvs_l.append(_dot(att, dov, _TN) + _dot(kd, dst, _NT))
                dkd = _dot(v, dst)
                e = jnp.exp(bl)
                dbls_l.append(e * _rowsum(st * dst) + _rowsum(dkd * kd))
                new_dstates.append(_dot(dov, qs, _TN) + dst * e)
                dqs_l.append(dqs * eq * Q_SCALE)
                dks_l.append(dks * ek + dkd * ed)
                dbs_l.append(dqs * qs - dks * ks - dkd * kd)
            results.append((jnp.concatenate(dqs_l, axis=1), jnp.concatenate(dks_l, axis=1),
                            jnp.concatenate(dvs_l, axis=1),
                            _tri_sum(m01_t, jnp.concatenate(dbs_l, axis=1)) + jnp.concatenate(dbls_l, axis=1)))
        for d in range(2):
            for k in range(4):
                outs[4 * d + k][...] = results[d][k]
        for j in range(2 * HEADS):
            dst_ref[j] = new_dstates[j]

    def in_specs(d):
        return [pl.BlockSpec((CH, KEY), lambda b, ip: (rowblk(d, b, nst - 1 - ip), 0)),
                pl.BlockSpec((CH, KEY), lambda b, ip: (rowblk(d, b, nst - 1 - ip), 1)),
                pl.BlockSpec((CH, D), lambda b, ip: (rowblk(d, b, nst - 1 - ip), 1)),
                pl.BlockSpec((CH, KEY), lambda b, ip: (rowblk(d, b, nst - 1 - ip), d)),
                pl.BlockSpec((CH, D), lambda b, ip: (mainblk(d, b, nst - 1 - ip), 0)),
                pl.BlockSpec((1, 1, HEADS, DV, DK), lambda b, ip: (b, nst - 1 - ip, 0, 0, 0))]

    def out_specs(d):
        row = lambda width: pl.BlockSpec((CH, width), lambda b, ip: (rowblk(d, b, nst - 1 - ip), 0))
        return [row(KEY), row(KEY), row(D), row(KEY)]

    shapes = [jax.ShapeDtypeStruct((ntot, KEY), F32), jax.ShapeDtypeStruct((ntot, KEY), F32),
              jax.ShapeDtypeStruct((ntot, D), F32), jax.ShapeDtypeStruct((ntot, KEY), F32)]
    outs = pl.pallas_call(
        body, name="gla_scan_bwd", grid=(nb, nst),
        in_specs=in_specs(0) + in_specs(1) + [pl.BlockSpec(memory_space=pl.ANY)],
        out_specs=out_specs(0) + out_specs(1),
        out_shape=shapes * 2,
        scratch_shapes=[pltpu.VMEM((2 * HEADS, DV, DK), F32)],
        compiler_params=_params(("parallel", "arbitrary")),
    )(p_all, p_all, p_all, la_all, do, ss[0], p_all, p_all, p_all, la_all, do, ss[1], after)
    return [[outs[k], outs[4 + k]] for k in range(4)]


def _gla_post_fwd(o2, p_all, head_gain, n):
    tt = _tile(n, 256)

    def body(of_ref, ob_ref, g_ref, hg_ref, y_ref):
        o = of_ref[...] + ob_ref[...]
        gv = g_ref[...]
        hg = hg_ref[...]
        for h in range(HEADS):
            oh = o[:, h * DV:(h + 1) * DV]
            r = lax.rsqrt(jnp.mean(oh * oh, axis=-1, keepdims=True) + EPS)
            y_ref[:, h * DV:(h + 1) * DV] = ((oh * r) * hg * _silu(gv[:, h * DV:(h + 1) * DV])).astype(BF16)

    row = pl.BlockSpec((tt, D), lambda i: (i, 0))
    return pl.pallas_call(
        body, name="gla_post_fwd", grid=(n // tt,),
        in_specs=[row, row, pl.BlockSpec((tt, D), lambda i: (i, 2)), pl.BlockSpec((1, DV), lambda i: (0, 0))],
        out_specs=row,
        out_shape=jax.ShapeDtypeStruct((n, D), BF16),
        compiler_params=_params(("parallel",)),
    )(o2[0], o2[1], p_all, head_gain)


def _gla_post_bwd(o2, p_all, head_gain, dyb, n):
    tt = _tile(n, 256)

    def body(of_ref, ob_ref, g_ref, hg_ref, dy_ref, do_ref, dg_ref, dhg_ref):
        i = pl.program_id(0)

        @pl.when(i == 0)
        def _():
            dhg_ref[...] = jnp.zeros_like(dhg_ref)

        o = of_ref[...] + ob_ref[...]
        gv = g_ref[...]
        hg = hg_ref[...]
        dy = dy_ref[...]
        acc = jnp.zeros((1, DV), F32)
        for h in range(HEADS):
            sl = slice(h * DV, (h + 1) * DV)
            oh = o[:, sl]
            r = lax.rsqrt(jnp.mean(oh * oh, axis=-1, keepdims=True) + EPS)
            on = oh * r
            gh = gv[:, sl]
            dyh = dy[:, sl]
            dg_ref[:, sl] = dyh * (on * hg) * _dsilu(gh)
            dog = dyh * _silu(gh)
            acc = acc + _rowsum(dog * on)
            don = dog * hg
            do_ref[:, sl] = r * (don - on * jnp.mean(don * on, axis=-1, keepdims=True))
        dhg_ref[...] += acc

    return pl.pallas_call(
        body, name="gla_post_bwd", grid=(n // tt,),
        in_specs=[pl.BlockSpec((tt, D), lambda i: (i, 0)), pl.BlockSpec((tt, D), lambda i: (i, 0)),
                  pl.BlockSpec((tt, D), lambda i: (i, 2)),
                  pl.BlockSpec((1, DV), lambda i: (0, 0)), pl.BlockSpec((tt, D), lambda i: (i, 0))],
        out_specs=[pl.BlockSpec((tt, D), lambda i: (i, 0)), pl.BlockSpec((tt, D), lambda i: (i, 0)),
                   pl.BlockSpec((1, DV), lambda i: (0, 0))],
        out_shape=[jax.ShapeDtypeStruct((n, D), F32), jax.ShapeDtypeStruct((n, D), F32),
                   jax.ShapeDtypeStruct((1, DV), F32)],
        compiler_params=_params(("arbitrary",)),
    )(o2[0], o2[1], p_all, head_gain, dyb)


def _gla_assemble(p_all, w2, b2, dq, dk, dv, dla, dgate, n):
    ntot = p_all.shape[0]
    tt = _tile(n, 128)
    nmain = n // tt
    assert ntot % tt == 0

    def body(a_ref, w_ref, b_ref, dqf_ref, dqb_ref, dkf_ref, dkb_ref, dvf_ref, dvb_ref, dlf_ref, dlb_ref, dg_ref,
             dp_ref, dw_ref, db_ref):
        i = pl.program_id(0)

        @pl.when(i == 0)
        def _():
            dw_ref[...] = jnp.zeros_like(dw_ref)
            db_ref[...] = jnp.zeros_like(db_ref)

        a = a_ref[...]
        w = w_ref[...]
        z = _dot(a, w) + b_ref[...]
        dla = jnp.concatenate([dlf_ref[...], dlb_ref[...]], axis=1)
        dz = dla * (1.0 / (1.0 + jnp.exp(z))) * (1.0 / TAU)
        dw_ref[...] += _dot(a, dz, _TN)
        db_ref[...] += _rowsum(dz)
        dp_ref[:, 0:KEY] = (dqf_ref[...] + dqb_ref[...]).astype(BF16)
        dp_ref[:, KEY:2 * KEY] = (dkf_ref[...] + dkb_ref[...]).astype(BF16)
        dp_ref[:, 2 * KEY:2 * KEY + D] = (dvf_ref[...] + dvb_ref[...]).astype(BF16)
        dp_ref[:, 2 * KEY + D:2 * KEY + 2 * D] = (dg_ref[...] * jnp.where(i < nmain, 1.0, 0.0)).astype(BF16)
        dp_ref[:, 2 * KEY + 2 * D:GLA_IN_PAD] = _dot(dz, w, _NT).astype(BF16)

    row = lambda width: pl.BlockSpec((tt, width), lambda i: (i, 0))
    return pl.pallas_call(
        body, name="gla_assemble", grid=(ntot // tt,),
        in_specs=[pl.BlockSpec((tt, 128), lambda i: (i, (2 * KEY + 2 * D) // 128)),
                  pl.BlockSpec((128, 2 * KEY), lambda i: (0, 0)), pl.BlockSpec((1, 2 * KEY), lambda i: (0, 0)),
                  row(KEY), row(KEY), row(KEY), row(KEY), row(D), row(D), row(KEY), row(KEY),
                  pl.BlockSpec((tt, D), lambda i: (jnp.minimum(i, nmain - 1), 0))],
        out_specs=[pl.BlockSpec((tt, GLA_IN_PAD), lambda i: (i, 0)), pl.BlockSpec((128, 2 * KEY), lambda i: (0, 0)),
                   pl.BlockSpec((1, 2 * KEY), lambda i: (0, 0))],
        out_shape=[jax.ShapeDtypeStruct((ntot, GLA_IN_PAD), BF16), jax.ShapeDtypeStruct((128, 2 * KEY), F32),
                   jax.ShapeDtypeStruct((1, 2 * KEY), F32)],
        compiler_params=_params(("arbitrary",)),
    )(p_all, w2, b2, dq[0], dq[1], dk[0], dk[1], dv[0], dv[1], dla[0], dla[1], dgate)


ADA_ROWS = 24
ADA_SH = N_MOD * D // N_CHIPS


def _ada_fwd(cvec, ada_w, ada_b_sh):
    def body(c_ref, w_ref, b_ref, o_ref):
        o_ref[0] = _dot(_silu(c_ref[...]), w_ref[0]) + b_ref[0]

    return pl.pallas_call(
        body, name="ada_fwd", grid=(2,),
        in_specs=[pl.BlockSpec((ADA_ROWS, D), lambda l: (0, 0)), pl.BlockSpec((1, D, ADA_SH), lambda l: (l, 0, 0)),
                  pl.BlockSpec((1, 1, ADA_SH), lambda l: (l, 0, 0))],
        out_specs=pl.BlockSpec((1, ADA_ROWS, ADA_SH), lambda l: (l, 0, 0)),
        out_shape=jax.ShapeDtypeStruct((2, ADA_ROWS, ADA_SH), F32),
        compiler_params=_params(("parallel",)),
    )(cvec, ada_w, ada_b_sh)


def _ada_bwd(cvec, ada_w, dmod_sh):
    def body(c_ref, w_ref, dm_ref, gw_ref, dc_ref):
        dm = dm_ref[0]
        gw_ref[0] = _dot(_silu(c_ref[...]), dm, _TN)
        dc_ref[0] = _dot(dm, w_ref[0], _NT)

    return pl.pallas_call(
        body, name="ada_bwd", grid=(2,),
        in_specs=[pl.BlockSpec((ADA_ROWS, D), lambda l: (0, 0)), pl.BlockSpec((1, D, ADA_SH), lambda l: (l, 0, 0)),
                  pl.BlockSpec((1, ADA_ROWS, ADA_SH), lambda l: (l, 0, 0))],
        out_specs=[pl.BlockSpec((1, D, ADA_SH), lambda l: (l, 0, 0)), pl.BlockSpec((1, ADA_ROWS, D), lambda l: (l, 0, 0))],
        out_shape=[jax.ShapeDtypeStruct((2, D, ADA_SH), F32), jax.ShapeDtypeStruct((2, ADA_ROWS, D), F32)],
        compiler_params=_params(("parallel",)),
    )(cvec, ada_w, dmod_sh)


def _sum_slots(x, name):
    s, r, _ = x.shape

    def body(x_ref, o_ref):
        acc = x_ref[0]
        for k in range(1, s):
            acc = acc + x_ref[k]
        o_ref[...] = acc

    return pl.pallas_call(
        body, name=name, out_shape=jax.ShapeDtypeStruct((r, 128), F32),
        in_specs=[pl.BlockSpec(memory_space=pltpu.VMEM)], out_specs=pl.BlockSpec(memory_space=pltpu.VMEM),
    )(x)


def _cctx_grad(dscc_parts, c_ctx):
    def body(p_ref, c_ref, o_ref):
        acc = p_ref[0]
        for k in range(1, N_CHIPS):
            acc = acc + p_ref[k]
        o_ref[...] = acc * _dsilu(c_ref[...])

    return pl.pallas_call(
        body, name="cctx_grad", out_shape=jax.ShapeDtypeStruct((8, 128), F32),
        in_specs=[pl.BlockSpec(memory_space=pltpu.VMEM)] * 2, out_specs=pl.BlockSpec(memory_space=pltpu.VMEM),
    )(dscc_parts, c_ctx)


def _adamw(w, g, m, v, name, after):
    nl, r, cdim = w.shape
    tr = _tile(r, 256)
    c1 = 1.0 - ADAM_B1 ** ADAM_STEP
    c2 = 1.0 - ADAM_B2 ** ADAM_STEP

    def body(w_ref, g_ref, m_ref, v_ref, after_ref, d_ref, mo_ref, vo_ref):
        gv = g_ref[...]
        mn = ADAM_B1 * m_ref[...] + (1.0 - ADAM_B1) * gv
        vn = ADAM_B2 * v_ref[...] + (1.0 - ADAM_B2) * (gv * gv)
        mo_ref[...] = mn
        vo_ref[...] = vn
        d_ref[...] = -ADAM_LR * ((mn / c1) / (jnp.sqrt(vn / c2) + ADAM_EPS) + ADAM_WD * w_ref[...])

    spec = pl.BlockSpec((1, tr, cdim), lambda l, i: (l, i, 0))
    sds = jax.ShapeDtypeStruct((nl, r, cdim), F32)
    return pl.pallas_call(
        body, name=name, grid=(nl, r // tr), in_specs=[spec] * 4 + [pl.BlockSpec(memory_space=pl.ANY)],
        out_specs=[spec] * 3, out_shape=[sds] * 3, compiler_params=_params(("parallel", "parallel")),
    )(w, g, m, v, after)


def _place():
    x, y, c = lax.axis_index("x"), lax.axis_index("y"), lax.axis_index("c")
    return x, y, c


def _allgather_small(blk, name):
    m_per, n = blk.shape

    def body(x_ref, out_ref, send_sems, recv_sems, local_sem):
        x, y, c = _place()
        me, sibling = (x, y, c), (x, y, 1 - c)
        chips = [(1 - x, y), (x, 1 - y), (1 - x, 1 - y)]

        def rows(px, py, pc):
            return out_ref.at[pl.ds((4 * px + 2 * py + pc) * m_per, m_per), :]

        def copy(k, block, to, src=None):
            return pltpu.make_async_remote_copy(
                src_ref=rows(*block) if src is None else src, dst_ref=rows(*block),
                send_sem=send_sems.at[k], recv_sem=recv_sems.at[k], device_id=to, device_id_type=MESH)

        mine = pltpu.make_async_copy(x_ref, rows(*me), local_sem)
        mine.start()
        first = [copy(0, me, sibling, src=x_ref)]
        first += [copy(1 + j, me, (*chip, c), src=x_ref) for j, chip in enumerate(chips)]
        for cp in first:
            cp.start()
        passed = [copy(4 + j, (*chip, c), sibling) for j, chip in enumerate(chips)]
        for j, chip in enumerate(chips):
            copy(1 + j, (*chip, c), me).wait_recv()
            passed[j].start()
        copy(0, sibling, me).wait_recv()
        for j, chip in enumerate(chips):
            copy(4 + j, (*chip, 1 - c), me).wait_recv()
        for cp in first + passed:
            cp.wait_send()
        mine.wait()

    return pl.pallas_call(
        body, name=name,
        out_shape=jax.ShapeDtypeStruct((N_DEV * m_per, n), blk.dtype),
        in_specs=[pl.BlockSpec(memory_space=pltpu.VMEM)],
        out_specs=pl.BlockSpec(memory_space=pltpu.VMEM),
        scratch_shapes=[pltpu.SemaphoreType.DMA((7,)), pltpu.SemaphoreType.DMA((7,)), pltpu.SemaphoreType.DMA],
    )(blk)


def _other_chips(x, y):
    return [(1 - x, y), (x, 1 - y), (1 - x, 1 - y)]


_HBM_SPEC = pl.BlockSpec(memory_space=pltpu.HBM)
_SEM_SPEC = pl.BlockSpec(memory_space=pltpu.SEMAPHORE)
_SPLIT_PARAMS = pltpu.CompilerParams(has_side_effects=pltpu.SideEffectType.DATAFLOW_SIDE_EFFECTING)


def _in_hbm(a):
    return pltpu.with_memory_space_constraint(a, pltpu.HBM)


def _ag_copies(own_ref, land_ref, send_sems, recv_sems):
    x, y, c = _place()
    chip = 2 * x + y
    hr = own_ref.shape[0] // 2

    def half(ch):
        return land_ref.at[ch, pl.ds(c * hr, hr), :]

    def copy(k, src, dst, to):
        return pltpu.make_async_remote_copy(src_ref=src, dst_ref=dst, send_sem=send_sems.at[k],
                                            recv_sem=recv_sems.at[k], device_id=to, device_id_type=MESH)

    sends, expects = [], []
    for j, (ox, oy) in enumerate(_other_chips(x, y)):
        sends.append(copy(j, own_ref.at[pl.ds(c * hr, hr), :], half(chip), (ox, oy, c)))
        expects.append(copy(j, half(2 * ox + oy), half(2 * ox + oy), (ox, oy, c)))
    own_slot = copy(3, own_ref, land_ref.at[chip], (x, y, 1 - c))
    return sends + [own_slot], expects + [own_slot]


def _sc_copies(p_ref, land_ref, send_sems, recv_sems):
    x, y, c = _place()
    chip = 2 * x + y
    sends, expects = [], []
    for j, (ox, oy) in enumerate(_other_chips(x, y)):
        och = 2 * ox + oy
        mk = lambda dst_slot: pltpu.make_async_remote_copy(
            src_ref=p_ref.at[och], dst_ref=land_ref.at[dst_slot], send_sem=send_sems.at[j],
            recv_sem=recv_sems.at[j], device_id=(ox, oy, c), device_id_type=MESH)
        sends.append(mk(chip))
        expects.append(mk(och))
    return sends, expects


def _pe_copies(g_ref, land_ref, send_sems, recv_sems):
    x, y, c = _place()
    hr = g_ref.shape[1] // 2
    cp = pltpu.make_async_remote_copy(
        src_ref=g_ref.at[:, pl.ds((1 - c) * hr, hr), :], dst_ref=land_ref, send_sem=send_sems.at[0],
        recv_sem=recv_sems.at[0], device_id=(x, y, 1 - c), device_id_type=MESH)
    return [cp], [cp]


def _split_start(src, land_shape, copies, n_copies, after, name):
    def body(src_ref, land_ref, after_ref, send_sems, recv_sems, src_thru, land_thru, token):
        for cp in copies(src_ref, land_ref, send_sems, recv_sems)[0]:
            cp.start()
        token[...] = jnp.zeros_like(token)

    land = lax.empty(land_shape, src.dtype)
    return pl.pallas_call(
        body, name=name,
        out_shape=(pltpu.SemaphoreType.DMA((n_copies,)), pltpu.SemaphoreType.DMA((n_copies,)),
                   pltpu.HBM(src.shape, src.dtype), pltpu.HBM(land_shape, src.dtype),
                   jax.ShapeDtypeStruct((8, 128), F32)),
        in_specs=(_HBM_SPEC, _HBM_SPEC, pl.BlockSpec(memory_space=pl.ANY)),
        out_specs=(_SEM_SPEC, _SEM_SPEC, _HBM_SPEC, _HBM_SPEC, pl.BlockSpec(memory_space=pltpu.VMEM)),
        input_output_aliases={0: 2, 1: 3}, compiler_params=_SPLIT_PARAMS,
    )(_in_hbm(src), _in_hbm(land), after)


def _split_wait(started, after, copies, name):
    send_sems, recv_sems, src_thru, land_thru, _ = started

    def body(src_ref, land_ref, send_sems, recv_sems, after_ref, src_dead, got_ref):
        sends, expects = copies(src_ref, land_ref, send_sems, recv_sems)
        for cp in sends:
            cp.wait_send()
        for cp in expects:
            cp.wait_recv()

    return pl.pallas_call(
        body, name=name,
        out_shape=(pltpu.HBM(src_thru.shape, src_thru.dtype), pltpu.HBM(land_thru.shape, land_thru.dtype)),
        in_specs=(_HBM_SPEC, _HBM_SPEC, _SEM_SPEC, _SEM_SPEC, pl.BlockSpec(memory_space=pl.ANY)),
        out_specs=(_HBM_SPEC, _HBM_SPEC), input_output_aliases={0: 0, 1: 1}, compiler_params=_SPLIT_PARAMS,
    )(src_thru, land_thru, send_sems, recv_sems, after)


def _ag_pass_on(land, name):
    hr = land.shape[1] // 2

    def body(in_ref, out_ref, send_sems, recv_sems):
        x, y, c = _place()

        def copy(j, ox, oy, cc):
            ref = out_ref.at[2 * ox + oy, pl.ds(cc * hr, hr), :]
            return pltpu.make_async_remote_copy(src_ref=ref, dst_ref=ref, send_sem=send_sems.at[j],
                                                recv_sem=recv_sems.at[j], device_id=(x, y, 1 - c),
                                                device_id_type=MESH)

        others = _other_chips(x, y)
        for j, (ox, oy) in enumerate(others):
            copy(j, ox, oy, c).start()
        for j, (ox, oy) in enumerate(others):
            copy(j, ox, oy, 1 - c).wait_recv()
        for j, (ox, oy) in enumerate(others):
            copy(j, ox, oy, c).wait_send()

    any_spec = pl.BlockSpec(memory_space=pl.ANY)
    return pl.pallas_call(
        body, name=name, out_shape=jax.ShapeDtypeStruct(land.shape, land.dtype),
        in_specs=[any_spec], out_specs=any_spec, input_output_aliases={0: 0},
        scratch_shapes=[pltpu.SemaphoreType.DMA((3,)), pltpu.SemaphoreType.DMA((3,))],
    )(land)


def _rs_pair_exchange(g, name):
    r = g.shape[1]
    hr = r // 2

    def body(g_ref, got_ref, send_sem, recv_sem):
        x, y, c = _place()
        cp = pltpu.make_async_remote_copy(
            src_ref=g_ref.at[:, pl.ds((1 - c) * hr, hr), :], dst_ref=got_ref, send_sem=send_sem, recv_sem=recv_sem,
            device_id=(x, y, 1 - c), device_id_type=MESH)
        cp.start()
        cp.wait()

    any_spec = pl.BlockSpec(memory_space=pl.ANY)
    return pl.pallas_call(
        body, name=name,
        out_shape=jax.ShapeDtypeStruct((N_CHIPS, hr, D), F32),
        in_specs=[any_spec], out_specs=any_spec,
        scratch_shapes=[pltpu.SemaphoreType.DMA, pltpu.SemaphoreType.DMA],
    )(g)


def _rs_chip_sum(place, g, got, name):
    r = g.shape[1]
    hr = r // 2
    tr = _tile(hr, 640, 16)
    nt = hr // tr

    def body(pl_ref, g_ref, got_ref, p16_ref, p32_ref):
        s = pl.program_id(1)
        p = g_ref[0] + got_ref[0]
        p16_ref[0] = p.astype(BF16)

        @pl.when(s == pl_ref[1])
        def _():
            p32_ref[...] = p

    return pl.pallas_call(
        body, name=name,
        grid_spec=pltpu.PrefetchScalarGridSpec(
            num_scalar_prefetch=1, grid=(nt, N_CHIPS),
            in_specs=[pl.BlockSpec((1, tr, D), lambda i, s, pr: (s, pr[0] * nt + i, 0)),
                      pl.BlockSpec((1, tr, D), lambda i, s, pr: (s, i, 0))],
            out_specs=[pl.BlockSpec((1, tr, D), lambda i, s, pr: (s, i, 0)),
                       pl.BlockSpec((tr, D), lambda i, s, pr: (i, 0))]),
        out_shape=[jax.ShapeDtypeStruct((N_CHIPS, hr, D), BF16), jax.ShapeDtypeStruct((hr, D), F32)],
        compiler_params=_params(("parallel", "arbitrary")),
    )(place, g, got)


def _rs_final_sum(place, parts, p32, name):
    hr = parts.shape[1]
    tr = _tile(hr, 640, 16)
    nt = hr // tr

    def body(pl_ref, a_ref, b_ref, c_ref, p32_ref, o_ref):
        o_ref[...] = ((p32_ref[...] + a_ref[0].astype(F32)) + b_ref[0].astype(F32)) + c_ref[0].astype(F32)

    def other(j):
        return pl.BlockSpec((1, tr, D), lambda i, pr: (j + jnp.where(pr[1] <= j, 1, 0), i, 0))

    return pl.pallas_call(
        body, name=name,
        grid_spec=pltpu.PrefetchScalarGridSpec(
            num_scalar_prefetch=1, grid=(nt,),
            in_specs=[other(0), other(1), other(2), pl.BlockSpec((tr, D), lambda i, pr: (i, 0))],
            out_specs=pl.BlockSpec((tr, D), lambda i, pr: (pr[0] * nt + i, 0))),
        out_shape=jax.ShapeDtypeStruct((2 * hr, D), F32),
        compiler_params=_params(("parallel",)),
    )(place, parts, parts, parts, p32)


def _rs_pair_gather(both, name):
    hr = both.shape[0] // 2

    def body(in_ref, out_ref, send_sem, recv_sem):
        x, y, c = _place()
        mine = out_ref.at[pl.ds(c * hr, hr), :]
        cp = pltpu.make_async_remote_copy(
            src_ref=mine, dst_ref=mine, send_sem=send_sem, recv_sem=recv_sem,
            device_id=(x, y, 1 - c), device_id_type=MESH)
        cp.start()
        theirs = out_ref.at[pl.ds((1 - c) * hr, hr), :]
        pltpu.make_async_remote_copy(
            src_ref=theirs, dst_ref=theirs, send_sem=send_sem, recv_sem=recv_sem,
            device_id=(x, y, 1 - c), device_id_type=MESH).wait_recv()
        cp.wait_send()

    any_spec = pl.BlockSpec(memory_space=pl.ANY)
    return pl.pallas_call(
        body, name=name,
        out_shape=jax.ShapeDtypeStruct(both.shape, F32),
        in_specs=[any_spec], out_specs=any_spec, input_output_aliases={0: 0},
        scratch_shapes=[pltpu.SemaphoreType.DMA, pltpu.SemaphoreType.DMA],
    )(both)


def _local_step(x, ctx, tgt, mods, mc, ag_gin, ag_main, place, small):
    nb, t, _ = x.shape
    tc = ctx.shape[1]
    n = nb * t
    nc = nb * tc
    xf = x.reshape(n, D)
    cf = ctx.reshape(nc, D)
    tf = tgt.reshape(n, D)
    vec = lambda a: a.reshape(1, -1)
    m = [[mods[l, :, k, :].reshape(nb, 1, D) for k in range(N_MOD)] for l in range(2)]
    mc_b = [jnp.broadcast_to(mc[k].reshape(1, 1, D), (nb, 1, D)) for k in range(2)]

    cw = [small["ffn_conv_w"][l] for l in range(2)]
    cb = [small["ffn_conv_b"][l].reshape(1, -1) for l in range(2)]
    w2 = jnp.zeros((128, 2 * KEY), F32)
    w2 = w2.at[0:RANK, 0:KEY].set(small["gla_w_a2"][0]).at[RANK:2 * RANK, KEY:].set(small["gla_w_a2"][1])
    b2 = small["gla_b_a"].reshape(1, 2 * KEY)
    hg = small["gla_head_norm"].reshape(1, DV)

    hn_all = _mod_fwd(xf, vec(small["norm_mix"][0]), m[0][0], m[0][1], t, "mod0_main", out_rows=n + nc)
    hn_all = _mod_fwd(cf, vec(small["norm_mix"][0]), mc_b[0], mc_b[1], tc, "mod0_ctx", out_rows=n + nc,
                      into=hn_all, row0=n)
    gin = _ag_pass_on(_split_wait(ag_gin, hn_all, _ag_copies, "ag_gin_wait")[1], "ag_gin_pass_on")
    w_gin = jnp.pad(gin[:, :_GIN_ROWS, :].reshape(GLA_IN, D), ((0, GLA_IN_PAD - GLA_IN), (0, 0)))
    p_all = _mm(hn_all, w_gin, "nt", F32, "gla_in_proj", 768, 3200)
    la_all = _gla_decay_fwd(p_all, w2, b2)
    o2, ss = _gla_scan_fwd(p_all, la_all, nb, t, tc)
    wg = _ag_pass_on(_split_wait(ag_main, o2[0], _ag_copies, "ag_main_wait")[1], "ag_main_pass_on")
    offs = _offsets(_MAIN, _MAIN_ROWS)
    rows = _MAIN_ROWS

    def w_nt(a, k, name, out_dtype=BF16, tm=1024):
        return _mm_nt_w(a, wg, offs[k], rows[k], name, tm, out_dtype)

    def w_nn(a3, k, name, tm, tn):
        return _mm_nn_w(a3, wg, offs[k], rows[k], name, tm, tn)

    yb0 = _gla_post_fwd(o2, p_all, hg, n)
    y0 = w_nn(yb0[None], "gla_out", "gla_out_proj", 1024, 1024)
    h1, hn1 = _mod_fwd(xf, vec(small["norm_ffn"][0]), m[0][3], m[0][4], t, "mod0_ffn", y=y0, gate=m[0][2])
    u0 = w_nt(hn1, "up_t0", "ffn0_up")
    z0 = _ffn_mid_fwd(u0, cw[0], cb[0], nb, t, "ffn0_mid_fwd")
    f0 = w_nn(z0[None], "down0", "ffn0_down", 1024, 1024)
    h2, hn2 = _mod_fwd(h1, vec(small["norm_mix"][1]), m[1][0], m[1][1], t, "mod1_mix", y=f0, gate=m[0][5])
    p1 = w_nt(hn2, "sc_in_t", "sc_in_proj")
    yb1 = _sc_mid_fwd(p1, small["sc_conv_w"], nb, t)
    y1 = w_nn(yb1[None], "sc_out", "sc_out_proj", 1024, 1024)
    h3, hn3 = _mod_fwd(h2, vec(small["norm_ffn"][1]), m[1][3], m[1][4], t, "mod1_ffn", y=y1, gate=m[1][2])
    u1 = w_nt(hn3, "up_t1", "ffn1_up")
    z1 = _ffn_mid_fwd(u1, cw[1], cb[1], nb, t, "ffn1_mid_fwd")
    f1 = w_nn(z1[None], "down1", "ffn1_down", 1024, 1024)
    loss, dh4, df1, dm15, dfinal = _final(h3, f1, m[1][5], vec(small["final_norm"]), tf, t)

    gs = {}
    dmods = [[None] * N_MOD for _ in range(2)]
    dmods[1][5] = dm15

    def w_dw(a3, b, g_prev, k, name, tm):
        return _mm_dw(a3, b, g_prev, offs[k], rows[k], name, tm)

    def ffn_bwd(l, df, u, z, hn, g_prev):
        dz = w_nt(df, f"down{l}", f"ffn{l}_down_dx")
        g_acc = w_dw(z[None], df, g_prev, f"down{l}", f"ffn{l}_down_dw", 640)
        du, dcw, dcb = _ffn_mid_bwd(u, cw[l], cb[l], dz, nb, t, f"ffn{l}_mid_bwd")
        dhn = w_nn(du, f"up_t{l}", f"ffn{l}_up_dx", 512, 512)
        g_acc = w_dw(du, hn, g_acc, f"up_t{l}", f"ffn{l}_up_dw", 640)
        return dhn, g_acc, jnp.moveaxis(dcw, 0, 1).reshape(3, 2 * HID), dcb.reshape(2 * HID)

    dhn3, g_acc, dcw1, dcb1 = ffn_bwd(1, df1, u1, z1, hn3, None)
    r = _mod_bwd(h3, dhn3, vec(small["norm_ffn"][1]), m[1][4], t, "mod1_ffn_bwd", dh_out=dh4, y_prev=y1,
                 gate_prev=m[1][2])
    dh3, dmods[1][4], dmods[1][3], dnf1, dy1, dmods[1][2] = (r["dh"], r["dscale"], r["dshift"], r["dgain"],
                                                             r["dy_prev"], r["dgate_prev"])
    dyb1 = w_nt(dy1, "sc_out", "sc_out_dx")
    g_acc = w_dw(yb1[None], dy1, g_acc, "sc_out", "sc_out_dw", 256)
    dp1, dscw = _sc_mid_bwd(p1, small["sc_conv_w"], dyb1, nb, t)
    dhn2 = w_nn(dp1, "sc_in_t", "sc_in_dx", 1024, 512)
    g_acc = w_dw(dp1, hn2, g_acc, "sc_in_t", "sc_in_dw", 256)
    r = _mod_bwd(h2, dhn2, vec(small["norm_mix"][1]), m[1][1], t, "mod1_mix_bwd", dh_out=dh3, y_prev=f0,
                 gate_prev=m[0][5])
    dh2, dmods[1][1], dmods[1][0], dnm1, df0, dmods[0][5] = (r["dh"], r["dscale"], r["dshift"], r["dgain"],
                                                             r["dy_prev"], r["dgate_prev"])
    dhn1, g_acc, dcw0, dcb0 = ffn_bwd(0, df0, u0, z0, hn1, g_acc)
    r = _mod_bwd(h1, dhn1, vec(small["norm_ffn"][0]), m[0][4], t, "mod0_ffn_bwd", dh_out=dh2, y_prev=y0,
                 gate_prev=m[0][2])
    dh1, dmods[0][4], dmods[0][3], dnf0, dy0, dmods[0][2] = (r["dh"], r["dscale"], r["dshift"], r["dgain"],
                                                             r["dy_prev"], r["dgate_prev"])
    g_packed = w_dw(yb0[None], dy0, g_acc, "gla_out", "gla_out_dw", 256)
    pair = _split_start(g_packed, (N_CHIPS, _MAIN_TOTAL // 2, D), _pe_copies, 1, dy0, "rs_main_pair_start")
    dyb0 = w_nt(dy0, "gla_out", "gla_out_dx", F32)
    do, dgate, dhg = _gla_post_bwd(o2, p_all, hg + pair[4][0:1, 0:1], dyb0, n)
    g_packed, from_sibling = _split_wait(pair, do, _pe_copies, "rs_main_pair_wait")
    p16, p32 = _rs_chip_sum(place, g_packed, from_sibling, "rs_main_chip_sum")
    sc_main = _split_start(p16, p16.shape, _sc_copies, 3, p32, "rs_main_scatter_start")
    dq, dk, dv, dla = _gla_scan_bwd(p_all, la_all, do, ss, nb, t, tc, sc_main[4])
    dp, dw2, db2 = _gla_assemble(p_all, w2, b2, dq, dk, dv, dla, dgate, n)
    dhn_all = _mm(dp, w_gin, "nn", F32, "gla_in_dx", 768, 512)
    landed = _split_wait(sc_main, dhn_all, _sc_copies, "rs_main_scatter_wait")[1]
    g_main = _rs_pair_gather(_rs_final_sum(place, landed, p32, "rs_main_final_sum"), "rs_main_pair_gather")
    g_gin = _mm(dp, hn_all, "tn", F32, "gla_in_dw", 640, 1024)[:GLA_IN]
    g_gin = jnp.pad(g_gin.reshape(N_CHIPS, _GIN_ROWS, D), ((0, 0), (0, _GIN_PAD - _GIN_ROWS), (0, 0)))
    from_sibling = _rs_pair_exchange(g_gin, "rs_gin_pair_exchange")
    p16_gin, p32_gin = _rs_chip_sum(place, g_gin, from_sibling, "rs_gin_chip_sum")
    r = _mod_bwd(xf, dhn_all, vec(small["norm_mix"][0]), m[0][1], t, "mod0_main_bwd", dh_out=dh1)
    grad_x, dmods[0][1], dmods[0][0], dnm0 = r["dh"], r["dscale"], r["dshift"], r["dgain"]
    rc = _mod_bwd(cf, dhn_all, vec(small["norm_mix"][0]), mc_b[1], tc, "mod0_ctx_bwd", dhn_row0=n, need_dh=False)
    dmc = jnp.stack([jnp.sum(rc["dshift"], axis=0).reshape(D), jnp.sum(rc["dscale"], axis=0).reshape(D)])
    dnm0 = dnm0 + rc["dgain"]

    gs["norm_mix"] = jnp.concatenate([dnm0, dnm1], axis=0)
    gs["norm_ffn"] = jnp.concatenate([dnf0, dnf1], axis=0)
    gs["final_norm"] = dfinal.reshape(D)
    gs["gla_w_a2"] = jnp.stack([dw2[0:RANK, 0:KEY], dw2[RANK:2 * RANK, KEY:]])
    gs["gla_b_a"] = db2.reshape(2, KEY)
    gs["gla_head_norm"] = dhg.reshape(DV)
    gs["sc_conv_w"] = dscw
    gs["ffn_conv_w"] = jnp.stack([dcw0, dcw1])
    gs["ffn_conv_b"] = jnp.stack([dcb0, dcb1])
    dmods_arr = jnp.stack([jnp.stack([dmods[l][k].reshape(nb, D) for k in range(N_MOD)], axis=1) for l in range(2)])
    return loss, grad_x.reshape(nb, t, D), g_main, p16_gin, p32_gin, gs, dmods_arr, dmc


def _pack(arrs):
    parts, meta, off = [], [], 0
    for a in arrs:
        r = a.size // 128
        rp = -(-r // 8) * 8
        a2 = a.reshape(r, 128).astype(F32)
        if rp != r:
            a2 = jnp.pad(a2, ((0, rp - r), (0, 0)))
        parts.append(a2)
        meta.append((off, r, a.shape))
        off += rp
    return jnp.concatenate(parts, axis=0), meta


def _unpack(buf, meta, lead=()):
    return [buf[..., off:off + r, :].reshape(*lead, *shape) for off, r, shape in meta]


_MAIN = ("up_t0", "up_t1", "down0", "down1", "sc_in_t", "gla_out", "sc_out")
_MAIN_ROWS = {"sc_in_t": 3 * D // N_CHIPS, "up_t0": 2 * HID // N_CHIPS, "up_t1": 2 * HID // N_CHIPS,
              "gla_out": D // N_CHIPS, "sc_out": D // N_CHIPS, "down0": HID // N_CHIPS, "down1": HID // N_CHIPS}
_MAIN_TOTAL = sum(_MAIN_ROWS.values())
_GIN_ROWS = GLA_IN // N_CHIPS
_GIN_PAD = -(-_GIN_ROWS // 32) * 32


def _offsets(names, rows):
    off, out = 0, {}
    for k in names:
        out[k] = off
        off += rows[k]
    return out


def kernel(x, c, ctx, c_ctx, ada_w, ada_b, norm_mix, norm_ffn, gla_w_in, gla_w_a2, gla_b_a, gla_head_norm, gla_w_out, sc_w_in, sc_conv_w, sc_w_out, ffn_w_up, ffn_conv_w, ffn_conv_b, ffn_w_down, final_norm, loss_target, m_c_ctx, m_ada_w, m_ada_b, m_norm_mix, m_norm_ffn, m_gla_w_in, m_gla_w_a2, m_gla_b_a, m_gla_head_norm, m_gla_w_out, m_sc_w_in, m_sc_conv_w, m_sc_w_out, m_ffn_w_up, m_ffn_conv_w, m_ffn_conv_b, m_ffn_w_down, m_final_norm, v_c_ctx, v_ada_w, v_ada_b, v_norm_mix, v_norm_ffn, v_gla_w_in, v_gla_w_a2, v_gla_b_a, v_gla_head_norm, v_gla_w_out, v_sc_w_in, v_sc_conv_w, v_sc_w_out, v_ffn_w_up, v_ffn_conv_w, v_ffn_conv_b, v_ffn_w_down, v_final_norm):
    ix, iy, ic = _place()
    chip = 2 * ix + iy
    dev = 2 * chip + ic
    place = jnp.stack([ic, chip]).astype(jnp.int32)
    nb = x.shape[0]
    offs = _offsets(_MAIN, _MAIN_ROWS)

    buf, meta = _pack([c, ffn_conv_w, sc_conv_w, gla_w_a2, gla_b_a])
    got = _allgather_small(buf, "gather_small_in").reshape(N_DEV, buf.shape[0], 128)
    c_all, fcw, scw, wa2, ba = _unpack(got, meta, (N_DEV,))
    c_all = c_all.reshape(N_DEV * nb, D)
    per_chip = lambda a: a[0::2]
    ffn_conv_w_full = jnp.moveaxis(per_chip(fcw), 0, 2).reshape(2, 3, 2 * HID)
    sc_conv_w_full = jnp.moveaxis(per_chip(scw)[:, 0], 0, 1).reshape(3, D)
    gla_w_a2_full = jnp.moveaxis(per_chip(wa2)[:, 0], 0, 2).reshape(2, RANK, KEY)
    gla_b_a_full = jnp.moveaxis(per_chip(ba)[:, 0], 0, 1).reshape(2, KEY)

    cvec = jnp.concatenate([c_all, c_ctx.reshape(1, D), jnp.zeros((ADA_ROWS - N_DEV * nb - 1, D), F32)], axis=0)
    ada_b_sh = lax.dynamic_slice_in_dim(ada_b, chip * ADA_SH, ADA_SH, axis=1).reshape(2, 1, ADA_SH)
    mod_sh = _ada_fwd(cvec, ada_w, ada_b_sh)
    got = _allgather_small(mod_sh.reshape(2 * ADA_ROWS, ADA_SH), "gather_mod")
    mod_full = jnp.moveaxis(per_chip(got.reshape(N_DEV, 2, ADA_ROWS, ADA_SH)), 0, 2).reshape(2, ADA_ROWS, N_MOD * D)
    mc = mod_full[0, N_DEV * nb, :2 * D].reshape(2, D)

    own = {"sc_in_t": sc_w_in[0].T, "up_t0": ffn_w_up[0].T, "up_t1": ffn_w_up[1].T,
           "gla_out": gla_w_out[0], "sc_out": sc_w_out[0], "down0": ffn_w_down[0], "down1": ffn_w_down[1]}
    own_main = jnp.concatenate([own[k].astype(BF16) for k in _MAIN], axis=0)
    own_gin = jnp.pad(gla_w_in[0].T.astype(BF16), ((0, _GIN_PAD - _GIN_ROWS), (0, 0)))
    ag_gin = _split_start(own_gin, (N_CHIPS, _GIN_PAD, D), _ag_copies, 4, mc, "ag_gin_start")
    ag_main = _split_start(own_main, (N_CHIPS, _MAIN_TOTAL, D), _ag_copies, 4, ag_gin[4], "ag_main_start")
    mods = lax.dynamic_slice_in_dim(mod_full, dev * nb, nb, axis=1).reshape(2, nb, N_MOD, D) + ag_main[4][0, 0]

    small = {"norm_mix": norm_mix, "norm_ffn": norm_ffn, "final_norm": final_norm, "gla_w_a2": gla_w_a2_full,
             "gla_b_a": gla_b_a_full, "gla_head_norm": gla_head_norm[0], "sc_conv_w": sc_conv_w_full,
             "ffn_conv_w": ffn_conv_w_full, "ffn_conv_b": ffn_conv_b}
    loss_p, grad_x, g_main, p16_gin, p32_gin, gs, dmods, dmc = _local_step(x, ctx, loss_target, mods, mc, ag_gin,
                                                                           ag_main, place, small)

    sum_names = ["norm_mix", "norm_ffn", "final_norm", "gla_w_a2", "gla_b_a", "gla_head_norm", "sc_conv_w",
                 "ffn_conv_w", "ffn_conv_b"]
    buf, meta = _pack([jnp.broadcast_to(loss_p, (8, 128))] + [gs[k] for k in sum_names] + [dmc, dmods])
    n_sum = meta[-1][0]
    got = _allgather_small(buf, "gather_small_grads").reshape(N_DEV, buf.shape[0], 128)
    summed = _sum_slots(got[:, :n_sum], "sum_small_grads")
    parts = _unpack(summed, meta[:-1])
    loss = parts[0][0, 0]
    g_small = dict(zip(sum_names, parts[1:-1]))
    dmc_tot = parts[-1]
    dmods_all = jnp.moveaxis(_unpack(got, meta[-1:], (N_DEV,))[0], 0, 1).reshape(2, N_DEV * nb, N_MOD * D)

    ctx_row = jnp.stack([jnp.concatenate([dmc_tot.reshape(2 * D), jnp.zeros(((N_MOD - 2) * D,), F32)]),
                         jnp.zeros((N_MOD * D,), F32)]).reshape(2, 1, N_MOD * D)
    dmod_ext = jnp.concatenate([dmods_all, ctx_row, jnp.zeros((2, ADA_ROWS - N_DEV * nb - 1, N_MOD * D), F32)], axis=1)
    g_ada_b = _sum_slots(jnp.moveaxis(dmod_ext, 1, 0).reshape(ADA_ROWS, 2 * N_MOD * D // 128, 128),
                         "sum_ada_b").reshape(2, N_MOD * D)
    dmod_sh = lax.dynamic_slice_in_dim(dmod_ext, chip * ADA_SH, ADA_SH, axis=2)
    g_ada_w, dcv = _ada_bwd(cvec, ada_w, dmod_sh)
    dscc_part = (dcv[0, N_DEV * nb] + dcv[1, N_DEV * nb]).reshape(8, 128)
    got = _allgather_small(dscc_part, "gather_dscc").reshape(N_DEV, 8, 128)
    g_c_ctx = _cctx_grad(per_chip(got), c_ctx.reshape(8, 128)).reshape(D)

    sc_gin = _split_start(p16_gin, p16_gin.shape, _sc_copies, 3, g_c_ctx, "rs_gin_scatter_start")
    seg = {k: g_main[offs[k]:offs[k] + _MAIN_ROWS[k]] for k in _MAIN}

    sl_chip = lambda a, axis, width: lax.dynamic_slice_in_dim(a, chip * width, width, axis=axis)
    grads = {
        "c_ctx": g_c_ctx, "ada_w": g_ada_w, "ada_b": g_ada_b, "norm_mix": g_small["norm_mix"],
        "norm_ffn": g_small["norm_ffn"],
        "gla_w_a2": sl_chip(g_small["gla_w_a2"], 2, KEY // N_CHIPS)[None],
        "gla_b_a": sl_chip(g_small["gla_b_a"], 1, KEY // N_CHIPS)[None],
        "gla_head_norm": g_small["gla_head_norm"][None], "gla_w_out": seg["gla_out"][None],
        "sc_w_in": seg["sc_in_t"].T[None], "sc_conv_w": sl_chip(g_small["sc_conv_w"], 1, D // N_CHIPS)[None],
        "sc_w_out": seg["sc_out"][None], "ffn_w_up": jnp.stack([seg["up_t0"].T, seg["up_t1"].T]),
        "ffn_conv_w": sl_chip(g_small["ffn_conv_w"], 2, 2 * HID // N_CHIPS), "ffn_conv_b": g_small["ffn_conv_b"],
        "ffn_w_down": jnp.stack([seg["down0"], seg["down1"]]), "final_norm": g_small["final_norm"],
    }
    weights = {"c_ctx": c_ctx, "ada_w": ada_w, "ada_b": ada_b, "norm_mix": norm_mix, "norm_ffn": norm_ffn,
               "gla_w_in": gla_w_in, "gla_w_a2": gla_w_a2, "gla_b_a": gla_b_a, "gla_head_norm": gla_head_norm,
               "gla_w_out": gla_w_out, "sc_w_in": sc_w_in, "sc_conv_w": sc_conv_w, "sc_w_out": sc_w_out,
               "ffn_w_up": ffn_w_up, "ffn_conv_w": ffn_conv_w, "ffn_conv_b": ffn_conv_b, "ffn_w_down": ffn_w_down,
               "final_norm": final_norm}
    mom1 = {"c_ctx": m_c_ctx, "ada_w": m_ada_w, "ada_b": m_ada_b, "norm_mix": m_norm_mix, "norm_ffn": m_norm_ffn,
            "gla_w_in": m_gla_w_in, "gla_w_a2": m_gla_w_a2, "gla_b_a": m_gla_b_a, "gla_head_norm": m_gla_head_norm,
            "gla_w_out": m_gla_w_out, "sc_w_in": m_sc_w_in, "sc_conv_w": m_sc_conv_w, "sc_w_out": m_sc_w_out,
            "ffn_w_up": m_ffn_w_up, "ffn_conv_w": m_ffn_conv_w, "ffn_conv_b": m_ffn_conv_b,
            "ffn_w_down": m_ffn_w_down, "final_norm": m_final_norm}
    mom2 = {"c_ctx": v_c_ctx, "ada_w": v_ada_w, "ada_b": v_ada_b, "norm_mix": v_norm_mix, "norm_ffn": v_norm_ffn,
            "gla_w_in": v_gla_w_in, "gla_w_a2": v_gla_w_a2, "gla_b_a": v_gla_b_a, "gla_head_norm": v_gla_head_norm,
            "gla_w_out": v_gla_w_out, "sc_w_in": v_sc_w_in, "sc_conv_w": v_sc_conv_w, "sc_w_out": v_sc_w_out,
            "ffn_w_up": v_ffn_w_up, "ffn_conv_w": v_ffn_conv_w, "ffn_conv_b": v_ffn_conv_b,
            "ffn_w_down": v_ffn_w_down, "final_norm": v_final_norm}
    names = list(weights)

    big_names = ["ada_w", "gla_w_out", "sc_w_in", "sc_w_out", "ffn_w_up", "ffn_w_down", "gla_w_in"]
    small_names = [k for k in names if k not in big_names]
    delta, new_m, new_v = {}, {}, {}
    done = []

    def big_adamw(k, token):
        delta[k], new_m[k], new_v[k] = _adamw(weights[k], grads[k], mom1[k], mom2[k], "adamw_" + k, token)
        done.append(new_v[k][0, 0:1, 0:128])

    for k in big_names[:-1]:
        grads[k] = grads[k].reshape(weights[k].shape)
        big_adamw(k, sc_gin[4])
    for k in small_names:
        grads[k] = grads[k].reshape(weights[k].shape)
    packed = [_pack([src[k] for k in small_names]) for src in (weights, grads, mom1, mom2)]
    meta = packed[0][1]
    rows_pad = -packed[0][0].shape[0] % 128
    bufs = [jnp.pad(p[0], ((0, rows_pad), (0, 0)))[None] for p in packed]
    outs = _adamw(bufs[0], bufs[1], bufs[2], bufs[3], "adamw_small", sc_gin[4])
    done.append(outs[2][0, 0:1, :])
    for dst, o in zip((delta, new_m, new_v), outs):
        for k, a in zip(small_names, _unpack(o[0], meta)):
            dst[k] = a
    landed = _split_wait(sc_gin, jnp.concatenate(done, axis=0), _sc_copies, "rs_gin_scatter_wait")[1]
    g_gin_shard = _rs_pair_gather(_rs_final_sum(place, landed, p32_gin, "rs_gin_final_sum"), "rs_gin_pair_gather")
    grads["gla_w_in"] = g_gin_shard[:_GIN_ROWS].T[None]
    big_adamw("gla_w_in", sc_gin[4])

    return (loss, grad_x, *[grads[k] for k in names], *[delta[k] for k in names], *[new_m[k] for k in names],
            *[new_v[k] for k in names])
```

```python
import functools

import jax
import jax.numpy as jnp
from jax import lax
from jax.experimental import pallas as pl
from jax.experimental.pallas import tpu as pltpu

F32 = jnp.float32
BF16 = jnp.bfloat16
MESH = pl.DeviceIdType.MESH

EPS = 1e-6
D = 1024
N_MOD = 6
HEADS = 4
DK = 128
DV = 256
KEY = HEADS * DK
RANK = 16
TAU = 16.0
CH = 64
GRID_W = 64
HID = 2560
GLA_IN = 2 * KEY + 2 * D + 2 * RANK
GLA_IN_PAD = 3200
Q_SCALE = DK ** -0.5
N_CHIPS = 4
N_DEV = 8

ADAM_LR = 0.001
ADAM_B1 = 0.9
ADAM_B2 = 0.999
ADAM_EPS = 1e-08
ADAM_WD = 0.01
ADAM_STEP = 10

VMEM_LIMIT = 56 * 1024 * 1024


def _params(sem):
    return pltpu.CompilerParams(dimension_semantics=sem, vmem_limit_bytes=VMEM_LIMIT)


def _tile(n, pref, mult=8):
    if n <= pref:
        return n
    for t in range(pref - pref % mult, 0, -mult):
        if n % t == 0:
            return t
    raise ValueError((n, pref, mult))


_NN = (((1,), (0,)), ((), ()))
_NT = (((1,), (1,)), ((), ()))
_TN = (((0,), (0,)), ((), ()))


def _dot(a, b, dims=_NN):
    return lax.dot_general(a.astype(BF16), b.astype(BF16), dims, preferred_element_type=F32)


def _sigmoid(x):
    return 1.0 / (1.0 + jnp.exp(-x))


def _rowsum(x):
    return jnp.sum(x, axis=0, keepdims=True)


def _mm(a, b, form, out_dtype, name, tm, tn):
    if form == "tn":
        K, M = a.shape
    else:
        M, K = a.shape
    N = b.shape[0] if form == "nt" else b.shape[1]
    tm = _tile(M, tm, 128)
    tn = _tile(N, tn, 128)
    dims = {"nn": _NN, "nt": _NT, "tn": _TN}[form]

    def body(a_ref, b_ref, o_ref):
        o_ref[...] = _dot(a_ref[...], b_ref[...], dims).astype(o_ref.dtype)

    if form == "tn":
        a_spec = pl.BlockSpec((K, tm), lambda i, j: (0, i))
    else:
        a_spec = pl.BlockSpec((tm, K), lambda i, j: (i, 0))
    if form == "nt":
        b_spec = pl.BlockSpec((tn, K), lambda i, j: (j, 0))
    else:
        b_spec = pl.BlockSpec((K, tn), lambda i, j: (0, j))
    return pl.pallas_call(
        body,
        name=name,
        grid=(M // tm, N // tn),
        in_specs=[a_spec, b_spec],
        out_specs=pl.BlockSpec((tm, tn), lambda i, j: (i, j)),
        out_shape=jax.ShapeDtypeStruct((M, N), out_dtype),
        compiler_params=_params(("parallel", "parallel")),
    )(a, b)


def _mm_nt_w(a, wg, off, rows, name, tm, out_dtype):
    m = a.shape[0]
    tm = _tile(m, tm, 128)
    if N_CHIPS * rows <= D:

        def body_small(a_ref, w_ref, o_ref):
            av = a_ref[...]
            for s in range(N_CHIPS):
                o_ref[:, s * rows:(s + 1) * rows] = _dot(av, w_ref[s], _NT).astype(o_ref.dtype)

        return pl.pallas_call(
            body_small, name=name, grid=(m // tm,),
            in_specs=[pl.BlockSpec((tm, D), lambda i: (i, 0)),
                      pl.BlockSpec((N_CHIPS, rows, D), lambda i: (0, off // rows, 0))],
            out_specs=pl.BlockSpec((tm, N_CHIPS * rows), lambda i: (i, 0)),
            out_shape=jax.ShapeDtypeStruct((m, N_CHIPS * rows), out_dtype),
            compiler_params=_params(("parallel",)),
        )(a, wg)

    def body(a_ref, w_ref, o_ref):
        o_ref[...] = _dot(a_ref[...], w_ref[0], _NT).astype(o_ref.dtype)

    return pl.pallas_call(
        body, name=name, grid=(m // tm, N_CHIPS),
        in_specs=[pl.BlockSpec((tm, D), lambda i, s: (i, 0)),
                  pl.BlockSpec((1, rows, D), lambda i, s: (s, off // rows, 0))],
        out_specs=pl.BlockSpec((tm, rows), lambda i, s: (i, s)),
        out_shape=jax.ShapeDtypeStruct((m, N_CHIPS * rows), out_dtype),
        compiler_params=_params(("parallel", "parallel")),
    )(a, wg)


def _mm_nn_w(a3, wg, off, rows, name, tm, tn):
    parts, m, kp = a3.shape
    assert parts * kp == N_CHIPS * rows
    tm = _tile(m, tm, 128)
    cuts = sorted({s * rows for s in range(N_CHIPS + 1)} | {p * kp for p in range(parts + 1)})
    pieces = [(k0 // kp, k0 % kp, k0 // rows, k0 % rows, k1 - k0) for k0, k1 in zip(cuts[:-1], cuts[1:])]

    def body(a_ref, w_ref, o_ref):
        acc = None
        for p, a0, s, r0, width in pieces:
            term = _dot(a_ref[p, :, a0:a0 + width], w_ref[s, r0:r0 + width, :])
            acc = term if acc is None else acc + term
        o_ref[...] = acc

    return pl.pallas_call(
        body, name=name, grid=(m // tm, D // tn),
        in_specs=[pl.BlockSpec((parts, tm, kp), lambda i, j: (0, i, 0)),
                  pl.BlockSpec((N_CHIPS, rows, tn), lambda i, j: (0, off // rows, j))],
        out_specs=pl.BlockSpec((tm, tn), lambda i, j: (i, j)),
        out_shape=jax.ShapeDtypeStruct((m, D), F32),
        compiler_params=_params(("parallel", "parallel")),
    )(a3, wg)


def _mm_dw(a3, b, g_prev, off, rows, name, tm):
    parts, ntok, cdim = a3.shape
    assert parts * cdim == N_CHIPS * rows and cdim % tm == 0 and rows % tm == 0 and off % tm == 0

    def body(a_ref, b_ref, *rest):
        rest[-1][0] = _dot(a_ref[0], b_ref[...], _TN)

    in_specs = [pl.BlockSpec((1, ntok, tm), lambda i: ((i * tm) // cdim, 0, ((i * tm) % cdim) // tm)),
                pl.BlockSpec((ntok, D), lambda i: (0, 0))]
    args = [a3, b]
    aliases = {}
    if g_prev is not None:
        in_specs.append(pl.BlockSpec(memory_space=pl.ANY))
        args.append(g_prev)
        aliases = {2: 0}
    return pl.pallas_call(
        body, name=name, grid=(N_CHIPS * rows // tm,),
        in_specs=in_specs,
        out_specs=pl.BlockSpec((1, tm, D), lambda i: ((i * tm) // rows, (off + (i * tm) % rows) // tm, 0)),
        out_shape=jax.ShapeDtypeStruct((N_CHIPS, _MAIN_TOTAL, D), F32),
        input_output_aliases=aliases,
        compiler_params=_params(("parallel",)),
    )(*args)


def _mod_fwd(h, gain, shift, scale, tpb_rows, name, y=None, gate=None, out_rows=None, into=None, row0=0):
    n = h.shape[0]
    tt = _tile(tpb_rows, 256)
    tpb = tpb_rows // tt
    has_res = y is not None
    assert row0 % tt == 0 and not (has_res and out_rows)

    def body(*refs):
        if has_res:
            h_ref, y_ref, gate_ref, gain_ref, sh_ref, sc_ref, hout_ref, hn_ref = refs
            hv = h_ref[...] + gate_ref[0] * y_ref[...]
            hout_ref[...] = hv
        else:
            h_ref, gain_ref, sh_ref, sc_ref, hn_ref = refs[0], refs[1], refs[2], refs[3], refs[-1]
            hv = h_ref[...]
        r = lax.rsqrt(jnp.mean(hv * hv, axis=-1, keepdims=True) + EPS)
        hn = (hv * r) * gain_ref[...] * (1.0 + sc_ref[0]) + sh_ref[0]
        hn_ref[...] = hn.astype(BF16)

    row = pl.BlockSpec((tt, D), lambda i: (i, 0))
    per_b = pl.BlockSpec((1, 1, D), lambda i: (i // tpb, 0, 0))
    vec = pl.BlockSpec((1, D), lambda i: (0, 0))
    if has_res:
        in_specs = [row, row, per_b, vec, per_b, per_b]
        args = (h, y, gate, gain, shift, scale)
        out_specs = [row, row]
        out_shape = [jax.ShapeDtypeStruct((n, D), F32), jax.ShapeDtypeStruct((n, D), BF16)]
    else:
        in_specs = [row, vec, per_b, per_b]
        args = (h, gain, shift, scale)
        out_specs = pl.BlockSpec((tt, D), lambda i: (i + row0 // tt, 0))
        out_shape = jax.ShapeDtypeStruct((out_rows or n, D), BF16)
    aliases = {}
    if into is not None:
        in_specs = in_specs + [pl.BlockSpec(memory_space=pl.ANY)]
        args = args + (into,)
        aliases = {4: 0}
    return pl.pallas_call(
        body, name=name, grid=(n // tt,), in_specs=in_specs, out_specs=out_specs, out_shape=out_shape,
        input_output_aliases=aliases, compiler_params=_params(("parallel",)),
    )(*args)


def _mod_bwd(h_in, dhn, gain, scale, tpb_rows, name, dhn_row0=0, dh_out=None, y_prev=None, gate_prev=None,
             need_dh=True):
    n = h_in.shape[0]
    nb = n // tpb_rows
    tt = _tile(tpb_rows, 256)
    tpb = tpb_rows // tt
    off = dhn_row0 // tt
    assert dhn_row0 % tt == 0
    has_out = dh_out is not None
    has_prev = y_prev is not None

    def body(*refs):
        it = iter(refs)
        h_ref, dhn_ref, gain_ref, sc_ref = next(it), next(it), next(it), next(it)
        dho_ref = next(it) if has_out else None
        yp_ref, gp_ref = (next(it), next(it)) if has_prev else (None, None)
        dh_ref = next(it) if need_dh else None
        dsc_ref, dsh_ref, dgain_ref = next(it), next(it), next(it)
        dyp_ref, dgp_ref = (next(it), next(it)) if has_prev else (None, None)
        i = pl.program_id(0)

        @pl.when(i == 0)
        def _():
            dgain_ref[...] = jnp.zeros_like(dgain_ref)

        @pl.when(i % tpb == 0)
        def _():
            dsc_ref[...] = jnp.zeros_like(dsc_ref)
            dsh_ref[...] = jnp.zeros_like(dsh_ref)
            if has_prev:
                dgp_ref[...] = jnp.zeros_like(dgp_ref)

        hv = h_ref[...]
        r = lax.rsqrt(jnp.mean(hv * hv, axis=-1, keepdims=True) + EPS)
        y = hv * r
        gain_v = gain_ref[...]
        g = dhn_ref[...].astype(F32)
        dsh_ref[0] += _rowsum(g)
        dsc_ref[0] += _rowsum(g * (y * gain_v))
        drn = g * (1.0 + sc_ref[0])
        dgain_ref[...] += _rowsum(drn * y)
        if need_dh:
            dy = drn * gain_v
            dh = r * (dy - y * jnp.mean(dy * y, axis=-1, keepdims=True))
            if has_out:
                dh = dh + dho_ref[...]
            dh_ref[...] = dh
            if has_prev:
                dyp_ref[...] = (dh * gp_ref[0]).astype(BF16)
                dgp_ref[0] += _rowsum(dh * yp_ref[...])

    row = pl.BlockSpec((tt, D), lambda i: (i, 0))
    row_off = pl.BlockSpec((tt, D), lambda i: (i + off, 0))
    per_b = pl.BlockSpec((1, 1, D), lambda i: (i // tpb, 0, 0))
    vec = pl.BlockSpec((1, D), lambda i: (0, 0))
    in_specs = [row, row_off, vec, per_b]
    args = [h_in, dhn, gain, scale]
    if has_out:
        in_specs.append(row)
        args.append(dh_out)
    if has_prev:
        in_specs += [row, per_b]
        args += [y_prev, gate_prev]
    out_specs, out_shape, names = [], [], []
    if need_dh:
        out_specs.append(row)
        out_shape.append(jax.ShapeDtypeStruct((n, D), F32))
        names.append("dh")
    for nm in ("dscale", "dshift"):
        out_specs.append(per_b)
        out_shape.append(jax.ShapeDtypeStruct((nb, 1, D), F32))
        names.append(nm)
    out_specs.append(vec)
    out_shape.append(jax.ShapeDtypeStruct((1, D), F32))
    names.append("dgain")
    if has_prev:
        out_specs += [row, per_b]
        out_shape += [jax.ShapeDtypeStruct((n, D), BF16), jax.ShapeDtypeStruct((nb, 1, D), F32)]
        names += ["dy_prev", "dgate_prev"]
    outs = pl.pallas_call(
        body, name=name, grid=(n // tt,), in_specs=in_specs, out_specs=out_specs, out_shape=out_shape,
        compiler_params=_params(("arbitrary",)),
    )(*args)
    return dict(zip(names, outs))


def _final(h, f, gate, gain, tgt, tpb_rows):
    n = h.shape[0]
    nb = n // tpb_rows
    tt = _tile(tpb_rows, 256)
    tpb = tpb_rows // tt

    def body(h_ref, f_ref, gate_ref, gain_ref, tgt_ref, loss_ref, dh_ref, df_ref, dgate_ref, dgain_ref):
        i = pl.program_id(0)

        @pl.when(i == 0)
        def _():
            loss_ref[...] = jnp.zeros_like(loss_ref)
            dgain_ref[...] = jnp.zeros_like(dgain_ref)

        @pl.when(i % tpb == 0)
        def _():
            dgate_ref[...] = jnp.zeros_like(dgate_ref)

        fv = f_ref[...]
        gate_v = gate_ref[0]
        hv = h_ref[...] + gate_v * fv
        r = lax.rsqrt(jnp.mean(hv * hv, axis=-1, keepdims=True) + EPS)
        y = hv * r
        gain_v = gain_ref[...]
        e = y * gain_v - tgt_ref[...]
        s = jnp.sum(_rowsum(e * e), axis=1, keepdims=True) * (0.5 / D)
        loss_ref[...] += jnp.broadcast_to(s, loss_ref.shape)
        dout = e * (1.0 / D)
        dgain_ref[...] += _rowsum(dout * y)
        dy = dout * gain_v
        dh = r * (dy - y * jnp.mean(dy * y, axis=-1, keepdims=True))
        dh_ref[...] = dh
        df_ref[...] = (dh * gate_v).astype(BF16)
        dgate_ref[0] += _rowsum(dh * fv)

    row = pl.BlockSpec((tt, D), lambda i: (i, 0))
    per_b = pl.BlockSpec((1, 1, D), lambda i: (i // tpb, 0, 0))
    vec = pl.BlockSpec((1, D), lambda i: (0, 0))
    return pl.pallas_call(
        body, name="final_loss", grid=(n // tt,),
        in_specs=[row, row, per_b, vec, row],
        out_specs=[pl.BlockSpec((1, 128), lambda i: (0, 0)), row, row, per_b, vec],
        out_shape=[jax.ShapeDtypeStruct((1, 128), F32), jax.ShapeDtypeStruct((n, D), F32),
                   jax.ShapeDtypeStruct((n, D), BF16), jax.ShapeDtypeStruct((nb, 1, D), F32),
                   jax.ShapeDtypeStruct((1, D), F32)],
        compiler_params=_params(("arbitrary",)),
    )(h, f, gate, gain, tgt)


def _row_dn1(x):
    t = lax.broadcasted_iota(jnp.int32, x.shape, 0)
    return jnp.where(t % GRID_W == 0, 0.0, pltpu.roll(x, 1, 0))


def _row_up1(x):
    t = lax.broadcasted_iota(jnp.int32, x.shape, 0)
    return jnp.where(t % GRID_W == GRID_W - 1, 0.0, pltpu.roll(x, x.shape[0] - 1, 0))


def _silu(x):
    return x * _sigmoid(x)


def _dsilu(x):
    s = _sigmoid(x)
    return s * (1.0 + x * (1.0 - s))


def _row_ds(i):
    start = i * GRID_W
    return pl.ds(start if isinstance(start, int) else pl.multiple_of(start, GRID_W), GRID_W)


def _grid_row(ref, i, first, last):
    def rows(k):
        return ref[_row_ds(k), :].astype(F32)

    cur = rows(i)
    return (jnp.zeros_like(cur) if first else rows(i - 1)), cur, (jnp.zeros_like(cur) if last else rows(i + 1))


def _over_grid_rows(n_rows, step, carry):
    carry = step(0, carry, True, n_rows == 1)
    if n_rows > 2:
        carry = lax.fori_loop(1, n_rows - 1, lambda i, c: step(i, c, False, False), carry)
    if n_rows > 1:
        carry = step(n_rows - 1, carry, False, True)
    return carry


def _fold8(p):
    return p.reshape(GRID_W // 8, 8, p.shape[1]).sum(axis=0)


def _ffn_up_mid_fwd(hn, wg, off, cw, cb, nb, t, name):
    tcol = 256
    ncol = HID // tcol
    rows_sh = 2 * HID // N_CHIPS

    def conv(x, w_ref):
        zeros = jnp.zeros((GRID_W, x.shape[1]), x.dtype)
        down = jnp.concatenate([zeros, x[: x.shape[0] - GRID_W]], axis=0)
        up = jnp.concatenate([x[GRID_W:], zeros], axis=0)
        return down * w_ref[0:1, :] + x * w_ref[1:2, :] + up * w_ref[2:3, :]

    def body(h_ref, wa_ref, wg_ref, cwa_ref, cwg_ref, cba_ref, cbg_ref, u_ref, z_ref):
        hv = h_ref[...]
        ua = _dot(hv, wa_ref[0], _NT)
        ug = _dot(hv, wg_ref[0], _NT)
        u_ref[0] = ua.astype(BF16)
        u_ref[1] = ug.astype(BF16)
        a = conv(ua, cwa_ref) + cba_ref[...]
        gt = conv(ug, cwg_ref) + cbg_ref[...]
        z_ref[...] = (a * _silu(gt)).astype(BF16)

    def w_spec(part):
        def idx(b, j):
            n = part * HID + j * tcol
            return (n // rows_sh, (off + n % rows_sh) // tcol, 0)
        return pl.BlockSpec((1, tcol, D), idx)

    chan = lambda rows, part: pl.BlockSpec((rows, tcol), lambda b, j: (0, part * ncol + j))
    return pl.pallas_call(
        body, name=name, grid=(nb, ncol),
        in_specs=[pl.BlockSpec((t, D), lambda b, j: (b, 0)), w_spec(0), w_spec(1),
                  chan(3, 0), chan(3, 1), chan(1, 0), chan(1, 1)],
        out_specs=[pl.BlockSpec((2, t, tcol), lambda b, j: (0, b, j)), pl.BlockSpec((t, tcol), lambda b, j: (b, j))],
        out_shape=[jax.ShapeDtypeStruct((2, nb * t, HID), BF16), jax.ShapeDtypeStruct((nb * t, HID), BF16)],
        compiler_params=_params(("parallel", "parallel")),
    )(hn, wg, wg, cw, cw, cb, cb)


def _ffn_mid_bwd(u0, cw, cb, dz, nb, t, name):
    nc = HID // 128
    n_rows = t // GRID_W

    def body(ua3_ref, ug3_ref, wa_ref, wg_ref, ba_ref, bg_ref, dz_ref, du_ref, dw_ref, db_ref, dua_ref, dug_ref):
        ua_ref, ug_ref = ua3_ref.at[0], ug3_ref.at[0]
        b = pl.program_id(1)

        @pl.when(b == 0)
        def _():
            dw_ref[...] = jnp.zeros_like(dw_ref)
            db_ref[...] = jnp.zeros_like(db_ref)

        wa = [wa_ref[k:k + 1, :] for k in range(3)]
        wg = [wg_ref[k:k + 1, :] for k in range(3)]
        ba, bg = ba_ref[...], bg_ref[...]

        def pass1(i, acc, first, last):
            here = _row_ds(i)
            ap, ac, an = _grid_row(ua_ref, i, first, last)
            gp, gc, gn = _grid_row(ug_ref, i, first, last)
            a = ap * wa[0] + ac * wa[1] + an * wa[2] + ba
            gt = gp * wg[0] + gc * wg[1] + gn * wg[2] + bg
            dzv = dz_ref[here, :].astype(F32)
            s = _sigmoid(gt)
            silu = gt * s
            da = dzv * silu
            dg = (dzv * a) * (s + silu * (1.0 - s))
            dua_ref[here, :] = da
            dug_ref[here, :] = dg
            terms = (da, da * ap, da * ac, da * an, dg, dg * gp, dg * gc, dg * gn)
            return tuple(r + _fold8(p) for r, p in zip(acc, terms))

        zero = jnp.zeros((8, 128), F32)
        acc = _over_grid_rows(n_rows, pass1, (zero,) * 8)
        for part in range(2):
            db_ref[part] += _rowsum(acc[4 * part])
            for k in range(3):
                dw_ref[part, k:k + 1, :] += _rowsum(acc[4 * part + 1 + k])

        def pass2(i, carry, first, last):
            for part, (ref, w) in enumerate(((dua_ref, wa), (dug_ref, wg))):
                dp_, dc_, dn_ = _grid_row(ref, i, first, last)
                du_ref[part, _row_ds(i), :] = (dn_ * w[0] + dc_ * w[1] + dp_ * w[2]).astype(BF16)
            return carry

        _over_grid_rows(n_rows, pass2, 0)

    col = lambda rows, part: pl.BlockSpec((rows, 128), lambda j, b: (0, part * nc + j))
    part_of_u = lambda part: pl.BlockSpec((1, t, 128), lambda j, b: (part, b, j))
    return pl.pallas_call(
        body, name=name, grid=(nc, nb),
        in_specs=[part_of_u(0), part_of_u(1), col(3, 0), col(3, 1), col(1, 0), col(1, 1),
                  pl.BlockSpec((t, 128), lambda j, b: (b, j))],
        out_specs=[pl.BlockSpec((2, t, 128), lambda j, b: (0, b, j)), pl.BlockSpec((2, 3, 128), lambda j, b: (0, 0, j)),
                   pl.BlockSpec((2, 1, 128), lambda j, b: (0, 0, j))],
        out_shape=[jax.ShapeDtypeStruct((2, nb * t, HID), BF16), jax.ShapeDtypeStruct((2, 3, HID), F32),
                   jax.ShapeDtypeStruct((2, 1, HID), F32)],
        scratch_shapes=[pltpu.VMEM((t, 128), F32), pltpu.VMEM((t, 128), F32)],
        compiler_params=_params(("parallel", "arbitrary")),
    )(u0, u0, cw, cw, cb, cb, dz)


def _sc_mid_fwd(p, cw, nb, t):
    nc = D // 128

    def body(bg_ref, cg_ref, v_ref, w_ref, y_ref):
        cv = cg_ref[...].astype(F32) * v_ref[...].astype(F32)
        cc = _row_dn1(cv) * w_ref[0:1, :] + cv * w_ref[1:2, :] + _row_up1(cv) * w_ref[2:3, :]
        y_ref[...] = (bg_ref[...].astype(F32) * cc).astype(BF16)

    part = lambda k: pl.BlockSpec((t, 128), lambda j, b: (b, k * nc + j))
    return pl.pallas_call(
        body, name="sc_mid_fwd", grid=(nc, nb),
        in_specs=[part(0), part(1), part(2), pl.BlockSpec((3, 128), lambda j, b: (0, j))],
        out_specs=pl.BlockSpec((t, 128), lambda j, b: (b, j)),
        out_shape=jax.ShapeDtypeStruct((nb * t, D), BF16),
        compiler_params=_params(("parallel", "parallel")),
    )(p, p, p, cw)


def _sc_mid_bwd(p, cw, dyb, nb, t):
    nc = D // 128

    def body(bg_ref, cg_ref, v_ref, w_ref, dy_ref, dp_ref, dw_ref):
        b = pl.program_id(1)

        @pl.when(b == 0)
        def _():
            dw_ref[...] = jnp.zeros_like(dw_ref)

        w0, w1, w2 = w_ref[0:1, :], w_ref[1:2, :], w_ref[2:3, :]
        cg, v = cg_ref[...].astype(F32), v_ref[...].astype(F32)
        cv = cg * v
        cvd = _row_dn1(cv)
        cvu = _row_up1(cv)
        cc = cvd * w0 + cv * w1 + cvu * w2
        dy = dy_ref[...].astype(F32)
        dcc = dy * bg_ref[...].astype(F32)
        dw_ref[0:1, :] += _rowsum(dcc * cvd)
        dw_ref[1:2, :] += _rowsum(dcc * cv)
        dw_ref[2:3, :] += _rowsum(dcc * cvu)
        dcv = _row_up1(dcc) * w0 + dcc * w1 + _row_dn1(dcc) * w2
        dp_ref[0] = (dy * cc).astype(BF16)
        dp_ref[1] = (dcv * v).astype(BF16)
        dp_ref[2] = (dcv * cg).astype(BF16)

    part = lambda k: pl.BlockSpec((t, 128), lambda j, b: (b, k * nc + j))
    return pl.pallas_call(
        body, name="sc_mid_bwd", grid=(nc, nb),
        in_specs=[part(0), part(1), part(2), pl.BlockSpec((3, 128), lambda j, b: (0, j)),
                  pl.BlockSpec((t, 128), lambda j, b: (b, j))],
        out_specs=[pl.BlockSpec((3, t, 128), lambda j, b: (0, b, j)), pl.BlockSpec((3, 128), lambda j, b: (0, j))],
        out_shape=[jax.ShapeDtypeStruct((3, nb * t, D), BF16), jax.ShapeDtypeStruct((3, D), F32)],
        compiler_params=_params(("parallel", "arbitrary")),
    )(p, p, p, cw, dyb)


def _gla_decay_fwd(p_all, w2, b2):
    n = p_all.shape[0]
    tt = _tile(n, 512)

    def body(a_ref, w_ref, b_ref, la_ref):
        z = _dot(a_ref[...], w_ref[...]) + b_ref[...]
        la_ref[...] = (jnp.minimum(z, 0.0) - jnp.log(1.0 + jnp.exp(-jnp.abs(z)))) * (1.0 / TAU)

    return pl.pallas_call(
        body, name="gla_decay_fwd", grid=(n // tt,),
        in_specs=[pl.BlockSpec((tt, 128), lambda i: (i, (2 * KEY + 2 * D) // 128)),
                  pl.BlockSpec((128, 2 * KEY), lambda i: (0, 0)), pl.BlockSpec((1, 2 * KEY), lambda i: (0, 0))],
        out_specs=pl.BlockSpec((tt, 2 * KEY), lambda i: (i, 0)),
        out_shape=jax.ShapeDtypeStruct((n, 2 * KEY), F32),
        compiler_params=_params(("parallel",)),
    )(p_all, w2, b2)


def _gla_blocks(nb, nm, ncx):
    def main_idx(d, i):
        return jnp.clip(jnp.where(d == 0, i - ncx, nm - 1 - (i - ncx)), 0, nm - 1)

    def rowblk(d, b, i):
        cidx = jnp.where(d == 0, i, ncx - 1 - i)
        return jnp.where(i < ncx, nb * nm + b * ncx + cidx, b * nm + main_idx(d, i))

    def mainblk(d, b, i):
        return b * nm + main_idx(d, i)

    return rowblk, mainblk


def _gla_mask(d):
    row = lax.broadcasted_iota(jnp.int32, (CH, CH), 0)
    col = lax.broadcasted_iota(jnp.int32, (CH, CH), 1)
    diff = jnp.where(d == 0, row - col, col - row)
    mask = diff >= 0
    return mask, jnp.where(mask, 1.0, 0.0).astype(BF16), jnp.where(diff <= 0, 1.0, 0.0).astype(BF16)


def _tri_sum(m01, x):
    w = x.shape[1]
    hi = x.astype(BF16)
    r1 = x - hi.astype(F32)
    mid = r1.astype(BF16)
    lo = (r1 - mid.astype(F32)).astype(BF16)
    s = lax.dot_general(m01, jnp.concatenate([hi, mid, lo], axis=1), _NN, preferred_element_type=F32)
    return s[:, :w] + s[:, w:2 * w] + s[:, 2 * w:]


def _gla_chunk(q, k, g, bc):
    bl = _rowsum(g)
    eq = jnp.exp(bc)
    ek = jnp.exp(-bc)
    ed = jnp.exp(bl - bc)
    return bl, eq, ek, ed, q * Q_SCALE * eq, k * ek, k * ed


def _gla_scan_fwd(p_all, la_all, nb, t, tc):
    nm, ncx = t // CH, tc // CH
    nst = nm + ncx
    rowblk, mainblk = _gla_blocks(nb, nm, ncx)

    def body(*refs):
        ins, (o_refs, ss_refs, st_ref) = refs[:8], (refs[8:10], refs[10:12], refs[12])
        i = pl.program_id(1)

        @pl.when(i == 0)
        def _():
            st_ref[...] = jnp.zeros_like(st_ref)

        loaded = [r[...] for r in ins]
        states = [st_ref[j] for j in range(2 * HEADS)]
        outs, new_states = [[], []], []
        for d in range(2):
            q_all, k_all, v_all, g_all = loaded[4 * d:4 * d + 4]
            mask, m01, _ = _gla_mask(d)
            bc_all = _tri_sum(m01, g_all)
            for h in range(HEADS):
                ksl = slice(h * DK, (h + 1) * DK)
                v = v_all[:, h * DV:(h + 1) * DV]
                st = states[d * HEADS + h]
                bl, _, _, _, qs, ks, kd = _gla_chunk(q_all[:, ksl], k_all[:, ksl], g_all[:, ksl], bc_all[:, ksl])
                att = jnp.where(mask, _dot(qs, ks, _NT), 0.0)
                outs[d].append(_dot(qs, st, _NT) + _dot(att, v))
                new_states.append(st * jnp.exp(bl) + _dot(v, kd, _TN))
        for d in range(2):
            o_refs[d][...] = jnp.concatenate(outs[d], axis=1)
            for h in range(HEADS):
                ss_refs[d][0, 0, h] = states[d * HEADS + h]
                st_ref[d * HEADS + h] = new_states[d * HEADS + h]

    def in_specs(d):
        return [pl.BlockSpec((CH, KEY), lambda b, i: (rowblk(d, b, i), 0)),
                pl.BlockSpec((CH, KEY), lambda b, i: (rowblk(d, b, i), 1)),
                pl.BlockSpec((CH, D), lambda b, i: (rowblk(d, b, i), 1)),
                pl.BlockSpec((CH, KEY), lambda b, i: (rowblk(d, b, i), d))]

    outs = pl.pallas_call(
        body, name="gla_scan_fwd", grid=(nb, nst),
        in_specs=in_specs(0) + in_specs(1),
        out_specs=[pl.BlockSpec((CH, D), lambda b, i: (mainblk(0, b, i), 0)),
                   pl.BlockSpec((CH, D), lambda b, i: (mainblk(1, b, i), 0)),
                   pl.BlockSpec((1, 1, HEADS, DV, DK), lambda b, i: (b, i, 0, 0, 0)),
                   pl.BlockSpec((1, 1, HEADS, DV, DK), lambda b, i: (b, i, 0, 0, 0))],
        out_shape=[jax.ShapeDtypeStruct((nb * t, D), F32)] * 2
        + [jax.ShapeDtypeStruct((nb, nst, HEADS, DV, DK), F32)] * 2,
        scratch_shapes=[pltpu.VMEM((2 * HEADS, DV, DK), F32)],
        compiler_params=_params(("parallel", "arbitrary")),
    )(*([p_all, p_all, p_all, la_all] * 2))
    return outs[:2], outs[2:]


def _gla_scan_bwd(p_all, la_all, do, ss, nb, t, tc, after):
    nm, ncx = t // CH, tc // CH
    nst = nm + ncx
    ntot = nb * (t + tc)
    rowblk, mainblk = _gla_blocks(nb, nm, ncx)

    def body(*refs):
        ins, outs, dst_ref = refs[:12], refs[13:21], refs[21]
        ip = pl.program_id(1)
        i = nst - 1 - ip

        @pl.when(ip == 0)
        def _():
            dst_ref[...] = jnp.zeros_like(dst_ref)

        live = jnp.where(i >= ncx, 1.0, 0.0)
        loaded = [[r[...] for r in ins[6 * d:6 * d + 5]] for d in range(2)]
        states = [ins[6 * d + 5][0, 0, h] for d in range(2) for h in range(HEADS)]
        dstates = [dst_ref[j] for j in range(2 * HEADS)]
        results, new_dstates = [], []
        for d in range(2):
            q_all, k_all, v_all, g_all, do_all = loaded[d]
            do_all = do_all * live
            mask, m01, m01_t = _gla_mask(d)
            bc_all = _tri_sum(m01, g_all)
            dqs_l, dks_l, dvs_l, dbs_l, dbls_l = [], [], [], [], []
            for h in range(HEADS):
                ksl = slice(h * DK, (h + 1) * DK)
                vsl = slice(h * DV, (h + 1) * DV)
                bl, eq, ek, ed, qs, ks, kd = _gla_chunk(q_all[:, ksl], k_all[:, ksl], g_all[:, ksl], bc_all[:, ksl])
                st, dst, v, dov = states[d * HEADS + h], dstates[d * HEADS + h], v_all[:, vsl], do_all[:, vsl]
                att = jnp.where(mask, _dot(qs, ks, _NT), 0.0)
                datt = jnp.where(mask, _dot(dov, v, _NT), 0.0)
                dqs = _dot(dov, st) + _dot(datt, ks)
                dks = _dot(datt, qs, _TN)
                dvs_l.append(_dot(att, dov, _TN) + _dot(kd, dst, _NT))
                dkd = _dot(v, dst)
                e = jnp.exp(bl)
                dbls_l.append(e * _rowsum(st * dst) + _rowsum(dkd * kd))
                new_dstates.append(_dot(dov, qs, _TN) + dst * e)
                dqs_l.append(dqs * eq * Q_SCALE)
                dks_l.append(dks * ek + dkd * ed)
                dbs_l.append(dqs * qs - dks * ks - dkd * kd)
            results.append((jnp.concatenate(dqs_l, axis=1), jnp.concatenate(dks_l, axis=1),
                            jnp.concatenate(dvs_l, axis=1),
                            _tri_sum(m01_t, jnp.concatenate(dbs_l, axis=1)) + jnp.concatenate(dbls_l, axis=1)))
        for d in range(2):
            for k in range(4):
                outs[4 * d + k][...] = results[d][k]
        for j in range(2 * HEADS):
            dst_ref[j] = new_dstates[j]

    def in_specs(d):
        return [pl.BlockSpec((CH, KEY), lambda b, ip: (rowblk(d, b, nst - 1 - ip), 0)),
                pl.BlockSpec((CH, KEY), lambda b, ip: (rowblk(d, b, nst - 1 - ip), 1)),
                pl.BlockSpec((CH, D), lambda b, ip: (rowblk(d, b, nst - 1 - ip), 1)),
                pl.BlockSpec((CH, KEY), lambda b, ip: (rowblk(d, b, nst - 1 - ip), d)),
                pl.BlockSpec((CH, D), lambda b, ip: (mainblk(d, b, nst - 1 - ip), 0)),
                pl.BlockSpec((1, 1, HEADS, DV, DK), lambda b, ip: (b, nst - 1 - ip, 0, 0, 0))]

    def out_specs(d):
        row = lambda width: pl.BlockSpec((CH, width), lambda b, ip: (rowblk(d, b, nst - 1 - ip), 0))
        return [row(KEY), row(KEY), row(D), row(KEY)]

    shapes = [jax.ShapeDtypeStruct((ntot, KEY), F32), jax.ShapeDtypeStruct((ntot, KEY), F32),
              jax.ShapeDtypeStruct((ntot, D), F32), jax.ShapeDtypeStruct((ntot, KEY), F32)]
    outs = pl.pallas_call(
        body, name="gla_scan_bwd", grid=(nb, nst),
        in_specs=in_specs(0) + in_specs(1) + [pl.BlockSpec(memory_space=pl.ANY)],
        out_specs=out_specs(0) + out_specs(1),
        out_shape=shapes * 2,
        scratch_shapes=[pltpu.VMEM((2 * HEADS, DV, DK), F32)],
        compiler_params=_params(("parallel", "arbitrary")),
    )(p_all, p_all, p_all, la_all, do, ss[0], p_all, p_all, p_all, la_all, do, ss[1], after)
    return [[outs[k], outs[4 + k]] for k in range(4)]


def _gla_post_fwd(o2, p_all, head_gain, n):
    tt = _tile(n, 256)

    def body(of_ref, ob_ref, g_ref, hg_ref, y_ref):
        o = of_ref[...] + ob_ref[...]
        gv = g_ref[...]
        hg = hg_ref[...]
        for h in range(HEADS):
            oh = o[:, h * DV:(h + 1) * DV]
            r = lax.rsqrt(jnp.mean(oh * oh, axis=-1, keepdims=True) + EPS)
            y_ref[:, h * DV:(h + 1) * DV] = ((oh * r) * hg * _silu(gv[:, h * DV:(h + 1) * DV])).astype(BF16)

    row = pl.BlockSpec((tt, D), lambda i: (i, 0))
    return pl.pallas_call(
        body, name="gla_post_fwd", grid=(n // tt,),
        in_specs=[row, row, pl.BlockSpec((tt, D), lambda i: (i, 2)), pl.BlockSpec((1, DV), lambda i: (0, 0))],
        out_specs=row,
        out_shape=jax.ShapeDtypeStruct((n, D), BF16),
        compiler_params=_params(("parallel",)),
    )(o2[0], o2[1], p_all, head_gain)


def _gla_post_bwd(o2, p_all, head_gain, dyb, n):
    tt = _tile(n, 256)

    def body(of_ref, ob_ref, g_ref, hg_ref, dy_ref, do_ref, dg_ref, dhg_ref):
        i = pl.program_id(0)

        @pl.when(i == 0)
        def _():
            dhg_ref[...] = jnp.zeros_like(dhg_ref)

        o = of_ref[...] + ob_ref[...]
        gv = g_ref[...]
        hg = hg_ref[...]
        dy = dy_ref[...]
        acc = jnp.zeros((1, DV), F32)
        for h in range(HEADS):
            sl = slice(h * DV, (h + 1) * DV)
            oh = o[:, sl]
            r = lax.rsqrt(jnp.mean(oh * oh, axis=-1, keepdims=True) + EPS)
            on = oh * r
            gh = gv[:, sl]
            dyh = dy[:, sl]
            dg_ref[:, sl] = dyh * (on * hg) * _dsilu(gh)
            dog = dyh * _silu(gh)
            acc = acc + _rowsum(dog * on)
            don = dog * hg
            do_ref[:, sl] = r * (don - on * jnp.mean(don * on, axis=-1, keepdims=True))
        dhg_ref[...] += acc

    return pl.pallas_call(
        body, name="gla_post_bwd", grid=(n // tt,),
        in_specs=[pl.BlockSpec((tt, D), lambda i: (i, 0)), pl.BlockSpec((tt, D), lambda i: (i, 0)),
                  pl.BlockSpec((tt, D), lambda i: (i, 2)),
                  pl.BlockSpec((1, DV), lambda i: (0, 0)), pl.BlockSpec((tt, D), lambda i: (i, 0))],
        out_specs=[pl.BlockSpec((tt, D), lambda i: (i, 0)), pl.BlockSpec((tt, D), lambda i: (i, 0)),
                   pl.BlockSpec((1, DV), lambda i: (0, 0))],
        out_shape=[jax.ShapeDtypeStruct((n, D), F32), jax.ShapeDtypeStruct((n, D), F32),
                   jax.ShapeDtypeStruct((1, DV), F32)],
        compiler_params=_params(("arbitrary",)),
    )(o2[0], o2[1], p_all, head_gain, dyb)


def _gla_assemble(p_all, w2, b2, dq, dk, dv, dla, dgate, n):
    ntot = p_all.shape[0]
    tt = _tile(n, 128)
    nmain = n // tt
    assert ntot % tt == 0

    def body(a_ref, w_ref, b_ref, dqf_ref, dqb_ref, dkf_ref, dkb_ref, dvf_ref, dvb_ref, dlf_ref, dlb_ref, dg_ref,
             dp_ref, dw_ref, db_ref):
        i = pl.program_id(0)

        @pl.when(i == 0)
        def _():
            dw_ref[...] = jnp.zeros_like(dw_ref)
            db_ref[...] = jnp.zeros_like(db_ref)

        a = a_ref[...]
        w = w_ref[...]
        z = _dot(a, w) + b_ref[...]
        dla = jnp.concatenate([dlf_ref[...], dlb_ref[...]], axis=1)
        dz = dla * (1.0 / (1.0 + jnp.exp(z))) * (1.0 / TAU)
        dw_ref[...] += _dot(a, dz, _TN)
        db_ref[...] += _rowsum(dz)
        dp_ref[:, 0:KEY] = (dqf_ref[...] + dqb_ref[...]).astype(BF16)
        dp_ref[:, KEY:2 * KEY] = (dkf_ref[...] + dkb_ref[...]).astype(BF16)
        dp_ref[:, 2 * KEY:2 * KEY + D] = (dvf_ref[...] + dvb_ref[...]).astype(BF16)
        dp_ref[:, 2 * KEY + D:2 * KEY + 2 * D] = (dg_ref[...] * jnp.where(i < nmain, 1.0, 0.0)).astype(BF16)
        dp_ref[:, 2 * KEY + 2 * D:GLA_IN_PAD] = _dot(dz, w, _NT).astype(BF16)

    row = lambda width: pl.BlockSpec((tt, width), lambda i: (i, 0))
    return pl.pallas_call(
        body, name="gla_assemble", grid=(ntot // tt,),
        in_specs=[pl.BlockSpec((tt, 128), lambda i: (i, (2 * KEY + 2 * D) // 128)),
                  pl.BlockSpec((128, 2 * KEY), lambda i: (0, 0)), pl.BlockSpec((1, 2 * KEY), lambda i: (0, 0)),
                  row(KEY), row(KEY), row(KEY), row(KEY), row(D), row(D), row(KEY), row(KEY),
                  pl.BlockSpec((tt, D), lambda i: (jnp.minimum(i, nmain - 1), 0))],
        out_specs=[pl.BlockSpec((tt, GLA_IN_PAD), lambda i: (i, 0)), pl.BlockSpec((128, 2 * KEY), lambda i: (0, 0)),
                   pl.BlockSpec((1, 2 * KEY), lambda i: (0, 0))],
        out_shape=[jax.ShapeDtypeStruct((ntot, GLA_IN_PAD), BF16), jax.ShapeDtypeStruct((128, 2 * KEY), F32),
                   jax.ShapeDtypeStruct((1, 2 * KEY), F32)],
        compiler_params=_params(("arbitrary",)),
    )(p_all, w2, b2, dq[0], dq[1], dk[0], dk[1], dv[0], dv[1], dla[0], dla[1], dgate)


ADA_ROWS = 24
ADA_SH = N_MOD * D // N_CHIPS


def _ada_fwd(cvec, ada_w, ada_b_sh):
    def body(c_ref, w_ref, b_ref, o_ref):
        o_ref[0] = _dot(_silu(c_ref[...]), w_ref[0]) + b_ref[0]

    return pl.pallas_call(
        body, name="ada_fwd", grid=(2,),
        in_specs=[pl.BlockSpec((ADA_ROWS, D), lambda l: (0, 0)), pl.BlockSpec((1, D, ADA_SH), lambda l: (l, 0, 0)),
                  pl.BlockSpec((1, 1, ADA_SH), lambda l: (l, 0, 0))],
        out_specs=pl.BlockSpec((1, ADA_ROWS, ADA_SH), lambda l: (l, 0, 0)),
        out_shape=jax.ShapeDtypeStruct((2, ADA_ROWS, ADA_SH), F32),
        compiler_params=_params(("parallel",)),
    )(cvec, ada_w, ada_b_sh)


def _ada_bwd(cvec, ada_w, dmod_sh):
    def body(c_ref, w_ref, dm_ref, gw_ref, dc_ref):
        dm = dm_ref[0]
        gw_ref[0] = _dot(_silu(c_ref[...]), dm, _TN)
        dc_ref[0] = _dot(dm, w_ref[0], _NT)

    return pl.pallas_call(
        body, name="ada_bwd", grid=(2,),
        in_specs=[pl.BlockSpec((ADA_ROWS, D), lambda l: (0, 0)), pl.BlockSpec((1, D, ADA_SH), lambda l: (l, 0, 0)),
                  pl.BlockSpec((1, ADA_ROWS, ADA_SH), lambda l: (l, 0, 0))],
        out_specs=[pl.BlockSpec((1, D, ADA_SH), lambda l: (l, 0, 0)), pl.BlockSpec((1, ADA_ROWS, D), lambda l: (l, 0, 0))],
        out_shape=[jax.ShapeDtypeStruct((2, D, ADA_SH), F32), jax.ShapeDtypeStruct((2, ADA_ROWS, D), F32)],
        compiler_params=_params(("parallel",)),
    )(cvec, ada_w, dmod_sh)


def _sum_slots(x, name):
    s, r, _ = x.shape

    def body(x_ref, o_ref):
        acc = x_ref[0]
        for k in range(1, s):
            acc = acc + x_ref[k]
        o_ref[...] = acc

    return pl.pallas_call(
        body, name=name, out_shape=jax.ShapeDtypeStruct((r, 128), F32),
        in_specs=[pl.BlockSpec(memory_space=pltpu.VMEM)], out_specs=pl.BlockSpec(memory_space=pltpu.VMEM),
    )(x)


def _cctx_grad(dscc_parts, c_ctx):
    def body(p_ref, c_ref, o_ref):
        acc = p_ref[0]
        for k in range(1, N_CHIPS):
            acc = acc + p_ref[k]
        o_ref[...] = acc * _dsilu(c_ref[...])

    return pl.pallas_call(
        body, name="cctx_grad", out_shape=jax.ShapeDtypeStruct((8, 128), F32),
        in_specs=[pl.BlockSpec(memory_space=pltpu.VMEM)] * 2, out_specs=pl.BlockSpec(memory_space=pltpu.VMEM),
    )(dscc_parts, c_ctx)


def _adamw(w, g, m, v, name, after):
    nl, r, cdim = w.shape
    tr = _tile(r, 256)
    c1 = 1.0 - ADAM_B1 ** ADAM_STEP
    c2 = 1.0 - ADAM_B2 ** ADAM_STEP

    def body(w_ref, g_ref, m_ref, v_ref, after_ref, d_ref, mo_ref, vo_ref):
        gv = g_ref[...]
        mn = ADAM_B1 * m_ref[...] + (1.0 - ADAM_B1) * gv
        vn = ADAM_B2 * v_ref[...] + (1.0 - ADAM_B2) * (gv * gv)
        mo_ref[...] = mn
        vo_ref[...] = vn
        d_ref[...] = -ADAM_LR * ((mn / c1) / (jnp.sqrt(vn / c2) + ADAM_EPS) + ADAM_WD * w_ref[...])

    spec = pl.BlockSpec((1, tr, cdim), lambda l, i: (l, i, 0))
    sds = jax.ShapeDtypeStruct((nl, r, cdim), F32)
    return pl.pallas_call(
        body, name=name, grid=(nl, r // tr), in_specs=[spec] * 4 + [pl.BlockSpec(memory_space=pl.ANY)],
        out_specs=[spec] * 3, out_shape=[sds] * 3, compiler_params=_params(("parallel", "parallel")),
    )(w, g, m, v, after)


def _place():
    x, y, c = lax.axis_index("x"), lax.axis_index("y"), lax.axis_index("c")
    return x, y, c


def _allgather_small(blk, name):
    m_per, n = blk.shape

    def body(x_ref, out_ref, send_sems, recv_sems, local_sem):
        x, y, c = _place()
        me, sibling = (x, y, c), (x, y, 1 - c)
        chips = [(1 - x, y), (x, 1 - y), (1 - x, 1 - y)]

        def rows(px, py, pc):
            return out_ref.at[pl.ds((4 * px + 2 * py + pc) * m_per, m_per), :]

        def copy(k, block, to, src=None):
            return pltpu.make_async_remote_copy(
                src_ref=rows(*block) if src is None else src, dst_ref=rows(*block),
                send_sem=send_sems.at[k], recv_sem=recv_sems.at[k], device_id=to, device_id_type=MESH)

        mine = pltpu.make_async_copy(x_ref, rows(*me), local_sem)
        mine.start()
        first = [copy(0, me, sibling, src=x_ref)]
        first += [copy(1 + j, me, (*chip, c), src=x_ref) for j, chip in enumerate(chips)]
        for cp in first:
            cp.start()
        passed = [copy(4 + j, (*chip, c), sibling) for j, chip in enumerate(chips)]
        for j, chip in enumerate(chips):
            copy(1 + j, (*chip, c), me).wait_recv()
            passed[j].start()
        copy(0, sibling, me).wait_recv()
        for j, chip in enumerate(chips):
            copy(4 + j, (*chip, 1 - c), me).wait_recv()
        for cp in first + passed:
            cp.wait_send()
        mine.wait()

    return pl.pallas_call(
        body, name=name,
        out_shape=jax.ShapeDtypeStruct((N_DEV * m_per, n), blk.dtype),
        in_specs=[pl.BlockSpec(memory_space=pltpu.VMEM)],
        out_specs=pl.BlockSpec(memory_space=pltpu.VMEM),
        scratch_shapes=[pltpu.SemaphoreType.DMA((7,)), pltpu.SemaphoreType.DMA((7,)), pltpu.SemaphoreType.DMA],
    )(blk)


def _other_chips(x, y):
    return [(1 - x, y), (x, 1 - y), (1 - x, 1 - y)]


_HBM_SPEC = pl.BlockSpec(memory_space=pltpu.HBM)
_SEM_SPEC = pl.BlockSpec(memory_space=pltpu.SEMAPHORE)
_SPLIT_PARAMS = pltpu.CompilerParams(has_side_effects=pltpu.SideEffectType.DATAFLOW_SIDE_EFFECTING)


def _in_hbm(a):
    return pltpu.with_memory_space_constraint(a, pltpu.HBM)


def _ag_copies(own_ref, land_ref, send_sems, recv_sems):
    x, y, c = _place()
    chip = 2 * x + y
    hr = own_ref.shape[0] // 2

    def half(ch):
        return land_ref.at[ch, pl.ds(c * hr, hr), :]

    def copy(k, src, dst, to):
        return pltpu.make_async_remote_copy(src_ref=src, dst_ref=dst, send_sem=send_sems.at[k],
                                            recv_sem=recv_sems.at[k], device_id=to, device_id_type=MESH)

    sends, expects = [], []
    for j, (ox, oy) in enumerate(_other_chips(x, y)):
        sends.append(copy(j, own_ref.at[pl.ds(c * hr, hr), :], half(chip), (ox, oy, c)))
        expects.append(copy(j, half(2 * ox + oy), half(2 * ox + oy), (ox, oy, c)))
    own_slot = copy(3, own_ref, land_ref.at[chip], (x, y, 1 - c))
    return sends + [own_slot], expects + [own_slot]


def _sc_copies(p_ref, land_ref, send_sems, recv_sems):
    x, y, c = _place()
    chip = 2 * x + y
    sends, expects = [], []
    for j, (ox, oy) in enumerate(_other_chips(x, y)):
        och = 2 * ox + oy
        mk = lambda dst_slot: pltpu.make_async_remote_copy(
            src_ref=p_ref.at[och], dst_ref=land_ref.at[dst_slot], send_sem=send_sems.at[j],
            recv_sem=recv_sems.at[j], device_id=(ox, oy, c), device_id_type=MESH)
        sends.append(mk(chip))
        expects.append(mk(och))
    return sends, expects


def _pe_copies(g_ref, land_ref, send_sems, recv_sems):
    x, y, c = _place()
    hr = g_ref.shape[1] // 2
    cp = pltpu.make_async_remote_copy(
        src_ref=g_ref.at[:, pl.ds((1 - c) * hr, hr), :], dst_ref=land_ref, send_sem=send_sems.at[0],
        recv_sem=recv_sems.at[0], device_id=(x, y, 1 - c), device_id_type=MESH)
    return [cp], [cp]


def _split_start(src, land_shape, copies, n_copies, after, name):
    def body(src_ref, land_ref, after_ref, send_sems, recv_sems, src_thru, land_thru, token):
        for cp in copies(src_ref, land_ref, send_sems, recv_sems)[0]:
            cp.start()
        token[...] = jnp.zeros_like(token)

    land = lax.empty(land_shape, src.dtype)
    return pl.pallas_call(
        body, name=name,
        out_shape=(pltpu.SemaphoreType.DMA((n_copies,)), pltpu.SemaphoreType.DMA((n_copies,)),
                   pltpu.HBM(src.shape, src.dtype), pltpu.HBM(land_shape, src.dtype),
                   jax.ShapeDtypeStruct((8, 128), F32)),
        in_specs=(_HBM_SPEC, _HBM_SPEC, pl.BlockSpec(memory_space=pl.ANY)),
        out_specs=(_SEM_SPEC, _SEM_SPEC, _HBM_SPEC, _HBM_SPEC, pl.BlockSpec(memory_space=pltpu.VMEM)),
        input_output_aliases={0: 2, 1: 3}, compiler_params=_SPLIT_PARAMS,
    )(_in_hbm(src), _in_hbm(land), after)


def _split_wait(started, after, copies, name):
    send_sems, recv_sems, src_thru, land_thru, _ = started

    def body(src_ref, land_ref, send_sems, recv_sems, after_ref, src_dead, got_ref):
        sends, expects = copies(src_ref, land_ref, send_sems, recv_sems)
        for cp in sends:
            cp.wait_send()
        for cp in expects:
            cp.wait_recv()

    return pl.pallas_call(
        body, name=name,
        out_shape=(pltpu.HBM(src_thru.shape, src_thru.dtype), pltpu.HBM(land_thru.shape, land_thru.dtype)),
        in_specs=(_HBM_SPEC, _HBM_SPEC, _SEM_SPEC, _SEM_SPEC, pl.BlockSpec(memory_space=pl.ANY)),
        out_specs=(_HBM_SPEC, _HBM_SPEC), input_output_aliases={0: 0, 1: 1}, compiler_params=_SPLIT_PARAMS,
    )(src_thru, land_thru, send_sems, recv_sems, after)


def _ag_pass_on(land, name):
    hr = land.shape[1] // 2

    def body(in_ref, out_ref, send_sems, recv_sems):
        x, y, c = _place()

        def copy(j, ox, oy, cc):
            ref = out_ref.at[2 * ox + oy, pl.ds(cc * hr, hr), :]
            return pltpu.make_async_remote_copy(src_ref=ref, dst_ref=ref, send_sem=send_sems.at[j],
                                                recv_sem=recv_sems.at[j], device_id=(x, y, 1 - c),
                                                device_id_type=MESH)

        others = _other_chips(x, y)
        for j, (ox, oy) in enumerate(others):
            copy(j, ox, oy, c).start()
        for j, (ox, oy) in enumerate(others):
            copy(j, ox, oy, 1 - c).wait_recv()
        for j, (ox, oy) in enumerate(others):
            copy(j, ox, oy, c).wait_send()

    any_spec = pl.BlockSpec(memory_space=pl.ANY)
    return pl.pallas_call(
        body, name=name, out_shape=jax.ShapeDtypeStruct(land.shape, land.dtype),
        in_specs=[any_spec], out_specs=any_spec, input_output_aliases={0: 0},
        scratch_shapes=[pltpu.SemaphoreType.DMA((3,)), pltpu.SemaphoreType.DMA((3,))],
    )(land)


def _rs_pair_exchange(g, name):
    r = g.shape[1]
    hr = r // 2

    def body(g_ref, got_ref, send_sem, recv_sem):
        x, y, c = _place()
        cp = pltpu.make_async_remote_copy(
            src_ref=g_ref.at[:, pl.ds((1 - c) * hr, hr), :], dst_ref=got_ref, send_sem=send_sem, recv_sem=recv_sem,
            device_id=(x, y, 1 - c), device_id_type=MESH)
        cp.start()
        cp.wait()

    any_spec = pl.BlockSpec(memory_space=pl.ANY)
    return pl.pallas_call(
        body, name=name,
        out_shape=jax.ShapeDtypeStruct((N_CHIPS, hr, D), F32),
        in_specs=[any_spec], out_specs=any_spec,
        scratch_shapes=[pltpu.SemaphoreType.DMA, pltpu.SemaphoreType.DMA],
    )(g)


def _rs_chip_sum(place, g, got, name):
    r = g.shape[1]
    hr = r // 2
    tr = _tile(hr, 640, 16)
    nt = hr // tr

    def body(pl_ref, g_ref, got_ref, p16_ref, p32_ref):
        s = pl.program_id(1)
        p = g_ref[0] + got_ref[0]
        p16_ref[0] = p.astype(BF16)

        @pl.when(s == pl_ref[1])
        def _():
            p32_ref[...] = p

    return pl.pallas_call(
        body, name=name,
        grid_spec=pltpu.PrefetchScalarGridSpec(
            num_scalar_prefetch=1, grid=(nt, N_CHIPS),
            in_specs=[pl.BlockSpec((1, tr, D), lambda i, s, pr: (s, pr[0] * nt + i, 0)),
                      pl.BlockSpec((1, tr, D), lambda i, s, pr: (s, i, 0))],
            out_specs=[pl.BlockSpec((1, tr, D), lambda i, s, pr: (s, i, 0)),
                       pl.BlockSpec((tr, D), lambda i, s, pr: (i, 0))]),
        out_shape=[jax.ShapeDtypeStruct((N_CHIPS, hr, D), BF16), jax.ShapeDtypeStruct((hr, D), F32)],
        compiler_params=_params(("parallel", "arbitrary")),
    )(place, g, got)


def _rs_final_sum(place, parts, p32, name):
    hr = parts.shape[1]
    tr = _tile(hr, 640, 16)
    nt = hr // tr

    def body(pl_ref, a_ref, b_ref, c_ref, p32_ref, o_ref):
        o_ref[...] = ((p32_ref[...] + a_ref[0].astype(F32)) + b_ref[0].astype(F32)) + c_ref[0].astype(F32)

    def other(j):
        return pl.BlockSpec((1, tr, D), lambda i, pr: (j + jnp.where(pr[1] <= j, 1, 0), i, 0))

    return pl.pallas_call(
        body, name=name,
        grid_spec=pltpu.PrefetchScalarGridSpec(
            num_scalar_prefetch=1, grid=(nt,),
            in_specs=[other(0), other(1), other(2), pl.BlockSpec((tr, D), lambda i, pr: (i, 0))],
            out_specs=pl.BlockSpec((tr, D), lambda i, pr: (pr[0] * nt + i, 0))),
        out_shape=jax.ShapeDtypeStruct((2 * hr, D), F32),
        compiler_params=_params(("parallel",)),
    )(place, parts, parts, parts, p32)


def _rs_pair_gather(both, name):
    hr = both.shape[0] // 2

    def body(in_ref, out_ref, send_sem, recv_sem):
        x, y, c = _place()
        mine = out_ref.at[pl.ds(c * hr, hr), :]
        cp = pltpu.make_async_remote_copy(
            src_ref=mine, dst_ref=mine, send_sem=send_sem, recv_sem=recv_sem,
            device_id=(x, y, 1 - c), device_id_type=MESH)
        cp.start()
        theirs = out_ref.at[pl.ds((1 - c) * hr, hr), :]
        pltpu.make_async_remote_copy(
            src_ref=theirs, dst_ref=theirs, send_sem=send_sem, recv_sem=recv_sem,
            device_id=(x, y, 1 - c), device_id_type=MESH).wait_recv()
        cp.wait_send()

    any_spec = pl.BlockSpec(memory_space=pl.ANY)
    return pl.pallas_call(
        body, name=name,
        out_shape=jax.ShapeDtypeStruct(both.shape, F32),
        in_specs=[any_spec], out_specs=any_spec, input_output_aliases={0: 0},
        scratch_shapes=[pltpu.SemaphoreType.DMA, pltpu.SemaphoreType.DMA],
    )(both)


def _local_step(x, ctx, tgt, mods, mc, ag_gin, ag_main, place, small):
    nb, t, _ = x.shape
    tc = ctx.shape[1]
    n = nb * t
    nc = nb * tc
    xf = x.reshape(n, D)
    cf = ctx.reshape(nc, D)
    tf = tgt.reshape(n, D)
    vec = lambda a: a.reshape(1, -1)
    m = [[mods[l, :, k, :].reshape(nb, 1, D) for k in range(N_MOD)] for l in range(2)]
    mc_b = [jnp.broadcast_to(mc[k].reshape(1, 1, D), (nb, 1, D)) for k in range(2)]

    cw = [small["ffn_conv_w"][l] for l in range(2)]
    cb = [small["ffn_conv_b"][l].reshape(1, -1) for l in range(2)]
    w2 = jnp.zeros((128, 2 * KEY), F32)
    w2 = w2.at[0:RANK, 0:KEY].set(small["gla_w_a2"][0]).at[RANK:2 * RANK, KEY:].set(small["gla_w_a2"][1])
    b2 = small["gla_b_a"].reshape(1, 2 * KEY)
    hg = small["gla_head_norm"].reshape(1, DV)

    hn_all = _mod_fwd(xf, vec(small["norm_mix"][0]), m[0][0], m[0][1], t, "mod0_main", out_rows=n + nc)
    hn_all = _mod_fwd(cf, vec(small["norm_mix"][0]), mc_b[0], mc_b[1], tc, "mod0_ctx", out_rows=n + nc,
                      into=hn_all, row0=n)
    gin = _ag_pass_on(_split_wait(ag_gin, hn_all, _ag_copies, "ag_gin_wait")[1], "ag_gin_pass_on")
    w_gin = jnp.pad(gin[:, :_GIN_ROWS, :].reshape(GLA_IN, D), ((0, GLA_IN_PAD - GLA_IN), (0, 0)))
    p_all = _mm(hn_all, w_gin, "nt", F32, "gla_in_proj", 768, 3200)
    la_all = _gla_decay_fwd(p_all, w2, b2)
    o2, ss = _gla_scan_fwd(p_all, la_all, nb, t, tc)
    wg = _ag_pass_on(_split_wait(ag_main, o2[0], _ag_copies, "ag_main_wait")[1], "ag_main_pass_on")
    offs = _offsets(_MAIN, _MAIN_ROWS)
    rows = _MAIN_ROWS

    def w_nt(a, k, name, out_dtype=BF16, tm=1024):
        return _mm_nt_w(a, wg, offs[k], rows[k], name, tm, out_dtype)

    def w_nn(a3, k, name, tm, tn):
        return _mm_nn_w(a3, wg, offs[k], rows[k], name, tm, tn)

    yb0 = _gla_post_fwd(o2, p_all, hg, n)
    y0 = w_nn(yb0[None], "gla_out", "gla_out_proj", 1024, 1024)
    h1, hn1 = _mod_fwd(xf, vec(small["norm_ffn"][0]), m[0][3], m[0][4], t, "mod0_ffn", y=y0, gate=m[0][2])
    u0, z0 = _ffn_up_mid_fwd(hn1, wg, offs["up_t0"], cw[0], cb[0], nb, t, "ffn0_up_mid")
    f0 = w_nn(z0[None], "down0", "ffn0_down", 1024, 1024)
    h2, hn2 = _mod_fwd(h1, vec(small["norm_mix"][1]), m[1][0], m[1][1], t, "mod1_mix", y=f0, gate=m[0][5])
    p1 = w_nt(hn2, "sc_in_t", "sc_in_proj")
    yb1 = _sc_mid_fwd(p1, small["sc_conv_w"], nb, t)
    y1 = w_nn(yb1[None], "sc_out", "sc_out_proj", 1024, 1024)
    h3, hn3 = _mod_fwd(h2, vec(small["norm_ffn"][1]), m[1][3], m[1][4], t, "mod1_ffn", y=y1, gate=m[1][2])
    u1, z1 = _ffn_up_mid_fwd(hn3, wg, offs["up_t1"], cw[1], cb[1], nb, t, "ffn1_up_mid")
    f1 = w_nn(z1[None], "down1", "ffn1_down", 1024, 1024)
    loss, dh4, df1, dm15, dfinal = _final(h3, f1, m[1][5], vec(small["final_norm"]), tf, t)

    gs = {}
    dmods = [[None] * N_MOD for _ in range(2)]
    dmods[1][5] = dm15

    def w_dw(a3, b, g_prev, k, name, tm):
        return _mm_dw(a3, b, g_prev, offs[k], rows[k], name, tm)

    def ffn_bwd(l, df, u, z, hn, g_prev):
        dz = w_nt(df, f"down{l}", f"ffn{l}_down_dx")
        g_acc = w_dw(z[None], df, g_prev, f"down{l}", f"ffn{l}_down_dw", 640)
        du, dcw, dcb = _ffn_mid_bwd(u, cw[l], cb[l], dz, nb, t, f"ffn{l}_mid_bwd")
        dhn = w_nn(du, f"up_t{l}", f"ffn{l}_up_dx", 512, 512)
        g_acc = w_dw(du, hn, g_acc, f"up_t{l}", f"ffn{l}_up_dw", 640)
        return dhn, g_acc, jnp.moveaxis(dcw, 0, 1).reshape(3, 2 * HID), dcb.reshape(2 * HID)

    dhn3, g_acc, dcw1, dcb1 = ffn_bwd(1, df1, u1, z1, hn3, None)
    r = _mod_bwd(h3, dhn3, vec(small["norm_ffn"][1]), m[1][4], t, "mod1_ffn_bwd", dh_out=dh4, y_prev=y1,
                 gate_prev=m[1][2])
    dh3, dmods[1][4], dmods[1][3], dnf1, dy1, dmods[1][2] = (r["dh"], r["dscale"], r["dshift"], r["dgain"],
                                                             r["dy_prev"], r["dgate_prev"])
    dyb1 = w_nt(dy1, "sc_out", "sc_out_dx")
    g_acc = w_dw(yb1[None], dy1, g_acc, "sc_out", "sc_out_dw", 256)
    dp1, dscw = _sc_mid_bwd(p1, small["sc_conv_w"], dyb1, nb, t)
    dhn2 = w_nn(dp1, "sc_in_t", "sc_in_dx", 1024, 512)
    g_acc = w_dw(dp1, hn2, g_acc, "sc_in_t", "sc_in_dw", 256)
    r = _mod_bwd(h2, dhn2, vec(small["norm_mix"][1]), m[1][1], t, "mod1_mix_bwd", dh_out=dh3, y_prev=f0,
                 gate_prev=m[0][5])
    dh2, dmods[1][1], dmods[1][0], dnm1, df0, dmods[0][5] = (r["dh"], r["dscale"], r["dshift"], r["dgain"],
                                                             r["dy_prev"], r["dgate_prev"])
    dhn1, g_acc, dcw0, dcb0 = ffn_bwd(0, df0, u0, z0, hn1, g_acc)
    r = _mod_bwd(h1, dhn1, vec(small["norm_ffn"][0]), m[0][4], t, "mod0_ffn_bwd", dh_out=dh2, y_prev=y0,
                 gate_prev=m[0][2])
    dh1, dmods[0][4], dmods[0][3], dnf0, dy0, dmods[0][2] = (r["dh"], r["dscale"], r["dshift"], r["dgain"],
                                                             r["dy_prev"], r["dgate_prev"])
    g_packed = w_dw(yb0[None], dy0, g_acc, "gla_out", "gla_out_dw", 256)
    pair = _split_start(g_packed, (N_CHIPS, _MAIN_TOTAL // 2, D), _pe_copies, 1, dy0, "rs_main_pair_start")
    dyb0 = w_nt(dy0, "gla_out", "gla_out_dx", F32)
    do, dgate, dhg = _gla_post_bwd(o2, p_all, hg + pair[4][0:1, 0:1], dyb0, n)
    g_packed, from_sibling = _split_wait(pair, do, _pe_copies, "rs_main_pair_wait")
    p16, p32 = _rs_chip_sum(place, g_packed, from_sibling, "rs_main_chip_sum")
    sc_main = _split_start(p16, p16.shape, _sc_copies, 3, p32, "rs_main_scatter_start")
    dq, dk, dv, dla = _gla_scan_bwd(p_all, la_all, do, ss, nb, t, tc, sc_main[4])
    dp, dw2, db2 = _gla_assemble(p_all, w2, b2, dq, dk, dv, dla, dgate, n)
    dhn_all = _mm(dp, w_gin, "nn", F32, "gla_in_dx", 768, 512)
    landed = _split_wait(sc_main, dhn_all, _sc_copies, "rs_main_scatter_wait")[1]
    g_main = _rs_pair_gather(_rs_final_sum(place, landed, p32, "rs_main_final_sum"), "rs_main_pair_gather")
    g_gin = _mm(dp, hn_all, "tn", F32, "gla_in_dw", 640, 1024)[:GLA_IN]
    g_gin = jnp.pad(g_gin.reshape(N_CHIPS, _GIN_ROWS, D), ((0, 0), (0, _GIN_PAD - _GIN_ROWS), (0, 0)))
    from_sibling = _rs_pair_exchange(g_gin, "rs_gin_pair_exchange")
    p16_gin, p32_gin = _rs_chip_sum(place, g_gin, from_sibling, "rs_gin_chip_sum")
    r = _mod_bwd(xf, dhn_all, vec(small["norm_mix"][0]), m[0][1], t, "mod0_main_bwd", dh_out=dh1)
    grad_x, dmods[0][1], dmods[0][0], dnm0 = r["dh"], r["dscale"], r["dshift"], r["dgain"]
    rc = _mod_bwd(cf, dhn_all, vec(small["norm_mix"][0]), mc_b[1], tc, "mod0_ctx_bwd", dhn_row0=n, need_dh=False)
    dmc = jnp.stack([jnp.sum(rc["dshift"], axis=0).reshape(D), jnp.sum(rc["dscale"], axis=0).reshape(D)])
    dnm0 = dnm0 + rc["dgain"]

    gs["norm_mix"] = jnp.concatenate([dnm0, dnm1], axis=0)
    gs["norm_ffn"] = jnp.concatenate([dnf0, dnf1], axis=0)
    gs["final_norm"] = dfinal.reshape(D)
    gs["gla_w_a2"] = jnp.stack([dw2[0:RANK, 0:KEY], dw2[RANK:2 * RANK, KEY:]])
    gs["gla_b_a"] = db2.reshape(2, KEY)
    gs["gla_head_norm"] = dhg.reshape(DV)
    gs["sc_conv_w"] = dscw
    gs["ffn_conv_w"] = jnp.stack([dcw0, dcw1])
    gs["ffn_conv_b"] = jnp.stack([dcb0, dcb1])
    dmods_arr = jnp.stack([jnp.stack([dmods[l][k].reshape(nb, D) for k in range(N_MOD)], axis=1) for l in range(2)])
    return loss, grad_x.reshape(nb, t, D), g_main, p16_gin, p32_gin, gs, dmods_arr, dmc


def _pack(arrs):
    parts, meta, off = [], [], 0
    for a in arrs:
        r = a.size // 128
        rp = -(-r // 8) * 8
        a2 = a.reshape(r, 128).astype(F32)
        if rp != r:
            a2 = jnp.pad(a2, ((0, rp - r), (0, 0)))
        parts.append(a2)
        meta.append((off, r, a.shape))
        off += rp
    return jnp.concatenate(parts, axis=0), meta


def _unpack(buf, meta, lead=()):
    return [buf[..., off:off + r, :].reshape(*lead, *shape) for off, r, shape in meta]


_MAIN = ("up_t0", "up_t1", "down0", "down1", "sc_in_t", "gla_out", "sc_out")
_MAIN_ROWS = {"sc_in_t": 3 * D // N_CHIPS, "up_t0": 2 * HID // N_CHIPS, "up_t1": 2 * HID // N_CHIPS,
              "gla_out": D // N_CHIPS, "sc_out": D // N_CHIPS, "down0": HID // N_CHIPS, "down1": HID // N_CHIPS}
_MAIN_TOTAL = sum(_MAIN_ROWS.values())
_GIN_ROWS = GLA_IN // N_CHIPS
_GIN_PAD = -(-_GIN_ROWS // 32) * 32


def _offsets(names, rows):
    off, out = 0, {}
    for k in names:
        out[k] = off
        off += rows[k]
    return out


def kernel(x, c, ctx, c_ctx, ada_w, ada_b, norm_mix, norm_ffn, gla_w_in, gla_w_a2, gla_b_a, gla_head_norm, gla_w_out, sc_w_in, sc_conv_w, sc_w_out, ffn_w_up, ffn_conv_w, ffn_conv_b, ffn_w_down, final_norm, loss_target, m_c_ctx, m_ada_w, m_ada_b, m_norm_mix, m_norm_ffn, m_gla_w_in, m_gla_w_a2, m_gla_b_a, m_gla_head_norm, m_gla_w_out, m_sc_w_in, m_sc_conv_w, m_sc_w_out, m_ffn_w_up, m_ffn_conv_w, m_ffn_conv_b, m_ffn_w_down, m_final_norm, v_c_ctx, v_ada_w, v_ada_b, v_norm_mix, v_norm_ffn, v_gla_w_in, v_gla_w_a2, v_gla_b_a, v_gla_head_norm, v_gla_w_out, v_sc_w_in, v_sc_conv_w, v_sc_w_out, v_ffn_w_up, v_ffn_conv_w, v_ffn_conv_b, v_ffn_w_down, v_final_norm):
    ix, iy, ic = _place()
    chip = 2 * ix + iy
    dev = 2 * chip + ic
    place = jnp.stack([ic, chip]).astype(jnp.int32)
    nb = x.shape[0]
    offs = _offsets(_MAIN, _MAIN_ROWS)

    buf, meta = _pack([c, ffn_conv_w, sc_conv_w, gla_w_a2, gla_b_a])
    got = _allgather_small(buf, "gather_small_in").reshape(N_DEV, buf.shape[0], 128)
    c_all, fcw, scw, wa2, ba = _unpack(got, meta, (N_DEV,))
    c_all = c_all.reshape(N_DEV * nb, D)
    per_chip = lambda a: a[0::2]
    ffn_conv_w_full = jnp.moveaxis(per_chip(fcw), 0, 2).reshape(2, 3, 2 * HID)
    sc_conv_w_full = jnp.moveaxis(per_chip(scw)[:, 0], 0, 1).reshape(3, D)
    gla_w_a2_full = jnp.moveaxis(per_chip(wa2)[:, 0], 0, 2).reshape(2, RANK, KEY)
    gla_b_a_full = jnp.moveaxis(per_chip(ba)[:, 0], 0, 1).reshape(2, KEY)

    cvec = jnp.concatenate([c_all, c_ctx.reshape(1, D), jnp.zeros((ADA_ROWS - N_DEV * nb - 1, D), F32)], axis=0)
    ada_b_sh = lax.dynamic_slice_in_dim(ada_b, chip * ADA_SH, ADA_SH, axis=1).reshape(2, 1, ADA_SH)
    mod_sh = _ada_fwd(cvec, ada_w, ada_b_sh)
    got = _allgather_small(mod_sh.reshape(2 * ADA_ROWS, ADA_SH), "gather_mod")
    mod_full = jnp.moveaxis(per_chip(got.reshape(N_DEV, 2, ADA_ROWS, ADA_SH)), 0, 2).reshape(2, ADA_ROWS, N_MOD * D)
    mc = mod_full[0, N_DEV * nb, :2 * D].reshape(2, D)

    own = {"sc_in_t": sc_w_in[0].T, "up_t0": ffn_w_up[0].T, "up_t1": ffn_w_up[1].T,
           "gla_out": gla_w_out[0], "sc_out": sc_w_out[0], "down0": ffn_w_down[0], "down1": ffn_w_down[1]}
    own_main = jnp.concatenate([own[k].astype(BF16) for k in _MAIN], axis=0)
    own_gin = jnp.pad(gla_w_in[0].T.astype(BF16), ((0, _GIN_PAD - _GIN_ROWS), (0, 0)))
    ag_gin = _split_start(own_gin, (N_CHIPS, _GIN_PAD, D), _ag_copies, 4, mc, "ag_gin_start")
    ag_main = _split_start(own_main, (N_CHIPS, _MAIN_TOTAL, D), _ag_copies, 4, ag_gin[4], "ag_main_start")
    mods = lax.dynamic_slice_in_dim(mod_full, dev * nb, nb, axis=1).reshape(2, nb, N_MOD, D) + ag_main[4][0, 0]

    small = {"norm_mix": norm_mix, "norm_ffn": norm_ffn, "final_norm": final_norm, "gla_w_a2": gla_w_a2_full,
             "gla_b_a": gla_b_a_full, "gla_head_norm": gla_head_norm[0], "sc_conv_w": sc_conv_w_full,
             "ffn_conv_w": ffn_conv_w_full, "ffn_conv_b": ffn_conv_b}
    loss_p, grad_x, g_main, p16_gin, p32_gin, gs, dmods, dmc = _local_step(x, ctx, loss_target, mods, mc, ag_gin,
                                                                           ag_main, place, small)

    sum_names = ["norm_mix", "norm_ffn", "final_norm", "gla_w_a2", "gla_b_a", "gla_head_norm", "sc_conv_w",
                 "ffn_conv_w", "ffn_conv_b"]
    buf, meta = _pack([jnp.broadcast_to(loss_p, (8, 128))] + [gs[k] for k in sum_names] + [dmc, dmods])
    n_sum = meta[-1][0]
    got = _allgather_small(buf, "gather_small_grads").reshape(N_DEV, buf.shape[0], 128)
    summed = _sum_slots(got[:, :n_sum], "sum_small_grads")
    parts = _unpack(summed, meta[:-1])
    loss = parts[0][0, 0]
    g_small = dict(zip(sum_names, parts[1:-1]))
    dmc_tot = parts[-1]
    dmods_all = jnp.moveaxis(_unpack(got, meta[-1:], (N_DEV,))[0], 0, 1).reshape(2, N_DEV * nb, N_MOD * D)

    ctx_row = jnp.stack([jnp.concatenate([dmc_tot.reshape(2 * D), jnp.zeros(((N_MOD - 2) * D,), F32)]),
                         jnp.zeros((N_MOD * D,), F32)]).reshape(2, 1, N_MOD * D)
    dmod_ext = jnp.concatenate([dmods_all, ctx_row, jnp.zeros((2, ADA_ROWS - N_DEV * nb - 1, N_MOD * D), F32)], axis=1)
    g_ada_b = _sum_slots(jnp.moveaxis(dmod_ext, 1, 0).reshape(ADA_ROWS, 2 * N_MOD * D // 128, 128),
                         "sum_ada_b").reshape(2, N_MOD * D)
    dmod_sh = lax.dynamic_slice_in_dim(dmod_ext, chip * ADA_SH, ADA_SH, axis=2)
    g_ada_w, dcv = _ada_bwd(cvec, ada_w, dmod_sh)
    dscc_part = (dcv[0, N_DEV * nb] + dcv[1, N_DEV * nb]).reshape(8, 128)
    got = _allgather_small(dscc_part, "gather_dscc").reshape(N_DEV, 8, 128)
    g_c_ctx = _cctx_grad(per_chip(got), c_ctx.reshape(8, 128)).reshape(D)

    sc_gin = _split_start(p16_gin, p16_gin.shape, _sc_copies, 3, g_c_ctx, "rs_gin_scatter_start")
    seg = {k: g_main[offs[k]:offs[k] + _MAIN_ROWS[k]] for k in _MAIN}

    sl_chip = lambda a, axis, width: lax.dynamic_slice_in_dim(a, chip * width, width, axis=axis)
    grads = {
        "c_ctx": g_c_ctx, "ada_w": g_ada_w, "ada_b": g_ada_b, "norm_mix": g_small["norm_mix"],
        "norm_ffn": g_small["norm_ffn"],
        "gla_w_a2": sl_chip(g_small["gla_w_a2"], 2, KEY // N_CHIPS)[None],
        "gla_b_a": sl_chip(g_small["gla_b_a"], 1, KEY // N_CHIPS)[None],
        "gla_head_norm": g_small["gla_head_norm"][None], "gla_w_out": seg["gla_out"][None],
        "sc_w_in": seg["sc_in_t"].T[None], "sc_conv_w": sl_chip(g_small["sc_conv_w"], 1, D // N_CHIPS)[None],
        "sc_w_out": seg["sc_out"][None], "ffn_w_up": jnp.stack([seg["up_t0"].T, seg["up_t1"].T]),
        "ffn_conv_w": sl_chip(g_small["ffn_conv_w"], 2, 2 * HID // N_CHIPS), "ffn_conv_b": g_small["ffn_conv_b"],
        "ffn_w_down": jnp.stack([seg["down0"], seg["down1"]]), "final_norm": g_small["final_norm"],
    }
    weights = {"c_ctx": c_ctx, "ada_w": ada_w, "ada_b": ada_b, "norm_mix": norm_mix, "norm_ffn": norm_ffn,
               "gla_w_in": gla_w_in, "gla_w_a2": gla_w_a2, "gla_b_a": gla_b_a, "gla_head_norm": gla_head_norm,
               "gla_w_out": gla_w_out, "sc_w_in": sc_w_in, "sc_conv_w": sc_conv_w, "sc_w_out": sc_w_out,
               "ffn_w_up": ffn_w_up, "ffn_conv_w": ffn_conv_w, "ffn_conv_b": ffn_conv_b, "ffn_w_down": ffn_w_down,
               "final_norm": final_norm}
    mom1 = {"c_ctx": m_c_ctx, "ada_w": m_ada_w, "ada_b": m_ada_b, "norm_mix": m_norm_mix, "norm_ffn": m_norm_ffn,
            "gla_w_in": m_gla_w_in, "gla_w_a2": m_gla_w_a2, "gla_b_a": m_gla_b_a, "gla_head_norm": m_gla_head_norm,
            "gla_w_out": m_gla_w_out, "sc_w_in": m_sc_w_in, "sc_conv_w": m_sc_conv_w, "sc_w_out": m_sc_w_out,
            "ffn_w_up": m_ffn_w_up, "ffn_conv_w": m_ffn_conv_w, "ffn_conv_b": m_ffn_conv_b,
            "ffn_w_down": m_ffn_w_down, "final_norm": m_final_norm}
    mom2 = {"c_ctx": v_c_ctx, "ada_w": v_ada_w, "ada_b": v_ada_b, "norm_mix": v_norm_mix, "norm_ffn": v_norm_ffn,
            "gla_w_in": v_gla_w_in, "gla_w_a2": v_gla_w_a2, "gla_b_a": v_gla_b_a, "gla_head_norm": v_gla_head_norm,
            "gla_w_out": v_gla_w_out, "sc_w_in": v_sc_w_in, "sc_conv_w": v_sc_conv_w, "sc_w_out": v_sc_w_out,
            "ffn_w_up": v_ffn_w_up, "ffn_conv_w": v_ffn_conv_w, "ffn_conv_b": v_ffn_conv_b,
            "ffn_w_down": v_ffn_w_down, "final_norm": v_final_norm}
    names = list(weights)

    big_names = ["ada_w", "gla_w_out", "sc_w_in", "sc_w_out", "ffn_w_up", "ffn_w_down", "gla_w_in"]
    small_names = [k for k in names if k not in big_names]
    delta, new_m, new_v = {}, {}, {}
    done = []

    def big_adamw(k, token):
        delta[k], new_m[k], new_v[k] = _adamw(weights[k], grads[k], mom1[k], mom2[k], "adamw_" + k, token)
        done.append(new_v[k][0, 0:1, 0:128])

    for k in big_names[:-1]:
        grads[k] = grads[k].reshape(weights[k].shape)
        big_adamw(k, sc_gin[4])
    for k in small_names:
        grads[k] = grads[k].reshape(weights[k].shape)
    packed = [_pack([src[k] for k in small_names]) for src in (weights, grads, mom1, mom2)]
    meta = packed[0][1]
    rows_pad = -packed[0][0].shape[0] % 128
    bufs = [jnp.pad(p[0], ((0, rows_pad), (0, 0)))[None] for p in packed]
    outs = _adamw(bufs[0], bufs[1], bufs[2], bufs[3], "adamw_small", sc_gin[4])
    done.append(outs[2][0, 0:1, :])
    for dst, o in zip((delta, new_m, new_v), outs):
        for k, a in zip(small_names, _unpack(o[0], meta)):
            dst[k] = a
    landed = _split_wait(sc_gin, jnp.concatenate(done, axis=0), _sc_copies, "rs_gin_scatter_wait")[1]
    g_gin_shard = _rs_pair_gather(_rs_final_sum(place, landed, p32_gin, "rs_gin_final_sum"), "rs_gin_pair_gather")
    grads["gla_w_in"] = g_gin_shard[:_GIN_ROWS].T[None]
    big_adamw("gla_w_in", sc_gin[4])

    return (loss, grad_x, *[grads[k] for k in names], *[delta[k] for k in names], *[new_m[k] for k in names],
            *[new_v[k] for k in names])
```

```python
import functools

import jax
import jax.numpy as jnp
from jax import lax
from jax.experimental import pallas as pl
from jax.experimental.pallas import tpu as pltpu

F32 = jnp.float32
BF16 = jnp.bfloat16
MESH = pl.DeviceIdType.MESH

EPS = 1e-6
D = 1024
N_MOD = 6
HEADS = 4
DK = 128
DV = 256
KEY = HEADS * DK
RANK = 16
TAU = 16.0
CH = 64
GRID_W = 64
HID = 2560
GLA_IN = 2 * KEY + 2 * D + 2 * RANK
GLA_IN_PAD = 3200
Q_SCALE = DK ** -0.5
N_CHIPS = 4
N_DEV = 8

ADAM_LR = 0.001
ADAM_B1 = 0.9
ADAM_B2 = 0.999
ADAM_EPS = 1e-08
ADAM_WD = 0.01
ADAM_STEP = 10

VMEM_LIMIT = 56 * 1024 * 1024


def _params(sem):
    return pltpu.CompilerParams(dimension_semantics=sem, vmem_limit_bytes=VMEM_LIMIT)


def _tile(n, pref, mult=8):
    if n <= pref:
        return n
    for t in range(pref - pref % mult, 0, -mult):
        if n % t == 0:
            return t
    raise ValueError((n, pref, mult))


_NN = (((1,), (0,)), ((), ()))
_NT = (((1,), (1,)), ((), ()))
_TN = (((0,), (0,)), ((), ()))


def _dot(a, b, dims=_NN):
    return lax.dot_general(a.astype(BF16), b.astype(BF16), dims, preferred_element_type=F32)


def _sigmoid(x):
    return 1.0 / (1.0 + jnp.exp(-x))


def _rowsum(x):
    return jnp.sum(x, axis=0, keepdims=True)


def _mm(a, b, form, out_dtype, name, tm, tn):
    if form == "tn":
        K, M = a.shape
    else:
        M, K = a.shape
    N = b.shape[0] if form == "nt" else b.shape[1]
    tm = _tile(M, tm, 128)
    tn = _tile(N, tn, 128)
    dims = {"nn": _NN, "nt": _NT, "tn": _TN}[form]

    def body(a_ref, b_ref, o_ref):
        o_ref[...] = _dot(a_ref[...], b_ref[...], dims).astype(o_ref.dtype)

    if form == "tn":
        a_spec = pl.BlockSpec((K, tm), lambda i, j: (0, i))
    else:
        a_spec = pl.BlockSpec((tm, K), lambda i, j: (i, 0))
    if form == "nt":
        b_spec = pl.BlockSpec((tn, K), lambda i, j: (j, 0))
    else:
        b_spec = pl.BlockSpec((K, tn), lambda i, j: (0, j))
    return pl.pallas_call(
        body,
        name=name,
        grid=(M // tm, N // tn),
        in_specs=[a_spec, b_spec],
        out_specs=pl.BlockSpec((tm, tn), lambda i, j: (i, j)),
        out_shape=jax.ShapeDtypeStruct((M, N), out_dtype),
        compiler_params=_params(("parallel", "parallel")),
    )(a, b)


def _mm_nt_w(a, wg, off, rows, name, tm, out_dtype):
    m = a.shape[0]
    tm = _tile(m, tm, 128)
    if N_CHIPS * rows <= D:

        def body_small(a_ref, w_ref, o_ref):
            av = a_ref[...]
            for s in range(N_CHIPS):
                o_ref[:, s * rows:(s + 1) * rows] = _dot(av, w_ref[s], _NT).astype(o_ref.dtype)

        return pl.pallas_call(
            body_small, name=name, grid=(m // tm,),
            in_specs=[pl.BlockSpec((tm, D), lambda i: (i, 0)),
                      pl.BlockSpec((N_CHIPS, rows, D), lambda i: (0, off // rows, 0))],
            out_specs=pl.BlockSpec((tm, N_CHIPS * rows), lambda i: (i, 0)),
            out_shape=jax.ShapeDtypeStruct((m, N_CHIPS * rows), out_dtype),
            compiler_params=_params(("parallel",)),
        )(a, wg)

    def body(a_ref, w_ref, o_ref):
        o_ref[...] = _dot(a_ref[...], w_ref[0], _NT).astype(o_ref.dtype)

    return pl.pallas_call(
        body, name=name, grid=(m // tm, N_CHIPS),
        in_specs=[pl.BlockSpec((tm, D), lambda i, s: (i, 0)),
                  pl.BlockSpec((1, rows, D), lambda i, s: (s, off // rows, 0))],
        out_specs=pl.BlockSpec((tm, rows), lambda i, s: (i, s)),
        out_shape=jax.ShapeDtypeStruct((m, N_CHIPS * rows), out_dtype),
        compiler_params=_params(("parallel", "parallel")),
    )(a, wg)


def _mm_nn_w(a3, wg, off, rows, name, tm, tn):
    parts, m, kp = a3.shape
    assert parts * kp == N_CHIPS * rows
    tm = _tile(m, tm, 128)
    cuts = sorted({s * rows for s in range(N_CHIPS + 1)} | {p * kp for p in range(parts + 1)})
    pieces = [(k0 // kp, k0 % kp, k0 // rows, k0 % rows, k1 - k0) for k0, k1 in zip(cuts[:-1], cuts[1:])]

    def body(a_ref, w_ref, o_ref):
        acc = None
        for p, a0, s, r0, width in pieces:
            term = _dot(a_ref[p, :, a0:a0 + width], w_ref[s, r0:r0 + width, :])
            acc = term if acc is None else acc + term
        o_ref[...] = acc

    return pl.pallas_call(
        body, name=name, grid=(m // tm, D // tn),
        in_specs=[pl.BlockSpec((parts, tm, kp), lambda i, j: (0, i, 0)),
                  pl.BlockSpec((N_CHIPS, rows, tn), lambda i, j: (0, off // rows, j))],
        out_specs=pl.BlockSpec((tm, tn), lambda i, j: (i, j)),
        out_shape=jax.ShapeDtypeStruct((m, D), F32),
        compiler_params=_params(("parallel", "parallel")),
    )(a3, wg)


def _mm_nn_w_mod(a3, wg, off, rows, h, gate, gain, shift, scale, tpb_rows, name, tm):
    parts, m, kp = a3.shape
    assert parts * kp == N_CHIPS * rows
    tm = _tile(tpb_rows, tm, 128)
    tpb = tpb_rows // tm
    cuts = sorted({s * rows for s in range(N_CHIPS + 1)} | {p * kp for p in range(parts + 1)})
    pieces = [(k0 // kp, k0 % kp, k0 // rows, k0 % rows, k1 - k0) for k0, k1 in zip(cuts[:-1], cuts[1:])]

    def body(a_ref, w_ref, h_ref, gate_ref, gain_ref, sh_ref, sc_ref, y_ref, hout_ref, hn_ref):
        acc = None
        for p, a0, s, r0, width in pieces:
            term = _dot(a_ref[p, :, a0:a0 + width], w_ref[s, r0:r0 + width, :])
            acc = term if acc is None else acc + term
        y_ref[...] = acc
        hv = h_ref[...] + gate_ref[0] * acc
        hout_ref[...] = hv
        r = lax.rsqrt(jnp.mean(hv * hv, axis=-1, keepdims=True) + EPS)
        hn_ref[...] = ((hv * r) * gain_ref[...] * (1.0 + sc_ref[0]) + sh_ref[0]).astype(BF16)

    row = pl.BlockSpec((tm, D), lambda i: (i, 0))
    per_b = pl.BlockSpec((1, 1, D), lambda i: (i // tpb, 0, 0))
    return pl.pallas_call(
        body, name=name, grid=(m // tm,),
        in_specs=[pl.BlockSpec((parts, tm, kp), lambda i: (0, i, 0)),
                  pl.BlockSpec((N_CHIPS, rows, D), lambda i: (0, off // rows, 0)),
                  row, per_b, pl.BlockSpec((1, D), lambda i: (0, 0)), per_b, per_b],
        out_specs=[row, row, row],
        out_shape=[jax.ShapeDtypeStruct((m, D), F32), jax.ShapeDtypeStruct((m, D), F32),
                   jax.ShapeDtypeStruct((m, D), BF16)],
        compiler_params=_params(("parallel",)),
    )(a3, wg, h, gate, gain, shift, scale)


def _mm_dw(a3, b, g_prev, off, rows, name, tm):
    parts, ntok, cdim = a3.shape
    assert parts * cdim == N_CHIPS * rows and cdim % tm == 0 and rows % tm == 0 and off % tm == 0

    def body(a_ref, b_ref, *rest):
        rest[-1][0] = _dot(a_ref[0], b_ref[...], _TN)

    in_specs = [pl.BlockSpec((1, ntok, tm), lambda i: ((i * tm) // cdim, 0, ((i * tm) % cdim) // tm)),
                pl.BlockSpec((ntok, D), lambda i: (0, 0))]
    args = [a3, b]
    aliases = {}
    if g_prev is not None:
        in_specs.append(pl.BlockSpec(memory_space=pl.ANY))
        args.append(g_prev)
        aliases = {2: 0}
    return pl.pallas_call(
        body, name=name, grid=(N_CHIPS * rows // tm,),
        in_specs=in_specs,
        out_specs=pl.BlockSpec((1, tm, D), lambda i: ((i * tm) // rows, (off + (i * tm) % rows) // tm, 0)),
        out_shape=jax.ShapeDtypeStruct((N_CHIPS, _MAIN_TOTAL, D), F32),
        input_output_aliases=aliases,
        compiler_params=_params(("parallel",)),
    )(*args)


def _mod_fwd(h, gain, shift, scale, tpb_rows, name, y=None, gate=None, out_rows=None, into=None, row0=0):
    n = h.shape[0]
    tt = _tile(tpb_rows, 256)
    tpb = tpb_rows // tt
    has_res = y is not None
    assert row0 % tt == 0 and not (has_res and out_rows)

    def body(*refs):
        if has_res:
            h_ref, y_ref, gate_ref, gain_ref, sh_ref, sc_ref, hout_ref, hn_ref = refs
            hv = h_ref[...] + gate_ref[0] * y_ref[...]
            hout_ref[...] = hv
        else:
            h_ref, gain_ref, sh_ref, sc_ref, hn_ref = refs[0], refs[1], refs[2], refs[3], refs[-1]
            hv = h_ref[...]
        r = lax.rsqrt(jnp.mean(hv * hv, axis=-1, keepdims=True) + EPS)
        hn = (hv * r) * gain_ref[...] * (1.0 + sc_ref[0]) + sh_ref[0]
        hn_ref[...] = hn.astype(BF16)

    row = pl.BlockSpec((tt, D), lambda i: (i, 0))
    per_b = pl.BlockSpec((1, 1, D), lambda i: (i // tpb, 0, 0))
    vec = pl.BlockSpec((1, D), lambda i: (0, 0))
    if has_res:
        in_specs = [row, row, per_b, vec, per_b, per_b]
        args = (h, y, gate, gain, shift, scale)
        out_specs = [row, row]
        out_shape = [jax.ShapeDtypeStruct((n, D), F32), jax.ShapeDtypeStruct((n, D), BF16)]
    else:
        in_specs = [row, vec, per_b, per_b]
        args = (h, gain, shift, scale)
        out_specs = pl.BlockSpec((tt, D), lambda i: (i + row0 // tt, 0))
        out_shape = jax.ShapeDtypeStruct((out_rows or n, D), BF16)
    aliases = {}
    if into is not None:
        in_specs = in_specs + [pl.BlockSpec(memory_space=pl.ANY)]
        args = args + (into,)
        aliases = {4: 0}
    return pl.pallas_call(
        body, name=name, grid=(n // tt,), in_specs=in_specs, out_specs=out_specs, out_shape=out_shape,
        input_output_aliases=aliases, compiler_params=_params(("parallel",)),
    )(*args)


def _mod_bwd(h_in, dhn, gain, scale, tpb_rows, name, dhn_row0=0, dh_out=None, y_prev=None, gate_prev=None,
             need_dh=True):
    n = h_in.shape[0]
    nb = n // tpb_rows
    tt = _tile(tpb_rows, 256)
    tpb = tpb_rows // tt
    off = dhn_row0 // tt
    assert dhn_row0 % tt == 0
    has_out = dh_out is not None
    has_prev = y_prev is not None

    def body(*refs):
        it = iter(refs)
        h_ref, dhn_ref, gain_ref, sc_ref = next(it), next(it), next(it), next(it)
        dho_ref = next(it) if has_out else None
        yp_ref, gp_ref = (next(it), next(it)) if has_prev else (None, None)
        dh_ref = next(it) if need_dh else None
        dsc_ref, dsh_ref, dgain_ref = next(it), next(it), next(it)
        dyp_ref, dgp_ref = (next(it), next(it)) if has_prev else (None, None)
        i = pl.program_id(0)

        @pl.when(i == 0)
        def _():
            dgain_ref[...] = jnp.zeros_like(dgain_ref)

        @pl.when(i % tpb == 0)
        def _():
            dsc_ref[...] = jnp.zeros_like(dsc_ref)
            dsh_ref[...] = jnp.zeros_like(dsh_ref)
            if has_prev:
                dgp_ref[...] = jnp.zeros_like(dgp_ref)

        hv = h_ref[...]
        r = lax.rsqrt(jnp.mean(hv * hv, axis=-1, keepdims=True) + EPS)
        y = hv * r
        gain_v = gain_ref[...]
        g = dhn_ref[...].astype(F32)
        dsh_ref[0] += _rowsum(g)
        dsc_ref[0] += _rowsum(g * (y * gain_v))
        drn = g * (1.0 + sc_ref[0])
        dgain_ref[...] += _rowsum(drn * y)
        if need_dh:
            dy = drn * gain_v
            dh = r * (dy - y * jnp.mean(dy * y, axis=-1, keepdims=True))
            if has_out:
                dh = dh + dho_ref[...]
            dh_ref[...] = dh
            if has_prev:
                dyp_ref[...] = (dh * gp_ref[0]).astype(BF16)
                dgp_ref[0] += _rowsum(dh * yp_ref[...])

    row = pl.BlockSpec((tt, D), lambda i: (i, 0))
    row_off = pl.BlockSpec((tt, D), lambda i: (i + off, 0))
    per_b = pl.BlockSpec((1, 1, D), lambda i: (i // tpb, 0, 0))
    vec = pl.BlockSpec((1, D), lambda i: (0, 0))
    in_specs = [row, row_off, vec, per_b]
    args = [h_in, dhn, gain, scale]
    if has_out:
        in_specs.append(row)
        args.append(dh_out)
    if has_prev:
        in_specs += [row, per_b]
        args += [y_prev, gate_prev]
    out_specs, out_shape, names = [], [], []
    if need_dh:
        out_specs.append(row)
        out_shape.append(jax.ShapeDtypeStruct((n, D), F32))
        names.append("dh")
    for nm in ("dscale", "dshift"):
        out_specs.append(per_b)
        out_shape.append(jax.ShapeDtypeStruct((nb, 1, D), F32))
        names.append(nm)
    out_specs.append(vec)
    out_shape.append(jax.ShapeDtypeStruct((1, D), F32))
    names.append("dgain")
    if has_prev:
        out_specs += [row, per_b]
        out_shape += [jax.ShapeDtypeStruct((n, D), BF16), jax.ShapeDtypeStruct((nb, 1, D), F32)]
        names += ["dy_prev", "dgate_prev"]
    outs = pl.pallas_call(
        body, name=name, grid=(n // tt,), in_specs=in_specs, out_specs=out_specs, out_shape=out_shape,
        compiler_params=_params(("arbitrary",)),
    )(*args)
    return dict(zip(names, outs))


def _final(h, f, gate, gain, tgt, tpb_rows):
    n = h.shape[0]
    nb = n // tpb_rows
    tt = _tile(tpb_rows, 256)
    tpb = tpb_rows // tt

    def body(h_ref, f_ref, gate_ref, gain_ref, tgt_ref, loss_ref, dh_ref, df_ref, dgate_ref, dgain_ref):
        i = pl.program_id(0)

        @pl.when(i == 0)
        def _():
            loss_ref[...] = jnp.zeros_like(loss_ref)
            dgain_ref[...] = jnp.zeros_like(dgain_ref)

        @pl.when(i % tpb == 0)
        def _():
            dgate_ref[...] = jnp.zeros_like(dgate_ref)

        fv = f_ref[...]
        gate_v = gate_ref[0]
        hv = h_ref[...] + gate_v * fv
        r = lax.rsqrt(jnp.mean(hv * hv, axis=-1, keepdims=True) + EPS)
        y = hv * r
        gain_v = gain_ref[...]
        e = y * gain_v - tgt_ref[...]
        s = jnp.sum(_rowsum(e * e), axis=1, keepdims=True) * (0.5 / D)
        loss_ref[...] += jnp.broadcast_to(s, loss_ref.shape)
        dout = e * (1.0 / D)
        dgain_ref[...] += _rowsum(dout * y)
        dy = dout * gain_v
        dh = r * (dy - y * jnp.mean(dy * y, axis=-1, keepdims=True))
        dh_ref[...] = dh
        df_ref[...] = (dh * gate_v).astype(BF16)
        dgate_ref[0] += _rowsum(dh * fv)

    row = pl.BlockSpec((tt, D), lambda i: (i, 0))
    per_b = pl.BlockSpec((1, 1, D), lambda i: (i // tpb, 0, 0))
    vec = pl.BlockSpec((1, D), lambda i: (0, 0))
    return pl.pallas_call(
        body, name="final_loss", grid=(n // tt,),
        in_specs=[row, row, per_b, vec, row],
        out_specs=[pl.BlockSpec((1, 128), lambda i: (0, 0)), row, row, per_b, vec],
        out_shape=[jax.ShapeDtypeStruct((1, 128), F32), jax.ShapeDtypeStruct((n, D), F32),
                   jax.ShapeDtypeStruct((n, D), BF16), jax.ShapeDtypeStruct((nb, 1, D), F32),
                   jax.ShapeDtypeStruct((1, D), F32)],
        compiler_params=_params(("arbitrary",)),
    )(h, f, gate, gain, tgt)


def _row_dn1(x):
    t = lax.broadcasted_iota(jnp.int32, x.shape, 0)
    return jnp.where(t % GRID_W == 0, 0.0, pltpu.roll(x, 1, 0))


def _row_up1(x):
    t = lax.broadcasted_iota(jnp.int32, x.shape, 0)
    return jnp.where(t % GRID_W == GRID_W - 1, 0.0, pltpu.roll(x, x.shape[0] - 1, 0))


def _silu(x):
    return x * _sigmoid(x)


def _dsilu(x):
    s = _sigmoid(x)
    return s * (1.0 + x * (1.0 - s))


def _row_ds(i):
    start = i * GRID_W
    return pl.ds(start if isinstance(start, int) else pl.multiple_of(start, GRID_W), GRID_W)


def _grid_row(ref, i, first, last):
    def rows(k):
        return ref[_row_ds(k), :].astype(F32)

    cur = rows(i)
    return (jnp.zeros_like(cur) if first else rows(i - 1)), cur, (jnp.zeros_like(cur) if last else rows(i + 1))


def _over_grid_rows(n_rows, step, carry):
    carry = step(0, carry, True, n_rows == 1)
    if n_rows > 2:
        carry = lax.fori_loop(1, n_rows - 1, lambda i, c: step(i, c, False, False), carry)
    if n_rows > 1:
        carry = step(n_rows - 1, carry, False, True)
    return carry


def _fold8(p):
    return p.reshape(GRID_W // 8, 8, p.shape[1]).sum(axis=0)


def _ffn_up_mid_fwd(hn, wg, off, cw, cb, nb, t, name):
    tcol = 256
    ncol = HID // tcol
    rows_sh = 2 * HID // N_CHIPS

    def conv(x, w_ref):
        zeros = jnp.zeros((GRID_W, x.shape[1]), x.dtype)
        down = jnp.concatenate([zeros, x[: x.shape[0] - GRID_W]], axis=0)
        up = jnp.concatenate([x[GRID_W:], zeros], axis=0)
        return down * w_ref[0:1, :] + x * w_ref[1:2, :] + up * w_ref[2:3, :]

    def body(h_ref, wa_ref, wg_ref, cwa_ref, cwg_ref, cba_ref, cbg_ref, u_ref, z_ref):
        hv = h_ref[...]
        ua = _dot(hv, wa_ref[0], _NT)
        ug = _dot(hv, wg_ref[0], _NT)
        u_ref[0] = ua.astype(BF16)
        u_ref[1] = ug.astype(BF16)
        a = conv(ua, cwa_ref) + cba_ref[...]
        gt = conv(ug, cwg_ref) + cbg_ref[...]
        z_ref[...] = (a * _silu(gt)).astype(BF16)

    def w_spec(part):
        def idx(b, j):
            n = part * HID + j * tcol
            return (n // rows_sh, (off + n % rows_sh) // tcol, 0)
        return pl.BlockSpec((1, tcol, D), idx)

    chan = lambda rows, part: pl.BlockSpec((rows, tcol), lambda b, j: (0, part * ncol + j))
    return pl.pallas_call(
        body, name=name, grid=(nb, ncol),
        in_specs=[pl.BlockSpec((t, D), lambda b, j: (b, 0)), w_spec(0), w_spec(1),
                  chan(3, 0), chan(3, 1), chan(1, 0), chan(1, 1)],
        out_specs=[pl.BlockSpec((2, t, tcol), lambda b, j: (0, b, j)), pl.BlockSpec((t, tcol), lambda b, j: (b, j))],
        out_shape=[jax.ShapeDtypeStruct((2, nb * t, HID), BF16), jax.ShapeDtypeStruct((nb * t, HID), BF16)],
        compiler_params=_params(("parallel", "parallel")),
    )(hn, wg, wg, cw, cw, cb, cb)


def _ffn_mid_bwd(u0, cw, cb, dz, nb, t, name):
    nc = HID // 128
    n_rows = t // GRID_W

    def body(ua3_ref, ug3_ref, wa_ref, wg_ref, ba_ref, bg_ref, dz_ref, du_ref, dw_ref, db_ref, dua_ref, dug_ref):
        ua_ref, ug_ref = ua3_ref.at[0], ug3_ref.at[0]
        b = pl.program_id(1)

        @pl.when(b == 0)
        def _():
            dw_ref[...] = jnp.zeros_like(dw_ref)
            db_ref[...] = jnp.zeros_like(db_ref)

        wa = [wa_ref[k:k + 1, :] for k in range(3)]
        wg = [wg_ref[k:k + 1, :] for k in range(3)]
        ba, bg = ba_ref[...], bg_ref[...]

        def pass1(i, acc, first, last):
            here = _row_ds(i)
            ap, ac, an = _grid_row(ua_ref, i, first, last)
            gp, gc, gn = _grid_row(ug_ref, i, first, last)
            a = ap * wa[0] + ac * wa[1] + an * wa[2] + ba
            gt = gp * wg[0] + gc * wg[1] + gn * wg[2] + bg
            dzv = dz_ref[here, :].astype(F32)
            s = _sigmoid(gt)
            silu = gt * s
            da = dzv * silu
            dg = (dzv * a) * (s + silu * (1.0 - s))
            dua_ref[here, :] = da
            dug_ref[here, :] = dg
            terms = (da, da * ap, da * ac, da * an, dg, dg * gp, dg * gc, dg * gn)
            return tuple(r + _fold8(p) for r, p in zip(acc, terms))

        zero = jnp.zeros((8, 128), F32)
        acc = _over_grid_rows(n_rows, pass1, (zero,) * 8)
        for part in range(2):
            db_ref[part] += _rowsum(acc[4 * part])
            for k in range(3):
                dw_ref[part, k:k + 1, :] += _rowsum(acc[4 * part + 1 + k])

        def pass2(i, carry, first, last):
            for part, (ref, w) in enumerate(((dua_ref, wa), (dug_ref, wg))):
                dp_, dc_, dn_ = _grid_row(ref, i, first, last)
                du_ref[part, _row_ds(i), :] = (dn_ * w[0] + dc_ * w[1] + dp_ * w[2]).astype(BF16)
            return carry

        _over_grid_rows(n_rows, pass2, 0)

    col = lambda rows, part: pl.BlockSpec((rows, 128), lambda j, b: (0, part * nc + j))
    part_of_u = lambda part: pl.BlockSpec((1, t, 128), lambda j, b: (part, b, j))
    return pl.pallas_call(
        body, name=name, grid=(nc, nb),
        in_specs=[part_of_u(0), part_of_u(1), col(3, 0), col(3, 1), col(1, 0), col(1, 1),
                  pl.BlockSpec((t, 128), lambda j, b: (b, j))],
        out_specs=[pl.BlockSpec((2, t, 128), lambda j, b: (0, b, j)), pl.BlockSpec((2, 3, 128), lambda j, b: (0, 0, j)),
                   pl.BlockSpec((2, 1, 128), lambda j, b: (0, 0, j))],
        out_shape=[jax.ShapeDtypeStruct((2, nb * t, HID), BF16), jax.ShapeDtypeStruct((2, 3, HID), F32),
                   jax.ShapeDtypeStruct((2, 1, HID), F32)],
        scratch_shapes=[pltpu.VMEM((t, 128), F32), pltpu.VMEM((t, 128), F32)],
        compiler_params=_params(("parallel", "arbitrary")),
    )(u0, u0, cw, cw, cb, cb, dz)


def _sc_mid_fwd(p, cw, nb, t):
    nc = D // 128

    def body(bg_ref, cg_ref, v_ref, w_ref, y_ref):
        cv = cg_ref[...].astype(F32) * v_ref[...].astype(F32)
        cc = _row_dn1(cv) * w_ref[0:1, :] + cv * w_ref[1:2, :] + _row_up1(cv) * w_ref[2:3, :]
        y_ref[...] = (bg_ref[...].astype(F32) * cc).astype(BF16)

    part = lambda k: pl.BlockSpec((t, 128), lambda j, b: (b, k * nc + j))
    return pl.pallas_call(
        body, name="sc_mid_fwd", grid=(nc, nb),
        in_specs=[part(0), part(1), part(2), pl.BlockSpec((3, 128), lambda j, b: (0, j))],
        out_specs=pl.BlockSpec((t, 128), lambda j, b: (b, j)),
        out_shape=jax.ShapeDtypeStruct((nb * t, D), BF16),
        compiler_params=_params(("parallel", "parallel")),
    )(p, p, p, cw)


def _sc_mid_bwd(p, cw, dyb, nb, t):
    nc = D // 128

    def body(bg_ref, cg_ref, v_ref, w_ref, dy_ref, dp_ref, dw_ref):
        b = pl.program_id(1)

        @pl.when(b == 0)
        def _():
            dw_ref[...] = jnp.zeros_like(dw_ref)

        w0, w1, w2 = w_ref[0:1, :], w_ref[1:2, :], w_ref[2:3, :]
        cg, v = cg_ref[...].astype(F32), v_ref[...].astype(F32)
        cv = cg * v
        cvd = _row_dn1(cv)
        cvu = _row_up1(cv)
        cc = cvd * w0 + cv * w1 + cvu * w2
        dy = dy_ref[...].astype(F32)
        dcc = dy * bg_ref[...].astype(F32)
        dw_ref[0:1, :] += _rowsum(dcc * cvd)
        dw_ref[1:2, :] += _rowsum(dcc * cv)
        dw_ref[2:3, :] += _rowsum(dcc * cvu)
        dcv = _row_up1(dcc) * w0 + dcc * w1 + _row_dn1(dcc) * w2
        dp_ref[0] = (dy * cc).astype(BF16)
        dp_ref[1] = (dcv * v).astype(BF16)
        dp_ref[2] = (dcv * cg).astype(BF16)

    part = lambda k: pl.BlockSpec((t, 128), lambda j, b: (b, k * nc + j))
    return pl.pallas_call(
        body, name="sc_mid_bwd", grid=(nc, nb),
        in_specs=[part(0), part(1), part(2), pl.BlockSpec((3, 128), lambda j, b: (0, j)),
                  pl.BlockSpec((t, 128), lambda j, b: (b, j))],
        out_specs=[pl.BlockSpec((3, t, 128), lambda j, b: (0, b, j)), pl.BlockSpec((3, 128), lambda j, b: (0, j))],
        out_shape=[jax.ShapeDtypeStruct((3, nb * t, D), BF16), jax.ShapeDtypeStruct((3, D), F32)],
        compiler_params=_params(("parallel", "arbitrary")),
    )(p, p, p, cw, dyb)


def _gla_decay_fwd(p_all, w2, b2):
    n = p_all.shape[0]
    tt = _tile(n, 512)

    def body(a_ref, w_ref, b_ref, la_ref):
        z = _dot(a_ref[...], w_ref[...]) + b_ref[...]
        la_ref[...] = (jnp.minimum(z, 0.0) - jnp.log(1.0 + jnp.exp(-jnp.abs(z)))) * (1.0 / TAU)

    return pl.pallas_call(
        body, name="gla_decay_fwd", grid=(n // tt,),
        in_specs=[pl.BlockSpec((tt, 128), lambda i: (i, (2 * KEY + 2 * D) // 128)),
                  pl.BlockSpec((128, 2 * KEY), lambda i: (0, 0)), pl.BlockSpec((1, 2 * KEY), lambda i: (0, 0))],
        out_specs=pl.BlockSpec((tt, 2 * KEY), lambda i: (i, 0)),
        out_shape=jax.ShapeDtypeStruct((n, 2 * KEY), F32),
        compiler_params=_params(("parallel",)),
    )(p_all, w2, b2)


def _gla_blocks(nb, nm, ncx):
    def main_idx(d, i):
        return jnp.clip(jnp.where(d == 0, i - ncx, nm - 1 - (i - ncx)), 0, nm - 1)

    def rowblk(d, b, i):
        cidx = jnp.where(d == 0, i, ncx - 1 - i)
        return jnp.where(i < ncx, nb * nm + b * ncx + cidx, b * nm + main_idx(d, i))

    def mainblk(d, b, i):
        return b * nm + main_idx(d, i)

    return rowblk, mainblk


def _gla_mask(d):
    row = lax.broadcasted_iota(jnp.int32, (CH, CH), 0)
    col = lax.broadcasted_iota(jnp.int32, (CH, CH), 1)
    diff = jnp.where(d == 0, row - col, col - row)
    mask = diff >= 0
    return mask, jnp.where(mask, 1.0, 0.0).astype(BF16), jnp.where(diff <= 0, 1.0, 0.0).astype(BF16)


def _tri_sum(m01, x):
    w = x.shape[1]
    hi = x.astype(BF16)
    r1 = x - hi.astype(F32)
    mid = r1.astype(BF16)
    lo = (r1 - mid.astype(F32)).astype(BF16)
    s = lax.dot_general(m01, jnp.concatenate([hi, mid, lo], axis=1), _NN, preferred_element_type=F32)
    return s[:, :w] + s[:, w:2 * w] + s[:, 2 * w:]


def _gla_chunk(q, k, g, bc):
    bl = _rowsum(g)
    eq = jnp.exp(bc)
    ek = jnp.exp(-bc)
    ed = jnp.exp(bl - bc)
    return bl, eq, ek, ed, q * Q_SCALE * eq, k * ek, k * ed


def _gla_scan_fwd(p_all, la_all, nb, t, tc):
    nm, ncx = t // CH, tc // CH
    nst = nm + ncx
    rowblk, mainblk = _gla_blocks(nb, nm, ncx)

    def body(*refs):
        ins, (o_refs, ss_refs, st_ref) = refs[:8], (refs[8:10], refs[10:12], refs[12])
        i = pl.program_id(1)

        @pl.when(i == 0)
        def _():
            st_ref[...] = jnp.zeros_like(st_ref)

        loaded = [r[...] for r in ins]
        states = [st_ref[j] for j in range(2 * HEADS)]
        outs, new_states = [[], []], []
        for d in range(2):
            q_all, k_all, v_all, g_all = loaded[4 * d:4 * d + 4]
            mask, m01, _ = _gla_mask(d)
            bc_all = _tri_sum(m01, g_all)
            for h in range(HEADS):
                ksl = slice(h * DK, (h + 1) * DK)
                v = v_all[:, h * DV:(h + 1) * DV]
                st = states[d * HEADS + h]
                bl, _, _, _, qs, ks, kd = _gla_chunk(q_all[:, ksl], k_all[:, ksl], g_all[:, ksl], bc_all[:, ksl])
                att = jnp.where(mask, _dot(qs, ks, _NT), 0.0)
                outs[d].append(_dot(qs, st, _NT) + _dot(att, v))
                new_states.append(st * jnp.exp(bl) + _dot(v, kd, _TN))
        for d in range(2):
            o_refs[d][...] = jnp.concatenate(outs[d], axis=1)
            for h in range(HEADS):
                ss_refs[d][0, 0, h] = states[d * HEADS + h]
                st_ref[d * HEADS + h] = new_states[d * HEADS + h]

    def in_specs(d):
        return [pl.BlockSpec((CH, KEY), lambda b, i: (rowblk(d, b, i), 0)),
                pl.BlockSpec((CH, KEY), lambda b, i: (rowblk(d, b, i), 1)),
                pl.BlockSpec((CH, D), lambda b, i: (rowblk(d, b, i), 1)),
                pl.BlockSpec((CH, KEY), lambda b, i: (rowblk(d, b, i), d))]

    outs = pl.pallas_call(
        body, name="gla_scan_fwd", grid=(nb, nst),
        in_specs=in_specs(0) + in_specs(1),
        out_specs=[pl.BlockSpec((CH, D), lambda b, i: (mainblk(0, b, i), 0)),
                   pl.BlockSpec((CH, D), lambda b, i: (mainblk(1, b, i), 0)),
                   pl.BlockSpec((1, 1, HEADS, DV, DK), lambda b, i: (b, i, 0, 0, 0)),
                   pl.BlockSpec((1, 1, HEADS, DV, DK), lambda b, i: (b, i, 0, 0, 0))],
        out_shape=[jax.ShapeDtypeStruct((nb * t, D), F32)] * 2
        + [jax.ShapeDtypeStruct((nb, nst, HEADS, DV, DK), F32)] * 2,
        scratch_shapes=[pltpu.VMEM((2 * HEADS, DV, DK), F32)],
        compiler_params=_params(("parallel", "arbitrary")),
    )(*([p_all, p_all, p_all, la_all] * 2))
    return outs[:2], outs[2:]


def _gla_scan_bwd(p_all, la_all, do, ss, nb, t, tc, after):
    nm, ncx = t // CH, tc // CH
    nst = nm + ncx
    ntot = nb * (t + tc)
    rowblk, mainblk = _gla_blocks(nb, nm, ncx)

    def body(*refs):
        ins, outs, dst_ref = refs[:12], refs[13:21], refs[21]
        ip = pl.program_id(1)
        i = nst - 1 - ip

        @pl.when(ip == 0)
        def _():
            dst_ref[...] = jnp.zeros_like(dst_ref)

        live = jnp.where(i >= ncx, 1.0, 0.0)
        loaded = [[r[...] for r in ins[6 * d:6 * d + 5]] for d in range(2)]
        states = [ins[6 * d + 5][0, 0, h] for d in range(2) for h in range(HEADS)]
        dstates = [dst_ref[j] for j in range(2 * HEADS)]
        results, new_dstates = [], []
        for d in range(2):
            q_all, k_all, v_all, g_all, do_all = loaded[d]
            do_all = do_all * live
            mask, m01, m01_t = _gla_mask(d)
            bc_all = _tri_sum(m01, g_all)
            dqs_l, dks_l, dvs_l, dbs_l, dbls_l = [], [], [], [], []
            for h in range(HEADS):
                ksl = slice(h * DK, (h + 1) * DK)
                vsl = slice(h * DV, (h + 1) * DV)
                bl, eq, ek, ed, qs, ks, kd = _gla_chunk(q_all[:, ksl], k_all[:, ksl], g_all[:, ksl], bc_all[:, ksl])
                st, dst, v, dov = states[d * HEADS + h], dstates[d * HEADS + h], v_all[:, vsl], do_all[:, vsl]
                att = jnp.where(mask, _dot(qs, ks, _NT), 0.0)
                datt = jnp.where(mask, _dot(dov, v, _NT), 0.0)
                dqs = _dot(dov, st) + _dot(datt, ks)
                dks = _dot(datt, qs, _TN)
                dvs_l.append(_dot(att, dov, _TN) + _dot(kd, dst, _NT))
                dkd = _dot(v, dst)
                e = jnp.exp(bl)
                dbls_l.append(e * _rowsum(st * dst) + _rowsum(dkd * kd))
                new_dstates.append(_dot(dov, qs, _TN) + dst * e)
                dqs_l.append(dqs * eq * Q_SCALE)
                dks_l.append(dks * ek + dkd * ed)
                dbs_l.append(dqs * qs - dks * ks - dkd * kd)
            results.append((jnp.concatenate(dqs_l, axis=1), jnp.concatenate(dks_l, axis=1),
                            jnp.concatenate(dvs_l, axis=1),
                            _tri_sum(m01_t, jnp.concatenate(dbs_l, axis=1)) + jnp.concatenate(dbls_l, axis=1)))
        for d in range(2):
            for k in range(4):
                outs[4 * d + k][...] = results[d][k]
        for j in range(2 * HEADS):
            dst_ref[j] = new_dstates[j]

    def in_specs(d):
        return [pl.BlockSpec((CH, KEY), lambda b, ip: (rowblk(d, b, nst - 1 - ip), 0)),
                pl.BlockSpec((CH, KEY), lambda b, ip: (rowblk(d, b, nst - 1 - ip), 1)),
                pl.BlockSpec((CH, D), lambda b, ip: (rowblk(d, b, nst - 1 - ip), 1)),
                pl.BlockSpec((CH, KEY), lambda b, ip: (rowblk(d, b, nst - 1 - ip), d)),
                pl.BlockSpec((CH, D), lambda b, ip: (mainblk(d, b, nst - 1 - ip), 0)),
                pl.BlockSpec((1, 1, HEADS, DV, DK), lambda b, ip: (b, nst - 1 - ip, 0, 0, 0))]

    def out_specs(d):
        row = lambda width: pl.BlockSpec((CH, width), lambda b, ip: (rowblk(d, b, nst - 1 - ip), 0))
        return [row(KEY), row(KEY), row(D), row(KEY)]

    shapes = [jax.ShapeDtypeStruct((ntot, KEY), F32), jax.ShapeDtypeStruct((ntot, KEY), F32),
              jax.ShapeDtypeStruct((ntot, D), F32), jax.ShapeDtypeStruct((ntot, KEY), F32)]
    outs = pl.pallas_call(
        body, name="gla_scan_bwd", grid=(nb, nst),
        in_specs=in_specs(0) + in_specs(1) + [pl.BlockSpec(memory_space=pl.ANY)],
        out_specs=out_specs(0) + out_specs(1),
        out_shape=shapes * 2,
        scratch_shapes=[pltpu.VMEM((2 * HEADS, DV, DK), F32)],
        compiler_params=_params(("parallel", "arbitrary")),
    )(p_all, p_all, p_all, la_all, do, ss[0], p_all, p_all, p_all, la_all, do, ss[1], after)
    return [[outs[k], outs[4 + k]] for k in range(4)]


def _gla_post_fwd(o2, p_all, head_gain, n):
    tt = _tile(n, 256)

    def body(of_ref, ob_ref, g_ref, hg_ref, y_ref):
        o = of_ref[...] + ob_ref[...]
        gv = g_ref[...]
        hg = hg_ref[...]
        for h in range(HEADS):
            oh = o[:, h * DV:(h + 1) * DV]
            r = lax.rsqrt(jnp.mean(oh * oh, axis=-1, keepdims=True) + EPS)
            y_ref[:, h * DV:(h + 1) * DV] = ((oh * r) * hg * _silu(gv[:, h * DV:(h + 1) * DV])).astype(BF16)

    row = pl.BlockSpec((tt, D), lambda i: (i, 0))
    return pl.pallas_call(
        body, name="gla_post_fwd", grid=(n // tt,),
        in_specs=[row, row, pl.BlockSpec((tt, D), lambda i: (i, 2)), pl.BlockSpec((1, DV), lambda i: (0, 0))],
        out_specs=row,
        out_shape=jax.ShapeDtypeStruct((n, D), BF16),
        compiler_params=_params(("parallel",)),
    )(o2[0], o2[1], p_all, head_gain)


def _gla_post_bwd(o2, p_all, head_gain, dyb, n):
    tt = _tile(n, 256)

    def body(of_ref, ob_ref, g_ref, hg_ref, dy_ref, do_ref, dg_ref, dhg_ref):
        i = pl.program_id(0)

        @pl.when(i == 0)
        def _():
            dhg_ref[...] = jnp.zeros_like(dhg_ref)

        o = of_ref[...] + ob_ref[...]
        gv = g_ref[...]
        hg = hg_ref[...]
        dy = dy_ref[...]
        acc = jnp.zeros((1, DV), F32)
        for h in range(HEADS):
            sl = slice(h * DV, (h + 1) * DV)
            oh = o[:, sl]
            r = lax.rsqrt(jnp.mean(oh * oh, axis=-1, keepdims=True) + EPS)
            on = oh * r
            gh = gv[:, sl]
            dyh = dy[:, sl]
            dg_ref[:, sl] = dyh * (on * hg) * _dsilu(gh)
            dog = dyh * _silu(gh)
            acc = acc + _rowsum(dog * on)
            don = dog * hg
            do_ref[:, sl] = r * (don - on * jnp.mean(don * on, axis=-1, keepdims=True))
        dhg_ref[...] += acc

    return pl.pallas_call(
        body, name="gla_post_bwd", grid=(n // tt,),
        in_specs=[pl.BlockSpec((tt, D), lambda i: (i, 0)), pl.BlockSpec((tt, D), lambda i: (i, 0)),
                  pl.BlockSpec((tt, D), lambda i: (i, 2)),
                  pl.BlockSpec((1, DV), lambda i: (0, 0)), pl.BlockSpec((tt, D), lambda i: (i, 0))],
        out_specs=[pl.BlockSpec((tt, D), lambda i: (i, 0)), pl.BlockSpec((tt, D), lambda i: (i, 0)),
                   pl.BlockSpec((1, DV), lambda i: (0, 0))],
        out_shape=[jax.ShapeDtypeStruct((n, D), F32), jax.ShapeDtypeStruct((n, D), F32),
                   jax.ShapeDtypeStruct((1, DV), F32)],
        compiler_params=_params(("arbitrary",)),
    )(o2[0], o2[1], p_all, head_gain, dyb)


def _gla_assemble(p_all, w2, b2, dq, dk, dv, dla, dgate, n):
    ntot = p_all.shape[0]
    tt = _tile(n, 128)
    nmain = n // tt
    assert ntot % tt == 0

    def body(a_ref, w_ref, b_ref, dqf_ref, dqb_ref, dkf_ref, dkb_ref, dvf_ref, dvb_ref, dlf_ref, dlb_ref, dg_ref,
             dp_ref, dw_ref, db_ref):
        i = pl.program_id(0)

        @pl.when(i == 0)
        def _():
            dw_ref[...] = jnp.zeros_like(dw_ref)
            db_ref[...] = jnp.zeros_like(db_ref)

        a = a_ref[...]
        w = w_ref[...]
        z = _dot(a, w) + b_ref[...]
        dla = jnp.concatenate([dlf_ref[...], dlb_ref[...]], axis=1)
        dz = dla * (1.0 / (1.0 + jnp.exp(z))) * (1.0 / TAU)
        dw_ref[...] += _dot(a, dz, _TN)
        db_ref[...] += _rowsum(dz)
        dp_ref[:, 0:KEY] = (dqf_ref[...] + dqb_ref[...]).astype(BF16)
        dp_ref[:, KEY:2 * KEY] = (dkf_ref[...] + dkb_ref[...]).astype(BF16)
        dp_ref[:, 2 * KEY:2 * KEY + D] = (dvf_ref[...] + dvb_ref[...]).astype(BF16)
        dp_ref[:, 2 * KEY + D:2 * KEY + 2 * D] = (dg_ref[...] * jnp.where(i < nmain, 1.0, 0.0)).astype(BF16)
        dp_ref[:, 2 * KEY + 2 * D:GLA_IN_PAD] = _dot(dz, w, _NT).astype(BF16)

    row = lambda width: pl.BlockSpec((tt, width), lambda i: (i, 0))
    return pl.pallas_call(
        body, name="gla_assemble", grid=(ntot // tt,),
        in_specs=[pl.BlockSpec((tt, 128), lambda i: (i, (2 * KEY + 2 * D) // 128)),
                  pl.BlockSpec((128, 2 * KEY), lambda i: (0, 0)), pl.BlockSpec((1, 2 * KEY), lambda i: (0, 0)),
                  row(KEY), row(KEY), row(KEY), row(KEY), row(D), row(D), row(KEY), row(KEY),
                  pl.BlockSpec((tt, D), lambda i: (jnp.minimum(i, nmain - 1), 0))],
        out_specs=[pl.BlockSpec((tt, GLA_IN_PAD), lambda i: (i, 0)), pl.BlockSpec((128, 2 * KEY), lambda i: (0, 0)),
                   pl.BlockSpec((1, 2 * KEY), lambda i: (0, 0))],
        out_shape=[jax.ShapeDtypeStruct((ntot, GLA_IN_PAD), BF16), jax.ShapeDtypeStruct((128, 2 * KEY), F32),
                   jax.ShapeDtypeStruct((1, 2 * KEY), F32)],
        compiler_params=_params(("arbitrary",)),
    )(p_all, w2, b2, dq[0], dq[1], dk[0], dk[1], dv[0], dv[1], dla[0], dla[1], dgate)


ADA_ROWS = 24
ADA_SH = N_MOD * D // N_CHIPS


def _ada_fwd(cvec, ada_w, ada_b_sh):
    def body(c_ref, w_ref, b_ref, o_ref):
        o_ref[0] = _dot(_silu(c_ref[...]), w_ref[0]) + b_ref[0]

    return pl.pallas_call(
        body, name="ada_fwd", grid=(2,),
        in_specs=[pl.BlockSpec((ADA_ROWS, D), lambda l: (0, 0)), pl.BlockSpec((1, D, ADA_SH), lambda l: (l, 0, 0)),
                  pl.BlockSpec((1, 1, ADA_SH), lambda l: (l, 0, 0))],
        out_specs=pl.BlockSpec((1, ADA_ROWS, ADA_SH), lambda l: (l, 0, 0)),
        out_shape=jax.ShapeDtypeStruct((2, ADA_ROWS, ADA_SH), F32),
        compiler_params=_params(("parallel",)),
    )(cvec, ada_w, ada_b_sh)


def _ada_bwd(cvec, ada_w, dmod_sh):
    def body(c_ref, w_ref, dm_ref, gw_ref, dc_ref):
        dm = dm_ref[0]
        gw_ref[0] = _dot(_silu(c_ref[...]), dm, _TN)
        dc_ref[0] = _dot(dm, w_ref[0], _NT)

    return pl.pallas_call(
        body, name="ada_bwd", grid=(2,),
        in_specs=[pl.BlockSpec((ADA_ROWS, D), lambda l: (0, 0)), pl.BlockSpec((1, D, ADA_SH), lambda l: (l, 0, 0)),
                  pl.BlockSpec((1, ADA_ROWS, ADA_SH), lambda l: (l, 0, 0))],
        out_specs=[pl.BlockSpec((1, D, ADA_SH), lambda l: (l, 0, 0)), pl.BlockSpec((1, ADA_ROWS, D), lambda l: (l, 0, 0))],
        out_shape=[jax.ShapeDtypeStruct((2, D, ADA_SH), F32), jax.ShapeDtypeStruct((2, ADA_ROWS, D), F32)],
        compiler_params=_params(("parallel",)),
    )(cvec, ada_w, dmod_sh)


def _sum_slots(x, name):
    s, r, _ = x.shape

    def body(x_ref, o_ref):
        acc = x_ref[0]
        for k in range(1, s):
            acc = acc + x_ref[k]
        o_ref[...] = acc

    return pl.pallas_call(
        body, name=name, out_shape=jax.ShapeDtypeStruct((r, 128), F32),
        in_specs=[pl.BlockSpec(memory_space=pltpu.VMEM)], out_specs=pl.BlockSpec(memory_space=pltpu.VMEM),
    )(x)


def _cctx_grad(dscc_parts, c_ctx):
    def body(p_ref, c_ref, o_ref):
        acc = p_ref[0]
        for k in range(1, N_CHIPS):
            acc = acc + p_ref[k]
        o_ref[...] = acc * _dsilu(c_ref[...])

    return pl.pallas_call(
        body, name="cctx_grad", out_shape=jax.ShapeDtypeStruct((8, 128), F32),
        in_specs=[pl.BlockSpec(memory_space=pltpu.VMEM)] * 2, out_specs=pl.BlockSpec(memory_space=pltpu.VMEM),
    )(dscc_parts, c_ctx)


def _adamw(w, g, m, v, name, after):
    nl, r, cdim = w.shape
    tr = _tile(r, 256)
    c1 = 1.0 - ADAM_B1 ** ADAM_STEP
    c2 = 1.0 - ADAM_B2 ** ADAM_STEP

    def body(w_ref, g_ref, m_ref, v_ref, after_ref, d_ref, mo_ref, vo_ref):
        gv = g_ref[...]
        mn = ADAM_B1 * m_ref[...] + (1.0 - ADAM_B1) * gv
        vn = ADAM_B2 * v_ref[...] + (1.0 - ADAM_B2) * (gv * gv)
        mo_ref[...] = mn
        vo_ref[...] = vn
        d_ref[...] = -ADAM_LR * ((mn / c1) / (jnp.sqrt(vn / c2) + ADAM_EPS) + ADAM_WD * w_ref[...])

    spec = pl.BlockSpec((1, tr, cdim), lambda l, i: (l, i, 0))
    sds = jax.ShapeDtypeStruct((nl, r, cdim), F32)
    return pl.pallas_call(
        body, name=name, grid=(nl, r // tr), in_specs=[spec] * 4 + [pl.BlockSpec(memory_space=pl.ANY)],
        out_specs=[spec] * 3, out_shape=[sds] * 3, compiler_params=_params(("parallel", "parallel")),
    )(w, g, m, v, after)


def _place():
    x, y, c = lax.axis_index("x"), lax.axis_index("y"), lax.axis_index("c")
    return x, y, c


def _allgather_small(blk, name):
    m_per, n = blk.shape

    def body(x_ref, out_ref, send_sems, recv_sems, local_sem):
        x, y, c = _place()
        me, sibling = (x, y, c), (x, y, 1 - c)
        chips = [(1 - x, y), (x, 1 - y), (1 - x, 1 - y)]

        def rows(px, py, pc):
            return out_ref.at[pl.ds((4 * px + 2 * py + pc) * m_per, m_per), :]

        def copy(k, block, to, src=None):
            return pltpu.make_async_remote_copy(
                src_ref=rows(*block) if src is None else src, dst_ref=rows(*block),
                send_sem=send_sems.at[k], recv_sem=recv_sems.at[k], device_id=to, device_id_type=MESH)

        mine = pltpu.make_async_copy(x_ref, rows(*me), local_sem)
        mine.start()
        first = [copy(0, me, sibling, src=x_ref)]
        first += [copy(1 + j, me, (*chip, c), src=x_ref) for j, chip in enumerate(chips)]
        for cp in first:
            cp.start()
        passed = [copy(4 + j, (*chip, c), sibling) for j, chip in enumerate(chips)]
        for j, chip in enumerate(chips):
            copy(1 + j, (*chip, c), me).wait_recv()
            passed[j].start()
        copy(0, sibling, me).wait_recv()
        for j, chip in enumerate(chips):
            copy(4 + j, (*chip, 1 - c), me).wait_recv()
        for cp in first + passed:
            cp.wait_send()
        mine.wait()

    return pl.pallas_call(
        body, name=name,
        out_shape=jax.ShapeDtypeStruct((N_DEV * m_per, n), blk.dtype),
        in_specs=[pl.BlockSpec(memory_space=pltpu.VMEM)],
        out_specs=pl.BlockSpec(memory_space=pltpu.VMEM),
        scratch_shapes=[pltpu.SemaphoreType.DMA((7,)), pltpu.SemaphoreType.DMA((7,)), pltpu.SemaphoreType.DMA],
    )(blk)


def _other_chips(x, y):
    return [(1 - x, y), (x, 1 - y), (1 - x, 1 - y)]


_HBM_SPEC = pl.BlockSpec(memory_space=pltpu.HBM)
_SEM_SPEC = pl.BlockSpec(memory_space=pltpu.SEMAPHORE)
_SPLIT_PARAMS = pltpu.CompilerParams(has_side_effects=pltpu.SideEffectType.DATAFLOW_SIDE_EFFECTING)


def _in_hbm(a):
    return pltpu.with_memory_space_constraint(a, pltpu.HBM)


def _ag_copies(own_ref, land_ref, send_sems, recv_sems):
    x, y, c = _place()
    chip = 2 * x + y
    hr = own_ref.shape[0] // 2

    def half(ch):
        return land_ref.at[ch, pl.ds(c * hr, hr), :]

    def copy(k, src, dst, to):
        return pltpu.make_async_remote_copy(src_ref=src, dst_ref=dst, send_sem=send_sems.at[k],
                                            recv_sem=recv_sems.at[k], device_id=to, device_id_type=MESH)

    sends, expects = [], []
    for j, (ox, oy) in enumerate(_other_chips(x, y)):
        sends.append(copy(j, own_ref.at[pl.ds(c * hr, hr), :], half(chip), (ox, oy, c)))
        expects.append(copy(j, half(2 * ox + oy), half(2 * ox + oy), (ox, oy, c)))
    own_slot = copy(3, own_ref, land_ref.at[chip], (x, y, 1 - c))
    return sends + [own_slot], expects + [own_slot]


def _sc_copies(p_ref, land_ref, send_sems, recv_sems):
    x, y, c = _place()
    chip = 2 * x + y
    sends, expects = [], []
    for j, (ox, oy) in enumerate(_other_chips(x, y)):
        och = 2 * ox + oy
        mk = lambda dst_slot: pltpu.make_async_remote_copy(
            src_ref=p_ref.at[och], dst_ref=land_ref.at[dst_slot], send_sem=send_sems.at[j],
            recv_sem=recv_sems.at[j], device_id=(ox, oy, c), device_id_type=MESH)
        sends.append(mk(chip))
        expects.append(mk(och))
    return sends, expects


def _pe_copies(g_ref, land_ref, send_sems, recv_sems):
    x, y, c = _place()
    hr = g_ref.shape[1] // 2
    cp = pltpu.make_async_remote_copy(
        src_ref=g_ref.at[:, pl.ds((1 - c) * hr, hr), :], dst_ref=land_ref, send_sem=send_sems.at[0],
        recv_sem=recv_sems.at[0], device_id=(x, y, 1 - c), device_id_type=MESH)
    return [cp], [cp]


def _split_start(src, land_shape, copies, n_copies, after, name):
    def body(src_ref, land_ref, after_ref, send_sems, recv_sems, src_thru, land_thru, token):
        for cp in copies(src_ref, land_ref, send_sems, recv_sems)[0]:
            cp.start()
        token[...] = jnp.zeros_like(token)

    land = lax.empty(land_shape, src.dtype)
    return pl.pallas_call(
        body, name=name,
        out_shape=(pltpu.SemaphoreType.DMA((n_copies,)), pltpu.SemaphoreType.DMA((n_copies,)),
                   pltpu.HBM(src.shape, src.dtype), pltpu.HBM(land_shape, src.dtype),
                   jax.ShapeDtypeStruct((8, 128), F32)),
        in_specs=(_HBM_SPEC, _HBM_SPEC, pl.BlockSpec(memory_space=pl.ANY)),
        out_specs=(_SEM_SPEC, _SEM_SPEC, _HBM_SPEC, _HBM_SPEC, pl.BlockSpec(memory_space=pltpu.VMEM)),
        input_output_aliases={0: 2, 1: 3}, compiler_params=_SPLIT_PARAMS,
    )(_in_hbm(src), _in_hbm(land), after)


def _split_wait(started, after, copies, name):
    send_sems, recv_sems, src_thru, land_thru, _ = started

    def body(src_ref, land_ref, send_sems, recv_sems, after_ref, src_dead, got_ref):
        sends, expects = copies(src_ref, land_ref, send_sems, recv_sems)
        for cp in sends:
            cp.wait_send()
        for cp in expects:
            cp.wait_recv()

    return pl.pallas_call(
        body, name=name,
        out_shape=(pltpu.HBM(src_thru.shape, src_thru.dtype), pltpu.HBM(land_thru.shape, land_thru.dtype)),
        in_specs=(_HBM_SPEC, _HBM_SPEC, _SEM_SPEC, _SEM_SPEC, pl.BlockSpec(memory_space=pl.ANY)),
        out_specs=(_HBM_SPEC, _HBM_SPEC), input_output_aliases={0: 0, 1: 1}, compiler_params=_SPLIT_PARAMS,
    )(src_thru, land_thru, send_sems, recv_sems, after)


def _ag_pass_on(land, name):
    hr = land.shape[1] // 2

    def body(in_ref, out_ref, send_sems, recv_sems):
        x, y, c = _place()

        def copy(j, ox, oy, cc):
            ref = out_ref.at[2 * ox + oy, pl.ds(cc * hr, hr), :]
            return pltpu.make_async_remote_copy(src_ref=ref, dst_ref=ref, send_sem=send_sems.at[j],
                                                recv_sem=recv_sems.at[j], device_id=(x, y, 1 - c),
                                                device_id_type=MESH)

        others = _other_chips(x, y)
        for j, (ox, oy) in enumerate(others):
            copy(j, ox, oy, c).start()
        for j, (ox, oy) in enumerate(others):
            copy(j, ox, oy, 1 - c).wait_recv()
        for j, (ox, oy) in enumerate(others):
            copy(j, ox, oy, c).wait_send()

    any_spec = pl.BlockSpec(memory_space=pl.ANY)
    return pl.pallas_call(
        body, name=name, out_shape=jax.ShapeDtypeStruct(land.shape, land.dtype),
        in_specs=[any_spec], out_specs=any_spec, input_output_aliases={0: 0},
        scratch_shapes=[pltpu.SemaphoreType.DMA((3,)), pltpu.SemaphoreType.DMA((3,))],
    )(land)


def _rs_pair_exchange(g, name):
    r = g.shape[1]
    hr = r // 2

    def body(g_ref, got_ref, send_sem, recv_sem):
        x, y, c = _place()
        cp = pltpu.make_async_remote_copy(
            src_ref=g_ref.at[:, pl.ds((1 - c) * hr, hr), :], dst_ref=got_ref, send_sem=send_sem, recv_sem=recv_sem,
            device_id=(x, y, 1 - c), device_id_type=MESH)
        cp.start()
        cp.wait()

    any_spec = pl.BlockSpec(memory_space=pl.ANY)
    return pl.pallas_call(
        body, name=name,
        out_shape=jax.ShapeDtypeStruct((N_CHIPS, hr, D), F32),
        in_specs=[any_spec], out_specs=any_spec,
        scratch_shapes=[pltpu.SemaphoreType.DMA, pltpu.SemaphoreType.DMA],
    )(g)


def _rs_chip_sum(place, g, got, name):
    r = g.shape[1]
    hr = r // 2
    tr = _tile(hr, 640, 16)
    nt = hr // tr

    def body(pl_ref, g_ref, got_ref, p16_ref, p32_ref):
        s = pl.program_id(1)
        p = g_ref[0] + got_ref[0]
        p16_ref[0] = p.astype(BF16)

        @pl.when(s == pl_ref[1])
        def _():
            p32_ref[...] = p

    return pl.pallas_call(
        body, name=name,
        grid_spec=pltpu.PrefetchScalarGridSpec(
            num_scalar_prefetch=1, grid=(nt, N_CHIPS),
            in_specs=[pl.BlockSpec((1, tr, D), lambda i, s, pr: (s, pr[0] * nt + i, 0)),
                      pl.BlockSpec((1, tr, D), lambda i, s, pr: (s, i, 0))],
            out_specs=[pl.BlockSpec((1, tr, D), lambda i, s, pr: (s, i, 0)),
                       pl.BlockSpec((tr, D), lambda i, s, pr: (i, 0))]),
        out_shape=[jax.ShapeDtypeStruct((N_CHIPS, hr, D), BF16), jax.ShapeDtypeStruct((hr, D), F32)],
        compiler_params=_params(("parallel", "arbitrary")),
    )(place, g, got)


def _rs_final_sum(place, parts, p32, name):
    hr = parts.shape[1]
    tr = _tile(hr, 640, 16)
    nt = hr // tr

    def body(pl_ref, a_ref, b_ref, c_ref, p32_ref, o_ref):
        o_ref[...] = ((p32_ref[...] + a_ref[0].astype(F32)) + b_ref[0].astype(F32)) + c_ref[0].astype(F32)

    def other(j):
        return pl.BlockSpec((1, tr, D), lambda i, pr: (j + jnp.where(pr[1] <= j, 1, 0), i, 0))

    return pl.pallas_call(
        body, name=name,
        grid_spec=pltpu.PrefetchScalarGridSpec(
            num_scalar_prefetch=1, grid=(nt,),
            in_specs=[other(0), other(1), other(2), pl.BlockSpec((tr, D), lambda i, pr: (i, 0))],
            out_specs=pl.BlockSpec((tr, D), lambda i, pr: (pr[0] * nt + i, 0))),
        out_shape=jax.ShapeDtypeStruct((2 * hr, D), F32),
        compiler_params=_params(("parallel",)),
    )(place, parts, parts, parts, p32)


def _rs_pair_gather(both, name):
    hr = both.shape[0] // 2

    def body(in_ref, out_ref, send_sem, recv_sem):
        x, y, c = _place()
        mine = out_ref.at[pl.ds(c * hr, hr), :]
        cp = pltpu.make_async_remote_copy(
            src_ref=mine, dst_ref=mine, send_sem=send_sem, recv_sem=recv_sem,
            device_id=(x, y, 1 - c), device_id_type=MESH)
        cp.start()
        theirs = out_ref.at[pl.ds((1 - c) * hr, hr), :]
        pltpu.make_async_remote_copy(
            src_ref=theirs, dst_ref=theirs, send_sem=send_sem, recv_sem=recv_sem,
            device_id=(x, y, 1 - c), device_id_type=MESH).wait_recv()
        cp.wait_send()

    any_spec = pl.BlockSpec(memory_space=pl.ANY)
    return pl.pallas_call(
        body, name=name,
        out_shape=jax.ShapeDtypeStruct(both.shape, F32),
        in_specs=[any_spec], out_specs=any_spec, input_output_aliases={0: 0},
        scratch_shapes=[pltpu.SemaphoreType.DMA, pltpu.SemaphoreType.DMA],
    )(both)


def _local_step(x, ctx, tgt, mods, mc, ag_gin, ag_main, place, small):
    nb, t, _ = x.shape
    tc = ctx.shape[1]
    n = nb * t
    nc = nb * tc
    xf = x.reshape(n, D)
    cf = ctx.reshape(nc, D)
    tf = tgt.reshape(n, D)
    vec = lambda a: a.reshape(1, -1)
    m = [[mods[l, :, k, :].reshape(nb, 1, D) for k in range(N_MOD)] for l in range(2)]
    mc_b = [jnp.broadcast_to(mc[k].reshape(1, 1, D), (nb, 1, D)) for k in range(2)]

    cw = [small["ffn_conv_w"][l] for l in range(2)]
    cb = [small["ffn_conv_b"][l].reshape(1, -1) for l in range(2)]
    w2 = jnp.zeros((128, 2 * KEY), F32)
    w2 = w2.at[0:RANK, 0:KEY].set(small["gla_w_a2"][0]).at[RANK:2 * RANK, KEY:].set(small["gla_w_a2"][1])
    b2 = small["gla_b_a"].reshape(1, 2 * KEY)
    hg = small["gla_head_norm"].reshape(1, DV)

    hn_all = _mod_fwd(xf, vec(small["norm_mix"][0]), m[0][0], m[0][1], t, "mod0_main", out_rows=n + nc)
    hn_all = _mod_fwd(cf, vec(small["norm_mix"][0]), mc_b[0], mc_b[1], tc, "mod0_ctx", out_rows=n + nc,
                      into=hn_all, row0=n)
    gin = _ag_pass_on(_split_wait(ag_gin, hn_all, _ag_copies, "ag_gin_wait")[1], "ag_gin_pass_on")
    w_gin = jnp.pad(gin[:, :_GIN_ROWS, :].reshape(GLA_IN, D), ((0, GLA_IN_PAD - GLA_IN), (0, 0)))
    p_all = _mm(hn_all, w_gin, "nt", F32, "gla_in_proj", 768, 3200)
    la_all = _gla_decay_fwd(p_all, w2, b2)
    o2, ss = _gla_scan_fwd(p_all, la_all, nb, t, tc)
    wg = _ag_pass_on(_split_wait(ag_main, o2[0], _ag_copies, "ag_main_wait")[1], "ag_main_pass_on")
    offs = _offsets(_MAIN, _MAIN_ROWS)
    rows = _MAIN_ROWS

    def w_nt(a, k, name, out_dtype=BF16, tm=1024):
        return _mm_nt_w(a, wg, offs[k], rows[k], name, tm, out_dtype)

    def w_nn(a3, k, name, tm, tn):
        return _mm_nn_w(a3, wg, offs[k], rows[k], name, tm, tn)

    def w_nn_mod(a3, k, h, gate, gain, shift, scale, name):
        return _mm_nn_w_mod(a3, wg, offs[k], rows[k], h, gate, vec(gain), shift, scale, t, name, 512)

    yb0 = _gla_post_fwd(o2, p_all, hg, n)
    y0, h1, hn1 = w_nn_mod(yb0[None], "gla_out", xf, m[0][2], small["norm_ffn"][0], m[0][3], m[0][4],
                           "gla_out_proj_mod")
    u0, z0 = _ffn_up_mid_fwd(hn1, wg, offs["up_t0"], cw[0], cb[0], nb, t, "ffn0_up_mid")
    f0, h2, hn2 = w_nn_mod(z0[None], "down0", h1, m[0][5], small["norm_mix"][1], m[1][0], m[1][1],
                           "ffn0_down_mod")
    p1 = w_nt(hn2, "sc_in_t", "sc_in_proj")
    yb1 = _sc_mid_fwd(p1, small["sc_conv_w"], nb, t)
    y1, h3, hn3 = w_nn_mod(yb1[None], "sc_out", h2, m[1][2], small["norm_ffn"][1], m[1][3], m[1][4],
                           "sc_out_proj_mod")
    u1, z1 = _ffn_up_mid_fwd(hn3, wg, offs["up_t1"], cw[1], cb[1], nb, t, "ffn1_up_mid")
    f1 = w_nn(z1[None], "down1", "ffn1_down", 1024, 1024)
    loss, dh4, df1, dm15, dfinal = _final(h3, f1, m[1][5], vec(small["final_norm"]), tf, t)

    gs = {}
    dmods = [[None] * N_MOD for _ in range(2)]
    dmods[1][5] = dm15

    def w_dw(a3, b, g_prev, k, name, tm):
        return _mm_dw(a3, b, g_prev, offs[k], rows[k], name, tm)

    def ffn_bwd(l, df, u, z, hn, g_prev):
        dz = w_nt(df, f"down{l}", f"ffn{l}_down_dx")
        g_acc = w_dw(z[None], df, g_prev, f"down{l}", f"ffn{l}_down_dw", 640)
        du, dcw, dcb = _ffn_mid_bwd(u, cw[l], cb[l], dz, nb, t, f"ffn{l}_mid_bwd")
        dhn = w_nn(du, f"up_t{l}", f"ffn{l}_up_dx", 512, 512)
        g_acc = w_dw(du, hn, g_acc, f"up_t{l}", f"ffn{l}_up_dw", 640)
        return dhn, g_acc, jnp.moveaxis(dcw, 0, 1).reshape(3, 2 * HID), dcb.reshape(2 * HID)

    dhn3, g_acc, dcw1, dcb1 = ffn_bwd(1, df1, u1, z1, hn3, None)
    r = _mod_bwd(h3, dhn3, vec(small["norm_ffn"][1]), m[1][4], t, "mod1_ffn_bwd", dh_out=dh4, y_prev=y1,
                 gate_prev=m[1][2])
    dh3, dmods[1][4], dmods[1][3], dnf1, dy1, dmods[1][2] = (r["dh"], r["dscale"], r["dshift"], r["dgain"],
                                                             r["dy_prev"], r["dgate_prev"])
    dyb1 = w_nt(dy1, "sc_out", "sc_out_dx")
    g_acc = w_dw(yb1[None], dy1, g_acc, "sc_out", "sc_out_dw", 256)
    dp1, dscw = _sc_mid_bwd(p1, small["sc_conv_w"], dyb1, nb, t)
    dhn2 = w_nn(dp1, "sc_in_t", "sc_in_dx", 1024, 512)
    g_acc = w_dw(dp1, hn2, g_acc, "sc_in_t", "sc_in_dw", 256)
    r = _mod_bwd(h2, dhn2, vec(small["norm_mix"][1]), m[1][1], t, "mod1_mix_bwd", dh_out=dh3, y_prev=f0,
                 gate_prev=m[0][5])
    dh2, dmods[1][1], dmods[1][0], dnm1, df0, dmods[0][5] = (r["dh"], r["dscale"], r["dshift"], r["dgain"],
                                                             r["dy_prev"], r["dgate_prev"])
    dhn1, g_acc, dcw0, dcb0 = ffn_bwd(0, df0, u0, z0, hn1, g_acc)
    r = _mod_bwd(h1, dhn1, vec(small["norm_ffn"][0]), m[0][4], t, "mod0_ffn_bwd", dh_out=dh2, y_prev=y0,
                 gate_prev=m[0][2])
    dh1, dmods[0][4], dmods[0][3], dnf0, dy0, dmods[0][2] = (r["dh"], r["dscale"], r["dshift"], r["dgain"],
                                                             r["dy_prev"], r["dgate_prev"])
    g_packed = w_dw(yb0[None], dy0, g_acc, "gla_out", "gla_out_dw", 256)
    pair = _split_start(g_packed, (N_CHIPS, _MAIN_TOTAL // 2, D), _pe_copies, 1, dy0, "rs_main_pair_start")
    dyb0 = w_nt(dy0, "gla_out", "gla_out_dx", F32)
    do, dgate, dhg = _gla_post_bwd(o2, p_all, hg + pair[4][0:1, 0:1], dyb0, n)
    g_packed, from_sibling = _split_wait(pair, do, _pe_copies, "rs_main_pair_wait")
    p16, p32 = _rs_chip_sum(place, g_packed, from_sibling, "rs_main_chip_sum")
    sc_main = _split_start(p16, p16.shape, _sc_copies, 3, p32, "rs_main_scatter_start")
    dq, dk, dv, dla = _gla_scan_bwd(p_all, la_all, do, ss, nb, t, tc, sc_main[4])
    dp, dw2, db2 = _gla_assemble(p_all, w2, b2, dq, dk, dv, dla, dgate, n)
    dhn_all = _mm(dp, w_gin, "nn", F32, "gla_in_dx", 768, 512)
    landed = _split_wait(sc_main, dhn_all, _sc_copies, "rs_main_scatter_wait")[1]
    g_main = _rs_pair_gather(_rs_final_sum(place, landed, p32, "rs_main_final_sum"), "rs_main_pair_gather")
    g_gin = _mm(dp, hn_all, "tn", F32, "gla_in_dw", 640, 1024)[:GLA_IN]
    g_gin = jnp.pad(g_gin.reshape(N_CHIPS, _GIN_ROWS, D), ((0, 0), (0, _GIN_PAD - _GIN_ROWS), (0, 0)))
    from_sibling = _rs_pair_exchange(g_gin, "rs_gin_pair_exchange")
    p16_gin, p32_gin = _rs_chip_sum(place, g_gin, from_sibling, "rs_gin_chip_sum")
    r = _mod_bwd(xf, dhn_all, vec(small["norm_mix"][0]), m[0][1], t, "mod0_main_bwd", dh_out=dh1)
    grad_x, dmods[0][1], dmods[0][0], dnm0 = r["dh"], r["dscale"], r["dshift"], r["dgain"]
    rc = _mod_bwd(cf, dhn_all, vec(small["norm_mix"][0]), mc_b[1], tc, "mod0_ctx_bwd", dhn_row0=n, need_dh=False)
    dmc = jnp.stack([jnp.sum(rc["dshift"], axis=0).reshape(D), jnp.sum(rc["dscale"], axis=0).reshape(D)])
    dnm0 = dnm0 + rc["dgain"]

    gs["norm_mix"] = jnp.concatenate([dnm0, dnm1], axis=0)
    gs["norm_ffn"] = jnp.concatenate([dnf0, dnf1], axis=0)
    gs["final_norm"] = dfinal.reshape(D)
    gs["gla_w_a2"] = jnp.stack([dw2[0:RANK, 0:KEY], dw2[RANK:2 * RANK, KEY:]])
    gs["gla_b_a"] = db2.reshape(2, KEY)
    gs["gla_head_norm"] = dhg.reshape(DV)
    gs["sc_conv_w"] = dscw
    gs["ffn_conv_w"] = jnp.stack([dcw0, dcw1])
    gs["ffn_conv_b"] = jnp.stack([dcb0, dcb1])
    dmods_arr = jnp.stack([jnp.stack([dmods[l][k].reshape(nb, D) for k in range(N_MOD)], axis=1) for l in range(2)])
    return loss, grad_x.reshape(nb, t, D), g_main, p16_gin, p32_gin, gs, dmods_arr, dmc


def _pack(arrs):
    parts, meta, off = [], [], 0
    for a in arrs:
        r = a.size // 128
        rp = -(-r // 8) * 8
        a2 = a.reshape(r, 128).astype(F32)
        if rp != r:
            a2 = jnp.pad(a2, ((0, rp - r), (0, 0)))
        parts.append(a2)
        meta.append((off, r, a.shape))
        off += rp
    return jnp.concatenate(parts, axis=0), meta


def _unpack(buf, meta, lead=()):
    return [buf[..., off:off + r, :].reshape(*lead, *shape) for off, r, shape in meta]


_MAIN = ("up_t0", "up_t1", "down0", "down1", "sc_in_t", "gla_out", "sc_out")
_MAIN_ROWS = {"sc_in_t": 3 * D // N_CHIPS, "up_t0": 2 * HID // N_CHIPS, "up_t1": 2 * HID // N_CHIPS,
              "gla_out": D // N_CHIPS, "sc_out": D // N_CHIPS, "down0": HID // N_CHIPS, "down1": HID // N_CHIPS}
_MAIN_TOTAL = sum(_MAIN_ROWS.values())
_GIN_ROWS = GLA_IN // N_CHIPS
_GIN_PAD = -(-_GIN_ROWS // 32) * 32


def _offsets(names, rows):
    off, out = 0, {}
    for k in names:
        out[k] = off
        off += rows[k]
    return out


def kernel(x, c, ctx, c_ctx, ada_w, ada_b, norm_mix, norm_ffn, gla_w_in, gla_w_a2, gla_b_a, gla_head_norm, gla_w_out, sc_w_in, sc_conv_w, sc_w_out, ffn_w_up, ffn_conv_w, ffn_conv_b, ffn_w_down, final_norm, loss_target, m_c_ctx, m_ada_w, m_ada_b, m_norm_mix, m_norm_ffn, m_gla_w_in, m_gla_w_a2, m_gla_b_a, m_gla_head_norm, m_gla_w_out, m_sc_w_in, m_sc_conv_w, m_sc_w_out, m_ffn_w_up, m_ffn_conv_w, m_ffn_conv_b, m_ffn_w_down, m_final_norm, v_c_ctx, v_ada_w, v_ada_b, v_norm_mix, v_norm_ffn, v_gla_w_in, v_gla_w_a2, v_gla_b_a, v_gla_head_norm, v_gla_w_out, v_sc_w_in, v_sc_conv_w, v_sc_w_out, v_ffn_w_up, v_ffn_conv_w, v_ffn_conv_b, v_ffn_w_down, v_final_norm):
    ix, iy, ic = _place()
    chip = 2 * ix + iy
    dev = 2 * chip + ic
    place = jnp.stack([ic, chip]).astype(jnp.int32)
    nb = x.shape[0]
    offs = _offsets(_MAIN, _MAIN_ROWS)

    buf, meta = _pack([c, ffn_conv_w, sc_conv_w, gla_w_a2, gla_b_a])
    got = _allgather_small(buf, "gather_small_in").reshape(N_DEV, buf.shape[0], 128)
    c_all, fcw, scw, wa2, ba = _unpack(got, meta, (N_DEV,))
    c_all = c_all.reshape(N_DEV * nb, D)
    per_chip = lambda a: a[0::2]
    ffn_conv_w_full = jnp.moveaxis(per_chip(fcw), 0, 2).reshape(2, 3, 2 * HID)
    sc_conv_w_full = jnp.moveaxis(per_chip(scw)[:, 0], 0, 1).reshape(3, D)
    gla_w_a2_full = jnp.moveaxis(per_chip(wa2)[:, 0], 0, 2).reshape(2, RANK, KEY)
    gla_b_a_full = jnp.moveaxis(per_chip(ba)[:, 0], 0, 1).reshape(2, KEY)

    cvec = jnp.concatenate([c_all, c_ctx.reshape(1, D), jnp.zeros((ADA_ROWS - N_DEV * nb - 1, D), F32)], axis=0)
    ada_b_sh = lax.dynamic_slice_in_dim(ada_b, chip * ADA_SH, ADA_SH, axis=1).reshape(2, 1, ADA_SH)
    mod_sh = _ada_fwd(cvec, ada_w, ada_b_sh)
    got = _allgather_small(mod_sh.reshape(2 * ADA_ROWS, ADA_SH), "gather_mod")
    mod_full = jnp.moveaxis(per_chip(got.reshape(N_DEV, 2, ADA_ROWS, ADA_SH)), 0, 2).reshape(2, ADA_ROWS, N_MOD * D)
    mc = mod_full[0, N_DEV * nb, :2 * D].reshape(2, D)

    own = {"sc_in_t": sc_w_in[0].T, "up_t0": ffn_w_up[0].T, "up_t1": ffn_w_up[1].T,
           "gla_out": gla_w_out[0], "sc_out": sc_w_out[0], "down0": ffn_w_down[0], "down1": ffn_w_down[1]}
    own_main = jnp.concatenate([own[k].astype(BF16) for k in _MAIN], axis=0)
    own_gin = jnp.pad(gla_w_in[0].T.astype(BF16), ((0, _GIN_PAD - _GIN_ROWS), (0, 0)))
    ag_gin = _split_start(own_gin, (N_CHIPS, _GIN_PAD, D), _ag_copies, 4, mc, "ag_gin_start")
    ag_main = _split_start(own_main, (N_CHIPS, _MAIN_TOTAL, D), _ag_copies, 4, ag_gin[4], "ag_main_start")
    mods = lax.dynamic_slice_in_dim(mod_full, dev * nb, nb, axis=1).reshape(2, nb, N_MOD, D) + ag_main[4][0, 0]

    small = {"norm_mix": norm_mix, "norm_ffn": norm_ffn, "final_norm": final_norm, "gla_w_a2": gla_w_a2_full,
             "gla_b_a": gla_b_a_full, "gla_head_norm": gla_head_norm[0], "sc_conv_w": sc_conv_w_full,
             "ffn_conv_w": ffn_conv_w_full, "ffn_conv_b": ffn_conv_b}
    loss_p, grad_x, g_main, p16_gin, p32_gin, gs, dmods, dmc = _local_step(x, ctx, loss_target, mods, mc, ag_gin,
                                                                           ag_main, place, small)

    sum_names = ["norm_mix", "norm_ffn", "final_norm", "gla_w_a2", "gla_b_a", "gla_head_norm", "sc_conv_w",
                 "ffn_conv_w", "ffn_conv_b"]
    buf, meta = _pack([jnp.broadcast_to(loss_p, (8, 128))] + [gs[k] for k in sum_names] + [dmc, dmods])
    n_sum = meta[-1][0]
    got = _allgather_small(buf, "gather_small_grads").reshape(N_DEV, buf.shape[0], 128)
    summed = _sum_slots(got[:, :n_sum], "sum_small_grads")
    parts = _unpack(summed, meta[:-1])
    loss = parts[0][0, 0]
    g_small = dict(zip(sum_names, parts[1:-1]))
    dmc_tot = parts[-1]
    dmods_all = jnp.moveaxis(_unpack(got, meta[-1:], (N_DEV,))[0], 0, 1).reshape(2, N_DEV * nb, N_MOD * D)

    ctx_row = jnp.stack([jnp.concatenate([dmc_tot.reshape(2 * D), jnp.zeros(((N_MOD - 2) * D,), F32)]),
                         jnp.zeros((N_MOD * D,), F32)]).reshape(2, 1, N_MOD * D)
    dmod_ext = jnp.concatenate([dmods_all, ctx_row, jnp.zeros((2, ADA_ROWS - N_DEV * nb - 1, N_MOD * D), F32)], axis=1)
    g_ada_b = _sum_slots(jnp.moveaxis(dmod_ext, 1, 0).reshape(ADA_ROWS, 2 * N_MOD * D // 128, 128),
                         "sum_ada_b").reshape(2, N_MOD * D)
    dmod_sh = lax.dynamic_slice_in_dim(dmod_ext, chip * ADA_SH, ADA_SH, axis=2)
    g_ada_w, dcv = _ada_bwd(cvec, ada_w, dmod_sh)
    dscc_part = (dcv[0, N_DEV * nb] + dcv[1, N_DEV * nb]).reshape(8, 128)
    got = _allgather_small(dscc_part, "gather_dscc").reshape(N_DEV, 8, 128)
    g_c_ctx = _cctx_grad(per_chip(got), c_ctx.reshape(8, 128)).reshape(D)

    sc_gin = _split_start(p16_gin, p16_gin.shape, _sc_copies, 3, g_c_ctx, "rs_gin_scatter_start")
    seg = {k: g_main[offs[k]:offs[k] + _MAIN_ROWS[k]] for k in _MAIN}

    sl_chip = lambda a, axis, width: lax.dynamic_slice_in_dim(a, chip * width, width, axis=axis)
    grads = {
        "c_ctx": g_c_ctx, "ada_w": g_ada_w, "ada_b": g_ada_b, "norm_mix": g_small["norm_mix"],
        "norm_ffn": g_small["norm_ffn"],
        "gla_w_a2": sl_chip(g_small["gla_w_a2"], 2, KEY // N_CHIPS)[None],
        "gla_b_a": sl_chip(g_small["gla_b_a"], 1, KEY // N_CHIPS)[None],
        "gla_head_norm": g_small["gla_head_norm"][None], "gla_w_out": seg["gla_out"][None],
        "sc_w_in": seg["sc_in_t"].T[None], "sc_conv_w": sl_chip(g_small["sc_conv_w"], 1, D // N_CHIPS)[None],
        "sc_w_out": seg["sc_out"][None], "ffn_w_up": jnp.stack([seg["up_t0"].T, seg["up_t1"].T]),
        "ffn_conv_w": sl_chip(g_small["ffn_conv_w"], 2, 2 * HID // N_CHIPS), "ffn_conv_b": g_small["ffn_conv_b"],
        "ffn_w_down": jnp.stack([seg["down0"], seg["down1"]]), "final_norm": g_small["final_norm"],
    }
    weights = {"c_ctx": c_ctx, "ada_w": ada_w, "ada_b": ada_b, "norm_mix": norm_mix, "norm_ffn": norm_ffn,
               "gla_w_in": gla_w_in, "gla_w_a2": gla_w_a2, "gla_b_a": gla_b_a, "gla_head_norm": gla_head_norm,
               "gla_w_out": gla_w_out, "sc_w_in": sc_w_in, "sc_conv_w": sc_conv_w, "sc_w_out": sc_w_out,
               "ffn_w_up": ffn_w_up, "ffn_conv_w": ffn_conv_w, "ffn_conv_b": ffn_conv_b, "ffn_w_down": ffn_w_down,
               "final_norm": final_norm}
    mom1 = {"c_ctx": m_c_ctx, "ada_w": m_ada_w, "ada_b": m_ada_b, "norm_mix": m_norm_mix, "norm_ffn": m_norm_ffn,
            "gla_w_in": m_gla_w_in, "gla_w_a2": m_gla_w_a2, "gla_b_a": m_gla_b_a, "gla_head_norm": m_gla_head_norm,
            "gla_w_out": m_gla_w_out, "sc_w_in": m_sc_w_in, "sc_conv_w": m_sc_conv_w, "sc_w_out": m_sc_w_out,
            "ffn_w_up": m_ffn_w_up, "ffn_conv_w": m_ffn_conv_w, "ffn_conv_b": m_ffn_conv_b,
            "ffn_w_down": m_ffn_w_down, "final_norm": m_final_norm}
    mom2 = {"c_ctx": v_c_ctx, "ada_w": v_ada_w, "ada_b": v_ada_b, "norm_mix": v_norm_mix, "norm_ffn": v_norm_ffn,
            "gla_w_in": v_gla_w_in, "gla_w_a2": v_gla_w_a2, "gla_b_a": v_gla_b_a, "gla_head_norm": v_gla_head_norm,
            "gla_w_out": v_gla_w_out, "sc_w_in": v_sc_w_in, "sc_conv_w": v_sc_conv_w, "sc_w_out": v_sc_w_out,
            "ffn_w_up": v_ffn_w_up, "ffn_conv_w": v_ffn_conv_w, "ffn_conv_b": v_ffn_conv_b,
            "ffn_w_down": v_ffn_w_down, "final_norm": v_final_norm}
    names = list(weights)

    big_names = ["ada_w", "gla_w_out", "sc_w_in", "sc_w_out", "ffn_w_up", "ffn_w_down", "gla_w_in"]
    small_names = [k for k in names if k not in big_names]
    delta, new_m, new_v = {}, {}, {}
    done = []

    def big_adamw(k, token):
        delta[k], new_m[k], new_v[k] = _adamw(weights[k], grads[k], mom1[k], mom2[k], "adamw_" + k, token)
        done.append(new_v[k][0, 0:1, 0:128])

    for k in big_names[:-1]:
        grads[k] = grads[k].reshape(weights[k].shape)
        big_adamw(k, sc_gin[4])
    for k in small_names:
        grads[k] = grads[k].reshape(weights[k].shape)
    packed = [_pack([src[k] for k in small_names]) for src in (weights, grads, mom1, mom2)]
    meta = packed[0][1]
    rows_pad = -packed[0][0].shape[0] % 128
    bufs = [jnp.pad(p[0], ((0, rows_pad), (0, 0)))[None] for p in packed]
    outs = _adamw(bufs[0], bufs[1], bufs[2], bufs[3], "adamw_small", sc_gin[4])
    done.append(outs[2][0, 0:1, :])
    for dst, o in zip((delta, new_m, new_v), outs):
        for k, a in zip(small_names, _unpack(o[0], meta)):
            dst[k] = a
    landed = _split_wait(sc_gin, jnp.concatenate(done, axis=0), _sc_copies, "rs_gin_scatter_wait")[1]
    g_gin_shard = _rs_pair_gather(_rs_final_sum(place, landed, p32_gin, "rs_gin_final_sum"), "rs_gin_pair_gather")
    grads["gla_w_in"] = g_gin_shard[:_GIN_ROWS].T[None]
    big_adamw("gla_w_in", sc_gin[4])

    return (loss, grad_x, *[grads[k] for k in names], *[delta[k] for k in names], *[new_m[k] for k in names],
            *[new_v[k] for k in names])
```

```python
import functools

import jax
import jax.numpy as jnp
from jax import lax
from jax.experimental import pallas as pl
from jax.experimental.pallas import tpu as pltpu

F32 = jnp.float32
BF16 = jnp.bfloat16
MESH = pl.DeviceIdType.MESH

EPS = 1e-6
D = 1024
N_MOD = 6
HEADS = 4
DK = 128
DV = 256
KEY = HEADS * DK
RANK = 16
TAU = 16.0
CH = 64
GRID_W = 64
HID = 2560
GLA_IN = 2 * KEY + 2 * D + 2 * RANK
GLA_IN_PAD = 3200
Q_SCALE = DK ** -0.5
N_CHIPS = 4
N_DEV = 8

ADAM_LR = 0.001
ADAM_B1 = 0.9
ADAM_B2 = 0.999
ADAM_EPS = 1e-08
ADAM_WD = 0.01
ADAM_STEP = 10

VMEM_LIMIT = 56 * 1024 * 1024


def _params(sem):
    return pltpu.CompilerParams(dimension_semantics=sem, vmem_limit_bytes=VMEM_LIMIT)


def _tile(n, pref, mult=8):
    if n <= pref:
        return n
    for t in range(pref - pref % mult, 0, -mult):
        if n % t == 0:
            return t
    raise ValueError((n, pref, mult))


_NN = (((1,), (0,)), ((), ()))
_NT = (((1,), (1,)), ((), ()))
_TN = (((0,), (0,)), ((), ()))


def _dot(a, b, dims=_NN):
    return lax.dot_general(a.astype(BF16), b.astype(BF16), dims, preferred_element_type=F32)


def _sigmoid(x):
    return 1.0 / (1.0 + jnp.exp(-x))


def _rowsum(x):
    return jnp.sum(x, axis=0, keepdims=True)


def _mm(a, b, form, out_dtype, name, tm, tn):
    if form == "tn":
        K, M = a.shape
    else:
        M, K = a.shape
    N = b.shape[0] if form == "nt" else b.shape[1]
    tm = _tile(M, tm, 128)
    tn = _tile(N, tn, 128)
    dims = {"nn": _NN, "nt": _NT, "tn": _TN}[form]

    def body(a_ref, b_ref, o_ref):
        o_ref[...] = _dot(a_ref[...], b_ref[...], dims).astype(o_ref.dtype)

    if form == "tn":
        a_spec = pl.BlockSpec((K, tm), lambda i, j: (0, i))
    else:
        a_spec = pl.BlockSpec((tm, K), lambda i, j: (i, 0))
    if form == "nt":
        b_spec = pl.BlockSpec((tn, K), lambda i, j: (j, 0))
    else:
        b_spec = pl.BlockSpec((K, tn), lambda i, j: (0, j))
    return pl.pallas_call(
        body,
        name=name,
        grid=(M // tm, N // tn),
        in_specs=[a_spec, b_spec],
        out_specs=pl.BlockSpec((tm, tn), lambda i, j: (i, j)),
        out_shape=jax.ShapeDtypeStruct((M, N), out_dtype),
        compiler_params=_params(("parallel", "parallel")),
    )(a, b)


def _mm_nt_w(a, wg, off, rows, name, tm, out_dtype):
    m = a.shape[0]
    tm = _tile(m, tm, 128)
    if N_CHIPS * rows <= D:

        def body_small(a_ref, w_ref, o_ref):
            av = a_ref[...]
            for s in range(N_CHIPS):
                o_ref[:, s * rows:(s + 1) * rows] = _dot(av, w_ref[s], _NT).astype(o_ref.dtype)

        return pl.pallas_call(
            body_small, name=name, grid=(m // tm,),
            in_specs=[pl.BlockSpec((tm, D), lambda i: (i, 0)),
                      pl.BlockSpec((N_CHIPS, rows, D), lambda i: (0, off // rows, 0))],
            out_specs=pl.BlockSpec((tm, N_CHIPS * rows), lambda i: (i, 0)),
            out_shape=jax.ShapeDtypeStruct((m, N_CHIPS * rows), out_dtype),
            compiler_params=_params(("parallel",)),
        )(a, wg)

    def body(a_ref, w_ref, o_ref):
        o_ref[...] = _dot(a_ref[...], w_ref[0], _NT).astype(o_ref.dtype)

    return pl.pallas_call(
        body, name=name, grid=(m // tm, N_CHIPS),
        in_specs=[pl.BlockSpec((tm, D), lambda i, s: (i, 0)),
                  pl.BlockSpec((1, rows, D), lambda i, s: (s, off // rows, 0))],
        out_specs=pl.BlockSpec((tm, rows), lambda i, s: (i, s)),
        out_shape=jax.ShapeDtypeStruct((m, N_CHIPS * rows), out_dtype),
        compiler_params=_params(("parallel", "parallel")),
    )(a, wg)


def _mm_nn_w(a3, wg, off, rows, name, tm, tn):
    parts, m, kp = a3.shape
    assert parts * kp == N_CHIPS * rows
    tm = _tile(m, tm, 128)
    cuts = sorted({s * rows for s in range(N_CHIPS + 1)} | {p * kp for p in range(parts + 1)})
    pieces = [(k0 // kp, k0 % kp, k0 // rows, k0 % rows, k1 - k0) for k0, k1 in zip(cuts[:-1], cuts[1:])]

    def body(a_ref, w_ref, o_ref):
        acc = None
        for p, a0, s, r0, width in pieces:
            term = _dot(a_ref[p, :, a0:a0 + width], w_ref[s, r0:r0 + width, :])
            acc = term if acc is None else acc + term
        o_ref[...] = acc

    return pl.pallas_call(
        body, name=name, grid=(m // tm, D // tn),
        in_specs=[pl.BlockSpec((parts, tm, kp), lambda i, j: (0, i, 0)),
                  pl.BlockSpec((N_CHIPS, rows, tn), lambda i, j: (0, off // rows, j))],
        out_specs=pl.BlockSpec((tm, tn), lambda i, j: (i, j)),
        out_shape=jax.ShapeDtypeStruct((m, D), F32),
        compiler_params=_params(("parallel", "parallel")),
    )(a3, wg)


def _mm_nn_w_mod(a3, wg, off, rows, h, gate, gain, shift, scale, tpb_rows, name, tm):
    parts, m, kp = a3.shape
    assert parts * kp == N_CHIPS * rows
    tm = _tile(tpb_rows, tm, 128)
    tpb = tpb_rows // tm
    cuts = sorted({s * rows for s in range(N_CHIPS + 1)} | {p * kp for p in range(parts + 1)})
    pieces = [(k0 // kp, k0 % kp, k0 // rows, k0 % rows, k1 - k0) for k0, k1 in zip(cuts[:-1], cuts[1:])]

    def body(a_ref, w_ref, h_ref, gate_ref, gain_ref, sh_ref, sc_ref, y_ref, hout_ref, hn_ref):
        acc = None
        for p, a0, s, r0, width in pieces:
            term = _dot(a_ref[p, :, a0:a0 + width], w_ref[s, r0:r0 + width, :])
            acc = term if acc is None else acc + term
        y_ref[...] = acc
        hv = h_ref[...] + gate_ref[0] * acc
        hout_ref[...] = hv
        r = lax.rsqrt(jnp.mean(hv * hv, axis=-1, keepdims=True) + EPS)
        hn_ref[...] = ((hv * r) * gain_ref[...] * (1.0 + sc_ref[0]) + sh_ref[0]).astype(BF16)

    row = pl.BlockSpec((tm, D), lambda i: (i, 0))
    per_b = pl.BlockSpec((1, 1, D), lambda i: (i // tpb, 0, 0))
    return pl.pallas_call(
        body, name=name, grid=(m // tm,),
        in_specs=[pl.BlockSpec((parts, tm, kp), lambda i: (0, i, 0)),
                  pl.BlockSpec((N_CHIPS, rows, D), lambda i: (0, off // rows, 0)),
                  row, per_b, pl.BlockSpec((1, D), lambda i: (0, 0)), per_b, per_b],
        out_specs=[row, row, row],
        out_shape=[jax.ShapeDtypeStruct((m, D), F32), jax.ShapeDtypeStruct((m, D), F32),
                   jax.ShapeDtypeStruct((m, D), BF16)],
        compiler_params=_params(("parallel",)),
    )(a3, wg, h, gate, gain, shift, scale)


def _mm_nn_w_modbwd(a3, wg, off, rows, h_in, dh_out, gain, scale, y_prev, gate_prev, tpb_rows, name, tm):
    parts, m, kp = a3.shape
    assert parts * kp == N_CHIPS * rows
    nb = m // tpb_rows
    tm = _tile(tpb_rows, tm, 128)
    tpb = tpb_rows // tm
    cuts = sorted({s * rows for s in range(N_CHIPS + 1)} | {p * kp for p in range(parts + 1)})
    pieces = [(k0 // kp, k0 % kp, k0 // rows, k0 % rows, k1 - k0) for k0, k1 in zip(cuts[:-1], cuts[1:])]

    def body(a_ref, w_ref, h_ref, gain_ref, sc_ref, dho_ref, yp_ref, gp_ref,
             dh_ref, dsc_ref, dsh_ref, dgain_ref, dyp_ref, dgp_ref):
        i = pl.program_id(0)

        @pl.when(i == 0)
        def _():
            dgain_ref[...] = jnp.zeros_like(dgain_ref)

        @pl.when(i % tpb == 0)
        def _():
            dsc_ref[...] = jnp.zeros_like(dsc_ref)
            dsh_ref[...] = jnp.zeros_like(dsh_ref)
            dgp_ref[...] = jnp.zeros_like(dgp_ref)

        g = None
        for p, a0, s, r0, width in pieces:
            term = _dot(a_ref[p, :, a0:a0 + width], w_ref[s, r0:r0 + width, :])
            g = term if g is None else g + term
        hv = h_ref[...]
        r = lax.rsqrt(jnp.mean(hv * hv, axis=-1, keepdims=True) + EPS)
        y = hv * r
        gain_v = gain_ref[...]
        dsh_ref[0] += _rowsum(g)
        dsc_ref[0] += _rowsum(g * (y * gain_v))
        drn = g * (1.0 + sc_ref[0])
        dgain_ref[...] += _rowsum(drn * y)
        dy = drn * gain_v
        dh = r * (dy - y * jnp.mean(dy * y, axis=-1, keepdims=True)) + dho_ref[...]
        dh_ref[...] = dh
        dyp_ref[...] = (dh * gp_ref[0]).astype(BF16)
        dgp_ref[0] += _rowsum(dh * yp_ref[...])

    row = pl.BlockSpec((tm, D), lambda i: (i, 0))
    per_b = pl.BlockSpec((1, 1, D), lambda i: (i // tpb, 0, 0))
    vec = pl.BlockSpec((1, D), lambda i: (0, 0))
    per_b_shape = jax.ShapeDtypeStruct((nb, 1, D), F32)
    outs = pl.pallas_call(
        body, name=name, grid=(m // tm,),
        in_specs=[pl.BlockSpec((parts, tm, kp), lambda i: (0, i, 0)),
                  pl.BlockSpec((N_CHIPS, rows, D), lambda i: (0, off // rows, 0)),
                  row, vec, per_b, row, row, per_b],
        out_specs=[row, per_b, per_b, vec, row, per_b],
        out_shape=[jax.ShapeDtypeStruct((m, D), F32), per_b_shape, per_b_shape, jax.ShapeDtypeStruct((1, D), F32),
                   jax.ShapeDtypeStruct((m, D), BF16), per_b_shape],
        compiler_params=_params(("arbitrary",)),
    )(a3, wg, h_in, gain, scale, dh_out, y_prev, gate_prev)
    return dict(zip(("dh", "dscale", "dshift", "dgain", "dy_prev", "dgate_prev"), outs))


def _mm_dw(a3, b, g_prev, off, rows, name, tm):
    parts, ntok, cdim = a3.shape
    assert parts * cdim == N_CHIPS * rows and cdim % tm == 0 and rows % tm == 0 and off % tm == 0

    def body(a_ref, b_ref, *rest):
        rest[-1][0] = _dot(a_ref[0], b_ref[...], _TN)

    in_specs = [pl.BlockSpec((1, ntok, tm), lambda i: ((i * tm) // cdim, 0, ((i * tm) % cdim) // tm)),
                pl.BlockSpec((ntok, D), lambda i: (0, 0))]
    args = [a3, b]
    aliases = {}
    if g_prev is not None:
        in_specs.append(pl.BlockSpec(memory_space=pl.ANY))
        args.append(g_prev)
        aliases = {2: 0}
    return pl.pallas_call(
        body, name=name, grid=(N_CHIPS * rows // tm,),
        in_specs=in_specs,
        out_specs=pl.BlockSpec((1, tm, D), lambda i: ((i * tm) // rows, (off + (i * tm) % rows) // tm, 0)),
        out_shape=jax.ShapeDtypeStruct((N_CHIPS, _MAIN_TOTAL, D), F32),
        input_output_aliases=aliases,
        compiler_params=_params(("parallel",)),
    )(*args)


def _mod_fwd(h, gain, shift, scale, tpb_rows, name, y=None, gate=None, out_rows=None, into=None, row0=0):
    n = h.shape[0]
    tt = _tile(tpb_rows, 256)
    tpb = tpb_rows // tt
    has_res = y is not None
    assert row0 % tt == 0 and not (has_res and out_rows)

    def body(*refs):
        if has_res:
            h_ref, y_ref, gate_ref, gain_ref, sh_ref, sc_ref, hout_ref, hn_ref = refs
            hv = h_ref[...] + gate_ref[0] * y_ref[...]
            hout_ref[...] = hv
        else:
            h_ref, gain_ref, sh_ref, sc_ref, hn_ref = refs[0], refs[1], refs[2], refs[3], refs[-1]
            hv = h_ref[...]
        r = lax.rsqrt(jnp.mean(hv * hv, axis=-1, keepdims=True) + EPS)
        hn = (hv * r) * gain_ref[...] * (1.0 + sc_ref[0]) + sh_ref[0]
        hn_ref[...] = hn.astype(BF16)

    row = pl.BlockSpec((tt, D), lambda i: (i, 0))
    per_b = pl.BlockSpec((1, 1, D), lambda i: (i // tpb, 0, 0))
    vec = pl.BlockSpec((1, D), lambda i: (0, 0))
    if has_res:
        in_specs = [row, row, per_b, vec, per_b, per_b]
        args = (h, y, gate, gain, shift, scale)
        out_specs = [row, row]
        out_shape = [jax.ShapeDtypeStruct((n, D), F32), jax.ShapeDtypeStruct((n, D), BF16)]
    else:
        in_specs = [row, vec, per_b, per_b]
        args = (h, gain, shift, scale)
        out_specs = pl.BlockSpec((tt, D), lambda i: (i + row0 // tt, 0))
        out_shape = jax.ShapeDtypeStruct((out_rows or n, D), BF16)
    aliases = {}
    if into is not None:
        in_specs = in_specs + [pl.BlockSpec(memory_space=pl.ANY)]
        args = args + (into,)
        aliases = {4: 0}
    return pl.pallas_call(
        body, name=name, grid=(n // tt,), in_specs=in_specs, out_specs=out_specs, out_shape=out_shape,
        input_output_aliases=aliases, compiler_params=_params(("parallel",)),
    )(*args)


def _mod_bwd(h_in, dhn, gain, scale, tpb_rows, name, dhn_row0=0, dh_out=None, y_prev=None, gate_prev=None,
             need_dh=True):
    n = h_in.shape[0]
    nb = n // tpb_rows
    tt = _tile(tpb_rows, 256)
    tpb = tpb_rows // tt
    off = dhn_row0 // tt
    assert dhn_row0 % tt == 0
    has_out = dh_out is not None
    has_prev = y_prev is not None

    def body(*refs):
        it = iter(refs)
        h_ref, dhn_ref, gain_ref, sc_ref = next(it), next(it), next(it), next(it)
        dho_ref = next(it) if has_out else None
        yp_ref, gp_ref = (next(it), next(it)) if has_prev else (None, None)
        dh_ref = next(it) if need_dh else None
        dsc_ref, dsh_ref, dgain_ref = next(it), next(it), next(it)
        dyp_ref, dgp_ref = (next(it), next(it)) if has_prev else (None, None)
        i = pl.program_id(0)

        @pl.when(i == 0)
        def _():
            dgain_ref[...] = jnp.zeros_like(dgain_ref)

        @pl.when(i % tpb == 0)
        def _():
            dsc_ref[...] = jnp.zeros_like(dsc_ref)
            dsh_ref[...] = jnp.zeros_like(dsh_ref)
            if has_prev:
                dgp_ref[...] = jnp.zeros_like(dgp_ref)

        hv = h_ref[...]
        r = lax.rsqrt(jnp.mean(hv * hv, axis=-1, keepdims=True) + EPS)
        y = hv * r
        gain_v = gain_ref[...]
        g = dhn_ref[...].astype(F32)
        dsh_ref[0] += _rowsum(g)
        dsc_ref[0] += _rowsum(g * (y * gain_v))
        drn = g * (1.0 + sc_ref[0])
        dgain_ref[...] += _rowsum(drn * y)
        if need_dh:
            dy = drn * gain_v
            dh = r * (dy - y * jnp.mean(dy * y, axis=-1, keepdims=True))
            if has_out:
                dh = dh + dho_ref[...]
            dh_ref[...] = dh
            if has_prev:
                dyp_ref[...] = (dh * gp_ref[0]).astype(BF16)
                dgp_ref[0] += _rowsum(dh * yp_ref[...])

    row = pl.BlockSpec((tt, D), lambda i: (i, 0))
    row_off = pl.BlockSpec((tt, D), lambda i: (i + off, 0))
    per_b = pl.BlockSpec((1, 1, D), lambda i: (i // tpb, 0, 0))
    vec = pl.BlockSpec((1, D), lambda i: (0, 0))
    in_specs = [row, row_off, vec, per_b]
    args = [h_in, dhn, gain, scale]
    if has_out:
        in_specs.append(row)
        args.append(dh_out)
    if has_prev:
        in_specs += [row, per_b]
        args += [y_prev, gate_prev]
    out_specs, out_shape, names = [], [], []
    if need_dh:
        out_specs.append(row)
        out_shape.append(jax.ShapeDtypeStruct((n, D), F32))
        names.append("dh")
    for nm in ("dscale", "dshift"):
        out_specs.append(per_b)
        out_shape.append(jax.ShapeDtypeStruct((nb, 1, D), F32))
        names.append(nm)
    out_specs.append(vec)
    out_shape.append(jax.ShapeDtypeStruct((1, D), F32))
    names.append("dgain")
    if has_prev:
        out_specs += [row, per_b]
        out_shape += [jax.ShapeDtypeStruct((n, D), BF16), jax.ShapeDtypeStruct((nb, 1, D), F32)]
        names += ["dy_prev", "dgate_prev"]
    outs = pl.pallas_call(
        body, name=name, grid=(n // tt,), in_specs=in_specs, out_specs=out_specs, out_shape=out_shape,
        compiler_params=_params(("arbitrary",)),
    )(*args)
    return dict(zip(names, outs))


def _final(h, f, gate, gain, tgt, tpb_rows):
    n = h.shape[0]
    nb = n // tpb_rows
    tt = _tile(tpb_rows, 256)
    tpb = tpb_rows // tt

    def body(h_ref, f_ref, gate_ref, gain_ref, tgt_ref, loss_ref, dh_ref, df_ref, dgate_ref, dgain_ref):
        i = pl.program_id(0)

        @pl.when(i == 0)
        def _():
            loss_ref[...] = jnp.zeros_like(loss_ref)
            dgain_ref[...] = jnp.zeros_like(dgain_ref)

        @pl.when(i % tpb == 0)
        def _():
            dgate_ref[...] = jnp.zeros_like(dgate_ref)

        fv = f_ref[...]
        gate_v = gate_ref[0]
        hv = h_ref[...] + gate_v * fv
        r = lax.rsqrt(jnp.mean(hv * hv, axis=-1, keepdims=True) + EPS)
        y = hv * r
        gain_v = gain_ref[...]
        e = y * gain_v - tgt_ref[...]
        s = jnp.sum(_rowsum(e * e), axis=1, keepdims=True) * (0.5 / D)
        loss_ref[...] += jnp.broadcast_to(s, loss_ref.shape)
        dout = e * (1.0 / D)
        dgain_ref[...] += _rowsum(dout * y)
        dy = dout * gain_v
        dh = r * (dy - y * jnp.mean(dy * y, axis=-1, keepdims=True))
        dh_ref[...] = dh
        df_ref[...] = (dh * gate_v).astype(BF16)
        dgate_ref[0] += _rowsum(dh * fv)

    row = pl.BlockSpec((tt, D), lambda i: (i, 0))
    per_b = pl.BlockSpec((1, 1, D), lambda i: (i // tpb, 0, 0))
    vec = pl.BlockSpec((1, D), lambda i: (0, 0))
    return pl.pallas_call(
        body, name="final_loss", grid=(n // tt,),
        in_specs=[row, row, per_b, vec, row],
        out_specs=[pl.BlockSpec((1, 128), lambda i: (0, 0)), row, row, per_b, vec],
        out_shape=[jax.ShapeDtypeStruct((1, 128), F32), jax.ShapeDtypeStruct((n, D), F32),
                   jax.ShapeDtypeStruct((n, D), BF16), jax.ShapeDtypeStruct((nb, 1, D), F32),
                   jax.ShapeDtypeStruct((1, D), F32)],
        compiler_params=_params(("arbitrary",)),
    )(h, f, gate, gain, tgt)


def _row_dn1(x):
    t = lax.broadcasted_iota(jnp.int32, x.shape, 0)
    return jnp.where(t % GRID_W == 0, 0.0, pltpu.roll(x, 1, 0))


def _row_up1(x):
    t = lax.broadcasted_iota(jnp.int32, x.shape, 0)
    return jnp.where(t % GRID_W == GRID_W - 1, 0.0, pltpu.roll(x, x.shape[0] - 1, 0))


def _silu(x):
    return x * _sigmoid(x)


def _dsilu(x):
    s = _sigmoid(x)
    return s * (1.0 + x * (1.0 - s))


def _row_ds(i):
    start = i * GRID_W
    return pl.ds(start if isinstance(start, int) else pl.multiple_of(start, GRID_W), GRID_W)


def _grid_row(ref, i, first, last):
    def rows(k):
        return ref[_row_ds(k), :].astype(F32)

    cur = rows(i)
    return (jnp.zeros_like(cur) if first else rows(i - 1)), cur, (jnp.zeros_like(cur) if last else rows(i + 1))


def _over_grid_rows(n_rows, step, carry):
    carry = step(0, carry, True, n_rows == 1)
    if n_rows > 2:
        carry = lax.fori_loop(1, n_rows - 1, lambda i, c: step(i, c, False, False), carry)
    if n_rows > 1:
        carry = step(n_rows - 1, carry, False, True)
    return carry


def _fold8(p):
    return p.reshape(GRID_W // 8, 8, p.shape[1]).sum(axis=0)


def _ffn_up_mid_fwd(hn, wg, off, cw, cb, nb, t, name):
    tcol = 256
    ncol = HID // tcol
    rows_sh = 2 * HID // N_CHIPS

    def conv(x, w_ref):
        zeros = jnp.zeros((GRID_W, x.shape[1]), x.dtype)
        down = jnp.concatenate([zeros, x[: x.shape[0] - GRID_W]], axis=0)
        up = jnp.concatenate([x[GRID_W:], zeros], axis=0)
        return down * w_ref[0:1, :] + x * w_ref[1:2, :] + up * w_ref[2:3, :]

    def body(h_ref, wa_ref, wg_ref, cwa_ref, cwg_ref, cba_ref, cbg_ref, u_ref, z_ref):
        hv = h_ref[...]
        ua = _dot(hv, wa_ref[0], _NT)
        ug = _dot(hv, wg_ref[0], _NT)
        u_ref[0] = ua.astype(BF16)
        u_ref[1] = ug.astype(BF16)
        a = conv(ua, cwa_ref) + cba_ref[...]
        gt = conv(ug, cwg_ref) + cbg_ref[...]
        z_ref[...] = (a * _silu(gt)).astype(BF16)

    def w_spec(part):
        def idx(b, j):
            n = part * HID + j * tcol
            return (n // rows_sh, (off + n % rows_sh) // tcol, 0)
        return pl.BlockSpec((1, tcol, D), idx)

    chan = lambda rows, part: pl.BlockSpec((rows, tcol), lambda b, j: (0, part * ncol + j))
    return pl.pallas_call(
        body, name=name, grid=(nb, ncol),
        in_specs=[pl.BlockSpec((t, D), lambda b, j: (b, 0)), w_spec(0), w_spec(1),
                  chan(3, 0), chan(3, 1), chan(1, 0), chan(1, 1)],
        out_specs=[pl.BlockSpec((2, t, tcol), lambda b, j: (0, b, j)), pl.BlockSpec((t, tcol), lambda b, j: (b, j))],
        out_shape=[jax.ShapeDtypeStruct((2, nb * t, HID), BF16), jax.ShapeDtypeStruct((nb * t, HID), BF16)],
        compiler_params=_params(("parallel", "parallel")),
    )(hn, wg, wg, cw, cw, cb, cb)


def _ffn_mid_bwd(u0, cw, cb, dz, nb, t, name):
    nc = HID // 128
    n_rows = t // GRID_W

    def body(ua3_ref, ug3_ref, wa_ref, wg_ref, ba_ref, bg_ref, dz_ref, du_ref, dw_ref, db_ref, dua_ref, dug_ref):
        ua_ref, ug_ref = ua3_ref.at[0], ug3_ref.at[0]
        b = pl.program_id(1)

        @pl.when(b == 0)
        def _():
            dw_ref[...] = jnp.zeros_like(dw_ref)
            db_ref[...] = jnp.zeros_like(db_ref)

        wa = [wa_ref[k:k + 1, :] for k in range(3)]
        wg = [wg_ref[k:k + 1, :] for k in range(3)]
        ba, bg = ba_ref[...], bg_ref[...]

        def pass1(i, acc, first, last):
            here = _row_ds(i)
            ap, ac, an = _grid_row(ua_ref, i, first, last)
            gp, gc, gn = _grid_row(ug_ref, i, first, last)
            a = ap * wa[0] + ac * wa[1] + an * wa[2] + ba
            gt = gp * wg[0] + gc * wg[1] + gn * wg[2] + bg
            dzv = dz_ref[here, :].astype(F32)
            s = _sigmoid(gt)
            silu = gt * s
            da = dzv * silu
            dg = (dzv * a) * (s + silu * (1.0 - s))
            dua_ref[here, :] = da
            dug_ref[here, :] = dg
            terms = (da, da * ap, da * ac, da * an, dg, dg * gp, dg * gc, dg * gn)
            return tuple(r + _fold8(p) for r, p in zip(acc, terms))

        zero = jnp.zeros((8, 128), F32)
        acc = _over_grid_rows(n_rows, pass1, (zero,) * 8)
        for part in range(2):
            db_ref[part] += _rowsum(acc[4 * part])
            for k in range(3):
                dw_ref[part, k:k + 1, :] += _rowsum(acc[4 * part + 1 + k])

        def pass2(i, carry, first, last):
            for part, (ref, w) in enumerate(((dua_ref, wa), (dug_ref, wg))):
                dp_, dc_, dn_ = _grid_row(ref, i, first, last)
                du_ref[part, _row_ds(i), :] = (dn_ * w[0] + dc_ * w[1] + dp_ * w[2]).astype(BF16)
            return carry

        _over_grid_rows(n_rows, pass2, 0)

    col = lambda rows, part: pl.BlockSpec((rows, 128), lambda j, b: (0, part * nc + j))
    part_of_u = lambda part: pl.BlockSpec((1, t, 128), lambda j, b: (part, b, j))
    return pl.pallas_call(
        body, name=name, grid=(nc, nb),
        in_specs=[part_of_u(0), part_of_u(1), col(3, 0), col(3, 1), col(1, 0), col(1, 1),
                  pl.BlockSpec((t, 128), lambda j, b: (b, j))],
        out_specs=[pl.BlockSpec((2, t, 128), lambda j, b: (0, b, j)), pl.BlockSpec((2, 3, 128), lambda j, b: (0, 0, j)),
                   pl.BlockSpec((2, 1, 128), lambda j, b: (0, 0, j))],
        out_shape=[jax.ShapeDtypeStruct((2, nb * t, HID), BF16), jax.ShapeDtypeStruct((2, 3, HID), F32),
                   jax.ShapeDtypeStruct((2, 1, HID), F32)],
        scratch_shapes=[pltpu.VMEM((t, 128), F32), pltpu.VMEM((t, 128), F32)],
        compiler_params=_params(("parallel", "arbitrary")),
    )(u0, u0, cw, cw, cb, cb, dz)


def _sc_mid_fwd(p, cw, nb, t):
    nc = D // 128

    def body(bg_ref, cg_ref, v_ref, w_ref, y_ref):
        cv = cg_ref[...].astype(F32) * v_ref[...].astype(F32)
        cc = _row_dn1(cv) * w_ref[0:1, :] + cv * w_ref[1:2, :] + _row_up1(cv) * w_ref[2:3, :]
        y_ref[...] = (bg_ref[...].astype(F32) * cc).astype(BF16)

    part = lambda k: pl.BlockSpec((t, 128), lambda j, b: (b, k * nc + j))
    return pl.pallas_call(
        body, name="sc_mid_fwd", grid=(nc, nb),
        in_specs=[part(0), part(1), part(2), pl.BlockSpec((3, 128), lambda j, b: (0, j))],
        out_specs=pl.BlockSpec((t, 128), lambda j, b: (b, j)),
        out_shape=jax.ShapeDtypeStruct((nb * t, D), BF16),
        compiler_params=_params(("parallel", "parallel")),
    )(p, p, p, cw)


def _sc_mid_bwd(p, cw, dyb, nb, t):
    nc = D // 128

    def body(bg_ref, cg_ref, v_ref, w_ref, dy_ref, dp_ref, dw_ref):
        b = pl.program_id(1)

        @pl.when(b == 0)
        def _():
            dw_ref[...] = jnp.zeros_like(dw_ref)

        w0, w1, w2 = w_ref[0:1, :], w_ref[1:2, :], w_ref[2:3, :]
        cg, v = cg_ref[...].astype(F32), v_ref[...].astype(F32)
        cv = cg * v
        cvd = _row_dn1(cv)
        cvu = _row_up1(cv)
        cc = cvd * w0 + cv * w1 + cvu * w2
        dy = dy_ref[...].astype(F32)
        dcc = dy * bg_ref[...].astype(F32)
        dw_ref[0:1, :] += _rowsum(dcc * cvd)
        dw_ref[1:2, :] += _rowsum(dcc * cv)
        dw_ref[2:3, :] += _rowsum(dcc * cvu)
        dcv = _row_up1(dcc) * w0 + dcc * w1 + _row_dn1(dcc) * w2
        dp_ref[0] = (dy * cc).astype(BF16)
        dp_ref[1] = (dcv * v).astype(BF16)
        dp_ref[2] = (dcv * cg).astype(BF16)

    part = lambda k: pl.BlockSpec((t, 128), lambda j, b: (b, k * nc + j))
    return pl.pallas_call(
        body, name="sc_mid_bwd", grid=(nc, nb),
        in_specs=[part(0), part(1), part(2), pl.BlockSpec((3, 128), lambda j, b: (0, j)),
                  pl.BlockSpec((t, 128), lambda j, b: (b, j))],
        out_specs=[pl.BlockSpec((3, t, 128), lambda j, b: (0, b, j)), pl.BlockSpec((3, 128), lambda j, b: (0, j))],
        out_shape=[jax.ShapeDtypeStruct((3, nb * t, D), BF16), jax.ShapeDtypeStruct((3, D), F32)],
        compiler_params=_params(("parallel", "arbitrary")),
    )(p, p, p, cw, dyb)


def _gla_decay_fwd(p_all, w2, b2):
    n = p_all.shape[0]
    tt = _tile(n, 512)

    def body(a_ref, w_ref, b_ref, la_ref):
        z = _dot(a_ref[...], w_ref[...]) + b_ref[...]
        la_ref[...] = (jnp.minimum(z, 0.0) - jnp.log(1.0 + jnp.exp(-jnp.abs(z)))) * (1.0 / TAU)

    return pl.pallas_call(
        body, name="gla_decay_fwd", grid=(n // tt,),
        in_specs=[pl.BlockSpec((tt, 128), lambda i: (i, (2 * KEY + 2 * D) // 128)),
                  pl.BlockSpec((128, 2 * KEY), lambda i: (0, 0)), pl.BlockSpec((1, 2 * KEY), lambda i: (0, 0))],
        out_specs=pl.BlockSpec((tt, 2 * KEY), lambda i: (i, 0)),
        out_shape=jax.ShapeDtypeStruct((n, 2 * KEY), F32),
        compiler_params=_params(("parallel",)),
    )(p_all, w2, b2)


def _gla_blocks(nb, nm, ncx):
    def main_idx(d, i):
        return jnp.clip(jnp.where(d == 0, i - ncx, nm - 1 - (i - ncx)), 0, nm - 1)

    def rowblk(d, b, i):
        cidx = jnp.where(d == 0, i, ncx - 1 - i)
        return jnp.where(i < ncx, nb * nm + b * ncx + cidx, b * nm + main_idx(d, i))

    def mainblk(d, b, i):
        return b * nm + main_idx(d, i)

    return rowblk, mainblk


def _gla_mask(d):
    row = lax.broadcasted_iota(jnp.int32, (CH, CH), 0)
    col = lax.broadcasted_iota(jnp.int32, (CH, CH), 1)
    diff = jnp.where(d == 0, row - col, col - row)
    mask = diff >= 0
    return mask, jnp.where(mask, 1.0, 0.0).astype(BF16), jnp.where(diff <= 0, 1.0, 0.0).astype(BF16)


def _tri_sum(m01, x):
    w = x.shape[1]
    hi = x.astype(BF16)
    r1 = x - hi.astype(F32)
    mid = r1.astype(BF16)
    lo = (r1 - mid.astype(F32)).astype(BF16)
    s = lax.dot_general(m01, jnp.concatenate([hi, mid, lo], axis=1), _NN, preferred_element_type=F32)
    return s[:, :w] + s[:, w:2 * w] + s[:, 2 * w:]


def _gla_chunk(q, k, g, bc):
    bl = _rowsum(g)
    eq = jnp.exp(bc)
    ek = jnp.exp(-bc)
    ed = jnp.exp(bl - bc)
    return bl, eq, ek, ed, q * Q_SCALE * eq, k * ek, k * ed


def _gla_scan_fwd(p_all, la_all, nb, t, tc):
    nm, ncx = t // CH, tc // CH
    nst = nm + ncx
    rowblk, mainblk = _gla_blocks(nb, nm, ncx)

    def body(*refs):
        ins, (o_refs, ss_refs, st_ref) = refs[:8], (refs[8:10], refs[10:12], refs[12])
        i = pl.program_id(1)

        @pl.when(i == 0)
        def _():
            st_ref[...] = jnp.zeros_like(st_ref)

        loaded = [r[...] for r in ins]
        states = [st_ref[j] for j in range(2 * HEADS)]
        outs, new_states = [[], []], []
        for d in range(2):
            q_all, k_all, v_all, g_all = loaded[4 * d:4 * d + 4]
            mask, m01, _ = _gla_mask(d)
            bc_all = _tri_sum(m01, g_all)
            for h in range(HEADS):
                ksl = slice(h * DK, (h + 1) * DK)
                v = v_all[:, h * DV:(h + 1) * DV]
                st = states[d * HEADS + h]
                bl, _, _, _, qs, ks, kd = _gla_chunk(q_all[:, ksl], k_all[:, ksl], g_all[:, ksl], bc_all[:, ksl])
                att = jnp.where(mask, _dot(qs, ks, _NT), 0.0)
                outs[d].append(_dot(qs, st, _NT) + _dot(att, v))
                new_states.append(st * jnp.exp(bl) + _dot(v, kd, _TN))
        for d in range(2):
            o_refs[d][...] = jnp.concatenate(outs[d], axis=1)
            for h in range(HEADS):
                ss_refs[d][0, 0, h] = states[d * HEADS + h]
                st_ref[d * HEADS + h] = new_states[d * HEADS + h]

    def in_specs(d):
        return [pl.BlockSpec((CH, KEY), lambda b, i: (rowblk(d, b, i), 0)),
                pl.BlockSpec((CH, KEY), lambda b, i: (rowblk(d, b, i), 1)),
                pl.BlockSpec((CH, D), lambda b, i: (rowblk(d, b, i), 1)),
                pl.BlockSpec((CH, KEY), lambda b, i: (rowblk(d, b, i), d))]

    outs = pl.pallas_call(
        body, name="gla_scan_fwd", grid=(nb, nst),
        in_specs=in_specs(0) + in_specs(1),
        out_specs=[pl.BlockSpec((CH, D), lambda b, i: (mainblk(0, b, i), 0)),
                   pl.BlockSpec((CH, D), lambda b, i: (mainblk(1, b, i), 0)),
                   pl.BlockSpec((1, 1, HEADS, DV, DK), lambda b, i: (b, i, 0, 0, 0)),
                   pl.BlockSpec((1, 1, HEADS, DV, DK), lambda b, i: (b, i, 0, 0, 0))],
        out_shape=[jax.ShapeDtypeStruct((nb * t, D), F32)] * 2
        + [jax.ShapeDtypeStruct((nb, nst, HEADS, DV, DK), F32)] * 2,
        scratch_shapes=[pltpu.VMEM((2 * HEADS, DV, DK), F32)],
        compiler_params=_params(("parallel", "arbitrary")),
    )(*([p_all, p_all, p_all, la_all] * 2))
    return outs[:2], outs[2:]


def _gla_scan_bwd(p_all, la_all, do, ss, nb, t, tc, after):
    nm, ncx = t // CH, tc // CH
    nst = nm + ncx
    ntot = nb * (t + tc)
    rowblk, mainblk = _gla_blocks(nb, nm, ncx)

    def body(*refs):
        ins, outs, dst_ref = refs[:12], refs[13:21], refs[21]
        ip = pl.program_id(1)
        i = nst - 1 - ip

        @pl.when(ip == 0)
        def _():
            dst_ref[...] = jnp.zeros_like(dst_ref)

        live = jnp.where(i >= ncx, 1.0, 0.0)
        loaded = [[r[...] for r in ins[6 * d:6 * d + 5]] for d in range(2)]
        states = [ins[6 * d + 5][0, 0, h] for d in range(2) for h in range(HEADS)]
        dstates = [dst_ref[j] for j in range(2 * HEADS)]
        results, new_dstates = [], []
        for d in range(2):
            q_all, k_all, v_all, g_all, do_all = loaded[d]
            do_all = do_all * live
            mask, m01, m01_t = _gla_mask(d)
            bc_all = _tri_sum(m01, g_all)
            dqs_l, dks_l, dvs_l, dbs_l, dbls_l = [], [], [], [], []
            for h in range(HEADS):
                ksl = slice(h * DK, (h + 1) * DK)
                vsl = slice(h * DV, (h + 1) * DV)
                bl, eq, ek, ed, qs, ks, kd = _gla_chunk(q_all[:, ksl], k_all[:, ksl], g_all[:, ksl], bc_all[:, ksl])
                st, dst, v, dov = states[d * HEADS + h], dstates[d * HEADS + h], v_all[:, vsl], do_all[:, vsl]
                att = jnp.where(mask, _dot(qs, ks, _NT), 0.0)
                datt = jnp.where(mask, _dot(dov, v, _NT), 0.0)
                dqs = _dot(dov, st) + _dot(datt, ks)
                dks = _dot(datt, qs, _TN)
                dvs_l.append(_dot(att, dov, _TN) + _dot(kd, dst, _NT))
                dkd = _dot(v, dst)
                e = jnp.exp(bl)
                dbls_l.append(e * _rowsum(st * dst) + _rowsum(dkd * kd))
                new_dstates.append(_dot(dov, qs, _TN) + dst * e)
                dqs_l.append(dqs * eq * Q_SCALE)
                dks_l.append(dks * ek + dkd * ed)
                dbs_l.append(dqs * qs - dks * ks - dkd * kd)
            results.append((jnp.concatenate(dqs_l, axis=1), jnp.concatenate(dks_l, axis=1),
                            jnp.concatenate(dvs_l, axis=1),
                            _tri_sum(m01_t, jnp.concatenate(dbs_l, axis=1)) + jnp.concatenate(dbls_l, axis=1)))
        for d in range(2):
            for k in range(4):
                outs[4 * d + k][...] = results[d][k]
        for j in range(2 * HEADS):
            dst_ref[j] = new_dstates[j]

    def in_specs(d):
        return [pl.BlockSpec((CH, KEY), lambda b, ip: (rowblk(d, b, nst - 1 - ip), 0)),
                pl.BlockSpec((CH, KEY), lambda b, ip: (rowblk(d, b, nst - 1 - ip), 1)),
                pl.BlockSpec((CH, D), lambda b, ip: (rowblk(d, b, nst - 1 - ip), 1)),
                pl.BlockSpec((CH, KEY), lambda b, ip: (rowblk(d, b, nst - 1 - ip), d)),
                pl.BlockSpec((CH, D), lambda b, ip: (mainblk(d, b, nst - 1 - ip), 0)),
                pl.BlockSpec((1, 1, HEADS, DV, DK), lambda b, ip: (b, nst - 1 - ip, 0, 0, 0))]

    def out_specs(d):
        row = lambda width: pl.BlockSpec((CH, width), lambda b, ip: (rowblk(d, b, nst - 1 - ip), 0))
        return [row(KEY), row(KEY), row(D), row(KEY)]

    shapes = [jax.ShapeDtypeStruct((ntot, KEY), F32), jax.ShapeDtypeStruct((ntot, KEY), F32),
              jax.ShapeDtypeStruct((ntot, D), F32), jax.ShapeDtypeStruct((ntot, KEY), F32)]
    outs = pl.pallas_call(
        body, name="gla_scan_bwd", grid=(nb, nst),
        in_specs=in_specs(0) + in_specs(1) + [pl.BlockSpec(memory_space=pl.ANY)],
        out_specs=out_specs(0) + out_specs(1),
        out_shape=shapes * 2,
        scratch_shapes=[pltpu.VMEM((2 * HEADS, DV, DK), F32)],
        compiler_params=_params(("parallel", "arbitrary")),
    )(p_all, p_all, p_all, la_all, do, ss[0], p_all, p_all, p_all, la_all, do, ss[1], after)
    return [[outs[k], outs[4 + k]] for k in range(4)]


def _gla_post_fwd(o2, p_all, head_gain, n):
    tt = _tile(n, 256)

    def body(of_ref, ob_ref, g_ref, hg_ref, y_ref):
        o = of_ref[...] + ob_ref[...]
        gv = g_ref[...]
        hg = hg_ref[...]
        for h in range(HEADS):
            oh = o[:, h * DV:(h + 1) * DV]
            r = lax.rsqrt(jnp.mean(oh * oh, axis=-1, keepdims=True) + EPS)
            y_ref[:, h * DV:(h + 1) * DV] = ((oh * r) * hg * _silu(gv[:, h * DV:(h + 1) * DV])).astype(BF16)

    row = pl.BlockSpec((tt, D), lambda i: (i, 0))
    return pl.pallas_call(
        body, name="gla_post_fwd", grid=(n // tt,),
        in_specs=[row, row, pl.BlockSpec((tt, D), lambda i: (i, 2)), pl.BlockSpec((1, DV), lambda i: (0, 0))],
        out_specs=row,
        out_shape=jax.ShapeDtypeStruct((n, D), BF16),
        compiler_params=_params(("parallel",)),
    )(o2[0], o2[1], p_all, head_gain)


def _gla_post_bwd(o2, p_all, head_gain, dyb, n):
    tt = _tile(n, 256)

    def body(of_ref, ob_ref, g_ref, hg_ref, dy_ref, do_ref, dg_ref, dhg_ref):
        i = pl.program_id(0)

        @pl.when(i == 0)
        def _():
            dhg_ref[...] = jnp.zeros_like(dhg_ref)

        o = of_ref[...] + ob_ref[...]
        gv = g_ref[...]
        hg = hg_ref[...]
        dy = dy_ref[...]
        acc = jnp.zeros((1, DV), F32)
        for h in range(HEADS):
            sl = slice(h * DV, (h + 1) * DV)
            oh = o[:, sl]
            r = lax.rsqrt(jnp.mean(oh * oh, axis=-1, keepdims=True) + EPS)
            on = oh * r
            gh = gv[:, sl]
            dyh = dy[:, sl]
            dg_ref[:, sl] = dyh * (on * hg) * _dsilu(gh)
            dog = dyh * _silu(gh)
            acc = acc + _rowsum(dog * on)
            don = dog * hg
            do_ref[:, sl] = r * (don - on * jnp.mean(don * on, axis=-1, keepdims=True))
        dhg_ref[...] += acc

    return pl.pallas_call(
        body, name="gla_post_bwd", grid=(n // tt,),
        in_specs=[pl.BlockSpec((tt, D), lambda i: (i, 0)), pl.BlockSpec((tt, D), lambda i: (i, 0)),
                  pl.BlockSpec((tt, D), lambda i: (i, 2)),
                  pl.BlockSpec((1, DV), lambda i: (0, 0)), pl.BlockSpec((tt, D), lambda i: (i, 0))],
        out_specs=[pl.BlockSpec((tt, D), lambda i: (i, 0)), pl.BlockSpec((tt, D), lambda i: (i, 0)),
                   pl.BlockSpec((1, DV), lambda i: (0, 0))],
        out_shape=[jax.ShapeDtypeStruct((n, D), F32), jax.ShapeDtypeStruct((n, D), F32),
                   jax.ShapeDtypeStruct((1, DV), F32)],
        compiler_params=_params(("arbitrary",)),
    )(o2[0], o2[1], p_all, head_gain, dyb)


def _gla_assemble(p_all, w2, b2, dq, dk, dv, dla, dgate, n):
    ntot = p_all.shape[0]
    tt = _tile(n, 128)
    nmain = n // tt
    assert ntot % tt == 0

    def body(a_ref, w_ref, b_ref, dqf_ref, dqb_ref, dkf_ref, dkb_ref, dvf_ref, dvb_ref, dlf_ref, dlb_ref, dg_ref,
             dp_ref, dw_ref, db_ref):
        i = pl.program_id(0)

        @pl.when(i == 0)
        def _():
            dw_ref[...] = jnp.zeros_like(dw_ref)
            db_ref[...] = jnp.zeros_like(db_ref)

        a = a_ref[...]
        w = w_ref[...]
        z = _dot(a, w) + b_ref[...]
        dla = jnp.concatenate([dlf_ref[...], dlb_ref[...]], axis=1)
        dz = dla * (1.0 / (1.0 + jnp.exp(z))) * (1.0 / TAU)
        dw_ref[...] += _dot(a, dz, _TN)
        db_ref[...] += _rowsum(dz)
        dp_ref[:, 0:KEY] = (dqf_ref[...] + dqb_ref[...]).astype(BF16)
        dp_ref[:, KEY:2 * KEY] = (dkf_ref[...] + dkb_ref[...]).astype(BF16)
        dp_ref[:, 2 * KEY:2 * KEY + D] = (dvf_ref[...] + dvb_ref[...]).astype(BF16)
        dp_ref[:, 2 * KEY + D:2 * KEY + 2 * D] = (dg_ref[...] * jnp.where(i < nmain, 1.0, 0.0)).astype(BF16)
        dp_ref[:, 2 * KEY + 2 * D:GLA_IN_PAD] = _dot(dz, w, _NT).astype(BF16)

    row = lambda width: pl.BlockSpec((tt, width), lambda i: (i, 0))
    return pl.pallas_call(
        body, name="gla_assemble", grid=(ntot // tt,),
        in_specs=[pl.BlockSpec((tt, 128), lambda i: (i, (2 * KEY + 2 * D) // 128)),
                  pl.BlockSpec((128, 2 * KEY), lambda i: (0, 0)), pl.BlockSpec((1, 2 * KEY), lambda i: (0, 0)),
                  row(KEY), row(KEY), row(KEY), row(KEY), row(D), row(D), row(KEY), row(KEY),
                  pl.BlockSpec((tt, D), lambda i: (jnp.minimum(i, nmain - 1), 0))],
        out_specs=[pl.BlockSpec((tt, GLA_IN_PAD), lambda i: (i, 0)), pl.BlockSpec((128, 2 * KEY), lambda i: (0, 0)),
                   pl.BlockSpec((1, 2 * KEY), lambda i: (0, 0))],
        out_shape=[jax.ShapeDtypeStruct((ntot, GLA_IN_PAD), BF16), jax.ShapeDtypeStruct((128, 2 * KEY), F32),
                   jax.ShapeDtypeStruct((1, 2 * KEY), F32)],
        compiler_params=_params(("arbitrary",)),
    )(p_all, w2, b2, dq[0], dq[1], dk[0], dk[1], dv[0], dv[1], dla[0], dla[1], dgate)


ADA_ROWS = 24
ADA_SH = N_MOD * D // N_CHIPS


def _ada_fwd(cvec, ada_w, ada_b_sh):
    def body(c_ref, w_ref, b_ref, o_ref):
        o_ref[0] = _dot(_silu(c_ref[...]), w_ref[0]) + b_ref[0]

    return pl.pallas_call(
        body, name="ada_fwd", grid=(2,),
        in_specs=[pl.BlockSpec((ADA_ROWS, D), lambda l: (0, 0)), pl.BlockSpec((1, D, ADA_SH), lambda l: (l, 0, 0)),
                  pl.BlockSpec((1, 1, ADA_SH), lambda l: (l, 0, 0))],
        out_specs=pl.BlockSpec((1, ADA_ROWS, ADA_SH), lambda l: (l, 0, 0)),
        out_shape=jax.ShapeDtypeStruct((2, ADA_ROWS, ADA_SH), F32),
        compiler_params=_params(("parallel",)),
    )(cvec, ada_w, ada_b_sh)


def _ada_bwd(cvec, ada_w, dmod_sh):
    def body(c_ref, w_ref, dm_ref, gw_ref, dc_ref):
        dm = dm_ref[0]
        gw_ref[0] = _dot(_silu(c_ref[...]), dm, _TN)
        dc_ref[0] = _dot(dm, w_ref[0], _NT)

    return pl.pallas_call(
        body, name="ada_bwd", grid=(2,),
        in_specs=[pl.BlockSpec((ADA_ROWS, D), lambda l: (0, 0)), pl.BlockSpec((1, D, ADA_SH), lambda l: (l, 0, 0)),
                  pl.BlockSpec((1, ADA_ROWS, ADA_SH), lambda l: (l, 0, 0))],
        out_specs=[pl.BlockSpec((1, D, ADA_SH), lambda l: (l, 0, 0)), pl.BlockSpec((1, ADA_ROWS, D), lambda l: (l, 0, 0))],
        out_shape=[jax.ShapeDtypeStruct((2, D, ADA_SH), F32), jax.ShapeDtypeStruct((2, ADA_ROWS, D), F32)],
        compiler_params=_params(("parallel",)),
    )(cvec, ada_w, dmod_sh)


def _sum_slots(x, name):
    s, r, _ = x.shape

    def body(x_ref, o_ref):
        acc = x_ref[0]
        for k in range(1, s):
            acc = acc + x_ref[k]
        o_ref[...] = acc

    return pl.pallas_call(
        body, name=name, out_shape=jax.ShapeDtypeStruct((r, 128), F32),
        in_specs=[pl.BlockSpec(memory_space=pltpu.VMEM)], out_specs=pl.BlockSpec(memory_space=pltpu.VMEM),
    )(x)


def _cctx_grad(dscc_parts, c_ctx):
    def body(p_ref, c_ref, o_ref):
        acc = p_ref[0]
        for k in range(1, N_CHIPS):
            acc = acc + p_ref[k]
        o_ref[...] = acc * _dsilu(c_ref[...])

    return pl.pallas_call(
        body, name="cctx_grad", out_shape=jax.ShapeDtypeStruct((8, 128), F32),
        in_specs=[pl.BlockSpec(memory_space=pltpu.VMEM)] * 2, out_specs=pl.BlockSpec(memory_space=pltpu.VMEM),
    )(dscc_parts, c_ctx)


def _adamw(w, g, m, v, name, after):
    nl, r, cdim = w.shape
    tr = _tile(r, 256)
    c1 = 1.0 - ADAM_B1 ** ADAM_STEP
    c2 = 1.0 - ADAM_B2 ** ADAM_STEP

    def body(w_ref, g_ref, m_ref, v_ref, after_ref, d_ref, mo_ref, vo_ref):
        gv = g_ref[...]
        mn = ADAM_B1 * m_ref[...] + (1.0 - ADAM_B1) * gv
        vn = ADAM_B2 * v_ref[...] + (1.0 - ADAM_B2) * (gv * gv)
        mo_ref[...] = mn
        vo_ref[...] = vn
        d_ref[...] = -ADAM_LR * ((mn / c1) / (jnp.sqrt(vn / c2) + ADAM_EPS) + ADAM_WD * w_ref[...])

    spec = pl.BlockSpec((1, tr, cdim), lambda l, i: (l, i, 0))
    sds = jax.ShapeDtypeStruct((nl, r, cdim), F32)
    return pl.pallas_call(
        body, name=name, grid=(nl, r // tr), in_specs=[spec] * 4 + [pl.BlockSpec(memory_space=pl.ANY)],
        out_specs=[spec] * 3, out_shape=[sds] * 3, compiler_params=_params(("parallel", "parallel")),
    )(w, g, m, v, after)


def _place():
    x, y, c = lax.axis_index("x"), lax.axis_index("y"), lax.axis_index("c")
    return x, y, c


def _allgather_small(blk, name):
    m_per, n = blk.shape

    def body(x_ref, out_ref, send_sems, recv_sems, local_sem):
        x, y, c = _place()
        me, sibling = (x, y, c), (x, y, 1 - c)
        chips = [(1 - x, y), (x, 1 - y), (1 - x, 1 - y)]

        def rows(px, py, pc):
            return out_ref.at[pl.ds((4 * px + 2 * py + pc) * m_per, m_per), :]

        def copy(k, block, to, src=None):
            return pltpu.make_async_remote_copy(
                src_ref=rows(*block) if src is None else src, dst_ref=rows(*block),
                send_sem=send_sems.at[k], recv_sem=recv_sems.at[k], device_id=to, device_id_type=MESH)

        mine = pltpu.make_async_copy(x_ref, rows(*me), local_sem)
        mine.start()
        first = [copy(0, me, sibling, src=x_ref)]
        first += [copy(1 + j, me, (*chip, c), src=x_ref) for j, chip in enumerate(chips)]
        for cp in first:
            cp.start()
        passed = [copy(4 + j, (*chip, c), sibling) for j, chip in enumerate(chips)]
        for j, chip in enumerate(chips):
            copy(1 + j, (*chip, c), me).wait_recv()
            passed[j].start()
        copy(0, sibling, me).wait_recv()
        for j, chip in enumerate(chips):
            copy(4 + j, (*chip, 1 - c), me).wait_recv()
        for cp in first + passed:
            cp.wait_send()
        mine.wait()

    return pl.pallas_call(
        body, name=name,
        out_shape=jax.ShapeDtypeStruct((N_DEV * m_per, n), blk.dtype),
        in_specs=[pl.BlockSpec(memory_space=pltpu.VMEM)],
        out_specs=pl.BlockSpec(memory_space=pltpu.VMEM),
        scratch_shapes=[pltpu.SemaphoreType.DMA((7,)), pltpu.SemaphoreType.DMA((7,)), pltpu.SemaphoreType.DMA],
    )(blk)


def _other_chips(x, y):
    return [(1 - x, y), (x, 1 - y), (1 - x, 1 - y)]


_HBM_SPEC = pl.BlockSpec(memory_space=pltpu.HBM)
_SEM_SPEC = pl.BlockSpec(memory_space=pltpu.SEMAPHORE)
_SPLIT_PARAMS = pltpu.CompilerParams(has_side_effects=pltpu.SideEffectType.DATAFLOW_SIDE_EFFECTING)


def _in_hbm(a):
    return pltpu.with_memory_space_constraint(a, pltpu.HBM)


def _ag_copies(own_ref, land_ref, send_sems, recv_sems):
    x, y, c = _place()
    chip = 2 * x + y
    hr = own_ref.shape[0] // 2

    def half(ch):
        return land_ref.at[ch, pl.ds(c * hr, hr), :]

    def copy(k, src, dst, to):
        return pltpu.make_async_remote_copy(src_ref=src, dst_ref=dst, send_sem=send_sems.at[k],
                                            recv_sem=recv_sems.at[k], device_id=to, device_id_type=MESH)

    sends, expects = [], []
    for j, (ox, oy) in enumerate(_other_chips(x, y)):
        sends.append(copy(j, own_ref.at[pl.ds(c * hr, hr), :], half(chip), (ox, oy, c)))
        expects.append(copy(j, half(2 * ox + oy), half(2 * ox + oy), (ox, oy, c)))
    own_slot = copy(3, own_ref, land_ref.at[chip], (x, y, 1 - c))
    return sends + [own_slot], expects + [own_slot]


def _sc_copies(p_ref, land_ref, send_sems, recv_sems):
    x, y, c = _place()
    chip = 2 * x + y
    sends, expects = [], []
    for j, (ox, oy) in enumerate(_other_chips(x, y)):
        och = 2 * ox + oy
        mk = lambda dst_slot: pltpu.make_async_remote_copy(
            src_ref=p_ref.at[och], dst_ref=land_ref.at[dst_slot], send_sem=send_sems.at[j],
            recv_sem=recv_sems.at[j], device_id=(ox, oy, c), device_id_type=MESH)
        sends.append(mk(chip))
        expects.append(mk(och))
    return sends, expects


def _pe_copies(g_ref, land_ref, send_sems, recv_sems):
    x, y, c = _place()
    hr = g_ref.shape[1] // 2
    cp = pltpu.make_async_remote_copy(
        src_ref=g_ref.at[:, pl.ds((1 - c) * hr, hr), :], dst_ref=land_ref, send_sem=send_sems.at[0],
        recv_sem=recv_sems.at[0], device_id=(x, y, 1 - c), device_id_type=MESH)
    return [cp], [cp]


def _split_start(src, land_shape, copies, n_copies, after, name):
    def body(src_ref, land_ref, after_ref, send_sems, recv_sems, src_thru, land_thru, token):
        for cp in copies(src_ref, land_ref, send_sems, recv_sems)[0]:
            cp.start()
        token[...] = jnp.zeros_like(token)

    land = lax.empty(land_shape, src.dtype)
    return pl.pallas_call(
        body, name=name,
        out_shape=(pltpu.SemaphoreType.DMA((n_copies,)), pltpu.SemaphoreType.DMA((n_copies,)),
                   pltpu.HBM(src.shape, src.dtype), pltpu.HBM(land_shape, src.dtype),
                   jax.ShapeDtypeStruct((8, 128), F32)),
        in_specs=(_HBM_SPEC, _HBM_SPEC, pl.BlockSpec(memory_space=pl.ANY)),
        out_specs=(_SEM_SPEC, _SEM_SPEC, _HBM_SPEC, _HBM_SPEC, pl.BlockSpec(memory_space=pltpu.VMEM)),
        input_output_aliases={0: 2, 1: 3}, compiler_params=_SPLIT_PARAMS,
    )(_in_hbm(src), _in_hbm(land), after)


def _split_wait(started, after, copies, name):
    send_sems, recv_sems, src_thru, land_thru, _ = started

    def body(src_ref, land_ref, send_sems, recv_sems, after_ref, src_dead, got_ref):
        sends, expects = copies(src_ref, land_ref, send_sems, recv_sems)
        for cp in sends:
            cp.wait_send()
        for cp in expects:
            cp.wait_recv()

    return pl.pallas_call(
        body, name=name,
        out_shape=(pltpu.HBM(src_thru.shape, src_thru.dtype), pltpu.HBM(land_thru.shape, land_thru.dtype)),
        in_specs=(_HBM_SPEC, _HBM_SPEC, _SEM_SPEC, _SEM_SPEC, pl.BlockSpec(memory_space=pl.ANY)),
        out_specs=(_HBM_SPEC, _HBM_SPEC), input_output_aliases={0: 0, 1: 1}, compiler_params=_SPLIT_PARAMS,
    )(src_thru, land_thru, send_sems, recv_sems, after)


def _ag_pass_on(land, name):
    hr = land.shape[1] // 2

    def body(in_ref, out_ref, send_sems, recv_sems):
        x, y, c = _place()

        def copy(j, ox, oy, cc):
            ref = out_ref.at[2 * ox + oy, pl.ds(cc * hr, hr), :]
            return pltpu.make_async_remote_copy(src_ref=ref, dst_ref=ref, send_sem=send_sems.at[j],
                                                recv_sem=recv_sems.at[j], device_id=(x, y, 1 - c),
                                                device_id_type=MESH)

        others = _other_chips(x, y)
        for j, (ox, oy) in enumerate(others):
            copy(j, ox, oy, c).start()
        for j, (ox, oy) in enumerate(others):
            copy(j, ox, oy, 1 - c).wait_recv()
        for j, (ox, oy) in enumerate(others):
            copy(j, ox, oy, c).wait_send()

    any_spec = pl.BlockSpec(memory_space=pl.ANY)
    return pl.pallas_call(
        body, name=name, out_shape=jax.ShapeDtypeStruct(land.shape, land.dtype),
        in_specs=[any_spec], out_specs=any_spec, input_output_aliases={0: 0},
        scratch_shapes=[pltpu.SemaphoreType.DMA((3,)), pltpu.SemaphoreType.DMA((3,))],
    )(land)


def _rs_pair_exchange(g, name):
    r = g.shape[1]
    hr = r // 2

    def body(g_ref, got_ref, send_sem, recv_sem):
        x, y, c = _place()
        cp = pltpu.make_async_remote_copy(
            src_ref=g_ref.at[:, pl.ds((1 - c) * hr, hr), :], dst_ref=got_ref, send_sem=send_sem, recv_sem=recv_sem,
            device_id=(x, y, 1 - c), device_id_type=MESH)
        cp.start()
        cp.wait()

    any_spec = pl.BlockSpec(memory_space=pl.ANY)
    return pl.pallas_call(
        body, name=name,
        out_shape=jax.ShapeDtypeStruct((N_CHIPS, hr, D), F32),
        in_specs=[any_spec], out_specs=any_spec,
        scratch_shapes=[pltpu.SemaphoreType.DMA, pltpu.SemaphoreType.DMA],
    )(g)


def _rs_chip_sum(place, g, got, name):
    r = g.shape[1]
    hr = r // 2
    tr = _tile(hr, 640, 16)
    nt = hr // tr

    def body(pl_ref, g_ref, got_ref, p16_ref, p32_ref):
        s = pl.program_id(1)
        p = g_ref[0] + got_ref[0]
        p16_ref[0] = p.astype(BF16)

        @pl.when(s == pl_ref[1])
        def _():
            p32_ref[...] = p

    return pl.pallas_call(
        body, name=name,
        grid_spec=pltpu.PrefetchScalarGridSpec(
            num_scalar_prefetch=1, grid=(nt, N_CHIPS),
            in_specs=[pl.BlockSpec((1, tr, D), lambda i, s, pr: (s, pr[0] * nt + i, 0)),
                      pl.BlockSpec((1, tr, D), lambda i, s, pr: (s, i, 0))],
            out_specs=[pl.BlockSpec((1, tr, D), lambda i, s, pr: (s, i, 0)),
                       pl.BlockSpec((tr, D), lambda i, s, pr: (i, 0))]),
        out_shape=[jax.ShapeDtypeStruct((N_CHIPS, hr, D), BF16), jax.ShapeDtypeStruct((hr, D), F32)],
        compiler_params=_params(("parallel", "arbitrary")),
    )(place, g, got)


def _rs_final_sum(place, parts, p32, name):
    hr = parts.shape[1]
    tr = _tile(hr, 640, 16)
    nt = hr // tr

    def body(pl_ref, a_ref, b_ref, c_ref, p32_ref, o_ref):
        o_ref[...] = ((p32_ref[...] + a_ref[0].astype(F32)) + b_ref[0].astype(F32)) + c_ref[0].astype(F32)

    def other(j):
        return pl.BlockSpec((1, tr, D), lambda i, pr: (j + jnp.where(pr[1] <= j, 1, 0), i, 0))

    return pl.pallas_call(
        body, name=name,
        grid_spec=pltpu.PrefetchScalarGridSpec(
            num_scalar_prefetch=1, grid=(nt,),
            in_specs=[other(0), other(1), other(2), pl.BlockSpec((tr, D), lambda i, pr: (i, 0))],
            out_specs=pl.BlockSpec((tr, D), lambda i, pr: (pr[0] * nt + i, 0))),
        out_shape=jax.ShapeDtypeStruct((2 * hr, D), F32),
        compiler_params=_params(("parallel",)),
    )(place, parts, parts, parts, p32)


def _rs_pair_gather(both, name):
    hr = both.shape[0] // 2

    def body(in_ref, out_ref, send_sem, recv_sem):
        x, y, c = _place()
        mine = out_ref.at[pl.ds(c * hr, hr), :]
        cp = pltpu.make_async_remote_copy(
            src_ref=mine, dst_ref=mine, send_sem=send_sem, recv_sem=recv_sem,
            device_id=(x, y, 1 - c), device_id_type=MESH)
        cp.start()
        theirs = out_ref.at[pl.ds((1 - c) * hr, hr), :]
        pltpu.make_async_remote_copy(
            src_ref=theirs, dst_ref=theirs, send_sem=send_sem, recv_sem=recv_sem,
            device_id=(x, y, 1 - c), device_id_type=MESH).wait_recv()
        cp.wait_send()

    any_spec = pl.BlockSpec(memory_space=pl.ANY)
    return pl.pallas_call(
        body, name=name,
        out_shape=jax.ShapeDtypeStruct(both.shape, F32),
        in_specs=[any_spec], out_specs=any_spec, input_output_aliases={0: 0},
        scratch_shapes=[pltpu.SemaphoreType.DMA, pltpu.SemaphoreType.DMA],
    )(both)


def _local_step(x, ctx, tgt, mods, mc, ag_gin, ag_main, place, small):
    nb, t, _ = x.shape
    tc = ctx.shape[1]
    n = nb * t
    nc = nb * tc
    xf = x.reshape(n, D)
    cf = ctx.reshape(nc, D)
    tf = tgt.reshape(n, D)
    vec = lambda a: a.reshape(1, -1)
    m = [[mods[l, :, k, :].reshape(nb, 1, D) for k in range(N_MOD)] for l in range(2)]
    mc_b = [jnp.broadcast_to(mc[k].reshape(1, 1, D), (nb, 1, D)) for k in range(2)]

    cw = [small["ffn_conv_w"][l] for l in range(2)]
    cb = [small["ffn_conv_b"][l].reshape(1, -1) for l in range(2)]
    w2 = jnp.zeros((128, 2 * KEY), F32)
    w2 = w2.at[0:RANK, 0:KEY].set(small["gla_w_a2"][0]).at[RANK:2 * RANK, KEY:].set(small["gla_w_a2"][1])
    b2 = small["gla_b_a"].reshape(1, 2 * KEY)
    hg = small["gla_head_norm"].reshape(1, DV)

    hn_all = _mod_fwd(xf, vec(small["norm_mix"][0]), m[0][0], m[0][1], t, "mod0_main", out_rows=n + nc)
    hn_all = _mod_fwd(cf, vec(small["norm_mix"][0]), mc_b[0], mc_b[1], tc, "mod0_ctx", out_rows=n + nc,
                      into=hn_all, row0=n)
    gin = _ag_pass_on(_split_wait(ag_gin, hn_all, _ag_copies, "ag_gin_wait")[1], "ag_gin_pass_on")
    w_gin = jnp.pad(gin[:, :_GIN_ROWS, :].reshape(GLA_IN, D), ((0, GLA_IN_PAD - GLA_IN), (0, 0)))
    p_all = _mm(hn_all, w_gin, "nt", F32, "gla_in_proj", 768, 3200)
    la_all = _gla_decay_fwd(p_all, w2, b2)
    o2, ss = _gla_scan_fwd(p_all, la_all, nb, t, tc)
    wg = _ag_pass_on(_split_wait(ag_main, o2[0], _ag_copies, "ag_main_wait")[1], "ag_main_pass_on")
    offs = _offsets(_MAIN, _MAIN_ROWS)
    rows = _MAIN_ROWS

    def w_nt(a, k, name, out_dtype=BF16, tm=1024):
        return _mm_nt_w(a, wg, offs[k], rows[k], name, tm, out_dtype)

    def w_nn(a3, k, name, tm, tn):
        return _mm_nn_w(a3, wg, offs[k], rows[k], name, tm, tn)

    def w_nn_mod(a3, k, h, gate, gain, shift, scale, name):
        return _mm_nn_w_mod(a3, wg, offs[k], rows[k], h, gate, vec(gain), shift, scale, t, name, 512)

    yb0 = _gla_post_fwd(o2, p_all, hg, n)
    y0, h1, hn1 = w_nn_mod(yb0[None], "gla_out", xf, m[0][2], small["norm_ffn"][0], m[0][3], m[0][4],
                           "gla_out_proj_mod")
    u0, z0 = _ffn_up_mid_fwd(hn1, wg, offs["up_t0"], cw[0], cb[0], nb, t, "ffn0_up_mid")
    f0, h2, hn2 = w_nn_mod(z0[None], "down0", h1, m[0][5], small["norm_mix"][1], m[1][0], m[1][1],
                           "ffn0_down_mod")
    p1 = w_nt(hn2, "sc_in_t", "sc_in_proj")
    yb1 = _sc_mid_fwd(p1, small["sc_conv_w"], nb, t)
    y1, h3, hn3 = w_nn_mod(yb1[None], "sc_out", h2, m[1][2], small["norm_ffn"][1], m[1][3], m[1][4],
                           "sc_out_proj_mod")
    u1, z1 = _ffn_up_mid_fwd(hn3, wg, offs["up_t1"], cw[1], cb[1], nb, t, "ffn1_up_mid")
    f1 = w_nn(z1[None], "down1", "ffn1_down", 1024, 1024)
    loss, dh4, df1, dm15, dfinal = _final(h3, f1, m[1][5], vec(small["final_norm"]), tf, t)

    gs = {}
    dmods = [[None] * N_MOD for _ in range(2)]
    dmods[1][5] = dm15

    def w_dw(a3, b, g_prev, k, name, tm):
        return _mm_dw(a3, b, g_prev, offs[k], rows[k], name, tm)

    def w_dx_mod(a3, k, h_in, dh_out, gain, scale, y_prev, gate_prev, name):
        return _mm_nn_w_modbwd(a3, wg, offs[k], rows[k], h_in, dh_out, vec(gain), scale, y_prev, gate_prev, t,
                               name, 256)

    def ffn_bwd(l, df, u, z, hn, g_prev, h_in, dh_out, scale, y_prev, gate_prev):
        dz = w_nt(df, f"down{l}", f"ffn{l}_down_dx")
        g_acc = w_dw(z[None], df, g_prev, f"down{l}", f"ffn{l}_down_dw", 640)
        du, dcw, dcb = _ffn_mid_bwd(u, cw[l], cb[l], dz, nb, t, f"ffn{l}_mid_bwd")
        r = w_dx_mod(du, f"up_t{l}", h_in, dh_out, small["norm_ffn"][l], scale, y_prev, gate_prev,
                     f"ffn{l}_up_dx_mod")
        g_acc = w_dw(du, hn, g_acc, f"up_t{l}", f"ffn{l}_up_dw", 640)
        return r, g_acc, jnp.moveaxis(dcw, 0, 1).reshape(3, 2 * HID), dcb.reshape(2 * HID)

    r, g_acc, dcw1, dcb1 = ffn_bwd(1, df1, u1, z1, hn3, None, h3, dh4, m[1][4], y1, m[1][2])
    dh3, dmods[1][4], dmods[1][3], dnf1, dy1, dmods[1][2] = (r["dh"], r["dscale"], r["dshift"], r["dgain"],
                                                             r["dy_prev"], r["dgate_prev"])
    dyb1 = w_nt(dy1, "sc_out", "sc_out_dx")
    g_acc = w_dw(yb1[None], dy1, g_acc, "sc_out", "sc_out_dw", 256)
    dp1, dscw = _sc_mid_bwd(p1, small["sc_conv_w"], dyb1, nb, t)
    r = w_dx_mod(dp1, "sc_in_t", h2, dh3, small["norm_mix"][1], m[1][1], f0, m[0][5], "sc_in_dx_mod")
    g_acc = w_dw(dp1, hn2, g_acc, "sc_in_t", "sc_in_dw", 256)
    dh2, dmods[1][1], dmods[1][0], dnm1, df0, dmods[0][5] = (r["dh"], r["dscale"], r["dshift"], r["dgain"],
                                                             r["dy_prev"], r["dgate_prev"])
    r, g_acc, dcw0, dcb0 = ffn_bwd(0, df0, u0, z0, hn1, g_acc, h1, dh2, m[0][4], y0, m[0][2])
    dh1, dmods[0][4], dmods[0][3], dnf0, dy0, dmods[0][2] = (r["dh"], r["dscale"], r["dshift"], r["dgain"],
                                                             r["dy_prev"], r["dgate_prev"])
    g_packed = w_dw(yb0[None], dy0, g_acc, "gla_out", "gla_out_dw", 256)
    pair = _split_start(g_packed, (N_CHIPS, _MAIN_TOTAL // 2, D), _pe_copies, 1, dy0, "rs_main_pair_start")
    dyb0 = w_nt(dy0, "gla_out", "gla_out_dx", F32)
    do, dgate, dhg = _gla_post_bwd(o2, p_all, hg + pair[4][0:1, 0:1], dyb0, n)
    g_packed, from_sibling = _split_wait(pair, do, _pe_copies, "rs_main_pair_wait")
    p16, p32 = _rs_chip_sum(place, g_packed, from_sibling, "rs_main_chip_sum")
    sc_main = _split_start(p16, p16.shape, _sc_copies, 3, p32, "rs_main_scatter_start")
    dq, dk, dv, dla = _gla_scan_bwd(p_all, la_all, do, ss, nb, t, tc, sc_main[4])
    dp, dw2, db2 = _gla_assemble(p_all, w2, b2, dq, dk, dv, dla, dgate, n)
    dhn_all = _mm(dp, w_gin, "nn", F32, "gla_in_dx", 768, 512)
    landed = _split_wait(sc_main, dhn_all, _sc_copies, "rs_main_scatter_wait")[1]
    g_main = _rs_pair_gather(_rs_final_sum(place, landed, p32, "rs_main_final_sum"), "rs_main_pair_gather")
    g_gin = _mm(dp, hn_all, "tn", F32, "gla_in_dw", 640, 1024)[:GLA_IN]
    g_gin = jnp.pad(g_gin.reshape(N_CHIPS, _GIN_ROWS, D), ((0, 0), (0, _GIN_PAD - _GIN_ROWS), (0, 0)))
    from_sibling = _rs_pair_exchange(g_gin, "rs_gin_pair_exchange")
    p16_gin, p32_gin = _rs_chip_sum(place, g_gin, from_sibling, "rs_gin_chip_sum")
    r = _mod_bwd(xf, dhn_all, vec(small["norm_mix"][0]), m[0][1], t, "mod0_main_bwd", dh_out=dh1)
    grad_x, dmods[0][1], dmods[0][0], dnm0 = r["dh"], r["dscale"], r["dshift"], r["dgain"]
    rc = _mod_bwd(cf, dhn_all, vec(small["norm_mix"][0]), mc_b[1], tc, "mod0_ctx_bwd", dhn_row0=n, need_dh=False)
    dmc = jnp.stack([jnp.sum(rc["dshift"], axis=0).reshape(D), jnp.sum(rc["dscale"], axis=0).reshape(D)])
    dnm0 = dnm0 + rc["dgain"]

    gs["norm_mix"] = jnp.concatenate([dnm0, dnm1], axis=0)
    gs["norm_ffn"] = jnp.concatenate([dnf0, dnf1], axis=0)
    gs["final_norm"] = dfinal.reshape(D)
    gs["gla_w_a2"] = jnp.stack([dw2[0:RANK, 0:KEY], dw2[RANK:2 * RANK, KEY:]])
    gs["gla_b_a"] = db2.reshape(2, KEY)
    gs["gla_head_norm"] = dhg.reshape(DV)
    gs["sc_conv_w"] = dscw
    gs["ffn_conv_w"] = jnp.stack([dcw0, dcw1])
    gs["ffn_conv_b"] = jnp.stack([dcb0, dcb1])
    dmods_arr = jnp.stack([jnp.stack([dmods[l][k].reshape(nb, D) for k in range(N_MOD)], axis=1) for l in range(2)])
    return loss, grad_x.reshape(nb, t, D), g_main, p16_gin, p32_gin, gs, dmods_arr, dmc


def _pack(arrs):
    parts, meta, off = [], [], 0
    for a in arrs:
        r = a.size // 128
        rp = -(-r // 8) * 8
        a2 = a.reshape(r, 128).astype(F32)
        if rp != r:
            a2 = jnp.pad(a2, ((0, rp - r), (0, 0)))
        parts.append(a2)
        meta.append((off, r, a.shape))
        off += rp
    return jnp.concatenate(parts, axis=0), meta


def _unpack(buf, meta, lead=()):
    return [buf[..., off:off + r, :].reshape(*lead, *shape) for off, r, shape in meta]


_MAIN = ("up_t0", "up_t1", "down0", "down1", "sc_in_t", "gla_out", "sc_out")
_MAIN_ROWS = {"sc_in_t": 3 * D // N_CHIPS, "up_t0": 2 * HID // N_CHIPS, "up_t1": 2 * HID // N_CHIPS,
              "gla_out": D // N_CHIPS, "sc_out": D // N_CHIPS, "down0": HID // N_CHIPS, "down1": HID // N_CHIPS}
_MAIN_TOTAL = sum(_MAIN_ROWS.values())
_GIN_ROWS = GLA_IN // N_CHIPS
_GIN_PAD = -(-_GIN_ROWS // 32) * 32


def _offsets(names, rows):
    off, out = 0, {}
    for k in names:
        out[k] = off
        off += rows[k]
    return out


def kernel(x, c, ctx, c_ctx, ada_w, ada_b, norm_mix, norm_ffn, gla_w_in, gla_w_a2, gla_b_a, gla_head_norm, gla_w_out, sc_w_in, sc_conv_w, sc_w_out, ffn_w_up, ffn_conv_w, ffn_conv_b, ffn_w_down, final_norm, loss_target, m_c_ctx, m_ada_w, m_ada_b, m_norm_mix, m_norm_ffn, m_gla_w_in, m_gla_w_a2, m_gla_b_a, m_gla_head_norm, m_gla_w_out, m_sc_w_in, m_sc_conv_w, m_sc_w_out, m_ffn_w_up, m_ffn_conv_w, m_ffn_conv_b, m_ffn_w_down, m_final_norm, v_c_ctx, v_ada_w, v_ada_b, v_norm_mix, v_norm_ffn, v_gla_w_in, v_gla_w_a2, v_gla_b_a, v_gla_head_norm, v_gla_w_out, v_sc_w_in, v_sc_conv_w, v_sc_w_out, v_ffn_w_up, v_ffn_conv_w, v_ffn_conv_b, v_ffn_w_down, v_final_norm):
    ix, iy, ic = _place()
    chip = 2 * ix + iy
    dev = 2 * chip + ic
    place = jnp.stack([ic, chip]).astype(jnp.int32)
    nb = x.shape[0]
    offs = _offsets(_MAIN, _MAIN_ROWS)

    buf, meta = _pack([c, ffn_conv_w, sc_conv_w, gla_w_a2, gla_b_a])
    got = _allgather_small(buf, "gather_small_in").reshape(N_DEV, buf.shape[0], 128)
    c_all, fcw, scw, wa2, ba = _unpack(got, meta, (N_DEV,))
    c_all = c_all.reshape(N_DEV * nb, D)
    per_chip = lambda a: a[0::2]
    ffn_conv_w_full = jnp.moveaxis(per_chip(fcw), 0, 2).reshape(2, 3, 2 * HID)
    sc_conv_w_full = jnp.moveaxis(per_chip(scw)[:, 0], 0, 1).reshape(3, D)
    gla_w_a2_full = jnp.moveaxis(per_chip(wa2)[:, 0], 0, 2).reshape(2, RANK, KEY)
    gla_b_a_full = jnp.moveaxis(per_chip(ba)[:, 0], 0, 1).reshape(2, KEY)

    cvec = jnp.concatenate([c_all, c_ctx.reshape(1, D), jnp.zeros((ADA_ROWS - N_DEV * nb - 1, D), F32)], axis=0)
    ada_b_sh = lax.dynamic_slice_in_dim(ada_b, chip * ADA_SH, ADA_SH, axis=1).reshape(2, 1, ADA_SH)
    mod_sh = _ada_fwd(cvec, ada_w, ada_b_sh)
    got = _allgather_small(mod_sh.reshape(2 * ADA_ROWS, ADA_SH), "gather_mod")
    mod_full = jnp.moveaxis(per_chip(got.reshape(N_DEV, 2, ADA_ROWS, ADA_SH)), 0, 2).reshape(2, ADA_ROWS, N_MOD * D)
    mc = mod_full[0, N_DEV * nb, :2 * D].reshape(2, D)

    own = {"sc_in_t": sc_w_in[0].T, "up_t0": ffn_w_up[0].T, "up_t1": ffn_w_up[1].T,
           "gla_out": gla_w_out[0], "sc_out": sc_w_out[0], "down0": ffn_w_down[0], "down1": ffn_w_down[1]}
    own_main = jnp.concatenate([own[k].astype(BF16) for k in _MAIN], axis=0)
    own_gin = jnp.pad(gla_w_in[0].T.astype(BF16), ((0, _GIN_PAD - _GIN_ROWS), (0, 0)))
    ag_gin = _split_start(own_gin, (N_CHIPS, _GIN_PAD, D), _ag_copies, 4, mc, "ag_gin_start")
    ag_main = _split_start(own_main, (N_CHIPS, _MAIN_TOTAL, D), _ag_copies, 4, ag_gin[4], "ag_main_start")
    mods = lax.dynamic_slice_in_dim(mod_full, dev * nb, nb, axis=1).reshape(2, nb, N_MOD, D) + ag_main[4][0, 0]

    small = {"norm_mix": norm_mix, "norm_ffn": norm_ffn, "final_norm": final_norm, "gla_w_a2": gla_w_a2_full,
             "gla_b_a": gla_b_a_full, "gla_head_norm": gla_head_norm[0], "sc_conv_w": sc_conv_w_full,
             "ffn_conv_w": ffn_conv_w_full, "ffn_conv_b": ffn_conv_b}
    loss_p, grad_x, g_main, p16_gin, p32_gin, gs, dmods, dmc = _local_step(x, ctx, loss_target, mods, mc, ag_gin,
                                                                           ag_main, place, small)

    sum_names = ["norm_mix", "norm_ffn", "final_norm", "gla_w_a2", "gla_b_a", "gla_head_norm", "sc_conv_w",
                 "ffn_conv_w", "ffn_conv_b"]
    buf, meta = _pack([jnp.broadcast_to(loss_p, (8, 128))] + [gs[k] for k in sum_names] + [dmc, dmods])
    n_sum = meta[-1][0]
    got = _allgather_small(buf, "gather_small_grads").reshape(N_DEV, buf.shape[0], 128)
    summed = _sum_slots(got[:, :n_sum], "sum_small_grads")
    parts = _unpack(summed, meta[:-1])
    loss = parts[0][0, 0]
    g_small = dict(zip(sum_names, parts[1:-1]))
    dmc_tot = parts[-1]
    dmods_all = jnp.moveaxis(_unpack(got, meta[-1:], (N_DEV,))[0], 0, 1).reshape(2, N_DEV * nb, N_MOD * D)

    ctx_row = jnp.stack([jnp.concatenate([dmc_tot.reshape(2 * D), jnp.zeros(((N_MOD - 2) * D,), F32)]),
                         jnp.zeros((N_MOD * D,), F32)]).reshape(2, 1, N_MOD * D)
    dmod_ext = jnp.concatenate([dmods_all, ctx_row, jnp.zeros((2, ADA_ROWS - N_DEV * nb - 1, N_MOD * D), F32)], axis=1)
    g_ada_b = _sum_slots(jnp.moveaxis(dmod_ext, 1, 0).reshape(ADA_ROWS, 2 * N_MOD * D // 128, 128),
                         "sum_ada_b").reshape(2, N_MOD * D)
    dmod_sh = lax.dynamic_slice_in_dim(dmod_ext, chip * ADA_SH, ADA_SH, axis=2)
    g_ada_w, dcv = _ada_bwd(cvec, ada_w, dmod_sh)
    dscc_part = (dcv[0, N_DEV * nb] + dcv[1, N_DEV * nb]).reshape(8, 128)
    got = _allgather_small(dscc_part, "gather_dscc").reshape(N_DEV, 8, 128)
    g_c_ctx = _cctx_grad(per_chip(got), c_ctx.reshape(8, 128)).reshape(D)

    sc_gin = _split_start(p16_gin, p16_gin.shape, _sc_copies, 3, g_c_ctx, "rs_gin_scatter_start")
    seg = {k: g_main[offs[k]:offs[k] + _MAIN_ROWS[k]] for k in _MAIN}

    sl_chip = lambda a, axis, width: lax.dynamic_slice_in_dim(a, chip * width, width, axis=axis)
    grads = {
        "c_ctx": g_c_ctx, "ada_w": g_ada_w, "ada_b": g_ada_b, "norm_mix": g_small["norm_mix"],
        "norm_ffn": g_small["norm_ffn"],
        "gla_w_a2": sl_chip(g_small["gla_w_a2"], 2, KEY // N_CHIPS)[None],
        "gla_b_a": sl_chip(g_small["gla_b_a"], 1, KEY // N_CHIPS)[None],
        "gla_head_norm": g_small["gla_head_norm"][None], "gla_w_out": seg["gla_out"][None],
        "sc_w_in": seg["sc_in_t"].T[None], "sc_conv_w": sl_chip(g_small["sc_conv_w"], 1, D // N_CHIPS)[None],
        "sc_w_out": seg["sc_out"][None], "ffn_w_up": jnp.stack([seg["up_t0"].T, seg["up_t1"].T]),
        "ffn_conv_w": sl_chip(g_small["ffn_conv_w"], 2, 2 * HID // N_CHIPS), "ffn_conv_b": g_small["ffn_conv_b"],
        "ffn_w_down": jnp.stack([seg["down0"], seg["down1"]]), "final_norm": g_small["final_norm"],
    }
    weights = {"c_ctx": c_ctx, "ada_w": ada_w, "ada_b": ada_b, "norm_mix": norm_mix, "norm_ffn": norm_ffn,
               "gla_w_in": gla_w_in, "gla_w_a2": gla_w_a2, "gla_b_a": gla_b_a, "gla_head_norm": gla_head_norm,
               "gla_w_out": gla_w_out, "sc_w_in": sc_w_in, "sc_conv_w": sc_conv_w, "sc_w_out": sc_w_out,
               "ffn_w_up": ffn_w_up, "ffn_conv_w": ffn_conv_w, "ffn_conv_b": ffn_conv_b, "ffn_w_down": ffn_w_down,
               "final_norm": final_norm}
    mom1 = {"c_ctx": m_c_ctx, "ada_w": m_ada_w, "ada_b": m_ada_b, "norm_mix": m_norm_mix, "norm_ffn": m_norm_ffn,
            "gla_w_in": m_gla_w_in, "gla_w_a2": m_gla_w_a2, "gla_b_a": m_gla_b_a, "gla_head_norm": m_gla_head_norm,
            "gla_w_out": m_gla_w_out, "sc_w_in": m_sc_w_in, "sc_conv_w": m_sc_conv_w, "sc_w_out": m_sc_w_out,
            "ffn_w_up": m_ffn_w_up, "ffn_conv_w": m_ffn_conv_w, "ffn_conv_b": m_ffn_conv_b,
            "ffn_w_down": m_ffn_w_down, "final_norm": m_final_norm}
    mom2 = {"c_ctx": v_c_ctx, "ada_w": v_ada_w, "ada_b": v_ada_b, "norm_mix": v_norm_mix, "norm_ffn": v_norm_ffn,
            "gla_w_in": v_gla_w_in, "gla_w_a2": v_gla_w_a2, "gla_b_a": v_gla_b_a, "gla_head_norm": v_gla_head_norm,
            "gla_w_out": v_gla_w_out, "sc_w_in": v_sc_w_in, "sc_conv_w": v_sc_conv_w, "sc_w_out": v_sc_w_out,
            "ffn_w_up": v_ffn_w_up, "ffn_conv_w": v_ffn_conv_w, "ffn_conv_b": v_ffn_conv_b,
            "ffn_w_down": v_ffn_w_down, "final_norm": v_final_norm}
    names = list(weights)

    big_names = ["ada_w", "gla_w_out", "sc_w_in", "sc_w_out", "ffn_w_up", "ffn_w_down", "gla_w_in"]
    small_names = [k for k in names if k not in big_names]
    delta, new_m, new_v = {}, {}, {}
    done = []

    def big_adamw(k, token):
        delta[k], new_m[k], new_v[k] = _adamw(weights[k], grads[k], mom1[k], mom2[k], "adamw_" + k, token)
        done.append(new_v[k][0, 0:1, 0:128])

    for k in big_names[:-1]:
        grads[k] = grads[k].reshape(weights[k].shape)
        big_adamw(k, sc_gin[4])
    for k in small_names:
        grads[k] = grads[k].reshape(weights[k].shape)
    packed = [_pack([src[k] for k in small_names]) for src in (weights, grads, mom1, mom2)]
    meta = packed[0][1]
    rows_pad = -packed[0][0].shape[0] % 128
    bufs = [jnp.pad(p[0], ((0, rows_pad), (0, 0)))[None] for p in packed]
    outs = _adamw(bufs[0], bufs[1], bufs[2], bufs[3], "adamw_small", sc_gin[4])
    done.append(outs[2][0, 0:1, :])
    for dst, o in zip((delta, new_m, new_v), outs):
        for k, a in zip(small_names, _unpack(o[0], meta)):
            dst[k] = a
    landed = _split_wait(sc_gin, jnp.concatenate(done, axis=0), _sc_copies, "rs_gin_scatter_wait")[1]
    g_gin_shard = _rs_pair_gather(_rs_final_sum(place, landed, p32_gin, "rs_gin_final_sum"), "rs_gin_pair_gather")
    grads["gla_w_in"] = g_gin_shard[:_GIN_ROWS].T[None]
    big_adamw("gla_w_in", sc_gin[4])

    return (loss, grad_x, *[grads[k] for k in names], *[delta[k] for k in names], *[new_m[k] for k in names],
            *[new_v[k] for k in names])
```

```python
import functools

import jax
import jax.numpy as jnp
from jax import lax
from jax.experimental import pallas as pl
from jax.experimental.pallas import tpu as pltpu

F32 = jnp.float32
BF16 = jnp.bfloat16
MESH = pl.DeviceIdType.MESH

EPS = 1e-6
D = 1024
N_MOD = 6
HEADS = 4
DK = 128
DV = 256
KEY = HEADS * DK
RANK = 16
TAU = 16.0
CH = 64
GRID_W = 64
HID = 2560
GLA_IN = 2 * KEY + 2 * D + 2 * RANK
GLA_IN_PAD = 3200
Q_SCALE = DK ** -0.5
N_CHIPS = 4
N_DEV = 8

ADAM_LR = 0.001
ADAM_B1 = 0.9
ADAM_B2 = 0.999
ADAM_EPS = 1e-08
ADAM_WD = 0.01
ADAM_STEP = 10

VMEM_LIMIT = 56 * 1024 * 1024


def _params(sem):
    return pltpu.CompilerParams(dimension_semantics=sem, vmem_limit_bytes=VMEM_LIMIT)


def _tile(n, pref, mult=8):
    if n <= pref:
        return n
    for t in range(pref - pref % mult, 0, -mult):
        if n % t == 0:
            return t
    raise ValueError((n, pref, mult))


_NN = (((1,), (0,)), ((), ()))
_NT = (((1,), (1,)), ((), ()))
_TN = (((0,), (0,)), ((), ()))


def _dot(a, b, dims=_NN):
    return lax.dot_general(a.astype(BF16), b.astype(BF16), dims, preferred_element_type=F32)


def _sigmoid(x):
    return 1.0 / (1.0 + jnp.exp(-x))


def _rowsum(x):
    return jnp.sum(x, axis=0, keepdims=True)


def _mm(a, b, form, out_dtype, name, tm, tn):
    if form == "tn":
        K, M = a.shape
    else:
        M, K = a.shape
    N = b.shape[0] if form == "nt" else b.shape[1]
    tm = _tile(M, tm, 128)
    tn = _tile(N, tn, 128)
    dims = {"nn": _NN, "nt": _NT, "tn": _TN}[form]

    def body(a_ref, b_ref, o_ref):
        o_ref[...] = _dot(a_ref[...], b_ref[...], dims).astype(o_ref.dtype)

    if form == "tn":
        a_spec = pl.BlockSpec((K, tm), lambda i, j: (0, i))
    else:
        a_spec = pl.BlockSpec((tm, K), lambda i, j: (i, 0))
    if form == "nt":
        b_spec = pl.BlockSpec((tn, K), lambda i, j: (j, 0))
    else:
        b_spec = pl.BlockSpec((K, tn), lambda i, j: (0, j))
    return pl.pallas_call(
        body,
        name=name,
        grid=(M // tm, N // tn),
        in_specs=[a_spec, b_spec],
        out_specs=pl.BlockSpec((tm, tn), lambda i, j: (i, j)),
        out_shape=jax.ShapeDtypeStruct((M, N), out_dtype),
        compiler_params=_params(("parallel", "parallel")),
    )(a, b)


def _mm_nt_w(a, wg, off, rows, name, tm, out_dtype):
    m = a.shape[0]
    tm = _tile(m, tm, 128)
    if N_CHIPS * rows <= D:

        def body_small(a_ref, w_ref, o_ref):
            av = a_ref[...]
            for s in range(N_CHIPS):
                o_ref[:, s * rows:(s + 1) * rows] = _dot(av, w_ref[s], _NT).astype(o_ref.dtype)

        return pl.pallas_call(
            body_small, name=name, grid=(m // tm,),
            in_specs=[pl.BlockSpec((tm, D), lambda i: (i, 0)),
                      pl.BlockSpec((N_CHIPS, rows, D), lambda i: (0, off // rows, 0))],
            out_specs=pl.BlockSpec((tm, N_CHIPS * rows), lambda i: (i, 0)),
            out_shape=jax.ShapeDtypeStruct((m, N_CHIPS * rows), out_dtype),
            compiler_params=_params(("parallel",)),
        )(a, wg)

    def body(a_ref, w_ref, o_ref):
        o_ref[...] = _dot(a_ref[...], w_ref[0], _NT).astype(o_ref.dtype)

    return pl.pallas_call(
        body, name=name, grid=(m // tm, N_CHIPS),
        in_specs=[pl.BlockSpec((tm, D), lambda i, s: (i, 0)),
                  pl.BlockSpec((1, rows, D), lambda i, s: (s, off // rows, 0))],
        out_specs=pl.BlockSpec((tm, rows), lambda i, s: (i, s)),
        out_shape=jax.ShapeDtypeStruct((m, N_CHIPS * rows), out_dtype),
        compiler_params=_params(("parallel", "parallel")),
    )(a, wg)


def _mm_nn_w_mod(a3, wg, off, rows, h, gate, gain, shift, scale, tpb_rows, name, tm):
    parts, m, kp = a3.shape
    assert parts * kp == N_CHIPS * rows
    tm = _tile(tpb_rows, tm, 128)
    tpb = tpb_rows // tm
    cuts = sorted({s * rows for s in range(N_CHIPS + 1)} | {p * kp for p in range(parts + 1)})
    pieces = [(k0 // kp, k0 % kp, k0 // rows, k0 % rows, k1 - k0) for k0, k1 in zip(cuts[:-1], cuts[1:])]

    def body(a_ref, w_ref, h_ref, gate_ref, gain_ref, sh_ref, sc_ref, y_ref, hout_ref, hn_ref):
        acc = None
        for p, a0, s, r0, width in pieces:
            term = _dot(a_ref[p, :, a0:a0 + width], w_ref[s, r0:r0 + width, :])
            acc = term if acc is None else acc + term
        y_ref[...] = acc
        hv = h_ref[...] + gate_ref[0] * acc
        hout_ref[...] = hv
        r = lax.rsqrt(jnp.mean(hv * hv, axis=-1, keepdims=True) + EPS)
        hn_ref[...] = ((hv * r) * gain_ref[...] * (1.0 + sc_ref[0]) + sh_ref[0]).astype(BF16)

    row = pl.BlockSpec((tm, D), lambda i: (i, 0))
    per_b = pl.BlockSpec((1, 1, D), lambda i: (i // tpb, 0, 0))
    return pl.pallas_call(
        body, name=name, grid=(m // tm,),
        in_specs=[pl.BlockSpec((parts, tm, kp), lambda i: (0, i, 0)),
                  pl.BlockSpec((N_CHIPS, rows, D), lambda i: (0, off // rows, 0)),
                  row, per_b, pl.BlockSpec((1, D), lambda i: (0, 0)), per_b, per_b],
        out_specs=[row, row, row],
        out_shape=[jax.ShapeDtypeStruct((m, D), F32), jax.ShapeDtypeStruct((m, D), F32),
                   jax.ShapeDtypeStruct((m, D), BF16)],
        compiler_params=_params(("parallel",)),
    )(a3, wg, h, gate, gain, shift, scale)


def _mm_nn_w_final(a3, wg, off, rows, h, gate, gain, tgt, tpb_rows, name, tm):
    parts, m, kp = a3.shape
    assert parts * kp == N_CHIPS * rows
    nb = m // tpb_rows
    tm = _tile(tpb_rows, tm, 128)
    tpb = tpb_rows // tm
    cuts = sorted({s * rows for s in range(N_CHIPS + 1)} | {p * kp for p in range(parts + 1)})
    pieces = [(k0 // kp, k0 % kp, k0 // rows, k0 % rows, k1 - k0) for k0, k1 in zip(cuts[:-1], cuts[1:])]

    def body(a_ref, w_ref, h_ref, gate_ref, gain_ref, tgt_ref, loss_ref, dh_ref, df_ref, dgate_ref, dgain_ref):
        i = pl.program_id(0)

        @pl.when(i == 0)
        def _():
            loss_ref[...] = jnp.zeros_like(loss_ref)
            dgain_ref[...] = jnp.zeros_like(dgain_ref)

        @pl.when(i % tpb == 0)
        def _():
            dgate_ref[...] = jnp.zeros_like(dgate_ref)

        fv = None
        for p, a0, s, r0, width in pieces:
            term = _dot(a_ref[p, :, a0:a0 + width], w_ref[s, r0:r0 + width, :])
            fv = term if fv is None else fv + term
        gate_v = gate_ref[0]
        hv = h_ref[...] + gate_v * fv
        r = lax.rsqrt(jnp.mean(hv * hv, axis=-1, keepdims=True) + EPS)
        y = hv * r
        gain_v = gain_ref[...]
        e = y * gain_v - tgt_ref[...]
        s_ = jnp.sum(_rowsum(e * e), axis=1, keepdims=True) * (0.5 / D)
        loss_ref[...] += jnp.broadcast_to(s_, loss_ref.shape)
        dout = e * (1.0 / D)
        dgain_ref[...] += _rowsum(dout * y)
        dy = dout * gain_v
        dh = r * (dy - y * jnp.mean(dy * y, axis=-1, keepdims=True))
        dh_ref[...] = dh
        df_ref[...] = (dh * gate_v).astype(BF16)
        dgate_ref[0] += _rowsum(dh * fv)

    row = pl.BlockSpec((tm, D), lambda i: (i, 0))
    per_b = pl.BlockSpec((1, 1, D), lambda i: (i // tpb, 0, 0))
    vec = pl.BlockSpec((1, D), lambda i: (0, 0))
    return pl.pallas_call(
        body, name=name, grid=(m // tm,),
        in_specs=[pl.BlockSpec((parts, tm, kp), lambda i: (0, i, 0)),
                  pl.BlockSpec((N_CHIPS, rows, D), lambda i: (0, off // rows, 0)), row, per_b, vec, row],
        out_specs=[pl.BlockSpec((1, 128), lambda i: (0, 0)), row, row, per_b, vec],
        out_shape=[jax.ShapeDtypeStruct((1, 128), F32), jax.ShapeDtypeStruct((m, D), F32),
                   jax.ShapeDtypeStruct((m, D), BF16), jax.ShapeDtypeStruct((nb, 1, D), F32),
                   jax.ShapeDtypeStruct((1, D), F32)],
        compiler_params=_params(("arbitrary",)),
    )(a3, wg, h, gate, gain, tgt)


def _mm_nn_w_modbwd(a3, wg, off, rows, h_in, dh_out, gain, scale, y_prev, gate_prev, tpb_rows, name, tm):
    parts, m, kp = a3.shape
    assert parts * kp == N_CHIPS * rows
    nb = m // tpb_rows
    tm = _tile(tpb_rows, tm, 128)
    tpb = tpb_rows // tm
    cuts = sorted({s * rows for s in range(N_CHIPS + 1)} | {p * kp for p in range(parts + 1)})
    pieces = [(k0 // kp, k0 % kp, k0 // rows, k0 % rows, k1 - k0) for k0, k1 in zip(cuts[:-1], cuts[1:])]

    def body(a_ref, w_ref, h_ref, gain_ref, sc_ref, dho_ref, yp_ref, gp_ref,
             dh_ref, dsc_ref, dsh_ref, dgain_ref, dyp_ref, dgp_ref):
        i = pl.program_id(0)

        @pl.when(i == 0)
        def _():
            dgain_ref[...] = jnp.zeros_like(dgain_ref)

        @pl.when(i % tpb == 0)
        def _():
            dsc_ref[...] = jnp.zeros_like(dsc_ref)
            dsh_ref[...] = jnp.zeros_like(dsh_ref)
            dgp_ref[...] = jnp.zeros_like(dgp_ref)

        g = None
        for p, a0, s, r0, width in pieces:
            term = _dot(a_ref[p, :, a0:a0 + width], w_ref[s, r0:r0 + width, :])
            g = term if g is None else g + term
        hv = h_ref[...]
        r = lax.rsqrt(jnp.mean(hv * hv, axis=-1, keepdims=True) + EPS)
        y = hv * r
        gain_v = gain_ref[...]
        dsh_ref[0] += _rowsum(g)
        dsc_ref[0] += _rowsum(g * (y * gain_v))
        drn = g * (1.0 + sc_ref[0])
        dgain_ref[...] += _rowsum(drn * y)
        dy = drn * gain_v
        dh = r * (dy - y * jnp.mean(dy * y, axis=-1, keepdims=True)) + dho_ref[...]
        dh_ref[...] = dh
        dyp_ref[...] = (dh * gp_ref[0]).astype(BF16)
        dgp_ref[0] += _rowsum(dh * yp_ref[...])

    row = pl.BlockSpec((tm, D), lambda i: (i, 0))
    per_b = pl.BlockSpec((1, 1, D), lambda i: (i // tpb, 0, 0))
    vec = pl.BlockSpec((1, D), lambda i: (0, 0))
    per_b_shape = jax.ShapeDtypeStruct((nb, 1, D), F32)
    outs = pl.pallas_call(
        body, name=name, grid=(m // tm,),
        in_specs=[pl.BlockSpec((parts, tm, kp), lambda i: (0, i, 0)),
                  pl.BlockSpec((N_CHIPS, rows, D), lambda i: (0, off // rows, 0)),
                  row, vec, per_b, row, row, per_b],
        out_specs=[row, per_b, per_b, vec, row, per_b],
        out_shape=[jax.ShapeDtypeStruct((m, D), F32), per_b_shape, per_b_shape, jax.ShapeDtypeStruct((1, D), F32),
                   jax.ShapeDtypeStruct((m, D), BF16), per_b_shape],
        compiler_params=_params(("arbitrary",)),
    )(a3, wg, h_in, gain, scale, dh_out, y_prev, gate_prev)
    return dict(zip(("dh", "dscale", "dshift", "dgain", "dy_prev", "dgate_prev"), outs))


def _mm_dw(a3, b, g_prev, off, rows, name, tm):
    parts, ntok, cdim = a3.shape
    assert parts * cdim == N_CHIPS * rows and cdim % tm == 0 and rows % tm == 0 and off % tm == 0

    def body(a_ref, b_ref, *rest):
        rest[-1][0] = _dot(a_ref[0], b_ref[...], _TN)

    in_specs = [pl.BlockSpec((1, ntok, tm), lambda i: ((i * tm) // cdim, 0, ((i * tm) % cdim) // tm)),
                pl.BlockSpec((ntok, D), lambda i: (0, 0))]
    args = [a3, b]
    aliases = {}
    if g_prev is not None:
        in_specs.append(pl.BlockSpec(memory_space=pl.ANY))
        args.append(g_prev)
        aliases = {2: 0}
    return pl.pallas_call(
        body, name=name, grid=(N_CHIPS * rows // tm,),
        in_specs=in_specs,
        out_specs=pl.BlockSpec((1, tm, D), lambda i: ((i * tm) // rows, (off + (i * tm) % rows) // tm, 0)),
        out_shape=jax.ShapeDtypeStruct((N_CHIPS, _MAIN_TOTAL, D), F32),
        input_output_aliases=aliases,
        compiler_params=_params(("parallel",)),
    )(*args)


def _mod_fwd(h, gain, shift, scale, tpb_rows, name, y=None, gate=None, out_rows=None, into=None, row0=0):
    n = h.shape[0]
    tt = _tile(tpb_rows, 256)
    tpb = tpb_rows // tt
    has_res = y is not None
    assert row0 % tt == 0 and not (has_res and out_rows)

    def body(*refs):
        if has_res:
            h_ref, y_ref, gate_ref, gain_ref, sh_ref, sc_ref, hout_ref, hn_ref = refs
            hv = h_ref[...] + gate_ref[0] * y_ref[...]
            hout_ref[...] = hv
        else:
            h_ref, gain_ref, sh_ref, sc_ref, hn_ref = refs[0], refs[1], refs[2], refs[3], refs[-1]
            hv = h_ref[...]
        r = lax.rsqrt(jnp.mean(hv * hv, axis=-1, keepdims=True) + EPS)
        hn = (hv * r) * gain_ref[...] * (1.0 + sc_ref[0]) + sh_ref[0]
        hn_ref[...] = hn.astype(BF16)

    row = pl.BlockSpec((tt, D), lambda i: (i, 0))
    per_b = pl.BlockSpec((1, 1, D), lambda i: (i // tpb, 0, 0))
    vec = pl.BlockSpec((1, D), lambda i: (0, 0))
    if has_res:
        in_specs = [row, row, per_b, vec, per_b, per_b]
        args = (h, y, gate, gain, shift, scale)
        out_specs = [row, row]
        out_shape = [jax.ShapeDtypeStruct((n, D), F32), jax.ShapeDtypeStruct((n, D), BF16)]
    else:
        in_specs = [row, vec, per_b, per_b]
        args = (h, gain, shift, scale)
        out_specs = pl.BlockSpec((tt, D), lambda i: (i + row0 // tt, 0))
        out_shape = jax.ShapeDtypeStruct((out_rows or n, D), BF16)
    aliases = {}
    if into is not None:
        in_specs = in_specs + [pl.BlockSpec(memory_space=pl.ANY)]
        args = args + (into,)
        aliases = {4: 0}
    return pl.pallas_call(
        body, name=name, grid=(n // tt,), in_specs=in_specs, out_specs=out_specs, out_shape=out_shape,
        input_output_aliases=aliases, compiler_params=_params(("parallel",)),
    )(*args)


def _mod_bwd(h_in, dhn, gain, scale, tpb_rows, name, dhn_row0=0, dh_out=None, y_prev=None, gate_prev=None,
             need_dh=True):
    n = h_in.shape[0]
    nb = n // tpb_rows
    tt = _tile(tpb_rows, 256)
    tpb = tpb_rows // tt
    off = dhn_row0 // tt
    assert dhn_row0 % tt == 0
    has_out = dh_out is not None
    has_prev = y_prev is not None

    def body(*refs):
        it = iter(refs)
        h_ref, dhn_ref, gain_ref, sc_ref = next(it), next(it), next(it), next(it)
        dho_ref = next(it) if has_out else None
        yp_ref, gp_ref = (next(it), next(it)) if has_prev else (None, None)
        dh_ref = next(it) if need_dh else None
        dsc_ref, dsh_ref, dgain_ref = next(it), next(it), next(it)
        dyp_ref, dgp_ref = (next(it), next(it)) if has_prev else (None, None)
        i = pl.program_id(0)

        @pl.when(i == 0)
        def _():
            dgain_ref[...] = jnp.zeros_like(dgain_ref)

        @pl.when(i % tpb == 0)
        def _():
            dsc_ref[...] = jnp.zeros_like(dsc_ref)
            dsh_ref[...] = jnp.zeros_like(dsh_ref)
            if has_prev:
                dgp_ref[...] = jnp.zeros_like(dgp_ref)

        hv = h_ref[...]
        r = lax.rsqrt(jnp.mean(hv * hv, axis=-1, keepdims=True) + EPS)
        y = hv * r
        gain_v = gain_ref[...]
        g = dhn_ref[...].astype(F32)
        dsh_ref[0] += _rowsum(g)
        dsc_ref[0] += _rowsum(g * (y * gain_v))
        drn = g * (1.0 + sc_ref[0])
        dgain_ref[...] += _rowsum(drn * y)
        if need_dh:
            dy = drn * gain_v
            dh = r * (dy - y * jnp.mean(dy * y, axis=-1, keepdims=True))
            if has_out:
                dh = dh + dho_ref[...]
            dh_ref[...] = dh
            if has_prev:
                dyp_ref[...] = (dh * gp_ref[0]).astype(BF16)
                dgp_ref[0] += _rowsum(dh * yp_ref[...])

    row = pl.BlockSpec((tt, D), lambda i: (i, 0))
    row_off = pl.BlockSpec((tt, D), lambda i: (i + off, 0))
    per_b = pl.BlockSpec((1, 1, D), lambda i: (i // tpb, 0, 0))
    vec = pl.BlockSpec((1, D), lambda i: (0, 0))
    in_specs = [row, row_off, vec, per_b]
    args = [h_in, dhn, gain, scale]
    if has_out:
        in_specs.append(row)
        args.append(dh_out)
    if has_prev:
        in_specs += [row, per_b]
        args += [y_prev, gate_prev]
    out_specs, out_shape, names = [], [], []
    if need_dh:
        out_specs.append(row)
        out_shape.append(jax.ShapeDtypeStruct((n, D), F32))
        names.append("dh")
    for nm in ("dscale", "dshift"):
        out_specs.append(per_b)
        out_shape.append(jax.ShapeDtypeStruct((nb, 1, D), F32))
        names.append(nm)
    out_specs.append(vec)
    out_shape.append(jax.ShapeDtypeStruct((1, D), F32))
    names.append("dgain")
    if has_prev:
        out_specs += [row, per_b]
        out_shape += [jax.ShapeDtypeStruct((n, D), BF16), jax.ShapeDtypeStruct((nb, 1, D), F32)]
        names += ["dy_prev", "dgate_prev"]
    outs = pl.pallas_call(
        body, name=name, grid=(n // tt,), in_specs=in_specs, out_specs=out_specs, out_shape=out_shape,
        compiler_params=_params(("arbitrary",)),
    )(*args)
    return dict(zip(names, outs))


def _row_dn1(x):
    t = lax.broadcasted_iota(jnp.int32, x.shape, 0)
    return jnp.where(t % GRID_W == 0, 0.0, pltpu.roll(x, 1, 0))


def _row_up1(x):
    t = lax.broadcasted_iota(jnp.int32, x.shape, 0)
    return jnp.where(t % GRID_W == GRID_W - 1, 0.0, pltpu.roll(x, x.shape[0] - 1, 0))


def _silu(x):
    return x * _sigmoid(x)


def _dsilu(x):
    s = _sigmoid(x)
    return s * (1.0 + x * (1.0 - s))


def _row_ds(i):
    start = i * GRID_W
    return pl.ds(start if isinstance(start, int) else pl.multiple_of(start, GRID_W), GRID_W)


def _grid_row(ref, i, first, last):
    def rows(k):
        return ref[_row_ds(k), :].astype(F32)

    cur = rows(i)
    return (jnp.zeros_like(cur) if first else rows(i - 1)), cur, (jnp.zeros_like(cur) if last else rows(i + 1))


def _over_grid_rows(n_rows, step, carry):
    carry = step(0, carry, True, n_rows == 1)
    if n_rows > 2:
        carry = lax.fori_loop(1, n_rows - 1, lambda i, c: step(i, c, False, False), carry)
    if n_rows > 1:
        carry = step(n_rows - 1, carry, False, True)
    return carry


def _fold8(p):
    return p.reshape(GRID_W // 8, 8, p.shape[1]).sum(axis=0)


def _ffn_up_mid_fwd(hn, wg, off, cw, cb, nb, t, name):
    tcol = 256
    ncol = HID // tcol
    rows_sh = 2 * HID // N_CHIPS

    def conv(x, w_ref):
        zeros = jnp.zeros((GRID_W, x.shape[1]), x.dtype)
        down = jnp.concatenate([zeros, x[: x.shape[0] - GRID_W]], axis=0)
        up = jnp.concatenate([x[GRID_W:], zeros], axis=0)
        return down * w_ref[0:1, :] + x * w_ref[1:2, :] + up * w_ref[2:3, :]

    def body(h_ref, wa_ref, wg_ref, cwa_ref, cwg_ref, cba_ref, cbg_ref, u_ref, z_ref):
        hv = h_ref[...]
        ua = _dot(hv, wa_ref[0], _NT)
        ug = _dot(hv, wg_ref[0], _NT)
        u_ref[0] = ua.astype(BF16)
        u_ref[1] = ug.astype(BF16)
        a = conv(ua, cwa_ref) + cba_ref[...]
        gt = conv(ug, cwg_ref) + cbg_ref[...]
        z_ref[...] = (a * _silu(gt)).astype(BF16)

    def w_spec(part):
        def idx(b, j):
            n = part * HID + j * tcol
            return (n // rows_sh, (off + n % rows_sh) // tcol, 0)
        return pl.BlockSpec((1, tcol, D), idx)

    chan = lambda rows, part: pl.BlockSpec((rows, tcol), lambda b, j: (0, part * ncol + j))
    return pl.pallas_call(
        body, name=name, grid=(nb, ncol),
        in_specs=[pl.BlockSpec((t, D), lambda b, j: (b, 0)), w_spec(0), w_spec(1),
                  chan(3, 0), chan(3, 1), chan(1, 0), chan(1, 1)],
        out_specs=[pl.BlockSpec((2, t, tcol), lambda b, j: (0, b, j)), pl.BlockSpec((t, tcol), lambda b, j: (b, j))],
        out_shape=[jax.ShapeDtypeStruct((2, nb * t, HID), BF16), jax.ShapeDtypeStruct((nb * t, HID), BF16)],
        compiler_params=_params(("parallel", "parallel")),
    )(hn, wg, wg, cw, cw, cb, cb)


def _ffn_mid_bwd(u0, cw, cb, dz, nb, t, name):
    nc = HID // 128
    n_rows = t // GRID_W

    def body(ua3_ref, ug3_ref, wa_ref, wg_ref, ba_ref, bg_ref, dz_ref, du_ref, dw_ref, db_ref, dua_ref, dug_ref):
        ua_ref, ug_ref = ua3_ref.at[0], ug3_ref.at[0]
        b = pl.program_id(1)

        @pl.when(b == 0)
        def _():
            dw_ref[...] = jnp.zeros_like(dw_ref)
            db_ref[...] = jnp.zeros_like(db_ref)

        wa = [wa_ref[k:k + 1, :] for k in range(3)]
        wg = [wg_ref[k:k + 1, :] for k in range(3)]
        ba, bg = ba_ref[...], bg_ref[...]

        def pass1(i, acc, first, last):
            here = _row_ds(i)
            ap, ac, an = _grid_row(ua_ref, i, first, last)
            gp, gc, gn = _grid_row(ug_ref, i, first, last)
            a = ap * wa[0] + ac * wa[1] + an * wa[2] + ba
            gt = gp * wg[0] + gc * wg[1] + gn * wg[2] + bg
            dzv = dz_ref[here, :].astype(F32)
            s = _sigmoid(gt)
            silu = gt * s
            da = dzv * silu
            dg = (dzv * a) * (s + silu * (1.0 - s))
            dua_ref[here, :] = da
            dug_ref[here, :] = dg
            terms = (da, da * ap, da * ac, da * an, dg, dg * gp, dg * gc, dg * gn)
            return tuple(r + _fold8(p) for r, p in zip(acc, terms))

        zero = jnp.zeros((8, 128), F32)
        acc = _over_grid_rows(n_rows, pass1, (zero,) * 8)
        for part in range(2):
            db_ref[part] += _rowsum(acc[4 * part])
            for k in range(3):
                dw_ref[part, k:k + 1, :] += _rowsum(acc[4 * part + 1 + k])

        def pass2(i, carry, first, last):
            for part, (ref, w) in enumerate(((dua_ref, wa), (dug_ref, wg))):
                dp_, dc_, dn_ = _grid_row(ref, i, first, last)
                du_ref[part, _row_ds(i), :] = (dn_ * w[0] + dc_ * w[1] + dp_ * w[2]).astype(BF16)
            return carry

        _over_grid_rows(n_rows, pass2, 0)

    col = lambda rows, part: pl.BlockSpec((rows, 128), lambda j, b: (0, part * nc + j))
    part_of_u = lambda part: pl.BlockSpec((1, t, 128), lambda j, b: (part, b, j))
    return pl.pallas_call(
        body, name=name, grid=(nc, nb),
        in_specs=[part_of_u(0), part_of_u(1), col(3, 0), col(3, 1), col(1, 0), col(1, 1),
                  pl.BlockSpec((t, 128), lambda j, b: (b, j))],
        out_specs=[pl.BlockSpec((2, t, 128), lambda j, b: (0, b, j)), pl.BlockSpec((2, 3, 128), lambda j, b: (0, 0, j)),
                   pl.BlockSpec((2, 1, 128), lambda j, b: (0, 0, j))],
        out_shape=[jax.ShapeDtypeStruct((2, nb * t, HID), BF16), jax.ShapeDtypeStruct((2, 3, HID), F32),
                   jax.ShapeDtypeStruct((2, 1, HID), F32)],
        scratch_shapes=[pltpu.VMEM((t, 128), F32), pltpu.VMEM((t, 128), F32)],
        compiler_params=_params(("parallel", "arbitrary")),
    )(u0, u0, cw, cw, cb, cb, dz)


def _sc_in_mid_fwd(hn, wg, off, cw, nb, t):
    tcol = 256
    ncol = D // tcol
    rows_sh = 3 * D // N_CHIPS

    def body(h_ref, wb_ref, wc_ref, wv_ref, cw_ref, p_ref, y_ref):
        hv = h_ref[...]
        bg = _dot(hv, wb_ref[0], _NT)
        cg = _dot(hv, wc_ref[0], _NT)
        v = _dot(hv, wv_ref[0], _NT)
        p_ref[0] = bg.astype(BF16)
        p_ref[1] = cg.astype(BF16)
        p_ref[2] = v.astype(BF16)
        cv = cg * v
        cc = _row_dn1(cv) * cw_ref[0:1, :] + cv * cw_ref[1:2, :] + _row_up1(cv) * cw_ref[2:3, :]
        y_ref[...] = (bg * cc).astype(BF16)

    def w_spec(part):
        def idx(b, j):
            n = part * D + j * tcol
            return (n // rows_sh, (off + n % rows_sh) // tcol, 0)
        return pl.BlockSpec((1, tcol, D), idx)

    return pl.pallas_call(
        body, name="sc_in_mid", grid=(nb, ncol),
        in_specs=[pl.BlockSpec((t, D), lambda b, j: (b, 0)), w_spec(0), w_spec(1), w_spec(2),
                  pl.BlockSpec((3, tcol), lambda b, j: (0, j))],
        out_specs=[pl.BlockSpec((3, t, tcol), lambda b, j: (0, b, j)), pl.BlockSpec((t, tcol), lambda b, j: (b, j))],
        out_shape=[jax.ShapeDtypeStruct((3, nb * t, D), BF16), jax.ShapeDtypeStruct((nb * t, D), BF16)],
        compiler_params=_params(("parallel", "parallel")),
    )(hn, wg, wg, wg, cw)


def _sc_mid_bwd(p, cw, dyb, nb, t):
    nc = D // 128

    def body(bg3_ref, cg3_ref, v3_ref, w_ref, dy_ref, dp_ref, dw_ref):
        bg_ref, cg_ref, v_ref = bg3_ref.at[0], cg3_ref.at[0], v3_ref.at[0]
        b = pl.program_id(1)

        @pl.when(b == 0)
        def _():
            dw_ref[...] = jnp.zeros_like(dw_ref)

        w0, w1, w2 = w_ref[0:1, :], w_ref[1:2, :], w_ref[2:3, :]
        cg, v = cg_ref[...].astype(F32), v_ref[...].astype(F32)
        cv = cg * v
        cvd = _row_dn1(cv)
        cvu = _row_up1(cv)
        cc = cvd * w0 + cv * w1 + cvu * w2
        dy = dy_ref[...].astype(F32)
        dcc = dy * bg_ref[...].astype(F32)
        dw_ref[0:1, :] += _rowsum(dcc * cvd)
        dw_ref[1:2, :] += _rowsum(dcc * cv)
        dw_ref[2:3, :] += _rowsum(dcc * cvu)
        dcv = _row_up1(dcc) * w0 + dcc * w1 + _row_dn1(dcc) * w2
        dp_ref[0] = (dy * cc).astype(BF16)
        dp_ref[1] = (dcv * v).astype(BF16)
        dp_ref[2] = (dcv * cg).astype(BF16)

    part = lambda k: pl.BlockSpec((1, t, 128), lambda j, b: (k, b, j))
    return pl.pallas_call(
        body, name="sc_mid_bwd", grid=(nc, nb),
        in_specs=[part(0), part(1), part(2), pl.BlockSpec((3, 128), lambda j, b: (0, j)),
                  pl.BlockSpec((t, 128), lambda j, b: (b, j))],
        out_specs=[pl.BlockSpec((3, t, 128), lambda j, b: (0, b, j)), pl.BlockSpec((3, 128), lambda j, b: (0, j))],
        out_shape=[jax.ShapeDtypeStruct((3, nb * t, D), BF16), jax.ShapeDtypeStruct((3, D), F32)],
        compiler_params=_params(("parallel", "arbitrary")),
    )(p, p, p, cw, dyb)


def _gla_decay_fwd(p_all, w2, b2):
    n = p_all.shape[0]
    tt = _tile(n, 512)

    def body(a_ref, w_ref, b_ref, la_ref):
        z = _dot(a_ref[...], w_ref[...]) + b_ref[...]
        la_ref[...] = (jnp.minimum(z, 0.0) - jnp.log(1.0 + jnp.exp(-jnp.abs(z)))) * (1.0 / TAU)

    return pl.pallas_call(
        body, name="gla_decay_fwd", grid=(n // tt,),
        in_specs=[pl.BlockSpec((tt, 128), lambda i: (i, (2 * KEY + 2 * D) // 128)),
                  pl.BlockSpec((128, 2 * KEY), lambda i: (0, 0)), pl.BlockSpec((1, 2 * KEY), lambda i: (0, 0))],
        out_specs=pl.BlockSpec((tt, 2 * KEY), lambda i: (i, 0)),
        out_shape=jax.ShapeDtypeStruct((n, 2 * KEY), F32),
        compiler_params=_params(("parallel",)),
    )(p_all, w2, b2)


def _gla_blocks(nb, nm, ncx):
    def main_idx(d, i):
        return jnp.clip(jnp.where(d == 0, i - ncx, nm - 1 - (i - ncx)), 0, nm - 1)

    def rowblk(d, b, i):
        cidx = jnp.where(d == 0, i, ncx - 1 - i)
        return jnp.where(i < ncx, nb * nm + b * ncx + cidx, b * nm + main_idx(d, i))

    def mainblk(d, b, i):
        return b * nm + main_idx(d, i)

    return rowblk, mainblk


def _gla_mask(d):
    row = lax.broadcasted_iota(jnp.int32, (CH, CH), 0)
    col = lax.broadcasted_iota(jnp.int32, (CH, CH), 1)
    diff = jnp.where(d == 0, row - col, col - row)
    mask = diff >= 0
    return mask, jnp.where(mask, 1.0, 0.0).astype(BF16), jnp.where(diff <= 0, 1.0, 0.0).astype(BF16)


def _tri_sum(m01, x):
    w = x.shape[1]
    hi = x.astype(BF16)
    r1 = x - hi.astype(F32)
    mid = r1.astype(BF16)
    lo = (r1 - mid.astype(F32)).astype(BF16)
    s = lax.dot_general(m01, jnp.concatenate([hi, mid, lo], axis=1), _NN, preferred_element_type=F32)
    return s[:, :w] + s[:, w:2 * w] + s[:, 2 * w:]


def _gla_chunk(q, k, g, bc):
    bl = _rowsum(g)
    eq = jnp.exp(bc)
    ek = jnp.exp(-bc)
    ed = jnp.exp(bl - bc)
    return bl, eq, ek, ed, q * Q_SCALE * eq, k * ek, k * ed


def _gla_scan_fwd(p_all, la_all, nb, t, tc):
    nm, ncx = t // CH, tc // CH
    nst = nm + ncx
    rowblk, mainblk = _gla_blocks(nb, nm, ncx)

    def body(*refs):
        ins, (o_refs, ss_refs, st_ref) = refs[:8], (refs[8:10], refs[10:12], refs[12])
        i = pl.program_id(1)

        @pl.when(i == 0)
        def _():
            st_ref[...] = jnp.zeros_like(st_ref)

        loaded = [r[...] for r in ins]
        states = [st_ref[j] for j in range(2 * HEADS)]
        outs, new_states = [[], []], []
        for d in range(2):
            q_all, k_all, v_all, g_all = loaded[4 * d:4 * d + 4]
            mask, m01, _ = _gla_mask(d)
            bc_all = _tri_sum(m01, g_all)
            for h in range(HEADS):
                ksl = slice(h * DK, (h + 1) * DK)
                v = v_all[:, h * DV:(h + 1) * DV]
                st = states[d * HEADS + h]
                bl, _, _, _, qs, ks, kd = _gla_chunk(q_all[:, ksl], k_all[:, ksl], g_all[:, ksl], bc_all[:, ksl])
                att = jnp.where(mask, _dot(qs, ks, _NT), 0.0)
                outs[d].append(_dot(qs, st, _NT) + _dot(att, v))
                new_states.append(st * jnp.exp(bl) + _dot(v, kd, _TN))
        for d in range(2):
            o_refs[d][...] = jnp.concatenate(outs[d], axis=1)
            for h in range(HEADS):
                ss_refs[d][0, 0, h] = states[d * HEADS + h]
                st_ref[d * HEADS + h] = new_states[d * HEADS + h]

    def in_specs(d):
        return [pl.BlockSpec((CH, KEY), lambda b, i: (rowblk(d, b, i), 0)),
                pl.BlockSpec((CH, KEY), lambda b, i: (rowblk(d, b, i), 1)),
                pl.BlockSpec((CH, D), lambda b, i: (rowblk(d, b, i), 1)),
                pl.BlockSpec((CH, KEY), lambda b, i: (rowblk(d, b, i), d))]

    outs = pl.pallas_call(
        body, name="gla_scan_fwd", grid=(nb, nst),
        in_specs=in_specs(0) + in_specs(1),
        out_specs=[pl.BlockSpec((CH, D), lambda b, i: (mainblk(0, b, i), 0)),
                   pl.BlockSpec((CH, D), lambda b, i: (mainblk(1, b, i), 0)),
                   pl.BlockSpec((1, 1, HEADS, DV, DK), lambda b, i: (b, i, 0, 0, 0)),
                   pl.BlockSpec((1, 1, HEADS, DV, DK), lambda b, i: (b, i, 0, 0, 0))],
        out_shape=[jax.ShapeDtypeStruct((nb * t, D), F32)] * 2
        + [jax.ShapeDtypeStruct((nb, nst, HEADS, DV, DK), F32)] * 2,
        scratch_shapes=[pltpu.VMEM((2 * HEADS, DV, DK), F32)],
        compiler_params=_params(("parallel", "arbitrary")),
    )(*([p_all, p_all, p_all, la_all] * 2))
    return outs[:2], outs[2:]


def _gla_scan_bwd(p_all, la_all, do, ss, nb, t, tc, after):
    nm, ncx = t // CH, tc // CH
    nst = nm + ncx
    ntot = nb * (t + tc)
    rowblk, mainblk = _gla_blocks(nb, nm, ncx)

    def body(*refs):
        ins, outs, dst_ref = refs[:12], refs[13:21], refs[21]
        ip = pl.program_id(1)
        i = nst - 1 - ip

        @pl.when(ip == 0)
        def _():
            dst_ref[...] = jnp.zeros_like(dst_ref)

        live = jnp.where(i >= ncx, 1.0, 0.0)
        loaded = [[r[...] for r in ins[6 * d:6 * d + 5]] for d in range(2)]
        states = [ins[6 * d + 5][0, 0, h] for d in range(2) for h in range(HEADS)]
        dstates = [dst_ref[j] for j in range(2 * HEADS)]
        results, new_dstates = [], []
        for d in range(2):
            q_all, k_all, v_all, g_all, do_all = loaded[d]
            do_all = do_all * live
            mask, m01, m01_t = _gla_mask(d)
            bc_all = _tri_sum(m01, g_all)
            dqs_l, dks_l, dvs_l, dbs_l, dbls_l = [], [], [], [], []
            for h in range(HEADS):
                ksl = slice(h * DK, (h + 1) * DK)
                vsl = slice(h * DV, (h + 1) * DV)
                bl, eq, ek, ed, qs, ks, kd = _gla_chunk(q_all[:, ksl], k_all[:, ksl], g_all[:, ksl], bc_all[:, ksl])
                st, dst, v, dov = states[d * HEADS + h], dstates[d * HEADS + h], v_all[:, vsl], do_all[:, vsl]
                att = jnp.where(mask, _dot(qs, ks, _NT), 0.0)
                datt = jnp.where(mask, _dot(dov, v, _NT), 0.0)
                dqs = _dot(dov, st) + _dot(datt, ks)
                dks = _dot(datt, qs, _TN)
                dvs_l.append(_dot(att, dov, _TN) + _dot(kd, dst, _NT))
                dkd = _dot(v, dst)
                e = jnp.exp(bl)
                dbls_l.append(e * _rowsum(st * dst) + _rowsum(dkd * kd))
                new_dstates.append(_dot(dov, qs, _TN) + dst * e)
                dqs_l.append(dqs * eq * Q_SCALE)
                dks_l.append(dks * ek + dkd * ed)
                dbs_l.append(dqs * qs - dks * ks - dkd * kd)
            results.append((jnp.concatenate(dqs_l, axis=1), jnp.concatenate(dks_l, axis=1),
                            jnp.concatenate(dvs_l, axis=1),
                            _tri_sum(m01_t, jnp.concatenate(dbs_l, axis=1)) + jnp.concatenate(dbls_l, axis=1)))
        for d in range(2):
            for k in range(4):
                outs[4 * d + k][...] = results[d][k]
        for j in range(2 * HEADS):
            dst_ref[j] = new_dstates[j]

    def in_specs(d):
        return [pl.BlockSpec((CH, KEY), lambda b, ip: (rowblk(d, b, nst - 1 - ip), 0)),
                pl.BlockSpec((CH, KEY), lambda b, ip: (rowblk(d, b, nst - 1 - ip), 1)),
                pl.BlockSpec((CH, D), lambda b, ip: (rowblk(d, b, nst - 1 - ip), 1)),
                pl.BlockSpec((CH, KEY), lambda b, ip: (rowblk(d, b, nst - 1 - ip), d)),
                pl.BlockSpec((CH, D), lambda b, ip: (mainblk(d, b, nst - 1 - ip), 0)),
                pl.BlockSpec((1, 1, HEADS, DV, DK), lambda b, ip: (b, nst - 1 - ip, 0, 0, 0))]

    def out_specs(d):
        row = lambda width: pl.BlockSpec((CH, width), lambda b, ip: (rowblk(d, b, nst - 1 - ip), 0))
        return [row(KEY), row(KEY), row(D), row(KEY)]

    shapes = [jax.ShapeDtypeStruct((ntot, KEY), F32), jax.ShapeDtypeStruct((ntot, KEY), F32),
              jax.ShapeDtypeStruct((ntot, D), F32), jax.ShapeDtypeStruct((ntot, KEY), F32)]
    outs = pl.pallas_call(
        body, name="gla_scan_bwd", grid=(nb, nst),
        in_specs=in_specs(0) + in_specs(1) + [pl.BlockSpec(memory_space=pl.ANY)],
        out_specs=out_specs(0) + out_specs(1),
        out_shape=shapes * 2,
        scratch_shapes=[pltpu.VMEM((2 * HEADS, DV, DK), F32)],
        compiler_params=_params(("parallel", "arbitrary")),
    )(p_all, p_all, p_all, la_all, do, ss[0], p_all, p_all, p_all, la_all, do, ss[1], after)
    return [[outs[k], outs[4 + k]] for k in range(4)]


def _gla_post_fwd(o2, p_all, head_gain, n):
    tt = _tile(n, 256)

    def body(of_ref, ob_ref, g_ref, hg_ref, y_ref):
        o = of_ref[...] + ob_ref[...]
        gv = g_ref[...]
        hg = hg_ref[...]
        for h in range(HEADS):
            oh = o[:, h * DV:(h + 1) * DV]
            r = lax.rsqrt(jnp.mean(oh * oh, axis=-1, keepdims=True) + EPS)
            y_ref[:, h * DV:(h + 1) * DV] = ((oh * r) * hg * _silu(gv[:, h * DV:(h + 1) * DV])).astype(BF16)

    row = pl.BlockSpec((tt, D), lambda i: (i, 0))
    return pl.pallas_call(
        body, name="gla_post_fwd", grid=(n // tt,),
        in_specs=[row, row, pl.BlockSpec((tt, D), lambda i: (i, 2)), pl.BlockSpec((1, DV), lambda i: (0, 0))],
        out_specs=row,
        out_shape=jax.ShapeDtypeStruct((n, D), BF16),
        compiler_params=_params(("parallel",)),
    )(o2[0], o2[1], p_all, head_gain)


def _gla_post_bwd(o2, p_all, head_gain, dyb, n):
    tt = _tile(n, 256)

    def body(of_ref, ob_ref, g_ref, hg_ref, dy_ref, do_ref, dg_ref, dhg_ref):
        i = pl.program_id(0)

        @pl.when(i == 0)
        def _():
            dhg_ref[...] = jnp.zeros_like(dhg_ref)

        o = of_ref[...] + ob_ref[...]
        gv = g_ref[...]
        hg = hg_ref[...]
        dy = dy_ref[...]
        acc = jnp.zeros((1, DV), F32)
        for h in range(HEADS):
            sl = slice(h * DV, (h + 1) * DV)
            oh = o[:, sl]
            r = lax.rsqrt(jnp.mean(oh * oh, axis=-1, keepdims=True) + EPS)
            on = oh * r
            gh = gv[:, sl]
            dyh = dy[:, sl]
            dg_ref[:, sl] = dyh * (on * hg) * _dsilu(gh)
            dog = dyh * _silu(gh)
            acc = acc + _rowsum(dog * on)
            don = dog * hg
            do_ref[:, sl] = r * (don - on * jnp.mean(don * on, axis=-1, keepdims=True))
        dhg_ref[...] += acc

    return pl.pallas_call(
        body, name="gla_post_bwd", grid=(n // tt,),
        in_specs=[pl.BlockSpec((tt, D), lambda i: (i, 0)), pl.BlockSpec((tt, D), lambda i: (i, 0)),
                  pl.BlockSpec((tt, D), lambda i: (i, 2)),
                  pl.BlockSpec((1, DV), lambda i: (0, 0)), pl.BlockSpec((tt, D), lambda i: (i, 0))],
        out_specs=[pl.BlockSpec((tt, D), lambda i: (i, 0)), pl.BlockSpec((tt, D), lambda i: (i, 0)),
                   pl.BlockSpec((1, DV), lambda i: (0, 0))],
        out_shape=[jax.ShapeDtypeStruct((n, D), F32), jax.ShapeDtypeStruct((n, D), F32),
                   jax.ShapeDtypeStruct((1, DV), F32)],
        compiler_params=_params(("arbitrary",)),
    )(o2[0], o2[1], p_all, head_gain, dyb)


def _gla_assemble(p_all, w2, b2, dq, dk, dv, dla, dgate, n):
    ntot = p_all.shape[0]
    tt = _tile(n, 128)
    nmain = n // tt
    assert ntot % tt == 0

    def body(a_ref, w_ref, b_ref, dqf_ref, dqb_ref, dkf_ref, dkb_ref, dvf_ref, dvb_ref, dlf_ref, dlb_ref, dg_ref,
             dp_ref, dw_ref, db_ref):
        i = pl.program_id(0)

        @pl.when(i == 0)
        def _():
            dw_ref[...] = jnp.zeros_like(dw_ref)
            db_ref[...] = jnp.zeros_like(db_ref)

        a = a_ref[...]
        w = w_ref[...]
        z = _dot(a, w) + b_ref[...]
        dla = jnp.concatenate([dlf_ref[...], dlb_ref[...]], axis=1)
        dz = dla * (1.0 / (1.0 + jnp.exp(z))) * (1.0 / TAU)
        dw_ref[...] += _dot(a, dz, _TN)
        db_ref[...] += _rowsum(dz)
        dp_ref[:, 0:KEY] = (dqf_ref[...] + dqb_ref[...]).astype(BF16)
        dp_ref[:, KEY:2 * KEY] = (dkf_ref[...] + dkb_ref[...]).astype(BF16)
        dp_ref[:, 2 * KEY:2 * KEY + D] = (dvf_ref[...] + dvb_ref[...]).astype(BF16)
        dp_ref[:, 2 * KEY + D:2 * KEY + 2 * D] = (dg_ref[...] * jnp.where(i < nmain, 1.0, 0.0)).astype(BF16)
        dp_ref[:, 2 * KEY + 2 * D:GLA_IN_PAD] = _dot(dz, w, _NT).astype(BF16)

    row = lambda width: pl.BlockSpec((tt, width), lambda i: (i, 0))
    return pl.pallas_call(
        body, name="gla_assemble", grid=(ntot // tt,),
        in_specs=[pl.BlockSpec((tt, 128), lambda i: (i, (2 * KEY + 2 * D) // 128)),
                  pl.BlockSpec((128, 2 * KEY), lambda i: (0, 0)), pl.BlockSpec((1, 2 * KEY), lambda i: (0, 0)),
                  row(KEY), row(KEY), row(KEY), row(KEY), row(D), row(D), row(KEY), row(KEY),
                  pl.BlockSpec((tt, D), lambda i: (jnp.minimum(i, nmain - 1), 0))],
        out_specs=[pl.BlockSpec((tt, GLA_IN_PAD), lambda i: (i, 0)), pl.BlockSpec((128, 2 * KEY), lambda i: (0, 0)),
                   pl.BlockSpec((1, 2 * KEY), lambda i: (0, 0))],
        out_shape=[jax.ShapeDtypeStruct((ntot, GLA_IN_PAD), BF16), jax.ShapeDtypeStruct((128, 2 * KEY), F32),
                   jax.ShapeDtypeStruct((1, 2 * KEY), F32)],
        compiler_params=_params(("arbitrary",)),
    )(p_all, w2, b2, dq[0], dq[1], dk[0], dk[1], dv[0], dv[1], dla[0], dla[1], dgate)


ADA_ROWS = 24
ADA_SH = N_MOD * D // N_CHIPS


def _ada_fwd(cvec, ada_w, ada_b_sh):
    def body(c_ref, w_ref, b_ref, o_ref):
        o_ref[0] = _dot(_silu(c_ref[...]), w_ref[0]) + b_ref[0]

    return pl.pallas_call(
        body, name="ada_fwd", grid=(2,),
        in_specs=[pl.BlockSpec((ADA_ROWS, D), lambda l: (0, 0)), pl.BlockSpec((1, D, ADA_SH), lambda l: (l, 0, 0)),
                  pl.BlockSpec((1, 1, ADA_SH), lambda l: (l, 0, 0))],
        out_specs=pl.BlockSpec((1, ADA_ROWS, ADA_SH), lambda l: (l, 0, 0)),
        out_shape=jax.ShapeDtypeStruct((2, ADA_ROWS, ADA_SH), F32),
        compiler_params=_params(("parallel",)),
    )(cvec, ada_w, ada_b_sh)


def _ada_bwd(cvec, ada_w, dmod_sh):
    def body(c_ref, w_ref, dm_ref, gw_ref, dc_ref):
        dm = dm_ref[0]
        gw_ref[0] = _dot(_silu(c_ref[...]), dm, _TN)
        dc_ref[0] = _dot(dm, w_ref[0], _NT)

    return pl.pallas_call(
        body, name="ada_bwd", grid=(2,),
        in_specs=[pl.BlockSpec((ADA_ROWS, D), lambda l: (0, 0)), pl.BlockSpec((1, D, ADA_SH), lambda l: (l, 0, 0)),
                  pl.BlockSpec((1, ADA_ROWS, ADA_SH), lambda l: (l, 0, 0))],
        out_specs=[pl.BlockSpec((1, D, ADA_SH), lambda l: (l, 0, 0)), pl.BlockSpec((1, ADA_ROWS, D), lambda l: (l, 0, 0))],
        out_shape=[jax.ShapeDtypeStruct((2, D, ADA_SH), F32), jax.ShapeDtypeStruct((2, ADA_ROWS, D), F32)],
        compiler_params=_params(("parallel",)),
    )(cvec, ada_w, dmod_sh)


def _sum_slots(x, name):
    s, r, _ = x.shape

    def body(x_ref, o_ref):
        acc = x_ref[0]
        for k in range(1, s):
            acc = acc + x_ref[k]
        o_ref[...] = acc

    return pl.pallas_call(
        body, name=name, out_shape=jax.ShapeDtypeStruct((r, 128), F32),
        in_specs=[pl.BlockSpec(memory_space=pltpu.VMEM)], out_specs=pl.BlockSpec(memory_space=pltpu.VMEM),
    )(x)


def _cctx_grad(dscc_parts, c_ctx):
    def body(p_ref, c_ref, o_ref):
        acc = p_ref[0]
        for k in range(1, N_CHIPS):
            acc = acc + p_ref[k]
        o_ref[...] = acc * _dsilu(c_ref[...])

    return pl.pallas_call(
        body, name="cctx_grad", out_shape=jax.ShapeDtypeStruct((8, 128), F32),
        in_specs=[pl.BlockSpec(memory_space=pltpu.VMEM)] * 2, out_specs=pl.BlockSpec(memory_space=pltpu.VMEM),
    )(dscc_parts, c_ctx)


def _adamw(w, g, m, v, name, after):
    nl, r, cdim = w.shape
    tr = _tile(r, 256)
    c1 = 1.0 - ADAM_B1 ** ADAM_STEP
    c2 = 1.0 - ADAM_B2 ** ADAM_STEP

    def body(w_ref, g_ref, m_ref, v_ref, after_ref, d_ref, mo_ref, vo_ref):
        gv = g_ref[...]
        mn = ADAM_B1 * m_ref[...] + (1.0 - ADAM_B1) * gv
        vn = ADAM_B2 * v_ref[...] + (1.0 - ADAM_B2) * (gv * gv)
        mo_ref[...] = mn
        vo_ref[...] = vn
        d_ref[...] = -ADAM_LR * ((mn / c1) / (jnp.sqrt(vn / c2) + ADAM_EPS) + ADAM_WD * w_ref[...])

    spec = pl.BlockSpec((1, tr, cdim), lambda l, i: (l, i, 0))
    sds = jax.ShapeDtypeStruct((nl, r, cdim), F32)
    return pl.pallas_call(
        body, name=name, grid=(nl, r // tr), in_specs=[spec] * 4 + [pl.BlockSpec(memory_space=pl.ANY)],
        out_specs=[spec] * 3, out_shape=[sds] * 3, compiler_params=_params(("parallel", "parallel")),
    )(w, g, m, v, after)


def _place():
    x, y, c = lax.axis_index("x"), lax.axis_index("y"), lax.axis_index("c")
    return x, y, c


def _allgather_small(blk, name):
    m_per, n = blk.shape

    def body(x_ref, out_ref, send_sems, recv_sems, local_sem):
        x, y, c = _place()
        me, sibling = (x, y, c), (x, y, 1 - c)
        chips = [(1 - x, y), (x, 1 - y), (1 - x, 1 - y)]

        def rows(px, py, pc):
            return out_ref.at[pl.ds((4 * px + 2 * py + pc) * m_per, m_per), :]

        def copy(k, block, to, src=None):
            return pltpu.make_async_remote_copy(
                src_ref=rows(*block) if src is None else src, dst_ref=rows(*block),
                send_sem=send_sems.at[k], recv_sem=recv_sems.at[k], device_id=to, device_id_type=MESH)

        mine = pltpu.make_async_copy(x_ref, rows(*me), local_sem)
        mine.start()
        first = [copy(0, me, sibling, src=x_ref)]
        first += [copy(1 + j, me, (*chip, c), src=x_ref) for j, chip in enumerate(chips)]
        for cp in first:
            cp.start()
        passed = [copy(4 + j, (*chip, c), sibling) for j, chip in enumerate(chips)]
        for j, chip in enumerate(chips):
            copy(1 + j, (*chip, c), me).wait_recv()
            passed[j].start()
        copy(0, sibling, me).wait_recv()
        for j, chip in enumerate(chips):
            copy(4 + j, (*chip, 1 - c), me).wait_recv()
        for cp in first + passed:
            cp.wait_send()
        mine.wait()

    return pl.pallas_call(
        body, name=name,
        out_shape=jax.ShapeDtypeStruct((N_DEV * m_per, n), blk.dtype),
        in_specs=[pl.BlockSpec(memory_space=pltpu.VMEM)],
        out_specs=pl.BlockSpec(memory_space=pltpu.VMEM),
        scratch_shapes=[pltpu.SemaphoreType.DMA((7,)), pltpu.SemaphoreType.DMA((7,)), pltpu.SemaphoreType.DMA],
    )(blk)


def _other_chips(x, y):
    return [(1 - x, y), (x, 1 - y), (1 - x, 1 - y)]


_HBM_SPEC = pl.BlockSpec(memory_space=pltpu.HBM)
_SEM_SPEC = pl.BlockSpec(memory_space=pltpu.SEMAPHORE)
_SPLIT_PARAMS = pltpu.CompilerParams(has_side_effects=pltpu.SideEffectType.DATAFLOW_SIDE_EFFECTING)


def _in_hbm(a):
    return pltpu.with_memory_space_constraint(a, pltpu.HBM)


def _ag_copies(own_ref, land_ref, send_sems, recv_sems):
    x, y, c = _place()
    chip = 2 * x + y
    hr = own_ref.shape[0] // 2

    def half(ch):
        return land_ref.at[ch, pl.ds(c * hr, hr), :]

    def copy(k, src, dst, to):
        return pltpu.make_async_remote_copy(src_ref=src, dst_ref=dst, send_sem=send_sems.at[k],
                                            recv_sem=recv_sems.at[k], device_id=to, device_id_type=MESH)

    sends, expects = [], []
    for j, (ox, oy) in enumerate(_other_chips(x, y)):
        sends.append(copy(j, own_ref.at[pl.ds(c * hr, hr), :], half(chip), (ox, oy, c)))
        expects.append(copy(j, half(2 * ox + oy), half(2 * ox + oy), (ox, oy, c)))
    own_slot = copy(3, own_ref, land_ref.at[chip], (x, y, 1 - c))
    return sends + [own_slot], expects + [own_slot]


def _sc_copies(p_ref, land_ref, send_sems, recv_sems):
    x, y, c = _place()
    chip = 2 * x + y
    sends, expects = [], []
    for j, (ox, oy) in enumerate(_other_chips(x, y)):
        och = 2 * ox + oy
        mk = lambda dst_slot: pltpu.make_async_remote_copy(
            src_ref=p_ref.at[och], dst_ref=land_ref.at[dst_slot], send_sem=send_sems.at[j],
            recv_sem=recv_sems.at[j], device_id=(ox, oy, c), device_id_type=MESH)
        sends.append(mk(chip))
        expects.append(mk(och))
    return sends, expects


def _pe_copies(g_ref, land_ref, send_sems, recv_sems):
    x, y, c = _place()
    hr = g_ref.shape[1] // 2
    cp = pltpu.make_async_remote_copy(
        src_ref=g_ref.at[:, pl.ds((1 - c) * hr, hr), :], dst_ref=land_ref, send_sem=send_sems.at[0],
        recv_sem=recv_sems.at[0], device_id=(x, y, 1 - c), device_id_type=MESH)
    return [cp], [cp]


def _split_start(src, land_shape, copies, n_copies, after, name):
    def body(src_ref, land_ref, after_ref, send_sems, recv_sems, src_thru, land_thru, token):
        for cp in copies(src_ref, land_ref, send_sems, recv_sems)[0]:
            cp.start()
        token[...] = jnp.zeros_like(token)

    land = lax.empty(land_shape, src.dtype)
    return pl.pallas_call(
        body, name=name,
        out_shape=(pltpu.SemaphoreType.DMA((n_copies,)), pltpu.SemaphoreType.DMA((n_copies,)),
                   pltpu.HBM(src.shape, src.dtype), pltpu.HBM(land_shape, src.dtype),
                   jax.ShapeDtypeStruct((8, 128), F32)),
        in_specs=(_HBM_SPEC, _HBM_SPEC, pl.BlockSpec(memory_space=pl.ANY)),
        out_specs=(_SEM_SPEC, _SEM_SPEC, _HBM_SPEC, _HBM_SPEC, pl.BlockSpec(memory_space=pltpu.VMEM)),
        input_output_aliases={0: 2, 1: 3}, compiler_params=_SPLIT_PARAMS,
    )(_in_hbm(src), _in_hbm(land), after)


def _split_wait(started, after, copies, name):
    send_sems, recv_sems, src_thru, land_thru, _ = started

    def body(src_ref, land_ref, send_sems, recv_sems, after_ref, src_dead, got_ref):
        sends, expects = copies(src_ref, land_ref, send_sems, recv_sems)
        for cp in sends:
            cp.wait_send()
        for cp in expects:
            cp.wait_recv()

    return pl.pallas_call(
        body, name=name,
        out_shape=(pltpu.HBM(src_thru.shape, src_thru.dtype), pltpu.HBM(land_thru.shape, land_thru.dtype)),
        in_specs=(_HBM_SPEC, _HBM_SPEC, _SEM_SPEC, _SEM_SPEC, pl.BlockSpec(memory_space=pl.ANY)),
        out_specs=(_HBM_SPEC, _HBM_SPEC), input_output_aliases={0: 0, 1: 1}, compiler_params=_SPLIT_PARAMS,
    )(src_thru, land_thru, send_sems, recv_sems, after)


def _ag_pass_on(land, name):
    hr = land.shape[1] // 2

    def body(in_ref, out_ref, send_sems, recv_sems):
        x, y, c = _place()

        def copy(j, ox, oy, cc):
            ref = out_ref.at[2 * ox + oy, pl.ds(cc * hr, hr), :]
            return pltpu.make_async_remote_copy(src_ref=ref, dst_ref=ref, send_sem=send_sems.at[j],
                                                recv_sem=recv_sems.at[j], device_id=(x, y, 1 - c),
                                                device_id_type=MESH)

        others = _other_chips(x, y)
        for j, (ox, oy) in enumerate(others):
            copy(j, ox, oy, c).start()
        for j, (ox, oy) in enumerate(others):
            copy(j, ox, oy, 1 - c).wait_recv()
        for j, (ox, oy) in enumerate(others):
            copy(j, ox, oy, c).wait_send()

    any_spec = pl.BlockSpec(memory_space=pl.ANY)
    return pl.pallas_call(
        body, name=name, out_shape=jax.ShapeDtypeStruct(land.shape, land.dtype),
        in_specs=[any_spec], out_specs=any_spec, input_output_aliases={0: 0},
        scratch_shapes=[pltpu.SemaphoreType.DMA((3,)), pltpu.SemaphoreType.DMA((3,))],
    )(land)


def _rs_pair_exchange(g, name):
    r = g.shape[1]
    hr = r // 2

    def body(g_ref, got_ref, send_sem, recv_sem):
        x, y, c = _place()
        cp = pltpu.make_async_remote_copy(
            src_ref=g_ref.at[:, pl.ds((1 - c) * hr, hr), :], dst_ref=got_ref, send_sem=send_sem, recv_sem=recv_sem,
            device_id=(x, y, 1 - c), device_id_type=MESH)
        cp.start()
        cp.wait()

    any_spec = pl.BlockSpec(memory_space=pl.ANY)
    return pl.pallas_call(
        body, name=name,
        out_shape=jax.ShapeDtypeStruct((N_CHIPS, hr, D), F32),
        in_specs=[any_spec], out_specs=any_spec,
        scratch_shapes=[pltpu.SemaphoreType.DMA, pltpu.SemaphoreType.DMA],
    )(g)


def _rs_chip_sum(place, g, got, name):
    r = g.shape[1]
    hr = r // 2
    tr = _tile(hr, 640, 16)
    nt = hr // tr

    def body(pl_ref, g_ref, got_ref, p16_ref, p32_ref):
        s = pl.program_id(1)
        p = g_ref[0] + got_ref[0]
        p16_ref[0] = p.astype(BF16)

        @pl.when(s == pl_ref[1])
        def _():
            p32_ref[...] = p

    return pl.pallas_call(
        body, name=name,
        grid_spec=pltpu.PrefetchScalarGridSpec(
            num_scalar_prefetch=1, grid=(nt, N_CHIPS),
            in_specs=[pl.BlockSpec((1, tr, D), lambda i, s, pr: (s, pr[0] * nt + i, 0)),
                      pl.BlockSpec((1, tr, D), lambda i, s, pr: (s, i, 0))],
            out_specs=[pl.BlockSpec((1, tr, D), lambda i, s, pr: (s, i, 0)),
                       pl.BlockSpec((tr, D), lambda i, s, pr: (i, 0))]),
        out_shape=[jax.ShapeDtypeStruct((N_CHIPS, hr, D), BF16), jax.ShapeDtypeStruct((hr, D), F32)],
        compiler_params=_params(("parallel", "arbitrary")),
    )(place, g, got)


def _rs_final_sum(place, parts, p32, name):
    hr = parts.shape[1]
    tr = _tile(hr, 640, 16)
    nt = hr // tr

    def body(pl_ref, a_ref, b_ref, c_ref, p32_ref, o_ref):
        o_ref[...] = ((p32_ref[...] + a_ref[0].astype(F32)) + b_ref[0].astype(F32)) + c_ref[0].astype(F32)

    def other(j):
        return pl.BlockSpec((1, tr, D), lambda i, pr: (j + jnp.where(pr[1] <= j, 1, 0), i, 0))

    return pl.pallas_call(
        body, name=name,
        grid_spec=pltpu.PrefetchScalarGridSpec(
            num_scalar_prefetch=1, grid=(nt,),
            in_specs=[other(0), other(1), other(2), pl.BlockSpec((tr, D), lambda i, pr: (i, 0))],
            out_specs=pl.BlockSpec((tr, D), lambda i, pr: (pr[0] * nt + i, 0))),
        out_shape=jax.ShapeDtypeStruct((2 * hr, D), F32),
        compiler_params=_params(("parallel",)),
    )(place, parts, parts, parts, p32)


def _rs_pair_gather(both, name):
    hr = both.shape[0] // 2

    def body(in_ref, out_ref, send_sem, recv_sem):
        x, y, c = _place()
        mine = out_ref.at[pl.ds(c * hr, hr), :]
        cp = pltpu.make_async_remote_copy(
            src_ref=mine, dst_ref=mine, send_sem=send_sem, recv_sem=recv_sem,
            device_id=(x, y, 1 - c), device_id_type=MESH)
        cp.start()
        theirs = out_ref.at[pl.ds((1 - c) * hr, hr), :]
        pltpu.make_async_remote_copy(
            src_ref=theirs, dst_ref=theirs, send_sem=send_sem, recv_sem=recv_sem,
            device_id=(x, y, 1 - c), device_id_type=MESH).wait_recv()
        cp.wait_send()

    any_spec = pl.BlockSpec(memory_space=pl.ANY)
    return pl.pallas_call(
        body, name=name,
        out_shape=jax.ShapeDtypeStruct(both.shape, F32),
        in_specs=[any_spec], out_specs=any_spec, input_output_aliases={0: 0},
        scratch_shapes=[pltpu.SemaphoreType.DMA, pltpu.SemaphoreType.DMA],
    )(both)


def _local_step(x, ctx, tgt, mods, mc, ag_gin, ag_main, place, small):
    nb, t, _ = x.shape
    tc = ctx.shape[1]
    n = nb * t
    nc = nb * tc
    xf = x.reshape(n, D)
    cf = ctx.reshape(nc, D)
    tf = tgt.reshape(n, D)
    vec = lambda a: a.reshape(1, -1)
    m = [[mods[l, :, k, :].reshape(nb, 1, D) for k in range(N_MOD)] for l in range(2)]
    mc_b = [jnp.broadcast_to(mc[k].reshape(1, 1, D), (nb, 1, D)) for k in range(2)]

    cw = [small["ffn_conv_w"][l] for l in range(2)]
    cb = [small["ffn_conv_b"][l].reshape(1, -1) for l in range(2)]
    w2 = jnp.zeros((128, 2 * KEY), F32)
    w2 = w2.at[0:RANK, 0:KEY].set(small["gla_w_a2"][0]).at[RANK:2 * RANK, KEY:].set(small["gla_w_a2"][1])
    b2 = small["gla_b_a"].reshape(1, 2 * KEY)
    hg = small["gla_head_norm"].reshape(1, DV)

    hn_all = _mod_fwd(xf, vec(small["norm_mix"][0]), m[0][0], m[0][1], t, "mod0_main", out_rows=n + nc)
    hn_all = _mod_fwd(cf, vec(small["norm_mix"][0]), mc_b[0], mc_b[1], tc, "mod0_ctx", out_rows=n + nc,
                      into=hn_all, row0=n)
    gin = _ag_pass_on(_split_wait(ag_gin, hn_all, _ag_copies, "ag_gin_wait")[1], "ag_gin_pass_on")
    w_gin = jnp.pad(gin[:, :_GIN_ROWS, :].reshape(GLA_IN, D), ((0, GLA_IN_PAD - GLA_IN), (0, 0)))
    p_all = _mm(hn_all, w_gin, "nt", F32, "gla_in_proj", 768, 3200)
    la_all = _gla_decay_fwd(p_all, w2, b2)
    o2, ss = _gla_scan_fwd(p_all, la_all, nb, t, tc)
    wg = _ag_pass_on(_split_wait(ag_main, o2[0], _ag_copies, "ag_main_wait")[1], "ag_main_pass_on")
    offs = _offsets(_MAIN, _MAIN_ROWS)
    rows = _MAIN_ROWS

    def w_nt(a, k, name, out_dtype=BF16, tm=1024):
        return _mm_nt_w(a, wg, offs[k], rows[k], name, tm, out_dtype)

    def w_nn_mod(a3, k, h, gate, gain, shift, scale, name):
        return _mm_nn_w_mod(a3, wg, offs[k], rows[k], h, gate, vec(gain), shift, scale, t, name, 512)

    yb0 = _gla_post_fwd(o2, p_all, hg, n)
    y0, h1, hn1 = w_nn_mod(yb0[None], "gla_out", xf, m[0][2], small["norm_ffn"][0], m[0][3], m[0][4],
                           "gla_out_proj_mod")
    u0, z0 = _ffn_up_mid_fwd(hn1, wg, offs["up_t0"], cw[0], cb[0], nb, t, "ffn0_up_mid")
    f0, h2, hn2 = w_nn_mod(z0[None], "down0", h1, m[0][5], small["norm_mix"][1], m[1][0], m[1][1],
                           "ffn0_down_mod")
    p1, yb1 = _sc_in_mid_fwd(hn2, wg, offs["sc_in_t"], small["sc_conv_w"], nb, t)
    y1, h3, hn3 = w_nn_mod(yb1[None], "sc_out", h2, m[1][2], small["norm_ffn"][1], m[1][3], m[1][4],
                           "sc_out_proj_mod")
    u1, z1 = _ffn_up_mid_fwd(hn3, wg, offs["up_t1"], cw[1], cb[1], nb, t, "ffn1_up_mid")
    loss, dh4, df1, dm15, dfinal = _mm_nn_w_final(z1[None], wg, offs["down1"], rows["down1"], h3, m[1][5],
                                                  vec(small["final_norm"]), tf, t, "ffn1_down_final", 512)

    gs = {}
    dmods = [[None] * N_MOD for _ in range(2)]
    dmods[1][5] = dm15

    def w_dw(a3, b, g_prev, k, name, tm):
        return _mm_dw(a3, b, g_prev, offs[k], rows[k], name, tm)

    def w_dx_mod(a3, k, h_in, dh_out, gain, scale, y_prev, gate_prev, name):
        return _mm_nn_w_modbwd(a3, wg, offs[k], rows[k], h_in, dh_out, vec(gain), scale, y_prev, gate_prev, t,
                               name, 256)

    def ffn_bwd(l, df, u, z, hn, g_prev, h_in, dh_out, scale, y_prev, gate_prev):
        dz = w_nt(df, f"down{l}", f"ffn{l}_down_dx")
        g_acc = w_dw(z[None], df, g_prev, f"down{l}", f"ffn{l}_down_dw", 640)
        du, dcw, dcb = _ffn_mid_bwd(u, cw[l], cb[l], dz, nb, t, f"ffn{l}_mid_bwd")
        r = w_dx_mod(du, f"up_t{l}", h_in, dh_out, small["norm_ffn"][l], scale, y_prev, gate_prev,
                     f"ffn{l}_up_dx_mod")
        g_acc = w_dw(du, hn, g_acc, f"up_t{l}", f"ffn{l}_up_dw", 640)
        return r, g_acc, jnp.moveaxis(dcw, 0, 1).reshape(3, 2 * HID), dcb.reshape(2 * HID)

    r, g_acc, dcw1, dcb1 = ffn_bwd(1, df1, u1, z1, hn3, None, h3, dh4, m[1][4], y1, m[1][2])
    dh3, dmods[1][4], dmods[1][3], dnf1, dy1, dmods[1][2] = (r["dh"], r["dscale"], r["dshift"], r["dgain"],
                                                             r["dy_prev"], r["dgate_prev"])
    dyb1 = w_nt(dy1, "sc_out", "sc_out_dx")
    g_acc = w_dw(yb1[None], dy1, g_acc, "sc_out", "sc_out_dw", 256)
    dp1, dscw = _sc_mid_bwd(p1, small["sc_conv_w"], dyb1, nb, t)
    r = w_dx_mod(dp1, "sc_in_t", h2, dh3, small["norm_mix"][1], m[1][1], f0, m[0][5], "sc_in_dx_mod")
    g_acc = w_dw(dp1, hn2, g_acc, "sc_in_t", "sc_in_dw", 256)
    dh2, dmods[1][1], dmods[1][0], dnm1, df0, dmods[0][5] = (r["dh"], r["dscale"], r["dshift"], r["dgain"],
                                                             r["dy_prev"], r["dgate_prev"])
    r, g_acc, dcw0, dcb0 = ffn_bwd(0, df0, u0, z0, hn1, g_acc, h1, dh2, m[0][4], y0, m[0][2])
    dh1, dmods[0][4], dmods[0][3], dnf0, dy0, dmods[0][2] = (r["dh"], r["dscale"], r["dshift"], r["dgain"],
                                                             r["dy_prev"], r["dgate_prev"])
    g_packed = w_dw(yb0[None], dy0, g_acc, "gla_out", "gla_out_dw", 256)
    pair = _split_start(g_packed, (N_CHIPS, _MAIN_TOTAL // 2, D), _pe_copies, 1, dy0, "rs_main_pair_start")
    dyb0 = w_nt(dy0, "gla_out", "gla_out_dx", F32)
    do, dgate, dhg = _gla_post_bwd(o2, p_all, hg + pair[4][0:1, 0:1], dyb0, n)
    g_packed, from_sibling = _split_wait(pair, do, _pe_copies, "rs_main_pair_wait")
    p16, p32 = _rs_chip_sum(place, g_packed, from_sibling, "rs_main_chip_sum")
    sc_main = _split_start(p16, p16.shape, _sc_copies, 3, p32, "rs_main_scatter_start")
    dq, dk, dv, dla = _gla_scan_bwd(p_all, la_all, do, ss, nb, t, tc, sc_main[4])
    dp, dw2, db2 = _gla_assemble(p_all, w2, b2, dq, dk, dv, dla, dgate, n)
    dhn_all = _mm(dp, w_gin, "nn", F32, "gla_in_dx", 768, 512)
    landed = _split_wait(sc_main, dhn_all, _sc_copies, "rs_main_scatter_wait")[1]
    g_main = _rs_pair_gather(_rs_final_sum(place, landed, p32, "rs_main_final_sum"), "rs_main_pair_gather")
    g_gin = _mm(dp, hn_all, "tn", F32, "gla_in_dw", 640, 1024)[:GLA_IN]
    g_gin = jnp.pad(g_gin.reshape(N_CHIPS, _GIN_ROWS, D), ((0, 0), (0, _GIN_PAD - _GIN_ROWS), (0, 0)))
    from_sibling = _rs_pair_exchange(g_gin, "rs_gin_pair_exchange")
    p16_gin, p32_gin = _rs_chip_sum(place, g_gin, from_sibling, "rs_gin_chip_sum")
    r = _mod_bwd(xf, dhn_all, vec(small["norm_mix"][0]), m[0][1], t, "mod0_main_bwd", dh_out=dh1)
    grad_x, dmods[0][1], dmods[0][0], dnm0 = r["dh"], r["dscale"], r["dshift"], r["dgain"]
    rc = _mod_bwd(cf, dhn_all, vec(small["norm_mix"][0]), mc_b[1], tc, "mod0_ctx_bwd", dhn_row0=n, need_dh=False)
    dmc = jnp.stack([jnp.sum(rc["dshift"], axis=0).reshape(D), jnp.sum(rc["dscale"], axis=0).reshape(D)])
    dnm0 = dnm0 + rc["dgain"]

    gs["norm_mix"] = jnp.concatenate([dnm0, dnm1], axis=0)
    gs["norm_ffn"] = jnp.concatenate([dnf0, dnf1], axis=0)
    gs["final_norm"] = dfinal.reshape(D)
    gs["gla_w_a2"] = jnp.stack([dw2[0:RANK, 0:KEY], dw2[RANK:2 * RANK, KEY:]])
    gs["gla_b_a"] = db2.reshape(2, KEY)
    gs["gla_head_norm"] = dhg.reshape(DV)
    gs["sc_conv_w"] = dscw
    gs["ffn_conv_w"] = jnp.stack([dcw0, dcw1])
    gs["ffn_conv_b"] = jnp.stack([dcb0, dcb1])
    dmods_arr = jnp.stack([jnp.stack([dmods[l][k].reshape(nb, D) for k in range(N_MOD)], axis=1) for l in range(2)])
    return loss, grad_x.reshape(nb, t, D), g_main, p16_gin, p32_gin, gs, dmods_arr, dmc


def _pack(arrs):
    parts, meta, off = [], [], 0
    for a in arrs:
        r = a.size // 128
        rp = -(-r // 8) * 8
        a2 = a.reshape(r, 128).astype(F32)
        if rp != r:
            a2 = jnp.pad(a2, ((0, rp - r), (0, 0)))
        parts.append(a2)
        meta.append((off, r, a.shape))
        off += rp
    return jnp.concatenate(parts, axis=0), meta


def _unpack(buf, meta, lead=()):
    return [buf[..., off:off + r, :].reshape(*lead, *shape) for off, r, shape in meta]


_MAIN = ("up_t0", "up_t1", "down0", "down1", "sc_in_t", "gla_out", "sc_out")
_MAIN_ROWS = {"sc_in_t": 3 * D // N_CHIPS, "up_t0": 2 * HID // N_CHIPS, "up_t1": 2 * HID // N_CHIPS,
              "gla_out": D // N_CHIPS, "sc_out": D // N_CHIPS, "down0": HID // N_CHIPS, "down1": HID // N_CHIPS}
_MAIN_TOTAL = sum(_MAIN_ROWS.values())
_GIN_ROWS = GLA_IN // N_CHIPS
_GIN_PAD = -(-_GIN_ROWS // 32) * 32


def _offsets(names, rows):
    off, out = 0, {}
    for k in names:
        out[k] = off
        off += rows[k]
    return out


def kernel(x, c, ctx, c_ctx, ada_w, ada_b, norm_mix, norm_ffn, gla_w_in, gla_w_a2, gla_b_a, gla_head_norm, gla_w_out, sc_w_in, sc_conv_w, sc_w_out, ffn_w_up, ffn_conv_w, ffn_conv_b, ffn_w_down, final_norm, loss_target, m_c_ctx, m_ada_w, m_ada_b, m_norm_mix, m_norm_ffn, m_gla_w_in, m_gla_w_a2, m_gla_b_a, m_gla_head_norm, m_gla_w_out, m_sc_w_in, m_sc_conv_w, m_sc_w_out, m_ffn_w_up, m_ffn_conv_w, m_ffn_conv_b, m_ffn_w_down, m_final_norm, v_c_ctx, v_ada_w, v_ada_b, v_norm_mix, v_norm_ffn, v_gla_w_in, v_gla_w_a2, v_gla_b_a, v_gla_head_norm, v_gla_w_out, v_sc_w_in, v_sc_conv_w, v_sc_w_out, v_ffn_w_up, v_ffn_conv_w, v_ffn_conv_b, v_ffn_w_down, v_final_norm):
    ix, iy, ic = _place()
    chip = 2 * ix + iy
    dev = 2 * chip + ic
    place = jnp.stack([ic, chip]).astype(jnp.int32)
    nb = x.shape[0]
    offs = _offsets(_MAIN, _MAIN_ROWS)

    buf, meta = _pack([c, ffn_conv_w, sc_conv_w, gla_w_a2, gla_b_a])
    got = _allgather_small(buf, "gather_small_in").reshape(N_DEV, buf.shape[0], 128)
    c_all, fcw, scw, wa2, ba = _unpack(got, meta, (N_DEV,))
    c_all = c_all.reshape(N_DEV * nb, D)
    per_chip = lambda a: a[0::2]
    ffn_conv_w_full = jnp.moveaxis(per_chip(fcw), 0, 2).reshape(2, 3, 2 * HID)
    sc_conv_w_full = jnp.moveaxis(per_chip(scw)[:, 0], 0, 1).reshape(3, D)
    gla_w_a2_full = jnp.moveaxis(per_chip(wa2)[:, 0], 0, 2).reshape(2, RANK, KEY)
    gla_b_a_full = jnp.moveaxis(per_chip(ba)[:, 0], 0, 1).reshape(2, KEY)

    cvec = jnp.concatenate([c_all, c_ctx.reshape(1, D), jnp.zeros((ADA_ROWS - N_DEV * nb - 1, D), F32)], axis=0)
    ada_b_sh = lax.dynamic_slice_in_dim(ada_b, chip * ADA_SH, ADA_SH, axis=1).reshape(2, 1, ADA_SH)
    mod_sh = _ada_fwd(cvec, ada_w, ada_b_sh)
    got = _allgather_small(mod_sh.reshape(2 * ADA_ROWS, ADA_SH), "gather_mod")
    mod_full = jnp.moveaxis(per_chip(got.reshape(N_DEV, 2, ADA_ROWS, ADA_SH)), 0, 2).reshape(2, ADA_ROWS, N_MOD * D)
    mc = mod_full[0, N_DEV * nb, :2 * D].reshape(2, D)

    own = {"sc_in_t": sc_w_in[0].T, "up_t0": ffn_w_up[0].T, "up_t1": ffn_w_up[1].T,
           "gla_out": gla_w_out[0], "sc_out": sc_w_out[0], "down0": ffn_w_down[0], "down1": ffn_w_down[1]}
    own_main = jnp.concatenate([own[k].astype(BF16) for k in _MAIN], axis=0)
    own_gin = jnp.pad(gla_w_in[0].T.astype(BF16), ((0, _GIN_PAD - _GIN_ROWS), (0, 0)))
    ag_gin = _split_start(own_gin, (N_CHIPS, _GIN_PAD, D), _ag_copies, 4, mc, "ag_gin_start")
    ag_main = _split_start(own_main, (N_CHIPS, _MAIN_TOTAL, D), _ag_copies, 4, ag_gin[4], "ag_main_start")
    mods = lax.dynamic_slice_in_dim(mod_full, dev * nb, nb, axis=1).reshape(2, nb, N_MOD, D) + ag_main[4][0, 0]

    small = {"norm_mix": norm_mix, "norm_ffn": norm_ffn, "final_norm": final_norm, "gla_w_a2": gla_w_a2_full,
             "gla_b_a": gla_b_a_full, "gla_head_norm": gla_head_norm[0], "sc_conv_w": sc_conv_w_full,
             "ffn_conv_w": ffn_conv_w_full, "ffn_conv_b": ffn_conv_b}
    loss_p, grad_x, g_main, p16_gin, p32_gin, gs, dmods, dmc = _local_step(x, ctx, loss_target, mods, mc, ag_gin,
                                                                           ag_main, place, small)

    sum_names = ["norm_mix", "norm_ffn", "final_norm", "gla_w_a2", "gla_b_a", "gla_head_norm", "sc_conv_w",
                 "ffn_conv_w", "ffn_conv_b"]
    buf, meta = _pack([jnp.broadcast_to(loss_p, (8, 128))] + [gs[k] for k in sum_names] + [dmc, dmods])
    n_sum = meta[-1][0]
    got = _allgather_small(buf, "gather_small_grads").reshape(N_DEV, buf.shape[0], 128)
    summed = _sum_slots(got[:, :n_sum], "sum_small_grads")
    parts = _unpack(summed, meta[:-1])
    loss = parts[0][0, 0]
    g_small = dict(zip(sum_names, parts[1:-1]))
    dmc_tot = parts[-1]
    dmods_all = jnp.moveaxis(_unpack(got, meta[-1:], (N_DEV,))[0], 0, 1).reshape(2, N_DEV * nb, N_MOD * D)

    ctx_row = jnp.stack([jnp.concatenate([dmc_tot.reshape(2 * D), jnp.zeros(((N_MOD - 2) * D,), F32)]),
                         jnp.zeros((N_MOD * D,), F32)]).reshape(2, 1, N_MOD * D)
    dmod_ext = jnp.concatenate([dmods_all, ctx_row, jnp.zeros((2, ADA_ROWS - N_DEV * nb - 1, N_MOD * D), F32)], axis=1)
    g_ada_b = _sum_slots(jnp.moveaxis(dmod_ext, 1, 0).reshape(ADA_ROWS, 2 * N_MOD * D // 128, 128),
                         "sum_ada_b").reshape(2, N_MOD * D)
    dmod_sh = lax.dynamic_slice_in_dim(dmod_ext, chip * ADA_SH, ADA_SH, axis=2)
    g_ada_w, dcv = _ada_bwd(cvec, ada_w, dmod_sh)
    dscc_part = (dcv[0, N_DEV * nb] + dcv[1, N_DEV * nb]).reshape(8, 128)
    got = _allgather_small(dscc_part, "gather_dscc").reshape(N_DEV, 8, 128)
    g_c_ctx = _cctx_grad(per_chip(got), c_ctx.reshape(8, 128)).reshape(D)

    sc_gin = _split_start(p16_gin, p16_gin.shape, _sc_copies, 3, g_c_ctx, "rs_gin_scatter_start")
    seg = {k: g_main[offs[k]:offs[k] + _MAIN_ROWS[k]] for k in _MAIN}

    sl_chip = lambda a, axis, width: lax.dynamic_slice_in_dim(a, chip * width, width, axis=axis)
    grads = {
        "c_ctx": g_c_ctx, "ada_w": g_ada_w, "ada_b": g_ada_b, "norm_mix": g_small["norm_mix"],
        "norm_ffn": g_small["norm_ffn"],
        "gla_w_a2": sl_chip(g_small["gla_w_a2"], 2, KEY // N_CHIPS)[None],
        "gla_b_a": sl_chip(g_small["gla_b_a"], 1, KEY // N_CHIPS)[None],
        "gla_head_norm": g_small["gla_head_norm"][None], "gla_w_out": seg["gla_out"][None],
        "sc_w_in": seg["sc_in_t"].T[None], "sc_conv_w": sl_chip(g_small["sc_conv_w"], 1, D // N_CHIPS)[None],
        "sc_w_out": seg["sc_out"][None], "ffn_w_up": jnp.stack([seg["up_t0"].T, seg["up_t1"].T]),
        "ffn_conv_w": sl_chip(g_small["ffn_conv_w"], 2, 2 * HID // N_CHIPS), "ffn_conv_b": g_small["ffn_conv_b"],
        "ffn_w_down": jnp.stack([seg["down0"], seg["down1"]]), "final_norm": g_small["final_norm"],
    }
    weights = {"c_ctx": c_ctx, "ada_w": ada_w, "ada_b": ada_b, "norm_mix": norm_mix, "norm_ffn": norm_ffn,
               "gla_w_in": gla_w_in, "gla_w_a2": gla_w_a2, "gla_b_a": gla_b_a, "gla_head_norm": gla_head_norm,
               "gla_w_out": gla_w_out, "sc_w_in": sc_w_in, "sc_conv_w": sc_conv_w, "sc_w_out": sc_w_out,
               "ffn_w_up": ffn_w_up, "ffn_conv_w": ffn_conv_w, "ffn_conv_b": ffn_conv_b, "ffn_w_down": ffn_w_down,
               "final_norm": final_norm}
    mom1 = {"c_ctx": m_c_ctx, "ada_w": m_ada_w, "ada_b": m_ada_b, "norm_mix": m_norm_mix, "norm_ffn": m_norm_ffn,
            "gla_w_in": m_gla_w_in, "gla_w_a2": m_gla_w_a2, "gla_b_a": m_gla_b_a, "gla_head_norm": m_gla_head_norm,
            "gla_w_out": m_gla_w_out, "sc_w_in": m_sc_w_in, "sc_conv_w": m_sc_conv_w, "sc_w_out": m_sc_w_out,
            "ffn_w_up": m_ffn_w_up, "ffn_conv_w": m_ffn_conv_w, "ffn_conv_b": m_ffn_conv_b,
            "ffn_w_down": m_ffn_w_down, "final_norm": m_final_norm}
    mom2 = {"c_ctx": v_c_ctx, "ada_w": v_ada_w, "ada_b": v_ada_b, "norm_mix": v_norm_mix, "norm_ffn": v_norm_ffn,
            "gla_w_in": v_gla_w_in, "gla_w_a2": v_gla_w_a2, "gla_b_a": v_gla_b_a, "gla_head_norm": v_gla_head_norm,
            "gla_w_out": v_gla_w_out, "sc_w_in": v_sc_w_in, "sc_conv_w": v_sc_conv_w, "sc_w_out": v_sc_w_out,
            "ffn_w_up": v_ffn_w_up, "ffn_conv_w": v_ffn_conv_w, "ffn_conv_b": v_ffn_conv_b,
            "ffn_w_down": v_ffn_w_down, "final_norm": v_final_norm}
    names = list(weights)

    big_names = ["ada_w", "gla_w_out", "sc_w_in", "sc_w_out", "ffn_w_up", "ffn_w_down", "gla_w_in"]
    small_names = [k for k in names if k not in big_names]
    delta, new_m, new_v = {}, {}, {}
    done = []

    def big_adamw(k, token):
        delta[k], new_m[k], new_v[k] = _adamw(weights[k], grads[k], mom1[k], mom2[k], "adamw_" + k, token)
        done.append(new_v[k][0, 0:1, 0:128])

    for k in big_names[:-1]:
        grads[k] = grads[k].reshape(weights[k].shape)
        big_adamw(k, sc_gin[4])
    for k in small_names:
        grads[k] = grads[k].reshape(weights[k].shape)
    packed = [_pack([src[k] for k in small_names]) for src in (weights, grads, mom1, mom2)]
    meta = packed[0][1]
    rows_pad = -packed[0][0].shape[0] % 128
    bufs = [jnp.pad(p[0], ((0, rows_pad), (0, 0)))[None] for p in packed]
    outs = _adamw(bufs[0], bufs[1], bufs[2], bufs[3], "adamw_small", sc_gin[4])
    done.append(outs[2][0, 0:1, :])
    for dst, o in zip((delta, new_m, new_v), outs):
        for k, a in zip(small_names, _unpack(o[0], meta)):
            dst[k] = a
    landed = _split_wait(sc_gin, jnp.concatenate(done, axis=0), _sc_copies, "rs_gin_scatter_wait")[1]
    g_gin_shard = _rs_pair_gather(_rs_final_sum(place, landed, p32_gin, "rs_gin_final_sum"), "rs_gin_pair_gather")
    grads["gla_w_in"] = g_gin_shard[:_GIN_ROWS].T[None]
    big_adamw("gla_w_in", sc_gin[4])

    return (loss, grad_x, *[grads[k] for k in names], *[delta[k] for k in names], *[new_m[k] for k in names],
            *[new_v[k] for k in names])
```

```python
import functools

import jax
import jax.numpy as jnp
from jax import lax
from jax.experimental import pallas as pl
from jax.experimental.pallas import tpu as pltpu

F32 = jnp.float32
BF16 = jnp.bfloat16
MESH = pl.DeviceIdType.MESH

EPS = 1e-6
D = 1024
N_MOD = 6
HEADS = 4
DK = 128
DV = 256
KEY = HEADS * DK
RANK = 16
TAU = 16.0
CH = 64
GRID_W = 64
HID = 2560
GLA_IN = 2 * KEY + 2 * D + 2 * RANK
GLA_IN_PAD = 3200
Q_SCALE = DK ** -0.5
N_CHIPS = 4
N_DEV = 8

ADAM_LR = 0.001
ADAM_B1 = 0.9
ADAM_B2 = 0.999
ADAM_EPS = 1e-08
ADAM_WD = 0.01
ADAM_STEP = 10

VMEM_LIMIT = 56 * 1024 * 1024


def _params(sem):
    return pltpu.CompilerParams(dimension_semantics=sem, vmem_limit_bytes=VMEM_LIMIT)


def _tile(n, pref, mult=8):
    if n <= pref:
        return n
    for t in range(pref - pref % mult, 0, -mult):
        if n % t == 0:
            return t
    raise ValueError((n, pref, mult))


_NN = (((1,), (0,)), ((), ()))
_NT = (((1,), (1,)), ((), ()))
_TN = (((0,), (0,)), ((), ()))


def _dot(a, b, dims=_NN):
    return lax.dot_general(a.astype(BF16), b.astype(BF16), dims, preferred_element_type=F32)


def _sigmoid(x):
    return 1.0 / (1.0 + jnp.exp(-x))


def _rowsum(x):
    return jnp.sum(x, axis=0, keepdims=True)


def _mm(a, b, form, out_dtype, name, tm, tn):
    if form == "tn":
        K, M = a.shape
    else:
        M, K = a.shape
    N = b.shape[0] if form == "nt" else b.shape[1]
    tm = _tile(M, tm, 128)
    tn = _tile(N, tn, 128)
    dims = {"nn": _NN, "nt": _NT, "tn": _TN}[form]

    def body(a_ref, b_ref, o_ref):
        o_ref[...] = _dot(a_ref[...], b_ref[...], dims).astype(o_ref.dtype)

    if form == "tn":
        a_spec = pl.BlockSpec((K, tm), lambda i, j: (0, i))
    else:
        a_spec = pl.BlockSpec((tm, K), lambda i, j: (i, 0))
    if form == "nt":
        b_spec = pl.BlockSpec((tn, K), lambda i, j: (j, 0))
    else:
        b_spec = pl.BlockSpec((K, tn), lambda i, j: (0, j))
    return pl.pallas_call(
        body,
        name=name,
        grid=(M // tm, N // tn),
        in_specs=[a_spec, b_spec],
        out_specs=pl.BlockSpec((tm, tn), lambda i, j: (i, j)),
        out_shape=jax.ShapeDtypeStruct((M, N), out_dtype),
        compiler_params=_params(("parallel", "parallel")),
    )(a, b)


def _mm_nt_w(a, wg, off, rows, name, tm, out_dtype):
    m = a.shape[0]
    tm = _tile(m, tm, 128)
    if N_CHIPS * rows <= D:

        def body_small(a_ref, w_ref, o_ref):
            av = a_ref[...]
            for s in range(N_CHIPS):
                o_ref[:, s * rows:(s + 1) * rows] = _dot(av, w_ref[s], _NT).astype(o_ref.dtype)

        return pl.pallas_call(
            body_small, name=name, grid=(m // tm,),
            in_specs=[pl.BlockSpec((tm, D), lambda i: (i, 0)),
                      pl.BlockSpec((N_CHIPS, rows, D), lambda i: (0, off // rows, 0))],
            out_specs=pl.BlockSpec((tm, N_CHIPS * rows), lambda i: (i, 0)),
            out_shape=jax.ShapeDtypeStruct((m, N_CHIPS * rows), out_dtype),
            compiler_params=_params(("parallel",)),
        )(a, wg)

    def body(a_ref, w_ref, o_ref):
        o_ref[...] = _dot(a_ref[...], w_ref[0], _NT).astype(o_ref.dtype)

    return pl.pallas_call(
        body, name=name, grid=(m // tm, N_CHIPS),
        in_specs=[pl.BlockSpec((tm, D), lambda i, s: (i, 0)),
                  pl.BlockSpec((1, rows, D), lambda i, s: (s, off // rows, 0))],
        out_specs=pl.BlockSpec((tm, rows), lambda i, s: (i, s)),
        out_shape=jax.ShapeDtypeStruct((m, N_CHIPS * rows), out_dtype),
        compiler_params=_params(("parallel", "parallel")),
    )(a, wg)


def _mm_nn_w_mod(a3, wg, off, rows, h, gate, gain, shift, scale, tpb_rows, name, tm):
    parts, m, kp = a3.shape
    assert parts * kp == N_CHIPS * rows
    tm = _tile(tpb_rows, tm, 128)
    tpb = tpb_rows // tm
    cuts = sorted({s * rows for s in range(N_CHIPS + 1)} | {p * kp for p in range(parts + 1)})
    pieces = [(k0 // kp, k0 % kp, k0 // rows, k0 % rows, k1 - k0) for k0, k1 in zip(cuts[:-1], cuts[1:])]

    def body(a_ref, w_ref, h_ref, gate_ref, gain_ref, sh_ref, sc_ref, y_ref, hout_ref, hn_ref):
        acc = None
        for p, a0, s, r0, width in pieces:
            term = _dot(a_ref[p, :, a0:a0 + width], w_ref[s, r0:r0 + width, :])
            acc = term if acc is None else acc + term
        y_ref[...] = acc
        hv = h_ref[...] + gate_ref[0] * acc
        hout_ref[...] = hv
        r = lax.rsqrt(jnp.mean(hv * hv, axis=-1, keepdims=True) + EPS)
        hn_ref[...] = ((hv * r) * gain_ref[...] * (1.0 + sc_ref[0]) + sh_ref[0]).astype(BF16)

    row = pl.BlockSpec((tm, D), lambda i: (i, 0))
    per_b = pl.BlockSpec((1, 1, D), lambda i: (i // tpb, 0, 0))
    return pl.pallas_call(
        body, name=name, grid=(m // tm,),
        in_specs=[pl.BlockSpec((parts, tm, kp), lambda i: (0, i, 0)),
                  pl.BlockSpec((N_CHIPS, rows, D), lambda i: (0, off // rows, 0)),
                  row, per_b, pl.BlockSpec((1, D), lambda i: (0, 0)), per_b, per_b],
        out_specs=[row, row, row],
        out_shape=[jax.ShapeDtypeStruct((m, D), F32), jax.ShapeDtypeStruct((m, D), F32),
                   jax.ShapeDtypeStruct((m, D), BF16)],
        compiler_params=_params(("parallel",)),
    )(a3, wg, h, gate, gain, shift, scale)


def _mm_nn_w_final(a3, wg, off, rows, h, gate, gain, tgt, tpb_rows, name, tm):
    parts, m, kp = a3.shape
    assert parts * kp == N_CHIPS * rows
    nb = m // tpb_rows
    tm = _tile(tpb_rows, tm, 128)
    tpb = tpb_rows // tm
    cuts = sorted({s * rows for s in range(N_CHIPS + 1)} | {p * kp for p in range(parts + 1)})
    pieces = [(k0 // kp, k0 % kp, k0 // rows, k0 % rows, k1 - k0) for k0, k1 in zip(cuts[:-1], cuts[1:])]

    def body(a_ref, w_ref, h_ref, gate_ref, gain_ref, tgt_ref, loss_ref, dh_ref, df_ref, dgate_ref, dgain_ref):
        i = pl.program_id(0)

        @pl.when(i == 0)
        def _():
            loss_ref[...] = jnp.zeros_like(loss_ref)
            dgain_ref[...] = jnp.zeros_like(dgain_ref)

        @pl.when(i % tpb == 0)
        def _():
            dgate_ref[...] = jnp.zeros_like(dgate_ref)

        fv = None
        for p, a0, s, r0, width in pieces:
            term = _dot(a_ref[p, :, a0:a0 + width], w_ref[s, r0:r0 + width, :])
            fv = term if fv is None else fv + term
        gate_v = gate_ref[0]
        hv = h_ref[...] + gate_v * fv
        r = lax.rsqrt(jnp.mean(hv * hv, axis=-1, keepdims=True) + EPS)
        y = hv * r
        gain_v = gain_ref[...]
        e = y * gain_v - tgt_ref[...]
        s_ = jnp.sum(_rowsum(e * e), axis=1, keepdims=True) * (0.5 / D)
        loss_ref[...] += jnp.broadcast_to(s_, loss_ref.shape)
        dout = e * (1.0 / D)
        dgain_ref[...] += _rowsum(dout * y)
        dy = dout * gain_v
        dh = r * (dy - y * jnp.mean(dy * y, axis=-1, keepdims=True))
        dh_ref[...] = dh
        df_ref[...] = (dh * gate_v).astype(BF16)
        dgate_ref[0] += _rowsum(dh * fv)

    row = pl.BlockSpec((tm, D), lambda i: (i, 0))
    per_b = pl.BlockSpec((1, 1, D), lambda i: (i // tpb, 0, 0))
    vec = pl.BlockSpec((1, D), lambda i: (0, 0))
    return pl.pallas_call(
        body, name=name, grid=(m // tm,),
        in_specs=[pl.BlockSpec((parts, tm, kp), lambda i: (0, i, 0)),
                  pl.BlockSpec((N_CHIPS, rows, D), lambda i: (0, off // rows, 0)), row, per_b, vec, row],
        out_specs=[pl.BlockSpec((1, 128), lambda i: (0, 0)), row, row, per_b, vec],
        out_shape=[jax.ShapeDtypeStruct((1, 128), F32), jax.ShapeDtypeStruct((m, D), F32),
                   jax.ShapeDtypeStruct((m, D), BF16), jax.ShapeDtypeStruct((nb, 1, D), F32),
                   jax.ShapeDtypeStruct((1, D), F32)],
        compiler_params=_params(("arbitrary",)),
    )(a3, wg, h, gate, gain, tgt)


def _mm_nn_w_modbwd(a3, wg, off, rows, h_in, dh_out, gain, scale, y_prev, gate_prev, tpb_rows, name, tm):
    parts, m, kp = a3.shape
    assert parts * kp == N_CHIPS * rows
    nb = m // tpb_rows
    tm = _tile(tpb_rows, tm, 128)
    tpb = tpb_rows // tm
    cuts = sorted({s * rows for s in range(N_CHIPS + 1)} | {p * kp for p in range(parts + 1)})
    pieces = [(k0 // kp, k0 % kp, k0 // rows, k0 % rows, k1 - k0) for k0, k1 in zip(cuts[:-1], cuts[1:])]

    def body(a_ref, w_ref, h_ref, gain_ref, sc_ref, dho_ref, yp_ref, gp_ref,
             dh_ref, dsc_ref, dsh_ref, dgain_ref, dyp_ref, dgp_ref):
        i = pl.program_id(0)

        @pl.when(i == 0)
        def _():
            dgain_ref[...] = jnp.zeros_like(dgain_ref)

        @pl.when(i % tpb == 0)
        def _():
            dsc_ref[...] = jnp.zeros_like(dsc_ref)
            dsh_ref[...] = jnp.zeros_like(dsh_ref)
            dgp_ref[...] = jnp.zeros_like(dgp_ref)

        g = None
        for p, a0, s, r0, width in pieces:
            term = _dot(a_ref[p, :, a0:a0 + width], w_ref[s, r0:r0 + width, :])
            g = term if g is None else g + term
        hv = h_ref[...]
        r = lax.rsqrt(jnp.mean(hv * hv, axis=-1, keepdims=True) + EPS)
        y = hv * r
        gain_v = gain_ref[...]
        dsh_ref[0] += _rowsum(g)
        dsc_ref[0] += _rowsum(g * (y * gain_v))
        drn = g * (1.0 + sc_ref[0])
        dgain_ref[...] += _rowsum(drn * y)
        dy = drn * gain_v
        dh = r * (dy - y * jnp.mean(dy * y, axis=-1, keepdims=True)) + dho_ref[...]
        dh_ref[...] = dh
        dyp_ref[...] = (dh * gp_ref[0]).astype(BF16)
        dgp_ref[0] += _rowsum(dh * yp_ref[...])

    row = pl.BlockSpec((tm, D), lambda i: (i, 0))
    per_b = pl.BlockSpec((1, 1, D), lambda i: (i // tpb, 0, 0))
    vec = pl.BlockSpec((1, D), lambda i: (0, 0))
    per_b_shape = jax.ShapeDtypeStruct((nb, 1, D), F32)
    outs = pl.pallas_call(
        body, name=name, grid=(m // tm,),
        in_specs=[pl.BlockSpec((parts, tm, kp), lambda i: (0, i, 0)),
                  pl.BlockSpec((N_CHIPS, rows, D), lambda i: (0, off // rows, 0)),
                  row, vec, per_b, row, row, per_b],
        out_specs=[row, per_b, per_b, vec, row, per_b],
        out_shape=[jax.ShapeDtypeStruct((m, D), F32), per_b_shape, per_b_shape, jax.ShapeDtypeStruct((1, D), F32),
                   jax.ShapeDtypeStruct((m, D), BF16), per_b_shape],
        compiler_params=_params(("arbitrary",)),
    )(a3, wg, h_in, gain, scale, dh_out, y_prev, gate_prev)
    return dict(zip(("dh", "dscale", "dshift", "dgain", "dy_prev", "dgate_prev"), outs))


def _mm_dw(a3, b, g_prev, off, rows, name, tm):
    parts, ntok, cdim = a3.shape
    assert parts * cdim == N_CHIPS * rows and cdim % tm == 0 and rows % tm == 0 and off % tm == 0

    def body(a_ref, b_ref, *rest):
        rest[-1][0] = _dot(a_ref[0], b_ref[...], _TN)

    in_specs = [pl.BlockSpec((1, ntok, tm), lambda i: ((i * tm) // cdim, 0, ((i * tm) % cdim) // tm)),
                pl.BlockSpec((ntok, D), lambda i: (0, 0))]
    args = [a3, b]
    aliases = {}
    if g_prev is not None:
        in_specs.append(pl.BlockSpec(memory_space=pl.ANY))
        args.append(g_prev)
        aliases = {2: 0}
    return pl.pallas_call(
        body, name=name, grid=(N_CHIPS * rows // tm,),
        in_specs=in_specs,
        out_specs=pl.BlockSpec((1, tm, D), lambda i: ((i * tm) // rows, (off + (i * tm) % rows) // tm, 0)),
        out_shape=jax.ShapeDtypeStruct((N_CHIPS, _MAIN_TOTAL, D), F32),
        input_output_aliases=aliases,
        compiler_params=_params(("parallel",)),
    )(*args)


def _mod_fwd(h, gain, shift, scale, tpb_rows, name, y=None, gate=None, out_rows=None, into=None, row0=0):
    n = h.shape[0]
    tt = _tile(tpb_rows, 256)
    tpb = tpb_rows // tt
    has_res = y is not None
    assert row0 % tt == 0 and not (has_res and out_rows)

    def body(*refs):
        if has_res:
            h_ref, y_ref, gate_ref, gain_ref, sh_ref, sc_ref, hout_ref, hn_ref = refs
            hv = h_ref[...] + gate_ref[0] * y_ref[...]
            hout_ref[...] = hv
        else:
            h_ref, gain_ref, sh_ref, sc_ref, hn_ref = refs[0], refs[1], refs[2], refs[3], refs[-1]
            hv = h_ref[...]
        r = lax.rsqrt(jnp.mean(hv * hv, axis=-1, keepdims=True) + EPS)
        hn = (hv * r) * gain_ref[...] * (1.0 + sc_ref[0]) + sh_ref[0]
        hn_ref[...] = hn.astype(BF16)

    row = pl.BlockSpec((tt, D), lambda i: (i, 0))
    per_b = pl.BlockSpec((1, 1, D), lambda i: (i // tpb, 0, 0))
    vec = pl.BlockSpec((1, D), lambda i: (0, 0))
    if has_res:
        in_specs = [row, row, per_b, vec, per_b, per_b]
        args = (h, y, gate, gain, shift, scale)
        out_specs = [row, row]
        out_shape = [jax.ShapeDtypeStruct((n, D), F32), jax.ShapeDtypeStruct((n, D), BF16)]
    else:
        in_specs = [row, vec, per_b, per_b]
        args = (h, gain, shift, scale)
        out_specs = pl.BlockSpec((tt, D), lambda i: (i + row0 // tt, 0))
        out_shape = jax.ShapeDtypeStruct((out_rows or n, D), BF16)
    aliases = {}
    if into is not None:
        in_specs = in_specs + [pl.BlockSpec(memory_space=pl.ANY)]
        args = args + (into,)
        aliases = {4: 0}
    return pl.pallas_call(
        body, name=name, grid=(n // tt,), in_specs=in_specs, out_specs=out_specs, out_shape=out_shape,
        input_output_aliases=aliases, compiler_params=_params(("parallel",)),
    )(*args)


def _mod_bwd(h_in, dhn, gain, scale, tpb_rows, name, dhn_row0=0, dh_out=None, y_prev=None, gate_prev=None,
             need_dh=True):
    n = h_in.shape[0]
    nb = n // tpb_rows
    tt = _tile(tpb_rows, 256)
    tpb = tpb_rows // tt
    off = dhn_row0 // tt
    assert dhn_row0 % tt == 0
    has_out = dh_out is not None
    has_prev = y_prev is not None

    def body(*refs):
        it = iter(refs)
        h_ref, dhn_ref, gain_ref, sc_ref = next(it), next(it), next(it), next(it)
        dho_ref = next(it) if has_out else None
        yp_ref, gp_ref = (next(it), next(it)) if has_prev else (None, None)
        dh_ref = next(it) if need_dh else None
        dsc_ref, dsh_ref, dgain_ref = next(it), next(it), next(it)
        dyp_ref, dgp_ref = (next(it), next(it)) if has_prev else (None, None)
        i = pl.program_id(0)

        @pl.when(i == 0)
        def _():
            dgain_ref[...] = jnp.zeros_like(dgain_ref)

        @pl.when(i % tpb == 0)
        def _():
            dsc_ref[...] = jnp.zeros_like(dsc_ref)
            dsh_ref[...] = jnp.zeros_like(dsh_ref)
            if has_prev:
                dgp_ref[...] = jnp.zeros_like(dgp_ref)

        hv = h_ref[...]
        r = lax.rsqrt(jnp.mean(hv * hv, axis=-1, keepdims=True) + EPS)
        y = hv * r
        gain_v = gain_ref[...]
        g = dhn_ref[...].astype(F32)
        dsh_ref[0] += _rowsum(g)
        dsc_ref[0] += _rowsum(g * (y * gain_v))
        drn = g * (1.0 + sc_ref[0])
        dgain_ref[...] += _rowsum(drn * y)
        if need_dh:
            dy = drn * gain_v
            dh = r * (dy - y * jnp.mean(dy * y, axis=-1, keepdims=True))
            if has_out:
                dh = dh + dho_ref[...]
            dh_ref[...] = dh
            if has_prev:
                dyp_ref[...] = (dh * gp_ref[0]).astype(BF16)
                dgp_ref[0] += _rowsum(dh * yp_ref[...])

    row = pl.BlockSpec((tt, D), lambda i: (i, 0))
    row_off = pl.BlockSpec((tt, D), lambda i: (i + off, 0))
    per_b = pl.BlockSpec((1, 1, D), lambda i: (i // tpb, 0, 0))
    vec = pl.BlockSpec((1, D), lambda i: (0, 0))
    in_specs = [row, row_off, vec, per_b]
    args = [h_in, dhn, gain, scale]
    if has_out:
        in_specs.append(row)
        args.append(dh_out)
    if has_prev:
        in_specs += [row, per_b]
        args += [y_prev, gate_prev]
    out_specs, out_shape, names = [], [], []
    if need_dh:
        out_specs.append(row)
        out_shape.append(jax.ShapeDtypeStruct((n, D), F32))
        names.append("dh")
    for nm in ("dscale", "dshift"):
        out_specs.append(per_b)
        out_shape.append(jax.ShapeDtypeStruct((nb, 1, D), F32))
        names.append(nm)
    out_specs.append(vec)
    out_shape.append(jax.ShapeDtypeStruct((1, D), F32))
    names.append("dgain")
    if has_prev:
        out_specs += [row, per_b]
        out_shape += [jax.ShapeDtypeStruct((n, D), BF16), jax.ShapeDtypeStruct((nb, 1, D), F32)]
        names += ["dy_prev", "dgate_prev"]
    outs = pl.pallas_call(
        body, name=name, grid=(n // tt,), in_specs=in_specs, out_specs=out_specs, out_shape=out_shape,
        compiler_params=_params(("arbitrary",)),
    )(*args)
    return dict(zip(names, outs))


def _row_dn1(x):
    t = lax.broadcasted_iota(jnp.int32, x.shape, 0)
    return jnp.where(t % GRID_W == 0, 0.0, pltpu.roll(x, 1, 0))


def _row_up1(x):
    t = lax.broadcasted_iota(jnp.int32, x.shape, 0)
    return jnp.where(t % GRID_W == GRID_W - 1, 0.0, pltpu.roll(x, x.shape[0] - 1, 0))


def _silu(x):
    return x * _sigmoid(x)


def _dsilu(x):
    s = _sigmoid(x)
    return s * (1.0 + x * (1.0 - s))


def _row_ds(i):
    start = i * GRID_W
    return pl.ds(start if isinstance(start, int) else pl.multiple_of(start, GRID_W), GRID_W)


def _grid_row(ref, i, first, last):
    def rows(k):
        return ref[_row_ds(k), :].astype(F32)

    cur = rows(i)
    return (jnp.zeros_like(cur) if first else rows(i - 1)), cur, (jnp.zeros_like(cur) if last else rows(i + 1))


def _over_grid_rows(n_rows, step, carry):
    carry = step(0, carry, True, n_rows == 1)
    if n_rows > 2:
        carry = lax.fori_loop(1, n_rows - 1, lambda i, c: step(i, c, False, False), carry)
    if n_rows > 1:
        carry = step(n_rows - 1, carry, False, True)
    return carry


def _fold8(p):
    return p.reshape(GRID_W // 8, 8, p.shape[1]).sum(axis=0)


def _ffn_up_mid_fwd(hn, wg, off, cw, cb, nb, t, name):
    tcol = 256
    ncol = HID // tcol
    rows_sh = 2 * HID // N_CHIPS

    def conv(x, w_ref):
        zeros = jnp.zeros((GRID_W, x.shape[1]), x.dtype)
        down = jnp.concatenate([zeros, x[: x.shape[0] - GRID_W]], axis=0)
        up = jnp.concatenate([x[GRID_W:], zeros], axis=0)
        return down * w_ref[0:1, :] + x * w_ref[1:2, :] + up * w_ref[2:3, :]

    def body(h_ref, wa_ref, wg_ref, cwa_ref, cwg_ref, cba_ref, cbg_ref, u_ref, z_ref):
        hv = h_ref[...]
        ua = _dot(hv, wa_ref[0], _NT)
        ug = _dot(hv, wg_ref[0], _NT)
        u_ref[0] = ua.astype(BF16)
        u_ref[1] = ug.astype(BF16)
        a = conv(ua, cwa_ref) + cba_ref[...]
        gt = conv(ug, cwg_ref) + cbg_ref[...]
        z_ref[...] = (a * _silu(gt)).astype(BF16)

    def w_spec(part):
        def idx(b, j):
            n = part * HID + j * tcol
            return (n // rows_sh, (off + n % rows_sh) // tcol, 0)
        return pl.BlockSpec((1, tcol, D), idx)

    chan = lambda rows, part: pl.BlockSpec((rows, tcol), lambda b, j: (0, part * ncol + j))
    return pl.pallas_call(
        body, name=name, grid=(nb, ncol),
        in_specs=[pl.BlockSpec((t, D), lambda b, j: (b, 0)), w_spec(0), w_spec(1),
                  chan(3, 0), chan(3, 1), chan(1, 0), chan(1, 1)],
        out_specs=[pl.BlockSpec((2, t, tcol), lambda b, j: (0, b, j)), pl.BlockSpec((t, tcol), lambda b, j: (b, j))],
        out_shape=[jax.ShapeDtypeStruct((2, nb * t, HID), BF16), jax.ShapeDtypeStruct((nb * t, HID), BF16)],
        compiler_params=_params(("parallel", "parallel")),
    )(hn, wg, wg, cw, cw, cb, cb)


def _ffn_mid_bwd(u0, cw, cb, dz, nb, t, name):
    nc = HID // 128
    n_rows = t // GRID_W

    def body(ua3_ref, ug3_ref, wa_ref, wg_ref, ba_ref, bg_ref, dz_ref, du_ref, dw_ref, db_ref, dua_ref, dug_ref):
        ua_ref, ug_ref = ua3_ref.at[0], ug3_ref.at[0]
        b = pl.program_id(1)

        @pl.when(b == 0)
        def _():
            dw_ref[...] = jnp.zeros_like(dw_ref)
            db_ref[...] = jnp.zeros_like(db_ref)

        wa = [wa_ref[k:k + 1, :] for k in range(3)]
        wg = [wg_ref[k:k + 1, :] for k in range(3)]
        ba, bg = ba_ref[...], bg_ref[...]

        def pass1(i, acc, first, last):
            here = _row_ds(i)
            ap, ac, an = _grid_row(ua_ref, i, first, last)
            gp, gc, gn = _grid_row(ug_ref, i, first, last)
            a = ap * wa[0] + ac * wa[1] + an * wa[2] + ba
            gt = gp * wg[0] + gc * wg[1] + gn * wg[2] + bg
            dzv = dz_ref[here, :].astype(F32)
            s = _sigmoid(gt)
            silu = gt * s
            da = dzv * silu
            dg = (dzv * a) * (s + silu * (1.0 - s))
            dua_ref[here, :] = da
            dug_ref[here, :] = dg
            terms = (da, da * ap, da * ac, da * an, dg, dg * gp, dg * gc, dg * gn)
            return tuple(r + _fold8(p) for r, p in zip(acc, terms))

        zero = jnp.zeros((8, 128), F32)
        acc = _over_grid_rows(n_rows, pass1, (zero,) * 8)
        for part in range(2):
            db_ref[part] += _rowsum(acc[4 * part])
            for k in range(3):
                dw_ref[part, k:k + 1, :] += _rowsum(acc[4 * part + 1 + k])

        def pass2(i, carry, first, last):
            for part, (ref, w) in enumerate(((dua_ref, wa), (dug_ref, wg))):
                dp_, dc_, dn_ = _grid_row(ref, i, first, last)
                du_ref[part, _row_ds(i), :] = (dn_ * w[0] + dc_ * w[1] + dp_ * w[2]).astype(BF16)
            return carry

        _over_grid_rows(n_rows, pass2, 0)

    col = lambda rows, part: pl.BlockSpec((rows, 128), lambda j, b: (0, part * nc + j))
    part_of_u = lambda part: pl.BlockSpec((1, t, 128), lambda j, b: (part, b, j))
    return pl.pallas_call(
        body, name=name, grid=(nc, nb),
        in_specs=[part_of_u(0), part_of_u(1), col(3, 0), col(3, 1), col(1, 0), col(1, 1),
                  pl.BlockSpec((t, 128), lambda j, b: (b, j))],
        out_specs=[pl.BlockSpec((2, t, 128), lambda j, b: (0, b, j)), pl.BlockSpec((2, 3, 128), lambda j, b: (0, 0, j)),
                   pl.BlockSpec((2, 1, 128), lambda j, b: (0, 0, j))],
        out_shape=[jax.ShapeDtypeStruct((2, nb * t, HID), BF16), jax.ShapeDtypeStruct((2, 3, HID), F32),
                   jax.ShapeDtypeStruct((2, 1, HID), F32)],
        scratch_shapes=[pltpu.VMEM((t, 128), F32), pltpu.VMEM((t, 128), F32)],
        compiler_params=_params(("parallel", "arbitrary")),
    )(u0, u0, cw, cw, cb, cb, dz)


def _sc_in_mid_fwd(hn, wg, off, cw, nb, t):
    tcol = 256
    ncol = D // tcol
    rows_sh = 3 * D // N_CHIPS

    def body(h_ref, wb_ref, wc_ref, wv_ref, cw_ref, p_ref, y_ref):
        hv = h_ref[...]
        bg = _dot(hv, wb_ref[0], _NT)
        cg = _dot(hv, wc_ref[0], _NT)
        v = _dot(hv, wv_ref[0], _NT)
        p_ref[0] = bg.astype(BF16)
        p_ref[1] = cg.astype(BF16)
        p_ref[2] = v.astype(BF16)
        cv = cg * v
        cc = _row_dn1(cv) * cw_ref[0:1, :] + cv * cw_ref[1:2, :] + _row_up1(cv) * cw_ref[2:3, :]
        y_ref[...] = (bg * cc).astype(BF16)

    def w_spec(part):
        def idx(b, j):
            n = part * D + j * tcol
            return (n // rows_sh, (off + n % rows_sh) // tcol, 0)
        return pl.BlockSpec((1, tcol, D), idx)

    return pl.pallas_call(
        body, name="sc_in_mid", grid=(nb, ncol),
        in_specs=[pl.BlockSpec((t, D), lambda b, j: (b, 0)), w_spec(0), w_spec(1), w_spec(2),
                  pl.BlockSpec((3, tcol), lambda b, j: (0, j))],
        out_specs=[pl.BlockSpec((3, t, tcol), lambda b, j: (0, b, j)), pl.BlockSpec((t, tcol), lambda b, j: (b, j))],
        out_shape=[jax.ShapeDtypeStruct((3, nb * t, D), BF16), jax.ShapeDtypeStruct((nb * t, D), BF16)],
        compiler_params=_params(("parallel", "parallel")),
    )(hn, wg, wg, wg, cw)


def _sc_mid_bwd(p, cw, dyb, nb, t):
    nc = D // 128

    def body(bg3_ref, cg3_ref, v3_ref, w_ref, dy_ref, dp_ref, dw_ref):
        bg_ref, cg_ref, v_ref = bg3_ref.at[0], cg3_ref.at[0], v3_ref.at[0]
        b = pl.program_id(1)

        @pl.when(b == 0)
        def _():
            dw_ref[...] = jnp.zeros_like(dw_ref)

        w0, w1, w2 = w_ref[0:1, :], w_ref[1:2, :], w_ref[2:3, :]
        cg, v = cg_ref[...].astype(F32), v_ref[...].astype(F32)
        cv = cg * v
        cvd = _row_dn1(cv)
        cvu = _row_up1(cv)
        cc = cvd * w0 + cv * w1 + cvu * w2
        dy = dy_ref[...].astype(F32)
        dcc = dy * bg_ref[...].astype(F32)
        dw_ref[0:1, :] += _rowsum(dcc * cvd)
        dw_ref[1:2, :] += _rowsum(dcc * cv)
        dw_ref[2:3, :] += _rowsum(dcc * cvu)
        dcv = _row_up1(dcc) * w0 + dcc * w1 + _row_dn1(dcc) * w2
        dp_ref[0] = (dy * cc).astype(BF16)
        dp_ref[1] = (dcv * v).astype(BF16)
        dp_ref[2] = (dcv * cg).astype(BF16)

    part = lambda k: pl.BlockSpec((1, t, 128), lambda j, b: (k, b, j))
    return pl.pallas_call(
        body, name="sc_mid_bwd", grid=(nc, nb),
        in_specs=[part(0), part(1), part(2), pl.BlockSpec((3, 128), lambda j, b: (0, j)),
                  pl.BlockSpec((t, 128), lambda j, b: (b, j))],
        out_specs=[pl.BlockSpec((3, t, 128), lambda j, b: (0, b, j)), pl.BlockSpec((3, 128), lambda j, b: (0, j))],
        out_shape=[jax.ShapeDtypeStruct((3, nb * t, D), BF16), jax.ShapeDtypeStruct((3, D), F32)],
        compiler_params=_params(("parallel", "arbitrary")),
    )(p, p, p, cw, dyb)


def _gla_in_proj(hn_all, w_gin, w2, b2):
    n = hn_all.shape[0]
    tm = _tile(n, 768, 128)

    def body(h_ref, w_ref, w2_ref, b2_ref, p_ref, la_ref):
        p = _dot(h_ref[...], w_ref[...], _NT)
        p_ref[...] = p
        z = _dot(p[:, 2 * KEY + 2 * D:], w2_ref[...]) + b2_ref[...]
        la_ref[...] = (jnp.minimum(z, 0.0) - jnp.log(1.0 + jnp.exp(-jnp.abs(z)))) * (1.0 / TAU)

    return pl.pallas_call(
        body, name="gla_in_proj", grid=(n // tm,),
        in_specs=[pl.BlockSpec((tm, D), lambda i: (i, 0)), pl.BlockSpec((GLA_IN_PAD, D), lambda i: (0, 0)),
                  pl.BlockSpec((128, 2 * KEY), lambda i: (0, 0)), pl.BlockSpec((1, 2 * KEY), lambda i: (0, 0))],
        out_specs=[pl.BlockSpec((tm, GLA_IN_PAD), lambda i: (i, 0)), pl.BlockSpec((tm, 2 * KEY), lambda i: (i, 0))],
        out_shape=[jax.ShapeDtypeStruct((n, GLA_IN_PAD), F32), jax.ShapeDtypeStruct((n, 2 * KEY), F32)],
        compiler_params=_params(("parallel",)),
    )(hn_all, w_gin, w2, b2)


def _gla_blocks(nb, nm, ncx):
    def main_idx(d, i):
        return jnp.clip(jnp.where(d == 0, i - ncx, nm - 1 - (i - ncx)), 0, nm - 1)

    def rowblk(d, b, i):
        cidx = jnp.where(d == 0, i, ncx - 1 - i)
        return jnp.where(i < ncx, nb * nm + b * ncx + cidx, b * nm + main_idx(d, i))

    def mainblk(d, b, i):
        return b * nm + main_idx(d, i)

    return rowblk, mainblk


def _gla_mask(d):
    row = lax.broadcasted_iota(jnp.int32, (CH, CH), 0)
    col = lax.broadcasted_iota(jnp.int32, (CH, CH), 1)
    diff = jnp.where(d == 0, row - col, col - row)
    mask = diff >= 0
    return mask, jnp.where(mask, 1.0, 0.0).astype(BF16), jnp.where(diff <= 0, 1.0, 0.0).astype(BF16)


def _tri_sum(m01, x):
    w = x.shape[1]
    hi = x.astype(BF16)
    r1 = x - hi.astype(F32)
    mid = r1.astype(BF16)
    lo = (r1 - mid.astype(F32)).astype(BF16)
    s = lax.dot_general(m01, jnp.concatenate([hi, mid, lo], axis=1), _NN, preferred_element_type=F32)
    return s[:, :w] + s[:, w:2 * w] + s[:, 2 * w:]


def _gla_chunk(q, k, g, bc):
    bl = _rowsum(g)
    eq = jnp.exp(bc)
    ek = jnp.exp(-bc)
    ed = jnp.exp(bl - bc)
    return bl, eq, ek, ed, q * Q_SCALE * eq, k * ek, k * ed


def _gla_scan_fwd(p_all, la_all, nb, t, tc):
    nm, ncx = t // CH, tc // CH
    nst = nm + ncx
    rowblk, mainblk = _gla_blocks(nb, nm, ncx)

    def body(*refs):
        ins, (o_refs, ss_refs, st_ref) = refs[:8], (refs[8:10], refs[10:12], refs[12])
        i = pl.program_id(1)

        @pl.when(i == 0)
        def _():
            st_ref[...] = jnp.zeros_like(st_ref)

        loaded = [r[...] for r in ins]
        states = [st_ref[j] for j in range(2 * HEADS)]
        outs, new_states = [[], []], []
        for d in range(2):
            q_all, k_all, v_all, g_all = loaded[4 * d:4 * d + 4]
            mask, m01, _ = _gla_mask(d)
            bc_all = _tri_sum(m01, g_all)
            for h in range(HEADS):
                ksl = slice(h * DK, (h + 1) * DK)
                v = v_all[:, h * DV:(h + 1) * DV]
                st = states[d * HEADS + h]
                bl, _, _, _, qs, ks, kd = _gla_chunk(q_all[:, ksl], k_all[:, ksl], g_all[:, ksl], bc_all[:, ksl])
                att = jnp.where(mask, _dot(qs, ks, _NT), 0.0)
                outs[d].append(_dot(qs, st, _NT) + _dot(att, v))
                new_states.append(st * jnp.exp(bl) + _dot(v, kd, _TN))
        for d in range(2):
            o_refs[d][...] = jnp.concatenate(outs[d], axis=1)
            for h in range(HEADS):
                ss_refs[d][0, 0, h] = states[d * HEADS + h]
                st_ref[d * HEADS + h] = new_states[d * HEADS + h]

    def in_specs(d):
        return [pl.BlockSpec((CH, KEY), lambda b, i: (rowblk(d, b, i), 0)),
                pl.BlockSpec((CH, KEY), lambda b, i: (rowblk(d, b, i), 1)),
                pl.BlockSpec((CH, D), lambda b, i: (rowblk(d, b, i), 1)),
                pl.BlockSpec((CH, KEY), lambda b, i: (rowblk(d, b, i), d))]

    outs = pl.pallas_call(
        body, name="gla_scan_fwd", grid=(nb, nst),
        in_specs=in_specs(0) + in_specs(1),
        out_specs=[pl.BlockSpec((CH, D), lambda b, i: (mainblk(0, b, i), 0)),
                   pl.BlockSpec((CH, D), lambda b, i: (mainblk(1, b, i), 0)),
                   pl.BlockSpec((1, 1, HEADS, DV, DK), lambda b, i: (b, i, 0, 0, 0)),
                   pl.BlockSpec((1, 1, HEADS, DV, DK), lambda b, i: (b, i, 0, 0, 0))],
        out_shape=[jax.ShapeDtypeStruct((nb * t, D), F32)] * 2
        + [jax.ShapeDtypeStruct((nb, nst, HEADS, DV, DK), F32)] * 2,
        scratch_shapes=[pltpu.VMEM((2 * HEADS, DV, DK), F32)],
        compiler_params=_params(("parallel", "arbitrary")),
    )(*([p_all, p_all, p_all, la_all] * 2))
    return outs[:2], outs[2:]


def _gla_scan_bwd(p_all, la_all, do, ss, nb, t, tc, after):
    nm, ncx = t // CH, tc // CH
    nst = nm + ncx
    ntot = nb * (t + tc)
    rowblk, mainblk = _gla_blocks(nb, nm, ncx)

    def body(*refs):
        ins, outs, dst_ref = refs[:12], refs[13:21], refs[21]
        ip = pl.program_id(1)
        i = nst - 1 - ip

        @pl.when(ip == 0)
        def _():
            dst_ref[...] = jnp.zeros_like(dst_ref)

        live = jnp.where(i >= ncx, 1.0, 0.0)
        loaded = [[r[...] for r in ins[6 * d:6 * d + 5]] for d in range(2)]
        states = [ins[6 * d + 5][0, 0, h] for d in range(2) for h in range(HEADS)]
        dstates = [dst_ref[j] for j in range(2 * HEADS)]
        results, new_dstates = [], []
        for d in range(2):
            q_all, k_all, v_all, g_all, do_all = loaded[d]
            do_all = do_all * live
            mask, m01, m01_t = _gla_mask(d)
            bc_all = _tri_sum(m01, g_all)
            dqs_l, dks_l, dvs_l, dbs_l, dbls_l = [], [], [], [], []
            for h in range(HEADS):
                ksl = slice(h * DK, (h + 1) * DK)
                vsl = slice(h * DV, (h + 1) * DV)
                bl, eq, ek, ed, qs, ks, kd = _gla_chunk(q_all[:, ksl], k_all[:, ksl], g_all[:, ksl], bc_all[:, ksl])
                st, dst, v, dov = states[d * HEADS + h], dstates[d * HEADS + h], v_all[:, vsl], do_all[:, vsl]
                att = jnp.where(mask, _dot(qs, ks, _NT), 0.0)
                datt = jnp.where(mask, _dot(dov, v, _NT), 0.0)
                dqs = _dot(dov, st) + _dot(datt, ks)
                dks = _dot(datt, qs, _TN)
                dvs_l.append(_dot(att, dov, _TN) + _dot(kd, dst, _NT))
                dkd = _dot(v, dst)
                e = jnp.exp(bl)
                dbls_l.append(e * _rowsum(st * dst) + _rowsum(dkd * kd))
                new_dstates.append(_dot(dov, qs, _TN) + dst * e)
                dqs_l.append(dqs * eq * Q_SCALE)
                dks_l.append(dks * ek + dkd * ed)
                dbs_l.append(dqs * qs - dks * ks - dkd * kd)
            results.append((jnp.concatenate(dqs_l, axis=1), jnp.concatenate(dks_l, axis=1),
                            jnp.concatenate(dvs_l, axis=1),
                            _tri_sum(m01_t, jnp.concatenate(dbs_l, axis=1)) + jnp.concatenate(dbls_l, axis=1)))
        for d in range(2):
            for k in range(4):
                outs[4 * d + k][...] = results[d][k]
        for j in range(2 * HEADS):
            dst_ref[j] = new_dstates[j]

    def in_specs(d):
        return [pl.BlockSpec((CH, KEY), lambda b, ip: (rowblk(d, b, nst - 1 - ip), 0)),
                pl.BlockSpec((CH, KEY), lambda b, ip: (rowblk(d, b, nst - 1 - ip), 1)),
                pl.BlockSpec((CH, D), lambda b, ip: (rowblk(d, b, nst - 1 - ip), 1)),
                pl.BlockSpec((CH, KEY), lambda b, ip: (rowblk(d, b, nst - 1 - ip), d)),
                pl.BlockSpec((CH, D), lambda b, ip: (mainblk(d, b, nst - 1 - ip), 0)),
                pl.BlockSpec((1, 1, HEADS, DV, DK), lambda b, ip: (b, nst - 1 - ip, 0, 0, 0))]

    def out_specs(d):
        row = lambda width: pl.BlockSpec((CH, width), lambda b, ip: (rowblk(d, b, nst - 1 - ip), 0))
        return [row(KEY), row(KEY), row(D), row(KEY)]

    shapes = [jax.ShapeDtypeStruct((ntot, KEY), F32), jax.ShapeDtypeStruct((ntot, KEY), F32),
              jax.ShapeDtypeStruct((ntot, D), F32), jax.ShapeDtypeStruct((ntot, KEY), F32)]
    outs = pl.pallas_call(
        body, name="gla_scan_bwd", grid=(nb, nst),
        in_specs=in_specs(0) + in_specs(1) + [pl.BlockSpec(memory_space=pl.ANY)],
        out_specs=out_specs(0) + out_specs(1),
        out_shape=shapes * 2,
        scratch_shapes=[pltpu.VMEM((2 * HEADS, DV, DK), F32)],
        compiler_params=_params(("parallel", "arbitrary")),
    )(p_all, p_all, p_all, la_all, do, ss[0], p_all, p_all, p_all, la_all, do, ss[1], after)
    return [[outs[k], outs[4 + k]] for k in range(4)]


def _gla_post_fwd(o2, p_all, head_gain, n):
    tt = _tile(n, 256)

    def body(of_ref, ob_ref, g_ref, hg_ref, y_ref):
        o = of_ref[...] + ob_ref[...]
        gv = g_ref[...]
        hg = hg_ref[...]
        for h in range(HEADS):
            oh = o[:, h * DV:(h + 1) * DV]
            r = lax.rsqrt(jnp.mean(oh * oh, axis=-1, keepdims=True) + EPS)
            y_ref[:, h * DV:(h + 1) * DV] = ((oh * r) * hg * _silu(gv[:, h * DV:(h + 1) * DV])).astype(BF16)

    row = pl.BlockSpec((tt, D), lambda i: (i, 0))
    return pl.pallas_call(
        body, name="gla_post_fwd", grid=(n // tt,),
        in_specs=[row, row, pl.BlockSpec((tt, D), lambda i: (i, 2)), pl.BlockSpec((1, DV), lambda i: (0, 0))],
        out_specs=row,
        out_shape=jax.ShapeDtypeStruct((n, D), BF16),
        compiler_params=_params(("parallel",)),
    )(o2[0], o2[1], p_all, head_gain)


def _gla_out_dx_post_bwd(dy, wg, off, o2, p_all, head_gain, n):
    tt = _tile(n, 256)

    def body(dy_ref, w_ref, of_ref, ob_ref, g_ref, hg_ref, do_ref, dg_ref, dhg_ref):
        i = pl.program_id(0)

        @pl.when(i == 0)
        def _():
            dhg_ref[...] = jnp.zeros_like(dhg_ref)

        dyv = dy_ref[...]
        o = of_ref[...] + ob_ref[...]
        gv = g_ref[...]
        hg = hg_ref[...]
        acc = jnp.zeros((1, DV), F32)
        for h in range(HEADS):
            sl = slice(h * DV, (h + 1) * DV)
            dyh = _dot(dyv, w_ref[h], _NT)
            oh = o[:, sl]
            r = lax.rsqrt(jnp.mean(oh * oh, axis=-1, keepdims=True) + EPS)
            on = oh * r
            gh = gv[:, sl]
            dg_ref[:, sl] = dyh * (on * hg) * _dsilu(gh)
            dog = dyh * _silu(gh)
            acc = acc + _rowsum(dog * on)
            don = dog * hg
            do_ref[:, sl] = r * (don - on * jnp.mean(don * on, axis=-1, keepdims=True))
        dhg_ref[...] += acc

    row = pl.BlockSpec((tt, D), lambda i: (i, 0))
    return pl.pallas_call(
        body, name="gla_out_dx_post_bwd", grid=(n // tt,),
        in_specs=[row, pl.BlockSpec((N_CHIPS, DV, D), lambda i: (0, off // DV, 0)), row, row,
                  pl.BlockSpec((tt, D), lambda i: (i, 2)), pl.BlockSpec((1, DV), lambda i: (0, 0))],
        out_specs=[row, row, pl.BlockSpec((1, DV), lambda i: (0, 0))],
        out_shape=[jax.ShapeDtypeStruct((n, D), F32), jax.ShapeDtypeStruct((n, D), F32),
                   jax.ShapeDtypeStruct((1, DV), F32)],
        compiler_params=_params(("arbitrary",)),
    )(dy, wg, o2[0], o2[1], p_all, head_gain)


def _gla_assemble(p_all, w2, b2, dq, dk, dv, dla, dgate, n):
    ntot = p_all.shape[0]
    tt = _tile(n, 128)
    nmain = n // tt
    assert ntot % tt == 0

    def body(a_ref, w_ref, b_ref, dqf_ref, dqb_ref, dkf_ref, dkb_ref, dvf_ref, dvb_ref, dlf_ref, dlb_ref, dg_ref,
             dp_ref, dw_ref, db_ref):
        i = pl.program_id(0)

        @pl.when(i == 0)
        def _():
            dw_ref[...] = jnp.zeros_like(dw_ref)
            db_ref[...] = jnp.zeros_like(db_ref)

        a = a_ref[...]
        w = w_ref[...]
        z = _dot(a, w) + b_ref[...]
        dla = jnp.concatenate([dlf_ref[...], dlb_ref[...]], axis=1)
        dz = dla * (1.0 / (1.0 + jnp.exp(z))) * (1.0 / TAU)
        dw_ref[...] += _dot(a, dz, _TN)
        db_ref[...] += _rowsum(dz)
        dp_ref[:, 0:KEY] = (dqf_ref[...] + dqb_ref[...]).astype(BF16)
        dp_ref[:, KEY:2 * KEY] = (dkf_ref[...] + dkb_ref[...]).astype(BF16)
        dp_ref[:, 2 * KEY:2 * KEY + D] = (dvf_ref[...] + dvb_ref[...]).astype(BF16)
        dp_ref[:, 2 * KEY + D:2 * KEY + 2 * D] = (dg_ref[...] * jnp.where(i < nmain, 1.0, 0.0)).astype(BF16)
        dp_ref[:, 2 * KEY + 2 * D:GLA_IN_PAD] = _dot(dz, w, _NT).astype(BF16)

    row = lambda width: pl.BlockSpec((tt, width), lambda i: (i, 0))
    return pl.pallas_call(
        body, name="gla_assemble", grid=(ntot // tt,),
        in_specs=[pl.BlockSpec((tt, 128), lambda i: (i, (2 * KEY + 2 * D) // 128)),
                  pl.BlockSpec((128, 2 * KEY), lambda i: (0, 0)), pl.BlockSpec((1, 2 * KEY), lambda i: (0, 0)),
                  row(KEY), row(KEY), row(KEY), row(KEY), row(D), row(D), row(KEY), row(KEY),
                  pl.BlockSpec((tt, D), lambda i: (jnp.minimum(i, nmain - 1), 0))],
        out_specs=[pl.BlockSpec((tt, GLA_IN_PAD), lambda i: (i, 0)), pl.BlockSpec((128, 2 * KEY), lambda i: (0, 0)),
                   pl.BlockSpec((1, 2 * KEY), lambda i: (0, 0))],
        out_shape=[jax.ShapeDtypeStruct((ntot, GLA_IN_PAD), BF16), jax.ShapeDtypeStruct((128, 2 * KEY), F32),
                   jax.ShapeDtypeStruct((1, 2 * KEY), F32)],
        compiler_params=_params(("arbitrary",)),
    )(p_all, w2, b2, dq[0], dq[1], dk[0], dk[1], dv[0], dv[1], dla[0], dla[1], dgate)


ADA_ROWS = 24
ADA_SH = N_MOD * D // N_CHIPS


def _ada_fwd(cvec, ada_w, ada_b_sh):
    def body(c_ref, w_ref, b_ref, o_ref):
        o_ref[0] = _dot(_silu(c_ref[...]), w_ref[0]) + b_ref[0]

    return pl.pallas_call(
        body, name="ada_fwd", grid=(2,),
        in_specs=[pl.BlockSpec((ADA_ROWS, D), lambda l: (0, 0)), pl.BlockSpec((1, D, ADA_SH), lambda l: (l, 0, 0)),
                  pl.BlockSpec((1, 1, ADA_SH), lambda l: (l, 0, 0))],
        out_specs=pl.BlockSpec((1, ADA_ROWS, ADA_SH), lambda l: (l, 0, 0)),
        out_shape=jax.ShapeDtypeStruct((2, ADA_ROWS, ADA_SH), F32),
        compiler_params=_params(("parallel",)),
    )(cvec, ada_w, ada_b_sh)


def _ada_bwd(cvec, ada_w, dmod_sh):
    def body(c_ref, w_ref, dm_ref, gw_ref, dc_ref):
        dm = dm_ref[0]
        gw_ref[0] = _dot(_silu(c_ref[...]), dm, _TN)
        dc_ref[0] = _dot(dm, w_ref[0], _NT)

    return pl.pallas_call(
        body, name="ada_bwd", grid=(2,),
        in_specs=[pl.BlockSpec((ADA_ROWS, D), lambda l: (0, 0)), pl.BlockSpec((1, D, ADA_SH), lambda l: (l, 0, 0)),
                  pl.BlockSpec((1, ADA_ROWS, ADA_SH), lambda l: (l, 0, 0))],
        out_specs=[pl.BlockSpec((1, D, ADA_SH), lambda l: (l, 0, 0)), pl.BlockSpec((1, ADA_ROWS, D), lambda l: (l, 0, 0))],
        out_shape=[jax.ShapeDtypeStruct((2, D, ADA_SH), F32), jax.ShapeDtypeStruct((2, ADA_ROWS, D), F32)],
        compiler_params=_params(("parallel",)),
    )(cvec, ada_w, dmod_sh)


def _sum_slots(x, name):
    s, r, _ = x.shape

    def body(x_ref, o_ref):
        acc = x_ref[0]
        for k in range(1, s):
            acc = acc + x_ref[k]
        o_ref[...] = acc

    return pl.pallas_call(
        body, name=name, out_shape=jax.ShapeDtypeStruct((r, 128), F32),
        in_specs=[pl.BlockSpec(memory_space=pltpu.VMEM)], out_specs=pl.BlockSpec(memory_space=pltpu.VMEM),
    )(x)


def _cctx_grad(dscc_parts, c_ctx):
    def body(p_ref, c_ref, o_ref):
        acc = p_ref[0]
        for k in range(1, N_CHIPS):
            acc = acc + p_ref[k]
        o_ref[...] = acc * _dsilu(c_ref[...])

    return pl.pallas_call(
        body, name="cctx_grad", out_shape=jax.ShapeDtypeStruct((8, 128), F32),
        in_specs=[pl.BlockSpec(memory_space=pltpu.VMEM)] * 2, out_specs=pl.BlockSpec(memory_space=pltpu.VMEM),
    )(dscc_parts, c_ctx)


def _adamw(w, g, m, v, name, after):
    nl, r, cdim = w.shape
    tr = _tile(r, 256)
    c1 = 1.0 - ADAM_B1 ** ADAM_STEP
    c2 = 1.0 - ADAM_B2 ** ADAM_STEP

    def body(w_ref, g_ref, m_ref, v_ref, after_ref, d_ref, mo_ref, vo_ref):
        gv = g_ref[...]
        mn = ADAM_B1 * m_ref[...] + (1.0 - ADAM_B1) * gv
        vn = ADAM_B2 * v_ref[...] + (1.0 - ADAM_B2) * (gv * gv)
        mo_ref[...] = mn
        vo_ref[...] = vn
        d_ref[...] = -ADAM_LR * ((mn / c1) / (jnp.sqrt(vn / c2) + ADAM_EPS) + ADAM_WD * w_ref[...])

    spec = pl.BlockSpec((1, tr, cdim), lambda l, i: (l, i, 0))
    sds = jax.ShapeDtypeStruct((nl, r, cdim), F32)
    return pl.pallas_call(
        body, name=name, grid=(nl, r // tr), in_specs=[spec] * 4 + [pl.BlockSpec(memory_space=pl.ANY)],
        out_specs=[spec] * 3, out_shape=[sds] * 3, compiler_params=_params(("parallel", "parallel")),
    )(w, g, m, v, after)


def _place():
    x, y, c = lax.axis_index("x"), lax.axis_index("y"), lax.axis_index("c")
    return x, y, c


def _allgather_small(blk, name):
    m_per, n = blk.shape

    def body(x_ref, out_ref, send_sems, recv_sems, local_sem):
        x, y, c = _place()
        me, sibling = (x, y, c), (x, y, 1 - c)
        chips = [(1 - x, y), (x, 1 - y), (1 - x, 1 - y)]

        def rows(px, py, pc):
            return out_ref.at[pl.ds((4 * px + 2 * py + pc) * m_per, m_per), :]

        def copy(k, block, to, src=None):
            return pltpu.make_async_remote_copy(
                src_ref=rows(*block) if src is None else src, dst_ref=rows(*block),
                send_sem=send_sems.at[k], recv_sem=recv_sems.at[k], device_id=to, device_id_type=MESH)

        mine = pltpu.make_async_copy(x_ref, rows(*me), local_sem)
        mine.start()
        first = [copy(0, me, sibling, src=x_ref)]
        first += [copy(1 + j, me, (*chip, c), src=x_ref) for j, chip in enumerate(chips)]
        for cp in first:
            cp.start()
        passed = [copy(4 + j, (*chip, c), sibling) for j, chip in enumerate(chips)]
        for j, chip in enumerate(chips):
            copy(1 + j, (*chip, c), me).wait_recv()
            passed[j].start()
        copy(0, sibling, me).wait_recv()
        for j, chip in enumerate(chips):
            copy(4 + j, (*chip, 1 - c), me).wait_recv()
        for cp in first + passed:
            cp.wait_send()
        mine.wait()

    return pl.pallas_call(
        body, name=name,
        out_shape=jax.ShapeDtypeStruct((N_DEV * m_per, n), blk.dtype),
        in_specs=[pl.BlockSpec(memory_space=pltpu.VMEM)],
        out_specs=pl.BlockSpec(memory_space=pltpu.VMEM),
        scratch_shapes=[pltpu.SemaphoreType.DMA((7,)), pltpu.SemaphoreType.DMA((7,)), pltpu.SemaphoreType.DMA],
    )(blk)


def _other_chips(x, y):
    return [(1 - x, y), (x, 1 - y), (1 - x, 1 - y)]


_HBM_SPEC = pl.BlockSpec(memory_space=pltpu.HBM)
_SEM_SPEC = pl.BlockSpec(memory_space=pltpu.SEMAPHORE)
_SPLIT_PARAMS = pltpu.CompilerParams(has_side_effects=pltpu.SideEffectType.DATAFLOW_SIDE_EFFECTING)


def _in_hbm(a):
    return pltpu.with_memory_space_constraint(a, pltpu.HBM)


def _ag_copies(own_ref, land_ref, send_sems, recv_sems):
    x, y, c = _place()
    chip = 2 * x + y
    hr = own_ref.shape[0] // 2

    def half(ch):
        return land_ref.at[ch, pl.ds(c * hr, hr), :]

    def copy(k, src, dst, to):
        return pltpu.make_async_remote_copy(src_ref=src, dst_ref=dst, send_sem=send_sems.at[k],
                                            recv_sem=recv_sems.at[k], device_id=to, device_id_type=MESH)

    sends, expects = [], []
    for j, (ox, oy) in enumerate(_other_chips(x, y)):
        sends.append(copy(j, own_ref.at[pl.ds(c * hr, hr), :], half(chip), (ox, oy, c)))
        expects.append(copy(j, half(2 * ox + oy), half(2 * ox + oy), (ox, oy, c)))
    own_slot = copy(3, own_ref, land_ref.at[chip], (x, y, 1 - c))
    return sends + [own_slot], expects + [own_slot]


def _sc_copies(p_ref, land_ref, send_sems, recv_sems):
    x, y, c = _place()
    chip = 2 * x + y
    sends, expects = [], []
    for j, (ox, oy) in enumerate(_other_chips(x, y)):
        och = 2 * ox + oy
        mk = lambda dst_slot: pltpu.make_async_remote_copy(
            src_ref=p_ref.at[och], dst_ref=land_ref.at[dst_slot], send_sem=send_sems.at[j],
            recv_sem=recv_sems.at[j], device_id=(ox, oy, c), device_id_type=MESH)
        sends.append(mk(chip))
        expects.append(mk(och))
    return sends, expects


def _pe_copies(g_ref, land_ref, send_sems, recv_sems):
    x, y, c = _place()
    hr = g_ref.shape[1] // 2
    cp = pltpu.make_async_remote_copy(
        src_ref=g_ref.at[:, pl.ds((1 - c) * hr, hr), :], dst_ref=land_ref, send_sem=send_sems.at[0],
        recv_sem=recv_sems.at[0], device_id=(x, y, 1 - c), device_id_type=MESH)
    return [cp], [cp]


def _split_start(src, land_shape, copies, n_copies, after, name):
    def body(src_ref, land_ref, after_ref, send_sems, recv_sems, src_thru, land_thru, token):
        for cp in copies(src_ref, land_ref, send_sems, recv_sems)[0]:
            cp.start()
        token[...] = jnp.zeros_like(token)

    land = lax.empty(land_shape, src.dtype)
    return pl.pallas_call(
        body, name=name,
        out_shape=(pltpu.SemaphoreType.DMA((n_copies,)), pltpu.SemaphoreType.DMA((n_copies,)),
                   pltpu.HBM(src.shape, src.dtype), pltpu.HBM(land_shape, src.dtype),
                   jax.ShapeDtypeStruct((8, 128), F32)),
        in_specs=(_HBM_SPEC, _HBM_SPEC, pl.BlockSpec(memory_space=pl.ANY)),
        out_specs=(_SEM_SPEC, _SEM_SPEC, _HBM_SPEC, _HBM_SPEC, pl.BlockSpec(memory_space=pltpu.VMEM)),
        input_output_aliases={0: 2, 1: 3}, compiler_params=_SPLIT_PARAMS,
    )(_in_hbm(src), _in_hbm(land), after)


def _split_wait(started, after, copies, name):
    send_sems, recv_sems, src_thru, land_thru, _ = started

    def body(src_ref, land_ref, send_sems, recv_sems, after_ref, src_dead, got_ref):
        sends, expects = copies(src_ref, land_ref, send_sems, recv_sems)
        for cp in sends:
            cp.wait_send()
        for cp in expects:
            cp.wait_recv()

    return pl.pallas_call(
        body, name=name,
        out_shape=(pltpu.HBM(src_thru.shape, src_thru.dtype), pltpu.HBM(land_thru.shape, land_thru.dtype)),
        in_specs=(_HBM_SPEC, _HBM_SPEC, _SEM_SPEC, _SEM_SPEC, pl.BlockSpec(memory_space=pl.ANY)),
        out_specs=(_HBM_SPEC, _HBM_SPEC), input_output_aliases={0: 0, 1: 1}, compiler_params=_SPLIT_PARAMS,
    )(src_thru, land_thru, send_sems, recv_sems, after)


def _ag_pass_on(land, name):
    hr = land.shape[1] // 2

    def body(in_ref, out_ref, send_sems, recv_sems):
        x, y, c = _place()

        def copy(j, ox, oy, cc):
            ref = out_ref.at[2 * ox + oy, pl.ds(cc * hr, hr), :]
            return pltpu.make_async_remote_copy(src_ref=ref, dst_ref=ref, send_sem=send_sems.at[j],
                                                recv_sem=recv_sems.at[j], device_id=(x, y, 1 - c),
                                                device_id_type=MESH)

        others = _other_chips(x, y)
        for j, (ox, oy) in enumerate(others):
            copy(j, ox, oy, c).start()
        for j, (ox, oy) in enumerate(others):
            copy(j, ox, oy, 1 - c).wait_recv()
        for j, (ox, oy) in enumerate(others):
            copy(j, ox, oy, c).wait_send()

    any_spec = pl.BlockSpec(memory_space=pl.ANY)
    return pl.pallas_call(
        body, name=name, out_shape=jax.ShapeDtypeStruct(land.shape, land.dtype),
        in_specs=[any_spec], out_specs=any_spec, input_output_aliases={0: 0},
        scratch_shapes=[pltpu.SemaphoreType.DMA((3,)), pltpu.SemaphoreType.DMA((3,))],
    )(land)


def _rs_pair_exchange(g, name):
    r = g.shape[1]
    hr = r // 2

    def body(g_ref, got_ref, send_sem, recv_sem):
        x, y, c = _place()
        cp = pltpu.make_async_remote_copy(
            src_ref=g_ref.at[:, pl.ds((1 - c) * hr, hr), :], dst_ref=got_ref, send_sem=send_sem, recv_sem=recv_sem,
            device_id=(x, y, 1 - c), device_id_type=MESH)
        cp.start()
        cp.wait()

    any_spec = pl.BlockSpec(memory_space=pl.ANY)
    return pl.pallas_call(
        body, name=name,
        out_shape=jax.ShapeDtypeStruct((N_CHIPS, hr, D), F32),
        in_specs=[any_spec], out_specs=any_spec,
        scratch_shapes=[pltpu.SemaphoreType.DMA, pltpu.SemaphoreType.DMA],
    )(g)


def _rs_chip_sum(place, g, got, name):
    r = g.shape[1]
    hr = r // 2
    tr = _tile(hr, 640, 16)
    nt = hr // tr

    def body(pl_ref, g_ref, got_ref, p16_ref, p32_ref):
        s = pl.program_id(1)
        p = g_ref[0] + got_ref[0]
        p16_ref[0] = p.astype(BF16)

        @pl.when(s == pl_ref[1])
        def _():
            p32_ref[...] = p

    return pl.pallas_call(
        body, name=name,
        grid_spec=pltpu.PrefetchScalarGridSpec(
            num_scalar_prefetch=1, grid=(nt, N_CHIPS),
            in_specs=[pl.BlockSpec((1, tr, D), lambda i, s, pr: (s, pr[0] * nt + i, 0)),
                      pl.BlockSpec((1, tr, D), lambda i, s, pr: (s, i, 0))],
            out_specs=[pl.BlockSpec((1, tr, D), lambda i, s, pr: (s, i, 0)),
                       pl.BlockSpec((tr, D), lambda i, s, pr: (i, 0))]),
        out_shape=[jax.ShapeDtypeStruct((N_CHIPS, hr, D), BF16), jax.ShapeDtypeStruct((hr, D), F32)],
        compiler_params=_params(("parallel", "arbitrary")),
    )(place, g, got)


def _rs_final_sum(place, parts, p32, name):
    hr = parts.shape[1]
    tr = _tile(hr, 640, 16)
    nt = hr // tr

    def body(pl_ref, a_ref, b_ref, c_ref, p32_ref, o_ref):
        o_ref[...] = ((p32_ref[...] + a_ref[0].astype(F32)) + b_ref[0].astype(F32)) + c_ref[0].astype(F32)

    def other(j):
        return pl.BlockSpec((1, tr, D), lambda i, pr: (j + jnp.where(pr[1] <= j, 1, 0), i, 0))

    return pl.pallas_call(
        body, name=name,
        grid_spec=pltpu.PrefetchScalarGridSpec(
            num_scalar_prefetch=1, grid=(nt,),
            in_specs=[other(0), other(1), other(2), pl.BlockSpec((tr, D), lambda i, pr: (i, 0))],
            out_specs=pl.BlockSpec((tr, D), lambda i, pr: (pr[0] * nt + i, 0))),
        out_shape=jax.ShapeDtypeStruct((2 * hr, D), F32),
        compiler_params=_params(("parallel",)),
    )(place, parts, parts, parts, p32)


def _rs_pair_gather(both, name):
    hr = both.shape[0] // 2

    def body(in_ref, out_ref, send_sem, recv_sem):
        x, y, c = _place()
        mine = out_ref.at[pl.ds(c * hr, hr), :]
        cp = pltpu.make_async_remote_copy(
            src_ref=mine, dst_ref=mine, send_sem=send_sem, recv_sem=recv_sem,
            device_id=(x, y, 1 - c), device_id_type=MESH)
        cp.start()
        theirs = out_ref.at[pl.ds((1 - c) * hr, hr), :]
        pltpu.make_async_remote_copy(
            src_ref=theirs, dst_ref=theirs, send_sem=send_sem, recv_sem=recv_sem,
            device_id=(x, y, 1 - c), device_id_type=MESH).wait_recv()
        cp.wait_send()

    any_spec = pl.BlockSpec(memory_space=pl.ANY)
    return pl.pallas_call(
        body, name=name,
        out_shape=jax.ShapeDtypeStruct(both.shape, F32),
        in_specs=[any_spec], out_specs=any_spec, input_output_aliases={0: 0},
        scratch_shapes=[pltpu.SemaphoreType.DMA, pltpu.SemaphoreType.DMA],
    )(both)


def _local_step(x, ctx, tgt, mods, mc, ag_gin, ag_main, place, small):
    nb, t, _ = x.shape
    tc = ctx.shape[1]
    n = nb * t
    nc = nb * tc
    xf = x.reshape(n, D)
    cf = ctx.reshape(nc, D)
    tf = tgt.reshape(n, D)
    vec = lambda a: a.reshape(1, -1)
    m = [[mods[l, :, k, :].reshape(nb, 1, D) for k in range(N_MOD)] for l in range(2)]
    mc_b = [jnp.broadcast_to(mc[k].reshape(1, 1, D), (nb, 1, D)) for k in range(2)]

    cw = [small["ffn_conv_w"][l] for l in range(2)]
    cb = [small["ffn_conv_b"][l].reshape(1, -1) for l in range(2)]
    w2 = jnp.zeros((128, 2 * KEY), F32)
    w2 = w2.at[0:RANK, 0:KEY].set(small["gla_w_a2"][0]).at[RANK:2 * RANK, KEY:].set(small["gla_w_a2"][1])
    b2 = small["gla_b_a"].reshape(1, 2 * KEY)
    hg = small["gla_head_norm"].reshape(1, DV)

    hn_all = _mod_fwd(xf, vec(small["norm_mix"][0]), m[0][0], m[0][1], t, "mod0_main", out_rows=n + nc)
    hn_all = _mod_fwd(cf, vec(small["norm_mix"][0]), mc_b[0], mc_b[1], tc, "mod0_ctx", out_rows=n + nc,
                      into=hn_all, row0=n)
    gin = _ag_pass_on(_split_wait(ag_gin, hn_all, _ag_copies, "ag_gin_wait")[1], "ag_gin_pass_on")
    w_gin = jnp.pad(gin[:, :_GIN_ROWS, :].reshape(GLA_IN, D), ((0, GLA_IN_PAD - GLA_IN), (0, 0)))
    p_all, la_all = _gla_in_proj(hn_all, w_gin, w2, b2)
    o2, ss = _gla_scan_fwd(p_all, la_all, nb, t, tc)
    wg = _ag_pass_on(_split_wait(ag_main, o2[0], _ag_copies, "ag_main_wait")[1], "ag_main_pass_on")
    offs = _offsets(_MAIN, _MAIN_ROWS)
    rows = _MAIN_ROWS

    def w_nt(a, k, name, out_dtype=BF16, tm=1024):
        return _mm_nt_w(a, wg, offs[k], rows[k], name, tm, out_dtype)

    def w_nn_mod(a3, k, h, gate, gain, shift, scale, name):
        return _mm_nn_w_mod(a3, wg, offs[k], rows[k], h, gate, vec(gain), shift, scale, t, name, 512)

    yb0 = _gla_post_fwd(o2, p_all, hg, n)
    y0, h1, hn1 = w_nn_mod(yb0[None], "gla_out", xf, m[0][2], small["norm_ffn"][0], m[0][3], m[0][4],
                           "gla_out_proj_mod")
    u0, z0 = _ffn_up_mid_fwd(hn1, wg, offs["up_t0"], cw[0], cb[0], nb, t, "ffn0_up_mid")
    f0, h2, hn2 = w_nn_mod(z0[None], "down0", h1, m[0][5], small["norm_mix"][1], m[1][0], m[1][1],
                           "ffn0_down_mod")
    p1, yb1 = _sc_in_mid_fwd(hn2, wg, offs["sc_in_t"], small["sc_conv_w"], nb, t)
    y1, h3, hn3 = w_nn_mod(yb1[None], "sc_out", h2, m[1][2], small["norm_ffn"][1], m[1][3], m[1][4],
                           "sc_out_proj_mod")
    u1, z1 = _ffn_up_mid_fwd(hn3, wg, offs["up_t1"], cw[1], cb[1], nb, t, "ffn1_up_mid")
    loss, dh4, df1, dm15, dfinal = _mm_nn_w_final(z1[None], wg, offs["down1"], rows["down1"], h3, m[1][5],
                                                  vec(small["final_norm"]), tf, t, "ffn1_down_final", 512)

    gs = {}
    dmods = [[None] * N_MOD for _ in range(2)]
    dmods[1][5] = dm15

    def w_dw(a3, b, g_prev, k, name, tm):
        return _mm_dw(a3, b, g_prev, offs[k], rows[k], name, tm)

    def w_dx_mod(a3, k, h_in, dh_out, gain, scale, y_prev, gate_prev, name):
        return _mm_nn_w_modbwd(a3, wg, offs[k], rows[k], h_in, dh_out, vec(gain), scale, y_prev, gate_prev, t,
                               name, 256)

    def ffn_bwd(l, df, u, z, hn, g_prev, h_in, dh_out, scale, y_prev, gate_prev):
        dz = w_nt(df, f"down{l}", f"ffn{l}_down_dx")
        g_acc = w_dw(z[None], df, g_prev, f"down{l}", f"ffn{l}_down_dw", 640)
        du, dcw, dcb = _ffn_mid_bwd(u, cw[l], cb[l], dz, nb, t, f"ffn{l}_mid_bwd")
        r = w_dx_mod(du, f"up_t{l}", h_in, dh_out, small["norm_ffn"][l], scale, y_prev, gate_prev,
                     f"ffn{l}_up_dx_mod")
        g_acc = w_dw(du, hn, g_acc, f"up_t{l}", f"ffn{l}_up_dw", 640)
        return r, g_acc, jnp.moveaxis(dcw, 0, 1).reshape(3, 2 * HID), dcb.reshape(2 * HID)

    r, g_acc, dcw1, dcb1 = ffn_bwd(1, df1, u1, z1, hn3, None, h3, dh4, m[1][4], y1, m[1][2])
    dh3, dmods[1][4], dmods[1][3], dnf1, dy1, dmods[1][2] = (r["dh"], r["dscale"], r["dshift"], r["dgain"],
                                                             r["dy_prev"], r["dgate_prev"])
    dyb1 = w_nt(dy1, "sc_out", "sc_out_dx")
    g_acc = w_dw(yb1[None], dy1, g_acc, "sc_out", "sc_out_dw", 256)
    dp1, dscw = _sc_mid_bwd(p1, small["sc_conv_w"], dyb1, nb, t)
    r = w_dx_mod(dp1, "sc_in_t", h2, dh3, small["norm_mix"][1], m[1][1], f0, m[0][5], "sc_in_dx_mod")
    g_acc = w_dw(dp1, hn2, g_acc, "sc_in_t", "sc_in_dw", 256)
    dh2, dmods[1][1], dmods[1][0], dnm1, df0, dmods[0][5] = (r["dh"], r["dscale"], r["dshift"], r["dgain"],
                                                             r["dy_prev"], r["dgate_prev"])
    r, g_acc, dcw0, dcb0 = ffn_bwd(0, df0, u0, z0, hn1, g_acc, h1, dh2, m[0][4], y0, m[0][2])
    dh1, dmods[0][4], dmods[0][3], dnf0, dy0, dmods[0][2] = (r["dh"], r["dscale"], r["dshift"], r["dgain"],
                                                             r["dy_prev"], r["dgate_prev"])
    g_packed = w_dw(yb0[None], dy0, g_acc, "gla_out", "gla_out_dw", 256)
    pair = _split_start(g_packed, (N_CHIPS, _MAIN_TOTAL // 2, D), _pe_copies, 1, dy0, "rs_main_pair_start")
    do, dgate, dhg = _gla_out_dx_post_bwd(dy0, wg, offs["gla_out"], o2, p_all, hg + pair[4][0:1, 0:1], n)
    g_packed, from_sibling = _split_wait(pair, do, _pe_copies, "rs_main_pair_wait")
    p16, p32 = _rs_chip_sum(place, g_packed, from_sibling, "rs_main_chip_sum")
    sc_main = _split_start(p16, p16.shape, _sc_copies, 3, p32, "rs_main_scatter_start")
    dq, dk, dv, dla = _gla_scan_bwd(p_all, la_all, do, ss, nb, t, tc, sc_main[4])
    dp, dw2, db2 = _gla_assemble(p_all, w2, b2, dq, dk, dv, dla, dgate, n)
    dhn_all = _mm(dp, w_gin, "nn", F32, "gla_in_dx", 768, 512)
    landed = _split_wait(sc_main, dhn_all, _sc_copies, "rs_main_scatter_wait")[1]
    g_main = _rs_pair_gather(_rs_final_sum(place, landed, p32, "rs_main_final_sum"), "rs_main_pair_gather")
    g_gin = _mm(dp, hn_all, "tn", F32, "gla_in_dw", 640, 1024)[:GLA_IN]
    g_gin = jnp.pad(g_gin.reshape(N_CHIPS, _GIN_ROWS, D), ((0, 0), (0, _GIN_PAD - _GIN_ROWS), (0, 0)))
    from_sibling = _rs_pair_exchange(g_gin, "rs_gin_pair_exchange")
    p16_gin, p32_gin = _rs_chip_sum(place, g_gin, from_sibling, "rs_gin_chip_sum")
    r = _mod_bwd(xf, dhn_all, vec(small["norm_mix"][0]), m[0][1], t, "mod0_main_bwd", dh_out=dh1)
    grad_x, dmods[0][1], dmods[0][0], dnm0 = r["dh"], r["dscale"], r["dshift"], r["dgain"]
    rc = _mod_bwd(cf, dhn_all, vec(small["norm_mix"][0]), mc_b[1], tc, "mod0_ctx_bwd", dhn_row0=n, need_dh=False)
    dmc = jnp.stack([jnp.sum(rc["dshift"], axis=0).reshape(D), jnp.sum(rc["dscale"], axis=0).reshape(D)])
    dnm0 = dnm0 + rc["dgain"]

    gs["norm_mix"] = jnp.concatenate([dnm0, dnm1], axis=0)
    gs["norm_ffn"] = jnp.concatenate([dnf0, dnf1], axis=0)
    gs["final_norm"] = dfinal.reshape(D)
    gs["gla_w_a2"] = jnp.stack([dw2[0:RANK, 0:KEY], dw2[RANK:2 * RANK, KEY:]])
    gs["gla_b_a"] = db2.reshape(2, KEY)
    gs["gla_head_norm"] = dhg.reshape(DV)
    gs["sc_conv_w"] = dscw
    gs["ffn_conv_w"] = jnp.stack([dcw0, dcw1])
    gs["ffn_conv_b"] = jnp.stack([dcb0, dcb1])
    dmods_arr = jnp.stack([jnp.stack([dmods[l][k].reshape(nb, D) for k in range(N_MOD)], axis=1) for l in range(2)])
    return loss, grad_x.reshape(nb, t, D), g_main, p16_gin, p32_gin, gs, dmods_arr, dmc


def _pack(arrs):
    parts, meta, off = [], [], 0
    for a in arrs:
        r = a.size // 128
        rp = -(-r // 8) * 8
        a2 = a.reshape(r, 128).astype(F32)
        if rp != r:
            a2 = jnp.pad(a2, ((0, rp - r), (0, 0)))
        parts.append(a2)
        meta.append((off, r, a.shape))
        off += rp
    return jnp.concatenate(parts, axis=0), meta


def _unpack(buf, meta, lead=()):
    return [buf[..., off:off + r, :].reshape(*lead, *shape) for off, r, shape in meta]


_MAIN = ("up_t0", "up_t1", "down0", "down1", "sc_in_t", "gla_out", "sc_out")
_MAIN_ROWS = {"sc_in_t": 3 * D // N_CHIPS, "up_t0": 2 * HID // N_CHIPS, "up_t1": 2 * HID // N_CHIPS,
              "gla_out": D // N_CHIPS, "sc_out": D // N_CHIPS, "down0": HID // N_CHIPS, "down1": HID // N_CHIPS}
_MAIN_TOTAL = sum(_MAIN_ROWS.values())
_GIN_ROWS = GLA_IN // N_CHIPS
_GIN_PAD = -(-_GIN_ROWS // 32) * 32


def _offsets(names, rows):
    off, out = 0, {}
    for k in names:
        out[k] = off
        off += rows[k]
    return out


def kernel(x, c, ctx, c_ctx, ada_w, ada_b, norm_mix, norm_ffn, gla_w_in, gla_w_a2, gla_b_a, gla_head_norm, gla_w_out, sc_w_in, sc_conv_w, sc_w_out, ffn_w_up, ffn_conv_w, ffn_conv_b, ffn_w_down, final_norm, loss_target, m_c_ctx, m_ada_w, m_ada_b, m_norm_mix, m_norm_ffn, m_gla_w_in, m_gla_w_a2, m_gla_b_a, m_gla_head_norm, m_gla_w_out, m_sc_w_in, m_sc_conv_w, m_sc_w_out, m_ffn_w_up, m_ffn_conv_w, m_ffn_conv_b, m_ffn_w_down, m_final_norm, v_c_ctx, v_ada_w, v_ada_b, v_norm_mix, v_norm_ffn, v_gla_w_in, v_gla_w_a2, v_gla_b_a, v_gla_head_norm, v_gla_w_out, v_sc_w_in, v_sc_conv_w, v_sc_w_out, v_ffn_w_up, v_ffn_conv_w, v_ffn_conv_b, v_ffn_w_down, v_final_norm):
    ix, iy, ic = _place()
    chip = 2 * ix + iy
    dev = 2 * chip + ic
    place = jnp.stack([ic, chip]).astype(jnp.int32)
    nb = x.shape[0]
    offs = _offsets(_MAIN, _MAIN_ROWS)

    buf, meta = _pack([c, ffn_conv_w, sc_conv_w, gla_w_a2, gla_b_a])
    got = _allgather_small(buf, "gather_small_in").reshape(N_DEV, buf.shape[0], 128)
    c_all, fcw, scw, wa2, ba = _unpack(got, meta, (N_DEV,))
    c_all = c_all.reshape(N_DEV * nb, D)
    per_chip = lambda a: a[0::2]
    ffn_conv_w_full = jnp.moveaxis(per_chip(fcw), 0, 2).reshape(2, 3, 2 * HID)
    sc_conv_w_full = jnp.moveaxis(per_chip(scw)[:, 0], 0, 1).reshape(3, D)
    gla_w_a2_full = jnp.moveaxis(per_chip(wa2)[:, 0], 0, 2).reshape(2, RANK, KEY)
    gla_b_a_full = jnp.moveaxis(per_chip(ba)[:, 0], 0, 1).reshape(2, KEY)

    cvec = jnp.concatenate([c_all, c_ctx.reshape(1, D), jnp.zeros((ADA_ROWS - N_DEV * nb - 1, D), F32)], axis=0)
    ada_b_sh = lax.dynamic_slice_in_dim(ada_b, chip * ADA_SH, ADA_SH, axis=1).reshape(2, 1, ADA_SH)
    mod_sh = _ada_fwd(cvec, ada_w, ada_b_sh)
    got = _allgather_small(mod_sh.reshape(2 * ADA_ROWS, ADA_SH), "gather_mod")
    mod_full = jnp.moveaxis(per_chip(got.reshape(N_DEV, 2, ADA_ROWS, ADA_SH)), 0, 2).reshape(2, ADA_ROWS, N_MOD * D)
    mc = mod_full[0, N_DEV * nb, :2 * D].reshape(2, D)

    own = {"sc_in_t": sc_w_in[0].T, "up_t0": ffn_w_up[0].T, "up_t1": ffn_w_up[1].T,
           "gla_out": gla_w_out[0], "sc_out": sc_w_out[0], "down0": ffn_w_down[0], "down1": ffn_w_down[1]}
    own_main = jnp.concatenate([own[k].astype(BF16) for k in _MAIN], axis=0)
    own_gin = jnp.pad(gla_w_in[0].T.astype(BF16), ((0, _GIN_PAD - _GIN_ROWS), (0, 0)))
    ag_gin = _split_start(own_gin, (N_CHIPS, _GIN_PAD, D), _ag_copies, 4, mc, "ag_gin_start")
    ag_main = _split_start(own_main, (N_CHIPS, _MAIN_TOTAL, D), _ag_copies, 4, ag_gin[4], "ag_main_start")
    mods = lax.dynamic_slice_in_dim(mod_full, dev * nb, nb, axis=1).reshape(2, nb, N_MOD, D) + ag_main[4][0, 0]

    small = {"norm_mix": norm_mix, "norm_ffn": norm_ffn, "final_norm": final_norm, "gla_w_a2": gla_w_a2_full,
             "gla_b_a": gla_b_a_full, "gla_head_norm": gla_head_norm[0], "sc_conv_w": sc_conv_w_full,
             "ffn_conv_w": ffn_conv_w_full, "ffn_conv_b": ffn_conv_b}
    loss_p, grad_x, g_main, p16_gin, p32_gin, gs, dmods, dmc = _local_step(x, ctx, loss_target, mods, mc, ag_gin,
                                                                           ag_main, place, small)

    sum_names = ["norm_mix", "norm_ffn", "final_norm", "gla_w_a2", "gla_b_a", "gla_head_norm", "sc_conv_w",
                 "ffn_conv_w", "ffn_conv_b"]
    buf, meta = _pack([jnp.broadcast_to(loss_p, (8, 128))] + [gs[k] for k in sum_names] + [dmc, dmods])
    n_sum = meta[-1][0]
    got = _allgather_small(buf, "gather_small_grads").reshape(N_DEV, buf.shape[0], 128)
    summed = _sum_slots(got[:, :n_sum], "sum_small_grads")
    parts = _unpack(summed, meta[:-1])
    loss = parts[0][0, 0]
    g_small = dict(zip(sum_names, parts[1:-1]))
    dmc_tot = parts[-1]
    dmods_all = jnp.moveaxis(_unpack(got, meta[-1:], (N_DEV,))[0], 0, 1).reshape(2, N_DEV * nb, N_MOD * D)

    ctx_row = jnp.stack([jnp.concatenate([dmc_tot.reshape(2 * D), jnp.zeros(((N_MOD - 2) * D,), F32)]),
                         jnp.zeros((N_MOD * D,), F32)]).reshape(2, 1, N_MOD * D)
    dmod_ext = jnp.concatenate([dmods_all, ctx_row, jnp.zeros((2, ADA_ROWS - N_DEV * nb - 1, N_MOD * D), F32)], axis=1)
    g_ada_b = _sum_slots(jnp.moveaxis(dmod_ext, 1, 0).reshape(ADA_ROWS, 2 * N_MOD * D // 128, 128),
                         "sum_ada_b").reshape(2, N_MOD * D)
    dmod_sh = lax.dynamic_slice_in_dim(dmod_ext, chip * ADA_SH, ADA_SH, axis=2)
    g_ada_w, dcv = _ada_bwd(cvec, ada_w, dmod_sh)
    dscc_part = (dcv[0, N_DEV * nb] + dcv[1, N_DEV * nb]).reshape(8, 128)
    got = _allgather_small(dscc_part, "gather_dscc").reshape(N_DEV, 8, 128)
    g_c_ctx = _cctx_grad(per_chip(got), c_ctx.reshape(8, 128)).reshape(D)

    sc_gin = _split_start(p16_gin, p16_gin.shape, _sc_copies, 3, g_c_ctx, "rs_gin_scatter_start")
    seg = {k: g_main[offs[k]:offs[k] + _MAIN_ROWS[k]] for k in _MAIN}

    sl_chip = lambda a, axis, width: lax.dynamic_slice_in_dim(a, chip * width, width, axis=axis)
    grads = {
        "c_ctx": g_c_ctx, "ada_w": g_ada_w, "ada_b": g_ada_b, "norm_mix": g_small["norm_mix"],
        "norm_ffn": g_small["norm_ffn"],
        "gla_w_a2": sl_chip(g_small["gla_w_a2"], 2, KEY // N_CHIPS)[None],
        "gla_b_a": sl_chip(g_small["gla_b_a"], 1, KEY // N_CHIPS)[None],
        "gla_head_norm": g_small["gla_head_norm"][None], "gla_w_out": seg["gla_out"][None],
        "sc_w_in": seg["sc_in_t"].T[None], "sc_conv_w": sl_chip(g_small["sc_conv_w"], 1, D // N_CHIPS)[None],
        "sc_w_out": seg["sc_out"][None], "ffn_w_up": jnp.stack([seg["up_t0"].T, seg["up_t1"].T]),
        "ffn_conv_w": sl_chip(g_small["ffn_conv_w"], 2, 2 * HID // N_CHIPS), "ffn_conv_b": g_small["ffn_conv_b"],
        "ffn_w_down": jnp.stack([seg["down0"], seg["down1"]]), "final_norm": g_small["final_norm"],
    }
    weights = {"c_ctx": c_ctx, "ada_w": ada_w, "ada_b": ada_b, "norm_mix": norm_mix, "norm_ffn": norm_ffn,
               "gla_w_in": gla_w_in, "gla_w_a2": gla_w_a2, "gla_b_a": gla_b_a, "gla_head_norm": gla_head_norm,
               "gla_w_out": gla_w_out, "sc_w_in": sc_w_in, "sc_conv_w": sc_conv_w, "sc_w_out": sc_w_out,
               "ffn_w_up": ffn_w_up, "ffn_conv_w": ffn_conv_w, "ffn_conv_b": ffn_conv_b, "ffn_w_down": ffn_w_down,
               "final_norm": final_norm}
    mom1 = {"c_ctx": m_c_ctx, "ada_w": m_ada_w, "ada_b": m_ada_b, "norm_mix": m_norm_mix, "norm_ffn": m_norm_ffn,
            "gla_w_in": m_gla_w_in, "gla_w_a2": m_gla_w_a2, "gla_b_a": m_gla_b_a, "gla_head_norm": m_gla_head_norm,
            "gla_w_out": m_gla_w_out, "sc_w_in": m_sc_w_in, "sc_conv_w": m_sc_conv_w, "sc_w_out": m_sc_w_out,
            "ffn_w_up": m_ffn_w_up, "ffn_conv_w": m_ffn_conv_w, "ffn_conv_b": m_ffn_conv_b,
            "ffn_w_down": m_ffn_w_down, "final_norm": m_final_norm}
    mom2 = {"c_ctx": v_c_ctx, "ada_w": v_ada_w, "ada_b": v_ada_b, "norm_mix": v_norm_mix, "norm_ffn": v_norm_ffn,
            "gla_w_in": v_gla_w_in, "gla_w_a2": v_gla_w_a2, "gla_b_a": v_gla_b_a, "gla_head_norm": v_gla_head_norm,
            "gla_w_out": v_gla_w_out, "sc_w_in": v_sc_w_in, "sc_conv_w": v_sc_conv_w, "sc_w_out": v_sc_w_out,
            "ffn_w_up": v_ffn_w_up, "ffn_conv_w": v_ffn_conv_w, "ffn_conv_b": v_ffn_conv_b,
            "ffn_w_down": v_ffn_w_down, "final_norm": v_final_norm}
    names = list(weights)

    big_names = ["ada_w", "gla_w_out", "sc_w_in", "sc_w_out", "ffn_w_up", "ffn_w_down", "gla_w_in"]
    small_names = [k for k in names if k not in big_names]
    delta, new_m, new_v = {}, {}, {}
    done = []

    def big_adamw(k, token):
        delta[k], new_m[k], new_v[k] = _adamw(weights[k], grads[k], mom1[k], mom2[k], "adamw_" + k, token)
        done.append(new_v[k][0, 0:1, 0:128])

    for k in big_names[:-1]:
        grads[k] = grads[k].reshape(weights[k].shape)
        big_adamw(k, sc_gin[4])
    for k in small_names:
        grads[k] = grads[k].reshape(weights[k].shape)
    packed = [_pack([src[k] for k in small_names]) for src in (weights, grads, mom1, mom2)]
    meta = packed[0][1]
    rows_pad = -packed[0][0].shape[0] % 128
    bufs = [jnp.pad(p[0], ((0, rows_pad), (0, 0)))[None] for p in packed]
    outs = _adamw(bufs[0], bufs[1], bufs[2], bufs[3], "adamw_small", sc_gin[4])
    done.append(outs[2][0, 0:1, :])
    for dst, o in zip((delta, new_m, new_v), outs):
        for k, a in zip(small_names, _unpack(o[0], meta)):
            dst[k] = a
    landed = _split_wait(sc_gin, jnp.concatenate(done, axis=0), _sc_copies, "rs_gin_scatter_wait")[1]
    g_gin_shard = _rs_pair_gather(_rs_final_sum(place, landed, p32_gin, "rs_gin_final_sum"), "rs_gin_pair_gather")
    grads["gla_w_in"] = g_gin_shard[:_GIN_ROWS].T[None]
    big_adamw("gla_w_in", sc_gin[4])

    return (loss, grad_x, *[grads[k] for k in names], *[delta[k] for k in names], *[new_m[k] for k in names],
            *[new_v[k] for k in names])
```

```python
import functools

import jax
import jax.numpy as jnp
from jax import lax
from jax.experimental import pallas as pl
from jax.experimental.pallas import tpu as pltpu

F32 = jnp.float32
BF16 = jnp.bfloat16
MESH = pl.DeviceIdType.MESH

EPS = 1e-6
D = 1024
N_MOD = 6
HEADS = 4
DK = 128
DV = 256
KEY = HEADS * DK
RANK = 16
TAU = 16.0
CH = 64
GRID_W = 64
HID = 2560
GLA_IN = 2 * KEY + 2 * D + 2 * RANK
GLA_IN_PAD = 3200
Q_SCALE = DK ** -0.5
N_CHIPS = 4
N_DEV = 8

ADAM_LR = 0.001
ADAM_B1 = 0.9
ADAM_B2 = 0.999
ADAM_EPS = 1e-08
ADAM_WD = 0.01
ADAM_STEP = 10

VMEM_LIMIT = 56 * 1024 * 1024


def _params(sem):
    return pltpu.CompilerParams(dimension_semantics=sem, vmem_limit_bytes=VMEM_LIMIT)


def _tile(n, pref, mult=8):
    if n <= pref:
        return n
    for t in range(pref - pref % mult, 0, -mult):
        if n % t == 0:
            return t
    raise ValueError((n, pref, mult))


_NN = (((1,), (0,)), ((), ()))
_NT = (((1,), (1,)), ((), ()))
_TN = (((0,), (0,)), ((), ()))


def _dot(a, b, dims=_NN):
    return lax.dot_general(a.astype(BF16), b.astype(BF16), dims, preferred_element_type=F32)


def _sigmoid(x):
    return 1.0 / (1.0 + jnp.exp(-x))


def _rowsum(x):
    return jnp.sum(x, axis=0, keepdims=True)


def _mm(a, b, form, out_dtype, name, tm, tn):
    if form == "tn":
        K, M = a.shape
    else:
        M, K = a.shape
    N = b.shape[0] if form == "nt" else b.shape[1]
    tm = _tile(M, tm, 128)
    tn = _tile(N, tn, 128)
    dims = {"nn": _NN, "nt": _NT, "tn": _TN}[form]

    def body(a_ref, b_ref, o_ref):
        o_ref[...] = _dot(a_ref[...], b_ref[...], dims).astype(o_ref.dtype)

    if form == "tn":
        a_spec = pl.BlockSpec((K, tm), lambda i, j: (0, i))
    else:
        a_spec = pl.BlockSpec((tm, K), lambda i, j: (i, 0))
    if form == "nt":
        b_spec = pl.BlockSpec((tn, K), lambda i, j: (j, 0))
    else:
        b_spec = pl.BlockSpec((K, tn), lambda i, j: (0, j))
    return pl.pallas_call(
        body,
        name=name,
        grid=(M // tm, N // tn),
        in_specs=[a_spec, b_spec],
        out_specs=pl.BlockSpec((tm, tn), lambda i, j: (i, j)),
        out_shape=jax.ShapeDtypeStruct((M, N), out_dtype),
        compiler_params=_params(("parallel", "parallel")),
    )(a, b)


def _mm_nt_w(a, wg, off, rows, name, tm, out_dtype):
    m = a.shape[0]
    tm = _tile(m, tm, 128)
    if N_CHIPS * rows <= D:

        def body_small(a_ref, w_ref, o_ref):
            av = a_ref[...]
            for s in range(N_CHIPS):
                o_ref[:, s * rows:(s + 1) * rows] = _dot(av, w_ref[s], _NT).astype(o_ref.dtype)

        return pl.pallas_call(
            body_small, name=name, grid=(m // tm,),
            in_specs=[pl.BlockSpec((tm, D), lambda i: (i, 0)),
                      pl.BlockSpec((N_CHIPS, rows, D), lambda i: (0, off // rows, 0))],
            out_specs=pl.BlockSpec((tm, N_CHIPS * rows), lambda i: (i, 0)),
            out_shape=jax.ShapeDtypeStruct((m, N_CHIPS * rows), out_dtype),
            compiler_params=_params(("parallel",)),
        )(a, wg)

    def body(a_ref, w_ref, o_ref):
        o_ref[...] = _dot(a_ref[...], w_ref[0], _NT).astype(o_ref.dtype)

    return pl.pallas_call(
        body, name=name, grid=(m // tm, N_CHIPS),
        in_specs=[pl.BlockSpec((tm, D), lambda i, s: (i, 0)),
                  pl.BlockSpec((1, rows, D), lambda i, s: (s, off // rows, 0))],
        out_specs=pl.BlockSpec((tm, rows), lambda i, s: (i, s)),
        out_shape=jax.ShapeDtypeStruct((m, N_CHIPS * rows), out_dtype),
        compiler_params=_params(("parallel", "parallel")),
    )(a, wg)


def _mm_nn_w_mod(a3, wg, off, rows, h, gate, gain, shift, scale, tpb_rows, name, tm):
    parts, m, kp = a3.shape
    assert parts * kp == N_CHIPS * rows
    tm = _tile(tpb_rows, tm, 128)
    tpb = tpb_rows // tm
    cuts = sorted({s * rows for s in range(N_CHIPS + 1)} | {p * kp for p in range(parts + 1)})
    pieces = [(k0 // kp, k0 % kp, k0 // rows, k0 % rows, k1 - k0) for k0, k1 in zip(cuts[:-1], cuts[1:])]

    def body(a_ref, w_ref, h_ref, gate_ref, gain_ref, sh_ref, sc_ref, y_ref, hout_ref, hn_ref):
        acc = None
        for p, a0, s, r0, width in pieces:
            term = _dot(a_ref[p, :, a0:a0 + width], w_ref[s, r0:r0 + width, :])
            acc = term if acc is None else acc + term
        y_ref[...] = acc
        hv = h_ref[...] + gate_ref[0] * acc
        hout_ref[...] = hv
        r = lax.rsqrt(jnp.mean(hv * hv, axis=-1, keepdims=True) + EPS)
        hn_ref[...] = ((hv * r) * gain_ref[...] * (1.0 + sc_ref[0]) + sh_ref[0]).astype(BF16)

    row = pl.BlockSpec((tm, D), lambda i: (i, 0))
    per_b = pl.BlockSpec((1, 1, D), lambda i: (i // tpb, 0, 0))
    return pl.pallas_call(
        body, name=name, grid=(m // tm,),
        in_specs=[pl.BlockSpec((parts, tm, kp), lambda i: (0, i, 0)),
                  pl.BlockSpec((N_CHIPS, rows, D), lambda i: (0, off // rows, 0)),
                  row, per_b, pl.BlockSpec((1, D), lambda i: (0, 0)), per_b, per_b],
        out_specs=[row, row, row],
        out_shape=[jax.ShapeDtypeStruct((m, D), F32), jax.ShapeDtypeStruct((m, D), F32),
                   jax.ShapeDtypeStruct((m, D), BF16)],
        compiler_params=_params(("parallel",)),
    )(a3, wg, h, gate, gain, shift, scale)


def _mm_nn_w_final(a3, wg, off, rows, h, gate, gain, tgt, tpb_rows, name, tm):
    parts, m, kp = a3.shape
    assert parts * kp == N_CHIPS * rows
    nb = m // tpb_rows
    tm = _tile(tpb_rows, tm, 128)
    tpb = tpb_rows // tm
    cuts = sorted({s * rows for s in range(N_CHIPS + 1)} | {p * kp for p in range(parts + 1)})
    pieces = [(k0 // kp, k0 % kp, k0 // rows, k0 % rows, k1 - k0) for k0, k1 in zip(cuts[:-1], cuts[1:])]

    def body(a_ref, w_ref, h_ref, gate_ref, gain_ref, tgt_ref, loss_ref, dh_ref, df_ref, dgate_ref, dgain_ref):
        i = pl.program_id(0)

        @pl.when(i == 0)
        def _():
            loss_ref[...] = jnp.zeros_like(loss_ref)
            dgain_ref[...] = jnp.zeros_like(dgain_ref)

        @pl.when(i % tpb == 0)
        def _():
            dgate_ref[...] = jnp.zeros_like(dgate_ref)

        fv = None
        for p, a0, s, r0, width in pieces:
            term = _dot(a_ref[p, :, a0:a0 + width], w_ref[s, r0:r0 + width, :])
            fv = term if fv is None else fv + term
        gate_v = gate_ref[0]
        hv = h_ref[...] + gate_v * fv
        r = lax.rsqrt(jnp.mean(hv * hv, axis=-1, keepdims=True) + EPS)
        y = hv * r
        gain_v = gain_ref[...]
        e = y * gain_v - tgt_ref[...]
        s_ = jnp.sum(_rowsum(e * e), axis=1, keepdims=True) * (0.5 / D)
        loss_ref[...] += jnp.broadcast_to(s_, loss_ref.shape)
        dout = e * (1.0 / D)
        dgain_ref[...] += _rowsum(dout * y)
        dy = dout * gain_v
        dh = r * (dy - y * jnp.mean(dy * y, axis=-1, keepdims=True))
        dh_ref[...] = dh
        df_ref[...] = (dh * gate_v).astype(BF16)
        dgate_ref[0] += _rowsum(dh * fv)

    row = pl.BlockSpec((tm, D), lambda i: (i, 0))
    per_b = pl.BlockSpec((1, 1, D), lambda i: (i // tpb, 0, 0))
    vec = pl.BlockSpec((1, D), lambda i: (0, 0))
    return pl.pallas_call(
        body, name=name, grid=(m // tm,),
        in_specs=[pl.BlockSpec((parts, tm, kp), lambda i: (0, i, 0)),
                  pl.BlockSpec((N_CHIPS, rows, D), lambda i: (0, off // rows, 0)), row, per_b, vec, row],
        out_specs=[pl.BlockSpec((1, 128), lambda i: (0, 0)), row, row, per_b, vec],
        out_shape=[jax.ShapeDtypeStruct((1, 128), F32), jax.ShapeDtypeStruct((m, D), F32),
                   jax.ShapeDtypeStruct((m, D), BF16), jax.ShapeDtypeStruct((nb, 1, D), F32),
                   jax.ShapeDtypeStruct((1, D), F32)],
        compiler_params=_params(("arbitrary",)),
    )(a3, wg, h, gate, gain, tgt)


def _mm_nn_w_modbwd(a3, wg, off, rows, h_in, dh_out, gain, scale, y_prev, gate_prev, tpb_rows, name, tm):
    parts, m, kp = a3.shape
    assert parts * kp == N_CHIPS * rows
    nb = m // tpb_rows
    tm = _tile(tpb_rows, tm, 128)
    tpb = tpb_rows // tm
    cuts = sorted({s * rows for s in range(N_CHIPS + 1)} | {p * kp for p in range(parts + 1)})
    pieces = [(k0 // kp, k0 % kp, k0 // rows, k0 % rows, k1 - k0) for k0, k1 in zip(cuts[:-1], cuts[1:])]

    def body(a_ref, w_ref, h_ref, gain_ref, sc_ref, dho_ref, yp_ref, gp_ref,
             dh_ref, dsc_ref, dsh_ref, dgain_ref, dyp_ref, dgp_ref):
        i = pl.program_id(0)

        @pl.when(i == 0)
        def _():
            dgain_ref[...] = jnp.zeros_like(dgain_ref)

        @pl.when(i % tpb == 0)
        def _():
            dsc_ref[...] = jnp.zeros_like(dsc_ref)
            dsh_ref[...] = jnp.zeros_like(dsh_ref)
            dgp_ref[...] = jnp.zeros_like(dgp_ref)

        g = None
        for p, a0, s, r0, width in pieces:
            term = _dot(a_ref[p, :, a0:a0 + width], w_ref[s, r0:r0 + width, :])
            g = term if g is None else g + term
        hv = h_ref[...]
        r = lax.rsqrt(jnp.mean(hv * hv, axis=-1, keepdims=True) + EPS)
        y = hv * r
        gain_v = gain_ref[...]
        dsh_ref[0] += _rowsum(g)
        dsc_ref[0] += _rowsum(g * (y * gain_v))
        drn = g * (1.0 + sc_ref[0])
        dgain_ref[...] += _rowsum(drn * y)
        dy = drn * gain_v
        dh = r * (dy - y * jnp.mean(dy * y, axis=-1, keepdims=True)) + dho_ref[...]
        dh_ref[...] = dh
        dyp_ref[...] = (dh * gp_ref[0]).astype(BF16)
        dgp_ref[0] += _rowsum(dh * yp_ref[...])

    row = pl.BlockSpec((tm, D), lambda i: (i, 0))
    per_b = pl.BlockSpec((1, 1, D), lambda i: (i // tpb, 0, 0))
    vec = pl.BlockSpec((1, D), lambda i: (0, 0))
    per_b_shape = jax.ShapeDtypeStruct((nb, 1, D), F32)
    outs = pl.pallas_call(
        body, name=name, grid=(m // tm,),
        in_specs=[pl.BlockSpec((parts, tm, kp), lambda i: (0, i, 0)),
                  pl.BlockSpec((N_CHIPS, rows, D), lambda i: (0, off // rows, 0)),
                  row, vec, per_b, row, row, per_b],
        out_specs=[row, per_b, per_b, vec, row, per_b],
        out_shape=[jax.ShapeDtypeStruct((m, D), F32), per_b_shape, per_b_shape, jax.ShapeDtypeStruct((1, D), F32),
                   jax.ShapeDtypeStruct((m, D), BF16), per_b_shape],
        compiler_params=_params(("arbitrary",)),
    )(a3, wg, h_in, gain, scale, dh_out, y_prev, gate_prev)
    return dict(zip(("dh", "dscale", "dshift", "dgain", "dy_prev", "dgate_prev"), outs))


def _mm_dw(a3, b, g_prev, off, rows, name, tm):
    parts, ntok, cdim = a3.shape
    assert parts * cdim == N_CHIPS * rows and cdim % tm == 0 and rows % tm == 0 and off % tm == 0

    def body(a_ref, b_ref, *rest):
        rest[-1][0] = _dot(a_ref[0], b_ref[...], _TN)

    in_specs = [pl.BlockSpec((1, ntok, tm), lambda i: ((i * tm) // cdim, 0, ((i * tm) % cdim) // tm)),
                pl.BlockSpec((ntok, D), lambda i: (0, 0))]
    args = [a3, b]
    aliases = {}
    if g_prev is not None:
        in_specs.append(pl.BlockSpec(memory_space=pl.ANY))
        args.append(g_prev)
        aliases = {2: 0}
    return pl.pallas_call(
        body, name=name, grid=(N_CHIPS * rows // tm,),
        in_specs=in_specs,
        out_specs=pl.BlockSpec((1, tm, D), lambda i: ((i * tm) // rows, (off + (i * tm) % rows) // tm, 0)),
        out_shape=jax.ShapeDtypeStruct((N_CHIPS, _MAIN_TOTAL, D), F32),
        input_output_aliases=aliases,
        compiler_params=_params(("parallel",)),
    )(*args)


def _mod_fwd(h, gain, shift, scale, tpb_rows, name, y=None, gate=None, out_rows=None, into=None, row0=0):
    n = h.shape[0]
    tt = _tile(tpb_rows, 256)
    tpb = tpb_rows // tt
    has_res = y is not None
    assert row0 % tt == 0 and not (has_res and out_rows)

    def body(*refs):
        if has_res:
            h_ref, y_ref, gate_ref, gain_ref, sh_ref, sc_ref, hout_ref, hn_ref = refs
            hv = h_ref[...] + gate_ref[0] * y_ref[...]
            hout_ref[...] = hv
        else:
            h_ref, gain_ref, sh_ref, sc_ref, hn_ref = refs[0], refs[1], refs[2], refs[3], refs[-1]
            hv = h_ref[...]
        r = lax.rsqrt(jnp.mean(hv * hv, axis=-1, keepdims=True) + EPS)
        hn = (hv * r) * gain_ref[...] * (1.0 + sc_ref[0]) + sh_ref[0]
        hn_ref[...] = hn.astype(BF16)

    row = pl.BlockSpec((tt, D), lambda i: (i, 0))
    per_b = pl.BlockSpec((1, 1, D), lambda i: (i // tpb, 0, 0))
    vec = pl.BlockSpec((1, D), lambda i: (0, 0))
    if has_res:
        in_specs = [row, row, per_b, vec, per_b, per_b]
        args = (h, y, gate, gain, shift, scale)
        out_specs = [row, row]
        out_shape = [jax.ShapeDtypeStruct((n, D), F32), jax.ShapeDtypeStruct((n, D), BF16)]
    else:
        in_specs = [row, vec, per_b, per_b]
        args = (h, gain, shift, scale)
        out_specs = pl.BlockSpec((tt, D), lambda i: (i + row0 // tt, 0))
        out_shape = jax.ShapeDtypeStruct((out_rows or n, D), BF16)
    aliases = {}
    if into is not None:
        in_specs = in_specs + [pl.BlockSpec(memory_space=pl.ANY)]
        args = args + (into,)
        aliases = {4: 0}
    return pl.pallas_call(
        body, name=name, grid=(n // tt,), in_specs=in_specs, out_specs=out_specs, out_shape=out_shape,
        input_output_aliases=aliases, compiler_params=_params(("parallel",)),
    )(*args)


def _mod_bwd(h_in, dhn, gain, scale, tpb_rows, name, dhn_row0=0, dh_out=None, y_prev=None, gate_prev=None,
             need_dh=True):
    n = h_in.shape[0]
    nb = n // tpb_rows
    tt = _tile(tpb_rows, 256)
    tpb = tpb_rows // tt
    off = dhn_row0 // tt
    assert dhn_row0 % tt == 0
    has_out = dh_out is not None
    has_prev = y_prev is not None

    def body(*refs):
        it = iter(refs)
        h_ref, dhn_ref, gain_ref, sc_ref = next(it), next(it), next(it), next(it)
        dho_ref = next(it) if has_out else None
        yp_ref, gp_ref = (next(it), next(it)) if has_prev else (None, None)
        dh_ref = next(it) if need_dh else None
        dsc_ref, dsh_ref, dgain_ref = next(it), next(it), next(it)
        dyp_ref, dgp_ref = (next(it), next(it)) if has_prev else (None, None)
        i = pl.program_id(0)

        @pl.when(i == 0)
        def _():
            dgain_ref[...] = jnp.zeros_like(dgain_ref)

        @pl.when(i % tpb == 0)
        def _():
            dsc_ref[...] = jnp.zeros_like(dsc_ref)
            dsh_ref[...] = jnp.zeros_like(dsh_ref)
            if has_prev:
                dgp_ref[...] = jnp.zeros_like(dgp_ref)

        hv = h_ref[...]
        r = lax.rsqrt(jnp.mean(hv * hv, axis=-1, keepdims=True) + EPS)
        y = hv * r
        gain_v = gain_ref[...]
        g = dhn_ref[...].astype(F32)
        dsh_ref[0] += _rowsum(g)
        dsc_ref[0] += _rowsum(g * (y * gain_v))
        drn = g * (1.0 + sc_ref[0])
        dgain_ref[...] += _rowsum(drn * y)
        if need_dh:
            dy = drn * gain_v
            dh = r * (dy - y * jnp.mean(dy * y, axis=-1, keepdims=True))
            if has_out:
                dh = dh + dho_ref[...]
            dh_ref[...] = dh
            if has_prev:
                dyp_ref[...] = (dh * gp_ref[0]).astype(BF16)
                dgp_ref[0] += _rowsum(dh * yp_ref[...])

    row = pl.BlockSpec((tt, D), lambda i: (i, 0))
    row_off = pl.BlockSpec((tt, D), lambda i: (i + off, 0))
    per_b = pl.BlockSpec((1, 1, D), lambda i: (i // tpb, 0, 0))
    vec = pl.BlockSpec((1, D), lambda i: (0, 0))
    in_specs = [row, row_off, vec, per_b]
    args = [h_in, dhn, gain, scale]
    if has_out:
        in_specs.append(row)
        args.append(dh_out)
    if has_prev:
        in_specs += [row, per_b]
        args += [y_prev, gate_prev]
    out_specs, out_shape, names = [], [], []
    if need_dh:
        out_specs.append(row)
        out_shape.append(jax.ShapeDtypeStruct((n, D), F32))
        names.append("dh")
    for nm in ("dscale", "dshift"):
        out_specs.append(per_b)
        out_shape.append(jax.ShapeDtypeStruct((nb, 1, D), F32))
        names.append(nm)
    out_specs.append(vec)
    out_shape.append(jax.ShapeDtypeStruct((1, D), F32))
    names.append("dgain")
    if has_prev:
        out_specs += [row, per_b]
        out_shape += [jax.ShapeDtypeStruct((n, D), BF16), jax.ShapeDtypeStruct((nb, 1, D), F32)]
        names += ["dy_prev", "dgate_prev"]
    outs = pl.pallas_call(
        body, name=name, grid=(n // tt,), in_specs=in_specs, out_specs=out_specs, out_shape=out_shape,
        compiler_params=_params(("arbitrary",)),
    )(*args)
    return dict(zip(names, outs))


def _row_dn1(x):
    t = lax.broadcasted_iota(jnp.int32, x.shape, 0)
    return jnp.where(t % GRID_W == 0, 0.0, pltpu.roll(x, 1, 0))


def _row_up1(x):
    t = lax.broadcasted_iota(jnp.int32, x.shape, 0)
    return jnp.where(t % GRID_W == GRID_W - 1, 0.0, pltpu.roll(x, x.shape[0] - 1, 0))


def _silu(x):
    return x * _sigmoid(x)


def _dsilu(x):
    s = _sigmoid(x)
    return s * (1.0 + x * (1.0 - s))


def _row_ds(i):
    start = i * GRID_W
    return pl.ds(start if isinstance(start, int) else pl.multiple_of(start, GRID_W), GRID_W)


def _grid_row(ref, i, first, last):
    def rows(k):
        return ref[_row_ds(k), :].astype(F32)

    cur = rows(i)
    return (jnp.zeros_like(cur) if first else rows(i - 1)), cur, (jnp.zeros_like(cur) if last else rows(i + 1))


def _over_grid_rows(n_rows, step, carry):
    carry = step(0, carry, True, n_rows == 1)
    if n_rows > 2:
        carry = lax.fori_loop(1, n_rows - 1, lambda i, c: step(i, c, False, False), carry)
    if n_rows > 1:
        carry = step(n_rows - 1, carry, False, True)
    return carry


def _fold8(p):
    return p.reshape(GRID_W // 8, 8, p.shape[1]).sum(axis=0)


def _ffn_up_mid_fwd(hn, wg, off, cw, cb, nb, t, name):
    tcol = 256
    ncol = HID // tcol
    rows_sh = 2 * HID // N_CHIPS

    def conv(x, w_ref):
        zeros = jnp.zeros((GRID_W, x.shape[1]), x.dtype)
        down = jnp.concatenate([zeros, x[: x.shape[0] - GRID_W]], axis=0)
        up = jnp.concatenate([x[GRID_W:], zeros], axis=0)
        return down * w_ref[0:1, :] + x * w_ref[1:2, :] + up * w_ref[2:3, :]

    def body(h_ref, wa_ref, wg_ref, cwa_ref, cwg_ref, cba_ref, cbg_ref, u_ref, z_ref):
        hv = h_ref[...]
        ua = _dot(hv, wa_ref[0], _NT)
        ug = _dot(hv, wg_ref[0], _NT)
        u_ref[0] = ua.astype(BF16)
        u_ref[1] = ug.astype(BF16)
        a = conv(ua, cwa_ref) + cba_ref[...]
        gt = conv(ug, cwg_ref) + cbg_ref[...]
        z_ref[...] = (a * _silu(gt)).astype(BF16)

    def w_spec(part):
        def idx(b, j):
            n = part * HID + j * tcol
            return (n // rows_sh, (off + n % rows_sh) // tcol, 0)
        return pl.BlockSpec((1, tcol, D), idx)

    chan = lambda rows, part: pl.BlockSpec((rows, tcol), lambda b, j: (0, part * ncol + j))
    return pl.pallas_call(
        body, name=name, grid=(nb, ncol),
        in_specs=[pl.BlockSpec((t, D), lambda b, j: (b, 0)), w_spec(0), w_spec(1),
                  chan(3, 0), chan(3, 1), chan(1, 0), chan(1, 1)],
        out_specs=[pl.BlockSpec((2, t, tcol), lambda b, j: (0, b, j)), pl.BlockSpec((t, tcol), lambda b, j: (b, j))],
        out_shape=[jax.ShapeDtypeStruct((2, nb * t, HID), BF16), jax.ShapeDtypeStruct((nb * t, HID), BF16)],
        compiler_params=_params(("parallel", "parallel")),
    )(hn, wg, wg, cw, cw, cb, cb)


def _ffn_mid_bwd(u0, cw, cb, dz, nb, t, name):
    nc = HID // 128
    n_rows = t // GRID_W

    def body(ua3_ref, ug3_ref, wa_ref, wg_ref, ba_ref, bg_ref, dz_ref, du_ref, dw_ref, db_ref, dua_ref, dug_ref):
        ua_ref, ug_ref = ua3_ref.at[0], ug3_ref.at[0]
        b = pl.program_id(1)

        @pl.when(b == 0)
        def _():
            dw_ref[...] = jnp.zeros_like(dw_ref)
            db_ref[...] = jnp.zeros_like(db_ref)

        wa = [wa_ref[k:k + 1, :] for k in range(3)]
        wg = [wg_ref[k:k + 1, :] for k in range(3)]
        ba, bg = ba_ref[...], bg_ref[...]

        def pass1(i, acc, first, last):
            here = _row_ds(i)
            ap, ac, an = _grid_row(ua_ref, i, first, last)
            gp, gc, gn = _grid_row(ug_ref, i, first, last)
            a = ap * wa[0] + ac * wa[1] + an * wa[2] + ba
            gt = gp * wg[0] + gc * wg[1] + gn * wg[2] + bg
            dzv = dz_ref[here, :].astype(F32)
            s = _sigmoid(gt)
            silu = gt * s
            da = dzv * silu
            dg = (dzv * a) * (s + silu * (1.0 - s))
            dua_ref[here, :] = da
            dug_ref[here, :] = dg
            terms = (da, da * ap, da * ac, da * an, dg, dg * gp, dg * gc, dg * gn)
            return tuple(r + _fold8(p) for r, p in zip(acc, terms))

        zero = jnp.zeros((8, 128), F32)
        acc = _over_grid_rows(n_rows, pass1, (zero,) * 8)
        for part in range(2):
            db_ref[part] += _rowsum(acc[4 * part])
            for k in range(3):
                dw_ref[part, k:k + 1, :] += _rowsum(acc[4 * part + 1 + k])

        def pass2(i, carry, first, last):
            for part, (ref, w) in enumerate(((dua_ref, wa), (dug_ref, wg))):
                dp_, dc_, dn_ = _grid_row(ref, i, first, last)
                du_ref[part, _row_ds(i), :] = (dn_ * w[0] + dc_ * w[1] + dp_ * w[2]).astype(BF16)
            return carry

        _over_grid_rows(n_rows, pass2, 0)

    col = lambda rows, part: pl.BlockSpec((rows, 128), lambda j, b: (0, part * nc + j))
    part_of_u = lambda part: pl.BlockSpec((1, t, 128), lambda j, b: (part, b, j))
    return pl.pallas_call(
        body, name=name, grid=(nc, nb),
        in_specs=[part_of_u(0), part_of_u(1), col(3, 0), col(3, 1), col(1, 0), col(1, 1),
                  pl.BlockSpec((t, 128), lambda j, b: (b, j))],
        out_specs=[pl.BlockSpec((2, t, 128), lambda j, b: (0, b, j)), pl.BlockSpec((2, 3, 128), lambda j, b: (0, 0, j)),
                   pl.BlockSpec((2, 1, 128), lambda j, b: (0, 0, j))],
        out_shape=[jax.ShapeDtypeStruct((2, nb * t, HID), BF16), jax.ShapeDtypeStruct((2, 3, HID), F32),
                   jax.ShapeDtypeStruct((2, 1, HID), F32)],
        scratch_shapes=[pltpu.VMEM((t, 128), F32), pltpu.VMEM((t, 128), F32)],
        compiler_params=_params(("parallel", "arbitrary")),
    )(u0, u0, cw, cw, cb, cb, dz)


def _sc_in_mid_fwd(hn, wg, off, cw, nb, t):
    tcol = 256
    ncol = D // tcol
    rows_sh = 3 * D // N_CHIPS

    def body(h_ref, wb_ref, wc_ref, wv_ref, cw_ref, p_ref, y_ref):
        hv = h_ref[...]
        bg = _dot(hv, wb_ref[0], _NT)
        cg = _dot(hv, wc_ref[0], _NT)
        v = _dot(hv, wv_ref[0], _NT)
        p_ref[0] = bg.astype(BF16)
        p_ref[1] = cg.astype(BF16)
        p_ref[2] = v.astype(BF16)
        cv = cg * v
        cc = _row_dn1(cv) * cw_ref[0:1, :] + cv * cw_ref[1:2, :] + _row_up1(cv) * cw_ref[2:3, :]
        y_ref[...] = (bg * cc).astype(BF16)

    def w_spec(part):
        def idx(b, j):
            n = part * D + j * tcol
            return (n // rows_sh, (off + n % rows_sh) // tcol, 0)
        return pl.BlockSpec((1, tcol, D), idx)

    return pl.pallas_call(
        body, name="sc_in_mid", grid=(nb, ncol),
        in_specs=[pl.BlockSpec((t, D), lambda b, j: (b, 0)), w_spec(0), w_spec(1), w_spec(2),
                  pl.BlockSpec((3, tcol), lambda b, j: (0, j))],
        out_specs=[pl.BlockSpec((3, t, tcol), lambda b, j: (0, b, j)), pl.BlockSpec((t, tcol), lambda b, j: (b, j))],
        out_shape=[jax.ShapeDtypeStruct((3, nb * t, D), BF16), jax.ShapeDtypeStruct((nb * t, D), BF16)],
        compiler_params=_params(("parallel", "parallel")),
    )(hn, wg, wg, wg, cw)


def _sc_mid_bwd(p, cw, dyb, nb, t):
    nc = D // 128

    def body(bg3_ref, cg3_ref, v3_ref, w_ref, dy_ref, dp_ref, dw_ref):
        bg_ref, cg_ref, v_ref = bg3_ref.at[0], cg3_ref.at[0], v3_ref.at[0]
        b = pl.program_id(1)

        @pl.when(b == 0)
        def _():
            dw_ref[...] = jnp.zeros_like(dw_ref)

        w0, w1, w2 = w_ref[0:1, :], w_ref[1:2, :], w_ref[2:3, :]
        cg, v = cg_ref[...].astype(F32), v_ref[...].astype(F32)
        cv = cg * v
        cvd = _row_dn1(cv)
        cvu = _row_up1(cv)
        cc = cvd * w0 + cv * w1 + cvu * w2
        dy = dy_ref[...].astype(F32)
        dcc = dy * bg_ref[...].astype(F32)
        dw_ref[0:1, :] += _rowsum(dcc * cvd)
        dw_ref[1:2, :] += _rowsum(dcc * cv)
        dw_ref[2:3, :] += _rowsum(dcc * cvu)
        dcv = _row_up1(dcc) * w0 + dcc * w1 + _row_dn1(dcc) * w2
        dp_ref[0] = (dy * cc).astype(BF16)
        dp_ref[1] = (dcv * v).astype(BF16)
        dp_ref[2] = (dcv * cg).astype(BF16)

    part = lambda k: pl.BlockSpec((1, t, 128), lambda j, b: (k, b, j))
    return pl.pallas_call(
        body, name="sc_mid_bwd", grid=(nc, nb),
        in_specs=[part(0), part(1), part(2), pl.BlockSpec((3, 128), lambda j, b: (0, j)),
                  pl.BlockSpec((t, 128), lambda j, b: (b, j))],
        out_specs=[pl.BlockSpec((3, t, 128), lambda j, b: (0, b, j)), pl.BlockSpec((3, 128), lambda j, b: (0, j))],
        out_shape=[jax.ShapeDtypeStruct((3, nb * t, D), BF16), jax.ShapeDtypeStruct((3, D), F32)],
        compiler_params=_params(("parallel", "arbitrary")),
    )(p, p, p, cw, dyb)


def _gla_in_proj(hn_all, w_gin, w2, b2):
    n = hn_all.shape[0]
    tm = _tile(n, 768, 128)

    def body(h_ref, w_ref, w2_ref, b2_ref, p_ref, la_ref):
        p = _dot(h_ref[...], w_ref[...], _NT)
        p_ref[...] = p
        z = _dot(p[:, 2 * KEY + 2 * D:], w2_ref[...]) + b2_ref[...]
        la_ref[...] = (jnp.minimum(z, 0.0) - jnp.log(1.0 + jnp.exp(-jnp.abs(z)))) * (1.0 / TAU)

    return pl.pallas_call(
        body, name="gla_in_proj", grid=(n // tm,),
        in_specs=[pl.BlockSpec((tm, D), lambda i: (i, 0)), pl.BlockSpec((GLA_IN_PAD, D), lambda i: (0, 0)),
                  pl.BlockSpec((128, 2 * KEY), lambda i: (0, 0)), pl.BlockSpec((1, 2 * KEY), lambda i: (0, 0))],
        out_specs=[pl.BlockSpec((tm, GLA_IN_PAD), lambda i: (i, 0)), pl.BlockSpec((tm, 2 * KEY), lambda i: (i, 0))],
        out_shape=[jax.ShapeDtypeStruct((n, GLA_IN_PAD), F32), jax.ShapeDtypeStruct((n, 2 * KEY), F32)],
        compiler_params=_params(("parallel",)),
    )(hn_all, w_gin, w2, b2)


def _gla_blocks(nb, nm, ncx):
    def main_idx(d, i):
        return jnp.clip(jnp.where(d == 0, i - ncx, nm - 1 - (i - ncx)), 0, nm - 1)

    def rowblk(d, b, i):
        cidx = jnp.where(d == 0, i, ncx - 1 - i)
        return jnp.where(i < ncx, nb * nm + b * ncx + cidx, b * nm + main_idx(d, i))

    def mainblk(d, b, i):
        return b * nm + main_idx(d, i)

    return rowblk, mainblk


def _gla_mask(d):
    row = lax.broadcasted_iota(jnp.int32, (CH, CH), 0)
    col = lax.broadcasted_iota(jnp.int32, (CH, CH), 1)
    diff = jnp.where(d == 0, row - col, col - row)
    mask = diff >= 0
    return mask, jnp.where(mask, 1.0, 0.0).astype(BF16), jnp.where(diff <= 0, 1.0, 0.0).astype(BF16)


def _tri_sum(m01, x):
    w = x.shape[1]
    hi = x.astype(BF16)
    r1 = x - hi.astype(F32)
    mid = r1.astype(BF16)
    lo = (r1 - mid.astype(F32)).astype(BF16)
    s = lax.dot_general(m01, jnp.concatenate([hi, mid, lo], axis=1), _NN, preferred_element_type=F32)
    return s[:, :w] + s[:, w:2 * w] + s[:, 2 * w:]


def _gla_chunk(q, k, g, bc):
    bl = _rowsum(g)
    eq = jnp.exp(bc)
    ek = jnp.exp(-bc)
    ed = jnp.exp(bl - bc)
    return bl, eq, ek, ed, q * Q_SCALE * eq, k * ek, k * ed


def _gla_scan_fwd(p_all, la_all, nb, t, tc):
    nm, ncx = t // CH, tc // CH
    nst = nm + ncx
    rowblk, mainblk = _gla_blocks(nb, nm, ncx)

    def body(*refs):
        ins, (o_refs, ss_refs, st_ref) = refs[:8], (refs[8:10], refs[10:12], refs[12])
        i = pl.program_id(1)

        @pl.when(i == 0)
        def _():
            st_ref[...] = jnp.zeros_like(st_ref)

        loaded = [r[...] for r in ins]
        states = [st_ref[j] for j in range(2 * HEADS)]
        outs, new_states = [[], []], []
        for d in range(2):
            q_all, k_all, v_all, g_all = loaded[4 * d:4 * d + 4]
            mask, m01, _ = _gla_mask(d)
            bc_all = _tri_sum(m01, g_all)
            for h in range(HEADS):
                ksl = slice(h * DK, (h + 1) * DK)
                v = v_all[:, h * DV:(h + 1) * DV]
                st = states[d * HEADS + h]
                bl, _, _, _, qs, ks, kd = _gla_chunk(q_all[:, ksl], k_all[:, ksl], g_all[:, ksl], bc_all[:, ksl])
                att = jnp.where(mask, _dot(qs, ks, _NT), 0.0)
                outs[d].append(_dot(qs, st, _NT) + _dot(att, v))
                new_states.append(st * jnp.exp(bl) + _dot(v, kd, _TN))
        for d in range(2):
            o_refs[d][...] = jnp.concatenate(outs[d], axis=1)
            for h in range(HEADS):
                ss_refs[d][0, 0, h] = states[d * HEADS + h]
                st_ref[d * HEADS + h] = new_states[d * HEADS + h]

    def in_specs(d):
        return [pl.BlockSpec((CH, KEY), lambda b, i: (rowblk(d, b, i), 0)),
                pl.BlockSpec((CH, KEY), lambda b, i: (rowblk(d, b, i), 1)),
                pl.BlockSpec((CH, D), lambda b, i: (rowblk(d, b, i), 1)),
                pl.BlockSpec((CH, KEY), lambda b, i: (rowblk(d, b, i), d))]

    outs = pl.pallas_call(
        body, name="gla_scan_fwd", grid=(nb, nst),
        in_specs=in_specs(0) + in_specs(1),
        out_specs=[pl.BlockSpec((CH, D), lambda b, i: (mainblk(0, b, i), 0)),
                   pl.BlockSpec((CH, D), lambda b, i: (mainblk(1, b, i), 0)),
                   pl.BlockSpec((1, 1, HEADS, DV, DK), lambda b, i: (b, i, 0, 0, 0)),
                   pl.BlockSpec((1, 1, HEADS, DV, DK), lambda b, i: (b, i, 0, 0, 0))],
        out_shape=[jax.ShapeDtypeStruct((nb * t, D), F32)] * 2
        + [jax.ShapeDtypeStruct((nb, nst, HEADS, DV, DK), F32)] * 2,
        scratch_shapes=[pltpu.VMEM((2 * HEADS, DV, DK), F32)],
        compiler_params=_params(("parallel", "arbitrary")),
    )(*([p_all, p_all, p_all, la_all] * 2))
    return outs[:2], outs[2:]


def _gla_scan_bwd(p_all, la_all, do, ss, nb, t, tc, after):
    nm, ncx = t // CH, tc // CH
    nst = nm + ncx
    ntot = nb * (t + tc)
    rowblk, mainblk = _gla_blocks(nb, nm, ncx)

    def body(*refs):
        ins, outs, dst_ref = refs[:12], refs[13:21], refs[21]
        ip = pl.program_id(1)
        i = nst - 1 - ip

        @pl.when(ip == 0)
        def _():
            dst_ref[...] = jnp.zeros_like(dst_ref)

        live = jnp.where(i >= ncx, 1.0, 0.0)
        loaded = [[r[...] for r in ins[6 * d:6 * d + 5]] for d in range(2)]
        states = [ins[6 * d + 5][0, 0, h] for d in range(2) for h in range(HEADS)]
        dstates = [dst_ref[j] for j in range(2 * HEADS)]
        results, new_dstates = [], []
        for d in range(2):
            q_all, k_all, v_all, g_all, do_all = loaded[d]
            do_all = do_all * live
            mask, m01, m01_t = _gla_mask(d)
            bc_all = _tri_sum(m01, g_all)
            dqs_l, dks_l, dvs_l, dbs_l, dbls_l = [], [], [], [], []
            for h in range(HEADS):
                ksl = slice(h * DK, (h + 1) * DK)
                vsl = slice(h * DV, (h + 1) * DV)
                bl, eq, ek, ed, qs, ks, kd = _gla_chunk(q_all[:, ksl], k_all[:, ksl], g_all[:, ksl], bc_all[:, ksl])
                st, dst, v, dov = states[d * HEADS + h], dstates[d * HEADS + h], v_all[:, vsl], do_all[:, vsl]
                att = jnp.where(mask, _dot(qs, ks, _NT), 0.0)
                datt = jnp.where(mask, _dot(dov, v, _NT), 0.0)
                dqs = _dot(dov, st) + _dot(datt, ks)
                dks = _dot(datt, qs, _TN)
                dvs_l.append(_dot(att, dov, _TN) + _dot(kd, dst, _NT))
                dkd = _dot(v, dst)
                e = jnp.exp(bl)
                dbls_l.append(e * _rowsum(st * dst) + _rowsum(dkd * kd))
                new_dstates.append(_dot(dov, qs, _TN) + dst * e)
                dqs_l.append(dqs * eq * Q_SCALE)
                dks_l.append(dks * ek + dkd * ed)
                dbs_l.append(dqs * qs - dks * ks - dkd * kd)
            results.append((jnp.concatenate(dqs_l, axis=1), jnp.concatenate(dks_l, axis=1),
                            jnp.concatenate(dvs_l, axis=1),
                            _tri_sum(m01_t, jnp.concatenate(dbs_l, axis=1)) + jnp.concatenate(dbls_l, axis=1)))
        for d in range(2):
            for k in range(4):
                outs[4 * d + k][...] = results[d][k]
        for j in range(2 * HEADS):
            dst_ref[j] = new_dstates[j]

    def in_specs(d):
        return [pl.BlockSpec((CH, KEY), lambda b, ip: (rowblk(d, b, nst - 1 - ip), 0)),
                pl.BlockSpec((CH, KEY), lambda b, ip: (rowblk(d, b, nst - 1 - ip), 1)),
                pl.BlockSpec((CH, D), lambda b, ip: (rowblk(d, b, nst - 1 - ip), 1)),
                pl.BlockSpec((CH, KEY), lambda b, ip: (rowblk(d, b, nst - 1 - ip), d)),
                pl.BlockSpec((CH, D), lambda b, ip: (mainblk(d, b, nst - 1 - ip), 0)),
                pl.BlockSpec((1, 1, HEADS, DV, DK), lambda b, ip: (b, nst - 1 - ip, 0, 0, 0))]

    def out_specs(d):
        row = lambda width: pl.BlockSpec((CH, width), lambda b, ip: (rowblk(d, b, nst - 1 - ip), 0))
        return [row(KEY), row(KEY), row(D), row(KEY)]

    shapes = [jax.ShapeDtypeStruct((ntot, KEY), F32), jax.ShapeDtypeStruct((ntot, KEY), F32),
              jax.ShapeDtypeStruct((ntot, D), F32), jax.ShapeDtypeStruct((ntot, KEY), F32)]
    outs = pl.pallas_call(
        body, name="gla_scan_bwd", grid=(nb, nst),
        in_specs=in_specs(0) + in_specs(1) + [pl.BlockSpec(memory_space=pl.ANY)],
        out_specs=out_specs(0) + out_specs(1),
        out_shape=shapes * 2,
        scratch_shapes=[pltpu.VMEM((2 * HEADS, DV, DK), F32)],
        compiler_params=_params(("parallel", "arbitrary")),
    )(p_all, p_all, p_all, la_all, do, ss[0], p_all, p_all, p_all, la_all, do, ss[1], after)
    return [[outs[k], outs[4 + k]] for k in range(4)]


def _gla_post_fwd(o2, p_all, head_gain, n):
    tt = _tile(n, 256)

    def body(of_ref, ob_ref, g_ref, hg_ref, y_ref):
        o = of_ref[...] + ob_ref[...]
        gv = g_ref[...]
        hg = hg_ref[...]
        for h in range(HEADS):
            oh = o[:, h * DV:(h + 1) * DV]
            r = lax.rsqrt(jnp.mean(oh * oh, axis=-1, keepdims=True) + EPS)
            y_ref[:, h * DV:(h + 1) * DV] = ((oh * r) * hg * _silu(gv[:, h * DV:(h + 1) * DV])).astype(BF16)

    row = pl.BlockSpec((tt, D), lambda i: (i, 0))
    return pl.pallas_call(
        body, name="gla_post_fwd", grid=(n // tt,),
        in_specs=[row, row, pl.BlockSpec((tt, D), lambda i: (i, 2)), pl.BlockSpec((1, DV), lambda i: (0, 0))],
        out_specs=row,
        out_shape=jax.ShapeDtypeStruct((n, D), BF16),
        compiler_params=_params(("parallel",)),
    )(o2[0], o2[1], p_all, head_gain)


def _gla_out_dx_post_bwd(dy, wg, off, o2, p_all, head_gain, n):
    tt = _tile(n, 256)

    def body(dy_ref, w_ref, of_ref, ob_ref, g_ref, hg_ref, do_ref, dg_ref, dhg_ref):
        i = pl.program_id(0)

        @pl.when(i == 0)
        def _():
            dhg_ref[...] = jnp.zeros_like(dhg_ref)

        dyv = dy_ref[...]
        o = of_ref[...] + ob_ref[...]
        gv = g_ref[...]
        hg = hg_ref[...]
        acc = jnp.zeros((1, DV), F32)
        for h in range(HEADS):
            sl = slice(h * DV, (h + 1) * DV)
            dyh = _dot(dyv, w_ref[h], _NT)
            oh = o[:, sl]
            r = lax.rsqrt(jnp.mean(oh * oh, axis=-1, keepdims=True) + EPS)
            on = oh * r
            gh = gv[:, sl]
            dg_ref[:, sl] = dyh * (on * hg) * _dsilu(gh)
            dog = dyh * _silu(gh)
            acc = acc + _rowsum(dog * on)
            don = dog * hg
            do_ref[:, sl] = r * (don - on * jnp.mean(don * on, axis=-1, keepdims=True))
        dhg_ref[...] += acc

    row = pl.BlockSpec((tt, D), lambda i: (i, 0))
    return pl.pallas_call(
        body, name="gla_out_dx_post_bwd", grid=(n // tt,),
        in_specs=[row, pl.BlockSpec((N_CHIPS, DV, D), lambda i: (0, off // DV, 0)), row, row,
                  pl.BlockSpec((tt, D), lambda i: (i, 2)), pl.BlockSpec((1, DV), lambda i: (0, 0))],
        out_specs=[row, row, pl.BlockSpec((1, DV), lambda i: (0, 0))],
        out_shape=[jax.ShapeDtypeStruct((n, D), F32), jax.ShapeDtypeStruct((n, D), F32),
                   jax.ShapeDtypeStruct((1, DV), F32)],
        compiler_params=_params(("arbitrary",)),
    )(dy, wg, o2[0], o2[1], p_all, head_gain)


def _gla_assemble(p_all, w2, b2, dq, dk, dv, dla, dgate, n):
    ntot = p_all.shape[0]
    tt = _tile(n, 128)
    nmain = n // tt
    assert ntot % tt == 0

    def body(a_ref, w_ref, b_ref, dqf_ref, dqb_ref, dkf_ref, dkb_ref, dvf_ref, dvb_ref, dlf_ref, dlb_ref, dg_ref,
             dp_ref, dw_ref, db_ref):
        i = pl.program_id(0)

        @pl.when(i == 0)
        def _():
            dw_ref[...] = jnp.zeros_like(dw_ref)
            db_ref[...] = jnp.zeros_like(db_ref)

        a = a_ref[...]
        w = w_ref[...]
        z = _dot(a, w) + b_ref[...]
        dla = jnp.concatenate([dlf_ref[...], dlb_ref[...]], axis=1)
        dz = dla * (1.0 / (1.0 + jnp.exp(z))) * (1.0 / TAU)
        dw_ref[...] += _dot(a, dz, _TN)
        db_ref[...] += _rowsum(dz)
        dp_ref[:, 0:KEY] = (dqf_ref[...] + dqb_ref[...]).astype(BF16)
        dp_ref[:, KEY:2 * KEY] = (dkf_ref[...] + dkb_ref[...]).astype(BF16)
        dp_ref[:, 2 * KEY:2 * KEY + D] = (dvf_ref[...] + dvb_ref[...]).astype(BF16)
        dp_ref[:, 2 * KEY + D:2 * KEY + 2 * D] = (dg_ref[...] * jnp.where(i < nmain, 1.0, 0.0)).astype(BF16)
        dp_ref[:, 2 * KEY + 2 * D:GLA_IN_PAD] = _dot(dz, w, _NT).astype(BF16)

    row = lambda width: pl.BlockSpec((tt, width), lambda i: (i, 0))
    return pl.pallas_call(
        body, name="gla_assemble", grid=(ntot // tt,),
        in_specs=[pl.BlockSpec((tt, 128), lambda i: (i, (2 * KEY + 2 * D) // 128)),
                  pl.BlockSpec((128, 2 * KEY), lambda i: (0, 0)), pl.BlockSpec((1, 2 * KEY), lambda i: (0, 0)),
                  row(KEY), row(KEY), row(KEY), row(KEY), row(D), row(D), row(KEY), row(KEY),
                  pl.BlockSpec((tt, D), lambda i: (jnp.minimum(i, nmain - 1), 0))],
        out_specs=[pl.BlockSpec((tt, GLA_IN_PAD), lambda i: (i, 0)), pl.BlockSpec((128, 2 * KEY), lambda i: (0, 0)),
                   pl.BlockSpec((1, 2 * KEY), lambda i: (0, 0))],
        out_shape=[jax.ShapeDtypeStruct((ntot, GLA_IN_PAD), BF16), jax.ShapeDtypeStruct((128, 2 * KEY), F32),
                   jax.ShapeDtypeStruct((1, 2 * KEY), F32)],
        compiler_params=_params(("arbitrary",)),
    )(p_all, w2, b2, dq[0], dq[1], dk[0], dk[1], dv[0], dv[1], dla[0], dla[1], dgate)


ADA_ROWS = 24
ADA_SH = N_MOD * D // N_CHIPS


def _ada_fwd(cvec, ada_w, ada_b_sh):
    def body(c_ref, w_ref, b_ref, o_ref):
        o_ref[0] = _dot(_silu(c_ref[...]), w_ref[0]) + b_ref[0]

    return pl.pallas_call(
        body, name="ada_fwd", grid=(2,),
        in_specs=[pl.BlockSpec((ADA_ROWS, D), lambda l: (0, 0)), pl.BlockSpec((1, D, ADA_SH), lambda l: (l, 0, 0)),
                  pl.BlockSpec((1, 1, ADA_SH), lambda l: (l, 0, 0))],
        out_specs=pl.BlockSpec((1, ADA_ROWS, ADA_SH), lambda l: (l, 0, 0)),
        out_shape=jax.ShapeDtypeStruct((2, ADA_ROWS, ADA_SH), F32),
        compiler_params=_params(("parallel",)),
    )(cvec, ada_w, ada_b_sh)


def _ada_bwd(cvec, ada_w, dmod_sh):
    def body(c_ref, w_ref, dm_ref, gw_ref, dc_ref):
        dm = dm_ref[0]
        gw_ref[0] = _dot(_silu(c_ref[...]), dm, _TN)
        dc_ref[0] = _dot(dm, w_ref[0], _NT)

    return pl.pallas_call(
        body, name="ada_bwd", grid=(2,),
        in_specs=[pl.BlockSpec((ADA_ROWS, D), lambda l: (0, 0)), pl.BlockSpec((1, D, ADA_SH), lambda l: (l, 0, 0)),
                  pl.BlockSpec((1, ADA_ROWS, ADA_SH), lambda l: (l, 0, 0))],
        out_specs=[pl.BlockSpec((1, D, ADA_SH), lambda l: (l, 0, 0)), pl.BlockSpec((1, ADA_ROWS, D), lambda l: (l, 0, 0))],
        out_shape=[jax.ShapeDtypeStruct((2, D, ADA_SH), F32), jax.ShapeDtypeStruct((2, ADA_ROWS, D), F32)],
        compiler_params=_params(("parallel",)),
    )(cvec, ada_w, dmod_sh)


def _sum_slots(x, name):
    s, r, _ = x.shape

    def body(x_ref, o_ref):
        acc = x_ref[0]
        for k in range(1, s):
            acc = acc + x_ref[k]
        o_ref[...] = acc

    return pl.pallas_call(
        body, name=name, out_shape=jax.ShapeDtypeStruct((r, 128), F32),
        in_specs=[pl.BlockSpec(memory_space=pltpu.VMEM)], out_specs=pl.BlockSpec(memory_space=pltpu.VMEM),
    )(x)


def _cctx_grad(dscc_parts, c_ctx):
    def body(p_ref, c_ref, o_ref):
        acc = p_ref[0]
        for k in range(1, N_CHIPS):
            acc = acc + p_ref[k]
        o_ref[...] = acc * _dsilu(c_ref[...])

    return pl.pallas_call(
        body, name="cctx_grad", out_shape=jax.ShapeDtypeStruct((8, 128), F32),
        in_specs=[pl.BlockSpec(memory_space=pltpu.VMEM)] * 2, out_specs=pl.BlockSpec(memory_space=pltpu.VMEM),
    )(dscc_parts, c_ctx)


def _adamw(w, g, m, v, name, after):
    nl, r, cdim = w.shape
    tr = _tile(r, 256)
    c1 = 1.0 - ADAM_B1 ** ADAM_STEP
    c2 = 1.0 - ADAM_B2 ** ADAM_STEP

    def body(w_ref, g_ref, m_ref, v_ref, after_ref, d_ref, mo_ref, vo_ref):
        gv = g_ref[...]
        mn = ADAM_B1 * m_ref[...] + (1.0 - ADAM_B1) * gv
        vn = ADAM_B2 * v_ref[...] + (1.0 - ADAM_B2) * (gv * gv)
        mo_ref[...] = mn
        vo_ref[...] = vn
        d_ref[...] = -ADAM_LR * ((mn / c1) / (jnp.sqrt(vn / c2) + ADAM_EPS) + ADAM_WD * w_ref[...])

    spec = pl.BlockSpec((1, tr, cdim), lambda l, i: (l, i, 0))
    sds = jax.ShapeDtypeStruct((nl, r, cdim), F32)
    return pl.pallas_call(
        body, name=name, grid=(nl, r // tr), in_specs=[spec] * 4 + [pl.BlockSpec(memory_space=pl.ANY)],
        out_specs=[spec] * 3, out_shape=[sds] * 3, compiler_params=_params(("parallel", "parallel")),
    )(w, g, m, v, after)


def _place():
    x, y, c = lax.axis_index("x"), lax.axis_index("y"), lax.axis_index("c")
    return x, y, c


def _allgather_small(blk, name):
    m_per, n = blk.shape

    def body(x_ref, out_ref, send_sems, recv_sems, local_sem):
        x, y, c = _place()
        me, sibling = (x, y, c), (x, y, 1 - c)
        chips = [(1 - x, y), (x, 1 - y), (1 - x, 1 - y)]

        def rows(px, py, pc):
            return out_ref.at[pl.ds((4 * px + 2 * py + pc) * m_per, m_per), :]

        def copy(k, block, to, src=None):
            return pltpu.make_async_remote_copy(
                src_ref=rows(*block) if src is None else src, dst_ref=rows(*block),
                send_sem=send_sems.at[k], recv_sem=recv_sems.at[k], device_id=to, device_id_type=MESH)

        mine = pltpu.make_async_copy(x_ref, rows(*me), local_sem)
        mine.start()
        first = [copy(0, me, sibling, src=x_ref)]
        first += [copy(1 + j, me, (*chip, c), src=x_ref) for j, chip in enumerate(chips)]
        for cp in first:
            cp.start()
        passed = [copy(4 + j, (*chip, c), sibling) for j, chip in enumerate(chips)]
        for j, chip in enumerate(chips):
            copy(1 + j, (*chip, c), me).wait_recv()
            passed[j].start()
        copy(0, sibling, me).wait_recv()
        for j, chip in enumerate(chips):
            copy(4 + j, (*chip, 1 - c), me).wait_recv()
        for cp in first + passed:
            cp.wait_send()
        mine.wait()

    return pl.pallas_call(
        body, name=name,
        out_shape=jax.ShapeDtypeStruct((N_DEV * m_per, n), blk.dtype),
        in_specs=[pl.BlockSpec(memory_space=pltpu.VMEM)],
        out_specs=pl.BlockSpec(memory_space=pltpu.VMEM),
        scratch_shapes=[pltpu.SemaphoreType.DMA((7,)), pltpu.SemaphoreType.DMA((7,)), pltpu.SemaphoreType.DMA],
    )(blk)


def _other_chips(x, y):
    return [(1 - x, y), (x, 1 - y), (1 - x, 1 - y)]


_HBM_SPEC = pl.BlockSpec(memory_space=pltpu.HBM)
_SEM_SPEC = pl.BlockSpec(memory_space=pltpu.SEMAPHORE)
_SPLIT_PARAMS = pltpu.CompilerParams(has_side_effects=pltpu.SideEffectType.DATAFLOW_SIDE_EFFECTING)


def _in_hbm(a):
    return pltpu.with_memory_space_constraint(a, pltpu.HBM)


def _ag_copies(own_ref, land_ref, send_sems, recv_sems):
    x, y, c = _place()
    chip = 2 * x + y
    hr = own_ref.shape[0] // 2

    def half(ch):
        return land_ref.at[ch, pl.ds(c * hr, hr), :]

    def copy(k, src, dst, to):
        return pltpu.make_async_remote_copy(src_ref=src, dst_ref=dst, send_sem=send_sems.at[k],
                                            recv_sem=recv_sems.at[k], device_id=to, device_id_type=MESH)

    sends, expects = [], []
    for j, (ox, oy) in enumerate(_other_chips(x, y)):
        sends.append(copy(j, own_ref.at[pl.ds(c * hr, hr), :], half(chip), (ox, oy, c)))
        expects.append(copy(j, half(2 * ox + oy), half(2 * ox + oy), (ox, oy, c)))
    own_slot = copy(3, own_ref, land_ref.at[chip], (x, y, 1 - c))
    return sends + [own_slot], expects + [own_slot]


def _sc_copies(p_ref, land_ref, send_sems, recv_sems):
    x, y, c = _place()
    chip = 2 * x + y
    sends, expects = [], []
    for j, (ox, oy) in enumerate(_other_chips(x, y)):
        och = 2 * ox + oy
        mk = lambda dst_slot: pltpu.make_async_remote_copy(
            src_ref=p_ref.at[och], dst_ref=land_ref.at[dst_slot], send_sem=send_sems.at[j],
            recv_sem=recv_sems.at[j], device_id=(ox, oy, c), device_id_type=MESH)
        sends.append(mk(chip))
        expects.append(mk(och))
    return sends, expects


def _pe_copies(g_ref, land_ref, send_sems, recv_sems):
    x, y, c = _place()
    hr = g_ref.shape[1] // 2
    cp = pltpu.make_async_remote_copy(
        src_ref=g_ref.at[:, pl.ds((1 - c) * hr, hr), :], dst_ref=land_ref, send_sem=send_sems.at[0],
        recv_sem=recv_sems.at[0], device_id=(x, y, 1 - c), device_id_type=MESH)
    return [cp], [cp]


def _pass_on_copies(unused_ref, land_ref, send_sems, recv_sems):
    x, y, c = _place()
    hr = land_ref.shape[1] // 2
    sends, expects = [], []
    for j, (ox, oy) in enumerate(_other_chips(x, y)):
        def mk(cc, j=j, och=2 * ox + oy):
            ref = land_ref.at[och, pl.ds(cc * hr, hr), :]
            return pltpu.make_async_remote_copy(src_ref=ref, dst_ref=ref, send_sem=send_sems.at[j],
                                                recv_sem=recv_sems.at[j], device_id=(x, y, 1 - c),
                                                device_id_type=MESH)
        sends.append(mk(c))
        expects.append(mk(1 - c))
    return sends, expects


def _pair_gather_copies(unused_ref, land_ref, send_sems, recv_sems):
    x, y, c = _place()
    hr = land_ref.shape[0] // 2

    def mk(cc):
        ref = land_ref.at[pl.ds(cc * hr, hr), :]
        return pltpu.make_async_remote_copy(src_ref=ref, dst_ref=ref, send_sem=send_sems.at[0],
                                            recv_sem=recv_sems.at[0], device_id=(x, y, 1 - c), device_id_type=MESH)
    return [mk(c)], [mk(1 - c)]


def _split_start(src, land, copies, n_copies, after, name):
    def body(src_ref, land_ref, after_ref, send_sems, recv_sems, src_thru, land_thru, token):
        for cp in copies(src_ref, land_ref, send_sems, recv_sems)[0]:
            cp.start()
        token[...] = jnp.zeros_like(token)

    if isinstance(land, tuple):
        land = lax.empty(land, src.dtype)
    land_shape = land.shape
    return pl.pallas_call(
        body, name=name,
        out_shape=(pltpu.SemaphoreType.DMA((n_copies,)), pltpu.SemaphoreType.DMA((n_copies,)),
                   pltpu.HBM(src.shape, src.dtype), pltpu.HBM(land_shape, land.dtype),
                   jax.ShapeDtypeStruct((8, 128), F32)),
        in_specs=(_HBM_SPEC, _HBM_SPEC, pl.BlockSpec(memory_space=pl.ANY)),
        out_specs=(_SEM_SPEC, _SEM_SPEC, _HBM_SPEC, _HBM_SPEC, pl.BlockSpec(memory_space=pltpu.VMEM)),
        input_output_aliases={0: 2, 1: 3}, compiler_params=_SPLIT_PARAMS,
    )(_in_hbm(src), _in_hbm(land), after)


def _split_wait(started, after, copies, name):
    send_sems, recv_sems, src_thru, land_thru, _ = started

    def body(src_ref, land_ref, send_sems, recv_sems, after_ref, src_dead, got_ref):
        sends, expects = copies(src_ref, land_ref, send_sems, recv_sems)
        for cp in sends:
            cp.wait_send()
        for cp in expects:
            cp.wait_recv()

    return pl.pallas_call(
        body, name=name,
        out_shape=(pltpu.HBM(src_thru.shape, src_thru.dtype), pltpu.HBM(land_thru.shape, land_thru.dtype)),
        in_specs=(_HBM_SPEC, _HBM_SPEC, _SEM_SPEC, _SEM_SPEC, pl.BlockSpec(memory_space=pl.ANY)),
        out_specs=(_HBM_SPEC, _HBM_SPEC), input_output_aliases={0: 0, 1: 1}, compiler_params=_SPLIT_PARAMS,
    )(src_thru, land_thru, send_sems, recv_sems, after)


def _ag_pass_on(land, name):
    hr = land.shape[1] // 2

    def body(in_ref, out_ref, send_sems, recv_sems):
        x, y, c = _place()

        def copy(j, ox, oy, cc):
            ref = out_ref.at[2 * ox + oy, pl.ds(cc * hr, hr), :]
            return pltpu.make_async_remote_copy(src_ref=ref, dst_ref=ref, send_sem=send_sems.at[j],
                                                recv_sem=recv_sems.at[j], device_id=(x, y, 1 - c),
                                                device_id_type=MESH)

        others = _other_chips(x, y)
        for j, (ox, oy) in enumerate(others):
            copy(j, ox, oy, c).start()
        for j, (ox, oy) in enumerate(others):
            copy(j, ox, oy, 1 - c).wait_recv()
        for j, (ox, oy) in enumerate(others):
            copy(j, ox, oy, c).wait_send()

    any_spec = pl.BlockSpec(memory_space=pl.ANY)
    return pl.pallas_call(
        body, name=name, out_shape=jax.ShapeDtypeStruct(land.shape, land.dtype),
        in_specs=[any_spec], out_specs=any_spec, input_output_aliases={0: 0},
        scratch_shapes=[pltpu.SemaphoreType.DMA((3,)), pltpu.SemaphoreType.DMA((3,))],
    )(land)


def _rs_pair_exchange(g, name):
    r = g.shape[1]
    hr = r // 2

    def body(g_ref, got_ref, send_sem, recv_sem):
        x, y, c = _place()
        cp = pltpu.make_async_remote_copy(
            src_ref=g_ref.at[:, pl.ds((1 - c) * hr, hr), :], dst_ref=got_ref, send_sem=send_sem, recv_sem=recv_sem,
            device_id=(x, y, 1 - c), device_id_type=MESH)
        cp.start()
        cp.wait()

    any_spec = pl.BlockSpec(memory_space=pl.ANY)
    return pl.pallas_call(
        body, name=name,
        out_shape=jax.ShapeDtypeStruct((N_CHIPS, hr, D), F32),
        in_specs=[any_spec], out_specs=any_spec,
        scratch_shapes=[pltpu.SemaphoreType.DMA, pltpu.SemaphoreType.DMA],
    )(g)


def _rs_chip_sum(place, g, got, name):
    r = g.shape[1]
    hr = r // 2
    tr = _tile(hr, 640, 16)
    nt = hr // tr

    def body(pl_ref, g_ref, got_ref, p16_ref, p32_ref):
        s = pl.program_id(1)
        p = g_ref[0] + got_ref[0]
        p16_ref[0] = p.astype(BF16)

        @pl.when(s == pl_ref[1])
        def _():
            p32_ref[...] = p

    return pl.pallas_call(
        body, name=name,
        grid_spec=pltpu.PrefetchScalarGridSpec(
            num_scalar_prefetch=1, grid=(nt, N_CHIPS),
            in_specs=[pl.BlockSpec((1, tr, D), lambda i, s, pr: (s, pr[0] * nt + i, 0)),
                      pl.BlockSpec((1, tr, D), lambda i, s, pr: (s, i, 0))],
            out_specs=[pl.BlockSpec((1, tr, D), lambda i, s, pr: (s, i, 0)),
                       pl.BlockSpec((tr, D), lambda i, s, pr: (i, 0))]),
        out_shape=[jax.ShapeDtypeStruct((N_CHIPS, hr, D), BF16), jax.ShapeDtypeStruct((hr, D), F32)],
        compiler_params=_params(("parallel", "arbitrary")),
    )(place, g, got)


def _rs_final_sum(place, parts, p32, name):
    hr = parts.shape[1]
    tr = _tile(hr, 640, 16)
    nt = hr // tr

    def body(pl_ref, a_ref, b_ref, c_ref, p32_ref, o_ref):
        o_ref[...] = ((p32_ref[...] + a_ref[0].astype(F32)) + b_ref[0].astype(F32)) + c_ref[0].astype(F32)

    def other(j):
        return pl.BlockSpec((1, tr, D), lambda i, pr: (j + jnp.where(pr[1] <= j, 1, 0), i, 0))

    return pl.pallas_call(
        body, name=name,
        grid_spec=pltpu.PrefetchScalarGridSpec(
            num_scalar_prefetch=1, grid=(nt,),
            in_specs=[other(0), other(1), other(2), pl.BlockSpec((tr, D), lambda i, pr: (i, 0))],
            out_specs=pl.BlockSpec((tr, D), lambda i, pr: (pr[0] * nt + i, 0))),
        out_shape=jax.ShapeDtypeStruct((2 * hr, D), F32),
        compiler_params=_params(("parallel",)),
    )(place, parts, parts, parts, p32)


def _rs_pair_gather(both, name):
    hr = both.shape[0] // 2

    def body(in_ref, out_ref, send_sem, recv_sem):
        x, y, c = _place()
        mine = out_ref.at[pl.ds(c * hr, hr), :]
        cp = pltpu.make_async_remote_copy(
            src_ref=mine, dst_ref=mine, send_sem=send_sem, recv_sem=recv_sem,
            device_id=(x, y, 1 - c), device_id_type=MESH)
        cp.start()
        theirs = out_ref.at[pl.ds((1 - c) * hr, hr), :]
        pltpu.make_async_remote_copy(
            src_ref=theirs, dst_ref=theirs, send_sem=send_sem, recv_sem=recv_sem,
            device_id=(x, y, 1 - c), device_id_type=MESH).wait_recv()
        cp.wait_send()

    any_spec = pl.BlockSpec(memory_space=pl.ANY)
    return pl.pallas_call(
        body, name=name,
        out_shape=jax.ShapeDtypeStruct(both.shape, F32),
        in_specs=[any_spec], out_specs=any_spec, input_output_aliases={0: 0},
        scratch_shapes=[pltpu.SemaphoreType.DMA, pltpu.SemaphoreType.DMA],
    )(both)


def _local_step(x, ctx, tgt, mods, mc, ag_gin, ag_main, place, small):
    nb, t, _ = x.shape
    tc = ctx.shape[1]
    n = nb * t
    nc = nb * tc
    xf = x.reshape(n, D)
    cf = ctx.reshape(nc, D)
    tf = tgt.reshape(n, D)
    vec = lambda a: a.reshape(1, -1)
    m = [[mods[l, :, k, :].reshape(nb, 1, D) for k in range(N_MOD)] for l in range(2)]
    mc_b = [jnp.broadcast_to(mc[k].reshape(1, 1, D), (nb, 1, D)) for k in range(2)]

    cw = [small["ffn_conv_w"][l] for l in range(2)]
    cb = [small["ffn_conv_b"][l].reshape(1, -1) for l in range(2)]
    w2 = jnp.zeros((128, 2 * KEY), F32)
    w2 = w2.at[0:RANK, 0:KEY].set(small["gla_w_a2"][0]).at[RANK:2 * RANK, KEY:].set(small["gla_w_a2"][1])
    b2 = small["gla_b_a"].reshape(1, 2 * KEY)
    hg = small["gla_head_norm"].reshape(1, DV)

    hn_all = _mod_fwd(xf, vec(small["norm_mix"][0]), m[0][0], m[0][1], t, "mod0_main", out_rows=n + nc)
    hn_all = _mod_fwd(cf, vec(small["norm_mix"][0]), mc_b[0], mc_b[1], tc, "mod0_ctx", out_rows=n + nc,
                      into=hn_all, row0=n)
    gin = _ag_pass_on(_split_wait(ag_gin, hn_all, _ag_copies, "ag_gin_wait")[1], "ag_gin_pass_on")
    w_gin = jnp.pad(gin[:, :_GIN_ROWS, :].reshape(GLA_IN, D), ((0, GLA_IN_PAD - GLA_IN), (0, 0)))
    p_all, la_all = _gla_in_proj(hn_all, w_gin, w2, b2)
    o2, ss = _gla_scan_fwd(p_all, la_all, nb, t, tc)
    arrived = _split_wait(ag_main, o2[0], _ag_copies, "ag_main_wait")[1]
    passing = _split_start(ag_main[4], arrived, _pass_on_copies, 3, o2[1], "ag_main_pass_start")
    offs = _offsets(_MAIN, _MAIN_ROWS)
    rows = _MAIN_ROWS

    def w_nt(a, k, name, out_dtype=BF16, tm=1024):
        return _mm_nt_w(a, wg, offs[k], rows[k], name, tm, out_dtype)

    def w_nn_mod(a3, k, h, gate, gain, shift, scale, name):
        return _mm_nn_w_mod(a3, wg, offs[k], rows[k], h, gate, vec(gain), shift, scale, t, name, 512)

    yb0 = _gla_post_fwd(o2, p_all, hg + passing[4][0:1, 0:1], n)
    wg = _split_wait(passing, yb0, _pass_on_copies, "ag_main_pass_wait")[1]
    y0, h1, hn1 = w_nn_mod(yb0[None], "gla_out", xf, m[0][2], small["norm_ffn"][0], m[0][3], m[0][4],
                           "gla_out_proj_mod")
    u0, z0 = _ffn_up_mid_fwd(hn1, wg, offs["up_t0"], cw[0], cb[0], nb, t, "ffn0_up_mid")
    f0, h2, hn2 = w_nn_mod(z0[None], "down0", h1, m[0][5], small["norm_mix"][1], m[1][0], m[1][1],
                           "ffn0_down_mod")
    p1, yb1 = _sc_in_mid_fwd(hn2, wg, offs["sc_in_t"], small["sc_conv_w"], nb, t)
    y1, h3, hn3 = w_nn_mod(yb1[None], "sc_out", h2, m[1][2], small["norm_ffn"][1], m[1][3], m[1][4],
                           "sc_out_proj_mod")
    u1, z1 = _ffn_up_mid_fwd(hn3, wg, offs["up_t1"], cw[1], cb[1], nb, t, "ffn1_up_mid")
    loss, dh4, df1, dm15, dfinal = _mm_nn_w_final(z1[None], wg, offs["down1"], rows["down1"], h3, m[1][5],
                                                  vec(small["final_norm"]), tf, t, "ffn1_down_final", 512)

    gs = {}
    dmods = [[None] * N_MOD for _ in range(2)]
    dmods[1][5] = dm15

    def w_dw(a3, b, g_prev, k, name, tm):
        return _mm_dw(a3, b, g_prev, offs[k], rows[k], name, tm)

    def w_dx_mod(a3, k, h_in, dh_out, gain, scale, y_prev, gate_prev, name):
        return _mm_nn_w_modbwd(a3, wg, offs[k], rows[k], h_in, dh_out, vec(gain), scale, y_prev, gate_prev, t,
                               name, 256)

    def ffn_bwd(l, df, u, z, hn, g_prev, h_in, dh_out, scale, y_prev, gate_prev):
        dz = w_nt(df, f"down{l}", f"ffn{l}_down_dx")
        g_acc = w_dw(z[None], df, g_prev, f"down{l}", f"ffn{l}_down_dw", 640)
        du, dcw, dcb = _ffn_mid_bwd(u, cw[l], cb[l], dz, nb, t, f"ffn{l}_mid_bwd")
        r = w_dx_mod(du, f"up_t{l}", h_in, dh_out, small["norm_ffn"][l], scale, y_prev, gate_prev,
                     f"ffn{l}_up_dx_mod")
        g_acc = w_dw(du, hn, g_acc, f"up_t{l}", f"ffn{l}_up_dw", 640)
        return r, g_acc, jnp.moveaxis(dcw, 0, 1).reshape(3, 2 * HID), dcb.reshape(2 * HID)

    r, g_acc, dcw1, dcb1 = ffn_bwd(1, df1, u1, z1, hn3, None, h3, dh4, m[1][4], y1, m[1][2])
    dh3, dmods[1][4], dmods[1][3], dnf1, dy1, dmods[1][2] = (r["dh"], r["dscale"], r["dshift"], r["dgain"],
                                                             r["dy_prev"], r["dgate_prev"])
    dyb1 = w_nt(dy1, "sc_out", "sc_out_dx")
    g_acc = w_dw(yb1[None], dy1, g_acc, "sc_out", "sc_out_dw", 256)
    dp1, dscw = _sc_mid_bwd(p1, small["sc_conv_w"], dyb1, nb, t)
    r = w_dx_mod(dp1, "sc_in_t", h2, dh3, small["norm_mix"][1], m[1][1], f0, m[0][5], "sc_in_dx_mod")
    g_acc = w_dw(dp1, hn2, g_acc, "sc_in_t", "sc_in_dw", 256)
    dh2, dmods[1][1], dmods[1][0], dnm1, df0, dmods[0][5] = (r["dh"], r["dscale"], r["dshift"], r["dgain"],
                                                             r["dy_prev"], r["dgate_prev"])
    r, g_acc, dcw0, dcb0 = ffn_bwd(0, df0, u0, z0, hn1, g_acc, h1, dh2, m[0][4], y0, m[0][2])
    dh1, dmods[0][4], dmods[0][3], dnf0, dy0, dmods[0][2] = (r["dh"], r["dscale"], r["dshift"], r["dgain"],
                                                             r["dy_prev"], r["dgate_prev"])
    g_packed = w_dw(yb0[None], dy0, g_acc, "gla_out", "gla_out_dw", 256)
    pair = _split_start(g_packed, (N_CHIPS, _MAIN_TOTAL // 2, D), _pe_copies, 1, dy0, "rs_main_pair_start")
    do, dgate, dhg = _gla_out_dx_post_bwd(dy0, wg, offs["gla_out"], o2, p_all, hg + pair[4][0:1, 0:1], n)
    g_packed, from_sibling = _split_wait(pair, do, _pe_copies, "rs_main_pair_wait")
    p16, p32 = _rs_chip_sum(place, g_packed, from_sibling, "rs_main_chip_sum")
    sc_main = _split_start(p16, p16.shape, _sc_copies, 3, p32, "rs_main_scatter_start")
    dq, dk, dv, dla = _gla_scan_bwd(p_all, la_all, do, ss, nb, t, tc, sc_main[4])
    dp, dw2, db2 = _gla_assemble(p_all, w2, b2, dq, dk, dv, dla, dgate, n)
    dhn_all = _mm(dp, w_gin, "nn", F32, "gla_in_dx", 768, 512)
    landed = _split_wait(sc_main, dhn_all, _sc_copies, "rs_main_scatter_wait")[1]
    g_main = _split_start(sc_main[4], _rs_final_sum(place, landed, p32, "rs_main_final_sum"),
                          _pair_gather_copies, 1, landed, "rs_main_gather_start")
    g_gin = _mm(dp, hn_all, "tn", F32, "gla_in_dw", 640, 1024)[:GLA_IN]
    g_gin = jnp.pad(g_gin.reshape(N_CHIPS, _GIN_ROWS, D), ((0, 0), (0, _GIN_PAD - _GIN_ROWS), (0, 0)))
    from_sibling = _rs_pair_exchange(g_gin, "rs_gin_pair_exchange")
    p16_gin, p32_gin = _rs_chip_sum(place, g_gin, from_sibling, "rs_gin_chip_sum")
    r = _mod_bwd(xf, dhn_all, vec(small["norm_mix"][0]) + g_main[4][0:1, 0:1], m[0][1], t, "mod0_main_bwd",
                 dh_out=dh1)
    grad_x, dmods[0][1], dmods[0][0], dnm0 = r["dh"], r["dscale"], r["dshift"], r["dgain"]
    rc = _mod_bwd(cf, dhn_all, vec(small["norm_mix"][0]), mc_b[1], tc, "mod0_ctx_bwd", dhn_row0=n, need_dh=False)
    dmc = jnp.stack([jnp.sum(rc["dshift"], axis=0).reshape(D), jnp.sum(rc["dscale"], axis=0).reshape(D)])
    dnm0 = dnm0 + rc["dgain"]

    gs["norm_mix"] = jnp.concatenate([dnm0, dnm1], axis=0)
    gs["norm_ffn"] = jnp.concatenate([dnf0, dnf1], axis=0)
    gs["final_norm"] = dfinal.reshape(D)
    gs["gla_w_a2"] = jnp.stack([dw2[0:RANK, 0:KEY], dw2[RANK:2 * RANK, KEY:]])
    gs["gla_b_a"] = db2.reshape(2, KEY)
    gs["gla_head_norm"] = dhg.reshape(DV)
    gs["sc_conv_w"] = dscw
    gs["ffn_conv_w"] = jnp.stack([dcw0, dcw1])
    gs["ffn_conv_b"] = jnp.stack([dcb0, dcb1])
    dmods_arr = jnp.stack([jnp.stack([dmods[l][k].reshape(nb, D) for k in range(N_MOD)], axis=1) for l in range(2)])
    return loss, grad_x.reshape(nb, t, D), g_main, p16_gin, p32_gin, gs, dmods_arr, dmc


def _pack(arrs):
    parts, meta, off = [], [], 0
    for a in arrs:
        r = a.size // 128
        rp = -(-r // 8) * 8
        a2 = a.reshape(r, 128).astype(F32)
        if rp != r:
            a2 = jnp.pad(a2, ((0, rp - r), (0, 0)))
        parts.append(a2)
        meta.append((off, r, a.shape))
        off += rp
    return jnp.concatenate(parts, axis=0), meta


def _unpack(buf, meta, lead=()):
    return [buf[..., off:off + r, :].reshape(*lead, *shape) for off, r, shape in meta]


_MAIN = ("up_t0", "up_t1", "down0", "down1", "sc_in_t", "gla_out", "sc_out")
_MAIN_ROWS = {"sc_in_t": 3 * D // N_CHIPS, "up_t0": 2 * HID // N_CHIPS, "up_t1": 2 * HID // N_CHIPS,
              "gla_out": D // N_CHIPS, "sc_out": D // N_CHIPS, "down0": HID // N_CHIPS, "down1": HID // N_CHIPS}
_MAIN_TOTAL = sum(_MAIN_ROWS.values())
_GIN_ROWS = GLA_IN // N_CHIPS
_GIN_PAD = -(-_GIN_ROWS // 32) * 32


def _offsets(names, rows):
    off, out = 0, {}
    for k in names:
        out[k] = off
        off += rows[k]
    return out


def kernel(x, c, ctx, c_ctx, ada_w, ada_b, norm_mix, norm_ffn, gla_w_in, gla_w_a2, gla_b_a, gla_head_norm, gla_w_out, sc_w_in, sc_conv_w, sc_w_out, ffn_w_up, ffn_conv_w, ffn_conv_b, ffn_w_down, final_norm, loss_target, m_c_ctx, m_ada_w, m_ada_b, m_norm_mix, m_norm_ffn, m_gla_w_in, m_gla_w_a2, m_gla_b_a, m_gla_head_norm, m_gla_w_out, m_sc_w_in, m_sc_conv_w, m_sc_w_out, m_ffn_w_up, m_ffn_conv_w, m_ffn_conv_b, m_ffn_w_down, m_final_norm, v_c_ctx, v_ada_w, v_ada_b, v_norm_mix, v_norm_ffn, v_gla_w_in, v_gla_w_a2, v_gla_b_a, v_gla_head_norm, v_gla_w_out, v_sc_w_in, v_sc_conv_w, v_sc_w_out, v_ffn_w_up, v_ffn_conv_w, v_ffn_conv_b, v_ffn_w_down, v_final_norm):
    ix, iy, ic = _place()
    chip = 2 * ix + iy
    dev = 2 * chip + ic
    place = jnp.stack([ic, chip]).astype(jnp.int32)
    nb = x.shape[0]
    offs = _offsets(_MAIN, _MAIN_ROWS)

    buf, meta = _pack([c, ffn_conv_w, sc_conv_w, gla_w_a2, gla_b_a])
    got = _allgather_small(buf, "gather_small_in").reshape(N_DEV, buf.shape[0], 128)
    c_all, fcw, scw, wa2, ba = _unpack(got, meta, (N_DEV,))
    c_all = c_all.reshape(N_DEV * nb, D)
    per_chip = lambda a: a[0::2]
    ffn_conv_w_full = jnp.moveaxis(per_chip(fcw), 0, 2).reshape(2, 3, 2 * HID)
    sc_conv_w_full = jnp.moveaxis(per_chip(scw)[:, 0], 0, 1).reshape(3, D)
    gla_w_a2_full = jnp.moveaxis(per_chip(wa2)[:, 0], 0, 2).reshape(2, RANK, KEY)
    gla_b_a_full = jnp.moveaxis(per_chip(ba)[:, 0], 0, 1).reshape(2, KEY)

    cvec = jnp.concatenate([c_all, c_ctx.reshape(1, D), jnp.zeros((ADA_ROWS - N_DEV * nb - 1, D), F32)], axis=0)
    ada_b_sh = lax.dynamic_slice_in_dim(ada_b, chip * ADA_SH, ADA_SH, axis=1).reshape(2, 1, ADA_SH)
    mod_sh = _ada_fwd(cvec, ada_w, ada_b_sh)
    got = _allgather_small(mod_sh.reshape(2 * ADA_ROWS, ADA_SH), "gather_mod")
    mod_full = jnp.moveaxis(per_chip(got.reshape(N_DEV, 2, ADA_ROWS, ADA_SH)), 0, 2).reshape(2, ADA_ROWS, N_MOD * D)
    mc = mod_full[0, N_DEV * nb, :2 * D].reshape(2, D)

    own = {"sc_in_t": sc_w_in[0].T, "up_t0": ffn_w_up[0].T, "up_t1": ffn_w_up[1].T,
           "gla_out": gla_w_out[0], "sc_out": sc_w_out[0], "down0": ffn_w_down[0], "down1": ffn_w_down[1]}
    own_main = jnp.concatenate([own[k].astype(BF16) for k in _MAIN], axis=0)
    own_gin = jnp.pad(gla_w_in[0].T.astype(BF16), ((0, _GIN_PAD - _GIN_ROWS), (0, 0)))
    ag_gin = _split_start(own_gin, (N_CHIPS, _GIN_PAD, D), _ag_copies, 4, mc, "ag_gin_start")
    ag_main = _split_start(own_main, (N_CHIPS, _MAIN_TOTAL, D), _ag_copies, 4, ag_gin[4], "ag_main_start")
    mods = lax.dynamic_slice_in_dim(mod_full, dev * nb, nb, axis=1).reshape(2, nb, N_MOD, D) + ag_main[4][0, 0]

    small = {"norm_mix": norm_mix, "norm_ffn": norm_ffn, "final_norm": final_norm, "gla_w_a2": gla_w_a2_full,
             "gla_b_a": gla_b_a_full, "gla_head_norm": gla_head_norm[0], "sc_conv_w": sc_conv_w_full,
             "ffn_conv_w": ffn_conv_w_full, "ffn_conv_b": ffn_conv_b}
    loss_p, grad_x, g_main, p16_gin, p32_gin, gs, dmods, dmc = _local_step(x, ctx, loss_target, mods, mc, ag_gin,
                                                                           ag_main, place, small)

    sum_names = ["norm_mix", "norm_ffn", "final_norm", "gla_w_a2", "gla_b_a", "gla_head_norm", "sc_conv_w",
                 "ffn_conv_w", "ffn_conv_b"]
    buf, meta = _pack([jnp.broadcast_to(loss_p, (8, 128))] + [gs[k] for k in sum_names] + [dmc, dmods])
    n_sum = meta[-1][0]
    got = _allgather_small(buf, "gather_small_grads").reshape(N_DEV, buf.shape[0], 128)
    summed = _sum_slots(got[:, :n_sum], "sum_small_grads")
    parts = _unpack(summed, meta[:-1])
    loss = parts[0][0, 0]
    g_small = dict(zip(sum_names, parts[1:-1]))
    dmc_tot = parts[-1]
    dmods_all = jnp.moveaxis(_unpack(got, meta[-1:], (N_DEV,))[0], 0, 1).reshape(2, N_DEV * nb, N_MOD * D)

    ctx_row = jnp.stack([jnp.concatenate([dmc_tot.reshape(2 * D), jnp.zeros(((N_MOD - 2) * D,), F32)]),
                         jnp.zeros((N_MOD * D,), F32)]).reshape(2, 1, N_MOD * D)
    dmod_ext = jnp.concatenate([dmods_all, ctx_row, jnp.zeros((2, ADA_ROWS - N_DEV * nb - 1, N_MOD * D), F32)], axis=1)
    g_ada_b = _sum_slots(jnp.moveaxis(dmod_ext, 1, 0).reshape(ADA_ROWS, 2 * N_MOD * D // 128, 128),
                         "sum_ada_b").reshape(2, N_MOD * D)
    dmod_sh = lax.dynamic_slice_in_dim(dmod_ext, chip * ADA_SH, ADA_SH, axis=2)
    g_ada_w, dcv = _ada_bwd(cvec, ada_w, dmod_sh)
    dscc_part = (dcv[0, N_DEV * nb] + dcv[1, N_DEV * nb]).reshape(8, 128)
    got = _allgather_small(dscc_part, "gather_dscc").reshape(N_DEV, 8, 128)
    g_c_ctx = _cctx_grad(per_chip(got), c_ctx.reshape(8, 128)).reshape(D)

    sc_gin = _split_start(p16_gin, p16_gin.shape, _sc_copies, 3, g_c_ctx, "rs_gin_scatter_start")
    g_main = _split_wait(g_main, sc_gin[4], _pair_gather_copies, "rs_main_gather_wait")[1]
    seg = {k: g_main[offs[k]:offs[k] + _MAIN_ROWS[k]] for k in _MAIN}

    sl_chip = lambda a, axis, width: lax.dynamic_slice_in_dim(a, chip * width, width, axis=axis)
    grads = {
        "c_ctx": g_c_ctx, "ada_w": g_ada_w, "ada_b": g_ada_b, "norm_mix": g_small["norm_mix"],
        "norm_ffn": g_small["norm_ffn"],
        "gla_w_a2": sl_chip(g_small["gla_w_a2"], 2, KEY // N_CHIPS)[None],
        "gla_b_a": sl_chip(g_small["gla_b_a"], 1, KEY // N_CHIPS)[None],
        "gla_head_norm": g_small["gla_head_norm"][None], "gla_w_out": seg["gla_out"][None],
        "sc_w_in": seg["sc_in_t"].T[None], "sc_conv_w": sl_chip(g_small["sc_conv_w"], 1, D // N_CHIPS)[None],
        "sc_w_out": seg["sc_out"][None], "ffn_w_up": jnp.stack([seg["up_t0"].T, seg["up_t1"].T]),
        "ffn_conv_w": sl_chip(g_small["ffn_conv_w"], 2, 2 * HID // N_CHIPS), "ffn_conv_b": g_small["ffn_conv_b"],
        "ffn_w_down": jnp.stack([seg["down0"], seg["down1"]]), "final_norm": g_small["final_norm"],
    }
    weights = {"c_ctx": c_ctx, "ada_w": ada_w, "ada_b": ada_b, "norm_mix": norm_mix, "norm_ffn": norm_ffn,
               "gla_w_in": gla_w_in, "gla_w_a2": gla_w_a2, "gla_b_a": gla_b_a, "gla_head_norm": gla_head_norm,
               "gla_w_out": gla_w_out, "sc_w_in": sc_w_in, "sc_conv_w": sc_conv_w, "sc_w_out": sc_w_out,
               "ffn_w_up": ffn_w_up, "ffn_conv_w": ffn_conv_w, "ffn_conv_b": ffn_conv_b, "ffn_w_down": ffn_w_down,
               "final_norm": final_norm}
    mom1 = {"c_ctx": m_c_ctx, "ada_w": m_ada_w, "ada_b": m_ada_b, "norm_mix": m_norm_mix, "norm_ffn": m_norm_ffn,
            "gla_w_in": m_gla_w_in, "gla_w_a2": m_gla_w_a2, "gla_b_a": m_gla_b_a, "gla_head_norm": m_gla_head_norm,
            "gla_w_out": m_gla_w_out, "sc_w_in": m_sc_w_in, "sc_conv_w": m_sc_conv_w, "sc_w_out": m_sc_w_out,
            "ffn_w_up": m_ffn_w_up, "ffn_conv_w": m_ffn_conv_w, "ffn_conv_b": m_ffn_conv_b,
            "ffn_w_down": m_ffn_w_down, "final_norm": m_final_norm}
    mom2 = {"c_ctx": v_c_ctx, "ada_w": v_ada_w, "ada_b": v_ada_b, "norm_mix": v_norm_mix, "norm_ffn": v_norm_ffn,
            "gla_w_in": v_gla_w_in, "gla_w_a2": v_gla_w_a2, "gla_b_a": v_gla_b_a, "gla_head_norm": v_gla_head_norm,
            "gla_w_out": v_gla_w_out, "sc_w_in": v_sc_w_in, "sc_conv_w": v_sc_conv_w, "sc_w_out": v_sc_w_out,
            "ffn_w_up": v_ffn_w_up, "ffn_conv_w": v_ffn_conv_w, "ffn_conv_b": v_ffn_conv_b,
            "ffn_w_down": v_ffn_w_down, "final_norm": v_final_norm}
    names = list(weights)

    big_names = ["ada_w", "gla_w_out", "sc_w_in", "sc_w_out", "ffn_w_up", "ffn_w_down", "gla_w_in"]
    small_names = [k for k in names if k not in big_names]
    delta, new_m, new_v = {}, {}, {}
    done = []

    def big_adamw(k, token):
        delta[k], new_m[k], new_v[k] = _adamw(weights[k], grads[k], mom1[k], mom2[k], "adamw_" + k, token)
        done.append(new_v[k][0, 0:1, 0:128])

    for k in big_names[:-1]:
        grads[k] = grads[k].reshape(weights[k].shape)
        big_adamw(k, sc_gin[4])
    for k in small_names:
        grads[k] = grads[k].reshape(weights[k].shape)
    packed = [_pack([src[k] for k in small_names]) for src in (weights, grads, mom1, mom2)]
    meta = packed[0][1]
    rows_pad = -packed[0][0].shape[0] % 128
    bufs = [jnp.pad(p[0], ((0, rows_pad), (0, 0)))[None] for p in packed]
    outs = _adamw(bufs[0], bufs[1], bufs[2], bufs[3], "adamw_small", sc_gin[4])
    done.append(outs[2][0, 0:1, :])
    for dst, o in zip((delta, new_m, new_v), outs):
        for k, a in zip(small_names, _unpack(o[0], meta)):
            dst[k] = a
    landed = _split_wait(sc_gin, jnp.concatenate(done, axis=0), _sc_copies, "rs_gin_scatter_wait")[1]
    g_gin_shard = _rs_pair_gather(_rs_final_sum(place, landed, p32_gin, "rs_gin_final_sum"), "rs_gin_pair_gather")
    grads["gla_w_in"] = g_gin_shard[:_GIN_ROWS].T[None]
    big_adamw("gla_w_in", sc_gin[4])

    return (loss, grad_x, *[grads[k] for k in names], *[delta[k] for k in names], *[new_m[k] for k in names],
            *[new_v[k] for k in names])
```

```python
import functools

import jax
import jax.numpy as jnp
from jax import lax
from jax.experimental import pallas as pl
from jax.experimental.pallas import tpu as pltpu

F32 = jnp.float32
BF16 = jnp.bfloat16
MESH = pl.DeviceIdType.MESH

EPS = 1e-6
D = 1024
N_MOD = 6
HEADS = 4
DK = 128
DV = 256
KEY = HEADS * DK
RANK = 16
TAU = 16.0
CH = 64
GRID_W = 64
HID = 2560
GLA_IN = 2 * KEY + 2 * D + 2 * RANK
GLA_IN_PAD = 3200
Q_SCALE = DK ** -0.5
N_CHIPS = 4
N_DEV = 8

ADAM_LR = 0.001
ADAM_B1 = 0.9
ADAM_B2 = 0.999
ADAM_EPS = 1e-08
ADAM_WD = 0.01
ADAM_STEP = 10

VMEM_LIMIT = 56 * 1024 * 1024


def _params(sem):
    return pltpu.CompilerParams(dimension_semantics=sem, vmem_limit_bytes=VMEM_LIMIT)


def _tile(n, pref, mult=8):
    if n <= pref:
        return n
    for t in range(pref - pref % mult, 0, -mult):
        if n % t == 0:
            return t
    raise ValueError((n, pref, mult))


_NN = (((1,), (0,)), ((), ()))
_NT = (((1,), (1,)), ((), ()))
_TN = (((0,), (0,)), ((), ()))


def _dot(a, b, dims=_NN):
    return lax.dot_general(a.astype(BF16), b.astype(BF16), dims, preferred_element_type=F32)


def _sigmoid(x):
    return 1.0 / (1.0 + jnp.exp(-x))


def _rowsum(x):
    return jnp.sum(x, axis=0, keepdims=True)


def _mm(a, b, form, out_dtype, name, tm, tn):
    if form == "tn":
        K, M = a.shape
    else:
        M, K = a.shape
    N = b.shape[0] if form == "nt" else b.shape[1]
    tm = _tile(M, tm, 128)
    tn = _tile(N, tn, 128)
    dims = {"nn": _NN, "nt": _NT, "tn": _TN}[form]

    def body(a_ref, b_ref, o_ref):
        o_ref[...] = _dot(a_ref[...], b_ref[...], dims).astype(o_ref.dtype)

    if form == "tn":
        a_spec = pl.BlockSpec((K, tm), lambda i, j: (0, i))
    else:
        a_spec = pl.BlockSpec((tm, K), lambda i, j: (i, 0))
    if form == "nt":
        b_spec = pl.BlockSpec((tn, K), lambda i, j: (j, 0))
    else:
        b_spec = pl.BlockSpec((K, tn), lambda i, j: (0, j))
    return pl.pallas_call(
        body,
        name=name,
        grid=(M // tm, N // tn),
        in_specs=[a_spec, b_spec],
        out_specs=pl.BlockSpec((tm, tn), lambda i, j: (i, j)),
        out_shape=jax.ShapeDtypeStruct((M, N), out_dtype),
        compiler_params=_params(("parallel", "parallel")),
    )(a, b)


def _mm_nt_w(a, wg, off, rows, name, tm, out_dtype):
    m = a.shape[0]
    tm = _tile(m, tm, 128)
    if N_CHIPS * rows <= D:

        def body_small(a_ref, w_ref, o_ref):
            av = a_ref[...]
            for s in range(N_CHIPS):
                o_ref[:, s * rows:(s + 1) * rows] = _dot(av, w_ref[s], _NT).astype(o_ref.dtype)

        return pl.pallas_call(
            body_small, name=name, grid=(m // tm,),
            in_specs=[pl.BlockSpec((tm, D), lambda i: (i, 0)),
                      pl.BlockSpec((N_CHIPS, rows, D), lambda i: (0, off // rows, 0))],
            out_specs=pl.BlockSpec((tm, N_CHIPS * rows), lambda i: (i, 0)),
            out_shape=jax.ShapeDtypeStruct((m, N_CHIPS * rows), out_dtype),
            compiler_params=_params(("parallel",)),
        )(a, wg)

    def body(a_ref, w_ref, o_ref):
        o_ref[...] = _dot(a_ref[...], w_ref[0], _NT).astype(o_ref.dtype)

    return pl.pallas_call(
        body, name=name, grid=(m // tm, N_CHIPS),
        in_specs=[pl.BlockSpec((tm, D), lambda i, s: (i, 0)),
                  pl.BlockSpec((1, rows, D), lambda i, s: (s, off // rows, 0))],
        out_specs=pl.BlockSpec((tm, rows), lambda i, s: (i, s)),
        out_shape=jax.ShapeDtypeStruct((m, N_CHIPS * rows), out_dtype),
        compiler_params=_params(("parallel", "parallel")),
    )(a, wg)


def _mm_nn_w_mod(a3, wg, off, rows, h, gate, gain, shift, scale, tpb_rows, name, tm):
    parts, m, kp = a3.shape
    assert parts * kp == N_CHIPS * rows
    tm = _tile(tpb_rows, tm, 128)
    tpb = tpb_rows // tm
    cuts = sorted({s * rows for s in range(N_CHIPS + 1)} | {p * kp for p in range(parts + 1)})
    pieces = [(k0 // kp, k0 % kp, k0 // rows, k0 % rows, k1 - k0) for k0, k1 in zip(cuts[:-1], cuts[1:])]

    def body(a_ref, w_ref, h_ref, gate_ref, gain_ref, sh_ref, sc_ref, y_ref, hout_ref, hn_ref):
        acc = None
        for p, a0, s, r0, width in pieces:
            term = _dot(a_ref[p, :, a0:a0 + width], w_ref[s, r0:r0 + width, :])
            acc = term if acc is None else acc + term
        y_ref[...] = acc
        hv = h_ref[...] + gate_ref[0] * acc
        hout_ref[...] = hv
        r = lax.rsqrt(jnp.mean(hv * hv, axis=-1, keepdims=True) + EPS)
        hn_ref[...] = ((hv * r) * gain_ref[...] * (1.0 + sc_ref[0]) + sh_ref[0]).astype(BF16)

    row = pl.BlockSpec((tm, D), lambda i: (i, 0))
    per_b = pl.BlockSpec((1, 1, D), lambda i: (i // tpb, 0, 0))
    return pl.pallas_call(
        body, name=name, grid=(m // tm,),
        in_specs=[pl.BlockSpec((parts, tm, kp), lambda i: (0, i, 0)),
                  pl.BlockSpec((N_CHIPS, rows, D), lambda i: (0, off // rows, 0)),
                  row, per_b, pl.BlockSpec((1, D), lambda i: (0, 0)), per_b, per_b],
        out_specs=[row, row, row],
        out_shape=[jax.ShapeDtypeStruct((m, D), F32), jax.ShapeDtypeStruct((m, D), F32),
                   jax.ShapeDtypeStruct((m, D), BF16)],
        compiler_params=_params(("parallel",)),
    )(a3, wg, h, gate, gain, shift, scale)


def _mm_nn_w_final(a3, wg, off, rows, h, gate, gain, tgt, tpb_rows, name, tm):
    parts, m, kp = a3.shape
    assert parts * kp == N_CHIPS * rows
    nb = m // tpb_rows
    tm = _tile(tpb_rows, tm, 128)
    tpb = tpb_rows // tm
    cuts = sorted({s * rows for s in range(N_CHIPS + 1)} | {p * kp for p in range(parts + 1)})
    pieces = [(k0 // kp, k0 % kp, k0 // rows, k0 % rows, k1 - k0) for k0, k1 in zip(cuts[:-1], cuts[1:])]

    def body(a_ref, w_ref, h_ref, gate_ref, gain_ref, tgt_ref, loss_ref, dh_ref, df_ref, dgate_ref, dgain_ref):
        i = pl.program_id(0)

        @pl.when(i == 0)
        def _():
            loss_ref[...] = jnp.zeros_like(loss_ref)
            dgain_ref[...] = jnp.zeros_like(dgain_ref)

        @pl.when(i % tpb == 0)
        def _():
            dgate_ref[...] = jnp.zeros_like(dgate_ref)

        fv = None
        for p, a0, s, r0, width in pieces:
            term = _dot(a_ref[p, :, a0:a0 + width], w_ref[s, r0:r0 + width, :])
            fv = term if fv is None else fv + term
        gate_v = gate_ref[0]
        hv = h_ref[...] + gate_v * fv
        r = lax.rsqrt(jnp.mean(hv * hv, axis=-1, keepdims=True) + EPS)
        y = hv * r
        gain_v = gain_ref[...]
        e = y * gain_v - tgt_ref[...]
        s_ = jnp.sum(_rowsum(e * e), axis=1, keepdims=True) * (0.5 / D)
        loss_ref[...] += jnp.broadcast_to(s_, loss_ref.shape)
        dout = e * (1.0 / D)
        dgain_ref[...] += _rowsum(dout * y)
        dy = dout * gain_v
        dh = r * (dy - y * jnp.mean(dy * y, axis=-1, keepdims=True))
        dh_ref[...] = dh
        df_ref[...] = (dh * gate_v).astype(BF16)
        dgate_ref[0] += _rowsum(dh * fv)

    row = pl.BlockSpec((tm, D), lambda i: (i, 0))
    per_b = pl.BlockSpec((1, 1, D), lambda i: (i // tpb, 0, 0))
    vec = pl.BlockSpec((1, D), lambda i: (0, 0))
    return pl.pallas_call(
        body, name=name, grid=(m // tm,),
        in_specs=[pl.BlockSpec((parts, tm, kp), lambda i: (0, i, 0)),
                  pl.BlockSpec((N_CHIPS, rows, D), lambda i: (0, off // rows, 0)), row, per_b, vec, row],
        out_specs=[pl.BlockSpec((1, 128), lambda i: (0, 0)), row, row, per_b, vec],
        out_shape=[jax.ShapeDtypeStruct((1, 128), F32), jax.ShapeDtypeStruct((m, D), F32),
                   jax.ShapeDtypeStruct((m, D), BF16), jax.ShapeDtypeStruct((nb, 1, D), F32),
                   jax.ShapeDtypeStruct((1, D), F32)],
        compiler_params=_params(("arbitrary",)),
    )(a3, wg, h, gate, gain, tgt)


def _mm_nn_w_modbwd(a3, wg, off, rows, h_in, dh_out, gain, scale, y_prev, gate_prev, tpb_rows, name, tm):
    parts, m, kp = a3.shape
    assert parts * kp == N_CHIPS * rows
    nb = m // tpb_rows
    tm = _tile(tpb_rows, tm, 128)
    tpb = tpb_rows // tm
    cuts = sorted({s * rows for s in range(N_CHIPS + 1)} | {p * kp for p in range(parts + 1)})
    pieces = [(k0 // kp, k0 % kp, k0 // rows, k0 % rows, k1 - k0) for k0, k1 in zip(cuts[:-1], cuts[1:])]

    def body(a_ref, w_ref, h_ref, gain_ref, sc_ref, dho_ref, yp_ref, gp_ref,
             dh_ref, dsc_ref, dsh_ref, dgain_ref, dyp_ref, dgp_ref):
        i = pl.program_id(0)

        @pl.when(i == 0)
        def _():
            dgain_ref[...] = jnp.zeros_like(dgain_ref)

        @pl.when(i % tpb == 0)
        def _():
            dsc_ref[...] = jnp.zeros_like(dsc_ref)
            dsh_ref[...] = jnp.zeros_like(dsh_ref)
            dgp_ref[...] = jnp.zeros_like(dgp_ref)

        g = None
        for p, a0, s, r0, width in pieces:
            term = _dot(a_ref[p, :, a0:a0 + width], w_ref[s, r0:r0 + width, :])
            g = term if g is None else g + term
        hv = h_ref[...]
        r = lax.rsqrt(jnp.mean(hv * hv, axis=-1, keepdims=True) + EPS)
        y = hv * r
        gain_v = gain_ref[...]
        dsh_ref[0] += _rowsum(g)
        dsc_ref[0] += _rowsum(g * (y * gain_v))
        drn = g * (1.0 + sc_ref[0])
        dgain_ref[...] += _rowsum(drn * y)
        dy = drn * gain_v
        dh = r * (dy - y * jnp.mean(dy * y, axis=-1, keepdims=True)) + dho_ref[...]
        dh_ref[...] = dh
        dyp_ref[...] = (dh * gp_ref[0]).astype(BF16)
        dgp_ref[0] += _rowsum(dh * yp_ref[...])

    row = pl.BlockSpec((tm, D), lambda i: (i, 0))
    per_b = pl.BlockSpec((1, 1, D), lambda i: (i // tpb, 0, 0))
    vec = pl.BlockSpec((1, D), lambda i: (0, 0))
    per_b_shape = jax.ShapeDtypeStruct((nb, 1, D), F32)
    outs = pl.pallas_call(
        body, name=name, grid=(m // tm,),
        in_specs=[pl.BlockSpec((parts, tm, kp), lambda i: (0, i, 0)),
                  pl.BlockSpec((N_CHIPS, rows, D), lambda i: (0, off // rows, 0)),
                  row, vec, per_b, row, row, per_b],
        out_specs=[row, per_b, per_b, vec, row, per_b],
        out_shape=[jax.ShapeDtypeStruct((m, D), F32), per_b_shape, per_b_shape, jax.ShapeDtypeStruct((1, D), F32),
                   jax.ShapeDtypeStruct((m, D), BF16), per_b_shape],
        compiler_params=_params(("arbitrary",)),
    )(a3, wg, h_in, gain, scale, dh_out, y_prev, gate_prev)
    return dict(zip(("dh", "dscale", "dshift", "dgain", "dy_prev", "dgate_prev"), outs))


def _mm_dw(a3, b, g_prev, off, rows, name, tm):
    parts, ntok, cdim = a3.shape
    assert parts * cdim == N_CHIPS * rows and cdim % tm == 0 and rows % tm == 0 and off % tm == 0

    def body(a_ref, b_ref, *rest):
        rest[-1][0] = _dot(a_ref[0], b_ref[...], _TN)

    in_specs = [pl.BlockSpec((1, ntok, tm), lambda i: ((i * tm) // cdim, 0, ((i * tm) % cdim) // tm)),
                pl.BlockSpec((ntok, D), lambda i: (0, 0))]
    args = [a3, b]
    aliases = {}
    if g_prev is not None:
        in_specs.append(pl.BlockSpec(memory_space=pl.ANY))
        args.append(g_prev)
        aliases = {2: 0}
    return pl.pallas_call(
        body, name=name, grid=(N_CHIPS * rows // tm,),
        in_specs=in_specs,
        out_specs=pl.BlockSpec((1, tm, D), lambda i: ((i * tm) // rows, (off + (i * tm) % rows) // tm, 0)),
        out_shape=jax.ShapeDtypeStruct((N_CHIPS, _MAIN_TOTAL, D), F32),
        input_output_aliases=aliases,
        compiler_params=_params(("parallel",)),
    )(*args)


def _mod_fwd(h, gain, shift, scale, tpb_rows, name, y=None, gate=None, out_rows=None, into=None, row0=0):
    n = h.shape[0]
    tt = _tile(tpb_rows, 256)
    tpb = tpb_rows // tt
    has_res = y is not None
    assert row0 % tt == 0 and not (has_res and out_rows)

    def body(*refs):
        if has_res:
            h_ref, y_ref, gate_ref, gain_ref, sh_ref, sc_ref, hout_ref, hn_ref = refs
            hv = h_ref[...] + gate_ref[0] * y_ref[...]
            hout_ref[...] = hv
        else:
            h_ref, gain_ref, sh_ref, sc_ref, hn_ref = refs[0], refs[1], refs[2], refs[3], refs[-1]
            hv = h_ref[...]
        r = lax.rsqrt(jnp.mean(hv * hv, axis=-1, keepdims=True) + EPS)
        hn = (hv * r) * gain_ref[...] * (1.0 + sc_ref[0]) + sh_ref[0]
        hn_ref[...] = hn.astype(BF16)

    row = pl.BlockSpec((tt, D), lambda i: (i, 0))
    per_b = pl.BlockSpec((1, 1, D), lambda i: (i // tpb, 0, 0))
    vec = pl.BlockSpec((1, D), lambda i: (0, 0))
    if has_res:
        in_specs = [row, row, per_b, vec, per_b, per_b]
        args = (h, y, gate, gain, shift, scale)
        out_specs = [row, row]
        out_shape = [jax.ShapeDtypeStruct((n, D), F32), jax.ShapeDtypeStruct((n, D), BF16)]
    else:
        in_specs = [row, vec, per_b, per_b]
        args = (h, gain, shift, scale)
        out_specs = pl.BlockSpec((tt, D), lambda i: (i + row0 // tt, 0))
        out_shape = jax.ShapeDtypeStruct((out_rows or n, D), BF16)
    aliases = {}
    if into is not None:
        in_specs = in_specs + [pl.BlockSpec(memory_space=pl.ANY)]
        args = args + (into,)
        aliases = {4: 0}
    return pl.pallas_call(
        body, name=name, grid=(n // tt,), in_specs=in_specs, out_specs=out_specs, out_shape=out_shape,
        input_output_aliases=aliases, compiler_params=_params(("parallel",)),
    )(*args)


def _mod_bwd(h_in, dhn, gain, scale, tpb_rows, name, dhn_row0=0, dh_out=None, y_prev=None, gate_prev=None,
             need_dh=True):
    n = h_in.shape[0]
    nb = n // tpb_rows
    tt = _tile(tpb_rows, 256)
    tpb = tpb_rows // tt
    off = dhn_row0 // tt
    assert dhn_row0 % tt == 0
    has_out = dh_out is not None
    has_prev = y_prev is not None

    def body(*refs):
        it = iter(refs)
        h_ref, dhn_ref, gain_ref, sc_ref = next(it), next(it), next(it), next(it)
        dho_ref = next(it) if has_out else None
        yp_ref, gp_ref = (next(it), next(it)) if has_prev else (None, None)
        dh_ref = next(it) if need_dh else None
        dsc_ref, dsh_ref, dgain_ref = next(it), next(it), next(it)
        dyp_ref, dgp_ref = (next(it), next(it)) if has_prev else (None, None)
        i = pl.program_id(0)

        @pl.when(i == 0)
        def _():
            dgain_ref[...] = jnp.zeros_like(dgain_ref)

        @pl.when(i % tpb == 0)
        def _():
            dsc_ref[...] = jnp.zeros_like(dsc_ref)
            dsh_ref[...] = jnp.zeros_like(dsh_ref)
            if has_prev:
                dgp_ref[...] = jnp.zeros_like(dgp_ref)

        hv = h_ref[...]
        r = lax.rsqrt(jnp.mean(hv * hv, axis=-1, keepdims=True) + EPS)
        y = hv * r
        gain_v = gain_ref[...]
        g = dhn_ref[...].astype(F32)
        dsh_ref[0] += _rowsum(g)
        dsc_ref[0] += _rowsum(g * (y * gain_v))
        drn = g * (1.0 + sc_ref[0])
        dgain_ref[...] += _rowsum(drn * y)
        if need_dh:
            dy = drn * gain_v
            dh = r * (dy - y * jnp.mean(dy * y, axis=-1, keepdims=True))
            if has_out:
                dh = dh + dho_ref[...]
            dh_ref[...] = dh
            if has_prev:
                dyp_ref[...] = (dh * gp_ref[0]).astype(BF16)
                dgp_ref[0] += _rowsum(dh * yp_ref[...])

    row = pl.BlockSpec((tt, D), lambda i: (i, 0))
    row_off = pl.BlockSpec((tt, D), lambda i: (i + off, 0))
    per_b = pl.BlockSpec((1, 1, D), lambda i: (i // tpb, 0, 0))
    vec = pl.BlockSpec((1, D), lambda i: (0, 0))
    in_specs = [row, row_off, vec, per_b]
    args = [h_in, dhn, gain, scale]
    if has_out:
        in_specs.append(row)
        args.append(dh_out)
    if has_prev:
        in_specs += [row, per_b]
        args += [y_prev, gate_prev]
    out_specs, out_shape, names = [], [], []
    if need_dh:
        out_specs.append(row)
        out_shape.append(jax.ShapeDtypeStruct((n, D), F32))
        names.append("dh")
    for nm in ("dscale", "dshift"):
        out_specs.append(per_b)
        out_shape.append(jax.ShapeDtypeStruct((nb, 1, D), F32))
        names.append(nm)
    out_specs.append(vec)
    out_shape.append(jax.ShapeDtypeStruct((1, D), F32))
    names.append("dgain")
    if has_prev:
        out_specs += [row, per_b]
        out_shape += [jax.ShapeDtypeStruct((n, D), BF16), jax.ShapeDtypeStruct((nb, 1, D), F32)]
        names += ["dy_prev", "dgate_prev"]
    outs = pl.pallas_call(
        body, name=name, grid=(n // tt,), in_specs=in_specs, out_specs=out_specs, out_shape=out_shape,
        compiler_params=_params(("arbitrary",)),
    )(*args)
    return dict(zip(names, outs))


def _row_dn1(x):
    t = lax.broadcasted_iota(jnp.int32, x.shape, 0)
    return jnp.where(t % GRID_W == 0, 0.0, pltpu.roll(x, 1, 0))


def _row_up1(x):
    t = lax.broadcasted_iota(jnp.int32, x.shape, 0)
    return jnp.where(t % GRID_W == GRID_W - 1, 0.0, pltpu.roll(x, x.shape[0] - 1, 0))


def _silu(x):
    return x * _sigmoid(x)


def _dsilu(x):
    s = _sigmoid(x)
    return s * (1.0 + x * (1.0 - s))


def _row_ds(i):
    start = i * GRID_W
    return pl.ds(start if isinstance(start, int) else pl.multiple_of(start, GRID_W), GRID_W)


def _grid_row(ref, i, first, last):
    def rows(k):
        return ref[_row_ds(k), :].astype(F32)

    cur = rows(i)
    return (jnp.zeros_like(cur) if first else rows(i - 1)), cur, (jnp.zeros_like(cur) if last else rows(i + 1))


def _over_grid_rows(n_rows, step, carry):
    carry = step(0, carry, True, n_rows == 1)
    if n_rows > 2:
        carry = lax.fori_loop(1, n_rows - 1, lambda i, c: step(i, c, False, False), carry)
    if n_rows > 1:
        carry = step(n_rows - 1, carry, False, True)
    return carry


def _fold8(p):
    return p.reshape(GRID_W // 8, 8, p.shape[1]).sum(axis=0)


def _ffn_up_mid_fwd(hn, wg, off, cw, cb, nb, t, name):
    tcol = 256
    ncol = HID // tcol
    rows_sh = 2 * HID // N_CHIPS

    def conv(x, w_ref):
        zeros = jnp.zeros((GRID_W, x.shape[1]), x.dtype)
        down = jnp.concatenate([zeros, x[: x.shape[0] - GRID_W]], axis=0)
        up = jnp.concatenate([x[GRID_W:], zeros], axis=0)
        return down * w_ref[0:1, :] + x * w_ref[1:2, :] + up * w_ref[2:3, :]

    def body(h_ref, wa_ref, wg_ref, cwa_ref, cwg_ref, cba_ref, cbg_ref, u_ref, z_ref):
        hv = h_ref[...]
        ua = _dot(hv, wa_ref[0], _NT)
        ug = _dot(hv, wg_ref[0], _NT)
        u_ref[0] = ua.astype(BF16)
        u_ref[1] = ug.astype(BF16)
        a = conv(ua, cwa_ref) + cba_ref[...]
        gt = conv(ug, cwg_ref) + cbg_ref[...]
        z_ref[...] = (a * _silu(gt)).astype(BF16)

    def w_spec(part):
        def idx(b, j):
            n = part * HID + j * tcol
            return (n // rows_sh, (off + n % rows_sh) // tcol, 0)
        return pl.BlockSpec((1, tcol, D), idx)

    chan = lambda rows, part: pl.BlockSpec((rows, tcol), lambda b, j: (0, part * ncol + j))
    return pl.pallas_call(
        body, name=name, grid=(nb, ncol),
        in_specs=[pl.BlockSpec((t, D), lambda b, j: (b, 0)), w_spec(0), w_spec(1),
                  chan(3, 0), chan(3, 1), chan(1, 0), chan(1, 1)],
        out_specs=[pl.BlockSpec((2, t, tcol), lambda b, j: (0, b, j)), pl.BlockSpec((t, tcol), lambda b, j: (b, j))],
        out_shape=[jax.ShapeDtypeStruct((2, nb * t, HID), BF16), jax.ShapeDtypeStruct((nb * t, HID), BF16)],
        compiler_params=_params(("parallel", "parallel")),
    )(hn, wg, wg, cw, cw, cb, cb)


def _ffn_mid_bwd(u0, cw, cb, dz, nb, t, name):
    nc = HID // 128
    n_rows = t // GRID_W

    def body(ua3_ref, ug3_ref, wa_ref, wg_ref, ba_ref, bg_ref, dz_ref, du_ref, dw_ref, db_ref, dua_ref, dug_ref):
        ua_ref, ug_ref = ua3_ref.at[0], ug3_ref.at[0]
        b = pl.program_id(1)

        @pl.when(b == 0)
        def _():
            dw_ref[...] = jnp.zeros_like(dw_ref)
            db_ref[...] = jnp.zeros_like(db_ref)

        wa = [wa_ref[k:k + 1, :] for k in range(3)]
        wg = [wg_ref[k:k + 1, :] for k in range(3)]
        ba, bg = ba_ref[...], bg_ref[...]

        def pass1(i, acc, first, last):
            here = _row_ds(i)
            ap, ac, an = _grid_row(ua_ref, i, first, last)
            gp, gc, gn = _grid_row(ug_ref, i, first, last)
            a = ap * wa[0] + ac * wa[1] + an * wa[2] + ba
            gt = gp * wg[0] + gc * wg[1] + gn * wg[2] + bg
            dzv = dz_ref[here, :].astype(F32)
            s = _sigmoid(gt)
            silu = gt * s
            da = dzv * silu
            dg = (dzv * a) * (s + silu * (1.0 - s))
            dua_ref[here, :] = da
            dug_ref[here, :] = dg
            terms = (da, da * ap, da * ac, da * an, dg, dg * gp, dg * gc, dg * gn)
            return tuple(r + _fold8(p) for r, p in zip(acc, terms))

        zero = jnp.zeros((8, 128), F32)
        acc = _over_grid_rows(n_rows, pass1, (zero,) * 8)
        for part in range(2):
            db_ref[part] += _rowsum(acc[4 * part])
            for k in range(3):
                dw_ref[part, k:k + 1, :] += _rowsum(acc[4 * part + 1 + k])

        def pass2(i, carry, first, last):
            for part, (ref, w) in enumerate(((dua_ref, wa), (dug_ref, wg))):
                dp_, dc_, dn_ = _grid_row(ref, i, first, last)
                du_ref[part, _row_ds(i), :] = (dn_ * w[0] + dc_ * w[1] + dp_ * w[2]).astype(BF16)
            return carry

        _over_grid_rows(n_rows, pass2, 0)

    col = lambda rows, part: pl.BlockSpec((rows, 128), lambda j, b: (0, part * nc + j))
    part_of_u = lambda part: pl.BlockSpec((1, t, 128), lambda j, b: (part, b, j))
    return pl.pallas_call(
        body, name=name, grid=(nc, nb),
        in_specs=[part_of_u(0), part_of_u(1), col(3, 0), col(3, 1), col(1, 0), col(1, 1),
                  pl.BlockSpec((t, 128), lambda j, b: (b, j))],
        out_specs=[pl.BlockSpec((2, t, 128), lambda j, b: (0, b, j)), pl.BlockSpec((2, 3, 128), lambda j, b: (0, 0, j)),
                   pl.BlockSpec((2, 1, 128), lambda j, b: (0, 0, j))],
        out_shape=[jax.ShapeDtypeStruct((2, nb * t, HID), BF16), jax.ShapeDtypeStruct((2, 3, HID), F32),
                   jax.ShapeDtypeStruct((2, 1, HID), F32)],
        scratch_shapes=[pltpu.VMEM((t, 128), F32), pltpu.VMEM((t, 128), F32)],
        compiler_params=_params(("parallel", "arbitrary")),
    )(u0, u0, cw, cw, cb, cb, dz)


def _sc_in_mid_fwd(hn, wg, off, cw, nb, t):
    tcol = 256
    ncol = D // tcol
    rows_sh = 3 * D // N_CHIPS

    def body(h_ref, wb_ref, wc_ref, wv_ref, cw_ref, p_ref, y_ref):
        hv = h_ref[...]
        bg = _dot(hv, wb_ref[0], _NT)
        cg = _dot(hv, wc_ref[0], _NT)
        v = _dot(hv, wv_ref[0], _NT)
        p_ref[0] = bg.astype(BF16)
        p_ref[1] = cg.astype(BF16)
        p_ref[2] = v.astype(BF16)
        cv = cg * v
        cc = _row_dn1(cv) * cw_ref[0:1, :] + cv * cw_ref[1:2, :] + _row_up1(cv) * cw_ref[2:3, :]
        y_ref[...] = (bg * cc).astype(BF16)

    def w_spec(part):
        def idx(b, j):
            n = part * D + j * tcol
            return (n // rows_sh, (off + n % rows_sh) // tcol, 0)
        return pl.BlockSpec((1, tcol, D), idx)

    return pl.pallas_call(
        body, name="sc_in_mid", grid=(nb, ncol),
        in_specs=[pl.BlockSpec((t, D), lambda b, j: (b, 0)), w_spec(0), w_spec(1), w_spec(2),
                  pl.BlockSpec((3, tcol), lambda b, j: (0, j))],
        out_specs=[pl.BlockSpec((3, t, tcol), lambda b, j: (0, b, j)), pl.BlockSpec((t, tcol), lambda b, j: (b, j))],
        out_shape=[jax.ShapeDtypeStruct((3, nb * t, D), BF16), jax.ShapeDtypeStruct((nb * t, D), BF16)],
        compiler_params=_params(("parallel", "parallel")),
    )(hn, wg, wg, wg, cw)


def _sc_mid_bwd(p, cw, dyb, nb, t):
    nc = D // 128

    def body(bg3_ref, cg3_ref, v3_ref, w_ref, dy_ref, dp_ref, dw_ref):
        bg_ref, cg_ref, v_ref = bg3_ref.at[0], cg3_ref.at[0], v3_ref.at[0]
        b = pl.program_id(1)

        @pl.when(b == 0)
        def _():
            dw_ref[...] = jnp.zeros_like(dw_ref)

        w0, w1, w2 = w_ref[0:1, :], w_ref[1:2, :], w_ref[2:3, :]
        cg, v = cg_ref[...].astype(F32), v_ref[...].astype(F32)
        cv = cg * v
        cvd = _row_dn1(cv)
        cvu = _row_up1(cv)
        cc = cvd * w0 + cv * w1 + cvu * w2
        dy = dy_ref[...].astype(F32)
        dcc = dy * bg_ref[...].astype(F32)
        dw_ref[0:1, :] += _rowsum(dcc * cvd)
        dw_ref[1:2, :] += _rowsum(dcc * cv)
        dw_ref[2:3, :] += _rowsum(dcc * cvu)
        dcv = _row_up1(dcc) * w0 + dcc * w1 + _row_dn1(dcc) * w2
        dp_ref[0] = (dy * cc).astype(BF16)
        dp_ref[1] = (dcv * v).astype(BF16)
        dp_ref[2] = (dcv * cg).astype(BF16)

    part = lambda k: pl.BlockSpec((1, t, 128), lambda j, b: (k, b, j))
    return pl.pallas_call(
        body, name="sc_mid_bwd", grid=(nc, nb),
        in_specs=[part(0), part(1), part(2), pl.BlockSpec((3, 128), lambda j, b: (0, j)),
                  pl.BlockSpec((t, 128), lambda j, b: (b, j))],
        out_specs=[pl.BlockSpec((3, t, 128), lambda j, b: (0, b, j)), pl.BlockSpec((3, 128), lambda j, b: (0, j))],
        out_shape=[jax.ShapeDtypeStruct((3, nb * t, D), BF16), jax.ShapeDtypeStruct((3, D), F32)],
        compiler_params=_params(("parallel", "arbitrary")),
    )(p, p, p, cw, dyb)


def _gla_in_proj(hn_all, w_gin, w2, b2):
    n = hn_all.shape[0]
    tm = _tile(n, 768, 128)

    def body(h_ref, w_ref, w2_ref, b2_ref, p_ref, la_ref):
        p = _dot(h_ref[...], w_ref[...], _NT)
        p_ref[...] = p
        z = _dot(p[:, 2 * KEY + 2 * D:], w2_ref[...]) + b2_ref[...]
        la_ref[...] = (jnp.minimum(z, 0.0) - jnp.log(1.0 + jnp.exp(-jnp.abs(z)))) * (1.0 / TAU)

    return pl.pallas_call(
        body, name="gla_in_proj", grid=(n // tm,),
        in_specs=[pl.BlockSpec((tm, D), lambda i: (i, 0)), pl.BlockSpec((GLA_IN_PAD, D), lambda i: (0, 0)),
                  pl.BlockSpec((128, 2 * KEY), lambda i: (0, 0)), pl.BlockSpec((1, 2 * KEY), lambda i: (0, 0))],
        out_specs=[pl.BlockSpec((tm, GLA_IN_PAD), lambda i: (i, 0)), pl.BlockSpec((tm, 2 * KEY), lambda i: (i, 0))],
        out_shape=[jax.ShapeDtypeStruct((n, GLA_IN_PAD), F32), jax.ShapeDtypeStruct((n, 2 * KEY), F32)],
        compiler_params=_params(("parallel",)),
    )(hn_all, w_gin, w2, b2)


def _gla_blocks(nb, nm, ncx):
    def main_idx(d, i):
        return jnp.clip(jnp.where(d == 0, i - ncx, nm - 1 - (i - ncx)), 0, nm - 1)

    def rowblk(d, b, i):
        cidx = jnp.where(d == 0, i, ncx - 1 - i)
        return jnp.where(i < ncx, nb * nm + b * ncx + cidx, b * nm + main_idx(d, i))

    def mainblk(d, b, i):
        return b * nm + main_idx(d, i)

    return rowblk, mainblk


def _gla_mask(d):
    row = lax.broadcasted_iota(jnp.int32, (CH, CH), 0)
    col = lax.broadcasted_iota(jnp.int32, (CH, CH), 1)
    diff = jnp.where(d == 0, row - col, col - row)
    mask = diff >= 0
    return mask, jnp.where(mask, 1.0, 0.0).astype(BF16), jnp.where(diff <= 0, 1.0, 0.0).astype(BF16)


def _tri_sum(m01, x):
    w = x.shape[1]
    hi = x.astype(BF16)
    r1 = x - hi.astype(F32)
    mid = r1.astype(BF16)
    lo = (r1 - mid.astype(F32)).astype(BF16)
    s = lax.dot_general(m01, jnp.concatenate([hi, mid, lo], axis=1), _NN, preferred_element_type=F32)
    return s[:, :w] + s[:, w:2 * w] + s[:, 2 * w:]


def _gla_chunk(q, k, g, bc):
    bl = _rowsum(g)
    eq = jnp.exp(bc)
    ek = jnp.exp(-bc)
    ed = jnp.exp(bl - bc)
    return bl, eq, ek, ed, q * Q_SCALE * eq, k * ek, k * ed


def _gla_scan_fwd(p_all, la_all, nb, t, tc):
    nm, ncx = t // CH, tc // CH
    nst = nm + ncx
    rowblk, mainblk = _gla_blocks(nb, nm, ncx)

    def body(*refs):
        ins, (o_refs, ss_refs, st_ref) = refs[:8], (refs[8:10], refs[10:12], refs[12])
        i = pl.program_id(1)

        @pl.when(i == 0)
        def _():
            st_ref[...] = jnp.zeros_like(st_ref)

        loaded = [r[...] for r in ins]
        states = [st_ref[j] for j in range(2 * HEADS)]
        outs, new_states = [[], []], []
        for d in range(2):
            q_all, k_all, v_all, g_all = loaded[4 * d:4 * d + 4]
            mask, m01, _ = _gla_mask(d)
            bc_all = _tri_sum(m01, g_all)
            for h in range(HEADS):
                ksl = slice(h * DK, (h + 1) * DK)
                v = v_all[:, h * DV:(h + 1) * DV]
                st = states[d * HEADS + h]
                bl, _, _, _, qs, ks, kd = _gla_chunk(q_all[:, ksl], k_all[:, ksl], g_all[:, ksl], bc_all[:, ksl])
                att = jnp.where(mask, _dot(qs, ks, _NT), 0.0)
                outs[d].append(_dot(qs, st, _NT) + _dot(att, v))
                new_states.append(st * jnp.exp(bl) + _dot(v, kd, _TN))
        for d in range(2):
            o_refs[d][...] = jnp.concatenate(outs[d], axis=1)
            for h in range(HEADS):
                ss_refs[d][0, 0, h] = states[d * HEADS + h]
                st_ref[d * HEADS + h] = new_states[d * HEADS + h]

    def in_specs(d):
        return [pl.BlockSpec((CH, KEY), lambda b, i: (rowblk(d, b, i), 0)),
                pl.BlockSpec((CH, KEY), lambda b, i: (rowblk(d, b, i), 1)),
                pl.BlockSpec((CH, D), lambda b, i: (rowblk(d, b, i), 1)),
                pl.BlockSpec((CH, KEY), lambda b, i: (rowblk(d, b, i), d))]

    outs = pl.pallas_call(
        body, name="gla_scan_fwd", grid=(nb, nst),
        in_specs=in_specs(0) + in_specs(1),
        out_specs=[pl.BlockSpec((CH, D), lambda b, i: (mainblk(0, b, i), 0)),
                   pl.BlockSpec((CH, D), lambda b, i: (mainblk(1, b, i), 0)),
                   pl.BlockSpec((1, 1, HEADS, DV, DK), lambda b, i: (b, i, 0, 0, 0)),
                   pl.BlockSpec((1, 1, HEADS, DV, DK), lambda b, i: (b, i, 0, 0, 0))],
        out_shape=[jax.ShapeDtypeStruct((nb * t, D), F32)] * 2
        + [jax.ShapeDtypeStruct((nb, nst, HEADS, DV, DK), F32)] * 2,
        scratch_shapes=[pltpu.VMEM((2 * HEADS, DV, DK), F32)],
        compiler_params=_params(("parallel", "arbitrary")),
    )(*([p_all, p_all, p_all, la_all] * 2))
    return outs[:2], outs[2:]


def _gla_scan_bwd(p_all, la_all, do, ss, nb, t, tc, after):
    nm, ncx = t // CH, tc // CH
    nst = nm + ncx
    ntot = nb * (t + tc)
    rowblk, mainblk = _gla_blocks(nb, nm, ncx)

    def body(*refs):
        ins, outs, dst_ref = refs[:12], refs[13:21], refs[21]
        ip = pl.program_id(1)
        i = nst - 1 - ip

        @pl.when(ip == 0)
        def _():
            dst_ref[...] = jnp.zeros_like(dst_ref)

        live = jnp.where(i >= ncx, 1.0, 0.0)
        loaded = [[r[...] for r in ins[6 * d:6 * d + 5]] for d in range(2)]
        states = [ins[6 * d + 5][0, 0, h] for d in range(2) for h in range(HEADS)]
        dstates = [dst_ref[j] for j in range(2 * HEADS)]
        results, new_dstates = [], []
        for d in range(2):
            q_all, k_all, v_all, g_all, do_all = loaded[d]
            do_all = do_all * live
            mask, m01, m01_t = _gla_mask(d)
            bc_all = _tri_sum(m01, g_all)
            dqs_l, dks_l, dvs_l, dbs_l, dbls_l = [], [], [], [], []
            for h in range(HEADS):
                ksl = slice(h * DK, (h + 1) * DK)
                vsl = slice(h * DV, (h + 1) * DV)
                bl, eq, ek, ed, qs, ks, kd = _gla_chunk(q_all[:, ksl], k_all[:, ksl], g_all[:, ksl], bc_all[:, ksl])
                st, dst, v, dov = states[d * HEADS + h], dstates[d * HEADS + h], v_all[:, vsl], do_all[:, vsl]
                att = jnp.where(mask, _dot(qs, ks, _NT), 0.0)
                datt = jnp.where(mask, _dot(dov, v, _NT), 0.0)
                dqs = _dot(dov, st) + _dot(datt, ks)
                dks = _dot(datt, qs, _TN)
                dvs_l.append(_dot(att, dov, _TN) + _dot(kd, dst, _NT))
                dkd = _dot(v, dst)
                e = jnp.exp(bl)
                dbls_l.append(e * _rowsum(st * dst) + _rowsum(dkd * kd))
                new_dstates.append(_dot(dov, qs, _TN) + dst * e)
                dqs_l.append(dqs * eq * Q_SCALE)
                dks_l.append(dks * ek + dkd * ed)
                dbs_l.append(dqs * qs - dks * ks - dkd * kd)
            results.append((jnp.concatenate(dqs_l, axis=1), jnp.concatenate(dks_l, axis=1),
                            jnp.concatenate(dvs_l, axis=1),
                            _tri_sum(m01_t, jnp.concatenate(dbs_l, axis=1)) + jnp.concatenate(dbls_l, axis=1)))
        for d in range(2):
            for k in range(4):
                outs[4 * d + k][...] = results[d][k]
        for j in range(2 * HEADS):
            dst_ref[j] = new_dstates[j]

    def in_specs(d):
        return [pl.BlockSpec((CH, KEY), lambda b, ip: (rowblk(d, b, nst - 1 - ip), 0)),
                pl.BlockSpec((CH, KEY), lambda b, ip: (rowblk(d, b, nst - 1 - ip), 1)),
                pl.BlockSpec((CH, D), lambda b, ip: (rowblk(d, b, nst - 1 - ip), 1)),
                pl.BlockSpec((CH, KEY), lambda b, ip: (rowblk(d, b, nst - 1 - ip), d)),
                pl.BlockSpec((CH, D), lambda b, ip: (mainblk(d, b, nst - 1 - ip), 0)),
                pl.BlockSpec((1, 1, HEADS, DV, DK), lambda b, ip: (b, nst - 1 - ip, 0, 0, 0))]

    def out_specs(d):
        row = lambda width: pl.BlockSpec((CH, width), lambda b, ip: (rowblk(d, b, nst - 1 - ip), 0))
        return [row(KEY), row(KEY), row(D), row(KEY)]

    shapes = [jax.ShapeDtypeStruct((ntot, KEY), F32), jax.ShapeDtypeStruct((ntot, KEY), F32),
              jax.ShapeDtypeStruct((ntot, D), F32), jax.ShapeDtypeStruct((ntot, KEY), F32)]
    outs = pl.pallas_call(
        body, name="gla_scan_bwd", grid=(nb, nst),
        in_specs=in_specs(0) + in_specs(1) + [pl.BlockSpec(memory_space=pl.ANY)],
        out_specs=out_specs(0) + out_specs(1),
        out_shape=shapes * 2,
        scratch_shapes=[pltpu.VMEM((2 * HEADS, DV, DK), F32)],
        compiler_params=_params(("parallel", "arbitrary")),
    )(p_all, p_all, p_all, la_all, do, ss[0], p_all, p_all, p_all, la_all, do, ss[1], after)
    return [[outs[k], outs[4 + k]] for k in range(4)]


def _gla_post_fwd(o2, p_all, head_gain, n):
    tt = _tile(n, 256)

    def body(of_ref, ob_ref, g_ref, hg_ref, y_ref):
        o = of_ref[...] + ob_ref[...]
        gv = g_ref[...]
        hg = hg_ref[...]
        for h in range(HEADS):
            oh = o[:, h * DV:(h + 1) * DV]
            r = lax.rsqrt(jnp.mean(oh * oh, axis=-1, keepdims=True) + EPS)
            y_ref[:, h * DV:(h + 1) * DV] = ((oh * r) * hg * _silu(gv[:, h * DV:(h + 1) * DV])).astype(BF16)

    row = pl.BlockSpec((tt, D), lambda i: (i, 0))
    return pl.pallas_call(
        body, name="gla_post_fwd", grid=(n // tt,),
        in_specs=[row, row, pl.BlockSpec((tt, D), lambda i: (i, 2)), pl.BlockSpec((1, DV), lambda i: (0, 0))],
        out_specs=row,
        out_shape=jax.ShapeDtypeStruct((n, D), BF16),
        compiler_params=_params(("parallel",)),
    )(o2[0], o2[1], p_all, head_gain)


def _gla_out_dx_post_bwd(dy, wg, off, o2, p_all, head_gain, n):
    tt = _tile(n, 256)

    def body(dy_ref, w_ref, of_ref, ob_ref, g_ref, hg_ref, do_ref, dg_ref, dhg_ref):
        i = pl.program_id(0)

        @pl.when(i == 0)
        def _():
            dhg_ref[...] = jnp.zeros_like(dhg_ref)

        dyv = dy_ref[...]
        o = of_ref[...] + ob_ref[...]
        gv = g_ref[...]
        hg = hg_ref[...]
        acc = jnp.zeros((1, DV), F32)
        for h in range(HEADS):
            sl = slice(h * DV, (h + 1) * DV)
            dyh = _dot(dyv, w_ref[h], _NT)
            oh = o[:, sl]
            r = lax.rsqrt(jnp.mean(oh * oh, axis=-1, keepdims=True) + EPS)
            on = oh * r
            gh = gv[:, sl]
            dg_ref[:, sl] = dyh * (on * hg) * _dsilu(gh)
            dog = dyh * _silu(gh)
            acc = acc + _rowsum(dog * on)
            don = dog * hg
            do_ref[:, sl] = r * (don - on * jnp.mean(don * on, axis=-1, keepdims=True))
        dhg_ref[...] += acc

    row = pl.BlockSpec((tt, D), lambda i: (i, 0))
    return pl.pallas_call(
        body, name="gla_out_dx_post_bwd", grid=(n // tt,),
        in_specs=[row, pl.BlockSpec((N_CHIPS, DV, D), lambda i: (0, off // DV, 0)), row, row,
                  pl.BlockSpec((tt, D), lambda i: (i, 2)), pl.BlockSpec((1, DV), lambda i: (0, 0))],
        out_specs=[row, row, pl.BlockSpec((1, DV), lambda i: (0, 0))],
        out_shape=[jax.ShapeDtypeStruct((n, D), F32), jax.ShapeDtypeStruct((n, D), F32),
                   jax.ShapeDtypeStruct((1, DV), F32)],
        compiler_params=_params(("arbitrary",)),
    )(dy, wg, o2[0], o2[1], p_all, head_gain)


def _gla_in_dx_mod(dp, w_gin, xf, cf, dh_out, gain, scale, scale_ctx, t, tc):
    n, nc = xf.shape[0], cf.shape[0]
    nb = n // t
    tm = _tile(tc, 256, 16)
    nmain, tpb = n // tm, t // tm

    def body(dp_ref, w_ref, x_ref, c_ref, dho_ref, gain_ref, sc_ref, scc_ref,
             dh_ref, dsc_ref, dsh_ref, dshc_ref, dscc_ref, dgain_ref):
        i = pl.program_id(0)
        is_main = i < nmain

        @pl.when(i == 0)
        def _():
            dgain_ref[...] = jnp.zeros_like(dgain_ref)
            dshc_ref[...] = jnp.zeros_like(dshc_ref)
            dscc_ref[...] = jnp.zeros_like(dscc_ref)

        @pl.when(is_main & (i % tpb == 0))
        def _():
            dsc_ref[...] = jnp.zeros_like(dsc_ref)
            dsh_ref[...] = jnp.zeros_like(dsh_ref)

        g = _dot(dp_ref[...], w_ref[...])
        hv = jnp.where(is_main, x_ref[...], c_ref[...])
        sc = jnp.where(is_main, sc_ref[0], scc_ref[0])
        r = lax.rsqrt(jnp.mean(hv * hv, axis=-1, keepdims=True) + EPS)
        y = hv * r
        gain_v = gain_ref[...]
        sum_g = _rowsum(g)
        sum_gy = _rowsum(g * (y * gain_v))
        drn = g * (1.0 + sc)
        dgain_ref[...] += _rowsum(drn * y)

        @pl.when(is_main)
        def _():
            dsh_ref[0] += sum_g
            dsc_ref[0] += sum_gy
            dy = drn * gain_v
            dh_ref[...] = r * (dy - y * jnp.mean(dy * y, axis=-1, keepdims=True)) + dho_ref[...]

        @pl.when(jnp.logical_not(is_main))
        def _():
            dshc_ref[...] += sum_g
            dscc_ref[...] += sum_gy

    main_row = pl.BlockSpec((tm, D), lambda i: (jnp.minimum(i, nmain - 1), 0))
    per_b = pl.BlockSpec((1, 1, D), lambda i: (jnp.minimum(i, nmain - 1) // tpb, 0, 0))
    vec = pl.BlockSpec((1, D), lambda i: (0, 0))
    per_b_shape = jax.ShapeDtypeStruct((nb, 1, D), F32)
    vec_shape = jax.ShapeDtypeStruct((1, D), F32)
    return pl.pallas_call(
        body, name="gla_in_dx_mod", grid=((n + nc) // tm,),
        in_specs=[pl.BlockSpec((tm, GLA_IN_PAD), lambda i: (i, 0)), pl.BlockSpec((GLA_IN_PAD, D), lambda i: (0, 0)),
                  main_row, pl.BlockSpec((tm, D), lambda i: (jnp.maximum(i - nmain, 0), 0)), main_row, vec, per_b,
                  pl.BlockSpec((1, 1, D), lambda i: (0, 0, 0))],
        out_specs=[main_row, per_b, per_b, vec, vec, vec],
        out_shape=[jax.ShapeDtypeStruct((n, D), F32), per_b_shape, per_b_shape, vec_shape, vec_shape, vec_shape],
        compiler_params=_params(("arbitrary",)),
    )(dp, w_gin, xf, cf, dh_out, gain, scale, scale_ctx)


def _gla_assemble(p_all, w2, b2, dq, dk, dv, dla, dgate, n):
    ntot = p_all.shape[0]
    tt = _tile(n, 128)
    nmain = n // tt
    assert ntot % tt == 0

    def body(a_ref, w_ref, b_ref, dqf_ref, dqb_ref, dkf_ref, dkb_ref, dvf_ref, dvb_ref, dlf_ref, dlb_ref, dg_ref,
             dp_ref, dw_ref, db_ref):
        i = pl.program_id(0)

        @pl.when(i == 0)
        def _():
            dw_ref[...] = jnp.zeros_like(dw_ref)
            db_ref[...] = jnp.zeros_like(db_ref)

        a = a_ref[...]
        w = w_ref[...]
        z = _dot(a, w) + b_ref[...]
        dla = jnp.concatenate([dlf_ref[...], dlb_ref[...]], axis=1)
        dz = dla * (1.0 / (1.0 + jnp.exp(z))) * (1.0 / TAU)
        dw_ref[...] += _dot(a, dz, _TN)
        db_ref[...] += _rowsum(dz)
        dp_ref[:, 0:KEY] = (dqf_ref[...] + dqb_ref[...]).astype(BF16)
        dp_ref[:, KEY:2 * KEY] = (dkf_ref[...] + dkb_ref[...]).astype(BF16)
        dp_ref[:, 2 * KEY:2 * KEY + D] = (dvf_ref[...] + dvb_ref[...]).astype(BF16)
        dp_ref[:, 2 * KEY + D:2 * KEY + 2 * D] = (dg_ref[...] * jnp.where(i < nmain, 1.0, 0.0)).astype(BF16)
        dp_ref[:, 2 * KEY + 2 * D:GLA_IN_PAD] = _dot(dz, w, _NT).astype(BF16)

    row = lambda width: pl.BlockSpec((tt, width), lambda i: (i, 0))
    return pl.pallas_call(
        body, name="gla_assemble", grid=(ntot // tt,),
        in_specs=[pl.BlockSpec((tt, 128), lambda i: (i, (2 * KEY + 2 * D) // 128)),
                  pl.BlockSpec((128, 2 * KEY), lambda i: (0, 0)), pl.BlockSpec((1, 2 * KEY), lambda i: (0, 0)),
                  row(KEY), row(KEY), row(KEY), row(KEY), row(D), row(D), row(KEY), row(KEY),
                  pl.BlockSpec((tt, D), lambda i: (jnp.minimum(i, nmain - 1), 0))],
        out_specs=[pl.BlockSpec((tt, GLA_IN_PAD), lambda i: (i, 0)), pl.BlockSpec((128, 2 * KEY), lambda i: (0, 0)),
                   pl.BlockSpec((1, 2 * KEY), lambda i: (0, 0))],
        out_shape=[jax.ShapeDtypeStruct((ntot, GLA_IN_PAD), BF16), jax.ShapeDtypeStruct((128, 2 * KEY), F32),
                   jax.ShapeDtypeStruct((1, 2 * KEY), F32)],
        compiler_params=_params(("arbitrary",)),
    )(p_all, w2, b2, dq[0], dq[1], dk[0], dk[1], dv[0], dv[1], dla[0], dla[1], dgate)


ADA_ROWS = 24
ADA_SH = N_MOD * D // N_CHIPS


def _ada_fwd(cvec, ada_w, ada_b_sh):
    def body(c_ref, w_ref, b_ref, o_ref):
        o_ref[0] = _dot(_silu(c_ref[...]), w_ref[0]) + b_ref[0]

    return pl.pallas_call(
        body, name="ada_fwd", grid=(2,),
        in_specs=[pl.BlockSpec((ADA_ROWS, D), lambda l: (0, 0)), pl.BlockSpec((1, D, ADA_SH), lambda l: (l, 0, 0)),
                  pl.BlockSpec((1, 1, ADA_SH), lambda l: (l, 0, 0))],
        out_specs=pl.BlockSpec((1, ADA_ROWS, ADA_SH), lambda l: (l, 0, 0)),
        out_shape=jax.ShapeDtypeStruct((2, ADA_ROWS, ADA_SH), F32),
        compiler_params=_params(("parallel",)),
    )(cvec, ada_w, ada_b_sh)


def _ada_bwd(cvec, ada_w, dmod_sh):
    def body(c_ref, w_ref, dm_ref, gw_ref, dc_ref):
        dm = dm_ref[0]
        gw_ref[0] = _dot(_silu(c_ref[...]), dm, _TN)
        dc_ref[0] = _dot(dm, w_ref[0], _NT)

    return pl.pallas_call(
        body, name="ada_bwd", grid=(2,),
        in_specs=[pl.BlockSpec((ADA_ROWS, D), lambda l: (0, 0)), pl.BlockSpec((1, D, ADA_SH), lambda l: (l, 0, 0)),
                  pl.BlockSpec((1, ADA_ROWS, ADA_SH), lambda l: (l, 0, 0))],
        out_specs=[pl.BlockSpec((1, D, ADA_SH), lambda l: (l, 0, 0)), pl.BlockSpec((1, ADA_ROWS, D), lambda l: (l, 0, 0))],
        out_shape=[jax.ShapeDtypeStruct((2, D, ADA_SH), F32), jax.ShapeDtypeStruct((2, ADA_ROWS, D), F32)],
        compiler_params=_params(("parallel",)),
    )(cvec, ada_w, dmod_sh)


def _sum_slots(x, name):
    s, r, _ = x.shape

    def body(x_ref, o_ref):
        acc = x_ref[0]
        for k in range(1, s):
            acc = acc + x_ref[k]
        o_ref[...] = acc

    return pl.pallas_call(
        body, name=name, out_shape=jax.ShapeDtypeStruct((r, 128), F32),
        in_specs=[pl.BlockSpec(memory_space=pltpu.VMEM)], out_specs=pl.BlockSpec(memory_space=pltpu.VMEM),
    )(x)


def _cctx_grad(dscc_parts, c_ctx):
    def body(p_ref, c_ref, o_ref):
        acc = p_ref[0]
        for k in range(1, N_CHIPS):
            acc = acc + p_ref[k]
        o_ref[...] = acc * _dsilu(c_ref[...])

    return pl.pallas_call(
        body, name="cctx_grad", out_shape=jax.ShapeDtypeStruct((8, 128), F32),
        in_specs=[pl.BlockSpec(memory_space=pltpu.VMEM)] * 2, out_specs=pl.BlockSpec(memory_space=pltpu.VMEM),
    )(dscc_parts, c_ctx)


def _adamw(w, g, m, v, name, after):
    nl, r, cdim = w.shape
    tr = _tile(r, 256)
    c1 = 1.0 - ADAM_B1 ** ADAM_STEP
    c2 = 1.0 - ADAM_B2 ** ADAM_STEP

    def body(w_ref, g_ref, m_ref, v_ref, after_ref, d_ref, mo_ref, vo_ref):
        gv = g_ref[...]
        mn = ADAM_B1 * m_ref[...] + (1.0 - ADAM_B1) * gv
        vn = ADAM_B2 * v_ref[...] + (1.0 - ADAM_B2) * (gv * gv)
        mo_ref[...] = mn
        vo_ref[...] = vn
        d_ref[...] = -ADAM_LR * ((mn / c1) / (jnp.sqrt(vn / c2) + ADAM_EPS) + ADAM_WD * w_ref[...])

    spec = pl.BlockSpec((1, tr, cdim), lambda l, i: (l, i, 0))
    sds = jax.ShapeDtypeStruct((nl, r, cdim), F32)
    return pl.pallas_call(
        body, name=name, grid=(nl, r // tr), in_specs=[spec] * 4 + [pl.BlockSpec(memory_space=pl.ANY)],
        out_specs=[spec] * 3, out_shape=[sds] * 3, compiler_params=_params(("parallel", "parallel")),
    )(w, g, m, v, after)


def _place():
    x, y, c = lax.axis_index("x"), lax.axis_index("y"), lax.axis_index("c")
    return x, y, c


def _allgather_small(blk, name):
    m_per, n = blk.shape

    def body(x_ref, out_ref, send_sems, recv_sems, local_sem):
        x, y, c = _place()
        me, sibling = (x, y, c), (x, y, 1 - c)
        chips = [(1 - x, y), (x, 1 - y), (1 - x, 1 - y)]

        def rows(px, py, pc):
            return out_ref.at[pl.ds((4 * px + 2 * py + pc) * m_per, m_per), :]

        def copy(k, block, to, src=None):
            return pltpu.make_async_remote_copy(
                src_ref=rows(*block) if src is None else src, dst_ref=rows(*block),
                send_sem=send_sems.at[k], recv_sem=recv_sems.at[k], device_id=to, device_id_type=MESH)

        mine = pltpu.make_async_copy(x_ref, rows(*me), local_sem)
        mine.start()
        first = [copy(0, me, sibling, src=x_ref)]
        first += [copy(1 + j, me, (*chip, c), src=x_ref) for j, chip in enumerate(chips)]
        for cp in first:
            cp.start()
        passed = [copy(4 + j, (*chip, c), sibling) for j, chip in enumerate(chips)]
        for j, chip in enumerate(chips):
            copy(1 + j, (*chip, c), me).wait_recv()
            passed[j].start()
        copy(0, sibling, me).wait_recv()
        for j, chip in enumerate(chips):
            copy(4 + j, (*chip, 1 - c), me).wait_recv()
        for cp in first + passed:
            cp.wait_send()
        mine.wait()

    return pl.pallas_call(
        body, name=name,
        out_shape=jax.ShapeDtypeStruct((N_DEV * m_per, n), blk.dtype),
        in_specs=[pl.BlockSpec(memory_space=pltpu.VMEM)],
        out_specs=pl.BlockSpec(memory_space=pltpu.VMEM),
        scratch_shapes=[pltpu.SemaphoreType.DMA((7,)), pltpu.SemaphoreType.DMA((7,)), pltpu.SemaphoreType.DMA],
    )(blk)


def _other_chips(x, y):
    return [(1 - x, y), (x, 1 - y), (1 - x, 1 - y)]


_HBM_SPEC = pl.BlockSpec(memory_space=pltpu.HBM)
_SEM_SPEC = pl.BlockSpec(memory_space=pltpu.SEMAPHORE)
_SPLIT_PARAMS = pltpu.CompilerParams(has_side_effects=pltpu.SideEffectType.DATAFLOW_SIDE_EFFECTING)


def _in_hbm(a):
    return pltpu.with_memory_space_constraint(a, pltpu.HBM)


def _ag_copies(own_ref, land_ref, send_sems, recv_sems):
    x, y, c = _place()
    chip = 2 * x + y
    hr = own_ref.shape[0] // 2

    def half(ch):
        return land_ref.at[ch, pl.ds(c * hr, hr), :]

    def copy(k, src, dst, to):
        return pltpu.make_async_remote_copy(src_ref=src, dst_ref=dst, send_sem=send_sems.at[k],
                                            recv_sem=recv_sems.at[k], device_id=to, device_id_type=MESH)

    sends, expects = [], []
    for j, (ox, oy) in enumerate(_other_chips(x, y)):
        sends.append(copy(j, own_ref.at[pl.ds(c * hr, hr), :], half(chip), (ox, oy, c)))
        expects.append(copy(j, half(2 * ox + oy), half(2 * ox + oy), (ox, oy, c)))
    own_slot = copy(3, own_ref, land_ref.at[chip], (x, y, 1 - c))
    return sends + [own_slot], expects + [own_slot]


def _sc_copies(p_ref, land_ref, send_sems, recv_sems):
    x, y, c = _place()
    chip = 2 * x + y
    sends, expects = [], []
    for j, (ox, oy) in enumerate(_other_chips(x, y)):
        och = 2 * ox + oy
        mk = lambda dst_slot: pltpu.make_async_remote_copy(
            src_ref=p_ref.at[och], dst_ref=land_ref.at[dst_slot], send_sem=send_sems.at[j],
            recv_sem=recv_sems.at[j], device_id=(ox, oy, c), device_id_type=MESH)
        sends.append(mk(chip))
        expects.append(mk(och))
    return sends, expects


def _pe_copies(g_ref, land_ref, send_sems, recv_sems):
    x, y, c = _place()
    hr = g_ref.shape[1] // 2
    cp = pltpu.make_async_remote_copy(
        src_ref=g_ref.at[:, pl.ds((1 - c) * hr, hr), :], dst_ref=land_ref, send_sem=send_sems.at[0],
        recv_sem=recv_sems.at[0], device_id=(x, y, 1 - c), device_id_type=MESH)
    return [cp], [cp]


def _pass_on_copies(unused_ref, land_ref, send_sems, recv_sems):
    x, y, c = _place()
    hr = land_ref.shape[1] // 2
    sends, expects = [], []
    for j, (ox, oy) in enumerate(_other_chips(x, y)):
        def mk(cc, j=j, och=2 * ox + oy):
            ref = land_ref.at[och, pl.ds(cc * hr, hr), :]
            return pltpu.make_async_remote_copy(src_ref=ref, dst_ref=ref, send_sem=send_sems.at[j],
                                                recv_sem=recv_sems.at[j], device_id=(x, y, 1 - c),
                                                device_id_type=MESH)
        sends.append(mk(c))
        expects.append(mk(1 - c))
    return sends, expects


def _pair_gather_copies(unused_ref, land_ref, send_sems, recv_sems):
    x, y, c = _place()
    hr = land_ref.shape[0] // 2

    def mk(cc):
        ref = land_ref.at[pl.ds(cc * hr, hr), :]
        return pltpu.make_async_remote_copy(src_ref=ref, dst_ref=ref, send_sem=send_sems.at[0],
                                            recv_sem=recv_sems.at[0], device_id=(x, y, 1 - c), device_id_type=MESH)
    return [mk(c)], [mk(1 - c)]


def _split_start(src, land, copies, n_copies, after, name):
    def body(src_ref, land_ref, after_ref, send_sems, recv_sems, src_thru, land_thru, token):
        for cp in copies(src_ref, land_ref, send_sems, recv_sems)[0]:
            cp.start()
        token[...] = jnp.zeros_like(token)

    if isinstance(land, tuple):
        land = lax.empty(land, src.dtype)
    land_shape = land.shape
    return pl.pallas_call(
        body, name=name,
        out_shape=(pltpu.SemaphoreType.DMA((n_copies,)), pltpu.SemaphoreType.DMA((n_copies,)),
                   pltpu.HBM(src.shape, src.dtype), pltpu.HBM(land_shape, land.dtype),
                   jax.ShapeDtypeStruct((8, 128), F32)),
        in_specs=(_HBM_SPEC, _HBM_SPEC, pl.BlockSpec(memory_space=pl.ANY)),
        out_specs=(_SEM_SPEC, _SEM_SPEC, _HBM_SPEC, _HBM_SPEC, pl.BlockSpec(memory_space=pltpu.VMEM)),
        input_output_aliases={0: 2, 1: 3}, compiler_params=_SPLIT_PARAMS,
    )(_in_hbm(src), _in_hbm(land), after)


def _split_wait(started, after, copies, name):
    send_sems, recv_sems, src_thru, land_thru, _ = started

    def body(src_ref, land_ref, send_sems, recv_sems, after_ref, src_dead, got_ref):
        sends, expects = copies(src_ref, land_ref, send_sems, recv_sems)
        for cp in sends:
            cp.wait_send()
        for cp in expects:
            cp.wait_recv()

    return pl.pallas_call(
        body, name=name,
        out_shape=(pltpu.HBM(src_thru.shape, src_thru.dtype), pltpu.HBM(land_thru.shape, land_thru.dtype)),
        in_specs=(_HBM_SPEC, _HBM_SPEC, _SEM_SPEC, _SEM_SPEC, pl.BlockSpec(memory_space=pl.ANY)),
        out_specs=(_HBM_SPEC, _HBM_SPEC), input_output_aliases={0: 0, 1: 1}, compiler_params=_SPLIT_PARAMS,
    )(src_thru, land_thru, send_sems, recv_sems, after)


def _ag_pass_on(land, name):
    hr = land.shape[1] // 2

    def body(in_ref, out_ref, send_sems, recv_sems):
        x, y, c = _place()

        def copy(j, ox, oy, cc):
            ref = out_ref.at[2 * ox + oy, pl.ds(cc * hr, hr), :]
            return pltpu.make_async_remote_copy(src_ref=ref, dst_ref=ref, send_sem=send_sems.at[j],
                                                recv_sem=recv_sems.at[j], device_id=(x, y, 1 - c),
                                                device_id_type=MESH)

        others = _other_chips(x, y)
        for j, (ox, oy) in enumerate(others):
            copy(j, ox, oy, c).start()
        for j, (ox, oy) in enumerate(others):
            copy(j, ox, oy, 1 - c).wait_recv()
        for j, (ox, oy) in enumerate(others):
            copy(j, ox, oy, c).wait_send()

    any_spec = pl.BlockSpec(memory_space=pl.ANY)
    return pl.pallas_call(
        body, name=name, out_shape=jax.ShapeDtypeStruct(land.shape, land.dtype),
        in_specs=[any_spec], out_specs=any_spec, input_output_aliases={0: 0},
        scratch_shapes=[pltpu.SemaphoreType.DMA((3,)), pltpu.SemaphoreType.DMA((3,))],
    )(land)


def _rs_pair_exchange(g, after, name):
    r = g.shape[1]
    hr = r // 2

    def body(g_ref, after_ref, got_ref, send_sem, recv_sem):
        x, y, c = _place()
        cp = pltpu.make_async_remote_copy(
            src_ref=g_ref.at[:, pl.ds((1 - c) * hr, hr), :], dst_ref=got_ref, send_sem=send_sem, recv_sem=recv_sem,
            device_id=(x, y, 1 - c), device_id_type=MESH)
        cp.start()
        cp.wait()

    any_spec = pl.BlockSpec(memory_space=pl.ANY)
    return pl.pallas_call(
        body, name=name,
        out_shape=jax.ShapeDtypeStruct((N_CHIPS, hr, D), F32),
        in_specs=[any_spec, any_spec], out_specs=any_spec,
        scratch_shapes=[pltpu.SemaphoreType.DMA, pltpu.SemaphoreType.DMA],
    )(g, after)


def _rs_chip_sum(place, g, got, name):
    r = g.shape[1]
    hr = r // 2
    tr = _tile(hr, 640, 16)
    nt = hr // tr

    def body(pl_ref, g_ref, got_ref, p16_ref, p32_ref):
        s = pl.program_id(1)
        p = g_ref[0] + got_ref[0]
        p16_ref[0] = p.astype(BF16)

        @pl.when(s == pl_ref[1])
        def _():
            p32_ref[...] = p

    return pl.pallas_call(
        body, name=name,
        grid_spec=pltpu.PrefetchScalarGridSpec(
            num_scalar_prefetch=1, grid=(nt, N_CHIPS),
            in_specs=[pl.BlockSpec((1, tr, D), lambda i, s, pr: (s, pr[0] * nt + i, 0)),
                      pl.BlockSpec((1, tr, D), lambda i, s, pr: (s, i, 0))],
            out_specs=[pl.BlockSpec((1, tr, D), lambda i, s, pr: (s, i, 0)),
                       pl.BlockSpec((tr, D), lambda i, s, pr: (i, 0))]),
        out_shape=[jax.ShapeDtypeStruct((N_CHIPS, hr, D), BF16), jax.ShapeDtypeStruct((hr, D), F32)],
        compiler_params=_params(("parallel", "arbitrary")),
    )(place, g, got)


def _rs_final_sum(place, parts, p32, name):
    hr = parts.shape[1]
    tr = _tile(hr, 640, 16)
    nt = hr // tr

    def body(pl_ref, a_ref, b_ref, c_ref, p32_ref, o_ref):
        o_ref[...] = ((p32_ref[...] + a_ref[0].astype(F32)) + b_ref[0].astype(F32)) + c_ref[0].astype(F32)

    def other(j):
        return pl.BlockSpec((1, tr, D), lambda i, pr: (j + jnp.where(pr[1] <= j, 1, 0), i, 0))

    return pl.pallas_call(
        body, name=name,
        grid_spec=pltpu.PrefetchScalarGridSpec(
            num_scalar_prefetch=1, grid=(nt,),
            in_specs=[other(0), other(1), other(2), pl.BlockSpec((tr, D), lambda i, pr: (i, 0))],
            out_specs=pl.BlockSpec((tr, D), lambda i, pr: (pr[0] * nt + i, 0))),
        out_shape=jax.ShapeDtypeStruct((2 * hr, D), F32),
        compiler_params=_params(("parallel",)),
    )(place, parts, parts, parts, p32)


def _rs_pair_gather(both, name):
    hr = both.shape[0] // 2

    def body(in_ref, out_ref, send_sem, recv_sem):
        x, y, c = _place()
        mine = out_ref.at[pl.ds(c * hr, hr), :]
        cp = pltpu.make_async_remote_copy(
            src_ref=mine, dst_ref=mine, send_sem=send_sem, recv_sem=recv_sem,
            device_id=(x, y, 1 - c), device_id_type=MESH)
        cp.start()
        theirs = out_ref.at[pl.ds((1 - c) * hr, hr), :]
        pltpu.make_async_remote_copy(
            src_ref=theirs, dst_ref=theirs, send_sem=send_sem, recv_sem=recv_sem,
            device_id=(x, y, 1 - c), device_id_type=MESH).wait_recv()
        cp.wait_send()

    any_spec = pl.BlockSpec(memory_space=pl.ANY)
    return pl.pallas_call(
        body, name=name,
        out_shape=jax.ShapeDtypeStruct(both.shape, F32),
        in_specs=[any_spec], out_specs=any_spec, input_output_aliases={0: 0},
        scratch_shapes=[pltpu.SemaphoreType.DMA, pltpu.SemaphoreType.DMA],
    )(both)


def _local_step(x, ctx, tgt, mods, mc, ag_gin, ag_main, place, small):
    nb, t, _ = x.shape
    tc = ctx.shape[1]
    n = nb * t
    nc = nb * tc
    xf = x.reshape(n, D)
    cf = ctx.reshape(nc, D)
    tf = tgt.reshape(n, D)
    vec = lambda a: a.reshape(1, -1)
    m = [[mods[l, :, k, :].reshape(nb, 1, D) for k in range(N_MOD)] for l in range(2)]
    mc_b = [jnp.broadcast_to(mc[k].reshape(1, 1, D), (nb, 1, D)) for k in range(2)]

    cw = [small["ffn_conv_w"][l] for l in range(2)]
    cb = [small["ffn_conv_b"][l].reshape(1, -1) for l in range(2)]
    w2 = jnp.zeros((128, 2 * KEY), F32)
    w2 = w2.at[0:RANK, 0:KEY].set(small["gla_w_a2"][0]).at[RANK:2 * RANK, KEY:].set(small["gla_w_a2"][1])
    b2 = small["gla_b_a"].reshape(1, 2 * KEY)
    hg = small["gla_head_norm"].reshape(1, DV)

    hn_all = _mod_fwd(xf, vec(small["norm_mix"][0]), m[0][0], m[0][1], t, "mod0_main", out_rows=n + nc)
    hn_all = _mod_fwd(cf, vec(small["norm_mix"][0]), mc_b[0], mc_b[1], tc, "mod0_ctx", out_rows=n + nc,
                      into=hn_all, row0=n)
    gin = _ag_pass_on(_split_wait(ag_gin, hn_all, _ag_copies, "ag_gin_wait")[1], "ag_gin_pass_on")
    w_gin = jnp.pad(gin[:, :_GIN_ROWS, :].reshape(GLA_IN, D), ((0, GLA_IN_PAD - GLA_IN), (0, 0)))
    p_all, la_all = _gla_in_proj(hn_all, w_gin, w2, b2)
    o2, ss = _gla_scan_fwd(p_all, la_all, nb, t, tc)
    arrived = _split_wait(ag_main, o2[0], _ag_copies, "ag_main_wait")[1]
    passing = _split_start(ag_main[4], arrived, _pass_on_copies, 3, o2[1], "ag_main_pass_start")
    offs = _offsets(_MAIN, _MAIN_ROWS)
    rows = _MAIN_ROWS

    def w_nt(a, k, name, out_dtype=BF16, tm=1024):
        return _mm_nt_w(a, wg, offs[k], rows[k], name, tm, out_dtype)

    def w_nn_mod(a3, k, h, gate, gain, shift, scale, name):
        return _mm_nn_w_mod(a3, wg, offs[k], rows[k], h, gate, vec(gain), shift, scale, t, name, 512)

    yb0 = _gla_post_fwd(o2, p_all, hg + passing[4][0:1, 0:1], n)
    wg = _split_wait(passing, yb0, _pass_on_copies, "ag_main_pass_wait")[1]
    y0, h1, hn1 = w_nn_mod(yb0[None], "gla_out", xf, m[0][2], small["norm_ffn"][0], m[0][3], m[0][4],
                           "gla_out_proj_mod")
    u0, z0 = _ffn_up_mid_fwd(hn1, wg, offs["up_t0"], cw[0], cb[0], nb, t, "ffn0_up_mid")
    f0, h2, hn2 = w_nn_mod(z0[None], "down0", h1, m[0][5], small["norm_mix"][1], m[1][0], m[1][1],
                           "ffn0_down_mod")
    p1, yb1 = _sc_in_mid_fwd(hn2, wg, offs["sc_in_t"], small["sc_conv_w"], nb, t)
    y1, h3, hn3 = w_nn_mod(yb1[None], "sc_out", h2, m[1][2], small["norm_ffn"][1], m[1][3], m[1][4],
                           "sc_out_proj_mod")
    u1, z1 = _ffn_up_mid_fwd(hn3, wg, offs["up_t1"], cw[1], cb[1], nb, t, "ffn1_up_mid")
    loss, dh4, df1, dm15, dfinal = _mm_nn_w_final(z1[None], wg, offs["down1"], rows["down1"], h3, m[1][5],
                                                  vec(small["final_norm"]), tf, t, "ffn1_down_final", 512)

    gs = {}
    dmods = [[None] * N_MOD for _ in range(2)]
    dmods[1][5] = dm15

    def w_dw(a3, b, g_prev, k, name, tm):
        return _mm_dw(a3, b, g_prev, offs[k], rows[k], name, tm)

    def w_dx_mod(a3, k, h_in, dh_out, gain, scale, y_prev, gate_prev, name):
        return _mm_nn_w_modbwd(a3, wg, offs[k], rows[k], h_in, dh_out, vec(gain), scale, y_prev, gate_prev, t,
                               name, 256)

    def ffn_bwd(l, df, u, z, hn, g_prev, h_in, dh_out, scale, y_prev, gate_prev):
        dz = w_nt(df, f"down{l}", f"ffn{l}_down_dx")
        g_acc = w_dw(z[None], df, g_prev, f"down{l}", f"ffn{l}_down_dw", 640)
        du, dcw, dcb = _ffn_mid_bwd(u, cw[l], cb[l], dz, nb, t, f"ffn{l}_mid_bwd")
        r = w_dx_mod(du, f"up_t{l}", h_in, dh_out, small["norm_ffn"][l], scale, y_prev, gate_prev,
                     f"ffn{l}_up_dx_mod")
        g_acc = w_dw(du, hn, g_acc, f"up_t{l}", f"ffn{l}_up_dw", 640)
        return r, g_acc, jnp.moveaxis(dcw, 0, 1).reshape(3, 2 * HID), dcb.reshape(2 * HID)

    r, g_acc, dcw1, dcb1 = ffn_bwd(1, df1, u1, z1, hn3, None, h3, dh4, m[1][4], y1, m[1][2])
    dh3, dmods[1][4], dmods[1][3], dnf1, dy1, dmods[1][2] = (r["dh"], r["dscale"], r["dshift"], r["dgain"],
                                                             r["dy_prev"], r["dgate_prev"])
    dyb1 = w_nt(dy1, "sc_out", "sc_out_dx")
    g_acc = w_dw(yb1[None], dy1, g_acc, "sc_out", "sc_out_dw", 256)
    dp1, dscw = _sc_mid_bwd(p1, small["sc_conv_w"], dyb1, nb, t)
    r = w_dx_mod(dp1, "sc_in_t", h2, dh3, small["norm_mix"][1], m[1][1], f0, m[0][5], "sc_in_dx_mod")
    g_acc = w_dw(dp1, hn2, g_acc, "sc_in_t", "sc_in_dw", 256)
    dh2, dmods[1][1], dmods[1][0], dnm1, df0, dmods[0][5] = (r["dh"], r["dscale"], r["dshift"], r["dgain"],
                                                             r["dy_prev"], r["dgate_prev"])
    r, g_acc, dcw0, dcb0 = ffn_bwd(0, df0, u0, z0, hn1, g_acc, h1, dh2, m[0][4], y0, m[0][2])
    dh1, dmods[0][4], dmods[0][3], dnf0, dy0, dmods[0][2] = (r["dh"], r["dscale"], r["dshift"], r["dgain"],
                                                             r["dy_prev"], r["dgate_prev"])
    g_packed = w_dw(yb0[None], dy0, g_acc, "gla_out", "gla_out_dw", 256)
    pair = _split_start(g_packed, (N_CHIPS, _MAIN_TOTAL // 2, D), _pe_copies, 1, dy0, "rs_main_pair_start")
    do, dgate, dhg = _gla_out_dx_post_bwd(dy0, wg, offs["gla_out"], o2, p_all, hg + pair[4][0:1, 0:1], n)
    g_packed, from_sibling = _split_wait(pair, do, _pe_copies, "rs_main_pair_wait")
    p16, p32 = _rs_chip_sum(place, g_packed, from_sibling, "rs_main_chip_sum")
    sc_main = _split_start(p16, p16.shape, _sc_copies, 3, p32, "rs_main_scatter_start")
    dq, dk, dv, dla = _gla_scan_bwd(p_all, la_all, do, ss, nb, t, tc, sc_main[4])
    dp, dw2, db2 = _gla_assemble(p_all, w2, b2, dq, dk, dv, dla, dgate, n)
    grad_x, dmods[0][1], dmods[0][0], dmc0, dmc1, dnm0 = _gla_in_dx_mod(
        dp, w_gin, xf, cf, dh1, vec(small["norm_mix"][0]), m[0][1], mc[1].reshape(1, 1, D), t, tc)
    dmc = jnp.concatenate([dmc0, dmc1], axis=0)
    landed = _split_wait(sc_main, grad_x, _sc_copies, "rs_main_scatter_wait")[1]
    g_main = _split_start(sc_main[4], _rs_final_sum(place, landed, p32, "rs_main_final_sum"),
                          _pair_gather_copies, 1, landed, "rs_main_gather_start")
    g_gin = _mm(dp, hn_all, "tn", F32, "gla_in_dw", 640, 1024)[:GLA_IN]
    g_gin = jnp.pad(g_gin.reshape(N_CHIPS, _GIN_ROWS, D), ((0, 0), (0, _GIN_PAD - _GIN_ROWS), (0, 0)))
    from_sibling = _rs_pair_exchange(g_gin, g_main[4], "rs_gin_pair_exchange")
    p16_gin, p32_gin = _rs_chip_sum(place, g_gin, from_sibling, "rs_gin_chip_sum")

    gs["norm_mix"] = jnp.concatenate([dnm0, dnm1], axis=0)
    gs["norm_ffn"] = jnp.concatenate([dnf0, dnf1], axis=0)
    gs["final_norm"] = dfinal.reshape(D)
    gs["gla_w_a2"] = jnp.stack([dw2[0:RANK, 0:KEY], dw2[RANK:2 * RANK, KEY:]])
    gs["gla_b_a"] = db2.reshape(2, KEY)
    gs["gla_head_norm"] = dhg.reshape(DV)
    gs["sc_conv_w"] = dscw
    gs["ffn_conv_w"] = jnp.stack([dcw0, dcw1])
    gs["ffn_conv_b"] = jnp.stack([dcb0, dcb1])
    dmods_arr = jnp.stack([jnp.stack([dmods[l][k].reshape(nb, D) for k in range(N_MOD)], axis=1) for l in range(2)])
    return loss, grad_x.reshape(nb, t, D), g_main, p16_gin, p32_gin, gs, dmods_arr, dmc


def _pack(arrs):
    parts, meta, off = [], [], 0
    for a in arrs:
        r = a.size // 128
        rp = -(-r // 8) * 8
        a2 = a.reshape(r, 128).astype(F32)
        if rp != r:
            a2 = jnp.pad(a2, ((0, rp - r), (0, 0)))
        parts.append(a2)
        meta.append((off, r, a.shape))
        off += rp
    return jnp.concatenate(parts, axis=0), meta


def _unpack(buf, meta, lead=()):
    return [buf[..., off:off + r, :].reshape(*lead, *shape) for off, r, shape in meta]


_MAIN = ("up_t0", "up_t1", "down0", "down1", "sc_in_t", "gla_out", "sc_out")
_MAIN_ROWS = {"sc_in_t": 3 * D // N_CHIPS, "up_t0": 2 * HID // N_CHIPS, "up_t1": 2 * HID // N_CHIPS,
              "gla_out": D // N_CHIPS, "sc_out": D // N_CHIPS, "down0": HID // N_CHIPS, "down1": HID // N_CHIPS}
_MAIN_TOTAL = sum(_MAIN_ROWS.values())
_GIN_ROWS = GLA_IN // N_CHIPS
_GIN_PAD = -(-_GIN_ROWS // 32) * 32


def _offsets(names, rows):
    off, out = 0, {}
    for k in names:
        out[k] = off
        off += rows[k]
    return out


def kernel(x, c, ctx, c_ctx, ada_w, ada_b, norm_mix, norm_ffn, gla_w_in, gla_w_a2, gla_b_a, gla_head_norm, gla_w_out, sc_w_in, sc_conv_w, sc_w_out, ffn_w_up, ffn_conv_w, ffn_conv_b, ffn_w_down, final_norm, loss_target, m_c_ctx, m_ada_w, m_ada_b, m_norm_mix, m_norm_ffn, m_gla_w_in, m_gla_w_a2, m_gla_b_a, m_gla_head_norm, m_gla_w_out, m_sc_w_in, m_sc_conv_w, m_sc_w_out, m_ffn_w_up, m_ffn_conv_w, m_ffn_conv_b, m_ffn_w_down, m_final_norm, v_c_ctx, v_ada_w, v_ada_b, v_norm_mix, v_norm_ffn, v_gla_w_in, v_gla_w_a2, v_gla_b_a, v_gla_head_norm, v_gla_w_out, v_sc_w_in, v_sc_conv_w, v_sc_w_out, v_ffn_w_up, v_ffn_conv_w, v_ffn_conv_b, v_ffn_w_down, v_final_norm):
    ix, iy, ic = _place()
    chip = 2 * ix + iy
    dev = 2 * chip + ic
    place = jnp.stack([ic, chip]).astype(jnp.int32)
    nb = x.shape[0]
    offs = _offsets(_MAIN, _MAIN_ROWS)

    buf, meta = _pack([c, ffn_conv_w, sc_conv_w, gla_w_a2, gla_b_a])
    got = _allgather_small(buf, "gather_small_in").reshape(N_DEV, buf.shape[0], 128)
    c_all, fcw, scw, wa2, ba = _unpack(got, meta, (N_DEV,))
    c_all = c_all.reshape(N_DEV * nb, D)
    per_chip = lambda a: a[0::2]
    ffn_conv_w_full = jnp.moveaxis(per_chip(fcw), 0, 2).reshape(2, 3, 2 * HID)
    sc_conv_w_full = jnp.moveaxis(per_chip(scw)[:, 0], 0, 1).reshape(3, D)
    gla_w_a2_full = jnp.moveaxis(per_chip(wa2)[:, 0], 0, 2).reshape(2, RANK, KEY)
    gla_b_a_full = jnp.moveaxis(per_chip(ba)[:, 0], 0, 1).reshape(2, KEY)

    cvec = jnp.concatenate([c_all, c_ctx.reshape(1, D), jnp.zeros((ADA_ROWS - N_DEV * nb - 1, D), F32)], axis=0)
    ada_b_sh = lax.dynamic_slice_in_dim(ada_b, chip * ADA_SH, ADA_SH, axis=1).reshape(2, 1, ADA_SH)
    mod_sh = _ada_fwd(cvec, ada_w, ada_b_sh)
    got = _allgather_small(mod_sh.reshape(2 * ADA_ROWS, ADA_SH), "gather_mod")
    mod_full = jnp.moveaxis(per_chip(got.reshape(N_DEV, 2, ADA_ROWS, ADA_SH)), 0, 2).reshape(2, ADA_ROWS, N_MOD * D)
    mc = mod_full[0, N_DEV * nb, :2 * D].reshape(2, D)

    own = {"sc_in_t": sc_w_in[0].T, "up_t0": ffn_w_up[0].T, "up_t1": ffn_w_up[1].T,
           "gla_out": gla_w_out[0], "sc_out": sc_w_out[0], "down0": ffn_w_down[0], "down1": ffn_w_down[1]}
    own_main = jnp.concatenate([own[k].astype(BF16) for k in _MAIN], axis=0)
    own_gin = jnp.pad(gla_w_in[0].T.astype(BF16), ((0, _GIN_PAD - _GIN_ROWS), (0, 0)))
    ag_gin = _split_start(own_gin, (N_CHIPS, _GIN_PAD, D), _ag_copies, 4, mc, "ag_gin_start")
    ag_main = _split_start(own_main, (N_CHIPS, _MAIN_TOTAL, D), _ag_copies, 4, ag_gin[4], "ag_main_start")
    mods = lax.dynamic_slice_in_dim(mod_full, dev * nb, nb, axis=1).reshape(2, nb, N_MOD, D) + ag_main[4][0, 0]

    small = {"norm_mix": norm_mix, "norm_ffn": norm_ffn, "final_norm": final_norm, "gla_w_a2": gla_w_a2_full,
             "gla_b_a": gla_b_a_full, "gla_head_norm": gla_head_norm[0], "sc_conv_w": sc_conv_w_full,
             "ffn_conv_w": ffn_conv_w_full, "ffn_conv_b": ffn_conv_b}
    loss_p, grad_x, g_main, p16_gin, p32_gin, gs, dmods, dmc = _local_step(x, ctx, loss_target, mods, mc, ag_gin,
                                                                           ag_main, place, small)

    sum_names = ["norm_mix", "norm_ffn", "final_norm", "gla_w_a2", "gla_b_a", "gla_head_norm", "sc_conv_w",
                 "ffn_conv_w", "ffn_conv_b"]
    buf, meta = _pack([jnp.broadcast_to(loss_p, (8, 128))] + [gs[k] for k in sum_names] + [dmc, dmods])
    n_sum = meta[-1][0]
    got = _allgather_small(buf, "gather_small_grads").reshape(N_DEV, buf.shape[0], 128)
    summed = _sum_slots(got[:, :n_sum], "sum_small_grads")
    parts = _unpack(summed, meta[:-1])
    loss = parts[0][0, 0]
    g_small = dict(zip(sum_names, parts[1:-1]))
    dmc_tot = parts[-1]
    dmods_all = jnp.moveaxis(_unpack(got, meta[-1:], (N_DEV,))[0], 0, 1).reshape(2, N_DEV * nb, N_MOD * D)

    ctx_row = jnp.stack([jnp.concatenate([dmc_tot.reshape(2 * D), jnp.zeros(((N_MOD - 2) * D,), F32)]),
                         jnp.zeros((N_MOD * D,), F32)]).reshape(2, 1, N_MOD * D)
    dmod_ext = jnp.concatenate([dmods_all, ctx_row, jnp.zeros((2, ADA_ROWS - N_DEV * nb - 1, N_MOD * D), F32)], axis=1)
    g_ada_b = _sum_slots(jnp.moveaxis(dmod_ext, 1, 0).reshape(ADA_ROWS, 2 * N_MOD * D // 128, 128),
                         "sum_ada_b").reshape(2, N_MOD * D)
    dmod_sh = lax.dynamic_slice_in_dim(dmod_ext, chip * ADA_SH, ADA_SH, axis=2)
    g_ada_w, dcv = _ada_bwd(cvec, ada_w, dmod_sh)
    dscc_part = (dcv[0, N_DEV * nb] + dcv[1, N_DEV * nb]).reshape(8, 128)
    got = _allgather_small(dscc_part, "gather_dscc").reshape(N_DEV, 8, 128)
    g_c_ctx = _cctx_grad(per_chip(got), c_ctx.reshape(8, 128)).reshape(D)

    sc_gin = _split_start(p16_gin, p16_gin.shape, _sc_copies, 3, g_c_ctx, "rs_gin_scatter_start")
    g_main = _split_wait(g_main, sc_gin[4], _pair_gather_copies, "rs_main_gather_wait")[1]
    seg = {k: g_main[offs[k]:offs[k] + _MAIN_ROWS[k]] for k in _MAIN}

    sl_chip = lambda a, axis, width: lax.dynamic_slice_in_dim(a, chip * width, width, axis=axis)
    grads = {
        "c_ctx": g_c_ctx, "ada_w": g_ada_w, "ada_b": g_ada_b, "norm_mix": g_small["norm_mix"],
        "norm_ffn": g_small["norm_ffn"],
        "gla_w_a2": sl_chip(g_small["gla_w_a2"], 2, KEY // N_CHIPS)[None],
        "gla_b_a": sl_chip(g_small["gla_b_a"], 1, KEY // N_CHIPS)[None],
        "gla_head_norm": g_small["gla_head_norm"][None], "gla_w_out": seg["gla_out"][None],
        "sc_w_in": seg["sc_in_t"].T[None], "sc_conv_w": sl_chip(g_small["sc_conv_w"], 1, D // N_CHIPS)[None],
        "sc_w_out": seg["sc_out"][None], "ffn_w_up": jnp.stack([seg["up_t0"].T, seg["up_t1"].T]),
        "ffn_conv_w": sl_chip(g_small["ffn_conv_w"], 2, 2 * HID // N_CHIPS), "ffn_conv_b": g_small["ffn_conv_b"],
        "ffn_w_down": jnp.stack([seg["down0"], seg["down1"]]), "final_norm": g_small["final_norm"],
    }
    weights = {"c_ctx": c_ctx, "ada_w": ada_w, "ada_b": ada_b, "norm_mix": norm_mix, "norm_ffn": norm_ffn,
               "gla_w_in": gla_w_in, "gla_w_a2": gla_w_a2, "gla_b_a": gla_b_a, "gla_head_norm": gla_head_norm,
               "gla_w_out": gla_w_out, "sc_w_in": sc_w_in, "sc_conv_w": sc_conv_w, "sc_w_out": sc_w_out,
               "ffn_w_up": ffn_w_up, "ffn_conv_w": ffn_conv_w, "ffn_conv_b": ffn_conv_b, "ffn_w_down": ffn_w_down,
               "final_norm": final_norm}
    mom1 = {"c_ctx": m_c_ctx, "ada_w": m_ada_w, "ada_b": m_ada_b, "norm_mix": m_norm_mix, "norm_ffn": m_norm_ffn,
            "gla_w_in": m_gla_w_in, "gla_w_a2": m_gla_w_a2, "gla_b_a": m_gla_b_a, "gla_head_norm": m_gla_head_norm,
            "gla_w_out": m_gla_w_out, "sc_w_in": m_sc_w_in, "sc_conv_w": m_sc_conv_w, "sc_w_out": m_sc_w_out,
            "ffn_w_up": m_ffn_w_up, "ffn_conv_w": m_ffn_conv_w, "ffn_conv_b": m_ffn_conv_b,
            "ffn_w_down": m_ffn_w_down, "final_norm": m_final_norm}
    mom2 = {"c_ctx": v_c_ctx, "ada_w": v_ada_w, "ada_b": v_ada_b, "norm_mix": v_norm_mix, "norm_ffn": v_norm_ffn,
            "gla_w_in": v_gla_w_in, "gla_w_a2": v_gla_w_a2, "gla_b_a": v_gla_b_a, "gla_head_norm": v_gla_head_norm,
            "gla_w_out": v_gla_w_out, "sc_w_in": v_sc_w_in, "sc_conv_w": v_sc_conv_w, "sc_w_out": v_sc_w_out,
            "ffn_w_up": v_ffn_w_up, "ffn_conv_w": v_ffn_conv_w, "ffn_conv_b": v_ffn_conv_b,
            "ffn_w_down": v_ffn_w_down, "final_norm": v_final_norm}
    names = list(weights)

    big_names = ["ada_w", "gla_w_out", "sc_w_in", "sc_w_out", "ffn_w_up", "ffn_w_down", "gla_w_in"]
    small_names = [k for k in names if k not in big_names]
    delta, new_m, new_v = {}, {}, {}
    done = []

    def big_adamw(k, token):
        delta[k], new_m[k], new_v[k] = _adamw(weights[k], grads[k], mom1[k], mom2[k], "adamw_" + k, token)
        done.append(new_v[k][0, 0:1, 0:128])

    for k in big_names[:-1]:
        grads[k] = grads[k].reshape(weights[k].shape)
        big_adamw(k, sc_gin[4])
    for k in small_names:
        grads[k] = grads[k].reshape(weights[k].shape)
    packed = [_pack([src[k] for k in small_names]) for src in (weights, grads, mom1, mom2)]
    meta = packed[0][1]
    rows_pad = -packed[0][0].shape[0] % 128
    bufs = [jnp.pad(p[0], ((0, rows_pad), (0, 0)))[None] for p in packed]
    outs = _adamw(bufs[0], bufs[1], bufs[2], bufs[3], "adamw_small", sc_gin[4])
    done.append(outs[2][0, 0:1, :])
    for dst, o in zip((delta, new_m, new_v), outs):
        for k, a in zip(small_names, _unpack(o[0], meta)):
            dst[k] = a
    landed = _split_wait(sc_gin, jnp.concatenate(done, axis=0), _sc_copies, "rs_gin_scatter_wait")[1]
    g_gin_shard = _rs_pair_gather(_rs_final_sum(place, landed, p32_gin, "rs_gin_final_sum"), "rs_gin_pair_gather")
    grads["gla_w_in"] = g_gin_shard[:_GIN_ROWS].T[None]
    big_adamw("gla_w_in", sc_gin[4])

    return (loss, grad_x, *[grads[k] for k in names], *[delta[k] for k in names], *[new_m[k] for k in names],
            *[new_v[k] for k in names])
```

```python
import jax
import jax.numpy as jnp
from jax import lax
from jax.experimental import pallas as pl
from jax.experimental.pallas import tpu as pltpu

F32 = jnp.float32
BF16 = jnp.bfloat16
MESH = pl.DeviceIdType.MESH

EPS = 1e-6
D = 1024
N_MOD = 6
HEADS = 4
DK = 128
DV = 256
KEY = HEADS * DK
RANK = 16
TAU = 16.0
CH = 64
GRID_W = 64
HID = 2560
GLA_IN = 2 * KEY + 2 * D + 2 * RANK
GLA_IN_PAD = 3200
Q_SCALE = DK ** -0.5
N_CHIPS = 4
N_DEV = 8

ADAM_LR = 0.001
ADAM_B1 = 0.9
ADAM_B2 = 0.999
ADAM_EPS = 1e-08
ADAM_WD = 0.01
ADAM_STEP = 10

VMEM_LIMIT = 56 * 1024 * 1024


def _params(sem):
    return pltpu.CompilerParams(dimension_semantics=sem, vmem_limit_bytes=VMEM_LIMIT)


def _tile(n, pref, mult=8):
    if n <= pref:
        return n
    for t in range(pref - pref % mult, 0, -mult):
        if n % t == 0:
            return t
    raise ValueError((n, pref, mult))


_NN = (((1,), (0,)), ((), ()))
_NT = (((1,), (1,)), ((), ()))
_TN = (((0,), (0,)), ((), ()))


def _dot(a, b, dims=_NN):
    return lax.dot_general(a.astype(BF16), b.astype(BF16), dims, preferred_element_type=F32)


def _sigmoid(x):
    return 1.0 / (1.0 + jnp.exp(-x))


def _rowsum(x):
    return jnp.sum(x, axis=0, keepdims=True)


def _mm(a, b, form, out_dtype, name, tm, tn):
    if form == "tn":
        K, M = a.shape
    else:
        M, K = a.shape
    N = b.shape[0] if form == "nt" else b.shape[1]
    tm = _tile(M, tm, 128)
    tn = _tile(N, tn, 128)
    dims = {"nn": _NN, "nt": _NT, "tn": _TN}[form]

    def body(a_ref, b_ref, o_ref):
        o_ref[...] = _dot(a_ref[...], b_ref[...], dims).astype(o_ref.dtype)

    if form == "tn":
        a_spec = pl.BlockSpec((K, tm), lambda i, j: (0, i))
    else:
        a_spec = pl.BlockSpec((tm, K), lambda i, j: (i, 0))
    if form == "nt":
        b_spec = pl.BlockSpec((tn, K), lambda i, j: (j, 0))
    else:
        b_spec = pl.BlockSpec((K, tn), lambda i, j: (0, j))
    return pl.pallas_call(
        body,
        name=name,
        grid=(M // tm, N // tn),
        in_specs=[a_spec, b_spec],
        out_specs=pl.BlockSpec((tm, tn), lambda i, j: (i, j)),
        out_shape=jax.ShapeDtypeStruct((M, N), out_dtype),
        compiler_params=_params(("parallel", "parallel")),
    )(a, b)


def _mm_nt_w(a, wg, off, rows, name, tm, out_dtype):
    m = a.shape[0]
    tm = _tile(m, tm, 128)
    if N_CHIPS * rows <= D:

        def body_small(a_ref, w_ref, o_ref):
            av = a_ref[...]
            for s in range(N_CHIPS):
                o_ref[:, s * rows:(s + 1) * rows] = _dot(av, w_ref[s], _NT).astype(o_ref.dtype)

        return pl.pallas_call(
            body_small, name=name, grid=(m // tm,),
            in_specs=[pl.BlockSpec((tm, D), lambda i: (i, 0)),
                      pl.BlockSpec((N_CHIPS, rows, D), lambda i: (0, off // rows, 0))],
            out_specs=pl.BlockSpec((tm, N_CHIPS * rows), lambda i: (i, 0)),
            out_shape=jax.ShapeDtypeStruct((m, N_CHIPS * rows), out_dtype),
            compiler_params=_params(("parallel",)),
        )(a, wg)

    def body(a_ref, w_ref, o_ref):
        o_ref[...] = _dot(a_ref[...], w_ref[0], _NT).astype(o_ref.dtype)

    return pl.pallas_call(
        body, name=name, grid=(m // tm, N_CHIPS),
        in_specs=[pl.BlockSpec((tm, D), lambda i, s: (i, 0)),
                  pl.BlockSpec((1, rows, D), lambda i, s: (s, off // rows, 0))],
        out_specs=pl.BlockSpec((tm, rows), lambda i, s: (i, s)),
        out_shape=jax.ShapeDtypeStruct((m, N_CHIPS * rows), out_dtype),
        compiler_params=_params(("parallel", "parallel")),
    )(a, wg)


def _mm_nn_w_mod(a3, wg, off, rows, h, gate, gain, shift, scale, tpb_rows, name, tm):
    parts, m, kp = a3.shape
    assert parts * kp == N_CHIPS * rows
    tm = _tile(tpb_rows, tm, 128)
    tpb = tpb_rows // tm
    cuts = sorted({s * rows for s in range(N_CHIPS + 1)} | {p * kp for p in range(parts + 1)})
    pieces = [(k0 // kp, k0 % kp, k0 // rows, k0 % rows, k1 - k0) for k0, k1 in zip(cuts[:-1], cuts[1:])]

    def body(a_ref, w_ref, h_ref, gate_ref, gain_ref, sh_ref, sc_ref, y_ref, hout_ref, hn_ref):
        acc = None
        for p, a0, s, r0, width in pieces:
            term = _dot(a_ref[p, :, a0:a0 + width], w_ref[s, r0:r0 + width, :])
            acc = term if acc is None else acc + term
        y_ref[...] = acc
        hv = h_ref[...] + gate_ref[0] * acc
        hout_ref[...] = hv
        r = lax.rsqrt(jnp.mean(hv * hv, axis=-1, keepdims=True) + EPS)
        hn_ref[...] = ((hv * r) * gain_ref[...] * (1.0 + sc_ref[0]) + sh_ref[0]).astype(BF16)

    row = pl.BlockSpec((tm, D), lambda i: (i, 0))
    per_b = pl.BlockSpec((1, 1, D), lambda i: (i // tpb, 0, 0))
    return pl.pallas_call(
        body, name=name, grid=(m // tm,),
        in_specs=[pl.BlockSpec((parts, tm, kp), lambda i: (0, i, 0)),
                  pl.BlockSpec((N_CHIPS, rows, D), lambda i: (0, off // rows, 0)),
                  row, per_b, pl.BlockSpec((1, D), lambda i: (0, 0)), per_b, per_b],
        out_specs=[row, row, row],
        out_shape=[jax.ShapeDtypeStruct((m, D), F32), jax.ShapeDtypeStruct((m, D), F32),
                   jax.ShapeDtypeStruct((m, D), BF16)],
        compiler_params=_params(("parallel",)),
    )(a3, wg, h, gate, gain, shift, scale)


def _mm_nn_w_final(a3, wg, off, rows, h, gate, gain, tgt, tpb_rows, name, tm):
    parts, m, kp = a3.shape
    assert parts * kp == N_CHIPS * rows
    nb = m // tpb_rows
    tm = _tile(tpb_rows, tm, 128)
    tpb = tpb_rows // tm
    cuts = sorted({s * rows for s in range(N_CHIPS + 1)} | {p * kp for p in range(parts + 1)})
    pieces = [(k0 // kp, k0 % kp, k0 // rows, k0 % rows, k1 - k0) for k0, k1 in zip(cuts[:-1], cuts[1:])]

    def body(a_ref, w_ref, h_ref, gate_ref, gain_ref, tgt_ref, loss_ref, dh_ref, df_ref, dgate_ref, dgain_ref):
        i = pl.program_id(0)

        @pl.when(i == 0)
        def _():
            loss_ref[...] = jnp.zeros_like(loss_ref)
            dgain_ref[...] = jnp.zeros_like(dgain_ref)

        @pl.when(i % tpb == 0)
        def _():
            dgate_ref[...] = jnp.zeros_like(dgate_ref)

        fv = None
        for p, a0, s, r0, width in pieces:
            term = _dot(a_ref[p, :, a0:a0 + width], w_ref[s, r0:r0 + width, :])
            fv = term if fv is None else fv + term
        gate_v = gate_ref[0]
        hv = h_ref[...] + gate_v * fv
        r = lax.rsqrt(jnp.mean(hv * hv, axis=-1, keepdims=True) + EPS)
        y = hv * r
        gain_v = gain_ref[...]
        e = y * gain_v - tgt_ref[...]
        s_ = jnp.sum(_rowsum(e * e), axis=1, keepdims=True) * (0.5 / D)
        loss_ref[...] += jnp.broadcast_to(s_, loss_ref.shape)
        dout = e * (1.0 / D)
        dgain_ref[...] += _rowsum(dout * y)
        dy = dout * gain_v
        dh = r * (dy - y * jnp.mean(dy * y, axis=-1, keepdims=True))
        dh_ref[...] = dh
        df_ref[...] = (dh * gate_v).astype(BF16)
        dgate_ref[0] += _rowsum(dh * fv)

    row = pl.BlockSpec((tm, D), lambda i: (i, 0))
    per_b = pl.BlockSpec((1, 1, D), lambda i: (i // tpb, 0, 0))
    vec = pl.BlockSpec((1, D), lambda i: (0, 0))
    return pl.pallas_call(
        body, name=name, grid=(m // tm,),
        in_specs=[pl.BlockSpec((parts, tm, kp), lambda i: (0, i, 0)),
                  pl.BlockSpec((N_CHIPS, rows, D), lambda i: (0, off // rows, 0)), row, per_b, vec, row],
        out_specs=[pl.BlockSpec((1, 128), lambda i: (0, 0)), row, row, per_b, vec],
        out_shape=[jax.ShapeDtypeStruct((1, 128), F32), jax.ShapeDtypeStruct((m, D), F32),
                   jax.ShapeDtypeStruct((m, D), BF16), jax.ShapeDtypeStruct((nb, 1, D), F32),
                   jax.ShapeDtypeStruct((1, D), F32)],
        compiler_params=_params(("arbitrary",)),
    )(a3, wg, h, gate, gain, tgt)


def _mm_nn_w_modbwd(a3, wg, off, rows, h_in, dh_out, gain, scale, y_prev, gate_prev, tpb_rows, name, tm):
    parts, m, kp = a3.shape
    assert parts * kp == N_CHIPS * rows
    nb = m // tpb_rows
    tm = _tile(tpb_rows, tm, 128)
    tpb = tpb_rows // tm
    cuts = sorted({s * rows for s in range(N_CHIPS + 1)} | {p * kp for p in range(parts + 1)})
    pieces = [(k0 // kp, k0 % kp, k0 // rows, k0 % rows, k1 - k0) for k0, k1 in zip(cuts[:-1], cuts[1:])]

    def body(a_ref, w_ref, h_ref, gain_ref, sc_ref, dho_ref, yp_ref, gp_ref,
             dh_ref, dsc_ref, dsh_ref, dgain_ref, dyp_ref, dgp_ref):
        i = pl.program_id(0)

        @pl.when(i == 0)
        def _():
            dgain_ref[...] = jnp.zeros_like(dgain_ref)

        @pl.when(i % tpb == 0)
        def _():
            dsc_ref[...] = jnp.zeros_like(dsc_ref)
            dsh_ref[...] = jnp.zeros_like(dsh_ref)
            dgp_ref[...] = jnp.zeros_like(dgp_ref)

        g = None
        for p, a0, s, r0, width in pieces:
            term = _dot(a_ref[p, :, a0:a0 + width], w_ref[s, r0:r0 + width, :])
            g = term if g is None else g + term
        hv = h_ref[...]
        r = lax.rsqrt(jnp.mean(hv * hv, axis=-1, keepdims=True) + EPS)
        y = hv * r
        gain_v = gain_ref[...]
        dsh_ref[0] += _rowsum(g)
        dsc_ref[0] += _rowsum(g * (y * gain_v))
        drn = g * (1.0 + sc_ref[0])
        dgain_ref[...] += _rowsum(drn * y)
        dy = drn * gain_v
        dh = r * (dy - y * jnp.mean(dy * y, axis=-1, keepdims=True)) + dho_ref[...]
        dh_ref[...] = dh
        dyp_ref[...] = (dh * gp_ref[0]).astype(BF16)
        dgp_ref[0] += _rowsum(dh * yp_ref[...])

    row = pl.BlockSpec((tm, D), lambda i: (i, 0))
    per_b = pl.BlockSpec((1, 1, D), lambda i: (i // tpb, 0, 0))
    vec = pl.BlockSpec((1, D), lambda i: (0, 0))
    per_b_shape = jax.ShapeDtypeStruct((nb, 1, D), F32)
    outs = pl.pallas_call(
        body, name=name, grid=(m // tm,),
        in_specs=[pl.BlockSpec((parts, tm, kp), lambda i: (0, i, 0)),
                  pl.BlockSpec((N_CHIPS, rows, D), lambda i: (0, off // rows, 0)),
                  row, vec, per_b, row, row, per_b],
        out_specs=[row, per_b, per_b, vec, row, per_b],
        out_shape=[jax.ShapeDtypeStruct((m, D), F32), per_b_shape, per_b_shape, jax.ShapeDtypeStruct((1, D), F32),
                   jax.ShapeDtypeStruct((m, D), BF16), per_b_shape],
        compiler_params=_params(("arbitrary",)),
    )(a3, wg, h_in, gain, scale, dh_out, y_prev, gate_prev)
    return dict(zip(("dh", "dscale", "dshift", "dgain", "dy_prev", "dgate_prev"), outs))


def _mm_dw(a3, b, g_prev, off, rows, name, tm):
    parts, ntok, cdim = a3.shape
    assert parts * cdim == N_CHIPS * rows and cdim % tm == 0 and rows % tm == 0 and off % tm == 0

    def body(a_ref, b_ref, *rest):
        rest[-1][0] = _dot(a_ref[0], b_ref[...], _TN)

    in_specs = [pl.BlockSpec((1, ntok, tm), lambda i: ((i * tm) // cdim, 0, ((i * tm) % cdim) // tm)),
                pl.BlockSpec((ntok, D), lambda i: (0, 0))]
    args = [a3, b]
    aliases = {}
    if g_prev is not None:
        in_specs.append(pl.BlockSpec(memory_space=pl.ANY))
        args.append(g_prev)
        aliases = {2: 0}
    return pl.pallas_call(
        body, name=name, grid=(N_CHIPS * rows // tm,),
        in_specs=in_specs,
        out_specs=pl.BlockSpec((1, tm, D), lambda i: ((i * tm) // rows, (off + (i * tm) % rows) // tm, 0)),
        out_shape=jax.ShapeDtypeStruct((N_CHIPS, _MAIN_TOTAL, D), F32),
        input_output_aliases=aliases,
        compiler_params=_params(("parallel",)),
    )(*args)


def _mod_fwd(h, gain, shift, scale, tpb_rows, name, out_rows, into=None, row0=0):
    n = h.shape[0]
    tt = _tile(tpb_rows, 256)
    tpb = tpb_rows // tt
    assert row0 % tt == 0

    def body(h_ref, gain_ref, sh_ref, sc_ref, *rest):
        hv = h_ref[...]
        r = lax.rsqrt(jnp.mean(hv * hv, axis=-1, keepdims=True) + EPS)
        rest[-1][...] = ((hv * r) * gain_ref[...] * (1.0 + sc_ref[0]) + sh_ref[0]).astype(BF16)

    per_b = pl.BlockSpec((1, 1, D), lambda i: (i // tpb, 0, 0))
    in_specs = [pl.BlockSpec((tt, D), lambda i: (i, 0)), pl.BlockSpec((1, D), lambda i: (0, 0)), per_b, per_b]
    args = (h, gain, shift, scale)
    aliases = {}
    if into is not None:
        in_specs = in_specs + [pl.BlockSpec(memory_space=pl.ANY)]
        args = args + (into,)
        aliases = {4: 0}
    return pl.pallas_call(
        body, name=name, grid=(n // tt,), in_specs=in_specs,
        out_specs=pl.BlockSpec((tt, D), lambda i: (i + row0 // tt, 0)),
        out_shape=jax.ShapeDtypeStruct((out_rows, D), BF16),
        input_output_aliases=aliases, compiler_params=_params(("parallel",)),
    )(*args)


def _row_dn1(x):
    t = lax.broadcasted_iota(jnp.int32, x.shape, 0)
    return jnp.where(t % GRID_W == 0, 0.0, pltpu.roll(x, 1, 0))


def _row_up1(x):
    t = lax.broadcasted_iota(jnp.int32, x.shape, 0)
    return jnp.where(t % GRID_W == GRID_W - 1, 0.0, pltpu.roll(x, x.shape[0] - 1, 0))


def _silu(x):
    return x * _sigmoid(x)


def _dsilu(x):
    s = _sigmoid(x)
    return s * (1.0 + x * (1.0 - s))


def _row_ds(i):
    start = i * GRID_W
    return pl.ds(start if isinstance(start, int) else pl.multiple_of(start, GRID_W), GRID_W)


def _grid_row(ref, i, first, last):
    def rows(k):
        return ref[_row_ds(k), :].astype(F32)

    cur = rows(i)
    return (jnp.zeros_like(cur) if first else rows(i - 1)), cur, (jnp.zeros_like(cur) if last else rows(i + 1))


def _over_grid_rows(n_rows, step, carry):
    carry = step(0, carry, True, n_rows == 1)
    if n_rows > 2:
        carry = lax.fori_loop(1, n_rows - 1, lambda i, c: step(i, c, False, False), carry)
    if n_rows > 1:
        carry = step(n_rows - 1, carry, False, True)
    return carry


def _fold8(p):
    return p.reshape(GRID_W // 8, 8, p.shape[1]).sum(axis=0)


def _ffn_up_mid_fwd(hn, wg, off, cw, cb, nb, t, name):
    tcol = 256
    ncol = HID // tcol
    rows_sh = 2 * HID // N_CHIPS

    def conv(x, w_ref):
        zeros = jnp.zeros((GRID_W, x.shape[1]), x.dtype)
        down = jnp.concatenate([zeros, x[: x.shape[0] - GRID_W]], axis=0)
        up = jnp.concatenate([x[GRID_W:], zeros], axis=0)
        return down * w_ref[0:1, :] + x * w_ref[1:2, :] + up * w_ref[2:3, :]

    def body(h_ref, wa_ref, wg_ref, cwa_ref, cwg_ref, cba_ref, cbg_ref, u_ref, z_ref):
        hv = h_ref[...]
        ua = _dot(hv, wa_ref[0], _NT)
        ug = _dot(hv, wg_ref[0], _NT)
        u_ref[0] = ua.astype(BF16)
        u_ref[1] = ug.astype(BF16)
        a = conv(ua, cwa_ref) + cba_ref[...]
        gt = conv(ug, cwg_ref) + cbg_ref[...]
        z_ref[...] = (a * _silu(gt)).astype(BF16)

    def w_spec(part):
        def idx(b, j):
            n = part * HID + j * tcol
            return (n // rows_sh, (off + n % rows_sh) // tcol, 0)
        return pl.BlockSpec((1, tcol, D), idx)

    chan = lambda rows, part: pl.BlockSpec((rows, tcol), lambda b, j: (0, part * ncol + j))
    return pl.pallas_call(
        body, name=name, grid=(nb, ncol),
        in_specs=[pl.BlockSpec((t, D), lambda b, j: (b, 0)), w_spec(0), w_spec(1),
                  chan(3, 0), chan(3, 1), chan(1, 0), chan(1, 1)],
        out_specs=[pl.BlockSpec((2, t, tcol), lambda b, j: (0, b, j)), pl.BlockSpec((t, tcol), lambda b, j: (b, j))],
        out_shape=[jax.ShapeDtypeStruct((2, nb * t, HID), BF16), jax.ShapeDtypeStruct((nb * t, HID), BF16)],
        compiler_params=_params(("parallel", "parallel")),
    )(hn, wg, wg, cw, cw, cb, cb)


def _ffn_mid_bwd(u0, cw, cb, dz, nb, t, name):
    nc = HID // 128
    n_rows = t // GRID_W

    def body(ua3_ref, ug3_ref, wa_ref, wg_ref, ba_ref, bg_ref, dz_ref, du_ref, dw_ref, db_ref, dua_ref, dug_ref):
        ua_ref, ug_ref = ua3_ref.at[0], ug3_ref.at[0]
        b = pl.program_id(1)

        @pl.when(b == 0)
        def _():
            dw_ref[...] = jnp.zeros_like(dw_ref)
            db_ref[...] = jnp.zeros_like(db_ref)

        wa = [wa_ref[k:k + 1, :] for k in range(3)]
        wg = [wg_ref[k:k + 1, :] for k in range(3)]
        ba, bg = ba_ref[...], bg_ref[...]

        def pass1(i, acc, first, last):
            here = _row_ds(i)
            ap, ac, an = _grid_row(ua_ref, i, first, last)
            gp, gc, gn = _grid_row(ug_ref, i, first, last)
            a = ap * wa[0] + ac * wa[1] + an * wa[2] + ba
            gt = gp * wg[0] + gc * wg[1] + gn * wg[2] + bg
            dzv = dz_ref[here, :].astype(F32)
            s = _sigmoid(gt)
            silu = gt * s
            da = dzv * silu
            dg = (dzv * a) * (s + silu * (1.0 - s))
            dua_ref[here, :] = da
            dug_ref[here, :] = dg
            terms = (da, da * ap, da * ac, da * an, dg, dg * gp, dg * gc, dg * gn)
            return tuple(r + _fold8(p) for r, p in zip(acc, terms))

        zero = jnp.zeros((8, 128), F32)
        acc = _over_grid_rows(n_rows, pass1, (zero,) * 8)
        for part in range(2):
            db_ref[part] += _rowsum(acc[4 * part])
            for k in range(3):
                dw_ref[part, k:k + 1, :] += _rowsum(acc[4 * part + 1 + k])

        def pass2(i, carry, first, last):
            for part, (ref, w) in enumerate(((dua_ref, wa), (dug_ref, wg))):
                dp_, dc_, dn_ = _grid_row(ref, i, first, last)
                du_ref[part, _row_ds(i), :] = (dn_ * w[0] + dc_ * w[1] + dp_ * w[2]).astype(BF16)
            return carry

        _over_grid_rows(n_rows, pass2, 0)

    col = lambda rows, part: pl.BlockSpec((rows, 128), lambda j, b: (0, part * nc + j))
    part_of_u = lambda part: pl.BlockSpec((1, t, 128), lambda j, b: (part, b, j))
    return pl.pallas_call(
        body, name=name, grid=(nc, nb),
        in_specs=[part_of_u(0), part_of_u(1), col(3, 0), col(3, 1), col(1, 0), col(1, 1),
                  pl.BlockSpec((t, 128), lambda j, b: (b, j))],
        out_specs=[pl.BlockSpec((2, t, 128), lambda j, b: (0, b, j)), pl.BlockSpec((2, 3, 128), lambda j, b: (0, 0, j)),
                   pl.BlockSpec((2, 1, 128), lambda j, b: (0, 0, j))],
        out_shape=[jax.ShapeDtypeStruct((2, nb * t, HID), BF16), jax.ShapeDtypeStruct((2, 3, HID), F32),
                   jax.ShapeDtypeStruct((2, 1, HID), F32)],
        scratch_shapes=[pltpu.VMEM((t, 128), F32), pltpu.VMEM((t, 128), F32)],
        compiler_params=_params(("parallel", "arbitrary")),
    )(u0, u0, cw, cw, cb, cb, dz)


def _sc_in_mid_fwd(hn, wg, off, cw, nb, t):
    tcol = 256
    ncol = D // tcol
    rows_sh = 3 * D // N_CHIPS

    def body(h_ref, wb_ref, wc_ref, wv_ref, cw_ref, p_ref, y_ref):
        hv = h_ref[...]
        bg = _dot(hv, wb_ref[0], _NT)
        cg = _dot(hv, wc_ref[0], _NT)
        v = _dot(hv, wv_ref[0], _NT)
        p_ref[0] = bg.astype(BF16)
        p_ref[1] = cg.astype(BF16)
        p_ref[2] = v.astype(BF16)
        cv = cg * v
        cc = _row_dn1(cv) * cw_ref[0:1, :] + cv * cw_ref[1:2, :] + _row_up1(cv) * cw_ref[2:3, :]
        y_ref[...] = (bg * cc).astype(BF16)

    def w_spec(part):
        def idx(b, j):
            n = part * D + j * tcol
            return (n // rows_sh, (off + n % rows_sh) // tcol, 0)
        return pl.BlockSpec((1, tcol, D), idx)

    return pl.pallas_call(
        body, name="sc_in_mid", grid=(nb, ncol),
        in_specs=[pl.BlockSpec((t, D), lambda b, j: (b, 0)), w_spec(0), w_spec(1), w_spec(2),
                  pl.BlockSpec((3, tcol), lambda b, j: (0, j))],
        out_specs=[pl.BlockSpec((3, t, tcol), lambda b, j: (0, b, j)), pl.BlockSpec((t, tcol), lambda b, j: (b, j))],
        out_shape=[jax.ShapeDtypeStruct((3, nb * t, D), BF16), jax.ShapeDtypeStruct((nb * t, D), BF16)],
        compiler_params=_params(("parallel", "parallel")),
    )(hn, wg, wg, wg, cw)


def _sc_mid_bwd(p, cw, dyb, nb, t):
    nc = D // 128

    def body(bg3_ref, cg3_ref, v3_ref, w_ref, dy_ref, dp_ref, dw_ref):
        bg_ref, cg_ref, v_ref = bg3_ref.at[0], cg3_ref.at[0], v3_ref.at[0]
        b = pl.program_id(1)

        @pl.when(b == 0)
        def _():
            dw_ref[...] = jnp.zeros_like(dw_ref)

        w0, w1, w2 = w_ref[0:1, :], w_ref[1:2, :], w_ref[2:3, :]
        cg, v = cg_ref[...].astype(F32), v_ref[...].astype(F32)
        cv = cg * v
        cvd = _row_dn1(cv)
        cvu = _row_up1(cv)
        cc = cvd * w0 + cv * w1 + cvu * w2
        dy = dy_ref[...].astype(F32)
        dcc = dy * bg_ref[...].astype(F32)
        dw_ref[0:1, :] += _rowsum(dcc * cvd)
        dw_ref[1:2, :] += _rowsum(dcc * cv)
        dw_ref[2:3, :] += _rowsum(dcc * cvu)
        dcv = _row_up1(dcc) * w0 + dcc * w1 + _row_dn1(dcc) * w2
        dp_ref[0] = (dy * cc).astype(BF16)
        dp_ref[1] = (dcv * v).astype(BF16)
        dp_ref[2] = (dcv * cg).astype(BF16)

    part = lambda k: pl.BlockSpec((1, t, 128), lambda j, b: (k, b, j))
    return pl.pallas_call(
        body, name="sc_mid_bwd", grid=(nc, nb),
        in_specs=[part(0), part(1), part(2), pl.BlockSpec((3, 128), lambda j, b: (0, j)),
                  pl.BlockSpec((t, 128), lambda j, b: (b, j))],
        out_specs=[pl.BlockSpec((3, t, 128), lambda j, b: (0, b, j)), pl.BlockSpec((3, 128), lambda j, b: (0, j))],
        out_shape=[jax.ShapeDtypeStruct((3, nb * t, D), BF16), jax.ShapeDtypeStruct((3, D), F32)],
        compiler_params=_params(("parallel", "arbitrary")),
    )(p, p, p, cw, dyb)


def _gla_in_proj(hn_all, w_gin, w2, b2):
    n = hn_all.shape[0]
    tm = _tile(n, 768, 128)

    def body(h_ref, w_ref, w2_ref, b2_ref, p_ref, la_ref):
        p = _dot(h_ref[...], w_ref[...], _NT)
        p_ref[...] = p
        z = _dot(p[:, 2 * KEY + 2 * D:], w2_ref[...]) + b2_ref[...]
        la_ref[...] = (jnp.minimum(z, 0.0) - jnp.log(1.0 + jnp.exp(-jnp.abs(z)))) * (1.0 / TAU)

    return pl.pallas_call(
        body, name="gla_in_proj", grid=(n // tm,),
        in_specs=[pl.BlockSpec((tm, D), lambda i: (i, 0)), pl.BlockSpec((GLA_IN_PAD, D), lambda i: (0, 0)),
                  pl.BlockSpec((128, 2 * KEY), lambda i: (0, 0)), pl.BlockSpec((1, 2 * KEY), lambda i: (0, 0))],
        out_specs=[pl.BlockSpec((tm, GLA_IN_PAD), lambda i: (i, 0)), pl.BlockSpec((tm, 2 * KEY), lambda i: (i, 0))],
        out_shape=[jax.ShapeDtypeStruct((n, GLA_IN_PAD), F32), jax.ShapeDtypeStruct((n, 2 * KEY), F32)],
        compiler_params=_params(("parallel",)),
    )(hn_all, w_gin, w2, b2)


def _gla_blocks(nb, nm, ncx):
    def main_idx(d, i):
        return jnp.clip(jnp.where(d == 0, i - ncx, nm - 1 - (i - ncx)), 0, nm - 1)

    def rowblk(d, b, i):
        cidx = jnp.where(d == 0, i, ncx - 1 - i)
        return jnp.where(i < ncx, nb * nm + b * ncx + cidx, b * nm + main_idx(d, i))

    def mainblk(d, b, i):
        return b * nm + main_idx(d, i)

    return rowblk, mainblk


def _gla_mask(d):
    row = lax.broadcasted_iota(jnp.int32, (CH, CH), 0)
    col = lax.broadcasted_iota(jnp.int32, (CH, CH), 1)
    diff = jnp.where(d == 0, row - col, col - row)
    mask = diff >= 0
    return mask, jnp.where(mask, 1.0, 0.0).astype(BF16), jnp.where(diff <= 0, 1.0, 0.0).astype(BF16)


def _tri_sum(m01, x):
    w = x.shape[1]
    hi = x.astype(BF16)
    r1 = x - hi.astype(F32)
    mid = r1.astype(BF16)
    lo = (r1 - mid.astype(F32)).astype(BF16)
    s = lax.dot_general(m01, jnp.concatenate([hi, mid, lo], axis=1), _NN, preferred_element_type=F32)
    return s[:, :w] + s[:, w:2 * w] + s[:, 2 * w:]


def _gla_chunk(q, k, g, bc):
    bl = _rowsum(g)
    eq = jnp.exp(bc)
    ek = jnp.exp(-bc)
    ed = jnp.exp(bl - bc)
    return bl, eq, ek, ed, q * Q_SCALE * eq, k * ek, k * ed


def _gla_scan_fwd(p_all, la_all, nb, t, tc):
    nm, ncx = t // CH, tc // CH
    nst = nm + ncx
    rowblk, mainblk = _gla_blocks(nb, nm, ncx)

    def body(*refs):
        ins, (o_refs, ss_refs, st_ref) = refs[:8], (refs[8:10], refs[10:12], refs[12])
        i = pl.program_id(1)

        @pl.when(i == 0)
        def _():
            st_ref[...] = jnp.zeros_like(st_ref)

        loaded = [r[...] for r in ins]
        states = [st_ref[j] for j in range(2 * HEADS)]
        outs, new_states = [[], []], []
        for d in range(2):
            q_all, k_all, v_all, g_all = loaded[4 * d:4 * d + 4]
            mask, m01, _ = _gla_mask(d)
            bc_all = _tri_sum(m01, g_all)
            for h in range(HEADS):
                ksl = slice(h * DK, (h + 1) * DK)
                v = v_all[:, h * DV:(h + 1) * DV]
                st = states[d * HEADS + h]
                bl, _, _, _, qs, ks, kd = _gla_chunk(q_all[:, ksl], k_all[:, ksl], g_all[:, ksl], bc_all[:, ksl])
                att = jnp.where(mask, _dot(qs, ks, _NT), 0.0)
                outs[d].append(_dot(qs, st, _NT) + _dot(att, v))
                new_states.append(st * jnp.exp(bl) + _dot(v, kd, _TN))
        for d in range(2):
            o_refs[d][...] = jnp.concatenate(outs[d], axis=1)
            for h in range(HEADS):
                ss_refs[d][0, 0, h] = states[d * HEADS + h]
                st_ref[d * HEADS + h] = new_states[d * HEADS + h]

    def in_specs(d):
        return [pl.BlockSpec((CH, KEY), lambda b, i: (rowblk(d, b, i), 0)),
                pl.BlockSpec((CH, KEY), lambda b, i: (rowblk(d, b, i), 1)),
                pl.BlockSpec((CH, D), lambda b, i: (rowblk(d, b, i), 1)),
                pl.BlockSpec((CH, KEY), lambda b, i: (rowblk(d, b, i), d))]

    outs = pl.pallas_call(
        body, name="gla_scan_fwd", grid=(nb, nst),
        in_specs=in_specs(0) + in_specs(1),
        out_specs=[pl.BlockSpec((CH, D), lambda b, i: (mainblk(0, b, i), 0)),
                   pl.BlockSpec((CH, D), lambda b, i: (mainblk(1, b, i), 0)),
                   pl.BlockSpec((1, 1, HEADS, DV, DK), lambda b, i: (b, i, 0, 0, 0)),
                   pl.BlockSpec((1, 1, HEADS, DV, DK), lambda b, i: (b, i, 0, 0, 0))],
        out_shape=[jax.ShapeDtypeStruct((nb * t, D), F32)] * 2
        + [jax.ShapeDtypeStruct((nb, nst, HEADS, DV, DK), F32)] * 2,
        scratch_shapes=[pltpu.VMEM((2 * HEADS, DV, DK), F32)],
        compiler_params=_params(("parallel", "arbitrary")),
    )(*([p_all, p_all, p_all, la_all] * 2))
    return outs[:2], outs[2:]


def _gla_scan_bwd(p_all, la_all, do, ss, nb, t, tc, after):
    nm, ncx = t // CH, tc // CH
    nst = nm + ncx
    ntot = nb * (t + tc)
    rowblk, mainblk = _gla_blocks(nb, nm, ncx)

    def body(*refs):
        ins, outs, dst_ref = refs[:12], refs[13:21], refs[21]
        ip = pl.program_id(1)
        i = nst - 1 - ip

        @pl.when(ip == 0)
        def _():
            dst_ref[...] = jnp.zeros_like(dst_ref)

        live = jnp.where(i >= ncx, 1.0, 0.0)
        loaded = [[r[...] for r in ins[6 * d:6 * d + 5]] for d in range(2)]
        states = [ins[6 * d + 5][0, 0, h] for d in range(2) for h in range(HEADS)]
        dstates = [dst_ref[j] for j in range(2 * HEADS)]
        results, new_dstates = [], []
        for d in range(2):
            q_all, k_all, v_all, g_all, do_all = loaded[d]
            do_all = do_all * live
            mask, m01, m01_t = _gla_mask(d)
            bc_all = _tri_sum(m01, g_all)
            dqs_l, dks_l, dvs_l, dbs_l, dbls_l = [], [], [], [], []
            for h in range(HEADS):
                ksl = slice(h * DK, (h + 1) * DK)
                vsl = slice(h * DV, (h + 1) * DV)
                bl, eq, ek, ed, qs, ks, kd = _gla_chunk(q_all[:, ksl], k_all[:, ksl], g_all[:, ksl], bc_all[:, ksl])
                st, dst, v, dov = states[d * HEADS + h], dstates[d * HEADS + h], v_all[:, vsl], do_all[:, vsl]
                att = jnp.where(mask, _dot(qs, ks, _NT), 0.0)
                datt = jnp.where(mask, _dot(dov, v, _NT), 0.0)
                dqs = _dot(dov, st) + _dot(datt, ks)
                dks = _dot(datt, qs, _TN)
                dvs_l.append(_dot(att, dov, _TN) + _dot(kd, dst, _NT))
                dkd = _dot(v, dst)
                e = jnp.exp(bl)
                dbls_l.append(e * _rowsum(st * dst) + _rowsum(dkd * kd))
                new_dstates.append(_dot(dov, qs, _TN) + dst * e)
                dqs_l.append(dqs * eq * Q_SCALE)
                dks_l.append(dks * ek + dkd * ed)
                dbs_l.append(dqs * qs - dks * ks - dkd * kd)
            results.append((jnp.concatenate(dqs_l, axis=1), jnp.concatenate(dks_l, axis=1),
                            jnp.concatenate(dvs_l, axis=1),
                            _tri_sum(m01_t, jnp.concatenate(dbs_l, axis=1)) + jnp.concatenate(dbls_l, axis=1)))
        for d in range(2):
            for k in range(4):
                outs[4 * d + k][...] = results[d][k]
        for j in range(2 * HEADS):
            dst_ref[j] = new_dstates[j]

    def in_specs(d):
        return [pl.BlockSpec((CH, KEY), lambda b, ip: (rowblk(d, b, nst - 1 - ip), 0)),
                pl.BlockSpec((CH, KEY), lambda b, ip: (rowblk(d, b, nst - 1 - ip), 1)),
                pl.BlockSpec((CH, D), lambda b, ip: (rowblk(d, b, nst - 1 - ip), 1)),
                pl.BlockSpec((CH, KEY), lambda b, ip: (rowblk(d, b, nst - 1 - ip), d)),
                pl.BlockSpec((CH, D), lambda b, ip: (mainblk(d, b, nst - 1 - ip), 0)),
                pl.BlockSpec((1, 1, HEADS, DV, DK), lambda b, ip: (b, nst - 1 - ip, 0, 0, 0))]

    def out_specs(d):
        row = lambda width: pl.BlockSpec((CH, width), lambda b, ip: (rowblk(d, b, nst - 1 - ip), 0))
        return [row(KEY), row(KEY), row(D), row(KEY)]

    shapes = [jax.ShapeDtypeStruct((ntot, KEY), F32), jax.ShapeDtypeStruct((ntot, KEY), F32),
              jax.ShapeDtypeStruct((ntot, D), F32), jax.ShapeDtypeStruct((ntot, KEY), F32)]
    outs = pl.pallas_call(
        body, name="gla_scan_bwd", grid=(nb, nst),
        in_specs=in_specs(0) + in_specs(1) + [pl.BlockSpec(memory_space=pl.ANY)],
        out_specs=out_specs(0) + out_specs(1),
        out_shape=shapes * 2,
        scratch_shapes=[pltpu.VMEM((2 * HEADS, DV, DK), F32)],
        compiler_params=_params(("parallel", "arbitrary")),
    )(p_all, p_all, p_all, la_all, do, ss[0], p_all, p_all, p_all, la_all, do, ss[1], after)
    return [[outs[k], outs[4 + k]] for k in range(4)]


def _gla_post_fwd(o2, p_all, head_gain, n):
    tt = _tile(n, 256)

    def body(of_ref, ob_ref, g_ref, hg_ref, y_ref):
        o = of_ref[...] + ob_ref[...]
        gv = g_ref[...]
        hg = hg_ref[...]
        for h in range(HEADS):
            oh = o[:, h * DV:(h + 1) * DV]
            r = lax.rsqrt(jnp.mean(oh * oh, axis=-1, keepdims=True) + EPS)
            y_ref[:, h * DV:(h + 1) * DV] = ((oh * r) * hg * _silu(gv[:, h * DV:(h + 1) * DV])).astype(BF16)

    row = pl.BlockSpec((tt, D), lambda i: (i, 0))
    return pl.pallas_call(
        body, name="gla_post_fwd", grid=(n // tt,),
        in_specs=[row, row, pl.BlockSpec((tt, D), lambda i: (i, 2)), pl.BlockSpec((1, DV), lambda i: (0, 0))],
        out_specs=row,
        out_shape=jax.ShapeDtypeStruct((n, D), BF16),
        compiler_params=_params(("parallel",)),
    )(o2[0], o2[1], p_all, head_gain)


def _gla_out_dx_post_bwd(dy, wg, off, o2, p_all, head_gain, n):
    tt = _tile(n, 256)

    def body(dy_ref, w_ref, of_ref, ob_ref, g_ref, hg_ref, do_ref, dg_ref, dhg_ref):
        i = pl.program_id(0)

        @pl.when(i == 0)
        def _():
            dhg_ref[...] = jnp.zeros_like(dhg_ref)

        dyv = dy_ref[...]
        o = of_ref[...] + ob_ref[...]
        gv = g_ref[...]
        hg = hg_ref[...]
        acc = jnp.zeros((1, DV), F32)
        for h in range(HEADS):
            sl = slice(h * DV, (h + 1) * DV)
            dyh = _dot(dyv, w_ref[h], _NT)
            oh = o[:, sl]
            r = lax.rsqrt(jnp.mean(oh * oh, axis=-1, keepdims=True) + EPS)
            on = oh * r
            gh = gv[:, sl]
            dg_ref[:, sl] = dyh * (on * hg) * _dsilu(gh)
            dog = dyh * _silu(gh)
            acc = acc + _rowsum(dog * on)
            don = dog * hg
            do_ref[:, sl] = r * (don - on * jnp.mean(don * on, axis=-1, keepdims=True))
        dhg_ref[...] += acc

    row = pl.BlockSpec((tt, D), lambda i: (i, 0))
    return pl.pallas_call(
        body, name="gla_out_dx_post_bwd", grid=(n // tt,),
        in_specs=[row, pl.BlockSpec((N_CHIPS, DV, D), lambda i: (0, off // DV, 0)), row, row,
                  pl.BlockSpec((tt, D), lambda i: (i, 2)), pl.BlockSpec((1, DV), lambda i: (0, 0))],
        out_specs=[row, row, pl.BlockSpec((1, DV), lambda i: (0, 0))],
        out_shape=[jax.ShapeDtypeStruct((n, D), F32), jax.ShapeDtypeStruct((n, D), F32),
                   jax.ShapeDtypeStruct((1, DV), F32)],
        compiler_params=_params(("arbitrary",)),
    )(dy, wg, o2[0], o2[1], p_all, head_gain)


def _gla_in_dx_mod(dp, w_gin, xf, cf, dh_out, gain, scale, scale_ctx, t, tc):
    n, nc = xf.shape[0], cf.shape[0]
    nb = n // t
    tm = _tile(tc, 256, 16)
    nmain, tpb = n // tm, t // tm

    def body(dp_ref, w_ref, x_ref, c_ref, dho_ref, gain_ref, sc_ref, scc_ref,
             dh_ref, dsc_ref, dsh_ref, dshc_ref, dscc_ref, dgain_ref):
        i = pl.program_id(0)
        is_main = i < nmain

        @pl.when(i == 0)
        def _():
            dgain_ref[...] = jnp.zeros_like(dgain_ref)
            dshc_ref[...] = jnp.zeros_like(dshc_ref)
            dscc_ref[...] = jnp.zeros_like(dscc_ref)

        @pl.when(is_main & (i % tpb == 0))
        def _():
            dsc_ref[...] = jnp.zeros_like(dsc_ref)
            dsh_ref[...] = jnp.zeros_like(dsh_ref)

        g = _dot(dp_ref[...], w_ref[...])
        hv = jnp.where(is_main, x_ref[...], c_ref[...])
        sc = jnp.where(is_main, sc_ref[0], scc_ref[0])
        r = lax.rsqrt(jnp.mean(hv * hv, axis=-1, keepdims=True) + EPS)
        y = hv * r
        gain_v = gain_ref[...]
        sum_g = _rowsum(g)
        sum_gy = _rowsum(g * (y * gain_v))
        drn = g * (1.0 + sc)
        dgain_ref[...] += _rowsum(drn * y)

        @pl.when(is_main)
        def _():
            dsh_ref[0] += sum_g
            dsc_ref[0] += sum_gy
            dy = drn * gain_v
            dh_ref[...] = r * (dy - y * jnp.mean(dy * y, axis=-1, keepdims=True)) + dho_ref[...]

        @pl.when(jnp.logical_not(is_main))
        def _():
            dshc_ref[...] += sum_g
            dscc_ref[...] += sum_gy

    main_row = pl.BlockSpec((tm, D), lambda i: (jnp.minimum(i, nmain - 1), 0))
    per_b = pl.BlockSpec((1, 1, D), lambda i: (jnp.minimum(i, nmain - 1) // tpb, 0, 0))
    vec = pl.BlockSpec((1, D), lambda i: (0, 0))
    per_b_shape = jax.ShapeDtypeStruct((nb, 1, D), F32)
    vec_shape = jax.ShapeDtypeStruct((1, D), F32)
    return pl.pallas_call(
        body, name="gla_in_dx_mod", grid=((n + nc) // tm,),
        in_specs=[pl.BlockSpec((tm, GLA_IN_PAD), lambda i: (i, 0)), pl.BlockSpec((GLA_IN_PAD, D), lambda i: (0, 0)),
                  main_row, pl.BlockSpec((tm, D), lambda i: (jnp.maximum(i - nmain, 0), 0)), main_row, vec, per_b,
                  pl.BlockSpec((1, 1, D), lambda i: (0, 0, 0))],
        out_specs=[main_row, per_b, per_b, vec, vec, vec],
        out_shape=[jax.ShapeDtypeStruct((n, D), F32), per_b_shape, per_b_shape, vec_shape, vec_shape, vec_shape],
        compiler_params=_params(("arbitrary",)),
    )(dp, w_gin, xf, cf, dh_out, gain, scale, scale_ctx)


def _gla_assemble(p_all, w2, b2, dq, dk, dv, dla, dgate, n):
    ntot = p_all.shape[0]
    tt = _tile(n, 128)
    nmain = n // tt
    assert ntot % tt == 0

    def body(a_ref, w_ref, b_ref, dqf_ref, dqb_ref, dkf_ref, dkb_ref, dvf_ref, dvb_ref, dlf_ref, dlb_ref, dg_ref,
             dp_ref, dw_ref, db_ref):
        i = pl.program_id(0)

        @pl.when(i == 0)
        def _():
            dw_ref[...] = jnp.zeros_like(dw_ref)
            db_ref[...] = jnp.zeros_like(db_ref)

        a = a_ref[...]
        w = w_ref[...]
        z = _dot(a, w) + b_ref[...]
        dla = jnp.concatenate([dlf_ref[...], dlb_ref[...]], axis=1)
        dz = dla * (1.0 / (1.0 + jnp.exp(z))) * (1.0 / TAU)
        dw_ref[...] += _dot(a, dz, _TN)
        db_ref[...] += _rowsum(dz)
        dp_ref[:, 0:KEY] = (dqf_ref[...] + dqb_ref[...]).astype(BF16)
        dp_ref[:, KEY:2 * KEY] = (dkf_ref[...] + dkb_ref[...]).astype(BF16)
        dp_ref[:, 2 * KEY:2 * KEY + D] = (dvf_ref[...] + dvb_ref[...]).astype(BF16)
        dp_ref[:, 2 * KEY + D:2 * KEY + 2 * D] = (dg_ref[...] * jnp.where(i < nmain, 1.0, 0.0)).astype(BF16)
        dp_ref[:, 2 * KEY + 2 * D:GLA_IN_PAD] = _dot(dz, w, _NT).astype(BF16)

    row = lambda width: pl.BlockSpec((tt, width), lambda i: (i, 0))
    return pl.pallas_call(
        body, name="gla_assemble", grid=(ntot // tt,),
        in_specs=[pl.BlockSpec((tt, 128), lambda i: (i, (2 * KEY + 2 * D) // 128)),
                  pl.BlockSpec((128, 2 * KEY), lambda i: (0, 0)), pl.BlockSpec((1, 2 * KEY), lambda i: (0, 0)),
                  row(KEY), row(KEY), row(KEY), row(KEY), row(D), row(D), row(KEY), row(KEY),
                  pl.BlockSpec((tt, D), lambda i: (jnp.minimum(i, nmain - 1), 0))],
        out_specs=[pl.BlockSpec((tt, GLA_IN_PAD), lambda i: (i, 0)), pl.BlockSpec((128, 2 * KEY), lambda i: (0, 0)),
                   pl.BlockSpec((1, 2 * KEY), lambda i: (0, 0))],
        out_shape=[jax.ShapeDtypeStruct((ntot, GLA_IN_PAD), BF16), jax.ShapeDtypeStruct((128, 2 * KEY), F32),
                   jax.ShapeDtypeStruct((1, 2 * KEY), F32)],
        compiler_params=_params(("arbitrary",)),
    )(p_all, w2, b2, dq[0], dq[1], dk[0], dk[1], dv[0], dv[1], dla[0], dla[1], dgate)


ADA_ROWS = 24
ADA_SH = N_MOD * D // N_CHIPS


def _ada_fwd(cvec, ada_w, ada_b_sh):
    def body(c_ref, w_ref, b_ref, o_ref):
        o_ref[0] = _dot(_silu(c_ref[...]), w_ref[0]) + b_ref[0]

    return pl.pallas_call(
        body, name="ada_fwd", grid=(2,),
        in_specs=[pl.BlockSpec((ADA_ROWS, D), lambda l: (0, 0)), pl.BlockSpec((1, D, ADA_SH), lambda l: (l, 0, 0)),
                  pl.BlockSpec((1, 1, ADA_SH), lambda l: (l, 0, 0))],
        out_specs=pl.BlockSpec((1, ADA_ROWS, ADA_SH), lambda l: (l, 0, 0)),
        out_shape=jax.ShapeDtypeStruct((2, ADA_ROWS, ADA_SH), F32),
        compiler_params=_params(("parallel",)),
    )(cvec, ada_w, ada_b_sh)


def _ada_bwd(cvec, ada_w, dmod_sh):
    def body(c_ref, w_ref, dm_ref, gw_ref, dc_ref):
        dm = dm_ref[0]
        gw_ref[0] = _dot(_silu(c_ref[...]), dm, _TN)
        dc_ref[0] = _dot(dm, w_ref[0], _NT)

    return pl.pallas_call(
        body, name="ada_bwd", grid=(2,),
        in_specs=[pl.BlockSpec((ADA_ROWS, D), lambda l: (0, 0)), pl.BlockSpec((1, D, ADA_SH), lambda l: (l, 0, 0)),
                  pl.BlockSpec((1, ADA_ROWS, ADA_SH), lambda l: (l, 0, 0))],
        out_specs=[pl.BlockSpec((1, D, ADA_SH), lambda l: (l, 0, 0)), pl.BlockSpec((1, ADA_ROWS, D), lambda l: (l, 0, 0))],
        out_shape=[jax.ShapeDtypeStruct((2, D, ADA_SH), F32), jax.ShapeDtypeStruct((2, ADA_ROWS, D), F32)],
        compiler_params=_params(("parallel",)),
    )(cvec, ada_w, dmod_sh)


def _sum_slots(x, name):
    s, r, _ = x.shape

    def body(x_ref, o_ref):
        acc = x_ref[0]
        for k in range(1, s):
            acc = acc + x_ref[k]
        o_ref[...] = acc

    return pl.pallas_call(
        body, name=name, out_shape=jax.ShapeDtypeStruct((r, 128), F32),
        in_specs=[pl.BlockSpec(memory_space=pltpu.VMEM)], out_specs=pl.BlockSpec(memory_space=pltpu.VMEM),
    )(x)


def _cctx_grad(dscc_parts, c_ctx):
    def body(p_ref, c_ref, o_ref):
        acc = p_ref[0]
        for k in range(1, N_CHIPS):
            acc = acc + p_ref[k]
        o_ref[...] = acc * _dsilu(c_ref[...])

    return pl.pallas_call(
        body, name="cctx_grad", out_shape=jax.ShapeDtypeStruct((8, 128), F32),
        in_specs=[pl.BlockSpec(memory_space=pltpu.VMEM)] * 2, out_specs=pl.BlockSpec(memory_space=pltpu.VMEM),
    )(dscc_parts, c_ctx)


def _adamw(w, g, m, v, name, after):
    nl, r, cdim = w.shape
    tr = _tile(r, 256)
    c1 = 1.0 - ADAM_B1 ** ADAM_STEP
    c2 = 1.0 - ADAM_B2 ** ADAM_STEP

    def body(w_ref, g_ref, m_ref, v_ref, after_ref, d_ref, mo_ref, vo_ref):
        gv = g_ref[...]
        mn = ADAM_B1 * m_ref[...] + (1.0 - ADAM_B1) * gv
        vn = ADAM_B2 * v_ref[...] + (1.0 - ADAM_B2) * (gv * gv)
        mo_ref[...] = mn
        vo_ref[...] = vn
        d_ref[...] = -ADAM_LR * ((mn / c1) / (jnp.sqrt(vn / c2) + ADAM_EPS) + ADAM_WD * w_ref[...])

    spec = pl.BlockSpec((1, tr, cdim), lambda l, i: (l, i, 0))
    sds = jax.ShapeDtypeStruct((nl, r, cdim), F32)
    return pl.pallas_call(
        body, name=name, grid=(nl, r // tr), in_specs=[spec] * 4 + [pl.BlockSpec(memory_space=pl.ANY)],
        out_specs=[spec] * 3, out_shape=[sds] * 3, compiler_params=_params(("parallel", "parallel")),
    )(w, g, m, v, after)


def _place():
    x, y, c = lax.axis_index("x"), lax.axis_index("y"), lax.axis_index("c")
    return x, y, c


def _allgather_small(blk, name):
    m_per, n = blk.shape

    def body(x_ref, out_ref, send_sems, recv_sems, local_sem):
        x, y, c = _place()
        me, sibling = (x, y, c), (x, y, 1 - c)
        chips = [(1 - x, y), (x, 1 - y), (1 - x, 1 - y)]

        def rows(px, py, pc):
            return out_ref.at[pl.ds((4 * px + 2 * py + pc) * m_per, m_per), :]

        def copy(k, block, to, src=None):
            return pltpu.make_async_remote_copy(
                src_ref=rows(*block) if src is None else src, dst_ref=rows(*block),
                send_sem=send_sems.at[k], recv_sem=recv_sems.at[k], device_id=to, device_id_type=MESH)

        mine = pltpu.make_async_copy(x_ref, rows(*me), local_sem)
        mine.start()
        first = [copy(0, me, sibling, src=x_ref)]
        first += [copy(1 + j, me, (*chip, c), src=x_ref) for j, chip in enumerate(chips)]
        for cp in first:
            cp.start()
        passed = [copy(4 + j, (*chip, c), sibling) for j, chip in enumerate(chips)]
        for j, chip in enumerate(chips):
            copy(1 + j, (*chip, c), me).wait_recv()
            passed[j].start()
        copy(0, sibling, me).wait_recv()
        for j, chip in enumerate(chips):
            copy(4 + j, (*chip, 1 - c), me).wait_recv()
        for cp in first + passed:
            cp.wait_send()
        mine.wait()

    return pl.pallas_call(
        body, name=name,
        out_shape=jax.ShapeDtypeStruct((N_DEV * m_per, n), blk.dtype),
        in_specs=[pl.BlockSpec(memory_space=pltpu.VMEM)],
        out_specs=pl.BlockSpec(memory_space=pltpu.VMEM),
        scratch_shapes=[pltpu.SemaphoreType.DMA((7,)), pltpu.SemaphoreType.DMA((7,)), pltpu.SemaphoreType.DMA],
    )(blk)


def _other_chips(x, y):
    return [(1 - x, y), (x, 1 - y), (1 - x, 1 - y)]


_HBM_SPEC = pl.BlockSpec(memory_space=pltpu.HBM)
_SEM_SPEC = pl.BlockSpec(memory_space=pltpu.SEMAPHORE)
_SPLIT_PARAMS = pltpu.CompilerParams(has_side_effects=pltpu.SideEffectType.DATAFLOW_SIDE_EFFECTING)


def _in_hbm(a):
    return pltpu.with_memory_space_constraint(a, pltpu.HBM)


def _ag_copies(own_ref, land_ref, send_sems, recv_sems):
    x, y, c = _place()
    chip = 2 * x + y
    hr = own_ref.shape[0] // 2

    def half(ch):
        return land_ref.at[ch, pl.ds(c * hr, hr), :]

    def copy(k, src, dst, to):
        return pltpu.make_async_remote_copy(src_ref=src, dst_ref=dst, send_sem=send_sems.at[k],
                                            recv_sem=recv_sems.at[k], device_id=to, device_id_type=MESH)

    sends, expects = [], []
    for j, (ox, oy) in enumerate(_other_chips(x, y)):
        sends.append(copy(j, own_ref.at[pl.ds(c * hr, hr), :], half(chip), (ox, oy, c)))
        expects.append(copy(j, half(2 * ox + oy), half(2 * ox + oy), (ox, oy, c)))
    own_slot = copy(3, own_ref, land_ref.at[chip], (x, y, 1 - c))
    return sends + [own_slot], expects + [own_slot]


def _sc_copies(p_ref, land_ref, send_sems, recv_sems):
    x, y, c = _place()
    chip = 2 * x + y
    sends, expects = [], []
    for j, (ox, oy) in enumerate(_other_chips(x, y)):
        och = 2 * ox + oy
        mk = lambda dst_slot: pltpu.make_async_remote_copy(
            src_ref=p_ref.at[och], dst_ref=land_ref.at[dst_slot], send_sem=send_sems.at[j],
            recv_sem=recv_sems.at[j], device_id=(ox, oy, c), device_id_type=MESH)
        sends.append(mk(chip))
        expects.append(mk(och))
    return sends, expects


def _pe_copies(g_ref, land_ref, send_sems, recv_sems):
    x, y, c = _place()
    hr = g_ref.shape[1] // 2
    cp = pltpu.make_async_remote_copy(
        src_ref=g_ref.at[:, pl.ds((1 - c) * hr, hr), :], dst_ref=land_ref, send_sem=send_sems.at[0],
        recv_sem=recv_sems.at[0], device_id=(x, y, 1 - c), device_id_type=MESH)
    return [cp], [cp]


def _pass_on_copies(unused_ref, land_ref, send_sems, recv_sems):
    x, y, c = _place()
    hr = land_ref.shape[1] // 2
    sends, expects = [], []
    for j, (ox, oy) in enumerate(_other_chips(x, y)):
        def mk(cc, j=j, och=2 * ox + oy):
            ref = land_ref.at[och, pl.ds(cc * hr, hr), :]
            return pltpu.make_async_remote_copy(src_ref=ref, dst_ref=ref, send_sem=send_sems.at[j],
                                                recv_sem=recv_sems.at[j], device_id=(x, y, 1 - c),
                                                device_id_type=MESH)
        sends.append(mk(c))
        expects.append(mk(1 - c))
    return sends, expects


def _pair_gather_copies(unused_ref, land_ref, send_sems, recv_sems):
    x, y, c = _place()
    hr = land_ref.shape[0] // 2

    def mk(cc):
        ref = land_ref.at[pl.ds(cc * hr, hr), :]
        return pltpu.make_async_remote_copy(src_ref=ref, dst_ref=ref, send_sem=send_sems.at[0],
                                            recv_sem=recv_sems.at[0], device_id=(x, y, 1 - c), device_id_type=MESH)
    return [mk(c)], [mk(1 - c)]


def _split_start(src, land, copies, n_copies, after, name):
    def body(src_ref, land_ref, after_ref, send_sems, recv_sems, src_thru, land_thru, token):
        for cp in copies(src_ref, land_ref, send_sems, recv_sems)[0]:
            cp.start()
        token[...] = jnp.zeros_like(token)

    if isinstance(land, tuple):
        land = lax.empty(land, src.dtype)
    land_shape = land.shape
    return pl.pallas_call(
        body, name=name,
        out_shape=(pltpu.SemaphoreType.DMA((n_copies,)), pltpu.SemaphoreType.DMA((n_copies,)),
                   pltpu.HBM(src.shape, src.dtype), pltpu.HBM(land_shape, land.dtype),
                   jax.ShapeDtypeStruct((8, 128), F32)),
        in_specs=(_HBM_SPEC, _HBM_SPEC, pl.BlockSpec(memory_space=pl.ANY)),
        out_specs=(_SEM_SPEC, _SEM_SPEC, _HBM_SPEC, _HBM_SPEC, pl.BlockSpec(memory_space=pltpu.VMEM)),
        input_output_aliases={0: 2, 1: 3}, compiler_params=_SPLIT_PARAMS,
    )(_in_hbm(src), _in_hbm(land), after)


def _split_wait(started, after, copies, name):
    send_sems, recv_sems, src_thru, land_thru, _ = started

    def body(src_ref, land_ref, send_sems, recv_sems, after_ref, src_dead, got_ref):
        sends, expects = copies(src_ref, land_ref, send_sems, recv_sems)
        for cp in sends:
            cp.wait_send()
        for cp in expects:
            cp.wait_recv()

    return pl.pallas_call(
        body, name=name,
        out_shape=(pltpu.HBM(src_thru.shape, src_thru.dtype), pltpu.HBM(land_thru.shape, land_thru.dtype)),
        in_specs=(_HBM_SPEC, _HBM_SPEC, _SEM_SPEC, _SEM_SPEC, pl.BlockSpec(memory_space=pl.ANY)),
        out_specs=(_HBM_SPEC, _HBM_SPEC), input_output_aliases={0: 0, 1: 1}, compiler_params=_SPLIT_PARAMS,
    )(src_thru, land_thru, send_sems, recv_sems, after)


def _ag_pass_on(land, name):
    hr = land.shape[1] // 2

    def body(in_ref, out_ref, send_sems, recv_sems):
        x, y, c = _place()

        def copy(j, ox, oy, cc):
            ref = out_ref.at[2 * ox + oy, pl.ds(cc * hr, hr), :]
            return pltpu.make_async_remote_copy(src_ref=ref, dst_ref=ref, send_sem=send_sems.at[j],
                                                recv_sem=recv_sems.at[j], device_id=(x, y, 1 - c),
                                                device_id_type=MESH)

        others = _other_chips(x, y)
        for j, (ox, oy) in enumerate(others):
            copy(j, ox, oy, c).start()
        for j, (ox, oy) in enumerate(others):
            copy(j, ox, oy, 1 - c).wait_recv()
        for j, (ox, oy) in enumerate(others):
            copy(j, ox, oy, c).wait_send()

    any_spec = pl.BlockSpec(memory_space=pl.ANY)
    return pl.pallas_call(
        body, name=name, out_shape=jax.ShapeDtypeStruct(land.shape, land.dtype),
        in_specs=[any_spec], out_specs=any_spec, input_output_aliases={0: 0},
        scratch_shapes=[pltpu.SemaphoreType.DMA((3,)), pltpu.SemaphoreType.DMA((3,))],
    )(land)


def _rs_pair_exchange(g, after, name):
    r = g.shape[1]
    hr = r // 2

    def body(g_ref, after_ref, got_ref, send_sem, recv_sem):
        x, y, c = _place()
        cp = pltpu.make_async_remote_copy(
            src_ref=g_ref.at[:, pl.ds((1 - c) * hr, hr), :], dst_ref=got_ref, send_sem=send_sem, recv_sem=recv_sem,
            device_id=(x, y, 1 - c), device_id_type=MESH)
        cp.start()
        cp.wait()

    any_spec = pl.BlockSpec(memory_space=pl.ANY)
    return pl.pallas_call(
        body, name=name,
        out_shape=jax.ShapeDtypeStruct((N_CHIPS, hr, D), F32),
        in_specs=[any_spec, any_spec], out_specs=any_spec,
        scratch_shapes=[pltpu.SemaphoreType.DMA, pltpu.SemaphoreType.DMA],
    )(g, after)


def _rs_chip_sum(place, g, got, name):
    r = g.shape[1]
    hr = r // 2
    tr = _tile(hr, 640, 16)
    nt = hr // tr

    def body(pl_ref, g_ref, got_ref, p16_ref, p32_ref):
        s = pl.program_id(1)
        p = g_ref[0] + got_ref[0]
        p16_ref[0] = p.astype(BF16)

        @pl.when(s == pl_ref[1])
        def _():
            p32_ref[...] = p

    return pl.pallas_call(
        body, name=name,
        grid_spec=pltpu.PrefetchScalarGridSpec(
            num_scalar_prefetch=1, grid=(nt, N_CHIPS),
            in_specs=[pl.BlockSpec((1, tr, D), lambda i, s, pr: (s, pr[0] * nt + i, 0)),
                      pl.BlockSpec((1, tr, D), lambda i, s, pr: (s, i, 0))],
            out_specs=[pl.BlockSpec((1, tr, D), lambda i, s, pr: (s, i, 0)),
                       pl.BlockSpec((tr, D), lambda i, s, pr: (i, 0))]),
        out_shape=[jax.ShapeDtypeStruct((N_CHIPS, hr, D), BF16), jax.ShapeDtypeStruct((hr, D), F32)],
        compiler_params=_params(("parallel", "arbitrary")),
    )(place, g, got)


def _rs_final_sum(place, parts, p32, name):
    hr = parts.shape[1]
    tr = _tile(hr, 640, 16)
    nt = hr // tr

    def body(pl_ref, a_ref, b_ref, c_ref, p32_ref, o_ref):
        o_ref[...] = ((p32_ref[...] + a_ref[0].astype(F32)) + b_ref[0].astype(F32)) + c_ref[0].astype(F32)

    def other(j):
        return pl.BlockSpec((1, tr, D), lambda i, pr: (j + jnp.where(pr[1] <= j, 1, 0), i, 0))

    return pl.pallas_call(
        body, name=name,
        grid_spec=pltpu.PrefetchScalarGridSpec(
            num_scalar_prefetch=1, grid=(nt,),
            in_specs=[other(0), other(1), other(2), pl.BlockSpec((tr, D), lambda i, pr: (i, 0))],
            out_specs=pl.BlockSpec((tr, D), lambda i, pr: (pr[0] * nt + i, 0))),
        out_shape=jax.ShapeDtypeStruct((2 * hr, D), F32),
        compiler_params=_params(("parallel",)),
    )(place, parts, parts, parts, p32)


def _rs_pair_gather(both, name):
    hr = both.shape[0] // 2

    def body(in_ref, out_ref, send_sem, recv_sem):
        x, y, c = _place()
        mine = out_ref.at[pl.ds(c * hr, hr), :]
        cp = pltpu.make_async_remote_copy(
            src_ref=mine, dst_ref=mine, send_sem=send_sem, recv_sem=recv_sem,
            device_id=(x, y, 1 - c), device_id_type=MESH)
        cp.start()
        theirs = out_ref.at[pl.ds((1 - c) * hr, hr), :]
        pltpu.make_async_remote_copy(
            src_ref=theirs, dst_ref=theirs, send_sem=send_sem, recv_sem=recv_sem,
            device_id=(x, y, 1 - c), device_id_type=MESH).wait_recv()
        cp.wait_send()

    any_spec = pl.BlockSpec(memory_space=pl.ANY)
    return pl.pallas_call(
        body, name=name,
        out_shape=jax.ShapeDtypeStruct(both.shape, F32),
        in_specs=[any_spec], out_specs=any_spec, input_output_aliases={0: 0},
        scratch_shapes=[pltpu.SemaphoreType.DMA, pltpu.SemaphoreType.DMA],
    )(both)


def _local_step(x, ctx, tgt, mods, mc, ag_gin, ag_main, place, small):
    nb, t, _ = x.shape
    tc = ctx.shape[1]
    n = nb * t
    nc = nb * tc
    xf = x.reshape(n, D)
    cf = ctx.reshape(nc, D)
    tf = tgt.reshape(n, D)
    vec = lambda a: a.reshape(1, -1)
    m = [[mods[l, :, k, :].reshape(nb, 1, D) for k in range(N_MOD)] for l in range(2)]
    mc_b = [jnp.broadcast_to(mc[k].reshape(1, 1, D), (nb, 1, D)) for k in range(2)]

    cw = [small["ffn_conv_w"][l] for l in range(2)]
    cb = [small["ffn_conv_b"][l].reshape(1, -1) for l in range(2)]
    w2 = jnp.zeros((128, 2 * KEY), F32)
    w2 = w2.at[0:RANK, 0:KEY].set(small["gla_w_a2"][0]).at[RANK:2 * RANK, KEY:].set(small["gla_w_a2"][1])
    b2 = small["gla_b_a"].reshape(1, 2 * KEY)
    hg = small["gla_head_norm"].reshape(1, DV)

    hn_all = _mod_fwd(xf, vec(small["norm_mix"][0]), m[0][0], m[0][1], t, "mod0_main", n + nc)
    hn_all = _mod_fwd(cf, vec(small["norm_mix"][0]), mc_b[0], mc_b[1], tc, "mod0_ctx", n + nc, into=hn_all, row0=n)
    gin = _ag_pass_on(_split_wait(ag_gin, hn_all, _ag_copies, "ag_gin_wait")[1], "ag_gin_pass_on")
    out_rows = _MAIN_ROWS["gla_out"]
    w_gin = jnp.pad(gin[:, out_rows:out_rows + _GIN_ROWS, :].reshape(GLA_IN, D), ((0, GLA_IN_PAD - GLA_IN), (0, 0)))
    p_all, la_all = _gla_in_proj(hn_all, w_gin, w2, b2)
    o2, ss = _gla_scan_fwd(p_all, la_all, nb, t, tc)
    yb0 = _gla_post_fwd(o2, p_all, hg, n)
    arrived = _split_wait(ag_main, yb0, _ag_copies, "ag_main_wait")[1]
    passing = _split_start(ag_main[4], arrived, _pass_on_copies, 3, yb0, "ag_main_pass_start")
    offs = _offsets(_MAIN, _MAIN_ROWS)
    woffs = _offsets(_WMAIN, _MAIN_ROWS)
    rows = _MAIN_ROWS

    def w_nt(a, k, name, out_dtype=BF16, tm=1024):
        return _mm_nt_w(a, wg, woffs[k], rows[k], name, tm, out_dtype)

    def w_nn_mod(a3, k, h, gate, gain, shift, scale, name):
        return _mm_nn_w_mod(a3, wg, woffs[k], rows[k], h, gate, vec(gain), shift, scale, t, name, 512)

    y0, h1, hn1 = _mm_nn_w_mod(yb0[None], gin, 0, out_rows, xf, m[0][2],
                               vec(small["norm_ffn"][0]) + passing[4][0:1, 0:1], m[0][3], m[0][4], t,
                               "gla_out_proj_mod", 512)
    wg = _split_wait(passing, hn1, _pass_on_copies, "ag_main_pass_wait")[1]
    u0, z0 = _ffn_up_mid_fwd(hn1, wg, woffs["up_t0"], cw[0], cb[0], nb, t, "ffn0_up_mid")
    f0, h2, hn2 = w_nn_mod(z0[None], "down0", h1, m[0][5], small["norm_mix"][1], m[1][0], m[1][1],
                           "ffn0_down_mod")
    p1, yb1 = _sc_in_mid_fwd(hn2, wg, woffs["sc_in_t"], small["sc_conv_w"], nb, t)
    y1, h3, hn3 = w_nn_mod(yb1[None], "sc_out", h2, m[1][2], small["norm_ffn"][1], m[1][3], m[1][4],
                           "sc_out_proj_mod")
    u1, z1 = _ffn_up_mid_fwd(hn3, wg, woffs["up_t1"], cw[1], cb[1], nb, t, "ffn1_up_mid")
    loss, dh4, df1, dm15, dfinal = _mm_nn_w_final(z1[None], wg, woffs["down1"], rows["down1"], h3, m[1][5],
                                                  vec(small["final_norm"]), tf, t, "ffn1_down_final", 512)

    gs = {}
    dmods = [[None] * N_MOD for _ in range(2)]
    dmods[1][5] = dm15

    def w_dw(a3, b, g_prev, k, name, tm):
        return _mm_dw(a3, b, g_prev, offs[k], rows[k], name, tm)

    def w_dx_mod(a3, k, h_in, dh_out, gain, scale, y_prev, gate_prev, name):
        return _mm_nn_w_modbwd(a3, wg, woffs[k], rows[k], h_in, dh_out, vec(gain), scale, y_prev, gate_prev, t,
                               name, 256)

    def ffn_bwd(l, df, u, z, hn, g_prev, h_in, dh_out, scale, y_prev, gate_prev):
        dz = w_nt(df, f"down{l}", f"ffn{l}_down_dx")
        g_acc = w_dw(z[None], df, g_prev, f"down{l}", f"ffn{l}_down_dw", 640)
        du, dcw, dcb = _ffn_mid_bwd(u, cw[l], cb[l], dz, nb, t, f"ffn{l}_mid_bwd")
        r = w_dx_mod(du, f"up_t{l}", h_in, dh_out, small["norm_ffn"][l], scale, y_prev, gate_prev,
                     f"ffn{l}_up_dx_mod")
        g_acc = w_dw(du, hn, g_acc, f"up_t{l}", f"ffn{l}_up_dw", 640)
        return r, g_acc, jnp.moveaxis(dcw, 0, 1).reshape(3, 2 * HID), dcb.reshape(2 * HID)

    r, g_acc, dcw1, dcb1 = ffn_bwd(1, df1, u1, z1, hn3, None, h3, dh4, m[1][4], y1, m[1][2])
    dh3, dmods[1][4], dmods[1][3], dnf1, dy1, dmods[1][2] = (r["dh"], r["dscale"], r["dshift"], r["dgain"],
                                                             r["dy_prev"], r["dgate_prev"])
    dyb1 = w_nt(dy1, "sc_out", "sc_out_dx")
    g_acc = w_dw(yb1[None], dy1, g_acc, "sc_out", "sc_out_dw", 256)
    dp1, dscw = _sc_mid_bwd(p1, small["sc_conv_w"], dyb1, nb, t)
    r = w_dx_mod(dp1, "sc_in_t", h2, dh3, small["norm_mix"][1], m[1][1], f0, m[0][5], "sc_in_dx_mod")
    g_acc = w_dw(dp1, hn2, g_acc, "sc_in_t", "sc_in_dw", 256)
    dh2, dmods[1][1], dmods[1][0], dnm1, df0, dmods[0][5] = (r["dh"], r["dscale"], r["dshift"], r["dgain"],
                                                             r["dy_prev"], r["dgate_prev"])
    r, g_acc, dcw0, dcb0 = ffn_bwd(0, df0, u0, z0, hn1, g_acc, h1, dh2, m[0][4], y0, m[0][2])
    dh1, dmods[0][4], dmods[0][3], dnf0, dy0, dmods[0][2] = (r["dh"], r["dscale"], r["dshift"], r["dgain"],
                                                             r["dy_prev"], r["dgate_prev"])
    g_packed = w_dw(yb0[None], dy0, g_acc, "gla_out", "gla_out_dw", 256)
    pair = _split_start(g_packed, (N_CHIPS, _MAIN_TOTAL // 2, D), _pe_copies, 1, dy0, "rs_main_pair_start")
    do, dgate, dhg = _gla_out_dx_post_bwd(dy0, gin, 0, o2, p_all, hg + pair[4][0:1, 0:1], n)
    g_packed, from_sibling = _split_wait(pair, do, _pe_copies, "rs_main_pair_wait")
    p16, p32 = _rs_chip_sum(place, g_packed, from_sibling, "rs_main_chip_sum")
    sc_main = _split_start(p16, p16.shape, _sc_copies, 3, p32, "rs_main_scatter_start")
    dq, dk, dv, dla = _gla_scan_bwd(p_all, la_all, do, ss, nb, t, tc, sc_main[4])
    dp, dw2, db2 = _gla_assemble(p_all, w2, b2, dq, dk, dv, dla, dgate, n)
    grad_x, dmods[0][1], dmods[0][0], dmc0, dmc1, dnm0 = _gla_in_dx_mod(
        dp, w_gin, xf, cf, dh1, vec(small["norm_mix"][0]), m[0][1], mc[1].reshape(1, 1, D), t, tc)
    dmc = jnp.concatenate([dmc0, dmc1], axis=0)
    landed = _split_wait(sc_main, grad_x, _sc_copies, "rs_main_scatter_wait")[1]
    g_main = _split_start(sc_main[4], _rs_final_sum(place, landed, p32, "rs_main_final_sum"),
                          _pair_gather_copies, 1, landed, "rs_main_gather_start")
    g_gin = _mm(dp, hn_all, "tn", F32, "gla_in_dw", 640, 1024)[:GLA_IN]
    g_gin = jnp.pad(g_gin.reshape(N_CHIPS, _GIN_ROWS, D), ((0, 0), (0, _GIN_PAD - _GIN_ROWS), (0, 0)))
    from_sibling = _rs_pair_exchange(g_gin, g_main[4], "rs_gin_pair_exchange")
    p16_gin, p32_gin = _rs_chip_sum(place, g_gin, from_sibling, "rs_gin_chip_sum")

    gs["norm_mix"] = jnp.concatenate([dnm0, dnm1], axis=0)
    gs["norm_ffn"] = jnp.concatenate([dnf0, dnf1], axis=0)
    gs["final_norm"] = dfinal.reshape(D)
    gs["gla_w_a2"] = jnp.stack([dw2[0:RANK, 0:KEY], dw2[RANK:2 * RANK, KEY:]])
    gs["gla_b_a"] = db2.reshape(2, KEY)
    gs["gla_head_norm"] = dhg.reshape(DV)
    gs["sc_conv_w"] = dscw
    gs["ffn_conv_w"] = jnp.stack([dcw0, dcw1])
    gs["ffn_conv_b"] = jnp.stack([dcb0, dcb1])
    dmods_arr = jnp.stack([jnp.stack([dmods[l][k].reshape(nb, D) for k in range(N_MOD)], axis=1) for l in range(2)])
    return loss, grad_x.reshape(nb, t, D), g_main, p16_gin, p32_gin, gs, dmods_arr, dmc


def _pack(arrs):
    parts, meta, off = [], [], 0
    for a in arrs:
        r = a.size // 128
        rp = -(-r // 8) * 8
        a2 = a.reshape(r, 128).astype(F32)
        if rp != r:
            a2 = jnp.pad(a2, ((0, rp - r), (0, 0)))
        parts.append(a2)
        meta.append((off, r, a.shape))
        off += rp
    return jnp.concatenate(parts, axis=0), meta


def _unpack(buf, meta, lead=()):
    return [buf[..., off:off + r, :].reshape(*lead, *shape) for off, r, shape in meta]


_MAIN = ("up_t0", "up_t1", "down0", "down1", "sc_in_t", "gla_out", "sc_out")
_WMAIN = tuple(k for k in _MAIN if k != "gla_out")
_MAIN_ROWS = {"sc_in_t": 3 * D // N_CHIPS, "up_t0": 2 * HID // N_CHIPS, "up_t1": 2 * HID // N_CHIPS,
              "gla_out": D // N_CHIPS, "sc_out": D // N_CHIPS, "down0": HID // N_CHIPS, "down1": HID // N_CHIPS}
_MAIN_TOTAL = sum(_MAIN_ROWS.values())
_GIN_ROWS = GLA_IN // N_CHIPS
_GIN_PAD = -(-_GIN_ROWS // 32) * 32
_WMAIN_TOTAL = _MAIN_TOTAL - _MAIN_ROWS["gla_out"]
_GLA_W_ROWS = _MAIN_ROWS["gla_out"] + _GIN_ROWS
_GLA_W_PAD = -(-_GLA_W_ROWS // 32) * 32


def _offsets(names, rows):
    off, out = 0, {}
    for k in names:
        out[k] = off
        off += rows[k]
    return out


def kernel(x, c, ctx, c_ctx, ada_w, ada_b, norm_mix, norm_ffn, gla_w_in, gla_w_a2, gla_b_a, gla_head_norm, gla_w_out, sc_w_in, sc_conv_w, sc_w_out, ffn_w_up, ffn_conv_w, ffn_conv_b, ffn_w_down, final_norm, loss_target, m_c_ctx, m_ada_w, m_ada_b, m_norm_mix, m_norm_ffn, m_gla_w_in, m_gla_w_a2, m_gla_b_a, m_gla_head_norm, m_gla_w_out, m_sc_w_in, m_sc_conv_w, m_sc_w_out, m_ffn_w_up, m_ffn_conv_w, m_ffn_conv_b, m_ffn_w_down, m_final_norm, v_c_ctx, v_ada_w, v_ada_b, v_norm_mix, v_norm_ffn, v_gla_w_in, v_gla_w_a2, v_gla_b_a, v_gla_head_norm, v_gla_w_out, v_sc_w_in, v_sc_conv_w, v_sc_w_out, v_ffn_w_up, v_ffn_conv_w, v_ffn_conv_b, v_ffn_w_down, v_final_norm):
    ix, iy, ic = _place()
    chip = 2 * ix + iy
    dev = 2 * chip + ic
    place = jnp.stack([ic, chip]).astype(jnp.int32)
    nb = x.shape[0]
    offs = _offsets(_MAIN, _MAIN_ROWS)

    buf, meta = _pack([c, ffn_conv_w, sc_conv_w, gla_w_a2, gla_b_a])
    got = _allgather_small(buf, "gather_small_in").reshape(N_DEV, buf.shape[0], 128)
    c_all, fcw, scw, wa2, ba = _unpack(got, meta, (N_DEV,))
    c_all = c_all.reshape(N_DEV * nb, D)
    per_chip = lambda a: a[0::2]
    ffn_conv_w_full = jnp.moveaxis(per_chip(fcw), 0, 2).reshape(2, 3, 2 * HID)
    sc_conv_w_full = jnp.moveaxis(per_chip(scw)[:, 0], 0, 1).reshape(3, D)
    gla_w_a2_full = jnp.moveaxis(per_chip(wa2)[:, 0], 0, 2).reshape(2, RANK, KEY)
    gla_b_a_full = jnp.moveaxis(per_chip(ba)[:, 0], 0, 1).reshape(2, KEY)

    cvec = jnp.concatenate([c_all, c_ctx.reshape(1, D), jnp.zeros((ADA_ROWS - N_DEV * nb - 1, D), F32)], axis=0)
    ada_b_sh = lax.dynamic_slice_in_dim(ada_b, chip * ADA_SH, ADA_SH, axis=1).reshape(2, 1, ADA_SH)
    mod_sh = _ada_fwd(cvec, ada_w, ada_b_sh)
    got = _allgather_small(mod_sh.reshape(2 * ADA_ROWS, ADA_SH), "gather_mod")
    mod_full = jnp.moveaxis(per_chip(got.reshape(N_DEV, 2, ADA_ROWS, ADA_SH)), 0, 2).reshape(2, ADA_ROWS, N_MOD * D)
    mc = mod_full[0, N_DEV * nb, :2 * D].reshape(2, D)

    own = {"sc_in_t": sc_w_in[0].T, "up_t0": ffn_w_up[0].T, "up_t1": ffn_w_up[1].T,
           "gla_out": gla_w_out[0], "sc_out": sc_w_out[0], "down0": ffn_w_down[0], "down1": ffn_w_down[1]}
    own_main = jnp.concatenate([own[k].astype(BF16) for k in _WMAIN], axis=0)
    own_gin = jnp.concatenate([own["gla_out"].astype(BF16), gla_w_in[0].T.astype(BF16),
                               jnp.zeros((_GLA_W_PAD - _GLA_W_ROWS, D), BF16)], axis=0)
    ag_gin = _split_start(own_gin, (N_CHIPS, _GLA_W_PAD, D), _ag_copies, 4, mc, "ag_gin_start")
    ag_main = _split_start(own_main, (N_CHIPS, _WMAIN_TOTAL, D), _ag_copies, 4, ag_gin[4], "ag_main_start")
    mods = lax.dynamic_slice_in_dim(mod_full, dev * nb, nb, axis=1).reshape(2, nb, N_MOD, D) + ag_main[4][0, 0]

    small = {"norm_mix": norm_mix, "norm_ffn": norm_ffn, "final_norm": final_norm, "gla_w_a2": gla_w_a2_full,
             "gla_b_a": gla_b_a_full, "gla_head_norm": gla_head_norm[0], "sc_conv_w": sc_conv_w_full,
             "ffn_conv_w": ffn_conv_w_full, "ffn_conv_b": ffn_conv_b}
    loss_p, grad_x, g_main, p16_gin, p32_gin, gs, dmods, dmc = _local_step(x, ctx, loss_target, mods, mc, ag_gin,
                                                                           ag_main, place, small)

    sum_names = ["norm_mix", "norm_ffn", "final_norm", "gla_w_a2", "gla_b_a", "gla_head_norm", "sc_conv_w",
                 "ffn_conv_w", "ffn_conv_b"]
    buf, meta = _pack([jnp.broadcast_to(loss_p, (8, 128))] + [gs[k] for k in sum_names] + [dmc, dmods])
    n_sum = meta[-1][0]
    got = _allgather_small(buf, "gather_small_grads").reshape(N_DEV, buf.shape[0], 128)
    summed = _sum_slots(got[:, :n_sum], "sum_small_grads")
    parts = _unpack(summed, meta[:-1])
    loss = parts[0][0, 0]
    g_small = dict(zip(sum_names, parts[1:-1]))
    dmc_tot = parts[-1]
    dmods_all = jnp.moveaxis(_unpack(got, meta[-1:], (N_DEV,))[0], 0, 1).reshape(2, N_DEV * nb, N_MOD * D)

    ctx_row = jnp.stack([jnp.concatenate([dmc_tot.reshape(2 * D), jnp.zeros(((N_MOD - 2) * D,), F32)]),
                         jnp.zeros((N_MOD * D,), F32)]).reshape(2, 1, N_MOD * D)
    dmod_ext = jnp.concatenate([dmods_all, ctx_row, jnp.zeros((2, ADA_ROWS - N_DEV * nb - 1, N_MOD * D), F32)], axis=1)
    g_ada_b = _sum_slots(jnp.moveaxis(dmod_ext, 1, 0).reshape(ADA_ROWS, 2 * N_MOD * D // 128, 128),
                         "sum_ada_b").reshape(2, N_MOD * D)
    dmod_sh = lax.dynamic_slice_in_dim(dmod_ext, chip * ADA_SH, ADA_SH, axis=2)
    g_ada_w, dcv = _ada_bwd(cvec, ada_w, dmod_sh)
    dscc_part = (dcv[0, N_DEV * nb] + dcv[1, N_DEV * nb]).reshape(8, 128)
    got = _allgather_small(dscc_part, "gather_dscc").reshape(N_DEV, 8, 128)
    g_c_ctx = _cctx_grad(per_chip(got), c_ctx.reshape(8, 128)).reshape(D)

    sc_gin = _split_start(p16_gin, p16_gin.shape, _sc_copies, 3, g_c_ctx, "rs_gin_scatter_start")
    g_main = _split_wait(g_main, sc_gin[4], _pair_gather_copies, "rs_main_gather_wait")[1]
    seg = {k: g_main[offs[k]:offs[k] + _MAIN_ROWS[k]] for k in _MAIN}

    sl_chip = lambda a, axis, width: lax.dynamic_slice_in_dim(a, chip * width, width, axis=axis)
    grads = {
        "c_ctx": g_c_ctx, "ada_w": g_ada_w, "ada_b": g_ada_b, "norm_mix": g_small["norm_mix"],
        "norm_ffn": g_small["norm_ffn"],
        "gla_w_a2": sl_chip(g_small["gla_w_a2"], 2, KEY // N_CHIPS)[None],
        "gla_b_a": sl_chip(g_small["gla_b_a"], 1, KEY // N_CHIPS)[None],
        "gla_head_norm": g_small["gla_head_norm"][None], "gla_w_out": seg["gla_out"][None],
        "sc_w_in": seg["sc_in_t"].T[None], "sc_conv_w": sl_chip(g_small["sc_conv_w"], 1, D // N_CHIPS)[None],
        "sc_w_out": seg["sc_out"][None], "ffn_w_up": jnp.stack([seg["up_t0"].T, seg["up_t1"].T]),
        "ffn_conv_w": sl_chip(g_small["ffn_conv_w"], 2, 2 * HID // N_CHIPS), "ffn_conv_b": g_small["ffn_conv_b"],
        "ffn_w_down": jnp.stack([seg["down0"], seg["down1"]]), "final_norm": g_small["final_norm"],
    }
    weights = {"c_ctx": c_ctx, "ada_w": ada_w, "ada_b": ada_b, "norm_mix": norm_mix, "norm_ffn": norm_ffn,
               "gla_w_in": gla_w_in, "gla_w_a2": gla_w_a2, "gla_b_a": gla_b_a, "gla_head_norm": gla_head_norm,
               "gla_w_out": gla_w_out, "sc_w_in": sc_w_in, "sc_conv_w": sc_conv_w, "sc_w_out": sc_w_out,
               "ffn_w_up": ffn_w_up, "ffn_conv_w": ffn_conv_w, "ffn_conv_b": ffn_conv_b, "ffn_w_down": ffn_w_down,
               "final_norm": final_norm}
    mom1 = {"c_ctx": m_c_ctx, "ada_w": m_ada_w, "ada_b": m_ada_b, "norm_mix": m_norm_mix, "norm_ffn": m_norm_ffn,
            "gla_w_in": m_gla_w_in, "gla_w_a2": m_gla_w_a2, "gla_b_a": m_gla_b_a, "gla_head_norm": m_gla_head_norm,
            "gla_w_out": m_gla_w_out, "sc_w_in": m_sc_w_in, "sc_conv_w": m_sc_conv_w, "sc_w_out": m_sc_w_out,
            "ffn_w_up": m_ffn_w_up, "ffn_conv_w": m_ffn_conv_w, "ffn_conv_b": m_ffn_conv_b,
            "ffn_w_down": m_ffn_w_down, "final_norm": m_final_norm}
    mom2 = {"c_ctx": v_c_ctx, "ada_w": v_ada_w, "ada_b": v_ada_b, "norm_mix": v_norm_mix, "norm_ffn": v_norm_ffn,
            "gla_w_in": v_gla_w_in, "gla_w_a2": v_gla_w_a2, "gla_b_a": v_gla_b_a, "gla_head_norm": v_gla_head_norm,
            "gla_w_out": v_gla_w_out, "sc_w_in": v_sc_w_in, "sc_conv_w": v_sc_conv_w, "sc_w_out": v_sc_w_out,
            "ffn_w_up": v_ffn_w_up, "ffn_conv_w": v_ffn_conv_w, "ffn_conv_b": v_ffn_conv_b,
            "ffn_w_down": v_ffn_w_down, "final_norm": v_final_norm}
    names = list(weights)

    big_names = ["ada_w", "gla_w_out", "sc_w_in", "sc_w_out", "ffn_w_up", "ffn_w_down", "gla_w_in"]
    small_names = [k for k in names if k not in big_names]
    delta, new_m, new_v = {}, {}, {}
    done = []

    def big_adamw(k, token):
        delta[k], new_m[k], new_v[k] = _adamw(weights[k], grads[k], mom1[k], mom2[k], "adamw_" + k, token)
        done.append(new_v[k][0, 0:1, 0:128])

    for k in big_names[:-1]:
        grads[k] = grads[k].reshape(weights[k].shape)
        big_adamw(k, sc_gin[4])
    for k in small_names:
        grads[k] = grads[k].reshape(weights[k].shape)
    packed = [_pack([src[k] for k in small_names]) for src in (weights, grads, mom1, mom2)]
    meta = packed[0][1]
    rows_pad = -packed[0][0].shape[0] % 128
    bufs = [jnp.pad(p[0], ((0, rows_pad), (0, 0)))[None] for p in packed]
    outs = _adamw(bufs[0], bufs[1], bufs[2], bufs[3], "adamw_small", sc_gin[4])
    done.append(outs[2][0, 0:1, :])
    for dst, o in zip((delta, new_m, new_v), outs):
        for k, a in zip(small_names, _unpack(o[0], meta)):
            dst[k] = a
    landed = _split_wait(sc_gin, jnp.concatenate(done, axis=0), _sc_copies, "rs_gin_scatter_wait")[1]
    g_gin_shard = _rs_pair_gather(_rs_final_sum(place, landed, p32_gin, "rs_gin_final_sum"), "rs_gin_pair_gather")
    grads["gla_w_in"] = g_gin_shard[:_GIN_ROWS].T[None]
    big_adamw("gla_w_in", sc_gin[4])

    return (loss, grad_x, *[grads[k] for k in names], *[delta[k] for k in names], *[new_m[k] for k in names],
            *[new_v[k] for k in names])
```

```python
import jax
import jax.numpy as jnp
from jax import lax
from jax.experimental import pallas as pl
from jax.experimental.pallas import tpu as pltpu

F32 = jnp.float32
BF16 = jnp.bfloat16
MESH = pl.DeviceIdType.MESH

EPS = 1e-6
D = 1024
N_MOD = 6
HEADS = 4
DK = 128
DV = 256
KEY = HEADS * DK
RANK = 16
TAU = 16.0
CH = 64
GRID_W = 64
HID = 2560
GLA_IN = 2 * KEY + 2 * D + 2 * RANK
GLA_IN_PAD = 3200
Q_SCALE = DK ** -0.5
N_CHIPS = 4
N_DEV = 8

ADAM_LR = 0.001
ADAM_B1 = 0.9
ADAM_B2 = 0.999
ADAM_EPS = 1e-08
ADAM_WD = 0.01
ADAM_STEP = 10

VMEM_LIMIT = 56 * 1024 * 1024


def _params(sem):
    return pltpu.CompilerParams(dimension_semantics=sem, vmem_limit_bytes=VMEM_LIMIT)


def _tile(n, pref, mult=8):
    if n <= pref:
        return n
    for t in range(pref - pref % mult, 0, -mult):
        if n % t == 0:
            return t
    raise ValueError((n, pref, mult))


_NN = (((1,), (0,)), ((), ()))
_NT = (((1,), (1,)), ((), ()))
_TN = (((0,), (0,)), ((), ()))


def _dot(a, b, dims=_NN):
    return lax.dot_general(a.astype(BF16), b.astype(BF16), dims, preferred_element_type=F32)


def _sigmoid(x):
    return 1.0 / (1.0 + jnp.exp(-x))


def _rowsum(x):
    return jnp.sum(x, axis=0, keepdims=True)


def _mm(a, b, form, out_dtype, name, tm, tn):
    if form == "tn":
        K, M = a.shape
    else:
        M, K = a.shape
    N = b.shape[0] if form == "nt" else b.shape[1]
    tm = _tile(M, tm, 128)
    tn = _tile(N, tn, 128)
    dims = {"nn": _NN, "nt": _NT, "tn": _TN}[form]

    def body(a_ref, b_ref, o_ref):
        o_ref[...] = _dot(a_ref[...], b_ref[...], dims).astype(o_ref.dtype)

    if form == "tn":
        a_spec = pl.BlockSpec((K, tm), lambda i, j: (0, i))
    else:
        a_spec = pl.BlockSpec((tm, K), lambda i, j: (i, 0))
    if form == "nt":
        b_spec = pl.BlockSpec((tn, K), lambda i, j: (j, 0))
    else:
        b_spec = pl.BlockSpec((K, tn), lambda i, j: (0, j))
    return pl.pallas_call(
        body,
        name=name,
        grid=(M // tm, N // tn),
        in_specs=[a_spec, b_spec],
        out_specs=pl.BlockSpec((tm, tn), lambda i, j: (i, j)),
        out_shape=jax.ShapeDtypeStruct((M, N), out_dtype),
        compiler_params=_params(("parallel", "parallel")),
    )(a, b)


def _mm_nt_w(a, wg, off, rows, name, tm, out_dtype):
    m = a.shape[0]
    tm = _tile(m, tm, 128)
    if N_CHIPS * rows <= D:

        def body_small(a_ref, w_ref, o_ref):
            av = a_ref[...]
            for s in range(N_CHIPS):
                o_ref[:, s * rows:(s + 1) * rows] = _dot(av, w_ref[s], _NT).astype(o_ref.dtype)

        return pl.pallas_call(
            body_small, name=name, grid=(m // tm,),
            in_specs=[pl.BlockSpec((tm, D), lambda i: (i, 0)),
                      pl.BlockSpec((N_CHIPS, rows, D), lambda i: (0, off // rows, 0))],
            out_specs=pl.BlockSpec((tm, N_CHIPS * rows), lambda i: (i, 0)),
            out_shape=jax.ShapeDtypeStruct((m, N_CHIPS * rows), out_dtype),
            compiler_params=_params(("parallel",)),
        )(a, wg)

    def body(a_ref, w_ref, o_ref):
        o_ref[...] = _dot(a_ref[...], w_ref[0], _NT).astype(o_ref.dtype)

    return pl.pallas_call(
        body, name=name, grid=(m // tm, N_CHIPS),
        in_specs=[pl.BlockSpec((tm, D), lambda i, s: (i, 0)),
                  pl.BlockSpec((1, rows, D), lambda i, s: (s, off // rows, 0))],
        out_specs=pl.BlockSpec((tm, rows), lambda i, s: (i, s)),
        out_shape=jax.ShapeDtypeStruct((m, N_CHIPS * rows), out_dtype),
        compiler_params=_params(("parallel", "parallel")),
    )(a, wg)


def _mm_nn_w_mod(a3, wg, off, rows, h, gate, gain, shift, scale, tpb_rows, name, tm):
    parts, m, kp = a3.shape
    assert parts * kp == N_CHIPS * rows
    tm = _tile(tpb_rows, tm, 128)
    tpb = tpb_rows // tm
    cuts = sorted({s * rows for s in range(N_CHIPS + 1)} | {p * kp for p in range(parts + 1)})
    pieces = [(k0 // kp, k0 % kp, k0 // rows, k0 % rows, k1 - k0) for k0, k1 in zip(cuts[:-1], cuts[1:])]

    def body(a_ref, w_ref, h_ref, gate_ref, gain_ref, sh_ref, sc_ref, y_ref, hout_ref, hn_ref):
        acc = None
        for p, a0, s, r0, width in pieces:
            term = _dot(a_ref[p, :, a0:a0 + width], w_ref[s, r0:r0 + width, :])
            acc = term if acc is None else acc + term
        y_ref[...] = acc
        hv = h_ref[...] + gate_ref[0] * acc
        hout_ref[...] = hv
        r = lax.rsqrt(jnp.mean(hv * hv, axis=-1, keepdims=True) + EPS)
        hn_ref[...] = ((hv * r) * gain_ref[...] * (1.0 + sc_ref[0]) + sh_ref[0]).astype(BF16)

    row = pl.BlockSpec((tm, D), lambda i: (i, 0))
    per_b = pl.BlockSpec((1, 1, D), lambda i: (i // tpb, 0, 0))
    return pl.pallas_call(
        body, name=name, grid=(m // tm,),
        in_specs=[pl.BlockSpec((parts, tm, kp), lambda i: (0, i, 0)),
                  pl.BlockSpec((N_CHIPS, rows, D), lambda i: (0, off // rows, 0)),
                  row, per_b, pl.BlockSpec((1, D), lambda i: (0, 0)), per_b, per_b],
        out_specs=[row, row, row],
        out_shape=[jax.ShapeDtypeStruct((m, D), F32), jax.ShapeDtypeStruct((m, D), F32),
                   jax.ShapeDtypeStruct((m, D), BF16)],
        compiler_params=_params(("parallel",)),
    )(a3, wg, h, gate, gain, shift, scale)


def _mm_nn_w_final(a3, wg, off, rows, h, gate, gain, tgt, tpb_rows, name, tm):
    parts, m, kp = a3.shape
    assert parts * kp == N_CHIPS * rows
    nb = m // tpb_rows
    tm = _tile(tpb_rows, tm, 128)
    tpb = tpb_rows // tm
    cuts = sorted({s * rows for s in range(N_CHIPS + 1)} | {p * kp for p in range(parts + 1)})
    pieces = [(k0 // kp, k0 % kp, k0 // rows, k0 % rows, k1 - k0) for k0, k1 in zip(cuts[:-1], cuts[1:])]

    def body(a_ref, w_ref, h_ref, gate_ref, gain_ref, tgt_ref, loss_ref, dh_ref, df_ref, dgate_ref, dgain_ref):
        i = pl.program_id(0)

        @pl.when(i == 0)
        def _():
            loss_ref[...] = jnp.zeros_like(loss_ref)
            dgain_ref[...] = jnp.zeros_like(dgain_ref)

        @pl.when(i % tpb == 0)
        def _():
            dgate_ref[...] = jnp.zeros_like(dgate_ref)

        fv = None
        for p, a0, s, r0, width in pieces:
            term = _dot(a_ref[p, :, a0:a0 + width], w_ref[s, r0:r0 + width, :])
            fv = term if fv is None else fv + term
        gate_v = gate_ref[0]
        hv = h_ref[...] + gate_v * fv
        r = lax.rsqrt(jnp.mean(hv * hv, axis=-1, keepdims=True) + EPS)
        y = hv * r
        gain_v = gain_ref[...]
        e = y * gain_v - tgt_ref[...]
        s_ = jnp.sum(_rowsum(e * e), axis=1, keepdims=True) * (0.5 / D)
        loss_ref[...] += jnp.broadcast_to(s_, loss_ref.shape)
        dout = e * (1.0 / D)
        dgain_ref[...] += _rowsum(dout * y)
        dy = dout * gain_v
        dh = r * (dy - y * jnp.mean(dy * y, axis=-1, keepdims=True))
        dh_ref[...] = dh
        df_ref[...] = (dh * gate_v).astype(BF16)
        dgate_ref[0] += _rowsum(dh * fv)

    row = pl.BlockSpec((tm, D), lambda i: (i, 0))
    per_b = pl.BlockSpec((1, 1, D), lambda i: (i // tpb, 0, 0))
    vec = pl.BlockSpec((1, D), lambda i: (0, 0))
    return pl.pallas_call(
        body, name=name, grid=(m // tm,),
        in_specs=[pl.BlockSpec((parts, tm, kp), lambda i: (0, i, 0)),
                  pl.BlockSpec((N_CHIPS, rows, D), lambda i: (0, off // rows, 0)), row, per_b, vec, row],
        out_specs=[pl.BlockSpec((1, 128), lambda i: (0, 0)), row, row, per_b, vec],
        out_shape=[jax.ShapeDtypeStruct((1, 128), F32), jax.ShapeDtypeStruct((m, D), F32),
                   jax.ShapeDtypeStruct((m, D), BF16), jax.ShapeDtypeStruct((nb, 1, D), F32),
                   jax.ShapeDtypeStruct((1, D), F32)],
        compiler_params=_params(("arbitrary",)),
    )(a3, wg, h, gate, gain, tgt)


def _mm_nn_w_modbwd(a3, wg, off, rows, h_in, dh_out, gain, scale, y_prev, gate_prev, tpb_rows, name, tm):
    parts, m, kp = a3.shape
    assert parts * kp == N_CHIPS * rows
    nb = m // tpb_rows
    tm = _tile(tpb_rows, tm, 128)
    tpb = tpb_rows // tm
    cuts = sorted({s * rows for s in range(N_CHIPS + 1)} | {p * kp for p in range(parts + 1)})
    pieces = [(k0 // kp, k0 % kp, k0 // rows, k0 % rows, k1 - k0) for k0, k1 in zip(cuts[:-1], cuts[1:])]

    def body(a_ref, w_ref, h_ref, gain_ref, sc_ref, dho_ref, yp_ref, gp_ref,
             dh_ref, dsc_ref, dsh_ref, dgain_ref, dyp_ref, dgp_ref):
        i = pl.program_id(0)

        @pl.when(i == 0)
        def _():
            dgain_ref[...] = jnp.zeros_like(dgain_ref)

        @pl.when(i % tpb == 0)
        def _():
            dsc_ref[...] = jnp.zeros_like(dsc_ref)
            dsh_ref[...] = jnp.zeros_like(dsh_ref)
            dgp_ref[...] = jnp.zeros_like(dgp_ref)

        g = None
        for p, a0, s, r0, width in pieces:
            term = _dot(a_ref[p, :, a0:a0 + width], w_ref[s, r0:r0 + width, :])
            g = term if g is None else g + term
        hv = h_ref[...]
        r = lax.rsqrt(jnp.mean(hv * hv, axis=-1, keepdims=True) + EPS)
        y = hv * r
        gain_v = gain_ref[...]
        dsh_ref[0] += _rowsum(g)
        dsc_ref[0] += _rowsum(g * (y * gain_v))
        drn = g * (1.0 + sc_ref[0])
        dgain_ref[...] += _rowsum(drn * y)
        dy = drn * gain_v
        dh = r * (dy - y * jnp.mean(dy * y, axis=-1, keepdims=True)) + dho_ref[...]
        dh_ref[...] = dh
        dyp_ref[...] = (dh * gp_ref[0]).astype(BF16)
        dgp_ref[0] += _rowsum(dh * yp_ref[...])

    row = pl.BlockSpec((tm, D), lambda i: (i, 0))
    per_b = pl.BlockSpec((1, 1, D), lambda i: (i // tpb, 0, 0))
    vec = pl.BlockSpec((1, D), lambda i: (0, 0))
    per_b_shape = jax.ShapeDtypeStruct((nb, 1, D), F32)
    outs = pl.pallas_call(
        body, name=name, grid=(m // tm,),
        in_specs=[pl.BlockSpec((parts, tm, kp), lambda i: (0, i, 0)),
                  pl.BlockSpec((N_CHIPS, rows, D), lambda i: (0, off // rows, 0)),
                  row, vec, per_b, row, row, per_b],
        out_specs=[row, per_b, per_b, vec, row, per_b],
        out_shape=[jax.ShapeDtypeStruct((m, D), F32), per_b_shape, per_b_shape, jax.ShapeDtypeStruct((1, D), F32),
                   jax.ShapeDtypeStruct((m, D), BF16), per_b_shape],
        compiler_params=_params(("arbitrary",)),
    )(a3, wg, h_in, gain, scale, dh_out, y_prev, gate_prev)
    return dict(zip(("dh", "dscale", "dshift", "dgain", "dy_prev", "dgate_prev"), outs))


def _mm_dw(a3, b, g_prev, off, rows, name, tm):
    parts, ntok, cdim = a3.shape
    assert parts * cdim == N_CHIPS * rows and cdim % tm == 0 and rows % tm == 0 and off % tm == 0

    def body(a_ref, b_ref, *rest):
        rest[-1][0] = _dot(a_ref[0], b_ref[...], _TN)

    in_specs = [pl.BlockSpec((1, ntok, tm), lambda i: ((i * tm) // cdim, 0, ((i * tm) % cdim) // tm)),
                pl.BlockSpec((ntok, D), lambda i: (0, 0))]
    args = [a3, b]
    aliases = {}
    if g_prev is not None:
        in_specs.append(pl.BlockSpec(memory_space=pl.ANY))
        args.append(g_prev)
        aliases = {2: 0}
    return pl.pallas_call(
        body, name=name, grid=(N_CHIPS * rows // tm,),
        in_specs=in_specs,
        out_specs=pl.BlockSpec((1, tm, D), lambda i: ((i * tm) // rows, (off + (i * tm) % rows) // tm, 0)),
        out_shape=jax.ShapeDtypeStruct((N_CHIPS, _MAIN_TOTAL, D), F32),
        input_output_aliases=aliases,
        compiler_params=_params(("parallel",)),
    )(*args)


def _mod_fwd(h, gain, shift, scale, tpb_rows, name, out_rows, into=None, row0=0):
    n = h.shape[0]
    tt = _tile(tpb_rows, 256)
    tpb = tpb_rows // tt
    assert row0 % tt == 0

    def body(h_ref, gain_ref, sh_ref, sc_ref, *rest):
        hv = h_ref[...]
        r = lax.rsqrt(jnp.mean(hv * hv, axis=-1, keepdims=True) + EPS)
        rest[-1][...] = ((hv * r) * gain_ref[...] * (1.0 + sc_ref[0]) + sh_ref[0]).astype(BF16)

    per_b = pl.BlockSpec((1, 1, D), lambda i: (i // tpb, 0, 0))
    in_specs = [pl.BlockSpec((tt, D), lambda i: (i, 0)), pl.BlockSpec((1, D), lambda i: (0, 0)), per_b, per_b]
    args = (h, gain, shift, scale)
    aliases = {}
    if into is not None:
        in_specs = in_specs + [pl.BlockSpec(memory_space=pl.ANY)]
        args = args + (into,)
        aliases = {4: 0}
    return pl.pallas_call(
        body, name=name, grid=(n // tt,), in_specs=in_specs,
        out_specs=pl.BlockSpec((tt, D), lambda i: (i + row0 // tt, 0)),
        out_shape=jax.ShapeDtypeStruct((out_rows, D), BF16),
        input_output_aliases=aliases, compiler_params=_params(("parallel",)),
    )(*args)


def _row_dn1(x):
    t = lax.broadcasted_iota(jnp.int32, x.shape, 0)
    return jnp.where(t % GRID_W == 0, 0.0, pltpu.roll(x, 1, 0))


def _row_up1(x):
    t = lax.broadcasted_iota(jnp.int32, x.shape, 0)
    return jnp.where(t % GRID_W == GRID_W - 1, 0.0, pltpu.roll(x, x.shape[0] - 1, 0))


def _silu(x):
    return x * _sigmoid(x)


def _dsilu(x):
    s = _sigmoid(x)
    return s * (1.0 + x * (1.0 - s))


def _row_ds(i):
    start = i * GRID_W
    return pl.ds(start if isinstance(start, int) else pl.multiple_of(start, GRID_W), GRID_W)


def _grid_row(ref, i, first, last):
    def rows(k):
        return ref[_row_ds(k), :].astype(F32)

    cur = rows(i)
    return (jnp.zeros_like(cur) if first else rows(i - 1)), cur, (jnp.zeros_like(cur) if last else rows(i + 1))


def _over_grid_rows(n_rows, step, carry):
    carry = step(0, carry, True, n_rows == 1)
    if n_rows > 2:
        carry = lax.fori_loop(1, n_rows - 1, lambda i, c: step(i, c, False, False), carry)
    if n_rows > 1:
        carry = step(n_rows - 1, carry, False, True)
    return carry


def _fold8(p):
    return p.reshape(GRID_W // 8, 8, p.shape[1]).sum(axis=0)


def _ffn_up_mid_fwd(hn, wg, off, cw, cb, nb, t, name):
    tcol = 256
    ncol = HID // tcol
    rows_sh = 2 * HID // N_CHIPS

    def conv(x, w_ref):
        zeros = jnp.zeros((GRID_W, x.shape[1]), x.dtype)
        down = jnp.concatenate([zeros, x[: x.shape[0] - GRID_W]], axis=0)
        up = jnp.concatenate([x[GRID_W:], zeros], axis=0)
        return down * w_ref[0:1, :] + x * w_ref[1:2, :] + up * w_ref[2:3, :]

    def body(h_ref, wa_ref, wg_ref, cwa_ref, cwg_ref, cba_ref, cbg_ref, u_ref, z_ref):
        hv = h_ref[...]
        ua = _dot(hv, wa_ref[0], _NT)
        ug = _dot(hv, wg_ref[0], _NT)
        u_ref[0] = ua.astype(BF16)
        u_ref[1] = ug.astype(BF16)
        a = conv(ua, cwa_ref) + cba_ref[...]
        gt = conv(ug, cwg_ref) + cbg_ref[...]
        z_ref[...] = (a * _silu(gt)).astype(BF16)

    def w_spec(part):
        def idx(b, j):
            n = part * HID + j * tcol
            return (n // rows_sh, (off + n % rows_sh) // tcol, 0)
        return pl.BlockSpec((1, tcol, D), idx)

    chan = lambda rows, part: pl.BlockSpec((rows, tcol), lambda b, j: (0, part * ncol + j))
    return pl.pallas_call(
        body, name=name, grid=(nb, ncol),
        in_specs=[pl.BlockSpec((t, D), lambda b, j: (b, 0)), w_spec(0), w_spec(1),
                  chan(3, 0), chan(3, 1), chan(1, 0), chan(1, 1)],
        out_specs=[pl.BlockSpec((2, t, tcol), lambda b, j: (0, b, j)), pl.BlockSpec((t, tcol), lambda b, j: (b, j))],
        out_shape=[jax.ShapeDtypeStruct((2, nb * t, HID), BF16), jax.ShapeDtypeStruct((nb * t, HID), BF16)],
        compiler_params=_params(("parallel", "parallel")),
    )(hn, wg, wg, cw, cw, cb, cb)


def _ffn_mid_bwd(u0, cw, cb, dz, nb, t, name):
    nc = HID // 128
    n_rows = t // GRID_W

    def body(ua3_ref, ug3_ref, wa_ref, wg_ref, ba_ref, bg_ref, dz_ref, du_ref, dw_ref, db_ref, dua_ref, dug_ref):
        ua_ref, ug_ref = ua3_ref.at[0], ug3_ref.at[0]
        b = pl.program_id(1)

        @pl.when(b == 0)
        def _():
            dw_ref[...] = jnp.zeros_like(dw_ref)
            db_ref[...] = jnp.zeros_like(db_ref)

        wa = [wa_ref[k:k + 1, :] for k in range(3)]
        wg = [wg_ref[k:k + 1, :] for k in range(3)]
        ba, bg = ba_ref[...], bg_ref[...]

        def pass1(i, acc, first, last):
            here = _row_ds(i)
            ap, ac, an = _grid_row(ua_ref, i, first, last)
            gp, gc, gn = _grid_row(ug_ref, i, first, last)
            a = ap * wa[0] + ac * wa[1] + an * wa[2] + ba
            gt = gp * wg[0] + gc * wg[1] + gn * wg[2] + bg
            dzv = dz_ref[here, :].astype(F32)
            s = _sigmoid(gt)
            silu = gt * s
            da = dzv * silu
            dg = (dzv * a) * (s + silu * (1.0 - s))
            dua_ref[here, :] = da
            dug_ref[here, :] = dg
            terms = (da, da * ap, da * ac, da * an, dg, dg * gp, dg * gc, dg * gn)
            return tuple(r + _fold8(p) for r, p in zip(acc, terms))

        zero = jnp.zeros((8, 128), F32)
        acc = _over_grid_rows(n_rows, pass1, (zero,) * 8)
        for part in range(2):
            db_ref[part] += _rowsum(acc[4 * part])
            for k in range(3):
                dw_ref[part, k:k + 1, :] += _rowsum(acc[4 * part + 1 + k])

        def pass2(i, carry, first, last):
            for part, (ref, w) in enumerate(((dua_ref, wa), (dug_ref, wg))):
                dp_, dc_, dn_ = _grid_row(ref, i, first, last)
                du_ref[part, _row_ds(i), :] = (dn_ * w[0] + dc_ * w[1] + dp_ * w[2]).astype(BF16)
            return carry

        _over_grid_rows(n_rows, pass2, 0)

    col = lambda rows, part: pl.BlockSpec((rows, 128), lambda j, b: (0, part * nc + j))
    part_of_u = lambda part: pl.BlockSpec((1, t, 128), lambda j, b: (part, b, j))
    return pl.pallas_call(
        body, name=name, grid=(nc, nb),
        in_specs=[part_of_u(0), part_of_u(1), col(3, 0), col(3, 1), col(1, 0), col(1, 1),
                  pl.BlockSpec((t, 128), lambda j, b: (b, j))],
        out_specs=[pl.BlockSpec((2, t, 128), lambda j, b: (0, b, j)), pl.BlockSpec((2, 3, 128), lambda j, b: (0, 0, j)),
                   pl.BlockSpec((2, 1, 128), lambda j, b: (0, 0, j))],
        out_shape=[jax.ShapeDtypeStruct((2, nb * t, HID), BF16), jax.ShapeDtypeStruct((2, 3, HID), F32),
                   jax.ShapeDtypeStruct((2, 1, HID), F32)],
        scratch_shapes=[pltpu.VMEM((t, 128), F32), pltpu.VMEM((t, 128), F32)],
        compiler_params=_params(("parallel", "arbitrary")),
    )(u0, u0, cw, cw, cb, cb, dz)


def _sc_in_mid_fwd(hn, wg, off, cw, nb, t):
    tcol = 256
    ncol = D // tcol
    rows_sh = 3 * D // N_CHIPS

    def body(h_ref, wb_ref, wc_ref, wv_ref, cw_ref, p_ref, y_ref):
        hv = h_ref[...]
        bg = _dot(hv, wb_ref[0], _NT)
        cg = _dot(hv, wc_ref[0], _NT)
        v = _dot(hv, wv_ref[0], _NT)
        p_ref[0] = bg.astype(BF16)
        p_ref[1] = cg.astype(BF16)
        p_ref[2] = v.astype(BF16)
        cv = cg * v
        cc = _row_dn1(cv) * cw_ref[0:1, :] + cv * cw_ref[1:2, :] + _row_up1(cv) * cw_ref[2:3, :]
        y_ref[...] = (bg * cc).astype(BF16)

    def w_spec(part):
        def idx(b, j):
            n = part * D + j * tcol
            return (n // rows_sh, (off + n % rows_sh) // tcol, 0)
        return pl.BlockSpec((1, tcol, D), idx)

    return pl.pallas_call(
        body, name="sc_in_mid", grid=(nb, ncol),
        in_specs=[pl.BlockSpec((t, D), lambda b, j: (b, 0)), w_spec(0), w_spec(1), w_spec(2),
                  pl.BlockSpec((3, tcol), lambda b, j: (0, j))],
        out_specs=[pl.BlockSpec((3, t, tcol), lambda b, j: (0, b, j)), pl.BlockSpec((t, tcol), lambda b, j: (b, j))],
        out_shape=[jax.ShapeDtypeStruct((3, nb * t, D), BF16), jax.ShapeDtypeStruct((nb * t, D), BF16)],
        compiler_params=_params(("parallel", "parallel")),
    )(hn, wg, wg, wg, cw)


def _sc_mid_bwd(p, cw, dyb, nb, t):
    nc = D // 128

    def body(bg3_ref, cg3_ref, v3_ref, w_ref, dy_ref, dp_ref, dw_ref):
        bg_ref, cg_ref, v_ref = bg3_ref.at[0], cg3_ref.at[0], v3_ref.at[0]
        b = pl.program_id(1)

        @pl.when(b == 0)
        def _():
            dw_ref[...] = jnp.zeros_like(dw_ref)

        w0, w1, w2 = w_ref[0:1, :], w_ref[1:2, :], w_ref[2:3, :]
        cg, v = cg_ref[...].astype(F32), v_ref[...].astype(F32)
        cv = cg * v
        cvd = _row_dn1(cv)
        cvu = _row_up1(cv)
        cc = cvd * w0 + cv * w1 + cvu * w2
        dy = dy_ref[...].astype(F32)
        dcc = dy * bg_ref[...].astype(F32)
        dw_ref[0:1, :] += _rowsum(dcc * cvd)
        dw_ref[1:2, :] += _rowsum(dcc * cv)
        dw_ref[2:3, :] += _rowsum(dcc * cvu)
        dcv = _row_up1(dcc) * w0 + dcc * w1 + _row_dn1(dcc) * w2
        dp_ref[0] = (dy * cc).astype(BF16)
        dp_ref[1] = (dcv * v).astype(BF16)
        dp_ref[2] = (dcv * cg).astype(BF16)

    part = lambda k: pl.BlockSpec((1, t, 128), lambda j, b: (k, b, j))
    return pl.pallas_call(
        body, name="sc_mid_bwd", grid=(nc, nb),
        in_specs=[part(0), part(1), part(2), pl.BlockSpec((3, 128), lambda j, b: (0, j)),
                  pl.BlockSpec((t, 128), lambda j, b: (b, j))],
        out_specs=[pl.BlockSpec((3, t, 128), lambda j, b: (0, b, j)), pl.BlockSpec((3, 128), lambda j, b: (0, j))],
        out_shape=[jax.ShapeDtypeStruct((3, nb * t, D), BF16), jax.ShapeDtypeStruct((3, D), F32)],
        compiler_params=_params(("parallel", "arbitrary")),
    )(p, p, p, cw, dyb)


def _gla_in_proj(hn_all, w_gin, w2, b2):
    n = hn_all.shape[0]
    tm = _tile(n, 768, 128)

    def body(h_ref, w_ref, w2_ref, b2_ref, p_ref, la_ref):
        p = _dot(h_ref[...], w_ref[...], _NT)
        p_ref[...] = p
        z = _dot(p[:, 2 * KEY + 2 * D:], w2_ref[...]) + b2_ref[...]
        la_ref[...] = (jnp.minimum(z, 0.0) - jnp.log(1.0 + jnp.exp(-jnp.abs(z)))) * (1.0 / TAU)

    return pl.pallas_call(
        body, name="gla_in_proj", grid=(n // tm,),
        in_specs=[pl.BlockSpec((tm, D), lambda i: (i, 0)), pl.BlockSpec((GLA_IN_PAD, D), lambda i: (0, 0)),
                  pl.BlockSpec((128, 2 * KEY), lambda i: (0, 0)), pl.BlockSpec((1, 2 * KEY), lambda i: (0, 0))],
        out_specs=[pl.BlockSpec((tm, GLA_IN_PAD), lambda i: (i, 0)), pl.BlockSpec((tm, 2 * KEY), lambda i: (i, 0))],
        out_shape=[jax.ShapeDtypeStruct((n, GLA_IN_PAD), F32), jax.ShapeDtypeStruct((n, 2 * KEY), F32)],
        compiler_params=_params(("parallel",)),
    )(hn_all, w_gin, w2, b2)


def _gla_blocks(nb, nm, ncx):
    def main_idx(d, i):
        return jnp.clip(jnp.where(d == 0, i - ncx, nm - 1 - (i - ncx)), 0, nm - 1)

    def rowblk(d, b, i):
        cidx = jnp.where(d == 0, i, ncx - 1 - i)
        return jnp.where(i < ncx, nb * nm + b * ncx + cidx, b * nm + main_idx(d, i))

    def mainblk(d, b, i):
        return b * nm + main_idx(d, i)

    return rowblk, mainblk


def _gla_mask(d):
    row = lax.broadcasted_iota(jnp.int32, (CH, CH), 0)
    col = lax.broadcasted_iota(jnp.int32, (CH, CH), 1)
    diff = jnp.where(d == 0, row - col, col - row)
    mask = diff >= 0
    return mask, jnp.where(mask, 1.0, 0.0).astype(BF16), jnp.where(diff <= 0, 1.0, 0.0).astype(BF16)


def _tri_sum(m01, x):
    w = x.shape[1]
    hi = x.astype(BF16)
    r1 = x - hi.astype(F32)
    mid = r1.astype(BF16)
    lo = (r1 - mid.astype(F32)).astype(BF16)
    s = lax.dot_general(m01, jnp.concatenate([hi, mid, lo], axis=1), _NN, preferred_element_type=F32)
    return s[:, :w] + s[:, w:2 * w] + s[:, 2 * w:]


def _gla_chunk(q, k, g, bc):
    bl = _rowsum(g)
    eq = jnp.exp(bc)
    ek = jnp.exp(-bc)
    ed = jnp.exp(bl - bc)
    return bl, eq, ek, ed, q * Q_SCALE * eq, k * ek, k * ed


def _gla_scan_fwd(p_all, la_all, nb, t, tc):
    nm, ncx = t // CH, tc // CH
    nst = nm + ncx
    rowblk, mainblk = _gla_blocks(nb, nm, ncx)

    def body(*refs):
        ins, (o_refs, ss_refs, st_ref) = refs[:8], (refs[8:10], refs[10:12], refs[12])
        i = pl.program_id(1)

        @pl.when(i == 0)
        def _():
            st_ref[...] = jnp.zeros_like(st_ref)

        loaded = [r[...] for r in ins]
        states = [st_ref[j] for j in range(2 * HEADS)]
        outs, new_states = [[], []], []
        for d in range(2):
            q_all, k_all, v_all, g_all = loaded[4 * d:4 * d + 4]
            mask, m01, _ = _gla_mask(d)
            bc_all = _tri_sum(m01, g_all)
            for h in range(HEADS):
                ksl = slice(h * DK, (h + 1) * DK)
                v = v_all[:, h * DV:(h + 1) * DV]
                st = states[d * HEADS + h]
                bl, _, _, _, qs, ks, kd = _gla_chunk(q_all[:, ksl], k_all[:, ksl], g_all[:, ksl], bc_all[:, ksl])
                att = jnp.where(mask, _dot(qs, ks, _NT), 0.0)
                outs[d].append(_dot(qs, st, _NT) + _dot(att, v))
                new_states.append(st * jnp.exp(bl) + _dot(v, kd, _TN))
        for d in range(2):
            o_refs[d][...] = jnp.concatenate(outs[d], axis=1)
            for h in range(HEADS):
                ss_refs[d][0, 0, h] = states[d * HEADS + h]
                st_ref[d * HEADS + h] = new_states[d * HEADS + h]

    def in_specs(d):
        return [pl.BlockSpec((CH, KEY), lambda b, i: (rowblk(d, b, i), 0)),
                pl.BlockSpec((CH, KEY), lambda b, i: (rowblk(d, b, i), 1)),
                pl.BlockSpec((CH, D), lambda b, i: (rowblk(d, b, i), 1)),
                pl.BlockSpec((CH, KEY), lambda b, i: (rowblk(d, b, i), d))]

    outs = pl.pallas_call(
        body, name="gla_scan_fwd", grid=(nb, nst),
        in_specs=in_specs(0) + in_specs(1),
        out_specs=[pl.BlockSpec((CH, D), lambda b, i: (mainblk(0, b, i), 0)),
                   pl.BlockSpec((CH, D), lambda b, i: (mainblk(1, b, i), 0)),
                   pl.BlockSpec((1, 1, HEADS, DV, DK), lambda b, i: (b, i, 0, 0, 0)),
                   pl.BlockSpec((1, 1, HEADS, DV, DK), lambda b, i: (b, i, 0, 0, 0))],
        out_shape=[jax.ShapeDtypeStruct((nb * t, D), F32)] * 2
        + [jax.ShapeDtypeStruct((nb, nst, HEADS, DV, DK), F32)] * 2,
        scratch_shapes=[pltpu.VMEM((2 * HEADS, DV, DK), F32)],
        compiler_params=_params(("parallel", "arbitrary")),
    )(*([p_all, p_all, p_all, la_all] * 2))
    return outs[:2], outs[2:]


def _gla_scan_bwd(p_all, la_all, do, ss, nb, t, tc, after):
    nm, ncx = t // CH, tc // CH
    nst = nm + ncx
    ntot = nb * (t + tc)
    rowblk, mainblk = _gla_blocks(nb, nm, ncx)

    def body(*refs):
        ins, outs, dst_ref = refs[:12], refs[13:21], refs[21]
        ip = pl.program_id(1)
        i = nst - 1 - ip

        @pl.when(ip == 0)
        def _():
            dst_ref[...] = jnp.zeros_like(dst_ref)

        live = jnp.where(i >= ncx, 1.0, 0.0)
        loaded = [[r[...] for r in ins[6 * d:6 * d + 5]] for d in range(2)]
        states = [ins[6 * d + 5][0, 0, h] for d in range(2) for h in range(HEADS)]
        dstates = [dst_ref[j] for j in range(2 * HEADS)]
        results, new_dstates = [], []
        for d in range(2):
            q_all, k_all, v_all, g_all, do_all = loaded[d]
            do_all = do_all * live
            mask, m01, m01_t = _gla_mask(d)
            bc_all = _tri_sum(m01, g_all)
            dqs_l, dks_l, dvs_l, dbs_l, dbls_l = [], [], [], [], []
            for h in range(HEADS):
                ksl = slice(h * DK, (h + 1) * DK)
                vsl = slice(h * DV, (h + 1) * DV)
                bl, eq, ek, ed, qs, ks, kd = _gla_chunk(q_all[:, ksl], k_all[:, ksl], g_all[:, ksl], bc_all[:, ksl])
                st, dst, v, dov = states[d * HEADS + h], dstates[d * HEADS + h], v_all[:, vsl], do_all[:, vsl]
                att = jnp.where(mask, _dot(qs, ks, _NT), 0.0)
                datt = jnp.where(mask, _dot(dov, v, _NT), 0.0)
                dqs = _dot(dov, st) + _dot(datt, ks)
                dks = _dot(datt, qs, _TN)
                dvs_l.append(_dot(att, dov, _TN) + _dot(kd, dst, _NT))
                dkd = _dot(v, dst)
                e = jnp.exp(bl)
                dbls_l.append(e * _rowsum(st * dst) + _rowsum(dkd * kd))
                new_dstates.append(_dot(dov, qs, _TN) + dst * e)
                dqs_l.append(dqs * eq * Q_SCALE)
                dks_l.append(dks * ek + dkd * ed)
                dbs_l.append(dqs * qs - dks * ks - dkd * kd)
            results.append((jnp.concatenate(dqs_l, axis=1), jnp.concatenate(dks_l, axis=1),
                            jnp.concatenate(dvs_l, axis=1),
                            _tri_sum(m01_t, jnp.concatenate(dbs_l, axis=1)) + jnp.concatenate(dbls_l, axis=1)))
        for d in range(2):
            for k in range(4):
                outs[4 * d + k][...] = results[d][k].astype(outs[4 * d + k].dtype)
        for j in range(2 * HEADS):
            dst_ref[j] = new_dstates[j]

    def in_specs(d):
        return [pl.BlockSpec((CH, KEY), lambda b, ip: (rowblk(d, b, nst - 1 - ip), 0)),
                pl.BlockSpec((CH, KEY), lambda b, ip: (rowblk(d, b, nst - 1 - ip), 1)),
                pl.BlockSpec((CH, D), lambda b, ip: (rowblk(d, b, nst - 1 - ip), 1)),
                pl.BlockSpec((CH, KEY), lambda b, ip: (rowblk(d, b, nst - 1 - ip), d)),
                pl.BlockSpec((CH, D), lambda b, ip: (mainblk(d, b, nst - 1 - ip), 0)),
                pl.BlockSpec((1, 1, HEADS, DV, DK), lambda b, ip: (b, nst - 1 - ip, 0, 0, 0))]

    def out_specs(d):
        row = lambda width: pl.BlockSpec((CH, width), lambda b, ip: (rowblk(d, b, nst - 1 - ip), 0))
        return [row(KEY), row(KEY), row(D), row(KEY)]

    shapes = [jax.ShapeDtypeStruct((ntot, KEY), BF16), jax.ShapeDtypeStruct((ntot, KEY), BF16),
              jax.ShapeDtypeStruct((ntot, D), BF16), jax.ShapeDtypeStruct((ntot, KEY), F32)]
    outs = pl.pallas_call(
        body, name="gla_scan_bwd", grid=(nb, nst),
        in_specs=in_specs(0) + in_specs(1) + [pl.BlockSpec(memory_space=pl.ANY)],
        out_specs=out_specs(0) + out_specs(1),
        out_shape=shapes * 2,
        scratch_shapes=[pltpu.VMEM((2 * HEADS, DV, DK), F32)],
        compiler_params=_params(("parallel", "arbitrary")),
    )(p_all, p_all, p_all, la_all, do, ss[0], p_all, p_all, p_all, la_all, do, ss[1], after)
    return [[outs[k], outs[4 + k]] for k in range(4)]


def _gla_post_fwd(o2, p_all, head_gain, n):
    tt = _tile(n, 256)

    def body(of_ref, ob_ref, g_ref, hg_ref, y_ref):
        o = of_ref[...] + ob_ref[...]
        gv = g_ref[...]
        hg = hg_ref[...]
        for h in range(HEADS):
            oh = o[:, h * DV:(h + 1) * DV]
            r = lax.rsqrt(jnp.mean(oh * oh, axis=-1, keepdims=True) + EPS)
            y_ref[:, h * DV:(h + 1) * DV] = ((oh * r) * hg * _silu(gv[:, h * DV:(h + 1) * DV])).astype(BF16)

    row = pl.BlockSpec((tt, D), lambda i: (i, 0))
    return pl.pallas_call(
        body, name="gla_post_fwd", grid=(n // tt,),
        in_specs=[row, row, pl.BlockSpec((tt, D), lambda i: (i, 2)), pl.BlockSpec((1, DV), lambda i: (0, 0))],
        out_specs=row,
        out_shape=jax.ShapeDtypeStruct((n, D), BF16),
        compiler_params=_params(("parallel",)),
    )(o2[0], o2[1], p_all, head_gain)


def _gla_out_dx_post_bwd(dy, wg, off, o2, p_all, head_gain, n):
    tt = _tile(n, 256)

    def body(dy_ref, w_ref, of_ref, ob_ref, g_ref, hg_ref, do_ref, dg_ref, dhg_ref):
        i = pl.program_id(0)

        @pl.when(i == 0)
        def _():
            dhg_ref[...] = jnp.zeros_like(dhg_ref)

        dyv = dy_ref[...]
        o = of_ref[...] + ob_ref[...]
        gv = g_ref[...]
        hg = hg_ref[...]
        acc = jnp.zeros((1, DV), F32)
        for h in range(HEADS):
            sl = slice(h * DV, (h + 1) * DV)
            dyh = _dot(dyv, w_ref[h], _NT)
            oh = o[:, sl]
            r = lax.rsqrt(jnp.mean(oh * oh, axis=-1, keepdims=True) + EPS)
            on = oh * r
            gh = gv[:, sl]
            dg_ref[:, sl] = dyh * (on * hg) * _dsilu(gh)
            dog = dyh * _silu(gh)
            acc = acc + _rowsum(dog * on)
            don = dog * hg
            do_ref[:, sl] = r * (don - on * jnp.mean(don * on, axis=-1, keepdims=True))
        dhg_ref[...] += acc

    row = pl.BlockSpec((tt, D), lambda i: (i, 0))
    return pl.pallas_call(
        body, name="gla_out_dx_post_bwd", grid=(n // tt,),
        in_specs=[row, pl.BlockSpec((N_CHIPS, DV, D), lambda i: (0, off // DV, 0)), row, row,
                  pl.BlockSpec((tt, D), lambda i: (i, 2)), pl.BlockSpec((1, DV), lambda i: (0, 0))],
        out_specs=[row, row, pl.BlockSpec((1, DV), lambda i: (0, 0))],
        out_shape=[jax.ShapeDtypeStruct((n, D), F32), jax.ShapeDtypeStruct((n, D), F32),
                   jax.ShapeDtypeStruct((1, DV), F32)],
        compiler_params=_params(("arbitrary",)),
    )(dy, wg, o2[0], o2[1], p_all, head_gain)


def _gla_in_dx_mod(dp, w_gin, xf, cf, dh_out, gain, scale, scale_ctx, t, tc):
    n, nc = xf.shape[0], cf.shape[0]
    nb = n // t
    tm = _tile(tc, 256, 16)
    nmain, tpb = n // tm, t // tm

    def body(dp_ref, w_ref, x_ref, c_ref, dho_ref, gain_ref, sc_ref, scc_ref,
             dh_ref, dsc_ref, dsh_ref, dshc_ref, dscc_ref, dgain_ref):
        i = pl.program_id(0)
        is_main = i < nmain

        @pl.when(i == 0)
        def _():
            dgain_ref[...] = jnp.zeros_like(dgain_ref)
            dshc_ref[...] = jnp.zeros_like(dshc_ref)
            dscc_ref[...] = jnp.zeros_like(dscc_ref)

        @pl.when(is_main & (i % tpb == 0))
        def _():
            dsc_ref[...] = jnp.zeros_like(dsc_ref)
            dsh_ref[...] = jnp.zeros_like(dsh_ref)

        g = _dot(dp_ref[...], w_ref[...])
        hv = jnp.where(is_main, x_ref[...], c_ref[...])
        sc = jnp.where(is_main, sc_ref[0], scc_ref[0])
        r = lax.rsqrt(jnp.mean(hv * hv, axis=-1, keepdims=True) + EPS)
        y = hv * r
        gain_v = gain_ref[...]
        sum_g = _rowsum(g)
        sum_gy = _rowsum(g * (y * gain_v))
        drn = g * (1.0 + sc)
        dgain_ref[...] += _rowsum(drn * y)

        @pl.when(is_main)
        def _():
            dsh_ref[0] += sum_g
            dsc_ref[0] += sum_gy
            dy = drn * gain_v
            dh_ref[...] = r * (dy - y * jnp.mean(dy * y, axis=-1, keepdims=True)) + dho_ref[...]

        @pl.when(jnp.logical_not(is_main))
        def _():
            dshc_ref[...] += sum_g
            dscc_ref[...] += sum_gy

    main_row = pl.BlockSpec((tm, D), lambda i: (jnp.minimum(i, nmain - 1), 0))
    per_b = pl.BlockSpec((1, 1, D), lambda i: (jnp.minimum(i, nmain - 1) // tpb, 0, 0))
    vec = pl.BlockSpec((1, D), lambda i: (0, 0))
    per_b_shape = jax.ShapeDtypeStruct((nb, 1, D), F32)
    vec_shape = jax.ShapeDtypeStruct((1, D), F32)
    return pl.pallas_call(
        body, name="gla_in_dx_mod", grid=((n + nc) // tm,),
        in_specs=[pl.BlockSpec((tm, GLA_IN_PAD), lambda i: (i, 0)), pl.BlockSpec((GLA_IN_PAD, D), lambda i: (0, 0)),
                  main_row, pl.BlockSpec((tm, D), lambda i: (jnp.maximum(i - nmain, 0), 0)), main_row, vec, per_b,
                  pl.BlockSpec((1, 1, D), lambda i: (0, 0, 0))],
        out_specs=[main_row, per_b, per_b, vec, vec, vec],
        out_shape=[jax.ShapeDtypeStruct((n, D), F32), per_b_shape, per_b_shape, vec_shape, vec_shape, vec_shape],
        compiler_params=_params(("arbitrary",)),
    )(dp, w_gin, xf, cf, dh_out, gain, scale, scale_ctx)


def _gla_assemble(p_all, w2, b2, dq, dk, dv, dla, dgate, n):
    ntot = p_all.shape[0]
    tt = _tile(n, 128)
    nmain = n // tt
    assert ntot % tt == 0

    def body(a_ref, w_ref, b_ref, dqf_ref, dqb_ref, dkf_ref, dkb_ref, dvf_ref, dvb_ref, dlf_ref, dlb_ref, dg_ref,
             dp_ref, dw_ref, db_ref):
        i = pl.program_id(0)

        @pl.when(i == 0)
        def _():
            dw_ref[...] = jnp.zeros_like(dw_ref)
            db_ref[...] = jnp.zeros_like(db_ref)

        a = a_ref[...]
        w = w_ref[...]
        z = _dot(a, w) + b_ref[...]
        dla = jnp.concatenate([dlf_ref[...], dlb_ref[...]], axis=1)
        dz = dla * (1.0 / (1.0 + jnp.exp(z))) * (1.0 / TAU)
        dw_ref[...] += _dot(a, dz, _TN)
        db_ref[...] += _rowsum(dz)
        both = lambda f_ref, b_ref: (f_ref[...].astype(F32) + b_ref[...].astype(F32)).astype(BF16)
        dp_ref[:, 0:KEY] = both(dqf_ref, dqb_ref)
        dp_ref[:, KEY:2 * KEY] = both(dkf_ref, dkb_ref)
        dp_ref[:, 2 * KEY:2 * KEY + D] = both(dvf_ref, dvb_ref)
        dp_ref[:, 2 * KEY + D:2 * KEY + 2 * D] = (dg_ref[...] * jnp.where(i < nmain, 1.0, 0.0)).astype(BF16)
        dp_ref[:, 2 * KEY + 2 * D:GLA_IN_PAD] = _dot(dz, w, _NT).astype(BF16)

    row = lambda width: pl.BlockSpec((tt, width), lambda i: (i, 0))
    return pl.pallas_call(
        body, name="gla_assemble", grid=(ntot // tt,),
        in_specs=[pl.BlockSpec((tt, 128), lambda i: (i, (2 * KEY + 2 * D) // 128)),
                  pl.BlockSpec((128, 2 * KEY), lambda i: (0, 0)), pl.BlockSpec((1, 2 * KEY), lambda i: (0, 0)),
                  row(KEY), row(KEY), row(KEY), row(KEY), row(D), row(D), row(KEY), row(KEY),
                  pl.BlockSpec((tt, D), lambda i: (jnp.minimum(i, nmain - 1), 0))],
        out_specs=[pl.BlockSpec((tt, GLA_IN_PAD), lambda i: (i, 0)), pl.BlockSpec((128, 2 * KEY), lambda i: (0, 0)),
                   pl.BlockSpec((1, 2 * KEY), lambda i: (0, 0))],
        out_shape=[jax.ShapeDtypeStruct((ntot, GLA_IN_PAD), BF16), jax.ShapeDtypeStruct((128, 2 * KEY), F32),
                   jax.ShapeDtypeStruct((1, 2 * KEY), F32)],
        compiler_params=_params(("arbitrary",)),
    )(p_all, w2, b2, dq[0], dq[1], dk[0], dk[1], dv[0], dv[1], dla[0], dla[1], dgate)


ADA_ROWS = 24
ADA_SH = N_MOD * D // N_CHIPS


def _ada_fwd(cvec, ada_w, ada_b_sh):
    def body(c_ref, w_ref, b_ref, o_ref):
        o_ref[0] = _dot(_silu(c_ref[...]), w_ref[0]) + b_ref[0]

    return pl.pallas_call(
        body, name="ada_fwd", grid=(2,),
        in_specs=[pl.BlockSpec((ADA_ROWS, D), lambda l: (0, 0)), pl.BlockSpec((1, D, ADA_SH), lambda l: (l, 0, 0)),
                  pl.BlockSpec((1, 1, ADA_SH), lambda l: (l, 0, 0))],
        out_specs=pl.BlockSpec((1, ADA_ROWS, ADA_SH), lambda l: (l, 0, 0)),
        out_shape=jax.ShapeDtypeStruct((2, ADA_ROWS, ADA_SH), F32),
        compiler_params=_params(("parallel",)),
    )(cvec, ada_w, ada_b_sh)


def _ada_bwd(cvec, ada_w, dmod_sh):
    def body(c_ref, w_ref, dm_ref, gw_ref, dc_ref):
        dm = dm_ref[0]
        gw_ref[0] = _dot(_silu(c_ref[...]), dm, _TN)
        dc_ref[0] = _dot(dm, w_ref[0], _NT)

    return pl.pallas_call(
        body, name="ada_bwd", grid=(2,),
        in_specs=[pl.BlockSpec((ADA_ROWS, D), lambda l: (0, 0)), pl.BlockSpec((1, D, ADA_SH), lambda l: (l, 0, 0)),
                  pl.BlockSpec((1, ADA_ROWS, ADA_SH), lambda l: (l, 0, 0))],
        out_specs=[pl.BlockSpec((1, D, ADA_SH), lambda l: (l, 0, 0)), pl.BlockSpec((1, ADA_ROWS, D), lambda l: (l, 0, 0))],
        out_shape=[jax.ShapeDtypeStruct((2, D, ADA_SH), F32), jax.ShapeDtypeStruct((2, ADA_ROWS, D), F32)],
        compiler_params=_params(("parallel",)),
    )(cvec, ada_w, dmod_sh)


def _sum_slots(x, name):
    s, r, _ = x.shape

    def body(x_ref, o_ref):
        acc = x_ref[0]
        for k in range(1, s):
            acc = acc + x_ref[k]
        o_ref[...] = acc

    return pl.pallas_call(
        body, name=name, out_shape=jax.ShapeDtypeStruct((r, 128), F32),
        in_specs=[pl.BlockSpec(memory_space=pltpu.VMEM)], out_specs=pl.BlockSpec(memory_space=pltpu.VMEM),
    )(x)


def _cctx_grad(dscc_parts, c_ctx):
    def body(p_ref, c_ref, o_ref):
        acc = p_ref[0]
        for k in range(1, N_CHIPS):
            acc = acc + p_ref[k]
        o_ref[...] = acc * _dsilu(c_ref[...])

    return pl.pallas_call(
        body, name="cctx_grad", out_shape=jax.ShapeDtypeStruct((8, 128), F32),
        in_specs=[pl.BlockSpec(memory_space=pltpu.VMEM)] * 2, out_specs=pl.BlockSpec(memory_space=pltpu.VMEM),
    )(dscc_parts, c_ctx)


def _adamw(w, g, m, v, name, after):
    nl, r, cdim = w.shape
    tr = _tile(r, 256)
    c1 = 1.0 - ADAM_B1 ** ADAM_STEP
    c2 = 1.0 - ADAM_B2 ** ADAM_STEP

    def body(w_ref, g_ref, m_ref, v_ref, after_ref, d_ref, mo_ref, vo_ref):
        gv = g_ref[...]
        mn = ADAM_B1 * m_ref[...] + (1.0 - ADAM_B1) * gv
        vn = ADAM_B2 * v_ref[...] + (1.0 - ADAM_B2) * (gv * gv)
        mo_ref[...] = mn
        vo_ref[...] = vn
        d_ref[...] = -ADAM_LR * ((mn / c1) / (jnp.sqrt(vn / c2) + ADAM_EPS) + ADAM_WD * w_ref[...])

    spec = pl.BlockSpec((1, tr, cdim), lambda l, i: (l, i, 0))
    sds = jax.ShapeDtypeStruct((nl, r, cdim), F32)
    return pl.pallas_call(
        body, name=name, grid=(nl, r // tr), in_specs=[spec] * 4 + [pl.BlockSpec(memory_space=pl.ANY)],
        out_specs=[spec] * 3, out_shape=[sds] * 3, compiler_params=_params(("parallel", "parallel")),
    )(w, g, m, v, after)


def _place():
    x, y, c = lax.axis_index("x"), lax.axis_index("y"), lax.axis_index("c")
    return x, y, c


def _allgather_small(blk, name):
    m_per, n = blk.shape

    def body(x_ref, out_ref, send_sems, recv_sems, local_sem):
        x, y, c = _place()
        me, sibling = (x, y, c), (x, y, 1 - c)
        chips = [(1 - x, y), (x, 1 - y), (1 - x, 1 - y)]

        def rows(px, py, pc):
            return out_ref.at[pl.ds((4 * px + 2 * py + pc) * m_per, m_per), :]

        def copy(k, block, to, src=None):
            return pltpu.make_async_remote_copy(
                src_ref=rows(*block) if src is None else src, dst_ref=rows(*block),
                send_sem=send_sems.at[k], recv_sem=recv_sems.at[k], device_id=to, device_id_type=MESH)

        mine = pltpu.make_async_copy(x_ref, rows(*me), local_sem)
        mine.start()
        first = [copy(0, me, sibling, src=x_ref)]
        first += [copy(1 + j, me, (*chip, c), src=x_ref) for j, chip in enumerate(chips)]
        for cp in first:
            cp.start()
        passed = [copy(4 + j, (*chip, c), sibling) for j, chip in enumerate(chips)]
        for j, chip in enumerate(chips):
            copy(1 + j, (*chip, c), me).wait_recv()
            passed[j].start()
        copy(0, sibling, me).wait_recv()
        for j, chip in enumerate(chips):
            copy(4 + j, (*chip, 1 - c), me).wait_recv()
        for cp in first + passed:
            cp.wait_send()
        mine.wait()

    return pl.pallas_call(
        body, name=name,
        out_shape=jax.ShapeDtypeStruct((N_DEV * m_per, n), blk.dtype),
        in_specs=[pl.BlockSpec(memory_space=pltpu.VMEM)],
        out_specs=pl.BlockSpec(memory_space=pltpu.VMEM),
        scratch_shapes=[pltpu.SemaphoreType.DMA((7,)), pltpu.SemaphoreType.DMA((7,)), pltpu.SemaphoreType.DMA],
    )(blk)


def _other_chips(x, y):
    return [(1 - x, y), (x, 1 - y), (1 - x, 1 - y)]


_HBM_SPEC = pl.BlockSpec(memory_space=pltpu.HBM)
_SEM_SPEC = pl.BlockSpec(memory_space=pltpu.SEMAPHORE)
_SPLIT_PARAMS = pltpu.CompilerParams(has_side_effects=pltpu.SideEffectType.DATAFLOW_SIDE_EFFECTING)


def _in_hbm(a):
    return pltpu.with_memory_space_constraint(a, pltpu.HBM)


def _ag_copies(own_ref, land_ref, send_sems, recv_sems):
    x, y, c = _place()
    chip = 2 * x + y
    hr = own_ref.shape[0] // 2

    def half(ch):
        return land_ref.at[ch, pl.ds(c * hr, hr), :]

    def copy(k, src, dst, to):
        return pltpu.make_async_remote_copy(src_ref=src, dst_ref=dst, send_sem=send_sems.at[k],
                                            recv_sem=recv_sems.at[k], device_id=to, device_id_type=MESH)

    sends, expects = [], []
    for j, (ox, oy) in enumerate(_other_chips(x, y)):
        sends.append(copy(j, own_ref.at[pl.ds(c * hr, hr), :], half(chip), (ox, oy, c)))
        expects.append(copy(j, half(2 * ox + oy), half(2 * ox + oy), (ox, oy, c)))
    own_slot = copy(3, own_ref, land_ref.at[chip], (x, y, 1 - c))
    return sends + [own_slot], expects + [own_slot]


def _sc_copies(p_ref, land_ref, send_sems, recv_sems):
    x, y, c = _place()
    chip = 2 * x + y
    sends, expects = [], []
    for j, (ox, oy) in enumerate(_other_chips(x, y)):
        och = 2 * ox + oy
        mk = lambda dst_slot: pltpu.make_async_remote_copy(
            src_ref=p_ref.at[och], dst_ref=land_ref.at[dst_slot], send_sem=send_sems.at[j],
            recv_sem=recv_sems.at[j], device_id=(ox, oy, c), device_id_type=MESH)
        sends.append(mk(chip))
        expects.append(mk(och))
    return sends, expects


def _pe_copies(g_ref, land_ref, send_sems, recv_sems):
    x, y, c = _place()
    hr = g_ref.shape[1] // 2
    cp = pltpu.make_async_remote_copy(
        src_ref=g_ref.at[:, pl.ds((1 - c) * hr, hr), :], dst_ref=land_ref, send_sem=send_sems.at[0],
        recv_sem=recv_sems.at[0], device_id=(x, y, 1 - c), device_id_type=MESH)
    return [cp], [cp]


def _pass_on_copies(unused_ref, land_ref, send_sems, recv_sems):
    x, y, c = _place()
    hr = land_ref.shape[1] // 2
    sends, expects = [], []
    for j, (ox, oy) in enumerate(_other_chips(x, y)):
        def mk(cc, j=j, och=2 * ox + oy):
            ref = land_ref.at[och, pl.ds(cc * hr, hr), :]
            return pltpu.make_async_remote_copy(src_ref=ref, dst_ref=ref, send_sem=send_sems.at[j],
                                                recv_sem=recv_sems.at[j], device_id=(x, y, 1 - c),
                                                device_id_type=MESH)
        sends.append(mk(c))
        expects.append(mk(1 - c))
    return sends, expects


def _pair_gather_copies(unused_ref, land_ref, send_sems, recv_sems):
    x, y, c = _place()
    hr = land_ref.shape[0] // 2

    def mk(cc):
        ref = land_ref.at[pl.ds(cc * hr, hr), :]
        return pltpu.make_async_remote_copy(src_ref=ref, dst_ref=ref, send_sem=send_sems.at[0],
                                            recv_sem=recv_sems.at[0], device_id=(x, y, 1 - c), device_id_type=MESH)
    return [mk(c)], [mk(1 - c)]


def _split_start(src, land, copies, n_copies, after, name):
    def body(src_ref, land_ref, after_ref, send_sems, recv_sems, src_thru, land_thru, token):
        for cp in copies(src_ref, land_ref, send_sems, recv_sems)[0]:
            cp.start()
        token[...] = jnp.zeros_like(token)

    if isinstance(land, tuple):
        land = lax.empty(land, src.dtype)
    land_shape = land.shape
    return pl.pallas_call(
        body, name=name,
        out_shape=(pltpu.SemaphoreType.DMA((n_copies,)), pltpu.SemaphoreType.DMA((n_copies,)),
                   pltpu.HBM(src.shape, src.dtype), pltpu.HBM(land_shape, land.dtype),
                   jax.ShapeDtypeStruct((8, 128), F32)),
        in_specs=(_HBM_SPEC, _HBM_SPEC, pl.BlockSpec(memory_space=pl.ANY)),
        out_specs=(_SEM_SPEC, _SEM_SPEC, _HBM_SPEC, _HBM_SPEC, pl.BlockSpec(memory_space=pltpu.VMEM)),
        input_output_aliases={0: 2, 1: 3}, compiler_params=_SPLIT_PARAMS,
    )(_in_hbm(src), _in_hbm(land), after)


def _split_wait(started, after, copies, name):
    send_sems, recv_sems, src_thru, land_thru, _ = started

    def body(src_ref, land_ref, send_sems, recv_sems, after_ref, src_dead, got_ref):
        sends, expects = copies(src_ref, land_ref, send_sems, recv_sems)
        for cp in sends:
            cp.wait_send()
        for cp in expects:
            cp.wait_recv()

    return pl.pallas_call(
        body, name=name,
        out_shape=(pltpu.HBM(src_thru.shape, src_thru.dtype), pltpu.HBM(land_thru.shape, land_thru.dtype)),
        in_specs=(_HBM_SPEC, _HBM_SPEC, _SEM_SPEC, _SEM_SPEC, pl.BlockSpec(memory_space=pl.ANY)),
        out_specs=(_HBM_SPEC, _HBM_SPEC), input_output_aliases={0: 0, 1: 1}, compiler_params=_SPLIT_PARAMS,
    )(src_thru, land_thru, send_sems, recv_sems, after)


def _ag_pass_on(land, name):
    hr = land.shape[1] // 2

    def body(in_ref, out_ref, send_sems, recv_sems):
        x, y, c = _place()

        def copy(j, ox, oy, cc):
            ref = out_ref.at[2 * ox + oy, pl.ds(cc * hr, hr), :]
            return pltpu.make_async_remote_copy(src_ref=ref, dst_ref=ref, send_sem=send_sems.at[j],
                                                recv_sem=recv_sems.at[j], device_id=(x, y, 1 - c),
                                                device_id_type=MESH)

        others = _other_chips(x, y)
        for j, (ox, oy) in enumerate(others):
            copy(j, ox, oy, c).start()
        for j, (ox, oy) in enumerate(others):
            copy(j, ox, oy, 1 - c).wait_recv()
        for j, (ox, oy) in enumerate(others):
            copy(j, ox, oy, c).wait_send()

    any_spec = pl.BlockSpec(memory_space=pl.ANY)
    return pl.pallas_call(
        body, name=name, out_shape=jax.ShapeDtypeStruct(land.shape, land.dtype),
        in_specs=[any_spec], out_specs=any_spec, input_output_aliases={0: 0},
        scratch_shapes=[pltpu.SemaphoreType.DMA((3,)), pltpu.SemaphoreType.DMA((3,))],
    )(land)


def _rs_pair_exchange(g, after, name):
    r = g.shape[1]
    hr = r // 2

    def body(g_ref, after_ref, got_ref, send_sem, recv_sem):
        x, y, c = _place()
        cp = pltpu.make_async_remote_copy(
            src_ref=g_ref.at[:, pl.ds((1 - c) * hr, hr), :], dst_ref=got_ref, send_sem=send_sem, recv_sem=recv_sem,
            device_id=(x, y, 1 - c), device_id_type=MESH)
        cp.start()
        cp.wait()

    any_spec = pl.BlockSpec(memory_space=pl.ANY)
    return pl.pallas_call(
        body, name=name,
        out_shape=jax.ShapeDtypeStruct((N_CHIPS, hr, D), F32),
        in_specs=[any_spec, any_spec], out_specs=any_spec,
        scratch_shapes=[pltpu.SemaphoreType.DMA, pltpu.SemaphoreType.DMA],
    )(g, after)


def _rs_chip_sum(place, g, got, name):
    r = g.shape[1]
    hr = r // 2
    tr = _tile(hr, 640, 16)
    nt = hr // tr

    def body(pl_ref, g_ref, got_ref, p16_ref, p32_ref):
        s = pl.program_id(1)
        p = g_ref[0] + got_ref[0]
        p16_ref[0] = p.astype(BF16)

        @pl.when(s == pl_ref[1])
        def _():
            p32_ref[...] = p

    return pl.pallas_call(
        body, name=name,
        grid_spec=pltpu.PrefetchScalarGridSpec(
            num_scalar_prefetch=1, grid=(nt, N_CHIPS),
            in_specs=[pl.BlockSpec((1, tr, D), lambda i, s, pr: (s, pr[0] * nt + i, 0)),
                      pl.BlockSpec((1, tr, D), lambda i, s, pr: (s, i, 0))],
            out_specs=[pl.BlockSpec((1, tr, D), lambda i, s, pr: (s, i, 0)),
                       pl.BlockSpec((tr, D), lambda i, s, pr: (i, 0))]),
        out_shape=[jax.ShapeDtypeStruct((N_CHIPS, hr, D), BF16), jax.ShapeDtypeStruct((hr, D), F32)],
        compiler_params=_params(("parallel", "arbitrary")),
    )(place, g, got)


def _rs_final_sum(place, parts, p32, name):
    hr = parts.shape[1]
    tr = _tile(hr, 640, 16)
    nt = hr // tr

    def body(pl_ref, a_ref, b_ref, c_ref, p32_ref, o_ref):
        o_ref[...] = ((p32_ref[...] + a_ref[0].astype(F32)) + b_ref[0].astype(F32)) + c_ref[0].astype(F32)

    def other(j):
        return pl.BlockSpec((1, tr, D), lambda i, pr: (j + jnp.where(pr[1] <= j, 1, 0), i, 0))

    return pl.pallas_call(
        body, name=name,
        grid_spec=pltpu.PrefetchScalarGridSpec(
            num_scalar_prefetch=1, grid=(nt,),
            in_specs=[other(0), other(1), other(2), pl.BlockSpec((tr, D), lambda i, pr: (i, 0))],
            out_specs=pl.BlockSpec((tr, D), lambda i, pr: (pr[0] * nt + i, 0))),
        out_shape=jax.ShapeDtypeStruct((2 * hr, D), F32),
        compiler_params=_params(("parallel",)),
    )(place, parts, parts, parts, p32)


def _rs_pair_gather(both, name):
    hr = both.shape[0] // 2

    def body(in_ref, out_ref, send_sem, recv_sem):
        x, y, c = _place()
        mine = out_ref.at[pl.ds(c * hr, hr), :]
        cp = pltpu.make_async_remote_copy(
            src_ref=mine, dst_ref=mine, send_sem=send_sem, recv_sem=recv_sem,
            device_id=(x, y, 1 - c), device_id_type=MESH)
        cp.start()
        theirs = out_ref.at[pl.ds((1 - c) * hr, hr), :]
        pltpu.make_async_remote_copy(
            src_ref=theirs, dst_ref=theirs, send_sem=send_sem, recv_sem=recv_sem,
            device_id=(x, y, 1 - c), device_id_type=MESH).wait_recv()
        cp.wait_send()

    any_spec = pl.BlockSpec(memory_space=pl.ANY)
    return pl.pallas_call(
        body, name=name,
        out_shape=jax.ShapeDtypeStruct(both.shape, F32),
        in_specs=[any_spec], out_specs=any_spec, input_output_aliases={0: 0},
        scratch_shapes=[pltpu.SemaphoreType.DMA, pltpu.SemaphoreType.DMA],
    )(both)


def _local_step(x, ctx, tgt, mods, mc, ag_gin, ag_main, place, small):
    nb, t, _ = x.shape
    tc = ctx.shape[1]
    n = nb * t
    nc = nb * tc
    xf = x.reshape(n, D)
    cf = ctx.reshape(nc, D)
    tf = tgt.reshape(n, D)
    vec = lambda a: a.reshape(1, -1)
    m = [[mods[l, :, k, :].reshape(nb, 1, D) for k in range(N_MOD)] for l in range(2)]
    mc_b = [jnp.broadcast_to(mc[k].reshape(1, 1, D), (nb, 1, D)) for k in range(2)]

    cw = [small["ffn_conv_w"][l] for l in range(2)]
    cb = [small["ffn_conv_b"][l].reshape(1, -1) for l in range(2)]
    w2 = jnp.zeros((128, 2 * KEY), F32)
    w2 = w2.at[0:RANK, 0:KEY].set(small["gla_w_a2"][0]).at[RANK:2 * RANK, KEY:].set(small["gla_w_a2"][1])
    b2 = small["gla_b_a"].reshape(1, 2 * KEY)
    hg = small["gla_head_norm"].reshape(1, DV)

    hn_all = _mod_fwd(xf, vec(small["norm_mix"][0]), m[0][0], m[0][1], t, "mod0_main", n + nc)
    hn_all = _mod_fwd(cf, vec(small["norm_mix"][0]), mc_b[0], mc_b[1], tc, "mod0_ctx", n + nc, into=hn_all, row0=n)
    gin = _ag_pass_on(_split_wait(ag_gin, hn_all, _ag_copies, "ag_gin_wait")[1], "ag_gin_pass_on")
    out_rows = _MAIN_ROWS["gla_out"]
    w_gin = jnp.pad(gin[:, out_rows:out_rows + _GIN_ROWS, :].reshape(GLA_IN, D), ((0, GLA_IN_PAD - GLA_IN), (0, 0)))
    p_all, la_all = _gla_in_proj(hn_all, w_gin, w2, b2)
    o2, ss = _gla_scan_fwd(p_all, la_all, nb, t, tc)
    yb0 = _gla_post_fwd(o2, p_all, hg, n)
    arrived = _split_wait(ag_main, yb0, _ag_copies, "ag_main_wait")[1]
    passing = _split_start(ag_main[4], arrived, _pass_on_copies, 3, yb0, "ag_main_pass_start")
    offs = _offsets(_MAIN, _MAIN_ROWS)
    woffs = _offsets(_WMAIN, _MAIN_ROWS)
    rows = _MAIN_ROWS

    def w_nt(a, k, name, out_dtype=BF16, tm=1024):
        return _mm_nt_w(a, wg, woffs[k], rows[k], name, tm, out_dtype)

    def w_nn_mod(a3, k, h, gate, gain, shift, scale, name):
        return _mm_nn_w_mod(a3, wg, woffs[k], rows[k], h, gate, vec(gain), shift, scale, t, name, 512)

    y0, h1, hn1 = _mm_nn_w_mod(yb0[None], gin, 0, out_rows, xf, m[0][2],
                               vec(small["norm_ffn"][0]) + passing[4][0:1, 0:1], m[0][3], m[0][4], t,
                               "gla_out_proj_mod", 512)
    wg = _split_wait(passing, hn1, _pass_on_copies, "ag_main_pass_wait")[1]
    u0, z0 = _ffn_up_mid_fwd(hn1, wg, woffs["up_t0"], cw[0], cb[0], nb, t, "ffn0_up_mid")
    f0, h2, hn2 = w_nn_mod(z0[None], "down0", h1, m[0][5], small["norm_mix"][1], m[1][0], m[1][1],
                           "ffn0_down_mod")
    p1, yb1 = _sc_in_mid_fwd(hn2, wg, woffs["sc_in_t"], small["sc_conv_w"], nb, t)
    y1, h3, hn3 = w_nn_mod(yb1[None], "sc_out", h2, m[1][2], small["norm_ffn"][1], m[1][3], m[1][4],
                           "sc_out_proj_mod")
    u1, z1 = _ffn_up_mid_fwd(hn3, wg, woffs["up_t1"], cw[1], cb[1], nb, t, "ffn1_up_mid")
    loss, dh4, df1, dm15, dfinal = _mm_nn_w_final(z1[None], wg, woffs["down1"], rows["down1"], h3, m[1][5],
                                                  vec(small["final_norm"]), tf, t, "ffn1_down_final", 512)

    gs = {}
    dmods = [[None] * N_MOD for _ in range(2)]
    dmods[1][5] = dm15

    def w_dw(a3, b, g_prev, k, name, tm):
        return _mm_dw(a3, b, g_prev, offs[k], rows[k], name, tm)

    def w_dx_mod(a3, k, h_in, dh_out, gain, scale, y_prev, gate_prev, name):
        return _mm_nn_w_modbwd(a3, wg, woffs[k], rows[k], h_in, dh_out, vec(gain), scale, y_prev, gate_prev, t,
                               name, 256)

    def ffn_bwd(l, df, u, z, hn, g_prev, h_in, dh_out, scale, y_prev, gate_prev):
        dz = w_nt(df, f"down{l}", f"ffn{l}_down_dx")
        g_acc = w_dw(z[None], df, g_prev, f"down{l}", f"ffn{l}_down_dw", 640)
        du, dcw, dcb = _ffn_mid_bwd(u, cw[l], cb[l], dz, nb, t, f"ffn{l}_mid_bwd")
        r = w_dx_mod(du, f"up_t{l}", h_in, dh_out, small["norm_ffn"][l], scale, y_prev, gate_prev,
                     f"ffn{l}_up_dx_mod")
        g_acc = w_dw(du, hn, g_acc, f"up_t{l}", f"ffn{l}_up_dw", 640)
        return r, g_acc, jnp.moveaxis(dcw, 0, 1).reshape(3, 2 * HID), dcb.reshape(2 * HID)

    r, g_acc, dcw1, dcb1 = ffn_bwd(1, df1, u1, z1, hn3, None, h3, dh4, m[1][4], y1, m[1][2])
    dh3, dmods[1][4], dmods[1][3], dnf1, dy1, dmods[1][2] = (r["dh"], r["dscale"], r["dshift"], r["dgain"],
                                                             r["dy_prev"], r["dgate_prev"])
    dyb1 = w_nt(dy1, "sc_out", "sc_out_dx")
    g_acc = w_dw(yb1[None], dy1, g_acc, "sc_out", "sc_out_dw", 256)
    dp1, dscw = _sc_mid_bwd(p1, small["sc_conv_w"], dyb1, nb, t)
    r = w_dx_mod(dp1, "sc_in_t", h2, dh3, small["norm_mix"][1], m[1][1], f0, m[0][5], "sc_in_dx_mod")
    g_acc = w_dw(dp1, hn2, g_acc, "sc_in_t", "sc_in_dw", 256)
    dh2, dmods[1][1], dmods[1][0], dnm1, df0, dmods[0][5] = (r["dh"], r["dscale"], r["dshift"], r["dgain"],
                                                             r["dy_prev"], r["dgate_prev"])
    r, g_acc, dcw0, dcb0 = ffn_bwd(0, df0, u0, z0, hn1, g_acc, h1, dh2, m[0][4], y0, m[0][2])
    dh1, dmods[0][4], dmods[0][3], dnf0, dy0, dmods[0][2] = (r["dh"], r["dscale"], r["dshift"], r["dgain"],
                                                             r["dy_prev"], r["dgate_prev"])
    g_packed = w_dw(yb0[None], dy0, g_acc, "gla_out", "gla_out_dw", 256)
    pair = _split_start(g_packed, (N_CHIPS, _MAIN_TOTAL // 2, D), _pe_copies, 1, dy0, "rs_main_pair_start")
    do, dgate, dhg = _gla_out_dx_post_bwd(dy0, gin, 0, o2, p_all, hg + pair[4][0:1, 0:1], n)
    g_packed, from_sibling = _split_wait(pair, do, _pe_copies, "rs_main_pair_wait")
    p16, p32 = _rs_chip_sum(place, g_packed, from_sibling, "rs_main_chip_sum")
    sc_main = _split_start(p16, p16.shape, _sc_copies, 3, p32, "rs_main_scatter_start")
    dq, dk, dv, dla = _gla_scan_bwd(p_all, la_all, do, ss, nb, t, tc, sc_main[4])
    dp, dw2, db2 = _gla_assemble(p_all, w2, b2, dq, dk, dv, dla, dgate, n)
    grad_x, dmods[0][1], dmods[0][0], dmc0, dmc1, dnm0 = _gla_in_dx_mod(
        dp, w_gin, xf, cf, dh1, vec(small["norm_mix"][0]), m[0][1], mc[1].reshape(1, 1, D), t, tc)
    dmc = jnp.concatenate([dmc0, dmc1], axis=0)
    landed = _split_wait(sc_main, grad_x, _sc_copies, "rs_main_scatter_wait")[1]
    g_main = _split_start(sc_main[4], _rs_final_sum(place, landed, p32, "rs_main_final_sum"),
                          _pair_gather_copies, 1, landed, "rs_main_gather_start")
    g_gin = _mm(dp, hn_all, "tn", F32, "gla_in_dw", 640, 1024)[:GLA_IN]
    g_gin = jnp.pad(g_gin.reshape(N_CHIPS, _GIN_ROWS, D), ((0, 0), (0, _GIN_PAD - _GIN_ROWS), (0, 0)))
    from_sibling = _rs_pair_exchange(g_gin, g_main[4], "rs_gin_pair_exchange")
    p16_gin, p32_gin = _rs_chip_sum(place, g_gin, from_sibling, "rs_gin_chip_sum")

    gs["norm_mix"] = jnp.concatenate([dnm0, dnm1], axis=0)
    gs["norm_ffn"] = jnp.concatenate([dnf0, dnf1], axis=0)
    gs["final_norm"] = dfinal.reshape(D)
    gs["gla_w_a2"] = jnp.stack([dw2[0:RANK, 0:KEY], dw2[RANK:2 * RANK, KEY:]])
    gs["gla_b_a"] = db2.reshape(2, KEY)
    gs["gla_head_norm"] = dhg.reshape(DV)
    gs["sc_conv_w"] = dscw
    gs["ffn_conv_w"] = jnp.stack([dcw0, dcw1])
    gs["ffn_conv_b"] = jnp.stack([dcb0, dcb1])
    dmods_arr = jnp.stack([jnp.stack([dmods[l][k].reshape(nb, D) for k in range(N_MOD)], axis=1) for l in range(2)])
    return loss, grad_x.reshape(nb, t, D), g_main, p16_gin, p32_gin, gs, dmods_arr, dmc


def _pack(arrs):
    parts, meta, off = [], [], 0
    for a in arrs:
        r = a.size // 128
        rp = -(-r // 8) * 8
        a2 = a.reshape(r, 128).astype(F32)
        if rp != r:
            a2 = jnp.pad(a2, ((0, rp - r), (0, 0)))
        parts.append(a2)
        meta.append((off, r, a.shape))
        off += rp
    return jnp.concatenate(parts, axis=0), meta


def _unpack(buf, meta, lead=()):
    return [buf[..., off:off + r, :].reshape(*lead, *shape) for off, r, shape in meta]


_MAIN = ("up_t0", "up_t1", "down0", "down1", "sc_in_t", "gla_out", "sc_out")
_WMAIN = tuple(k for k in _MAIN if k != "gla_out")
_MAIN_ROWS = {"sc_in_t": 3 * D // N_CHIPS, "up_t0": 2 * HID // N_CHIPS, "up_t1": 2 * HID // N_CHIPS,
              "gla_out": D // N_CHIPS, "sc_out": D // N_CHIPS, "down0": HID // N_CHIPS, "down1": HID // N_CHIPS}
_MAIN_TOTAL = sum(_MAIN_ROWS.values())
_GIN_ROWS = GLA_IN // N_CHIPS
_GIN_PAD = -(-_GIN_ROWS // 32) * 32
_WMAIN_TOTAL = _MAIN_TOTAL - _MAIN_ROWS["gla_out"]
_GLA_W_ROWS = _MAIN_ROWS["gla_out"] + _GIN_ROWS
_GLA_W_PAD = -(-_GLA_W_ROWS // 32) * 32


def _offsets(names, rows):
    off, out = 0, {}
    for k in names:
        out[k] = off
        off += rows[k]
    return out


def kernel(x, c, ctx, c_ctx, ada_w, ada_b, norm_mix, norm_ffn, gla_w_in, gla_w_a2, gla_b_a, gla_head_norm, gla_w_out, sc_w_in, sc_conv_w, sc_w_out, ffn_w_up, ffn_conv_w, ffn_conv_b, ffn_w_down, final_norm, loss_target, m_c_ctx, m_ada_w, m_ada_b, m_norm_mix, m_norm_ffn, m_gla_w_in, m_gla_w_a2, m_gla_b_a, m_gla_head_norm, m_gla_w_out, m_sc_w_in, m_sc_conv_w, m_sc_w_out, m_ffn_w_up, m_ffn_conv_w, m_ffn_conv_b, m_ffn_w_down, m_final_norm, v_c_ctx, v_ada_w, v_ada_b, v_norm_mix, v_norm_ffn, v_gla_w_in, v_gla_w_a2, v_gla_b_a, v_gla_head_norm, v_gla_w_out, v_sc_w_in, v_sc_conv_w, v_sc_w_out, v_ffn_w_up, v_ffn_conv_w, v_ffn_conv_b, v_ffn_w_down, v_final_norm):
    ix, iy, ic = _place()
    chip = 2 * ix + iy
    dev = 2 * chip + ic
    place = jnp.stack([ic, chip]).astype(jnp.int32)
    nb = x.shape[0]
    offs = _offsets(_MAIN, _MAIN_ROWS)

    buf, meta = _pack([c, ffn_conv_w, sc_conv_w, gla_w_a2, gla_b_a])
    got = _allgather_small(buf, "gather_small_in").reshape(N_DEV, buf.shape[0], 128)
    c_all, fcw, scw, wa2, ba = _unpack(got, meta, (N_DEV,))
    c_all = c_all.reshape(N_DEV * nb, D)
    per_chip = lambda a: a[0::2]
    ffn_conv_w_full = jnp.moveaxis(per_chip(fcw), 0, 2).reshape(2, 3, 2 * HID)
    sc_conv_w_full = jnp.moveaxis(per_chip(scw)[:, 0], 0, 1).reshape(3, D)
    gla_w_a2_full = jnp.moveaxis(per_chip(wa2)[:, 0], 0, 2).reshape(2, RANK, KEY)
    gla_b_a_full = jnp.moveaxis(per_chip(ba)[:, 0], 0, 1).reshape(2, KEY)

    cvec = jnp.concatenate([c_all, c_ctx.reshape(1, D), jnp.zeros((ADA_ROWS - N_DEV * nb - 1, D), F32)], axis=0)
    ada_b_sh = lax.dynamic_slice_in_dim(ada_b, chip * ADA_SH, ADA_SH, axis=1).reshape(2, 1, ADA_SH)
    mod_sh = _ada_fwd(cvec, ada_w, ada_b_sh)
    got = _allgather_small(mod_sh.reshape(2 * ADA_ROWS, ADA_SH), "gather_mod")
    mod_full = jnp.moveaxis(per_chip(got.reshape(N_DEV, 2, ADA_ROWS, ADA_SH)), 0, 2).reshape(2, ADA_ROWS, N_MOD * D)
    mc = mod_full[0, N_DEV * nb, :2 * D].reshape(2, D)

    own = {"sc_in_t": sc_w_in[0].T, "up_t0": ffn_w_up[0].T, "up_t1": ffn_w_up[1].T,
           "gla_out": gla_w_out[0], "sc_out": sc_w_out[0], "down0": ffn_w_down[0], "down1": ffn_w_down[1]}
    own_main = jnp.concatenate([own[k].astype(BF16) for k in _WMAIN], axis=0)
    own_gin = jnp.concatenate([own["gla_out"].astype(BF16), gla_w_in[0].T.astype(BF16),
                               jnp.zeros((_GLA_W_PAD - _GLA_W_ROWS, D), BF16)], axis=0)
    ag_gin = _split_start(own_gin, (N_CHIPS, _GLA_W_PAD, D), _ag_copies, 4, mc, "ag_gin_start")
    ag_main = _split_start(own_main, (N_CHIPS, _WMAIN_TOTAL, D), _ag_copies, 4, ag_gin[4], "ag_main_start")
    mods = lax.dynamic_slice_in_dim(mod_full, dev * nb, nb, axis=1).reshape(2, nb, N_MOD, D) + ag_main[4][0, 0]

    small = {"norm_mix": norm_mix, "norm_ffn": norm_ffn, "final_norm": final_norm, "gla_w_a2": gla_w_a2_full,
             "gla_b_a": gla_b_a_full, "gla_head_norm": gla_head_norm[0], "sc_conv_w": sc_conv_w_full,
             "ffn_conv_w": ffn_conv_w_full, "ffn_conv_b": ffn_conv_b}
    loss_p, grad_x, g_main, p16_gin, p32_gin, gs, dmods, dmc = _local_step(x, ctx, loss_target, mods, mc, ag_gin,
                                                                           ag_main, place, small)

    sum_names = ["norm_mix", "norm_ffn", "final_norm", "gla_w_a2", "gla_b_a", "gla_head_norm", "sc_conv_w",
                 "ffn_conv_w", "ffn_conv_b"]
    buf, meta = _pack([jnp.broadcast_to(loss_p, (8, 128))] + [gs[k] for k in sum_names] + [dmc, dmods])
    n_sum = meta[-1][0]
    got = _allgather_small(buf, "gather_small_grads").reshape(N_DEV, buf.shape[0], 128)
    summed = _sum_slots(got[:, :n_sum], "sum_small_grads")
    parts = _unpack(summed, meta[:-1])
    loss = parts[0][0, 0]
    g_small = dict(zip(sum_names, parts[1:-1]))
    dmc_tot = parts[-1]
    dmods_all = jnp.moveaxis(_unpack(got, meta[-1:], (N_DEV,))[0], 0, 1).reshape(2, N_DEV * nb, N_MOD * D)

    ctx_row = jnp.stack([jnp.concatenate([dmc_tot.reshape(2 * D), jnp.zeros(((N_MOD - 2) * D,), F32)]),
                         jnp.zeros((N_MOD * D,), F32)]).reshape(2, 1, N_MOD * D)
    dmod_ext = jnp.concatenate([dmods_all, ctx_row, jnp.zeros((2, ADA_ROWS - N_DEV * nb - 1, N_MOD * D), F32)], axis=1)
    g_ada_b = _sum_slots(jnp.moveaxis(dmod_ext, 1, 0).reshape(ADA_ROWS, 2 * N_MOD * D // 128, 128),
                         "sum_ada_b").reshape(2, N_MOD * D)
    dmod_sh = lax.dynamic_slice_in_dim(dmod_ext, chip * ADA_SH, ADA_SH, axis=2)
    g_ada_w, dcv = _ada_bwd(cvec, ada_w, dmod_sh)
    dscc_part = (dcv[0, N_DEV * nb] + dcv[1, N_DEV * nb]).reshape(8, 128)
    got = _allgather_small(dscc_part, "gather_dscc").reshape(N_DEV, 8, 128)
    g_c_ctx = _cctx_grad(per_chip(got), c_ctx.reshape(8, 128)).reshape(D)

    sc_gin = _split_start(p16_gin, p16_gin.shape, _sc_copies, 3, g_c_ctx, "rs_gin_scatter_start")
    g_main = _split_wait(g_main, sc_gin[4], _pair_gather_copies, "rs_main_gather_wait")[1]
    seg = {k: g_main[offs[k]:offs[k] + _MAIN_ROWS[k]] for k in _MAIN}

    sl_chip = lambda a, axis, width: lax.dynamic_slice_in_dim(a, chip * width, width, axis=axis)
    grads = {
        "c_ctx": g_c_ctx, "ada_w": g_ada_w, "ada_b": g_ada_b, "norm_mix": g_small["norm_mix"],
        "norm_ffn": g_small["norm_ffn"],
        "gla_w_a2": sl_chip(g_small["gla_w_a2"], 2, KEY // N_CHIPS)[None],
        "gla_b_a": sl_chip(g_small["gla_b_a"], 1, KEY // N_CHIPS)[None],
        "gla_head_norm": g_small["gla_head_norm"][None], "gla_w_out": seg["gla_out"][None],
        "sc_w_in": seg["sc_in_t"].T[None], "sc_conv_w": sl_chip(g_small["sc_conv_w"], 1, D // N_CHIPS)[None],
        "sc_w_out": seg["sc_out"][None], "ffn_w_up": jnp.stack([seg["up_t0"].T, seg["up_t1"].T]),
        "ffn_conv_w": sl_chip(g_small["ffn_conv_w"], 2, 2 * HID // N_CHIPS), "ffn_conv_b": g_small["ffn_conv_b"],
        "ffn_w_down": jnp.stack([seg["down0"], seg["down1"]]), "final_norm": g_small["final_norm"],
    }
    weights = {"c_ctx": c_ctx, "ada_w": ada_w, "ada_b": ada_b, "norm_mix": norm_mix, "norm_ffn": norm_ffn,
               "gla_w_in": gla_w_in, "gla_w_a2": gla_w_a2, "gla_b_a": gla_b_a, "gla_head_norm": gla_head_norm,
               "gla_w_out": gla_w_out, "sc_w_in": sc_w_in, "sc_conv_w": sc_conv_w, "sc_w_out": sc_w_out,
               "ffn_w_up": ffn_w_up, "ffn_conv_w": ffn_conv_w, "ffn_conv_b": ffn_conv_b, "ffn_w_down": ffn_w_down,
               "final_norm": final_norm}
    mom1 = {"c_ctx": m_c_ctx, "ada_w": m_ada_w, "ada_b": m_ada_b, "norm_mix": m_norm_mix, "norm_ffn": m_norm_ffn,
            "gla_w_in": m_gla_w_in, "gla_w_a2": m_gla_w_a2, "gla_b_a": m_gla_b_a, "gla_head_norm": m_gla_head_norm,
            "gla_w_out": m_gla_w_out, "sc_w_in": m_sc_w_in, "sc_conv_w": m_sc_conv_w, "sc_w_out": m_sc_w_out,
            "ffn_w_up": m_ffn_w_up, "ffn_conv_w": m_ffn_conv_w, "ffn_conv_b": m_ffn_conv_b,
            "ffn_w_down": m_ffn_w_down, "final_norm": m_final_norm}
    mom2 = {"c_ctx": v_c_ctx, "ada_w": v_ada_w, "ada_b": v_ada_b, "norm_mix": v_norm_mix, "norm_ffn": v_norm_ffn,
            "gla_w_in": v_gla_w_in, "gla_w_a2": v_gla_w_a2, "gla_b_a": v_gla_b_a, "gla_head_norm": v_gla_head_norm,
            "gla_w_out": v_gla_w_out, "sc_w_in": v_sc_w_in, "sc_conv_w": v_sc_conv_w, "sc_w_out": v_sc_w_out,
            "ffn_w_up": v_ffn_w_up, "ffn_conv_w": v_ffn_conv_w, "ffn_conv_b": v_ffn_conv_b,
            "ffn_w_down": v_ffn_w_down, "final_norm": v_final_norm}
    names = list(weights)

    big_names = ["ada_w", "gla_w_out", "sc_w_in", "sc_w_out", "ffn_w_up", "ffn_w_down", "gla_w_in"]
    small_names = [k for k in names if k not in big_names]
    delta, new_m, new_v = {}, {}, {}
    done = []

    def big_adamw(k, token):
        delta[k], new_m[k], new_v[k] = _adamw(weights[k], grads[k], mom1[k], mom2[k], "adamw_" + k, token)
        done.append(new_v[k][0, 0:1, 0:128])

    for k in big_names[:-1]:
        grads[k] = grads[k].reshape(weights[k].shape)
        big_adamw(k, sc_gin[4])
    for k in small_names:
        grads[k] = grads[k].reshape(weights[k].shape)
    packed = [_pack([src[k] for k in small_names]) for src in (weights, grads, mom1, mom2)]
    meta = packed[0][1]
    rows_pad = -packed[0][0].shape[0] % 128
    bufs = [jnp.pad(p[0], ((0, rows_pad), (0, 0)))[None] for p in packed]
    outs = _adamw(bufs[0], bufs[1], bufs[2], bufs[3], "adamw_small", sc_gin[4])
    done.append(outs[2][0, 0:1, :])
    for dst, o in zip((delta, new_m, new_v), outs):
        for k, a in zip(small_names, _unpack(o[0], meta)):
            dst[k] = a
    landed = _split_wait(sc_gin, jnp.concatenate(done, axis=0), _sc_copies, "rs_gin_scatter_wait")[1]
    g_gin_shard = _rs_pair_gather(_rs_final_sum(place, landed, p32_gin, "rs_gin_final_sum"), "rs_gin_pair_gather")
    grads["gla_w_in"] = g_gin_shard[:_GIN_ROWS].T[None]
    big_adamw("gla_w_in", sc_gin[4])

    return (loss, grad_x, *[grads[k] for k in names], *[delta[k] for k in names], *[new_m[k] for k in names],
            *[new_v[k] for k in names])
```

```python
import jax
import jax.numpy as jnp
from jax import lax
from jax.experimental import pallas as pl
from jax.experimental.pallas import tpu as pltpu

F32 = jnp.float32
BF16 = jnp.bfloat16
MESH = pl.DeviceIdType.MESH

EPS = 1e-6
D = 1024
N_MOD = 6
HEADS = 4
DK = 128
DV = 256
KEY = HEADS * DK
RANK = 16
TAU = 16.0
CH = 64
GRID_W = 64
HID = 2560
GLA_IN = 2 * KEY + 2 * D + 2 * RANK
GLA_IN_PAD = 3200
Q_SCALE = DK ** -0.5
N_CHIPS = 4
N_DEV = 8

ADAM_LR = 0.001
ADAM_B1 = 0.9
ADAM_B2 = 0.999
ADAM_EPS = 1e-08
ADAM_WD = 0.01
ADAM_STEP = 10

VMEM_LIMIT = 56 * 1024 * 1024


def _params(sem):
    return pltpu.CompilerParams(dimension_semantics=sem, vmem_limit_bytes=VMEM_LIMIT)


def _tile(n, pref, mult=8):
    if n <= pref:
        return n
    for t in range(pref - pref % mult, 0, -mult):
        if n % t == 0:
            return t
    raise ValueError((n, pref, mult))


_NN = (((1,), (0,)), ((), ()))
_NT = (((1,), (1,)), ((), ()))
_TN = (((0,), (0,)), ((), ()))


def _dot(a, b, dims=_NN):
    return lax.dot_general(a.astype(BF16), b.astype(BF16), dims, preferred_element_type=F32)


def _sigmoid(x):
    return 1.0 / (1.0 + jnp.exp(-x))


def _rowsum(x):
    return jnp.sum(x, axis=0, keepdims=True)


def _mm(a, b, form, out_dtype, name, tm, tn):
    if form == "tn":
        K, M = a.shape
    else:
        M, K = a.shape
    N = b.shape[0] if form == "nt" else b.shape[1]
    tm = _tile(M, tm, 128)
    tn = _tile(N, tn, 128)
    dims = {"nn": _NN, "nt": _NT, "tn": _TN}[form]

    def body(a_ref, b_ref, o_ref):
        o_ref[...] = _dot(a_ref[...], b_ref[...], dims).astype(o_ref.dtype)

    if form == "tn":
        a_spec = pl.BlockSpec((K, tm), lambda i, j: (0, i))
    else:
        a_spec = pl.BlockSpec((tm, K), lambda i, j: (i, 0))
    if form == "nt":
        b_spec = pl.BlockSpec((tn, K), lambda i, j: (j, 0))
    else:
        b_spec = pl.BlockSpec((K, tn), lambda i, j: (0, j))
    return pl.pallas_call(
        body,
        name=name,
        grid=(M // tm, N // tn),
        in_specs=[a_spec, b_spec],
        out_specs=pl.BlockSpec((tm, tn), lambda i, j: (i, j)),
        out_shape=jax.ShapeDtypeStruct((M, N), out_dtype),
        compiler_params=_params(("parallel", "parallel")),
    )(a, b)


def _mm_nt_w(a, wg, off, rows, name, tm, out_dtype):
    m = a.shape[0]
    tm = _tile(m, tm, 128)
    if N_CHIPS * rows <= D:

        def body_small(a_ref, w_ref, o_ref):
            av = a_ref[...]
            for s in range(N_CHIPS):
                o_ref[:, s * rows:(s + 1) * rows] = _dot(av, w_ref[s], _NT).astype(o_ref.dtype)

        return pl.pallas_call(
            body_small, name=name, grid=(m // tm,),
            in_specs=[pl.BlockSpec((tm, D), lambda i: (i, 0)),
                      pl.BlockSpec((N_CHIPS, rows, D), lambda i: (0, off // rows, 0))],
            out_specs=pl.BlockSpec((tm, N_CHIPS * rows), lambda i: (i, 0)),
            out_shape=jax.ShapeDtypeStruct((m, N_CHIPS * rows), out_dtype),
            compiler_params=_params(("parallel",)),
        )(a, wg)

    def body(a_ref, w_ref, o_ref):
        o_ref[...] = _dot(a_ref[...], w_ref[0], _NT).astype(o_ref.dtype)

    return pl.pallas_call(
        body, name=name, grid=(m // tm, N_CHIPS),
        in_specs=[pl.BlockSpec((tm, D), lambda i, s: (i, 0)),
                  pl.BlockSpec((1, rows, D), lambda i, s: (s, off // rows, 0))],
        out_specs=pl.BlockSpec((tm, rows), lambda i, s: (i, s)),
        out_shape=jax.ShapeDtypeStruct((m, N_CHIPS * rows), out_dtype),
        compiler_params=_params(("parallel", "parallel")),
    )(a, wg)


def _mm_nn_w_mod(a3, wg, off, rows, h, gate, gain, shift, scale, tpb_rows, name, tm):
    parts, m, kp = a3.shape
    assert parts * kp == N_CHIPS * rows
    tm = _tile(tpb_rows, tm, 128)
    tpb = tpb_rows // tm
    cuts = sorted({s * rows for s in range(N_CHIPS + 1)} | {p * kp for p in range(parts + 1)})
    pieces = [(k0 // kp, k0 % kp, k0 // rows, k0 % rows, k1 - k0) for k0, k1 in zip(cuts[:-1], cuts[1:])]

    def body(a_ref, w_ref, h_ref, gate_ref, gain_ref, sh_ref, sc_ref, y_ref, hout_ref, hn_ref):
        acc = None
        for p, a0, s, r0, width in pieces:
            term = _dot(a_ref[p, :, a0:a0 + width], w_ref[s, r0:r0 + width, :])
            acc = term if acc is None else acc + term
        y_ref[...] = acc
        hv = h_ref[...] + gate_ref[0] * acc
        hout_ref[...] = hv
        r = lax.rsqrt(jnp.mean(hv * hv, axis=-1, keepdims=True) + EPS)
        hn_ref[...] = ((hv * r) * gain_ref[...] * (1.0 + sc_ref[0]) + sh_ref[0]).astype(BF16)

    row = pl.BlockSpec((tm, D), lambda i: (i, 0))
    per_b = pl.BlockSpec((1, 1, D), lambda i: (i // tpb, 0, 0))
    return pl.pallas_call(
        body, name=name, grid=(m // tm,),
        in_specs=[pl.BlockSpec((parts, tm, kp), lambda i: (0, i, 0)),
                  pl.BlockSpec((N_CHIPS, rows, D), lambda i: (0, off // rows, 0)),
                  row, per_b, pl.BlockSpec((1, D), lambda i: (0, 0)), per_b, per_b],
        out_specs=[row, row, row],
        out_shape=[jax.ShapeDtypeStruct((m, D), F32), jax.ShapeDtypeStruct((m, D), F32),
                   jax.ShapeDtypeStruct((m, D), BF16)],
        compiler_params=_params(("parallel",)),
    )(a3, wg, h, gate, gain, shift, scale)


def _mm_nn_w_final(a3, wg, off, rows, h, gate, gain, tgt, tpb_rows, name, tm):
    parts, m, kp = a3.shape
    assert parts * kp == N_CHIPS * rows
    nb = m // tpb_rows
    tm = _tile(tpb_rows, tm, 128)
    tpb = tpb_rows // tm
    cuts = sorted({s * rows for s in range(N_CHIPS + 1)} | {p * kp for p in range(parts + 1)})
    pieces = [(k0 // kp, k0 % kp, k0 // rows, k0 % rows, k1 - k0) for k0, k1 in zip(cuts[:-1], cuts[1:])]

    def body(a_ref, w_ref, h_ref, gate_ref, gain_ref, tgt_ref, loss_ref, dh_ref, df_ref, dgate_ref, dgain_ref):
        i = pl.program_id(0)

        @pl.when(i == 0)
        def _():
            loss_ref[...] = jnp.zeros_like(loss_ref)
            dgain_ref[...] = jnp.zeros_like(dgain_ref)

        @pl.when(i % tpb == 0)
        def _():
            dgate_ref[...] = jnp.zeros_like(dgate_ref)

        fv = None
        for p, a0, s, r0, width in pieces:
            term = _dot(a_ref[p, :, a0:a0 + width], w_ref[s, r0:r0 + width, :])
            fv = term if fv is None else fv + term
        gate_v = gate_ref[0]
        hv = h_ref[...] + gate_v * fv
        r = lax.rsqrt(jnp.mean(hv * hv, axis=-1, keepdims=True) + EPS)
        y = hv * r
        gain_v = gain_ref[...]
        e = y * gain_v - tgt_ref[...]
        s_ = jnp.sum(_rowsum(e * e), axis=1, keepdims=True) * (0.5 / D)
        loss_ref[...] += jnp.broadcast_to(s_, loss_ref.shape)
        dout = e * (1.0 / D)
        dgain_ref[...] += _rowsum(dout * y)
        dy = dout * gain_v
        dh = r * (dy - y * jnp.mean(dy * y, axis=-1, keepdims=True))
        dh_ref[...] = dh
        df_ref[...] = (dh * gate_v).astype(BF16)
        dgate_ref[0] += _rowsum(dh * fv)

    row = pl.BlockSpec((tm, D), lambda i: (i, 0))
    per_b = pl.BlockSpec((1, 1, D), lambda i: (i // tpb, 0, 0))
    vec = pl.BlockSpec((1, D), lambda i: (0, 0))
    return pl.pallas_call(
        body, name=name, grid=(m // tm,),
        in_specs=[pl.BlockSpec((parts, tm, kp), lambda i: (0, i, 0)),
                  pl.BlockSpec((N_CHIPS, rows, D), lambda i: (0, off // rows, 0)), row, per_b, vec, row],
        out_specs=[pl.BlockSpec((1, 128), lambda i: (0, 0)), row, row, per_b, vec],
        out_shape=[jax.ShapeDtypeStruct((1, 128), F32), jax.ShapeDtypeStruct((m, D), F32),
                   jax.ShapeDtypeStruct((m, D), BF16), jax.ShapeDtypeStruct((nb, 1, D), F32),
                   jax.ShapeDtypeStruct((1, D), F32)],
        compiler_params=_params(("arbitrary",)),
    )(a3, wg, h, gate, gain, tgt)


def _mm_nn_w_modbwd(a3, wg, off, rows, h_in, dh_out, gain, scale, y_prev, gate_prev, tpb_rows, name, tm):
    parts, m, kp = a3.shape
    assert parts * kp == N_CHIPS * rows
    nb = m // tpb_rows
    tm = _tile(tpb_rows, tm, 128)
    tpb = tpb_rows // tm
    cuts = sorted({s * rows for s in range(N_CHIPS + 1)} | {p * kp for p in range(parts + 1)})
    pieces = [(k0 // kp, k0 % kp, k0 // rows, k0 % rows, k1 - k0) for k0, k1 in zip(cuts[:-1], cuts[1:])]

    def body(a_ref, w_ref, h_ref, gain_ref, sc_ref, dho_ref, yp_ref, gp_ref,
             dh_ref, dsc_ref, dsh_ref, dgain_ref, dyp_ref, dgp_ref):
        i = pl.program_id(0)

        @pl.when(i == 0)
        def _():
            dgain_ref[...] = jnp.zeros_like(dgain_ref)

        @pl.when(i % tpb == 0)
        def _():
            dsc_ref[...] = jnp.zeros_like(dsc_ref)
            dsh_ref[...] = jnp.zeros_like(dsh_ref)
            dgp_ref[...] = jnp.zeros_like(dgp_ref)

        g = None
        for p, a0, s, r0, width in pieces:
            term = _dot(a_ref[p, :, a0:a0 + width], w_ref[s, r0:r0 + width, :])
            g = term if g is None else g + term
        hv = h_ref[...]
        r = lax.rsqrt(jnp.mean(hv * hv, axis=-1, keepdims=True) + EPS)
        y = hv * r
        gain_v = gain_ref[...]
        dsh_ref[0] += _rowsum(g)
        dsc_ref[0] += _rowsum(g * (y * gain_v))
        drn = g * (1.0 + sc_ref[0])
        dgain_ref[...] += _rowsum(drn * y)
        dy = drn * gain_v
        dh = r * (dy - y * jnp.mean(dy * y, axis=-1, keepdims=True)) + dho_ref[...]
        dh_ref[...] = dh
        dyp_ref[...] = (dh * gp_ref[0]).astype(BF16)
        dgp_ref[0] += _rowsum(dh * yp_ref[...])

    row = pl.BlockSpec((tm, D), lambda i: (i, 0))
    per_b = pl.BlockSpec((1, 1, D), lambda i: (i // tpb, 0, 0))
    vec = pl.BlockSpec((1, D), lambda i: (0, 0))
    per_b_shape = jax.ShapeDtypeStruct((nb, 1, D), F32)
    outs = pl.pallas_call(
        body, name=name, grid=(m // tm,),
        in_specs=[pl.BlockSpec((parts, tm, kp), lambda i: (0, i, 0)),
                  pl.BlockSpec((N_CHIPS, rows, D), lambda i: (0, off // rows, 0)),
                  row, vec, per_b, row, row, per_b],
        out_specs=[row, per_b, per_b, vec, row, per_b],
        out_shape=[jax.ShapeDtypeStruct((m, D), F32), per_b_shape, per_b_shape, jax.ShapeDtypeStruct((1, D), F32),
                   jax.ShapeDtypeStruct((m, D), BF16), per_b_shape],
        compiler_params=_params(("arbitrary",)),
    )(a3, wg, h_in, gain, scale, dh_out, y_prev, gate_prev)
    return dict(zip(("dh", "dscale", "dshift", "dgain", "dy_prev", "dgate_prev"), outs))


def _mm_dw(a3, b, g_prev, off, rows, name, tm):
    parts, ntok, cdim = a3.shape
    assert parts * cdim == N_CHIPS * rows and cdim % tm == 0 and rows % tm == 0 and off % tm == 0

    def body(a_ref, b_ref, *rest):
        rest[-1][0] = _dot(a_ref[0], b_ref[...], _TN)

    in_specs = [pl.BlockSpec((1, ntok, tm), lambda i: ((i * tm) // cdim, 0, ((i * tm) % cdim) // tm)),
                pl.BlockSpec((ntok, D), lambda i: (0, 0))]
    args = [a3, b]
    aliases = {}
    if g_prev is not None:
        in_specs.append(pl.BlockSpec(memory_space=pl.ANY))
        args.append(g_prev)
        aliases = {2: 0}
    return pl.pallas_call(
        body, name=name, grid=(N_CHIPS * rows // tm,),
        in_specs=in_specs,
        out_specs=pl.BlockSpec((1, tm, D), lambda i: ((i * tm) // rows, (off + (i * tm) % rows) // tm, 0)),
        out_shape=jax.ShapeDtypeStruct((N_CHIPS, _MAIN_TOTAL, D), F32),
        input_output_aliases=aliases,
        compiler_params=_params(("parallel",)),
    )(*args)


def _mod_fwd(h, gain, shift, scale, tpb_rows, name, out_rows, into=None, row0=0):
    n = h.shape[0]
    tt = _tile(tpb_rows, 256)
    tpb = tpb_rows // tt
    assert row0 % tt == 0

    def body(h_ref, gain_ref, sh_ref, sc_ref, *rest):
        hv = h_ref[...]
        r = lax.rsqrt(jnp.mean(hv * hv, axis=-1, keepdims=True) + EPS)
        rest[-1][...] = ((hv * r) * gain_ref[...] * (1.0 + sc_ref[0]) + sh_ref[0]).astype(BF16)

    per_b = pl.BlockSpec((1, 1, D), lambda i: (i // tpb, 0, 0))
    in_specs = [pl.BlockSpec((tt, D), lambda i: (i, 0)), pl.BlockSpec((1, D), lambda i: (0, 0)), per_b, per_b]
    args = (h, gain, shift, scale)
    aliases = {}
    if into is not None:
        in_specs = in_specs + [pl.BlockSpec(memory_space=pl.ANY)]
        args = args + (into,)
        aliases = {4: 0}
    return pl.pallas_call(
        body, name=name, grid=(n // tt,), in_specs=in_specs,
        out_specs=pl.BlockSpec((tt, D), lambda i: (i + row0 // tt, 0)),
        out_shape=jax.ShapeDtypeStruct((out_rows, D), BF16),
        input_output_aliases=aliases, compiler_params=_params(("parallel",)),
    )(*args)


def _row_dn1(x):
    t = lax.broadcasted_iota(jnp.int32, x.shape, 0)
    return jnp.where(t % GRID_W == 0, 0.0, pltpu.roll(x, 1, 0))


def _row_up1(x):
    t = lax.broadcasted_iota(jnp.int32, x.shape, 0)
    return jnp.where(t % GRID_W == GRID_W - 1, 0.0, pltpu.roll(x, x.shape[0] - 1, 0))


def _silu(x):
    return x * _sigmoid(x)


def _dsilu(x):
    s = _sigmoid(x)
    return s * (1.0 + x * (1.0 - s))


def _row_ds(i):
    start = i * GRID_W
    return pl.ds(start if isinstance(start, int) else pl.multiple_of(start, GRID_W), GRID_W)


def _grid_row(ref, i, first, last):
    def rows(k):
        return ref[_row_ds(k), :].astype(F32)

    cur = rows(i)
    return (jnp.zeros_like(cur) if first else rows(i - 1)), cur, (jnp.zeros_like(cur) if last else rows(i + 1))


def _over_grid_rows(n_rows, step, carry):
    carry = step(0, carry, True, n_rows == 1)
    if n_rows > 2:
        carry = lax.fori_loop(1, n_rows - 1, lambda i, c: step(i, c, False, False), carry)
    if n_rows > 1:
        carry = step(n_rows - 1, carry, False, True)
    return carry


def _fold8(p):
    return p.reshape(GRID_W // 8, 8, p.shape[1]).sum(axis=0)


def _ffn_up_mid_fwd(hn, wg, off, cw, cb, nb, t, name):
    tcol = 256
    ncol = HID // tcol
    rows_sh = 2 * HID // N_CHIPS

    def conv(x, w_ref):
        zeros = jnp.zeros((GRID_W, x.shape[1]), x.dtype)
        down = jnp.concatenate([zeros, x[: x.shape[0] - GRID_W]], axis=0)
        up = jnp.concatenate([x[GRID_W:], zeros], axis=0)
        return down * w_ref[0:1, :] + x * w_ref[1:2, :] + up * w_ref[2:3, :]

    def body(h_ref, wa_ref, wg_ref, cwa_ref, cwg_ref, cba_ref, cbg_ref, u_ref, z_ref):
        hv = h_ref[...]
        ua = _dot(hv, wa_ref[0], _NT)
        ug = _dot(hv, wg_ref[0], _NT)
        u_ref[0] = ua.astype(BF16)
        u_ref[1] = ug.astype(BF16)
        a = conv(ua, cwa_ref) + cba_ref[...]
        gt = conv(ug, cwg_ref) + cbg_ref[...]
        z_ref[...] = (a * _silu(gt)).astype(BF16)

    def w_spec(part):
        def idx(b, j):
            n = part * HID + j * tcol
            return (n // rows_sh, (off + n % rows_sh) // tcol, 0)
        return pl.BlockSpec((1, tcol, D), idx)

    chan = lambda rows, part: pl.BlockSpec((rows, tcol), lambda b, j: (0, part * ncol + j))
    return pl.pallas_call(
        body, name=name, grid=(nb, ncol),
        in_specs=[pl.BlockSpec((t, D), lambda b, j: (b, 0)), w_spec(0), w_spec(1),
                  chan(3, 0), chan(3, 1), chan(1, 0), chan(1, 1)],
        out_specs=[pl.BlockSpec((2, t, tcol), lambda b, j: (0, b, j)), pl.BlockSpec((t, tcol), lambda b, j: (b, j))],
        out_shape=[jax.ShapeDtypeStruct((2, nb * t, HID), BF16), jax.ShapeDtypeStruct((nb * t, HID), BF16)],
        compiler_params=_params(("parallel", "parallel")),
    )(hn, wg, wg, cw, cw, cb, cb)


def _ffn_mid_bwd(u0, cw, cb, dz, nb, t, name):
    nc = HID // 128
    n_rows = t // GRID_W

    def body(ua3_ref, ug3_ref, wa_ref, wg_ref, ba_ref, bg_ref, dz_ref, du_ref, dw_ref, db_ref, dua_ref, dug_ref):
        ua_ref, ug_ref = ua3_ref.at[0], ug3_ref.at[0]
        b = pl.program_id(1)

        @pl.when(b == 0)
        def _():
            dw_ref[...] = jnp.zeros_like(dw_ref)
            db_ref[...] = jnp.zeros_like(db_ref)

        wa = [wa_ref[k:k + 1, :] for k in range(3)]
        wg = [wg_ref[k:k + 1, :] for k in range(3)]
        ba, bg = ba_ref[...], bg_ref[...]

        def pass1(i, acc, first, last):
            here = _row_ds(i)
            ap, ac, an = _grid_row(ua_ref, i, first, last)
            gp, gc, gn = _grid_row(ug_ref, i, first, last)
            a = ap * wa[0] + ac * wa[1] + an * wa[2] + ba
            gt = gp * wg[0] + gc * wg[1] + gn * wg[2] + bg
            dzv = dz_ref[here, :].astype(F32)
            s = _sigmoid(gt)
            silu = gt * s
            da = dzv * silu
            dg = (dzv * a) * (s + silu * (1.0 - s))
            dua_ref[here, :] = da
            dug_ref[here, :] = dg
            terms = (da, da * ap, da * ac, da * an, dg, dg * gp, dg * gc, dg * gn)
            return tuple(r + _fold8(p) for r, p in zip(acc, terms))

        zero = jnp.zeros((8, 128), F32)
        acc = _over_grid_rows(n_rows, pass1, (zero,) * 8)
        for part in range(2):
            db_ref[part] += _rowsum(acc[4 * part])
            for k in range(3):
                dw_ref[part, k:k + 1, :] += _rowsum(acc[4 * part + 1 + k])

        def pass2(i, carry, first, last):
            for part, (ref, w) in enumerate(((dua_ref, wa), (dug_ref, wg))):
                dp_, dc_, dn_ = _grid_row(ref, i, first, last)
                du_ref[part, _row_ds(i), :] = (dn_ * w[0] + dc_ * w[1] + dp_ * w[2]).astype(BF16)
            return carry

        _over_grid_rows(n_rows, pass2, 0)

    col = lambda rows, part: pl.BlockSpec((rows, 128), lambda j, b: (0, part * nc + j))
    part_of_u = lambda part: pl.BlockSpec((1, t, 128), lambda j, b: (part, b, j))
    return pl.pallas_call(
        body, name=name, grid=(nc, nb),
        in_specs=[part_of_u(0), part_of_u(1), col(3, 0), col(3, 1), col(1, 0), col(1, 1),
                  pl.BlockSpec((t, 128), lambda j, b: (b, j))],
        out_specs=[pl.BlockSpec((2, t, 128), lambda j, b: (0, b, j)), pl.BlockSpec((2, 3, 128), lambda j, b: (0, 0, j)),
                   pl.BlockSpec((2, 1, 128), lambda j, b: (0, 0, j))],
        out_shape=[jax.ShapeDtypeStruct((2, nb * t, HID), BF16), jax.ShapeDtypeStruct((2, 3, HID), F32),
                   jax.ShapeDtypeStruct((2, 1, HID), F32)],
        scratch_shapes=[pltpu.VMEM((t, 128), F32), pltpu.VMEM((t, 128), F32)],
        compiler_params=_params(("parallel", "arbitrary")),
    )(u0, u0, cw, cw, cb, cb, dz)


def _sc_in_mid_fwd(hn, wg, off, cw, nb, t):
    tcol = 256
    ncol = D // tcol
    rows_sh = 3 * D // N_CHIPS

    def body(h_ref, wb_ref, wc_ref, wv_ref, cw_ref, p_ref, y_ref):
        hv = h_ref[...]
        bg = _dot(hv, wb_ref[0], _NT)
        cg = _dot(hv, wc_ref[0], _NT)
        v = _dot(hv, wv_ref[0], _NT)
        p_ref[0] = bg.astype(BF16)
        p_ref[1] = cg.astype(BF16)
        p_ref[2] = v.astype(BF16)
        cv = cg * v
        cc = _row_dn1(cv) * cw_ref[0:1, :] + cv * cw_ref[1:2, :] + _row_up1(cv) * cw_ref[2:3, :]
        y_ref[...] = (bg * cc).astype(BF16)

    def w_spec(part):
        def idx(b, j):
            n = part * D + j * tcol
            return (n // rows_sh, (off + n % rows_sh) // tcol, 0)
        return pl.BlockSpec((1, tcol, D), idx)

    return pl.pallas_call(
        body, name="sc_in_mid", grid=(nb, ncol),
        in_specs=[pl.BlockSpec((t, D), lambda b, j: (b, 0)), w_spec(0), w_spec(1), w_spec(2),
                  pl.BlockSpec((3, tcol), lambda b, j: (0, j))],
        out_specs=[pl.BlockSpec((3, t, tcol), lambda b, j: (0, b, j)), pl.BlockSpec((t, tcol), lambda b, j: (b, j))],
        out_shape=[jax.ShapeDtypeStruct((3, nb * t, D), BF16), jax.ShapeDtypeStruct((nb * t, D), BF16)],
        compiler_params=_params(("parallel", "parallel")),
    )(hn, wg, wg, wg, cw)


def _sc_mid_bwd(p, cw, dyb, nb, t):
    nc = D // 128

    def body(bg3_ref, cg3_ref, v3_ref, w_ref, dy_ref, dp_ref, dw_ref):
        bg_ref, cg_ref, v_ref = bg3_ref.at[0], cg3_ref.at[0], v3_ref.at[0]
        b = pl.program_id(1)

        @pl.when(b == 0)
        def _():
            dw_ref[...] = jnp.zeros_like(dw_ref)

        w0, w1, w2 = w_ref[0:1, :], w_ref[1:2, :], w_ref[2:3, :]
        cg, v = cg_ref[...].astype(F32), v_ref[...].astype(F32)
        cv = cg * v
        cvd = _row_dn1(cv)
        cvu = _row_up1(cv)
        cc = cvd * w0 + cv * w1 + cvu * w2
        dy = dy_ref[...].astype(F32)
        dcc = dy * bg_ref[...].astype(F32)
        dw_ref[0:1, :] += _rowsum(dcc * cvd)
        dw_ref[1:2, :] += _rowsum(dcc * cv)
        dw_ref[2:3, :] += _rowsum(dcc * cvu)
        dcv = _row_up1(dcc) * w0 + dcc * w1 + _row_dn1(dcc) * w2
        dp_ref[0] = (dy * cc).astype(BF16)
        dp_ref[1] = (dcv * v).astype(BF16)
        dp_ref[2] = (dcv * cg).astype(BF16)

    part = lambda k: pl.BlockSpec((1, t, 128), lambda j, b: (k, b, j))
    return pl.pallas_call(
        body, name="sc_mid_bwd", grid=(nc, nb),
        in_specs=[part(0), part(1), part(2), pl.BlockSpec((3, 128), lambda j, b: (0, j)),
                  pl.BlockSpec((t, 128), lambda j, b: (b, j))],
        out_specs=[pl.BlockSpec((3, t, 128), lambda j, b: (0, b, j)), pl.BlockSpec((3, 128), lambda j, b: (0, j))],
        out_shape=[jax.ShapeDtypeStruct((3, nb * t, D), BF16), jax.ShapeDtypeStruct((3, D), F32)],
        compiler_params=_params(("parallel", "arbitrary")),
    )(p, p, p, cw, dyb)


def _gla_in_proj(hn_all, w_gin, w2, b2):
    n = hn_all.shape[0]
    tm = _tile(n, 768, 128)

    def body(h_ref, w_ref, w2_ref, b2_ref, p_ref, la_ref):
        p = _dot(h_ref[...], w_ref[...], _NT)
        p_ref[...] = p
        z = _dot(p[:, 2 * KEY + 2 * D:], w2_ref[...]) + b2_ref[...]
        la_ref[...] = (jnp.minimum(z, 0.0) - jnp.log(1.0 + jnp.exp(-jnp.abs(z)))) * (1.0 / TAU)

    return pl.pallas_call(
        body, name="gla_in_proj", grid=(n // tm,),
        in_specs=[pl.BlockSpec((tm, D), lambda i: (i, 0)), pl.BlockSpec((GLA_IN_PAD, D), lambda i: (0, 0)),
                  pl.BlockSpec((128, 2 * KEY), lambda i: (0, 0)), pl.BlockSpec((1, 2 * KEY), lambda i: (0, 0))],
        out_specs=[pl.BlockSpec((tm, GLA_IN_PAD), lambda i: (i, 0)), pl.BlockSpec((tm, 2 * KEY), lambda i: (i, 0))],
        out_shape=[jax.ShapeDtypeStruct((n, GLA_IN_PAD), F32), jax.ShapeDtypeStruct((n, 2 * KEY), F32)],
        compiler_params=_params(("parallel",)),
    )(hn_all, w_gin, w2, b2)


def _gla_blocks(nb, nm, ncx):
    def main_idx(d, i):
        return jnp.clip(jnp.where(d == 0, i - ncx, nm - 1 - (i - ncx)), 0, nm - 1)

    def rowblk(d, b, i):
        cidx = jnp.where(d == 0, i, ncx - 1 - i)
        return jnp.where(i < ncx, nb * nm + b * ncx + cidx, b * nm + main_idx(d, i))

    def mainblk(d, b, i):
        return b * nm + main_idx(d, i)

    return rowblk, mainblk


def _gla_mask(d):
    row = lax.broadcasted_iota(jnp.int32, (CH, CH), 0)
    col = lax.broadcasted_iota(jnp.int32, (CH, CH), 1)
    diff = jnp.where(d == 0, row - col, col - row)
    mask = diff >= 0
    return mask, jnp.where(mask, 1.0, 0.0).astype(BF16), jnp.where(diff <= 0, 1.0, 0.0).astype(BF16)


def _tri_sum(m01, x):
    w = x.shape[1]
    hi = x.astype(BF16)
    r1 = x - hi.astype(F32)
    mid = r1.astype(BF16)
    lo = (r1 - mid.astype(F32)).astype(BF16)
    s = lax.dot_general(m01, jnp.concatenate([hi, mid, lo], axis=1), _NN, preferred_element_type=F32)
    return s[:, :w] + s[:, w:2 * w] + s[:, 2 * w:]


def _gla_chunk(q, k, g, bc):
    bl = _rowsum(g)
    eq = jnp.exp(bc)
    ek = jnp.exp(-bc)
    ed = jnp.exp(bl - bc)
    return bl, eq, ek, ed, q * Q_SCALE * eq, k * ek, k * ed


def _gla_scan_fwd(p_all, la_all, nb, t, tc):
    nm, ncx = t // CH, tc // CH
    nst = nm + ncx
    rowblk, mainblk = _gla_blocks(nb, nm, ncx)

    def body(*refs):
        ins, (o_refs, ss_refs, st_ref) = refs[:8], (refs[8:10], refs[10:12], refs[12])
        i = pl.program_id(1)

        @pl.when(i == 0)
        def _():
            st_ref[...] = jnp.zeros_like(st_ref)

        loaded = [r[...] for r in ins]
        states = [st_ref[j] for j in range(2 * HEADS)]
        outs, new_states = [[], []], []
        for d in range(2):
            q_all, k_all, v_all, g_all = loaded[4 * d:4 * d + 4]
            mask, m01, _ = _gla_mask(d)
            bc_all = _tri_sum(m01, g_all)
            for h in range(HEADS):
                ksl = slice(h * DK, (h + 1) * DK)
                v = v_all[:, h * DV:(h + 1) * DV]
                st = states[d * HEADS + h]
                bl, _, _, _, qs, ks, kd = _gla_chunk(q_all[:, ksl], k_all[:, ksl], g_all[:, ksl], bc_all[:, ksl])
                att = jnp.where(mask, _dot(qs, ks, _NT), 0.0)
                outs[d].append(_dot(qs, st, _NT) + _dot(att, v))
                new_states.append(st * jnp.exp(bl) + _dot(v, kd, _TN))
        for d in range(2):
            o_refs[d][...] = jnp.concatenate(outs[d], axis=1)
            for h in range(HEADS):
                ss_refs[d][0, 0, h] = states[d * HEADS + h]
                st_ref[d * HEADS + h] = new_states[d * HEADS + h]

    def in_specs(d):
        return [pl.BlockSpec((CH, KEY), lambda b, i: (rowblk(d, b, i), 0)),
                pl.BlockSpec((CH, KEY), lambda b, i: (rowblk(d, b, i), 1)),
                pl.BlockSpec((CH, D), lambda b, i: (rowblk(d, b, i), 1)),
                pl.BlockSpec((CH, KEY), lambda b, i: (rowblk(d, b, i), d))]

    outs = pl.pallas_call(
        body, name="gla_scan_fwd", grid=(nb, nst),
        in_specs=in_specs(0) + in_specs(1),
        out_specs=[pl.BlockSpec((CH, D), lambda b, i: (mainblk(0, b, i), 0)),
                   pl.BlockSpec((CH, D), lambda b, i: (mainblk(1, b, i), 0)),
                   pl.BlockSpec((1, 1, HEADS, DV, DK), lambda b, i: (b, i, 0, 0, 0)),
                   pl.BlockSpec((1, 1, HEADS, DV, DK), lambda b, i: (b, i, 0, 0, 0))],
        out_shape=[jax.ShapeDtypeStruct((nb * t, D), F32)] * 2
        + [jax.ShapeDtypeStruct((nb, nst, HEADS, DV, DK), F32)] * 2,
        scratch_shapes=[pltpu.VMEM((2 * HEADS, DV, DK), F32)],
        compiler_params=_params(("parallel", "arbitrary")),
    )(*([p_all, p_all, p_all, la_all] * 2))
    return outs[:2], outs[2:]


def _gla_scan_bwd(p_all, la_all, do, ss, nb, t, tc, after):
    nm, ncx = t // CH, tc // CH
    nst = nm + ncx
    ntot = nb * (t + tc)
    rowblk, mainblk = _gla_blocks(nb, nm, ncx)

    def body(*refs):
        ins, outs, dst_ref = refs[:12], refs[13:21], refs[21]
        ip = pl.program_id(1)
        i = nst - 1 - ip

        @pl.when(ip == 0)
        def _():
            dst_ref[...] = jnp.zeros_like(dst_ref)

        live = jnp.where(i >= ncx, 1.0, 0.0)
        loaded = [[r[...] for r in ins[6 * d:6 * d + 5]] for d in range(2)]
        states = [ins[6 * d + 5][0, 0, h] for d in range(2) for h in range(HEADS)]
        dstates = [dst_ref[j] for j in range(2 * HEADS)]
        results, new_dstates = [], []
        for d in range(2):
            q_all, k_all, v_all, g_all, do_all = loaded[d]
            do_all = do_all * live
            mask, m01, m01_t = _gla_mask(d)
            bc_all = _tri_sum(m01, g_all)
            dqs_l, dks_l, dvs_l, dbs_l, dbls_l = [], [], [], [], []
            for h in range(HEADS):
                ksl = slice(h * DK, (h + 1) * DK)
                vsl = slice(h * DV, (h + 1) * DV)
                bl, eq, ek, ed, qs, ks, kd = _gla_chunk(q_all[:, ksl], k_all[:, ksl], g_all[:, ksl], bc_all[:, ksl])
                st, dst, v, dov = states[d * HEADS + h], dstates[d * HEADS + h], v_all[:, vsl], do_all[:, vsl]
                att = jnp.where(mask, _dot(qs, ks, _NT), 0.0)
                datt = jnp.where(mask, _dot(dov, v, _NT), 0.0)
                dqs = _dot(dov, st) + _dot(datt, ks)
                dks = _dot(datt, qs, _TN)
                dvs_l.append(_dot(att, dov, _TN) + _dot(kd, dst, _NT))
                dkd = _dot(v, dst)
                e = jnp.exp(bl)
                dbls_l.append(e * _rowsum(st * dst) + _rowsum(dkd * kd))
                new_dstates.append(_dot(dov, qs, _TN) + dst * e)
                dqs_l.append(dqs * eq * Q_SCALE)
                dks_l.append(dks * ek + dkd * ed)
                dbs_l.append(dqs * qs - dks * ks - dkd * kd)
            results.append((jnp.concatenate(dqs_l, axis=1), jnp.concatenate(dks_l, axis=1),
                            jnp.concatenate(dvs_l, axis=1),
                            _tri_sum(m01_t, jnp.concatenate(dbs_l, axis=1)) + jnp.concatenate(dbls_l, axis=1)))
        for d in range(2):
            for k in range(4):
                outs[4 * d + k][...] = results[d][k]
        for j in range(2 * HEADS):
            dst_ref[j] = new_dstates[j]

    def in_specs(d):
        return [pl.BlockSpec((CH, KEY), lambda b, ip: (rowblk(d, b, nst - 1 - ip), 0)),
                pl.BlockSpec((CH, KEY), lambda b, ip: (rowblk(d, b, nst - 1 - ip), 1)),
                pl.BlockSpec((CH, D), lambda b, ip: (rowblk(d, b, nst - 1 - ip), 1)),
                pl.BlockSpec((CH, KEY), lambda b, ip: (rowblk(d, b, nst - 1 - ip), d)),
                pl.BlockSpec((CH, D), lambda b, ip: (mainblk(d, b, nst - 1 - ip), 0)),
                pl.BlockSpec((1, 1, HEADS, DV, DK), lambda b, ip: (b, nst - 1 - ip, 0, 0, 0))]

    def out_specs(d):
        row = lambda width: pl.BlockSpec((CH, width), lambda b, ip: (rowblk(d, b, nst - 1 - ip), 0))
        return [row(KEY), row(KEY), row(D), row(KEY)]

    shapes = [jax.ShapeDtypeStruct((ntot, KEY), F32), jax.ShapeDtypeStruct((ntot, KEY), F32),
              jax.ShapeDtypeStruct((ntot, D), F32), jax.ShapeDtypeStruct((ntot, KEY), F32)]
    outs = pl.pallas_call(
        body, name="gla_scan_bwd", grid=(nb, nst),
        in_specs=in_specs(0) + in_specs(1) + [pl.BlockSpec(memory_space=pl.ANY)],
        out_specs=out_specs(0) + out_specs(1),
        out_shape=shapes * 2,
        scratch_shapes=[pltpu.VMEM((2 * HEADS, DV, DK), F32)],
        compiler_params=_params(("parallel", "arbitrary")),
    )(p_all, p_all, p_all, la_all, do, ss[0], p_all, p_all, p_all, la_all, do, ss[1], after)
    return [[outs[k], outs[4 + k]] for k in range(4)]


def _gla_post_fwd(o2, p_all, head_gain, n):
    tt = _tile(n, 256)

    def body(of_ref, ob_ref, g_ref, hg_ref, y_ref):
        o = of_ref[...] + ob_ref[...]
        gv = g_ref[...]
        hg = hg_ref[...]
        for h in range(HEADS):
            oh = o[:, h * DV:(h + 1) * DV]
            r = lax.rsqrt(jnp.mean(oh * oh, axis=-1, keepdims=True) + EPS)
            y_ref[:, h * DV:(h + 1) * DV] = ((oh * r) * hg * _silu(gv[:, h * DV:(h + 1) * DV])).astype(BF16)

    row = pl.BlockSpec((tt, D), lambda i: (i, 0))
    return pl.pallas_call(
        body, name="gla_post_fwd", grid=(n // tt,),
        in_specs=[row, row, pl.BlockSpec((tt, D), lambda i: (i, 2)), pl.BlockSpec((1, DV), lambda i: (0, 0))],
        out_specs=row,
        out_shape=jax.ShapeDtypeStruct((n, D), BF16),
        compiler_params=_params(("parallel",)),
    )(o2[0], o2[1], p_all, head_gain)


def _gla_out_dx_post_bwd(dy, wg, off, o2, p_all, head_gain, n):
    tt = _tile(n, 256)

    def body(dy_ref, w_ref, of_ref, ob_ref, g_ref, hg_ref, do_ref, dg_ref, dhg_ref):
        i = pl.program_id(0)

        @pl.when(i == 0)
        def _():
            dhg_ref[...] = jnp.zeros_like(dhg_ref)

        dyv = dy_ref[...]
        o = of_ref[...] + ob_ref[...]
        gv = g_ref[...]
        hg = hg_ref[...]
        acc = jnp.zeros((1, DV), F32)
        for h in range(HEADS):
            sl = slice(h * DV, (h + 1) * DV)
            dyh = _dot(dyv, w_ref[h], _NT)
            oh = o[:, sl]
            r = lax.rsqrt(jnp.mean(oh * oh, axis=-1, keepdims=True) + EPS)
            on = oh * r
            gh = gv[:, sl]
            dg_ref[:, sl] = dyh * (on * hg) * _dsilu(gh)
            dog = dyh * _silu(gh)
            acc = acc + _rowsum(dog * on)
            don = dog * hg
            do_ref[:, sl] = r * (don - on * jnp.mean(don * on, axis=-1, keepdims=True))
        dhg_ref[...] += acc

    row = pl.BlockSpec((tt, D), lambda i: (i, 0))
    return pl.pallas_call(
        body, name="gla_out_dx_post_bwd", grid=(n // tt,),
        in_specs=[row, pl.BlockSpec((N_CHIPS, DV, D), lambda i: (0, off // DV, 0)), row, row,
                  pl.BlockSpec((tt, D), lambda i: (i, 2)), pl.BlockSpec((1, DV), lambda i: (0, 0))],
        out_specs=[row, row, pl.BlockSpec((1, DV), lambda i: (0, 0))],
        out_shape=[jax.ShapeDtypeStruct((n, D), F32), jax.ShapeDtypeStruct((n, D), F32),
                   jax.ShapeDtypeStruct((1, DV), F32)],
        compiler_params=_params(("arbitrary",)),
    )(dy, wg, o2[0], o2[1], p_all, head_gain)


def _gla_in_dx_mod(dp, w_gin, xf, cf, dh_out, gain, scale, scale_ctx, t, tc):
    n, nc = xf.shape[0], cf.shape[0]
    nb = n // t
    tm = _tile(tc, 256, 16)
    nmain, tpb = n // tm, t // tm

    def body(dp_ref, w_ref, x_ref, c_ref, dho_ref, gain_ref, sc_ref, scc_ref,
             dh_ref, dsc_ref, dsh_ref, dshc_ref, dscc_ref, dgain_ref):
        i = pl.program_id(0)
        is_main = i < nmain

        @pl.when(i == 0)
        def _():
            dgain_ref[...] = jnp.zeros_like(dgain_ref)
            dshc_ref[...] = jnp.zeros_like(dshc_ref)
            dscc_ref[...] = jnp.zeros_like(dscc_ref)

        @pl.when(is_main & (i % tpb == 0))
        def _():
            dsc_ref[...] = jnp.zeros_like(dsc_ref)
            dsh_ref[...] = jnp.zeros_like(dsh_ref)

        g = _dot(dp_ref[...], w_ref[...])
        hv = jnp.where(is_main, x_ref[...], c_ref[...])
        sc = jnp.where(is_main, sc_ref[0], scc_ref[0])
        r = lax.rsqrt(jnp.mean(hv * hv, axis=-1, keepdims=True) + EPS)
        y = hv * r
        gain_v = gain_ref[...]
        sum_g = _rowsum(g)
        sum_gy = _rowsum(g * (y * gain_v))
        drn = g * (1.0 + sc)
        dgain_ref[...] += _rowsum(drn * y)

        @pl.when(is_main)
        def _():
            dsh_ref[0] += sum_g
            dsc_ref[0] += sum_gy
            dy = drn * gain_v
            dh_ref[...] = r * (dy - y * jnp.mean(dy * y, axis=-1, keepdims=True)) + dho_ref[...]

        @pl.when(jnp.logical_not(is_main))
        def _():
            dshc_ref[...] += sum_g
            dscc_ref[...] += sum_gy

    main_row = pl.BlockSpec((tm, D), lambda i: (jnp.minimum(i, nmain - 1), 0))
    per_b = pl.BlockSpec((1, 1, D), lambda i: (jnp.minimum(i, nmain - 1) // tpb, 0, 0))
    vec = pl.BlockSpec((1, D), lambda i: (0, 0))
    per_b_shape = jax.ShapeDtypeStruct((nb, 1, D), F32)
    vec_shape = jax.ShapeDtypeStruct((1, D), F32)
    return pl.pallas_call(
        body, name="gla_in_dx_mod", grid=((n + nc) // tm,),
        in_specs=[pl.BlockSpec((tm, GLA_IN_PAD), lambda i: (i, 0)), pl.BlockSpec((GLA_IN_PAD, D), lambda i: (0, 0)),
                  main_row, pl.BlockSpec((tm, D), lambda i: (jnp.maximum(i - nmain, 0), 0)), main_row, vec, per_b,
                  pl.BlockSpec((1, 1, D), lambda i: (0, 0, 0))],
        out_specs=[main_row, per_b, per_b, vec, vec, vec],
        out_shape=[jax.ShapeDtypeStruct((n, D), F32), per_b_shape, per_b_shape, vec_shape, vec_shape, vec_shape],
        compiler_params=_params(("arbitrary",)),
    )(dp, w_gin, xf, cf, dh_out, gain, scale, scale_ctx)


def _gla_assemble(p_all, w2, b2, dq, dk, dv, dla, dgate, n):
    ntot = p_all.shape[0]
    tt = _tile(n, 128)
    nmain = n // tt
    assert ntot % tt == 0

    def body(a_ref, w_ref, b_ref, dqf_ref, dqb_ref, dkf_ref, dkb_ref, dvf_ref, dvb_ref, dlf_ref, dlb_ref, dg_ref,
             dp_ref, dw_ref, db_ref):
        i = pl.program_id(0)

        @pl.when(i == 0)
        def _():
            dw_ref[...] = jnp.zeros_like(dw_ref)
            db_ref[...] = jnp.zeros_like(db_ref)

        a = a_ref[...]
        w = w_ref[...]
        z = _dot(a, w) + b_ref[...]
        dla = jnp.concatenate([dlf_ref[...], dlb_ref[...]], axis=1)
        dz = dla * (1.0 / (1.0 + jnp.exp(z))) * (1.0 / TAU)
        dw_ref[...] += _dot(a, dz, _TN)
        db_ref[...] += _rowsum(dz)
        dp_ref[:, 0:KEY] = (dqf_ref[...] + dqb_ref[...]).astype(BF16)
        dp_ref[:, KEY:2 * KEY] = (dkf_ref[...] + dkb_ref[...]).astype(BF16)
        dp_ref[:, 2 * KEY:2 * KEY + D] = (dvf_ref[...] + dvb_ref[...]).astype(BF16)
        dp_ref[:, 2 * KEY + D:2 * KEY + 2 * D] = (dg_ref[...] * jnp.where(i < nmain, 1.0, 0.0)).astype(BF16)
        dp_ref[:, 2 * KEY + 2 * D:GLA_IN_PAD] = _dot(dz, w, _NT).astype(BF16)

    row = lambda width: pl.BlockSpec((tt, width), lambda i: (i, 0))
    return pl.pallas_call(
        body, name="gla_assemble", grid=(ntot // tt,),
        in_specs=[pl.BlockSpec((tt, 128), lambda i: (i, (2 * KEY + 2 * D) // 128)),
                  pl.BlockSpec((128, 2 * KEY), lambda i: (0, 0)), pl.BlockSpec((1, 2 * KEY), lambda i: (0, 0)),
                  row(KEY), row(KEY), row(KEY), row(KEY), row(D), row(D), row(KEY), row(KEY),
                  pl.BlockSpec((tt, D), lambda i: (jnp.minimum(i, nmain - 1), 0))],
        out_specs=[pl.BlockSpec((tt, GLA_IN_PAD), lambda i: (i, 0)), pl.BlockSpec((128, 2 * KEY), lambda i: (0, 0)),
                   pl.BlockSpec((1, 2 * KEY), lambda i: (0, 0))],
        out_shape=[jax.ShapeDtypeStruct((ntot, GLA_IN_PAD), BF16), jax.ShapeDtypeStruct((128, 2 * KEY), F32),
                   jax.ShapeDtypeStruct((1, 2 * KEY), F32)],
        compiler_params=_params(("arbitrary",)),
    )(p_all, w2, b2, dq[0], dq[1], dk[0], dk[1], dv[0], dv[1], dla[0], dla[1], dgate)


ADA_ROWS = 24
ADA_SH = N_MOD * D // N_CHIPS


def _ada_fwd(cvec, ada_w, ada_b_sh):
    def body(c_ref, w_ref, b_ref, o_ref):
        o_ref[0] = _dot(_silu(c_ref[...]), w_ref[0]) + b_ref[0]

    return pl.pallas_call(
        body, name="ada_fwd", grid=(2,),
        in_specs=[pl.BlockSpec((ADA_ROWS, D), lambda l: (0, 0)), pl.BlockSpec((1, D, ADA_SH), lambda l: (l, 0, 0)),
                  pl.BlockSpec((1, 1, ADA_SH), lambda l: (l, 0, 0))],
        out_specs=pl.BlockSpec((1, ADA_ROWS, ADA_SH), lambda l: (l, 0, 0)),
        out_shape=jax.ShapeDtypeStruct((2, ADA_ROWS, ADA_SH), F32),
        compiler_params=_params(("parallel",)),
    )(cvec, ada_w, ada_b_sh)


def _ada_bwd(cvec, ada_w, dmod_sh):
    def body(c_ref, w_ref, dm_ref, gw_ref, dc_ref):
        dm = dm_ref[0]
        gw_ref[0] = _dot(_silu(c_ref[...]), dm, _TN)
        dc_ref[0] = _dot(dm, w_ref[0], _NT)

    return pl.pallas_call(
        body, name="ada_bwd", grid=(2,),
        in_specs=[pl.BlockSpec((ADA_ROWS, D), lambda l: (0, 0)), pl.BlockSpec((1, D, ADA_SH), lambda l: (l, 0, 0)),
                  pl.BlockSpec((1, ADA_ROWS, ADA_SH), lambda l: (l, 0, 0))],
        out_specs=[pl.BlockSpec((1, D, ADA_SH), lambda l: (l, 0, 0)), pl.BlockSpec((1, ADA_ROWS, D), lambda l: (l, 0, 0))],
        out_shape=[jax.ShapeDtypeStruct((2, D, ADA_SH), F32), jax.ShapeDtypeStruct((2, ADA_ROWS, D), F32)],
        compiler_params=_params(("parallel",)),
    )(cvec, ada_w, dmod_sh)


def _sum_slots(x, name):
    s, r, _ = x.shape

    def body(x_ref, o_ref):
        acc = x_ref[0]
        for k in range(1, s):
            acc = acc + x_ref[k]
        o_ref[...] = acc

    return pl.pallas_call(
        body, name=name, out_shape=jax.ShapeDtypeStruct((r, 128), F32),
        in_specs=[pl.BlockSpec(memory_space=pltpu.VMEM)], out_specs=pl.BlockSpec(memory_space=pltpu.VMEM),
    )(x)


def _cctx_grad(dscc_parts, c_ctx):
    def body(p_ref, c_ref, o_ref):
        acc = p_ref[0]
        for k in range(1, N_CHIPS):
            acc = acc + p_ref[k]
        o_ref[...] = acc * _dsilu(c_ref[...])

    return pl.pallas_call(
        body, name="cctx_grad", out_shape=jax.ShapeDtypeStruct((8, 128), F32),
        in_specs=[pl.BlockSpec(memory_space=pltpu.VMEM)] * 2, out_specs=pl.BlockSpec(memory_space=pltpu.VMEM),
    )(dscc_parts, c_ctx)


def _adamw(w, g, m, v, name, after):
    nl, r, cdim = w.shape
    tr = _tile(r, 256)
    c1 = 1.0 - ADAM_B1 ** ADAM_STEP
    c2 = 1.0 - ADAM_B2 ** ADAM_STEP

    def body(w_ref, g_ref, m_ref, v_ref, after_ref, d_ref, mo_ref, vo_ref):
        gv = g_ref[...]
        mn = ADAM_B1 * m_ref[...] + (1.0 - ADAM_B1) * gv
        vn = ADAM_B2 * v_ref[...] + (1.0 - ADAM_B2) * (gv * gv)
        mo_ref[...] = mn
        vo_ref[...] = vn
        d_ref[...] = -ADAM_LR * ((mn / c1) / (jnp.sqrt(vn / c2) + ADAM_EPS) + ADAM_WD * w_ref[...])

    spec = pl.BlockSpec((1, tr, cdim), lambda l, i: (l, i, 0))
    sds = jax.ShapeDtypeStruct((nl, r, cdim), F32)
    return pl.pallas_call(
        body, name=name, grid=(nl, r // tr), in_specs=[spec] * 4 + [pl.BlockSpec(memory_space=pl.ANY)],
        out_specs=[spec] * 3, out_shape=[sds] * 3, compiler_params=_params(("parallel", "parallel")),
    )(w, g, m, v, after)


def _place():
    x, y, c = lax.axis_index("x"), lax.axis_index("y"), lax.axis_index("c")
    return x, y, c


def _allgather_small(blk, name):
    m_per, n = blk.shape

    def body(x_ref, out_ref, send_sems, recv_sems, local_sem):
        x, y, c = _place()
        me, sibling = (x, y, c), (x, y, 1 - c)
        chips = [(1 - x, y), (x, 1 - y), (1 - x, 1 - y)]

        def rows(px, py, pc):
            return out_ref.at[pl.ds((4 * px + 2 * py + pc) * m_per, m_per), :]

        def copy(k, block, to, src=None):
            return pltpu.make_async_remote_copy(
                src_ref=rows(*block) if src is None else src, dst_ref=rows(*block),
                send_sem=send_sems.at[k], recv_sem=recv_sems.at[k], device_id=to, device_id_type=MESH)

        mine = pltpu.make_async_copy(x_ref, rows(*me), local_sem)
        mine.start()
        first = [copy(0, me, sibling, src=x_ref)]
        first += [copy(1 + j, me, (*chip, c), src=x_ref) for j, chip in enumerate(chips)]
        for cp in first:
            cp.start()
        passed = [copy(4 + j, (*chip, c), sibling) for j, chip in enumerate(chips)]
        for j, chip in enumerate(chips):
            copy(1 + j, (*chip, c), me).wait_recv()
            passed[j].start()
        copy(0, sibling, me).wait_recv()
        for j, chip in enumerate(chips):
            copy(4 + j, (*chip, 1 - c), me).wait_recv()
        for cp in first + passed:
            cp.wait_send()
        mine.wait()

    return pl.pallas_call(
        body, name=name,
        out_shape=jax.ShapeDtypeStruct((N_DEV * m_per, n), blk.dtype),
        in_specs=[pl.BlockSpec(memory_space=pltpu.VMEM)],
        out_specs=pl.BlockSpec(memory_space=pltpu.VMEM),
        scratch_shapes=[pltpu.SemaphoreType.DMA((7,)), pltpu.SemaphoreType.DMA((7,)), pltpu.SemaphoreType.DMA],
    )(blk)


def _other_chips(x, y):
    return [(1 - x, y), (x, 1 - y), (1 - x, 1 - y)]


_HBM_SPEC = pl.BlockSpec(memory_space=pltpu.HBM)
_SEM_SPEC = pl.BlockSpec(memory_space=pltpu.SEMAPHORE)
_SPLIT_PARAMS = pltpu.CompilerParams(has_side_effects=pltpu.SideEffectType.DATAFLOW_SIDE_EFFECTING)


def _in_hbm(a):
    return pltpu.with_memory_space_constraint(a, pltpu.HBM)


def _ag_copies(own_ref, land_ref, send_sems, recv_sems):
    x, y, c = _place()
    chip = 2 * x + y
    hr = own_ref.shape[0] // 2

    def half(ch):
        return land_ref.at[ch, pl.ds(c * hr, hr), :]

    def copy(k, src, dst, to):
        return pltpu.make_async_remote_copy(src_ref=src, dst_ref=dst, send_sem=send_sems.at[k],
                                            recv_sem=recv_sems.at[k], device_id=to, device_id_type=MESH)

    sends, expects = [], []
    for j, (ox, oy) in enumerate(_other_chips(x, y)):
        sends.append(copy(j, own_ref.at[pl.ds(c * hr, hr), :], half(chip), (ox, oy, c)))
        expects.append(copy(j, half(2 * ox + oy), half(2 * ox + oy), (ox, oy, c)))
    own_slot = copy(3, own_ref, land_ref.at[chip], (x, y, 1 - c))
    return sends + [own_slot], expects + [own_slot]


def _sc_copies(p_ref, land_ref, send_sems, recv_sems):
    x, y, c = _place()
    chip = 2 * x + y
    sends, expects = [], []
    for j, (ox, oy) in enumerate(_other_chips(x, y)):
        och = 2 * ox + oy
        mk = lambda dst_slot: pltpu.make_async_remote_copy(
            src_ref=p_ref.at[och], dst_ref=land_ref.at[dst_slot], send_sem=send_sems.at[j],
            recv_sem=recv_sems.at[j], device_id=(ox, oy, c), device_id_type=MESH)
        sends.append(mk(chip))
        expects.append(mk(och))
    return sends, expects


def _pe_copies(g_ref, land_ref, send_sems, recv_sems):
    x, y, c = _place()
    hr = g_ref.shape[1] // 2
    cp = pltpu.make_async_remote_copy(
        src_ref=g_ref.at[:, pl.ds((1 - c) * hr, hr), :], dst_ref=land_ref, send_sem=send_sems.at[0],
        recv_sem=recv_sems.at[0], device_id=(x, y, 1 - c), device_id_type=MESH)
    return [cp], [cp]


def _pass_on_copies(unused_ref, land_ref, send_sems, recv_sems):
    x, y, c = _place()
    hr = land_ref.shape[1] // 2
    sends, expects = [], []
    for j, (ox, oy) in enumerate(_other_chips(x, y)):
        def mk(cc, j=j, och=2 * ox + oy):
            ref = land_ref.at[och, pl.ds(cc * hr, hr), :]
            return pltpu.make_async_remote_copy(src_ref=ref, dst_ref=ref, send_sem=send_sems.at[j],
                                                recv_sem=recv_sems.at[j], device_id=(x, y, 1 - c),
                                                device_id_type=MESH)
        sends.append(mk(c))
        expects.append(mk(1 - c))
    return sends, expects


def _pair_gather_copies(unused_ref, land_ref, send_sems, recv_sems):
    x, y, c = _place()
    hr = land_ref.shape[0] // 2

    def mk(cc):
        ref = land_ref.at[pl.ds(cc * hr, hr), :]
        return pltpu.make_async_remote_copy(src_ref=ref, dst_ref=ref, send_sem=send_sems.at[0],
                                            recv_sem=recv_sems.at[0], device_id=(x, y, 1 - c), device_id_type=MESH)
    return [mk(c)], [mk(1 - c)]


def _split_start(src, land, copies, n_copies, after, name):
    def body(src_ref, land_ref, after_ref, send_sems, recv_sems, src_thru, land_thru, token):
        for cp in copies(src_ref, land_ref, send_sems, recv_sems)[0]:
            cp.start()
        token[...] = jnp.zeros_like(token)

    if isinstance(land, tuple):
        land = lax.empty(land, src.dtype)
    land_shape = land.shape
    return pl.pallas_call(
        body, name=name,
        out_shape=(pltpu.SemaphoreType.DMA((n_copies,)), pltpu.SemaphoreType.DMA((n_copies,)),
                   pltpu.HBM(src.shape, src.dtype), pltpu.HBM(land_shape, land.dtype),
                   jax.ShapeDtypeStruct((8, 128), F32)),
        in_specs=(_HBM_SPEC, _HBM_SPEC, pl.BlockSpec(memory_space=pl.ANY)),
        out_specs=(_SEM_SPEC, _SEM_SPEC, _HBM_SPEC, _HBM_SPEC, pl.BlockSpec(memory_space=pltpu.VMEM)),
        input_output_aliases={0: 2, 1: 3}, compiler_params=_SPLIT_PARAMS,
    )(_in_hbm(src), _in_hbm(land), after)


def _split_wait(started, after, copies, name):
    send_sems, recv_sems, src_thru, land_thru, _ = started

    def body(src_ref, land_ref, send_sems, recv_sems, after_ref, src_dead, got_ref):
        sends, expects = copies(src_ref, land_ref, send_sems, recv_sems)
        for cp in sends:
            cp.wait_send()
        for cp in expects:
            cp.wait_recv()

    return pl.pallas_call(
        body, name=name,
        out_shape=(pltpu.HBM(src_thru.shape, src_thru.dtype), pltpu.HBM(land_thru.shape, land_thru.dtype)),
        in_specs=(_HBM_SPEC, _HBM_SPEC, _SEM_SPEC, _SEM_SPEC, pl.BlockSpec(memory_space=pl.ANY)),
        out_specs=(_HBM_SPEC, _HBM_SPEC), input_output_aliases={0: 0, 1: 1}, compiler_params=_SPLIT_PARAMS,
    )(src_thru, land_thru, send_sems, recv_sems, after)


def _ag_pass_on(land, name):
    hr = land.shape[1] // 2

    def body(in_ref, out_ref, send_sems, recv_sems):
        x, y, c = _place()

        def copy(j, ox, oy, cc):
            ref = out_ref.at[2 * ox + oy, pl.ds(cc * hr, hr), :]
            return pltpu.make_async_remote_copy(src_ref=ref, dst_ref=ref, send_sem=send_sems.at[j],
                                                recv_sem=recv_sems.at[j], device_id=(x, y, 1 - c),
                                                device_id_type=MESH)

        others = _other_chips(x, y)
        for j, (ox, oy) in enumerate(others):
            copy(j, ox, oy, c).start()
        for j, (ox, oy) in enumerate(others):
            copy(j, ox, oy, 1 - c).wait_recv()
        for j, (ox, oy) in enumerate(others):
            copy(j, ox, oy, c).wait_send()

    any_spec = pl.BlockSpec(memory_space=pl.ANY)
    return pl.pallas_call(
        body, name=name, out_shape=jax.ShapeDtypeStruct(land.shape, land.dtype),
        in_specs=[any_spec], out_specs=any_spec, input_output_aliases={0: 0},
        scratch_shapes=[pltpu.SemaphoreType.DMA((3,)), pltpu.SemaphoreType.DMA((3,))],
    )(land)


def _rs_pair_exchange(g, after, name):
    r = g.shape[1]
    hr = r // 2

    def body(g_ref, after_ref, got_ref, send_sem, recv_sem):
        x, y, c = _place()
        cp = pltpu.make_async_remote_copy(
            src_ref=g_ref.at[:, pl.ds((1 - c) * hr, hr), :], dst_ref=got_ref, send_sem=send_sem, recv_sem=recv_sem,
            device_id=(x, y, 1 - c), device_id_type=MESH)
        cp.start()
        cp.wait()

    any_spec = pl.BlockSpec(memory_space=pl.ANY)
    return pl.pallas_call(
        body, name=name,
        out_shape=jax.ShapeDtypeStruct((N_CHIPS, hr, D), F32),
        in_specs=[any_spec, any_spec], out_specs=any_spec,
        scratch_shapes=[pltpu.SemaphoreType.DMA, pltpu.SemaphoreType.DMA],
    )(g, after)


def _rs_chip_sum(place, g, got, name):
    r = g.shape[1]
    hr = r // 2
    tr = _tile(hr, 640, 16)
    nt = hr // tr

    def body(pl_ref, g_ref, got_ref, p16_ref, p32_ref):
        s = pl.program_id(1)
        p = g_ref[0] + got_ref[0]
        p16_ref[0] = p.astype(BF16)

        @pl.when(s == pl_ref[1])
        def _():
            p32_ref[...] = p

    return pl.pallas_call(
        body, name=name,
        grid_spec=pltpu.PrefetchScalarGridSpec(
            num_scalar_prefetch=1, grid=(nt, N_CHIPS),
            in_specs=[pl.BlockSpec((1, tr, D), lambda i, s, pr: (s, pr[0] * nt + i, 0)),
                      pl.BlockSpec((1, tr, D), lambda i, s, pr: (s, i, 0))],
            out_specs=[pl.BlockSpec((1, tr, D), lambda i, s, pr: (s, i, 0)),
                       pl.BlockSpec((tr, D), lambda i, s, pr: (i, 0))]),
        out_shape=[jax.ShapeDtypeStruct((N_CHIPS, hr, D), BF16), jax.ShapeDtypeStruct((hr, D), F32)],
        compiler_params=_params(("parallel", "arbitrary")),
    )(place, g, got)


def _rs_final_sum(place, parts, p32, name):
    hr = parts.shape[1]
    tr = _tile(hr, 640, 16)
    nt = hr // tr

    def body(pl_ref, a_ref, b_ref, c_ref, p32_ref, o_ref):
        o_ref[...] = ((p32_ref[...] + a_ref[0].astype(F32)) + b_ref[0].astype(F32)) + c_ref[0].astype(F32)

    def other(j):
        return pl.BlockSpec((1, tr, D), lambda i, pr: (j + jnp.where(pr[1] <= j, 1, 0), i, 0))

    return pl.pallas_call(
        body, name=name,
        grid_spec=pltpu.PrefetchScalarGridSpec(
            num_scalar_prefetch=1, grid=(nt,),
            in_specs=[other(0), other(1), other(2), pl.BlockSpec((tr, D), lambda i, pr: (i, 0))],
            out_specs=pl.BlockSpec((tr, D), lambda i, pr: (pr[0] * nt + i, 0))),
        out_shape=jax.ShapeDtypeStruct((2 * hr, D), F32),
        compiler_params=_params(("parallel",)),
    )(place, parts, parts, parts, p32)


def _rs_pair_gather(both, name):
    hr = both.shape[0] // 2

    def body(in_ref, out_ref, send_sem, recv_sem):
        x, y, c = _place()
        mine = out_ref.at[pl.ds(c * hr, hr), :]
        cp = pltpu.make_async_remote_copy(
            src_ref=mine, dst_ref=mine, send_sem=send_sem, recv_sem=recv_sem,
            device_id=(x, y, 1 - c), device_id_type=MESH)
        cp.start()
        theirs = out_ref.at[pl.ds((1 - c) * hr, hr), :]
        pltpu.make_async_remote_copy(
            src_ref=theirs, dst_ref=theirs, send_sem=send_sem, recv_sem=recv_sem,
            device_id=(x, y, 1 - c), device_id_type=MESH).wait_recv()
        cp.wait_send()

    any_spec = pl.BlockSpec(memory_space=pl.ANY)
    return pl.pallas_call(
        body, name=name,
        out_shape=jax.ShapeDtypeStruct(both.shape, F32),
        in_specs=[any_spec], out_specs=any_spec, input_output_aliases={0: 0},
        scratch_shapes=[pltpu.SemaphoreType.DMA, pltpu.SemaphoreType.DMA],
    )(both)


def _local_step(x, ctx, tgt, mods, mc, ag_gin, ag_main, place, small):
    nb, t, _ = x.shape
    tc = ctx.shape[1]
    n = nb * t
    nc = nb * tc
    xf = x.reshape(n, D)
    cf = ctx.reshape(nc, D)
    tf = tgt.reshape(n, D)
    vec = lambda a: a.reshape(1, -1)
    m = [[mods[l, :, k, :].reshape(nb, 1, D) for k in range(N_MOD)] for l in range(2)]
    mc_b = [jnp.broadcast_to(mc[k].reshape(1, 1, D), (nb, 1, D)) for k in range(2)]

    cw = [small["ffn_conv_w"][l] for l in range(2)]
    cb = [small["ffn_conv_b"][l].reshape(1, -1) for l in range(2)]
    w2 = jnp.zeros((128, 2 * KEY), F32)
    w2 = w2.at[0:RANK, 0:KEY].set(small["gla_w_a2"][0]).at[RANK:2 * RANK, KEY:].set(small["gla_w_a2"][1])
    b2 = small["gla_b_a"].reshape(1, 2 * KEY)
    hg = small["gla_head_norm"].reshape(1, DV)

    hn_all = _mod_fwd(xf, vec(small["norm_mix"][0]), m[0][0], m[0][1], t, "mod0_main", n + nc)
    hn_all = _mod_fwd(cf, vec(small["norm_mix"][0]), mc_b[0], mc_b[1], tc, "mod0_ctx", n + nc, into=hn_all, row0=n)
    gin = _ag_pass_on(_split_wait(ag_gin, hn_all, _ag_copies, "ag_gin_wait")[1], "ag_gin_pass_on")
    out_rows = _MAIN_ROWS["gla_out"]
    w_gin = jnp.pad(gin[:, out_rows:out_rows + _GIN_ROWS, :].reshape(GLA_IN, D), ((0, GLA_IN_PAD - GLA_IN), (0, 0)))
    p_all, la_all = _gla_in_proj(hn_all, w_gin, w2, b2)
    o2, ss = _gla_scan_fwd(p_all, la_all, nb, t, tc)
    yb0 = _gla_post_fwd(o2, p_all, hg, n)
    arrived = _split_wait(ag_main, yb0, _ag_copies, "ag_main_wait")[1]
    passing = _split_start(ag_main[4], arrived, _pass_on_copies, 3, yb0, "ag_main_pass_start")
    offs = _offsets(_MAIN, _MAIN_ROWS)
    woffs = _offsets(_WMAIN, _MAIN_ROWS)
    rows = _MAIN_ROWS

    def w_nt(a, k, name, out_dtype=BF16, tm=1024):
        return _mm_nt_w(a, wg, woffs[k], rows[k], name, tm, out_dtype)

    def w_nn_mod(a3, k, h, gate, gain, shift, scale, name):
        return _mm_nn_w_mod(a3, wg, woffs[k], rows[k], h, gate, vec(gain), shift, scale, t, name, 512)

    y0, h1, hn1 = _mm_nn_w_mod(yb0[None], gin, 0, out_rows, xf, m[0][2],
                               vec(small["norm_ffn"][0]) + passing[4][0:1, 0:1], m[0][3], m[0][4], t,
                               "gla_out_proj_mod", 512)
    wg = _split_wait(passing, hn1, _pass_on_copies, "ag_main_pass_wait")[1]
    u0, z0 = _ffn_up_mid_fwd(hn1, wg, woffs["up_t0"], cw[0], cb[0], nb, t, "ffn0_up_mid")
    f0, h2, hn2 = w_nn_mod(z0[None], "down0", h1, m[0][5], small["norm_mix"][1], m[1][0], m[1][1],
                           "ffn0_down_mod")
    p1, yb1 = _sc_in_mid_fwd(hn2, wg, woffs["sc_in_t"], small["sc_conv_w"], nb, t)
    y1, h3, hn3 = w_nn_mod(yb1[None], "sc_out", h2, m[1][2], small["norm_ffn"][1], m[1][3], m[1][4],
                           "sc_out_proj_mod")
    u1, z1 = _ffn_up_mid_fwd(hn3, wg, woffs["up_t1"], cw[1], cb[1], nb, t, "ffn1_up_mid")
    loss, dh4, df1, dm15, dfinal = _mm_nn_w_final(z1[None], wg, woffs["down1"], rows["down1"], h3, m[1][5],
                                                  vec(small["final_norm"]), tf, t, "ffn1_down_final", 512)

    gs = {}
    dmods = [[None] * N_MOD for _ in range(2)]
    dmods[1][5] = dm15

    def w_dw(a3, b, g_prev, k, name, tm):
        return _mm_dw(a3, b, g_prev, offs[k], rows[k], name, tm)

    def w_dx_mod(a3, k, h_in, dh_out, gain, scale, y_prev, gate_prev, name):
        return _mm_nn_w_modbwd(a3, wg, woffs[k], rows[k], h_in, dh_out, vec(gain), scale, y_prev, gate_prev, t,
                               name, 256)

    def ffn_bwd(l, df, u, z, hn, g_prev, h_in, dh_out, scale, y_prev, gate_prev):
        dz = w_nt(df, f"down{l}", f"ffn{l}_down_dx")
        g_acc = w_dw(z[None], df, g_prev, f"down{l}", f"ffn{l}_down_dw", 640)
        du, dcw, dcb = _ffn_mid_bwd(u, cw[l], cb[l], dz, nb, t, f"ffn{l}_mid_bwd")
        r = w_dx_mod(du, f"up_t{l}", h_in, dh_out, small["norm_ffn"][l], scale, y_prev, gate_prev,
                     f"ffn{l}_up_dx_mod")
        g_acc = w_dw(du, hn, g_acc, f"up_t{l}", f"ffn{l}_up_dw", 640)
        return r, g_acc, jnp.moveaxis(dcw, 0, 1).reshape(3, 2 * HID), dcb.reshape(2 * HID)

    r, g_acc, dcw1, dcb1 = ffn_bwd(1, df1, u1, z1, hn3, None, h3, dh4, m[1][4], y1, m[1][2])
    dh3, dmods[1][4], dmods[1][3], dnf1, dy1, dmods[1][2] = (r["dh"], r["dscale"], r["dshift"], r["dgain"],
                                                             r["dy_prev"], r["dgate_prev"])
    dyb1 = w_nt(dy1, "sc_out", "sc_out_dx")
    g_acc = w_dw(yb1[None], dy1, g_acc, "sc_out", "sc_out_dw", 256)
    dp1, dscw = _sc_mid_bwd(p1, small["sc_conv_w"], dyb1, nb, t)
    r = w_dx_mod(dp1, "sc_in_t", h2, dh3, small["norm_mix"][1], m[1][1], f0, m[0][5], "sc_in_dx_mod")
    g_acc = w_dw(dp1, hn2, g_acc, "sc_in_t", "sc_in_dw", 256)
    dh2, dmods[1][1], dmods[1][0], dnm1, df0, dmods[0][5] = (r["dh"], r["dscale"], r["dshift"], r["dgain"],
                                                             r["dy_prev"], r["dgate_prev"])
    r, g_acc, dcw0, dcb0 = ffn_bwd(0, df0, u0, z0, hn1, g_acc, h1, dh2, m[0][4], y0, m[0][2])
    dh1, dmods[0][4], dmods[0][3], dnf0, dy0, dmods[0][2] = (r["dh"], r["dscale"], r["dshift"], r["dgain"],
                                                             r["dy_prev"], r["dgate_prev"])
    g_packed = w_dw(yb0[None], dy0, g_acc, "gla_out", "gla_out_dw", 256)
    pair = _split_start(g_packed, (N_CHIPS, _MAIN_TOTAL // 2, D), _pe_copies, 1, dy0, "rs_main_pair_start")
    do, dgate, dhg = _gla_out_dx_post_bwd(dy0, gin, 0, o2, p_all, hg + pair[4][0:1, 0:1], n)
    g_packed, from_sibling = _split_wait(pair, do, _pe_copies, "rs_main_pair_wait")
    p16, p32 = _rs_chip_sum(place, g_packed, from_sibling, "rs_main_chip_sum")
    sc_main = _split_start(p16, p16.shape, _sc_copies, 3, p32, "rs_main_scatter_start")
    dq, dk, dv, dla = _gla_scan_bwd(p_all, la_all, do, ss, nb, t, tc, sc_main[4])
    dp, dw2, db2 = _gla_assemble(p_all, w2, b2, dq, dk, dv, dla, dgate, n)
    grad_x, dmods[0][1], dmods[0][0], dmc0, dmc1, dnm0 = _gla_in_dx_mod(
        dp, w_gin, xf, cf, dh1, vec(small["norm_mix"][0]), m[0][1], mc[1].reshape(1, 1, D), t, tc)
    dmc = jnp.concatenate([dmc0, dmc1], axis=0)
    g_gin = _mm(dp, hn_all, "tn", F32, "gla_in_dw", 640, 1024)
    landed = _split_wait(sc_main, g_gin, _sc_copies, "rs_main_scatter_wait")[1]
    g_main = _split_start(sc_main[4], _rs_final_sum(place, landed, p32, "rs_main_final_sum"),
                          _pair_gather_copies, 1, landed, "rs_main_gather_start")
    g_gin = jnp.pad(g_gin[:GLA_IN].reshape(N_CHIPS, _GIN_ROWS, D), ((0, 0), (0, _GIN_PAD - _GIN_ROWS), (0, 0)))
    from_sibling = _rs_pair_exchange(g_gin, g_main[4], "rs_gin_pair_exchange")
    p16_gin, p32_gin = _rs_chip_sum(place, g_gin, from_sibling, "rs_gin_chip_sum")

    gs["norm_mix"] = jnp.concatenate([dnm0, dnm1], axis=0)
    gs["norm_ffn"] = jnp.concatenate([dnf0, dnf1], axis=0)
    gs["final_norm"] = dfinal.reshape(D)
    gs["gla_w_a2"] = jnp.stack([dw2[0:RANK, 0:KEY], dw2[RANK:2 * RANK, KEY:]])
    gs["gla_b_a"] = db2.reshape(2, KEY)
    gs["gla_head_norm"] = dhg.reshape(DV)
    gs["sc_conv_w"] = dscw
    gs["ffn_conv_w"] = jnp.stack([dcw0, dcw1])
    gs["ffn_conv_b"] = jnp.stack([dcb0, dcb1])
    dmods_arr = jnp.stack([jnp.stack([dmods[l][k].reshape(nb, D) for k in range(N_MOD)], axis=1) for l in range(2)])
    return loss, grad_x.reshape(nb, t, D), g_main, p16_gin, p32_gin, gs, dmods_arr, dmc


def _pack(arrs):
    parts, meta, off = [], [], 0
    for a in arrs:
        r = a.size // 128
        rp = -(-r // 8) * 8
        a2 = a.reshape(r, 128).astype(F32)
        if rp != r:
            a2 = jnp.pad(a2, ((0, rp - r), (0, 0)))
        parts.append(a2)
        meta.append((off, r, a.shape))
        off += rp
    return jnp.concatenate(parts, axis=0), meta


def _unpack(buf, meta, lead=()):
    return [buf[..., off:off + r, :].reshape(*lead, *shape) for off, r, shape in meta]


_MAIN = ("up_t0", "up_t1", "down0", "down1", "sc_in_t", "gla_out", "sc_out")
_WMAIN = tuple(k for k in _MAIN if k != "gla_out")
_MAIN_ROWS = {"sc_in_t": 3 * D // N_CHIPS, "up_t0": 2 * HID // N_CHIPS, "up_t1": 2 * HID // N_CHIPS,
              "gla_out": D // N_CHIPS, "sc_out": D // N_CHIPS, "down0": HID // N_CHIPS, "down1": HID // N_CHIPS}
_MAIN_TOTAL = sum(_MAIN_ROWS.values())
_GIN_ROWS = GLA_IN // N_CHIPS
_GIN_PAD = -(-_GIN_ROWS // 32) * 32
_WMAIN_TOTAL = _MAIN_TOTAL - _MAIN_ROWS["gla_out"]
_GLA_W_ROWS = _MAIN_ROWS["gla_out"] + _GIN_ROWS
_GLA_W_PAD = -(-_GLA_W_ROWS // 32) * 32


def _offsets(names, rows):
    off, out = 0, {}
    for k in names:
        out[k] = off
        off += rows[k]
    return out


def kernel(x, c, ctx, c_ctx, ada_w, ada_b, norm_mix, norm_ffn, gla_w_in, gla_w_a2, gla_b_a, gla_head_norm, gla_w_out, sc_w_in, sc_conv_w, sc_w_out, ffn_w_up, ffn_conv_w, ffn_conv_b, ffn_w_down, final_norm, loss_target, m_c_ctx, m_ada_w, m_ada_b, m_norm_mix, m_norm_ffn, m_gla_w_in, m_gla_w_a2, m_gla_b_a, m_gla_head_norm, m_gla_w_out, m_sc_w_in, m_sc_conv_w, m_sc_w_out, m_ffn_w_up, m_ffn_conv_w, m_ffn_conv_b, m_ffn_w_down, m_final_norm, v_c_ctx, v_ada_w, v_ada_b, v_norm_mix, v_norm_ffn, v_gla_w_in, v_gla_w_a2, v_gla_b_a, v_gla_head_norm, v_gla_w_out, v_sc_w_in, v_sc_conv_w, v_sc_w_out, v_ffn_w_up, v_ffn_conv_w, v_ffn_conv_b, v_ffn_w_down, v_final_norm):
    ix, iy, ic = _place()
    chip = 2 * ix + iy
    dev = 2 * chip + ic
    place = jnp.stack([ic, chip]).astype(jnp.int32)
    nb = x.shape[0]
    offs = _offsets(_MAIN, _MAIN_ROWS)

    buf, meta = _pack([c, ffn_conv_w, sc_conv_w, gla_w_a2, gla_b_a])
    got = _allgather_small(buf, "gather_small_in").reshape(N_DEV, buf.shape[0], 128)
    c_all, fcw, scw, wa2, ba = _unpack(got, meta, (N_DEV,))
    c_all = c_all.reshape(N_DEV * nb, D)
    per_chip = lambda a: a[0::2]
    ffn_conv_w_full = jnp.moveaxis(per_chip(fcw), 0, 2).reshape(2, 3, 2 * HID)
    sc_conv_w_full = jnp.moveaxis(per_chip(scw)[:, 0], 0, 1).reshape(3, D)
    gla_w_a2_full = jnp.moveaxis(per_chip(wa2)[:, 0], 0, 2).reshape(2, RANK, KEY)
    gla_b_a_full = jnp.moveaxis(per_chip(ba)[:, 0], 0, 1).reshape(2, KEY)

    cvec = jnp.concatenate([c_all, c_ctx.reshape(1, D), jnp.zeros((ADA_ROWS - N_DEV * nb - 1, D), F32)], axis=0)
    ada_b_sh = lax.dynamic_slice_in_dim(ada_b, chip * ADA_SH, ADA_SH, axis=1).reshape(2, 1, ADA_SH)
    mod_sh = _ada_fwd(cvec, ada_w, ada_b_sh)
    got = _allgather_small(mod_sh.reshape(2 * ADA_ROWS, ADA_SH), "gather_mod")
    mod_full = jnp.moveaxis(per_chip(got.reshape(N_DEV, 2, ADA_ROWS, ADA_SH)), 0, 2).reshape(2, ADA_ROWS, N_MOD * D)
    mc = mod_full[0, N_DEV * nb, :2 * D].reshape(2, D)

    own = {"sc_in_t": sc_w_in[0].T, "up_t0": ffn_w_up[0].T, "up_t1": ffn_w_up[1].T,
           "gla_out": gla_w_out[0], "sc_out": sc_w_out[0], "down0": ffn_w_down[0], "down1": ffn_w_down[1]}
    own_main = jnp.concatenate([own[k].astype(BF16) for k in _WMAIN], axis=0)
    own_gin = jnp.concatenate([own["gla_out"].astype(BF16), gla_w_in[0].T.astype(BF16),
                               jnp.zeros((_GLA_W_PAD - _GLA_W_ROWS, D), BF16)], axis=0)
    ag_gin = _split_start(own_gin, (N_CHIPS, _GLA_W_PAD, D), _ag_copies, 4, mc, "ag_gin_start")
    ag_main = _split_start(own_main, (N_CHIPS, _WMAIN_TOTAL, D), _ag_copies, 4, ag_gin[4], "ag_main_start")
    mods = lax.dynamic_slice_in_dim(mod_full, dev * nb, nb, axis=1).reshape(2, nb, N_MOD, D) + ag_main[4][0, 0]

    small = {"norm_mix": norm_mix, "norm_ffn": norm_ffn, "final_norm": final_norm, "gla_w_a2": gla_w_a2_full,
             "gla_b_a": gla_b_a_full, "gla_head_norm": gla_head_norm[0], "sc_conv_w": sc_conv_w_full,
             "ffn_conv_w": ffn_conv_w_full, "ffn_conv_b": ffn_conv_b}
    loss_p, grad_x, g_main, p16_gin, p32_gin, gs, dmods, dmc = _local_step(x, ctx, loss_target, mods, mc, ag_gin,
                                                                           ag_main, place, small)

    sum_names = ["norm_mix", "norm_ffn", "final_norm", "gla_w_a2", "gla_b_a", "gla_head_norm", "sc_conv_w",
                 "ffn_conv_w", "ffn_conv_b"]
    buf, meta = _pack([jnp.broadcast_to(loss_p, (8, 128))] + [gs[k] for k in sum_names] + [dmc, dmods])
    n_sum = meta[-1][0]
    got = _allgather_small(buf, "gather_small_grads").reshape(N_DEV, buf.shape[0], 128)
    summed = _sum_slots(got[:, :n_sum], "sum_small_grads")
    parts = _unpack(summed, meta[:-1])
    loss = parts[0][0, 0]
    g_small = dict(zip(sum_names, parts[1:-1]))
    dmc_tot = parts[-1]
    dmods_all = jnp.moveaxis(_unpack(got, meta[-1:], (N_DEV,))[0], 0, 1).reshape(2, N_DEV * nb, N_MOD * D)

    ctx_row = jnp.stack([jnp.concatenate([dmc_tot.reshape(2 * D), jnp.zeros(((N_MOD - 2) * D,), F32)]),
                         jnp.zeros((N_MOD * D,), F32)]).reshape(2, 1, N_MOD * D)
    dmod_ext = jnp.concatenate([dmods_all, ctx_row, jnp.zeros((2, ADA_ROWS - N_DEV * nb - 1, N_MOD * D), F32)], axis=1)
    g_ada_b = _sum_slots(jnp.moveaxis(dmod_ext, 1, 0).reshape(ADA_ROWS, 2 * N_MOD * D // 128, 128),
                         "sum_ada_b").reshape(2, N_MOD * D)
    dmod_sh = lax.dynamic_slice_in_dim(dmod_ext, chip * ADA_SH, ADA_SH, axis=2)
    g_ada_w, dcv = _ada_bwd(cvec, ada_w, dmod_sh)
    dscc_part = (dcv[0, N_DEV * nb] + dcv[1, N_DEV * nb]).reshape(8, 128)
    got = _allgather_small(dscc_part, "gather_dscc").reshape(N_DEV, 8, 128)
    g_c_ctx = _cctx_grad(per_chip(got), c_ctx.reshape(8, 128)).reshape(D)

    sc_gin = _split_start(p16_gin, p16_gin.shape, _sc_copies, 3, g_c_ctx, "rs_gin_scatter_start")
    g_main = _split_wait(g_main, sc_gin[4], _pair_gather_copies, "rs_main_gather_wait")[1]
    seg = {k: g_main[offs[k]:offs[k] + _MAIN_ROWS[k]] for k in _MAIN}

    sl_chip = lambda a, axis, width: lax.dynamic_slice_in_dim(a, chip * width, width, axis=axis)
    grads = {
        "c_ctx": g_c_ctx, "ada_w": g_ada_w, "ada_b": g_ada_b, "norm_mix": g_small["norm_mix"],
        "norm_ffn": g_small["norm_ffn"],
        "gla_w_a2": sl_chip(g_small["gla_w_a2"], 2, KEY // N_CHIPS)[None],
        "gla_b_a": sl_chip(g_small["gla_b_a"], 1, KEY // N_CHIPS)[None],
        "gla_head_norm": g_small["gla_head_norm"][None], "gla_w_out": seg["gla_out"][None],
        "sc_w_in": seg["sc_in_t"].T[None], "sc_conv_w": sl_chip(g_small["sc_conv_w"], 1, D // N_CHIPS)[None],
        "sc_w_out": seg["sc_out"][None], "ffn_w_up": jnp.stack([seg["up_t0"].T, seg["up_t1"].T]),
        "ffn_conv_w": sl_chip(g_small["ffn_conv_w"], 2, 2 * HID // N_CHIPS), "ffn_conv_b": g_small["ffn_conv_b"],
        "ffn_w_down": jnp.stack([seg["down0"], seg["down1"]]), "final_norm": g_small["final_norm"],
    }
    weights = {"c_ctx": c_ctx, "ada_w": ada_w, "ada_b": ada_b, "norm_mix": norm_mix, "norm_ffn": norm_ffn,
               "gla_w_in": gla_w_in, "gla_w_a2": gla_w_a2, "gla_b_a": gla_b_a, "gla_head_norm": gla_head_norm,
               "gla_w_out": gla_w_out, "sc_w_in": sc_w_in, "sc_conv_w": sc_conv_w, "sc_w_out": sc_w_out,
               "ffn_w_up": ffn_w_up, "ffn_conv_w": ffn_conv_w, "ffn_conv_b": ffn_conv_b, "ffn_w_down": ffn_w_down,
               "final_norm": final_norm}
    mom1 = {"c_ctx": m_c_ctx, "ada_w": m_ada_w, "ada_b": m_ada_b, "norm_mix": m_norm_mix, "norm_ffn": m_norm_ffn,
            "gla_w_in": m_gla_w_in, "gla_w_a2": m_gla_w_a2, "gla_b_a": m_gla_b_a, "gla_head_norm": m_gla_head_norm,
            "gla_w_out": m_gla_w_out, "sc_w_in": m_sc_w_in, "sc_conv_w": m_sc_conv_w, "sc_w_out": m_sc_w_out,
            "ffn_w_up": m_ffn_w_up, "ffn_conv_w": m_ffn_conv_w, "ffn_conv_b": m_ffn_conv_b,
            "ffn_w_down": m_ffn_w_down, "final_norm": m_final_norm}
    mom2 = {"c_ctx": v_c_ctx, "ada_w": v_ada_w, "ada_b": v_ada_b, "norm_mix": v_norm_mix, "norm_ffn": v_norm_ffn,
            "gla_w_in": v_gla_w_in, "gla_w_a2": v_gla_w_a2, "gla_b_a": v_gla_b_a, "gla_head_norm": v_gla_head_norm,
            "gla_w_out": v_gla_w_out, "sc_w_in": v_sc_w_in, "sc_conv_w": v_sc_conv_w, "sc_w_out": v_sc_w_out,
            "ffn_w_up": v_ffn_w_up, "ffn_conv_w": v_ffn_conv_w, "ffn_conv_b": v_ffn_conv_b,
            "ffn_w_down": v_ffn_w_down, "final_norm": v_final_norm}
    names = list(weights)

    big_names = ["ada_w", "gla_w_out", "sc_w_in", "sc_w_out", "ffn_w_up", "ffn_w_down", "gla_w_in"]
    small_names = [k for k in names if k not in big_names]
    delta, new_m, new_v = {}, {}, {}
    done = []

    def big_adamw(k, token):
        delta[k], new_m[k], new_v[k] = _adamw(weights[k], grads[k], mom1[k], mom2[k], "adamw_" + k, token)
        done.append(new_v[k][0, 0:1, 0:128])

    for k in big_names[:-1]:
        grads[k] = grads[k].reshape(weights[k].shape)
        big_adamw(k, sc_gin[4])
    for k in small_names:
        grads[k] = grads[k].reshape(weights[k].shape)
    packed = [_pack([src[k] for k in small_names]) for src in (weights, grads, mom1, mom2)]
    meta = packed[0][1]
    rows_pad = -packed[0][0].shape[0] % 128
    bufs = [jnp.pad(p[0], ((0, rows_pad), (0, 0)))[None] for p in packed]
    outs = _adamw(bufs[0], bufs[1], bufs[2], bufs[3], "adamw_small", sc_gin[4])
    done.append(outs[2][0, 0:1, :])
    for dst, o in zip((delta, new_m, new_v), outs):
        for k, a in zip(small_names, _unpack(o[0], meta)):
            dst[k] = a
    landed = _split_wait(sc_gin, jnp.concatenate(done, axis=0), _sc_copies, "rs_gin_scatter_wait")[1]
    g_gin_shard = _rs_pair_gather(_rs_final_sum(place, landed, p32_gin, "rs_gin_final_sum"), "rs_gin_pair_gather")
    grads["gla_w_in"] = g_gin_shard[:_GIN_ROWS].T[None]
    big_adamw("gla_w_in", sc_gin[4])

    return (loss, grad_x, *[grads[k] for k in names], *[delta[k] for k in names], *[new_m[k] for k in names],
            *[new_v[k] for k in names])
```
